```python
import jax
import jax.numpy as jnp
from jax import lax
import numpy as np

D_MODEL = 1024
BATCH = 8
SEQ = 4096
DEPTH = 1

MIX_WIDTH = D_MODEL
HG_WIDTH = MIX_WIDTH // 2
HG_HEAD_DIM = 128
HG_HEADS = HG_WIDTH // HG_HEAD_DIM
HG_EXPAND = 128
HG_FDIM = HG_HEADS * HG_EXPAND
HG_CHUNK = 64
ATT_WIDTH = MIX_WIDTH - HG_WIDTH
ATT_HEAD_DIM = 64
ATT_HEADS = ATT_WIDTH // ATT_HEAD_DIM
DILATED_PAIRS = ((128, 1), (512, 4), (2048, 16))
ATT_BLOCK = 128
D_FF = ((8 * D_MODEL + 3 * 256 - 1) // (3 * 256)) * 256
RMS_EPS = 1e-6
IN_SIZES = (HG_FDIM, HG_FDIM, HG_WIDTH, HG_WIDTH, ATT_WIDTH, ATT_WIDTH, ATT_WIDTH)
IN_WIDTH = HG_FDIM * 2 + HG_WIDTH * 2 + ATT_WIDTH * 3

kernel_name = 'hybrid_hgrn2_dilated_attn_adaln_block'


def rms_norm(x, g):
    xf = x.astype(jnp.float32)
    y = xf * lax.rsqrt(jnp.mean(xf * xf, axis=-1, keepdims=True) + RMS_EPS)
    return (y * g.astype(jnp.float32)).astype(x.dtype)


def modulate(h, shift, scale):
    return h * (1 + scale[:, None, :]) + shift[:, None, :]


def hgrn2_mixer(q, f_raw, i, g, lb, onorm_g):
    B, S = q.shape[0], q.shape[1]
    nc = S // HG_CHUNK
    lb = lb.reshape(HG_HEADS, HG_EXPAND)
    f = lb + (1.0 - lb) * jax.nn.sigmoid(f_raw.astype(jnp.float32))
    k = 1.0 - f
    log_f = jnp.log(f)
    qf = jax.nn.silu(q.astype(jnp.float32))
    vf = i.astype(jnp.float32)

    def chunks(t):
        return t.reshape(B, nc, HG_CHUNK, HG_HEADS, t.shape[-1]).transpose(1, 0, 3, 2, 4)

    qc, kc, vc = chunks(qf), chunks(k), chunks(vf)
    bc = jnp.cumsum(chunks(log_f), axis=3)
    causal = jnp.tril(jnp.ones((HG_CHUNK, HG_CHUNK), dtype=bool))

    def step(state, inp):
        q_c, k_c, v_c, b_c = inp
        o_inter = jnp.einsum('bhtk,bhkv->bhtv', q_c * jnp.exp(b_c), state)
        diff = b_c[:, :, :, None, :] - b_c[:, :, None, :, :]
        decay = jnp.where(causal[:, :, None], jnp.exp(jnp.minimum(diff, 0.0)), 0.0)
        scores = jnp.einsum('bhtk,bhsk,bhtsk->bhts', q_c, k_c, decay)
        o_intra = jnp.einsum('bhts,bhsv->bhtv', scores, v_c)
        b_last = b_c[:, :, -1, :]
        k_dec = k_c * jnp.exp(b_last[:, :, None, :] - b_c)
        state = jnp.exp(b_last)[..., None] * state + jnp.einsum('bhsk,bhsv->bhkv', k_dec, v_c)
        return state, o_inter + o_intra

    state0 = jnp.zeros((B, HG_HEADS, HG_EXPAND, HG_HEAD_DIM), jnp.float32)
    _, o = lax.scan(step, state0, (qc, kc, vc, bc))
    o = o.transpose(1, 0, 3, 2, 4).reshape(B, S, HG_HEADS, HG_HEAD_DIM)
    o = rms_norm(o, onorm_g) * jax.nn.silu(g.astype(jnp.float32))
    return o.reshape(B, S, HG_WIDTH).astype(q.dtype)


def dilated_branch(q, k, v, window, dil):
    B, H, S, E = q.shape
    span = window // dil
    seg = dil * ATT_BLOCK
    s_pad = -(-S // seg) * seg
    m = s_pad // dil
    nb = m // ATT_BLOCK

    def to_blocks(t):
        t = jnp.pad(t, ((0, 0), (0, 0), (0, s_pad - S), (0, 0)))
        t = t.reshape(B, H, m, dil, E).transpose(0, 1, 3, 2, 4)
        return t.reshape(B, H, dil, nb, ATT_BLOCK, E)

    def with_prev(t):
        prev = jnp.pad(t[:, :, :, :-1], ((0, 0), (0, 0), (0, 0), (1, 0), (0, 0), (0, 0)))
        return jnp.concatenate([prev, t], axis=4)

    qb = to_blocks(q)
    kb = with_prev(to_blocks(k))
    vb = with_prev(to_blocks(v))
    s = jnp.einsum('bhrnqe,bhrnke->bhrnqk', qb, kb).astype(jnp.float32)
    qi = jnp.arange(ATT_BLOCK)[:, None]
    kj = jnp.arange(2 * ATT_BLOCK)[None, :]
    dist = ATT_BLOCK + qi - kj
    band = (dist >= 0) & (dist <= span)
    real = (jnp.arange(nb) > 0)[:, None, None] | (kj >= ATT_BLOCK)[None]
    mask = band[None] & real
    s = jnp.where(mask, s, -jnp.inf)
    s_max = jnp.max(s, axis=-1, keepdims=True)
    p = jnp.exp(s - s_max)
    l = jnp.sum(p, axis=-1, keepdims=True)
    o = jnp.einsum('bhrnqk,bhrnke->bhrnqe', p, vb.astype(jnp.float32)) / l
    lse = (s_max + jnp.log(l))[..., 0]
    o = o.reshape(B, H, dil, m, E).transpose(0, 1, 3, 2, 4).reshape(B, H, s_pad, E)[:, :, :S]
    lse = lse.reshape(B, H, dil, m).transpose(0, 1, 3, 2).reshape(B, H, s_pad)[:, :, :S]
    return o, lse


def dilated_attention(q, k, v):
    B, S = q.shape[0], q.shape[1]
    qh = q.transpose(0, 2, 1, 3) * (ATT_HEAD_DIM ** -0.5)
    kh = k.transpose(0, 2, 1, 3)
    vh = v.transpose(0, 2, 1, 3)
    outs, lses = zip(*[dilated_branch(qh, kh, vh, w, d) for (w, d) in DILATED_PAIRS])
    weights = jax.nn.softmax(jnp.stack(lses), axis=0)
    o = jnp.sum(weights[..., None] * jnp.stack(outs), axis=0)
    return o.transpose(0, 2, 1, 3).reshape(B, S, ATT_WIDTH).astype(q.dtype)


def _fwd_setup_inputs(seed: int = 0) -> dict:
    key = jax.random.key(seed)
    ks = jax.random.split(key, 14)

    def nrm(k, shape, scale):
        return jax.random.normal(k, shape, jnp.float32) * scale

    return {
        'x': nrm(ks[0], (BATCH, SEQ, D_MODEL), 1.0),
        'c': nrm(ks[1], (BATCH, D_MODEL), 1.0),
        'w_ada': nrm(ks[2], (DEPTH, D_MODEL, 6 * D_MODEL), D_MODEL ** -0.5),
        'b_ada': nrm(ks[3], (DEPTH, 6 * D_MODEL), 0.01),
        'norm1_g': 1.0 + nrm(ks[4], (DEPTH, D_MODEL), 0.01),
        'w_in': nrm(ks[5], (DEPTH, D_MODEL, IN_WIDTH), D_MODEL ** -0.5),
        'hg_lb_logits': nrm(ks[6], (DEPTH + 1, HG_FDIM), 0.1),
        'hg_onorm_g': 1.0 + nrm(ks[7], (DEPTH, HG_HEAD_DIM), 0.01),
        'att_onorm_g': 1.0 + nrm(ks[8], (DEPTH, ATT_WIDTH), 0.01),
        'w_out': nrm(ks[9], (DEPTH, MIX_WIDTH, D_MODEL), MIX_WIDTH ** -0.5),
        'norm2_g': 1.0 + nrm(ks[10], (DEPTH, D_MODEL), 0.01),
        'w_gate_up': nrm(ks[11], (DEPTH, D_MODEL, 2 * D_FF), D_MODEL ** -0.5),
        'w_down': nrm(ks[12], (DEPTH, D_FF, D_MODEL), D_FF ** -0.5),
        'final_g': 1.0 + nrm(ks[13], (D_MODEL,), 0.01),
    }


def _fwd_reference(x, c, w_ada, b_ada, norm1_g, w_in, hg_lb_logits, hg_onorm_g, att_onorm_g,
              w_out, norm2_g, w_gate_up, w_down, final_g):
    B, S = x.shape[0], x.shape[1]
    lower_bounds = jnp.cumsum(jax.nn.softmax(hg_lb_logits.astype(jnp.float32), axis=0), axis=0)
    c_act = jax.nn.silu(c)
    split_at = np.cumsum(IN_SIZES)[:-1].tolist()
    for l in range(DEPTH):
        mod = c_act @ w_ada[l] + b_ada[l]
        shift1, scale1, gate1, shift2, scale2, gate2 = jnp.split(mod, 6, axis=-1)
        h = modulate(rms_norm(x, norm1_g[l]), shift1, scale1)
        hq, hf, hi, hgt, aq, ak, av = jnp.split(h @ w_in[l], split_at, axis=-1)
        hg_out = hgrn2_mixer(
            hq.reshape(B, S, HG_HEADS, HG_EXPAND),
            hf.reshape(B, S, HG_HEADS, HG_EXPAND),
            hi.reshape(B, S, HG_HEADS, HG_HEAD_DIM),
            hgt.reshape(B, S, HG_HEADS, HG_HEAD_DIM),
            lower_bounds[l], hg_onorm_g[l])
        att = dilated_attention(
            aq.reshape(B, S, ATT_HEADS, ATT_HEAD_DIM),
            ak.reshape(B, S, ATT_HEADS, ATT_HEAD_DIM),
            av.reshape(B, S, ATT_HEADS, ATT_HEAD_DIM))
        att_out = rms_norm(att, att_onorm_g[l])
        mix = jnp.concatenate([hg_out, att_out], axis=-1) @ w_out[l]
        x = x + gate1[:, None, :] * mix
        h = modulate(rms_norm(x, norm2_g[l]), shift2, scale2)
        a, u = jnp.split(h @ w_gate_up[l], 2, axis=-1)
        x = x + gate2[:, None, :] * ((jax.nn.silu(a) * u) @ w_down[l])
    return rms_norm(x, final_g)


import jax as _jax
import jax.numpy as _jnp

TWIN_FORMAT = 'train_step'
FWD_PARAMS = ['x', 'c', 'w_ada', 'b_ada', 'norm1_g', 'w_in', 'hg_lb_logits', 'hg_onorm_g', 'att_onorm_g', 'w_out', 'norm2_g', 'w_gate_up', 'w_down', 'final_g']
TWIN_WEIGHTS = ['w_ada', 'b_ada', 'norm1_g', 'w_in', 'hg_lb_logits', 'hg_onorm_g', 'att_onorm_g', 'w_out', 'norm2_g', 'w_gate_up', 'w_down', 'final_g']
TWIN_DIFF_INPUT = 'x'
TWIN_INPUTS = ['x', 'c', 'w_ada', 'b_ada', 'norm1_g', 'w_in', 'hg_lb_logits', 'hg_onorm_g', 'att_onorm_g', 'w_out', 'norm2_g', 'w_gate_up', 'w_down', 'final_g', 'loss_target', 'm_w_ada', 'm_b_ada', 'm_norm1_g', 'm_w_in', 'm_hg_lb_logits', 'm_hg_onorm_g', 'm_att_onorm_g', 'm_w_out', 'm_norm2_g', 'm_w_gate_up', 'm_w_down', 'm_final_g', 'v_w_ada', 'v_b_ada', 'v_norm1_g', 'v_w_in', 'v_hg_lb_logits', 'v_hg_onorm_g', 'v_att_onorm_g', 'v_w_out', 'v_norm2_g', 'v_w_gate_up', 'v_w_down', 'v_final_g']
TWIN_OUTPUTS = ['loss', 'grad_x', 'grad_w_ada', 'grad_b_ada', 'grad_norm1_g', 'grad_w_in', 'grad_hg_lb_logits', 'grad_hg_onorm_g', 'grad_att_onorm_g', 'grad_w_out', 'grad_norm2_g', 'grad_w_gate_up', 'grad_w_down', 'grad_final_g', 'delta_w_ada', 'delta_b_ada', 'delta_norm1_g', 'delta_w_in', 'delta_hg_lb_logits', 'delta_hg_onorm_g', 'delta_att_onorm_g', 'delta_w_out', 'delta_norm2_g', 'delta_w_gate_up', 'delta_w_down', 'delta_final_g', 'new_m_w_ada', 'new_m_b_ada', 'new_m_norm1_g', 'new_m_w_in', 'new_m_hg_lb_logits', 'new_m_hg_onorm_g', 'new_m_att_onorm_g', 'new_m_w_out', 'new_m_norm2_g', 'new_m_w_gate_up', 'new_m_w_down', 'new_m_final_g', 'new_v_w_ada', 'new_v_b_ada', 'new_v_norm1_g', 'new_v_w_in', 'new_v_hg_lb_logits', 'new_v_hg_onorm_g', 'new_v_att_onorm_g', 'new_v_w_out', 'new_v_norm2_g', 'new_v_w_gate_up', 'new_v_w_down', 'new_v_final_g']
TWIN_LEAF_KINDS = {'loss': 'loss', 'grad_x': 'grad_x', 'grad_w_ada': 'grad_w', 'grad_b_ada': 'grad_w', 'grad_norm1_g': 'grad_w', 'grad_w_in': 'grad_w', 'grad_hg_lb_logits': 'grad_w', 'grad_hg_onorm_g': 'grad_w', 'grad_att_onorm_g': 'grad_w', 'grad_w_out': 'grad_w', 'grad_norm2_g': 'grad_w', 'grad_w_gate_up': 'grad_w', 'grad_w_down': 'grad_w', 'grad_final_g': 'grad_w', 'delta_w_ada': 'delta_w', 'delta_b_ada': 'delta_w', 'delta_norm1_g': 'delta_w', 'delta_w_in': 'delta_w', 'delta_hg_lb_logits': 'delta_w', 'delta_hg_onorm_g': 'delta_w', 'delta_att_onorm_g': 'delta_w', 'delta_w_out': 'delta_w', 'delta_norm2_g': 'delta_w', 'delta_w_gate_up': 'delta_w', 'delta_w_down': 'delta_w', 'delta_final_g': 'delta_w', 'new_m_w_ada': 'new_m', 'new_m_b_ada': 'new_m', 'new_m_norm1_g': 'new_m', 'new_m_w_in': 'new_m', 'new_m_hg_lb_logits': 'new_m', 'new_m_hg_onorm_g': 'new_m', 'new_m_att_onorm_g': 'new_m', 'new_m_w_out': 'new_m', 'new_m_norm2_g': 'new_m', 'new_m_w_gate_up': 'new_m', 'new_m_w_down': 'new_m', 'new_m_final_g': 'new_m', 'new_v_w_ada': 'new_v', 'new_v_b_ada': 'new_v', 'new_v_norm1_g': 'new_v', 'new_v_w_in': 'new_v', 'new_v_hg_lb_logits': 'new_v', 'new_v_hg_onorm_g': 'new_v', 'new_v_att_onorm_g': 'new_v', 'new_v_w_out': 'new_v', 'new_v_norm2_g': 'new_v', 'new_v_w_gate_up': 'new_v', 'new_v_w_down': 'new_v', 'new_v_final_g': 'new_v'}


def _forward(args):
    return _fwd_reference(*[args[k] for k in FWD_PARAMS])


def _output_shape():
    out = _jax.eval_shape(lambda: _forward(_fwd_setup_inputs(0)))
    return out.shape, out.dtype

N_MICROBATCH = 1
ADAM_LR = 0.001
ADAM_B1 = 0.9
ADAM_B2 = 0.999
ADAM_EPS = 1e-08
ADAM_WD = 0.01
ADAM_STEP = 10
PER_EXAMPLE_BATCH_AXIS = {'x': 0, 'c': 0, 'loss_target': 0}
SHARED_INPUTS = []
_WEIGHT_DTYPES = {'w_ada': _jnp.float32, 'b_ada': _jnp.float32, 'norm1_g': _jnp.float32, 'w_in': _jnp.float32, 'hg_lb_logits': _jnp.float32, 'hg_onorm_g': _jnp.float32, 'att_onorm_g': _jnp.float32, 'w_out': _jnp.float32, 'norm2_g': _jnp.float32, 'w_gate_up': _jnp.float32, 'w_down': _jnp.float32, 'final_g': _jnp.float32}
MOMENT_SCALE = {'w_ada': 8.253882e-02, 'b_ada': 1.359285e-01, 'norm1_g': 9.552995e-02, 'w_in': 6.761494e-02, 'hg_lb_logits': 4.675611e-03, 'hg_onorm_g': 2.104411e-01, 'att_onorm_g': 1.091114e-01, 'w_out': 9.935644e-02, 'norm2_g': 1.148301e-01, 'w_gate_up': 5.569348e-02, 'w_down': 9.040245e-02, 'final_g': 3.317550e+01}


def _to_microbatches(a, axis):
    t = _jnp.moveaxis(a, axis, 0)
    t = t.reshape((N_MICROBATCH, t.shape[0] // N_MICROBATCH) + t.shape[1:])
    return _jnp.moveaxis(t, 1, axis + 1)


def setup_inputs(seed: int = 0) -> dict:
    inp = _fwd_setup_inputs(seed)
    key = _jax.random.fold_in(_jax.random.key(seed), 7919)
    shape, _ = _output_shape()
    out = dict(inp)
    out["loss_target"] = _jax.random.normal(_jax.random.fold_in(key, 0), shape, _jnp.float32)
    for i, name in enumerate(TWIN_WEIGHTS):
        w = inp[name].astype(_jnp.float32)
        if MOMENT_SCALE is None:
            s = _jnp.sqrt(_jnp.mean(_jnp.square(w)) + 1e-30)
        else:
            s = MOMENT_SCALE[name]
        km, kv = _jax.random.split(_jax.random.fold_in(key, i + 1))
        out[name] = w
        out["m_" + name] = s * _jax.random.normal(km, w.shape, _jnp.float32)
        out["v_" + name] = (s * s) * _jax.random.uniform(kv, w.shape, _jnp.float32, 0.5, 1.5)
    if N_MICROBATCH > 1:
        for name, axis in PER_EXAMPLE_BATCH_AXIS.items():
            out[name] = _to_microbatches(out[name], axis)
    return {'x': out['x'], 'c': out['c'], 'w_ada': out['w_ada'], 'b_ada': out['b_ada'], 'norm1_g': out['norm1_g'], 'w_in': out['w_in'], 'hg_lb_logits': out['hg_lb_logits'], 'hg_onorm_g': out['hg_onorm_g'], 'att_onorm_g': out['att_onorm_g'], 'w_out': out['w_out'], 'norm2_g': out['norm2_g'], 'w_gate_up': out['w_gate_up'], 'w_down': out['w_down'], 'final_g': out['final_g'], 'loss_target': out['loss_target'], 'm_w_ada': out['m_w_ada'], 'm_b_ada': out['m_b_ada'], 'm_norm1_g': out['m_norm1_g'], 'm_w_in': out['m_w_in'], 'm_hg_lb_logits': out['m_hg_lb_logits'], 'm_hg_onorm_g': out['m_hg_onorm_g'], 'm_att_onorm_g': out['m_att_onorm_g'], 'm_w_out': out['m_w_out'], 'm_norm2_g': out['m_norm2_g'], 'm_w_gate_up': out['m_w_gate_up'], 'm_w_down': out['m_w_down'], 'm_final_g': out['m_final_g'], 'v_w_ada': out['v_w_ada'], 'v_b_ada': out['v_b_ada'], 'v_norm1_g': out['v_norm1_g'], 'v_w_in': out['v_w_in'], 'v_hg_lb_logits': out['v_hg_lb_logits'], 'v_hg_onorm_g': out['v_hg_onorm_g'], 'v_att_onorm_g': out['v_att_onorm_g'], 'v_w_out': out['v_w_out'], 'v_norm2_g': out['v_norm2_g'], 'v_w_gate_up': out['v_w_gate_up'], 'v_w_down': out['v_w_down'], 'v_final_g': out['v_final_g']}


def _loss(weights, diff, rest, loss_target):
    with _jax.named_scope("forward"):
        args = {**rest, TWIN_DIFF_INPUT: diff, **{k: w.astype(_WEIGHT_DTYPES[k]) for k, w in weights.items()}}
        y = _forward(args)
    with _jax.named_scope("loss_head"):
        err = _jnp.square(y.astype(_jnp.float32) - loss_target)
        return 0.5 * _jnp.sum(_jnp.mean(err, axis=-1)) if err.ndim else 0.5 * err


def _adamw(w, g, m, v):
    m = ADAM_B1 * m + (1.0 - ADAM_B1) * g
    v = ADAM_B2 * v + (1.0 - ADAM_B2) * _jnp.square(g)
    m_hat = m / (1.0 - ADAM_B1 ** ADAM_STEP)
    v_hat = v / (1.0 - ADAM_B2 ** ADAM_STEP)
    delta = -ADAM_LR * (m_hat / (_jnp.sqrt(v_hat) + ADAM_EPS) + ADAM_WD * w)
    return delta, m, v


def reference(x, c, w_ada, b_ada, norm1_g, w_in, hg_lb_logits, hg_onorm_g, att_onorm_g, w_out, norm2_g, w_gate_up, w_down, final_g, loss_target, m_w_ada, m_b_ada, m_norm1_g, m_w_in, m_hg_lb_logits, m_hg_onorm_g, m_att_onorm_g, m_w_out, m_norm2_g, m_w_gate_up, m_w_down, m_final_g, v_w_ada, v_b_ada, v_norm1_g, v_w_in, v_hg_lb_logits, v_hg_onorm_g, v_att_onorm_g, v_w_out, v_norm2_g, v_w_gate_up, v_w_down, v_final_g):
    given = dict(x=x, c=c, w_ada=w_ada, b_ada=b_ada, norm1_g=norm1_g, w_in=w_in, hg_lb_logits=hg_lb_logits, hg_onorm_g=hg_onorm_g, att_onorm_g=att_onorm_g, w_out=w_out, norm2_g=norm2_g, w_gate_up=w_gate_up, w_down=w_down, final_g=final_g, loss_target=loss_target, m_w_ada=m_w_ada, m_b_ada=m_b_ada, m_norm1_g=m_norm1_g, m_w_in=m_w_in, m_hg_lb_logits=m_hg_lb_logits, m_hg_onorm_g=m_hg_onorm_g, m_att_onorm_g=m_att_onorm_g, m_w_out=m_w_out, m_norm2_g=m_norm2_g, m_w_gate_up=m_w_gate_up, m_w_down=m_w_down, m_final_g=m_final_g, v_w_ada=v_w_ada, v_b_ada=v_b_ada, v_norm1_g=v_norm1_g, v_w_in=v_w_in, v_hg_lb_logits=v_hg_lb_logits, v_hg_onorm_g=v_hg_onorm_g, v_att_onorm_g=v_att_onorm_g, v_w_out=v_w_out, v_norm2_g=v_norm2_g, v_w_gate_up=v_w_gate_up, v_w_down=v_w_down, v_final_g=v_final_g)
    weights = {n: given[n] for n in TWIN_WEIGHTS}
    shared = {n: given[n] for n in SHARED_INPUTS}
    per_example = {n: given[n] for n in ['x', 'c']}
    grad_fn = _jax.value_and_grad(_loss, argnums=(0, 1))

    def one_microbatch(ex, loss_target):
        ex = dict(ex)
        diff = ex.pop(TWIN_DIFF_INPUT)
        return grad_fn(weights, diff, {**shared, **ex}, loss_target)

    if N_MICROBATCH == 1:
        loss, (grad_w, grad_x) = one_microbatch(per_example, given["loss_target"])
    else:
        def body(carry, xs):
            loss_sum, grad_sum = carry
            l_k, (gw_k, gx_k) = one_microbatch(xs[0], xs[1])
            with _jax.named_scope("update"):
                return (loss_sum + l_k, _jax.tree.map(_jnp.add, grad_sum, gw_k)), gx_k

        init = (_jnp.zeros((), _jnp.float32), _jax.tree.map(_jnp.zeros_like, weights))
        (loss, grad_w), grad_x = _jax.lax.scan(body, init, (per_example, given["loss_target"]))
    with _jax.named_scope("update"):
        delta_w, new_m, new_v = {}, {}, {}
        for n in TWIN_WEIGHTS:
            delta_w[n], new_m[n], new_v[n] = _adamw(weights[n], grad_w[n], given["m_" + n], given["v_" + n])
    return (loss, grad_x, *[grad_w[n] for n in TWIN_WEIGHTS], *[delta_w[n] for n in TWIN_WEIGHTS],
            *[new_m[n] for n in TWIN_WEIGHTS], *[new_v[n] for n in TWIN_WEIGHTS])
```

```python
import functools

import jax
import jax.numpy as jnp
from jax import lax
from jax.experimental import pallas as pl
from jax.experimental.pallas import tpu as pltpu

F32 = jnp.float32
BF16 = jnp.bfloat16
MESH = pl.DeviceIdType.MESH

D_MODEL = 1024
HG_WIDTH = 512
HG_HEAD = 128
HG_CHUNK = 64
ATT_WIDTH = 512
ATT_HEADS = 8
ATT_BLOCK = 128
DILATIONS = (1, 4, 16)
D_FF = 2816
IN_WIDTH = 3584
N_SHARD = 4
RMS_EPS = 1e-6
NEG = -1e30

ADAM_LR = 0.001
ADAM_B1 = 0.9
ADAM_B2 = 0.999
ADAM_EPS = 1e-08
ADAM_WD = 0.01
ADAM_STEP = 10

VMEM_LIMIT = 56 * 2**20

ST_LOSS, ST_DFG, ST_DG2, ST_DG1 = 0, 1024, 2048, 3072
ST_DLB, ST_DAG, ST_DOG, ST_DMOD = 4096, 4608, 5120, 5248
ST_WIDTH = 5248 + 6144
SP_BADA, SP_N1, SP_LB, SP_OG, SP_AG, SP_N2, SP_FG = 0, 6144, 7168, 8192, 8320, 8832, 9856
SP_WIDTH = 10880


def _call(body, *, name, grid, in_specs, out_specs, out_shape, scratch_shapes=()):
    return pl.pallas_call(
        body, name=name, grid=grid, in_specs=in_specs, out_specs=out_specs, out_shape=out_shape,
        scratch_shapes=list(scratch_shapes),
        compiler_params=pltpu.CompilerParams(
            dimension_semantics=("arbitrary",) * len(grid), vmem_limit_bytes=VMEM_LIMIT))


def _sds(shape, dtype=F32):
    return jax.ShapeDtypeStruct(shape, dtype)


def _dot(a, b):
    return jnp.dot(a, b, preferred_element_type=F32)


def _dot_nt(a, b):
    return lax.dot_general(a, b, (((1,), (1,)), ((), ())), preferred_element_type=F32)


def _dot_tn(a, b):
    return lax.dot_general(a, b, (((0,), (0,)), ((), ())), preferred_element_type=F32)


def _sigmoid(x):
    return 1.0 / (1.0 + jnp.exp(-x))


def _rows(tr, width):
    return pl.BlockSpec((tr, width), lambda i: (i, 0))


def _vec(width):
    return pl.BlockSpec((1, width), lambda i: (0, 0))


def _acc(ref, val, first):
    @pl.when(first)
    def _():
        ref[...] = val

    @pl.when(jnp.logical_not(first))
    def _():
        ref[...] += val


def _mm_nn(a, b3, name, tm=512):
    m, k = a.shape
    s, _, n = b3.shape

    def body(a_ref, b_ref, o_ref):
        o_ref[...] = _dot(a_ref[...], b_ref[0])

    return _call(
        body, name=name, grid=(s, m // tm),
        in_specs=[pl.BlockSpec((tm, k), lambda j, i: (i, 0)), pl.BlockSpec((1, k, n), lambda j, i: (j, 0, 0))],
        out_specs=pl.BlockSpec((tm, n), lambda j, i: (i, j)), out_shape=_sds((m, s * n)))(a, b3)


def _mm_nt(dy, b3, name, tm=512):
    m = dy.shape[0]
    s, k, n = b3.shape

    def body(dy_ref, b_ref, o_ref):
        _acc(o_ref, _dot_nt(dy_ref[...], b_ref[0]), pl.program_id(1) == 0)

    return _call(
        body, name=name, grid=(m // tm, s),
        in_specs=[pl.BlockSpec((tm, n), lambda i, j: (i, j)), pl.BlockSpec((1, k, n), lambda i, j: (j, 0, 0))],
        out_specs=pl.BlockSpec((tm, k), lambda i, j: (i, 0)), out_shape=_sds((m, k)))(dy, b3)


def _mm_tn(a, dy, s, name, tm=512, tk=None):
    m, k = a.shape
    n = dy.shape[1] // s
    tk = k if tk is None else tk
    steps = m // tm

    def body(a_ref, dy_ref, o_ref, ob_ref):
        i = pl.program_id(2)
        _acc(o_ref, _dot_tn(a_ref[...], dy_ref[...])[None], i == 0)

        @pl.when(i == steps - 1)
        def _():
            ob_ref[...] = o_ref[...].astype(BF16)

    out = pl.BlockSpec((1, tk, n), lambda kk, j, i: (j, kk, 0))
    return _call(
        body, name=name, grid=(k // tk, s, steps),
        in_specs=[pl.BlockSpec((tm, tk), lambda kk, j, i: (i, kk)), pl.BlockSpec((tm, n), lambda kk, j, i: (i, j))],
        out_specs=[out, out], out_shape=[_sds((s, k, n)), _sds((s, k, n), BF16)])(a, dy)


def _rms(x):
    return lax.rsqrt(jnp.mean(x * x, axis=-1, keepdims=True) + RMS_EPS)


def _rms_bwd(dxh, xh, r):
    return r * (dxh - xh * jnp.mean(dxh * xh, axis=-1, keepdims=True))


def _norm_mod(x, g, scale, shift, name, tr=512):
    t = x.shape[0]

    def body(x_ref, g_ref, sc_ref, sh_ref, h_ref):
        xv = x_ref[...]
        n = xv * _rms(xv) * g_ref[...]
        h_ref[...] = (n * (1.0 + sc_ref[...]) + sh_ref[...]).astype(BF16)

    return _call(body, name=name, grid=(t // tr,),
                 in_specs=[_rows(tr, D_MODEL), _vec(D_MODEL), _vec(D_MODEL), _vec(D_MODEL)],
                 out_specs=_rows(tr, D_MODEL), out_shape=_sds((t, D_MODEL), BF16))(x, g, scale, shift)


def _mix_in(o_hg, proj, att, og, ag, name, tr=256):
    t = o_hg.shape[0]

    def body(o_ref, g_ref, a_ref, og_ref, ag_ref, m_ref):
        for h in range(HG_WIDTH // HG_HEAD):
            sl = slice(h * HG_HEAD, (h + 1) * HG_HEAD)
            oh = o_ref[:, sl]
            gv = g_ref[:, sl]
            m_ref[:, sl] = (oh * _rms(oh) * og_ref[...] * (gv * _sigmoid(gv))).astype(BF16)
        av = a_ref[...]
        m_ref[:, HG_WIDTH:] = (av * _rms(av) * ag_ref[...]).astype(BF16)

    return _call(body, name=name, grid=(t // tr,),
                 in_specs=[_rows(tr, HG_WIDTH), pl.BlockSpec((tr, HG_WIDTH), lambda i: (i, 3)), _rows(tr, ATT_WIDTH),
                           _vec(HG_HEAD), _vec(ATT_WIDTH)],
                 out_specs=_rows(tr, D_MODEL), out_shape=_sds((t, D_MODEL), BF16))(o_hg, proj, att, og, ag)


def _resid_norm_mod(x, mix, gate, g, scale, shift, name, tr=256):
    t = x.shape[0]

    def body(x_ref, m_ref, gt_ref, g_ref, sc_ref, sh_ref, x2_ref, h_ref):
        x2 = x_ref[...] + gt_ref[...] * m_ref[...]
        x2_ref[...] = x2
        n = x2 * _rms(x2) * g_ref[...]
        h_ref[...] = (n * (1.0 + sc_ref[...]) + sh_ref[...]).astype(BF16)

    return _call(body, name=name, grid=(t // tr,),
                 in_specs=[_rows(tr, D_MODEL), _rows(tr, D_MODEL)] + [_vec(D_MODEL)] * 4,
                 out_specs=[_rows(tr, D_MODEL), _rows(tr, D_MODEL)],
                 out_shape=[_sds((t, D_MODEL)), _sds((t, D_MODEL), BF16)])(x, mix, gate, g, scale, shift)


def _swiglu(au, name, tr=256):
    t = au.shape[0]

    def body(a_ref, u_ref, o_ref):
        a = a_ref[...]
        o_ref[...] = (a * _sigmoid(a) * u_ref[...]).astype(BF16)

    return _call(body, name=name, grid=(t // tr,),
                 in_specs=[pl.BlockSpec((tr, D_FF), lambda i: (i, 0)), pl.BlockSpec((tr, D_FF), lambda i: (i, 1))],
                 out_specs=_rows(tr, D_FF), out_shape=_sds((t, D_FF), BF16))(au, au)


def _swiglu_bwd(dact, au, name, tr=256):
    t = au.shape[0]

    def body(d_ref, a_ref, u_ref, o_ref):
        a = a_ref[...]
        d = d_ref[...]
        sg = _sigmoid(a)
        o_ref[:, :D_FF] = (d * u_ref[...] * sg * (1.0 + a * (1.0 - sg))).astype(BF16)
        o_ref[:, D_FF:] = (d * a * sg).astype(BF16)

    return _call(body, name=name, grid=(t // tr,),
                 in_specs=[_rows(tr, D_FF), pl.BlockSpec((tr, D_FF), lambda i: (i, 0)),
                           pl.BlockSpec((tr, D_FF), lambda i: (i, 1))],
                 out_specs=_rows(tr, 2 * D_FF), out_shape=_sds((t, 2 * D_FF), BF16))(dact, au, au)


def _final_loss(x2, ffn, gate, fg, tgt, name, tr=256):
    t = x2.shape[0]

    def body(x_ref, f_ref, gt_ref, fg_ref, t_ref, dx_ref, df_ref, l_ref, dfg_ref, dgt_ref):
        first = pl.program_id(0) == 0
        ffn_v = f_ref[...]
        x3 = x_ref[...] + gt_ref[...] * ffn_v
        r = _rms(x3)
        xh = x3 * r
        err = xh * fg_ref[...] - t_ref[...]
        dy = err * (1.0 / D_MODEL)
        dx3 = _rms_bwd(dy * fg_ref[...], xh, r)
        dx_ref[...] = dx3
        df_ref[...] = (dx3 * gt_ref[...]).astype(BF16)
        _acc(l_ref, jnp.sum(err * err, axis=0, keepdims=True), first)
        _acc(dfg_ref, jnp.sum(dy * xh, axis=0, keepdims=True), first)
        _acc(dgt_ref, jnp.sum(dx3 * ffn_v, axis=0, keepdims=True), first)

    row, vec = _rows(tr, D_MODEL), _vec(D_MODEL)
    return _call(body, name=name, grid=(t // tr,), in_specs=[row, row, vec, vec, row],
                 out_specs=[row, row, vec, vec, vec],
                 out_shape=[_sds((t, D_MODEL)), _sds((t, D_MODEL), BF16)] + [_sds((1, D_MODEL))] * 3)(
                     x2, ffn, gate, fg, tgt)


def _norm_mod_bwd(dh, x, g, scale, dres, name, gate=None, mix=None, tr=256):
    t = x.shape[0]
    below = gate is not None

    def body(*refs):
        if below:
            dh_ref, x_ref, g_ref, sc_ref, dr_ref, gt_ref, m_ref, dx_ref, dsh_ref, dsc_ref, dg_ref, dgt_ref, dm_ref = refs
        else:
            dh_ref, x_ref, g_ref, sc_ref, dr_ref, dx_ref, dsh_ref, dsc_ref, dg_ref = refs
        first = pl.program_id(0) == 0
        xv = x_ref[...]
        dhv = dh_ref[...]
        r = _rms(xv)
        xh = xv * r
        dn = dhv * (1.0 + sc_ref[...])
        dx = dr_ref[...] + _rms_bwd(dn * g_ref[...], xh, r)
        dx_ref[...] = dx
        _acc(dsh_ref, jnp.sum(dhv, axis=0, keepdims=True), first)
        _acc(dsc_ref, jnp.sum(dhv * xh * g_ref[...], axis=0, keepdims=True), first)
        _acc(dg_ref, jnp.sum(dn * xh, axis=0, keepdims=True), first)
        if below:
            _acc(dgt_ref, jnp.sum(dx * m_ref[...], axis=0, keepdims=True), first)
            dm_ref[...] = (dx * gt_ref[...]).astype(BF16)

    row, vec = _rows(tr, D_MODEL), _vec(D_MODEL)
    in_specs = [row, row, vec, vec, row] + ([vec, row] if below else [])
    out_specs = [row, vec, vec, vec] + ([vec, row] if below else [])
    out_shape = [_sds((t, D_MODEL))] + [_sds((1, D_MODEL))] * 3 + ([_sds((1, D_MODEL)), _sds((t, D_MODEL), BF16)] if below else [])
    args = (dh, x, g, scale, dres) + ((gate, mix) if below else ())
    return _call(body, name=name, grid=(t // tr,), in_specs=in_specs, out_specs=out_specs, out_shape=out_shape)(*args)


def _mix_in_bwd(dmi, o_hg, proj, att, og, ag, name, tr=256):
    t = o_hg.shape[0]

    def body(d_ref, o_ref, g_ref, a_ref, og_ref, ag_ref, do_ref, dg_ref, da_ref, dd_ref, dog_ref, dag_ref):
        first = pl.program_id(0) == 0
        dog = jnp.zeros((1, HG_HEAD), F32)
        for h in range(HG_WIDTH // HG_HEAD):
            sl = slice(h * HG_HEAD, (h + 1) * HG_HEAD)
            oh = o_ref[:, sl]
            gv = g_ref[:, sl]
            dv = d_ref[:, sl]
            r = _rms(oh)
            xh = oh * r
            sg = _sigmoid(gv)
            dno = dv * gv * sg
            dg_ref[:, sl] = dv * xh * og_ref[...] * sg * (1.0 + gv * (1.0 - sg))
            dog = dog + jnp.sum(dno * xh, axis=0, keepdims=True)
            do_ref[:, sl] = _rms_bwd(dno * og_ref[...], xh, r)
        _acc(dog_ref, dog, first)
        av = a_ref[...]
        dav = d_ref[:, HG_WIDTH:]
        r = _rms(av)
        xa = av * r
        _acc(dag_ref, jnp.sum(dav * xa, axis=0, keepdims=True), first)
        datt = _rms_bwd(dav * ag_ref[...], xa, r)
        da_ref[...] = datt
        prod = datt * av
        lane = lax.broadcasted_iota(jnp.int32, (1, 128), 1)
        dd = jnp.zeros((tr, 128), F32)
        for hp in range(ATT_HEADS // 2):
            pp = prod[:, hp * 128:(hp + 1) * 128]
            lo = jnp.sum(jnp.where(lane < 64, pp, 0.0), axis=-1, keepdims=True)
            hi = jnp.sum(jnp.where(lane >= 64, pp, 0.0), axis=-1, keepdims=True)
            dd = jnp.where(lane == 2 * hp, lo, dd)
            dd = jnp.where(lane == 2 * hp + 1, hi, dd)
        dd_ref[...] = dd

    half = _rows(tr, HG_WIDTH)
    return _call(body, name=name, grid=(t // tr,),
                 in_specs=[_rows(tr, D_MODEL), half, pl.BlockSpec((tr, HG_WIDTH), lambda i: (i, 3)), half,
                           _vec(HG_HEAD), _vec(ATT_WIDTH)],
                 out_specs=[half, half, half, _rows(tr, 128), _vec(HG_HEAD), _vec(ATT_WIDTH)],
                 out_shape=[_sds((t, HG_WIDTH))] * 3 + [_sds((t, 128)), _sds((1, HG_HEAD)), _sds((1, ATT_WIDTH))])(
                     dmi, o_hg, proj, att, og, ag)


def _dproj(dhg, dg, dqkv, name, tr=256):
    t = dhg.shape[0]
    w3 = 3 * HG_WIDTH

    def body(h_ref, g_ref, a1_ref, a2_ref, a3_ref, o_ref):
        o_ref[:, :w3] = h_ref[...].astype(BF16)
        o_ref[:, w3:w3 + HG_WIDTH] = g_ref[...].astype(BF16)
        o_ref[:, w3 + HG_WIDTH:] = (a1_ref[...] + a2_ref[...] + a3_ref[...]).astype(BF16)

    return _call(body, name=name, grid=(t // tr,),
                 in_specs=[_rows(tr, w3), _rows(tr, HG_WIDTH)] + [_rows(tr, w3)] * 3,
                 out_specs=_rows(tr, IN_WIDTH), out_shape=_sds((t, IN_WIDTH), BF16))(dhg, dg, *dqkv)


def _hg_gates(f_raw, q_raw, lb, tri):
    sg = _sigmoid(f_raw)
    f = lb + (1.0 - lb) * sg
    k = 1.0 - f
    b = jnp.dot(tri, jnp.log(f), precision=lax.Precision.HIGHEST, preferred_element_type=F32)
    sq = _sigmoid(q_raw)
    return sg, f, k, b, sq


def _hgrn_fwd(proj, lb_logits, name):
    t = proj.shape[0]
    nc = t // HG_CHUNK
    nh = HG_WIDTH // HG_HEAD

    def body(q_ref, f_ref, i_ref, lg_ref, o_ref, st_ref, s_scr):
        @pl.when(pl.program_id(0) == 0)
        def _():
            s_scr[...] = jnp.zeros_like(s_scr)

        lg = lg_ref[...]
        lb_all = _sigmoid(lg[0:1] - lg[1:2])
        row = lax.broadcasted_iota(jnp.int32, (HG_CHUNK, HG_CHUNK), 0)
        col = lax.broadcasted_iota(jnp.int32, (HG_CHUNK, HG_CHUNK), 1)
        causal = row >= col
        tri = causal.astype(F32)
        for h in range(nh):
            sl = slice(h * HG_HEAD, (h + 1) * HG_HEAD)
            q_raw = q_ref[:, sl]
            _, _, k, b, sq = _hg_gates(f_ref[:, sl], q_raw, lb_all[:, sl], tri)
            v = i_ref[:, sl].astype(BF16)
            gl = b[HG_CHUNK - 1:HG_CHUNK]
            qd = (q_raw * sq * jnp.exp(b)).astype(BF16)
            kd = (k * jnp.exp(-b)).astype(BF16)
            ke = (k * jnp.exp(gl - b)).astype(BF16)
            st = s_scr[h]
            st_ref[0, sl, :] = st
            a = jnp.where(causal, _dot_nt(qd, kd), 0.0).astype(BF16)
            o_ref[:, sl] = _dot_nt(qd, st.astype(BF16)) + _dot(a, v)
            s_scr[h] = st * jnp.exp(gl) + _dot_tn(v, ke)

    blk = lambda j: pl.BlockSpec((HG_CHUNK, HG_WIDTH), lambda c: (c, j))
    return _call(body, name=name, grid=(nc,),
                 in_specs=[blk(0), blk(1), blk(2), pl.BlockSpec((2, HG_WIDTH), lambda c: (0, 0))],
                 out_specs=[blk(0), pl.BlockSpec((1, HG_WIDTH, HG_HEAD), lambda c: (c, 0, 0))],
                 out_shape=[_sds((t, HG_WIDTH)), _sds((nc, HG_WIDTH, HG_HEAD))],
                 scratch_shapes=[pltpu.VMEM((nh, HG_HEAD, HG_HEAD), F32)])(proj, proj, proj, lb_logits)


def _hgrn_bwd(proj, lb_logits, states, do, name):
    t = proj.shape[0]
    nc = t // HG_CHUNK
    nh = HG_WIDTH // HG_HEAD

    def body(q_ref, f_ref, i_ref, lg_ref, st_ref, do_ref, d_ref, dlb_ref, ds_scr):
        first = pl.program_id(0) == 0

        @pl.when(first)
        def _():
            ds_scr[...] = jnp.zeros_like(ds_scr)

        lg = lg_ref[...]
        lb_all = _sigmoid(lg[0:1] - lg[1:2])
        row = lax.broadcasted_iota(jnp.int32, (HG_CHUNK, HG_CHUNK), 0)
        col = lax.broadcasted_iota(jnp.int32, (HG_CHUNK, HG_CHUNK), 1)
        causal = row >= col
        tri = causal.astype(F32)
        tri_t = (row <= col).astype(F32)
        dlb = []
        for h in range(nh):
            sl = slice(h * HG_HEAD, (h + 1) * HG_HEAD)
            q_raw = q_ref[:, sl]
            lb = lb_all[:, sl]
            sg, f, k, b, sq = _hg_gates(f_ref[:, sl], q_raw, lb, tri)
            v = i_ref[:, sl].astype(BF16)
            gl = b[HG_CHUNK - 1:HG_CHUNK]
            egl = jnp.exp(gl)
            eb = jnp.exp(b)
            enb = jnp.exp(-b)
            egb = jnp.exp(gl - b)
            qd = q_raw * sq * eb
            kd = k * enb
            ke = k * egb
            qd_b, kd_b, ke_b = qd.astype(BF16), kd.astype(BF16), ke.astype(BF16)
            st = st_ref[0, sl, :]
            dst = ds_scr[h]
            dst_b = dst.astype(BF16)
            dov = do_ref[:, sl].astype(BF16)
            a = jnp.where(causal, _dot_nt(qd_b, kd_b), 0.0).astype(BF16)
            da = jnp.where(causal, _dot_nt(dov, v), 0.0).astype(BF16)
            dqd = _dot(dov, st.astype(BF16)) + _dot(da, kd_b)
            dkd = _dot_tn(da, qd_b)
            dv = _dot_tn(a, dov) + _dot_nt(ke_b, dst_b)
            dke = _dot(v, dst_b)
            dgl = jnp.sum(dst * st, axis=0, keepdims=True) * egl
            ds_scr[h] = _dot_tn(dov, qd_b) + dst * egl
            t1 = dke * ke
            db = dqd * qd_b.astype(F32) - dkd * kd_b.astype(F32) - t1
            dgl = dgl + jnp.sum(t1, axis=0, keepdims=True)
            dlf = jnp.dot(tri_t, db, precision=lax.Precision.HIGHEST, preferred_element_type=F32) + dgl
            df = dlf / f - (dkd * enb + dke * egb)
            d_ref[:, sl] = dqd * eb * sq * (1.0 + q_raw * (1.0 - sq))
            d_ref[:, HG_WIDTH + h * HG_HEAD:HG_WIDTH + (h + 1) * HG_HEAD] = df * (1.0 - lb) * sg * (1.0 - sg)
            d_ref[:, 2 * HG_WIDTH + h * HG_HEAD:2 * HG_WIDTH + (h + 1) * HG_HEAD] = dv
            dlb.append(jnp.sum(df * (1.0 - sg), axis=0, keepdims=True))
        _acc(dlb_ref, jnp.concatenate(dlb, axis=1), first)

    rev = lambda j: pl.BlockSpec((HG_CHUNK, HG_WIDTH), lambda c: (nc - 1 - c, j))
    return _call(body, name=name, grid=(nc,),
                 in_specs=[rev(0), rev(1), rev(2), pl.BlockSpec((2, HG_WIDTH), lambda c: (0, 0)),
                           pl.BlockSpec((1, HG_WIDTH, HG_HEAD), lambda c: (nc - 1 - c, 0, 0)), rev(0)],
                 out_specs=[pl.BlockSpec((HG_CHUNK, 3 * HG_WIDTH), lambda c: (nc - 1 - c, 0)), _vec(HG_WIDTH)],
                 out_shape=[_sds((t, 3 * HG_WIDTH)), _sds((1, HG_WIDTH))],
                 scratch_shapes=[pltpu.VMEM((nh, HG_HEAD, HG_HEAD), F32)])(proj, proj, proj, lb_logits, states, do)


def _to_sub(a, dil):
    t, w = a.shape
    return a if dil == 1 else a.reshape(t // dil, dil, w).transpose(1, 0, 2).reshape(t, w)


def _from_sub(a, dil):
    t, w = a.shape
    return a if dil == 1 else a.reshape(dil, t // dil, w).transpose(1, 0, 2).reshape(t, w)


def _att_masks():
    qi = lax.broadcasted_iota(jnp.int32, (ATT_BLOCK, ATT_BLOCK), 0)
    kj = lax.broadcasted_iota(jnp.int32, (ATT_BLOCK, ATT_BLOCK), 1)
    lane = lax.broadcasted_iota(jnp.int32, (1, 128), 1)
    return kj <= qi, kj >= qi, lane


def _attn_fwd(qkv, dil, name):
    t = qkv.shape[0]
    nb = t // ATT_BLOCK
    bps = nb // dil

    def body(q_ref, kc_ref, kp_ref, vc_ref, vp_ref, o_ref, l_ref):
        cur, prev, lane = _att_masks()
        prev = jnp.logical_and(prev, (pl.program_id(0) % bps) != 0)
        lse_all = jnp.zeros((ATT_BLOCK, 128), F32)
        for hp in range(ATT_HEADS // 2):
            sl = slice(hp * 128, (hp + 1) * 128)
            q2, kc, kp, vc, vp = q_ref[:, sl], kc_ref[:, sl], kp_ref[:, sl], vc_ref[:, sl], vp_ref[:, sl]
            outs = []
            for hh in range(2):
                mine = (lane < 64) if hh == 0 else (lane >= 64)
                qm = jnp.where(mine, q2, jnp.zeros_like(q2))
                sc = jnp.where(cur, _dot_nt(qm, kc) * 0.125, NEG)
                sp = jnp.where(prev, _dot_nt(qm, kp) * 0.125, NEG)
                mx = jnp.maximum(jnp.max(sc, axis=-1, keepdims=True), jnp.max(sp, axis=-1, keepdims=True))
                pc = jnp.exp(sc - mx)
                pp = jnp.exp(sp - mx)
                l = jnp.sum(pc, axis=-1, keepdims=True) + jnp.sum(pp, axis=-1, keepdims=True)
                acc = _dot(pc.astype(BF16), vc) + _dot(pp.astype(BF16), vp)
                outs.append(acc / l)
                lse_all = jnp.where(lane == 2 * hp + hh, mx + jnp.log(l), lse_all)
            o_ref[:, sl] = jnp.where(lane < 64, outs[0], outs[1])
        l_ref[...] = lse_all

    blk = lambda j, back: pl.BlockSpec((ATT_BLOCK, ATT_WIDTH), lambda n: (jnp.maximum(n - back, 0), j))
    return _call(body, name=name, grid=(nb,),
                 in_specs=[blk(0, 0), blk(1, 0), blk(1, 1), blk(2, 0), blk(2, 1)],
                 out_specs=[pl.BlockSpec((ATT_BLOCK, ATT_WIDTH), lambda n: (n, 0)),
                            pl.BlockSpec((ATT_BLOCK, 128), lambda n: (n, 0))],
                 out_shape=[_sds((t, ATT_WIDTH)), _sds((t, 128))])(qkv, qkv, qkv, qkv, qkv)


def _attn_combine(os_, ls_, name, tr=256):
    t = os_[0].shape[0]
    nbr = len(os_)

    def body(*refs):
        o_refs, l_refs, (a_ref, lt_ref) = refs[:nbr], refs[nbr:2 * nbr], refs[2 * nbr:]
        lane = lax.broadcasted_iota(jnp.int32, (1, 128), 1)
        ls = [r[...] for r in l_refs]
        mx = functools.reduce(jnp.maximum, ls)
        tot = mx + jnp.log(sum(jnp.exp(l - mx) for l in ls))
        lt_ref[...] = tot
        ws = [jnp.exp(l - tot) for l in ls]
        for hp in range(ATT_HEADS // 2):
            sl = slice(hp * 128, (hp + 1) * 128)
            acc = jnp.zeros((tr, 128), F32)
            for w, o_ref in zip(ws, o_refs):
                wf = jnp.where(lane < 64, w[:, 2 * hp:2 * hp + 1], w[:, 2 * hp + 1:2 * hp + 2])
                acc = acc + wf * o_ref[:, sl]
            a_ref[:, sl] = acc

    return _call(body, name=name, grid=(t // tr,),
                 in_specs=[_rows(tr, ATT_WIDTH)] * nbr + [_rows(tr, 128)] * nbr,
                 out_specs=[_rows(tr, ATT_WIDTH), _rows(tr, 128)],
                 out_shape=[_sds((t, ATT_WIDTH)), _sds((t, 128))])(*os_, *ls_)


def _attn_bwd(qkv, dout, lse, dd, dil, name):
    t = qkv.shape[0]
    nb = t // ATT_BLOCK
    bps = nb // dil

    def body(q_ref, qn_ref, kc_ref, kp_ref, vc_ref, vp_ref, do_ref, don_ref, l_ref, ln_ref, d_ref, dn_ref, g_ref):
        n = pl.program_id(0)
        cur, prev, lane = _att_masks()
        nxt = jnp.logical_and(prev, ((n + 1) % bps) != 0)
        prev = jnp.logical_and(prev, (n % bps) != 0)
        for hp in range(ATT_HEADS // 2):
            sl = slice(hp * 128, (hp + 1) * 128)
            q2, qn2, kc, kp, vc, vp = q_ref[:, sl], qn_ref[:, sl], kc_ref[:, sl], kp_ref[:, sl], vc_ref[:, sl], vp_ref[:, sl]
            do2, don2 = do_ref[:, sl], don_ref[:, sl]
            dq, dk, dv = [], [], []
            for hh in range(2):
                h = 2 * hp + hh
                mine = (lane < 64) if hh == 0 else (lane >= 64)
                zero = jnp.zeros_like(q2)
                qm, qnm = jnp.where(mine, q2, zero), jnp.where(mine, qn2, zero)
                dom, donm = jnp.where(mine, do2, zero), jnp.where(mine, don2, zero)
                ls, lsn = l_ref[:, h:h + 1], ln_ref[:, h:h + 1]
                dh, dhn = d_ref[:, h:h + 1], dn_ref[:, h:h + 1]
                pc = jnp.exp(jnp.where(cur, _dot_nt(qm, kc) * 0.125 - ls, NEG))
                pp = jnp.exp(jnp.where(prev, _dot_nt(qm, kp) * 0.125 - ls, NEG))
                px = jnp.exp(jnp.where(nxt, _dot_nt(qnm, kc) * 0.125 - lsn, NEG))
                dsc = (pc * (_dot_nt(dom, vc) - dh)).astype(BF16)
                dsp = (pp * (_dot_nt(dom, vp) - dh)).astype(BF16)
                dsx = (px * (_dot_nt(donm, vc) - dhn)).astype(BF16)
                dq.append((_dot(dsc, kc) + _dot(dsp, kp)) * 0.125)
                dk.append((_dot_tn(dsc, qm) + _dot_tn(dsx, qnm)) * 0.125)
                dv.append(_dot_tn(pc.astype(BF16), do2) + _dot_tn(px.astype(BF16), don2))
            lo = lane < 64
            g_ref[:, sl] = jnp.where(lo, dq[0], dq[1])
            g_ref[:, ATT_WIDTH + hp * 128:ATT_WIDTH + (hp + 1) * 128] = dk[0] + dk[1]
            g_ref[:, 2 * ATT_WIDTH + hp * 128:2 * ATT_WIDTH + (hp + 1) * 128] = jnp.where(lo, dv[0], dv[1])

    def blk(width, j, off):
        return pl.BlockSpec((ATT_BLOCK, width), lambda n: (jnp.clip(n + off, 0, nb - 1), j))

    w = ATT_WIDTH
    return _call(body, name=name, grid=(nb,),
                 in_specs=[blk(w, 0, 0), blk(w, 0, 1), blk(w, 1, 0), blk(w, 1, -1), blk(w, 2, 0), blk(w, 2, -1),
                           blk(w, 0, 0), blk(w, 0, 1), blk(128, 0, 0), blk(128, 0, 1), blk(128, 0, 0), blk(128, 0, 1)],
                 out_specs=pl.BlockSpec((ATT_BLOCK, 3 * w), lambda n: (n, 0)),
                 out_shape=_sds((t, 3 * w)))(qkv, qkv, qkv, qkv, qkv, qkv, dout, dout, lse, lse, dd, dd)


def _local_step(x, tgt, mod, norm1_g, lb_logits, og, ag, norm2_g, fg, w_in, w_out, w_gu, w_down):
    shift1, scale1, gate1, shift2, scale2, gate2 = [mod[:, i * D_MODEL:(i + 1) * D_MODEL] for i in range(6)]
    fg = fg.reshape(1, D_MODEL)

    h1 = _norm_mod(x, norm1_g, scale1, shift1, "norm_mod1")
    proj = _mm_nn(h1, w_in, "mm_in")
    o_hg, states = _hgrn_fwd(proj, lb_logits, "hgrn_fwd")
    qkv = proj[:, 4 * HG_WIDTH:].astype(BF16)
    qkvs = [_to_sub(qkv, d) for d in DILATIONS]
    outs = [_attn_fwd(q, d, f"attn_fwd{d}") for q, d in zip(qkvs, DILATIONS)]
    att, lse = _attn_combine([_from_sub(o, d) for (o, _), d in zip(outs, DILATIONS)],
                             [_from_sub(l, d) for (_, l), d in zip(outs, DILATIONS)], "attn_combine")
    mixin = _mix_in(o_hg, proj, att, og, ag, "mix_in")
    mix = _mm_nn(mixin, w_out, "mm_out")
    x2, h2 = _resid_norm_mod(x, mix, gate1, norm2_g, scale2, shift2, "resid_norm_mod2")
    au = _mm_nn(h2, w_gu, "mm_gu")
    act = _swiglu(au, "swiglu")
    ffn = _mm_nn(act, w_down, "mm_down")
    dx3, dffn, loss_v, dfg, dgate2 = _final_loss(x2, ffn, gate2, fg, tgt, "final_loss")

    dact = _mm_nt(dffn, w_down, "mm_down_dx")
    gw_down, gw_down_b = _mm_tn(act, dffn, 1, "mm_down_dw", tk=D_FF // 2)
    dau = _swiglu_bwd(dact, au, "swiglu_bwd")
    dh2 = _mm_nt(dau, w_gu, "mm_gu_dx", tm=1024)
    gw_gu, gw_gu_b = _mm_tn(h2, dau, N_SHARD, "mm_gu_dw")
    dx2, dshift2, dscale2, dg2, dgate1, dmix = _norm_mod_bwd(
        dh2, x2, norm2_g, scale2, dx3, "norm_mod2_bwd", gate=gate1, mix=mix)
    dmixin = _mm_nt(dmix, w_out, "mm_out_dx", tm=1024)
    gw_out, gw_out_b = _mm_tn(mixin, dmix, 1, "mm_out_dw")
    do_hg, dg_raw, datt, dd, dog, dag = _mix_in_bwd(dmixin, o_hg, proj, att, og, ag, "mix_in_bwd")
    datt_b = datt.astype(BF16)
    dqkv = [_from_sub(_attn_bwd(q, _to_sub(datt_b, d), _to_sub(lse, d), _to_sub(dd, d), d, f"attn_bwd{d}"), d)
            for q, d in zip(qkvs, DILATIONS)]
    dhg, dlb = _hgrn_bwd(proj, lb_logits, states, do_hg, "hgrn_bwd")
    dproj = _dproj(dhg, dg_raw, dqkv, "dproj")
    dh1 = _mm_nt(dproj, w_in, "mm_in_dx", tm=1024)
    gw_in, gw_in_b = _mm_tn(h1, dproj, N_SHARD, "mm_in_dw")
    dx, dshift1, dscale1, dg1 = _norm_mod_bwd(dh1, x, norm1_g, scale1, dx2, "norm_mod1_bwd")

    stats = jnp.concatenate([loss_v, dfg, dg2, dg1, dlb, dag, dog,
                             dshift1, dscale1, dgate1, dshift2, dscale2, dgate2], axis=1)
    return dx, (gw_in, gw_out, gw_gu, gw_down), (gw_in_b, gw_out_b, gw_gu_b, gw_down_b), stats


def _place():
    x, y, c = lax.axis_index("x"), lax.axis_index("y"), lax.axis_index("c")
    return x, y, c


def _chip_peers(x, y, c):
    return [(1 - x, y, c), (x, 1 - y, c), (1 - x, 1 - y, c)]


def _comm_call(body, name, n_in, out_shape, scratch_shapes):
    hbm = pl.BlockSpec(memory_space=pl.ANY)
    return pl.pallas_call(body, name=name, in_specs=[hbm] * n_in, out_specs=[hbm] * len(out_shape),
                          out_shape=out_shape, scratch_shapes=scratch_shapes)


def _gather_shards(ws, name):
    n = len(ws)

    def body(*refs):
        w_refs, o_refs, (send, recv, loc) = refs[:n], refs[n:2 * n], refs[2 * n:]
        x, y, c = _place()
        s_me = 2 * x + y
        own = [pltpu.make_async_copy(w_refs[a], o_refs[a].at[s_me], loc.at[a]) for a in range(n)]
        for cp in own:
            cp.start()
        peers = _chip_peers(x, y, c)
        sends = []
        for j, peer in enumerate(peers):
            for a in range(n):
                cp = pltpu.make_async_remote_copy(
                    src_ref=w_refs[a], dst_ref=o_refs[a].at[s_me], send_sem=send.at[j * n + a],
                    recv_sem=recv.at[j * n + a], device_id=peer, device_id_type=MESH)
                cp.start()
                sends.append(cp)
        for j, peer in enumerate(peers):
            s_peer = 2 * peer[0] + peer[1]
            for a in range(n):
                pltpu.make_async_remote_copy(
                    src_ref=w_refs[a], dst_ref=o_refs[a].at[s_peer], send_sem=send.at[j * n + a],
                    recv_sem=recv.at[j * n + a], device_id=peer, device_id_type=MESH).wait_recv()
        for cp in sends:
            cp.wait_send()
        for cp in own:
            cp.wait()

    return _comm_call(body, name, n, [_sds((N_SHARD,) + w.shape, w.dtype) for w in ws],
                      [pltpu.SemaphoreType.DMA((3 * n,)), pltpu.SemaphoreType.DMA((3 * n,)),
                       pltpu.SemaphoreType.DMA((n,))])(*ws)


def _scatter_shards(gs, name):
    n = len(gs)

    def body(*refs):
        g_refs, l_refs, (send, recv) = refs[:n], refs[n:2 * n], refs[2 * n:]
        x, y, c = _place()
        peers = _chip_peers(x, y, c)
        sends = []
        for j, peer in enumerate(peers):
            s_peer = 2 * peer[0] + peer[1]
            for a in range(n):
                cp = pltpu.make_async_remote_copy(
                    src_ref=g_refs[a].at[s_peer], dst_ref=l_refs[a].at[j], send_sem=send.at[j * n + a],
                    recv_sem=recv.at[j * n + a], device_id=peer, device_id_type=MESH)
                cp.start()
                sends.append(cp)
        for j, peer in enumerate(peers):
            for a in range(n):
                pltpu.make_async_remote_copy(
                    src_ref=g_refs[a].at[0], dst_ref=l_refs[a].at[j], send_sem=send.at[j * n + a],
                    recv_sem=recv.at[j * n + a], device_id=peer, device_id_type=MESH).wait_recv()
        for cp in sends:
            cp.wait_send()

    return _comm_call(body, name, n, [_sds((3,) + g.shape[1:], g.dtype) for g in gs],
                      [pltpu.SemaphoreType.DMA((3 * n,)), pltpu.SemaphoreType.DMA((3 * n,))])(*gs)


def _swap_sibling(vs, name):
    n = len(vs)

    def body(*refs):
        v_refs, o_refs, (send, recv) = refs[:n], refs[n:2 * n], refs[2 * n:]
        x, y, c = _place()
        cps = [pltpu.make_async_remote_copy(
            src_ref=v_refs[a], dst_ref=o_refs[a], send_sem=send.at[a], recv_sem=recv.at[a],
            device_id=(x, y, 1 - c), device_id_type=MESH) for a in range(n)]
        for cp in cps:
            cp.start()
        for cp in cps:
            cp.wait()

    return _comm_call(body, name, n, [_sds(v.shape, v.dtype) for v in vs],
                      [pltpu.SemaphoreType.DMA((n,)), pltpu.SemaphoreType.DMA((n,))])(*vs)


def _gather_rows(v, name):
    r, n = v.shape

    def body(v_ref, o_ref, send, recv, loc):
        x, y, c = _place()
        me = 4 * x + 2 * y + c
        own = pltpu.make_async_copy(v_ref, o_ref.at[me], loc)
        own.start()
        peers = []
        for k in range(1, 8):
            px = 1 - x if k & 4 else x
            py = 1 - y if k & 2 else y
            pc = 1 - c if k & 1 else c
            peers.append((px, py, pc))
        sends = []
        for k, peer in enumerate(peers):
            cp = pltpu.make_async_remote_copy(src_ref=v_ref, dst_ref=o_ref.at[me], send_sem=send.at[k],
                                              recv_sem=recv.at[k], device_id=peer, device_id_type=MESH)
            cp.start()
            sends.append(cp)
        for k, peer in enumerate(peers):
            pltpu.make_async_remote_copy(src_ref=v_ref, dst_ref=o_ref.at[4 * peer[0] + 2 * peer[1] + peer[2]],
                                         send_sem=send.at[k], recv_sem=recv.at[k], device_id=peer,
                                         device_id_type=MESH).wait_recv()
        for cp in sends:
            cp.wait_send()
        own.wait()

    vmem = pl.BlockSpec(memory_space=pltpu.VMEM)
    return pl.pallas_call(body, name=name, in_specs=[vmem], out_specs=vmem, out_shape=_sds((8, r, n), v.dtype),
                          scratch_shapes=[pltpu.SemaphoreType.DMA((7,)), pltpu.SemaphoreType.DMA((7,)),
                                          pltpu.SemaphoreType.DMA])(v)


def _cast_bf16(w, name):
    r, c = w.shape
    tr = r // 4

    def body(w_ref, o_ref):
        o_ref[...] = w_ref[...].astype(BF16)

    return _call(body, name=name, grid=(4,), in_specs=[_rows(tr, c)], out_specs=_rows(tr, c),
                 out_shape=_sds((r, c), BF16))(w)


def _mod_part(c_all, w_ada, b_ada, name):
    n = w_ada.shape[1]

    def body(c_ref, w_ref, b_ref, a_ref, p_ref):
        cv = c_ref[...]
        ca = cv * _sigmoid(cv)
        a_ref[...] = ca
        p_ref[...] = jnp.dot(ca, w_ref[...], precision=lax.Precision.HIGHEST, preferred_element_type=F32) + b_ref[...]

    full = lambda a: pl.BlockSpec(a.shape, lambda i: (0, 0))
    return _call(body, name=name, grid=(1,), in_specs=[full(c_all), full(w_ada), full(b_ada)],
                 out_specs=[pl.BlockSpec((8, D_MODEL), lambda i: (0, 0)), pl.BlockSpec((8, n), lambda i: (0, 0))],
                 out_shape=[_sds((8, D_MODEL)), _sds((8, n))])(c_all, w_ada, b_ada)


def _sum_received(g_own, land, name):
    r, c = g_own.shape
    tr = r // 4

    def body(g_ref, l_ref, o_ref):
        o_ref[...] = ((g_ref[...] + l_ref[0].astype(F32)) + l_ref[1].astype(F32)) + l_ref[2].astype(F32)

    return _call(body, name=name, grid=(4,),
                 in_specs=[_rows(tr, c), pl.BlockSpec((3, tr, c), lambda i: (0, i, 0))],
                 out_specs=_rows(tr, c), out_shape=_sds((r, c)))(g_own, land)


def _outer_sum(ct, dm, name):
    k, n = ct.shape[0], dm.shape[1]
    tr = k // 4

    def body(c_ref, d_ref, o_ref):
        cv = c_ref[...]
        dv = d_ref[...]
        acc = cv[:, 0:1] * dv[0:1, :]
        for i in range(1, 8):
            acc = acc + cv[:, i:i + 1] * dv[i:i + 1, :]
        o_ref[...] = acc

    return _call(body, name=name, grid=(4,), in_specs=[_rows(tr, 8), pl.BlockSpec((8, n), lambda i: (0, 0))],
                 out_specs=_rows(tr, n), out_shape=_sds((k, n)))(ct, dm)


def _small_grads(stats, lb_logits, name):
    def body(s_ref, lg_ref, g_ref, l_ref):
        tot = s_ref[0:1, :]
        for i in range(1, 8):
            tot = tot + s_ref[i:i + 1, :]
        part = lambda off, n: tot[:, off:off + n]
        l_ref[...] = jnp.zeros((1, 128), F32) + (0.5 / D_MODEL) * jnp.sum(part(ST_LOSS, D_MODEL))
        lg = lg_ref[...]
        lb = _sigmoid(lg[0:1] - lg[1:2])
        dl0 = part(ST_DLB, HG_WIDTH) * lb * (1.0 - lb)
        g_ref[:, SP_BADA:SP_BADA + 6 * D_MODEL] = part(ST_DMOD, 6 * D_MODEL)
        g_ref[:, SP_N1:SP_N1 + D_MODEL] = part(ST_DG1, D_MODEL)
        g_ref[:, SP_LB:SP_LB + HG_WIDTH] = dl0
        g_ref[:, SP_LB + HG_WIDTH:SP_LB + 2 * HG_WIDTH] = -dl0
        g_ref[:, SP_OG:SP_OG + HG_HEAD] = part(ST_DOG, HG_HEAD)
        g_ref[:, SP_AG:SP_AG + ATT_WIDTH] = part(ST_DAG, ATT_WIDTH)
        g_ref[:, SP_N2:SP_N2 + D_MODEL] = part(ST_DG2, D_MODEL)
        g_ref[:, SP_FG:SP_FG + D_MODEL] = part(ST_DFG, D_MODEL)

    return _call(body, name=name, grid=(1,),
                 in_specs=[pl.BlockSpec((8, ST_WIDTH), lambda i: (0, 0)), pl.BlockSpec((2, HG_WIDTH), lambda i: (0, 0))],
                 out_specs=[pl.BlockSpec((1, SP_WIDTH), lambda i: (0, 0)), pl.BlockSpec((1, 128), lambda i: (0, 0))],
                 out_shape=[_sds((1, SP_WIDTH)), _sds((1, 128))])(stats, lb_logits)


def _adamw(w, gs, m, v, name, steps=4):
    r, c = w.shape
    tr = r // steps
    ng = len(gs)

    def body(*refs):
        w_ref, g_refs, (m_ref, v_ref, g_out, d_out, m_out, v_out) = refs[0], refs[1:1 + ng], refs[1 + ng:]
        g = g_refs[0][...]
        for g_ref in g_refs[1:]:
            g = g + g_ref[...]
        m_new = ADAM_B1 * m_ref[...] + (1.0 - ADAM_B1) * g
        v_new = ADAM_B2 * v_ref[...] + (1.0 - ADAM_B2) * (g * g)
        m_hat = m_new / (1.0 - ADAM_B1 ** ADAM_STEP)
        v_hat = v_new / (1.0 - ADAM_B2 ** ADAM_STEP)
        g_out[...] = g
        d_out[...] = -ADAM_LR * (m_hat / (jnp.sqrt(v_hat) + ADAM_EPS) + ADAM_WD * w_ref[...])
        m_out[...] = m_new
        v_out[...] = v_new

    row = _rows(tr, c)
    return _call(body, name=name, grid=(steps,), in_specs=[row] * (3 + ng), out_specs=[row] * 4,
                 out_shape=[_sds((r, c))] * 4)(w, *gs, m, v)


def kernel(x, c, w_ada, b_ada, norm1_g, w_in, hg_lb_logits, hg_onorm_g, att_onorm_g, w_out, norm2_g, w_gate_up, w_down, final_g, loss_target, m_w_ada, m_b_ada, m_norm1_g, m_w_in, m_hg_lb_logits, m_hg_onorm_g, m_att_onorm_g, m_w_out, m_norm2_g, m_w_gate_up, m_w_down, m_final_g, v_w_ada, v_b_ada, v_norm1_g, v_w_in, v_hg_lb_logits, v_hg_onorm_g, v_att_onorm_g, v_w_out, v_norm2_g, v_w_gate_up, v_w_down, v_final_g):
    ix, iy, ic = _place()
    shard = 2 * ix + iy
    sample = 4 * ix + 2 * iy + ic
    n_ada = w_ada.shape[2]

    c_all = _gather_rows(c, "gather_c").reshape(8, D_MODEL)
    b_part = lax.dynamic_slice(b_ada, (0, shard * n_ada), (1, n_ada))
    c_act, part = _mod_part(c_all, w_ada[0], b_part, "mod_part")
    parts = _gather_rows(part, "gather_mod")[::2]
    mod = lax.dynamic_index_in_dim(parts, sample, axis=1, keepdims=False).reshape(1, 6 * D_MODEL)

    shards = [w_in[0], w_out[0], w_gate_up[0], w_down[0]]
    names = ["w_in", "w_out", "w_gu", "w_down"]
    full = _gather_shards([_cast_bf16(w, "cast_" + nm) for w, nm in zip(shards, names)], "gather_weights")
    w_in_b, w_gu_b = full[0], full[2]
    w_out_b = full[1].reshape(1, D_MODEL, D_MODEL)
    w_down_b = full[3].reshape(1, D_FF, D_MODEL)

    dx, gws, gwbs, stats = _local_step(x[0], loss_target[0], mod, norm1_g, hg_lb_logits, hg_onorm_g, att_onorm_g,
                                       norm2_g, final_g, w_in_b, w_out_b, w_gu_b, w_down_b)

    shapes = [(N_SHARD,) + w.shape for w in shards]
    land = _scatter_shards([g.reshape(s) for g, s in zip(gwbs, shapes)], "scatter_grads")
    own = [lax.dynamic_index_in_dim(g.reshape(s), shard, axis=0, keepdims=False) for g, s in zip(gws, shapes)]
    sums = [_sum_received(g, l, "sum_" + nm) for g, l, nm in zip(own, land, names)]
    other = _swap_sibling(sums, "swap_sums")

    stats_all = _gather_rows(stats, "gather_stats").reshape(8, ST_WIDTH)
    g_small, loss = _small_grads(stats_all, hg_lb_logits, "small_grads")
    dmod = lax.dynamic_slice(stats_all, (0, ST_DMOD + shard * n_ada), (8, n_ada))
    g_ada = _outer_sum(c_act.T, dmod, "w_ada_grad")

    smalls = [(b_ada, m_b_ada, v_b_ada), (norm1_g, m_norm1_g, v_norm1_g),
              (hg_lb_logits, m_hg_lb_logits, v_hg_lb_logits), (hg_onorm_g, m_hg_onorm_g, v_hg_onorm_g),
              (att_onorm_g, m_att_onorm_g, v_att_onorm_g), (norm2_g, m_norm2_g, v_norm2_g),
              (final_g, m_final_g, v_final_g)]
    pack = lambda i: jnp.concatenate([t[i].reshape(1, -1) for t in smalls], axis=1)
    small_out = _adamw(pack(0), [g_small], pack(1), pack(2), "adamw_small", steps=1)
    offs = [SP_BADA, SP_N1, SP_LB, SP_OG, SP_AG, SP_N2, SP_FG, SP_WIDTH]
    unpack = lambda a: [a[0, offs[i]:offs[i + 1]].reshape(smalls[i][0].shape) for i in range(7)]
    sg, sd, sm, sv = [unpack(a) for a in small_out]

    big = [_adamw(w_ada[0], [g_ada], m_w_ada[0], v_w_ada[0], "adamw_w_ada")]
    moments = [(m_w_in, v_w_in), (m_w_out, v_w_out), (m_w_gate_up, v_w_gate_up), (m_w_down, v_w_down)]
    for w, s, o, (m, v), nm in zip(shards, sums, other, moments, names):
        big.append(_adamw(w, [s, o], m[0], v[0], "adamw_" + nm))
    bg, bd, bm, bv = [[t[i][None] for t in big] for i in range(4)]

    def order(b, s):
        return [b[0], s[0], s[1], b[1], s[2], s[3], s[4], b[2], s[5], b[3], b[4], s[6]]

    return (loss[0, 0], dx[None], *order(bg, sg), *order(bd, sd), *order(bm, sm), *order(bv, sv))
```

```python
import functools

import jax
import jax.numpy as jnp
from jax import lax
from jax.experimental import pallas as pl
from jax.experimental.pallas import tpu as pltpu

F32 = jnp.float32
BF16 = jnp.bfloat16
MESH = pl.DeviceIdType.MESH

D_MODEL = 1024
HG_WIDTH = 512
HG_HEAD = 128
HG_CHUNK = 64
ATT_WIDTH = 512
ATT_HEADS = 8
ATT_BLOCK = 128
DILATIONS = (1, 4, 16)
D_FF = 2816
IN_WIDTH = 3584
N_SHARD = 4
RMS_EPS = 1e-6
NEG = -1e30

ADAM_LR = 0.001
ADAM_B1 = 0.9
ADAM_B2 = 0.999
ADAM_EPS = 1e-08
ADAM_WD = 0.01
ADAM_STEP = 10

VMEM_LIMIT = 56 * 2**20

ST_LOSS, ST_DFG, ST_DG2, ST_DG1 = 0, 1024, 2048, 3072
ST_DLB, ST_DAG, ST_DOG, ST_DMOD = 4096, 4608, 5120, 5248
ST_WIDTH = 5248 + 6144
SP_BADA, SP_N1, SP_LB, SP_OG, SP_AG, SP_N2, SP_FG = 0, 6144, 7168, 8192, 8320, 8832, 9856
SP_WIDTH = 10880


def _call(body, *, name, grid, in_specs, out_specs, out_shape, scratch_shapes=()):
    return pl.pallas_call(
        body, name=name, grid=grid, in_specs=in_specs, out_specs=out_specs, out_shape=out_shape,
        scratch_shapes=list(scratch_shapes),
        compiler_params=pltpu.CompilerParams(
            dimension_semantics=("arbitrary",) * len(grid), vmem_limit_bytes=VMEM_LIMIT))


def _sds(shape, dtype=F32):
    return jax.ShapeDtypeStruct(shape, dtype)


def _dot(a, b):
    return jnp.dot(a, b, preferred_element_type=F32)


def _dot_nt(a, b):
    return lax.dot_general(a, b, (((1,), (1,)), ((), ())), preferred_element_type=F32)


def _dot_tn(a, b):
    return lax.dot_general(a, b, (((0,), (0,)), ((), ())), preferred_element_type=F32)


def _sigmoid(x):
    return 1.0 / (1.0 + jnp.exp(-x))


def _rows(tr, width):
    return pl.BlockSpec((tr, width), lambda i: (i, 0))


def _vec(width):
    return pl.BlockSpec((1, width), lambda i: (0, 0))


def _acc(ref, val, first):
    @pl.when(first)
    def _():
        ref[...] = val

    @pl.when(jnp.logical_not(first))
    def _():
        ref[...] += val


def _mm_nn(a, b3, name, tm=512):
    m, k = a.shape
    s, _, n = b3.shape

    def body(a_ref, b_ref, o_ref):
        o_ref[...] = _dot(a_ref[...], b_ref[0])

    return _call(
        body, name=name, grid=(s, m // tm),
        in_specs=[pl.BlockSpec((tm, k), lambda j, i: (i, 0)), pl.BlockSpec((1, k, n), lambda j, i: (j, 0, 0))],
        out_specs=pl.BlockSpec((tm, n), lambda j, i: (i, j)), out_shape=_sds((m, s * n)))(a, b3)


def _mm_nt(dy, b3, name, tm=512):
    m = dy.shape[0]
    s, k, n = b3.shape

    def body(dy_ref, b_ref, o_ref):
        _acc(o_ref, _dot_nt(dy_ref[...], b_ref[0]), pl.program_id(1) == 0)

    return _call(
        body, name=name, grid=(m // tm, s),
        in_specs=[pl.BlockSpec((tm, n), lambda i, j: (i, j)), pl.BlockSpec((1, k, n), lambda i, j: (j, 0, 0))],
        out_specs=pl.BlockSpec((tm, k), lambda i, j: (i, 0)), out_shape=_sds((m, k)))(dy, b3)


def _mm_tn(a, dy, s, name, tm=512, tk=None):
    m, k = a.shape
    n = dy.shape[1] // s
    tk = k if tk is None else tk
    steps = m // tm

    def body(a_ref, dy_ref, o_ref, ob_ref):
        i = pl.program_id(2)
        _acc(o_ref, _dot_tn(a_ref[...], dy_ref[...])[None], i == 0)

        @pl.when(i == steps - 1)
        def _():
            ob_ref[...] = o_ref[...].astype(BF16)

    out = pl.BlockSpec((1, tk, n), lambda kk, j, i: (j, kk, 0))
    return _call(
        body, name=name, grid=(k // tk, s, steps),
        in_specs=[pl.BlockSpec((tm, tk), lambda kk, j, i: (i, kk)), pl.BlockSpec((tm, n), lambda kk, j, i: (i, j))],
        out_specs=[out, out], out_shape=[_sds((s, k, n)), _sds((s, k, n), BF16)])(a, dy)


def _rms(x):
    return lax.rsqrt(jnp.mean(x * x, axis=-1, keepdims=True) + RMS_EPS)


def _rms_bwd(dxh, xh, r):
    return r * (dxh - xh * jnp.mean(dxh * xh, axis=-1, keepdims=True))


def _norm_mod(x, g, scale, shift, name, tr=512):
    t = x.shape[0]

    def body(x_ref, g_ref, sc_ref, sh_ref, h_ref):
        xv = x_ref[...]
        n = xv * _rms(xv) * g_ref[...]
        h_ref[...] = (n * (1.0 + sc_ref[...]) + sh_ref[...]).astype(BF16)

    return _call(body, name=name, grid=(t // tr,),
                 in_specs=[_rows(tr, D_MODEL), _vec(D_MODEL), _vec(D_MODEL), _vec(D_MODEL)],
                 out_specs=_rows(tr, D_MODEL), out_shape=_sds((t, D_MODEL), BF16))(x, g, scale, shift)


def _mix_in(o_hg, proj, att, og, ag, name, tr=256):
    t = o_hg.shape[0]

    def body(o_ref, g_ref, a_ref, og_ref, ag_ref, m_ref):
        for h in range(HG_WIDTH // HG_HEAD):
            sl = slice(h * HG_HEAD, (h + 1) * HG_HEAD)
            oh = o_ref[:, sl]
            gv = g_ref[:, sl]
            m_ref[:, sl] = (oh * _rms(oh) * og_ref[...] * (gv * _sigmoid(gv))).astype(BF16)
        av = a_ref[...]
        m_ref[:, HG_WIDTH:] = (av * _rms(av) * ag_ref[...]).astype(BF16)

    return _call(body, name=name, grid=(t // tr,),
                 in_specs=[_rows(tr, HG_WIDTH), pl.BlockSpec((tr, HG_WIDTH), lambda i: (i, 3)), _rows(tr, ATT_WIDTH),
                           _vec(HG_HEAD), _vec(ATT_WIDTH)],
                 out_specs=_rows(tr, D_MODEL), out_shape=_sds((t, D_MODEL), BF16))(o_hg, proj, att, og, ag)


def _resid_norm_mod(x, mix, gate, g, scale, shift, name, tr=256):
    t = x.shape[0]

    def body(x_ref, m_ref, gt_ref, g_ref, sc_ref, sh_ref, x2_ref, h_ref):
        x2 = x_ref[...] + gt_ref[...] * m_ref[...]
        x2_ref[...] = x2
        n = x2 * _rms(x2) * g_ref[...]
        h_ref[...] = (n * (1.0 + sc_ref[...]) + sh_ref[...]).astype(BF16)

    return _call(body, name=name, grid=(t // tr,),
                 in_specs=[_rows(tr, D_MODEL), _rows(tr, D_MODEL)] + [_vec(D_MODEL)] * 4,
                 out_specs=[_rows(tr, D_MODEL), _rows(tr, D_MODEL)],
                 out_shape=[_sds((t, D_MODEL)), _sds((t, D_MODEL), BF16)])(x, mix, gate, g, scale, shift)


def _swiglu(au, name, tr=256):
    t = au.shape[0]

    def body(a_ref, u_ref, o_ref):
        a = a_ref[...]
        o_ref[...] = (a * _sigmoid(a) * u_ref[...]).astype(BF16)

    return _call(body, name=name, grid=(t // tr,),
                 in_specs=[pl.BlockSpec((tr, D_FF), lambda i: (i, 0)), pl.BlockSpec((tr, D_FF), lambda i: (i, 1))],
                 out_specs=_rows(tr, D_FF), out_shape=_sds((t, D_FF), BF16))(au, au)


def _swiglu_bwd(dact, au, name, tr=256):
    t = au.shape[0]

    def body(d_ref, a_ref, u_ref, o_ref):
        a = a_ref[...]
        d = d_ref[...]
        sg = _sigmoid(a)
        o_ref[:, :D_FF] = (d * u_ref[...] * sg * (1.0 + a * (1.0 - sg))).astype(BF16)
        o_ref[:, D_FF:] = (d * a * sg).astype(BF16)

    return _call(body, name=name, grid=(t // tr,),
                 in_specs=[_rows(tr, D_FF), pl.BlockSpec((tr, D_FF), lambda i: (i, 0)),
                           pl.BlockSpec((tr, D_FF), lambda i: (i, 1))],
                 out_specs=_rows(tr, 2 * D_FF), out_shape=_sds((t, 2 * D_FF), BF16))(dact, au, au)


def _final_loss(x2, ffn, gate, fg, tgt, name, tr=256):
    t = x2.shape[0]

    def body(x_ref, f_ref, gt_ref, fg_ref, t_ref, dx_ref, df_ref, l_ref, dfg_ref, dgt_ref):
        first = pl.program_id(0) == 0
        ffn_v = f_ref[...]
        x3 = x_ref[...] + gt_ref[...] * ffn_v
        r = _rms(x3)
        xh = x3 * r
        err = xh * fg_ref[...] - t_ref[...]
        dy = err * (1.0 / D_MODEL)
        dx3 = _rms_bwd(dy * fg_ref[...], xh, r)
        dx_ref[...] = dx3
        df_ref[...] = (dx3 * gt_ref[...]).astype(BF16)
        _acc(l_ref, jnp.sum(err * err, axis=0, keepdims=True), first)
        _acc(dfg_ref, jnp.sum(dy * xh, axis=0, keepdims=True), first)
        _acc(dgt_ref, jnp.sum(dx3 * ffn_v, axis=0, keepdims=True), first)

    row, vec = _rows(tr, D_MODEL), _vec(D_MODEL)
    return _call(body, name=name, grid=(t // tr,), in_specs=[row, row, vec, vec, row],
                 out_specs=[row, row, vec, vec, vec],
                 out_shape=[_sds((t, D_MODEL)), _sds((t, D_MODEL), BF16)] + [_sds((1, D_MODEL))] * 3)(
                     x2, ffn, gate, fg, tgt)


def _norm_mod_bwd(dh, x, g, scale, dres, name, gate=None, mix=None, tr=256):
    t = x.shape[0]
    below = gate is not None

    def body(*refs):
        if below:
            dh_ref, x_ref, g_ref, sc_ref, dr_ref, gt_ref, m_ref, dx_ref, dsh_ref, dsc_ref, dg_ref, dgt_ref, dm_ref = refs
        else:
            dh_ref, x_ref, g_ref, sc_ref, dr_ref, dx_ref, dsh_ref, dsc_ref, dg_ref = refs
        first = pl.program_id(0) == 0
        xv = x_ref[...]
        dhv = dh_ref[...]
        r = _rms(xv)
        xh = xv * r
        dn = dhv * (1.0 + sc_ref[...])
        dx = dr_ref[...] + _rms_bwd(dn * g_ref[...], xh, r)
        dx_ref[...] = dx
        _acc(dsh_ref, jnp.sum(dhv, axis=0, keepdims=True), first)
        _acc(dsc_ref, jnp.sum(dhv * xh * g_ref[...], axis=0, keepdims=True), first)
        _acc(dg_ref, jnp.sum(dn * xh, axis=0, keepdims=True), first)
        if below:
            _acc(dgt_ref, jnp.sum(dx * m_ref[...], axis=0, keepdims=True), first)
            dm_ref[...] = (dx * gt_ref[...]).astype(BF16)

    row, vec = _rows(tr, D_MODEL), _vec(D_MODEL)
    in_specs = [row, row, vec, vec, row] + ([vec, row] if below else [])
    out_specs = [row, vec, vec, vec] + ([vec, row] if below else [])
    out_shape = [_sds((t, D_MODEL))] + [_sds((1, D_MODEL))] * 3 + ([_sds((1, D_MODEL)), _sds((t, D_MODEL), BF16)] if below else [])
    args = (dh, x, g, scale, dres) + ((gate, mix) if below else ())
    return _call(body, name=name, grid=(t // tr,), in_specs=in_specs, out_specs=out_specs, out_shape=out_shape)(*args)


def _mix_in_bwd(dmi, o_hg, proj, att, og, ag, name, tr=256):
    t = o_hg.shape[0]

    def body(d_ref, o_ref, g_ref, a_ref, og_ref, ag_ref, do_ref, dg_ref, da_ref, dd_ref, dog_ref, dag_ref):
        first = pl.program_id(0) == 0
        dog = jnp.zeros((1, HG_HEAD), F32)
        for h in range(HG_WIDTH // HG_HEAD):
            sl = slice(h * HG_HEAD, (h + 1) * HG_HEAD)
            oh = o_ref[:, sl]
            gv = g_ref[:, sl]
            dv = d_ref[:, sl]
            r = _rms(oh)
            xh = oh * r
            sg = _sigmoid(gv)
            dno = dv * gv * sg
            dg_ref[:, sl] = dv * xh * og_ref[...] * sg * (1.0 + gv * (1.0 - sg))
            dog = dog + jnp.sum(dno * xh, axis=0, keepdims=True)
            do_ref[:, sl] = _rms_bwd(dno * og_ref[...], xh, r)
        _acc(dog_ref, dog, first)
        av = a_ref[...]
        dav = d_ref[:, HG_WIDTH:]
        r = _rms(av)
        xa = av * r
        _acc(dag_ref, jnp.sum(dav * xa, axis=0, keepdims=True), first)
        datt = _rms_bwd(dav * ag_ref[...], xa, r)
        da_ref[...] = datt
        prod = datt * av
        lane = lax.broadcasted_iota(jnp.int32, (1, 128), 1)
        dd = jnp.zeros((tr, 128), F32)
        for hp in range(ATT_HEADS // 2):
            pp = prod[:, hp * 128:(hp + 1) * 128]
            lo = jnp.sum(jnp.where(lane < 64, pp, 0.0), axis=-1, keepdims=True)
            hi = jnp.sum(jnp.where(lane >= 64, pp, 0.0), axis=-1, keepdims=True)
            dd = jnp.where(lane == 2 * hp, lo, dd)
            dd = jnp.where(lane == 2 * hp + 1, hi, dd)
        dd_ref[...] = dd

    half = _rows(tr, HG_WIDTH)
    return _call(body, name=name, grid=(t // tr,),
                 in_specs=[_rows(tr, D_MODEL), half, pl.BlockSpec((tr, HG_WIDTH), lambda i: (i, 3)), half,
                           _vec(HG_HEAD), _vec(ATT_WIDTH)],
                 out_specs=[half, half, half, _rows(tr, 128), _vec(HG_HEAD), _vec(ATT_WIDTH)],
                 out_shape=[_sds((t, HG_WIDTH))] * 3 + [_sds((t, 128)), _sds((1, HG_HEAD)), _sds((1, ATT_WIDTH))])(
                     dmi, o_hg, proj, att, og, ag)


def _dproj(dhg, dg, dqkv, name, tr=256):
    t = dhg.shape[0]
    w3 = 3 * HG_WIDTH

    def body(h_ref, g_ref, a1_ref, a2_ref, a3_ref, o_ref):
        o_ref[:, :w3] = h_ref[...].astype(BF16)
        o_ref[:, w3:w3 + HG_WIDTH] = g_ref[...].astype(BF16)
        o_ref[:, w3 + HG_WIDTH:] = (a1_ref[...] + a2_ref[...] + a3_ref[...]).astype(BF16)

    return _call(body, name=name, grid=(t // tr,),
                 in_specs=[_rows(tr, w3), _rows(tr, HG_WIDTH)] + [_rows(tr, w3)] * 3,
                 out_specs=_rows(tr, IN_WIDTH), out_shape=_sds((t, IN_WIDTH), BF16))(dhg, dg, *dqkv)


def _hg_gates(f_raw, q_raw, lb, tri):
    sg = _sigmoid(f_raw)
    f = lb + (1.0 - lb) * sg
    k = 1.0 - f
    b = jnp.dot(tri, jnp.log(f), precision=lax.Precision.HIGHEST, preferred_element_type=F32)
    sq = _sigmoid(q_raw)
    return sg, f, k, b, sq


def _hgrn_fwd(proj, lb_logits, name):
    t = proj.shape[0]
    nc = t // HG_CHUNK
    nh = HG_WIDTH // HG_HEAD

    def body(q_ref, f_ref, i_ref, lg_ref, o_ref, st_ref, s_scr):
        @pl.when(pl.program_id(0) == 0)
        def _():
            s_scr[...] = jnp.zeros_like(s_scr)

        lg = lg_ref[...]
        lb_all = _sigmoid(lg[0:1] - lg[1:2])
        row = lax.broadcasted_iota(jnp.int32, (HG_CHUNK, HG_CHUNK), 0)
        col = lax.broadcasted_iota(jnp.int32, (HG_CHUNK, HG_CHUNK), 1)
        causal = row >= col
        tri = causal.astype(F32)
        for h in range(nh):
            sl = slice(h * HG_HEAD, (h + 1) * HG_HEAD)
            q_raw = q_ref[:, sl]
            _, _, k, b, sq = _hg_gates(f_ref[:, sl], q_raw, lb_all[:, sl], tri)
            v = i_ref[:, sl].astype(BF16)
            gl = b[HG_CHUNK - 1:HG_CHUNK]
            qd = (q_raw * sq * jnp.exp(b)).astype(BF16)
            kd = (k * jnp.exp(-b)).astype(BF16)
            ke = (k * jnp.exp(gl - b)).astype(BF16)
            st = s_scr[h]
            st_ref[0, sl, :] = st
            a = jnp.where(causal, _dot_nt(qd, kd), 0.0).astype(BF16)
            o_ref[:, sl] = _dot_nt(qd, st.astype(BF16)) + _dot(a, v)
            s_scr[h] = st * jnp.exp(gl) + _dot_tn(v, ke)

    blk = lambda j: pl.BlockSpec((HG_CHUNK, HG_WIDTH), lambda c: (c, j))
    return _call(body, name=name, grid=(nc,),
                 in_specs=[blk(0), blk(1), blk(2), pl.BlockSpec((2, HG_WIDTH), lambda c: (0, 0))],
                 out_specs=[blk(0), pl.BlockSpec((1, HG_WIDTH, HG_HEAD), lambda c: (c, 0, 0))],
                 out_shape=[_sds((t, HG_WIDTH)), _sds((nc, HG_WIDTH, HG_HEAD))],
                 scratch_shapes=[pltpu.VMEM((nh, HG_HEAD, HG_HEAD), F32)])(proj, proj, proj, lb_logits)


def _hgrn_bwd(proj, lb_logits, states, do, name):
    t = proj.shape[0]
    nc = t // HG_CHUNK
    nh = HG_WIDTH // HG_HEAD

    def body(q_ref, f_ref, i_ref, lg_ref, st_ref, do_ref, d_ref, dlb_ref, ds_scr):
        first = pl.program_id(0) == 0

        @pl.when(first)
        def _():
            ds_scr[...] = jnp.zeros_like(ds_scr)

        lg = lg_ref[...]
        lb_all = _sigmoid(lg[0:1] - lg[1:2])
        row = lax.broadcasted_iota(jnp.int32, (HG_CHUNK, HG_CHUNK), 0)
        col = lax.broadcasted_iota(jnp.int32, (HG_CHUNK, HG_CHUNK), 1)
        causal = row >= col
        tri = causal.astype(F32)
        tri_t = (row <= col).astype(F32)
        dlb = []
        for h in range(nh):
            sl = slice(h * HG_HEAD, (h + 1) * HG_HEAD)
            q_raw = q_ref[:, sl]
            lb = lb_all[:, sl]
            sg, f, k, b, sq = _hg_gates(f_ref[:, sl], q_raw, lb, tri)
            v = i_ref[:, sl].astype(BF16)
            gl = b[HG_CHUNK - 1:HG_CHUNK]
            egl = jnp.exp(gl)
            eb = jnp.exp(b)
            enb = jnp.exp(-b)
            egb = jnp.exp(gl - b)
            qd = q_raw * sq * eb
            kd = k * enb
            ke = k * egb
            qd_b, kd_b, ke_b = qd.astype(BF16), kd.astype(BF16), ke.astype(BF16)
            st = st_ref[0, sl, :]
            dst = ds_scr[h]
            dst_b = dst.astype(BF16)
            dov = do_ref[:, sl].astype(BF16)
            a = jnp.where(causal, _dot_nt(qd_b, kd_b), 0.0).astype(BF16)
            da = jnp.where(causal, _dot_nt(dov, v), 0.0).astype(BF16)
            dqd = _dot(dov, st.astype(BF16)) + _dot(da, kd_b)
            dkd = _dot_tn(da, qd_b)
            dv = _dot_tn(a, dov) + _dot_nt(ke_b, dst_b)
            dke = _dot(v, dst_b)
            dgl = jnp.sum(dst * st, axis=0, keepdims=True) * egl
            ds_scr[h] = _dot_tn(dov, qd_b) + dst * egl
            t1 = dke * ke
            db = dqd * qd_b.astype(F32) - dkd * kd_b.astype(F32) - t1
            dgl = dgl + jnp.sum(t1, axis=0, keepdims=True)
            dlf = jnp.dot(tri_t, db, precision=lax.Precision.HIGHEST, preferred_element_type=F32) + dgl
            df = dlf / f - (dkd * enb + dke * egb)
            d_ref[:, sl] = dqd * eb * sq * (1.0 + q_raw * (1.0 - sq))
            d_ref[:, HG_WIDTH + h * HG_HEAD:HG_WIDTH + (h + 1) * HG_HEAD] = df * (1.0 - lb) * sg * (1.0 - sg)
            d_ref[:, 2 * HG_WIDTH + h * HG_HEAD:2 * HG_WIDTH + (h + 1) * HG_HEAD] = dv
            dlb.append(jnp.sum(df * (1.0 - sg), axis=0, keepdims=True))
        _acc(dlb_ref, jnp.concatenate(dlb, axis=1), first)

    rev = lambda j: pl.BlockSpec((HG_CHUNK, HG_WIDTH), lambda c: (nc - 1 - c, j))
    return _call(body, name=name, grid=(nc,),
                 in_specs=[rev(0), rev(1), rev(2), pl.BlockSpec((2, HG_WIDTH), lambda c: (0, 0)),
                           pl.BlockSpec((1, HG_WIDTH, HG_HEAD), lambda c: (nc - 1 - c, 0, 0)), rev(0)],
                 out_specs=[pl.BlockSpec((HG_CHUNK, 3 * HG_WIDTH), lambda c: (nc - 1 - c, 0)), _vec(HG_WIDTH)],
                 out_shape=[_sds((t, 3 * HG_WIDTH)), _sds((1, HG_WIDTH))],
                 scratch_shapes=[pltpu.VMEM((nh, HG_HEAD, HG_HEAD), F32)])(proj, proj, proj, lb_logits, states, do)


def _to_sub(a, dil):
    t, w = a.shape
    return a if dil == 1 else a.reshape(t // dil, dil, w).transpose(1, 0, 2).reshape(t, w)


def _from_sub(a, dil):
    t, w = a.shape
    return a if dil == 1 else a.reshape(dil, t // dil, w).transpose(1, 0, 2).reshape(t, w)


def _att_masks():
    qi = lax.broadcasted_iota(jnp.int32, (ATT_BLOCK, ATT_BLOCK), 0)
    kj = lax.broadcasted_iota(jnp.int32, (ATT_BLOCK, ATT_BLOCK), 1)
    lane = lax.broadcasted_iota(jnp.int32, (1, 128), 1)
    return kj <= qi, kj >= qi, lane


def _attn_fwd(qkv, dil, name):
    t = qkv.shape[0]
    nb = t // ATT_BLOCK
    bps = nb // dil

    def body(q_ref, kc_ref, kp_ref, vc_ref, vp_ref, o_ref, l_ref):
        cur, prev, lane = _att_masks()
        prev = jnp.logical_and(prev, (pl.program_id(0) % bps) != 0)
        lse_all = jnp.zeros((ATT_BLOCK, 128), F32)
        for hp in range(ATT_HEADS // 2):
            sl = slice(hp * 128, (hp + 1) * 128)
            q2, kc, kp, vc, vp = q_ref[:, sl], kc_ref[:, sl], kp_ref[:, sl], vc_ref[:, sl], vp_ref[:, sl]
            outs = []
            for hh in range(2):
                mine = (lane < 64) if hh == 0 else (lane >= 64)
                qm = jnp.where(mine, q2, jnp.zeros_like(q2))
                sc = jnp.where(cur, _dot_nt(qm, kc) * 0.125, NEG)
                sp = jnp.where(prev, _dot_nt(qm, kp) * 0.125, NEG)
                mx = jnp.maximum(jnp.max(sc, axis=-1, keepdims=True), jnp.max(sp, axis=-1, keepdims=True))
                pc = jnp.exp(sc - mx)
                pp = jnp.exp(sp - mx)
                l = jnp.sum(pc, axis=-1, keepdims=True) + jnp.sum(pp, axis=-1, keepdims=True)
                acc = _dot(pc.astype(BF16), vc) + _dot(pp.astype(BF16), vp)
                outs.append(acc / l)
                lse_all = jnp.where(lane == 2 * hp + hh, mx + jnp.log(l), lse_all)
            o_ref[:, sl] = jnp.where(lane < 64, outs[0], outs[1])
        l_ref[...] = lse_all

    blk = lambda j, back: pl.BlockSpec((ATT_BLOCK, ATT_WIDTH), lambda n: (jnp.maximum(n - back, 0), j))
    return _call(body, name=name, grid=(nb,),
                 in_specs=[blk(0, 0), blk(1, 0), blk(1, 1), blk(2, 0), blk(2, 1)],
                 out_specs=[pl.BlockSpec((ATT_BLOCK, ATT_WIDTH), lambda n: (n, 0)),
                            pl.BlockSpec((ATT_BLOCK, 128), lambda n: (n, 0))],
                 out_shape=[_sds((t, ATT_WIDTH)), _sds((t, 128))])(qkv, qkv, qkv, qkv, qkv)


def _attn_combine(os_, ls_, name, tr=256):
    t = os_[0].shape[0]
    nbr = len(os_)

    def body(*refs):
        o_refs, l_refs, (a_ref, lt_ref) = refs[:nbr], refs[nbr:2 * nbr], refs[2 * nbr:]
        lane = lax.broadcasted_iota(jnp.int32, (1, 128), 1)
        ls = [r[...] for r in l_refs]
        mx = functools.reduce(jnp.maximum, ls)
        tot = mx + jnp.log(sum(jnp.exp(l - mx) for l in ls))
        lt_ref[...] = tot
        ws = [jnp.exp(l - tot) for l in ls]
        for hp in range(ATT_HEADS // 2):
            sl = slice(hp * 128, (hp + 1) * 128)
            acc = jnp.zeros((tr, 128), F32)
            for w, o_ref in zip(ws, o_refs):
                wf = jnp.where(lane < 64, w[:, 2 * hp:2 * hp + 1], w[:, 2 * hp + 1:2 * hp + 2])
                acc = acc + wf * o_ref[:, sl]
            a_ref[:, sl] = acc

    return _call(body, name=name, grid=(t // tr,),
                 in_specs=[_rows(tr, ATT_WIDTH)] * nbr + [_rows(tr, 128)] * nbr,
                 out_specs=[_rows(tr, ATT_WIDTH), _rows(tr, 128)],
                 out_shape=[_sds((t, ATT_WIDTH)), _sds((t, 128))])(*os_, *ls_)


def _attn_bwd(qkv, dout, lse, dd, dil, name):
    t = qkv.shape[0]
    nb = t // ATT_BLOCK
    bps = nb // dil

    def body(q_ref, qn_ref, kc_ref, kp_ref, vc_ref, vp_ref, do_ref, don_ref, l_ref, ln_ref, d_ref, dn_ref, g_ref):
        n = pl.program_id(0)
        cur, prev, lane = _att_masks()
        nxt = jnp.logical_and(prev, ((n + 1) % bps) != 0)
        prev = jnp.logical_and(prev, (n % bps) != 0)
        for hp in range(ATT_HEADS // 2):
            sl = slice(hp * 128, (hp + 1) * 128)
            q2, qn2, kc, kp, vc, vp = q_ref[:, sl], qn_ref[:, sl], kc_ref[:, sl], kp_ref[:, sl], vc_ref[:, sl], vp_ref[:, sl]
            do2, don2 = do_ref[:, sl], don_ref[:, sl]
            dq, dk, dv = [], [], []
            for hh in range(2):
                h = 2 * hp + hh
                mine = (lane < 64) if hh == 0 else (lane >= 64)
                zero = jnp.zeros_like(q2)
                qm, qnm = jnp.where(mine, q2, zero), jnp.where(mine, qn2, zero)
                dom, donm = jnp.where(mine, do2, zero), jnp.where(mine, don2, zero)
                ls, lsn = l_ref[:, h:h + 1], ln_ref[:, h:h + 1]
                dh, dhn = d_ref[:, h:h + 1], dn_ref[:, h:h + 1]
                pc = jnp.exp(jnp.where(cur, _dot_nt(qm, kc) * 0.125 - ls, NEG))
                pp = jnp.exp(jnp.where(prev, _dot_nt(qm, kp) * 0.125 - ls, NEG))
                px = jnp.exp(jnp.where(nxt, _dot_nt(qnm, kc) * 0.125 - lsn, NEG))
                dsc = (pc * (_dot_nt(dom, vc) - dh)).astype(BF16)
                dsp = (pp * (_dot_nt(dom, vp) - dh)).astype(BF16)
                dsx = (px * (_dot_nt(donm, vc) - dhn)).astype(BF16)
                dq.append((_dot(dsc, kc) + _dot(dsp, kp)) * 0.125)
                dk.append((_dot_tn(dsc, qm) + _dot_tn(dsx, qnm)) * 0.125)
                dv.append(_dot_tn(pc.astype(BF16), do2) + _dot_tn(px.astype(BF16), don2))
            lo = lane < 64
            g_ref[:, sl] = jnp.where(lo, dq[0], dq[1])
            g_ref[:, ATT_WIDTH + hp * 128:ATT_WIDTH + (hp + 1) * 128] = dk[0] + dk[1]
            g_ref[:, 2 * ATT_WIDTH + hp * 128:2 * ATT_WIDTH + (hp + 1) * 128] = jnp.where(lo, dv[0], dv[1])

    def blk(width, j, off):
        return pl.BlockSpec((ATT_BLOCK, width), lambda n: (jnp.clip(n + off, 0, nb - 1), j))

    w = ATT_WIDTH
    return _call(body, name=name, grid=(nb,),
                 in_specs=[blk(w, 0, 0), blk(w, 0, 1), blk(w, 1, 0), blk(w, 1, -1), blk(w, 2, 0), blk(w, 2, -1),
                           blk(w, 0, 0), blk(w, 0, 1), blk(128, 0, 0), blk(128, 0, 1), blk(128, 0, 0), blk(128, 0, 1)],
                 out_specs=pl.BlockSpec((ATT_BLOCK, 3 * w), lambda n: (n, 0)),
                 out_shape=_sds((t, 3 * w)))(qkv, qkv, qkv, qkv, qkv, qkv, dout, dout, lse, lse, dd, dd)


def _local_step(x, tgt, mod, norm1_g, lb_logits, og, ag, norm2_g, fg, get_w, put_g):
    shift1, scale1, gate1, shift2, scale2, gate2 = [mod[:, i * D_MODEL:(i + 1) * D_MODEL] for i in range(6)]
    fg = fg.reshape(1, D_MODEL)

    h1 = _norm_mod(x, norm1_g, scale1, shift1, "norm_mod1")
    w_in = get_w("w_in", h1)
    proj = _mm_nn(h1, w_in, "mm_in")
    o_hg, states = _hgrn_fwd(proj, lb_logits, "hgrn_fwd")
    qkv = proj[:, 4 * HG_WIDTH:].astype(BF16)
    qkvs = [_to_sub(qkv, d) for d in DILATIONS]
    outs = [_attn_fwd(q, d, f"attn_fwd{d}") for q, d in zip(qkvs, DILATIONS)]
    att, lse = _attn_combine([_from_sub(o, d) for (o, _), d in zip(outs, DILATIONS)],
                             [_from_sub(l, d) for (_, l), d in zip(outs, DILATIONS)], "attn_combine")
    mixin = _mix_in(o_hg, proj, att, og, ag, "mix_in")
    w_out = get_w("w_out", mixin)
    mix = _mm_nn(mixin, w_out, "mm_out")
    x2, h2 = _resid_norm_mod(x, mix, gate1, norm2_g, scale2, shift2, "resid_norm_mod2")
    w_gu = get_w("w_gu", h2)
    au = _mm_nn(h2, w_gu, "mm_gu")
    act = _swiglu(au, "swiglu")
    w_down = get_w("w_down", act)
    ffn = _mm_nn(act, w_down, "mm_down")
    dx3, dffn, loss_v, dfg, dgate2 = _final_loss(x2, ffn, gate2, fg, tgt, "final_loss")

    put_g("w_down", *_mm_tn(act, dffn, 1, "mm_down_dw", tk=D_FF // 2))
    dact = _mm_nt(dffn, w_down, "mm_down_dx")
    dau = _swiglu_bwd(dact, au, "swiglu_bwd")
    put_g("w_gu", *_mm_tn(h2, dau, N_SHARD, "mm_gu_dw"))
    dh2 = _mm_nt(dau, w_gu, "mm_gu_dx", tm=1024)
    dx2, dshift2, dscale2, dg2, dgate1, dmix = _norm_mod_bwd(
        dh2, x2, norm2_g, scale2, dx3, "norm_mod2_bwd", gate=gate1, mix=mix)
    put_g("w_out", *_mm_tn(mixin, dmix, 1, "mm_out_dw"))
    dmixin = _mm_nt(dmix, w_out, "mm_out_dx", tm=1024)
    do_hg, dg_raw, datt, dd, dog, dag = _mix_in_bwd(dmixin, o_hg, proj, att, og, ag, "mix_in_bwd")
    datt_b = datt.astype(BF16)
    dqkv = [_from_sub(_attn_bwd(q, _to_sub(datt_b, d), _to_sub(lse, d), _to_sub(dd, d), d, f"attn_bwd{d}"), d)
            for q, d in zip(qkvs, DILATIONS)]
    dhg, dlb = _hgrn_bwd(proj, lb_logits, states, do_hg, "hgrn_bwd")
    dproj = _dproj(dhg, dg_raw, dqkv, "dproj")
    put_g("w_in", *_mm_tn(h1, dproj, N_SHARD, "mm_in_dw"))
    dh1 = _mm_nt(dproj, w_in, "mm_in_dx", tm=1024)
    dx, dshift1, dscale1, dg1 = _norm_mod_bwd(dh1, x, norm1_g, scale1, dx2, "norm_mod1_bwd")

    stats = jnp.concatenate([loss_v, dfg, dg2, dg1, dlb, dag, dog,
                             dshift1, dscale1, dgate1, dshift2, dscale2, dgate2], axis=1)
    return dx, stats


def _place():
    x, y, c = lax.axis_index("x"), lax.axis_index("y"), lax.axis_index("c")
    return x, y, c


def _chip_peers(x, y, c):
    return [(1 - x, y, c), (x, 1 - y, c), (1 - x, 1 - y, c)]


def _comm_call(body, name, n_in, out_shape, scratch_shapes):
    hbm = pl.BlockSpec(memory_space=pl.ANY)
    return pl.pallas_call(body, name=name, in_specs=[hbm] * n_in, out_specs=[hbm] * len(out_shape),
                          out_shape=out_shape, scratch_shapes=scratch_shapes)


_HBM = pl.BlockSpec(memory_space=pltpu.HBM)
_SEM = pl.BlockSpec(memory_space=pltpu.SEMAPHORE)
_EFFECT = pltpu.SideEffectType.DATAFLOW_SIDE_EFFECTING


def _exchange_copy(bufs, send, recv, j, peer, place, gather):
    if gather:
        src = dst = bufs[0].at[2 * place[0] + place[1]]
    else:
        src, dst = bufs[0].at[2 * peer[0] + peer[1]], bufs[1].at[j]
    return pltpu.make_async_remote_copy(src_ref=src, dst_ref=dst, send_sem=send.at[j], recv_sem=recv.at[j],
                                        device_id=peer, device_id_type=MESH)


def _exchange_start(groups, gather, name):
    sizes = [len(g) for g in groups]
    flat = [b for g in groups for b in g]
    ng, nb = len(groups), len(flat)

    def body(*refs):
        bufs, sems, token = refs[:nb], refs[nb:nb + 2 * ng], refs[-1]
        x, y, c = _place()
        for j, peer in enumerate(_chip_peers(x, y, c)):
            at = 0
            for i, size in enumerate(sizes):
                _exchange_copy(bufs[at:at + size], sems[2 * i], sems[2 * i + 1], j, peer, (x, y), gather).start()
                at += size
        token[...] = jnp.zeros_like(token)

    out = pl.pallas_call(
        body, name=name, in_specs=[_HBM] * nb,
        out_specs=[_SEM] * (2 * ng) + [_HBM] * nb + [pl.BlockSpec(memory_space=pltpu.VMEM)],
        out_shape=[pltpu.SemaphoreType.DMA((3,))] * (2 * ng) + [pltpu.HBM(b.shape, b.dtype) for b in flat] + [_sds((8, 128))],
        input_output_aliases={i: 2 * ng + i for i in range(nb)},
        compiler_params=pltpu.CompilerParams(has_side_effects=_EFFECT),
    )(*[pltpu.with_memory_space_constraint(b, pltpu.HBM) for b in flat])
    started, at = [], 2 * ng
    for i, size in enumerate(sizes):
        started.append((out[2 * i], out[2 * i + 1], tuple(out[at:at + size])))
        at += size
    return started, out[-1]


def _exchange_wait(started, after, gather, name):
    send, recv, bufs = started
    nb = len(bufs)

    def body(*refs):
        x, y, c = _place()
        for j, peer in enumerate(_chip_peers(x, y, c)):
            cp = _exchange_copy(refs[:nb], refs[nb], refs[nb + 1], j, peer, (x, y), gather)
            cp.wait_send()
            cp.wait_recv()

    return pl.pallas_call(
        body, name=name, in_specs=[_HBM] * nb + [_SEM, _SEM, pl.BlockSpec(memory_space=pl.ANY)],
        out_specs=[_HBM] * nb, out_shape=[pltpu.HBM(b.shape, b.dtype) for b in bufs],
        input_output_aliases={i: i for i in range(nb)},
        compiler_params=pltpu.CompilerParams(has_side_effects=_EFFECT),
    )(*bufs, send, recv, after)


def _swap_sibling(vs, name):
    n = len(vs)

    def body(*refs):
        v_refs, o_refs, (send, recv) = refs[:n], refs[n:2 * n], refs[2 * n:]
        x, y, c = _place()
        cps = [pltpu.make_async_remote_copy(
            src_ref=v_refs[a], dst_ref=o_refs[a], send_sem=send.at[a], recv_sem=recv.at[a],
            device_id=(x, y, 1 - c), device_id_type=MESH) for a in range(n)]
        for cp in cps:
            cp.start()
        for cp in cps:
            cp.wait()

    return _comm_call(body, name, n, [_sds(v.shape, v.dtype) for v in vs],
                      [pltpu.SemaphoreType.DMA((n,)), pltpu.SemaphoreType.DMA((n,))])(*vs)


def _gather_rows(v, name):
    r, n = v.shape

    def body(v_ref, o_ref, send, recv, loc):
        x, y, c = _place()
        me = 4 * x + 2 * y + c
        own = pltpu.make_async_copy(v_ref, o_ref.at[me], loc)
        own.start()
        peers = []
        for k in range(1, 8):
            px = 1 - x if k & 4 else x
            py = 1 - y if k & 2 else y
            pc = 1 - c if k & 1 else c
            peers.append((px, py, pc))
        sends = []
        for k, peer in enumerate(peers):
            cp = pltpu.make_async_remote_copy(src_ref=v_ref, dst_ref=o_ref.at[me], send_sem=send.at[k],
                                              recv_sem=recv.at[k], device_id=peer, device_id_type=MESH)
            cp.start()
            sends.append(cp)
        for k, peer in enumerate(peers):
            pltpu.make_async_remote_copy(src_ref=v_ref, dst_ref=o_ref.at[4 * peer[0] + 2 * peer[1] + peer[2]],
                                         send_sem=send.at[k], recv_sem=recv.at[k], device_id=peer,
                                         device_id_type=MESH).wait_recv()
        for cp in sends:
            cp.wait_send()
        own.wait()

    vmem = pl.BlockSpec(memory_space=pltpu.VMEM)
    return pl.pallas_call(body, name=name, in_specs=[vmem], out_specs=vmem, out_shape=_sds((8, r, n), v.dtype),
                          scratch_shapes=[pltpu.SemaphoreType.DMA((7,)), pltpu.SemaphoreType.DMA((7,)),
                                          pltpu.SemaphoreType.DMA])(v)


def _cast_place(w, shard, name):
    r, c = w.shape
    tr = r // 4

    def body(s_ref, w_ref, o_ref):
        o_ref[0] = w_ref[...].astype(BF16)

    return pl.pallas_call(
        body, name=name, out_shape=_sds((N_SHARD, r, c), BF16),
        grid_spec=pltpu.PrefetchScalarGridSpec(
            num_scalar_prefetch=1, grid=(4,), in_specs=[pl.BlockSpec((tr, c), lambda i, s: (i, 0))],
            out_specs=pl.BlockSpec((1, tr, c), lambda i, s: (s[0], i, 0))),
        compiler_params=pltpu.CompilerParams(dimension_semantics=("arbitrary",)),
    )(shard.reshape(1).astype(jnp.int32), w)


def _mod_part(c_all, w_ada, b_ada, name):
    n = w_ada.shape[1]

    def body(c_ref, w_ref, b_ref, a_ref, p_ref):
        cv = c_ref[...]
        ca = cv * _sigmoid(cv)
        a_ref[...] = ca
        p_ref[...] = jnp.dot(ca, w_ref[...], precision=lax.Precision.HIGHEST, preferred_element_type=F32) + b_ref[...]

    full = lambda a: pl.BlockSpec(a.shape, lambda i: (0, 0))
    return _call(body, name=name, grid=(1,), in_specs=[full(c_all), full(w_ada), full(b_ada)],
                 out_specs=[pl.BlockSpec((8, D_MODEL), lambda i: (0, 0)), pl.BlockSpec((8, n), lambda i: (0, 0))],
                 out_shape=[_sds((8, D_MODEL)), _sds((8, n))])(c_all, w_ada, b_ada)


def _sum_received(g_own, land, name):
    r, c = g_own.shape
    tr = r // 4

    def body(g_ref, l_ref, o_ref):
        o_ref[...] = ((g_ref[...] + l_ref[0].astype(F32)) + l_ref[1].astype(F32)) + l_ref[2].astype(F32)

    return _call(body, name=name, grid=(4,),
                 in_specs=[_rows(tr, c), pl.BlockSpec((3, tr, c), lambda i: (0, i, 0))],
                 out_specs=_rows(tr, c), out_shape=_sds((r, c)))(g_own, land)


def _outer_sum(ct, dm, name):
    k, n = ct.shape[0], dm.shape[1]
    tr = k // 4

    def body(c_ref, d_ref, o_ref):
        cv = c_ref[...]
        dv = d_ref[...]
        acc = cv[:, 0:1] * dv[0:1, :]
        for i in range(1, 8):
            acc = acc + cv[:, i:i + 1] * dv[i:i + 1, :]
        o_ref[...] = acc

    return _call(body, name=name, grid=(4,), in_specs=[_rows(tr, 8), pl.BlockSpec((8, n), lambda i: (0, 0))],
                 out_specs=_rows(tr, n), out_shape=_sds((k, n)))(ct, dm)


def _small_grads(stats, lb_logits, name):
    def body(s_ref, lg_ref, g_ref, l_ref):
        tot = s_ref[0:1, :]
        for i in range(1, 8):
            tot = tot + s_ref[i:i + 1, :]
        part = lambda off, n: tot[:, off:off + n]
        l_ref[...] = jnp.zeros((1, 128), F32) + (0.5 / D_MODEL) * jnp.sum(part(ST_LOSS, D_MODEL))
        lg = lg_ref[...]
        lb = _sigmoid(lg[0:1] - lg[1:2])
        dl0 = part(ST_DLB, HG_WIDTH) * lb * (1.0 - lb)
        g_ref[:, SP_BADA:SP_BADA + 6 * D_MODEL] = part(ST_DMOD, 6 * D_MODEL)
        g_ref[:, SP_N1:SP_N1 + D_MODEL] = part(ST_DG1, D_MODEL)
        g_ref[:, SP_LB:SP_LB + HG_WIDTH] = dl0
        g_ref[:, SP_LB + HG_WIDTH:SP_LB + 2 * HG_WIDTH] = -dl0
        g_ref[:, SP_OG:SP_OG + HG_HEAD] = part(ST_DOG, HG_HEAD)
        g_ref[:, SP_AG:SP_AG + ATT_WIDTH] = part(ST_DAG, ATT_WIDTH)
        g_ref[:, SP_N2:SP_N2 + D_MODEL] = part(ST_DG2, D_MODEL)
        g_ref[:, SP_FG:SP_FG + D_MODEL] = part(ST_DFG, D_MODEL)

    return _call(body, name=name, grid=(1,),
                 in_specs=[pl.BlockSpec((8, ST_WIDTH), lambda i: (0, 0)), pl.BlockSpec((2, HG_WIDTH), lambda i: (0, 0))],
                 out_specs=[pl.BlockSpec((1, SP_WIDTH), lambda i: (0, 0)), pl.BlockSpec((1, 128), lambda i: (0, 0))],
                 out_shape=[_sds((1, SP_WIDTH)), _sds((1, 128))])(stats, lb_logits)


def _adamw(w, gs, m, v, name, steps=4):
    r, c = w.shape
    tr = r // steps
    ng = len(gs)

    def body(*refs):
        w_ref, g_refs, (m_ref, v_ref, g_out, d_out, m_out, v_out) = refs[0], refs[1:1 + ng], refs[1 + ng:]
        g = g_refs[0][...]
        for g_ref in g_refs[1:]:
            g = g + g_ref[...]
        m_new = ADAM_B1 * m_ref[...] + (1.0 - ADAM_B1) * g
        v_new = ADAM_B2 * v_ref[...] + (1.0 - ADAM_B2) * (g * g)
        m_hat = m_new / (1.0 - ADAM_B1 ** ADAM_STEP)
        v_hat = v_new / (1.0 - ADAM_B2 ** ADAM_STEP)
        g_out[...] = g
        d_out[...] = -ADAM_LR * (m_hat / (jnp.sqrt(v_hat) + ADAM_EPS) + ADAM_WD * w_ref[...])
        m_out[...] = m_new
        v_out[...] = v_new

    row = _rows(tr, c)
    return _call(body, name=name, grid=(steps,), in_specs=[row] * (3 + ng), out_specs=[row] * 4,
                 out_shape=[_sds((r, c))] * 4)(w, *gs, m, v)


def kernel(x, c, w_ada, b_ada, norm1_g, w_in, hg_lb_logits, hg_onorm_g, att_onorm_g, w_out, norm2_g, w_gate_up, w_down, final_g, loss_target, m_w_ada, m_b_ada, m_norm1_g, m_w_in, m_hg_lb_logits, m_hg_onorm_g, m_att_onorm_g, m_w_out, m_norm2_g, m_w_gate_up, m_w_down, m_final_g, v_w_ada, v_b_ada, v_norm1_g, v_w_in, v_hg_lb_logits, v_hg_onorm_g, v_att_onorm_g, v_w_out, v_norm2_g, v_w_gate_up, v_w_down, v_final_g):
    ix, iy, ic = _place()
    shard = 2 * ix + iy
    sample = 4 * ix + 2 * iy + ic
    n_ada = w_ada.shape[2]

    shards = [w_in[0], w_out[0], w_gate_up[0], w_down[0]]
    names = ["w_in", "w_out", "w_gu", "w_down"]
    shapes = [(N_SHARD,) + w.shape for w in shards]
    gathering, token = _exchange_start([(_cast_place(w, shard, "place_" + nm),) for w, nm in zip(shards, names)],
                                       True, "gather_start")
    c = c + token[0:1, 0:1]

    c_all = _gather_rows(c, "gather_c").reshape(8, D_MODEL)
    b_part = lax.dynamic_slice(b_ada, (0, shard * n_ada), (1, n_ada))
    c_act, part = _mod_part(c_all, w_ada[0], b_part, "mod_part")
    parts = _gather_rows(part, "gather_mod")[::2]
    mod = lax.dynamic_index_in_dim(parts, sample, axis=1, keepdims=False).reshape(1, 6 * D_MODEL)

    def get_w(name, after):
        i = names.index(name)
        (full,) = _exchange_wait(gathering[i], after, True, "gather_wait_" + name)
        return full if name in ("w_in", "w_gu") else full.reshape(1, -1, D_MODEL)

    scattering = {}

    def put_g(name, g, g_bf16):
        shape = shapes[names.index(name)]
        land = lax.empty((3,) + shape[1:], BF16)
        (started,), _ = _exchange_start([(g_bf16.reshape(shape), land)], False, "scatter_start_" + name)
        scattering[name] = (g.reshape(shape), started)

    dx, stats = _local_step(x[0], loss_target[0], mod, norm1_g, hg_lb_logits, hg_onorm_g, att_onorm_g,
                            norm2_g, final_g, get_w, put_g)

    def summed(name, after):
        g, started = scattering[name]
        _, land = _exchange_wait(started, after, False, "scatter_wait_" + name)
        own = lax.dynamic_index_in_dim(g, shard, axis=0, keepdims=False)
        return _sum_received(own, land, "sum_" + name)

    stats_all = _gather_rows(stats, "gather_stats").reshape(8, ST_WIDTH)
    g_small, loss = _small_grads(stats_all, hg_lb_logits, "small_grads")
    dmod = lax.dynamic_slice(stats_all, (0, ST_DMOD + shard * n_ada), (8, n_ada))
    g_ada = _outer_sum(c_act.T, dmod, "w_ada_grad")

    smalls = [(b_ada, m_b_ada, v_b_ada), (norm1_g, m_norm1_g, v_norm1_g),
              (hg_lb_logits, m_hg_lb_logits, v_hg_lb_logits), (hg_onorm_g, m_hg_onorm_g, v_hg_onorm_g),
              (att_onorm_g, m_att_onorm_g, v_att_onorm_g), (norm2_g, m_norm2_g, v_norm2_g),
              (final_g, m_final_g, v_final_g)]
    pack = lambda i: jnp.concatenate([t[i].reshape(1, -1) for t in smalls], axis=1)
    small_out = _adamw(pack(0), [g_small], pack(1), pack(2), "adamw_small", steps=1)
    offs = [SP_BADA, SP_N1, SP_LB, SP_OG, SP_AG, SP_N2, SP_FG, SP_WIDTH]
    unpack = lambda a: [a[0, offs[i]:offs[i + 1]].reshape(smalls[i][0].shape) for i in range(7)]
    sg, sd, sm, sv = [unpack(a) for a in small_out]

    ada = _adamw(w_ada[0], [g_ada], m_w_ada[0], v_w_ada[0], "adamw_w_ada")
    moments = [(m_w_in, v_w_in), (m_w_out, v_w_out), (m_w_gate_up, v_w_gate_up), (m_w_down, v_w_down)]

    def update(group, after, tag):
        sums = [summed(nm, after) for nm in group]
        other = _swap_sibling(sums, "swap_sums_" + tag)
        return {nm: _adamw(shards[names.index(nm)], [s, o], moments[names.index(nm)][0][0],
                           moments[names.index(nm)][1][0], "adamw_" + nm) for nm, s, o in zip(group, sums, other)}

    done = update(["w_down", "w_gu", "w_out"], ada[1], "late")
    done.update(update(["w_in"], done["w_out"][1], "in"))
    big = [ada] + [done[nm] for nm in names]
    bg, bd, bm, bv = [[t[i][None] for t in big] for i in range(4)]

    def order(b, s):
        return [b[0], s[0], s[1], b[1], s[2], s[3], s[4], b[2], s[5], b[3], b[4], s[6]]

    return (loss[0, 0], dx[None], *order(bg, sg), *order(bd, sd), *order(bm, sm), *order(bv, sv))
```

```python
import functools

import jax
import jax.numpy as jnp
from jax import lax
from jax.experimental import pallas as pl
from jax.experimental.pallas import tpu as pltpu

F32 = jnp.float32
BF16 = jnp.bfloat16
MESH = pl.DeviceIdType.MESH

D_MODEL = 1024
HG_WIDTH = 512
HG_HEAD = 128
HG_CHUNK = 64
ATT_WIDTH = 512
ATT_HEADS = 8
ATT_BLOCK = 128
DILATIONS = (1, 4, 16)
D_FF = 2816
IN_WIDTH = 3584
N_SHARD = 4
RMS_EPS = 1e-6
NEG = -1e30

ADAM_LR = 0.001
ADAM_B1 = 0.9
ADAM_B2 = 0.999
ADAM_EPS = 1e-08
ADAM_WD = 0.01
ADAM_STEP = 10

VMEM_LIMIT = 56 * 2**20

ST_LOSS, ST_DFG, ST_DG2, ST_DG1 = 0, 1024, 2048, 3072
ST_DLB, ST_DAG, ST_DOG, ST_DMOD = 4096, 4608, 5120, 5248
ST_WIDTH = 5248 + 6144
SP_BADA, SP_N1, SP_LB, SP_OG, SP_AG, SP_N2, SP_FG = 0, 6144, 7168, 8192, 8320, 8832, 9856
SP_WIDTH = 10880


def _call(body, *, name, grid, in_specs, out_specs, out_shape, scratch_shapes=()):
    return pl.pallas_call(
        body, name=name, grid=grid, in_specs=in_specs, out_specs=out_specs, out_shape=out_shape,
        scratch_shapes=list(scratch_shapes),
        compiler_params=pltpu.CompilerParams(
            dimension_semantics=("arbitrary",) * len(grid), vmem_limit_bytes=VMEM_LIMIT))


def _sds(shape, dtype=F32):
    return jax.ShapeDtypeStruct(shape, dtype)


def _dot(a, b):
    return jnp.dot(a, b, preferred_element_type=F32)


def _dot_nt(a, b):
    return lax.dot_general(a, b, (((1,), (1,)), ((), ())), preferred_element_type=F32)


def _dot_tn(a, b):
    return lax.dot_general(a, b, (((0,), (0,)), ((), ())), preferred_element_type=F32)


def _sigmoid(x):
    return 1.0 / (1.0 + jnp.exp(-x))


def _rows(tr, width):
    return pl.BlockSpec((tr, width), lambda i: (i, 0))


def _vec(width):
    return pl.BlockSpec((1, width), lambda i: (0, 0))


def _acc(ref, val, first):
    @pl.when(first)
    def _():
        ref[...] = val

    @pl.when(jnp.logical_not(first))
    def _():
        ref[...] += val


def _mm_nn(a, b3, name, tm=512):
    m, k = a.shape
    s, _, n = b3.shape

    def body(a_ref, b_ref, o_ref):
        o_ref[...] = _dot(a_ref[...], b_ref[0])

    return _call(
        body, name=name, grid=(s, m // tm),
        in_specs=[pl.BlockSpec((tm, k), lambda j, i: (i, 0)), pl.BlockSpec((1, k, n), lambda j, i: (j, 0, 0))],
        out_specs=pl.BlockSpec((tm, n), lambda j, i: (i, j)), out_shape=_sds((m, s * n)))(a, b3)


def _mm_nt(dy, b3, name, tm=512):
    m = dy.shape[0]
    s, k, n = b3.shape

    def body(dy_ref, b_ref, o_ref):
        _acc(o_ref, _dot_nt(dy_ref[...], b_ref[0]), pl.program_id(1) == 0)

    return _call(
        body, name=name, grid=(m // tm, s),
        in_specs=[pl.BlockSpec((tm, n), lambda i, j: (i, j)), pl.BlockSpec((1, k, n), lambda i, j: (j, 0, 0))],
        out_specs=pl.BlockSpec((tm, k), lambda i, j: (i, 0)), out_shape=_sds((m, k)))(dy, b3)


def _mm_tn(a, dy, s, name, tm=512, tk=None):
    m, k = a.shape
    n = dy.shape[1] // s
    tk = k if tk is None else tk
    steps = m // tm

    def body(a_ref, dy_ref, o_ref, ob_ref):
        i = pl.program_id(2)
        _acc(o_ref, _dot_tn(a_ref[...], dy_ref[...])[None], i == 0)

        @pl.when(i == steps - 1)
        def _():
            ob_ref[...] = o_ref[...].astype(BF16)

    out = pl.BlockSpec((1, tk, n), lambda kk, j, i: (j, kk, 0))
    return _call(
        body, name=name, grid=(k // tk, s, steps),
        in_specs=[pl.BlockSpec((tm, tk), lambda kk, j, i: (i, kk)), pl.BlockSpec((tm, n), lambda kk, j, i: (i, j))],
        out_specs=[out, out], out_shape=[_sds((s, k, n)), _sds((s, k, n), BF16)])(a, dy)


def _rms(x):
    return lax.rsqrt(jnp.mean(x * x, axis=-1, keepdims=True) + RMS_EPS)


def _rms_bwd(dxh, xh, r):
    return r * (dxh - xh * jnp.mean(dxh * xh, axis=-1, keepdims=True))


def _norm_mod(x, g, scale, shift, name, tr=512):
    t = x.shape[0]

    def body(x_ref, g_ref, sc_ref, sh_ref, h_ref):
        xv = x_ref[...]
        n = xv * _rms(xv) * g_ref[...]
        h_ref[...] = (n * (1.0 + sc_ref[...]) + sh_ref[...]).astype(BF16)

    return _call(body, name=name, grid=(t // tr,),
                 in_specs=[_rows(tr, D_MODEL), _vec(D_MODEL), _vec(D_MODEL), _vec(D_MODEL)],
                 out_specs=_rows(tr, D_MODEL), out_shape=_sds((t, D_MODEL), BF16))(x, g, scale, shift)


def _mix_in(o_hg, proj, att, og, ag, name, tr=256):
    t = o_hg.shape[0]

    def body(o_ref, g_ref, a_ref, og_ref, ag_ref, m_ref):
        for h in range(HG_WIDTH // HG_HEAD):
            sl = slice(h * HG_HEAD, (h + 1) * HG_HEAD)
            oh = o_ref[:, sl]
            gv = g_ref[:, sl]
            m_ref[:, sl] = (oh * _rms(oh) * og_ref[...] * (gv * _sigmoid(gv))).astype(BF16)
        av = a_ref[...]
        m_ref[:, HG_WIDTH:] = (av * _rms(av) * ag_ref[...]).astype(BF16)

    return _call(body, name=name, grid=(t // tr,),
                 in_specs=[_rows(tr, HG_WIDTH), pl.BlockSpec((tr, HG_WIDTH), lambda i: (i, 3)), _rows(tr, ATT_WIDTH),
                           _vec(HG_HEAD), _vec(ATT_WIDTH)],
                 out_specs=_rows(tr, D_MODEL), out_shape=_sds((t, D_MODEL), BF16))(o_hg, proj, att, og, ag)


def _resid_norm_mod(x, mix, gate, g, scale, shift, name, tr=256):
    t = x.shape[0]

    def body(x_ref, m_ref, gt_ref, g_ref, sc_ref, sh_ref, x2_ref, h_ref):
        x2 = x_ref[...] + gt_ref[...] * m_ref[...]
        x2_ref[...] = x2
        n = x2 * _rms(x2) * g_ref[...]
        h_ref[...] = (n * (1.0 + sc_ref[...]) + sh_ref[...]).astype(BF16)

    return _call(body, name=name, grid=(t // tr,),
                 in_specs=[_rows(tr, D_MODEL), _rows(tr, D_MODEL)] + [_vec(D_MODEL)] * 4,
                 out_specs=[_rows(tr, D_MODEL), _rows(tr, D_MODEL)],
                 out_shape=[_sds((t, D_MODEL)), _sds((t, D_MODEL), BF16)])(x, mix, gate, g, scale, shift)


def _swiglu(au, name, tr=256):
    t = au.shape[0]

    def body(a_ref, u_ref, o_ref):
        a = a_ref[...]
        o_ref[...] = (a * _sigmoid(a) * u_ref[...]).astype(BF16)

    return _call(body, name=name, grid=(t // tr,),
                 in_specs=[pl.BlockSpec((tr, D_FF), lambda i: (i, 0)), pl.BlockSpec((tr, D_FF), lambda i: (i, 1))],
                 out_specs=_rows(tr, D_FF), out_shape=_sds((t, D_FF), BF16))(au, au)


def _swiglu_bwd(dact, au, name, tr=256):
    t = au.shape[0]

    def body(d_ref, a_ref, u_ref, o_ref):
        a = a_ref[...]
        d = d_ref[...]
        sg = _sigmoid(a)
        o_ref[:, :D_FF] = (d * u_ref[...] * sg * (1.0 + a * (1.0 - sg))).astype(BF16)
        o_ref[:, D_FF:] = (d * a * sg).astype(BF16)

    return _call(body, name=name, grid=(t // tr,),
                 in_specs=[_rows(tr, D_FF), pl.BlockSpec((tr, D_FF), lambda i: (i, 0)),
                           pl.BlockSpec((tr, D_FF), lambda i: (i, 1))],
                 out_specs=_rows(tr, 2 * D_FF), out_shape=_sds((t, 2 * D_FF), BF16))(dact, au, au)


def _final_loss(x2, ffn, gate, fg, tgt, name, tr=256):
    t = x2.shape[0]

    def body(x_ref, f_ref, gt_ref, fg_ref, t_ref, dx_ref, df_ref, l_ref, dfg_ref, dgt_ref):
        first = pl.program_id(0) == 0
        ffn_v = f_ref[...]
        x3 = x_ref[...] + gt_ref[...] * ffn_v
        r = _rms(x3)
        xh = x3 * r
        err = xh * fg_ref[...] - t_ref[...]
        dy = err * (1.0 / D_MODEL)
        dx3 = _rms_bwd(dy * fg_ref[...], xh, r)
        dx_ref[...] = dx3
        df_ref[...] = (dx3 * gt_ref[...]).astype(BF16)
        _acc(l_ref, jnp.sum(err * err, axis=0, keepdims=True), first)
        _acc(dfg_ref, jnp.sum(dy * xh, axis=0, keepdims=True), first)
        _acc(dgt_ref, jnp.sum(dx3 * ffn_v, axis=0, keepdims=True), first)

    row, vec = _rows(tr, D_MODEL), _vec(D_MODEL)
    return _call(body, name=name, grid=(t // tr,), in_specs=[row, row, vec, vec, row],
                 out_specs=[row, row, vec, vec, vec],
                 out_shape=[_sds((t, D_MODEL)), _sds((t, D_MODEL), BF16)] + [_sds((1, D_MODEL))] * 3)(
                     x2, ffn, gate, fg, tgt)


def _norm_mod_bwd(dh, x, g, scale, dres, name, gate=None, mix=None, tr=256):
    t = x.shape[0]
    below = gate is not None

    def body(*refs):
        if below:
            dh_ref, x_ref, g_ref, sc_ref, dr_ref, gt_ref, m_ref, dx_ref, dsh_ref, dsc_ref, dg_ref, dgt_ref, dm_ref = refs
        else:
            dh_ref, x_ref, g_ref, sc_ref, dr_ref, dx_ref, dsh_ref, dsc_ref, dg_ref = refs
        first = pl.program_id(0) == 0
        xv = x_ref[...]
        dhv = dh_ref[...]
        r = _rms(xv)
        xh = xv * r
        dn = dhv * (1.0 + sc_ref[...])
        dx = dr_ref[...] + _rms_bwd(dn * g_ref[...], xh, r)
        dx_ref[...] = dx
        _acc(dsh_ref, jnp.sum(dhv, axis=0, keepdims=True), first)
        _acc(dsc_ref, jnp.sum(dhv * xh * g_ref[...], axis=0, keepdims=True), first)
        _acc(dg_ref, jnp.sum(dn * xh, axis=0, keepdims=True), first)
        if below:
            _acc(dgt_ref, jnp.sum(dx * m_ref[...], axis=0, keepdims=True), first)
            dm_ref[...] = (dx * gt_ref[...]).astype(BF16)

    row, vec = _rows(tr, D_MODEL), _vec(D_MODEL)
    in_specs = [row, row, vec, vec, row] + ([vec, row] if below else [])
    out_specs = [row, vec, vec, vec] + ([vec, row] if below else [])
    out_shape = [_sds((t, D_MODEL))] + [_sds((1, D_MODEL))] * 3 + ([_sds((1, D_MODEL)), _sds((t, D_MODEL), BF16)] if below else [])
    args = (dh, x, g, scale, dres) + ((gate, mix) if below else ())
    return _call(body, name=name, grid=(t // tr,), in_specs=in_specs, out_specs=out_specs, out_shape=out_shape)(*args)


def _mix_in_bwd(dmi, o_hg, proj, att, og, ag, name, tr=256):
    t = o_hg.shape[0]

    def body(d_ref, o_ref, g_ref, a_ref, og_ref, ag_ref, do_ref, dg_ref, da_ref, dd_ref, dog_ref, dag_ref):
        first = pl.program_id(0) == 0
        dog = jnp.zeros((1, HG_HEAD), F32)
        for h in range(HG_WIDTH // HG_HEAD):
            sl = slice(h * HG_HEAD, (h + 1) * HG_HEAD)
            oh = o_ref[:, sl]
            gv = g_ref[:, sl]
            dv = d_ref[:, sl]
            r = _rms(oh)
            xh = oh * r
            sg = _sigmoid(gv)
            dno = dv * gv * sg
            dg_ref[:, sl] = dv * xh * og_ref[...] * sg * (1.0 + gv * (1.0 - sg))
            dog = dog + jnp.sum(dno * xh, axis=0, keepdims=True)
            do_ref[:, sl] = _rms_bwd(dno * og_ref[...], xh, r)
        _acc(dog_ref, dog, first)
        av = a_ref[...]
        dav = d_ref[:, HG_WIDTH:]
        r = _rms(av)
        xa = av * r
        _acc(dag_ref, jnp.sum(dav * xa, axis=0, keepdims=True), first)
        datt = _rms_bwd(dav * ag_ref[...], xa, r)
        da_ref[...] = datt
        prod = datt * av
        lane = lax.broadcasted_iota(jnp.int32, (1, 128), 1)
        dd = jnp.zeros((tr, 128), F32)
        for hp in range(ATT_HEADS // 2):
            pp = prod[:, hp * 128:(hp + 1) * 128]
            lo = jnp.sum(jnp.where(lane < 64, pp, 0.0), axis=-1, keepdims=True)
            hi = jnp.sum(jnp.where(lane >= 64, pp, 0.0), axis=-1, keepdims=True)
            dd = jnp.where(lane == 2 * hp, lo, dd)
            dd = jnp.where(lane == 2 * hp + 1, hi, dd)
        dd_ref[...] = dd

    half = _rows(tr, HG_WIDTH)
    return _call(body, name=name, grid=(t // tr,),
                 in_specs=[_rows(tr, D_MODEL), half, pl.BlockSpec((tr, HG_WIDTH), lambda i: (i, 3)), half,
                           _vec(HG_HEAD), _vec(ATT_WIDTH)],
                 out_specs=[half, half, half, _rows(tr, 128), _vec(HG_HEAD), _vec(ATT_WIDTH)],
                 out_shape=[_sds((t, HG_WIDTH))] * 3 + [_sds((t, 128)), _sds((1, HG_HEAD)), _sds((1, ATT_WIDTH))])(
                     dmi, o_hg, proj, att, og, ag)


def _dproj(dhg, dg, dqkv, name, tr=256):
    t = dhg.shape[0]
    w3 = 3 * HG_WIDTH

    def body(h_ref, g_ref, a1_ref, a2_ref, a3_ref, o_ref):
        o_ref[:, :w3] = h_ref[...].astype(BF16)
        o_ref[:, w3:w3 + HG_WIDTH] = g_ref[...].astype(BF16)
        o_ref[:, w3 + HG_WIDTH:] = (a1_ref[...] + a2_ref[...] + a3_ref[...]).astype(BF16)

    return _call(body, name=name, grid=(t // tr,),
                 in_specs=[_rows(tr, w3), _rows(tr, HG_WIDTH)] + [_rows(tr, w3)] * 3,
                 out_specs=_rows(tr, IN_WIDTH), out_shape=_sds((t, IN_WIDTH), BF16))(dhg, dg, *dqkv)


def _hg_gates(f_raw, q_raw, lb, tri):
    sg = _sigmoid(f_raw)
    f = lb + (1.0 - lb) * sg
    k = 1.0 - f
    b = jnp.dot(tri, jnp.log(f), precision=lax.Precision.HIGHEST, preferred_element_type=F32)
    sq = _sigmoid(q_raw)
    return sg, f, k, b, sq


def _hgrn_fwd(proj, lb_logits, name):
    t = proj.shape[0]
    nc = t // HG_CHUNK
    nh = HG_WIDTH // HG_HEAD

    def body(q_ref, f_ref, i_ref, lg_ref, o_ref, st_ref, s_scr):
        @pl.when(pl.program_id(0) == 0)
        def _():
            s_scr[...] = jnp.zeros_like(s_scr)

        lg = lg_ref[...]
        lb_all = _sigmoid(lg[0:1] - lg[1:2])
        row = lax.broadcasted_iota(jnp.int32, (HG_CHUNK, HG_CHUNK), 0)
        col = lax.broadcasted_iota(jnp.int32, (HG_CHUNK, HG_CHUNK), 1)
        causal = row >= col
        tri = causal.astype(F32)
        for h in range(nh):
            sl = slice(h * HG_HEAD, (h + 1) * HG_HEAD)
            q_raw = q_ref[:, sl]
            _, _, k, b, sq = _hg_gates(f_ref[:, sl], q_raw, lb_all[:, sl], tri)
            v = i_ref[:, sl].astype(BF16)
            gl = b[HG_CHUNK - 1:HG_CHUNK]
            qd = (q_raw * sq * jnp.exp(b)).astype(BF16)
            kd = (k * jnp.exp(-b)).astype(BF16)
            ke = (k * jnp.exp(gl - b)).astype(BF16)
            st = s_scr[h]
            st_ref[0, sl, :] = st
            a = jnp.where(causal, _dot_nt(qd, kd), 0.0).astype(BF16)
            o_ref[:, sl] = _dot_nt(qd, st.astype(BF16)) + _dot(a, v)
            s_scr[h] = st * jnp.exp(gl) + _dot_tn(v, ke)

    blk = lambda j: pl.BlockSpec((HG_CHUNK, HG_WIDTH), lambda c: (c, j))
    return _call(body, name=name, grid=(nc,),
                 in_specs=[blk(0), blk(1), blk(2), pl.BlockSpec((2, HG_WIDTH), lambda c: (0, 0))],
                 out_specs=[blk(0), pl.BlockSpec((1, HG_WIDTH, HG_HEAD), lambda c: (c, 0, 0))],
                 out_shape=[_sds((t, HG_WIDTH)), _sds((nc, HG_WIDTH, HG_HEAD))],
                 scratch_shapes=[pltpu.VMEM((nh, HG_HEAD, HG_HEAD), F32)])(proj, proj, proj, lb_logits)


def _hgrn_bwd(proj, lb_logits, states, do, name):
    t = proj.shape[0]
    nc = t // HG_CHUNK
    nh = HG_WIDTH // HG_HEAD

    def body(q_ref, f_ref, i_ref, lg_ref, st_ref, do_ref, d_ref, dlb_ref, ds_scr):
        first = pl.program_id(0) == 0

        @pl.when(first)
        def _():
            ds_scr[...] = jnp.zeros_like(ds_scr)

        lg = lg_ref[...]
        lb_all = _sigmoid(lg[0:1] - lg[1:2])
        row = lax.broadcasted_iota(jnp.int32, (HG_CHUNK, HG_CHUNK), 0)
        col = lax.broadcasted_iota(jnp.int32, (HG_CHUNK, HG_CHUNK), 1)
        causal = row >= col
        tri = causal.astype(F32)
        tri_t = (row <= col).astype(F32)
        dlb = []
        for h in range(nh):
            sl = slice(h * HG_HEAD, (h + 1) * HG_HEAD)
            q_raw = q_ref[:, sl]
            lb = lb_all[:, sl]
            sg, f, k, b, sq = _hg_gates(f_ref[:, sl], q_raw, lb, tri)
            v = i_ref[:, sl].astype(BF16)
            gl = b[HG_CHUNK - 1:HG_CHUNK]
            egl = jnp.exp(gl)
            eb = jnp.exp(b)
            enb = jnp.exp(-b)
            egb = jnp.exp(gl - b)
            qd = q_raw * sq * eb
            kd = k * enb
            ke = k * egb
            qd_b, kd_b, ke_b = qd.astype(BF16), kd.astype(BF16), ke.astype(BF16)
            st = st_ref[0, sl, :]
            dst = ds_scr[h]
            dst_b = dst.astype(BF16)
            dov = do_ref[:, sl].astype(BF16)
            a = jnp.where(causal, _dot_nt(qd_b, kd_b), 0.0).astype(BF16)
            da = jnp.where(causal, _dot_nt(dov, v), 0.0).astype(BF16)
            dqd = _dot(dov, st.astype(BF16)) + _dot(da, kd_b)
            dkd = _dot_tn(da, qd_b)
            dv = _dot_tn(a, dov) + _dot_nt(ke_b, dst_b)
            dke = _dot(v, dst_b)
            dgl = jnp.sum(dst * st, axis=0, keepdims=True) * egl
            ds_scr[h] = _dot_tn(dov, qd_b) + dst * egl
            t1 = dke * ke
            db = dqd * qd_b.astype(F32) - dkd * kd_b.astype(F32) - t1
            dgl = dgl + jnp.sum(t1, axis=0, keepdims=True)
            dlf = jnp.dot(tri_t, db, precision=lax.Precision.HIGHEST, preferred_element_type=F32) + dgl
            df = dlf / f - (dkd * enb + dke * egb)
            d_ref[:, sl] = dqd * eb * sq * (1.0 + q_raw * (1.0 - sq))
            d_ref[:, HG_WIDTH + h * HG_HEAD:HG_WIDTH + (h + 1) * HG_HEAD] = df * (1.0 - lb) * sg * (1.0 - sg)
            d_ref[:, 2 * HG_WIDTH + h * HG_HEAD:2 * HG_WIDTH + (h + 1) * HG_HEAD] = dv
            dlb.append(jnp.sum(df * (1.0 - sg), axis=0, keepdims=True))
        _acc(dlb_ref, jnp.concatenate(dlb, axis=1), first)

    rev = lambda j: pl.BlockSpec((HG_CHUNK, HG_WIDTH), lambda c: (nc - 1 - c, j))
    return _call(body, name=name, grid=(nc,),
                 in_specs=[rev(0), rev(1), rev(2), pl.BlockSpec((2, HG_WIDTH), lambda c: (0, 0)),
                           pl.BlockSpec((1, HG_WIDTH, HG_HEAD), lambda c: (nc - 1 - c, 0, 0)), rev(0)],
                 out_specs=[pl.BlockSpec((HG_CHUNK, 3 * HG_WIDTH), lambda c: (nc - 1 - c, 0)), _vec(HG_WIDTH)],
                 out_shape=[_sds((t, 3 * HG_WIDTH)), _sds((1, HG_WIDTH))],
                 scratch_shapes=[pltpu.VMEM((nh, HG_HEAD, HG_HEAD), F32)])(proj, proj, proj, lb_logits, states, do)


def _to_sub(a, dil):
    t, w = a.shape
    return a if dil == 1 else a.reshape(t // dil, dil, w).transpose(1, 0, 2).reshape(t, w)


def _from_sub(a, dil):
    t, w = a.shape
    return a if dil == 1 else a.reshape(dil, t // dil, w).transpose(1, 0, 2).reshape(t, w)


def _att_masks():
    qi = lax.broadcasted_iota(jnp.int32, (ATT_BLOCK, ATT_BLOCK), 0)
    kj = lax.broadcasted_iota(jnp.int32, (ATT_BLOCK, ATT_BLOCK), 1)
    lane = lax.broadcasted_iota(jnp.int32, (1, 128), 1)
    return kj <= qi, kj >= qi, lane


def _attn_fwd(qkv, dil, name):
    t = qkv.shape[0]
    nb = t // ATT_BLOCK
    bps = nb // dil

    def body(q_ref, kc_ref, kp_ref, vc_ref, vp_ref, o_ref, l_ref):
        cur, prev, lane = _att_masks()
        prev = jnp.logical_and(prev, (pl.program_id(0) % bps) != 0)
        lse_all = jnp.zeros((ATT_BLOCK, 128), F32)
        for hp in range(ATT_HEADS // 2):
            sl = slice(hp * 128, (hp + 1) * 128)
            q2, kc, kp, vc, vp = q_ref[:, sl], kc_ref[:, sl], kp_ref[:, sl], vc_ref[:, sl], vp_ref[:, sl]
            outs = []
            for hh in range(2):
                mine = (lane < 64) if hh == 0 else (lane >= 64)
                qm = jnp.where(mine, q2, jnp.zeros_like(q2))
                sc = jnp.where(cur, _dot_nt(qm, kc) * 0.125, NEG)
                sp = jnp.where(prev, _dot_nt(qm, kp) * 0.125, NEG)
                mx = jnp.maximum(jnp.max(sc, axis=-1, keepdims=True), jnp.max(sp, axis=-1, keepdims=True))
                pc = jnp.exp(sc - mx)
                pp = jnp.exp(sp - mx)
                l = jnp.sum(pc, axis=-1, keepdims=True) + jnp.sum(pp, axis=-1, keepdims=True)
                acc = _dot(pc.astype(BF16), vc) + _dot(pp.astype(BF16), vp)
                outs.append(acc / l)
                lse_all = jnp.where(lane == 2 * hp + hh, mx + jnp.log(l), lse_all)
            o_ref[:, sl] = jnp.where(lane < 64, outs[0], outs[1])
        l_ref[...] = lse_all

    blk = lambda j, back: pl.BlockSpec((ATT_BLOCK, ATT_WIDTH), lambda n: (jnp.maximum(n - back, 0), j))
    return _call(body, name=name, grid=(nb,),
                 in_specs=[blk(0, 0), blk(1, 0), blk(1, 1), blk(2, 0), blk(2, 1)],
                 out_specs=[pl.BlockSpec((ATT_BLOCK, ATT_WIDTH), lambda n: (n, 0)),
                            pl.BlockSpec((ATT_BLOCK, 128), lambda n: (n, 0))],
                 out_shape=[_sds((t, ATT_WIDTH)), _sds((t, 128))])(qkv, qkv, qkv, qkv, qkv)


def _attn_combine(os_, ls_, name, tr=256):
    t = os_[0].shape[0]
    nbr = len(os_)

    def body(*refs):
        o_refs, l_refs, (a_ref, lt_ref) = refs[:nbr], refs[nbr:2 * nbr], refs[2 * nbr:]
        lane = lax.broadcasted_iota(jnp.int32, (1, 128), 1)
        ls = [r[...] for r in l_refs]
        mx = functools.reduce(jnp.maximum, ls)
        tot = mx + jnp.log(sum(jnp.exp(l - mx) for l in ls))
        lt_ref[...] = tot
        ws = [jnp.exp(l - tot) for l in ls]
        for hp in range(ATT_HEADS // 2):
            sl = slice(hp * 128, (hp + 1) * 128)
            acc = jnp.zeros((tr, 128), F32)
            for w, o_ref in zip(ws, o_refs):
                wf = jnp.where(lane < 64, w[:, 2 * hp:2 * hp + 1], w[:, 2 * hp + 1:2 * hp + 2])
                acc = acc + wf * o_ref[:, sl]
            a_ref[:, sl] = acc

    return _call(body, name=name, grid=(t // tr,),
                 in_specs=[_rows(tr, ATT_WIDTH)] * nbr + [_rows(tr, 128)] * nbr,
                 out_specs=[_rows(tr, ATT_WIDTH), _rows(tr, 128)],
                 out_shape=[_sds((t, ATT_WIDTH)), _sds((t, 128))])(*os_, *ls_)


def _attn_bwd(qkv, dout, lse, dd, dil, name):
    t = qkv.shape[0]
    nb = t // ATT_BLOCK
    bps = nb // dil

    def body(q_ref, qn_ref, kc_ref, kp_ref, vc_ref, vp_ref, do_ref, don_ref, l_ref, ln_ref, d_ref, dn_ref, g_ref):
        n = pl.program_id(0)
        cur, prev, lane = _att_masks()
        nxt = jnp.logical_and(prev, ((n + 1) % bps) != 0)
        prev = jnp.logical_and(prev, (n % bps) != 0)
        for hp in range(ATT_HEADS // 2):
            sl = slice(hp * 128, (hp + 1) * 128)
            q2, qn2, kc, kp, vc, vp = q_ref[:, sl], qn_ref[:, sl], kc_ref[:, sl], kp_ref[:, sl], vc_ref[:, sl], vp_ref[:, sl]
            do2, don2 = do_ref[:, sl], don_ref[:, sl]
            dq, dk, dv = [], [], []
            for hh in range(2):
                h = 2 * hp + hh
                mine = (lane < 64) if hh == 0 else (lane >= 64)
                zero = jnp.zeros_like(q2)
                qm, qnm = jnp.where(mine, q2, zero), jnp.where(mine, qn2, zero)
                dom, donm = jnp.where(mine, do2, zero), jnp.where(mine, don2, zero)
                ls, lsn = l_ref[:, h:h + 1], ln_ref[:, h:h + 1]
                dh, dhn = d_ref[:, h:h + 1], dn_ref[:, h:h + 1]
                pc = jnp.exp(jnp.where(cur, _dot_nt(qm, kc) * 0.125 - ls, NEG))
                pp = jnp.exp(jnp.where(prev, _dot_nt(qm, kp) * 0.125 - ls, NEG))
                px = jnp.exp(jnp.where(nxt, _dot_nt(qnm, kc) * 0.125 - lsn, NEG))
                dsc = (pc * (_dot_nt(dom, vc) - dh)).astype(BF16)
                dsp = (pp * (_dot_nt(dom, vp) - dh)).astype(BF16)
                dsx = (px * (_dot_nt(donm, vc) - dhn)).astype(BF16)
                dq.append((_dot(dsc, kc) + _dot(dsp, kp)) * 0.125)
                dk.append((_dot_tn(dsc, qm) + _dot_tn(dsx, qnm)) * 0.125)
                dv.append(_dot_tn(pc.astype(BF16), do2) + _dot_tn(px.astype(BF16), don2))
            lo = lane < 64
            g_ref[:, sl] = jnp.where(lo, dq[0], dq[1])
            g_ref[:, ATT_WIDTH + hp * 128:ATT_WIDTH + (hp + 1) * 128] = dk[0] + dk[1]
            g_ref[:, 2 * ATT_WIDTH + hp * 128:2 * ATT_WIDTH + (hp + 1) * 128] = jnp.where(lo, dv[0], dv[1])

    def blk(width, j, off):
        return pl.BlockSpec((ATT_BLOCK, width), lambda n: (jnp.clip(n + off, 0, nb - 1), j))

    w = ATT_WIDTH
    return _call(body, name=name, grid=(nb,),
                 in_specs=[blk(w, 0, 0), blk(w, 0, 1), blk(w, 1, 0), blk(w, 1, -1), blk(w, 2, 0), blk(w, 2, -1),
                           blk(w, 0, 0), blk(w, 0, 1), blk(128, 0, 0), blk(128, 0, 1), blk(128, 0, 0), blk(128, 0, 1)],
                 out_specs=pl.BlockSpec((ATT_BLOCK, 3 * w), lambda n: (n, 0)),
                 out_shape=_sds((t, 3 * w)))(qkv, qkv, qkv, qkv, qkv, qkv, dout, dout, lse, lse, dd, dd)


def _local_step(x, tgt, mod, norm1_g, lb_logits, og, ag, norm2_g, fg, get_w, put_g):
    shift1, scale1, gate1, shift2, scale2, gate2 = [mod[:, i * D_MODEL:(i + 1) * D_MODEL] for i in range(6)]
    fg = fg.reshape(1, D_MODEL)

    h1 = _norm_mod(x, norm1_g, scale1, shift1, "norm_mod1")
    w_in = get_w("w_in", h1)
    proj = _mm_nn(h1, w_in, "mm_in")
    o_hg, states = _hgrn_fwd(proj, lb_logits, "hgrn_fwd")
    qkv = proj[:, 4 * HG_WIDTH:].astype(BF16)
    qkvs = [_to_sub(qkv, d) for d in DILATIONS]
    outs = [_attn_fwd(q, d, f"attn_fwd{d}") for q, d in zip(qkvs, DILATIONS)]
    att, lse = _attn_combine([_from_sub(o, d) for (o, _), d in zip(outs, DILATIONS)],
                             [_from_sub(l, d) for (_, l), d in zip(outs, DILATIONS)], "attn_combine")
    mixin = _mix_in(o_hg, proj, att, og, ag, "mix_in")
    w_out = get_w("w_out", mixin)
    mix = _mm_nn(mixin, w_out, "mm_out")
    x2, h2 = _resid_norm_mod(x, mix, gate1, norm2_g, scale2, shift2, "resid_norm_mod2")
    w_gu = get_w("w_gu", h2)
    au = _mm_nn(h2, w_gu, "mm_gu")
    act = _swiglu(au, "swiglu")
    w_down = get_w("w_down", act)
    ffn = _mm_nn(act, w_down, "mm_down")
    dx3, dffn, loss_v, dfg, dgate2 = _final_loss(x2, ffn, gate2, fg, tgt, "final_loss")

    put_g("w_down", *_mm_tn(act, dffn, 1, "mm_down_dw", tk=D_FF // 2))
    dact = _mm_nt(dffn, w_down, "mm_down_dx")
    dau = _swiglu_bwd(dact, au, "swiglu_bwd")
    put_g("w_gu", *_mm_tn(h2, dau, N_SHARD, "mm_gu_dw"))
    dh2 = _mm_nt(dau, w_gu, "mm_gu_dx", tm=1024)
    dx2, dshift2, dscale2, dg2, dgate1, dmix = _norm_mod_bwd(
        dh2, x2, norm2_g, scale2, dx3, "norm_mod2_bwd", gate=gate1, mix=mix)
    put_g("w_out", *_mm_tn(mixin, dmix, 1, "mm_out_dw"))
    dmixin = _mm_nt(dmix, w_out, "mm_out_dx", tm=1024)
    do_hg, dg_raw, datt, dd, dog, dag = _mix_in_bwd(dmixin, o_hg, proj, att, og, ag, "mix_in_bwd")
    datt_b = datt.astype(BF16)
    dqkv = [_from_sub(_attn_bwd(q, _to_sub(datt_b, d), _to_sub(lse, d), _to_sub(dd, d), d, f"attn_bwd{d}"), d)
            for q, d in zip(qkvs, DILATIONS)]
    dhg, dlb = _hgrn_bwd(proj, lb_logits, states, do_hg, "hgrn_bwd")
    dproj = _dproj(dhg, dg_raw, dqkv, "dproj")
    put_g("w_in", *_mm_tn(h1, dproj, N_SHARD, "mm_in_dw"))
    dh1 = _mm_nt(dproj, w_in, "mm_in_dx", tm=1024)
    dx, dshift1, dscale1, dg1 = _norm_mod_bwd(dh1, x, norm1_g, scale1, dx2, "norm_mod1_bwd")

    stats = jnp.concatenate([loss_v, dfg, dg2, dg1, dlb, dag, dog,
                             dshift1, dscale1, dgate1, dshift2, dscale2, dgate2], axis=1)
    return dx, stats


def _place():
    x, y, c = lax.axis_index("x"), lax.axis_index("y"), lax.axis_index("c")
    return x, y, c


def _chip_peers(x, y, c):
    return [(1 - x, y, c), (x, 1 - y, c), (1 - x, 1 - y, c)]


def _comm_call(body, name, n_in, out_shape, scratch_shapes):
    hbm = pl.BlockSpec(memory_space=pl.ANY)
    return pl.pallas_call(body, name=name, in_specs=[hbm] * n_in, out_specs=[hbm] * len(out_shape),
                          out_shape=out_shape, scratch_shapes=scratch_shapes)


_HBM = pl.BlockSpec(memory_space=pltpu.HBM)
_SEM = pl.BlockSpec(memory_space=pltpu.SEMAPHORE)
_EFFECT = pltpu.SideEffectType.DATAFLOW_SIDE_EFFECTING


def _exchange_copy(bufs, send, recv, j, peer, place, gather):
    if gather:
        src = dst = bufs[0].at[2 * place[0] + place[1]]
    else:
        src, dst = bufs[0].at[2 * peer[0] + peer[1]], bufs[1].at[j]
    return pltpu.make_async_remote_copy(src_ref=src, dst_ref=dst, send_sem=send.at[j], recv_sem=recv.at[j],
                                        device_id=peer, device_id_type=MESH)


def _exchange_start(groups, after, gather, name):
    sizes = [len(g) for g in groups]
    flat = [b for g in groups for b in g]
    ng, nb = len(groups), len(flat)

    def body(*refs):
        bufs, sems, token = refs[:nb], refs[nb + 1:nb + 1 + 2 * ng], refs[-1]
        x, y, c = _place()
        for j, peer in enumerate(_chip_peers(x, y, c)):
            at = 0
            for i, size in enumerate(sizes):
                _exchange_copy(bufs[at:at + size], sems[2 * i], sems[2 * i + 1], j, peer, (x, y), gather).start()
                at += size
        token[...] = jnp.zeros_like(token)

    out = pl.pallas_call(
        body, name=name, in_specs=[_HBM] * nb + [pl.BlockSpec(memory_space=pl.ANY)],
        out_specs=[_SEM] * (2 * ng) + [_HBM] * nb + [pl.BlockSpec(memory_space=pltpu.VMEM)],
        out_shape=[pltpu.SemaphoreType.DMA((3,))] * (2 * ng) + [pltpu.HBM(b.shape, b.dtype) for b in flat] + [_sds((8, 128))],
        input_output_aliases={i: 2 * ng + i for i in range(nb)},
        compiler_params=pltpu.CompilerParams(has_side_effects=_EFFECT),
    )(*[pltpu.with_memory_space_constraint(b, pltpu.HBM) for b in flat], after)
    started, at = [], 2 * ng
    for i, size in enumerate(sizes):
        started.append((out[2 * i], out[2 * i + 1], tuple(out[at:at + size])))
        at += size
    return started, out[-1]


def _exchange_wait(started, after, gather, name):
    send, recv, bufs = started
    nb = len(bufs)

    def body(*refs):
        x, y, c = _place()
        for j, peer in enumerate(_chip_peers(x, y, c)):
            cp = _exchange_copy(refs[:nb], refs[nb], refs[nb + 1], j, peer, (x, y), gather)
            cp.wait_send()
            cp.wait_recv()

    return pl.pallas_call(
        body, name=name, in_specs=[_HBM] * nb + [_SEM, _SEM, pl.BlockSpec(memory_space=pl.ANY)],
        out_specs=[_HBM] * nb, out_shape=[pltpu.HBM(b.shape, b.dtype) for b in bufs],
        input_output_aliases={i: i for i in range(nb)},
        compiler_params=pltpu.CompilerParams(has_side_effects=_EFFECT),
    )(*bufs, send, recv, after)


def _swap_sibling(vs, name):
    n = len(vs)

    def body(*refs):
        v_refs, o_refs, (send, recv) = refs[:n], refs[n:2 * n], refs[2 * n:]
        x, y, c = _place()
        cps = [pltpu.make_async_remote_copy(
            src_ref=v_refs[a], dst_ref=o_refs[a], send_sem=send.at[a], recv_sem=recv.at[a],
            device_id=(x, y, 1 - c), device_id_type=MESH) for a in range(n)]
        for cp in cps:
            cp.start()
        for cp in cps:
            cp.wait()

    return _comm_call(body, name, n, [_sds(v.shape, v.dtype) for v in vs],
                      [pltpu.SemaphoreType.DMA((n,)), pltpu.SemaphoreType.DMA((n,))])(*vs)


def _gather_rows(v, name):
    r, n = v.shape

    def body(v_ref, o_ref, send, recv, loc):
        x, y, c = _place()
        me = 4 * x + 2 * y + c
        own = pltpu.make_async_copy(v_ref, o_ref.at[me], loc)
        own.start()
        peers = []
        for k in range(1, 8):
            px = 1 - x if k & 4 else x
            py = 1 - y if k & 2 else y
            pc = 1 - c if k & 1 else c
            peers.append((px, py, pc))
        sends = []
        for k, peer in enumerate(peers):
            cp = pltpu.make_async_remote_copy(src_ref=v_ref, dst_ref=o_ref.at[me], send_sem=send.at[k],
                                              recv_sem=recv.at[k], device_id=peer, device_id_type=MESH)
            cp.start()
            sends.append(cp)
        for k, peer in enumerate(peers):
            pltpu.make_async_remote_copy(src_ref=v_ref, dst_ref=o_ref.at[4 * peer[0] + 2 * peer[1] + peer[2]],
                                         send_sem=send.at[k], recv_sem=recv.at[k], device_id=peer,
                                         device_id_type=MESH).wait_recv()
        for cp in sends:
            cp.wait_send()
        own.wait()

    vmem = pl.BlockSpec(memory_space=pltpu.VMEM)
    return pl.pallas_call(body, name=name, in_specs=[vmem], out_specs=vmem, out_shape=_sds((8, r, n), v.dtype),
                          scratch_shapes=[pltpu.SemaphoreType.DMA((7,)), pltpu.SemaphoreType.DMA((7,)),
                                          pltpu.SemaphoreType.DMA])(v)


def _cast_place(w, shard, name):
    r, c = w.shape
    tr = r // 4

    def body(s_ref, w_ref, o_ref):
        o_ref[0] = w_ref[...].astype(BF16)

    return pl.pallas_call(
        body, name=name, out_shape=_sds((N_SHARD, r, c), BF16),
        grid_spec=pltpu.PrefetchScalarGridSpec(
            num_scalar_prefetch=1, grid=(4,), in_specs=[pl.BlockSpec((tr, c), lambda i, s: (i, 0))],
            out_specs=pl.BlockSpec((1, tr, c), lambda i, s: (s[0], i, 0))),
        compiler_params=pltpu.CompilerParams(dimension_semantics=("arbitrary",)),
    )(shard.reshape(1).astype(jnp.int32), w)


def _mod_part(c_all, w_ada, b_ada, name):
    n = w_ada.shape[1]

    def body(c_ref, w_ref, b_ref, a_ref, p_ref):
        cv = c_ref[...]
        ca = cv * _sigmoid(cv)
        a_ref[...] = ca
        p_ref[...] = jnp.dot(ca, w_ref[...], precision=lax.Precision.HIGHEST, preferred_element_type=F32) + b_ref[...]

    full = lambda a: pl.BlockSpec(a.shape, lambda i: (0, 0))
    return _call(body, name=name, grid=(1,), in_specs=[full(c_all), full(w_ada), full(b_ada)],
                 out_specs=[pl.BlockSpec((8, D_MODEL), lambda i: (0, 0)), pl.BlockSpec((8, n), lambda i: (0, 0))],
                 out_shape=[_sds((8, D_MODEL)), _sds((8, n))])(c_all, w_ada, b_ada)


def _sum_received(g_own, land, name):
    r, c = g_own.shape
    tr = r // 4

    def body(g_ref, l_ref, o_ref):
        o_ref[...] = ((g_ref[...] + l_ref[0].astype(F32)) + l_ref[1].astype(F32)) + l_ref[2].astype(F32)

    return _call(body, name=name, grid=(4,),
                 in_specs=[_rows(tr, c), pl.BlockSpec((3, tr, c), lambda i: (0, i, 0))],
                 out_specs=_rows(tr, c), out_shape=_sds((r, c)))(g_own, land)


def _outer_sum(ct, dm, name):
    k, n = ct.shape[0], dm.shape[1]
    tr = k // 4

    def body(c_ref, d_ref, o_ref):
        cv = c_ref[...]
        dv = d_ref[...]
        acc = cv[:, 0:1] * dv[0:1, :]
        for i in range(1, 8):
            acc = acc + cv[:, i:i + 1] * dv[i:i + 1, :]
        o_ref[...] = acc

    return _call(body, name=name, grid=(4,), in_specs=[_rows(tr, 8), pl.BlockSpec((8, n), lambda i: (0, 0))],
                 out_specs=_rows(tr, n), out_shape=_sds((k, n)))(ct, dm)


def _small_grads(stats, lb_logits, name):
    def body(s_ref, lg_ref, g_ref, l_ref):
        tot = s_ref[0:1, :]
        for i in range(1, 8):
            tot = tot + s_ref[i:i + 1, :]
        part = lambda off, n: tot[:, off:off + n]
        l_ref[...] = jnp.zeros((1, 128), F32) + (0.5 / D_MODEL) * jnp.sum(part(ST_LOSS, D_MODEL))
        lg = lg_ref[...]
        lb = _sigmoid(lg[0:1] - lg[1:2])
        dl0 = part(ST_DLB, HG_WIDTH) * lb * (1.0 - lb)
        g_ref[:, SP_BADA:SP_BADA + 6 * D_MODEL] = part(ST_DMOD, 6 * D_MODEL)
        g_ref[:, SP_N1:SP_N1 + D_MODEL] = part(ST_DG1, D_MODEL)
        g_ref[:, SP_LB:SP_LB + HG_WIDTH] = dl0
        g_ref[:, SP_LB + HG_WIDTH:SP_LB + 2 * HG_WIDTH] = -dl0
        g_ref[:, SP_OG:SP_OG + HG_HEAD] = part(ST_DOG, HG_HEAD)
        g_ref[:, SP_AG:SP_AG + ATT_WIDTH] = part(ST_DAG, ATT_WIDTH)
        g_ref[:, SP_N2:SP_N2 + D_MODEL] = part(ST_DG2, D_MODEL)
        g_ref[:, SP_FG:SP_FG + D_MODEL] = part(ST_DFG, D_MODEL)

    return _call(body, name=name, grid=(1,),
                 in_specs=[pl.BlockSpec((8, ST_WIDTH), lambda i: (0, 0)), pl.BlockSpec((2, HG_WIDTH), lambda i: (0, 0))],
                 out_specs=[pl.BlockSpec((1, SP_WIDTH), lambda i: (0, 0)), pl.BlockSpec((1, 128), lambda i: (0, 0))],
                 out_shape=[_sds((1, SP_WIDTH)), _sds((1, 128))])(stats, lb_logits)


def _adamw(w, gs, m, v, name, steps=4):
    r, c = w.shape
    tr = r // steps
    ng = len(gs)

    def body(*refs):
        w_ref, g_refs, (m_ref, v_ref, g_out, d_out, m_out, v_out) = refs[0], refs[1:1 + ng], refs[1 + ng:]
        g = g_refs[0][...]
        for g_ref in g_refs[1:]:
            g = g + g_ref[...]
        m_new = ADAM_B1 * m_ref[...] + (1.0 - ADAM_B1) * g
        v_new = ADAM_B2 * v_ref[...] + (1.0 - ADAM_B2) * (g * g)
        m_hat = m_new / (1.0 - ADAM_B1 ** ADAM_STEP)
        v_hat = v_new / (1.0 - ADAM_B2 ** ADAM_STEP)
        g_out[...] = g
        d_out[...] = -ADAM_LR * (m_hat / (jnp.sqrt(v_hat) + ADAM_EPS) + ADAM_WD * w_ref[...])
        m_out[...] = m_new
        v_out[...] = v_new

    row = _rows(tr, c)
    return _call(body, name=name, grid=(steps,), in_specs=[row] * (3 + ng), out_specs=[row] * 4,
                 out_shape=[_sds((r, c))] * 4)(w, *gs, m, v)


def kernel(x, c, w_ada, b_ada, norm1_g, w_in, hg_lb_logits, hg_onorm_g, att_onorm_g, w_out, norm2_g, w_gate_up, w_down, final_g, loss_target, m_w_ada, m_b_ada, m_norm1_g, m_w_in, m_hg_lb_logits, m_hg_onorm_g, m_att_onorm_g, m_w_out, m_norm2_g, m_w_gate_up, m_w_down, m_final_g, v_w_ada, v_b_ada, v_norm1_g, v_w_in, v_hg_lb_logits, v_hg_onorm_g, v_att_onorm_g, v_w_out, v_norm2_g, v_w_gate_up, v_w_down, v_final_g):
    ix, iy, ic = _place()
    shard = 2 * ix + iy
    sample = 4 * ix + 2 * iy + ic
    n_ada = w_ada.shape[2]

    shards = [w_in[0], w_out[0], w_gate_up[0], w_down[0]]
    names = ["w_in", "w_out", "w_gu", "w_down"]
    shapes = [(N_SHARD,) + w.shape for w in shards]
    placed = [(_cast_place(w, shard, "place_" + nm),) for w, nm in zip(shards, names)]

    c_all = _gather_rows(c, "gather_c").reshape(8, D_MODEL)
    b_part = lax.dynamic_slice(b_ada, (0, shard * n_ada), (1, n_ada))
    c_act, part = _mod_part(c_all, w_ada[0], b_part, "mod_part")
    parts = _gather_rows(part, "gather_mod")[::2]
    mod = lax.dynamic_index_in_dim(parts, sample, axis=1, keepdims=False).reshape(1, 6 * D_MODEL)
    gathering, token = _exchange_start(placed, mod, True, "gather_start")
    mod = mod + token[0:1, 0:1]

    def get_w(name, after):
        i = names.index(name)
        (full,) = _exchange_wait(gathering[i], after, True, "gather_wait_" + name)
        return full if name in ("w_in", "w_gu") else full.reshape(1, -1, D_MODEL)

    scattering = {}

    def put_g(name, g, g_bf16):
        shape = shapes[names.index(name)]
        land = lax.empty((3,) + shape[1:], BF16)
        (started,), _ = _exchange_start([(g_bf16.reshape(shape), land)], g, False, "scatter_start_" + name)
        scattering[name] = (g.reshape(shape), started)

    dx, stats = _local_step(x[0], loss_target[0], mod, norm1_g, hg_lb_logits, hg_onorm_g, att_onorm_g,
                            norm2_g, final_g, get_w, put_g)

    def summed(name, after):
        g, started = scattering[name]
        _, land = _exchange_wait(started, after, False, "scatter_wait_" + name)
        own = lax.dynamic_index_in_dim(g, shard, axis=0, keepdims=False)
        return _sum_received(own, land, "sum_" + name)

    stats_all = _gather_rows(stats, "gather_stats").reshape(8, ST_WIDTH)
    g_small, loss = _small_grads(stats_all, hg_lb_logits, "small_grads")
    dmod = lax.dynamic_slice(stats_all, (0, ST_DMOD + shard * n_ada), (8, n_ada))
    g_ada = _outer_sum(c_act.T, dmod, "w_ada_grad")

    smalls = [(b_ada, m_b_ada, v_b_ada), (norm1_g, m_norm1_g, v_norm1_g),
              (hg_lb_logits, m_hg_lb_logits, v_hg_lb_logits), (hg_onorm_g, m_hg_onorm_g, v_hg_onorm_g),
              (att_onorm_g, m_att_onorm_g, v_att_onorm_g), (norm2_g, m_norm2_g, v_norm2_g),
              (final_g, m_final_g, v_final_g)]
    pack = lambda i: jnp.concatenate([t[i].reshape(1, -1) for t in smalls], axis=1)
    small_out = _adamw(pack(0), [g_small], pack(1), pack(2), "adamw_small", steps=1)
    offs = [SP_BADA, SP_N1, SP_LB, SP_OG, SP_AG, SP_N2, SP_FG, SP_WIDTH]
    unpack = lambda a: [a[0, offs[i]:offs[i + 1]].reshape(smalls[i][0].shape) for i in range(7)]
    sg, sd, sm, sv = [unpack(a) for a in small_out]

    ada = _adamw(w_ada[0], [g_ada], m_w_ada[0], v_w_ada[0], "adamw_w_ada")
    moments = [(m_w_in, v_w_in), (m_w_out, v_w_out), (m_w_gate_up, v_w_gate_up), (m_w_down, v_w_down)]

    def update(group, after, tag):
        sums = [summed(nm, after) for nm in group]
        other = _swap_sibling(sums, "swap_sums_" + tag)
        return {nm: _adamw(shards[names.index(nm)], [s, o], moments[names.index(nm)][0][0],
                           moments[names.index(nm)][1][0], "adamw_" + nm) for nm, s, o in zip(group, sums, other)}

    done = update(["w_down", "w_gu", "w_out"], ada[1], "late")
    done.update(update(["w_in"], done["w_out"][1], "in"))
    big = [ada] + [done[nm] for nm in names]
    bg, bd, bm, bv = [[t[i][None] for t in big] for i in range(4)]

    def order(b, s):
        return [b[0], s[0], s[1], b[1], s[2], s[3], s[4], b[2], s[5], b[3], b[4], s[6]]

    return (loss[0, 0], dx[None], *order(bg, sg), *order(bd, sd), *order(bm, sm), *order(bv, sv))
```

```python
import functools

import jax
import jax.numpy as jnp
from jax import lax
from jax.experimental import pallas as pl
from jax.experimental.pallas import tpu as pltpu

F32 = jnp.float32
BF16 = jnp.bfloat16
MESH = pl.DeviceIdType.MESH

D_MODEL = 1024
HG_WIDTH = 512
HG_HEAD = 128
HG_CHUNK = 64
ATT_WIDTH = 512
ATT_HEADS = 8
ATT_BLOCK = 128
DILATIONS = (1, 4, 16)
D_FF = 2816
IN_WIDTH = 3584
N_SHARD = 4
RMS_EPS = 1e-6
NEG = -1e30

ADAM_LR = 0.001
ADAM_B1 = 0.9
ADAM_B2 = 0.999
ADAM_EPS = 1e-08
ADAM_WD = 0.01
ADAM_STEP = 10

VMEM_LIMIT = 56 * 2**20

ST_LOSS, ST_DFG, ST_DG2, ST_DG1 = 0, 1024, 2048, 3072
ST_DLB, ST_DAG, ST_DOG, ST_DMOD = 4096, 4608, 5120, 5248
ST_WIDTH = 5248 + 6144
SP_BADA, SP_N1, SP_LB, SP_OG, SP_AG, SP_N2, SP_FG = 0, 6144, 7168, 8192, 8320, 8832, 9856
SP_WIDTH = 10880


def _call(body, *, name, grid, in_specs, out_specs, out_shape, scratch_shapes=()):
    return pl.pallas_call(
        body, name=name, grid=grid, in_specs=in_specs, out_specs=out_specs, out_shape=out_shape,
        scratch_shapes=list(scratch_shapes),
        compiler_params=pltpu.CompilerParams(
            dimension_semantics=("arbitrary",) * len(grid), vmem_limit_bytes=VMEM_LIMIT))


def _sds(shape, dtype=F32):
    return jax.ShapeDtypeStruct(shape, dtype)


def _dot(a, b):
    return jnp.dot(a, b, preferred_element_type=F32)


def _dot_nt(a, b):
    return lax.dot_general(a, b, (((1,), (1,)), ((), ())), preferred_element_type=F32)


def _dot_tn(a, b):
    return lax.dot_general(a, b, (((0,), (0,)), ((), ())), preferred_element_type=F32)


def _sigmoid(x):
    return 1.0 / (1.0 + jnp.exp(-x))


def _rows(tr, width):
    return pl.BlockSpec((tr, width), lambda i: (i, 0))


def _vec(width):
    return pl.BlockSpec((1, width), lambda i: (0, 0))


def _acc(ref, val, first):
    @pl.when(first)
    def _():
        ref[...] = val

    @pl.when(jnp.logical_not(first))
    def _():
        ref[...] += val


def _mm_nn(a, b3, name, tm=512):
    m, k = a.shape
    s, _, n = b3.shape

    def body(a_ref, b_ref, o_ref):
        o_ref[...] = _dot(a_ref[...], b_ref[0])

    return _call(
        body, name=name, grid=(s, m // tm),
        in_specs=[pl.BlockSpec((tm, k), lambda j, i: (i, 0)), pl.BlockSpec((1, k, n), lambda j, i: (j, 0, 0))],
        out_specs=pl.BlockSpec((tm, n), lambda j, i: (i, j)), out_shape=_sds((m, s * n)))(a, b3)


def _mm_nt(dy, b3, name, tm=512):
    m = dy.shape[0]
    s, k, n = b3.shape

    def body(dy_ref, b_ref, o_ref):
        _acc(o_ref, _dot_nt(dy_ref[...], b_ref[0]), pl.program_id(1) == 0)

    return _call(
        body, name=name, grid=(m // tm, s),
        in_specs=[pl.BlockSpec((tm, n), lambda i, j: (i, j)), pl.BlockSpec((1, k, n), lambda i, j: (j, 0, 0))],
        out_specs=pl.BlockSpec((tm, k), lambda i, j: (i, 0)), out_shape=_sds((m, k)))(dy, b3)


def _mm_tn(a, dy, s, name, tm=1024, tk=None):
    m, k = a.shape
    n = dy.shape[1] // s
    tk = k if tk is None else tk
    steps = m // tm

    def body(a_ref, dy_ref, o_ref, ob_ref):
        i = pl.program_id(2)
        _acc(o_ref, _dot_tn(a_ref[...], dy_ref[...])[None], i == 0)

        @pl.when(i == steps - 1)
        def _():
            ob_ref[...] = o_ref[...].astype(BF16)

    out = pl.BlockSpec((1, tk, n), lambda kk, j, i: (j, kk, 0))
    return _call(
        body, name=name, grid=(k // tk, s, steps),
        in_specs=[pl.BlockSpec((tm, tk), lambda kk, j, i: (i, kk)), pl.BlockSpec((tm, n), lambda kk, j, i: (i, j))],
        out_specs=[out, out], out_shape=[_sds((s, k, n)), _sds((s, k, n), BF16)])(a, dy)


def _rms(x):
    return lax.rsqrt(jnp.mean(x * x, axis=-1, keepdims=True) + RMS_EPS)


def _rms_bwd(dxh, xh, r):
    return r * (dxh - xh * jnp.mean(dxh * xh, axis=-1, keepdims=True))


def _norm_mod(x, g, scale, shift, name, tr=512):
    t = x.shape[0]

    def body(x_ref, g_ref, sc_ref, sh_ref, h_ref):
        xv = x_ref[...]
        n = xv * _rms(xv) * g_ref[...]
        h_ref[...] = (n * (1.0 + sc_ref[...]) + sh_ref[...]).astype(BF16)

    return _call(body, name=name, grid=(t // tr,),
                 in_specs=[_rows(tr, D_MODEL), _vec(D_MODEL), _vec(D_MODEL), _vec(D_MODEL)],
                 out_specs=_rows(tr, D_MODEL), out_shape=_sds((t, D_MODEL), BF16))(x, g, scale, shift)


def _mix_in(o_hg, proj, att, og, ag, name, tr=256):
    t = o_hg.shape[0]

    def body(o_ref, g_ref, a_ref, og_ref, ag_ref, m_ref):
        for h in range(HG_WIDTH // HG_HEAD):
            sl = slice(h * HG_HEAD, (h + 1) * HG_HEAD)
            oh = o_ref[:, sl]
            gv = g_ref[:, sl]
            m_ref[:, sl] = (oh * _rms(oh) * og_ref[...] * (gv * _sigmoid(gv))).astype(BF16)
        av = a_ref[...]
        m_ref[:, HG_WIDTH:] = (av * _rms(av) * ag_ref[...]).astype(BF16)

    return _call(body, name=name, grid=(t // tr,),
                 in_specs=[_rows(tr, HG_WIDTH), pl.BlockSpec((tr, HG_WIDTH), lambda i: (i, 3)), _rows(tr, ATT_WIDTH),
                           _vec(HG_HEAD), _vec(ATT_WIDTH)],
                 out_specs=_rows(tr, D_MODEL), out_shape=_sds((t, D_MODEL), BF16))(o_hg, proj, att, og, ag)


def _resid_norm_mod(x, mix, gate, g, scale, shift, name, tr=256):
    t = x.shape[0]

    def body(x_ref, m_ref, gt_ref, g_ref, sc_ref, sh_ref, x2_ref, h_ref):
        x2 = x_ref[...] + gt_ref[...] * m_ref[...]
        x2_ref[...] = x2
        n = x2 * _rms(x2) * g_ref[...]
        h_ref[...] = (n * (1.0 + sc_ref[...]) + sh_ref[...]).astype(BF16)

    return _call(body, name=name, grid=(t // tr,),
                 in_specs=[_rows(tr, D_MODEL), _rows(tr, D_MODEL)] + [_vec(D_MODEL)] * 4,
                 out_specs=[_rows(tr, D_MODEL), _rows(tr, D_MODEL)],
                 out_shape=[_sds((t, D_MODEL)), _sds((t, D_MODEL), BF16)])(x, mix, gate, g, scale, shift)


def _mm_gate_up(h, w_gu, name, tm=512):
    m, k = h.shape
    n = w_gu.shape[2]

    def body(h_ref, wa_ref, wu_ref, a_ref, u_ref, o_ref):
        hv = h_ref[...]
        a = _dot(hv, wa_ref[0])
        u = _dot(hv, wu_ref[0])
        a_ref[...] = a.astype(BF16)
        u_ref[...] = u.astype(BF16)
        o_ref[...] = (a * _sigmoid(a) * u).astype(BF16)

    out = pl.BlockSpec((tm, n), lambda j, i: (i, j))
    return _call(body, name=name, grid=(2, m // tm),
                 in_specs=[pl.BlockSpec((tm, k), lambda j, i: (i, 0)), pl.BlockSpec((1, k, n), lambda j, i: (j, 0, 0)),
                           pl.BlockSpec((1, k, n), lambda j, i: (j + 2, 0, 0))],
                 out_specs=[out, out, out], out_shape=[_sds((m, 2 * n), BF16)] * 3)(h, w_gu, w_gu)


def _mm_down_dx(dffn, w_down, a, u, name, tm=512):
    m = dffn.shape[0]
    _, k, n = w_down.shape

    def body(d_ref, w_ref, a_ref, u_ref, o_ref):
        dact = _dot_nt(d_ref[...], w_ref[0])
        av = a_ref[...].astype(F32)
        sg = _sigmoid(av)
        o_ref[:, :k] = (dact * u_ref[...].astype(F32) * sg * (1.0 + av * (1.0 - sg))).astype(BF16)
        o_ref[:, k:] = (dact * av * sg).astype(BF16)

    return _call(body, name=name, grid=(m // tm,),
                 in_specs=[_rows(tm, n), pl.BlockSpec((1, k, n), lambda i: (0, 0, 0)), _rows(tm, k), _rows(tm, k)],
                 out_specs=_rows(tm, 2 * k), out_shape=_sds((m, 2 * k), BF16))(dffn, w_down, a, u)


def _final_loss(x2, ffn, gate, fg, tgt, name, tr=256):
    t = x2.shape[0]

    def body(x_ref, f_ref, gt_ref, fg_ref, t_ref, dx_ref, df_ref, l_ref, dfg_ref, dgt_ref):
        first = pl.program_id(0) == 0
        ffn_v = f_ref[...]
        x3 = x_ref[...] + gt_ref[...] * ffn_v
        r = _rms(x3)
        xh = x3 * r
        err = xh * fg_ref[...] - t_ref[...]
        dy = err * (1.0 / D_MODEL)
        dx3 = _rms_bwd(dy * fg_ref[...], xh, r)
        dx_ref[...] = dx3
        df_ref[...] = (dx3 * gt_ref[...]).astype(BF16)
        _acc(l_ref, jnp.sum(err * err, axis=0, keepdims=True), first)
        _acc(dfg_ref, jnp.sum(dy * xh, axis=0, keepdims=True), first)
        _acc(dgt_ref, jnp.sum(dx3 * ffn_v, axis=0, keepdims=True), first)

    row, vec = _rows(tr, D_MODEL), _vec(D_MODEL)
    return _call(body, name=name, grid=(t // tr,), in_specs=[row, row, vec, vec, row],
                 out_specs=[row, row, vec, vec, vec],
                 out_shape=[_sds((t, D_MODEL)), _sds((t, D_MODEL), BF16)] + [_sds((1, D_MODEL))] * 3)(
                     x2, ffn, gate, fg, tgt)


def _norm_mod_bwd(dh, x, g, scale, dres, name, gate=None, mix=None, tr=256):
    t = x.shape[0]
    below = gate is not None

    def body(*refs):
        if below:
            dh_ref, x_ref, g_ref, sc_ref, dr_ref, gt_ref, m_ref, dx_ref, dsh_ref, dsc_ref, dg_ref, dgt_ref, dm_ref = refs
        else:
            dh_ref, x_ref, g_ref, sc_ref, dr_ref, dx_ref, dsh_ref, dsc_ref, dg_ref = refs
        first = pl.program_id(0) == 0
        xv = x_ref[...]
        dhv = dh_ref[...]
        r = _rms(xv)
        xh = xv * r
        dn = dhv * (1.0 + sc_ref[...])
        dx = dr_ref[...] + _rms_bwd(dn * g_ref[...], xh, r)
        dx_ref[...] = dx
        _acc(dsh_ref, jnp.sum(dhv, axis=0, keepdims=True), first)
        _acc(dsc_ref, jnp.sum(dhv * xh * g_ref[...], axis=0, keepdims=True), first)
        _acc(dg_ref, jnp.sum(dn * xh, axis=0, keepdims=True), first)
        if below:
            _acc(dgt_ref, jnp.sum(dx * m_ref[...], axis=0, keepdims=True), first)
            dm_ref[...] = (dx * gt_ref[...]).astype(BF16)

    row, vec = _rows(tr, D_MODEL), _vec(D_MODEL)
    in_specs = [row, row, vec, vec, row] + ([vec, row] if below else [])
    out_specs = [row, vec, vec, vec] + ([vec, row] if below else [])
    out_shape = [_sds((t, D_MODEL))] + [_sds((1, D_MODEL))] * 3 + ([_sds((1, D_MODEL)), _sds((t, D_MODEL), BF16)] if below else [])
    args = (dh, x, g, scale, dres) + ((gate, mix) if below else ())
    return _call(body, name=name, grid=(t // tr,), in_specs=in_specs, out_specs=out_specs, out_shape=out_shape)(*args)


def _mix_in_bwd(dmi, o_hg, proj, att, og, ag, name, tr=256):
    t = o_hg.shape[0]

    def body(d_ref, o_ref, g_ref, a_ref, og_ref, ag_ref, do_ref, dg_ref, da_ref, dd_ref, dog_ref, dag_ref):
        first = pl.program_id(0) == 0
        dog = jnp.zeros((1, HG_HEAD), F32)
        for h in range(HG_WIDTH // HG_HEAD):
            sl = slice(h * HG_HEAD, (h + 1) * HG_HEAD)
            oh = o_ref[:, sl]
            gv = g_ref[:, sl]
            dv = d_ref[:, sl]
            r = _rms(oh)
            xh = oh * r
            sg = _sigmoid(gv)
            dno = dv * gv * sg
            dg_ref[:, sl] = dv * xh * og_ref[...] * sg * (1.0 + gv * (1.0 - sg))
            dog = dog + jnp.sum(dno * xh, axis=0, keepdims=True)
            do_ref[:, sl] = _rms_bwd(dno * og_ref[...], xh, r)
        _acc(dog_ref, dog, first)
        av = a_ref[...]
        dav = d_ref[:, HG_WIDTH:]
        r = _rms(av)
        xa = av * r
        _acc(dag_ref, jnp.sum(dav * xa, axis=0, keepdims=True), first)
        datt = _rms_bwd(dav * ag_ref[...], xa, r)
        da_ref[...] = datt
        prod = datt * av
        lane = lax.broadcasted_iota(jnp.int32, (1, 128), 1)
        dd = jnp.zeros((tr, 128), F32)
        for hp in range(ATT_HEADS // 2):
            pp = prod[:, hp * 128:(hp + 1) * 128]
            lo = jnp.sum(jnp.where(lane < 64, pp, 0.0), axis=-1, keepdims=True)
            hi = jnp.sum(jnp.where(lane >= 64, pp, 0.0), axis=-1, keepdims=True)
            dd = jnp.where(lane == 2 * hp, lo, dd)
            dd = jnp.where(lane == 2 * hp + 1, hi, dd)
        dd_ref[...] = dd

    half = _rows(tr, HG_WIDTH)
    return _call(body, name=name, grid=(t // tr,),
                 in_specs=[_rows(tr, D_MODEL), half, pl.BlockSpec((tr, HG_WIDTH), lambda i: (i, 3)), half,
                           _vec(HG_HEAD), _vec(ATT_WIDTH)],
                 out_specs=[half, half, half, _rows(tr, 128), _vec(HG_HEAD), _vec(ATT_WIDTH)],
                 out_shape=[_sds((t, HG_WIDTH))] * 3 + [_sds((t, 128)), _sds((1, HG_HEAD)), _sds((1, ATT_WIDTH))])(
                     dmi, o_hg, proj, att, og, ag)


def _dproj(dhg, dg, dqkv, name, tr=256):
    t = dhg.shape[0]
    w3 = 3 * HG_WIDTH

    def body(h_ref, g_ref, a1_ref, a2_ref, a3_ref, o_ref):
        o_ref[:, :w3] = h_ref[...].astype(BF16)
        o_ref[:, w3:w3 + HG_WIDTH] = g_ref[...].astype(BF16)
        o_ref[:, w3 + HG_WIDTH:] = (a1_ref[...] + a2_ref[...] + a3_ref[...]).astype(BF16)

    return _call(body, name=name, grid=(t // tr,),
                 in_specs=[_rows(tr, w3), _rows(tr, HG_WIDTH)] + [_rows(tr, w3)] * 3,
                 out_specs=_rows(tr, IN_WIDTH), out_shape=_sds((t, IN_WIDTH), BF16))(dhg, dg, *dqkv)


def _hg_gates(f_raw, q_raw, lb, tri):
    sg = _sigmoid(f_raw)
    f = lb + (1.0 - lb) * sg
    k = 1.0 - f
    b = jnp.dot(tri, jnp.log(f), precision=lax.Precision.HIGHEST, preferred_element_type=F32)
    sq = _sigmoid(q_raw)
    return sg, f, k, b, sq


def _hgrn_fwd(proj, lb_logits, name):
    t = proj.shape[0]
    nc = t // HG_CHUNK
    nh = HG_WIDTH // HG_HEAD

    def body(q_ref, f_ref, i_ref, lg_ref, o_ref, st_ref, s_scr):
        @pl.when(pl.program_id(0) == 0)
        def _():
            s_scr[...] = jnp.zeros_like(s_scr)

        lg = lg_ref[...]
        lb_all = _sigmoid(lg[0:1] - lg[1:2])
        row = lax.broadcasted_iota(jnp.int32, (HG_CHUNK, HG_CHUNK), 0)
        col = lax.broadcasted_iota(jnp.int32, (HG_CHUNK, HG_CHUNK), 1)
        causal = row >= col
        tri = causal.astype(F32)
        for h in range(nh):
            sl = slice(h * HG_HEAD, (h + 1) * HG_HEAD)
            q_raw = q_ref[:, sl]
            _, _, k, b, sq = _hg_gates(f_ref[:, sl], q_raw, lb_all[:, sl], tri)
            v = i_ref[:, sl].astype(BF16)
            gl = b[HG_CHUNK - 1:HG_CHUNK]
            qd = (q_raw * sq * jnp.exp(b)).astype(BF16)
            kd = (k * jnp.exp(-b)).astype(BF16)
            ke = (k * jnp.exp(gl - b)).astype(BF16)
            st = s_scr[h]
            st_ref[0, sl, :] = st
            a = jnp.where(causal, _dot_nt(qd, kd), 0.0).astype(BF16)
            o_ref[:, sl] = _dot_nt(qd, st.astype(BF16)) + _dot(a, v)
            s_scr[h] = st * jnp.exp(gl) + _dot_tn(v, ke)

    blk = lambda j: pl.BlockSpec((HG_CHUNK, HG_WIDTH), lambda c: (c, j))
    return _call(body, name=name, grid=(nc,),
                 in_specs=[blk(0), blk(1), blk(2), pl.BlockSpec((2, HG_WIDTH), lambda c: (0, 0))],
                 out_specs=[blk(0), pl.BlockSpec((1, HG_WIDTH, HG_HEAD), lambda c: (c, 0, 0))],
                 out_shape=[_sds((t, HG_WIDTH)), _sds((nc, HG_WIDTH, HG_HEAD))],
                 scratch_shapes=[pltpu.VMEM((nh, HG_HEAD, HG_HEAD), F32)])(proj, proj, proj, lb_logits)


def _hgrn_bwd(proj, lb_logits, states, do, name):
    t = proj.shape[0]
    nc = t // HG_CHUNK
    nh = HG_WIDTH // HG_HEAD

    def body(q_ref, f_ref, i_ref, lg_ref, st_ref, do_ref, d_ref, dlb_ref, ds_scr):
        first = pl.program_id(0) == 0

        @pl.when(first)
        def _():
            ds_scr[...] = jnp.zeros_like(ds_scr)

        lg = lg_ref[...]
        lb_all = _sigmoid(lg[0:1] - lg[1:2])
        row = lax.broadcasted_iota(jnp.int32, (HG_CHUNK, HG_CHUNK), 0)
        col = lax.broadcasted_iota(jnp.int32, (HG_CHUNK, HG_CHUNK), 1)
        causal = row >= col
        tri = causal.astype(F32)
        tri_t = (row <= col).astype(F32)
        dlb = []
        for h in range(nh):
            sl = slice(h * HG_HEAD, (h + 1) * HG_HEAD)
            q_raw = q_ref[:, sl]
            lb = lb_all[:, sl]
            sg, f, k, b, sq = _hg_gates(f_ref[:, sl], q_raw, lb, tri)
            v = i_ref[:, sl].astype(BF16)
            gl = b[HG_CHUNK - 1:HG_CHUNK]
            egl = jnp.exp(gl)
            eb = jnp.exp(b)
            enb = jnp.exp(-b)
            egb = jnp.exp(gl - b)
            qd = q_raw * sq * eb
            kd = k * enb
            ke = k * egb
            qd_b, kd_b, ke_b = qd.astype(BF16), kd.astype(BF16), ke.astype(BF16)
            st = st_ref[0, sl, :]
            dst = ds_scr[h]
            dst_b = dst.astype(BF16)
            dov = do_ref[:, sl].astype(BF16)
            a = jnp.where(causal, _dot_nt(qd_b, kd_b), 0.0).astype(BF16)
            da = jnp.where(causal, _dot_nt(dov, v), 0.0).astype(BF16)
            dqd = _dot(dov, st.astype(BF16)) + _dot(da, kd_b)
            dkd = _dot_tn(da, qd_b)
            dv = _dot_tn(a, dov) + _dot_nt(ke_b, dst_b)
            dke = _dot(v, dst_b)
            dgl = jnp.sum(dst * st, axis=0, keepdims=True) * egl
            ds_scr[h] = _dot_tn(dov, qd_b) + dst * egl
            t1 = dke * ke
            db = dqd * qd_b.astype(F32) - dkd * kd_b.astype(F32) - t1
            dgl = dgl + jnp.sum(t1, axis=0, keepdims=True)
            dlf = jnp.dot(tri_t, db, precision=lax.Precision.HIGHEST, preferred_element_type=F32) + dgl
            df = dlf / f - (dkd * enb + dke * egb)
            d_ref[:, sl] = dqd * eb * sq * (1.0 + q_raw * (1.0 - sq))
            d_ref[:, HG_WIDTH + h * HG_HEAD:HG_WIDTH + (h + 1) * HG_HEAD] = df * (1.0 - lb) * sg * (1.0 - sg)
            d_ref[:, 2 * HG_WIDTH + h * HG_HEAD:2 * HG_WIDTH + (h + 1) * HG_HEAD] = dv
            dlb.append(jnp.sum(df * (1.0 - sg), axis=0, keepdims=True))
        _acc(dlb_ref, jnp.concatenate(dlb, axis=1), first)

    rev = lambda j: pl.BlockSpec((HG_CHUNK, HG_WIDTH), lambda c: (nc - 1 - c, j))
    return _call(body, name=name, grid=(nc,),
                 in_specs=[rev(0), rev(1), rev(2), pl.BlockSpec((2, HG_WIDTH), lambda c: (0, 0)),
                           pl.BlockSpec((1, HG_WIDTH, HG_HEAD), lambda c: (nc - 1 - c, 0, 0)), rev(0)],
                 out_specs=[pl.BlockSpec((HG_CHUNK, 3 * HG_WIDTH), lambda c: (nc - 1 - c, 0)), _vec(HG_WIDTH)],
                 out_shape=[_sds((t, 3 * HG_WIDTH)), _sds((1, HG_WIDTH))],
                 scratch_shapes=[pltpu.VMEM((nh, HG_HEAD, HG_HEAD), F32)])(proj, proj, proj, lb_logits, states, do)


def _to_sub(a, dil):
    t, w = a.shape
    return a if dil == 1 else a.reshape(t // dil, dil, w).transpose(1, 0, 2).reshape(t, w)


def _from_sub(a, dil):
    t, w = a.shape
    return a if dil == 1 else a.reshape(dil, t // dil, w).transpose(1, 0, 2).reshape(t, w)


def _att_masks():
    qi = lax.broadcasted_iota(jnp.int32, (ATT_BLOCK, ATT_BLOCK), 0)
    kj = lax.broadcasted_iota(jnp.int32, (ATT_BLOCK, ATT_BLOCK), 1)
    lane = lax.broadcasted_iota(jnp.int32, (1, 128), 1)
    return kj <= qi, kj >= qi, lane


def _attn_fwd(qkv, dil, name):
    t = qkv.shape[0]
    nb = t // ATT_BLOCK
    bps = nb // dil

    def body(q_ref, kc_ref, kp_ref, vc_ref, vp_ref, o_ref, l_ref):
        cur, prev, lane = _att_masks()
        prev = jnp.logical_and(prev, (pl.program_id(0) % bps) != 0)
        lse_all = jnp.zeros((ATT_BLOCK, 128), F32)
        for hp in range(ATT_HEADS // 2):
            sl = slice(hp * 128, (hp + 1) * 128)
            q2, kc, kp, vc, vp = q_ref[:, sl], kc_ref[:, sl], kp_ref[:, sl], vc_ref[:, sl], vp_ref[:, sl]
            outs = []
            for hh in range(2):
                mine = (lane < 64) if hh == 0 else (lane >= 64)
                qm = jnp.where(mine, q2, jnp.zeros_like(q2))
                sc = jnp.where(cur, _dot_nt(qm, kc) * 0.125, NEG)
                sp = jnp.where(prev, _dot_nt(qm, kp) * 0.125, NEG)
                mx = jnp.maximum(jnp.max(sc, axis=-1, keepdims=True), jnp.max(sp, axis=-1, keepdims=True))
                pc = jnp.exp(sc - mx)
                pp = jnp.exp(sp - mx)
                l = jnp.sum(pc, axis=-1, keepdims=True) + jnp.sum(pp, axis=-1, keepdims=True)
                acc = _dot(pc.astype(BF16), vc) + _dot(pp.astype(BF16), vp)
                outs.append(acc / l)
                lse_all = jnp.where(lane == 2 * hp + hh, mx + jnp.log(l), lse_all)
            o_ref[:, sl] = jnp.where(lane < 64, outs[0], outs[1])
        l_ref[...] = lse_all

    blk = lambda j, back: pl.BlockSpec((ATT_BLOCK, ATT_WIDTH), lambda n: (jnp.maximum(n - back, 0), j))
    return _call(body, name=name, grid=(nb,),
                 in_specs=[blk(0, 0), blk(1, 0), blk(1, 1), blk(2, 0), blk(2, 1)],
                 out_specs=[pl.BlockSpec((ATT_BLOCK, ATT_WIDTH), lambda n: (n, 0)),
                            pl.BlockSpec((ATT_BLOCK, 128), lambda n: (n, 0))],
                 out_shape=[_sds((t, ATT_WIDTH)), _sds((t, 128))])(qkv, qkv, qkv, qkv, qkv)


def _attn_combine(os_, ls_, name, tr=256):
    t = os_[0].shape[0]
    nbr = len(os_)

    def body(*refs):
        o_refs, l_refs, (a_ref, lt_ref) = refs[:nbr], refs[nbr:2 * nbr], refs[2 * nbr:]
        lane = lax.broadcasted_iota(jnp.int32, (1, 128), 1)
        ls = [r[...] for r in l_refs]
        mx = functools.reduce(jnp.maximum, ls)
        tot = mx + jnp.log(sum(jnp.exp(l - mx) for l in ls))
        lt_ref[...] = tot
        ws = [jnp.exp(l - tot) for l in ls]
        for hp in range(ATT_HEADS // 2):
            sl = slice(hp * 128, (hp + 1) * 128)
            acc = jnp.zeros((tr, 128), F32)
            for w, o_ref in zip(ws, o_refs):
                wf = jnp.where(lane < 64, w[:, 2 * hp:2 * hp + 1], w[:, 2 * hp + 1:2 * hp + 2])
                acc = acc + wf * o_ref[:, sl]
            a_ref[:, sl] = acc

    return _call(body, name=name, grid=(t // tr,),
                 in_specs=[_rows(tr, ATT_WIDTH)] * nbr + [_rows(tr, 128)] * nbr,
                 out_specs=[_rows(tr, ATT_WIDTH), _rows(tr, 128)],
                 out_shape=[_sds((t, ATT_WIDTH)), _sds((t, 128))])(*os_, *ls_)


def _attn_bwd(qkv, dout, lse, dd, dil, name):
    t = qkv.shape[0]
    nb = t // ATT_BLOCK
    bps = nb // dil

    def body(q_ref, qn_ref, kc_ref, kp_ref, vc_ref, vp_ref, do_ref, don_ref, l_ref, ln_ref, d_ref, dn_ref, g_ref):
        n = pl.program_id(0)
        cur, prev, lane = _att_masks()
        nxt = jnp.logical_and(prev, ((n + 1) % bps) != 0)
        prev = jnp.logical_and(prev, (n % bps) != 0)
        for hp in range(ATT_HEADS // 2):
            sl = slice(hp * 128, (hp + 1) * 128)
            q2, qn2, kc, kp, vc, vp = q_ref[:, sl], qn_ref[:, sl], kc_ref[:, sl], kp_ref[:, sl], vc_ref[:, sl], vp_ref[:, sl]
            do2, don2 = do_ref[:, sl], don_ref[:, sl]
            dq, dk, dv = [], [], []
            for hh in range(2):
                h = 2 * hp + hh
                mine = (lane < 64) if hh == 0 else (lane >= 64)
                zero = jnp.zeros_like(q2)
                qm, qnm = jnp.where(mine, q2, zero), jnp.where(mine, qn2, zero)
                dom, donm = jnp.where(mine, do2, zero), jnp.where(mine, don2, zero)
                ls, lsn = l_ref[:, h:h + 1], ln_ref[:, h:h + 1]
                dh, dhn = d_ref[:, h:h + 1], dn_ref[:, h:h + 1]
                pc = jnp.exp(jnp.where(cur, _dot_nt(qm, kc) * 0.125 - ls, NEG))
                pp = jnp.exp(jnp.where(prev, _dot_nt(qm, kp) * 0.125 - ls, NEG))
                px = jnp.exp(jnp.where(nxt, _dot_nt(qnm, kc) * 0.125 - lsn, NEG))
                dsc = (pc * (_dot_nt(dom, vc) - dh)).astype(BF16)
                dsp = (pp * (_dot_nt(dom, vp) - dh)).astype(BF16)
                dsx = (px * (_dot_nt(donm, vc) - dhn)).astype(BF16)
                dq.append((_dot(dsc, kc) + _dot(dsp, kp)) * 0.125)
                dk.append((_dot_tn(dsc, qm) + _dot_tn(dsx, qnm)) * 0.125)
                dv.append(_dot_tn(pc.astype(BF16), do2) + _dot_tn(px.astype(BF16), don2))
            lo = lane < 64
            g_ref[:, sl] = jnp.where(lo, dq[0], dq[1])
            g_ref[:, ATT_WIDTH + hp * 128:ATT_WIDTH + (hp + 1) * 128] = dk[0] + dk[1]
            g_ref[:, 2 * ATT_WIDTH + hp * 128:2 * ATT_WIDTH + (hp + 1) * 128] = jnp.where(lo, dv[0], dv[1])

    def blk(width, j, off):
        return pl.BlockSpec((ATT_BLOCK, width), lambda n: (jnp.clip(n + off, 0, nb - 1), j))

    w = ATT_WIDTH
    return _call(body, name=name, grid=(nb,),
                 in_specs=[blk(w, 0, 0), blk(w, 0, 1), blk(w, 1, 0), blk(w, 1, -1), blk(w, 2, 0), blk(w, 2, -1),
                           blk(w, 0, 0), blk(w, 0, 1), blk(128, 0, 0), blk(128, 0, 1), blk(128, 0, 0), blk(128, 0, 1)],
                 out_specs=pl.BlockSpec((ATT_BLOCK, 3 * w), lambda n: (n, 0)),
                 out_shape=_sds((t, 3 * w)))(qkv, qkv, qkv, qkv, qkv, qkv, dout, dout, lse, lse, dd, dd)


def _local_step(x, tgt, mod, norm1_g, lb_logits, og, ag, norm2_g, fg, get_w, put_g):
    shift1, scale1, gate1, shift2, scale2, gate2 = [mod[:, i * D_MODEL:(i + 1) * D_MODEL] for i in range(6)]
    fg = fg.reshape(1, D_MODEL)

    h1 = _norm_mod(x, norm1_g, scale1, shift1, "norm_mod1")
    w_in = get_w("w_in", h1)
    proj = _mm_nn(h1, w_in, "mm_in")
    o_hg, states = _hgrn_fwd(proj, lb_logits, "hgrn_fwd")
    qkv = proj[:, 4 * HG_WIDTH:].astype(BF16)
    qkvs = [_to_sub(qkv, d) for d in DILATIONS]
    outs = [_attn_fwd(q, d, f"attn_fwd{d}") for q, d in zip(qkvs, DILATIONS)]
    att, lse = _attn_combine([_from_sub(o, d) for (o, _), d in zip(outs, DILATIONS)],
                             [_from_sub(l, d) for (_, l), d in zip(outs, DILATIONS)], "attn_combine")
    mixin = _mix_in(o_hg, proj, att, og, ag, "mix_in")
    w_out = get_w("w_out", mixin)
    mix = _mm_nn(mixin, w_out, "mm_out")
    x2, h2 = _resid_norm_mod(x, mix, gate1, norm2_g, scale2, shift2, "resid_norm_mod2")
    w_gu = get_w("w_gu", h2)
    a_ff, u_ff, act = _mm_gate_up(h2, w_gu, "mm_gu")
    w_down = get_w("w_down", act)
    ffn = _mm_nn(act, w_down, "mm_down")
    dx3, dffn, loss_v, dfg, dgate2 = _final_loss(x2, ffn, gate2, fg, tgt, "final_loss")

    put_g("w_down", *_mm_tn(act, dffn, 1, "mm_down_dw", tk=D_FF // 2))
    dau = _mm_down_dx(dffn, w_down, a_ff, u_ff, "mm_down_dx")
    put_g("w_gu", *_mm_tn(h2, dau, N_SHARD, "mm_gu_dw"))
    dh2 = _mm_nt(dau, w_gu, "mm_gu_dx", tm=1024)
    dx2, dshift2, dscale2, dg2, dgate1, dmix = _norm_mod_bwd(
        dh2, x2, norm2_g, scale2, dx3, "norm_mod2_bwd", gate=gate1, mix=mix)
    put_g("w_out", *_mm_tn(mixin, dmix, 1, "mm_out_dw"))
    dmixin = _mm_nt(dmix, w_out, "mm_out_dx", tm=1024)
    do_hg, dg_raw, datt, dd, dog, dag = _mix_in_bwd(dmixin, o_hg, proj, att, og, ag, "mix_in_bwd")
    datt_b = datt.astype(BF16)
    dqkv = [_from_sub(_attn_bwd(q, _to_sub(datt_b, d), _to_sub(lse, d), _to_sub(dd, d), d, f"attn_bwd{d}"), d)
            for q, d in zip(qkvs, DILATIONS)]
    dhg, dlb = _hgrn_bwd(proj, lb_logits, states, do_hg, "hgrn_bwd")
    dproj = _dproj(dhg, dg_raw, dqkv, "dproj")
    put_g("w_in", *_mm_tn(h1, dproj, N_SHARD, "mm_in_dw"))
    dh1 = _mm_nt(dproj, w_in, "mm_in_dx", tm=1024)
    dx, dshift1, dscale1, dg1 = _norm_mod_bwd(dh1, x, norm1_g, scale1, dx2, "norm_mod1_bwd")

    stats = jnp.concatenate([loss_v, dfg, dg2, dg1, dlb, dag, dog,
                             dshift1, dscale1, dgate1, dshift2, dscale2, dgate2], axis=1)
    return dx, stats


def _place():
    x, y, c = lax.axis_index("x"), lax.axis_index("y"), lax.axis_index("c")
    return x, y, c


def _chip_peers(x, y, c):
    return [(1 - x, y, c), (x, 1 - y, c), (1 - x, 1 - y, c)]


def _comm_call(body, name, n_in, out_shape, scratch_shapes):
    hbm = pl.BlockSpec(memory_space=pl.ANY)
    return pl.pallas_call(body, name=name, in_specs=[hbm] * n_in, out_specs=[hbm] * len(out_shape),
                          out_shape=out_shape, scratch_shapes=scratch_shapes)


_HBM = pl.BlockSpec(memory_space=pltpu.HBM)
_SEM = pl.BlockSpec(memory_space=pltpu.SEMAPHORE)
_EFFECT = pltpu.SideEffectType.DATAFLOW_SIDE_EFFECTING


def _exchange_copy(bufs, send, recv, j, peer, place, gather):
    if gather:
        src = dst = bufs[0].at[2 * place[0] + place[1]]
    else:
        src, dst = bufs[0].at[2 * peer[0] + peer[1]], bufs[1].at[j]
    return pltpu.make_async_remote_copy(src_ref=src, dst_ref=dst, send_sem=send.at[j], recv_sem=recv.at[j],
                                        device_id=peer, device_id_type=MESH)


def _exchange_start(groups, after, gather, name):
    sizes = [len(g) for g in groups]
    flat = [b for g in groups for b in g]
    ng, nb = len(groups), len(flat)

    def body(*refs):
        bufs, sems = refs[:nb], refs[nb + 1:nb + 1 + 2 * ng]
        x, y, c = _place()
        for j, peer in enumerate(_chip_peers(x, y, c)):
            at = 0
            for i, size in enumerate(sizes):
                _exchange_copy(bufs[at:at + size], sems[2 * i], sems[2 * i + 1], j, peer, (x, y), gather).start()
                at += size

    any_space = pl.BlockSpec(memory_space=pl.ANY)
    out = pl.pallas_call(
        body, name=name, in_specs=[_HBM] * nb + [any_space],
        out_specs=[_SEM] * (2 * ng) + [_HBM] * nb + [any_space],
        out_shape=[pltpu.SemaphoreType.DMA((3,))] * (2 * ng) + [pltpu.HBM(b.shape, b.dtype) for b in flat]
        + [_sds(after.shape, after.dtype)],
        input_output_aliases={i: 2 * ng + i for i in range(nb + 1)},
        compiler_params=pltpu.CompilerParams(has_side_effects=_EFFECT),
    )(*[pltpu.with_memory_space_constraint(b, pltpu.HBM) for b in flat], after)
    started, at = [], 2 * ng
    for i, size in enumerate(sizes):
        started.append((out[2 * i], out[2 * i + 1], tuple(out[at:at + size])))
        at += size
    return started, out[-1]


def _exchange_wait(started, after, gather, name):
    send, recv, bufs = started
    nb = len(bufs)

    def body(*refs):
        x, y, c = _place()
        for j, peer in enumerate(_chip_peers(x, y, c)):
            cp = _exchange_copy(refs[:nb], refs[nb], refs[nb + 1], j, peer, (x, y), gather)
            cp.wait_send()
            cp.wait_recv()

    return pl.pallas_call(
        body, name=name, in_specs=[_HBM] * nb + [_SEM, _SEM, pl.BlockSpec(memory_space=pl.ANY)],
        out_specs=[_HBM] * nb, out_shape=[pltpu.HBM(b.shape, b.dtype) for b in bufs],
        input_output_aliases={i: i for i in range(nb)},
        compiler_params=pltpu.CompilerParams(has_side_effects=_EFFECT),
    )(*bufs, send, recv, after)


def _swap_sibling(vs, name):
    n = len(vs)

    def body(*refs):
        v_refs, o_refs, (send, recv) = refs[:n], refs[n:2 * n], refs[2 * n:]
        x, y, c = _place()
        cps = [pltpu.make_async_remote_copy(
            src_ref=v_refs[a], dst_ref=o_refs[a], send_sem=send.at[a], recv_sem=recv.at[a],
            device_id=(x, y, 1 - c), device_id_type=MESH) for a in range(n)]
        for cp in cps:
            cp.start()
        for cp in cps:
            cp.wait()

    return _comm_call(body, name, n, [_sds(v.shape, v.dtype) for v in vs],
                      [pltpu.SemaphoreType.DMA((n,)), pltpu.SemaphoreType.DMA((n,))])(*vs)


def _gather_rows(v, name):
    r, n = v.shape

    def body(v_ref, o_ref, send, recv, loc):
        x, y, c = _place()
        me = 4 * x + 2 * y + c
        own = pltpu.make_async_copy(v_ref, o_ref.at[me], loc)
        own.start()
        peers = []
        for k in range(1, 8):
            px = 1 - x if k & 4 else x
            py = 1 - y if k & 2 else y
            pc = 1 - c if k & 1 else c
            peers.append((px, py, pc))
        sends = []
        for k, peer in enumerate(peers):
            cp = pltpu.make_async_remote_copy(src_ref=v_ref, dst_ref=o_ref.at[me], send_sem=send.at[k],
                                              recv_sem=recv.at[k], device_id=peer, device_id_type=MESH)
            cp.start()
            sends.append(cp)
        for k, peer in enumerate(peers):
            pltpu.make_async_remote_copy(src_ref=v_ref, dst_ref=o_ref.at[4 * peer[0] + 2 * peer[1] + peer[2]],
                                         send_sem=send.at[k], recv_sem=recv.at[k], device_id=peer,
                                         device_id_type=MESH).wait_recv()
        for cp in sends:
            cp.wait_send()
        own.wait()

    vmem = pl.BlockSpec(memory_space=pltpu.VMEM)
    return pl.pallas_call(body, name=name, in_specs=[vmem], out_specs=vmem, out_shape=_sds((8, r, n), v.dtype),
                          scratch_shapes=[pltpu.SemaphoreType.DMA((7,)), pltpu.SemaphoreType.DMA((7,)),
                                          pltpu.SemaphoreType.DMA])(v)


def _cast_place(w, shard, name):
    r, c = w.shape
    tr = r // 4

    def body(s_ref, w_ref, o_ref):
        o_ref[0] = w_ref[...].astype(BF16)

    return pl.pallas_call(
        body, name=name, out_shape=_sds((N_SHARD, r, c), BF16),
        grid_spec=pltpu.PrefetchScalarGridSpec(
            num_scalar_prefetch=1, grid=(4,), in_specs=[pl.BlockSpec((tr, c), lambda i, s: (i, 0))],
            out_specs=pl.BlockSpec((1, tr, c), lambda i, s: (s[0], i, 0))),
        compiler_params=pltpu.CompilerParams(dimension_semantics=("arbitrary",)),
    )(shard.reshape(1).astype(jnp.int32), w)


def _mod_part(c_all, w_ada, b_ada, name):
    n = w_ada.shape[1]

    def body(c_ref, w_ref, b_ref, a_ref, p_ref):
        cv = c_ref[...]
        ca = cv * _sigmoid(cv)
        a_ref[...] = ca
        p_ref[...] = jnp.dot(ca, w_ref[...], precision=lax.Precision.HIGHEST, preferred_element_type=F32) + b_ref[...]

    full = lambda a: pl.BlockSpec(a.shape, lambda i: (0, 0))
    return _call(body, name=name, grid=(1,), in_specs=[full(c_all), full(w_ada), full(b_ada)],
                 out_specs=[pl.BlockSpec((8, D_MODEL), lambda i: (0, 0)), pl.BlockSpec((8, n), lambda i: (0, 0))],
                 out_shape=[_sds((8, D_MODEL)), _sds((8, n))])(c_all, w_ada, b_ada)


def _sum_received(g_own, land, name):
    r, c = g_own.shape
    tr = r // 4

    def body(g_ref, l_ref, o_ref):
        o_ref[...] = ((g_ref[...] + l_ref[0].astype(F32)) + l_ref[1].astype(F32)) + l_ref[2].astype(F32)

    return _call(body, name=name, grid=(4,),
                 in_specs=[_rows(tr, c), pl.BlockSpec((3, tr, c), lambda i: (0, i, 0))],
                 out_specs=_rows(tr, c), out_shape=_sds((r, c)))(g_own, land)


def _outer_sum(ct, dm, name):
    k, n = ct.shape[0], dm.shape[1]
    tr = k // 4

    def body(c_ref, d_ref, o_ref):
        cv = c_ref[...]
        dv = d_ref[...]
        acc = cv[:, 0:1] * dv[0:1, :]
        for i in range(1, 8):
            acc = acc + cv[:, i:i + 1] * dv[i:i + 1, :]
        o_ref[...] = acc

    return _call(body, name=name, grid=(4,), in_specs=[_rows(tr, 8), pl.BlockSpec((8, n), lambda i: (0, 0))],
                 out_specs=_rows(tr, n), out_shape=_sds((k, n)))(ct, dm)


def _small_grads(stats, lb_logits, name):
    def body(s_ref, lg_ref, g_ref, l_ref):
        tot = s_ref[0:1, :]
        for i in range(1, 8):
            tot = tot + s_ref[i:i + 1, :]
        part = lambda off, n: tot[:, off:off + n]
        l_ref[...] = jnp.zeros((1, 128), F32) + (0.5 / D_MODEL) * jnp.sum(part(ST_LOSS, D_MODEL))
        lg = lg_ref[...]
        lb = _sigmoid(lg[0:1] - lg[1:2])
        dl0 = part(ST_DLB, HG_WIDTH) * lb * (1.0 - lb)
        g_ref[:, SP_BADA:SP_BADA + 6 * D_MODEL] = part(ST_DMOD, 6 * D_MODEL)
        g_ref[:, SP_N1:SP_N1 + D_MODEL] = part(ST_DG1, D_MODEL)
        g_ref[:, SP_LB:SP_LB + HG_WIDTH] = dl0
        g_ref[:, SP_LB + HG_WIDTH:SP_LB + 2 * HG_WIDTH] = -dl0
        g_ref[:, SP_OG:SP_OG + HG_HEAD] = part(ST_DOG, HG_HEAD)
        g_ref[:, SP_AG:SP_AG + ATT_WIDTH] = part(ST_DAG, ATT_WIDTH)
        g_ref[:, SP_N2:SP_N2 + D_MODEL] = part(ST_DG2, D_MODEL)
        g_ref[:, SP_FG:SP_FG + D_MODEL] = part(ST_DFG, D_MODEL)

    return _call(body, name=name, grid=(1,),
                 in_specs=[pl.BlockSpec((8, ST_WIDTH), lambda i: (0, 0)), pl.BlockSpec((2, HG_WIDTH), lambda i: (0, 0))],
                 out_specs=[pl.BlockSpec((1, SP_WIDTH), lambda i: (0, 0)), pl.BlockSpec((1, 128), lambda i: (0, 0))],
                 out_shape=[_sds((1, SP_WIDTH)), _sds((1, 128))])(stats, lb_logits)


def _adamw(w, gs, m, v, name, steps=4):
    r, c = w.shape
    tr = r // steps
    ng = len(gs)

    def body(*refs):
        w_ref, g_refs, (m_ref, v_ref, g_out, d_out, m_out, v_out) = refs[0], refs[1:1 + ng], refs[1 + ng:]
        g = g_refs[0][...]
        for g_ref in g_refs[1:]:
            g = g + g_ref[...]
        m_new = ADAM_B1 * m_ref[...] + (1.0 - ADAM_B1) * g
        v_new = ADAM_B2 * v_ref[...] + (1.0 - ADAM_B2) * (g * g)
        m_hat = m_new / (1.0 - ADAM_B1 ** ADAM_STEP)
        v_hat = v_new / (1.0 - ADAM_B2 ** ADAM_STEP)
        g_out[...] = g
        d_out[...] = -ADAM_LR * (m_hat / (jnp.sqrt(v_hat) + ADAM_EPS) + ADAM_WD * w_ref[...])
        m_out[...] = m_new
        v_out[...] = v_new

    row = _rows(tr, c)
    return _call(body, name=name, grid=(steps,), in_specs=[row] * (3 + ng), out_specs=[row] * 4,
                 out_shape=[_sds((r, c))] * 4)(w, *gs, m, v)


def kernel(x, c, w_ada, b_ada, norm1_g, w_in, hg_lb_logits, hg_onorm_g, att_onorm_g, w_out, norm2_g, w_gate_up, w_down, final_g, loss_target, m_w_ada, m_b_ada, m_norm1_g, m_w_in, m_hg_lb_logits, m_hg_onorm_g, m_att_onorm_g, m_w_out, m_norm2_g, m_w_gate_up, m_w_down, m_final_g, v_w_ada, v_b_ada, v_norm1_g, v_w_in, v_hg_lb_logits, v_hg_onorm_g, v_att_onorm_g, v_w_out, v_norm2_g, v_w_gate_up, v_w_down, v_final_g):
    ix, iy, ic = _place()
    shard = 2 * ix + iy
    sample = 4 * ix + 2 * iy + ic
    n_ada = w_ada.shape[2]

    shards = [w_in[0], w_out[0], w_gate_up[0], w_down[0]]
    names = ["w_in", "w_out", "w_gu", "w_down"]
    shapes = [(N_SHARD,) + w.shape for w in shards]
    placed = [(_cast_place(w, shard, "place_" + nm),) for w, nm in zip(shards, names)]

    c_all = _gather_rows(c, "gather_c").reshape(8, D_MODEL)
    b_part = lax.dynamic_slice(b_ada, (0, shard * n_ada), (1, n_ada))
    c_act, part = _mod_part(c_all, w_ada[0], b_part, "mod_part")
    parts = _gather_rows(part, "gather_mod")[::2]
    mod = lax.dynamic_index_in_dim(parts, sample, axis=1, keepdims=False).reshape(1, 6 * D_MODEL)
    (first,), mod = _exchange_start(placed[:1], mod, True, "gather_start_w_in")
    gathering = {"w_in": first}

    def get_w(name, after):
        (full,) = _exchange_wait(gathering[name], after, True, "gather_wait_" + name)
        if name == "w_in":
            rest, full = _exchange_start(placed[1:], full, True, "gather_start_rest")
            gathering.update(zip(names[1:], rest))
        return full if name in ("w_in", "w_gu") else full.reshape(1, -1, D_MODEL)

    scattering = {}

    def put_g(name, g, g_bf16):
        shape = shapes[names.index(name)]
        land = lax.empty((3,) + shape[1:], BF16)
        (started,), g = _exchange_start([(g_bf16.reshape(shape), land)], g, False, "scatter_start_" + name)
        scattering[name] = (g.reshape(shape), started)

    dx, stats = _local_step(x[0], loss_target[0], mod, norm1_g, hg_lb_logits, hg_onorm_g, att_onorm_g,
                            norm2_g, final_g, get_w, put_g)

    def summed(name, after):
        g, started = scattering[name]
        _, land = _exchange_wait(started, after, False, "scatter_wait_" + name)
        own = lax.dynamic_index_in_dim(g, shard, axis=0, keepdims=False)
        return _sum_received(own, land, "sum_" + name)

    stats_all = _gather_rows(stats, "gather_stats").reshape(8, ST_WIDTH)
    g_small, loss = _small_grads(stats_all, hg_lb_logits, "small_grads")
    dmod = lax.dynamic_slice(stats_all, (0, ST_DMOD + shard * n_ada), (8, n_ada))
    g_ada = _outer_sum(c_act.T, dmod, "w_ada_grad")

    smalls = [(b_ada, m_b_ada, v_b_ada), (norm1_g, m_norm1_g, v_norm1_g),
              (hg_lb_logits, m_hg_lb_logits, v_hg_lb_logits), (hg_onorm_g, m_hg_onorm_g, v_hg_onorm_g),
              (att_onorm_g, m_att_onorm_g, v_att_onorm_g), (norm2_g, m_norm2_g, v_norm2_g),
              (final_g, m_final_g, v_final_g)]
    pack = lambda i: jnp.concatenate([t[i].reshape(1, -1) for t in smalls], axis=1)
    small_out = _adamw(pack(0), [g_small], pack(1), pack(2), "adamw_small", steps=1)
    offs = [SP_BADA, SP_N1, SP_LB, SP_OG, SP_AG, SP_N2, SP_FG, SP_WIDTH]
    unpack = lambda a: [a[0, offs[i]:offs[i + 1]].reshape(smalls[i][0].shape) for i in range(7)]
    sg, sd, sm, sv = [unpack(a) for a in small_out]

    ada = _adamw(w_ada[0], [g_ada], m_w_ada[0], v_w_ada[0], "adamw_w_ada")
    moments = [(m_w_in, v_w_in), (m_w_out, v_w_out), (m_w_gate_up, v_w_gate_up), (m_w_down, v_w_down)]

    def update(group, after, tag):
        sums = [summed(nm, after) for nm in group]
        other = _swap_sibling(sums, "swap_sums_" + tag)
        return {nm: _adamw(shards[names.index(nm)], [s, o], moments[names.index(nm)][0][0],
                           moments[names.index(nm)][1][0], "adamw_" + nm) for nm, s, o in zip(group, sums, other)}

    done = update(["w_down", "w_gu", "w_out"], ada[1], "late")
    done.update(update(["w_in"], done["w_out"][1], "in"))
    big = [ada] + [done[nm] for nm in names]
    bg, bd, bm, bv = [[t[i][None] for t in big] for i in range(4)]

    def order(b, s):
        return [b[0], s[0], s[1], b[1], s[2], s[3], s[4], b[2], s[5], b[3], b[4], s[6]]

    return (loss[0, 0], dx[None], *order(bg, sg), *order(bd, sd), *order(bm, sm), *order(bv, sv))
```

```python
import functools

import jax
import jax.numpy as jnp
from jax import lax
from jax.experimental import pallas as pl
from jax.experimental.pallas import tpu as pltpu

F32 = jnp.float32
BF16 = jnp.bfloat16
MESH = pl.DeviceIdType.MESH

D_MODEL = 1024
HG_WIDTH = 512
HG_HEAD = 128
HG_CHUNK = 64
ATT_WIDTH = 512
ATT_HEADS = 8
ATT_BLOCK = 128
DILATIONS = (1, 4, 16)
D_FF = 2816
IN_WIDTH = 3584
N_SHARD = 4
RMS_EPS = 1e-6
NEG = -1e30

ADAM_LR = 0.001
ADAM_B1 = 0.9
ADAM_B2 = 0.999
ADAM_EPS = 1e-08
ADAM_WD = 0.01
ADAM_STEP = 10

VMEM_LIMIT = 56 * 2**20

ST_LOSS, ST_DFG, ST_DG2, ST_DG1 = 0, 1024, 2048, 3072
ST_DLB, ST_DAG, ST_DOG, ST_DMOD = 4096, 4608, 5120, 5248
ST_WIDTH = 5248 + 6144
SP_BADA, SP_N1, SP_LB, SP_OG, SP_AG, SP_N2, SP_FG = 0, 6144, 7168, 8192, 8320, 8832, 9856
SP_WIDTH = 10880


def _call(body, *, name, grid, in_specs, out_specs, out_shape, scratch_shapes=()):
    return pl.pallas_call(
        body, name=name, grid=grid, in_specs=in_specs, out_specs=out_specs, out_shape=out_shape,
        scratch_shapes=list(scratch_shapes),
        compiler_params=pltpu.CompilerParams(
            dimension_semantics=("arbitrary",) * len(grid), vmem_limit_bytes=VMEM_LIMIT))


def _sds(shape, dtype=F32):
    return jax.ShapeDtypeStruct(shape, dtype)


def _dot(a, b):
    return jnp.dot(a, b, preferred_element_type=F32)


def _dot_nt(a, b):
    return lax.dot_general(a, b, (((1,), (1,)), ((), ())), preferred_element_type=F32)


def _dot_tn(a, b):
    return lax.dot_general(a, b, (((0,), (0,)), ((), ())), preferred_element_type=F32)


def _sigmoid(x):
    return 1.0 / (1.0 + jnp.exp(-x))


def _rows(tr, width):
    return pl.BlockSpec((tr, width), lambda i: (i, 0))


def _vec(width):
    return pl.BlockSpec((1, width), lambda i: (0, 0))


def _acc(ref, val, first):
    @pl.when(first)
    def _():
        ref[...] = val

    @pl.when(jnp.logical_not(first))
    def _():
        ref[...] += val


def _mm_nn(a, b3, name, tm=512):
    m, k = a.shape
    s, _, n = b3.shape

    def body(a_ref, b_ref, o_ref):
        o_ref[...] = _dot(a_ref[...], b_ref[0])

    return _call(
        body, name=name, grid=(s, m // tm),
        in_specs=[pl.BlockSpec((tm, k), lambda j, i: (i, 0)), pl.BlockSpec((1, k, n), lambda j, i: (j, 0, 0))],
        out_specs=pl.BlockSpec((tm, n), lambda j, i: (i, j)), out_shape=_sds((m, s * n)))(a, b3)


def _mm_nt(dy, b3, name, tm=512):
    m = dy.shape[0]
    s, k, n = b3.shape

    def body(dy_ref, b_ref, o_ref):
        _acc(o_ref, _dot_nt(dy_ref[...], b_ref[0]), pl.program_id(1) == 0)

    return _call(
        body, name=name, grid=(m // tm, s),
        in_specs=[pl.BlockSpec((tm, n), lambda i, j: (i, j)), pl.BlockSpec((1, k, n), lambda i, j: (j, 0, 0))],
        out_specs=pl.BlockSpec((tm, k), lambda i, j: (i, 0)), out_shape=_sds((m, k)))(dy, b3)


def _mm_tn(a, dy, s, name, tm=1024, tk=None):
    m, k = a.shape
    n = dy.shape[1] // s
    tk = k if tk is None else tk
    steps = m // tm

    def body(a_ref, dy_ref, o_ref, ob_ref):
        i = pl.program_id(2)
        _acc(o_ref, _dot_tn(a_ref[...], dy_ref[...])[None], i == 0)

        @pl.when(i == steps - 1)
        def _():
            ob_ref[...] = o_ref[...].astype(BF16)

    out = pl.BlockSpec((1, tk, n), lambda kk, j, i: (j, kk, 0))
    return _call(
        body, name=name, grid=(k // tk, s, steps),
        in_specs=[pl.BlockSpec((tm, tk), lambda kk, j, i: (i, kk)), pl.BlockSpec((tm, n), lambda kk, j, i: (i, j))],
        out_specs=[out, out], out_shape=[_sds((s, k, n)), _sds((s, k, n), BF16)])(a, dy)


def _rms(x):
    return lax.rsqrt(jnp.mean(x * x, axis=-1, keepdims=True) + RMS_EPS)


def _rms_bwd(dxh, xh, r):
    return r * (dxh - xh * jnp.mean(dxh * xh, axis=-1, keepdims=True))


def _norm_mod(x, g, scale, shift, name, tr=512):
    t = x.shape[0]

    def body(x_ref, g_ref, sc_ref, sh_ref, h_ref):
        xv = x_ref[...]
        n = xv * _rms(xv) * g_ref[...]
        h_ref[...] = (n * (1.0 + sc_ref[...]) + sh_ref[...]).astype(BF16)

    return _call(body, name=name, grid=(t // tr,),
                 in_specs=[_rows(tr, D_MODEL), _vec(D_MODEL), _vec(D_MODEL), _vec(D_MODEL)],
                 out_specs=_rows(tr, D_MODEL), out_shape=_sds((t, D_MODEL), BF16))(x, g, scale, shift)


def _mix_in(o_hg, proj, att, og, ag, name, tr=256):
    t = o_hg.shape[0]

    def body(o_ref, g_ref, a_ref, og_ref, ag_ref, m_ref):
        for h in range(HG_WIDTH // HG_HEAD):
            sl = slice(h * HG_HEAD, (h + 1) * HG_HEAD)
            oh = o_ref[:, sl]
            gv = g_ref[:, sl]
            m_ref[:, sl] = (oh * _rms(oh) * og_ref[...] * (gv * _sigmoid(gv))).astype(BF16)
        av = a_ref[...]
        m_ref[:, HG_WIDTH:] = (av * _rms(av) * ag_ref[...]).astype(BF16)

    return _call(body, name=name, grid=(t // tr,),
                 in_specs=[_rows(tr, HG_WIDTH), pl.BlockSpec((tr, HG_WIDTH), lambda i: (i, 3)), _rows(tr, ATT_WIDTH),
                           _vec(HG_HEAD), _vec(ATT_WIDTH)],
                 out_specs=_rows(tr, D_MODEL), out_shape=_sds((t, D_MODEL), BF16))(o_hg, proj, att, og, ag)


def _resid_norm_mod(x, mix, gate, g, scale, shift, name, tr=256):
    t = x.shape[0]

    def body(x_ref, m_ref, gt_ref, g_ref, sc_ref, sh_ref, x2_ref, h_ref):
        x2 = x_ref[...] + gt_ref[...] * m_ref[...]
        x2_ref[...] = x2
        n = x2 * _rms(x2) * g_ref[...]
        h_ref[...] = (n * (1.0 + sc_ref[...]) + sh_ref[...]).astype(BF16)

    return _call(body, name=name, grid=(t // tr,),
                 in_specs=[_rows(tr, D_MODEL), _rows(tr, D_MODEL)] + [_vec(D_MODEL)] * 4,
                 out_specs=[_rows(tr, D_MODEL), _rows(tr, D_MODEL)],
                 out_shape=[_sds((t, D_MODEL)), _sds((t, D_MODEL), BF16)])(x, mix, gate, g, scale, shift)


def _mm_gate_up(h, w_gu, name, tm=512):
    m, k = h.shape
    n = w_gu.shape[2]

    def body(h_ref, wa_ref, wu_ref, a_ref, u_ref, o_ref):
        hv = h_ref[...]
        a = _dot(hv, wa_ref[0])
        u = _dot(hv, wu_ref[0])
        a_ref[...] = a.astype(BF16)
        u_ref[...] = u.astype(BF16)
        o_ref[...] = (a * _sigmoid(a) * u).astype(BF16)

    out = pl.BlockSpec((tm, n), lambda j, i: (i, j))
    return _call(body, name=name, grid=(2, m // tm),
                 in_specs=[pl.BlockSpec((tm, k), lambda j, i: (i, 0)), pl.BlockSpec((1, k, n), lambda j, i: (j, 0, 0)),
                           pl.BlockSpec((1, k, n), lambda j, i: (j + 2, 0, 0))],
                 out_specs=[out, out, out], out_shape=[_sds((m, 2 * n), BF16)] * 3)(h, w_gu, w_gu)


def _mm_down_dx(dffn, w_down, a, u, name, tm=512):
    m = dffn.shape[0]
    _, k, n = w_down.shape

    def body(d_ref, w_ref, a_ref, u_ref, o_ref):
        dact = _dot_nt(d_ref[...], w_ref[0])
        av = a_ref[...].astype(F32)
        sg = _sigmoid(av)
        o_ref[:, :k] = (dact * u_ref[...].astype(F32) * sg * (1.0 + av * (1.0 - sg))).astype(BF16)
        o_ref[:, k:] = (dact * av * sg).astype(BF16)

    return _call(body, name=name, grid=(m // tm,),
                 in_specs=[_rows(tm, n), pl.BlockSpec((1, k, n), lambda i: (0, 0, 0)), _rows(tm, k), _rows(tm, k)],
                 out_specs=_rows(tm, 2 * k), out_shape=_sds((m, 2 * k), BF16))(dffn, w_down, a, u)


def _final_loss(x2, ffn, gate, fg, tgt, name, tr=256):
    t = x2.shape[0]

    def body(x_ref, f_ref, gt_ref, fg_ref, t_ref, dx_ref, df_ref, l_ref, dfg_ref, dgt_ref):
        first = pl.program_id(0) == 0
        ffn_v = f_ref[...]
        x3 = x_ref[...] + gt_ref[...] * ffn_v
        r = _rms(x3)
        xh = x3 * r
        err = xh * fg_ref[...] - t_ref[...]
        dy = err * (1.0 / D_MODEL)
        dx3 = _rms_bwd(dy * fg_ref[...], xh, r)
        dx_ref[...] = dx3
        df_ref[...] = (dx3 * gt_ref[...]).astype(BF16)
        _acc(l_ref, jnp.sum(err * err, axis=0, keepdims=True), first)
        _acc(dfg_ref, jnp.sum(dy * xh, axis=0, keepdims=True), first)
        _acc(dgt_ref, jnp.sum(dx3 * ffn_v, axis=0, keepdims=True), first)

    row, vec = _rows(tr, D_MODEL), _vec(D_MODEL)
    return _call(body, name=name, grid=(t // tr,), in_specs=[row, row, vec, vec, row],
                 out_specs=[row, row, vec, vec, vec],
                 out_shape=[_sds((t, D_MODEL)), _sds((t, D_MODEL), BF16)] + [_sds((1, D_MODEL))] * 3)(
                     x2, ffn, gate, fg, tgt)


def _norm_mod_bwd(dh, x, g, scale, dres, name, gate=None, mix=None, tr=256):
    t = x.shape[0]
    below = gate is not None

    def body(*refs):
        if below:
            dh_ref, x_ref, g_ref, sc_ref, dr_ref, gt_ref, m_ref, dx_ref, dsh_ref, dsc_ref, dg_ref, dgt_ref, dm_ref = refs
        else:
            dh_ref, x_ref, g_ref, sc_ref, dr_ref, dx_ref, dsh_ref, dsc_ref, dg_ref = refs
        first = pl.program_id(0) == 0
        xv = x_ref[...]
        dhv = dh_ref[...]
        r = _rms(xv)
        xh = xv * r
        dn = dhv * (1.0 + sc_ref[...])
        dx = dr_ref[...] + _rms_bwd(dn * g_ref[...], xh, r)
        dx_ref[...] = dx
        _acc(dsh_ref, jnp.sum(dhv, axis=0, keepdims=True), first)
        _acc(dsc_ref, jnp.sum(dhv * xh * g_ref[...], axis=0, keepdims=True), first)
        _acc(dg_ref, jnp.sum(dn * xh, axis=0, keepdims=True), first)
        if below:
            _acc(dgt_ref, jnp.sum(dx * m_ref[...], axis=0, keepdims=True), first)
            dm_ref[...] = (dx * gt_ref[...]).astype(BF16)

    row, vec = _rows(tr, D_MODEL), _vec(D_MODEL)
    in_specs = [row, row, vec, vec, row] + ([vec, row] if below else [])
    out_specs = [row, vec, vec, vec] + ([vec, row] if below else [])
    out_shape = [_sds((t, D_MODEL))] + [_sds((1, D_MODEL))] * 3 + ([_sds((1, D_MODEL)), _sds((t, D_MODEL), BF16)] if below else [])
    args = (dh, x, g, scale, dres) + ((gate, mix) if below else ())
    return _call(body, name=name, grid=(t // tr,), in_specs=in_specs, out_specs=out_specs, out_shape=out_shape)(*args)


def _mix_in_bwd(dmi, o_hg, proj, att, og, ag, name, tr=256):
    t = o_hg.shape[0]

    def body(d_ref, o_ref, g_ref, a_ref, og_ref, ag_ref, do_ref, dg_ref, da_ref, dd_ref, dog_ref, dag_ref):
        first = pl.program_id(0) == 0
        dog = jnp.zeros((1, HG_HEAD), F32)
        for h in range(HG_WIDTH // HG_HEAD):
            sl = slice(h * HG_HEAD, (h + 1) * HG_HEAD)
            oh = o_ref[:, sl]
            gv = g_ref[:, sl]
            dv = d_ref[:, sl]
            r = _rms(oh)
            xh = oh * r
            sg = _sigmoid(gv)
            dno = dv * gv * sg
            dg_ref[:, sl] = dv * xh * og_ref[...] * sg * (1.0 + gv * (1.0 - sg))
            dog = dog + jnp.sum(dno * xh, axis=0, keepdims=True)
            do_ref[:, sl] = _rms_bwd(dno * og_ref[...], xh, r)
        _acc(dog_ref, dog, first)
        av = a_ref[...]
        dav = d_ref[:, HG_WIDTH:]
        r = _rms(av)
        xa = av * r
        _acc(dag_ref, jnp.sum(dav * xa, axis=0, keepdims=True), first)
        datt = _rms_bwd(dav * ag_ref[...], xa, r)
        da_ref[...] = datt
        prod = datt * av
        lane = lax.broadcasted_iota(jnp.int32, (1, 128), 1)
        dd = jnp.zeros((tr, 128), F32)
        for hp in range(ATT_HEADS // 2):
            pp = prod[:, hp * 128:(hp + 1) * 128]
            lo = jnp.sum(jnp.where(lane < 64, pp, 0.0), axis=-1, keepdims=True)
            hi = jnp.sum(jnp.where(lane >= 64, pp, 0.0), axis=-1, keepdims=True)
            dd = jnp.where(lane == 2 * hp, lo, dd)
            dd = jnp.where(lane == 2 * hp + 1, hi, dd)
        dd_ref[...] = dd

    half = _rows(tr, HG_WIDTH)
    return _call(body, name=name, grid=(t // tr,),
                 in_specs=[_rows(tr, D_MODEL), half, pl.BlockSpec((tr, HG_WIDTH), lambda i: (i, 3)), half,
                           _vec(HG_HEAD), _vec(ATT_WIDTH)],
                 out_specs=[half, half, half, _rows(tr, 128), _vec(HG_HEAD), _vec(ATT_WIDTH)],
                 out_shape=[_sds((t, HG_WIDTH))] * 3 + [_sds((t, 128)), _sds((1, HG_HEAD)), _sds((1, ATT_WIDTH))])(
                     dmi, o_hg, proj, att, og, ag)


def _dproj(dhg, dg, dqs, dkvs, name, tr=256):
    t = dhg.shape[0]
    w3 = 3 * HG_WIDTH
    w4 = w3 + HG_WIDTH
    nbr = len(dqs)

    def body(*refs):
        h_ref, g_ref, q_refs, kv_refs, o_ref = refs[0], refs[1], refs[2:2 + nbr], refs[2 + nbr:2 + 2 * nbr], refs[-1]
        o_ref[:, :w3] = h_ref[...].astype(BF16)
        o_ref[:, w3:w4] = g_ref[...].astype(BF16)
        o_ref[:, w4:w4 + ATT_WIDTH] = sum(r[...] for r in q_refs).astype(BF16)
        o_ref[:, w4 + ATT_WIDTH:] = sum(r[...] for r in kv_refs).astype(BF16)

    return _call(body, name=name, grid=(t // tr,),
                 in_specs=[_rows(tr, w3), _rows(tr, HG_WIDTH)] + [_rows(tr, ATT_WIDTH)] * nbr
                 + [_rows(tr, 2 * ATT_WIDTH)] * nbr,
                 out_specs=_rows(tr, IN_WIDTH), out_shape=_sds((t, IN_WIDTH), BF16))(dhg, dg, *dqs, *dkvs)


def _hg_gates(f_raw, q_raw, lb, tri):
    sg = _sigmoid(f_raw)
    f = lb + (1.0 - lb) * sg
    k = 1.0 - f
    b = jnp.dot(tri, jnp.log(f), precision=lax.Precision.HIGHEST, preferred_element_type=F32)
    sq = _sigmoid(q_raw)
    return sg, f, k, b, sq


def _hgrn_fwd(proj, lb_logits, name):
    t = proj.shape[0]
    nc = t // HG_CHUNK
    nh = HG_WIDTH // HG_HEAD

    def body(q_ref, f_ref, i_ref, lg_ref, o_ref, st_ref, s_scr):
        @pl.when(pl.program_id(0) == 0)
        def _():
            s_scr[...] = jnp.zeros_like(s_scr)

        lg = lg_ref[...]
        lb_all = _sigmoid(lg[0:1] - lg[1:2])
        row = lax.broadcasted_iota(jnp.int32, (HG_CHUNK, HG_CHUNK), 0)
        col = lax.broadcasted_iota(jnp.int32, (HG_CHUNK, HG_CHUNK), 1)
        causal = row >= col
        tri = causal.astype(F32)
        for h in range(nh):
            sl = slice(h * HG_HEAD, (h + 1) * HG_HEAD)
            q_raw = q_ref[:, sl]
            _, _, k, b, sq = _hg_gates(f_ref[:, sl], q_raw, lb_all[:, sl], tri)
            v = i_ref[:, sl].astype(BF16)
            gl = b[HG_CHUNK - 1:HG_CHUNK]
            qd = (q_raw * sq * jnp.exp(b)).astype(BF16)
            kd = (k * jnp.exp(-b)).astype(BF16)
            ke = (k * jnp.exp(gl - b)).astype(BF16)
            st = s_scr[h]
            st_ref[0, sl, :] = st
            a = jnp.where(causal, _dot_nt(qd, kd), 0.0).astype(BF16)
            o_ref[:, sl] = _dot_nt(qd, st.astype(BF16)) + _dot(a, v)
            s_scr[h] = st * jnp.exp(gl) + _dot_tn(v, ke)

    blk = lambda j: pl.BlockSpec((HG_CHUNK, HG_WIDTH), lambda c: (c, j))
    return _call(body, name=name, grid=(nc,),
                 in_specs=[blk(0), blk(1), blk(2), pl.BlockSpec((2, HG_WIDTH), lambda c: (0, 0))],
                 out_specs=[blk(0), pl.BlockSpec((1, HG_WIDTH, HG_HEAD), lambda c: (c, 0, 0))],
                 out_shape=[_sds((t, HG_WIDTH)), _sds((nc, HG_WIDTH, HG_HEAD))],
                 scratch_shapes=[pltpu.VMEM((nh, HG_HEAD, HG_HEAD), F32)])(proj, proj, proj, lb_logits)


def _hgrn_bwd(proj, lb_logits, states, do, name):
    t = proj.shape[0]
    nc = t // HG_CHUNK
    nh = HG_WIDTH // HG_HEAD

    def body(q_ref, f_ref, i_ref, lg_ref, st_ref, do_ref, d_ref, dlb_ref, ds_scr):
        first = pl.program_id(0) == 0

        @pl.when(first)
        def _():
            ds_scr[...] = jnp.zeros_like(ds_scr)

        lg = lg_ref[...]
        lb_all = _sigmoid(lg[0:1] - lg[1:2])
        row = lax.broadcasted_iota(jnp.int32, (HG_CHUNK, HG_CHUNK), 0)
        col = lax.broadcasted_iota(jnp.int32, (HG_CHUNK, HG_CHUNK), 1)
        causal = row >= col
        tri = causal.astype(F32)
        tri_t = (row <= col).astype(F32)
        dlb = []
        for h in range(nh):
            sl = slice(h * HG_HEAD, (h + 1) * HG_HEAD)
            q_raw = q_ref[:, sl]
            lb = lb_all[:, sl]
            sg, f, k, b, sq = _hg_gates(f_ref[:, sl], q_raw, lb, tri)
            v = i_ref[:, sl].astype(BF16)
            gl = b[HG_CHUNK - 1:HG_CHUNK]
            egl = jnp.exp(gl)
            eb = jnp.exp(b)
            enb = jnp.exp(-b)
            egb = jnp.exp(gl - b)
            qd = q_raw * sq * eb
            kd = k * enb
            ke = k * egb
            qd_b, kd_b, ke_b = qd.astype(BF16), kd.astype(BF16), ke.astype(BF16)
            st = st_ref[0, sl, :]
            dst = ds_scr[h]
            dst_b = dst.astype(BF16)
            dov = do_ref[:, sl].astype(BF16)
            a = jnp.where(causal, _dot_nt(qd_b, kd_b), 0.0).astype(BF16)
            da = jnp.where(causal, _dot_nt(dov, v), 0.0).astype(BF16)
            dqd = _dot(dov, st.astype(BF16)) + _dot(da, kd_b)
            dkd = _dot_tn(da, qd_b)
            dv = _dot_tn(a, dov) + _dot_nt(ke_b, dst_b)
            dke = _dot(v, dst_b)
            dgl = jnp.sum(dst * st, axis=0, keepdims=True) * egl
            ds_scr[h] = _dot_tn(dov, qd_b) + dst * egl
            t1 = dke * ke
            db = dqd * qd_b.astype(F32) - dkd * kd_b.astype(F32) - t1
            dgl = dgl + jnp.sum(t1, axis=0, keepdims=True)
            dlf = jnp.dot(tri_t, db, precision=lax.Precision.HIGHEST, preferred_element_type=F32) + dgl
            df = dlf / f - (dkd * enb + dke * egb)
            d_ref[:, sl] = dqd * eb * sq * (1.0 + q_raw * (1.0 - sq))
            d_ref[:, HG_WIDTH + h * HG_HEAD:HG_WIDTH + (h + 1) * HG_HEAD] = df * (1.0 - lb) * sg * (1.0 - sg)
            d_ref[:, 2 * HG_WIDTH + h * HG_HEAD:2 * HG_WIDTH + (h + 1) * HG_HEAD] = dv
            dlb.append(jnp.sum(df * (1.0 - sg), axis=0, keepdims=True))
        _acc(dlb_ref, jnp.concatenate(dlb, axis=1), first)

    rev = lambda j: pl.BlockSpec((HG_CHUNK, HG_WIDTH), lambda c: (nc - 1 - c, j))
    return _call(body, name=name, grid=(nc,),
                 in_specs=[rev(0), rev(1), rev(2), pl.BlockSpec((2, HG_WIDTH), lambda c: (0, 0)),
                           pl.BlockSpec((1, HG_WIDTH, HG_HEAD), lambda c: (nc - 1 - c, 0, 0)), rev(0)],
                 out_specs=[pl.BlockSpec((HG_CHUNK, 3 * HG_WIDTH), lambda c: (nc - 1 - c, 0)), _vec(HG_WIDTH)],
                 out_shape=[_sds((t, 3 * HG_WIDTH)), _sds((1, HG_WIDTH))],
                 scratch_shapes=[pltpu.VMEM((nh, HG_HEAD, HG_HEAD), F32)])(proj, proj, proj, lb_logits, states, do)


def _to_sub(a, dil):
    t, w = a.shape
    return a if dil == 1 else a.reshape(t // dil, dil, w).transpose(1, 0, 2).reshape(t, w)


def _from_sub(a, dil):
    t, w = a.shape
    return a if dil == 1 else a.reshape(dil, t // dil, w).transpose(1, 0, 2).reshape(t, w)


def _att_mask(has_prev):
    qi = lax.broadcasted_iota(jnp.int32, (2 * ATT_BLOCK, 2 * ATT_BLOCK), 0) % ATT_BLOCK
    kj = lax.broadcasted_iota(jnp.int32, (2 * ATT_BLOCK, 2 * ATT_BLOCK), 1)
    prev = jnp.logical_and(jnp.logical_and(kj < ATT_BLOCK, kj >= qi), has_prev)
    cur = jnp.logical_and(kj >= ATT_BLOCK, kj - ATT_BLOCK <= qi)
    return jnp.logical_or(prev, cur), lax.broadcasted_iota(jnp.int32, (1, 128), 1)


def _attn_fwd(qkv, dil, name):
    t = qkv.shape[0]
    nb = t // ATT_BLOCK
    bps = nb // dil

    def body(q_ref, kc_ref, kp_ref, vc_ref, vp_ref, o_ref, l_ref):
        mask, lane = _att_mask((pl.program_id(0) % bps) != 0)
        lo = lane < 64
        nq = ATT_BLOCK
        lse_all = jnp.zeros((nq, 128), F32)
        for hp in range(ATT_HEADS // 2):
            sl = slice(hp * 128, (hp + 1) * 128)
            q2 = q_ref[:, sl]
            zero = jnp.zeros_like(q2)
            qs = jnp.concatenate([jnp.where(lo, q2, zero), jnp.where(lo, zero, q2)], axis=0)
            kk = jnp.concatenate([kp_ref[:, sl], kc_ref[:, sl]], axis=0)
            vv = jnp.concatenate([vp_ref[:, sl], vc_ref[:, sl]], axis=0)
            s = jnp.where(mask, _dot_nt(qs, kk) * 0.125, NEG)
            mx = jnp.max(s, axis=-1, keepdims=True)
            p = jnp.exp(s - mx)
            l = jnp.sum(p, axis=-1, keepdims=True)
            o = _dot(p.astype(BF16), vv) * (1.0 / l)
            o_ref[:, sl] = jnp.where(lo, o[:nq], o[nq:])
            lse = mx + jnp.log(l)
            lse_all = jnp.where(lane == 2 * hp, lse[:nq], lse_all)
            lse_all = jnp.where(lane == 2 * hp + 1, lse[nq:], lse_all)
        l_ref[...] = lse_all

    blk = lambda j, back: pl.BlockSpec((ATT_BLOCK, ATT_WIDTH), lambda n: (jnp.maximum(n - back, 0), j))
    return _call(body, name=name, grid=(nb,),
                 in_specs=[blk(0, 0), blk(1, 0), blk(1, 1), blk(2, 0), blk(2, 1)],
                 out_specs=[pl.BlockSpec((ATT_BLOCK, ATT_WIDTH), lambda n: (n, 0)),
                            pl.BlockSpec((ATT_BLOCK, 128), lambda n: (n, 0))],
                 out_shape=[_sds((t, ATT_WIDTH)), _sds((t, 128))])(qkv, qkv, qkv, qkv, qkv)


def _attn_combine(os_, ls_, name, tr=256):
    t = os_[0].shape[0]
    nbr = len(os_)

    def body(*refs):
        o_refs, l_refs, (a_ref, lt_ref) = refs[:nbr], refs[nbr:2 * nbr], refs[2 * nbr:]
        lane = lax.broadcasted_iota(jnp.int32, (1, 128), 1)
        ls = [r[...] for r in l_refs]
        mx = functools.reduce(jnp.maximum, ls)
        tot = mx + jnp.log(sum(jnp.exp(l - mx) for l in ls))
        lt_ref[...] = tot
        ws = [jnp.exp(l - tot) for l in ls]
        for hp in range(ATT_HEADS // 2):
            sl = slice(hp * 128, (hp + 1) * 128)
            acc = jnp.zeros((tr, 128), F32)
            for w, o_ref in zip(ws, o_refs):
                wf = jnp.where(lane < 64, w[:, 2 * hp:2 * hp + 1], w[:, 2 * hp + 1:2 * hp + 2])
                acc = acc + wf * o_ref[:, sl]
            a_ref[:, sl] = acc

    return _call(body, name=name, grid=(t // tr,),
                 in_specs=[_rows(tr, ATT_WIDTH)] * nbr + [_rows(tr, 128)] * nbr,
                 out_specs=[_rows(tr, ATT_WIDTH), _rows(tr, 128)],
                 out_shape=[_sds((t, ATT_WIDTH)), _sds((t, 128))])(*os_, *ls_)


def _attn_bwd(qkv, dout, lse, dd, dil, name):
    t = qkv.shape[0]
    nb = t // ATT_BLOCK
    bps = nb // dil

    w = ATT_WIDTH
    nq = ATT_BLOCK

    def body(q_ref, kc_ref, kp_ref, vc_ref, vp_ref, do_ref, l_ref, d_ref, dq_ref, dkv_ref, carry):
        n = pl.program_id(0)

        @pl.when(n == 0)
        def _():
            carry[...] = jnp.zeros_like(carry)

        @pl.when(n < nb)
        def _():
            mask, lane = _att_mask((n % bps) != 0)
            lo = lane < 64
            for hp in range(ATT_HEADS // 2):
                sl = slice(hp * 128, (hp + 1) * 128)
                sv = slice(w + hp * 128, w + (hp + 1) * 128)
                q2, do2 = q_ref[:, sl], do_ref[:, sl]
                zero = jnp.zeros_like(q2)
                qs = jnp.concatenate([jnp.where(lo, q2, zero), jnp.where(lo, zero, q2)], axis=0)
                dos = jnp.concatenate([jnp.where(lo, do2, zero), jnp.where(lo, zero, do2)], axis=0)
                kk = jnp.concatenate([kp_ref[:, sl], kc_ref[:, sl]], axis=0)
                vv = jnp.concatenate([vp_ref[:, sl], vc_ref[:, sl]], axis=0)
                ls = jnp.concatenate([l_ref[:, 2 * hp:2 * hp + 1], l_ref[:, 2 * hp + 1:2 * hp + 2]], axis=0)
                dh = jnp.concatenate([d_ref[:, 2 * hp:2 * hp + 1], d_ref[:, 2 * hp + 1:2 * hp + 2]], axis=0)
                p = jnp.exp(jnp.where(mask, _dot_nt(qs, kk) * 0.125 - ls, NEG))
                ds = (p * (_dot_nt(dos, vv) - dh)).astype(BF16)
                dq = _dot(ds, kk) * 0.125
                dq_ref[:, sl] = jnp.where(lo, dq[:nq], dq[nq:])
                dk = _dot_tn(ds, qs) * 0.125
                dv = _dot_tn(p.astype(BF16), dos)
                dkv_ref[:, sl] = carry[:, sl] + dk[:nq]
                dkv_ref[:, sv] = carry[:, sv] + dv[:nq]
                carry[:, sl] = dk[nq:]
                carry[:, sv] = dv[nq:]

        @pl.when(n == nb)
        def _():
            dkv_ref[...] = carry[...]

    def blk(width, j, back):
        return pl.BlockSpec((nq, width), lambda n: (jnp.clip(n - back, 0, nb - 1), j))

    return _call(body, name=name, grid=(nb + 1,),
                 in_specs=[blk(w, 0, 0), blk(w, 1, 0), blk(w, 1, 1), blk(w, 2, 0), blk(w, 2, 1),
                           blk(w, 0, 0), blk(128, 0, 0), blk(128, 0, 0)],
                 out_specs=[blk(w, 0, 0), blk(2 * w, 0, 1)],
                 out_shape=[_sds((t, w)), _sds((t, 2 * w))],
                 scratch_shapes=[pltpu.VMEM((nq, 2 * w), F32)])(qkv, qkv, qkv, qkv, qkv, dout, lse, dd)


def _local_step(x, tgt, mod, norm1_g, lb_logits, og, ag, norm2_g, fg, get_w, put_g):
    shift1, scale1, gate1, shift2, scale2, gate2 = [mod[:, i * D_MODEL:(i + 1) * D_MODEL] for i in range(6)]
    fg = fg.reshape(1, D_MODEL)

    h1 = _norm_mod(x, norm1_g, scale1, shift1, "norm_mod1")
    w_in = get_w("w_in", h1)
    proj = _mm_nn(h1, w_in, "mm_in")
    o_hg, states = _hgrn_fwd(proj, lb_logits, "hgrn_fwd")
    qkv = proj[:, 4 * HG_WIDTH:].astype(BF16)
    qkvs = [_to_sub(qkv, d) for d in DILATIONS]
    outs = [_attn_fwd(q, d, f"attn_fwd{d}") for q, d in zip(qkvs, DILATIONS)]
    att, lse = _attn_combine([_from_sub(o, d) for (o, _), d in zip(outs, DILATIONS)],
                             [_from_sub(l, d) for (_, l), d in zip(outs, DILATIONS)], "attn_combine")
    mixin = _mix_in(o_hg, proj, att, og, ag, "mix_in")
    w_out = get_w("w_out", mixin)
    mix = _mm_nn(mixin, w_out, "mm_out")
    x2, h2 = _resid_norm_mod(x, mix, gate1, norm2_g, scale2, shift2, "resid_norm_mod2")
    w_gu = get_w("w_gu", h2)
    a_ff, u_ff, act = _mm_gate_up(h2, w_gu, "mm_gu")
    w_down = get_w("w_down", act)
    ffn = _mm_nn(act, w_down, "mm_down")
    dx3, dffn, loss_v, dfg, dgate2 = _final_loss(x2, ffn, gate2, fg, tgt, "final_loss")

    put_g("w_down", *_mm_tn(act, dffn, 1, "mm_down_dw", tk=D_FF // 2))
    dau = _mm_down_dx(dffn, w_down, a_ff, u_ff, "mm_down_dx")
    put_g("w_gu", *_mm_tn(h2, dau, N_SHARD, "mm_gu_dw"))
    dh2 = _mm_nt(dau, w_gu, "mm_gu_dx", tm=1024)
    dx2, dshift2, dscale2, dg2, dgate1, dmix = _norm_mod_bwd(
        dh2, x2, norm2_g, scale2, dx3, "norm_mod2_bwd", gate=gate1, mix=mix)
    put_g("w_out", *_mm_tn(mixin, dmix, 1, "mm_out_dw"))
    dmixin = _mm_nt(dmix, w_out, "mm_out_dx", tm=1024)
    do_hg, dg_raw, datt, dd, dog, dag = _mix_in_bwd(dmixin, o_hg, proj, att, og, ag, "mix_in_bwd")
    datt_b = datt.astype(BF16)
    datts = [_attn_bwd(q, _to_sub(datt_b, d), _to_sub(lse, d), _to_sub(dd, d), d, f"attn_bwd{d}")
             for q, d in zip(qkvs, DILATIONS)]
    dhg, dlb = _hgrn_bwd(proj, lb_logits, states, do_hg, "hgrn_bwd")
    dproj = _dproj(dhg, dg_raw, [_from_sub(dq, d) for (dq, _), d in zip(datts, DILATIONS)],
                   [_from_sub(dkv, d) for (_, dkv), d in zip(datts, DILATIONS)], "dproj")
    put_g("w_in", *_mm_tn(h1, dproj, N_SHARD, "mm_in_dw"))
    dh1 = _mm_nt(dproj, w_in, "mm_in_dx", tm=1024)
    dx, dshift1, dscale1, dg1 = _norm_mod_bwd(dh1, x, norm1_g, scale1, dx2, "norm_mod1_bwd")

    stats = jnp.concatenate([loss_v, dfg, dg2, dg1, dlb, dag, dog,
                             dshift1, dscale1, dgate1, dshift2, dscale2, dgate2], axis=1)
    return dx, stats


def _place():
    x, y, c = lax.axis_index("x"), lax.axis_index("y"), lax.axis_index("c")
    return x, y, c


def _chip_peers(x, y, c):
    return [(1 - x, y, c), (x, 1 - y, c), (1 - x, 1 - y, c)]


def _comm_call(body, name, n_in, out_shape, scratch_shapes):
    hbm = pl.BlockSpec(memory_space=pl.ANY)
    return pl.pallas_call(body, name=name, in_specs=[hbm] * n_in, out_specs=[hbm] * len(out_shape),
                          out_shape=out_shape, scratch_shapes=scratch_shapes)


_HBM = pl.BlockSpec(memory_space=pltpu.HBM)
_SEM = pl.BlockSpec(memory_space=pltpu.SEMAPHORE)
_EFFECT = pltpu.SideEffectType.DATAFLOW_SIDE_EFFECTING


def _exchange_copy(bufs, send, recv, j, peer, place, gather):
    if gather:
        src = dst = bufs[0].at[2 * place[0] + place[1]]
    else:
        src, dst = bufs[0].at[2 * peer[0] + peer[1]], bufs[1].at[j]
    return pltpu.make_async_remote_copy(src_ref=src, dst_ref=dst, send_sem=send.at[j], recv_sem=recv.at[j],
                                        device_id=peer, device_id_type=MESH)


def _exchange_start(groups, after, gather, name):
    sizes = [len(g) for g in groups]
    flat = [b for g in groups for b in g]
    ng, nb = len(groups), len(flat)

    def body(*refs):
        bufs, sems = refs[:nb], refs[nb + 1:nb + 1 + 2 * ng]
        x, y, c = _place()
        for j, peer in enumerate(_chip_peers(x, y, c)):
            at = 0
            for i, size in enumerate(sizes):
                _exchange_copy(bufs[at:at + size], sems[2 * i], sems[2 * i + 1], j, peer, (x, y), gather).start()
                at += size

    any_space = pl.BlockSpec(memory_space=pl.ANY)
    out = pl.pallas_call(
        body, name=name, in_specs=[_HBM] * nb + [any_space],
        out_specs=[_SEM] * (2 * ng) + [_HBM] * nb + [any_space],
        out_shape=[pltpu.SemaphoreType.DMA((3,))] * (2 * ng) + [pltpu.HBM(b.shape, b.dtype) for b in flat]
        + [_sds(after.shape, after.dtype)],
        input_output_aliases={i: 2 * ng + i for i in range(nb + 1)},
        compiler_params=pltpu.CompilerParams(has_side_effects=_EFFECT),
    )(*[pltpu.with_memory_space_constraint(b, pltpu.HBM) for b in flat], after)
    started, at = [], 2 * ng
    for i, size in enumerate(sizes):
        started.append((out[2 * i], out[2 * i + 1], tuple(out[at:at + size])))
        at += size
    return started, out[-1]


def _exchange_wait(started, after, gather, name):
    send, recv, bufs = started
    nb = len(bufs)

    def body(*refs):
        x, y, c = _place()
        for j, peer in enumerate(_chip_peers(x, y, c)):
            cp = _exchange_copy(refs[:nb], refs[nb], refs[nb + 1], j, peer, (x, y), gather)
            cp.wait_send()
            cp.wait_recv()

    return pl.pallas_call(
        body, name=name, in_specs=[_HBM] * nb + [_SEM, _SEM, pl.BlockSpec(memory_space=pl.ANY)],
        out_specs=[_HBM] * nb, out_shape=[pltpu.HBM(b.shape, b.dtype) for b in bufs],
        input_output_aliases={i: i for i in range(nb)},
        compiler_params=pltpu.CompilerParams(has_side_effects=_EFFECT),
    )(*bufs, send, recv, after)


def _swap_sibling(vs, name):
    n = len(vs)

    def body(*refs):
        v_refs, o_refs, (send, recv) = refs[:n], refs[n:2 * n], refs[2 * n:]
        x, y, c = _place()
        cps = [pltpu.make_async_remote_copy(
            src_ref=v_refs[a], dst_ref=o_refs[a], send_sem=send.at[a], recv_sem=recv.at[a],
            device_id=(x, y, 1 - c), device_id_type=MESH) for a in range(n)]
        for cp in cps:
            cp.start()
        for cp in cps:
            cp.wait()

    return _comm_call(body, name, n, [_sds(v.shape, v.dtype) for v in vs],
                      [pltpu.SemaphoreType.DMA((n,)), pltpu.SemaphoreType.DMA((n,))])(*vs)


def _gather_rows(v, name):
    r, n = v.shape

    def body(v_ref, o_ref, send, recv, loc):
        x, y, c = _place()
        me = 4 * x + 2 * y + c
        own = pltpu.make_async_copy(v_ref, o_ref.at[me], loc)
        own.start()
        peers = []
        for k in range(1, 8):
            px = 1 - x if k & 4 else x
            py = 1 - y if k & 2 else y
            pc = 1 - c if k & 1 else c
            peers.append((px, py, pc))
        sends = []
        for k, peer in enumerate(peers):
            cp = pltpu.make_async_remote_copy(src_ref=v_ref, dst_ref=o_ref.at[me], send_sem=send.at[k],
                                              recv_sem=recv.at[k], device_id=peer, device_id_type=MESH)
            cp.start()
            sends.append(cp)
        for k, peer in enumerate(peers):
            pltpu.make_async_remote_copy(src_ref=v_ref, dst_ref=o_ref.at[4 * peer[0] + 2 * peer[1] + peer[2]],
                                         send_sem=send.at[k], recv_sem=recv.at[k], device_id=peer,
                                         device_id_type=MESH).wait_recv()
        for cp in sends:
            cp.wait_send()
        own.wait()

    vmem = pl.BlockSpec(memory_space=pltpu.VMEM)
    return pl.pallas_call(body, name=name, in_specs=[vmem], out_specs=vmem, out_shape=_sds((8, r, n), v.dtype),
                          scratch_shapes=[pltpu.SemaphoreType.DMA((7,)), pltpu.SemaphoreType.DMA((7,)),
                                          pltpu.SemaphoreType.DMA])(v)


def _cast_place(w, shard, name):
    r, c = w.shape
    tr = r // 4

    def body(s_ref, w_ref, o_ref):
        o_ref[0] = w_ref[...].astype(BF16)

    return pl.pallas_call(
        body, name=name, out_shape=_sds((N_SHARD, r, c), BF16),
        grid_spec=pltpu.PrefetchScalarGridSpec(
            num_scalar_prefetch=1, grid=(4,), in_specs=[pl.BlockSpec((tr, c), lambda i, s: (i, 0))],
            out_specs=pl.BlockSpec((1, tr, c), lambda i, s: (s[0], i, 0))),
        compiler_params=pltpu.CompilerParams(dimension_semantics=("arbitrary",)),
    )(shard.reshape(1).astype(jnp.int32), w)


def _mod_part(c_all, w_ada, b_ada, name):
    n = w_ada.shape[1]

    def body(c_ref, w_ref, b_ref, a_ref, p_ref):
        cv = c_ref[...]
        ca = cv * _sigmoid(cv)
        a_ref[...] = ca
        p_ref[...] = jnp.dot(ca, w_ref[...], precision=lax.Precision.HIGHEST, preferred_element_type=F32) + b_ref[...]

    full = lambda a: pl.BlockSpec(a.shape, lambda i: (0, 0))
    return _call(body, name=name, grid=(1,), in_specs=[full(c_all), full(w_ada), full(b_ada)],
                 out_specs=[pl.BlockSpec((8, D_MODEL), lambda i: (0, 0)), pl.BlockSpec((8, n), lambda i: (0, 0))],
                 out_shape=[_sds((8, D_MODEL)), _sds((8, n))])(c_all, w_ada, b_ada)


def _sum_received(g_own, land, name):
    r, c = g_own.shape
    tr = r // 4

    def body(g_ref, l_ref, o_ref):
        o_ref[...] = ((g_ref[...] + l_ref[0].astype(F32)) + l_ref[1].astype(F32)) + l_ref[2].astype(F32)

    return _call(body, name=name, grid=(4,),
                 in_specs=[_rows(tr, c), pl.BlockSpec((3, tr, c), lambda i: (0, i, 0))],
                 out_specs=_rows(tr, c), out_shape=_sds((r, c)))(g_own, land)


def _outer_sum(ct, dm, name):
    k, n = ct.shape[0], dm.shape[1]
    tr = k // 4

    def body(c_ref, d_ref, o_ref):
        cv = c_ref[...]
        dv = d_ref[...]
        acc = cv[:, 0:1] * dv[0:1, :]
        for i in range(1, 8):
            acc = acc + cv[:, i:i + 1] * dv[i:i + 1, :]
        o_ref[...] = acc

    return _call(body, name=name, grid=(4,), in_specs=[_rows(tr, 8), pl.BlockSpec((8, n), lambda i: (0, 0))],
                 out_specs=_rows(tr, n), out_shape=_sds((k, n)))(ct, dm)


def _small_grads(stats, lb_logits, name):
    def body(s_ref, lg_ref, g_ref, l_ref):
        tot = s_ref[0:1, :]
        for i in range(1, 8):
            tot = tot + s_ref[i:i + 1, :]
        part = lambda off, n: tot[:, off:off + n]
        l_ref[...] = jnp.zeros((1, 128), F32) + (0.5 / D_MODEL) * jnp.sum(part(ST_LOSS, D_MODEL))
        lg = lg_ref[...]
        lb = _sigmoid(lg[0:1] - lg[1:2])
        dl0 = part(ST_DLB, HG_WIDTH) * lb * (1.0 - lb)
        g_ref[:, SP_BADA:SP_BADA + 6 * D_MODEL] = part(ST_DMOD, 6 * D_MODEL)
        g_ref[:, SP_N1:SP_N1 + D_MODEL] = part(ST_DG1, D_MODEL)
        g_ref[:, SP_LB:SP_LB + HG_WIDTH] = dl0
        g_ref[:, SP_LB + HG_WIDTH:SP_LB + 2 * HG_WIDTH] = -dl0
        g_ref[:, SP_OG:SP_OG + HG_HEAD] = part(ST_DOG, HG_HEAD)
        g_ref[:, SP_AG:SP_AG + ATT_WIDTH] = part(ST_DAG, ATT_WIDTH)
        g_ref[:, SP_N2:SP_N2 + D_MODEL] = part(ST_DG2, D_MODEL)
        g_ref[:, SP_FG:SP_FG + D_MODEL] = part(ST_DFG, D_MODEL)

    return _call(body, name=name, grid=(1,),
                 in_specs=[pl.BlockSpec((8, ST_WIDTH), lambda i: (0, 0)), pl.BlockSpec((2, HG_WIDTH), lambda i: (0, 0))],
                 out_specs=[pl.BlockSpec((1, SP_WIDTH), lambda i: (0, 0)), pl.BlockSpec((1, 128), lambda i: (0, 0))],
                 out_shape=[_sds((1, SP_WIDTH)), _sds((1, 128))])(stats, lb_logits)


def _adamw(w, gs, m, v, name, steps=4):
    r, c = w.shape
    tr = r // steps
    ng = len(gs)

    def body(*refs):
        w_ref, g_refs, (m_ref, v_ref, g_out, d_out, m_out, v_out) = refs[0], refs[1:1 + ng], refs[1 + ng:]
        g = g_refs[0][...]
        for g_ref in g_refs[1:]:
            g = g + g_ref[...]
        m_new = ADAM_B1 * m_ref[...] + (1.0 - ADAM_B1) * g
        v_new = ADAM_B2 * v_ref[...] + (1.0 - ADAM_B2) * (g * g)
        m_hat = m_new / (1.0 - ADAM_B1 ** ADAM_STEP)
        v_hat = v_new / (1.0 - ADAM_B2 ** ADAM_STEP)
        g_out[...] = g
        d_out[...] = -ADAM_LR * (m_hat / (jnp.sqrt(v_hat) + ADAM_EPS) + ADAM_WD * w_ref[...])
        m_out[...] = m_new
        v_out[...] = v_new

    row = _rows(tr, c)
    return _call(body, name=name, grid=(steps,), in_specs=[row] * (3 + ng), out_specs=[row] * 4,
                 out_shape=[_sds((r, c))] * 4)(w, *gs, m, v)


def kernel(x, c, w_ada, b_ada, norm1_g, w_in, hg_lb_logits, hg_onorm_g, att_onorm_g, w_out, norm2_g, w_gate_up, w_down, final_g, loss_target, m_w_ada, m_b_ada, m_norm1_g, m_w_in, m_hg_lb_logits, m_hg_onorm_g, m_att_onorm_g, m_w_out, m_norm2_g, m_w_gate_up, m_w_down, m_final_g, v_w_ada, v_b_ada, v_norm1_g, v_w_in, v_hg_lb_logits, v_hg_onorm_g, v_att_onorm_g, v_w_out, v_norm2_g, v_w_gate_up, v_w_down, v_final_g):
    ix, iy, ic = _place()
    shard = 2 * ix + iy
    sample = 4 * ix + 2 * iy + ic
    n_ada = w_ada.shape[2]

    shards = [w_in[0], w_out[0], w_gate_up[0], w_down[0]]
    names = ["w_in", "w_out", "w_gu", "w_down"]
    shapes = [(N_SHARD,) + w.shape for w in shards]
    placed = [(_cast_place(w, shard, "place_" + nm),) for w, nm in zip(shards, names)]

    c_all = _gather_rows(c, "gather_c").reshape(8, D_MODEL)
    b_part = lax.dynamic_slice(b_ada, (0, shard * n_ada), (1, n_ada))
    c_act, part = _mod_part(c_all, w_ada[0], b_part, "mod_part")
    parts = _gather_rows(part, "gather_mod")[::2]
    mod = lax.dynamic_index_in_dim(parts, sample, axis=1, keepdims=False).reshape(1, 6 * D_MODEL)
    (first,), mod = _exchange_start(placed[:1], mod, True, "gather_start_w_in")
    gathering = {"w_in": first}

    def get_w(name, after):
        (full,) = _exchange_wait(gathering[name], after, True, "gather_wait_" + name)
        if name == "w_in":
            rest, full = _exchange_start(placed[1:], full, True, "gather_start_rest")
            gathering.update(zip(names[1:], rest))
        return full if name in ("w_in", "w_gu") else full.reshape(1, -1, D_MODEL)

    scattering = {}

    def put_g(name, g, g_bf16):
        shape = shapes[names.index(name)]
        land = lax.empty((3,) + shape[1:], BF16)
        (started,), g = _exchange_start([(g_bf16.reshape(shape), land)], g, False, "scatter_start_" + name)
        scattering[name] = (g.reshape(shape), started)

    dx, stats = _local_step(x[0], loss_target[0], mod, norm1_g, hg_lb_logits, hg_onorm_g, att_onorm_g,
                            norm2_g, final_g, get_w, put_g)

    def summed(name, after):
        g, started = scattering[name]
        _, land = _exchange_wait(started, after, False, "scatter_wait_" + name)
        own = lax.dynamic_index_in_dim(g, shard, axis=0, keepdims=False)
        return _sum_received(own, land, "sum_" + name)

    stats_all = _gather_rows(stats, "gather_stats").reshape(8, ST_WIDTH)
    g_small, loss = _small_grads(stats_all, hg_lb_logits, "small_grads")
    dmod = lax.dynamic_slice(stats_all, (0, ST_DMOD + shard * n_ada), (8, n_ada))
    g_ada = _outer_sum(c_act.T, dmod, "w_ada_grad")

    smalls = [(b_ada, m_b_ada, v_b_ada), (norm1_g, m_norm1_g, v_norm1_g),
              (hg_lb_logits, m_hg_lb_logits, v_hg_lb_logits), (hg_onorm_g, m_hg_onorm_g, v_hg_onorm_g),
              (att_onorm_g, m_att_onorm_g, v_att_onorm_g), (norm2_g, m_norm2_g, v_norm2_g),
              (final_g, m_final_g, v_final_g)]
    pack = lambda i: jnp.concatenate([t[i].reshape(1, -1) for t in smalls], axis=1)
    small_out = _adamw(pack(0), [g_small], pack(1), pack(2), "adamw_small", steps=1)
    offs = [SP_BADA, SP_N1, SP_LB, SP_OG, SP_AG, SP_N2, SP_FG, SP_WIDTH]
    unpack = lambda a: [a[0, offs[i]:offs[i + 1]].reshape(smalls[i][0].shape) for i in range(7)]
    sg, sd, sm, sv = [unpack(a) for a in small_out]

    ada = _adamw(w_ada[0], [g_ada], m_w_ada[0], v_w_ada[0], "adamw_w_ada")
    moments = [(m_w_in, v_w_in), (m_w_out, v_w_out), (m_w_gate_up, v_w_gate_up), (m_w_down, v_w_down)]

    def update(group, after, tag):
        sums = [summed(nm, after) for nm in group]
        other = _swap_sibling(sums, "swap_sums_" + tag)
        return {nm: _adamw(shards[names.index(nm)], [s, o], moments[names.index(nm)][0][0],
                           moments[names.index(nm)][1][0], "adamw_" + nm) for nm, s, o in zip(group, sums, other)}

    done = update(["w_down", "w_gu", "w_out"], ada[1], "late")
    done.update(update(["w_in"], done["w_out"][1], "in"))
    big = [ada] + [done[nm] for nm in names]
    bg, bd, bm, bv = [[t[i][None] for t in big] for i in range(4)]

    def order(b, s):
        return [b[0], s[0], s[1], b[1], s[2], s[3], s[4], b[2], s[5], b[3], b[4], s[6]]

    return (loss[0, 0], dx[None], *order(bg, sg), *order(bd, sd), *order(bm, sm), *order(bv, sv))
```

```python
import functools

import jax
import jax.numpy as jnp
from jax import lax
from jax.experimental import pallas as pl
from jax.experimental.pallas import tpu as pltpu

F32 = jnp.float32
BF16 = jnp.bfloat16
MESH = pl.DeviceIdType.MESH

D_MODEL = 1024
HG_WIDTH = 512
HG_HEAD = 128
HG_CHUNK = 64
HG_GROUP = 4
ATT_WIDTH = 512
ATT_HEADS = 8
ATT_BLOCK = 128
DILATIONS = (1, 4, 16)
D_FF = 2816
IN_WIDTH = 3584
N_SHARD = 4
RMS_EPS = 1e-6
NEG = -1e30

ADAM_LR = 0.001
ADAM_B1 = 0.9
ADAM_B2 = 0.999
ADAM_EPS = 1e-08
ADAM_WD = 0.01
ADAM_STEP = 10

VMEM_LIMIT = 56 * 2**20

ST_LOSS, ST_DFG, ST_DG2, ST_DG1 = 0, 1024, 2048, 3072
ST_DLB, ST_DAG, ST_DOG, ST_DMOD = 4096, 4608, 5120, 5248
ST_WIDTH = 5248 + 6144
SP_BADA, SP_N1, SP_LB, SP_OG, SP_AG, SP_N2, SP_FG = 0, 6144, 7168, 8192, 8320, 8832, 9856
SP_WIDTH = 10880


def _call(body, *, name, grid, in_specs, out_specs, out_shape, scratch_shapes=()):
    return pl.pallas_call(
        body, name=name, grid=grid, in_specs=in_specs, out_specs=out_specs, out_shape=out_shape,
        scratch_shapes=list(scratch_shapes),
        compiler_params=pltpu.CompilerParams(
            dimension_semantics=("arbitrary",) * len(grid), vmem_limit_bytes=VMEM_LIMIT))


def _sds(shape, dtype=F32):
    return jax.ShapeDtypeStruct(shape, dtype)


def _dot(a, b):
    return jnp.dot(a, b, preferred_element_type=F32)


def _dot_nt(a, b):
    return lax.dot_general(a, b, (((1,), (1,)), ((), ())), preferred_element_type=F32)


def _dot_tn(a, b):
    return lax.dot_general(a, b, (((0,), (0,)), ((), ())), preferred_element_type=F32)


def _sigmoid(x):
    return 1.0 / (1.0 + jnp.exp(-x))


def _rows(tr, width):
    return pl.BlockSpec((tr, width), lambda i: (i, 0))


def _vec(width):
    return pl.BlockSpec((1, width), lambda i: (0, 0))


def _acc(ref, val, first):
    @pl.when(first)
    def _():
        ref[...] = val

    @pl.when(jnp.logical_not(first))
    def _():
        ref[...] += val


def _mm_nn(a, b3, name, tm=512):
    m, k = a.shape
    s, _, n = b3.shape

    def body(a_ref, b_ref, o_ref):
        o_ref[...] = _dot(a_ref[...], b_ref[0])

    return _call(
        body, name=name, grid=(s, m // tm),
        in_specs=[pl.BlockSpec((tm, k), lambda j, i: (i, 0)), pl.BlockSpec((1, k, n), lambda j, i: (j, 0, 0))],
        out_specs=pl.BlockSpec((tm, n), lambda j, i: (i, j)), out_shape=_sds((m, s * n)))(a, b3)


def _mm_nt(dy, b3, name, tm=512):
    m = dy.shape[0]
    s, k, n = b3.shape

    def body(dy_ref, b_ref, o_ref):
        _acc(o_ref, _dot_nt(dy_ref[...], b_ref[0]), pl.program_id(1) == 0)

    return _call(
        body, name=name, grid=(m // tm, s),
        in_specs=[pl.BlockSpec((tm, n), lambda i, j: (i, j)), pl.BlockSpec((1, k, n), lambda i, j: (j, 0, 0))],
        out_specs=pl.BlockSpec((tm, k), lambda i, j: (i, 0)), out_shape=_sds((m, k)))(dy, b3)


def _mm_tn(a, dy, s, name, tm=1024, tk=None):
    m, k = a.shape
    n = dy.shape[1] // s
    tk = k if tk is None else tk
    steps = m // tm

    def body(a_ref, dy_ref, o_ref, ob_ref):
        i = pl.program_id(2)
        _acc(o_ref, _dot_tn(a_ref[...], dy_ref[...])[None], i == 0)

        @pl.when(i == steps - 1)
        def _():
            ob_ref[...] = o_ref[...].astype(BF16)

    out = pl.BlockSpec((1, tk, n), lambda kk, j, i: (j, kk, 0))
    return _call(
        body, name=name, grid=(k // tk, s, steps),
        in_specs=[pl.BlockSpec((tm, tk), lambda kk, j, i: (i, kk)), pl.BlockSpec((tm, n), lambda kk, j, i: (i, j))],
        out_specs=[out, out], out_shape=[_sds((s, k, n)), _sds((s, k, n), BF16)])(a, dy)


def _rms(x):
    return lax.rsqrt(jnp.mean(x * x, axis=-1, keepdims=True) + RMS_EPS)


def _rms_bwd(dxh, xh, r):
    return r * (dxh - xh * jnp.mean(dxh * xh, axis=-1, keepdims=True))


def _norm_mod(x, g, scale, shift, name, tr=512):
    t = x.shape[0]

    def body(x_ref, g_ref, sc_ref, sh_ref, h_ref):
        xv = x_ref[...]
        n = xv * _rms(xv) * g_ref[...]
        h_ref[...] = (n * (1.0 + sc_ref[...]) + sh_ref[...]).astype(BF16)

    return _call(body, name=name, grid=(t // tr,),
                 in_specs=[_rows(tr, D_MODEL), _vec(D_MODEL), _vec(D_MODEL), _vec(D_MODEL)],
                 out_specs=_rows(tr, D_MODEL), out_shape=_sds((t, D_MODEL), BF16))(x, g, scale, shift)


def _mix_in(o_hg, proj, att, og, ag, name, tr=256):
    t = o_hg.shape[0]

    def body(o_ref, g_ref, a_ref, og_ref, ag_ref, m_ref):
        for h in range(HG_WIDTH // HG_HEAD):
            sl = slice(h * HG_HEAD, (h + 1) * HG_HEAD)
            oh = o_ref[:, sl]
            gv = g_ref[:, sl]
            m_ref[:, sl] = (oh * _rms(oh) * og_ref[...] * (gv * _sigmoid(gv))).astype(BF16)
        av = a_ref[...]
        m_ref[:, HG_WIDTH:] = (av * _rms(av) * ag_ref[...]).astype(BF16)

    return _call(body, name=name, grid=(t // tr,),
                 in_specs=[_rows(tr, HG_WIDTH), pl.BlockSpec((tr, HG_WIDTH), lambda i: (i, 3)), _rows(tr, ATT_WIDTH),
                           _vec(HG_HEAD), _vec(ATT_WIDTH)],
                 out_specs=_rows(tr, D_MODEL), out_shape=_sds((t, D_MODEL), BF16))(o_hg, proj, att, og, ag)


def _resid_norm_mod(x, mix, gate, g, scale, shift, name, tr=256):
    t = x.shape[0]

    def body(x_ref, m_ref, gt_ref, g_ref, sc_ref, sh_ref, x2_ref, h_ref):
        x2 = x_ref[...] + gt_ref[...] * m_ref[...]
        x2_ref[...] = x2
        n = x2 * _rms(x2) * g_ref[...]
        h_ref[...] = (n * (1.0 + sc_ref[...]) + sh_ref[...]).astype(BF16)

    return _call(body, name=name, grid=(t // tr,),
                 in_specs=[_rows(tr, D_MODEL), _rows(tr, D_MODEL)] + [_vec(D_MODEL)] * 4,
                 out_specs=[_rows(tr, D_MODEL), _rows(tr, D_MODEL)],
                 out_shape=[_sds((t, D_MODEL)), _sds((t, D_MODEL), BF16)])(x, mix, gate, g, scale, shift)


def _mm_gate_up(h, w_gu, name, tm=512):
    m, k = h.shape
    n = w_gu.shape[2]

    def body(h_ref, wa_ref, wu_ref, a_ref, u_ref, o_ref):
        hv = h_ref[...]
        a = _dot(hv, wa_ref[0])
        u = _dot(hv, wu_ref[0])
        a_ref[...] = a.astype(BF16)
        u_ref[...] = u.astype(BF16)
        o_ref[...] = (a * _sigmoid(a) * u).astype(BF16)

    out = pl.BlockSpec((tm, n), lambda j, i: (i, j))
    return _call(body, name=name, grid=(2, m // tm),
                 in_specs=[pl.BlockSpec((tm, k), lambda j, i: (i, 0)), pl.BlockSpec((1, k, n), lambda j, i: (j, 0, 0)),
                           pl.BlockSpec((1, k, n), lambda j, i: (j + 2, 0, 0))],
                 out_specs=[out, out, out], out_shape=[_sds((m, 2 * n), BF16)] * 3)(h, w_gu, w_gu)


def _mm_down_dx(dffn, w_down, a, u, name, tm=512):
    m = dffn.shape[0]
    _, k, n = w_down.shape

    def body(d_ref, w_ref, a_ref, u_ref, o_ref):
        dact = _dot_nt(d_ref[...], w_ref[0])
        av = a_ref[...].astype(F32)
        sg = _sigmoid(av)
        o_ref[:, :k] = (dact * u_ref[...].astype(F32) * sg * (1.0 + av * (1.0 - sg))).astype(BF16)
        o_ref[:, k:] = (dact * av * sg).astype(BF16)

    return _call(body, name=name, grid=(m // tm,),
                 in_specs=[_rows(tm, n), pl.BlockSpec((1, k, n), lambda i: (0, 0, 0)), _rows(tm, k), _rows(tm, k)],
                 out_specs=_rows(tm, 2 * k), out_shape=_sds((m, 2 * k), BF16))(dffn, w_down, a, u)


def _final_loss(x2, ffn, gate, fg, tgt, name, tr=256):
    t = x2.shape[0]

    def body(x_ref, f_ref, gt_ref, fg_ref, t_ref, dx_ref, df_ref, l_ref, dfg_ref, dgt_ref):
        first = pl.program_id(0) == 0
        ffn_v = f_ref[...]
        x3 = x_ref[...] + gt_ref[...] * ffn_v
        r = _rms(x3)
        xh = x3 * r
        err = xh * fg_ref[...] - t_ref[...]
        dy = err * (1.0 / D_MODEL)
        dx3 = _rms_bwd(dy * fg_ref[...], xh, r)
        dx_ref[...] = dx3
        df_ref[...] = (dx3 * gt_ref[...]).astype(BF16)
        _acc(l_ref, jnp.sum(err * err, axis=0, keepdims=True), first)
        _acc(dfg_ref, jnp.sum(dy * xh, axis=0, keepdims=True), first)
        _acc(dgt_ref, jnp.sum(dx3 * ffn_v, axis=0, keepdims=True), first)

    row, vec = _rows(tr, D_MODEL), _vec(D_MODEL)
    return _call(body, name=name, grid=(t // tr,), in_specs=[row, row, vec, vec, row],
                 out_specs=[row, row, vec, vec, vec],
                 out_shape=[_sds((t, D_MODEL)), _sds((t, D_MODEL), BF16)] + [_sds((1, D_MODEL))] * 3)(
                     x2, ffn, gate, fg, tgt)


def _norm_mod_bwd(dh, x, g, scale, dres, name, gate=None, mix=None, tr=256):
    t = x.shape[0]
    below = gate is not None

    def body(*refs):
        if below:
            dh_ref, x_ref, g_ref, sc_ref, dr_ref, gt_ref, m_ref, dx_ref, dsh_ref, dsc_ref, dg_ref, dgt_ref, dm_ref = refs
        else:
            dh_ref, x_ref, g_ref, sc_ref, dr_ref, dx_ref, dsh_ref, dsc_ref, dg_ref = refs
        first = pl.program_id(0) == 0
        xv = x_ref[...]
        dhv = dh_ref[...]
        r = _rms(xv)
        xh = xv * r
        dn = dhv * (1.0 + sc_ref[...])
        dx = dr_ref[...] + _rms_bwd(dn * g_ref[...], xh, r)
        dx_ref[...] = dx
        _acc(dsh_ref, jnp.sum(dhv, axis=0, keepdims=True), first)
        _acc(dsc_ref, jnp.sum(dhv * xh * g_ref[...], axis=0, keepdims=True), first)
        _acc(dg_ref, jnp.sum(dn * xh, axis=0, keepdims=True), first)
        if below:
            _acc(dgt_ref, jnp.sum(dx * m_ref[...], axis=0, keepdims=True), first)
            dm_ref[...] = (dx * gt_ref[...]).astype(BF16)

    row, vec = _rows(tr, D_MODEL), _vec(D_MODEL)
    in_specs = [row, row, vec, vec, row] + ([vec, row] if below else [])
    out_specs = [row, vec, vec, vec] + ([vec, row] if below else [])
    out_shape = [_sds((t, D_MODEL))] + [_sds((1, D_MODEL))] * 3 + ([_sds((1, D_MODEL)), _sds((t, D_MODEL), BF16)] if below else [])
    args = (dh, x, g, scale, dres) + ((gate, mix) if below else ())
    return _call(body, name=name, grid=(t // tr,), in_specs=in_specs, out_specs=out_specs, out_shape=out_shape)(*args)


def _mix_in_bwd(dmi, o_hg, proj, att, og, ag, name, tr=256):
    t = o_hg.shape[0]

    def body(d_ref, o_ref, g_ref, a_ref, og_ref, ag_ref, do_ref, dg_ref, da_ref, dd_ref, dog_ref, dag_ref):
        first = pl.program_id(0) == 0
        dog = jnp.zeros((1, HG_HEAD), F32)
        for h in range(HG_WIDTH // HG_HEAD):
            sl = slice(h * HG_HEAD, (h + 1) * HG_HEAD)
            oh = o_ref[:, sl]
            gv = g_ref[:, sl]
            dv = d_ref[:, sl]
            r = _rms(oh)
            xh = oh * r
            sg = _sigmoid(gv)
            dno = dv * gv * sg
            dg_ref[:, sl] = dv * xh * og_ref[...] * sg * (1.0 + gv * (1.0 - sg))
            dog = dog + jnp.sum(dno * xh, axis=0, keepdims=True)
            do_ref[:, sl] = _rms_bwd(dno * og_ref[...], xh, r)
        _acc(dog_ref, dog, first)
        av = a_ref[...]
        dav = d_ref[:, HG_WIDTH:]
        r = _rms(av)
        xa = av * r
        _acc(dag_ref, jnp.sum(dav * xa, axis=0, keepdims=True), first)
        datt = _rms_bwd(dav * ag_ref[...], xa, r)
        da_ref[...] = datt
        prod = datt * av
        lane = lax.broadcasted_iota(jnp.int32, (1, 128), 1)
        dd = jnp.zeros((tr, 128), F32)
        for hp in range(ATT_HEADS // 2):
            pp = prod[:, hp * 128:(hp + 1) * 128]
            lo = jnp.sum(jnp.where(lane < 64, pp, 0.0), axis=-1, keepdims=True)
            hi = jnp.sum(jnp.where(lane >= 64, pp, 0.0), axis=-1, keepdims=True)
            dd = jnp.where(lane == 2 * hp, lo, dd)
            dd = jnp.where(lane == 2 * hp + 1, hi, dd)
        dd_ref[...] = dd

    half = _rows(tr, HG_WIDTH)
    return _call(body, name=name, grid=(t // tr,),
                 in_specs=[_rows(tr, D_MODEL), half, pl.BlockSpec((tr, HG_WIDTH), lambda i: (i, 3)), half,
                           _vec(HG_HEAD), _vec(ATT_WIDTH)],
                 out_specs=[half, half, half, _rows(tr, 128), _vec(HG_HEAD), _vec(ATT_WIDTH)],
                 out_shape=[_sds((t, HG_WIDTH))] * 3 + [_sds((t, 128)), _sds((1, HG_HEAD)), _sds((1, ATT_WIDTH))])(
                     dmi, o_hg, proj, att, og, ag)


def _dproj(dhg, dg, dqs, dkvs, name, tr=256):
    t = dhg.shape[0]
    w3 = 3 * HG_WIDTH
    w4 = w3 + HG_WIDTH
    nbr = len(dqs)

    def body(*refs):
        h_ref, g_ref, q_refs, kv_refs, o_ref = refs[0], refs[1], refs[2:2 + nbr], refs[2 + nbr:2 + 2 * nbr], refs[-1]
        o_ref[:, :w3] = h_ref[...]
        o_ref[:, w3:w4] = g_ref[...].astype(BF16)
        o_ref[:, w4:w4 + ATT_WIDTH] = sum(r[...].astype(F32) for r in q_refs).astype(BF16)
        o_ref[:, w4 + ATT_WIDTH:] = sum(r[...].astype(F32) for r in kv_refs).astype(BF16)

    return _call(body, name=name, grid=(t // tr,),
                 in_specs=[_rows(tr, w3), _rows(tr, HG_WIDTH)] + [_rows(tr, ATT_WIDTH)] * nbr
                 + [_rows(tr, 2 * ATT_WIDTH)] * nbr,
                 out_specs=_rows(tr, IN_WIDTH), out_shape=_sds((t, IN_WIDTH), BF16))(dhg, dg, *dqs, *dkvs)


def _chunk_tri(upper):
    row = lax.broadcasted_iota(jnp.int32, (HG_GROUP, HG_CHUNK, HG_CHUNK), 1)
    col = lax.broadcasted_iota(jnp.int32, (HG_GROUP, HG_CHUNK, HG_CHUNK), 2)
    return (row <= col if upper else row >= col).astype(BF16)


def _chunk_cumsum(x, tri):
    x3 = x.reshape(HG_GROUP, HG_CHUNK, x.shape[1])
    dims = (((2,), (1,)), ((0,), (0,)))
    out = None
    for _ in range(3):
        part = x3.astype(BF16)
        x3 = x3 - part.astype(F32)
        term = lax.dot_general(tri, part, dims, preferred_element_type=F32)
        out = term if out is None else out + term
    return out.reshape(x.shape)


def _hg_gates(f_raw, q_raw, lb, tri):
    sg = _sigmoid(f_raw)
    f = lb + (1.0 - lb) * sg
    k = 1.0 - f
    b = _chunk_cumsum(jnp.log(f), tri)
    sq = _sigmoid(q_raw)
    return sg, f, k, b, sq


def _hg_masks(rows):
    row = lax.broadcasted_iota(jnp.int32, (rows, rows), 0)
    col = lax.broadcasted_iota(jnp.int32, (rows, rows), 1)
    same = (row // HG_CHUNK) == (col // HG_CHUNK)
    return jnp.logical_and(row >= col, same), jnp.logical_and(row <= col, same)


def _per_chunk(rows_of):
    return jnp.concatenate([jnp.broadcast_to(r, (HG_CHUNK, r.shape[1])) for r in rows_of], axis=0)


def _hgrn_fwd(proj, lb_logits, name):
    t = proj.shape[0]
    nc = t // HG_CHUNK
    nh = HG_WIDTH // HG_HEAD
    rows = HG_GROUP * HG_CHUNK

    def body(q_ref, f_ref, i_ref, lg_ref, o_ref, st_ref, s_scr):
        @pl.when(pl.program_id(0) == 0)
        def _():
            s_scr[...] = jnp.zeros_like(s_scr)

        lg = lg_ref[...]
        lb_all = _sigmoid(lg[0:1] - lg[1:2])
        causal, _ = _hg_masks(rows)
        tri = _chunk_tri(False)
        for h in range(nh):
            sl = slice(h * HG_HEAD, (h + 1) * HG_HEAD)
            q_raw = q_ref[:, sl]
            _, _, k, b, sq = _hg_gates(f_ref[:, sl], q_raw, lb_all[:, sl], tri)
            v = i_ref[:, sl].astype(BF16)
            gls = [b[(g + 1) * HG_CHUNK - 1:(g + 1) * HG_CHUNK] for g in range(HG_GROUP)]
            qd = (q_raw * sq * jnp.exp(b)).astype(BF16)
            kd = (k * jnp.exp(-b)).astype(BF16)
            ke = (k * jnp.exp(_per_chunk(gls) - b)).astype(BF16)
            a = jnp.where(causal, _dot_nt(qd, kd), 0.0).astype(BF16)
            o_intra = _dot(a, v)
            st = s_scr[h]
            o_inter = []
            for g in range(HG_GROUP):
                rs = slice(g * HG_CHUNK, (g + 1) * HG_CHUNK)
                st_ref[g, sl, :] = st
                o_inter.append(_dot_nt(qd[rs], st.astype(BF16)))
                st = st * jnp.exp(gls[g]) + _dot_tn(v[rs], ke[rs])
            s_scr[h] = st
            o_ref[:, sl] = o_intra + jnp.concatenate(o_inter, axis=0)

    blk = lambda j: pl.BlockSpec((rows, HG_WIDTH), lambda c: (c, j))
    return _call(body, name=name, grid=(nc // HG_GROUP,),
                 in_specs=[blk(0), blk(1), blk(2), pl.BlockSpec((2, HG_WIDTH), lambda c: (0, 0))],
                 out_specs=[blk(0), pl.BlockSpec((HG_GROUP, HG_WIDTH, HG_HEAD), lambda c: (c, 0, 0))],
                 out_shape=[_sds((t, HG_WIDTH)), _sds((nc, HG_WIDTH, HG_HEAD))],
                 scratch_shapes=[pltpu.VMEM((nh, HG_HEAD, HG_HEAD), F32)])(proj, proj, proj, lb_logits)


def _hgrn_bwd(proj, lb_logits, states, do, name):
    t = proj.shape[0]
    ng = t // (HG_GROUP * HG_CHUNK)
    nh = HG_WIDTH // HG_HEAD
    rows = HG_GROUP * HG_CHUNK

    def body(q_ref, f_ref, i_ref, lg_ref, st_ref, do_ref, d_ref, dlb_ref, ds_scr):
        first = pl.program_id(0) == 0

        @pl.when(first)
        def _():
            ds_scr[...] = jnp.zeros_like(ds_scr)

        lg = lg_ref[...]
        lb_all = _sigmoid(lg[0:1] - lg[1:2])
        causal, _ = _hg_masks(rows)
        tri = _chunk_tri(False)
        tri_t = _chunk_tri(True)
        dlb = []
        for h in range(nh):
            sl = slice(h * HG_HEAD, (h + 1) * HG_HEAD)
            q_raw = q_ref[:, sl]
            lb = lb_all[:, sl]
            sg, f, k, b, sq = _hg_gates(f_ref[:, sl], q_raw, lb, tri)
            v = i_ref[:, sl].astype(BF16)
            gls = [b[(g + 1) * HG_CHUNK - 1:(g + 1) * HG_CHUNK] for g in range(HG_GROUP)]
            eb = jnp.exp(b)
            enb = jnp.exp(-b)
            egb = jnp.exp(_per_chunk(gls) - b)
            ke = k * egb
            qd_b, kd_b, ke_b = (q_raw * sq * eb).astype(BF16), (k * enb).astype(BF16), ke.astype(BF16)
            dov = do_ref[:, sl].astype(BF16)
            a = jnp.where(causal, _dot_nt(qd_b, kd_b), 0.0).astype(BF16)
            da = jnp.where(causal, _dot_nt(dov, v), 0.0).astype(BF16)
            dkd = _dot_tn(da, qd_b)
            dst = ds_scr[h]
            dqd_s, dv_s, dke_s, dgl_s = [None] * HG_GROUP, [None] * HG_GROUP, [None] * HG_GROUP, [None] * HG_GROUP
            for g in reversed(range(HG_GROUP)):
                rs = slice(g * HG_CHUNK, (g + 1) * HG_CHUNK)
                st = st_ref[g, sl, :]
                dst_b = dst.astype(BF16)
                egl = jnp.exp(gls[g])
                dqd_s[g] = _dot(dov[rs], st.astype(BF16))
                dv_s[g] = _dot_nt(ke_b[rs], dst_b)
                dke_s[g] = _dot(v[rs], dst_b)
                dgl_s[g] = jnp.sum(dst * st, axis=0, keepdims=True) * egl
                dst = _dot_tn(dov[rs], qd_b[rs]) + dst * egl
            ds_scr[h] = dst
            dqd = _dot(da, kd_b) + jnp.concatenate(dqd_s, axis=0)
            dv = _dot_tn(a, dov) + jnp.concatenate(dv_s, axis=0)
            dke = jnp.concatenate(dke_s, axis=0)
            t1 = dke * ke
            db = dqd * qd_b.astype(F32) - dkd * kd_b.astype(F32) - t1
            dgl = _per_chunk([dgl_s[g] + jnp.sum(t1[g * HG_CHUNK:(g + 1) * HG_CHUNK], axis=0, keepdims=True)
                              for g in range(HG_GROUP)])
            dlf = _chunk_cumsum(db, tri_t) + dgl
            df = dlf / f - (dkd * enb + dke * egb)
            d_ref[:, sl] = (dqd * eb * sq * (1.0 + q_raw * (1.0 - sq))).astype(BF16)
            d_ref[:, HG_WIDTH + h * HG_HEAD:HG_WIDTH + (h + 1) * HG_HEAD] = (
                df * (1.0 - lb) * sg * (1.0 - sg)).astype(BF16)
            d_ref[:, 2 * HG_WIDTH + h * HG_HEAD:2 * HG_WIDTH + (h + 1) * HG_HEAD] = dv.astype(BF16)
            dlb.append(jnp.sum(df * (1.0 - sg), axis=0, keepdims=True))
        _acc(dlb_ref, jnp.concatenate(dlb, axis=1), first)

    rev = lambda j: pl.BlockSpec((rows, HG_WIDTH), lambda c: (ng - 1 - c, j))
    return _call(body, name=name, grid=(ng,),
                 in_specs=[rev(0), rev(1), rev(2), pl.BlockSpec((2, HG_WIDTH), lambda c: (0, 0)),
                           pl.BlockSpec((HG_GROUP, HG_WIDTH, HG_HEAD), lambda c: (ng - 1 - c, 0, 0)), rev(0)],
                 out_specs=[pl.BlockSpec((rows, 3 * HG_WIDTH), lambda c: (ng - 1 - c, 0)), _vec(HG_WIDTH)],
                 out_shape=[_sds((t, 3 * HG_WIDTH), BF16), _sds((1, HG_WIDTH))],
                 scratch_shapes=[pltpu.VMEM((nh, HG_HEAD, HG_HEAD), F32)])(proj, proj, proj, lb_logits, states, do)


def _to_sub(a, dil):
    t, w = a.shape
    return a if dil == 1 else a.reshape(t // dil, dil, w).transpose(1, 0, 2).reshape(t, w)


def _from_sub(a, dil):
    t, w = a.shape
    return a if dil == 1 else a.reshape(dil, t // dil, w).transpose(1, 0, 2).reshape(t, w)


def _att_mask(has_prev):
    qi = lax.broadcasted_iota(jnp.int32, (2 * ATT_BLOCK, 2 * ATT_BLOCK), 0) % ATT_BLOCK
    kj = lax.broadcasted_iota(jnp.int32, (2 * ATT_BLOCK, 2 * ATT_BLOCK), 1)
    prev = jnp.logical_and(jnp.logical_and(kj < ATT_BLOCK, kj >= qi), has_prev)
    cur = jnp.logical_and(kj >= ATT_BLOCK, kj - ATT_BLOCK <= qi)
    return jnp.logical_or(prev, cur), lax.broadcasted_iota(jnp.int32, (1, 128), 1)


def _attn_fwd(qkv, dil, name):
    t = qkv.shape[0]
    nb = t // ATT_BLOCK
    bps = nb // dil

    def body(q_ref, kc_ref, kp_ref, vc_ref, vp_ref, o_ref, l_ref):
        mask, lane = _att_mask((pl.program_id(0) % bps) != 0)
        lo = lane < 64
        nq = ATT_BLOCK
        lse_all = jnp.zeros((nq, 128), F32)
        for hp in range(ATT_HEADS // 2):
            sl = slice(hp * 128, (hp + 1) * 128)
            q2 = q_ref[:, sl]
            zero = jnp.zeros_like(q2)
            qs = jnp.concatenate([jnp.where(lo, q2, zero), jnp.where(lo, zero, q2)], axis=0)
            kk = jnp.concatenate([kp_ref[:, sl], kc_ref[:, sl]], axis=0)
            vv = jnp.concatenate([vp_ref[:, sl], vc_ref[:, sl]], axis=0)
            s = jnp.where(mask, _dot_nt(qs, kk) * 0.125, NEG)
            mx = jnp.max(s, axis=-1, keepdims=True)
            p = jnp.exp(s - mx)
            l = jnp.sum(p, axis=-1, keepdims=True)
            o = _dot(p.astype(BF16), vv) * (1.0 / l)
            o_ref[:, sl] = jnp.where(lo, o[:nq], o[nq:])
            lse = mx + jnp.log(l)
            lse_all = jnp.where(lane == 2 * hp, lse[:nq], lse_all)
            lse_all = jnp.where(lane == 2 * hp + 1, lse[nq:], lse_all)
        l_ref[...] = lse_all

    blk = lambda j, back: pl.BlockSpec((ATT_BLOCK, ATT_WIDTH), lambda n: (jnp.maximum(n - back, 0), j))
    return _call(body, name=name, grid=(nb,),
                 in_specs=[blk(0, 0), blk(1, 0), blk(1, 1), blk(2, 0), blk(2, 1)],
                 out_specs=[pl.BlockSpec((ATT_BLOCK, ATT_WIDTH), lambda n: (n, 0)),
                            pl.BlockSpec((ATT_BLOCK, 128), lambda n: (n, 0))],
                 out_shape=[_sds((t, ATT_WIDTH)), _sds((t, 128))])(qkv, qkv, qkv, qkv, qkv)


def _attn_combine(os_, ls_, name, tr=256):
    t = os_[0].shape[0]
    nbr = len(os_)

    def body(*refs):
        o_refs, l_refs, (a_ref, lt_ref) = refs[:nbr], refs[nbr:2 * nbr], refs[2 * nbr:]
        lane = lax.broadcasted_iota(jnp.int32, (1, 128), 1)
        ls = [r[...] for r in l_refs]
        mx = functools.reduce(jnp.maximum, ls)
        tot = mx + jnp.log(sum(jnp.exp(l - mx) for l in ls))
        lt_ref[...] = tot
        ws = [jnp.exp(l - tot) for l in ls]
        for hp in range(ATT_HEADS // 2):
            sl = slice(hp * 128, (hp + 1) * 128)
            acc = jnp.zeros((tr, 128), F32)
            for w, o_ref in zip(ws, o_refs):
                wf = jnp.where(lane < 64, w[:, 2 * hp:2 * hp + 1], w[:, 2 * hp + 1:2 * hp + 2])
                acc = acc + wf * o_ref[:, sl]
            a_ref[:, sl] = acc

    return _call(body, name=name, grid=(t // tr,),
                 in_specs=[_rows(tr, ATT_WIDTH)] * nbr + [_rows(tr, 128)] * nbr,
                 out_specs=[_rows(tr, ATT_WIDTH), _rows(tr, 128)],
                 out_shape=[_sds((t, ATT_WIDTH)), _sds((t, 128))])(*os_, *ls_)


def _attn_bwd(qkv, dout, lse, dd, dil, name):
    t = qkv.shape[0]
    nb = t // ATT_BLOCK
    bps = nb // dil

    w = ATT_WIDTH
    nq = ATT_BLOCK

    def body(q_ref, kc_ref, kp_ref, vc_ref, vp_ref, do_ref, l_ref, d_ref, dq_ref, dkv_ref, carry):
        n = pl.program_id(0)

        @pl.when(n == 0)
        def _():
            carry[...] = jnp.zeros_like(carry)

        @pl.when(n < nb)
        def _():
            mask, lane = _att_mask((n % bps) != 0)
            lo = lane < 64
            for hp in range(ATT_HEADS // 2):
                sl = slice(hp * 128, (hp + 1) * 128)
                sv = slice(w + hp * 128, w + (hp + 1) * 128)
                q2, do2 = q_ref[:, sl], do_ref[:, sl]
                zero = jnp.zeros_like(q2)
                qs = jnp.concatenate([jnp.where(lo, q2, zero), jnp.where(lo, zero, q2)], axis=0)
                dos = jnp.concatenate([jnp.where(lo, do2, zero), jnp.where(lo, zero, do2)], axis=0)
                kk = jnp.concatenate([kp_ref[:, sl], kc_ref[:, sl]], axis=0)
                vv = jnp.concatenate([vp_ref[:, sl], vc_ref[:, sl]], axis=0)
                ls = jnp.concatenate([l_ref[:, 2 * hp:2 * hp + 1], l_ref[:, 2 * hp + 1:2 * hp + 2]], axis=0)
                dh = jnp.concatenate([d_ref[:, 2 * hp:2 * hp + 1], d_ref[:, 2 * hp + 1:2 * hp + 2]], axis=0)
                p = jnp.exp(jnp.where(mask, _dot_nt(qs, kk) * 0.125 - ls, NEG))
                ds = (p * (_dot_nt(dos, vv) - dh)).astype(BF16)
                dq = _dot(ds, kk) * 0.125
                dq_ref[:, sl] = jnp.where(lo, dq[:nq], dq[nq:]).astype(BF16)
                dk = _dot_tn(ds, qs) * 0.125
                dv = _dot_tn(p.astype(BF16), dos)
                dkv_ref[:, sl] = (carry[:, sl] + dk[:nq]).astype(BF16)
                dkv_ref[:, sv] = (carry[:, sv] + dv[:nq]).astype(BF16)
                carry[:, sl] = dk[nq:]
                carry[:, sv] = dv[nq:]

        @pl.when(n == nb)
        def _():
            dkv_ref[...] = carry[...].astype(BF16)

    def blk(width, j, back):
        return pl.BlockSpec((nq, width), lambda n: (jnp.clip(n - back, 0, nb - 1), j))

    return _call(body, name=name, grid=(nb + 1,),
                 in_specs=[blk(w, 0, 0), blk(w, 1, 0), blk(w, 1, 1), blk(w, 2, 0), blk(w, 2, 1),
                           blk(w, 0, 0), blk(128, 0, 0), blk(128, 0, 0)],
                 out_specs=[blk(w, 0, 0), blk(2 * w, 0, 1)],
                 out_shape=[_sds((t, w), BF16), _sds((t, 2 * w), BF16)],
                 scratch_shapes=[pltpu.VMEM((nq, 2 * w), F32)])(qkv, qkv, qkv, qkv, qkv, dout, lse, dd)


def _local_step(x, tgt, mod, norm1_g, lb_logits, og, ag, norm2_g, fg, get_w, put_g):
    shift1, scale1, gate1, shift2, scale2, gate2 = [mod[:, i * D_MODEL:(i + 1) * D_MODEL] for i in range(6)]
    fg = fg.reshape(1, D_MODEL)

    h1 = _norm_mod(x, norm1_g, scale1, shift1, "norm_mod1")
    w_in = get_w("w_in", h1)
    proj = _mm_nn(h1, w_in, "mm_in")
    o_hg, states = _hgrn_fwd(proj, lb_logits, "hgrn_fwd")
    qkv = proj[:, 4 * HG_WIDTH:].astype(BF16)
    qkvs = [_to_sub(qkv, d) for d in DILATIONS]
    outs = [_attn_fwd(q, d, f"attn_fwd{d}") for q, d in zip(qkvs, DILATIONS)]
    att, lse = _attn_combine([_from_sub(o, d) for (o, _), d in zip(outs, DILATIONS)],
                             [_from_sub(l, d) for (_, l), d in zip(outs, DILATIONS)], "attn_combine")
    mixin = _mix_in(o_hg, proj, att, og, ag, "mix_in")
    w_out = get_w("w_out", mixin)
    mix = _mm_nn(mixin, w_out, "mm_out")
    x2, h2 = _resid_norm_mod(x, mix, gate1, norm2_g, scale2, shift2, "resid_norm_mod2")
    w_gu = get_w("w_gu", h2)
    a_ff, u_ff, act = _mm_gate_up(h2, w_gu, "mm_gu")
    w_down = get_w("w_down", act)
    ffn = _mm_nn(act, w_down, "mm_down")
    dx3, dffn, loss_v, dfg, dgate2 = _final_loss(x2, ffn, gate2, fg, tgt, "final_loss")

    put_g("w_down", *_mm_tn(act, dffn, 1, "mm_down_dw", tk=D_FF // 2))
    dau = _mm_down_dx(dffn, w_down, a_ff, u_ff, "mm_down_dx")
    put_g("w_gu", *_mm_tn(h2, dau, N_SHARD, "mm_gu_dw"))
    dh2 = _mm_nt(dau, w_gu, "mm_gu_dx", tm=1024)
    dx2, dshift2, dscale2, dg2, dgate1, dmix = _norm_mod_bwd(
        dh2, x2, norm2_g, scale2, dx3, "norm_mod2_bwd", gate=gate1, mix=mix)
    put_g("w_out", *_mm_tn(mixin, dmix, 1, "mm_out_dw"))
    dmixin = _mm_nt(dmix, w_out, "mm_out_dx", tm=1024)
    do_hg, dg_raw, datt, dd, dog, dag = _mix_in_bwd(dmixin, o_hg, proj, att, og, ag, "mix_in_bwd")
    datt_b = datt.astype(BF16)
    datts = [_attn_bwd(q, _to_sub(datt_b, d), _to_sub(lse, d), _to_sub(dd, d), d, f"attn_bwd{d}")
             for q, d in zip(qkvs, DILATIONS)]
    dhg, dlb = _hgrn_bwd(proj, lb_logits, states, do_hg, "hgrn_bwd")
    dproj = _dproj(dhg, dg_raw, [_from_sub(dq, d) for (dq, _), d in zip(datts, DILATIONS)],
                   [_from_sub(dkv, d) for (_, dkv), d in zip(datts, DILATIONS)], "dproj")
    put_g("w_in", *_mm_tn(h1, dproj, N_SHARD, "mm_in_dw"))
    dh1 = _mm_nt(dproj, w_in, "mm_in_dx", tm=1024)
    dx, dshift1, dscale1, dg1 = _norm_mod_bwd(dh1, x, norm1_g, scale1, dx2, "norm_mod1_bwd")

    stats = jnp.concatenate([loss_v, dfg, dg2, dg1, dlb, dag, dog,
                             dshift1, dscale1, dgate1, dshift2, dscale2, dgate2], axis=1)
    return dx, stats


def _place():
    x, y, c = lax.axis_index("x"), lax.axis_index("y"), lax.axis_index("c")
    return x, y, c


def _chip_peers(x, y, c):
    return [(1 - x, y, c), (x, 1 - y, c), (1 - x, 1 - y, c)]


def _comm_call(body, name, n_in, out_shape, scratch_shapes):
    hbm = pl.BlockSpec(memory_space=pl.ANY)
    return pl.pallas_call(body, name=name, in_specs=[hbm] * n_in, out_specs=[hbm] * len(out_shape),
                          out_shape=out_shape, scratch_shapes=scratch_shapes)


_HBM = pl.BlockSpec(memory_space=pltpu.HBM)
_SEM = pl.BlockSpec(memory_space=pltpu.SEMAPHORE)
_EFFECT = pltpu.SideEffectType.DATAFLOW_SIDE_EFFECTING


def _exchange_copy(bufs, send, recv, j, peer, place, gather):
    if gather:
        src = dst = bufs[0].at[2 * place[0] + place[1]]
    else:
        src, dst = bufs[0].at[2 * peer[0] + peer[1]], bufs[1].at[j]
    return pltpu.make_async_remote_copy(src_ref=src, dst_ref=dst, send_sem=send.at[j], recv_sem=recv.at[j],
                                        device_id=peer, device_id_type=MESH)


def _exchange_start(groups, after, gather, name):
    sizes = [len(g) for g in groups]
    flat = [b for g in groups for b in g]
    ng, nb = len(groups), len(flat)

    def body(*refs):
        bufs, sems = refs[:nb], refs[nb + 1:nb + 1 + 2 * ng]
        x, y, c = _place()
        for j, peer in enumerate(_chip_peers(x, y, c)):
            at = 0
            for i, size in enumerate(sizes):
                _exchange_copy(bufs[at:at + size], sems[2 * i], sems[2 * i + 1], j, peer, (x, y), gather).start()
                at += size

    any_space = pl.BlockSpec(memory_space=pl.ANY)
    out = pl.pallas_call(
        body, name=name, in_specs=[_HBM] * nb + [any_space],
        out_specs=[_SEM] * (2 * ng) + [_HBM] * nb + [any_space],
        out_shape=[pltpu.SemaphoreType.DMA((3,))] * (2 * ng) + [pltpu.HBM(b.shape, b.dtype) for b in flat]
        + [_sds(after.shape, after.dtype)],
        input_output_aliases={i: 2 * ng + i for i in range(nb + 1)},
        compiler_params=pltpu.CompilerParams(has_side_effects=_EFFECT),
    )(*[pltpu.with_memory_space_constraint(b, pltpu.HBM) for b in flat], after)
    started, at = [], 2 * ng
    for i, size in enumerate(sizes):
        started.append((out[2 * i], out[2 * i + 1], tuple(out[at:at + size])))
        at += size
    return started, out[-1]


def _exchange_wait(started, after, gather, name):
    send, recv, bufs = started
    nb = len(bufs)

    def body(*refs):
        x, y, c = _place()
        for j, peer in enumerate(_chip_peers(x, y, c)):
            cp = _exchange_copy(refs[:nb], refs[nb], refs[nb + 1], j, peer, (x, y), gather)
            cp.wait_send()
            cp.wait_recv()

    return pl.pallas_call(
        body, name=name, in_specs=[_HBM] * nb + [_SEM, _SEM, pl.BlockSpec(memory_space=pl.ANY)],
        out_specs=[_HBM] * nb, out_shape=[pltpu.HBM(b.shape, b.dtype) for b in bufs],
        input_output_aliases={i: i for i in range(nb)},
        compiler_params=pltpu.CompilerParams(has_side_effects=_EFFECT),
    )(*bufs, send, recv, after)


def _swap_sibling(vs, name):
    n = len(vs)

    def body(*refs):
        v_refs, o_refs, (send, recv) = refs[:n], refs[n:2 * n], refs[2 * n:]
        x, y, c = _place()
        cps = [pltpu.make_async_remote_copy(
            src_ref=v_refs[a], dst_ref=o_refs[a], send_sem=send.at[a], recv_sem=recv.at[a],
            device_id=(x, y, 1 - c), device_id_type=MESH) for a in range(n)]
        for cp in cps:
            cp.start()
        for cp in cps:
            cp.wait()

    return _comm_call(body, name, n, [_sds(v.shape, v.dtype) for v in vs],
                      [pltpu.SemaphoreType.DMA((n,)), pltpu.SemaphoreType.DMA((n,))])(*vs)


def _gather_rows(v, name):
    r, n = v.shape

    def body(v_ref, o_ref, send, recv, loc):
        x, y, c = _place()
        me = 4 * x + 2 * y + c
        own = pltpu.make_async_copy(v_ref, o_ref.at[me], loc)
        own.start()
        peers = []
        for k in range(1, 8):
            px = 1 - x if k & 4 else x
            py = 1 - y if k & 2 else y
            pc = 1 - c if k & 1 else c
            peers.append((px, py, pc))
        sends = []
        for k, peer in enumerate(peers):
            cp = pltpu.make_async_remote_copy(src_ref=v_ref, dst_ref=o_ref.at[me], send_sem=send.at[k],
                                              recv_sem=recv.at[k], device_id=peer, device_id_type=MESH)
            cp.start()
            sends.append(cp)
        for k, peer in enumerate(peers):
            pltpu.make_async_remote_copy(src_ref=v_ref, dst_ref=o_ref.at[4 * peer[0] + 2 * peer[1] + peer[2]],
                                         send_sem=send.at[k], recv_sem=recv.at[k], device_id=peer,
                                         device_id_type=MESH).wait_recv()
        for cp in sends:
            cp.wait_send()
        own.wait()

    vmem = pl.BlockSpec(memory_space=pltpu.VMEM)
    return pl.pallas_call(body, name=name, in_specs=[vmem], out_specs=vmem, out_shape=_sds((8, r, n), v.dtype),
                          scratch_shapes=[pltpu.SemaphoreType.DMA((7,)), pltpu.SemaphoreType.DMA((7,)),
                                          pltpu.SemaphoreType.DMA])(v)


def _cast_place(w, shard, name):
    r, c = w.shape
    tr = r // 4

    def body(s_ref, w_ref, o_ref):
        o_ref[0] = w_ref[...].astype(BF16)

    return pl.pallas_call(
        body, name=name, out_shape=_sds((N_SHARD, r, c), BF16),
        grid_spec=pltpu.PrefetchScalarGridSpec(
            num_scalar_prefetch=1, grid=(4,), in_specs=[pl.BlockSpec((tr, c), lambda i, s: (i, 0))],
            out_specs=pl.BlockSpec((1, tr, c), lambda i, s: (s[0], i, 0))),
        compiler_params=pltpu.CompilerParams(dimension_semantics=("arbitrary",)),
    )(shard.reshape(1).astype(jnp.int32), w)


def _mod_part(c_all, w_ada, b_ada, name):
    n = w_ada.shape[1]

    def body(c_ref, w_ref, b_ref, a_ref, p_ref):
        cv = c_ref[...]
        ca = cv * _sigmoid(cv)
        a_ref[...] = ca
        p_ref[...] = jnp.dot(ca, w_ref[...], precision=lax.Precision.HIGHEST, preferred_element_type=F32) + b_ref[...]

    full = lambda a: pl.BlockSpec(a.shape, lambda i: (0, 0))
    return _call(body, name=name, grid=(1,), in_specs=[full(c_all), full(w_ada), full(b_ada)],
                 out_specs=[pl.BlockSpec((8, D_MODEL), lambda i: (0, 0)), pl.BlockSpec((8, n), lambda i: (0, 0))],
                 out_shape=[_sds((8, D_MODEL)), _sds((8, n))])(c_all, w_ada, b_ada)


def _sum_received(g_own, land, name):
    r, c = g_own.shape
    tr = r // 4

    def body(g_ref, l_ref, o_ref):
        o_ref[...] = ((g_ref[...] + l_ref[0].astype(F32)) + l_ref[1].astype(F32)) + l_ref[2].astype(F32)

    return _call(body, name=name, grid=(4,),
                 in_specs=[_rows(tr, c), pl.BlockSpec((3, tr, c), lambda i: (0, i, 0))],
                 out_specs=_rows(tr, c), out_shape=_sds((r, c)))(g_own, land)


def _outer_sum(ct, dm, name):
    k, n = ct.shape[0], dm.shape[1]
    tr = k // 4

    def body(c_ref, d_ref, o_ref):
        cv = c_ref[...]
        dv = d_ref[...]
        acc = cv[:, 0:1] * dv[0:1, :]
        for i in range(1, 8):
            acc = acc + cv[:, i:i + 1] * dv[i:i + 1, :]
        o_ref[...] = acc

    return _call(body, name=name, grid=(4,), in_specs=[_rows(tr, 8), pl.BlockSpec((8, n), lambda i: (0, 0))],
                 out_specs=_rows(tr, n), out_shape=_sds((k, n)))(ct, dm)


def _small_grads(stats, lb_logits, name):
    def body(s_ref, lg_ref, g_ref, l_ref):
        tot = s_ref[0:1, :]
        for i in range(1, 8):
            tot = tot + s_ref[i:i + 1, :]
        part = lambda off, n: tot[:, off:off + n]
        l_ref[...] = jnp.zeros((1, 128), F32) + (0.5 / D_MODEL) * jnp.sum(part(ST_LOSS, D_MODEL))
        lg = lg_ref[...]
        lb = _sigmoid(lg[0:1] - lg[1:2])
        dl0 = part(ST_DLB, HG_WIDTH) * lb * (1.0 - lb)
        g_ref[:, SP_BADA:SP_BADA + 6 * D_MODEL] = part(ST_DMOD, 6 * D_MODEL)
        g_ref[:, SP_N1:SP_N1 + D_MODEL] = part(ST_DG1, D_MODEL)
        g_ref[:, SP_LB:SP_LB + HG_WIDTH] = dl0
        g_ref[:, SP_LB + HG_WIDTH:SP_LB + 2 * HG_WIDTH] = -dl0
        g_ref[:, SP_OG:SP_OG + HG_HEAD] = part(ST_DOG, HG_HEAD)
        g_ref[:, SP_AG:SP_AG + ATT_WIDTH] = part(ST_DAG, ATT_WIDTH)
        g_ref[:, SP_N2:SP_N2 + D_MODEL] = part(ST_DG2, D_MODEL)
        g_ref[:, SP_FG:SP_FG + D_MODEL] = part(ST_DFG, D_MODEL)

    return _call(body, name=name, grid=(1,),
                 in_specs=[pl.BlockSpec((8, ST_WIDTH), lambda i: (0, 0)), pl.BlockSpec((2, HG_WIDTH), lambda i: (0, 0))],
                 out_specs=[pl.BlockSpec((1, SP_WIDTH), lambda i: (0, 0)), pl.BlockSpec((1, 128), lambda i: (0, 0))],
                 out_shape=[_sds((1, SP_WIDTH)), _sds((1, 128))])(stats, lb_logits)


def _adamw(w, gs, m, v, name, steps=4):
    r, c = w.shape
    tr = r // steps
    ng = len(gs)

    def body(*refs):
        w_ref, g_refs, (m_ref, v_ref, g_out, d_out, m_out, v_out) = refs[0], refs[1:1 + ng], refs[1 + ng:]
        g = g_refs[0][...]
        for g_ref in g_refs[1:]:
            g = g + g_ref[...]
        m_new = ADAM_B1 * m_ref[...] + (1.0 - ADAM_B1) * g
        v_new = ADAM_B2 * v_ref[...] + (1.0 - ADAM_B2) * (g * g)
        m_hat = m_new / (1.0 - ADAM_B1 ** ADAM_STEP)
        v_hat = v_new / (1.0 - ADAM_B2 ** ADAM_STEP)
        g_out[...] = g
        d_out[...] = -ADAM_LR * (m_hat / (jnp.sqrt(v_hat) + ADAM_EPS) + ADAM_WD * w_ref[...])
        m_out[...] = m_new
        v_out[...] = v_new

    row = _rows(tr, c)
    return _call(body, name=name, grid=(steps,), in_specs=[row] * (3 + ng), out_specs=[row] * 4,
                 out_shape=[_sds((r, c))] * 4)(w, *gs, m, v)


def kernel(x, c, w_ada, b_ada, norm1_g, w_in, hg_lb_logits, hg_onorm_g, att_onorm_g, w_out, norm2_g, w_gate_up, w_down, final_g, loss_target, m_w_ada, m_b_ada, m_norm1_g, m_w_in, m_hg_lb_logits, m_hg_onorm_g, m_att_onorm_g, m_w_out, m_norm2_g, m_w_gate_up, m_w_down, m_final_g, v_w_ada, v_b_ada, v_norm1_g, v_w_in, v_hg_lb_logits, v_hg_onorm_g, v_att_onorm_g, v_w_out, v_norm2_g, v_w_gate_up, v_w_down, v_final_g):
    ix, iy, ic = _place()
    shard = 2 * ix + iy
    sample = 4 * ix + 2 * iy + ic
    n_ada = w_ada.shape[2]

    shards = [w_in[0], w_out[0], w_gate_up[0], w_down[0]]
    names = ["w_in", "w_out", "w_gu", "w_down"]
    shapes = [(N_SHARD,) + w.shape for w in shards]
    placed = [(_cast_place(w, shard, "place_" + nm),) for w, nm in zip(shards, names)]

    c_all = _gather_rows(c, "gather_c").reshape(8, D_MODEL)
    b_part = lax.dynamic_slice(b_ada, (0, shard * n_ada), (1, n_ada))
    c_act, part = _mod_part(c_all, w_ada[0], b_part, "mod_part")
    parts = _gather_rows(part, "gather_mod")[::2]
    mod = lax.dynamic_index_in_dim(parts, sample, axis=1, keepdims=False).reshape(1, 6 * D_MODEL)
    (first,), mod = _exchange_start(placed[:1], mod, True, "gather_start_w_in")
    gathering = {"w_in": first}

    def get_w(name, after):
        (full,) = _exchange_wait(gathering[name], after, True, "gather_wait_" + name)
        if name == "w_in":
            rest, full = _exchange_start(placed[1:], full, True, "gather_start_rest")
            gathering.update(zip(names[1:], rest))
        return full if name in ("w_in", "w_gu") else full.reshape(1, -1, D_MODEL)

    scattering = {}

    def put_g(name, g, g_bf16):
        shape = shapes[names.index(name)]
        land = lax.empty((3,) + shape[1:], BF16)
        (started,), g = _exchange_start([(g_bf16.reshape(shape), land)], g, False, "scatter_start_" + name)
        scattering[name] = (g.reshape(shape), started)

    dx, stats = _local_step(x[0], loss_target[0], mod, norm1_g, hg_lb_logits, hg_onorm_g, att_onorm_g,
                            norm2_g, final_g, get_w, put_g)

    def summed(name, after):
        g, started = scattering[name]
        _, land = _exchange_wait(started, after, False, "scatter_wait_" + name)
        own = lax.dynamic_index_in_dim(g, shard, axis=0, keepdims=False)
        return _sum_received(own, land, "sum_" + name)

    stats_all = _gather_rows(stats, "gather_stats").reshape(8, ST_WIDTH)
    g_small, loss = _small_grads(stats_all, hg_lb_logits, "small_grads")
    dmod = lax.dynamic_slice(stats_all, (0, ST_DMOD + shard * n_ada), (8, n_ada))
    g_ada = _outer_sum(c_act.T, dmod, "w_ada_grad")

    smalls = [(b_ada, m_b_ada, v_b_ada), (norm1_g, m_norm1_g, v_norm1_g),
              (hg_lb_logits, m_hg_lb_logits, v_hg_lb_logits), (hg_onorm_g, m_hg_onorm_g, v_hg_onorm_g),
              (att_onorm_g, m_att_onorm_g, v_att_onorm_g), (norm2_g, m_norm2_g, v_norm2_g),
              (final_g, m_final_g, v_final_g)]
    pack = lambda i: jnp.concatenate([t[i].reshape(1, -1) for t in smalls], axis=1)
    small_out = _adamw(pack(0), [g_small], pack(1), pack(2), "adamw_small", steps=1)
    offs = [SP_BADA, SP_N1, SP_LB, SP_OG, SP_AG, SP_N2, SP_FG, SP_WIDTH]
    unpack = lambda a: [a[0, offs[i]:offs[i + 1]].reshape(smalls[i][0].shape) for i in range(7)]
    sg, sd, sm, sv = [unpack(a) for a in small_out]

    ada = _adamw(w_ada[0], [g_ada], m_w_ada[0], v_w_ada[0], "adamw_w_ada")
    moments = [(m_w_in, v_w_in), (m_w_out, v_w_out), (m_w_gate_up, v_w_gate_up), (m_w_down, v_w_down)]

    def update(group, after, tag):
        sums = [summed(nm, after) for nm in group]
        other = _swap_sibling(sums, "swap_sums_" + tag)
        return {nm: _adamw(shards[names.index(nm)], [s, o], moments[names.index(nm)][0][0],
                           moments[names.index(nm)][1][0], "adamw_" + nm) for nm, s, o in zip(group, sums, other)}

    done = update(["w_down", "w_gu", "w_out"], ada[1], "late")
    done.update(update(["w_in"], done["w_out"][1], "in"))
    big = [ada] + [done[nm] for nm in names]
    bg, bd, bm, bv = [[t[i][None] for t in big] for i in range(4)]

    def order(b, s):
        return [b[0], s[0], s[1], b[1], s[2], s[3], s[4], b[2], s[5], b[3], b[4], s[6]]

    return (loss[0, 0], dx[None], *order(bg, sg), *order(bd, sd), *order(bm, sm), *order(bv, sv))
```

```python
import functools

import jax
import jax.numpy as jnp
from jax import lax
from jax.experimental import pallas as pl
from jax.experimental.pallas import tpu as pltpu

F32 = jnp.float32
BF16 = jnp.bfloat16
MESH = pl.DeviceIdType.MESH

D_MODEL = 1024
HG_WIDTH = 512
HG_HEAD = 128
HG_CHUNK = 64
HG_GROUP = 4
ATT_WIDTH = 512
ATT_HEADS = 8
ATT_BLOCK = 128
DILATIONS = (1, 4, 16)
D_FF = 2816
IN_WIDTH = 3584
N_SHARD = 4
RMS_EPS = 1e-6
NEG = -1e30

ADAM_LR = 0.001
ADAM_B1 = 0.9
ADAM_B2 = 0.999
ADAM_EPS = 1e-08
ADAM_WD = 0.01
ADAM_STEP = 10

VMEM_LIMIT = 56 * 2**20

ST_LOSS, ST_DFG, ST_DG2, ST_DG1 = 0, 1024, 2048, 3072
ST_DLB, ST_DAG, ST_DOG, ST_DMOD = 4096, 4608, 5120, 5248
ST_WIDTH = 5248 + 6144
SP_BADA, SP_N1, SP_LB, SP_OG, SP_AG, SP_N2, SP_FG = 0, 6144, 7168, 8192, 8320, 8832, 9856
SP_WIDTH = 10880


def _call(body, *, name, grid, in_specs, out_specs, out_shape, scratch_shapes=()):
    return pl.pallas_call(
        body, name=name, grid=grid, in_specs=in_specs, out_specs=out_specs, out_shape=out_shape,
        scratch_shapes=list(scratch_shapes),
        compiler_params=pltpu.CompilerParams(
            dimension_semantics=("arbitrary",) * len(grid), vmem_limit_bytes=VMEM_LIMIT))


def _sds(shape, dtype=F32):
    return jax.ShapeDtypeStruct(shape, dtype)


def _dot(a, b):
    return jnp.dot(a, b, preferred_element_type=F32)


def _dot_nt(a, b):
    return lax.dot_general(a, b, (((1,), (1,)), ((), ())), preferred_element_type=F32)


def _dot_tn(a, b):
    return lax.dot_general(a, b, (((0,), (0,)), ((), ())), preferred_element_type=F32)


def _sigmoid(x):
    return 1.0 / (1.0 + jnp.exp(-x))


def _rows(tr, width):
    return pl.BlockSpec((tr, width), lambda i: (i, 0))


def _vec(width):
    return pl.BlockSpec((1, width), lambda i: (0, 0))


def _acc(ref, val, first):
    @pl.when(first)
    def _():
        ref[...] = val

    @pl.when(jnp.logical_not(first))
    def _():
        ref[...] += val


def _mm_nn(a, b3, name, tm=1024, out_dtype=F32):
    m, k = a.shape
    s, _, n = b3.shape

    def body(a_ref, b_ref, o_ref):
        o_ref[...] = _dot(a_ref[...], b_ref[0]).astype(out_dtype)

    return _call(
        body, name=name, grid=(s, m // tm),
        in_specs=[pl.BlockSpec((tm, k), lambda j, i: (i, 0)), pl.BlockSpec((1, k, n), lambda j, i: (j, 0, 0))],
        out_specs=pl.BlockSpec((tm, n), lambda j, i: (i, j)), out_shape=_sds((m, s * n), out_dtype))(a, b3)


def _mm_nt(dy, b3, name, tm=512):
    m = dy.shape[0]
    s, k, n = b3.shape

    def body(dy_ref, b_ref, o_ref):
        acc = _dot_nt(dy_ref[:, 0:n], b_ref[0])
        for j in range(1, s):
            acc = acc + _dot_nt(dy_ref[:, j * n:(j + 1) * n], b_ref[j])
        o_ref[...] = acc

    return _call(
        body, name=name, grid=(m // tm,),
        in_specs=[_rows(tm, s * n), pl.BlockSpec((s, k, n), lambda i: (0, 0, 0))],
        out_specs=_rows(tm, k), out_shape=_sds((m, k)))(dy, b3)


def _mm_tn(a, dy, s, name, tm, tk):
    m, k = a.shape
    n = dy.shape[1] // s
    steps = m // tm

    def body(a_ref, dy_ref, o_ref, ob_ref):
        p = _dot_tn(a_ref[...], dy_ref[...])[None]
        if steps == 1:
            o_ref[...] = p
            ob_ref[...] = p.astype(BF16)
        else:
            i = pl.program_id(2)
            _acc(o_ref, p, i == 0)

            @pl.when(i == steps - 1)
            def _():
                ob_ref[...] = o_ref[...].astype(BF16)

    out = pl.BlockSpec((1, tk, n), lambda kk, j, i: (j, kk, 0))
    return _call(
        body, name=name, grid=(k // tk, s, steps),
        in_specs=[pl.BlockSpec((tm, tk), lambda kk, j, i: (i, kk)), pl.BlockSpec((tm, n), lambda kk, j, i: (i, j))],
        out_specs=[out, out], out_shape=[_sds((s, k, n)), _sds((s, k, n), BF16)])(a, dy)


def _rms(x):
    return lax.rsqrt(jnp.mean(x * x, axis=-1, keepdims=True) + RMS_EPS)


def _rms_bwd(dxh, xh, r):
    return r * (dxh - xh * jnp.mean(dxh * xh, axis=-1, keepdims=True))


def _norm_mod(x, g, scale, shift, name, tr=512):
    t = x.shape[0]

    def body(x_ref, g_ref, sc_ref, sh_ref, h_ref):
        xv = x_ref[...]
        n = xv * _rms(xv) * g_ref[...]
        h_ref[...] = (n * (1.0 + sc_ref[...]) + sh_ref[...]).astype(BF16)

    return _call(body, name=name, grid=(t // tr,),
                 in_specs=[_rows(tr, D_MODEL), _vec(D_MODEL), _vec(D_MODEL), _vec(D_MODEL)],
                 out_specs=_rows(tr, D_MODEL), out_shape=_sds((t, D_MODEL), BF16))(x, g, scale, shift)


def _mix_in(o_hg, proj, att, og, ag, name, tr=256):
    t = o_hg.shape[0]

    def body(o_ref, g_ref, a_ref, og_ref, ag_ref, m_ref):
        for h in range(HG_WIDTH // HG_HEAD):
            sl = slice(h * HG_HEAD, (h + 1) * HG_HEAD)
            oh = o_ref[:, sl]
            gv = g_ref[:, sl].astype(F32)
            m_ref[:, sl] = (oh * _rms(oh) * og_ref[...] * (gv * _sigmoid(gv))).astype(BF16)
        av = a_ref[...]
        m_ref[:, HG_WIDTH:] = (av * _rms(av) * ag_ref[...]).astype(BF16)

    return _call(body, name=name, grid=(t // tr,),
                 in_specs=[_rows(tr, HG_WIDTH), pl.BlockSpec((tr, HG_WIDTH), lambda i: (i, 3)), _rows(tr, ATT_WIDTH),
                           _vec(HG_HEAD), _vec(ATT_WIDTH)],
                 out_specs=_rows(tr, D_MODEL), out_shape=_sds((t, D_MODEL), BF16))(o_hg, proj, att, og, ag)


def _resid_norm_mod(x, mix, gate, g, scale, shift, name, tr=256):
    t = x.shape[0]

    def body(x_ref, m_ref, gt_ref, g_ref, sc_ref, sh_ref, x2_ref, h_ref):
        x2 = x_ref[...] + gt_ref[...] * m_ref[...]
        x2_ref[...] = x2
        n = x2 * _rms(x2) * g_ref[...]
        h_ref[...] = (n * (1.0 + sc_ref[...]) + sh_ref[...]).astype(BF16)

    return _call(body, name=name, grid=(t // tr,),
                 in_specs=[_rows(tr, D_MODEL), _rows(tr, D_MODEL)] + [_vec(D_MODEL)] * 4,
                 out_specs=[_rows(tr, D_MODEL), _rows(tr, D_MODEL)],
                 out_shape=[_sds((t, D_MODEL)), _sds((t, D_MODEL), BF16)])(x, mix, gate, g, scale, shift)


def _mm_gate_up(h, w_gu, name, tm=1024):
    m, k = h.shape
    n = w_gu.shape[2]

    def body(h_ref, wa_ref, wu_ref, a_ref, u_ref, o_ref):
        hv = h_ref[...]
        a = _dot(hv, wa_ref[0])
        u = _dot(hv, wu_ref[0])
        a_ref[...] = a.astype(BF16)
        u_ref[...] = u.astype(BF16)
        o_ref[...] = (a * _sigmoid(a) * u).astype(BF16)

    out = pl.BlockSpec((tm, n), lambda j, i: (i, j))
    return _call(body, name=name, grid=(2, m // tm),
                 in_specs=[pl.BlockSpec((tm, k), lambda j, i: (i, 0)), pl.BlockSpec((1, k, n), lambda j, i: (j, 0, 0)),
                           pl.BlockSpec((1, k, n), lambda j, i: (j + 2, 0, 0))],
                 out_specs=[out, out, out], out_shape=[_sds((m, 2 * n), BF16)] * 3)(h, w_gu, w_gu)


def _mm_down_dx(dffn, w_down, a, u, name, tm=512):
    m = dffn.shape[0]
    _, k, n = w_down.shape

    def body(d_ref, w_ref, a_ref, u_ref, o_ref):
        dact = _dot_nt(d_ref[...], w_ref[0])
        av = a_ref[...].astype(F32)
        sg = _sigmoid(av)
        o_ref[:, :k] = (dact * u_ref[...].astype(F32) * sg * (1.0 + av * (1.0 - sg))).astype(BF16)
        o_ref[:, k:] = (dact * av * sg).astype(BF16)

    return _call(body, name=name, grid=(m // tm,),
                 in_specs=[_rows(tm, n), pl.BlockSpec((1, k, n), lambda i: (0, 0, 0)), _rows(tm, k), _rows(tm, k)],
                 out_specs=_rows(tm, 2 * k), out_shape=_sds((m, 2 * k), BF16))(dffn, w_down, a, u)


def _final_loss(x2, ffn, gate, fg, tgt, name, tr=256):
    t = x2.shape[0]

    def body(x_ref, f_ref, gt_ref, fg_ref, t_ref, dx_ref, df_ref, l_ref, dfg_ref, dgt_ref):
        first = pl.program_id(0) == 0
        ffn_v = f_ref[...]
        x3 = x_ref[...] + gt_ref[...] * ffn_v
        r = _rms(x3)
        xh = x3 * r
        err = xh * fg_ref[...] - t_ref[...]
        dy = err * (1.0 / D_MODEL)
        dx3 = _rms_bwd(dy * fg_ref[...], xh, r)
        dx_ref[...] = dx3
        df_ref[...] = (dx3 * gt_ref[...]).astype(BF16)
        _acc(l_ref, jnp.sum(err * err, axis=0, keepdims=True), first)
        _acc(dfg_ref, jnp.sum(dy * xh, axis=0, keepdims=True), first)
        _acc(dgt_ref, jnp.sum(dx3 * ffn_v, axis=0, keepdims=True), first)

    row, vec = _rows(tr, D_MODEL), _vec(D_MODEL)
    return _call(body, name=name, grid=(t // tr,), in_specs=[row, row, vec, vec, row],
                 out_specs=[row, row, vec, vec, vec],
                 out_shape=[_sds((t, D_MODEL)), _sds((t, D_MODEL), BF16)] + [_sds((1, D_MODEL))] * 3)(
                     x2, ffn, gate, fg, tgt)


def _norm_mod_bwd(dh, x, g, scale, dres, name, gate=None, mix=None, tr=256):
    t = x.shape[0]
    below = gate is not None

    def body(*refs):
        if below:
            dh_ref, x_ref, g_ref, sc_ref, dr_ref, gt_ref, m_ref, dx_ref, dsh_ref, dsc_ref, dg_ref, dgt_ref, dm_ref = refs
        else:
            dh_ref, x_ref, g_ref, sc_ref, dr_ref, dx_ref, dsh_ref, dsc_ref, dg_ref = refs
        first = pl.program_id(0) == 0
        xv = x_ref[...]
        dhv = dh_ref[...]
        r = _rms(xv)
        xh = xv * r
        dn = dhv * (1.0 + sc_ref[...])
        dx = dr_ref[...] + _rms_bwd(dn * g_ref[...], xh, r)
        dx_ref[...] = dx
        _acc(dsh_ref, jnp.sum(dhv, axis=0, keepdims=True), first)
        _acc(dsc_ref, jnp.sum(dhv * xh * g_ref[...], axis=0, keepdims=True), first)
        _acc(dg_ref, jnp.sum(dn * xh, axis=0, keepdims=True), first)
        if below:
            _acc(dgt_ref, jnp.sum(dx * m_ref[...], axis=0, keepdims=True), first)
            dm_ref[...] = (dx * gt_ref[...]).astype(BF16)

    row, vec = _rows(tr, D_MODEL), _vec(D_MODEL)
    in_specs = [row, row, vec, vec, row] + ([vec, row] if below else [])
    out_specs = [row, vec, vec, vec] + ([vec, row] if below else [])
    out_shape = [_sds((t, D_MODEL))] + [_sds((1, D_MODEL))] * 3 + ([_sds((1, D_MODEL)), _sds((t, D_MODEL), BF16)] if below else [])
    args = (dh, x, g, scale, dres) + ((gate, mix) if below else ())
    return _call(body, name=name, grid=(t // tr,), in_specs=in_specs, out_specs=out_specs, out_shape=out_shape)(*args)


def _mix_in_bwd(dmi, o_hg, proj, att, og, ag, name, tr=256):
    t = o_hg.shape[0]

    def body(d_ref, o_ref, g_ref, a_ref, og_ref, ag_ref, do_ref, dg_ref, da_ref, dd_ref, dog_ref, dag_ref):
        first = pl.program_id(0) == 0
        dog = jnp.zeros((1, HG_HEAD), F32)
        for h in range(HG_WIDTH // HG_HEAD):
            sl = slice(h * HG_HEAD, (h + 1) * HG_HEAD)
            oh = o_ref[:, sl]
            gv = g_ref[:, sl].astype(F32)
            dv = d_ref[:, sl]
            r = _rms(oh)
            xh = oh * r
            sg = _sigmoid(gv)
            dno = dv * gv * sg
            dg_ref[:, sl] = dv * xh * og_ref[...] * sg * (1.0 + gv * (1.0 - sg))
            dog = dog + jnp.sum(dno * xh, axis=0, keepdims=True)
            do_ref[:, sl] = _rms_bwd(dno * og_ref[...], xh, r)
        _acc(dog_ref, dog, first)
        av = a_ref[...]
        dav = d_ref[:, HG_WIDTH:]
        r = _rms(av)
        xa = av * r
        _acc(dag_ref, jnp.sum(dav * xa, axis=0, keepdims=True), first)
        datt = _rms_bwd(dav * ag_ref[...], xa, r)
        da_ref[...] = datt
        prod = datt * av
        lane = lax.broadcasted_iota(jnp.int32, (1, 128), 1)
        dd = jnp.zeros((tr, 128), F32)
        for hp in range(ATT_HEADS // 2):
            pp = prod[:, hp * 128:(hp + 1) * 128]
            lo = jnp.sum(jnp.where(lane < 64, pp, 0.0), axis=-1, keepdims=True)
            hi = jnp.sum(jnp.where(lane >= 64, pp, 0.0), axis=-1, keepdims=True)
            dd = jnp.where(lane == 2 * hp, lo, dd)
            dd = jnp.where(lane == 2 * hp + 1, hi, dd)
        dd_ref[...] = dd

    half = _rows(tr, HG_WIDTH)
    return _call(body, name=name, grid=(t // tr,),
                 in_specs=[_rows(tr, D_MODEL), half, pl.BlockSpec((tr, HG_WIDTH), lambda i: (i, 3)), half,
                           _vec(HG_HEAD), _vec(ATT_WIDTH)],
                 out_specs=[half, half, half, _rows(tr, 128), _vec(HG_HEAD), _vec(ATT_WIDTH)],
                 out_shape=[_sds((t, HG_WIDTH))] * 3 + [_sds((t, 128)), _sds((1, HG_HEAD)), _sds((1, ATT_WIDTH))])(
                     dmi, o_hg, proj, att, og, ag)


def _dproj(dhg, dg, dqs, dkvs, name, tr=256):
    t = dhg.shape[0]
    w3 = 3 * HG_WIDTH
    w4 = w3 + HG_WIDTH
    nbr = len(dqs)

    def body(*refs):
        h_ref, g_ref, q_refs, kv_refs, o_ref = refs[0], refs[1], refs[2:2 + nbr], refs[2 + nbr:2 + 2 * nbr], refs[-1]
        o_ref[:, :w3] = h_ref[...]
        o_ref[:, w3:w4] = g_ref[...].astype(BF16)
        o_ref[:, w4:w4 + ATT_WIDTH] = sum(r[...].astype(F32) for r in q_refs).astype(BF16)
        o_ref[:, w4 + ATT_WIDTH:] = sum(r[...].astype(F32) for r in kv_refs).astype(BF16)

    return _call(body, name=name, grid=(t // tr,),
                 in_specs=[_rows(tr, w3), _rows(tr, HG_WIDTH)] + [_rows(tr, ATT_WIDTH)] * nbr
                 + [_rows(tr, 2 * ATT_WIDTH)] * nbr,
                 out_specs=_rows(tr, IN_WIDTH), out_shape=_sds((t, IN_WIDTH), BF16))(dhg, dg, *dqs, *dkvs)


def _chunk_tri(upper):
    row = lax.broadcasted_iota(jnp.int32, (HG_GROUP, HG_CHUNK, HG_CHUNK), 1)
    col = lax.broadcasted_iota(jnp.int32, (HG_GROUP, HG_CHUNK, HG_CHUNK), 2)
    return (row <= col if upper else row >= col).astype(BF16)


def _chunk_cumsum(x, tri):
    x3 = x.reshape(HG_GROUP, HG_CHUNK, x.shape[1])
    dims = (((2,), (1,)), ((0,), (0,)))
    out = None
    for _ in range(3):
        part = x3.astype(BF16)
        x3 = x3 - part.astype(F32)
        term = lax.dot_general(tri, part, dims, preferred_element_type=F32)
        out = term if out is None else out + term
    return out.reshape(x.shape)


def _hg_gates(f_raw, q_raw, lb, tri):
    sg = _sigmoid(f_raw)
    f = lb + (1.0 - lb) * sg
    k = 1.0 - f
    b = _chunk_cumsum(jnp.log(f), tri)
    sq = _sigmoid(q_raw)
    return sg, f, k, b, sq


def _hg_masks(rows):
    row = lax.broadcasted_iota(jnp.int32, (rows, rows), 0)
    col = lax.broadcasted_iota(jnp.int32, (rows, rows), 1)
    same = (row // HG_CHUNK) == (col // HG_CHUNK)
    return jnp.logical_and(row >= col, same), jnp.logical_and(row <= col, same)


def _per_chunk(rows_of):
    return jnp.concatenate([jnp.broadcast_to(r, (HG_CHUNK, r.shape[1])) for r in rows_of], axis=0)


def _hgrn_fwd(proj, lb_logits, name):
    t = proj.shape[0]
    nc = t // HG_CHUNK
    nh = HG_WIDTH // HG_HEAD
    rows = HG_GROUP * HG_CHUNK

    def body(q_ref, f_ref, i_ref, lg_ref, o_ref, st_ref, s_scr):
        @pl.when(pl.program_id(0) == 0)
        def _():
            s_scr[...] = jnp.zeros_like(s_scr)

        lg = lg_ref[...]
        lb_all = _sigmoid(lg[0:1] - lg[1:2])
        causal, _ = _hg_masks(rows)
        tri = _chunk_tri(False)
        for h in range(nh):
            sl = slice(h * HG_HEAD, (h + 1) * HG_HEAD)
            q_raw = q_ref[:, sl].astype(F32)
            _, _, k, b, sq = _hg_gates(f_ref[:, sl].astype(F32), q_raw, lb_all[:, sl], tri)
            v = i_ref[:, sl].astype(BF16)
            gls = [b[(g + 1) * HG_CHUNK - 1:(g + 1) * HG_CHUNK] for g in range(HG_GROUP)]
            qd = (q_raw * sq * jnp.exp(b)).astype(BF16)
            kd = (k * jnp.exp(-b)).astype(BF16)
            ke = (k * jnp.exp(_per_chunk(gls) - b)).astype(BF16)
            a = jnp.where(causal, _dot_nt(qd, kd), 0.0).astype(BF16)
            o_intra = _dot(a, v)
            st = s_scr[h]
            o_inter = []
            for g in range(HG_GROUP):
                rs = slice(g * HG_CHUNK, (g + 1) * HG_CHUNK)
                st_ref[g, sl, :] = st
                o_inter.append(_dot_nt(qd[rs], st.astype(BF16)))
                st = st * jnp.exp(gls[g]) + _dot_tn(v[rs], ke[rs])
            s_scr[h] = st
            o_ref[:, sl] = o_intra + jnp.concatenate(o_inter, axis=0)

    blk = lambda j: pl.BlockSpec((rows, HG_WIDTH), lambda c: (c, j))
    return _call(body, name=name, grid=(nc // HG_GROUP,),
                 in_specs=[blk(0), blk(1), blk(2), pl.BlockSpec((2, HG_WIDTH), lambda c: (0, 0))],
                 out_specs=[blk(0), pl.BlockSpec((HG_GROUP, HG_WIDTH, HG_HEAD), lambda c: (c, 0, 0))],
                 out_shape=[_sds((t, HG_WIDTH)), _sds((nc, HG_WIDTH, HG_HEAD))],
                 scratch_shapes=[pltpu.VMEM((nh, HG_HEAD, HG_HEAD), F32)])(proj, proj, proj, lb_logits)


def _hgrn_bwd(proj, lb_logits, states, do, name):
    t = proj.shape[0]
    ng = t // (HG_GROUP * HG_CHUNK)
    nh = HG_WIDTH // HG_HEAD
    rows = HG_GROUP * HG_CHUNK

    def body(q_ref, f_ref, i_ref, lg_ref, st_ref, do_ref, d_ref, dlb_ref, ds_scr):
        first = pl.program_id(0) == 0

        @pl.when(first)
        def _():
            ds_scr[...] = jnp.zeros_like(ds_scr)

        lg = lg_ref[...]
        lb_all = _sigmoid(lg[0:1] - lg[1:2])
        causal, _ = _hg_masks(rows)
        tri = _chunk_tri(False)
        tri_t = _chunk_tri(True)
        dlb = []
        for h in range(nh):
            sl = slice(h * HG_HEAD, (h + 1) * HG_HEAD)
            q_raw = q_ref[:, sl].astype(F32)
            lb = lb_all[:, sl]
            sg, f, k, b, sq = _hg_gates(f_ref[:, sl].astype(F32), q_raw, lb, tri)
            v = i_ref[:, sl].astype(BF16)
            gls = [b[(g + 1) * HG_CHUNK - 1:(g + 1) * HG_CHUNK] for g in range(HG_GROUP)]
            eb = jnp.exp(b)
            enb = jnp.exp(-b)
            egb = jnp.exp(_per_chunk(gls) - b)
            ke = k * egb
            qd_b, kd_b, ke_b = (q_raw * sq * eb).astype(BF16), (k * enb).astype(BF16), ke.astype(BF16)
            dov = do_ref[:, sl].astype(BF16)
            a = jnp.where(causal, _dot_nt(qd_b, kd_b), 0.0).astype(BF16)
            da = jnp.where(causal, _dot_nt(dov, v), 0.0).astype(BF16)
            dkd = _dot_tn(da, qd_b)
            dst = ds_scr[h]
            dqd_s, dv_s, dke_s, dgl_s = [None] * HG_GROUP, [None] * HG_GROUP, [None] * HG_GROUP, [None] * HG_GROUP
            for g in reversed(range(HG_GROUP)):
                rs = slice(g * HG_CHUNK, (g + 1) * HG_CHUNK)
                st = st_ref[g, sl, :]
                dst_b = dst.astype(BF16)
                egl = jnp.exp(gls[g])
                dqd_s[g] = _dot(dov[rs], st.astype(BF16))
                dv_s[g] = _dot_nt(ke_b[rs], dst_b)
                dke_s[g] = _dot(v[rs], dst_b)
                dgl_s[g] = jnp.sum(dst * st, axis=0, keepdims=True) * egl
                dst = _dot_tn(dov[rs], qd_b[rs]) + dst * egl
            ds_scr[h] = dst
            dqd = _dot(da, kd_b) + jnp.concatenate(dqd_s, axis=0)
            dv = _dot_tn(a, dov) + jnp.concatenate(dv_s, axis=0)
            dke = jnp.concatenate(dke_s, axis=0)
            t1 = dke * ke
            db = dqd * qd_b.astype(F32) - dkd * kd_b.astype(F32) - t1
            dgl = _per_chunk([dgl_s[g] + jnp.sum(t1[g * HG_CHUNK:(g + 1) * HG_CHUNK], axis=0, keepdims=True)
                              for g in range(HG_GROUP)])
            dlf = _chunk_cumsum(db, tri_t) + dgl
            df = dlf / f - (dkd * enb + dke * egb)
            d_ref[:, sl] = (dqd * eb * sq * (1.0 + q_raw * (1.0 - sq))).astype(BF16)
            d_ref[:, HG_WIDTH + h * HG_HEAD:HG_WIDTH + (h + 1) * HG_HEAD] = (
                df * (1.0 - lb) * sg * (1.0 - sg)).astype(BF16)
            d_ref[:, 2 * HG_WIDTH + h * HG_HEAD:2 * HG_WIDTH + (h + 1) * HG_HEAD] = dv.astype(BF16)
            dlb.append(jnp.sum(df * (1.0 - sg), axis=0, keepdims=True))
        _acc(dlb_ref, jnp.concatenate(dlb, axis=1), first)

    rev = lambda j: pl.BlockSpec((rows, HG_WIDTH), lambda c: (ng - 1 - c, j))
    return _call(body, name=name, grid=(ng,),
                 in_specs=[rev(0), rev(1), rev(2), pl.BlockSpec((2, HG_WIDTH), lambda c: (0, 0)),
                           pl.BlockSpec((HG_GROUP, HG_WIDTH, HG_HEAD), lambda c: (ng - 1 - c, 0, 0)), rev(0)],
                 out_specs=[pl.BlockSpec((rows, 3 * HG_WIDTH), lambda c: (ng - 1 - c, 0)), _vec(HG_WIDTH)],
                 out_shape=[_sds((t, 3 * HG_WIDTH), BF16), _sds((1, HG_WIDTH))],
                 scratch_shapes=[pltpu.VMEM((nh, HG_HEAD, HG_HEAD), F32)])(proj, proj, proj, lb_logits, states, do)


def _to_sub(a, dil):
    t, w = a.shape
    return a if dil == 1 else a.reshape(t // dil, dil, w).transpose(1, 0, 2).reshape(t, w)


def _from_sub(a, dil):
    t, w = a.shape
    return a if dil == 1 else a.reshape(dil, t // dil, w).transpose(1, 0, 2).reshape(t, w)


def _att_mask(has_prev):
    qi = lax.broadcasted_iota(jnp.int32, (2 * ATT_BLOCK, 2 * ATT_BLOCK), 0) % ATT_BLOCK
    kj = lax.broadcasted_iota(jnp.int32, (2 * ATT_BLOCK, 2 * ATT_BLOCK), 1)
    prev = jnp.logical_and(jnp.logical_and(kj < ATT_BLOCK, kj >= qi), has_prev)
    cur = jnp.logical_and(kj >= ATT_BLOCK, kj - ATT_BLOCK <= qi)
    return jnp.logical_or(prev, cur), lax.broadcasted_iota(jnp.int32, (1, 128), 1)


def _attn_fwd(qkv, dil, name):
    t = qkv.shape[0]
    nb = t // ATT_BLOCK
    bps = nb // dil

    def body(q_ref, kc_ref, kp_ref, vc_ref, vp_ref, o_ref, l_ref):
        mask, lane = _att_mask((pl.program_id(0) % bps) != 0)
        lo = lane < 64
        nq = ATT_BLOCK
        lse_all = jnp.zeros((nq, 128), F32)
        for hp in range(ATT_HEADS // 2):
            sl = slice(hp * 128, (hp + 1) * 128)
            q2 = q_ref[:, sl]
            zero = jnp.zeros_like(q2)
            qs = jnp.concatenate([jnp.where(lo, q2, zero), jnp.where(lo, zero, q2)], axis=0)
            kk = jnp.concatenate([kp_ref[:, sl], kc_ref[:, sl]], axis=0)
            vv = jnp.concatenate([vp_ref[:, sl], vc_ref[:, sl]], axis=0)
            s = jnp.where(mask, _dot_nt(qs, kk) * 0.125, NEG)
            mx = jnp.max(s, axis=-1, keepdims=True)
            p = jnp.exp(s - mx)
            l = jnp.sum(p, axis=-1, keepdims=True)
            o = _dot(p.astype(BF16), vv) * (1.0 / l)
            o_ref[:, sl] = jnp.where(lo, o[:nq], o[nq:])
            lse = mx + jnp.log(l)
            lse_all = jnp.where(lane == 2 * hp, lse[:nq], lse_all)
            lse_all = jnp.where(lane == 2 * hp + 1, lse[nq:], lse_all)
        l_ref[...] = lse_all

    col0 = qkv.shape[1] // ATT_WIDTH - 3
    blk = lambda j, back: pl.BlockSpec((ATT_BLOCK, ATT_WIDTH), lambda n: (jnp.maximum(n - back, 0), col0 + j))
    return _call(body, name=name, grid=(nb,),
                 in_specs=[blk(0, 0), blk(1, 0), blk(1, 1), blk(2, 0), blk(2, 1)],
                 out_specs=[pl.BlockSpec((ATT_BLOCK, ATT_WIDTH), lambda n: (n, 0)),
                            pl.BlockSpec((ATT_BLOCK, 128), lambda n: (n, 0))],
                 out_shape=[_sds((t, ATT_WIDTH)), _sds((t, 128))])(qkv, qkv, qkv, qkv, qkv)


def _attn_combine(os_, ls_, name, tr=256):
    t = os_[0].shape[0]
    nbr = len(os_)

    def body(*refs):
        o_refs, l_refs, (a_ref, lt_ref) = refs[:nbr], refs[nbr:2 * nbr], refs[2 * nbr:]
        lane = lax.broadcasted_iota(jnp.int32, (1, 128), 1)
        ls = [r[...] for r in l_refs]
        mx = functools.reduce(jnp.maximum, ls)
        tot = mx + jnp.log(sum(jnp.exp(l - mx) for l in ls))
        lt_ref[...] = tot
        ws = [jnp.exp(l - tot) for l in ls]
        for hp in range(ATT_HEADS // 2):
            sl = slice(hp * 128, (hp + 1) * 128)
            acc = jnp.zeros((tr, 128), F32)
            for w, o_ref in zip(ws, o_refs):
                wf = jnp.where(lane < 64, w[:, 2 * hp:2 * hp + 1], w[:, 2 * hp + 1:2 * hp + 2])
                acc = acc + wf * o_ref[:, sl]
            a_ref[:, sl] = acc

    return _call(body, name=name, grid=(t // tr,),
                 in_specs=[_rows(tr, ATT_WIDTH)] * nbr + [_rows(tr, 128)] * nbr,
                 out_specs=[_rows(tr, ATT_WIDTH), _rows(tr, 128)],
                 out_shape=[_sds((t, ATT_WIDTH)), _sds((t, 128))])(*os_, *ls_)


def _attn_bwd(qkv, dout, lse, dd, dil, name):
    t = qkv.shape[0]
    nb = t // ATT_BLOCK
    bps = nb // dil

    w = ATT_WIDTH
    nq = ATT_BLOCK

    def body(q_ref, kc_ref, kp_ref, vc_ref, vp_ref, do_ref, l_ref, d_ref, dq_ref, dkv_ref, carry):
        n = pl.program_id(0)

        @pl.when(n == 0)
        def _():
            carry[...] = jnp.zeros_like(carry)

        @pl.when(n < nb)
        def _():
            mask, lane = _att_mask((n % bps) != 0)
            lo = lane < 64
            for hp in range(ATT_HEADS // 2):
                sl = slice(hp * 128, (hp + 1) * 128)
                sv = slice(w + hp * 128, w + (hp + 1) * 128)
                q2, do2 = q_ref[:, sl], do_ref[:, sl]
                zero = jnp.zeros_like(q2)
                qs = jnp.concatenate([jnp.where(lo, q2, zero), jnp.where(lo, zero, q2)], axis=0)
                dos = jnp.concatenate([jnp.where(lo, do2, zero), jnp.where(lo, zero, do2)], axis=0)
                kk = jnp.concatenate([kp_ref[:, sl], kc_ref[:, sl]], axis=0)
                vv = jnp.concatenate([vp_ref[:, sl], vc_ref[:, sl]], axis=0)
                ls = jnp.concatenate([l_ref[:, 2 * hp:2 * hp + 1], l_ref[:, 2 * hp + 1:2 * hp + 2]], axis=0)
                dh = jnp.concatenate([d_ref[:, 2 * hp:2 * hp + 1], d_ref[:, 2 * hp + 1:2 * hp + 2]], axis=0)
                p = jnp.exp(jnp.where(mask, _dot_nt(qs, kk) * 0.125 - ls, NEG))
                ds = (p * (_dot_nt(dos, vv) - dh)).astype(BF16)
                dq = _dot(ds, kk) * 0.125
                dq_ref[:, sl] = jnp.where(lo, dq[:nq], dq[nq:]).astype(BF16)
                dk = _dot_tn(ds, qs) * 0.125
                dv = _dot_tn(p.astype(BF16), dos)
                dkv_ref[:, sl] = (carry[:, sl] + dk[:nq]).astype(BF16)
                dkv_ref[:, sv] = (carry[:, sv] + dv[:nq]).astype(BF16)
                carry[:, sl] = dk[nq:]
                carry[:, sv] = dv[nq:]

        @pl.when(n == nb)
        def _():
            dkv_ref[...] = carry[...].astype(BF16)

    def blk(width, j, back):
        return pl.BlockSpec((nq, width), lambda n: (jnp.clip(n - back, 0, nb - 1), j))

    c0 = qkv.shape[1] // w - 3
    return _call(body, name=name, grid=(nb + 1,),
                 in_specs=[blk(w, c0, 0), blk(w, c0 + 1, 0), blk(w, c0 + 1, 1), blk(w, c0 + 2, 0), blk(w, c0 + 2, 1),
                           blk(w, 0, 0), blk(128, 0, 0), blk(128, 0, 0)],
                 out_specs=[blk(w, 0, 0), blk(2 * w, 0, 1)],
                 out_shape=[_sds((t, w), BF16), _sds((t, 2 * w), BF16)],
                 scratch_shapes=[pltpu.VMEM((nq, 2 * w), F32)])(qkv, qkv, qkv, qkv, qkv, dout, lse, dd)


def _local_step(x, tgt, mod, norm1_g, lb_logits, og, ag, norm2_g, fg, get_w, put_g):
    shift1, scale1, gate1, shift2, scale2, gate2 = [mod[:, i * D_MODEL:(i + 1) * D_MODEL] for i in range(6)]
    fg = fg.reshape(1, D_MODEL)

    h1 = _norm_mod(x, norm1_g, scale1, shift1, "norm_mod1")
    w_in = get_w("w_in", h1)
    proj = _mm_nn(h1, w_in, "mm_in", out_dtype=BF16)
    o_hg, states = _hgrn_fwd(proj, lb_logits, "hgrn_fwd")
    qkvs = [proj if d == 1 else _to_sub(proj[:, 4 * HG_WIDTH:], d) for d in DILATIONS]
    outs = [_attn_fwd(q, d, f"attn_fwd{d}") for q, d in zip(qkvs, DILATIONS)]
    att, lse = _attn_combine([_from_sub(o, d) for (o, _), d in zip(outs, DILATIONS)],
                             [_from_sub(l, d) for (_, l), d in zip(outs, DILATIONS)], "attn_combine")
    mixin = _mix_in(o_hg, proj, att, og, ag, "mix_in")
    w_out = get_w("w_out", mixin)
    mix = _mm_nn(mixin, w_out, "mm_out")
    x2, h2 = _resid_norm_mod(x, mix, gate1, norm2_g, scale2, shift2, "resid_norm_mod2")
    w_gu = get_w("w_gu", h2)
    a_ff, u_ff, act = _mm_gate_up(h2, w_gu, "mm_gu")
    w_down = get_w("w_down", act)
    ffn = _mm_nn(act, w_down, "mm_down")
    dx3, dffn, loss_v, dfg, dgate2 = _final_loss(x2, ffn, gate2, fg, tgt, "final_loss")

    put_g("w_down", *_mm_tn(act, dffn, 1, "mm_down_dw", tm=2048, tk=D_FF // 2))
    dau = _mm_down_dx(dffn, w_down, a_ff, u_ff, "mm_down_dx")
    put_g("w_gu", *_mm_tn(h2, dau, N_SHARD, "mm_gu_dw", tm=x.shape[0], tk=512))
    dh2 = _mm_nt(dau, w_gu, "mm_gu_dx")
    dx2, dshift2, dscale2, dg2, dgate1, dmix = _norm_mod_bwd(
        dh2, x2, norm2_g, scale2, dx3, "norm_mod2_bwd", gate=gate1, mix=mix)
    put_g("w_out", *_mm_tn(mixin, dmix, 1, "mm_out_dw", tm=x.shape[0], tk=512))
    dmixin = _mm_nt(dmix, w_out, "mm_out_dx", tm=1024)
    do_hg, dg_raw, datt, dd, dog, dag = _mix_in_bwd(dmixin, o_hg, proj, att, og, ag, "mix_in_bwd")
    datt_b = datt.astype(BF16)
    datts = [_attn_bwd(q, _to_sub(datt_b, d), _to_sub(lse, d), _to_sub(dd, d), d, f"attn_bwd{d}")
             for q, d in zip(qkvs, DILATIONS)]
    dhg, dlb = _hgrn_bwd(proj, lb_logits, states, do_hg, "hgrn_bwd")
    dproj = _dproj(dhg, dg_raw, [_from_sub(dq, d) for (dq, _), d in zip(datts, DILATIONS)],
                   [_from_sub(dkv, d) for (_, dkv), d in zip(datts, DILATIONS)], "dproj")
    put_g("w_in", *_mm_tn(h1, dproj, N_SHARD, "mm_in_dw", tm=x.shape[0], tk=512))
    dh1 = _mm_nt(dproj, w_in, "mm_in_dx", tm=1024)
    dx, dshift1, dscale1, dg1 = _norm_mod_bwd(dh1, x, norm1_g, scale1, dx2, "norm_mod1_bwd")

    stats = jnp.concatenate([loss_v, dfg, dg2, dg1, dlb, dag, dog,
                             dshift1, dscale1, dgate1, dshift2, dscale2, dgate2], axis=1)
    return dx, stats


def _place():
    x, y, c = lax.axis_index("x"), lax.axis_index("y"), lax.axis_index("c")
    return x, y, c


def _chip_peers(x, y, c):
    return [(1 - x, y, c), (x, 1 - y, c), (1 - x, 1 - y, c)]


def _comm_call(body, name, n_in, out_shape, scratch_shapes):
    hbm = pl.BlockSpec(memory_space=pl.ANY)
    return pl.pallas_call(body, name=name, in_specs=[hbm] * n_in, out_specs=[hbm] * len(out_shape),
                          out_shape=out_shape, scratch_shapes=scratch_shapes)


_HBM = pl.BlockSpec(memory_space=pltpu.HBM)
_SEM = pl.BlockSpec(memory_space=pltpu.SEMAPHORE)
_EFFECT = pltpu.SideEffectType.DATAFLOW_SIDE_EFFECTING


def _exchange_copy(bufs, send, recv, j, peer, place, gather):
    if gather:
        src = dst = bufs[0].at[2 * place[0] + place[1]]
    else:
        src, dst = bufs[0].at[2 * peer[0] + peer[1]], bufs[1].at[j]
    return pltpu.make_async_remote_copy(src_ref=src, dst_ref=dst, send_sem=send.at[j], recv_sem=recv.at[j],
                                        device_id=peer, device_id_type=MESH)


def _exchange_start(groups, after, gather, name):
    sizes = [len(g) for g in groups]
    flat = [b for g in groups for b in g]
    ng, nb = len(groups), len(flat)

    def body(*refs):
        bufs, sems = refs[:nb], refs[nb + 1:nb + 1 + 2 * ng]
        x, y, c = _place()
        for j, peer in enumerate(_chip_peers(x, y, c)):
            at = 0
            for i, size in enumerate(sizes):
                _exchange_copy(bufs[at:at + size], sems[2 * i], sems[2 * i + 1], j, peer, (x, y), gather).start()
                at += size

    any_space = pl.BlockSpec(memory_space=pl.ANY)
    out = pl.pallas_call(
        body, name=name, in_specs=[_HBM] * nb + [any_space],
        out_specs=[_SEM] * (2 * ng) + [_HBM] * nb + [any_space],
        out_shape=[pltpu.SemaphoreType.DMA((3,))] * (2 * ng) + [pltpu.HBM(b.shape, b.dtype) for b in flat]
        + [_sds(after.shape, after.dtype)],
        input_output_aliases={i: 2 * ng + i for i in range(nb + 1)},
        compiler_params=pltpu.CompilerParams(has_side_effects=_EFFECT),
    )(*[pltpu.with_memory_space_constraint(b, pltpu.HBM) for b in flat], after)
    started, at = [], 2 * ng
    for i, size in enumerate(sizes):
        started.append((out[2 * i], out[2 * i + 1], tuple(out[at:at + size])))
        at += size
    return started, out[-1]


def _exchange_wait(started, after, gather, name):
    send, recv, bufs = started
    nb = len(bufs)

    def body(*refs):
        x, y, c = _place()
        for j, peer in enumerate(_chip_peers(x, y, c)):
            cp = _exchange_copy(refs[:nb], refs[nb], refs[nb + 1], j, peer, (x, y), gather)
            cp.wait_send()
            cp.wait_recv()

    return pl.pallas_call(
        body, name=name, in_specs=[_HBM] * nb + [_SEM, _SEM, pl.BlockSpec(memory_space=pl.ANY)],
        out_specs=[_HBM] * nb, out_shape=[pltpu.HBM(b.shape, b.dtype) for b in bufs],
        input_output_aliases={i: i for i in range(nb)},
        compiler_params=pltpu.CompilerParams(has_side_effects=_EFFECT),
    )(*bufs, send, recv, after)


def _swap_sibling(vs, name):
    n = len(vs)

    def body(*refs):
        v_refs, o_refs, (send, recv) = refs[:n], refs[n:2 * n], refs[2 * n:]
        x, y, c = _place()
        cps = [pltpu.make_async_remote_copy(
            src_ref=v_refs[a], dst_ref=o_refs[a], send_sem=send.at[a], recv_sem=recv.at[a],
            device_id=(x, y, 1 - c), device_id_type=MESH) for a in range(n)]
        for cp in cps:
            cp.start()
        for cp in cps:
            cp.wait()

    return _comm_call(body, name, n, [_sds(v.shape, v.dtype) for v in vs],
                      [pltpu.SemaphoreType.DMA((n,)), pltpu.SemaphoreType.DMA((n,))])(*vs)


def _gather_rows(v, name):
    r, n = v.shape

    def body(v_ref, o_ref, send, recv, loc):
        x, y, c = _place()
        me = 4 * x + 2 * y + c
        own = pltpu.make_async_copy(v_ref, o_ref.at[me], loc)
        own.start()
        peers = []
        for k in range(1, 8):
            px = 1 - x if k & 4 else x
            py = 1 - y if k & 2 else y
            pc = 1 - c if k & 1 else c
            peers.append((px, py, pc))
        sends = []
        for k, peer in enumerate(peers):
            cp = pltpu.make_async_remote_copy(src_ref=v_ref, dst_ref=o_ref.at[me], send_sem=send.at[k],
                                              recv_sem=recv.at[k], device_id=peer, device_id_type=MESH)
            cp.start()
            sends.append(cp)
        for k, peer in enumerate(peers):
            pltpu.make_async_remote_copy(src_ref=v_ref, dst_ref=o_ref.at[4 * peer[0] + 2 * peer[1] + peer[2]],
                                         send_sem=send.at[k], recv_sem=recv.at[k], device_id=peer,
                                         device_id_type=MESH).wait_recv()
        for cp in sends:
            cp.wait_send()
        own.wait()

    vmem = pl.BlockSpec(memory_space=pltpu.VMEM)
    return pl.pallas_call(body, name=name, in_specs=[vmem], out_specs=vmem, out_shape=_sds((8, r, n), v.dtype),
                          scratch_shapes=[pltpu.SemaphoreType.DMA((7,)), pltpu.SemaphoreType.DMA((7,)),
                                          pltpu.SemaphoreType.DMA])(v)


def _cast_place(w, shard, name):
    r, c = w.shape
    tr = r // 4

    def body(s_ref, w_ref, o_ref):
        o_ref[0] = w_ref[...].astype(BF16)

    return pl.pallas_call(
        body, name=name, out_shape=_sds((N_SHARD, r, c), BF16),
        grid_spec=pltpu.PrefetchScalarGridSpec(
            num_scalar_prefetch=1, grid=(4,), in_specs=[pl.BlockSpec((tr, c), lambda i, s: (i, 0))],
            out_specs=pl.BlockSpec((1, tr, c), lambda i, s: (s[0], i, 0))),
        compiler_params=pltpu.CompilerParams(dimension_semantics=("arbitrary",)),
    )(shard.reshape(1).astype(jnp.int32), w)


def _mod_part(c_all, w_ada, b_ada, name):
    n = w_ada.shape[1]

    def body(c_ref, w_ref, b_ref, a_ref, p_ref):
        cv = c_ref[...]
        ca = cv * _sigmoid(cv)
        a_ref[...] = ca
        p_ref[...] = jnp.dot(ca, w_ref[...], precision=lax.Precision.HIGHEST, preferred_element_type=F32) + b_ref[...]

    full = lambda a: pl.BlockSpec(a.shape, lambda i: (0, 0))
    return _call(body, name=name, grid=(1,), in_specs=[full(c_all), full(w_ada), full(b_ada)],
                 out_specs=[pl.BlockSpec((8, D_MODEL), lambda i: (0, 0)), pl.BlockSpec((8, n), lambda i: (0, 0))],
                 out_shape=[_sds((8, D_MODEL)), _sds((8, n))])(c_all, w_ada, b_ada)


def _sum_received(g_own, land, name):
    r, c = g_own.shape
    tr = r // 4

    def body(g_ref, l_ref, o_ref):
        o_ref[...] = ((g_ref[...] + l_ref[0].astype(F32)) + l_ref[1].astype(F32)) + l_ref[2].astype(F32)

    return _call(body, name=name, grid=(4,),
                 in_specs=[_rows(tr, c), pl.BlockSpec((3, tr, c), lambda i: (0, i, 0))],
                 out_specs=_rows(tr, c), out_shape=_sds((r, c)))(g_own, land)


def _outer_sum(ct, dm, name):
    k, n = ct.shape[0], dm.shape[1]
    tr = k // 4

    def body(c_ref, d_ref, o_ref):
        cv = c_ref[...]
        dv = d_ref[...]
        acc = cv[:, 0:1] * dv[0:1, :]
        for i in range(1, 8):
            acc = acc + cv[:, i:i + 1] * dv[i:i + 1, :]
        o_ref[...] = acc

    return _call(body, name=name, grid=(4,), in_specs=[_rows(tr, 8), pl.BlockSpec((8, n), lambda i: (0, 0))],
                 out_specs=_rows(tr, n), out_shape=_sds((k, n)))(ct, dm)


def _small_grads(stats, lb_logits, name):
    def body(s_ref, lg_ref, g_ref, l_ref):
        tot = s_ref[0:1, :]
        for i in range(1, 8):
            tot = tot + s_ref[i:i + 1, :]
        part = lambda off, n: tot[:, off:off + n]
        l_ref[...] = jnp.zeros((1, 128), F32) + (0.5 / D_MODEL) * jnp.sum(part(ST_LOSS, D_MODEL))
        lg = lg_ref[...]
        lb = _sigmoid(lg[0:1] - lg[1:2])
        dl0 = part(ST_DLB, HG_WIDTH) * lb * (1.0 - lb)
        g_ref[:, SP_BADA:SP_BADA + 6 * D_MODEL] = part(ST_DMOD, 6 * D_MODEL)
        g_ref[:, SP_N1:SP_N1 + D_MODEL] = part(ST_DG1, D_MODEL)
        g_ref[:, SP_LB:SP_LB + HG_WIDTH] = dl0
        g_ref[:, SP_LB + HG_WIDTH:SP_LB + 2 * HG_WIDTH] = -dl0
        g_ref[:, SP_OG:SP_OG + HG_HEAD] = part(ST_DOG, HG_HEAD)
        g_ref[:, SP_AG:SP_AG + ATT_WIDTH] = part(ST_DAG, ATT_WIDTH)
        g_ref[:, SP_N2:SP_N2 + D_MODEL] = part(ST_DG2, D_MODEL)
        g_ref[:, SP_FG:SP_FG + D_MODEL] = part(ST_DFG, D_MODEL)

    return _call(body, name=name, grid=(1,),
                 in_specs=[pl.BlockSpec((8, ST_WIDTH), lambda i: (0, 0)), pl.BlockSpec((2, HG_WIDTH), lambda i: (0, 0))],
                 out_specs=[pl.BlockSpec((1, SP_WIDTH), lambda i: (0, 0)), pl.BlockSpec((1, 128), lambda i: (0, 0))],
                 out_shape=[_sds((1, SP_WIDTH)), _sds((1, 128))])(stats, lb_logits)


def _adamw(w, gs, m, v, name, steps=4):
    r, c = w.shape
    tr = r // steps
    ng = len(gs)

    def body(*refs):
        w_ref, g_refs, (m_ref, v_ref, g_out, d_out, m_out, v_out) = refs[0], refs[1:1 + ng], refs[1 + ng:]
        g = g_refs[0][...]
        for g_ref in g_refs[1:]:
            g = g + g_ref[...]
        m_new = ADAM_B1 * m_ref[...] + (1.0 - ADAM_B1) * g
        v_new = ADAM_B2 * v_ref[...] + (1.0 - ADAM_B2) * (g * g)
        m_hat = m_new / (1.0 - ADAM_B1 ** ADAM_STEP)
        v_hat = v_new / (1.0 - ADAM_B2 ** ADAM_STEP)
        g_out[...] = g
        d_out[...] = -ADAM_LR * (m_hat / (jnp.sqrt(v_hat) + ADAM_EPS) + ADAM_WD * w_ref[...])
        m_out[...] = m_new
        v_out[...] = v_new

    row = _rows(tr, c)
    return _call(body, name=name, grid=(steps,), in_specs=[row] * (3 + ng), out_specs=[row] * 4,
                 out_shape=[_sds((r, c))] * 4)(w, *gs, m, v)


def kernel(x, c, w_ada, b_ada, norm1_g, w_in, hg_lb_logits, hg_onorm_g, att_onorm_g, w_out, norm2_g, w_gate_up, w_down, final_g, loss_target, m_w_ada, m_b_ada, m_norm1_g, m_w_in, m_hg_lb_logits, m_hg_onorm_g, m_att_onorm_g, m_w_out, m_norm2_g, m_w_gate_up, m_w_down, m_final_g, v_w_ada, v_b_ada, v_norm1_g, v_w_in, v_hg_lb_logits, v_hg_onorm_g, v_att_onorm_g, v_w_out, v_norm2_g, v_w_gate_up, v_w_down, v_final_g):
    ix, iy, ic = _place()
    shard = 2 * ix + iy
    sample = 4 * ix + 2 * iy + ic
    n_ada = w_ada.shape[2]

    shards = [w_in[0], w_out[0], w_gate_up[0], w_down[0]]
    names = ["w_in", "w_out", "w_gu", "w_down"]
    shapes = [(N_SHARD,) + w.shape for w in shards]
    placed = [(_cast_place(w, shard, "place_" + nm),) for w, nm in zip(shards, names)]

    c_all = _gather_rows(c, "gather_c").reshape(8, D_MODEL)
    b_part = lax.dynamic_slice(b_ada, (0, shard * n_ada), (1, n_ada))
    c_act, part = _mod_part(c_all, w_ada[0], b_part, "mod_part")
    parts = _gather_rows(part, "gather_mod")[::2]
    mod = lax.dynamic_index_in_dim(parts, sample, axis=1, keepdims=False).reshape(1, 6 * D_MODEL)
    (first,), mod = _exchange_start(placed[:1], mod, True, "gather_start_w_in")
    gathering = {"w_in": first}

    def get_w(name, after):
        (full,) = _exchange_wait(gathering[name], after, True, "gather_wait_" + name)
        if name == "w_in":
            rest, full = _exchange_start(placed[1:], full, True, "gather_start_rest")
            gathering.update(zip(names[1:], rest))
        return full if name in ("w_in", "w_gu") else full.reshape(1, -1, D_MODEL)

    scattering = {}

    def put_g(name, g, g_bf16):
        shape = shapes[names.index(name)]
        land = lax.empty((3,) + shape[1:], BF16)
        (started,), g = _exchange_start([(g_bf16.reshape(shape), land)], g, False, "scatter_start_" + name)
        scattering[name] = (g.reshape(shape), started)

    dx, stats = _local_step(x[0], loss_target[0], mod, norm1_g, hg_lb_logits, hg_onorm_g, att_onorm_g,
                            norm2_g, final_g, get_w, put_g)

    def summed(name, after):
        g, started = scattering[name]
        _, land = _exchange_wait(started, after, False, "scatter_wait_" + name)
        own = lax.dynamic_index_in_dim(g, shard, axis=0, keepdims=False)
        return _sum_received(own, land, "sum_" + name)

    stats_all = _gather_rows(stats, "gather_stats").reshape(8, ST_WIDTH)
    g_small, loss = _small_grads(stats_all, hg_lb_logits, "small_grads")
    dmod = lax.dynamic_slice(stats_all, (0, ST_DMOD + shard * n_ada), (8, n_ada))
    g_ada = _outer_sum(c_act.T, dmod, "w_ada_grad")

    smalls = [(b_ada, m_b_ada, v_b_ada), (norm1_g, m_norm1_g, v_norm1_g),
              (hg_lb_logits, m_hg_lb_logits, v_hg_lb_logits), (hg_onorm_g, m_hg_onorm_g, v_hg_onorm_g),
              (att_onorm_g, m_att_onorm_g, v_att_onorm_g), (norm2_g, m_norm2_g, v_norm2_g),
              (final_g, m_final_g, v_final_g)]
    pack = lambda i: jnp.concatenate([t[i].reshape(1, -1) for t in smalls], axis=1)
    small_out = _adamw(pack(0), [g_small], pack(1), pack(2), "adamw_small", steps=1)
    offs = [SP_BADA, SP_N1, SP_LB, SP_OG, SP_AG, SP_N2, SP_FG, SP_WIDTH]
    unpack = lambda a: [a[0, offs[i]:offs[i + 1]].reshape(smalls[i][0].shape) for i in range(7)]
    sg, sd, sm, sv = [unpack(a) for a in small_out]

    ada = _adamw(w_ada[0], [g_ada], m_w_ada[0], v_w_ada[0], "adamw_w_ada")
    moments = [(m_w_in, v_w_in), (m_w_out, v_w_out), (m_w_gate_up, v_w_gate_up), (m_w_down, v_w_down)]

    def update(group, after, tag):
        sums = [summed(nm, after) for nm in group]
        other = _swap_sibling(sums, "swap_sums_" + tag)
        return {nm: _adamw(shards[names.index(nm)], [s, o], moments[names.index(nm)][0][0],
                           moments[names.index(nm)][1][0], "adamw_" + nm) for nm, s, o in zip(group, sums, other)}

    done = update(["w_down", "w_gu", "w_out"], ada[1], "late")
    done.update(update(["w_in"], done["w_out"][1], "in"))
    big = [ada] + [done[nm] for nm in names]
    bg, bd, bm, bv = [[t[i][None] for t in big] for i in range(4)]

    def order(b, s):
        return [b[0], s[0], s[1], b[1], s[2], s[3], s[4], b[2], s[5], b[3], b[4], s[6]]

    return (loss[0, 0], dx[None], *order(bg, sg), *order(bd, sd), *order(bm, sm), *order(bv, sv))
```

```python
import functools

import jax
import jax.numpy as jnp
from jax import lax
from jax.experimental import pallas as pl
from jax.experimental.pallas import tpu as pltpu

F32 = jnp.float32
BF16 = jnp.bfloat16
MESH = pl.DeviceIdType.MESH

D_MODEL = 1024
HG_WIDTH = 512
HG_HEAD = 128
HG_CHUNK = 64
HG_GROUP = 4
ATT_WIDTH = 512
ATT_HEADS = 8
ATT_BLOCK = 128
DILATIONS = (1, 4, 16)
D_FF = 2816
IN_WIDTH = 3584
N_SHARD = 4
RMS_EPS = 1e-6
NEG = -1e30

ADAM_LR = 0.001
ADAM_B1 = 0.9
ADAM_B2 = 0.999
ADAM_EPS = 1e-08
ADAM_WD = 0.01
ADAM_STEP = 10

VMEM_LIMIT = 56 * 2**20

ST_LOSS, ST_DFG, ST_DG2, ST_DG1 = 0, 1024, 2048, 3072
ST_DLB, ST_DAG, ST_DOG, ST_DMOD = 4096, 4608, 5120, 5248
ST_WIDTH = 5248 + 6144
SP_BADA, SP_N1, SP_LB, SP_OG, SP_AG, SP_N2, SP_FG = 0, 6144, 7168, 8192, 8320, 8832, 9856
SP_WIDTH = 10880


def _call(body, *, name, grid, in_specs, out_specs, out_shape, scratch_shapes=()):
    return pl.pallas_call(
        body, name=name, grid=grid, in_specs=in_specs, out_specs=out_specs, out_shape=out_shape,
        scratch_shapes=list(scratch_shapes),
        compiler_params=pltpu.CompilerParams(
            dimension_semantics=("arbitrary",) * len(grid), vmem_limit_bytes=VMEM_LIMIT))


def _sds(shape, dtype=F32):
    return jax.ShapeDtypeStruct(shape, dtype)


def _dot(a, b):
    return jnp.dot(a, b, preferred_element_type=F32)


def _dot_nt(a, b):
    return lax.dot_general(a, b, (((1,), (1,)), ((), ())), preferred_element_type=F32)


def _dot_tn(a, b):
    return lax.dot_general(a, b, (((0,), (0,)), ((), ())), preferred_element_type=F32)


def _sigmoid(x):
    return 1.0 / (1.0 + jnp.exp(-x))


def _rows(tr, width):
    return pl.BlockSpec((tr, width), lambda i: (i, 0))


def _vec(width):
    return pl.BlockSpec((1, width), lambda i: (0, 0))


def _acc(ref, val, first):
    @pl.when(first)
    def _():
        ref[...] = val

    @pl.when(jnp.logical_not(first))
    def _():
        ref[...] += val


def _mm_nn(a, b3, name, tm=1024, out_dtype=F32):
    m, k = a.shape
    s, _, n = b3.shape

    def body(a_ref, b_ref, o_ref):
        o_ref[...] = _dot(a_ref[...], b_ref[0]).astype(out_dtype)

    return _call(
        body, name=name, grid=(s, m // tm),
        in_specs=[pl.BlockSpec((tm, k), lambda j, i: (i, 0)), pl.BlockSpec((1, k, n), lambda j, i: (j, 0, 0))],
        out_specs=pl.BlockSpec((tm, n), lambda j, i: (i, j)), out_shape=_sds((m, s * n), out_dtype))(a, b3)


def _mm_nt(dy, b3, name, tm=512):
    m = dy.shape[0]
    s, k, n = b3.shape

    def body(dy_ref, b_ref, o_ref):
        acc = _dot_nt(dy_ref[:, 0:n], b_ref[0])
        for j in range(1, s):
            acc = acc + _dot_nt(dy_ref[:, j * n:(j + 1) * n], b_ref[j])
        o_ref[...] = acc

    return _call(
        body, name=name, grid=(m // tm,),
        in_specs=[_rows(tm, s * n), pl.BlockSpec((s, k, n), lambda i: (0, 0, 0))],
        out_specs=_rows(tm, k), out_shape=_sds((m, k)))(dy, b3)


def _mm_tn(a, dy, s, name, tm, tk):
    m, k = a.shape
    n = dy.shape[1] // s
    steps = m // tm

    def body(a_ref, dy_ref, o_ref, ob_ref):
        p = _dot_tn(a_ref[...], dy_ref[...])[None]
        if steps == 1:
            o_ref[...] = p
            ob_ref[...] = p.astype(BF16)
        else:
            i = pl.program_id(2)
            _acc(o_ref, p, i == 0)

            @pl.when(i == steps - 1)
            def _():
                ob_ref[...] = o_ref[...].astype(BF16)

    out = pl.BlockSpec((1, tk, n), lambda kk, j, i: (j, kk, 0))
    return _call(
        body, name=name, grid=(k // tk, s, steps),
        in_specs=[pl.BlockSpec((tm, tk), lambda kk, j, i: (i, kk)), pl.BlockSpec((tm, n), lambda kk, j, i: (i, j))],
        out_specs=[out, out], out_shape=[_sds((s, k, n)), _sds((s, k, n), BF16)])(a, dy)


def _rms(x):
    return lax.rsqrt(jnp.mean(x * x, axis=-1, keepdims=True) + RMS_EPS)


def _rms_bwd(dxh, xh, r):
    return r * (dxh - xh * jnp.mean(dxh * xh, axis=-1, keepdims=True))


def _norm_mod(x, g, scale, shift, name, tr=512):
    t = x.shape[0]

    def body(x_ref, g_ref, sc_ref, sh_ref, h_ref):
        xv = x_ref[...]
        n = xv * _rms(xv) * g_ref[...]
        h_ref[...] = (n * (1.0 + sc_ref[...]) + sh_ref[...]).astype(BF16)

    return _call(body, name=name, grid=(t // tr,),
                 in_specs=[_rows(tr, D_MODEL), _vec(D_MODEL), _vec(D_MODEL), _vec(D_MODEL)],
                 out_specs=_rows(tr, D_MODEL), out_shape=_sds((t, D_MODEL), BF16))(x, g, scale, shift)


def _mix_in(o_hg, proj, att, og, ag, name, tr=256):
    t = o_hg.shape[0]

    def body(o_ref, g_ref, a_ref, og_ref, ag_ref, m_ref):
        for h in range(HG_WIDTH // HG_HEAD):
            sl = slice(h * HG_HEAD, (h + 1) * HG_HEAD)
            oh = o_ref[:, sl]
            gv = g_ref[:, sl].astype(F32)
            m_ref[:, sl] = (oh * _rms(oh) * og_ref[...] * (gv * _sigmoid(gv))).astype(BF16)
        av = a_ref[...]
        m_ref[:, HG_WIDTH:] = (av * _rms(av) * ag_ref[...]).astype(BF16)

    return _call(body, name=name, grid=(t // tr,),
                 in_specs=[_rows(tr, HG_WIDTH), pl.BlockSpec((tr, HG_WIDTH), lambda i: (i, 3)), _rows(tr, ATT_WIDTH),
                           _vec(HG_HEAD), _vec(ATT_WIDTH)],
                 out_specs=_rows(tr, D_MODEL), out_shape=_sds((t, D_MODEL), BF16))(o_hg, proj, att, og, ag)


def _resid_norm_mod(x, mix, gate, g, scale, shift, name, tr=256):
    t = x.shape[0]

    def body(x_ref, m_ref, gt_ref, g_ref, sc_ref, sh_ref, x2_ref, h_ref):
        x2 = x_ref[...] + gt_ref[...] * m_ref[...]
        x2_ref[...] = x2
        n = x2 * _rms(x2) * g_ref[...]
        h_ref[...] = (n * (1.0 + sc_ref[...]) + sh_ref[...]).astype(BF16)

    return _call(body, name=name, grid=(t // tr,),
                 in_specs=[_rows(tr, D_MODEL), _rows(tr, D_MODEL)] + [_vec(D_MODEL)] * 4,
                 out_specs=[_rows(tr, D_MODEL), _rows(tr, D_MODEL)],
                 out_shape=[_sds((t, D_MODEL)), _sds((t, D_MODEL), BF16)])(x, mix, gate, g, scale, shift)


def _mm_gate_up(h, w_gu, name, tm=1024):
    m, k = h.shape
    n = w_gu.shape[2]

    def body(h_ref, wa_ref, wu_ref, da_ref, du_ref, o_ref):
        hv = h_ref[...]
        a = _dot(hv, wa_ref[0])
        u = _dot(hv, wu_ref[0])
        sg = _sigmoid(a)
        silu = a * sg
        da_ref[...] = (u * sg * (1.0 + a * (1.0 - sg))).astype(BF16)
        du_ref[...] = silu.astype(BF16)
        o_ref[...] = (silu * u).astype(BF16)

    out = pl.BlockSpec((tm, n), lambda j, i: (i, j))
    return _call(body, name=name, grid=(2, m // tm),
                 in_specs=[pl.BlockSpec((tm, k), lambda j, i: (i, 0)), pl.BlockSpec((1, k, n), lambda j, i: (j, 0, 0)),
                           pl.BlockSpec((1, k, n), lambda j, i: (j + 2, 0, 0))],
                 out_specs=[out, out, out], out_shape=[_sds((m, 2 * n), BF16)] * 3)(h, w_gu, w_gu)


def _mm_down_dx(dffn, w_down, act_da, act_du, name, tm=512):
    m = dffn.shape[0]
    _, k, n = w_down.shape

    def body(d_ref, w_ref, da_ref, du_ref, o_ref):
        dact = _dot_nt(d_ref[...], w_ref[0])
        o_ref[:, :k] = (dact * da_ref[...].astype(F32)).astype(BF16)
        o_ref[:, k:] = (dact * du_ref[...].astype(F32)).astype(BF16)

    return _call(body, name=name, grid=(m // tm,),
                 in_specs=[_rows(tm, n), pl.BlockSpec((1, k, n), lambda i: (0, 0, 0)), _rows(tm, k), _rows(tm, k)],
                 out_specs=_rows(tm, 2 * k), out_shape=_sds((m, 2 * k), BF16))(dffn, w_down, act_da, act_du)


def _final_loss(x2, ffn, gate, fg, tgt, name, tr=256):
    t = x2.shape[0]

    def body(x_ref, f_ref, gt_ref, fg_ref, t_ref, dx_ref, df_ref, l_ref, dfg_ref, dgt_ref):
        first = pl.program_id(0) == 0
        ffn_v = f_ref[...]
        x3 = x_ref[...] + gt_ref[...] * ffn_v
        r = _rms(x3)
        xh = x3 * r
        err = xh * fg_ref[...] - t_ref[...]
        dy = err * (1.0 / D_MODEL)
        dx3 = _rms_bwd(dy * fg_ref[...], xh, r)
        dx_ref[...] = dx3
        df_ref[...] = (dx3 * gt_ref[...]).astype(BF16)
        _acc(l_ref, jnp.sum(err * err, axis=0, keepdims=True), first)
        _acc(dfg_ref, jnp.sum(dy * xh, axis=0, keepdims=True), first)
        _acc(dgt_ref, jnp.sum(dx3 * ffn_v, axis=0, keepdims=True), first)

    row, vec = _rows(tr, D_MODEL), _vec(D_MODEL)
    return _call(body, name=name, grid=(t // tr,), in_specs=[row, row, vec, vec, row],
                 out_specs=[row, row, vec, vec, vec],
                 out_shape=[_sds((t, D_MODEL)), _sds((t, D_MODEL), BF16)] + [_sds((1, D_MODEL))] * 3)(
                     x2, ffn, gate, fg, tgt)


def _norm_mod_bwd(dh, x, g, scale, dres, name, gate=None, mix=None, tr=256):
    t = x.shape[0]
    below = gate is not None

    def body(*refs):
        if below:
            dh_ref, x_ref, g_ref, sc_ref, dr_ref, gt_ref, m_ref, dx_ref, dsh_ref, dsc_ref, dg_ref, dgt_ref, dm_ref = refs
        else:
            dh_ref, x_ref, g_ref, sc_ref, dr_ref, dx_ref, dsh_ref, dsc_ref, dg_ref = refs
        first = pl.program_id(0) == 0
        xv = x_ref[...]
        dhv = dh_ref[...]
        r = _rms(xv)
        xh = xv * r
        dn = dhv * (1.0 + sc_ref[...])
        dx = dr_ref[...] + _rms_bwd(dn * g_ref[...], xh, r)
        dx_ref[...] = dx
        _acc(dsh_ref, jnp.sum(dhv, axis=0, keepdims=True), first)
        _acc(dsc_ref, jnp.sum(dhv * xh * g_ref[...], axis=0, keepdims=True), first)
        _acc(dg_ref, jnp.sum(dn * xh, axis=0, keepdims=True), first)
        if below:
            _acc(dgt_ref, jnp.sum(dx * m_ref[...], axis=0, keepdims=True), first)
            dm_ref[...] = (dx * gt_ref[...]).astype(BF16)

    row, vec = _rows(tr, D_MODEL), _vec(D_MODEL)
    in_specs = [row, row, vec, vec, row] + ([vec, row] if below else [])
    out_specs = [row, vec, vec, vec] + ([vec, row] if below else [])
    out_shape = [_sds((t, D_MODEL))] + [_sds((1, D_MODEL))] * 3 + ([_sds((1, D_MODEL)), _sds((t, D_MODEL), BF16)] if below else [])
    args = (dh, x, g, scale, dres) + ((gate, mix) if below else ())
    return _call(body, name=name, grid=(t // tr,), in_specs=in_specs, out_specs=out_specs, out_shape=out_shape)(*args)


def _mix_in_bwd(dmi, o_hg, proj, att, og, ag, name, tr=256):
    t = o_hg.shape[0]

    def body(d_ref, o_ref, g_ref, a_ref, og_ref, ag_ref, do_ref, dg_ref, da_ref, dd_ref, dog_ref, dag_ref):
        first = pl.program_id(0) == 0
        dog = jnp.zeros((1, HG_HEAD), F32)
        for h in range(HG_WIDTH // HG_HEAD):
            sl = slice(h * HG_HEAD, (h + 1) * HG_HEAD)
            oh = o_ref[:, sl]
            gv = g_ref[:, sl].astype(F32)
            dv = d_ref[:, sl]
            r = _rms(oh)
            xh = oh * r
            sg = _sigmoid(gv)
            dno = dv * gv * sg
            dg_ref[:, sl] = dv * xh * og_ref[...] * sg * (1.0 + gv * (1.0 - sg))
            dog = dog + jnp.sum(dno * xh, axis=0, keepdims=True)
            do_ref[:, sl] = _rms_bwd(dno * og_ref[...], xh, r)
        _acc(dog_ref, dog, first)
        av = a_ref[...]
        dav = d_ref[:, HG_WIDTH:]
        r = _rms(av)
        xa = av * r
        _acc(dag_ref, jnp.sum(dav * xa, axis=0, keepdims=True), first)
        datt = _rms_bwd(dav * ag_ref[...], xa, r)
        da_ref[...] = datt
        prod = datt * av
        lane = lax.broadcasted_iota(jnp.int32, (1, 128), 1)
        dd = jnp.zeros((tr, 128), F32)
        for hp in range(ATT_HEADS // 2):
            pp = prod[:, hp * 128:(hp + 1) * 128]
            lo = jnp.sum(jnp.where(lane < 64, pp, 0.0), axis=-1, keepdims=True)
            hi = jnp.sum(jnp.where(lane >= 64, pp, 0.0), axis=-1, keepdims=True)
            dd = jnp.where(lane == 2 * hp, lo, dd)
            dd = jnp.where(lane == 2 * hp + 1, hi, dd)
        dd_ref[...] = dd

    half = _rows(tr, HG_WIDTH)
    return _call(body, name=name, grid=(t // tr,),
                 in_specs=[_rows(tr, D_MODEL), half, pl.BlockSpec((tr, HG_WIDTH), lambda i: (i, 3)), half,
                           _vec(HG_HEAD), _vec(ATT_WIDTH)],
                 out_specs=[half, half, half, _rows(tr, 128), _vec(HG_HEAD), _vec(ATT_WIDTH)],
                 out_shape=[_sds((t, HG_WIDTH))] * 3 + [_sds((t, 128)), _sds((1, HG_HEAD)), _sds((1, ATT_WIDTH))])(
                     dmi, o_hg, proj, att, og, ag)


def _dproj(dhg, dg, dqs, dkvs, name, tr=256):
    t = dhg.shape[0]
    w3 = 3 * HG_WIDTH
    w4 = w3 + HG_WIDTH
    nbr = len(dqs)

    def body(*refs):
        h_ref, g_ref, q_refs, kv_refs, o_ref = refs[0], refs[1], refs[2:2 + nbr], refs[2 + nbr:2 + 2 * nbr], refs[-1]
        o_ref[:, :w3] = h_ref[...]
        o_ref[:, w3:w4] = g_ref[...].astype(BF16)
        o_ref[:, w4:w4 + ATT_WIDTH] = sum(r[...].astype(F32) for r in q_refs).astype(BF16)
        o_ref[:, w4 + ATT_WIDTH:] = sum(r[...].astype(F32) for r in kv_refs).astype(BF16)

    return _call(body, name=name, grid=(t // tr,),
                 in_specs=[_rows(tr, w3), _rows(tr, HG_WIDTH)] + [_rows(tr, ATT_WIDTH)] * nbr
                 + [_rows(tr, 2 * ATT_WIDTH)] * nbr,
                 out_specs=_rows(tr, IN_WIDTH), out_shape=_sds((t, IN_WIDTH), BF16))(dhg, dg, *dqs, *dkvs)


def _chunk_tri(upper):
    row = lax.broadcasted_iota(jnp.int32, (HG_GROUP, HG_CHUNK, HG_CHUNK), 1)
    col = lax.broadcasted_iota(jnp.int32, (HG_GROUP, HG_CHUNK, HG_CHUNK), 2)
    return (row <= col if upper else row >= col).astype(BF16)


def _chunk_cumsum(x, tri):
    x3 = x.reshape(HG_GROUP, HG_CHUNK, x.shape[1])
    dims = (((2,), (1,)), ((0,), (0,)))
    out = None
    for _ in range(3):
        part = x3.astype(BF16)
        x3 = x3 - part.astype(F32)
        term = lax.dot_general(tri, part, dims, preferred_element_type=F32)
        out = term if out is None else out + term
    return out.reshape(x.shape)


def _hg_gates(f_raw, q_raw, lb, tri):
    sg = _sigmoid(f_raw)
    f = lb + (1.0 - lb) * sg
    k = 1.0 - f
    b = _chunk_cumsum(jnp.log(f), tri)
    sq = _sigmoid(q_raw)
    return sg, f, k, b, sq


def _hg_masks(rows):
    row = lax.broadcasted_iota(jnp.int32, (rows, rows), 0)
    col = lax.broadcasted_iota(jnp.int32, (rows, rows), 1)
    same = (row // HG_CHUNK) == (col // HG_CHUNK)
    return jnp.logical_and(row >= col, same), jnp.logical_and(row <= col, same)


def _per_chunk(rows_of):
    return jnp.concatenate([jnp.broadcast_to(r, (HG_CHUNK, r.shape[1])) for r in rows_of], axis=0)


def _hgrn_fwd(proj, lb_logits, name):
    t = proj.shape[0]
    nc = t // HG_CHUNK
    nh = HG_WIDTH // HG_HEAD
    rows = HG_GROUP * HG_CHUNK

    def body(q_ref, f_ref, i_ref, lg_ref, o_ref, st_ref, s_scr):
        @pl.when(pl.program_id(0) == 0)
        def _():
            s_scr[...] = jnp.zeros_like(s_scr)

        lg = lg_ref[...]
        lb_all = _sigmoid(lg[0:1] - lg[1:2])
        causal, _ = _hg_masks(rows)
        tri = _chunk_tri(False)
        for h in range(nh):
            sl = slice(h * HG_HEAD, (h + 1) * HG_HEAD)
            q_raw = q_ref[:, sl].astype(F32)
            _, _, k, b, sq = _hg_gates(f_ref[:, sl].astype(F32), q_raw, lb_all[:, sl], tri)
            v = i_ref[:, sl].astype(BF16)
            gls = [b[(g + 1) * HG_CHUNK - 1:(g + 1) * HG_CHUNK] for g in range(HG_GROUP)]
            qd = (q_raw * sq * jnp.exp(b)).astype(BF16)
            kd = (k * jnp.exp(-b)).astype(BF16)
            ke = (k * jnp.exp(_per_chunk(gls) - b)).astype(BF16)
            a = jnp.where(causal, _dot_nt(qd, kd), 0.0).astype(BF16)
            o_intra = _dot(a, v)
            st = s_scr[h]
            o_inter = []
            for g in range(HG_GROUP):
                rs = slice(g * HG_CHUNK, (g + 1) * HG_CHUNK)
                st_ref[g, sl, :] = st
                o_inter.append(_dot_nt(qd[rs], st.astype(BF16)))
                st = st * jnp.exp(gls[g]) + _dot_tn(v[rs], ke[rs])
            s_scr[h] = st
            o_ref[:, sl] = o_intra + jnp.concatenate(o_inter, axis=0)

    blk = lambda j: pl.BlockSpec((rows, HG_WIDTH), lambda c: (c, j))
    return _call(body, name=name, grid=(nc // HG_GROUP,),
                 in_specs=[blk(0), blk(1), blk(2), pl.BlockSpec((2, HG_WIDTH), lambda c: (0, 0))],
                 out_specs=[blk(0), pl.BlockSpec((HG_GROUP, HG_WIDTH, HG_HEAD), lambda c: (c, 0, 0))],
                 out_shape=[_sds((t, HG_WIDTH)), _sds((nc, HG_WIDTH, HG_HEAD))],
                 scratch_shapes=[pltpu.VMEM((nh, HG_HEAD, HG_HEAD), F32)])(proj, proj, proj, lb_logits)


def _hgrn_bwd(proj, lb_logits, states, do, name):
    t = proj.shape[0]
    ng = t // (HG_GROUP * HG_CHUNK)
    nh = HG_WIDTH // HG_HEAD
    rows = HG_GROUP * HG_CHUNK

    def body(q_ref, f_ref, i_ref, lg_ref, st_ref, do_ref, d_ref, dlb_ref, ds_scr):
        first = pl.program_id(0) == 0

        @pl.when(first)
        def _():
            ds_scr[...] = jnp.zeros_like(ds_scr)

        lg = lg_ref[...]
        lb_all = _sigmoid(lg[0:1] - lg[1:2])
        causal, _ = _hg_masks(rows)
        tri = _chunk_tri(False)
        tri_t = _chunk_tri(True)
        dlb = []
        for h in range(nh):
            sl = slice(h * HG_HEAD, (h + 1) * HG_HEAD)
            q_raw = q_ref[:, sl].astype(F32)
            lb = lb_all[:, sl]
            sg, f, k, b, sq = _hg_gates(f_ref[:, sl].astype(F32), q_raw, lb, tri)
            v = i_ref[:, sl].astype(BF16)
            gls = [b[(g + 1) * HG_CHUNK - 1:(g + 1) * HG_CHUNK] for g in range(HG_GROUP)]
            eb = jnp.exp(b)
            enb = jnp.exp(-b)
            egb = jnp.exp(_per_chunk(gls) - b)
            ke = k * egb
            qd_b, kd_b, ke_b = (q_raw * sq * eb).astype(BF16), (k * enb).astype(BF16), ke.astype(BF16)
            dov = do_ref[:, sl].astype(BF16)
            a = jnp.where(causal, _dot_nt(qd_b, kd_b), 0.0).astype(BF16)
            da = jnp.where(causal, _dot_nt(dov, v), 0.0).astype(BF16)
            dkd = _dot_tn(da, qd_b)
            dst = ds_scr[h]
            dqd_s, dv_s, dke_s, dgl_s = [None] * HG_GROUP, [None] * HG_GROUP, [None] * HG_GROUP, [None] * HG_GROUP
            for g in reversed(range(HG_GROUP)):
                rs = slice(g * HG_CHUNK, (g + 1) * HG_CHUNK)
                st = st_ref[g, sl, :]
                dst_b = dst.astype(BF16)
                egl = jnp.exp(gls[g])
                dqd_s[g] = _dot(dov[rs], st.astype(BF16))
                dv_s[g] = _dot_nt(ke_b[rs], dst_b)
                dke_s[g] = _dot(v[rs], dst_b)
                dgl_s[g] = jnp.sum(dst * st, axis=0, keepdims=True) * egl
                dst = _dot_tn(dov[rs], qd_b[rs]) + dst * egl
            ds_scr[h] = dst
            dqd = _dot(da, kd_b) + jnp.concatenate(dqd_s, axis=0)
            dv = _dot_tn(a, dov) + jnp.concatenate(dv_s, axis=0)
            dke = jnp.concatenate(dke_s, axis=0)
            t1 = dke * ke
            db = dqd * qd_b.astype(F32) - dkd * kd_b.astype(F32) - t1
            dgl = _per_chunk([dgl_s[g] + jnp.sum(t1[g * HG_CHUNK:(g + 1) * HG_CHUNK], axis=0, keepdims=True)
                              for g in range(HG_GROUP)])
            dlf = _chunk_cumsum(db, tri_t) + dgl
            df = dlf / f - (dkd * enb + dke * egb)
            d_ref[:, sl] = (dqd * eb * sq * (1.0 + q_raw * (1.0 - sq))).astype(BF16)
            d_ref[:, HG_WIDTH + h * HG_HEAD:HG_WIDTH + (h + 1) * HG_HEAD] = (
                df * (1.0 - lb) * sg * (1.0 - sg)).astype(BF16)
            d_ref[:, 2 * HG_WIDTH + h * HG_HEAD:2 * HG_WIDTH + (h + 1) * HG_HEAD] = dv.astype(BF16)
            dlb.append(jnp.sum(df * (1.0 - sg), axis=0, keepdims=True))
        _acc(dlb_ref, jnp.concatenate(dlb, axis=1), first)

    rev = lambda j: pl.BlockSpec((rows, HG_WIDTH), lambda c: (ng - 1 - c, j))
    return _call(body, name=name, grid=(ng,),
                 in_specs=[rev(0), rev(1), rev(2), pl.BlockSpec((2, HG_WIDTH), lambda c: (0, 0)),
                           pl.BlockSpec((HG_GROUP, HG_WIDTH, HG_HEAD), lambda c: (ng - 1 - c, 0, 0)), rev(0)],
                 out_specs=[pl.BlockSpec((rows, 3 * HG_WIDTH), lambda c: (ng - 1 - c, 0)), _vec(HG_WIDTH)],
                 out_shape=[_sds((t, 3 * HG_WIDTH), BF16), _sds((1, HG_WIDTH))],
                 scratch_shapes=[pltpu.VMEM((nh, HG_HEAD, HG_HEAD), F32)])(proj, proj, proj, lb_logits, states, do)


def _to_sub(a, dil):
    t, w = a.shape
    return a if dil == 1 else a.reshape(t // dil, dil, w).transpose(1, 0, 2).reshape(t, w)


def _from_sub(a, dil):
    t, w = a.shape
    return a if dil == 1 else a.reshape(dil, t // dil, w).transpose(1, 0, 2).reshape(t, w)


def _att_mask(has_prev):
    qi = lax.broadcasted_iota(jnp.int32, (2 * ATT_BLOCK, 2 * ATT_BLOCK), 0) % ATT_BLOCK
    kj = lax.broadcasted_iota(jnp.int32, (2 * ATT_BLOCK, 2 * ATT_BLOCK), 1)
    prev = jnp.logical_and(jnp.logical_and(kj < ATT_BLOCK, kj >= qi), has_prev)
    cur = jnp.logical_and(kj >= ATT_BLOCK, kj - ATT_BLOCK <= qi)
    return jnp.logical_or(prev, cur), lax.broadcasted_iota(jnp.int32, (1, 128), 1)


def _attn_fwd(qkv, dil, name):
    t = qkv.shape[0]
    nb = t // ATT_BLOCK
    bps = nb // dil

    def body(q_ref, kc_ref, kp_ref, vc_ref, vp_ref, o_ref, l_ref):
        mask, lane = _att_mask((pl.program_id(0) % bps) != 0)
        lo = lane < 64
        nq = ATT_BLOCK
        lse_all = jnp.zeros((nq, 128), F32)
        for hp in range(ATT_HEADS // 2):
            sl = slice(hp * 128, (hp + 1) * 128)
            q2 = q_ref[:, sl]
            zero = jnp.zeros_like(q2)
            qs = jnp.concatenate([jnp.where(lo, q2, zero), jnp.where(lo, zero, q2)], axis=0)
            kk = jnp.concatenate([kp_ref[:, sl], kc_ref[:, sl]], axis=0)
            vv = jnp.concatenate([vp_ref[:, sl], vc_ref[:, sl]], axis=0)
            s = jnp.where(mask, _dot_nt(qs, kk) * 0.125, NEG)
            mx = jnp.max(s, axis=-1, keepdims=True)
            p = jnp.exp(s - mx)
            l = jnp.sum(p, axis=-1, keepdims=True)
            o = _dot(p.astype(BF16), vv) * (1.0 / l)
            o_ref[:, sl] = jnp.where(lo, o[:nq], o[nq:])
            lse = mx + jnp.log(l)
            lse_all = jnp.where(lane == 2 * hp, lse[:nq], lse_all)
            lse_all = jnp.where(lane == 2 * hp + 1, lse[nq:], lse_all)
        l_ref[...] = lse_all

    col0 = qkv.shape[1] // ATT_WIDTH - 3
    blk = lambda j, back: pl.BlockSpec((ATT_BLOCK, ATT_WIDTH), lambda n: (jnp.maximum(n - back, 0), col0 + j))
    return _call(body, name=name, grid=(nb,),
                 in_specs=[blk(0, 0), blk(1, 0), blk(1, 1), blk(2, 0), blk(2, 1)],
                 out_specs=[pl.BlockSpec((ATT_BLOCK, ATT_WIDTH), lambda n: (n, 0)),
                            pl.BlockSpec((ATT_BLOCK, 128), lambda n: (n, 0))],
                 out_shape=[_sds((t, ATT_WIDTH)), _sds((t, 128))])(qkv, qkv, qkv, qkv, qkv)


def _attn_combine(os_, ls_, name, tr=256):
    t = os_[0].shape[0]
    nbr = len(os_)

    def body(*refs):
        o_refs, l_refs, (a_ref, lt_ref) = refs[:nbr], refs[nbr:2 * nbr], refs[2 * nbr:]
        lane = lax.broadcasted_iota(jnp.int32, (1, 128), 1)
        ls = [r[...] for r in l_refs]
        mx = functools.reduce(jnp.maximum, ls)
        tot = mx + jnp.log(sum(jnp.exp(l - mx) for l in ls))
        lt_ref[...] = tot
        ws = [jnp.exp(l - tot) for l in ls]
        for hp in range(ATT_HEADS // 2):
            sl = slice(hp * 128, (hp + 1) * 128)
            acc = jnp.zeros((tr, 128), F32)
            for w, o_ref in zip(ws, o_refs):
                wf = jnp.where(lane < 64, w[:, 2 * hp:2 * hp + 1], w[:, 2 * hp + 1:2 * hp + 2])
                acc = acc + wf * o_ref[:, sl]
            a_ref[:, sl] = acc

    return _call(body, name=name, grid=(t // tr,),
                 in_specs=[_rows(tr, ATT_WIDTH)] * nbr + [_rows(tr, 128)] * nbr,
                 out_specs=[_rows(tr, ATT_WIDTH), _rows(tr, 128)],
                 out_shape=[_sds((t, ATT_WIDTH)), _sds((t, 128))])(*os_, *ls_)


def _attn_bwd(qkv, dout, lse, dd, dil, name):
    t = qkv.shape[0]
    nb = t // ATT_BLOCK
    bps = nb // dil

    w = ATT_WIDTH
    nq = ATT_BLOCK

    def body(q_ref, kc_ref, kp_ref, vc_ref, vp_ref, do_ref, l_ref, d_ref, dq_ref, dkv_ref, carry):
        n = pl.program_id(0)

        @pl.when(n == 0)
        def _():
            carry[...] = jnp.zeros_like(carry)

        @pl.when(n < nb)
        def _():
            mask, lane = _att_mask((n % bps) != 0)
            lo = lane < 64
            for hp in range(ATT_HEADS // 2):
                sl = slice(hp * 128, (hp + 1) * 128)
                sv = slice(w + hp * 128, w + (hp + 1) * 128)
                q2, do2 = q_ref[:, sl], do_ref[:, sl]
                zero = jnp.zeros_like(q2)
                qs = jnp.concatenate([jnp.where(lo, q2, zero), jnp.where(lo, zero, q2)], axis=0)
                dos = jnp.concatenate([jnp.where(lo, do2, zero), jnp.where(lo, zero, do2)], axis=0)
                kk = jnp.concatenate([kp_ref[:, sl], kc_ref[:, sl]], axis=0)
                vv = jnp.concatenate([vp_ref[:, sl], vc_ref[:, sl]], axis=0)
                ls = jnp.concatenate([l_ref[:, 2 * hp:2 * hp + 1], l_ref[:, 2 * hp + 1:2 * hp + 2]], axis=0)
                dh = jnp.concatenate([d_ref[:, 2 * hp:2 * hp + 1], d_ref[:, 2 * hp + 1:2 * hp + 2]], axis=0)
                p = jnp.exp(jnp.where(mask, _dot_nt(qs, kk) * 0.125 - ls, NEG))
                ds = (p * (_dot_nt(dos, vv) - dh)).astype(BF16)
                dq = _dot(ds, kk) * 0.125
                dq_ref[:, sl] = jnp.where(lo, dq[:nq], dq[nq:]).astype(BF16)
                dk = _dot_tn(ds, qs) * 0.125
                dv = _dot_tn(p.astype(BF16), dos)
                dkv_ref[:, sl] = (carry[:, sl] + dk[:nq]).astype(BF16)
                dkv_ref[:, sv] = (carry[:, sv] + dv[:nq]).astype(BF16)
                carry[:, sl] = dk[nq:]
                carry[:, sv] = dv[nq:]

        @pl.when(n == nb)
        def _():
            dkv_ref[...] = carry[...].astype(BF16)

    def blk(width, j, back):
        return pl.BlockSpec((nq, width), lambda n: (jnp.clip(n - back, 0, nb - 1), j))

    c0 = qkv.shape[1] // w - 3
    return _call(body, name=name, grid=(nb + 1,),
                 in_specs=[blk(w, c0, 0), blk(w, c0 + 1, 0), blk(w, c0 + 1, 1), blk(w, c0 + 2, 0), blk(w, c0 + 2, 1),
                           blk(w, 0, 0), blk(128, 0, 0), blk(128, 0, 0)],
                 out_specs=[blk(w, 0, 0), blk(2 * w, 0, 1)],
                 out_shape=[_sds((t, w), BF16), _sds((t, 2 * w), BF16)],
                 scratch_shapes=[pltpu.VMEM((nq, 2 * w), F32)])(qkv, qkv, qkv, qkv, qkv, dout, lse, dd)


def _local_step(x, tgt, mod, norm1_g, lb_logits, og, ag, norm2_g, fg, get_w, put_g, late=lambda a: a):
    shift1, scale1, gate1, shift2, scale2, gate2 = [mod[:, i * D_MODEL:(i + 1) * D_MODEL] for i in range(6)]
    fg = fg.reshape(1, D_MODEL)

    h1 = _norm_mod(x, norm1_g, scale1, shift1, "norm_mod1")
    w_in = get_w("w_in", h1)
    proj = _mm_nn(h1, w_in, "mm_in", out_dtype=BF16)
    o_hg, states = _hgrn_fwd(proj, lb_logits, "hgrn_fwd")
    qkvs = [proj if d == 1 else _to_sub(proj[:, 4 * HG_WIDTH:], d) for d in DILATIONS]
    outs = [_attn_fwd(q, d, f"attn_fwd{d}") for q, d in zip(qkvs, DILATIONS)]
    att, lse = _attn_combine([_from_sub(o, d) for (o, _), d in zip(outs, DILATIONS)],
                             [_from_sub(l, d) for (_, l), d in zip(outs, DILATIONS)], "attn_combine")
    mixin = _mix_in(o_hg, proj, att, og, ag, "mix_in")
    w_out = get_w("w_out", mixin)
    mix = _mm_nn(mixin, w_out, "mm_out")
    x2, h2 = _resid_norm_mod(x, mix, gate1, norm2_g, scale2, shift2, "resid_norm_mod2")
    w_gu = get_w("w_gu", h2)
    a_ff, u_ff, act = _mm_gate_up(h2, w_gu, "mm_gu")
    w_down = get_w("w_down", act)
    ffn = _mm_nn(act, w_down, "mm_down")
    dx3, dffn, loss_v, dfg, dgate2 = _final_loss(x2, ffn, gate2, fg, tgt, "final_loss")

    put_g("w_down", *_mm_tn(act, dffn, 1, "mm_down_dw", tm=2048, tk=D_FF // 2))
    dau = _mm_down_dx(dffn, w_down, a_ff, u_ff, "mm_down_dx")
    put_g("w_gu", *_mm_tn(h2, dau, N_SHARD, "mm_gu_dw", tm=x.shape[0], tk=512))
    dh2 = _mm_nt(dau, w_gu, "mm_gu_dx")
    dx2, dshift2, dscale2, dg2, dgate1, dmix = _norm_mod_bwd(
        dh2, x2, norm2_g, scale2, dx3, "norm_mod2_bwd", gate=gate1, mix=mix)
    put_g("w_out", *_mm_tn(mixin, dmix, 1, "mm_out_dw", tm=x.shape[0], tk=512))
    dmixin = _mm_nt(dmix, w_out, "mm_out_dx", tm=1024)
    do_hg, dg_raw, datt, dd, dog, dag = _mix_in_bwd(dmixin, o_hg, proj, att, og, ag, "mix_in_bwd")
    datt_b = datt.astype(BF16)
    datts = [_attn_bwd(q, _to_sub(datt_b, d), _to_sub(lse, d), _to_sub(dd, d), d, f"attn_bwd{d}")
             for q, d in zip(qkvs, DILATIONS)]
    dhg, dlb = _hgrn_bwd(proj, lb_logits, states, do_hg, "hgrn_bwd")
    dhg = late(dhg)
    dproj = _dproj(dhg, dg_raw, [_from_sub(dq, d) for (dq, _), d in zip(datts, DILATIONS)],
                   [_from_sub(dkv, d) for (_, dkv), d in zip(datts, DILATIONS)], "dproj")
    put_g("w_in", *_mm_tn(h1, dproj, N_SHARD, "mm_in_dw", tm=x.shape[0], tk=512))
    dh1 = _mm_nt(dproj, w_in, "mm_in_dx", tm=1024)
    dx, dshift1, dscale1, dg1 = _norm_mod_bwd(dh1, x, norm1_g, scale1, dx2, "norm_mod1_bwd")

    stats = jnp.concatenate([loss_v, dfg, dg2, dg1, dlb, dag, dog,
                             dshift1, dscale1, dgate1, dshift2, dscale2, dgate2], axis=1)
    return dx, stats


def _place():
    x, y, c = lax.axis_index("x"), lax.axis_index("y"), lax.axis_index("c")
    return x, y, c


def _chip_peers(x, y, c):
    return [(1 - x, y, c), (x, 1 - y, c), (1 - x, 1 - y, c)]


def _comm_call(body, name, n_in, out_shape, scratch_shapes):
    hbm = pl.BlockSpec(memory_space=pl.ANY)
    return pl.pallas_call(body, name=name, in_specs=[hbm] * n_in, out_specs=[hbm] * len(out_shape),
                          out_shape=out_shape, scratch_shapes=scratch_shapes)


_HBM = pl.BlockSpec(memory_space=pltpu.HBM)
_SEM = pl.BlockSpec(memory_space=pltpu.SEMAPHORE)
_EFFECT = pltpu.SideEffectType.DATAFLOW_SIDE_EFFECTING


def _exchange_copy(bufs, send, recv, j, peer, place, gather):
    if gather:
        src = dst = bufs[0].at[2 * place[0] + place[1]]
    else:
        src, dst = bufs[0].at[2 * peer[0] + peer[1]], bufs[1].at[j]
    return pltpu.make_async_remote_copy(src_ref=src, dst_ref=dst, send_sem=send.at[j], recv_sem=recv.at[j],
                                        device_id=peer, device_id_type=MESH)


def _exchange_start(groups, after, gather, name):
    sizes = [len(g) for g in groups]
    flat = [b for g in groups for b in g]
    ng, nb = len(groups), len(flat)

    def body(*refs):
        bufs, sems = refs[:nb], refs[nb + 1:nb + 1 + 2 * ng]
        x, y, c = _place()
        for j, peer in enumerate(_chip_peers(x, y, c)):
            at = 0
            for i, size in enumerate(sizes):
                _exchange_copy(bufs[at:at + size], sems[2 * i], sems[2 * i + 1], j, peer, (x, y), gather).start()
                at += size

    any_space = pl.BlockSpec(memory_space=pl.ANY)
    out = pl.pallas_call(
        body, name=name, in_specs=[_HBM] * nb + [any_space],
        out_specs=[_SEM] * (2 * ng) + [_HBM] * nb + [any_space],
        out_shape=[pltpu.SemaphoreType.DMA((3,))] * (2 * ng) + [pltpu.HBM(b.shape, b.dtype) for b in flat]
        + [_sds(after.shape, after.dtype)],
        input_output_aliases={i: 2 * ng + i for i in range(nb + 1)},
        compiler_params=pltpu.CompilerParams(has_side_effects=_EFFECT),
    )(*[pltpu.with_memory_space_constraint(b, pltpu.HBM) for b in flat], after)
    started, at = [], 2 * ng
    for i, size in enumerate(sizes):
        started.append((out[2 * i], out[2 * i + 1], tuple(out[at:at + size])))
        at += size
    return started, out[-1]


def _exchange_wait(started, after, gather, name):
    send, recv, bufs = started
    nb = len(bufs)

    def body(*refs):
        x, y, c = _place()
        for j, peer in enumerate(_chip_peers(x, y, c)):
            cp = _exchange_copy(refs[:nb], refs[nb], refs[nb + 1], j, peer, (x, y), gather)
            cp.wait_send()
            cp.wait_recv()

    return pl.pallas_call(
        body, name=name, in_specs=[_HBM] * nb + [_SEM, _SEM, pl.BlockSpec(memory_space=pl.ANY)],
        out_specs=[_HBM] * nb, out_shape=[pltpu.HBM(b.shape, b.dtype) for b in bufs],
        input_output_aliases={i: i for i in range(nb)},
        compiler_params=pltpu.CompilerParams(has_side_effects=_EFFECT),
    )(*bufs, send, recv, after)


def _sibling_copies(v_refs, l_refs, send, recv):
    x, y, c = _place()
    return [pltpu.make_async_remote_copy(src_ref=v, dst_ref=l, send_sem=send.at[a], recv_sem=recv.at[a],
                                         device_id=(x, y, 1 - c), device_id_type=MESH)
            for a, (v, l) in enumerate(zip(v_refs, l_refs))]


def _sibling_start(vs, after, name):
    n = len(vs)
    lands = [lax.empty(v.shape, v.dtype) for v in vs]

    def body(*refs):
        for cp in _sibling_copies(refs[:n], refs[n:2 * n], refs[2 * n + 1], refs[2 * n + 2]):
            cp.start()

    any_space = pl.BlockSpec(memory_space=pl.ANY)
    out = pl.pallas_call(
        body, name=name, in_specs=[_HBM] * (2 * n) + [any_space],
        out_specs=[_SEM, _SEM] + [_HBM] * (2 * n) + [any_space],
        out_shape=[pltpu.SemaphoreType.DMA((n,))] * 2 + [pltpu.HBM(b.shape, b.dtype) for b in vs + lands]
        + [_sds(after.shape, after.dtype)],
        input_output_aliases={i: 2 + i for i in range(2 * n + 1)},
        compiler_params=pltpu.CompilerParams(has_side_effects=_EFFECT),
    )(*[pltpu.with_memory_space_constraint(b, pltpu.HBM) for b in vs + lands], after)
    return (out[0], out[1], tuple(out[2:2 + n]), tuple(out[2 + n:2 + 2 * n])), out[-1]


def _sibling_wait(started, after, name):
    send, recv, vs, lands = started
    n = len(vs)

    def body(*refs):
        for cp in _sibling_copies(refs[:n], refs[n:2 * n], refs[2 * n], refs[2 * n + 1]):
            cp.wait_send()
            cp.wait_recv()

    out = pl.pallas_call(
        body, name=name, in_specs=[_HBM] * (2 * n) + [_SEM, _SEM, pl.BlockSpec(memory_space=pl.ANY)],
        out_specs=[_HBM] * (2 * n), out_shape=[pltpu.HBM(b.shape, b.dtype) for b in vs + lands],
        input_output_aliases={i: i for i in range(2 * n)},
        compiler_params=pltpu.CompilerParams(has_side_effects=_EFFECT),
    )(*vs, *lands, send, recv, after)
    return out[:n], out[n:]


def _swap_sibling(vs, name):
    n = len(vs)

    def body(*refs):
        v_refs, o_refs, (send, recv) = refs[:n], refs[n:2 * n], refs[2 * n:]
        x, y, c = _place()
        cps = [pltpu.make_async_remote_copy(
            src_ref=v_refs[a], dst_ref=o_refs[a], send_sem=send.at[a], recv_sem=recv.at[a],
            device_id=(x, y, 1 - c), device_id_type=MESH) for a in range(n)]
        for cp in cps:
            cp.start()
        for cp in cps:
            cp.wait()

    return _comm_call(body, name, n, [_sds(v.shape, v.dtype) for v in vs],
                      [pltpu.SemaphoreType.DMA((n,)), pltpu.SemaphoreType.DMA((n,))])(*vs)


def _gather_rows(v, name):
    r, n = v.shape

    def body(v_ref, o_ref, send, recv, loc):
        x, y, c = _place()
        me = 4 * x + 2 * y + c
        own = pltpu.make_async_copy(v_ref, o_ref.at[me], loc)
        own.start()
        peers = []
        for k in range(1, 8):
            px = 1 - x if k & 4 else x
            py = 1 - y if k & 2 else y
            pc = 1 - c if k & 1 else c
            peers.append((px, py, pc))
        sends = []
        for k, peer in enumerate(peers):
            cp = pltpu.make_async_remote_copy(src_ref=v_ref, dst_ref=o_ref.at[me], send_sem=send.at[k],
                                              recv_sem=recv.at[k], device_id=peer, device_id_type=MESH)
            cp.start()
            sends.append(cp)
        for k, peer in enumerate(peers):
            pltpu.make_async_remote_copy(src_ref=v_ref, dst_ref=o_ref.at[4 * peer[0] + 2 * peer[1] + peer[2]],
                                         send_sem=send.at[k], recv_sem=recv.at[k], device_id=peer,
                                         device_id_type=MESH).wait_recv()
        for cp in sends:
            cp.wait_send()
        own.wait()

    vmem = pl.BlockSpec(memory_space=pltpu.VMEM)
    return pl.pallas_call(body, name=name, in_specs=[vmem], out_specs=vmem, out_shape=_sds((8, r, n), v.dtype),
                          scratch_shapes=[pltpu.SemaphoreType.DMA((7,)), pltpu.SemaphoreType.DMA((7,)),
                                          pltpu.SemaphoreType.DMA])(v)


def _cast_place(w, shard, name):
    r, c = w.shape
    tr = r // 4

    def body(s_ref, w_ref, o_ref):
        o_ref[0] = w_ref[...].astype(BF16)

    return pl.pallas_call(
        body, name=name, out_shape=_sds((N_SHARD, r, c), BF16),
        grid_spec=pltpu.PrefetchScalarGridSpec(
            num_scalar_prefetch=1, grid=(4,), in_specs=[pl.BlockSpec((tr, c), lambda i, s: (i, 0))],
            out_specs=pl.BlockSpec((1, tr, c), lambda i, s: (s[0], i, 0))),
        compiler_params=pltpu.CompilerParams(dimension_semantics=("arbitrary",)),
    )(shard.reshape(1).astype(jnp.int32), w)


def _mod_part(c_all, w_ada, b_ada, name):
    n = w_ada.shape[1]

    def body(c_ref, w_ref, b_ref, a_ref, p_ref):
        cv = c_ref[...]
        ca = cv * _sigmoid(cv)
        a_ref[...] = ca
        p_ref[...] = jnp.dot(ca, w_ref[...], precision=lax.Precision.HIGHEST, preferred_element_type=F32) + b_ref[...]

    full = lambda a: pl.BlockSpec(a.shape, lambda i: (0, 0))
    return _call(body, name=name, grid=(1,), in_specs=[full(c_all), full(w_ada), full(b_ada)],
                 out_specs=[pl.BlockSpec((8, D_MODEL), lambda i: (0, 0)), pl.BlockSpec((8, n), lambda i: (0, 0))],
                 out_shape=[_sds((8, D_MODEL)), _sds((8, n))])(c_all, w_ada, b_ada)


def _sum_received(g_own, land, name):
    r, c = g_own.shape
    tr = r // 4

    def body(g_ref, l_ref, o_ref):
        o_ref[...] = ((g_ref[...] + l_ref[0].astype(F32)) + l_ref[1].astype(F32)) + l_ref[2].astype(F32)

    return _call(body, name=name, grid=(4,),
                 in_specs=[_rows(tr, c), pl.BlockSpec((3, tr, c), lambda i: (0, i, 0))],
                 out_specs=_rows(tr, c), out_shape=_sds((r, c)))(g_own, land)


def _outer_sum(ct, dm, name):
    k, n = ct.shape[0], dm.shape[1]
    tr = k // 4

    def body(c_ref, d_ref, o_ref):
        cv = c_ref[...]
        dv = d_ref[...]
        acc = cv[:, 0:1] * dv[0:1, :]
        for i in range(1, 8):
            acc = acc + cv[:, i:i + 1] * dv[i:i + 1, :]
        o_ref[...] = acc

    return _call(body, name=name, grid=(4,), in_specs=[_rows(tr, 8), pl.BlockSpec((8, n), lambda i: (0, 0))],
                 out_specs=_rows(tr, n), out_shape=_sds((k, n)))(ct, dm)


def _small_grads(stats, lb_logits, name):
    def body(s_ref, lg_ref, g_ref, l_ref):
        tot = s_ref[0:1, :]
        for i in range(1, 8):
            tot = tot + s_ref[i:i + 1, :]
        part = lambda off, n: tot[:, off:off + n]
        l_ref[...] = jnp.zeros((1, 128), F32) + (0.5 / D_MODEL) * jnp.sum(part(ST_LOSS, D_MODEL))
        lg = lg_ref[...]
        lb = _sigmoid(lg[0:1] - lg[1:2])
        dl0 = part(ST_DLB, HG_WIDTH) * lb * (1.0 - lb)
        g_ref[:, SP_BADA:SP_BADA + 6 * D_MODEL] = part(ST_DMOD, 6 * D_MODEL)
        g_ref[:, SP_N1:SP_N1 + D_MODEL] = part(ST_DG1, D_MODEL)
        g_ref[:, SP_LB:SP_LB + HG_WIDTH] = dl0
        g_ref[:, SP_LB + HG_WIDTH:SP_LB + 2 * HG_WIDTH] = -dl0
        g_ref[:, SP_OG:SP_OG + HG_HEAD] = part(ST_DOG, HG_HEAD)
        g_ref[:, SP_AG:SP_AG + ATT_WIDTH] = part(ST_DAG, ATT_WIDTH)
        g_ref[:, SP_N2:SP_N2 + D_MODEL] = part(ST_DG2, D_MODEL)
        g_ref[:, SP_FG:SP_FG + D_MODEL] = part(ST_DFG, D_MODEL)

    return _call(body, name=name, grid=(1,),
                 in_specs=[pl.BlockSpec((8, ST_WIDTH), lambda i: (0, 0)), pl.BlockSpec((2, HG_WIDTH), lambda i: (0, 0))],
                 out_specs=[pl.BlockSpec((1, SP_WIDTH), lambda i: (0, 0)), pl.BlockSpec((1, 128), lambda i: (0, 0))],
                 out_shape=[_sds((1, SP_WIDTH)), _sds((1, 128))])(stats, lb_logits)


def _adamw(w, gs, m, v, name, steps=4):
    r, c = w.shape
    tr = r // steps
    ng = len(gs)

    def body(*refs):
        w_ref, g_refs, (m_ref, v_ref, g_out, d_out, m_out, v_out) = refs[0], refs[1:1 + ng], refs[1 + ng:]
        g = g_refs[0][...]
        for g_ref in g_refs[1:]:
            g = g + g_ref[...]
        m_new = ADAM_B1 * m_ref[...] + (1.0 - ADAM_B1) * g
        v_new = ADAM_B2 * v_ref[...] + (1.0 - ADAM_B2) * (g * g)
        m_hat = m_new / (1.0 - ADAM_B1 ** ADAM_STEP)
        v_hat = v_new / (1.0 - ADAM_B2 ** ADAM_STEP)
        g_out[...] = g
        d_out[...] = -ADAM_LR * (m_hat / (jnp.sqrt(v_hat) + ADAM_EPS) + ADAM_WD * w_ref[...])
        m_out[...] = m_new
        v_out[...] = v_new

    row = _rows(tr, c)
    return _call(body, name=name, grid=(steps,), in_specs=[row] * (3 + ng), out_specs=[row] * 4,
                 out_shape=[_sds((r, c))] * 4)(w, *gs, m, v)


def kernel(x, c, w_ada, b_ada, norm1_g, w_in, hg_lb_logits, hg_onorm_g, att_onorm_g, w_out, norm2_g, w_gate_up, w_down, final_g, loss_target, m_w_ada, m_b_ada, m_norm1_g, m_w_in, m_hg_lb_logits, m_hg_onorm_g, m_att_onorm_g, m_w_out, m_norm2_g, m_w_gate_up, m_w_down, m_final_g, v_w_ada, v_b_ada, v_norm1_g, v_w_in, v_hg_lb_logits, v_hg_onorm_g, v_att_onorm_g, v_w_out, v_norm2_g, v_w_gate_up, v_w_down, v_final_g):
    ix, iy, ic = _place()
    shard = 2 * ix + iy
    sample = 4 * ix + 2 * iy + ic
    n_ada = w_ada.shape[2]

    shards = [w_in[0], w_out[0], w_gate_up[0], w_down[0]]
    names = ["w_in", "w_out", "w_gu", "w_down"]
    shapes = [(N_SHARD,) + w.shape for w in shards]
    placed = [(_cast_place(w, shard, "place_" + nm),) for w, nm in zip(shards, names)]

    c_all = _gather_rows(c, "gather_c").reshape(8, D_MODEL)
    b_part = lax.dynamic_slice(b_ada, (0, shard * n_ada), (1, n_ada))
    c_act, part = _mod_part(c_all, w_ada[0], b_part, "mod_part")
    parts = _gather_rows(part, "gather_mod")[::2]
    mod = lax.dynamic_index_in_dim(parts, sample, axis=1, keepdims=False).reshape(1, 6 * D_MODEL)
    (first,), mod = _exchange_start(placed[:1], mod, True, "gather_start_w_in")
    gathering = {"w_in": first}

    def get_w(name, after):
        (full,) = _exchange_wait(gathering[name], after, True, "gather_wait_" + name)
        if name == "w_in":
            rest, full = _exchange_start(placed[1:], full, True, "gather_start_rest")
            gathering.update(zip(names[1:], rest))
        return full if name in ("w_in", "w_gu") else full.reshape(1, -1, D_MODEL)

    scattering = {}

    def put_g(name, g, g_bf16):
        shape = shapes[names.index(name)]
        land = lax.empty((3,) + shape[1:], BF16)
        (started,), g = _exchange_start([(g_bf16.reshape(shape), land)], g, False, "scatter_start_" + name)
        scattering[name] = (g.reshape(shape), started)

    def summed(name, after):
        g, started = scattering[name]
        _, land = _exchange_wait(started, after, False, "scatter_wait_" + name)
        own = lax.dynamic_index_in_dim(g, shard, axis=0, keepdims=False)
        return _sum_received(own, land, "sum_" + name)

    early = ["w_down", "w_gu", "w_out"]
    swapping = []

    def late(a):
        started, a = _sibling_start([summed(nm, a) for nm in early], a, "swap_start")
        swapping.append(started)
        return a

    dx, stats = _local_step(x[0], loss_target[0], mod, norm1_g, hg_lb_logits, hg_onorm_g, att_onorm_g,
                            norm2_g, final_g, get_w, put_g, late)

    stats_all = _gather_rows(stats, "gather_stats").reshape(8, ST_WIDTH)
    g_small, loss = _small_grads(stats_all, hg_lb_logits, "small_grads")
    dmod = lax.dynamic_slice(stats_all, (0, ST_DMOD + shard * n_ada), (8, n_ada))
    g_ada = _outer_sum(c_act.T, dmod, "w_ada_grad")

    smalls = [(b_ada, m_b_ada, v_b_ada), (norm1_g, m_norm1_g, v_norm1_g),
              (hg_lb_logits, m_hg_lb_logits, v_hg_lb_logits), (hg_onorm_g, m_hg_onorm_g, v_hg_onorm_g),
              (att_onorm_g, m_att_onorm_g, v_att_onorm_g), (norm2_g, m_norm2_g, v_norm2_g),
              (final_g, m_final_g, v_final_g)]
    pack = lambda i: jnp.concatenate([t[i].reshape(1, -1) for t in smalls], axis=1)
    small_out = _adamw(pack(0), [g_small], pack(1), pack(2), "adamw_small", steps=1)
    offs = [SP_BADA, SP_N1, SP_LB, SP_OG, SP_AG, SP_N2, SP_FG, SP_WIDTH]
    unpack = lambda a: [a[0, offs[i]:offs[i + 1]].reshape(smalls[i][0].shape) for i in range(7)]
    sg, sd, sm, sv = [unpack(a) for a in small_out]

    ada = _adamw(w_ada[0], [g_ada], m_w_ada[0], v_w_ada[0], "adamw_w_ada")
    moments = [(m_w_in, v_w_in), (m_w_out, v_w_out), (m_w_gate_up, v_w_gate_up), (m_w_down, v_w_down)]

    def update(group, sums, other):
        return {nm: _adamw(shards[names.index(nm)], [s, o], moments[names.index(nm)][0][0],
                           moments[names.index(nm)][1][0], "adamw_" + nm) for nm, s, o in zip(group, sums, other)}

    sums, other = _sibling_wait(swapping[0], ada[1], "swap_wait")
    done = update(early, sums, other)
    sum_in = summed("w_in", done["w_out"][1])
    done.update(update(["w_in"], [sum_in], _swap_sibling([sum_in], "swap_sum_in")))
    big = [ada] + [done[nm] for nm in names]
    bg, bd, bm, bv = [[t[i][None] for t in big] for i in range(4)]

    def order(b, s):
        return [b[0], s[0], s[1], b[1], s[2], s[3], s[4], b[2], s[5], b[3], b[4], s[6]]

    return (loss[0, 0], dx[None], *order(bg, sg), *order(bd, sd), *order(bm, sm), *order(bv, sv))
```

```python
import functools

import jax
import jax.numpy as jnp
from jax import lax
from jax.experimental import pallas as pl
from jax.experimental.pallas import tpu as pltpu

F32 = jnp.float32
BF16 = jnp.bfloat16
MESH = pl.DeviceIdType.MESH

D_MODEL = 1024
HG_WIDTH = 512
HG_HEAD = 128
HG_CHUNK = 64
HG_GROUP = 4
ATT_WIDTH = 512
ATT_HEADS = 8
ATT_BLOCK = 128
DILATIONS = (1, 4, 16)
D_FF = 2816
IN_WIDTH = 3584
N_SHARD = 4
RMS_EPS = 1e-6
NEG = -1e30

ADAM_LR = 0.001
ADAM_B1 = 0.9
ADAM_B2 = 0.999
ADAM_EPS = 1e-08
ADAM_WD = 0.01
ADAM_STEP = 10

VMEM_LIMIT = 56 * 2**20

ST_LOSS, ST_DFG, ST_DG2, ST_DG1 = 0, 1024, 2048, 3072
ST_DLB, ST_DAG, ST_DOG, ST_DMOD = 4096, 4608, 5120, 5248
ST_WIDTH = 5248 + 6144
SP_BADA, SP_N1, SP_LB, SP_OG, SP_AG, SP_N2, SP_FG = 0, 6144, 7168, 8192, 8320, 8832, 9856
SP_WIDTH = 10880


def _call(body, *, name, grid, in_specs, out_specs, out_shape, scratch_shapes=()):
    return pl.pallas_call(
        body, name=name, grid=grid, in_specs=in_specs, out_specs=out_specs, out_shape=out_shape,
        scratch_shapes=list(scratch_shapes),
        compiler_params=pltpu.CompilerParams(
            dimension_semantics=("arbitrary",) * len(grid), vmem_limit_bytes=VMEM_LIMIT))


def _sds(shape, dtype=F32):
    return jax.ShapeDtypeStruct(shape, dtype)


def _dot(a, b):
    return jnp.dot(a, b, preferred_element_type=F32)


def _dot_nt(a, b):
    return lax.dot_general(a, b, (((1,), (1,)), ((), ())), preferred_element_type=F32)


def _dot_tn(a, b):
    return lax.dot_general(a, b, (((0,), (0,)), ((), ())), preferred_element_type=F32)


def _sigmoid(x):
    return 1.0 / (1.0 + jnp.exp(-x))


def _rows(tr, width):
    return pl.BlockSpec((tr, width), lambda i: (i, 0))


def _vec(width):
    return pl.BlockSpec((1, width), lambda i: (0, 0))


def _acc(ref, val, first):
    @pl.when(first)
    def _():
        ref[...] = val

    @pl.when(jnp.logical_not(first))
    def _():
        ref[...] += val


def _mm_nn(a, b3, name, tm=1024, out_dtype=F32):
    m, k = a.shape
    s, _, n = b3.shape

    def body(a_ref, b_ref, o_ref):
        o_ref[...] = _dot(a_ref[...], b_ref[0]).astype(out_dtype)

    return _call(
        body, name=name, grid=(s, m // tm),
        in_specs=[pl.BlockSpec((tm, k), lambda j, i: (i, 0)), pl.BlockSpec((1, k, n), lambda j, i: (j, 0, 0))],
        out_specs=pl.BlockSpec((tm, n), lambda j, i: (i, j)), out_shape=_sds((m, s * n), out_dtype))(a, b3)


def _mm_nt(dy, b3, name, tm=512):
    m = dy.shape[0]
    s, k, n = b3.shape

    def body(dy_ref, b_ref, o_ref):
        acc = _dot_nt(dy_ref[:, 0:n], b_ref[0])
        for j in range(1, s):
            acc = acc + _dot_nt(dy_ref[:, j * n:(j + 1) * n], b_ref[j])
        o_ref[...] = acc

    return _call(
        body, name=name, grid=(m // tm,),
        in_specs=[_rows(tm, s * n), pl.BlockSpec((s, k, n), lambda i: (0, 0, 0))],
        out_specs=_rows(tm, k), out_shape=_sds((m, k)))(dy, b3)


def _mm_tn(a, dy, s, name, tm, tk):
    m, k = a.shape
    n = dy.shape[1] // s
    steps = m // tm

    def body(a_ref, dy_ref, o_ref, ob_ref):
        p = _dot_tn(a_ref[...], dy_ref[...])[None]
        if steps == 1:
            o_ref[...] = p
            ob_ref[...] = p.astype(BF16)
        else:
            i = pl.program_id(2)
            _acc(o_ref, p, i == 0)

            @pl.when(i == steps - 1)
            def _():
                ob_ref[...] = o_ref[...].astype(BF16)

    out = pl.BlockSpec((1, tk, n), lambda kk, j, i: (j, kk, 0))
    return _call(
        body, name=name, grid=(k // tk, s, steps),
        in_specs=[pl.BlockSpec((tm, tk), lambda kk, j, i: (i, kk)), pl.BlockSpec((tm, n), lambda kk, j, i: (i, j))],
        out_specs=[out, out], out_shape=[_sds((s, k, n)), _sds((s, k, n), BF16)])(a, dy)


def _rms(x):
    return lax.rsqrt(jnp.mean(x * x, axis=-1, keepdims=True) + RMS_EPS)


def _rms_bwd(dxh, xh, r):
    return r * (dxh - xh * jnp.mean(dxh * xh, axis=-1, keepdims=True))


def _norm_mod(x, g, scale, shift, name, tr=512):
    t = x.shape[0]

    def body(x_ref, g_ref, sc_ref, sh_ref, h_ref):
        xv = x_ref[...]
        n = xv * _rms(xv) * g_ref[...]
        h_ref[...] = (n * (1.0 + sc_ref[...]) + sh_ref[...]).astype(BF16)

    return _call(body, name=name, grid=(t // tr,),
                 in_specs=[_rows(tr, D_MODEL), _vec(D_MODEL), _vec(D_MODEL), _vec(D_MODEL)],
                 out_specs=_rows(tr, D_MODEL), out_shape=_sds((t, D_MODEL), BF16))(x, g, scale, shift)


def _mix_in(o_hg, proj, att, og, ag, name, tr=256):
    t = o_hg.shape[0]

    def body(o_ref, g_ref, a_ref, og_ref, ag_ref, m_ref):
        for h in range(HG_WIDTH // HG_HEAD):
            sl = slice(h * HG_HEAD, (h + 1) * HG_HEAD)
            oh = o_ref[:, sl]
            gv = g_ref[:, sl].astype(F32)
            m_ref[:, sl] = (oh * _rms(oh) * og_ref[...] * (gv * _sigmoid(gv))).astype(BF16)
        av = a_ref[...]
        m_ref[:, HG_WIDTH:] = (av * _rms(av) * ag_ref[...]).astype(BF16)

    return _call(body, name=name, grid=(t // tr,),
                 in_specs=[_rows(tr, HG_WIDTH), pl.BlockSpec((tr, HG_WIDTH), lambda i: (i, 3)), _rows(tr, ATT_WIDTH),
                           _vec(HG_HEAD), _vec(ATT_WIDTH)],
                 out_specs=_rows(tr, D_MODEL), out_shape=_sds((t, D_MODEL), BF16))(o_hg, proj, att, og, ag)


def _resid_norm_mod(x, mix, gate, g, scale, shift, name, tr=256):
    t = x.shape[0]

    def body(x_ref, m_ref, gt_ref, g_ref, sc_ref, sh_ref, x2_ref, h_ref):
        x2 = x_ref[...] + gt_ref[...] * m_ref[...]
        x2_ref[...] = x2
        n = x2 * _rms(x2) * g_ref[...]
        h_ref[...] = (n * (1.0 + sc_ref[...]) + sh_ref[...]).astype(BF16)

    return _call(body, name=name, grid=(t // tr,),
                 in_specs=[_rows(tr, D_MODEL), _rows(tr, D_MODEL)] + [_vec(D_MODEL)] * 4,
                 out_specs=[_rows(tr, D_MODEL), _rows(tr, D_MODEL)],
                 out_shape=[_sds((t, D_MODEL)), _sds((t, D_MODEL), BF16)])(x, mix, gate, g, scale, shift)


def _mm_gate_up(h, w_gu, name, tm=1024):
    m, k = h.shape
    n = w_gu.shape[2]

    def body(h_ref, wa_ref, wu_ref, da_ref, du_ref, o_ref):
        hv = h_ref[...]
        a = _dot(hv, wa_ref[0])
        u = _dot(hv, wu_ref[0])
        sg = _sigmoid(a)
        silu = a * sg
        da_ref[...] = (u * sg * (1.0 + a * (1.0 - sg))).astype(BF16)
        du_ref[...] = silu.astype(BF16)
        o_ref[...] = (silu * u).astype(BF16)

    out = pl.BlockSpec((tm, n), lambda j, i: (i, j))
    return _call(body, name=name, grid=(2, m // tm),
                 in_specs=[pl.BlockSpec((tm, k), lambda j, i: (i, 0)), pl.BlockSpec((1, k, n), lambda j, i: (j, 0, 0)),
                           pl.BlockSpec((1, k, n), lambda j, i: (j + 2, 0, 0))],
                 out_specs=[out, out, out], out_shape=[_sds((m, 2 * n), BF16)] * 3)(h, w_gu, w_gu)


def _mm_down_dx(dffn, w_down, act_da, act_du, name, tm=512):
    m = dffn.shape[0]
    _, k, n = w_down.shape

    def body(d_ref, w_ref, da_ref, du_ref, o_ref):
        dact = _dot_nt(d_ref[...], w_ref[0])
        o_ref[:, :k] = (dact * da_ref[...].astype(F32)).astype(BF16)
        o_ref[:, k:] = (dact * du_ref[...].astype(F32)).astype(BF16)

    return _call(body, name=name, grid=(m // tm,),
                 in_specs=[_rows(tm, n), pl.BlockSpec((1, k, n), lambda i: (0, 0, 0)), _rows(tm, k), _rows(tm, k)],
                 out_specs=_rows(tm, 2 * k), out_shape=_sds((m, 2 * k), BF16))(dffn, w_down, act_da, act_du)


def _final_loss(x2, ffn, gate, fg, tgt, name, tr=256):
    t = x2.shape[0]

    def body(x_ref, f_ref, gt_ref, fg_ref, t_ref, dx_ref, df_ref, l_ref, dfg_ref, dgt_ref):
        first = pl.program_id(0) == 0
        ffn_v = f_ref[...]
        x3 = x_ref[...] + gt_ref[...] * ffn_v
        r = _rms(x3)
        xh = x3 * r
        err = xh * fg_ref[...] - t_ref[...]
        dy = err * (1.0 / D_MODEL)
        dx3 = _rms_bwd(dy * fg_ref[...], xh, r)
        dx_ref[...] = dx3
        df_ref[...] = (dx3 * gt_ref[...]).astype(BF16)
        _acc(l_ref, jnp.sum(err * err, axis=0, keepdims=True), first)
        _acc(dfg_ref, jnp.sum(dy * xh, axis=0, keepdims=True), first)
        _acc(dgt_ref, jnp.sum(dx3 * ffn_v, axis=0, keepdims=True), first)

    row, vec = _rows(tr, D_MODEL), _vec(D_MODEL)
    return _call(body, name=name, grid=(t // tr,), in_specs=[row, row, vec, vec, row],
                 out_specs=[row, row, vec, vec, vec],
                 out_shape=[_sds((t, D_MODEL)), _sds((t, D_MODEL), BF16)] + [_sds((1, D_MODEL))] * 3)(
                     x2, ffn, gate, fg, tgt)


def _norm_mod_bwd(dh, x, g, scale, dres, name, gate=None, mix=None, tr=256):
    t = x.shape[0]
    below = gate is not None

    def body(*refs):
        if below:
            dh_ref, x_ref, g_ref, sc_ref, dr_ref, gt_ref, m_ref, dx_ref, dsh_ref, dsc_ref, dg_ref, dgt_ref, dm_ref = refs
        else:
            dh_ref, x_ref, g_ref, sc_ref, dr_ref, dx_ref, dsh_ref, dsc_ref, dg_ref = refs
        first = pl.program_id(0) == 0
        xv = x_ref[...]
        dhv = dh_ref[...]
        r = _rms(xv)
        xh = xv * r
        dn = dhv * (1.0 + sc_ref[...])
        dx = dr_ref[...] + _rms_bwd(dn * g_ref[...], xh, r)
        dx_ref[...] = dx
        _acc(dsh_ref, jnp.sum(dhv, axis=0, keepdims=True), first)
        _acc(dsc_ref, jnp.sum(dhv * xh * g_ref[...], axis=0, keepdims=True), first)
        _acc(dg_ref, jnp.sum(dn * xh, axis=0, keepdims=True), first)
        if below:
            _acc(dgt_ref, jnp.sum(dx * m_ref[...], axis=0, keepdims=True), first)
            dm_ref[...] = (dx * gt_ref[...]).astype(BF16)

    row, vec = _rows(tr, D_MODEL), _vec(D_MODEL)
    in_specs = [row, row, vec, vec, row] + ([vec, row] if below else [])
    out_specs = [row, vec, vec, vec] + ([vec, row] if below else [])
    out_shape = [_sds((t, D_MODEL))] + [_sds((1, D_MODEL))] * 3 + ([_sds((1, D_MODEL)), _sds((t, D_MODEL), BF16)] if below else [])
    args = (dh, x, g, scale, dres) + ((gate, mix) if below else ())
    return _call(body, name=name, grid=(t // tr,), in_specs=in_specs, out_specs=out_specs, out_shape=out_shape)(*args)


def _mix_in_bwd(dmi, o_hg, proj, att, og, ag, name, tr=256):
    t = o_hg.shape[0]

    def body(d_ref, o_ref, g_ref, a_ref, og_ref, ag_ref, do_ref, dg_ref, da_ref, dd_ref, dog_ref, dag_ref):
        first = pl.program_id(0) == 0
        dog = jnp.zeros((1, HG_HEAD), F32)
        for h in range(HG_WIDTH // HG_HEAD):
            sl = slice(h * HG_HEAD, (h + 1) * HG_HEAD)
            oh = o_ref[:, sl]
            gv = g_ref[:, sl].astype(F32)
            dv = d_ref[:, sl]
            r = _rms(oh)
            xh = oh * r
            sg = _sigmoid(gv)
            dno = dv * gv * sg
            dg_ref[:, sl] = dv * xh * og_ref[...] * sg * (1.0 + gv * (1.0 - sg))
            dog = dog + jnp.sum(dno * xh, axis=0, keepdims=True)
            do_ref[:, sl] = _rms_bwd(dno * og_ref[...], xh, r)
        _acc(dog_ref, dog, first)
        av = a_ref[...]
        dav = d_ref[:, HG_WIDTH:]
        r = _rms(av)
        xa = av * r
        _acc(dag_ref, jnp.sum(dav * xa, axis=0, keepdims=True), first)
        datt = _rms_bwd(dav * ag_ref[...], xa, r)
        da_ref[...] = datt
        prod = datt * av
        lane = lax.broadcasted_iota(jnp.int32, (1, 128), 1)
        dd = jnp.zeros((tr, 128), F32)
        for hp in range(ATT_HEADS // 2):
            pp = prod[:, hp * 128:(hp + 1) * 128]
            lo = jnp.sum(jnp.where(lane < 64, pp, 0.0), axis=-1, keepdims=True)
            hi = jnp.sum(jnp.where(lane >= 64, pp, 0.0), axis=-1, keepdims=True)
            dd = jnp.where(lane == 2 * hp, lo, dd)
            dd = jnp.where(lane == 2 * hp + 1, hi, dd)
        dd_ref[...] = dd

    half = _rows(tr, HG_WIDTH)
    return _call(body, name=name, grid=(t // tr,),
                 in_specs=[_rows(tr, D_MODEL), half, pl.BlockSpec((tr, HG_WIDTH), lambda i: (i, 3)), half,
                           _vec(HG_HEAD), _vec(ATT_WIDTH)],
                 out_specs=[half, half, half, _rows(tr, 128), _vec(HG_HEAD), _vec(ATT_WIDTH)],
                 out_shape=[_sds((t, HG_WIDTH))] * 3 + [_sds((t, 128)), _sds((1, HG_HEAD)), _sds((1, ATT_WIDTH))])(
                     dmi, o_hg, proj, att, og, ag)


def _dproj(dhg, dg, dqs, dkvs, name, tr=256):
    t = dhg.shape[0]
    w3 = 3 * HG_WIDTH
    w4 = w3 + HG_WIDTH
    nbr = len(dqs)

    def body(*refs):
        h_ref, g_ref, q_refs, kv_refs, o_ref = refs[0], refs[1], refs[2:2 + nbr], refs[2 + nbr:2 + 2 * nbr], refs[-1]
        o_ref[:, :w3] = h_ref[...]
        o_ref[:, w3:w4] = g_ref[...].astype(BF16)
        o_ref[:, w4:w4 + ATT_WIDTH] = sum(r[...].astype(F32) for r in q_refs).astype(BF16)
        o_ref[:, w4 + ATT_WIDTH:] = sum(r[...].astype(F32) for r in kv_refs).astype(BF16)

    return _call(body, name=name, grid=(t // tr,),
                 in_specs=[_rows(tr, w3), _rows(tr, HG_WIDTH)] + [_rows(tr, ATT_WIDTH)] * nbr
                 + [_rows(tr, 2 * ATT_WIDTH)] * nbr,
                 out_specs=_rows(tr, IN_WIDTH), out_shape=_sds((t, IN_WIDTH), BF16))(dhg, dg, *dqs, *dkvs)


def _chunk_tri(upper):
    row = lax.broadcasted_iota(jnp.int32, (HG_GROUP, HG_CHUNK, HG_CHUNK), 1)
    col = lax.broadcasted_iota(jnp.int32, (HG_GROUP, HG_CHUNK, HG_CHUNK), 2)
    return (row <= col if upper else row >= col).astype(BF16)


def _chunk_cumsum(x, tri):
    x3 = x.reshape(HG_GROUP, HG_CHUNK, x.shape[1])
    dims = (((2,), (1,)), ((0,), (0,)))
    out = None
    for _ in range(3):
        part = x3.astype(BF16)
        x3 = x3 - part.astype(F32)
        term = lax.dot_general(tri, part, dims, preferred_element_type=F32)
        out = term if out is None else out + term
    return out.reshape(x.shape)


def _hg_gates(f_raw, q_raw, lb, tri):
    sg = _sigmoid(f_raw)
    f = lb + (1.0 - lb) * sg
    k = 1.0 - f
    b = _chunk_cumsum(jnp.log(f), tri)
    sq = _sigmoid(q_raw)
    return sg, f, k, b, sq


def _hg_masks(rows):
    row = lax.broadcasted_iota(jnp.int32, (rows, rows), 0)
    col = lax.broadcasted_iota(jnp.int32, (rows, rows), 1)
    same = (row // HG_CHUNK) == (col // HG_CHUNK)
    return jnp.logical_and(row >= col, same), jnp.logical_and(row <= col, same)


def _per_chunk(rows_of):
    return jnp.concatenate([jnp.broadcast_to(r, (HG_CHUNK, r.shape[1])) for r in rows_of], axis=0)


def _hgrn_fwd(proj, lb_logits, name):
    t = proj.shape[0]
    nc = t // HG_CHUNK
    nh = HG_WIDTH // HG_HEAD
    rows = HG_GROUP * HG_CHUNK

    def body(q_ref, f_ref, i_ref, lg_ref, o_ref, st_ref, s_scr):
        @pl.when(pl.program_id(0) == 0)
        def _():
            s_scr[...] = jnp.zeros_like(s_scr)

        lg = lg_ref[...]
        lb_all = _sigmoid(lg[0:1] - lg[1:2])
        causal, _ = _hg_masks(rows)
        tri = _chunk_tri(False)
        for h in range(nh):
            sl = slice(h * HG_HEAD, (h + 1) * HG_HEAD)
            q_raw = q_ref[:, sl].astype(F32)
            _, _, k, b, sq = _hg_gates(f_ref[:, sl].astype(F32), q_raw, lb_all[:, sl], tri)
            v = i_ref[:, sl].astype(BF16)
            gls = [b[(g + 1) * HG_CHUNK - 1:(g + 1) * HG_CHUNK] for g in range(HG_GROUP)]
            qd = (q_raw * sq * jnp.exp(b)).astype(BF16)
            kd = (k * jnp.exp(-b)).astype(BF16)
            ke = (k * jnp.exp(_per_chunk(gls) - b)).astype(BF16)
            a = jnp.where(causal, _dot_nt(qd, kd), 0.0).astype(BF16)
            o_intra = _dot(a, v)
            st = s_scr[h]
            o_inter = []
            for g in range(HG_GROUP):
                rs = slice(g * HG_CHUNK, (g + 1) * HG_CHUNK)
                st_ref[g, sl, :] = st
                o_inter.append(_dot_nt(qd[rs], st.astype(BF16)))
                st = st * jnp.exp(gls[g]) + _dot_tn(v[rs], ke[rs])
            s_scr[h] = st
            o_ref[:, sl] = o_intra + jnp.concatenate(o_inter, axis=0)

    blk = lambda j: pl.BlockSpec((rows, HG_WIDTH), lambda c: (c, j))
    return _call(body, name=name, grid=(nc // HG_GROUP,),
                 in_specs=[blk(0), blk(1), blk(2), pl.BlockSpec((2, HG_WIDTH), lambda c: (0, 0))],
                 out_specs=[blk(0), pl.BlockSpec((HG_GROUP, HG_WIDTH, HG_HEAD), lambda c: (c, 0, 0))],
                 out_shape=[_sds((t, HG_WIDTH)), _sds((nc, HG_WIDTH, HG_HEAD))],
                 scratch_shapes=[pltpu.VMEM((nh, HG_HEAD, HG_HEAD), F32)])(proj, proj, proj, lb_logits)


def _hgrn_bwd(proj, lb_logits, states, do, name):
    t = proj.shape[0]
    ng = t // (HG_GROUP * HG_CHUNK)
    nh = HG_WIDTH // HG_HEAD
    rows = HG_GROUP * HG_CHUNK

    def body(q_ref, f_ref, i_ref, lg_ref, st_ref, do_ref, d_ref, dlb_ref, ds_scr):
        first = pl.program_id(0) == 0

        @pl.when(first)
        def _():
            ds_scr[...] = jnp.zeros_like(ds_scr)

        lg = lg_ref[...]
        lb_all = _sigmoid(lg[0:1] - lg[1:2])
        causal, _ = _hg_masks(rows)
        tri = _chunk_tri(False)
        tri_t = _chunk_tri(True)
        dlb = []
        for h in range(nh):
            sl = slice(h * HG_HEAD, (h + 1) * HG_HEAD)
            q_raw = q_ref[:, sl].astype(F32)
            lb = lb_all[:, sl]
            sg, f, k, b, sq = _hg_gates(f_ref[:, sl].astype(F32), q_raw, lb, tri)
            v = i_ref[:, sl].astype(BF16)
            gls = [b[(g + 1) * HG_CHUNK - 1:(g + 1) * HG_CHUNK] for g in range(HG_GROUP)]
            eb = jnp.exp(b)
            enb = jnp.exp(-b)
            egb = jnp.exp(_per_chunk(gls) - b)
            ke = k * egb
            qd_b, kd_b, ke_b = (q_raw * sq * eb).astype(BF16), (k * enb).astype(BF16), ke.astype(BF16)
            dov = do_ref[:, sl].astype(BF16)
            a = jnp.where(causal, _dot_nt(qd_b, kd_b), 0.0).astype(BF16)
            da = jnp.where(causal, _dot_nt(dov, v), 0.0).astype(BF16)
            dkd = _dot_tn(da, qd_b)
            dst = ds_scr[h]
            dqd_s, dv_s, dke_s, dgl_s = [None] * HG_GROUP, [None] * HG_GROUP, [None] * HG_GROUP, [None] * HG_GROUP
            for g in reversed(range(HG_GROUP)):
                rs = slice(g * HG_CHUNK, (g + 1) * HG_CHUNK)
                st = st_ref[g, sl, :]
                dst_b = dst.astype(BF16)
                egl = jnp.exp(gls[g])
                dqd_s[g] = _dot(dov[rs], st.astype(BF16))
                dv_s[g] = _dot_nt(ke_b[rs], dst_b)
                dke_s[g] = _dot(v[rs], dst_b)
                dgl_s[g] = jnp.sum(dst * st, axis=0, keepdims=True) * egl
                dst = _dot_tn(dov[rs], qd_b[rs]) + dst * egl
            ds_scr[h] = dst
            dqd = _dot(da, kd_b) + jnp.concatenate(dqd_s, axis=0)
            dv = _dot_tn(a, dov) + jnp.concatenate(dv_s, axis=0)
            dke = jnp.concatenate(dke_s, axis=0)
            t1 = dke * ke
            db = dqd * qd_b.astype(F32) - dkd * kd_b.astype(F32) - t1
            dgl = _per_chunk([dgl_s[g] + jnp.sum(t1[g * HG_CHUNK:(g + 1) * HG_CHUNK], axis=0, keepdims=True)
                              for g in range(HG_GROUP)])
            dlf = _chunk_cumsum(db, tri_t) + dgl
            df = dlf / f - (dkd * enb + dke * egb)
            d_ref[:, sl] = (dqd * eb * sq * (1.0 + q_raw * (1.0 - sq))).astype(BF16)
            d_ref[:, HG_WIDTH + h * HG_HEAD:HG_WIDTH + (h + 1) * HG_HEAD] = (
                df * (1.0 - lb) * sg * (1.0 - sg)).astype(BF16)
            d_ref[:, 2 * HG_WIDTH + h * HG_HEAD:2 * HG_WIDTH + (h + 1) * HG_HEAD] = dv.astype(BF16)
            dlb.append(jnp.sum(df * (1.0 - sg), axis=0, keepdims=True))
        _acc(dlb_ref, jnp.concatenate(dlb, axis=1), first)

    rev = lambda j: pl.BlockSpec((rows, HG_WIDTH), lambda c: (ng - 1 - c, j))
    return _call(body, name=name, grid=(ng,),
                 in_specs=[rev(0), rev(1), rev(2), pl.BlockSpec((2, HG_WIDTH), lambda c: (0, 0)),
                           pl.BlockSpec((HG_GROUP, HG_WIDTH, HG_HEAD), lambda c: (ng - 1 - c, 0, 0)), rev(0)],
                 out_specs=[pl.BlockSpec((rows, 3 * HG_WIDTH), lambda c: (ng - 1 - c, 0)), _vec(HG_WIDTH)],
                 out_shape=[_sds((t, 3 * HG_WIDTH), BF16), _sds((1, HG_WIDTH))],
                 scratch_shapes=[pltpu.VMEM((nh, HG_HEAD, HG_HEAD), F32)])(proj, proj, proj, lb_logits, states, do)


def _to_sub(a, dil):
    t, w = a.shape
    return a if dil == 1 else a.reshape(t // dil, dil, w).transpose(1, 0, 2).reshape(t, w)


def _from_sub(a, dil):
    t, w = a.shape
    return a if dil == 1 else a.reshape(dil, t // dil, w).transpose(1, 0, 2).reshape(t, w)


def _att_mask(has_prev):
    qi = lax.broadcasted_iota(jnp.int32, (2 * ATT_BLOCK, 2 * ATT_BLOCK), 0) % ATT_BLOCK
    kj = lax.broadcasted_iota(jnp.int32, (2 * ATT_BLOCK, 2 * ATT_BLOCK), 1)
    prev = jnp.logical_and(jnp.logical_and(kj < ATT_BLOCK, kj >= qi), has_prev)
    cur = jnp.logical_and(kj >= ATT_BLOCK, kj - ATT_BLOCK <= qi)
    return jnp.logical_or(prev, cur), lax.broadcasted_iota(jnp.int32, (1, 128), 1)


def _attn_fwd(qkv, dil, name):
    t = qkv.shape[0]
    nb = t // ATT_BLOCK
    bps = nb // dil

    def body(q_ref, kc_ref, kp_ref, vc_ref, vp_ref, o_ref, l_ref):
        mask, lane = _att_mask((pl.program_id(0) % bps) != 0)
        lo = lane < 64
        nq = ATT_BLOCK
        lse_all = jnp.zeros((nq, 128), F32)
        for hp in range(ATT_HEADS // 2):
            sl = slice(hp * 128, (hp + 1) * 128)
            q2 = q_ref[:, sl]
            zero = jnp.zeros_like(q2)
            qs = jnp.concatenate([jnp.where(lo, q2, zero), jnp.where(lo, zero, q2)], axis=0)
            kk = jnp.concatenate([kp_ref[:, sl], kc_ref[:, sl]], axis=0)
            vv = jnp.concatenate([vp_ref[:, sl], vc_ref[:, sl]], axis=0)
            s = jnp.where(mask, _dot_nt(qs, kk) * 0.125, NEG)
            mx = jnp.max(s, axis=-1, keepdims=True)
            p = jnp.exp(s - mx)
            l = jnp.sum(p, axis=-1, keepdims=True)
            o = _dot(p.astype(BF16), vv) * (1.0 / l)
            o_ref[:, sl] = jnp.where(lo, o[:nq], o[nq:])
            lse = mx + jnp.log(l)
            lse_all = jnp.where(lane == 2 * hp, lse[:nq], lse_all)
            lse_all = jnp.where(lane == 2 * hp + 1, lse[nq:], lse_all)
        l_ref[...] = lse_all

    col0 = qkv.shape[1] // ATT_WIDTH - 3
    blk = lambda j, back: pl.BlockSpec((ATT_BLOCK, ATT_WIDTH), lambda n: (jnp.maximum(n - back, 0), col0 + j))
    return _call(body, name=name, grid=(nb,),
                 in_specs=[blk(0, 0), blk(1, 0), blk(1, 1), blk(2, 0), blk(2, 1)],
                 out_specs=[pl.BlockSpec((ATT_BLOCK, ATT_WIDTH), lambda n: (n, 0)),
                            pl.BlockSpec((ATT_BLOCK, 128), lambda n: (n, 0))],
                 out_shape=[_sds((t, ATT_WIDTH)), _sds((t, 128))])(qkv, qkv, qkv, qkv, qkv)


def _attn_combine(os_, ls_, name, tr=256):
    t = os_[0].shape[0]
    nbr = len(os_)

    def body(*refs):
        o_refs, l_refs, (a_ref, lt_ref) = refs[:nbr], refs[nbr:2 * nbr], refs[2 * nbr:]
        lane = lax.broadcasted_iota(jnp.int32, (1, 128), 1)
        ls = [r[...] for r in l_refs]
        mx = functools.reduce(jnp.maximum, ls)
        tot = mx + jnp.log(sum(jnp.exp(l - mx) for l in ls))
        lt_ref[...] = tot
        ws = [jnp.exp(l - tot) for l in ls]
        for hp in range(ATT_HEADS // 2):
            sl = slice(hp * 128, (hp + 1) * 128)
            acc = jnp.zeros((tr, 128), F32)
            for w, o_ref in zip(ws, o_refs):
                wf = jnp.where(lane < 64, w[:, 2 * hp:2 * hp + 1], w[:, 2 * hp + 1:2 * hp + 2])
                acc = acc + wf * o_ref[:, sl]
            a_ref[:, sl] = acc

    return _call(body, name=name, grid=(t // tr,),
                 in_specs=[_rows(tr, ATT_WIDTH)] * nbr + [_rows(tr, 128)] * nbr,
                 out_specs=[_rows(tr, ATT_WIDTH), _rows(tr, 128)],
                 out_shape=[_sds((t, ATT_WIDTH)), _sds((t, 128))])(*os_, *ls_)


def _attn_bwd(qkv, dout, lse, dd, dil, name):
    t = qkv.shape[0]
    nb = t // ATT_BLOCK
    bps = nb // dil

    w = ATT_WIDTH
    nq = ATT_BLOCK

    def body(q_ref, kc_ref, kp_ref, vc_ref, vp_ref, do_ref, l_ref, d_ref, dq_ref, dkv_ref, carry):
        n = pl.program_id(0)

        @pl.when(n == 0)
        def _():
            carry[...] = jnp.zeros_like(carry)

        @pl.when(n < nb)
        def _():
            mask, lane = _att_mask((n % bps) != 0)
            lo = lane < 64
            for hp in range(ATT_HEADS // 2):
                sl = slice(hp * 128, (hp + 1) * 128)
                sv = slice(w + hp * 128, w + (hp + 1) * 128)
                q2, do2 = q_ref[:, sl], do_ref[:, sl]
                zero = jnp.zeros_like(q2)
                qs = jnp.concatenate([jnp.where(lo, q2, zero), jnp.where(lo, zero, q2)], axis=0)
                dos = jnp.concatenate([jnp.where(lo, do2, zero), jnp.where(lo, zero, do2)], axis=0)
                kk = jnp.concatenate([kp_ref[:, sl], kc_ref[:, sl]], axis=0)
                vv = jnp.concatenate([vp_ref[:, sl], vc_ref[:, sl]], axis=0)
                ls = jnp.concatenate([l_ref[:, 2 * hp:2 * hp + 1], l_ref[:, 2 * hp + 1:2 * hp + 2]], axis=0)
                dh = jnp.concatenate([d_ref[:, 2 * hp:2 * hp + 1], d_ref[:, 2 * hp + 1:2 * hp + 2]], axis=0)
                p = jnp.exp(jnp.where(mask, _dot_nt(qs, kk) * 0.125 - ls, NEG))
                ds = (p * (_dot_nt(dos, vv) - dh)).astype(BF16)
                dq = _dot(ds, kk) * 0.125
                dq_ref[:, sl] = jnp.where(lo, dq[:nq], dq[nq:]).astype(BF16)
                dk = _dot_tn(ds, qs) * 0.125
                dv = _dot_tn(p.astype(BF16), dos)
                dkv_ref[:, sl] = (carry[:, sl] + dk[:nq]).astype(BF16)
                dkv_ref[:, sv] = (carry[:, sv] + dv[:nq]).astype(BF16)
                carry[:, sl] = dk[nq:]
                carry[:, sv] = dv[nq:]

        @pl.when(n == nb)
        def _():
            dkv_ref[...] = carry[...].astype(BF16)

    def blk(width, j, back):
        return pl.BlockSpec((nq, width), lambda n: (jnp.clip(n - back, 0, nb - 1), j))

    c0 = qkv.shape[1] // w - 3
    return _call(body, name=name, grid=(nb + 1,),
                 in_specs=[blk(w, c0, 0), blk(w, c0 + 1, 0), blk(w, c0 + 1, 1), blk(w, c0 + 2, 0), blk(w, c0 + 2, 1),
                           blk(w, 0, 0), blk(128, 0, 0), blk(128, 0, 0)],
                 out_specs=[blk(w, 0, 0), blk(2 * w, 0, 1)],
                 out_shape=[_sds((t, w), BF16), _sds((t, 2 * w), BF16)],
                 scratch_shapes=[pltpu.VMEM((nq, 2 * w), F32)])(qkv, qkv, qkv, qkv, qkv, dout, lse, dd)


def _local_step(x, tgt, mod, norm1_g, lb_logits, og, ag, norm2_g, fg, get_w, put_g, late=lambda a: a):
    shift1, scale1, gate1, shift2, scale2, gate2 = [mod[:, i * D_MODEL:(i + 1) * D_MODEL] for i in range(6)]
    fg = fg.reshape(1, D_MODEL)

    h1 = _norm_mod(x, norm1_g, scale1, shift1, "norm_mod1")
    w_in = get_w("w_in", h1)
    proj = _mm_nn(h1, w_in, "mm_in", out_dtype=BF16)
    o_hg, states = _hgrn_fwd(proj, lb_logits, "hgrn_fwd")
    qkvs = [proj if d == 1 else _to_sub(proj[:, 4 * HG_WIDTH:], d) for d in DILATIONS]
    outs = [_attn_fwd(q, d, f"attn_fwd{d}") for q, d in zip(qkvs, DILATIONS)]
    att, lse = _attn_combine([_from_sub(o, d) for (o, _), d in zip(outs, DILATIONS)],
                             [_from_sub(l, d) for (_, l), d in zip(outs, DILATIONS)], "attn_combine")
    mixin = _mix_in(o_hg, proj, att, og, ag, "mix_in")
    w_out = get_w("w_out", mixin)
    mix = _mm_nn(mixin, w_out, "mm_out")
    x2, h2 = _resid_norm_mod(x, mix, gate1, norm2_g, scale2, shift2, "resid_norm_mod2")
    w_gu = get_w("w_gu", h2)
    a_ff, u_ff, act = _mm_gate_up(h2, w_gu, "mm_gu")
    w_down = get_w("w_down", act)
    ffn = _mm_nn(act, w_down, "mm_down")
    dx3, dffn, loss_v, dfg, dgate2 = _final_loss(x2, ffn, gate2, fg, tgt, "final_loss")

    dffn = put_g("w_down", *_mm_tn(act, dffn, 1, "mm_down_dw", tm=2048, tk=D_FF // 2), dffn)
    dau = _mm_down_dx(dffn, w_down, a_ff, u_ff, "mm_down_dx")
    dau = put_g("w_gu", *_mm_tn(h2, dau, N_SHARD, "mm_gu_dw", tm=x.shape[0], tk=512), dau)
    dh2 = _mm_nt(dau, w_gu, "mm_gu_dx")
    dx2, dshift2, dscale2, dg2, dgate1, dmix = _norm_mod_bwd(
        dh2, x2, norm2_g, scale2, dx3, "norm_mod2_bwd", gate=gate1, mix=mix)
    dmix = put_g("w_out", *_mm_tn(mixin, dmix, 1, "mm_out_dw", tm=x.shape[0], tk=512), dmix)
    dmixin = _mm_nt(dmix, w_out, "mm_out_dx", tm=1024)
    do_hg, dg_raw, datt, dd, dog, dag = _mix_in_bwd(dmixin, o_hg, proj, att, og, ag, "mix_in_bwd")
    datt_b = datt.astype(BF16)
    datts = [_attn_bwd(q, _to_sub(datt_b, d), _to_sub(lse, d), _to_sub(dd, d), d, f"attn_bwd{d}")
             for q, d in zip(qkvs, DILATIONS)]
    dhg, dlb = _hgrn_bwd(proj, lb_logits, states, do_hg, "hgrn_bwd")
    dhg = late(dhg)
    dproj = _dproj(dhg, dg_raw, [_from_sub(dq, d) for (dq, _), d in zip(datts, DILATIONS)],
                   [_from_sub(dkv, d) for (_, dkv), d in zip(datts, DILATIONS)], "dproj")
    dproj = put_g("w_in", *_mm_tn(h1, dproj, N_SHARD, "mm_in_dw", tm=x.shape[0], tk=512), dproj)
    dh1 = _mm_nt(dproj, w_in, "mm_in_dx", tm=1024)
    dx, dshift1, dscale1, dg1 = _norm_mod_bwd(dh1, x, norm1_g, scale1, dx2, "norm_mod1_bwd")

    stats = jnp.concatenate([loss_v, dfg, dg2, dg1, dlb, dag, dog,
                             dshift1, dscale1, dgate1, dshift2, dscale2, dgate2], axis=1)
    return dx, stats


def _place():
    x, y, c = lax.axis_index("x"), lax.axis_index("y"), lax.axis_index("c")
    return x, y, c


def _chip_peers(x, y, c):
    return [(1 - x, y, c), (x, 1 - y, c), (1 - x, 1 - y, c)]


def _comm_call(body, name, n_in, out_shape, scratch_shapes):
    hbm = pl.BlockSpec(memory_space=pl.ANY)
    return pl.pallas_call(body, name=name, in_specs=[hbm] * n_in, out_specs=[hbm] * len(out_shape),
                          out_shape=out_shape, scratch_shapes=scratch_shapes)


_HBM = pl.BlockSpec(memory_space=pltpu.HBM)
_SEM = pl.BlockSpec(memory_space=pltpu.SEMAPHORE)
_EFFECT = pltpu.SideEffectType.DATAFLOW_SIDE_EFFECTING


def _exchange_copy(bufs, send, recv, j, peer, place, gather):
    if gather:
        src = dst = bufs[0].at[2 * place[0] + place[1]]
    else:
        src, dst = bufs[0].at[2 * peer[0] + peer[1]], bufs[1].at[j]
    return pltpu.make_async_remote_copy(src_ref=src, dst_ref=dst, send_sem=send.at[j], recv_sem=recv.at[j],
                                        device_id=peer, device_id_type=MESH)


def _exchange_start(groups, after, gather, name):
    sizes = [len(g) for g in groups]
    flat = [b for g in groups for b in g]
    ng, nb = len(groups), len(flat)

    def body(*refs):
        bufs, sems = refs[:nb], refs[nb + 1:nb + 1 + 2 * ng]
        x, y, c = _place()
        for j, peer in enumerate(_chip_peers(x, y, c)):
            at = 0
            for i, size in enumerate(sizes):
                _exchange_copy(bufs[at:at + size], sems[2 * i], sems[2 * i + 1], j, peer, (x, y), gather).start()
                at += size

    any_space = pl.BlockSpec(memory_space=pl.ANY)
    out = pl.pallas_call(
        body, name=name, in_specs=[_HBM] * nb + [any_space],
        out_specs=[_SEM] * (2 * ng) + [_HBM] * nb + [any_space],
        out_shape=[pltpu.SemaphoreType.DMA((3,))] * (2 * ng) + [pltpu.HBM(b.shape, b.dtype) for b in flat]
        + [_sds(after.shape, after.dtype)],
        input_output_aliases={i: 2 * ng + i for i in range(nb + 1)},
        compiler_params=pltpu.CompilerParams(has_side_effects=_EFFECT),
    )(*[pltpu.with_memory_space_constraint(b, pltpu.HBM) for b in flat], after)
    started, at = [], 2 * ng
    for i, size in enumerate(sizes):
        started.append((out[2 * i], out[2 * i + 1], tuple(out[at:at + size])))
        at += size
    return started, out[-1]


def _exchange_wait(started, after, gather, name):
    send, recv, bufs = started
    nb = len(bufs)

    def body(*refs):
        x, y, c = _place()
        for j, peer in enumerate(_chip_peers(x, y, c)):
            cp = _exchange_copy(refs[:nb], refs[nb], refs[nb + 1], j, peer, (x, y), gather)
            cp.wait_send()
            cp.wait_recv()

    return pl.pallas_call(
        body, name=name, in_specs=[_HBM] * nb + [_SEM, _SEM, pl.BlockSpec(memory_space=pl.ANY)],
        out_specs=[_HBM] * nb, out_shape=[pltpu.HBM(b.shape, b.dtype) for b in bufs],
        input_output_aliases={i: i for i in range(nb)},
        compiler_params=pltpu.CompilerParams(has_side_effects=_EFFECT),
    )(*bufs, send, recv, after)


def _sibling_copies(v_refs, l_refs, send, recv):
    x, y, c = _place()
    return [pltpu.make_async_remote_copy(src_ref=v, dst_ref=l, send_sem=send.at[a], recv_sem=recv.at[a],
                                         device_id=(x, y, 1 - c), device_id_type=MESH)
            for a, (v, l) in enumerate(zip(v_refs, l_refs))]


def _sibling_start(vs, after, name):
    n = len(vs)
    lands = [lax.empty(v.shape, v.dtype) for v in vs]

    def body(*refs):
        for cp in _sibling_copies(refs[:n], refs[n:2 * n], refs[2 * n + 1], refs[2 * n + 2]):
            cp.start()

    any_space = pl.BlockSpec(memory_space=pl.ANY)
    out = pl.pallas_call(
        body, name=name, in_specs=[_HBM] * (2 * n) + [any_space],
        out_specs=[_SEM, _SEM] + [_HBM] * (2 * n) + [any_space],
        out_shape=[pltpu.SemaphoreType.DMA((n,))] * 2 + [pltpu.HBM(b.shape, b.dtype) for b in vs + lands]
        + [_sds(after.shape, after.dtype)],
        input_output_aliases={i: 2 + i for i in range(2 * n + 1)},
        compiler_params=pltpu.CompilerParams(has_side_effects=_EFFECT),
    )(*[pltpu.with_memory_space_constraint(b, pltpu.HBM) for b in vs + lands], after)
    return (out[0], out[1], tuple(out[2:2 + n]), tuple(out[2 + n:2 + 2 * n])), out[-1]


def _sibling_wait(started, after, name):
    send, recv, vs, lands = started
    n = len(vs)

    def body(*refs):
        for cp in _sibling_copies(refs[:n], refs[n:2 * n], refs[2 * n], refs[2 * n + 1]):
            cp.wait_send()
            cp.wait_recv()

    out = pl.pallas_call(
        body, name=name, in_specs=[_HBM] * (2 * n) + [_SEM, _SEM, pl.BlockSpec(memory_space=pl.ANY)],
        out_specs=[_HBM] * (2 * n), out_shape=[pltpu.HBM(b.shape, b.dtype) for b in vs + lands],
        input_output_aliases={i: i for i in range(2 * n)},
        compiler_params=pltpu.CompilerParams(has_side_effects=_EFFECT),
    )(*vs, *lands, send, recv, after)
    return out[:n], out[n:]


def _swap_sibling(vs, name):
    n = len(vs)

    def body(*refs):
        v_refs, o_refs, (send, recv) = refs[:n], refs[n:2 * n], refs[2 * n:]
        x, y, c = _place()
        cps = [pltpu.make_async_remote_copy(
            src_ref=v_refs[a], dst_ref=o_refs[a], send_sem=send.at[a], recv_sem=recv.at[a],
            device_id=(x, y, 1 - c), device_id_type=MESH) for a in range(n)]
        for cp in cps:
            cp.start()
        for cp in cps:
            cp.wait()

    return _comm_call(body, name, n, [_sds(v.shape, v.dtype) for v in vs],
                      [pltpu.SemaphoreType.DMA((n,)), pltpu.SemaphoreType.DMA((n,))])(*vs)


def _gather_rows(v, name):
    r, n = v.shape

    def body(v_ref, o_ref, send, recv, loc):
        x, y, c = _place()
        me = 4 * x + 2 * y + c
        own = pltpu.make_async_copy(v_ref, o_ref.at[me], loc)
        own.start()
        peers = []
        for k in range(1, 8):
            px = 1 - x if k & 4 else x
            py = 1 - y if k & 2 else y
            pc = 1 - c if k & 1 else c
            peers.append((px, py, pc))
        sends = []
        for k, peer in enumerate(peers):
            cp = pltpu.make_async_remote_copy(src_ref=v_ref, dst_ref=o_ref.at[me], send_sem=send.at[k],
                                              recv_sem=recv.at[k], device_id=peer, device_id_type=MESH)
            cp.start()
            sends.append(cp)
        for k, peer in enumerate(peers):
            pltpu.make_async_remote_copy(src_ref=v_ref, dst_ref=o_ref.at[4 * peer[0] + 2 * peer[1] + peer[2]],
                                         send_sem=send.at[k], recv_sem=recv.at[k], device_id=peer,
                                         device_id_type=MESH).wait_recv()
        for cp in sends:
            cp.wait_send()
        own.wait()

    vmem = pl.BlockSpec(memory_space=pltpu.VMEM)
    return pl.pallas_call(body, name=name, in_specs=[vmem], out_specs=vmem, out_shape=_sds((8, r, n), v.dtype),
                          scratch_shapes=[pltpu.SemaphoreType.DMA((7,)), pltpu.SemaphoreType.DMA((7,)),
                                          pltpu.SemaphoreType.DMA])(v)


def _cast_place(w, shard, name):
    r, c = w.shape
    tr = r // 4

    def body(s_ref, w_ref, o_ref):
        o_ref[0] = w_ref[...].astype(BF16)

    return pl.pallas_call(
        body, name=name, out_shape=_sds((N_SHARD, r, c), BF16),
        grid_spec=pltpu.PrefetchScalarGridSpec(
            num_scalar_prefetch=1, grid=(4,), in_specs=[pl.BlockSpec((tr, c), lambda i, s: (i, 0))],
            out_specs=pl.BlockSpec((1, tr, c), lambda i, s: (s[0], i, 0))),
        compiler_params=pltpu.CompilerParams(dimension_semantics=("arbitrary",)),
    )(shard.reshape(1).astype(jnp.int32), w)


def _mod_part(c_all, w_ada, b_ada, name):
    n = w_ada.shape[1]

    def body(c_ref, w_ref, b_ref, a_ref, p_ref):
        cv = c_ref[...]
        ca = cv * _sigmoid(cv)
        a_ref[...] = ca
        p_ref[...] = jnp.dot(ca, w_ref[...], precision=lax.Precision.HIGHEST, preferred_element_type=F32) + b_ref[...]

    full = lambda a: pl.BlockSpec(a.shape, lambda i: (0, 0))
    return _call(body, name=name, grid=(1,), in_specs=[full(c_all), full(w_ada), full(b_ada)],
                 out_specs=[pl.BlockSpec((8, D_MODEL), lambda i: (0, 0)), pl.BlockSpec((8, n), lambda i: (0, 0))],
                 out_shape=[_sds((8, D_MODEL)), _sds((8, n))])(c_all, w_ada, b_ada)


def _sum_received(g, shard, land, name):
    _, r, c = g.shape
    tr = r // 4

    def body(s_ref, g_ref, l_ref, o_ref):
        o_ref[...] = ((g_ref[0] + l_ref[0].astype(F32)) + l_ref[1].astype(F32)) + l_ref[2].astype(F32)

    return pl.pallas_call(
        body, name=name, out_shape=_sds((r, c)),
        grid_spec=pltpu.PrefetchScalarGridSpec(
            num_scalar_prefetch=1, grid=(4,),
            in_specs=[pl.BlockSpec((1, tr, c), lambda i, s: (s[0], i, 0)),
                      pl.BlockSpec((3, tr, c), lambda i, s: (0, i, 0))],
            out_specs=pl.BlockSpec((tr, c), lambda i, s: (i, 0))),
        compiler_params=pltpu.CompilerParams(dimension_semantics=("arbitrary",), vmem_limit_bytes=VMEM_LIMIT),
    )(shard.reshape(1).astype(jnp.int32), g, land)


def _outer_sum(ct, dm, name):
    k, n = ct.shape[0], dm.shape[1]
    tr = k // 4

    def body(c_ref, d_ref, o_ref):
        cv = c_ref[...]
        dv = d_ref[...]
        acc = cv[:, 0:1] * dv[0:1, :]
        for i in range(1, 8):
            acc = acc + cv[:, i:i + 1] * dv[i:i + 1, :]
        o_ref[...] = acc

    return _call(body, name=name, grid=(4,), in_specs=[_rows(tr, 8), pl.BlockSpec((8, n), lambda i: (0, 0))],
                 out_specs=_rows(tr, n), out_shape=_sds((k, n)))(ct, dm)


def _adamw_math(w, g, m, v):
    m_new = ADAM_B1 * m + (1.0 - ADAM_B1) * g
    v_new = ADAM_B2 * v + (1.0 - ADAM_B2) * (g * g)
    m_hat = m_new / (1.0 - ADAM_B1 ** ADAM_STEP)
    v_hat = v_new / (1.0 - ADAM_B2 ** ADAM_STEP)
    return -ADAM_LR * (m_hat / (jnp.sqrt(v_hat) + ADAM_EPS) + ADAM_WD * w), m_new, v_new


def _small_update(stats, smalls, name):
    offsets = [ST_DMOD, ST_DG1, ST_DLB, ST_DOG, ST_DAG, ST_DG2, ST_DFG]
    lb_index = 2

    def body(*refs):
        s_ref, ins, l_ref, outs = refs[0], refs[1:22], refs[22], refs[23:]
        tot = s_ref[0:1, :]
        for i in range(1, 8):
            tot = tot + s_ref[i:i + 1, :]
        l_ref[...] = jnp.zeros((1, 128), F32) + (0.5 / D_MODEL) * jnp.sum(tot[:, ST_LOSS:ST_LOSS + D_MODEL])
        for p, off in enumerate(offsets):
            w_ref, m_ref, v_ref = ins[3 * p:3 * p + 3]
            g_out, d_out, m_out, v_out = outs[4 * p:4 * p + 4]
            g = tot[:, off:off + w_ref.shape[1]]
            if p == lb_index:
                lg = w_ref[...]
                lb = _sigmoid(lg[0:1] - lg[1:2])
                g = g * lb * (1.0 - lb)
            for r in range(w_ref.shape[0]):
                rows = slice(r, r + 1)
                gr = g if r == 0 else -g
                delta, m_new, v_new = _adamw_math(w_ref[rows, :], gr, m_ref[rows, :], v_ref[rows, :])
                g_out[rows, :] = gr
                d_out[rows, :] = delta
                m_out[rows, :] = m_new
                v_out[rows, :] = v_new

    full = lambda a: pl.BlockSpec(a.shape, lambda i: (0, 0))
    flat = [a for t in smalls for a in t]
    return _call(body, name=name, grid=(1,),
                 in_specs=[full(stats)] + [full(a) for a in flat],
                 out_specs=[pl.BlockSpec((1, 128), lambda i: (0, 0))] + [full(t[0]) for t in smalls for _ in range(4)],
                 out_shape=[_sds((1, 128))] + [_sds(t[0].shape) for t in smalls for _ in range(4)])(stats, *flat)


def _adamw(w, gs, m, v, name, steps=4):
    r, c = w.shape
    tr = r // steps
    ng = len(gs)

    def body(*refs):
        w_ref, g_refs, (m_ref, v_ref, g_out, d_out, m_out, v_out) = refs[0], refs[1:1 + ng], refs[1 + ng:]
        g = g_refs[0][...]
        for g_ref in g_refs[1:]:
            g = g + g_ref[...]
        g_out[...] = g
        d_out[...], m_out[...], v_out[...] = _adamw_math(w_ref[...], g, m_ref[...], v_ref[...])

    row = _rows(tr, c)
    return _call(body, name=name, grid=(steps,), in_specs=[row] * (3 + ng), out_specs=[row] * 4,
                 out_shape=[_sds((r, c))] * 4)(w, *gs, m, v)


def kernel(x, c, w_ada, b_ada, norm1_g, w_in, hg_lb_logits, hg_onorm_g, att_onorm_g, w_out, norm2_g, w_gate_up, w_down, final_g, loss_target, m_w_ada, m_b_ada, m_norm1_g, m_w_in, m_hg_lb_logits, m_hg_onorm_g, m_att_onorm_g, m_w_out, m_norm2_g, m_w_gate_up, m_w_down, m_final_g, v_w_ada, v_b_ada, v_norm1_g, v_w_in, v_hg_lb_logits, v_hg_onorm_g, v_att_onorm_g, v_w_out, v_norm2_g, v_w_gate_up, v_w_down, v_final_g):
    ix, iy, ic = _place()
    shard = 2 * ix + iy
    sample = 4 * ix + 2 * iy + ic
    n_ada = w_ada.shape[2]

    shards = [w_in[0], w_out[0], w_gate_up[0], w_down[0]]
    names = ["w_in", "w_out", "w_gu", "w_down"]
    shapes = [(N_SHARD,) + w.shape for w in shards]
    placed = [(_cast_place(w, shard, "place_" + nm),) for w, nm in zip(shards, names)]

    c_all = _gather_rows(c, "gather_c").reshape(8, D_MODEL)
    b_part = lax.dynamic_slice(b_ada, (0, shard * n_ada), (1, n_ada))
    c_act, part = _mod_part(c_all, w_ada[0], b_part, "mod_part")
    parts = _gather_rows(part, "gather_mod")[::2]
    mod = lax.dynamic_index_in_dim(parts, sample, axis=1, keepdims=False).reshape(1, 6 * D_MODEL)
    (first,), mod = _exchange_start(placed[:1], mod, True, "gather_start_w_in")
    gathering = {"w_in": first}

    def get_w(name, after):
        (full,) = _exchange_wait(gathering[name], after, True, "gather_wait_" + name)
        if name == "w_in":
            rest, full = _exchange_start(placed[1:], full, True, "gather_start_rest")
            gathering.update(zip(names[1:], rest))
        return full if name in ("w_in", "w_gu") else full.reshape(1, -1, D_MODEL)

    scattering = {}

    def put_g(name, g, g_bf16, then):
        shape = shapes[names.index(name)]
        land = lax.empty((3,) + shape[1:], BF16)
        (started,), then = _exchange_start([(g_bf16.reshape(shape), land)], then, False, "scatter_start_" + name)
        scattering[name] = (g.reshape(shape), started)
        return then

    def summed(name, after):
        g, started = scattering[name]
        _, land = _exchange_wait(started, after, False, "scatter_wait_" + name)
        return _sum_received(g, shard, land, "sum_" + name)

    early = ["w_down", "w_gu", "w_out"]
    swapping = []

    def late(a):
        started, a = _sibling_start([summed(nm, a) for nm in early], a, "swap_start")
        swapping.append(started)
        return a

    dx, stats = _local_step(x[0], loss_target[0], mod, norm1_g, hg_lb_logits, hg_onorm_g, att_onorm_g,
                            norm2_g, final_g, get_w, put_g, late)

    stats_all = _gather_rows(stats, "gather_stats").reshape(8, ST_WIDTH)
    dmod = lax.dynamic_slice(stats_all, (0, ST_DMOD + shard * n_ada), (8, n_ada))
    g_ada = _outer_sum(c_act.T, dmod, "w_ada_grad")

    as_row = lambda a: a.reshape(1, -1) if a.ndim == 1 else a
    smalls = [tuple(as_row(a) for a in t) for t in [
        (b_ada, m_b_ada, v_b_ada), (norm1_g, m_norm1_g, v_norm1_g),
        (hg_lb_logits, m_hg_lb_logits, v_hg_lb_logits), (hg_onorm_g, m_hg_onorm_g, v_hg_onorm_g),
        (att_onorm_g, m_att_onorm_g, v_att_onorm_g), (norm2_g, m_norm2_g, v_norm2_g),
        (final_g, m_final_g, v_final_g)]]
    loss, *small_out = _small_update(stats_all, smalls, "small_update")
    shapes_out = [b_ada.shape, norm1_g.shape, hg_lb_logits.shape, hg_onorm_g.shape, att_onorm_g.shape,
                  norm2_g.shape, final_g.shape]
    sg, sd, sm, sv = [[small_out[4 * p + i].reshape(shapes_out[p]) for p in range(7)] for i in range(4)]

    ada = _adamw(w_ada[0], [g_ada], m_w_ada[0], v_w_ada[0], "adamw_w_ada")
    moments = [(m_w_in, v_w_in), (m_w_out, v_w_out), (m_w_gate_up, v_w_gate_up), (m_w_down, v_w_down)]

    def update(group, sums, other):
        return {nm: _adamw(shards[names.index(nm)], [s, o], moments[names.index(nm)][0][0],
                           moments[names.index(nm)][1][0], "adamw_" + nm) for nm, s, o in zip(group, sums, other)}

    sums, other = _sibling_wait(swapping[0], ada[1], "swap_wait")
    done = update(early, sums, other)
    sum_in = summed("w_in", done["w_out"][1])
    done.update(update(["w_in"], [sum_in], _swap_sibling([sum_in], "swap_sum_in")))
    big = [ada] + [done[nm] for nm in names]
    bg, bd, bm, bv = [[t[i][None] for t in big] for i in range(4)]

    def order(b, s):
        return [b[0], s[0], s[1], b[1], s[2], s[3], s[4], b[2], s[5], b[3], b[4], s[6]]

    return (loss[0, 0], dx[None], *order(bg, sg), *order(bd, sd), *order(bm, sm), *order(bv, sv))
```

```python
import functools

import jax
import jax.numpy as jnp
from jax import lax
from jax.experimental import pallas as pl
from jax.experimental.pallas import tpu as pltpu

F32 = jnp.float32
BF16 = jnp.bfloat16
MESH = pl.DeviceIdType.MESH

D_MODEL = 1024
HG_WIDTH = 512
HG_HEAD = 128
HG_CHUNK = 64
HG_GROUP = 4
ATT_WIDTH = 512
ATT_HEADS = 8
ATT_BLOCK = 128
DILATIONS = (1, 4, 16)
D_FF = 2816
IN_WIDTH = 3584
N_SHARD = 4
RMS_EPS = 1e-6
NEG = -1e30

ADAM_LR = 0.001
ADAM_B1 = 0.9
ADAM_B2 = 0.999
ADAM_EPS = 1e-08
ADAM_WD = 0.01
ADAM_STEP = 10

VMEM_LIMIT = 56 * 2**20

ST_LOSS, ST_DFG, ST_DG2, ST_DG1 = 0, 1024, 2048, 3072
ST_DLB, ST_DAG, ST_DOG, ST_DMOD = 4096, 4608, 5120, 5248
ST_WIDTH = 5248 + 6144
SP_BADA, SP_N1, SP_LB, SP_OG, SP_AG, SP_N2, SP_FG = 0, 6144, 7168, 8192, 8320, 8832, 9856
SP_WIDTH = 10880


def _call(body, *, name, grid, in_specs, out_specs, out_shape, scratch_shapes=()):
    return pl.pallas_call(
        body, name=name, grid=grid, in_specs=in_specs, out_specs=out_specs, out_shape=out_shape,
        scratch_shapes=list(scratch_shapes),
        compiler_params=pltpu.CompilerParams(
            dimension_semantics=("arbitrary",) * len(grid), vmem_limit_bytes=VMEM_LIMIT))


def _sds(shape, dtype=F32):
    return jax.ShapeDtypeStruct(shape, dtype)


def _dot(a, b):
    return jnp.dot(a, b, preferred_element_type=F32)


def _dot_nt(a, b):
    return lax.dot_general(a, b, (((1,), (1,)), ((), ())), preferred_element_type=F32)


def _dot_tn(a, b):
    return lax.dot_general(a, b, (((0,), (0,)), ((), ())), preferred_element_type=F32)


def _sigmoid(x):
    return 1.0 / (1.0 + jnp.exp(-x))


def _rows(tr, width):
    return pl.BlockSpec((tr, width), lambda i: (i, 0))


def _vec(width):
    return pl.BlockSpec((1, width), lambda i: (0, 0))


def _acc(ref, val, first):
    @pl.when(first)
    def _():
        ref[...] = val

    @pl.when(jnp.logical_not(first))
    def _():
        ref[...] += val


def _mm_nn(a, b3, name, tm=1024, out_dtype=F32):
    m, k = a.shape
    s, _, n = b3.shape

    def body(a_ref, b_ref, o_ref):
        o_ref[...] = _dot(a_ref[...], b_ref[0]).astype(out_dtype)

    return _call(
        body, name=name, grid=(s, m // tm),
        in_specs=[pl.BlockSpec((tm, k), lambda j, i: (i, 0)), pl.BlockSpec((1, k, n), lambda j, i: (j, 0, 0))],
        out_specs=pl.BlockSpec((tm, n), lambda j, i: (i, j)), out_shape=_sds((m, s * n), out_dtype))(a, b3)


def _mm_nt(dy, b3, name, tm=512):
    m = dy.shape[0]
    s, k, n = b3.shape

    def body(dy_ref, b_ref, o_ref):
        acc = _dot_nt(dy_ref[:, 0:n], b_ref[0])
        for j in range(1, s):
            acc = acc + _dot_nt(dy_ref[:, j * n:(j + 1) * n], b_ref[j])
        o_ref[...] = acc

    return _call(
        body, name=name, grid=(m // tm,),
        in_specs=[_rows(tm, s * n), pl.BlockSpec((s, k, n), lambda i: (0, 0, 0))],
        out_specs=_rows(tm, k), out_shape=_sds((m, k)))(dy, b3)


def _mm_tn(a, dy, s, name, tm, tk):
    m, k = a.shape
    n = dy.shape[1] // s
    steps = m // tm

    def body(a_ref, dy_ref, o_ref, ob_ref):
        p = _dot_tn(a_ref[...], dy_ref[...])[None]
        if steps == 1:
            o_ref[...] = p
            ob_ref[...] = p.astype(BF16)
        else:
            i = pl.program_id(2)
            _acc(o_ref, p, i == 0)

            @pl.when(i == steps - 1)
            def _():
                ob_ref[...] = o_ref[...].astype(BF16)

    out = pl.BlockSpec((1, tk, n), lambda kk, j, i: (j, kk, 0))
    return _call(
        body, name=name, grid=(k // tk, s, steps),
        in_specs=[pl.BlockSpec((tm, tk), lambda kk, j, i: (i, kk)), pl.BlockSpec((tm, n), lambda kk, j, i: (i, j))],
        out_specs=[out, out], out_shape=[_sds((s, k, n)), _sds((s, k, n), BF16)])(a, dy)


def _rms(x):
    return lax.rsqrt(jnp.mean(x * x, axis=-1, keepdims=True) + RMS_EPS)


def _rms_bwd(dxh, xh, r):
    return r * (dxh - xh * jnp.mean(dxh * xh, axis=-1, keepdims=True))


def _norm_mod(x, g, scale, shift, name, tr=512):
    t = x.shape[0]

    def body(x_ref, g_ref, sc_ref, sh_ref, h_ref):
        xv = x_ref[...]
        n = xv * _rms(xv) * g_ref[...]
        h_ref[...] = (n * (1.0 + sc_ref[...]) + sh_ref[...]).astype(BF16)

    return _call(body, name=name, grid=(t // tr,),
                 in_specs=[_rows(tr, D_MODEL), _vec(D_MODEL), _vec(D_MODEL), _vec(D_MODEL)],
                 out_specs=_rows(tr, D_MODEL), out_shape=_sds((t, D_MODEL), BF16))(x, g, scale, shift)


def _mix_in(o_hg, proj, att, og, ag, name, tr=256):
    t = o_hg.shape[0]

    def body(o_ref, g_ref, a_ref, og_ref, ag_ref, m_ref):
        for h in range(HG_WIDTH // HG_HEAD):
            sl = slice(h * HG_HEAD, (h + 1) * HG_HEAD)
            oh = o_ref[:, sl]
            gv = g_ref[:, sl].astype(F32)
            m_ref[:, sl] = (oh * _rms(oh) * og_ref[...] * (gv * _sigmoid(gv))).astype(BF16)
        av = a_ref[...]
        m_ref[:, HG_WIDTH:] = (av * _rms(av) * ag_ref[...]).astype(BF16)

    return _call(body, name=name, grid=(t // tr,),
                 in_specs=[_rows(tr, HG_WIDTH), pl.BlockSpec((tr, HG_WIDTH), lambda i: (i, 3)), _rows(tr, ATT_WIDTH),
                           _vec(HG_HEAD), _vec(ATT_WIDTH)],
                 out_specs=_rows(tr, D_MODEL), out_shape=_sds((t, D_MODEL), BF16))(o_hg, proj, att, og, ag)


def _resid_norm_mod(x, mix, gate, g, scale, shift, name, tr=256):
    t = x.shape[0]

    def body(x_ref, m_ref, gt_ref, g_ref, sc_ref, sh_ref, x2_ref, h_ref):
        x2 = x_ref[...] + gt_ref[...] * m_ref[...]
        x2_ref[...] = x2
        n = x2 * _rms(x2) * g_ref[...]
        h_ref[...] = (n * (1.0 + sc_ref[...]) + sh_ref[...]).astype(BF16)

    return _call(body, name=name, grid=(t // tr,),
                 in_specs=[_rows(tr, D_MODEL), _rows(tr, D_MODEL)] + [_vec(D_MODEL)] * 4,
                 out_specs=[_rows(tr, D_MODEL), _rows(tr, D_MODEL)],
                 out_shape=[_sds((t, D_MODEL)), _sds((t, D_MODEL), BF16)])(x, mix, gate, g, scale, shift)


def _mm_gate_up(h, w_gu, name, tm=1024):
    m, k = h.shape
    n = w_gu.shape[2]

    def body(h_ref, wa_ref, wu_ref, da_ref, du_ref, o_ref):
        hv = h_ref[...]
        a = _dot(hv, wa_ref[0])
        u = _dot(hv, wu_ref[0])
        sg = _sigmoid(a)
        silu = a * sg
        da_ref[...] = (u * sg * (1.0 + a * (1.0 - sg))).astype(BF16)
        du_ref[...] = silu.astype(BF16)
        o_ref[...] = (silu * u).astype(BF16)

    out = pl.BlockSpec((tm, n), lambda j, i: (i, j))
    return _call(body, name=name, grid=(2, m // tm),
                 in_specs=[pl.BlockSpec((tm, k), lambda j, i: (i, 0)), pl.BlockSpec((1, k, n), lambda j, i: (j, 0, 0)),
                           pl.BlockSpec((1, k, n), lambda j, i: (j + 2, 0, 0))],
                 out_specs=[out, out, out], out_shape=[_sds((m, 2 * n), BF16)] * 3)(h, w_gu, w_gu)


def _mm_down_dx(dffn, w_down, act_da, act_du, name, tm=512):
    m = dffn.shape[0]
    _, k, n = w_down.shape

    def body(d_ref, w_ref, da_ref, du_ref, o_ref):
        dact = _dot_nt(d_ref[...], w_ref[0])
        o_ref[:, :k] = (dact * da_ref[...].astype(F32)).astype(BF16)
        o_ref[:, k:] = (dact * du_ref[...].astype(F32)).astype(BF16)

    return _call(body, name=name, grid=(m // tm,),
                 in_specs=[_rows(tm, n), pl.BlockSpec((1, k, n), lambda i: (0, 0, 0)), _rows(tm, k), _rows(tm, k)],
                 out_specs=_rows(tm, 2 * k), out_shape=_sds((m, 2 * k), BF16))(dffn, w_down, act_da, act_du)


def _final_loss(x2, ffn, gate, fg, tgt, name, tr=256):
    t = x2.shape[0]

    def body(x_ref, f_ref, gt_ref, fg_ref, t_ref, dx_ref, df_ref, l_ref, dfg_ref, dgt_ref):
        first = pl.program_id(0) == 0
        ffn_v = f_ref[...]
        x3 = x_ref[...] + gt_ref[...] * ffn_v
        r = _rms(x3)
        xh = x3 * r
        err = xh * fg_ref[...] - t_ref[...]
        dy = err * (1.0 / D_MODEL)
        dx3 = _rms_bwd(dy * fg_ref[...], xh, r)
        dx_ref[...] = dx3
        df_ref[...] = (dx3 * gt_ref[...]).astype(BF16)
        _acc(l_ref, jnp.sum(err * err, axis=0, keepdims=True), first)
        _acc(dfg_ref, jnp.sum(dy * xh, axis=0, keepdims=True), first)
        _acc(dgt_ref, jnp.sum(dx3 * ffn_v, axis=0, keepdims=True), first)

    row, vec = _rows(tr, D_MODEL), _vec(D_MODEL)
    return _call(body, name=name, grid=(t // tr,), in_specs=[row, row, vec, vec, row],
                 out_specs=[row, row, vec, vec, vec],
                 out_shape=[_sds((t, D_MODEL)), _sds((t, D_MODEL), BF16)] + [_sds((1, D_MODEL))] * 3)(
                     x2, ffn, gate, fg, tgt)


def _norm_mod_bwd(dh, x, g, scale, dres, name, gate=None, mix=None, tr=256):
    t = x.shape[0]
    below = gate is not None

    def body(*refs):
        if below:
            dh_ref, x_ref, g_ref, sc_ref, dr_ref, gt_ref, m_ref, dx_ref, dsh_ref, dsc_ref, dg_ref, dgt_ref, dm_ref = refs
        else:
            dh_ref, x_ref, g_ref, sc_ref, dr_ref, dx_ref, dsh_ref, dsc_ref, dg_ref = refs
        first = pl.program_id(0) == 0
        xv = x_ref[...]
        dhv = dh_ref[...]
        r = _rms(xv)
        xh = xv * r
        dn = dhv * (1.0 + sc_ref[...])
        dx = dr_ref[...] + _rms_bwd(dn * g_ref[...], xh, r)
        dx_ref[...] = dx
        _acc(dsh_ref, jnp.sum(dhv, axis=0, keepdims=True), first)
        _acc(dsc_ref, jnp.sum(dhv * xh * g_ref[...], axis=0, keepdims=True), first)
        _acc(dg_ref, jnp.sum(dn * xh, axis=0, keepdims=True), first)
        if below:
            _acc(dgt_ref, jnp.sum(dx * m_ref[...], axis=0, keepdims=True), first)
            dm_ref[...] = (dx * gt_ref[...]).astype(BF16)

    row, vec = _rows(tr, D_MODEL), _vec(D_MODEL)
    in_specs = [row, row, vec, vec, row] + ([vec, row] if below else [])
    out_specs = [row, vec, vec, vec] + ([vec, row] if below else [])
    out_shape = [_sds((t, D_MODEL))] + [_sds((1, D_MODEL))] * 3 + ([_sds((1, D_MODEL)), _sds((t, D_MODEL), BF16)] if below else [])
    args = (dh, x, g, scale, dres) + ((gate, mix) if below else ())
    return _call(body, name=name, grid=(t // tr,), in_specs=in_specs, out_specs=out_specs, out_shape=out_shape)(*args)


def _mix_in_bwd(dmi, o_hg, proj, att, og, ag, name, tr=256):
    t = o_hg.shape[0]

    def body(d_ref, o_ref, g_ref, a_ref, og_ref, ag_ref, do_ref, dg_ref, da_ref, dd_ref, dog_ref, dag_ref):
        first = pl.program_id(0) == 0
        dog = jnp.zeros((1, HG_HEAD), F32)
        for h in range(HG_WIDTH // HG_HEAD):
            sl = slice(h * HG_HEAD, (h + 1) * HG_HEAD)
            oh = o_ref[:, sl]
            gv = g_ref[:, sl].astype(F32)
            dv = d_ref[:, sl]
            r = _rms(oh)
            xh = oh * r
            sg = _sigmoid(gv)
            dno = dv * gv * sg
            dg_ref[:, sl] = dv * xh * og_ref[...] * sg * (1.0 + gv * (1.0 - sg))
            dog = dog + jnp.sum(dno * xh, axis=0, keepdims=True)
            do_ref[:, sl] = _rms_bwd(dno * og_ref[...], xh, r)
        _acc(dog_ref, dog, first)
        av = a_ref[...]
        dav = d_ref[:, HG_WIDTH:]
        r = _rms(av)
        xa = av * r
        _acc(dag_ref, jnp.sum(dav * xa, axis=0, keepdims=True), first)
        datt = _rms_bwd(dav * ag_ref[...], xa, r)
        da_ref[...] = datt
        prod = datt * av
        lane = lax.broadcasted_iota(jnp.int32, (1, 128), 1)
        dd = jnp.zeros((tr, 128), F32)
        for hp in range(ATT_HEADS // 2):
            pp = prod[:, hp * 128:(hp + 1) * 128]
            lo = jnp.sum(jnp.where(lane < 64, pp, 0.0), axis=-1, keepdims=True)
            hi = jnp.sum(jnp.where(lane >= 64, pp, 0.0), axis=-1, keepdims=True)
            dd = jnp.where(lane == 2 * hp, lo, dd)
            dd = jnp.where(lane == 2 * hp + 1, hi, dd)
        dd_ref[...] = dd

    half = _rows(tr, HG_WIDTH)
    return _call(body, name=name, grid=(t // tr,),
                 in_specs=[_rows(tr, D_MODEL), half, pl.BlockSpec((tr, HG_WIDTH), lambda i: (i, 3)), half,
                           _vec(HG_HEAD), _vec(ATT_WIDTH)],
                 out_specs=[half, half, half, _rows(tr, 128), _vec(HG_HEAD), _vec(ATT_WIDTH)],
                 out_shape=[_sds((t, HG_WIDTH))] * 3 + [_sds((t, 128)), _sds((1, HG_HEAD)), _sds((1, ATT_WIDTH))])(
                     dmi, o_hg, proj, att, og, ag)


def _dproj(dhg, dg, dqs, dkvs, name, tr=256):
    t = dhg.shape[0]
    w3 = 3 * HG_WIDTH
    w4 = w3 + HG_WIDTH
    nbr = len(dqs)

    def body(*refs):
        h_ref, g_ref, q_refs, kv_refs, o_ref = refs[0], refs[1], refs[2:2 + nbr], refs[2 + nbr:2 + 2 * nbr], refs[-1]
        o_ref[:, :w3] = h_ref[...]
        o_ref[:, w3:w4] = g_ref[...].astype(BF16)
        o_ref[:, w4:w4 + ATT_WIDTH] = sum(r[...].astype(F32) for r in q_refs).astype(BF16)
        o_ref[:, w4 + ATT_WIDTH:] = sum(r[...].astype(F32) for r in kv_refs).astype(BF16)

    return _call(body, name=name, grid=(t // tr,),
                 in_specs=[_rows(tr, w3), _rows(tr, HG_WIDTH)] + [_rows(tr, ATT_WIDTH)] * nbr
                 + [_rows(tr, 2 * ATT_WIDTH)] * nbr,
                 out_specs=_rows(tr, IN_WIDTH), out_shape=_sds((t, IN_WIDTH), BF16))(dhg, dg, *dqs, *dkvs)


def _chunk_tri(upper):
    row = lax.broadcasted_iota(jnp.int32, (HG_GROUP, HG_CHUNK, HG_CHUNK), 1)
    col = lax.broadcasted_iota(jnp.int32, (HG_GROUP, HG_CHUNK, HG_CHUNK), 2)
    return (row <= col if upper else row >= col).astype(BF16)


def _chunk_cumsum(x, tri):
    x3 = x.reshape(HG_GROUP, HG_CHUNK, x.shape[1])
    dims = (((2,), (1,)), ((0,), (0,)))
    out = None
    for _ in range(3):
        part = x3.astype(BF16)
        x3 = x3 - part.astype(F32)
        term = lax.dot_general(tri, part, dims, preferred_element_type=F32)
        out = term if out is None else out + term
    return out.reshape(x.shape)


def _hg_gates(f_raw, q_raw, lb, tri):
    sg = _sigmoid(f_raw)
    f = lb + (1.0 - lb) * sg
    k = 1.0 - f
    b = _chunk_cumsum(jnp.log(f), tri)
    sq = _sigmoid(q_raw)
    return sg, f, k, b, sq


def _hg_masks(rows):
    row = lax.broadcasted_iota(jnp.int32, (rows, rows), 0)
    col = lax.broadcasted_iota(jnp.int32, (rows, rows), 1)
    same = (row // HG_CHUNK) == (col // HG_CHUNK)
    return jnp.logical_and(row >= col, same), jnp.logical_and(row <= col, same)


def _per_chunk(rows_of):
    return jnp.concatenate([jnp.broadcast_to(r, (HG_CHUNK, r.shape[1])) for r in rows_of], axis=0)


def _hgrn_fwd(proj, lb_logits, name):
    t = proj.shape[0]
    nc = t // HG_CHUNK
    nh = HG_WIDTH // HG_HEAD
    rows = HG_GROUP * HG_CHUNK

    def body(q_ref, f_ref, i_ref, lg_ref, o_ref, st_ref, s_scr):
        @pl.when(pl.program_id(0) == 0)
        def _():
            s_scr[...] = jnp.zeros_like(s_scr)

        lg = lg_ref[...]
        lb_all = _sigmoid(lg[0:1] - lg[1:2])
        causal, _ = _hg_masks(rows)
        tri = _chunk_tri(False)
        for h in range(nh):
            sl = slice(h * HG_HEAD, (h + 1) * HG_HEAD)
            q_raw = q_ref[:, sl].astype(F32)
            _, _, k, b, sq = _hg_gates(f_ref[:, sl].astype(F32), q_raw, lb_all[:, sl], tri)
            v = i_ref[:, sl].astype(BF16)
            gls = [b[(g + 1) * HG_CHUNK - 1:(g + 1) * HG_CHUNK] for g in range(HG_GROUP)]
            qd = (q_raw * sq * jnp.exp(b)).astype(BF16)
            kd = (k * jnp.exp(-b)).astype(BF16)
            ke = (k * jnp.exp(_per_chunk(gls) - b)).astype(BF16)
            a = jnp.where(causal, _dot_nt(qd, kd), 0.0).astype(BF16)
            o_intra = _dot(a, v)
            st = s_scr[h]
            o_inter = []
            for g in range(HG_GROUP):
                rs = slice(g * HG_CHUNK, (g + 1) * HG_CHUNK)
                st_ref[g, sl, :] = st
                o_inter.append(_dot_nt(qd[rs], st.astype(BF16)))
                st = st * jnp.exp(gls[g]) + _dot_tn(v[rs], ke[rs])
            s_scr[h] = st
            o_ref[:, sl] = o_intra + jnp.concatenate(o_inter, axis=0)

    blk = lambda j: pl.BlockSpec((rows, HG_WIDTH), lambda c: (c, j))
    return _call(body, name=name, grid=(nc // HG_GROUP,),
                 in_specs=[blk(0), blk(1), blk(2), pl.BlockSpec((2, HG_WIDTH), lambda c: (0, 0))],
                 out_specs=[blk(0), pl.BlockSpec((HG_GROUP, HG_WIDTH, HG_HEAD), lambda c: (c, 0, 0))],
                 out_shape=[_sds((t, HG_WIDTH)), _sds((nc, HG_WIDTH, HG_HEAD))],
                 scratch_shapes=[pltpu.VMEM((nh, HG_HEAD, HG_HEAD), F32)])(proj, proj, proj, lb_logits)


def _hgrn_bwd(proj, lb_logits, states, do, name):
    t = proj.shape[0]
    ng = t // (HG_GROUP * HG_CHUNK)
    nh = HG_WIDTH // HG_HEAD
    rows = HG_GROUP * HG_CHUNK

    def body(q_ref, f_ref, i_ref, lg_ref, st_ref, do_ref, d_ref, dlb_ref, ds_scr):
        first = pl.program_id(0) == 0

        @pl.when(first)
        def _():
            ds_scr[...] = jnp.zeros_like(ds_scr)

        lg = lg_ref[...]
        lb_all = _sigmoid(lg[0:1] - lg[1:2])
        causal, _ = _hg_masks(rows)
        tri = _chunk_tri(False)
        tri_t = _chunk_tri(True)
        dlb = []
        for h in range(nh):
            sl = slice(h * HG_HEAD, (h + 1) * HG_HEAD)
            q_raw = q_ref[:, sl].astype(F32)
            lb = lb_all[:, sl]
            sg, f, k, b, sq = _hg_gates(f_ref[:, sl].astype(F32), q_raw, lb, tri)
            v = i_ref[:, sl].astype(BF16)
            gls = [b[(g + 1) * HG_CHUNK - 1:(g + 1) * HG_CHUNK] for g in range(HG_GROUP)]
            eb = jnp.exp(b)
            enb = jnp.exp(-b)
            egb = jnp.exp(_per_chunk(gls) - b)
            ke = k * egb
            qd_b, kd_b, ke_b = (q_raw * sq * eb).astype(BF16), (k * enb).astype(BF16), ke.astype(BF16)
            dov = do_ref[:, sl].astype(BF16)
            a = jnp.where(causal, _dot_nt(qd_b, kd_b), 0.0).astype(BF16)
            da = jnp.where(causal, _dot_nt(dov, v), 0.0).astype(BF16)
            dkd = _dot_tn(da, qd_b)
            dst = ds_scr[h]
            dqd_s, dv_s, dke_s, dgl_s = [None] * HG_GROUP, [None] * HG_GROUP, [None] * HG_GROUP, [None] * HG_GROUP
            for g in reversed(range(HG_GROUP)):
                rs = slice(g * HG_CHUNK, (g + 1) * HG_CHUNK)
                st = st_ref[g, sl, :]
                dst_b = dst.astype(BF16)
                egl = jnp.exp(gls[g])
                dqd_s[g] = _dot(dov[rs], st.astype(BF16))
                dv_s[g] = _dot_nt(ke_b[rs], dst_b)
                dke_s[g] = _dot(v[rs], dst_b)
                dgl_s[g] = jnp.sum(dst * st, axis=0, keepdims=True) * egl
                dst = _dot_tn(dov[rs], qd_b[rs]) + dst * egl
            ds_scr[h] = dst
            dqd = _dot(da, kd_b) + jnp.concatenate(dqd_s, axis=0)
            dv = _dot_tn(a, dov) + jnp.concatenate(dv_s, axis=0)
            dke = jnp.concatenate(dke_s, axis=0)
            t1 = dke * ke
            db = dqd * qd_b.astype(F32) - dkd * kd_b.astype(F32) - t1
            dgl = _per_chunk([dgl_s[g] + jnp.sum(t1[g * HG_CHUNK:(g + 1) * HG_CHUNK], axis=0, keepdims=True)
                              for g in range(HG_GROUP)])
            dlf = _chunk_cumsum(db, tri_t) + dgl
            df = dlf / f - (dkd * enb + dke * egb)
            d_ref[:, sl] = (dqd * eb * sq * (1.0 + q_raw * (1.0 - sq))).astype(BF16)
            d_ref[:, HG_WIDTH + h * HG_HEAD:HG_WIDTH + (h + 1) * HG_HEAD] = (
                df * (1.0 - lb) * sg * (1.0 - sg)).astype(BF16)
            d_ref[:, 2 * HG_WIDTH + h * HG_HEAD:2 * HG_WIDTH + (h + 1) * HG_HEAD] = dv.astype(BF16)
            dlb.append(jnp.sum(df * (1.0 - sg), axis=0, keepdims=True))
        _acc(dlb_ref, jnp.concatenate(dlb, axis=1), first)

    rev = lambda j: pl.BlockSpec((rows, HG_WIDTH), lambda c: (ng - 1 - c, j))
    return _call(body, name=name, grid=(ng,),
                 in_specs=[rev(0), rev(1), rev(2), pl.BlockSpec((2, HG_WIDTH), lambda c: (0, 0)),
                           pl.BlockSpec((HG_GROUP, HG_WIDTH, HG_HEAD), lambda c: (ng - 1 - c, 0, 0)), rev(0)],
                 out_specs=[pl.BlockSpec((rows, 3 * HG_WIDTH), lambda c: (ng - 1 - c, 0)), _vec(HG_WIDTH)],
                 out_shape=[_sds((t, 3 * HG_WIDTH), BF16), _sds((1, HG_WIDTH))],
                 scratch_shapes=[pltpu.VMEM((nh, HG_HEAD, HG_HEAD), F32)])(proj, proj, proj, lb_logits, states, do)


def _to_sub(a, dil):
    t, w = a.shape
    return a if dil == 1 else a.reshape(t // dil, dil, w).transpose(1, 0, 2).reshape(t, w)


def _from_sub(a, dil):
    t, w = a.shape
    return a if dil == 1 else a.reshape(dil, t // dil, w).transpose(1, 0, 2).reshape(t, w)


def _att_mask(has_prev):
    qi = lax.broadcasted_iota(jnp.int32, (2 * ATT_BLOCK, 2 * ATT_BLOCK), 0) % ATT_BLOCK
    kj = lax.broadcasted_iota(jnp.int32, (2 * ATT_BLOCK, 2 * ATT_BLOCK), 1)
    prev = jnp.logical_and(jnp.logical_and(kj < ATT_BLOCK, kj >= qi), has_prev)
    cur = jnp.logical_and(kj >= ATT_BLOCK, kj - ATT_BLOCK <= qi)
    return jnp.logical_or(prev, cur), lax.broadcasted_iota(jnp.int32, (1, 128), 1)


def _attn_fwd(qkv, dil, name):
    t = qkv.shape[0]
    nb = t // ATT_BLOCK
    bps = nb // dil

    def body(q_ref, kc_ref, kp_ref, vc_ref, vp_ref, o_ref, l_ref):
        mask, lane = _att_mask((pl.program_id(0) % bps) != 0)
        lo = lane < 64
        nq = ATT_BLOCK
        lse_all = jnp.zeros((nq, 128), F32)
        for hp in range(ATT_HEADS // 2):
            sl = slice(hp * 128, (hp + 1) * 128)
            q2 = q_ref[:, sl]
            zero = jnp.zeros_like(q2)
            qs = jnp.concatenate([jnp.where(lo, q2, zero), jnp.where(lo, zero, q2)], axis=0)
            kk = jnp.concatenate([kp_ref[:, sl], kc_ref[:, sl]], axis=0)
            vv = jnp.concatenate([vp_ref[:, sl], vc_ref[:, sl]], axis=0)
            s = jnp.where(mask, _dot_nt(qs, kk) * 0.125, NEG)
            mx = jnp.max(s, axis=-1, keepdims=True)
            p = jnp.exp(s - mx)
            l = jnp.sum(p, axis=-1, keepdims=True)
            o = _dot(p.astype(BF16), vv) * (1.0 / l)
            o_ref[:, sl] = jnp.where(lo, o[:nq], o[nq:])
            lse = mx + jnp.log(l)
            lse_all = jnp.where(lane == 2 * hp, lse[:nq], lse_all)
            lse_all = jnp.where(lane == 2 * hp + 1, lse[nq:], lse_all)
        l_ref[...] = lse_all

    col0 = qkv.shape[1] // ATT_WIDTH - 3
    blk = lambda j, back: pl.BlockSpec((ATT_BLOCK, ATT_WIDTH), lambda n: (jnp.maximum(n - back, 0), col0 + j))
    return _call(body, name=name, grid=(nb,),
                 in_specs=[blk(0, 0), blk(1, 0), blk(1, 1), blk(2, 0), blk(2, 1)],
                 out_specs=[pl.BlockSpec((ATT_BLOCK, ATT_WIDTH), lambda n: (n, 0)),
                            pl.BlockSpec((ATT_BLOCK, 128), lambda n: (n, 0))],
                 out_shape=[_sds((t, ATT_WIDTH)), _sds((t, 128))])(qkv, qkv, qkv, qkv, qkv)


def _attn_combine(os_, ls_, name, tr=256):
    t = os_[0].shape[0]
    nbr = len(os_)

    def body(*refs):
        o_refs, l_refs, (a_ref, lt_ref) = refs[:nbr], refs[nbr:2 * nbr], refs[2 * nbr:]
        lane = lax.broadcasted_iota(jnp.int32, (1, 128), 1)
        ls = [r[...] for r in l_refs]
        mx = functools.reduce(jnp.maximum, ls)
        tot = mx + jnp.log(sum(jnp.exp(l - mx) for l in ls))
        lt_ref[...] = tot
        ws = [jnp.exp(l - tot) for l in ls]
        for hp in range(ATT_HEADS // 2):
            sl = slice(hp * 128, (hp + 1) * 128)
            acc = jnp.zeros((tr, 128), F32)
            for w, o_ref in zip(ws, o_refs):
                wf = jnp.where(lane < 64, w[:, 2 * hp:2 * hp + 1], w[:, 2 * hp + 1:2 * hp + 2])
                acc = acc + wf * o_ref[:, sl]
            a_ref[:, sl] = acc

    return _call(body, name=name, grid=(t // tr,),
                 in_specs=[_rows(tr, ATT_WIDTH)] * nbr + [_rows(tr, 128)] * nbr,
                 out_specs=[_rows(tr, ATT_WIDTH), _rows(tr, 128)],
                 out_shape=[_sds((t, ATT_WIDTH)), _sds((t, 128))])(*os_, *ls_)


def _attn_bwd(qkv, dout, lse, dd, dil, name):
    t = qkv.shape[0]
    nb = t // ATT_BLOCK
    bps = nb // dil

    w = ATT_WIDTH
    nq = ATT_BLOCK

    def body(q_ref, kc_ref, kp_ref, vc_ref, vp_ref, do_ref, l_ref, d_ref, dq_ref, dkv_ref, carry):
        n = pl.program_id(0)

        @pl.when(n == 0)
        def _():
            carry[...] = jnp.zeros_like(carry)

        @pl.when(n < nb)
        def _():
            mask, lane = _att_mask((n % bps) != 0)
            lo = lane < 64
            for hp in range(ATT_HEADS // 2):
                sl = slice(hp * 128, (hp + 1) * 128)
                sv = slice(w + hp * 128, w + (hp + 1) * 128)
                q2, do2 = q_ref[:, sl], do_ref[:, sl]
                zero = jnp.zeros_like(q2)
                qs = jnp.concatenate([jnp.where(lo, q2, zero), jnp.where(lo, zero, q2)], axis=0)
                dos = jnp.concatenate([jnp.where(lo, do2, zero), jnp.where(lo, zero, do2)], axis=0)
                kk = jnp.concatenate([kp_ref[:, sl], kc_ref[:, sl]], axis=0)
                vv = jnp.concatenate([vp_ref[:, sl], vc_ref[:, sl]], axis=0)
                ls = jnp.concatenate([l_ref[:, 2 * hp:2 * hp + 1], l_ref[:, 2 * hp + 1:2 * hp + 2]], axis=0)
                dh = jnp.concatenate([d_ref[:, 2 * hp:2 * hp + 1], d_ref[:, 2 * hp + 1:2 * hp + 2]], axis=0)
                p = jnp.exp(jnp.where(mask, _dot_nt(qs, kk) * 0.125 - ls, NEG))
                ds = (p * (_dot_nt(dos, vv) - dh)).astype(BF16)
                dq = _dot(ds, kk) * 0.125
                dq_ref[:, sl] = jnp.where(lo, dq[:nq], dq[nq:]).astype(BF16)
                dk = _dot_tn(ds, qs) * 0.125
                dv = _dot_tn(p.astype(BF16), dos)
                dkv_ref[:, sl] = (carry[:, sl] + dk[:nq]).astype(BF16)
                dkv_ref[:, sv] = (carry[:, sv] + dv[:nq]).astype(BF16)
                carry[:, sl] = dk[nq:]
                carry[:, sv] = dv[nq:]

        @pl.when(n == nb)
        def _():
            dkv_ref[...] = carry[...].astype(BF16)

    def blk(width, j, back):
        return pl.BlockSpec((nq, width), lambda n: (jnp.clip(n - back, 0, nb - 1), j))

    c0 = qkv.shape[1] // w - 3
    return _call(body, name=name, grid=(nb + 1,),
                 in_specs=[blk(w, c0, 0), blk(w, c0 + 1, 0), blk(w, c0 + 1, 1), blk(w, c0 + 2, 0), blk(w, c0 + 2, 1),
                           blk(w, 0, 0), blk(128, 0, 0), blk(128, 0, 0)],
                 out_specs=[blk(w, 0, 0), blk(2 * w, 0, 1)],
                 out_shape=[_sds((t, w), BF16), _sds((t, 2 * w), BF16)],
                 scratch_shapes=[pltpu.VMEM((nq, 2 * w), F32)])(qkv, qkv, qkv, qkv, qkv, dout, lse, dd)


def _local_step(x, tgt, mod, norm1_g, lb_logits, og, ag, norm2_g, fg, get_w, put_g, late=lambda a: a):
    shift1, scale1, gate1, shift2, scale2, gate2 = [mod[:, i * D_MODEL:(i + 1) * D_MODEL] for i in range(6)]
    fg = fg.reshape(1, D_MODEL)

    h1 = _norm_mod(x, norm1_g, scale1, shift1, "norm_mod1")
    w_in = get_w("w_in", h1)
    proj = _mm_nn(h1, w_in, "mm_in", out_dtype=BF16)
    o_hg, states = _hgrn_fwd(proj, lb_logits, "hgrn_fwd")
    qkvs = [proj if d == 1 else _to_sub(proj[:, 4 * HG_WIDTH:], d) for d in DILATIONS]
    outs = [_attn_fwd(q, d, f"attn_fwd{d}") for q, d in zip(qkvs, DILATIONS)]
    att, lse = _attn_combine([_from_sub(o, d) for (o, _), d in zip(outs, DILATIONS)],
                             [_from_sub(l, d) for (_, l), d in zip(outs, DILATIONS)], "attn_combine")
    mixin = _mix_in(o_hg, proj, att, og, ag, "mix_in")
    w_out = get_w("w_out", mixin)
    mix = _mm_nn(mixin, w_out, "mm_out")
    x2, h2 = _resid_norm_mod(x, mix, gate1, norm2_g, scale2, shift2, "resid_norm_mod2")
    w_gu = get_w("w_gu", h2)
    a_ff, u_ff, act = _mm_gate_up(h2, w_gu, "mm_gu")
    w_down = get_w("w_down", act)
    ffn = _mm_nn(act, w_down, "mm_down")
    dx3, dffn, loss_v, dfg, dgate2 = _final_loss(x2, ffn, gate2, fg, tgt, "final_loss")

    dffn = put_g("w_down", *_mm_tn(act, dffn, 1, "mm_down_dw", tm=2048, tk=D_FF // 2), dffn)
    dau = _mm_down_dx(dffn, w_down, a_ff, u_ff, "mm_down_dx")
    dau = put_g("w_gu", *_mm_tn(h2, dau, N_SHARD, "mm_gu_dw", tm=x.shape[0], tk=512), dau)
    dh2 = _mm_nt(dau, w_gu, "mm_gu_dx")
    dx2, dshift2, dscale2, dg2, dgate1, dmix = _norm_mod_bwd(
        dh2, x2, norm2_g, scale2, dx3, "norm_mod2_bwd", gate=gate1, mix=mix)
    dmix = put_g("w_out", *_mm_tn(mixin, dmix, 1, "mm_out_dw", tm=x.shape[0], tk=512), dmix)
    dmixin = _mm_nt(dmix, w_out, "mm_out_dx", tm=1024)
    do_hg, dg_raw, datt, dd, dog, dag = _mix_in_bwd(dmixin, o_hg, proj, att, og, ag, "mix_in_bwd")
    datt_b = datt.astype(BF16)
    datts = [_attn_bwd(q, _to_sub(datt_b, d), _to_sub(lse, d), _to_sub(dd, d), d, f"attn_bwd{d}")
             for q, d in zip(qkvs, DILATIONS)]
    dhg, dlb = _hgrn_bwd(proj, lb_logits, states, do_hg, "hgrn_bwd")
    dhg = late(dhg)
    dproj = _dproj(dhg, dg_raw, [_from_sub(dq, d) for (dq, _), d in zip(datts, DILATIONS)],
                   [_from_sub(dkv, d) for (_, dkv), d in zip(datts, DILATIONS)], "dproj")
    dproj = put_g("w_in", *_mm_tn(h1, dproj, N_SHARD, "mm_in_dw", tm=x.shape[0], tk=512), dproj)
    dh1 = _mm_nt(dproj, w_in, "mm_in_dx", tm=1024)
    dx, dshift1, dscale1, dg1 = _norm_mod_bwd(dh1, x, norm1_g, scale1, dx2, "norm_mod1_bwd")

    stats = jnp.concatenate([loss_v, dfg, dg2, dg1, dlb, dag, dog,
                             dshift1, dscale1, dgate1, dshift2, dscale2, dgate2], axis=1)
    return dx, stats


def _place():
    x, y, c = lax.axis_index("x"), lax.axis_index("y"), lax.axis_index("c")
    return x, y, c


def _chip_peers(x, y, c):
    return [(1 - x, y, c), (x, 1 - y, c), (1 - x, 1 - y, c)]


def _comm_call(body, name, n_in, out_shape, scratch_shapes):
    hbm = pl.BlockSpec(memory_space=pl.ANY)
    return pl.pallas_call(body, name=name, in_specs=[hbm] * n_in, out_specs=[hbm] * len(out_shape),
                          out_shape=out_shape, scratch_shapes=scratch_shapes)


_HBM = pl.BlockSpec(memory_space=pltpu.HBM)
_SEM = pl.BlockSpec(memory_space=pltpu.SEMAPHORE)
_EFFECT = pltpu.SideEffectType.DATAFLOW_SIDE_EFFECTING


def _exchange_copy(bufs, send, recv, j, peer, place, kind):
    x, y, c = place
    target = peer
    if kind == "gather":
        src = dst = bufs[0].at[2 * x + y]
    elif kind == "scatter":
        src, dst = bufs[0].at[2 * peer[0] + peer[1]], bufs[1].at[j]
    else:
        half = bufs[0].shape[1] // 2
        rows = pl.ds(c * half, half)
        if kind == "half":
            src = dst = bufs[0].at[2 * x + y, rows]
        else:
            src = dst = bufs[0].at[2 * peer[0] + peer[1], rows]
            target = (x, y, 1 - c)
    return pltpu.make_async_remote_copy(src_ref=src, dst_ref=dst, send_sem=send.at[j], recv_sem=recv.at[j],
                                        device_id=target, device_id_type=MESH)


def _exchange_start(groups, after, kind, name):
    sizes = [len(g) for g in groups]
    flat = [b for g in groups for b in g]
    ng, nb = len(groups), len(flat)

    def body(*refs):
        bufs, sems = refs[:nb], refs[nb + 1:nb + 1 + 2 * ng]
        x, y, c = _place()
        for j, peer in enumerate(_chip_peers(x, y, c)):
            at = 0
            for i, size in enumerate(sizes):
                _exchange_copy(bufs[at:at + size], sems[2 * i], sems[2 * i + 1], j, peer, (x, y, c), kind).start()
                at += size

    any_space = pl.BlockSpec(memory_space=pl.ANY)
    out = pl.pallas_call(
        body, name=name, in_specs=[_HBM] * nb + [any_space],
        out_specs=[_SEM] * (2 * ng) + [_HBM] * nb + [any_space],
        out_shape=[pltpu.SemaphoreType.DMA((3,))] * (2 * ng) + [pltpu.HBM(b.shape, b.dtype) for b in flat]
        + [_sds(after.shape, after.dtype)],
        input_output_aliases={i: 2 * ng + i for i in range(nb + 1)},
        compiler_params=pltpu.CompilerParams(has_side_effects=_EFFECT),
    )(*[pltpu.with_memory_space_constraint(b, pltpu.HBM) for b in flat], after)
    started, at = [], 2 * ng
    for i, size in enumerate(sizes):
        started.append((out[2 * i], out[2 * i + 1], tuple(out[at:at + size])))
        at += size
    return started, out[-1]


def _exchange_wait(started, after, kind, name):
    send, recv, bufs = started
    nb = len(bufs)

    def body(*refs):
        x, y, c = _place()
        for j, peer in enumerate(_chip_peers(x, y, c)):
            cp = _exchange_copy(refs[:nb], refs[nb], refs[nb + 1], j, peer, (x, y, c), kind)
            cp.wait_send()
            cp.wait_recv()

    return pl.pallas_call(
        body, name=name, in_specs=[_HBM] * nb + [_SEM, _SEM, pl.BlockSpec(memory_space=pl.ANY)],
        out_specs=[_HBM] * nb, out_shape=[pltpu.HBM(b.shape, b.dtype) for b in bufs],
        input_output_aliases={i: i for i in range(nb)},
        compiler_params=pltpu.CompilerParams(has_side_effects=_EFFECT),
    )(*bufs, send, recv, after)


def _sibling_copies(v_refs, l_refs, send, recv):
    x, y, c = _place()
    return [pltpu.make_async_remote_copy(src_ref=v, dst_ref=l, send_sem=send.at[a], recv_sem=recv.at[a],
                                         device_id=(x, y, 1 - c), device_id_type=MESH)
            for a, (v, l) in enumerate(zip(v_refs, l_refs))]


def _sibling_start(vs, after, name):
    n = len(vs)
    lands = [lax.empty(v.shape, v.dtype) for v in vs]

    def body(*refs):
        for cp in _sibling_copies(refs[:n], refs[n:2 * n], refs[2 * n + 1], refs[2 * n + 2]):
            cp.start()

    any_space = pl.BlockSpec(memory_space=pl.ANY)
    out = pl.pallas_call(
        body, name=name, in_specs=[_HBM] * (2 * n) + [any_space],
        out_specs=[_SEM, _SEM] + [_HBM] * (2 * n) + [any_space],
        out_shape=[pltpu.SemaphoreType.DMA((n,))] * 2 + [pltpu.HBM(b.shape, b.dtype) for b in vs + lands]
        + [_sds(after.shape, after.dtype)],
        input_output_aliases={i: 2 + i for i in range(2 * n + 1)},
        compiler_params=pltpu.CompilerParams(has_side_effects=_EFFECT),
    )(*[pltpu.with_memory_space_constraint(b, pltpu.HBM) for b in vs + lands], after)
    return (out[0], out[1], tuple(out[2:2 + n]), tuple(out[2 + n:2 + 2 * n])), out[-1]


def _sibling_wait(started, after, name):
    send, recv, vs, lands = started
    n = len(vs)

    def body(*refs):
        for cp in _sibling_copies(refs[:n], refs[n:2 * n], refs[2 * n], refs[2 * n + 1]):
            cp.wait_send()
            cp.wait_recv()

    out = pl.pallas_call(
        body, name=name, in_specs=[_HBM] * (2 * n) + [_SEM, _SEM, pl.BlockSpec(memory_space=pl.ANY)],
        out_specs=[_HBM] * (2 * n), out_shape=[pltpu.HBM(b.shape, b.dtype) for b in vs + lands],
        input_output_aliases={i: i for i in range(2 * n)},
        compiler_params=pltpu.CompilerParams(has_side_effects=_EFFECT),
    )(*vs, *lands, send, recv, after)
    return out[:n], out[n:]


def _swap_sibling(vs, name):
    n = len(vs)

    def body(*refs):
        v_refs, o_refs, (send, recv) = refs[:n], refs[n:2 * n], refs[2 * n:]
        x, y, c = _place()
        cps = [pltpu.make_async_remote_copy(
            src_ref=v_refs[a], dst_ref=o_refs[a], send_sem=send.at[a], recv_sem=recv.at[a],
            device_id=(x, y, 1 - c), device_id_type=MESH) for a in range(n)]
        for cp in cps:
            cp.start()
        for cp in cps:
            cp.wait()

    return _comm_call(body, name, n, [_sds(v.shape, v.dtype) for v in vs],
                      [pltpu.SemaphoreType.DMA((n,)), pltpu.SemaphoreType.DMA((n,))])(*vs)


def _gather_rows(v, name):
    r, n = v.shape

    def body(v_ref, o_ref, send, recv, loc):
        x, y, c = _place()
        me = 4 * x + 2 * y + c
        own = pltpu.make_async_copy(v_ref, o_ref.at[me], loc)
        own.start()
        peers = []
        for k in range(1, 8):
            px = 1 - x if k & 4 else x
            py = 1 - y if k & 2 else y
            pc = 1 - c if k & 1 else c
            peers.append((px, py, pc))
        sends = []
        for k, peer in enumerate(peers):
            cp = pltpu.make_async_remote_copy(src_ref=v_ref, dst_ref=o_ref.at[me], send_sem=send.at[k],
                                              recv_sem=recv.at[k], device_id=peer, device_id_type=MESH)
            cp.start()
            sends.append(cp)
        for k, peer in enumerate(peers):
            pltpu.make_async_remote_copy(src_ref=v_ref, dst_ref=o_ref.at[4 * peer[0] + 2 * peer[1] + peer[2]],
                                         send_sem=send.at[k], recv_sem=recv.at[k], device_id=peer,
                                         device_id_type=MESH).wait_recv()
        for cp in sends:
            cp.wait_send()
        own.wait()

    vmem = pl.BlockSpec(memory_space=pltpu.VMEM)
    return pl.pallas_call(body, name=name, in_specs=[vmem], out_specs=vmem, out_shape=_sds((8, r, n), v.dtype),
                          scratch_shapes=[pltpu.SemaphoreType.DMA((7,)), pltpu.SemaphoreType.DMA((7,)),
                                          pltpu.SemaphoreType.DMA])(v)


def _cast_place(w, shard, name):
    r, c = w.shape
    tr = r // 4

    def body(s_ref, w_ref, o_ref):
        o_ref[0] = w_ref[...].astype(BF16)

    return pl.pallas_call(
        body, name=name, out_shape=_sds((N_SHARD, r, c), BF16),
        grid_spec=pltpu.PrefetchScalarGridSpec(
            num_scalar_prefetch=1, grid=(4,), in_specs=[pl.BlockSpec((tr, c), lambda i, s: (i, 0))],
            out_specs=pl.BlockSpec((1, tr, c), lambda i, s: (s[0], i, 0))),
        compiler_params=pltpu.CompilerParams(dimension_semantics=("arbitrary",)),
    )(shard.reshape(1).astype(jnp.int32), w)


def _mod_part(c_all, w_ada, b_ada, name):
    n = w_ada.shape[1]

    def body(c_ref, w_ref, b_ref, a_ref, p_ref):
        cv = c_ref[...]
        ca = cv * _sigmoid(cv)
        a_ref[...] = ca
        p_ref[...] = jnp.dot(ca, w_ref[...], precision=lax.Precision.HIGHEST, preferred_element_type=F32) + b_ref[...]

    full = lambda a: pl.BlockSpec(a.shape, lambda i: (0, 0))
    return _call(body, name=name, grid=(1,), in_specs=[full(c_all), full(w_ada), full(b_ada)],
                 out_specs=[pl.BlockSpec((8, D_MODEL), lambda i: (0, 0)), pl.BlockSpec((8, n), lambda i: (0, 0))],
                 out_shape=[_sds((8, D_MODEL)), _sds((8, n))])(c_all, w_ada, b_ada)


def _sum_received(g, shard, land, name):
    _, r, c = g.shape
    tr = r // 4

    def body(s_ref, g_ref, l_ref, o_ref):
        o_ref[...] = ((g_ref[0] + l_ref[0].astype(F32)) + l_ref[1].astype(F32)) + l_ref[2].astype(F32)

    return pl.pallas_call(
        body, name=name, out_shape=_sds((r, c)),
        grid_spec=pltpu.PrefetchScalarGridSpec(
            num_scalar_prefetch=1, grid=(4,),
            in_specs=[pl.BlockSpec((1, tr, c), lambda i, s: (s[0], i, 0)),
                      pl.BlockSpec((3, tr, c), lambda i, s: (0, i, 0))],
            out_specs=pl.BlockSpec((tr, c), lambda i, s: (i, 0))),
        compiler_params=pltpu.CompilerParams(dimension_semantics=("arbitrary",), vmem_limit_bytes=VMEM_LIMIT),
    )(shard.reshape(1).astype(jnp.int32), g, land)


def _outer_sum(ct, dm, name):
    k, n = ct.shape[0], dm.shape[1]
    tr = k // 4

    def body(c_ref, d_ref, o_ref):
        cv = c_ref[...]
        dv = d_ref[...]
        acc = cv[:, 0:1] * dv[0:1, :]
        for i in range(1, 8):
            acc = acc + cv[:, i:i + 1] * dv[i:i + 1, :]
        o_ref[...] = acc

    return _call(body, name=name, grid=(4,), in_specs=[_rows(tr, 8), pl.BlockSpec((8, n), lambda i: (0, 0))],
                 out_specs=_rows(tr, n), out_shape=_sds((k, n)))(ct, dm)


def _adamw_math(w, g, m, v):
    m_new = ADAM_B1 * m + (1.0 - ADAM_B1) * g
    v_new = ADAM_B2 * v + (1.0 - ADAM_B2) * (g * g)
    m_hat = m_new / (1.0 - ADAM_B1 ** ADAM_STEP)
    v_hat = v_new / (1.0 - ADAM_B2 ** ADAM_STEP)
    return -ADAM_LR * (m_hat / (jnp.sqrt(v_hat) + ADAM_EPS) + ADAM_WD * w), m_new, v_new


def _small_update(stats, smalls, name):
    offsets = [ST_DMOD, ST_DG1, ST_DLB, ST_DOG, ST_DAG, ST_DG2, ST_DFG]
    lb_index = 2

    def body(*refs):
        s_ref, ins, l_ref, outs = refs[0], refs[1:22], refs[22], refs[23:]
        tot = s_ref[0:1, :]
        for i in range(1, 8):
            tot = tot + s_ref[i:i + 1, :]
        l_ref[...] = jnp.zeros((1, 128), F32) + (0.5 / D_MODEL) * jnp.sum(tot[:, ST_LOSS:ST_LOSS + D_MODEL])
        for p, off in enumerate(offsets):
            w_ref, m_ref, v_ref = ins[3 * p:3 * p + 3]
            g_out, d_out, m_out, v_out = outs[4 * p:4 * p + 4]
            g = tot[:, off:off + w_ref.shape[1]]
            if p == lb_index:
                lg = w_ref[...]
                lb = _sigmoid(lg[0:1] - lg[1:2])
                g = g * lb * (1.0 - lb)
            for r in range(w_ref.shape[0]):
                rows = slice(r, r + 1)
                gr = g if r == 0 else -g
                delta, m_new, v_new = _adamw_math(w_ref[rows, :], gr, m_ref[rows, :], v_ref[rows, :])
                g_out[rows, :] = gr
                d_out[rows, :] = delta
                m_out[rows, :] = m_new
                v_out[rows, :] = v_new

    full = lambda a: pl.BlockSpec(a.shape, lambda i: (0, 0))
    flat = [a for t in smalls for a in t]
    return _call(body, name=name, grid=(1,),
                 in_specs=[full(stats)] + [full(a) for a in flat],
                 out_specs=[pl.BlockSpec((1, 128), lambda i: (0, 0))] + [full(t[0]) for t in smalls for _ in range(4)],
                 out_shape=[_sds((1, 128))] + [_sds(t[0].shape) for t in smalls for _ in range(4)])(stats, *flat)


def _adamw(w, gs, m, v, name, steps=4):
    r, c = w.shape
    tr = r // steps
    ng = len(gs)

    def body(*refs):
        w_ref, g_refs, (m_ref, v_ref, g_out, d_out, m_out, v_out) = refs[0], refs[1:1 + ng], refs[1 + ng:]
        g = g_refs[0][...]
        for g_ref in g_refs[1:]:
            g = g + g_ref[...]
        g_out[...] = g
        d_out[...], m_out[...], v_out[...] = _adamw_math(w_ref[...], g, m_ref[...], v_ref[...])

    row = _rows(tr, c)
    return _call(body, name=name, grid=(steps,), in_specs=[row] * (3 + ng), out_specs=[row] * 4,
                 out_shape=[_sds((r, c))] * 4)(w, *gs, m, v)


def kernel(x, c, w_ada, b_ada, norm1_g, w_in, hg_lb_logits, hg_onorm_g, att_onorm_g, w_out, norm2_g, w_gate_up, w_down, final_g, loss_target, m_w_ada, m_b_ada, m_norm1_g, m_w_in, m_hg_lb_logits, m_hg_onorm_g, m_att_onorm_g, m_w_out, m_norm2_g, m_w_gate_up, m_w_down, m_final_g, v_w_ada, v_b_ada, v_norm1_g, v_w_in, v_hg_lb_logits, v_hg_onorm_g, v_att_onorm_g, v_w_out, v_norm2_g, v_w_gate_up, v_w_down, v_final_g):
    ix, iy, ic = _place()
    shard = 2 * ix + iy
    sample = 4 * ix + 2 * iy + ic
    n_ada = w_ada.shape[2]

    shards = [w_in[0], w_out[0], w_gate_up[0], w_down[0]]
    names = ["w_in", "w_out", "w_gu", "w_down"]
    shapes = [(N_SHARD,) + w.shape for w in shards]
    placed = [(_cast_place(w, shard, "place_" + nm),) for w, nm in zip(shards, names)]

    c_all = _gather_rows(c, "gather_c").reshape(8, D_MODEL)
    b_part = lax.dynamic_slice(b_ada, (0, shard * n_ada), (1, n_ada))
    c_act, part = _mod_part(c_all, w_ada[0], b_part, "mod_part")
    parts = _gather_rows(part, "gather_mod")[::2]
    mod = lax.dynamic_index_in_dim(parts, sample, axis=1, keepdims=False).reshape(1, 6 * D_MODEL)
    (first,), mod = _exchange_start(placed[:1], mod, "half", "gather_start_w_in")
    gathering = {}

    def get_w(name, after):
        if name == "w_in":
            halves = _exchange_wait(first, after, "half", "gather_wait_w_in")
            (passing,), after = _exchange_start([tuple(halves)], after, "forward", "forward_start_w_in")
            (full,) = _exchange_wait(passing, after, "forward", "forward_wait_w_in")
            rest, full = _exchange_start(placed[1:], full, "gather", "gather_start_rest")
            gathering.update(zip(names[1:], rest))
            return full
        (full,) = _exchange_wait(gathering[name], after, "gather", "gather_wait_" + name)
        return full if name == "w_gu" else full.reshape(1, -1, D_MODEL)

    scattering = {}

    def put_g(name, g, g_bf16, then):
        shape = shapes[names.index(name)]
        land = lax.empty((3,) + shape[1:], BF16)
        (started,), then = _exchange_start([(g_bf16.reshape(shape), land)], then, "scatter", "scatter_start_" + name)
        scattering[name] = (g.reshape(shape), started)
        return then

    def summed(name, after):
        g, started = scattering[name]
        _, land = _exchange_wait(started, after, "scatter", "scatter_wait_" + name)
        return _sum_received(g, shard, land, "sum_" + name)

    early = ["w_down", "w_gu", "w_out"]
    swapping = []

    def late(a):
        started, a = _sibling_start([summed(nm, a) for nm in early], a, "swap_start")
        swapping.append(started)
        return a

    dx, stats = _local_step(x[0], loss_target[0], mod, norm1_g, hg_lb_logits, hg_onorm_g, att_onorm_g,
                            norm2_g, final_g, get_w, put_g, late)

    stats_all = _gather_rows(stats, "gather_stats").reshape(8, ST_WIDTH)
    dmod = lax.dynamic_slice(stats_all, (0, ST_DMOD + shard * n_ada), (8, n_ada))
    g_ada = _outer_sum(c_act.T, dmod, "w_ada_grad")

    as_row = lambda a: a.reshape(1, -1) if a.ndim == 1 else a
    smalls = [tuple(as_row(a) for a in t) for t in [
        (b_ada, m_b_ada, v_b_ada), (norm1_g, m_norm1_g, v_norm1_g),
        (hg_lb_logits, m_hg_lb_logits, v_hg_lb_logits), (hg_onorm_g, m_hg_onorm_g, v_hg_onorm_g),
        (att_onorm_g, m_att_onorm_g, v_att_onorm_g), (norm2_g, m_norm2_g, v_norm2_g),
        (final_g, m_final_g, v_final_g)]]
    loss, *small_out = _small_update(stats_all, smalls, "small_update")
    shapes_out = [b_ada.shape, norm1_g.shape, hg_lb_logits.shape, hg_onorm_g.shape, att_onorm_g.shape,
                  norm2_g.shape, final_g.shape]
    sg, sd, sm, sv = [[small_out[4 * p + i].reshape(shapes_out[p]) for p in range(7)] for i in range(4)]

    ada = _adamw(w_ada[0], [g_ada], m_w_ada[0], v_w_ada[0], "adamw_w_ada")
    moments = [(m_w_in, v_w_in), (m_w_out, v_w_out), (m_w_gate_up, v_w_gate_up), (m_w_down, v_w_down)]

    def update(group, sums, other):
        return {nm: _adamw(shards[names.index(nm)], [s, o], moments[names.index(nm)][0][0],
                           moments[names.index(nm)][1][0], "adamw_" + nm) for nm, s, o in zip(group, sums, other)}

    sums, other = _sibling_wait(swapping[0], ada[1], "swap_wait")
    done = update(early, sums, other)
    sum_in = summed("w_in", done["w_out"][1])
    done.update(update(["w_in"], [sum_in], _swap_sibling([sum_in], "swap_sum_in")))
    big = [ada] + [done[nm] for nm in names]
    bg, bd, bm, bv = [[t[i][None] for t in big] for i in range(4)]

    def order(b, s):
        return [b[0], s[0], s[1], b[1], s[2], s[3], s[4], b[2], s[5], b[3], b[4], s[6]]

    return (loss[0, 0], dx[None], *order(bg, sg), *order(bd, sd), *order(bm, sm), *order(bv, sv))
```

```python
import functools

import jax
import jax.numpy as jnp
from jax import lax
from jax.experimental import pallas as pl
from jax.experimental.pallas import tpu as pltpu

F32 = jnp.float32
BF16 = jnp.bfloat16
MESH = pl.DeviceIdType.MESH

D_MODEL = 1024
HG_WIDTH = 512
HG_HEAD = 128
HG_CHUNK = 64
HG_GROUP = 4
ATT_WIDTH = 512
ATT_HEADS = 8
ATT_BLOCK = 128
DILATIONS = (1, 4, 16)
D_FF = 2816
IN_WIDTH = 3584
N_SHARD = 4
RMS_EPS = 1e-6
NEG = -1e30

ADAM_LR = 0.001
ADAM_B1 = 0.9
ADAM_B2 = 0.999
ADAM_EPS = 1e-08
ADAM_WD = 0.01
ADAM_STEP = 10

VMEM_LIMIT = 56 * 2**20

ST_LOSS, ST_DFG, ST_DG2, ST_DG1 = 0, 1024, 2048, 3072
ST_DLB, ST_DAG, ST_DOG, ST_DMOD = 4096, 4608, 5120, 5248
ST_WIDTH = 5248 + 6144
SP_BADA, SP_N1, SP_LB, SP_OG, SP_AG, SP_N2, SP_FG = 0, 6144, 7168, 8192, 8320, 8832, 9856
SP_WIDTH = 10880


def _call(body, *, name, grid, in_specs, out_specs, out_shape, scratch_shapes=()):
    return pl.pallas_call(
        body, name=name, grid=grid, in_specs=in_specs, out_specs=out_specs, out_shape=out_shape,
        scratch_shapes=list(scratch_shapes),
        compiler_params=pltpu.CompilerParams(
            dimension_semantics=("arbitrary",) * len(grid), vmem_limit_bytes=VMEM_LIMIT))


def _sds(shape, dtype=F32):
    return jax.ShapeDtypeStruct(shape, dtype)


def _dot(a, b):
    return jnp.dot(a, b, preferred_element_type=F32)


def _dot_nt(a, b):
    return lax.dot_general(a, b, (((1,), (1,)), ((), ())), preferred_element_type=F32)


def _dot_tn(a, b):
    return lax.dot_general(a, b, (((0,), (0,)), ((), ())), preferred_element_type=F32)


def _sigmoid(x):
    return 1.0 / (1.0 + jnp.exp(-x))


def _rows(tr, width):
    return pl.BlockSpec((tr, width), lambda i: (i, 0))


def _vec(width):
    return pl.BlockSpec((1, width), lambda i: (0, 0))


def _acc(ref, val, first):
    @pl.when(first)
    def _():
        ref[...] = val

    @pl.when(jnp.logical_not(first))
    def _():
        ref[...] += val


def _mm_nn(a, b3, name, tm=1024, out_dtype=F32):
    m, k = a.shape
    s, _, n = b3.shape

    def body(a_ref, b_ref, o_ref):
        o_ref[...] = _dot(a_ref[...], b_ref[0]).astype(out_dtype)

    return _call(
        body, name=name, grid=(s, m // tm),
        in_specs=[pl.BlockSpec((tm, k), lambda j, i: (i, 0)), pl.BlockSpec((1, k, n), lambda j, i: (j, 0, 0))],
        out_specs=pl.BlockSpec((tm, n), lambda j, i: (i, j)), out_shape=_sds((m, s * n), out_dtype))(a, b3)


def _mm_nt(dy, b3, name, tm=512):
    m = dy.shape[0]
    s, k, n = b3.shape

    def body(dy_ref, b_ref, o_ref):
        acc = _dot_nt(dy_ref[:, 0:n], b_ref[0])
        for j in range(1, s):
            acc = acc + _dot_nt(dy_ref[:, j * n:(j + 1) * n], b_ref[j])
        o_ref[...] = acc

    return _call(
        body, name=name, grid=(m // tm,),
        in_specs=[_rows(tm, s * n), pl.BlockSpec((s, k, n), lambda i: (0, 0, 0))],
        out_specs=_rows(tm, k), out_shape=_sds((m, k)))(dy, b3)


def _mm_tn(a, dy, s, name, tm, tk):
    m, k = a.shape
    n = dy.shape[1] // s
    steps = m // tm

    def body(a_ref, dy_ref, o_ref, ob_ref):
        p = _dot_tn(a_ref[...], dy_ref[...])[None]
        if steps == 1:
            o_ref[...] = p
            ob_ref[...] = p.astype(BF16)
        else:
            i = pl.program_id(2)
            _acc(o_ref, p, i == 0)

            @pl.when(i == steps - 1)
            def _():
                ob_ref[...] = o_ref[...].astype(BF16)

    out = pl.BlockSpec((1, tk, n), lambda kk, j, i: (j, kk, 0))
    return _call(
        body, name=name, grid=(k // tk, s, steps),
        in_specs=[pl.BlockSpec((tm, tk), lambda kk, j, i: (i, kk)), pl.BlockSpec((tm, n), lambda kk, j, i: (i, j))],
        out_specs=[out, out], out_shape=[_sds((s, k, n)), _sds((s, k, n), BF16)])(a, dy)


def _rms(x):
    return lax.rsqrt(jnp.mean(x * x, axis=-1, keepdims=True) + RMS_EPS)


def _rms_bwd(dxh, xh, r):
    return r * (dxh - xh * jnp.mean(dxh * xh, axis=-1, keepdims=True))


def _norm_mod(x, g, scale, shift, name, tr=512):
    t = x.shape[0]

    def body(x_ref, g_ref, sc_ref, sh_ref, h_ref):
        xv = x_ref[...]
        n = xv * _rms(xv) * g_ref[...]
        h_ref[...] = (n * (1.0 + sc_ref[...]) + sh_ref[...]).astype(BF16)

    return _call(body, name=name, grid=(t // tr,),
                 in_specs=[_rows(tr, D_MODEL), _vec(D_MODEL), _vec(D_MODEL), _vec(D_MODEL)],
                 out_specs=_rows(tr, D_MODEL), out_shape=_sds((t, D_MODEL), BF16))(x, g, scale, shift)


def _mix_in(o_hg, proj, att, og, ag, name, tr=256):
    t = o_hg.shape[0]

    def body(o_ref, g_ref, a_ref, og_ref, ag_ref, m_ref):
        for h in range(HG_WIDTH // HG_HEAD):
            sl = slice(h * HG_HEAD, (h + 1) * HG_HEAD)
            oh = o_ref[:, sl]
            gv = g_ref[:, sl].astype(F32)
            m_ref[:, sl] = (oh * _rms(oh) * og_ref[...] * (gv * _sigmoid(gv))).astype(BF16)
        av = a_ref[...]
        m_ref[:, HG_WIDTH:] = (av * _rms(av) * ag_ref[...]).astype(BF16)

    return _call(body, name=name, grid=(t // tr,),
                 in_specs=[_rows(tr, HG_WIDTH), pl.BlockSpec((tr, HG_WIDTH), lambda i: (i, 3)), _rows(tr, ATT_WIDTH),
                           _vec(HG_HEAD), _vec(ATT_WIDTH)],
                 out_specs=_rows(tr, D_MODEL), out_shape=_sds((t, D_MODEL), BF16))(o_hg, proj, att, og, ag)


def _resid_norm_mod(x, mix, gate, g, scale, shift, name, tr=256):
    t = x.shape[0]

    def body(x_ref, m_ref, gt_ref, g_ref, sc_ref, sh_ref, x2_ref, h_ref):
        x2 = x_ref[...] + gt_ref[...] * m_ref[...]
        x2_ref[...] = x2
        n = x2 * _rms(x2) * g_ref[...]
        h_ref[...] = (n * (1.0 + sc_ref[...]) + sh_ref[...]).astype(BF16)

    return _call(body, name=name, grid=(t // tr,),
                 in_specs=[_rows(tr, D_MODEL), _rows(tr, D_MODEL)] + [_vec(D_MODEL)] * 4,
                 out_specs=[_rows(tr, D_MODEL), _rows(tr, D_MODEL)],
                 out_shape=[_sds((t, D_MODEL)), _sds((t, D_MODEL), BF16)])(x, mix, gate, g, scale, shift)


def _mm_gate_up(h, w_gu, name, tm=1024):
    m, k = h.shape
    n = w_gu.shape[2]

    def body(h_ref, wa_ref, wu_ref, da_ref, du_ref, o_ref):
        hv = h_ref[...]
        a = _dot(hv, wa_ref[0])
        u = _dot(hv, wu_ref[0])
        sg = _sigmoid(a)
        silu = a * sg
        da_ref[...] = (u * sg * (1.0 + a * (1.0 - sg))).astype(BF16)
        du_ref[...] = silu.astype(BF16)
        o_ref[...] = (silu * u).astype(BF16)

    out = pl.BlockSpec((tm, n), lambda j, i: (i, j))
    return _call(body, name=name, grid=(2, m // tm),
                 in_specs=[pl.BlockSpec((tm, k), lambda j, i: (i, 0)), pl.BlockSpec((1, k, n), lambda j, i: (j, 0, 0)),
                           pl.BlockSpec((1, k, n), lambda j, i: (j + 2, 0, 0))],
                 out_specs=[out, out, out], out_shape=[_sds((m, 2 * n), BF16)] * 3)(h, w_gu, w_gu)


def _mm_down_dx(dffn, w_down, act_da, act_du, name, tm=512):
    m = dffn.shape[0]
    _, k, n = w_down.shape

    def body(d_ref, w_ref, da_ref, du_ref, o_ref):
        dact = _dot_nt(d_ref[...], w_ref[0])
        o_ref[:, :k] = (dact * da_ref[...].astype(F32)).astype(BF16)
        o_ref[:, k:] = (dact * du_ref[...].astype(F32)).astype(BF16)

    return _call(body, name=name, grid=(m // tm,),
                 in_specs=[_rows(tm, n), pl.BlockSpec((1, k, n), lambda i: (0, 0, 0)), _rows(tm, k), _rows(tm, k)],
                 out_specs=_rows(tm, 2 * k), out_shape=_sds((m, 2 * k), BF16))(dffn, w_down, act_da, act_du)


def _final_loss(x2, ffn, gate, fg, tgt, name, tr=256):
    t = x2.shape[0]

    def body(x_ref, f_ref, gt_ref, fg_ref, t_ref, dx_ref, df_ref, l_ref, dfg_ref, dgt_ref):
        first = pl.program_id(0) == 0
        ffn_v = f_ref[...]
        x3 = x_ref[...] + gt_ref[...] * ffn_v
        r = _rms(x3)
        xh = x3 * r
        err = xh * fg_ref[...] - t_ref[...]
        dy = err * (1.0 / D_MODEL)
        dx3 = _rms_bwd(dy * fg_ref[...], xh, r)
        dx_ref[...] = dx3
        df_ref[...] = (dx3 * gt_ref[...]).astype(BF16)
        _acc(l_ref, jnp.sum(err * err, axis=0, keepdims=True), first)
        _acc(dfg_ref, jnp.sum(dy * xh, axis=0, keepdims=True), first)
        _acc(dgt_ref, jnp.sum(dx3 * ffn_v, axis=0, keepdims=True), first)

    row, vec = _rows(tr, D_MODEL), _vec(D_MODEL)
    return _call(body, name=name, grid=(t // tr,), in_specs=[row, row, vec, vec, row],
                 out_specs=[row, row, vec, vec, vec],
                 out_shape=[_sds((t, D_MODEL)), _sds((t, D_MODEL), BF16)] + [_sds((1, D_MODEL))] * 3)(
                     x2, ffn, gate, fg, tgt)


def _norm_mod_bwd(dh, x, g, scale, dres, name, gate=None, mix=None, tr=256):
    t = x.shape[0]
    below = gate is not None

    def body(*refs):
        if below:
            dh_ref, x_ref, g_ref, sc_ref, dr_ref, gt_ref, m_ref, dx_ref, dsh_ref, dsc_ref, dg_ref, dgt_ref, dm_ref = refs
        else:
            dh_ref, x_ref, g_ref, sc_ref, dr_ref, dx_ref, dsh_ref, dsc_ref, dg_ref = refs
        first = pl.program_id(0) == 0
        xv = x_ref[...]
        dhv = dh_ref[...]
        r = _rms(xv)
        xh = xv * r
        dn = dhv * (1.0 + sc_ref[...])
        dx = dr_ref[...] + _rms_bwd(dn * g_ref[...], xh, r)
        dx_ref[...] = dx
        _acc(dsh_ref, jnp.sum(dhv, axis=0, keepdims=True), first)
        _acc(dsc_ref, jnp.sum(dhv * xh * g_ref[...], axis=0, keepdims=True), first)
        _acc(dg_ref, jnp.sum(dn * xh, axis=0, keepdims=True), first)
        if below:
            _acc(dgt_ref, jnp.sum(dx * m_ref[...], axis=0, keepdims=True), first)
            dm_ref[...] = (dx * gt_ref[...]).astype(BF16)

    row, vec = _rows(tr, D_MODEL), _vec(D_MODEL)
    in_specs = [row, row, vec, vec, row] + ([vec, row] if below else [])
    out_specs = [row, vec, vec, vec] + ([vec, row] if below else [])
    out_shape = [_sds((t, D_MODEL))] + [_sds((1, D_MODEL))] * 3 + ([_sds((1, D_MODEL)), _sds((t, D_MODEL), BF16)] if below else [])
    args = (dh, x, g, scale, dres) + ((gate, mix) if below else ())
    return _call(body, name=name, grid=(t // tr,), in_specs=in_specs, out_specs=out_specs, out_shape=out_shape)(*args)


def _mix_in_bwd(dmi, o_hg, proj, att, og, ag, name, tr=256):
    t = o_hg.shape[0]

    def body(d_ref, o_ref, g_ref, a_ref, og_ref, ag_ref, do_ref, dg_ref, da_ref, dd_ref, dog_ref, dag_ref):
        first = pl.program_id(0) == 0
        dog = jnp.zeros((1, HG_HEAD), F32)
        for h in range(HG_WIDTH // HG_HEAD):
            sl = slice(h * HG_HEAD, (h + 1) * HG_HEAD)
            oh = o_ref[:, sl]
            gv = g_ref[:, sl].astype(F32)
            dv = d_ref[:, sl]
            r = _rms(oh)
            xh = oh * r
            sg = _sigmoid(gv)
            dno = dv * gv * sg
            dg_ref[:, sl] = dv * xh * og_ref[...] * sg * (1.0 + gv * (1.0 - sg))
            dog = dog + jnp.sum(dno * xh, axis=0, keepdims=True)
            do_ref[:, sl] = _rms_bwd(dno * og_ref[...], xh, r)
        _acc(dog_ref, dog, first)
        av = a_ref[...]
        dav = d_ref[:, HG_WIDTH:]
        r = _rms(av)
        xa = av * r
        _acc(dag_ref, jnp.sum(dav * xa, axis=0, keepdims=True), first)
        datt = _rms_bwd(dav * ag_ref[...], xa, r)
        da_ref[...] = datt
        prod = datt * av
        lane = lax.broadcasted_iota(jnp.int32, (1, 128), 1)
        dd = jnp.zeros((tr, 128), F32)
        for hp in range(ATT_HEADS // 2):
            pp = prod[:, hp * 128:(hp + 1) * 128]
            lo = jnp.sum(jnp.where(lane < 64, pp, 0.0), axis=-1, keepdims=True)
            hi = jnp.sum(jnp.where(lane >= 64, pp, 0.0), axis=-1, keepdims=True)
            dd = jnp.where(lane == 2 * hp, lo, dd)
            dd = jnp.where(lane == 2 * hp + 1, hi, dd)
        dd_ref[...] = dd

    half = _rows(tr, HG_WIDTH)
    return _call(body, name=name, grid=(t // tr,),
                 in_specs=[_rows(tr, D_MODEL), half, pl.BlockSpec((tr, HG_WIDTH), lambda i: (i, 3)), half,
                           _vec(HG_HEAD), _vec(ATT_WIDTH)],
                 out_specs=[half, half, half, _rows(tr, 128), _vec(HG_HEAD), _vec(ATT_WIDTH)],
                 out_shape=[_sds((t, HG_WIDTH))] * 3 + [_sds((t, 128)), _sds((1, HG_HEAD)), _sds((1, ATT_WIDTH))])(
                     dmi, o_hg, proj, att, og, ag)


def _dproj(dhg, dg, dqs, dkvs, name, tr=256):
    t = dhg.shape[0]
    w3 = 3 * HG_WIDTH
    w4 = w3 + HG_WIDTH
    nbr = len(dqs)

    def body(*refs):
        h_ref, g_ref, q_refs, kv_refs, o_ref = refs[0], refs[1], refs[2:2 + nbr], refs[2 + nbr:2 + 2 * nbr], refs[-1]
        o_ref[:, :w3] = h_ref[...]
        o_ref[:, w3:w4] = g_ref[...].astype(BF16)
        o_ref[:, w4:w4 + ATT_WIDTH] = sum(r[...].astype(F32) for r in q_refs).astype(BF16)
        o_ref[:, w4 + ATT_WIDTH:] = sum(r[...].astype(F32) for r in kv_refs).astype(BF16)

    return _call(body, name=name, grid=(t // tr,),
                 in_specs=[_rows(tr, w3), _rows(tr, HG_WIDTH)] + [_rows(tr, ATT_WIDTH)] * nbr
                 + [_rows(tr, 2 * ATT_WIDTH)] * nbr,
                 out_specs=_rows(tr, IN_WIDTH), out_shape=_sds((t, IN_WIDTH), BF16))(dhg, dg, *dqs, *dkvs)


def _chunk_tri(upper):
    row = lax.broadcasted_iota(jnp.int32, (HG_GROUP, HG_CHUNK, HG_CHUNK), 1)
    col = lax.broadcasted_iota(jnp.int32, (HG_GROUP, HG_CHUNK, HG_CHUNK), 2)
    return (row <= col if upper else row >= col).astype(BF16)


def _chunk_cumsum(x, tri):
    x3 = x.reshape(HG_GROUP, HG_CHUNK, x.shape[1])
    dims = (((2,), (1,)), ((0,), (0,)))
    out = None
    for _ in range(3):
        part = x3.astype(BF16)
        x3 = x3 - part.astype(F32)
        term = lax.dot_general(tri, part, dims, preferred_element_type=F32)
        out = term if out is None else out + term
    return out.reshape(x.shape)


def _hg_gates(f_raw, q_raw, lb, tri):
    sg = _sigmoid(f_raw)
    f = lb + (1.0 - lb) * sg
    k = 1.0 - f
    b = _chunk_cumsum(jnp.log(f), tri)
    sq = _sigmoid(q_raw)
    return sg, f, k, b, sq


def _hg_masks(rows):
    row = lax.broadcasted_iota(jnp.int32, (rows, rows), 0)
    col = lax.broadcasted_iota(jnp.int32, (rows, rows), 1)
    same = (row // HG_CHUNK) == (col // HG_CHUNK)
    return jnp.logical_and(row >= col, same), jnp.logical_and(row <= col, same)


def _per_chunk(rows_of):
    return jnp.concatenate([jnp.broadcast_to(r, (HG_CHUNK, r.shape[1])) for r in rows_of], axis=0)


def _hgrn_fwd(proj, lb_logits, name):
    t = proj.shape[0]
    nc = t // HG_CHUNK
    nh = HG_WIDTH // HG_HEAD
    rows = HG_GROUP * HG_CHUNK

    def body(q_ref, f_ref, i_ref, lg_ref, o_ref, st_ref, s_scr):
        @pl.when(pl.program_id(0) == 0)
        def _():
            s_scr[...] = jnp.zeros_like(s_scr)

        lg = lg_ref[...]
        lb_all = _sigmoid(lg[0:1] - lg[1:2])
        causal, _ = _hg_masks(rows)
        tri = _chunk_tri(False)
        for h in range(nh):
            sl = slice(h * HG_HEAD, (h + 1) * HG_HEAD)
            q_raw = q_ref[:, sl].astype(F32)
            _, _, k, b, sq = _hg_gates(f_ref[:, sl].astype(F32), q_raw, lb_all[:, sl], tri)
            v = i_ref[:, sl].astype(BF16)
            gls = [b[(g + 1) * HG_CHUNK - 1:(g + 1) * HG_CHUNK] for g in range(HG_GROUP)]
            qd = (q_raw * sq * jnp.exp(b)).astype(BF16)
            kd = (k * jnp.exp(-b)).astype(BF16)
            ke = (k * jnp.exp(_per_chunk(gls) - b)).astype(BF16)
            a = jnp.where(causal, _dot_nt(qd, kd), 0.0).astype(BF16)
            o_intra = _dot(a, v)
            st = s_scr[h]
            o_inter = []
            for g in range(HG_GROUP):
                rs = slice(g * HG_CHUNK, (g + 1) * HG_CHUNK)
                st_ref[g, sl, :] = st
                o_inter.append(_dot_nt(qd[rs], st.astype(BF16)))
                st = st * jnp.exp(gls[g]) + _dot_tn(v[rs], ke[rs])
            s_scr[h] = st
            o_ref[:, sl] = o_intra + jnp.concatenate(o_inter, axis=0)

    blk = lambda j: pl.BlockSpec((rows, HG_WIDTH), lambda c: (c, j))
    return _call(body, name=name, grid=(nc // HG_GROUP,),
                 in_specs=[blk(0), blk(1), blk(2), pl.BlockSpec((2, HG_WIDTH), lambda c: (0, 0))],
                 out_specs=[blk(0), pl.BlockSpec((HG_GROUP, HG_WIDTH, HG_HEAD), lambda c: (c, 0, 0))],
                 out_shape=[_sds((t, HG_WIDTH)), _sds((nc, HG_WIDTH, HG_HEAD))],
                 scratch_shapes=[pltpu.VMEM((nh, HG_HEAD, HG_HEAD), F32)])(proj, proj, proj, lb_logits)


def _hgrn_bwd(proj, lb_logits, states, do, name):
    t = proj.shape[0]
    ng = t // (HG_GROUP * HG_CHUNK)
    nh = HG_WIDTH // HG_HEAD
    rows = HG_GROUP * HG_CHUNK

    def body(q_ref, f_ref, i_ref, lg_ref, st_ref, do_ref, d_ref, dlb_ref, ds_scr):
        first = pl.program_id(0) == 0

        @pl.when(first)
        def _():
            ds_scr[...] = jnp.zeros_like(ds_scr)

        lg = lg_ref[...]
        lb_all = _sigmoid(lg[0:1] - lg[1:2])
        causal, _ = _hg_masks(rows)
        tri = _chunk_tri(False)
        tri_t = _chunk_tri(True)
        dlb = []
        for h in range(nh):
            sl = slice(h * HG_HEAD, (h + 1) * HG_HEAD)
            q_raw = q_ref[:, sl].astype(F32)
            lb = lb_all[:, sl]
            sg, f, k, b, sq = _hg_gates(f_ref[:, sl].astype(F32), q_raw, lb, tri)
            v = i_ref[:, sl].astype(BF16)
            gls = [b[(g + 1) * HG_CHUNK - 1:(g + 1) * HG_CHUNK] for g in range(HG_GROUP)]
            eb = jnp.exp(b)
            enb = jnp.exp(-b)
            egb = jnp.exp(_per_chunk(gls) - b)
            ke = k * egb
            qd_b, kd_b, ke_b = (q_raw * sq * eb).astype(BF16), (k * enb).astype(BF16), ke.astype(BF16)
            dov = do_ref[:, sl].astype(BF16)
            a = jnp.where(causal, _dot_nt(qd_b, kd_b), 0.0).astype(BF16)
            da = jnp.where(causal, _dot_nt(dov, v), 0.0).astype(BF16)
            dkd = _dot_tn(da, qd_b)
            dst = ds_scr[h]
            dqd_s, dv_s, dke_s, dgl_s = [None] * HG_GROUP, [None] * HG_GROUP, [None] * HG_GROUP, [None] * HG_GROUP
            for g in reversed(range(HG_GROUP)):
                rs = slice(g * HG_CHUNK, (g + 1) * HG_CHUNK)
                st = st_ref[g, sl, :]
                dst_b = dst.astype(BF16)
                egl = jnp.exp(gls[g])
                dqd_s[g] = _dot(dov[rs], st.astype(BF16))
                dv_s[g] = _dot_nt(ke_b[rs], dst_b)
                dke_s[g] = _dot(v[rs], dst_b)
                dgl_s[g] = jnp.sum(dst * st, axis=0, keepdims=True) * egl
                dst = _dot_tn(dov[rs], qd_b[rs]) + dst * egl
            ds_scr[h] = dst
            dqd = _dot(da, kd_b) + jnp.concatenate(dqd_s, axis=0)
            dv = _dot_tn(a, dov) + jnp.concatenate(dv_s, axis=0)
            dke = jnp.concatenate(dke_s, axis=0)
            t1 = dke * ke
            db = dqd * qd_b.astype(F32) - dkd * kd_b.astype(F32) - t1
            dgl = _per_chunk([dgl_s[g] + jnp.sum(t1[g * HG_CHUNK:(g + 1) * HG_CHUNK], axis=0, keepdims=True)
                              for g in range(HG_GROUP)])
            dlf = _chunk_cumsum(db, tri_t) + dgl
            df = dlf / f - (dkd * enb + dke * egb)
            d_ref[:, sl] = (dqd * eb * sq * (1.0 + q_raw * (1.0 - sq))).astype(BF16)
            d_ref[:, HG_WIDTH + h * HG_HEAD:HG_WIDTH + (h + 1) * HG_HEAD] = (
                df * (1.0 - lb) * sg * (1.0 - sg)).astype(BF16)
            d_ref[:, 2 * HG_WIDTH + h * HG_HEAD:2 * HG_WIDTH + (h + 1) * HG_HEAD] = dv.astype(BF16)
            dlb.append(jnp.sum(df * (1.0 - sg), axis=0, keepdims=True))
        _acc(dlb_ref, jnp.concatenate(dlb, axis=1), first)

    rev = lambda j: pl.BlockSpec((rows, HG_WIDTH), lambda c: (ng - 1 - c, j))
    return _call(body, name=name, grid=(ng,),
                 in_specs=[rev(0), rev(1), rev(2), pl.BlockSpec((2, HG_WIDTH), lambda c: (0, 0)),
                           pl.BlockSpec((HG_GROUP, HG_WIDTH, HG_HEAD), lambda c: (ng - 1 - c, 0, 0)), rev(0)],
                 out_specs=[pl.BlockSpec((rows, 3 * HG_WIDTH), lambda c: (ng - 1 - c, 0)), _vec(HG_WIDTH)],
                 out_shape=[_sds((t, 3 * HG_WIDTH), BF16), _sds((1, HG_WIDTH))],
                 scratch_shapes=[pltpu.VMEM((nh, HG_HEAD, HG_HEAD), F32)])(proj, proj, proj, lb_logits, states, do)


def _to_sub(a, dil):
    t, w = a.shape
    return a if dil == 1 else a.reshape(t // dil, dil, w).transpose(1, 0, 2).reshape(t, w)


def _from_sub(a, dil):
    t, w = a.shape
    return a if dil == 1 else a.reshape(dil, t // dil, w).transpose(1, 0, 2).reshape(t, w)


def _att_mask(has_prev):
    qi = lax.broadcasted_iota(jnp.int32, (2 * ATT_BLOCK, 2 * ATT_BLOCK), 0) % ATT_BLOCK
    kj = lax.broadcasted_iota(jnp.int32, (2 * ATT_BLOCK, 2 * ATT_BLOCK), 1)
    prev = jnp.logical_and(jnp.logical_and(kj < ATT_BLOCK, kj >= qi), has_prev)
    cur = jnp.logical_and(kj >= ATT_BLOCK, kj - ATT_BLOCK <= qi)
    return jnp.logical_or(prev, cur), lax.broadcasted_iota(jnp.int32, (1, 128), 1)


def _attn_fwd_block(q_ref, kc_ref, kp_ref, vc_ref, vp_ref, o_ref, l_ref, has_prev):
    mask, lane = _att_mask(has_prev)
    lo = lane < 64
    nq = ATT_BLOCK
    lse_all = jnp.zeros((nq, 128), F32)
    for hp in range(ATT_HEADS // 2):
        sl = slice(hp * 128, (hp + 1) * 128)
        q2 = q_ref[:, sl]
        zero = jnp.zeros_like(q2)
        qs = jnp.concatenate([jnp.where(lo, q2, zero), jnp.where(lo, zero, q2)], axis=0)
        kk = jnp.concatenate([kp_ref[:, sl], kc_ref[:, sl]], axis=0)
        vv = jnp.concatenate([vp_ref[:, sl], vc_ref[:, sl]], axis=0)
        s = jnp.where(mask, _dot_nt(qs, kk) * 0.125, NEG)
        mx = jnp.max(s, axis=-1, keepdims=True)
        p = jnp.exp(s - mx)
        l = jnp.sum(p, axis=-1, keepdims=True)
        o = _dot(p.astype(BF16), vv) * (1.0 / l)
        o_ref[:, sl] = jnp.where(lo, o[:nq], o[nq:])
        lse = mx + jnp.log(l)
        lse_all = jnp.where(lane == 2 * hp, lse[:nq], lse_all)
        lse_all = jnp.where(lane == 2 * hp + 1, lse[nq:], lse_all)
    l_ref[...] = lse_all


def _attn_fwd(qkvs, dils, name):
    t = qkvs[0].shape[0]
    nb = t // ATT_BLOCK
    nbr = len(dils)

    def body(*refs):
        n = pl.program_id(0)
        for i, dil in enumerate(dils):
            _attn_fwd_block(*refs[5 * i:5 * i + 5], *refs[5 * nbr + 2 * i:5 * nbr + 2 * i + 2],
                            (n % (nb // dil)) != 0)

    def blk(col, back):
        return pl.BlockSpec((ATT_BLOCK, ATT_WIDTH), lambda n: (jnp.maximum(n - back, 0), col))

    in_specs, args = [], []
    for qkv in qkvs:
        c0 = qkv.shape[1] // ATT_WIDTH - 3
        in_specs += [blk(c0, 0), blk(c0 + 1, 0), blk(c0 + 1, 1), blk(c0 + 2, 0), blk(c0 + 2, 1)]
        args += [qkv] * 5
    out = _call(body, name=name, grid=(nb,), in_specs=in_specs,
                out_specs=[pl.BlockSpec((ATT_BLOCK, ATT_WIDTH), lambda n: (n, 0)),
                           pl.BlockSpec((ATT_BLOCK, 128), lambda n: (n, 0))] * nbr,
                out_shape=[_sds((t, ATT_WIDTH)), _sds((t, 128))] * nbr)(*args)
    return [(out[2 * i], out[2 * i + 1]) for i in range(nbr)]


def _attn_combine(os_, ls_, name, tr=256):
    t = os_[0].shape[0]
    nbr = len(os_)

    def body(*refs):
        o_refs, l_refs, (a_ref, lt_ref) = refs[:nbr], refs[nbr:2 * nbr], refs[2 * nbr:]
        lane = lax.broadcasted_iota(jnp.int32, (1, 128), 1)
        ls = [r[...] for r in l_refs]
        mx = functools.reduce(jnp.maximum, ls)
        tot = mx + jnp.log(sum(jnp.exp(l - mx) for l in ls))
        lt_ref[...] = tot
        ws = [jnp.exp(l - tot) for l in ls]
        for hp in range(ATT_HEADS // 2):
            sl = slice(hp * 128, (hp + 1) * 128)
            acc = jnp.zeros((tr, 128), F32)
            for w, o_ref in zip(ws, o_refs):
                wf = jnp.where(lane < 64, w[:, 2 * hp:2 * hp + 1], w[:, 2 * hp + 1:2 * hp + 2])
                acc = acc + wf * o_ref[:, sl]
            a_ref[:, sl] = acc

    return _call(body, name=name, grid=(t // tr,),
                 in_specs=[_rows(tr, ATT_WIDTH)] * nbr + [_rows(tr, 128)] * nbr,
                 out_specs=[_rows(tr, ATT_WIDTH), _rows(tr, 128)],
                 out_shape=[_sds((t, ATT_WIDTH)), _sds((t, 128))])(*os_, *ls_)


def _attn_bwd_block(q_ref, kc_ref, kp_ref, vc_ref, vp_ref, do_ref, l_ref, d_ref, dq_ref, dkv_ref, carry, has_prev):
    w = ATT_WIDTH
    nq = ATT_BLOCK
    mask, lane = _att_mask(has_prev)
    lo = lane < 64
    for hp in range(ATT_HEADS // 2):
        sl = slice(hp * 128, (hp + 1) * 128)
        sv = slice(w + hp * 128, w + (hp + 1) * 128)
        q2, do2 = q_ref[:, sl], do_ref[:, sl]
        zero = jnp.zeros_like(q2)
        qs = jnp.concatenate([jnp.where(lo, q2, zero), jnp.where(lo, zero, q2)], axis=0)
        dos = jnp.concatenate([jnp.where(lo, do2, zero), jnp.where(lo, zero, do2)], axis=0)
        kk = jnp.concatenate([kp_ref[:, sl], kc_ref[:, sl]], axis=0)
        vv = jnp.concatenate([vp_ref[:, sl], vc_ref[:, sl]], axis=0)
        ls = jnp.concatenate([l_ref[:, 2 * hp:2 * hp + 1], l_ref[:, 2 * hp + 1:2 * hp + 2]], axis=0)
        dh = jnp.concatenate([d_ref[:, 2 * hp:2 * hp + 1], d_ref[:, 2 * hp + 1:2 * hp + 2]], axis=0)
        p = jnp.exp(jnp.where(mask, _dot_nt(qs, kk) * 0.125 - ls, NEG))
        ds = (p * (_dot_nt(dos, vv) - dh)).astype(BF16)
        dq = _dot(ds, kk) * 0.125
        dq_ref[:, sl] = jnp.where(lo, dq[:nq], dq[nq:]).astype(BF16)
        dk = _dot_tn(ds, qs) * 0.125
        dv = _dot_tn(p.astype(BF16), dos)
        dkv_ref[:, sl] = (carry[:, sl] + dk[:nq]).astype(BF16)
        dkv_ref[:, sv] = (carry[:, sv] + dv[:nq]).astype(BF16)
        carry[:, sl] = dk[nq:]
        carry[:, sv] = dv[nq:]


def _attn_bwd(qkvs, douts, lses, dds, dils, name):
    t = qkvs[0].shape[0]
    nb = t // ATT_BLOCK
    nbr = len(dils)
    w = ATT_WIDTH
    nq = ATT_BLOCK

    def body(*refs):
        ins, outs, carries = refs[:8 * nbr], refs[8 * nbr:10 * nbr], refs[10 * nbr:]
        n = pl.program_id(0)

        @pl.when(n == 0)
        def _():
            for carry in carries:
                carry[...] = jnp.zeros_like(carry)

        @pl.when(n < nb)
        def _():
            for i, dil in enumerate(dils):
                _attn_bwd_block(*ins[8 * i:8 * i + 8], *outs[2 * i:2 * i + 2], carries[i], (n % (nb // dil)) != 0)

        @pl.when(n == nb)
        def _():
            for i in range(nbr):
                outs[2 * i + 1][...] = carries[i][...].astype(BF16)

    def blk(width, j, back):
        return pl.BlockSpec((nq, width), lambda n: (jnp.clip(n - back, 0, nb - 1), j))

    in_specs, args = [], []
    for qkv, dout, lse, dd in zip(qkvs, douts, lses, dds):
        c0 = qkv.shape[1] // w - 3
        in_specs += [blk(w, c0, 0), blk(w, c0 + 1, 0), blk(w, c0 + 1, 1), blk(w, c0 + 2, 0), blk(w, c0 + 2, 1),
                     blk(w, 0, 0), blk(128, 0, 0), blk(128, 0, 0)]
        args += [qkv] * 5 + [dout, lse, dd]
    out = _call(body, name=name, grid=(nb + 1,), in_specs=in_specs,
                out_specs=[blk(w, 0, 0), blk(2 * w, 0, 1)] * nbr,
                out_shape=[_sds((t, w), BF16), _sds((t, 2 * w), BF16)] * nbr,
                scratch_shapes=[pltpu.VMEM((nq, 2 * w), F32)] * nbr)(*args)
    return [(out[2 * i], out[2 * i + 1]) for i in range(nbr)]


def _local_step(x, tgt, mod, norm1_g, lb_logits, og, ag, norm2_g, fg, get_w, put_g, late=lambda a: a):
    shift1, scale1, gate1, shift2, scale2, gate2 = [mod[:, i * D_MODEL:(i + 1) * D_MODEL] for i in range(6)]
    fg = fg.reshape(1, D_MODEL)

    h1 = _norm_mod(x, norm1_g, scale1, shift1, "norm_mod1")
    w_in = get_w("w_in", h1)
    proj = _mm_nn(h1, w_in, "mm_in", out_dtype=BF16)
    o_hg, states = _hgrn_fwd(proj, lb_logits, "hgrn_fwd")
    qkvs = [proj if d == 1 else _to_sub(proj[:, 4 * HG_WIDTH:], d) for d in DILATIONS]
    outs = _attn_fwd(qkvs, DILATIONS, "attn_fwd")
    att, lse = _attn_combine([_from_sub(o, d) for (o, _), d in zip(outs, DILATIONS)],
                             [_from_sub(l, d) for (_, l), d in zip(outs, DILATIONS)], "attn_combine")
    mixin = _mix_in(o_hg, proj, att, og, ag, "mix_in")
    w_out = get_w("w_out", mixin)
    mix = _mm_nn(mixin, w_out, "mm_out")
    x2, h2 = _resid_norm_mod(x, mix, gate1, norm2_g, scale2, shift2, "resid_norm_mod2")
    w_gu = get_w("w_gu", h2)
    a_ff, u_ff, act = _mm_gate_up(h2, w_gu, "mm_gu")
    w_down = get_w("w_down", act)
    ffn = _mm_nn(act, w_down, "mm_down")
    dx3, dffn, loss_v, dfg, dgate2 = _final_loss(x2, ffn, gate2, fg, tgt, "final_loss")

    dffn = put_g("w_down", *_mm_tn(act, dffn, 1, "mm_down_dw", tm=2048, tk=D_FF // 2), dffn)
    dau = _mm_down_dx(dffn, w_down, a_ff, u_ff, "mm_down_dx")
    dau = put_g("w_gu", *_mm_tn(h2, dau, N_SHARD, "mm_gu_dw", tm=x.shape[0], tk=512), dau)
    dh2 = _mm_nt(dau, w_gu, "mm_gu_dx")
    dx2, dshift2, dscale2, dg2, dgate1, dmix = _norm_mod_bwd(
        dh2, x2, norm2_g, scale2, dx3, "norm_mod2_bwd", gate=gate1, mix=mix)
    dmix = put_g("w_out", *_mm_tn(mixin, dmix, 1, "mm_out_dw", tm=x.shape[0], tk=512), dmix)
    dmixin = _mm_nt(dmix, w_out, "mm_out_dx", tm=1024)
    do_hg, dg_raw, datt, dd, dog, dag = _mix_in_bwd(dmixin, o_hg, proj, att, og, ag, "mix_in_bwd")
    datt_b = datt.astype(BF16)
    datts = _attn_bwd(qkvs, [_to_sub(datt_b, d) for d in DILATIONS], [_to_sub(lse, d) for d in DILATIONS],
                      [_to_sub(dd, d) for d in DILATIONS], DILATIONS, "attn_bwd")
    dhg, dlb = _hgrn_bwd(proj, lb_logits, states, do_hg, "hgrn_bwd")
    dhg = late(dhg)
    dproj = _dproj(dhg, dg_raw, [_from_sub(dq, d) for (dq, _), d in zip(datts, DILATIONS)],
                   [_from_sub(dkv, d) for (_, dkv), d in zip(datts, DILATIONS)], "dproj")
    dproj = put_g("w_in", *_mm_tn(h1, dproj, N_SHARD, "mm_in_dw", tm=x.shape[0], tk=512), dproj)
    dh1 = _mm_nt(dproj, w_in, "mm_in_dx", tm=1024)
    dx, dshift1, dscale1, dg1 = _norm_mod_bwd(dh1, x, norm1_g, scale1, dx2, "norm_mod1_bwd")

    stats = jnp.concatenate([loss_v, dfg, dg2, dg1, dlb, dag, dog,
                             dshift1, dscale1, dgate1, dshift2, dscale2, dgate2], axis=1)
    return dx, stats


def _place():
    x, y, c = lax.axis_index("x"), lax.axis_index("y"), lax.axis_index("c")
    return x, y, c


def _chip_peers(x, y, c):
    return [(1 - x, y, c), (x, 1 - y, c), (1 - x, 1 - y, c)]


def _comm_call(body, name, n_in, out_shape, scratch_shapes):
    hbm = pl.BlockSpec(memory_space=pl.ANY)
    return pl.pallas_call(body, name=name, in_specs=[hbm] * n_in, out_specs=[hbm] * len(out_shape),
                          out_shape=out_shape, scratch_shapes=scratch_shapes)


_HBM = pl.BlockSpec(memory_space=pltpu.HBM)
_SEM = pl.BlockSpec(memory_space=pltpu.SEMAPHORE)
_EFFECT = pltpu.SideEffectType.DATAFLOW_SIDE_EFFECTING


def _exchange_copy(bufs, send, recv, j, peer, place, kind):
    x, y, c = place
    target = peer
    if kind == "gather":
        src = dst = bufs[0].at[2 * x + y]
    elif kind == "scatter":
        src, dst = bufs[0].at[2 * peer[0] + peer[1]], bufs[1].at[j]
    else:
        half = bufs[0].shape[1] // 2
        rows = pl.ds(c * half, half)
        if kind == "half":
            src = dst = bufs[0].at[2 * x + y, rows]
        else:
            src = dst = bufs[0].at[2 * peer[0] + peer[1], rows]
            target = (x, y, 1 - c)
    return pltpu.make_async_remote_copy(src_ref=src, dst_ref=dst, send_sem=send.at[j], recv_sem=recv.at[j],
                                        device_id=target, device_id_type=MESH)


def _exchange_start(groups, after, kind, name):
    sizes = [len(g) for g in groups]
    flat = [b for g in groups for b in g]
    ng, nb = len(groups), len(flat)

    def body(*refs):
        bufs, sems = refs[:nb], refs[nb + 1:nb + 1 + 2 * ng]
        x, y, c = _place()
        for j, peer in enumerate(_chip_peers(x, y, c)):
            at = 0
            for i, size in enumerate(sizes):
                _exchange_copy(bufs[at:at + size], sems[2 * i], sems[2 * i + 1], j, peer, (x, y, c), kind).start()
                at += size

    any_space = pl.BlockSpec(memory_space=pl.ANY)
    out = pl.pallas_call(
        body, name=name, in_specs=[_HBM] * nb + [any_space],
        out_specs=[_SEM] * (2 * ng) + [_HBM] * nb + [any_space],
        out_shape=[pltpu.SemaphoreType.DMA((3,))] * (2 * ng) + [pltpu.HBM(b.shape, b.dtype) for b in flat]
        + [_sds(after.shape, after.dtype)],
        input_output_aliases={i: 2 * ng + i for i in range(nb + 1)},
        compiler_params=pltpu.CompilerParams(has_side_effects=_EFFECT),
    )(*[pltpu.with_memory_space_constraint(b, pltpu.HBM) for b in flat], after)
    started, at = [], 2 * ng
    for i, size in enumerate(sizes):
        started.append((out[2 * i], out[2 * i + 1], tuple(out[at:at + size])))
        at += size
    return started, out[-1]


def _exchange_wait(started, after, kind, name):
    send, recv, bufs = started
    nb = len(bufs)

    def body(*refs):
        x, y, c = _place()
        for j, peer in enumerate(_chip_peers(x, y, c)):
            cp = _exchange_copy(refs[:nb], refs[nb], refs[nb + 1], j, peer, (x, y, c), kind)
            cp.wait_send()
            cp.wait_recv()

    return pl.pallas_call(
        body, name=name, in_specs=[_HBM] * nb + [_SEM, _SEM, pl.BlockSpec(memory_space=pl.ANY)],
        out_specs=[_HBM] * nb, out_shape=[pltpu.HBM(b.shape, b.dtype) for b in bufs],
        input_output_aliases={i: i for i in range(nb)},
        compiler_params=pltpu.CompilerParams(has_side_effects=_EFFECT),
    )(*bufs, send, recv, after)


def _sibling_copies(v_refs, l_refs, send, recv):
    x, y, c = _place()
    return [pltpu.make_async_remote_copy(src_ref=v, dst_ref=l, send_sem=send.at[a], recv_sem=recv.at[a],
                                         device_id=(x, y, 1 - c), device_id_type=MESH)
            for a, (v, l) in enumerate(zip(v_refs, l_refs))]


def _sibling_start(vs, after, name):
    n = len(vs)
    lands = [lax.empty(v.shape, v.dtype) for v in vs]

    def body(*refs):
        for cp in _sibling_copies(refs[:n], refs[n:2 * n], refs[2 * n + 1], refs[2 * n + 2]):
            cp.start()

    any_space = pl.BlockSpec(memory_space=pl.ANY)
    out = pl.pallas_call(
        body, name=name, in_specs=[_HBM] * (2 * n) + [any_space],
        out_specs=[_SEM, _SEM] + [_HBM] * (2 * n) + [any_space],
        out_shape=[pltpu.SemaphoreType.DMA((n,))] * 2 + [pltpu.HBM(b.shape, b.dtype) for b in vs + lands]
        + [_sds(after.shape, after.dtype)],
        input_output_aliases={i: 2 + i for i in range(2 * n + 1)},
        compiler_params=pltpu.CompilerParams(has_side_effects=_EFFECT),
    )(*[pltpu.with_memory_space_constraint(b, pltpu.HBM) for b in vs + lands], after)
    return (out[0], out[1], tuple(out[2:2 + n]), tuple(out[2 + n:2 + 2 * n])), out[-1]


def _sibling_wait(started, after, name):
    send, recv, vs, lands = started
    n = len(vs)

    def body(*refs):
        for cp in _sibling_copies(refs[:n], refs[n:2 * n], refs[2 * n], refs[2 * n + 1]):
            cp.wait_send()
            cp.wait_recv()

    out = pl.pallas_call(
        body, name=name, in_specs=[_HBM] * (2 * n) + [_SEM, _SEM, pl.BlockSpec(memory_space=pl.ANY)],
        out_specs=[_HBM] * (2 * n), out_shape=[pltpu.HBM(b.shape, b.dtype) for b in vs + lands],
        input_output_aliases={i: i for i in range(2 * n)},
        compiler_params=pltpu.CompilerParams(has_side_effects=_EFFECT),
    )(*vs, *lands, send, recv, after)
    return out[:n], out[n:]


def _swap_sibling(vs, name):
    n = len(vs)

    def body(*refs):
        v_refs, o_refs, (send, recv) = refs[:n], refs[n:2 * n], refs[2 * n:]
        x, y, c = _place()
        cps = [pltpu.make_async_remote_copy(
            src_ref=v_refs[a], dst_ref=o_refs[a], send_sem=send.at[a], recv_sem=recv.at[a],
            device_id=(x, y, 1 - c), device_id_type=MESH) for a in range(n)]
        for cp in cps:
            cp.start()
        for cp in cps:
            cp.wait()

    return _comm_call(body, name, n, [_sds(v.shape, v.dtype) for v in vs],
                      [pltpu.SemaphoreType.DMA((n,)), pltpu.SemaphoreType.DMA((n,))])(*vs)


def _gather_rows(v, name):
    r, n = v.shape

    def body(v_ref, o_ref, send, recv, loc):
        x, y, c = _place()
        me = 4 * x + 2 * y + c
        own = pltpu.make_async_copy(v_ref, o_ref.at[me], loc)
        own.start()
        peers = []
        for k in range(1, 8):
            px = 1 - x if k & 4 else x
            py = 1 - y if k & 2 else y
            pc = 1 - c if k & 1 else c
            peers.append((px, py, pc))
        sends = []
        for k, peer in enumerate(peers):
            cp = pltpu.make_async_remote_copy(src_ref=v_ref, dst_ref=o_ref.at[me], send_sem=send.at[k],
                                              recv_sem=recv.at[k], device_id=peer, device_id_type=MESH)
            cp.start()
            sends.append(cp)
        for k, peer in enumerate(peers):
            pltpu.make_async_remote_copy(src_ref=v_ref, dst_ref=o_ref.at[4 * peer[0] + 2 * peer[1] + peer[2]],
                                         send_sem=send.at[k], recv_sem=recv.at[k], device_id=peer,
                                         device_id_type=MESH).wait_recv()
        for cp in sends:
            cp.wait_send()
        own.wait()

    vmem = pl.BlockSpec(memory_space=pltpu.VMEM)
    return pl.pallas_call(body, name=name, in_specs=[vmem], out_specs=vmem, out_shape=_sds((8, r, n), v.dtype),
                          scratch_shapes=[pltpu.SemaphoreType.DMA((7,)), pltpu.SemaphoreType.DMA((7,)),
                                          pltpu.SemaphoreType.DMA])(v)


def _cast_place(w, shard, name):
    r, c = w.shape
    tr = r // 4

    def body(s_ref, w_ref, o_ref):
        o_ref[0] = w_ref[...].astype(BF16)

    return pl.pallas_call(
        body, name=name, out_shape=_sds((N_SHARD, r, c), BF16),
        grid_spec=pltpu.PrefetchScalarGridSpec(
            num_scalar_prefetch=1, grid=(4,), in_specs=[pl.BlockSpec((tr, c), lambda i, s: (i, 0))],
            out_specs=pl.BlockSpec((1, tr, c), lambda i, s: (s[0], i, 0))),
        compiler_params=pltpu.CompilerParams(dimension_semantics=("arbitrary",)),
    )(shard.reshape(1).astype(jnp.int32), w)


def _mod_part(c_all, w_ada, b_ada, name):
    n = w_ada.shape[1]

    def body(c_ref, w_ref, b_ref, a_ref, p_ref):
        cv = c_ref[...]
        ca = cv * _sigmoid(cv)
        a_ref[...] = ca
        p_ref[...] = jnp.dot(ca, w_ref[...], precision=lax.Precision.HIGHEST, preferred_element_type=F32) + b_ref[...]

    full = lambda a: pl.BlockSpec(a.shape, lambda i: (0, 0))
    return _call(body, name=name, grid=(1,), in_specs=[full(c_all), full(w_ada), full(b_ada)],
                 out_specs=[pl.BlockSpec((8, D_MODEL), lambda i: (0, 0)), pl.BlockSpec((8, n), lambda i: (0, 0))],
                 out_shape=[_sds((8, D_MODEL)), _sds((8, n))])(c_all, w_ada, b_ada)


def _sum_received(g, shard, land, name):
    _, r, c = g.shape
    tr = r // 4

    def body(s_ref, g_ref, l_ref, o_ref):
        o_ref[...] = ((g_ref[0] + l_ref[0].astype(F32)) + l_ref[1].astype(F32)) + l_ref[2].astype(F32)

    return pl.pallas_call(
        body, name=name, out_shape=_sds((r, c)),
        grid_spec=pltpu.PrefetchScalarGridSpec(
            num_scalar_prefetch=1, grid=(4,),
            in_specs=[pl.BlockSpec((1, tr, c), lambda i, s: (s[0], i, 0)),
                      pl.BlockSpec((3, tr, c), lambda i, s: (0, i, 0))],
            out_specs=pl.BlockSpec((tr, c), lambda i, s: (i, 0))),
        compiler_params=pltpu.CompilerParams(dimension_semantics=("arbitrary",), vmem_limit_bytes=VMEM_LIMIT),
    )(shard.reshape(1).astype(jnp.int32), g, land)


def _outer_sum(ct, dm, name):
    k, n = ct.shape[0], dm.shape[1]
    tr = k // 4

    def body(c_ref, d_ref, o_ref):
        cv = c_ref[...]
        dv = d_ref[...]
        acc = cv[:, 0:1] * dv[0:1, :]
        for i in range(1, 8):
            acc = acc + cv[:, i:i + 1] * dv[i:i + 1, :]
        o_ref[...] = acc

    return _call(body, name=name, grid=(4,), in_specs=[_rows(tr, 8), pl.BlockSpec((8, n), lambda i: (0, 0))],
                 out_specs=_rows(tr, n), out_shape=_sds((k, n)))(ct, dm)


def _adamw_math(w, g, m, v):
    m_new = ADAM_B1 * m + (1.0 - ADAM_B1) * g
    v_new = ADAM_B2 * v + (1.0 - ADAM_B2) * (g * g)
    m_hat = m_new / (1.0 - ADAM_B1 ** ADAM_STEP)
    v_hat = v_new / (1.0 - ADAM_B2 ** ADAM_STEP)
    return -ADAM_LR * (m_hat / (jnp.sqrt(v_hat) + ADAM_EPS) + ADAM_WD * w), m_new, v_new


def _small_update(stats, smalls, name):
    offsets = [ST_DMOD, ST_DG1, ST_DLB, ST_DOG, ST_DAG, ST_DG2, ST_DFG]
    lb_index = 2

    def body(*refs):
        s_ref, ins, l_ref, outs = refs[0], refs[1:22], refs[22], refs[23:]
        tot = s_ref[0:1, :]
        for i in range(1, 8):
            tot = tot + s_ref[i:i + 1, :]
        l_ref[...] = jnp.zeros((1, 128), F32) + (0.5 / D_MODEL) * jnp.sum(tot[:, ST_LOSS:ST_LOSS + D_MODEL])
        for p, off in enumerate(offsets):
            w_ref, m_ref, v_ref = ins[3 * p:3 * p + 3]
            g_out, d_out, m_out, v_out = outs[4 * p:4 * p + 4]
            g = tot[:, off:off + w_ref.shape[1]]
            if p == lb_index:
                lg = w_ref[...]
                lb = _sigmoid(lg[0:1] - lg[1:2])
                g = g * lb * (1.0 - lb)
            for r in range(w_ref.shape[0]):
                rows = slice(r, r + 1)
                gr = g if r == 0 else -g
                delta, m_new, v_new = _adamw_math(w_ref[rows, :], gr, m_ref[rows, :], v_ref[rows, :])
                g_out[rows, :] = gr
                d_out[rows, :] = delta
                m_out[rows, :] = m_new
                v_out[rows, :] = v_new

    full = lambda a: pl.BlockSpec(a.shape, lambda i: (0, 0))
    flat = [a for t in smalls for a in t]
    return _call(body, name=name, grid=(1,),
                 in_specs=[full(stats)] + [full(a) for a in flat],
                 out_specs=[pl.BlockSpec((1, 128), lambda i: (0, 0))] + [full(t[0]) for t in smalls for _ in range(4)],
                 out_shape=[_sds((1, 128))] + [_sds(t[0].shape) for t in smalls for _ in range(4)])(stats, *flat)


def _adamw(w, gs, m, v, name, steps=4):
    r, c = w.shape
    tr = r // steps
    ng = len(gs)

    def body(*refs):
        w_ref, g_refs, (m_ref, v_ref, g_out, d_out, m_out, v_out) = refs[0], refs[1:1 + ng], refs[1 + ng:]
        g = g_refs[0][...]
        for g_ref in g_refs[1:]:
            g = g + g_ref[...]
        g_out[...] = g
        d_out[...], m_out[...], v_out[...] = _adamw_math(w_ref[...], g, m_ref[...], v_ref[...])

    row = _rows(tr, c)
    return _call(body, name=name, grid=(steps,), in_specs=[row] * (3 + ng), out_specs=[row] * 4,
                 out_shape=[_sds((r, c))] * 4)(w, *gs, m, v)


def kernel(x, c, w_ada, b_ada, norm1_g, w_in, hg_lb_logits, hg_onorm_g, att_onorm_g, w_out, norm2_g, w_gate_up, w_down, final_g, loss_target, m_w_ada, m_b_ada, m_norm1_g, m_w_in, m_hg_lb_logits, m_hg_onorm_g, m_att_onorm_g, m_w_out, m_norm2_g, m_w_gate_up, m_w_down, m_final_g, v_w_ada, v_b_ada, v_norm1_g, v_w_in, v_hg_lb_logits, v_hg_onorm_g, v_att_onorm_g, v_w_out, v_norm2_g, v_w_gate_up, v_w_down, v_final_g):
    ix, iy, ic = _place()
    shard = 2 * ix + iy
    sample = 4 * ix + 2 * iy + ic
    n_ada = w_ada.shape[2]

    shards = [w_in[0], w_out[0], w_gate_up[0], w_down[0]]
    names = ["w_in", "w_out", "w_gu", "w_down"]
    shapes = [(N_SHARD,) + w.shape for w in shards]
    placed = [(_cast_place(w, shard, "place_" + nm),) for w, nm in zip(shards, names)]

    c_all = _gather_rows(c, "gather_c").reshape(8, D_MODEL)
    b_part = lax.dynamic_slice(b_ada, (0, shard * n_ada), (1, n_ada))
    c_act, part = _mod_part(c_all, w_ada[0], b_part, "mod_part")
    parts = _gather_rows(part, "gather_mod")[::2]
    mod = lax.dynamic_index_in_dim(parts, sample, axis=1, keepdims=False).reshape(1, 6 * D_MODEL)
    (first,), mod = _exchange_start(placed[:1], mod, "half", "gather_start_w_in")
    gathering = {}

    def get_w(name, after):
        if name == "w_in":
            halves = _exchange_wait(first, after, "half", "gather_wait_w_in")
            (passing,), token = _exchange_start([tuple(halves)], mod, "forward", "forward_start_w_in")
            (full,) = _exchange_wait(passing, token, "forward", "forward_wait_w_in")
            rest, full = _exchange_start(placed[1:], full, "gather", "gather_start_rest")
            gathering.update(zip(names[1:], rest))
            return full
        (full,) = _exchange_wait(gathering[name], after, "gather", "gather_wait_" + name)
        return full if name == "w_gu" else full.reshape(1, -1, D_MODEL)

    scattering = {}

    def put_g(name, g, g_bf16, then):
        shape = shapes[names.index(name)]
        land = lax.empty((3,) + shape[1:], BF16)
        (started,), then = _exchange_start([(g_bf16.reshape(shape), land)], then, "scatter", "scatter_start_" + name)
        scattering[name] = (g.reshape(shape), started)
        return then

    def summed(name, after):
        g, started = scattering[name]
        _, land = _exchange_wait(started, after, "scatter", "scatter_wait_" + name)
        return _sum_received(g, shard, land, "sum_" + name)

    early = ["w_down", "w_gu", "w_out"]
    swapping = []

    def late(a):
        started, a = _sibling_start([summed(nm, a) for nm in early], a, "swap_start")
        swapping.append(started)
        return a

    dx, stats = _local_step(x[0], loss_target[0], mod, norm1_g, hg_lb_logits, hg_onorm_g, att_onorm_g,
                            norm2_g, final_g, get_w, put_g, late)

    stats_all = _gather_rows(stats, "gather_stats").reshape(8, ST_WIDTH)
    dmod = lax.dynamic_slice(stats_all, (0, ST_DMOD + shard * n_ada), (8, n_ada))
    g_ada = _outer_sum(c_act.T, dmod, "w_ada_grad")

    as_row = lambda a: a.reshape(1, -1) if a.ndim == 1 else a
    smalls = [tuple(as_row(a) for a in t) for t in [
        (b_ada, m_b_ada, v_b_ada), (norm1_g, m_norm1_g, v_norm1_g),
        (hg_lb_logits, m_hg_lb_logits, v_hg_lb_logits), (hg_onorm_g, m_hg_onorm_g, v_hg_onorm_g),
        (att_onorm_g, m_att_onorm_g, v_att_onorm_g), (norm2_g, m_norm2_g, v_norm2_g),
        (final_g, m_final_g, v_final_g)]]
    loss, *small_out = _small_update(stats_all, smalls, "small_update")
    shapes_out = [b_ada.shape, norm1_g.shape, hg_lb_logits.shape, hg_onorm_g.shape, att_onorm_g.shape,
                  norm2_g.shape, final_g.shape]
    sg, sd, sm, sv = [[small_out[4 * p + i].reshape(shapes_out[p]) for p in range(7)] for i in range(4)]

    ada = _adamw(w_ada[0], [g_ada], m_w_ada[0], v_w_ada[0], "adamw_w_ada")
    moments = [(m_w_in, v_w_in), (m_w_out, v_w_out), (m_w_gate_up, v_w_gate_up), (m_w_down, v_w_down)]

    def update(group, sums, other):
        return {nm: _adamw(shards[names.index(nm)], [s, o], moments[names.index(nm)][0][0],
                           moments[names.index(nm)][1][0], "adamw_" + nm) for nm, s, o in zip(group, sums, other)}

    sums, other = _sibling_wait(swapping[0], ada[1], "swap_wait")
    done = update(early, sums, other)
    sum_in = summed("w_in", done["w_out"][1])
    done.update(update(["w_in"], [sum_in], _swap_sibling([sum_in], "swap_sum_in")))
    big = [ada] + [done[nm] for nm in names]
    bg, bd, bm, bv = [[t[i][None] for t in big] for i in range(4)]

    def order(b, s):
        return [b[0], s[0], s[1], b[1], s[2], s[3], s[4], b[2], s[5], b[3], b[4], s[6]]

    return (loss[0, 0], dx[None], *order(bg, sg), *order(bd, sd), *order(bm, sm), *order(bv, sv))
```

```python
import functools

import jax
import jax.numpy as jnp
from jax import lax
from jax.experimental import pallas as pl
from jax.experimental.pallas import tpu as pltpu

F32 = jnp.float32
BF16 = jnp.bfloat16
MESH = pl.DeviceIdType.MESH

D_MODEL = 1024
HG_WIDTH = 512
HG_HEAD = 128
HG_CHUNK = 64
HG_GROUP = 4
ATT_WIDTH = 512
ATT_HEADS = 8
ATT_BLOCK = 128
DILATIONS = (1, 4, 16)
D_FF = 2816
IN_WIDTH = 3584
N_SHARD = 4
RMS_EPS = 1e-6
NEG = -1e30

ADAM_LR = 0.001
ADAM_B1 = 0.9
ADAM_B2 = 0.999
ADAM_EPS = 1e-08
ADAM_WD = 0.01
ADAM_STEP = 10

VMEM_LIMIT = 56 * 2**20

ST_LOSS, ST_DFG, ST_DG2, ST_DG1 = 0, 1024, 2048, 3072
ST_DLB, ST_DAG, ST_DOG, ST_DMOD = 4096, 4608, 5120, 5248
ST_WIDTH = 5248 + 6144
SP_BADA, SP_N1, SP_LB, SP_OG, SP_AG, SP_N2, SP_FG = 0, 6144, 7168, 8192, 8320, 8832, 9856
SP_WIDTH = 10880


def _call(body, *, name, grid, in_specs, out_specs, out_shape, scratch_shapes=()):
    return pl.pallas_call(
        body, name=name, grid=grid, in_specs=in_specs, out_specs=out_specs, out_shape=out_shape,
        scratch_shapes=list(scratch_shapes),
        compiler_params=pltpu.CompilerParams(
            dimension_semantics=("arbitrary",) * len(grid), vmem_limit_bytes=VMEM_LIMIT))


def _sds(shape, dtype=F32):
    return jax.ShapeDtypeStruct(shape, dtype)


def _dot(a, b):
    return jnp.dot(a, b, preferred_element_type=F32)


def _dot_nt(a, b):
    return lax.dot_general(a, b, (((1,), (1,)), ((), ())), preferred_element_type=F32)


def _dot_tn(a, b):
    return lax.dot_general(a, b, (((0,), (0,)), ((), ())), preferred_element_type=F32)


def _sigmoid(x):
    return 1.0 / (1.0 + jnp.exp(-x))


def _rows(tr, width):
    return pl.BlockSpec((tr, width), lambda i: (i, 0))


def _vec(width):
    return pl.BlockSpec((1, width), lambda i: (0, 0))


def _acc(ref, val, first):
    @pl.when(first)
    def _():
        ref[...] = val

    @pl.when(jnp.logical_not(first))
    def _():
        ref[...] += val


def _mm_nn(a, b3, name, tm=1024, out_dtype=F32):
    m, k = a.shape
    s, _, n = b3.shape

    def body(a_ref, b_ref, o_ref):
        o_ref[...] = _dot(a_ref[...], b_ref[0]).astype(out_dtype)

    return _call(
        body, name=name, grid=(s, m // tm),
        in_specs=[pl.BlockSpec((tm, k), lambda j, i: (i, 0)), pl.BlockSpec((1, k, n), lambda j, i: (j, 0, 0))],
        out_specs=pl.BlockSpec((tm, n), lambda j, i: (i, j)), out_shape=_sds((m, s * n), out_dtype))(a, b3)


def _mm_nt(dy, b3, name, tm=512):
    m = dy.shape[0]
    s, k, n = b3.shape

    def body(dy_ref, b_ref, o_ref):
        acc = _dot_nt(dy_ref[:, 0:n], b_ref[0])
        for j in range(1, s):
            acc = acc + _dot_nt(dy_ref[:, j * n:(j + 1) * n], b_ref[j])
        o_ref[...] = acc

    return _call(
        body, name=name, grid=(m // tm,),
        in_specs=[_rows(tm, s * n), pl.BlockSpec((s, k, n), lambda i: (0, 0, 0))],
        out_specs=_rows(tm, k), out_shape=_sds((m, k)))(dy, b3)


def _mm_tn(a, dy, s, name, tm, tk):
    m, k = a.shape
    n = dy.shape[1] // s
    steps = m // tm

    def body(a_ref, dy_ref, o_ref, ob_ref):
        p = _dot_tn(a_ref[...], dy_ref[...])[None]
        if steps == 1:
            o_ref[...] = p
            ob_ref[...] = p.astype(BF16)
        else:
            i = pl.program_id(2)
            _acc(o_ref, p, i == 0)

            @pl.when(i == steps - 1)
            def _():
                ob_ref[...] = o_ref[...].astype(BF16)

    out = pl.BlockSpec((1, tk, n), lambda kk, j, i: (j, kk, 0))
    return _call(
        body, name=name, grid=(k // tk, s, steps),
        in_specs=[pl.BlockSpec((tm, tk), lambda kk, j, i: (i, kk)), pl.BlockSpec((tm, n), lambda kk, j, i: (i, j))],
        out_specs=[out, out], out_shape=[_sds((s, k, n)), _sds((s, k, n), BF16)])(a, dy)


def _rms(x):
    return lax.rsqrt(jnp.mean(x * x, axis=-1, keepdims=True) + RMS_EPS)


def _rms_bwd(dxh, xh, r):
    return r * (dxh - xh * jnp.mean(dxh * xh, axis=-1, keepdims=True))


def _norm_mod(x, g, scale, shift, name, tr=512):
    t = x.shape[0]

    def body(x_ref, g_ref, sc_ref, sh_ref, h_ref):
        xv = x_ref[...]
        n = xv * _rms(xv) * g_ref[...]
        h_ref[...] = (n * (1.0 + sc_ref[...]) + sh_ref[...]).astype(BF16)

    return _call(body, name=name, grid=(t // tr,),
                 in_specs=[_rows(tr, D_MODEL), _vec(D_MODEL), _vec(D_MODEL), _vec(D_MODEL)],
                 out_specs=_rows(tr, D_MODEL), out_shape=_sds((t, D_MODEL), BF16))(x, g, scale, shift)


def _mix_in(o_hg, proj, att, og, ag, name, tr=256):
    t = o_hg.shape[0]

    def body(o_ref, g_ref, a_ref, og_ref, ag_ref, m_ref):
        for h in range(HG_WIDTH // HG_HEAD):
            sl = slice(h * HG_HEAD, (h + 1) * HG_HEAD)
            oh = o_ref[:, sl]
            gv = g_ref[:, sl].astype(F32)
            m_ref[:, sl] = (oh * _rms(oh) * og_ref[...] * (gv * _sigmoid(gv))).astype(BF16)
        av = a_ref[...]
        m_ref[:, HG_WIDTH:] = (av * _rms(av) * ag_ref[...]).astype(BF16)

    return _call(body, name=name, grid=(t // tr,),
                 in_specs=[_rows(tr, HG_WIDTH), pl.BlockSpec((tr, HG_WIDTH), lambda i: (i, 3)), _rows(tr, ATT_WIDTH),
                           _vec(HG_HEAD), _vec(ATT_WIDTH)],
                 out_specs=_rows(tr, D_MODEL), out_shape=_sds((t, D_MODEL), BF16))(o_hg, proj, att, og, ag)


def _resid_norm_mod(x, mix, gate, g, scale, shift, name, tr=256):
    t = x.shape[0]

    def body(x_ref, m_ref, gt_ref, g_ref, sc_ref, sh_ref, x2_ref, h_ref):
        x2 = x_ref[...] + gt_ref[...] * m_ref[...]
        x2_ref[...] = x2
        n = x2 * _rms(x2) * g_ref[...]
        h_ref[...] = (n * (1.0 + sc_ref[...]) + sh_ref[...]).astype(BF16)

    return _call(body, name=name, grid=(t // tr,),
                 in_specs=[_rows(tr, D_MODEL), _rows(tr, D_MODEL)] + [_vec(D_MODEL)] * 4,
                 out_specs=[_rows(tr, D_MODEL), _rows(tr, D_MODEL)],
                 out_shape=[_sds((t, D_MODEL)), _sds((t, D_MODEL), BF16)])(x, mix, gate, g, scale, shift)


def _mm_gate_up(h, w_gu, name, tm=1024):
    m, k = h.shape
    n = w_gu.shape[2]

    def body(h_ref, wa_ref, wu_ref, da_ref, du_ref, o_ref):
        hv = h_ref[...]
        a = _dot(hv, wa_ref[0])
        u = _dot(hv, wu_ref[0])
        sg = _sigmoid(a)
        silu = a * sg
        da_ref[...] = (u * sg * (1.0 + a * (1.0 - sg))).astype(BF16)
        du_ref[...] = silu.astype(BF16)
        o_ref[...] = (silu * u).astype(BF16)

    out = pl.BlockSpec((tm, n), lambda j, i: (i, j))
    return _call(body, name=name, grid=(2, m // tm),
                 in_specs=[pl.BlockSpec((tm, k), lambda j, i: (i, 0)), pl.BlockSpec((1, k, n), lambda j, i: (j, 0, 0)),
                           pl.BlockSpec((1, k, n), lambda j, i: (j + 2, 0, 0))],
                 out_specs=[out, out, out], out_shape=[_sds((m, 2 * n), BF16)] * 3)(h, w_gu, w_gu)


def _mm_down_dx(dffn, w_down, act_da, act_du, name, tm=512):
    m = dffn.shape[0]
    _, k, n = w_down.shape

    def body(d_ref, w_ref, da_ref, du_ref, o_ref):
        dact = _dot_nt(d_ref[...], w_ref[0])
        o_ref[:, :k] = (dact * da_ref[...].astype(F32)).astype(BF16)
        o_ref[:, k:] = (dact * du_ref[...].astype(F32)).astype(BF16)

    return _call(body, name=name, grid=(m // tm,),
                 in_specs=[_rows(tm, n), pl.BlockSpec((1, k, n), lambda i: (0, 0, 0)), _rows(tm, k), _rows(tm, k)],
                 out_specs=_rows(tm, 2 * k), out_shape=_sds((m, 2 * k), BF16))(dffn, w_down, act_da, act_du)


def _final_loss(x2, ffn, gate, fg, tgt, name, tr=256):
    t = x2.shape[0]

    def body(x_ref, f_ref, gt_ref, fg_ref, t_ref, dx_ref, df_ref, l_ref, dfg_ref, dgt_ref):
        first = pl.program_id(0) == 0
        ffn_v = f_ref[...]
        x3 = x_ref[...] + gt_ref[...] * ffn_v
        r = _rms(x3)
        xh = x3 * r
        err = xh * fg_ref[...] - t_ref[...]
        dy = err * (1.0 / D_MODEL)
        dx3 = _rms_bwd(dy * fg_ref[...], xh, r)
        dx_ref[...] = dx3
        df_ref[...] = (dx3 * gt_ref[...]).astype(BF16)
        _acc(l_ref, jnp.sum(err * err, axis=0, keepdims=True), first)
        _acc(dfg_ref, jnp.sum(dy * xh, axis=0, keepdims=True), first)
        _acc(dgt_ref, jnp.sum(dx3 * ffn_v, axis=0, keepdims=True), first)

    row, vec = _rows(tr, D_MODEL), _vec(D_MODEL)
    return _call(body, name=name, grid=(t // tr,), in_specs=[row, row, vec, vec, row],
                 out_specs=[row, row, vec, vec, vec],
                 out_shape=[_sds((t, D_MODEL)), _sds((t, D_MODEL), BF16)] + [_sds((1, D_MODEL))] * 3)(
                     x2, ffn, gate, fg, tgt)


def _norm_mod_bwd(dh, x, g, scale, dres, name, gate=None, mix=None, tr=256):
    t = x.shape[0]
    below = gate is not None

    def body(*refs):
        if below:
            dh_ref, x_ref, g_ref, sc_ref, dr_ref, gt_ref, m_ref, dx_ref, dsh_ref, dsc_ref, dg_ref, dgt_ref, dm_ref = refs
        else:
            dh_ref, x_ref, g_ref, sc_ref, dr_ref, dx_ref, dsh_ref, dsc_ref, dg_ref = refs
        first = pl.program_id(0) == 0
        xv = x_ref[...]
        dhv = dh_ref[...]
        r = _rms(xv)
        xh = xv * r
        dn = dhv * (1.0 + sc_ref[...])
        dx = dr_ref[...] + _rms_bwd(dn * g_ref[...], xh, r)
        dx_ref[...] = dx
        _acc(dsh_ref, jnp.sum(dhv, axis=0, keepdims=True), first)
        _acc(dsc_ref, jnp.sum(dhv * xh * g_ref[...], axis=0, keepdims=True), first)
        _acc(dg_ref, jnp.sum(dn * xh, axis=0, keepdims=True), first)
        if below:
            _acc(dgt_ref, jnp.sum(dx * m_ref[...], axis=0, keepdims=True), first)
            dm_ref[...] = (dx * gt_ref[...]).astype(BF16)

    row, vec = _rows(tr, D_MODEL), _vec(D_MODEL)
    in_specs = [row, row, vec, vec, row] + ([vec, row] if below else [])
    out_specs = [row, vec, vec, vec] + ([vec, row] if below else [])
    out_shape = [_sds((t, D_MODEL))] + [_sds((1, D_MODEL))] * 3 + ([_sds((1, D_MODEL)), _sds((t, D_MODEL), BF16)] if below else [])
    args = (dh, x, g, scale, dres) + ((gate, mix) if below else ())
    return _call(body, name=name, grid=(t // tr,), in_specs=in_specs, out_specs=out_specs, out_shape=out_shape)(*args)


def _mix_in_bwd(dmi, o_hg, proj, att, og, ag, name, tr=256):
    t = o_hg.shape[0]

    def body(d_ref, o_ref, g_ref, a_ref, og_ref, ag_ref, do_ref, dg_ref, da_ref, dd_ref, dog_ref, dag_ref):
        first = pl.program_id(0) == 0
        dog = jnp.zeros((1, HG_HEAD), F32)
        for h in range(HG_WIDTH // HG_HEAD):
            sl = slice(h * HG_HEAD, (h + 1) * HG_HEAD)
            oh = o_ref[:, sl]
            gv = g_ref[:, sl].astype(F32)
            dv = d_ref[:, sl]
            r = _rms(oh)
            xh = oh * r
            sg = _sigmoid(gv)
            dno = dv * gv * sg
            dg_ref[:, sl] = dv * xh * og_ref[...] * sg * (1.0 + gv * (1.0 - sg))
            dog = dog + jnp.sum(dno * xh, axis=0, keepdims=True)
            do_ref[:, sl] = _rms_bwd(dno * og_ref[...], xh, r)
        _acc(dog_ref, dog, first)
        av = a_ref[...]
        dav = d_ref[:, HG_WIDTH:]
        r = _rms(av)
        xa = av * r
        _acc(dag_ref, jnp.sum(dav * xa, axis=0, keepdims=True), first)
        datt = _rms_bwd(dav * ag_ref[...], xa, r)
        da_ref[...] = datt
        prod = datt * av
        lane = lax.broadcasted_iota(jnp.int32, (1, 128), 1)
        dd = jnp.zeros((tr, 128), F32)
        for hp in range(ATT_HEADS // 2):
            pp = prod[:, hp * 128:(hp + 1) * 128]
            lo = jnp.sum(jnp.where(lane < 64, pp, 0.0), axis=-1, keepdims=True)
            hi = jnp.sum(jnp.where(lane >= 64, pp, 0.0), axis=-1, keepdims=True)
            dd = jnp.where(lane == 2 * hp, lo, dd)
            dd = jnp.where(lane == 2 * hp + 1, hi, dd)
        dd_ref[...] = dd

    half = _rows(tr, HG_WIDTH)
    return _call(body, name=name, grid=(t // tr,),
                 in_specs=[_rows(tr, D_MODEL), half, pl.BlockSpec((tr, HG_WIDTH), lambda i: (i, 3)), half,
                           _vec(HG_HEAD), _vec(ATT_WIDTH)],
                 out_specs=[half, half, half, _rows(tr, 128), _vec(HG_HEAD), _vec(ATT_WIDTH)],
                 out_shape=[_sds((t, HG_WIDTH))] * 3 + [_sds((t, 128)), _sds((1, HG_HEAD)), _sds((1, ATT_WIDTH))])(
                     dmi, o_hg, proj, att, og, ag)


def _dproj(dhg, dg, dqs, dkvs, name, tr=256):
    t = dhg.shape[0]
    w3 = 3 * HG_WIDTH
    w4 = w3 + HG_WIDTH
    nbr = len(dqs)

    def body(*refs):
        h_ref, g_ref, q_refs, kv_refs, o_ref = refs[0], refs[1], refs[2:2 + nbr], refs[2 + nbr:2 + 2 * nbr], refs[-1]
        o_ref[:, :w3] = h_ref[...]
        o_ref[:, w3:w4] = g_ref[...].astype(BF16)
        o_ref[:, w4:w4 + ATT_WIDTH] = sum(r[...].astype(F32) for r in q_refs).astype(BF16)
        o_ref[:, w4 + ATT_WIDTH:] = sum(r[...].astype(F32) for r in kv_refs).astype(BF16)

    return _call(body, name=name, grid=(t // tr,),
                 in_specs=[_rows(tr, w3), _rows(tr, HG_WIDTH)] + [_rows(tr, ATT_WIDTH)] * nbr
                 + [_rows(tr, 2 * ATT_WIDTH)] * nbr,
                 out_specs=_rows(tr, IN_WIDTH), out_shape=_sds((t, IN_WIDTH), BF16))(dhg, dg, *dqs, *dkvs)


def _chunk_tri(upper):
    row = lax.broadcasted_iota(jnp.int32, (HG_GROUP, HG_CHUNK, HG_CHUNK), 1)
    col = lax.broadcasted_iota(jnp.int32, (HG_GROUP, HG_CHUNK, HG_CHUNK), 2)
    return (row <= col if upper else row >= col).astype(BF16)


def _chunk_cumsum(x, tri):
    x3 = x.reshape(HG_GROUP, HG_CHUNK, x.shape[1])
    dims = (((2,), (1,)), ((0,), (0,)))
    out = None
    for _ in range(3):
        part = x3.astype(BF16)
        x3 = x3 - part.astype(F32)
        term = lax.dot_general(tri, part, dims, preferred_element_type=F32)
        out = term if out is None else out + term
    return out.reshape(x.shape)


def _hg_gates(f_raw, q_raw, lb, tri):
    sg = _sigmoid(f_raw)
    f = lb + (1.0 - lb) * sg
    k = 1.0 - f
    b = _chunk_cumsum(jnp.log(f), tri)
    sq = _sigmoid(q_raw)
    return sg, f, k, b, sq


def _hg_masks(rows):
    row = lax.broadcasted_iota(jnp.int32, (rows, rows), 0)
    col = lax.broadcasted_iota(jnp.int32, (rows, rows), 1)
    same = (row // HG_CHUNK) == (col // HG_CHUNK)
    return jnp.logical_and(row >= col, same), jnp.logical_and(row <= col, same)


def _per_chunk(rows_of):
    return jnp.concatenate([jnp.broadcast_to(r, (HG_CHUNK, r.shape[1])) for r in rows_of], axis=0)


def _hgrn_fwd(proj, lb_logits, name):
    t = proj.shape[0]
    nc = t // HG_CHUNK
    nh = HG_WIDTH // HG_HEAD
    rows = HG_GROUP * HG_CHUNK

    def body(q_ref, f_ref, i_ref, lg_ref, o_ref, st_ref, s_scr):
        @pl.when(pl.program_id(0) == 0)
        def _():
            s_scr[...] = jnp.zeros_like(s_scr)

        lg = lg_ref[...]
        lb_all = _sigmoid(lg[0:1] - lg[1:2])
        causal, _ = _hg_masks(rows)
        tri = _chunk_tri(False)
        for h in range(nh):
            sl = slice(h * HG_HEAD, (h + 1) * HG_HEAD)
            q_raw = q_ref[:, sl].astype(F32)
            _, _, k, b, sq = _hg_gates(f_ref[:, sl].astype(F32), q_raw, lb_all[:, sl], tri)
            v = i_ref[:, sl].astype(BF16)
            gls = [b[(g + 1) * HG_CHUNK - 1:(g + 1) * HG_CHUNK] for g in range(HG_GROUP)]
            qd = (q_raw * sq * jnp.exp(b)).astype(BF16)
            kd = (k * jnp.exp(-b)).astype(BF16)
            ke = (k * jnp.exp(_per_chunk(gls) - b)).astype(BF16)
            a = jnp.where(causal, _dot_nt(qd, kd), 0.0).astype(BF16)
            o_intra = _dot(a, v)
            st = s_scr[h]
            o_inter = []
            for g in range(HG_GROUP):
                rs = slice(g * HG_CHUNK, (g + 1) * HG_CHUNK)
                st_ref[g, sl, :] = st
                o_inter.append(_dot_nt(qd[rs], st.astype(BF16)))
                st = st * jnp.exp(gls[g]) + _dot_tn(v[rs], ke[rs])
            s_scr[h] = st
            o_ref[:, sl] = o_intra + jnp.concatenate(o_inter, axis=0)

    blk = lambda j: pl.BlockSpec((rows, HG_WIDTH), lambda c: (c, j))
    return _call(body, name=name, grid=(nc // HG_GROUP,),
                 in_specs=[blk(0), blk(1), blk(2), pl.BlockSpec((2, HG_WIDTH), lambda c: (0, 0))],
                 out_specs=[blk(0), pl.BlockSpec((HG_GROUP, HG_WIDTH, HG_HEAD), lambda c: (c, 0, 0))],
                 out_shape=[_sds((t, HG_WIDTH)), _sds((nc, HG_WIDTH, HG_HEAD))],
                 scratch_shapes=[pltpu.VMEM((nh, HG_HEAD, HG_HEAD), F32)])(proj, proj, proj, lb_logits)


def _hgrn_bwd(proj, lb_logits, states, do, name):
    t = proj.shape[0]
    ng = t // (HG_GROUP * HG_CHUNK)
    nh = HG_WIDTH // HG_HEAD
    rows = HG_GROUP * HG_CHUNK

    def body(q_ref, f_ref, i_ref, lg_ref, st_ref, do_ref, d_ref, dlb_ref, ds_scr):
        first = pl.program_id(0) == 0

        @pl.when(first)
        def _():
            ds_scr[...] = jnp.zeros_like(ds_scr)

        lg = lg_ref[...]
        lb_all = _sigmoid(lg[0:1] - lg[1:2])
        causal, _ = _hg_masks(rows)
        tri = _chunk_tri(False)
        tri_t = _chunk_tri(True)
        dlb = []
        for h in range(nh):
            sl = slice(h * HG_HEAD, (h + 1) * HG_HEAD)
            q_raw = q_ref[:, sl].astype(F32)
            lb = lb_all[:, sl]
            sg, f, k, b, sq = _hg_gates(f_ref[:, sl].astype(F32), q_raw, lb, tri)
            v = i_ref[:, sl].astype(BF16)
            gls = [b[(g + 1) * HG_CHUNK - 1:(g + 1) * HG_CHUNK] for g in range(HG_GROUP)]
            eb = jnp.exp(b)
            enb = jnp.exp(-b)
            egb = jnp.exp(_per_chunk(gls) - b)
            ke = k * egb
            qd_b, kd_b, ke_b = (q_raw * sq * eb).astype(BF16), (k * enb).astype(BF16), ke.astype(BF16)
            dov = do_ref[:, sl].astype(BF16)
            a = jnp.where(causal, _dot_nt(qd_b, kd_b), 0.0).astype(BF16)
            da = jnp.where(causal, _dot_nt(dov, v), 0.0).astype(BF16)
            dkd = _dot_tn(da, qd_b)
            dst = ds_scr[h]
            dqd_s, dv_s, dke_s, dgl_s = [None] * HG_GROUP, [None] * HG_GROUP, [None] * HG_GROUP, [None] * HG_GROUP
            for g in reversed(range(HG_GROUP)):
                rs = slice(g * HG_CHUNK, (g + 1) * HG_CHUNK)
                st = st_ref[g, sl, :]
                dst_b = dst.astype(BF16)
                egl = jnp.exp(gls[g])
                dqd_s[g] = _dot(dov[rs], st.astype(BF16))
                dv_s[g] = _dot_nt(ke_b[rs], dst_b)
                dke_s[g] = _dot(v[rs], dst_b)
                dgl_s[g] = jnp.sum(dst * st, axis=0, keepdims=True) * egl
                dst = _dot_tn(dov[rs], qd_b[rs]) + dst * egl
            ds_scr[h] = dst
            dqd = _dot(da, kd_b) + jnp.concatenate(dqd_s, axis=0)
            dv = _dot_tn(a, dov) + jnp.concatenate(dv_s, axis=0)
            dke = jnp.concatenate(dke_s, axis=0)
            t1 = dke * ke
            db = dqd * qd_b.astype(F32) - dkd * kd_b.astype(F32) - t1
            dgl = _per_chunk([dgl_s[g] + jnp.sum(t1[g * HG_CHUNK:(g + 1) * HG_CHUNK], axis=0, keepdims=True)
                              for g in range(HG_GROUP)])
            dlf = _chunk_cumsum(db, tri_t) + dgl
            df = dlf / f - (dkd * enb + dke * egb)
            d_ref[:, sl] = (dqd * eb * sq * (1.0 + q_raw * (1.0 - sq))).astype(BF16)
            d_ref[:, HG_WIDTH + h * HG_HEAD:HG_WIDTH + (h + 1) * HG_HEAD] = (
                df * (1.0 - lb) * sg * (1.0 - sg)).astype(BF16)
            d_ref[:, 2 * HG_WIDTH + h * HG_HEAD:2 * HG_WIDTH + (h + 1) * HG_HEAD] = dv.astype(BF16)
            dlb.append(jnp.sum(df * (1.0 - sg), axis=0, keepdims=True))
        _acc(dlb_ref, jnp.concatenate(dlb, axis=1), first)

    rev = lambda j: pl.BlockSpec((rows, HG_WIDTH), lambda c: (ng - 1 - c, j))
    return _call(body, name=name, grid=(ng,),
                 in_specs=[rev(0), rev(1), rev(2), pl.BlockSpec((2, HG_WIDTH), lambda c: (0, 0)),
                           pl.BlockSpec((HG_GROUP, HG_WIDTH, HG_HEAD), lambda c: (ng - 1 - c, 0, 0)), rev(0)],
                 out_specs=[pl.BlockSpec((rows, 3 * HG_WIDTH), lambda c: (ng - 1 - c, 0)), _vec(HG_WIDTH)],
                 out_shape=[_sds((t, 3 * HG_WIDTH), BF16), _sds((1, HG_WIDTH))],
                 scratch_shapes=[pltpu.VMEM((nh, HG_HEAD, HG_HEAD), F32)])(proj, proj, proj, lb_logits, states, do)


def _to_sub(a, dil):
    t, w = a.shape
    return a if dil == 1 else a.reshape(t // dil, dil, w).transpose(1, 0, 2).reshape(t, w)


def _from_sub(a, dil):
    t, w = a.shape
    return a if dil == 1 else a.reshape(dil, t // dil, w).transpose(1, 0, 2).reshape(t, w)


def _att_mask(has_prev, seg):
    def place(v):
        v = v % ATT_BLOCK
        return v if seg == 1 else seg * (v % (ATT_BLOCK // seg)) + v // (ATT_BLOCK // seg)

    row = lax.broadcasted_iota(jnp.int32, (2 * ATT_BLOCK, 2 * ATT_BLOCK), 0)
    col = lax.broadcasted_iota(jnp.int32, (2 * ATT_BLOCK, 2 * ATT_BLOCK), 1)
    qi, kj = place(row), place(col)
    prev = jnp.logical_and(jnp.logical_and(col < ATT_BLOCK, kj >= qi), has_prev)
    cur = jnp.logical_and(col >= ATT_BLOCK, kj <= qi)
    return jnp.logical_or(prev, cur), lax.broadcasted_iota(jnp.int32, (1, 128), 1)


def _get(ref, sl):
    if len(ref.shape) == 2:
        return ref[:, sl]
    v = ref[:, :, sl]
    return v.reshape(ATT_BLOCK, v.shape[2])


def _put(ref, sl, val):
    if len(ref.shape) == 2:
        ref[:, sl] = val
    else:
        ref[:, :, sl] = val.reshape(ref.shape[0], ref.shape[1], val.shape[1])


def _att_spec(nb, dil, seg, width, col, back):
    bps = nb // dil

    def plain(n):
        return jnp.clip(n - back, 0, nb - 1), col

    def segmented(n):
        m = jnp.clip(n - back, 0, nb - 1)
        return 0, m // bps, m % bps, 0, col

    if seg == 1:
        return pl.BlockSpec((ATT_BLOCK, width), plain)
    return pl.BlockSpec((seg, None, None, ATT_BLOCK // seg, width), segmented)


def _att_shape(nb, dil, seg, width):
    t = nb * ATT_BLOCK
    return (t, width) if seg == 1 else (seg, dil, nb // dil, ATT_BLOCK // seg, width)


def _att_view(a, nb, dil, seg):
    return a.reshape(_att_shape(nb, dil, seg, a.shape[1]))


def _attn_fwd_block(q_ref, kc_ref, kp_ref, vc_ref, vp_ref, o_ref, l_ref, has_prev, seg):
    mask, lane = _att_mask(has_prev, seg)
    lo = lane < 64
    nq = ATT_BLOCK
    lse_all = jnp.zeros((nq, 128), F32)
    for hp in range(ATT_HEADS // 2):
        sl = slice(hp * 128, (hp + 1) * 128)
        q2 = _get(q_ref, sl)
        zero = jnp.zeros_like(q2)
        qs = jnp.concatenate([jnp.where(lo, q2, zero), jnp.where(lo, zero, q2)], axis=0)
        kk = jnp.concatenate([_get(kp_ref, sl), _get(kc_ref, sl)], axis=0)
        vv = jnp.concatenate([_get(vp_ref, sl), _get(vc_ref, sl)], axis=0)
        s = jnp.where(mask, _dot_nt(qs, kk) * 0.125, NEG)
        mx = jnp.max(s, axis=-1, keepdims=True)
        p = jnp.exp(s - mx)
        l = jnp.sum(p, axis=-1, keepdims=True)
        o = _dot(p.astype(BF16), vv) * (1.0 / l)
        _put(o_ref, sl, jnp.where(lo, o[:nq], o[nq:]))
        lse = mx + jnp.log(l)
        lse_all = jnp.where(lane == 2 * hp, lse[:nq], lse_all)
        lse_all = jnp.where(lane == 2 * hp + 1, lse[nq:], lse_all)
    _put(l_ref, slice(None), lse_all)


def _attn_fwd(branches, name):
    t = branches[0][0].shape[0]
    nb = t // ATT_BLOCK
    nbr = len(branches)

    def body(*refs):
        n = pl.program_id(0)
        for i, (_, dil, seg) in enumerate(branches):
            _attn_fwd_block(*refs[5 * i:5 * i + 5], *refs[5 * nbr + 2 * i:5 * nbr + 2 * i + 2],
                            (n % (nb // dil)) != 0, seg)

    in_specs, args, out_specs, out_shape = [], [], [], []
    for qkv, dil, seg in branches:
        c0 = qkv.shape[1] // ATT_WIDTH - 3
        in_specs += [_att_spec(nb, dil, seg, ATT_WIDTH, c0 + j, back) for j, back in [(0, 0), (1, 0), (1, 1), (2, 0), (2, 1)]]
        args += [_att_view(qkv, nb, dil, seg)] * 5
        out_specs += [_att_spec(nb, dil, seg, ATT_WIDTH, 0, 0), _att_spec(nb, dil, seg, 128, 0, 0)]
        out_shape += [_sds(_att_shape(nb, dil, seg, ATT_WIDTH)), _sds(_att_shape(nb, dil, seg, 128))]
    out = _call(body, name=name, grid=(nb,), in_specs=in_specs, out_specs=out_specs, out_shape=out_shape)(*args)
    return [(out[2 * i].reshape(t, ATT_WIDTH), out[2 * i + 1].reshape(t, 128)) for i in range(nbr)]


def _attn_combine(os_, ls_, name, tr=256):
    t = os_[0].shape[0]
    nbr = len(os_)

    def body(*refs):
        o_refs, l_refs, (a_ref, lt_ref) = refs[:nbr], refs[nbr:2 * nbr], refs[2 * nbr:]
        lane = lax.broadcasted_iota(jnp.int32, (1, 128), 1)
        ls = [r[...] for r in l_refs]
        mx = functools.reduce(jnp.maximum, ls)
        tot = mx + jnp.log(sum(jnp.exp(l - mx) for l in ls))
        lt_ref[...] = tot
        ws = [jnp.exp(l - tot) for l in ls]
        for hp in range(ATT_HEADS // 2):
            sl = slice(hp * 128, (hp + 1) * 128)
            acc = jnp.zeros((tr, 128), F32)
            for w, o_ref in zip(ws, o_refs):
                wf = jnp.where(lane < 64, w[:, 2 * hp:2 * hp + 1], w[:, 2 * hp + 1:2 * hp + 2])
                acc = acc + wf * o_ref[:, sl]
            a_ref[:, sl] = acc

    return _call(body, name=name, grid=(t // tr,),
                 in_specs=[_rows(tr, ATT_WIDTH)] * nbr + [_rows(tr, 128)] * nbr,
                 out_specs=[_rows(tr, ATT_WIDTH), _rows(tr, 128)],
                 out_shape=[_sds((t, ATT_WIDTH)), _sds((t, 128))])(*os_, *ls_)


def _attn_bwd_block(q_ref, kc_ref, kp_ref, vc_ref, vp_ref, do_ref, l_ref, d_ref, dq_ref, dkv_ref, carry, has_prev, seg):
    w = ATT_WIDTH
    nq = ATT_BLOCK
    mask, lane = _att_mask(has_prev, seg)
    lo = lane < 64
    lse, ddv = _get(l_ref, slice(None)), _get(d_ref, slice(None))
    for hp in range(ATT_HEADS // 2):
        sl = slice(hp * 128, (hp + 1) * 128)
        sv = slice(w + hp * 128, w + (hp + 1) * 128)
        q2, do2 = _get(q_ref, sl), _get(do_ref, sl)
        zero = jnp.zeros_like(q2)
        qs = jnp.concatenate([jnp.where(lo, q2, zero), jnp.where(lo, zero, q2)], axis=0)
        dos = jnp.concatenate([jnp.where(lo, do2, zero), jnp.where(lo, zero, do2)], axis=0)
        kk = jnp.concatenate([_get(kp_ref, sl), _get(kc_ref, sl)], axis=0)
        vv = jnp.concatenate([_get(vp_ref, sl), _get(vc_ref, sl)], axis=0)
        ls = jnp.concatenate([lse[:, 2 * hp:2 * hp + 1], lse[:, 2 * hp + 1:2 * hp + 2]], axis=0)
        dh = jnp.concatenate([ddv[:, 2 * hp:2 * hp + 1], ddv[:, 2 * hp + 1:2 * hp + 2]], axis=0)
        p = jnp.exp(jnp.where(mask, _dot_nt(qs, kk) * 0.125 - ls, NEG))
        ds = (p * (_dot_nt(dos, vv) - dh)).astype(BF16)
        dq = _dot(ds, kk) * 0.125
        _put(dq_ref, sl, jnp.where(lo, dq[:nq], dq[nq:]).astype(BF16))
        dk = _dot_tn(ds, qs) * 0.125
        dv = _dot_tn(p.astype(BF16), dos)
        _put(dkv_ref, sl, (carry[:, sl] + dk[:nq]).astype(BF16))
        _put(dkv_ref, sv, (carry[:, sv] + dv[:nq]).astype(BF16))
        carry[:, sl] = dk[nq:]
        carry[:, sv] = dv[nq:]


def _attn_bwd(branches, name):
    t = branches[0][0].shape[0]
    nb = t // ATT_BLOCK
    nbr = len(branches)
    w = ATT_WIDTH

    def body(*refs):
        ins, outs, carries = refs[:8 * nbr], refs[8 * nbr:10 * nbr], refs[10 * nbr:]
        n = pl.program_id(0)

        @pl.when(n == 0)
        def _():
            for carry in carries:
                carry[...] = jnp.zeros_like(carry)

        @pl.when(n < nb)
        def _():
            for i, branch in enumerate(branches):
                dil, seg = branch[4:]
                _attn_bwd_block(*ins[8 * i:8 * i + 8], *outs[2 * i:2 * i + 2], carries[i], (n % (nb // dil)) != 0, seg)

        @pl.when(n == nb)
        def _():
            for i in range(nbr):
                _put(outs[2 * i + 1], slice(None), carries[i][...].astype(BF16))

    in_specs, args, out_specs, out_shape = [], [], [], []
    for qkv, dout, lse, dd, dil, seg in branches:
        c0 = qkv.shape[1] // w - 3
        in_specs += [_att_spec(nb, dil, seg, w, c0 + j, back) for j, back in [(0, 0), (1, 0), (1, 1), (2, 0), (2, 1)]]
        in_specs += [_att_spec(nb, dil, seg, w, 0, 0), _att_spec(nb, dil, seg, 128, 0, 0), _att_spec(nb, dil, seg, 128, 0, 0)]
        args += [_att_view(a, nb, dil, seg) for a in [qkv] * 5 + [dout, lse, dd]]
        out_specs += [_att_spec(nb, dil, seg, w, 0, 0), _att_spec(nb, dil, seg, 2 * w, 0, 1)]
        out_shape += [_sds(_att_shape(nb, dil, seg, w), BF16), _sds(_att_shape(nb, dil, seg, 2 * w), BF16)]
    out = _call(body, name=name, grid=(nb + 1,), in_specs=in_specs, out_specs=out_specs, out_shape=out_shape,
                scratch_shapes=[pltpu.VMEM((ATT_BLOCK, 2 * w), F32)] * nbr)(*args)
    return [(out[2 * i].reshape(t, w), out[2 * i + 1].reshape(t, 2 * w)) for i in range(nbr)]


def _local_step(x, tgt, mod, norm1_g, lb_logits, og, ag, norm2_g, fg, get_w, put_g, late=lambda a: a):
    shift1, scale1, gate1, shift2, scale2, gate2 = [mod[:, i * D_MODEL:(i + 1) * D_MODEL] for i in range(6)]
    fg = fg.reshape(1, D_MODEL)

    h1 = _norm_mod(x, norm1_g, scale1, shift1, "norm_mod1")
    w_in = get_w("w_in", h1)
    proj = _mm_nn(h1, w_in, "mm_in", out_dtype=BF16)
    o_hg, states = _hgrn_fwd(proj, lb_logits, "hgrn_fwd")
    fine = DILATIONS[-1]
    layouts = [(d, 1 if d == 1 else fine // d) for d in DILATIONS]
    qkv_fine = _to_sub(proj[:, 4 * HG_WIDTH:], fine)
    qkvs = [proj if d == 1 else qkv_fine for d in DILATIONS]
    natural = lambda a, d: a if d == 1 else _from_sub(a, fine)
    outs = _attn_fwd([(q, d, seg) for q, (d, seg) in zip(qkvs, layouts)], "attn_fwd")
    att, lse = _attn_combine([natural(o, d) for (o, _), d in zip(outs, DILATIONS)],
                             [natural(l, d) for (_, l), d in zip(outs, DILATIONS)], "attn_combine")
    mixin = _mix_in(o_hg, proj, att, og, ag, "mix_in")
    w_out = get_w("w_out", mixin)
    mix = _mm_nn(mixin, w_out, "mm_out")
    x2, h2 = _resid_norm_mod(x, mix, gate1, norm2_g, scale2, shift2, "resid_norm_mod2")
    w_gu = get_w("w_gu", h2)
    a_ff, u_ff, act = _mm_gate_up(h2, w_gu, "mm_gu")
    w_down = get_w("w_down", act)
    ffn = _mm_nn(act, w_down, "mm_down")
    dx3, dffn, loss_v, dfg, dgate2 = _final_loss(x2, ffn, gate2, fg, tgt, "final_loss")

    dffn = put_g("w_down", *_mm_tn(act, dffn, 1, "mm_down_dw", tm=2048, tk=D_FF // 2), dffn)
    dau = _mm_down_dx(dffn, w_down, a_ff, u_ff, "mm_down_dx")
    dau = put_g("w_gu", *_mm_tn(h2, dau, N_SHARD, "mm_gu_dw", tm=x.shape[0], tk=512), dau)
    dh2 = _mm_nt(dau, w_gu, "mm_gu_dx")
    dx2, dshift2, dscale2, dg2, dgate1, dmix = _norm_mod_bwd(
        dh2, x2, norm2_g, scale2, dx3, "norm_mod2_bwd", gate=gate1, mix=mix)
    dmix = put_g("w_out", *_mm_tn(mixin, dmix, 1, "mm_out_dw", tm=x.shape[0], tk=512), dmix)
    dmixin = _mm_nt(dmix, w_out, "mm_out_dx", tm=1024)
    do_hg, dg_raw, datt, dd, dog, dag = _mix_in_bwd(dmixin, o_hg, proj, att, og, ag, "mix_in_bwd")
    datt_b = datt.astype(BF16)
    reordered = [_to_sub(a, fine) for a in (datt_b, lse, dd)]
    datts = _attn_bwd([(q,) + tuple((datt_b, lse, dd) if d == 1 else reordered) + (d, seg)
                       for q, (d, seg) in zip(qkvs, layouts)], "attn_bwd")
    dhg, dlb = _hgrn_bwd(proj, lb_logits, states, do_hg, "hgrn_bwd")
    dhg = late(dhg)
    dproj = _dproj(dhg, dg_raw, [natural(dq, d) for (dq, _), d in zip(datts, DILATIONS)],
                   [natural(dkv, d) for (_, dkv), d in zip(datts, DILATIONS)], "dproj")
    dproj = put_g("w_in", *_mm_tn(h1, dproj, N_SHARD, "mm_in_dw", tm=x.shape[0], tk=512), dproj)
    dh1 = _mm_nt(dproj, w_in, "mm_in_dx", tm=1024)
    dx, dshift1, dscale1, dg1 = _norm_mod_bwd(dh1, x, norm1_g, scale1, dx2, "norm_mod1_bwd")

    stats = jnp.concatenate([loss_v, dfg, dg2, dg1, dlb, dag, dog,
                             dshift1, dscale1, dgate1, dshift2, dscale2, dgate2], axis=1)
    return dx, stats


def _place():
    x, y, c = lax.axis_index("x"), lax.axis_index("y"), lax.axis_index("c")
    return x, y, c


def _chip_peers(x, y, c):
    return [(1 - x, y, c), (x, 1 - y, c), (1 - x, 1 - y, c)]


def _comm_call(body, name, n_in, out_shape, scratch_shapes):
    hbm = pl.BlockSpec(memory_space=pl.ANY)
    return pl.pallas_call(body, name=name, in_specs=[hbm] * n_in, out_specs=[hbm] * len(out_shape),
                          out_shape=out_shape, scratch_shapes=scratch_shapes)


_HBM = pl.BlockSpec(memory_space=pltpu.HBM)
_SEM = pl.BlockSpec(memory_space=pltpu.SEMAPHORE)
_EFFECT = pltpu.SideEffectType.DATAFLOW_SIDE_EFFECTING


def _exchange_copy(bufs, send, recv, j, peer, place, kind):
    x, y, c = place
    target = peer
    if kind == "gather":
        src = dst = bufs[0].at[2 * x + y]
    elif kind == "scatter":
        src, dst = bufs[0].at[2 * peer[0] + peer[1]], bufs[1].at[j]
    else:
        half = bufs[0].shape[1] // 2
        rows = pl.ds(c * half, half)
        if kind == "half":
            src = dst = bufs[0].at[2 * x + y, rows]
        else:
            src = dst = bufs[0].at[2 * peer[0] + peer[1], rows]
            target = (x, y, 1 - c)
    return pltpu.make_async_remote_copy(src_ref=src, dst_ref=dst, send_sem=send.at[j], recv_sem=recv.at[j],
                                        device_id=target, device_id_type=MESH)


def _exchange_start(groups, after, kind, name):
    sizes = [len(g) for g in groups]
    flat = [b for g in groups for b in g]
    ng, nb = len(groups), len(flat)

    def body(*refs):
        bufs, sems = refs[:nb], refs[nb + 1:nb + 1 + 2 * ng]
        x, y, c = _place()
        for j, peer in enumerate(_chip_peers(x, y, c)):
            at = 0
            for i, size in enumerate(sizes):
                _exchange_copy(bufs[at:at + size], sems[2 * i], sems[2 * i + 1], j, peer, (x, y, c), kind).start()
                at += size

    any_space = pl.BlockSpec(memory_space=pl.ANY)
    out = pl.pallas_call(
        body, name=name, in_specs=[_HBM] * nb + [any_space],
        out_specs=[_SEM] * (2 * ng) + [_HBM] * nb + [any_space],
        out_shape=[pltpu.SemaphoreType.DMA((3,))] * (2 * ng) + [pltpu.HBM(b.shape, b.dtype) for b in flat]
        + [_sds(after.shape, after.dtype)],
        input_output_aliases={i: 2 * ng + i for i in range(nb + 1)},
        compiler_params=pltpu.CompilerParams(has_side_effects=_EFFECT),
    )(*[pltpu.with_memory_space_constraint(b, pltpu.HBM) for b in flat], after)
    started, at = [], 2 * ng
    for i, size in enumerate(sizes):
        started.append((out[2 * i], out[2 * i + 1], tuple(out[at:at + size])))
        at += size
    return started, out[-1]


def _exchange_wait(started, after, kind, name):
    send, recv, bufs = started
    nb = len(bufs)

    def body(*refs):
        x, y, c = _place()
        for j, peer in enumerate(_chip_peers(x, y, c)):
            cp = _exchange_copy(refs[:nb], refs[nb], refs[nb + 1], j, peer, (x, y, c), kind)
            cp.wait_send()
            cp.wait_recv()

    return pl.pallas_call(
        body, name=name, in_specs=[_HBM] * nb + [_SEM, _SEM, pl.BlockSpec(memory_space=pl.ANY)],
        out_specs=[_HBM] * nb, out_shape=[pltpu.HBM(b.shape, b.dtype) for b in bufs],
        input_output_aliases={i: i for i in range(nb)},
        compiler_params=pltpu.CompilerParams(has_side_effects=_EFFECT),
    )(*bufs, send, recv, after)


def _sibling_copies(v_refs, l_refs, send, recv):
    x, y, c = _place()
    return [pltpu.make_async_remote_copy(src_ref=v, dst_ref=l, send_sem=send.at[a], recv_sem=recv.at[a],
                                         device_id=(x, y, 1 - c), device_id_type=MESH)
            for a, (v, l) in enumerate(zip(v_refs, l_refs))]


def _sibling_start(vs, after, name):
    n = len(vs)
    lands = [lax.empty(v.shape, v.dtype) for v in vs]

    def body(*refs):
        for cp in _sibling_copies(refs[:n], refs[n:2 * n], refs[2 * n + 1], refs[2 * n + 2]):
            cp.start()

    any_space = pl.BlockSpec(memory_space=pl.ANY)
    out = pl.pallas_call(
        body, name=name, in_specs=[_HBM] * (2 * n) + [any_space],
        out_specs=[_SEM, _SEM] + [_HBM] * (2 * n) + [any_space],
        out_shape=[pltpu.SemaphoreType.DMA((n,))] * 2 + [pltpu.HBM(b.shape, b.dtype) for b in vs + lands]
        + [_sds(after.shape, after.dtype)],
        input_output_aliases={i: 2 + i for i in range(2 * n + 1)},
        compiler_params=pltpu.CompilerParams(has_side_effects=_EFFECT),
    )(*[pltpu.with_memory_space_constraint(b, pltpu.HBM) for b in vs + lands], after)
    return (out[0], out[1], tuple(out[2:2 + n]), tuple(out[2 + n:2 + 2 * n])), out[-1]


def _sibling_wait(started, after, name):
    send, recv, vs, lands = started
    n = len(vs)

    def body(*refs):
        for cp in _sibling_copies(refs[:n], refs[n:2 * n], refs[2 * n], refs[2 * n + 1]):
            cp.wait_send()
            cp.wait_recv()

    out = pl.pallas_call(
        body, name=name, in_specs=[_HBM] * (2 * n) + [_SEM, _SEM, pl.BlockSpec(memory_space=pl.ANY)],
        out_specs=[_HBM] * (2 * n), out_shape=[pltpu.HBM(b.shape, b.dtype) for b in vs + lands],
        input_output_aliases={i: i for i in range(2 * n)},
        compiler_params=pltpu.CompilerParams(has_side_effects=_EFFECT),
    )(*vs, *lands, send, recv, after)
    return out[:n], out[n:]


def _swap_sibling(vs, name):
    n = len(vs)

    def body(*refs):
        v_refs, o_refs, (send, recv) = refs[:n], refs[n:2 * n], refs[2 * n:]
        x, y, c = _place()
        cps = [pltpu.make_async_remote_copy(
            src_ref=v_refs[a], dst_ref=o_refs[a], send_sem=send.at[a], recv_sem=recv.at[a],
            device_id=(x, y, 1 - c), device_id_type=MESH) for a in range(n)]
        for cp in cps:
            cp.start()
        for cp in cps:
            cp.wait()

    return _comm_call(body, name, n, [_sds(v.shape, v.dtype) for v in vs],
                      [pltpu.SemaphoreType.DMA((n,)), pltpu.SemaphoreType.DMA((n,))])(*vs)


def _gather_rows(v, name):
    r, n = v.shape

    def body(v_ref, o_ref, send, recv, loc):
        x, y, c = _place()
        me = 4 * x + 2 * y + c
        own = pltpu.make_async_copy(v_ref, o_ref.at[me], loc)
        own.start()
        peers = []
        for k in range(1, 8):
            px = 1 - x if k & 4 else x
            py = 1 - y if k & 2 else y
            pc = 1 - c if k & 1 else c
            peers.append((px, py, pc))
        sends = []
        for k, peer in enumerate(peers):
            cp = pltpu.make_async_remote_copy(src_ref=v_ref, dst_ref=o_ref.at[me], send_sem=send.at[k],
                                              recv_sem=recv.at[k], device_id=peer, device_id_type=MESH)
            cp.start()
            sends.append(cp)
        for k, peer in enumerate(peers):
            pltpu.make_async_remote_copy(src_ref=v_ref, dst_ref=o_ref.at[4 * peer[0] + 2 * peer[1] + peer[2]],
                                         send_sem=send.at[k], recv_sem=recv.at[k], device_id=peer,
                                         device_id_type=MESH).wait_recv()
        for cp in sends:
            cp.wait_send()
        own.wait()

    vmem = pl.BlockSpec(memory_space=pltpu.VMEM)
    return pl.pallas_call(body, name=name, in_specs=[vmem], out_specs=vmem, out_shape=_sds((8, r, n), v.dtype),
                          scratch_shapes=[pltpu.SemaphoreType.DMA((7,)), pltpu.SemaphoreType.DMA((7,)),
                                          pltpu.SemaphoreType.DMA])(v)


def _cast_place(w, shard, name):
    r, c = w.shape
    tr = r // 4

    def body(s_ref, w_ref, o_ref):
        o_ref[0] = w_ref[...].astype(BF16)

    return pl.pallas_call(
        body, name=name, out_shape=_sds((N_SHARD, r, c), BF16),
        grid_spec=pltpu.PrefetchScalarGridSpec(
            num_scalar_prefetch=1, grid=(4,), in_specs=[pl.BlockSpec((tr, c), lambda i, s: (i, 0))],
            out_specs=pl.BlockSpec((1, tr, c), lambda i, s: (s[0], i, 0))),
        compiler_params=pltpu.CompilerParams(dimension_semantics=("arbitrary",)),
    )(shard.reshape(1).astype(jnp.int32), w)


def _mod_part(c_all, w_ada, b_ada, name):
    n = w_ada.shape[1]

    def body(c_ref, w_ref, b_ref, a_ref, p_ref):
        cv = c_ref[...]
        ca = cv * _sigmoid(cv)
        a_ref[...] = ca
        p_ref[...] = jnp.dot(ca, w_ref[...], precision=lax.Precision.HIGHEST, preferred_element_type=F32) + b_ref[...]

    full = lambda a: pl.BlockSpec(a.shape, lambda i: (0, 0))
    return _call(body, name=name, grid=(1,), in_specs=[full(c_all), full(w_ada), full(b_ada)],
                 out_specs=[pl.BlockSpec((8, D_MODEL), lambda i: (0, 0)), pl.BlockSpec((8, n), lambda i: (0, 0))],
                 out_shape=[_sds((8, D_MODEL)), _sds((8, n))])(c_all, w_ada, b_ada)


def _sum_received(g, shard, land, name):
    _, r, c = g.shape
    tr = r // 4

    def body(s_ref, g_ref, l_ref, o_ref):
        o_ref[...] = ((g_ref[0] + l_ref[0].astype(F32)) + l_ref[1].astype(F32)) + l_ref[2].astype(F32)

    return pl.pallas_call(
        body, name=name, out_shape=_sds((r, c)),
        grid_spec=pltpu.PrefetchScalarGridSpec(
            num_scalar_prefetch=1, grid=(4,),
            in_specs=[pl.BlockSpec((1, tr, c), lambda i, s: (s[0], i, 0)),
                      pl.BlockSpec((3, tr, c), lambda i, s: (0, i, 0))],
            out_specs=pl.BlockSpec((tr, c), lambda i, s: (i, 0))),
        compiler_params=pltpu.CompilerParams(dimension_semantics=("arbitrary",), vmem_limit_bytes=VMEM_LIMIT),
    )(shard.reshape(1).astype(jnp.int32), g, land)


def _outer_sum(ct, dm, name):
    k, n = ct.shape[0], dm.shape[1]
    tr = k // 4

    def body(c_ref, d_ref, o_ref):
        cv = c_ref[...]
        dv = d_ref[...]
        acc = cv[:, 0:1] * dv[0:1, :]
        for i in range(1, 8):
            acc = acc + cv[:, i:i + 1] * dv[i:i + 1, :]
        o_ref[...] = acc

    return _call(body, name=name, grid=(4,), in_specs=[_rows(tr, 8), pl.BlockSpec((8, n), lambda i: (0, 0))],
                 out_specs=_rows(tr, n), out_shape=_sds((k, n)))(ct, dm)


def _adamw_math(w, g, m, v):
    m_new = ADAM_B1 * m + (1.0 - ADAM_B1) * g
    v_new = ADAM_B2 * v + (1.0 - ADAM_B2) * (g * g)
    m_hat = m_new / (1.0 - ADAM_B1 ** ADAM_STEP)
    v_hat = v_new / (1.0 - ADAM_B2 ** ADAM_STEP)
    return -ADAM_LR * (m_hat / (jnp.sqrt(v_hat) + ADAM_EPS) + ADAM_WD * w), m_new, v_new


def _small_update(stats, smalls, name):
    offsets = [ST_DMOD, ST_DG1, ST_DLB, ST_DOG, ST_DAG, ST_DG2, ST_DFG]
    lb_index = 2

    def body(*refs):
        s_ref, ins, l_ref, outs = refs[0], refs[1:22], refs[22], refs[23:]
        tot = s_ref[0:1, :]
        for i in range(1, 8):
            tot = tot + s_ref[i:i + 1, :]
        l_ref[...] = jnp.zeros((1, 128), F32) + (0.5 / D_MODEL) * jnp.sum(tot[:, ST_LOSS:ST_LOSS + D_MODEL])
        for p, off in enumerate(offsets):
            w_ref, m_ref, v_ref = ins[3 * p:3 * p + 3]
            g_out, d_out, m_out, v_out = outs[4 * p:4 * p + 4]
            g = tot[:, off:off + w_ref.shape[1]]
            if p == lb_index:
                lg = w_ref[...]
                lb = _sigmoid(lg[0:1] - lg[1:2])
                g = g * lb * (1.0 - lb)
            for r in range(w_ref.shape[0]):
                rows = slice(r, r + 1)
                gr = g if r == 0 else -g
                delta, m_new, v_new = _adamw_math(w_ref[rows, :], gr, m_ref[rows, :], v_ref[rows, :])
                g_out[rows, :] = gr
                d_out[rows, :] = delta
                m_out[rows, :] = m_new
                v_out[rows, :] = v_new

    full = lambda a: pl.BlockSpec(a.shape, lambda i: (0, 0))
    flat = [a for t in smalls for a in t]
    return _call(body, name=name, grid=(1,),
                 in_specs=[full(stats)] + [full(a) for a in flat],
                 out_specs=[pl.BlockSpec((1, 128), lambda i: (0, 0))] + [full(t[0]) for t in smalls for _ in range(4)],
                 out_shape=[_sds((1, 128))] + [_sds(t[0].shape) for t in smalls for _ in range(4)])(stats, *flat)


def _adamw(w, gs, m, v, name, steps=4):
    r, c = w.shape
    tr = r // steps
    ng = len(gs)

    def body(*refs):
        w_ref, g_refs, (m_ref, v_ref, g_out, d_out, m_out, v_out) = refs[0], refs[1:1 + ng], refs[1 + ng:]
        g = g_refs[0][...]
        for g_ref in g_refs[1:]:
            g = g + g_ref[...]
        g_out[...] = g
        d_out[...], m_out[...], v_out[...] = _adamw_math(w_ref[...], g, m_ref[...], v_ref[...])

    row = _rows(tr, c)
    return _call(body, name=name, grid=(steps,), in_specs=[row] * (3 + ng), out_specs=[row] * 4,
                 out_shape=[_sds((r, c))] * 4)(w, *gs, m, v)


def kernel(x, c, w_ada, b_ada, norm1_g, w_in, hg_lb_logits, hg_onorm_g, att_onorm_g, w_out, norm2_g, w_gate_up, w_down, final_g, loss_target, m_w_ada, m_b_ada, m_norm1_g, m_w_in, m_hg_lb_logits, m_hg_onorm_g, m_att_onorm_g, m_w_out, m_norm2_g, m_w_gate_up, m_w_down, m_final_g, v_w_ada, v_b_ada, v_norm1_g, v_w_in, v_hg_lb_logits, v_hg_onorm_g, v_att_onorm_g, v_w_out, v_norm2_g, v_w_gate_up, v_w_down, v_final_g):
    ix, iy, ic = _place()
    shard = 2 * ix + iy
    sample = 4 * ix + 2 * iy + ic
    n_ada = w_ada.shape[2]

    shards = [w_in[0], w_out[0], w_gate_up[0], w_down[0]]
    names = ["w_in", "w_out", "w_gu", "w_down"]
    shapes = [(N_SHARD,) + w.shape for w in shards]
    placed = [(_cast_place(w, shard, "place_" + nm),) for w, nm in zip(shards, names)]

    c_all = _gather_rows(c, "gather_c").reshape(8, D_MODEL)
    b_part = lax.dynamic_slice(b_ada, (0, shard * n_ada), (1, n_ada))
    c_act, part = _mod_part(c_all, w_ada[0], b_part, "mod_part")
    parts = _gather_rows(part, "gather_mod")[::2]
    mod = lax.dynamic_index_in_dim(parts, sample, axis=1, keepdims=False).reshape(1, 6 * D_MODEL)
    (first,), mod = _exchange_start(placed[:1], mod, "half", "gather_start_w_in")
    gathering = {}

    def get_w(name, after):
        if name == "w_in":
            halves = _exchange_wait(first, after, "half", "gather_wait_w_in")
            (passing,), token = _exchange_start([tuple(halves)], mod, "forward", "forward_start_w_in")
            (full,) = _exchange_wait(passing, token, "forward", "forward_wait_w_in")
            rest, full = _exchange_start(placed[1:], full, "gather", "gather_start_rest")
            gathering.update(zip(names[1:], rest))
            return full
        (full,) = _exchange_wait(gathering[name], after, "gather", "gather_wait_" + name)
        return full if name == "w_gu" else full.reshape(1, -1, D_MODEL)

    scattering = {}

    def put_g(name, g, g_bf16, then):
        shape = shapes[names.index(name)]
        land = lax.empty((3,) + shape[1:], BF16)
        (started,), then = _exchange_start([(g_bf16.reshape(shape), land)], then, "scatter", "scatter_start_" + name)
        scattering[name] = (g.reshape(shape), started)
        return then

    def summed(name, after):
        g, started = scattering[name]
        _, land = _exchange_wait(started, after, "scatter", "scatter_wait_" + name)
        return _sum_received(g, shard, land, "sum_" + name)

    early = ["w_down", "w_gu", "w_out"]
    swapping = []

    def late(a):
        started, a = _sibling_start([summed(nm, a) for nm in early], a, "swap_start")
        swapping.append(started)
        return a

    dx, stats = _local_step(x[0], loss_target[0], mod, norm1_g, hg_lb_logits, hg_onorm_g, att_onorm_g,
                            norm2_g, final_g, get_w, put_g, late)

    stats_all = _gather_rows(stats, "gather_stats").reshape(8, ST_WIDTH)
    dmod = lax.dynamic_slice(stats_all, (0, ST_DMOD + shard * n_ada), (8, n_ada))
    g_ada = _outer_sum(c_act.T, dmod, "w_ada_grad")

    as_row = lambda a: a.reshape(1, -1) if a.ndim == 1 else a
    smalls = [tuple(as_row(a) for a in t) for t in [
        (b_ada, m_b_ada, v_b_ada), (norm1_g, m_norm1_g, v_norm1_g),
        (hg_lb_logits, m_hg_lb_logits, v_hg_lb_logits), (hg_onorm_g, m_hg_onorm_g, v_hg_onorm_g),
        (att_onorm_g, m_att_onorm_g, v_att_onorm_g), (norm2_g, m_norm2_g, v_norm2_g),
        (final_g, m_final_g, v_final_g)]]
    loss, *small_out = _small_update(stats_all, smalls, "small_update")
    shapes_out = [b_ada.shape, norm1_g.shape, hg_lb_logits.shape, hg_onorm_g.shape, att_onorm_g.shape,
                  norm2_g.shape, final_g.shape]
    sg, sd, sm, sv = [[small_out[4 * p + i].reshape(shapes_out[p]) for p in range(7)] for i in range(4)]

    ada = _adamw(w_ada[0], [g_ada], m_w_ada[0], v_w_ada[0], "adamw_w_ada")
    moments = [(m_w_in, v_w_in), (m_w_out, v_w_out), (m_w_gate_up, v_w_gate_up), (m_w_down, v_w_down)]

    def update(group, sums, other):
        return {nm: _adamw(shards[names.index(nm)], [s, o], moments[names.index(nm)][0][0],
                           moments[names.index(nm)][1][0], "adamw_" + nm) for nm, s, o in zip(group, sums, other)}

    sums, other = _sibling_wait(swapping[0], ada[1], "swap_wait")
    done = update(early, sums, other)
    sum_in = summed("w_in", done["w_out"][1])
    done.update(update(["w_in"], [sum_in], _swap_sibling([sum_in], "swap_sum_in")))
    big = [ada] + [done[nm] for nm in names]
    bg, bd, bm, bv = [[t[i][None] for t in big] for i in range(4)]

    def order(b, s):
        return [b[0], s[0], s[1], b[1], s[2], s[3], s[4], b[2], s[5], b[3], b[4], s[6]]

    return (loss[0, 0], dx[None], *order(bg, sg), *order(bd, sd), *order(bm, sm), *order(bv, sv))
```

```python
import functools

import jax
import jax.numpy as jnp
from jax import lax
from jax.experimental import pallas as pl
from jax.experimental.pallas import tpu as pltpu

F32 = jnp.float32
BF16 = jnp.bfloat16
MESH = pl.DeviceIdType.MESH

D_MODEL = 1024
HG_WIDTH = 512
HG_HEAD = 128
HG_CHUNK = 64
HG_GROUP = 4
ATT_WIDTH = 512
ATT_HEADS = 8
ATT_BLOCK = 128
DILATIONS = (1, 4, 16)
D_FF = 2816
IN_WIDTH = 3584
N_SHARD = 4
RMS_EPS = 1e-6
NEG = -1e30

ADAM_LR = 0.001
ADAM_B1 = 0.9
ADAM_B2 = 0.999
ADAM_EPS = 1e-08
ADAM_WD = 0.01
ADAM_STEP = 10

VMEM_LIMIT = 56 * 2**20

ST_LOSS, ST_DFG, ST_DG2, ST_DG1 = 0, 1024, 2048, 3072
ST_DLB, ST_DAG, ST_DOG, ST_DMOD = 4096, 4608, 5120, 5248
ST_WIDTH = 5248 + 6144
SP_BADA, SP_N1, SP_LB, SP_OG, SP_AG, SP_N2, SP_FG = 0, 6144, 7168, 8192, 8320, 8832, 9856
SP_WIDTH = 10880


def _call(body, *, name, grid, in_specs, out_specs, out_shape, scratch_shapes=()):
    return pl.pallas_call(
        body, name=name, grid=grid, in_specs=in_specs, out_specs=out_specs, out_shape=out_shape,
        scratch_shapes=list(scratch_shapes),
        compiler_params=pltpu.CompilerParams(
            dimension_semantics=("arbitrary",) * len(grid), vmem_limit_bytes=VMEM_LIMIT))


def _sds(shape, dtype=F32):
    return jax.ShapeDtypeStruct(shape, dtype)


def _dot(a, b):
    return jnp.dot(a, b, preferred_element_type=F32)


def _dot_nt(a, b):
    return lax.dot_general(a, b, (((1,), (1,)), ((), ())), preferred_element_type=F32)


def _dot_tn(a, b):
    return lax.dot_general(a, b, (((0,), (0,)), ((), ())), preferred_element_type=F32)


def _sigmoid(x):
    return 1.0 / (1.0 + jnp.exp(-x))


def _rows(tr, width):
    return pl.BlockSpec((tr, width), lambda i: (i, 0))


def _vec(width):
    return pl.BlockSpec((1, width), lambda i: (0, 0))


def _acc(ref, val, first):
    @pl.when(first)
    def _():
        ref[...] = val

    @pl.when(jnp.logical_not(first))
    def _():
        ref[...] += val


def _mm_nn(a, b3, name, tm=1024, out_dtype=F32):
    m, k = a.shape
    s, _, n = b3.shape

    def body(a_ref, b_ref, o_ref):
        o_ref[...] = _dot(a_ref[...], b_ref[0]).astype(out_dtype)

    return _call(
        body, name=name, grid=(s, m // tm),
        in_specs=[pl.BlockSpec((tm, k), lambda j, i: (i, 0)), pl.BlockSpec((1, k, n), lambda j, i: (j, 0, 0))],
        out_specs=pl.BlockSpec((tm, n), lambda j, i: (i, j)), out_shape=_sds((m, s * n), out_dtype))(a, b3)


def _mm_nt(dy, b3, name, tm=512):
    m = dy.shape[0]
    s, k, n = b3.shape

    def body(dy_ref, b_ref, o_ref):
        acc = _dot_nt(dy_ref[:, 0:n], b_ref[0])
        for j in range(1, s):
            acc = acc + _dot_nt(dy_ref[:, j * n:(j + 1) * n], b_ref[j])
        o_ref[...] = acc.astype(BF16)

    return _call(
        body, name=name, grid=(m // tm,),
        in_specs=[_rows(tm, s * n), pl.BlockSpec((s, k, n), lambda i: (0, 0, 0))],
        out_specs=_rows(tm, k), out_shape=_sds((m, k), BF16))(dy, b3)


def _mm_tn(a, dy, s, name, tm, tk):
    m, k = a.shape
    n = dy.shape[1] // s
    steps = m // tm

    def body(a_ref, dy_ref, o_ref, ob_ref):
        p = _dot_tn(a_ref[...], dy_ref[...])[None]
        if steps == 1:
            o_ref[...] = p
            ob_ref[...] = p.astype(BF16)
        else:
            i = pl.program_id(2)
            _acc(o_ref, p, i == 0)

            @pl.when(i == steps - 1)
            def _():
                ob_ref[...] = o_ref[...].astype(BF16)

    out = pl.BlockSpec((1, tk, n), lambda kk, j, i: (j, kk, 0))
    return _call(
        body, name=name, grid=(k // tk, s, steps),
        in_specs=[pl.BlockSpec((tm, tk), lambda kk, j, i: (i, kk)), pl.BlockSpec((tm, n), lambda kk, j, i: (i, j))],
        out_specs=[out, out], out_shape=[_sds((s, k, n)), _sds((s, k, n), BF16)])(a, dy)


def _rms(x):
    return lax.rsqrt(jnp.mean(x * x, axis=-1, keepdims=True) + RMS_EPS)


def _rms_bwd(dxh, xh, r):
    return r * (dxh - xh * jnp.mean(dxh * xh, axis=-1, keepdims=True))


def _norm_mod(x, g, scale, shift, name, tr=512):
    t = x.shape[0]

    def body(x_ref, g_ref, sc_ref, sh_ref, h_ref):
        xv = x_ref[...]
        n = xv * _rms(xv) * g_ref[...]
        h_ref[...] = (n * (1.0 + sc_ref[...]) + sh_ref[...]).astype(BF16)

    return _call(body, name=name, grid=(t // tr,),
                 in_specs=[_rows(tr, D_MODEL), _vec(D_MODEL), _vec(D_MODEL), _vec(D_MODEL)],
                 out_specs=_rows(tr, D_MODEL), out_shape=_sds((t, D_MODEL), BF16))(x, g, scale, shift)


def _mix_in(o_hg, proj, att, og, ag, name, tr=256):
    t = o_hg.shape[0]

    def body(o_ref, g_ref, a_ref, og_ref, ag_ref, m_ref):
        for h in range(HG_WIDTH // HG_HEAD):
            sl = slice(h * HG_HEAD, (h + 1) * HG_HEAD)
            oh = o_ref[:, sl].astype(F32)
            gv = g_ref[:, sl].astype(F32)
            m_ref[:, sl] = (oh * _rms(oh) * og_ref[...] * (gv * _sigmoid(gv))).astype(BF16)
        av = a_ref[...]
        m_ref[:, HG_WIDTH:] = (av * _rms(av) * ag_ref[...]).astype(BF16)

    return _call(body, name=name, grid=(t // tr,),
                 in_specs=[_rows(tr, HG_WIDTH), pl.BlockSpec((tr, HG_WIDTH), lambda i: (i, 3)), _rows(tr, ATT_WIDTH),
                           _vec(HG_HEAD), _vec(ATT_WIDTH)],
                 out_specs=_rows(tr, D_MODEL), out_shape=_sds((t, D_MODEL), BF16))(o_hg, proj, att, og, ag)


def _resid_norm_mod(x, mix, gate, g, scale, shift, name, tr=256):
    t = x.shape[0]

    def body(x_ref, m_ref, gt_ref, g_ref, sc_ref, sh_ref, x2_ref, h_ref):
        x2 = x_ref[...] + gt_ref[...] * m_ref[...]
        x2_ref[...] = x2
        n = x2 * _rms(x2) * g_ref[...]
        h_ref[...] = (n * (1.0 + sc_ref[...]) + sh_ref[...]).astype(BF16)

    return _call(body, name=name, grid=(t // tr,),
                 in_specs=[_rows(tr, D_MODEL), _rows(tr, D_MODEL)] + [_vec(D_MODEL)] * 4,
                 out_specs=[_rows(tr, D_MODEL), _rows(tr, D_MODEL)],
                 out_shape=[_sds((t, D_MODEL)), _sds((t, D_MODEL), BF16)])(x, mix, gate, g, scale, shift)


def _mm_gate_up(h, w_gu, name, tm=1024):
    m, k = h.shape
    n = w_gu.shape[2]

    def body(h_ref, wa_ref, wu_ref, da_ref, du_ref, o_ref):
        hv = h_ref[...]
        a = _dot(hv, wa_ref[0])
        u = _dot(hv, wu_ref[0])
        sg = _sigmoid(a)
        silu = a * sg
        da_ref[...] = (u * sg * (1.0 + a * (1.0 - sg))).astype(BF16)
        du_ref[...] = silu.astype(BF16)
        o_ref[...] = (silu * u).astype(BF16)

    out = pl.BlockSpec((tm, n), lambda j, i: (i, j))
    return _call(body, name=name, grid=(2, m // tm),
                 in_specs=[pl.BlockSpec((tm, k), lambda j, i: (i, 0)), pl.BlockSpec((1, k, n), lambda j, i: (j, 0, 0)),
                           pl.BlockSpec((1, k, n), lambda j, i: (j + 2, 0, 0))],
                 out_specs=[out, out, out], out_shape=[_sds((m, 2 * n), BF16)] * 3)(h, w_gu, w_gu)


def _mm_down_dx(dffn, w_down, act_da, act_du, name, tm=512):
    m = dffn.shape[0]
    _, k, n = w_down.shape

    def body(d_ref, w_ref, da_ref, du_ref, o_ref):
        dact = _dot_nt(d_ref[...], w_ref[0])
        o_ref[:, :k] = (dact * da_ref[...].astype(F32)).astype(BF16)
        o_ref[:, k:] = (dact * du_ref[...].astype(F32)).astype(BF16)

    return _call(body, name=name, grid=(m // tm,),
                 in_specs=[_rows(tm, n), pl.BlockSpec((1, k, n), lambda i: (0, 0, 0)), _rows(tm, k), _rows(tm, k)],
                 out_specs=_rows(tm, 2 * k), out_shape=_sds((m, 2 * k), BF16))(dffn, w_down, act_da, act_du)


def _final_loss(x2, ffn, gate, fg, tgt, name, tr=256):
    t = x2.shape[0]

    def body(x_ref, f_ref, gt_ref, fg_ref, t_ref, dx_ref, df_ref, l_ref, dfg_ref, dgt_ref):
        first = pl.program_id(0) == 0
        ffn_v = f_ref[...]
        x3 = x_ref[...] + gt_ref[...] * ffn_v
        r = _rms(x3)
        xh = x3 * r
        err = xh * fg_ref[...] - t_ref[...]
        dy = err * (1.0 / D_MODEL)
        dx3 = _rms_bwd(dy * fg_ref[...], xh, r)
        dx_ref[...] = dx3
        df_ref[...] = (dx3 * gt_ref[...]).astype(BF16)
        _acc(l_ref, jnp.sum(err * err, axis=0, keepdims=True), first)
        _acc(dfg_ref, jnp.sum(dy * xh, axis=0, keepdims=True), first)
        _acc(dgt_ref, jnp.sum(dx3 * ffn_v, axis=0, keepdims=True), first)

    row, vec = _rows(tr, D_MODEL), _vec(D_MODEL)
    return _call(body, name=name, grid=(t // tr,), in_specs=[row, row, vec, vec, row],
                 out_specs=[row, row, vec, vec, vec],
                 out_shape=[_sds((t, D_MODEL)), _sds((t, D_MODEL), BF16)] + [_sds((1, D_MODEL))] * 3)(
                     x2, ffn, gate, fg, tgt)


def _norm_mod_bwd(dh, x, g, scale, dres, name, gate=None, mix=None, tr=256):
    t = x.shape[0]
    below = gate is not None

    def body(*refs):
        if below:
            dh_ref, x_ref, g_ref, sc_ref, dr_ref, gt_ref, m_ref, dx_ref, dsh_ref, dsc_ref, dg_ref, dgt_ref, dm_ref = refs
        else:
            dh_ref, x_ref, g_ref, sc_ref, dr_ref, dx_ref, dsh_ref, dsc_ref, dg_ref = refs
        first = pl.program_id(0) == 0
        xv = x_ref[...]
        dhv = dh_ref[...].astype(F32)
        r = _rms(xv)
        xh = xv * r
        dn = dhv * (1.0 + sc_ref[...])
        dx = dr_ref[...] + _rms_bwd(dn * g_ref[...], xh, r)
        dx_ref[...] = dx
        _acc(dsh_ref, jnp.sum(dhv, axis=0, keepdims=True), first)
        _acc(dsc_ref, jnp.sum(dhv * xh * g_ref[...], axis=0, keepdims=True), first)
        _acc(dg_ref, jnp.sum(dn * xh, axis=0, keepdims=True), first)
        if below:
            _acc(dgt_ref, jnp.sum(dx * m_ref[...], axis=0, keepdims=True), first)
            dm_ref[...] = (dx * gt_ref[...]).astype(BF16)

    row, vec = _rows(tr, D_MODEL), _vec(D_MODEL)
    in_specs = [row, row, vec, vec, row] + ([vec, row] if below else [])
    out_specs = [row, vec, vec, vec] + ([vec, row] if below else [])
    out_shape = [_sds((t, D_MODEL))] + [_sds((1, D_MODEL))] * 3 + ([_sds((1, D_MODEL)), _sds((t, D_MODEL), BF16)] if below else [])
    args = (dh, x, g, scale, dres) + ((gate, mix) if below else ())
    return _call(body, name=name, grid=(t // tr,), in_specs=in_specs, out_specs=out_specs, out_shape=out_shape)(*args)


def _mix_in_bwd(dmi, o_hg, proj, att, og, ag, name, tr=256):
    t = o_hg.shape[0]

    def body(d_ref, o_ref, g_ref, a_ref, og_ref, ag_ref, do_ref, dg_ref, da_ref, dd_ref, dog_ref, dag_ref):
        first = pl.program_id(0) == 0
        dog = jnp.zeros((1, HG_HEAD), F32)
        for h in range(HG_WIDTH // HG_HEAD):
            sl = slice(h * HG_HEAD, (h + 1) * HG_HEAD)
            oh = o_ref[:, sl].astype(F32)
            gv = g_ref[:, sl].astype(F32)
            dv = d_ref[:, sl].astype(F32)
            r = _rms(oh)
            xh = oh * r
            sg = _sigmoid(gv)
            dno = dv * gv * sg
            dg_ref[:, sl] = (dv * xh * og_ref[...] * sg * (1.0 + gv * (1.0 - sg))).astype(BF16)
            dog = dog + jnp.sum(dno * xh, axis=0, keepdims=True)
            do_ref[:, sl] = _rms_bwd(dno * og_ref[...], xh, r).astype(BF16)
        _acc(dog_ref, dog, first)
        av = a_ref[...]
        dav = d_ref[:, HG_WIDTH:].astype(F32)
        r = _rms(av)
        xa = av * r
        _acc(dag_ref, jnp.sum(dav * xa, axis=0, keepdims=True), first)
        datt = _rms_bwd(dav * ag_ref[...], xa, r)
        da_ref[...] = datt.astype(BF16)
        prod = datt * av
        lane = lax.broadcasted_iota(jnp.int32, (1, 128), 1)
        dd = jnp.zeros((tr, 128), F32)
        for hp in range(ATT_HEADS // 2):
            pp = prod[:, hp * 128:(hp + 1) * 128]
            lo = jnp.sum(jnp.where(lane < 64, pp, 0.0), axis=-1, keepdims=True)
            hi = jnp.sum(jnp.where(lane >= 64, pp, 0.0), axis=-1, keepdims=True)
            dd = jnp.where(lane == 2 * hp, lo, dd)
            dd = jnp.where(lane == 2 * hp + 1, hi, dd)
        dd_ref[...] = dd

    half = _rows(tr, HG_WIDTH)
    return _call(body, name=name, grid=(t // tr,),
                 in_specs=[_rows(tr, D_MODEL), half, pl.BlockSpec((tr, HG_WIDTH), lambda i: (i, 3)), half,
                           _vec(HG_HEAD), _vec(ATT_WIDTH)],
                 out_specs=[half, half, half, _rows(tr, 128), _vec(HG_HEAD), _vec(ATT_WIDTH)],
                 out_shape=[_sds((t, HG_WIDTH), BF16)] * 3 + [_sds((t, 128)), _sds((1, HG_HEAD)), _sds((1, ATT_WIDTH))])(
                     dmi, o_hg, proj, att, og, ag)


def _dproj(dhg, dg, dqs, dkvs, name, tr=256):
    t = dhg.shape[0]
    w3 = 3 * HG_WIDTH
    w4 = w3 + HG_WIDTH
    nbr = len(dqs)

    def body(*refs):
        h_ref, g_ref, q_refs, kv_refs, o_ref = refs[0], refs[1], refs[2:2 + nbr], refs[2 + nbr:2 + 2 * nbr], refs[-1]
        o_ref[:, :w3] = h_ref[...]
        o_ref[:, w3:w4] = g_ref[...].astype(BF16)
        o_ref[:, w4:w4 + ATT_WIDTH] = sum(r[...].astype(F32) for r in q_refs).astype(BF16)
        o_ref[:, w4 + ATT_WIDTH:] = sum(r[...].astype(F32) for r in kv_refs).astype(BF16)

    return _call(body, name=name, grid=(t // tr,),
                 in_specs=[_rows(tr, w3), _rows(tr, HG_WIDTH)] + [_rows(tr, ATT_WIDTH)] * nbr
                 + [_rows(tr, 2 * ATT_WIDTH)] * nbr,
                 out_specs=_rows(tr, IN_WIDTH), out_shape=_sds((t, IN_WIDTH), BF16))(dhg, dg, *dqs, *dkvs)


def _chunk_tri(upper):
    row = lax.broadcasted_iota(jnp.int32, (HG_GROUP, HG_CHUNK, HG_CHUNK), 1)
    col = lax.broadcasted_iota(jnp.int32, (HG_GROUP, HG_CHUNK, HG_CHUNK), 2)
    return (row <= col if upper else row >= col).astype(BF16)


def _chunk_cumsum(x, tri):
    x3 = x.reshape(HG_GROUP, HG_CHUNK, x.shape[1])
    dims = (((2,), (1,)), ((0,), (0,)))
    out = None
    for _ in range(3):
        part = x3.astype(BF16)
        x3 = x3 - part.astype(F32)
        term = lax.dot_general(tri, part, dims, preferred_element_type=F32)
        out = term if out is None else out + term
    return out.reshape(x.shape)


def _hg_gates(f_raw, q_raw, lb, tri):
    sg = _sigmoid(f_raw)
    f = lb + (1.0 - lb) * sg
    k = 1.0 - f
    b = _chunk_cumsum(jnp.log(f), tri)
    sq = _sigmoid(q_raw)
    return sg, f, k, b, sq


def _hg_masks(rows):
    row = lax.broadcasted_iota(jnp.int32, (rows, rows), 0)
    col = lax.broadcasted_iota(jnp.int32, (rows, rows), 1)
    same = (row // HG_CHUNK) == (col // HG_CHUNK)
    return jnp.logical_and(row >= col, same), jnp.logical_and(row <= col, same)


def _per_chunk(rows_of):
    return jnp.concatenate([jnp.broadcast_to(r, (HG_CHUNK, r.shape[1])) for r in rows_of], axis=0)


def _hgrn_fwd(proj, lb_logits, name):
    t = proj.shape[0]
    nc = t // HG_CHUNK
    nh = HG_WIDTH // HG_HEAD
    rows = HG_GROUP * HG_CHUNK

    def body(q_ref, f_ref, i_ref, lg_ref, o_ref, st_ref, s_scr):
        @pl.when(pl.program_id(0) == 0)
        def _():
            s_scr[...] = jnp.zeros_like(s_scr)

        lg = lg_ref[...]
        lb_all = _sigmoid(lg[0:1] - lg[1:2])
        causal, _ = _hg_masks(rows)
        tri = _chunk_tri(False)
        for h in range(nh):
            sl = slice(h * HG_HEAD, (h + 1) * HG_HEAD)
            q_raw = q_ref[:, sl].astype(F32)
            _, _, k, b, sq = _hg_gates(f_ref[:, sl].astype(F32), q_raw, lb_all[:, sl], tri)
            v = i_ref[:, sl].astype(BF16)
            gls = [b[(g + 1) * HG_CHUNK - 1:(g + 1) * HG_CHUNK] for g in range(HG_GROUP)]
            qd = (q_raw * sq * jnp.exp(b)).astype(BF16)
            kd = (k * jnp.exp(-b)).astype(BF16)
            ke = (k * jnp.exp(_per_chunk(gls) - b)).astype(BF16)
            a = jnp.where(causal, _dot_nt(qd, kd), 0.0).astype(BF16)
            o_intra = _dot(a, v)
            st = s_scr[h]
            o_inter = []
            for g in range(HG_GROUP):
                rs = slice(g * HG_CHUNK, (g + 1) * HG_CHUNK)
                st_ref[g, sl, :] = st
                o_inter.append(_dot_nt(qd[rs], st.astype(BF16)))
                st = st * jnp.exp(gls[g]) + _dot_tn(v[rs], ke[rs])
            s_scr[h] = st
            o_ref[:, sl] = (o_intra + jnp.concatenate(o_inter, axis=0)).astype(BF16)

    blk = lambda j: pl.BlockSpec((rows, HG_WIDTH), lambda c: (c, j))
    return _call(body, name=name, grid=(nc // HG_GROUP,),
                 in_specs=[blk(0), blk(1), blk(2), pl.BlockSpec((2, HG_WIDTH), lambda c: (0, 0))],
                 out_specs=[blk(0), pl.BlockSpec((HG_GROUP, HG_WIDTH, HG_HEAD), lambda c: (c, 0, 0))],
                 out_shape=[_sds((t, HG_WIDTH), BF16), _sds((nc, HG_WIDTH, HG_HEAD))],
                 scratch_shapes=[pltpu.VMEM((nh, HG_HEAD, HG_HEAD), F32)])(proj, proj, proj, lb_logits)


def _hgrn_bwd(proj, lb_logits, states, do, name):
    t = proj.shape[0]
    ng = t // (HG_GROUP * HG_CHUNK)
    nh = HG_WIDTH // HG_HEAD
    rows = HG_GROUP * HG_CHUNK

    def body(q_ref, f_ref, i_ref, lg_ref, st_ref, do_ref, d_ref, dlb_ref, ds_scr):
        first = pl.program_id(0) == 0

        @pl.when(first)
        def _():
            ds_scr[...] = jnp.zeros_like(ds_scr)

        lg = lg_ref[...]
        lb_all = _sigmoid(lg[0:1] - lg[1:2])
        causal, _ = _hg_masks(rows)
        tri = _chunk_tri(False)
        tri_t = _chunk_tri(True)
        dlb = []
        for h in range(nh):
            sl = slice(h * HG_HEAD, (h + 1) * HG_HEAD)
            q_raw = q_ref[:, sl].astype(F32)
            lb = lb_all[:, sl]
            sg, f, k, b, sq = _hg_gates(f_ref[:, sl].astype(F32), q_raw, lb, tri)
            v = i_ref[:, sl].astype(BF16)
            gls = [b[(g + 1) * HG_CHUNK - 1:(g + 1) * HG_CHUNK] for g in range(HG_GROUP)]
            eb = jnp.exp(b)
            enb = jnp.exp(-b)
            egb = jnp.exp(_per_chunk(gls) - b)
            ke = k * egb
            qd_b, kd_b, ke_b = (q_raw * sq * eb).astype(BF16), (k * enb).astype(BF16), ke.astype(BF16)
            dov = do_ref[:, sl].astype(BF16)
            a = jnp.where(causal, _dot_nt(qd_b, kd_b), 0.0).astype(BF16)
            da = jnp.where(causal, _dot_nt(dov, v), 0.0).astype(BF16)
            dkd = _dot_tn(da, qd_b)
            dst = ds_scr[h]
            dqd_s, dv_s, dke_s, dgl_s = [None] * HG_GROUP, [None] * HG_GROUP, [None] * HG_GROUP, [None] * HG_GROUP
            for g in reversed(range(HG_GROUP)):
                rs = slice(g * HG_CHUNK, (g + 1) * HG_CHUNK)
                st = st_ref[g, sl, :]
                dst_b = dst.astype(BF16)
                egl = jnp.exp(gls[g])
                dqd_s[g] = _dot(dov[rs], st.astype(BF16))
                dv_s[g] = _dot_nt(ke_b[rs], dst_b)
                dke_s[g] = _dot(v[rs], dst_b)
                dgl_s[g] = jnp.sum(dst * st, axis=0, keepdims=True) * egl
                dst = _dot_tn(dov[rs], qd_b[rs]) + dst * egl
            ds_scr[h] = dst
            dqd = _dot(da, kd_b) + jnp.concatenate(dqd_s, axis=0)
            dv = _dot_tn(a, dov) + jnp.concatenate(dv_s, axis=0)
            dke = jnp.concatenate(dke_s, axis=0)
            t1 = dke * ke
            db = dqd * qd_b.astype(F32) - dkd * kd_b.astype(F32) - t1
            dgl = _per_chunk([dgl_s[g] + jnp.sum(t1[g * HG_CHUNK:(g + 1) * HG_CHUNK], axis=0, keepdims=True)
                              for g in range(HG_GROUP)])
            dlf = _chunk_cumsum(db, tri_t) + dgl
            df = dlf / f - (dkd * enb + dke * egb)
            d_ref[:, sl] = (dqd * eb * sq * (1.0 + q_raw * (1.0 - sq))).astype(BF16)
            d_ref[:, HG_WIDTH + h * HG_HEAD:HG_WIDTH + (h + 1) * HG_HEAD] = (
                df * (1.0 - lb) * sg * (1.0 - sg)).astype(BF16)
            d_ref[:, 2 * HG_WIDTH + h * HG_HEAD:2 * HG_WIDTH + (h + 1) * HG_HEAD] = dv.astype(BF16)
            dlb.append(jnp.sum(df * (1.0 - sg), axis=0, keepdims=True))
        _acc(dlb_ref, jnp.concatenate(dlb, axis=1), first)

    rev = lambda j: pl.BlockSpec((rows, HG_WIDTH), lambda c: (ng - 1 - c, j))
    return _call(body, name=name, grid=(ng,),
                 in_specs=[rev(0), rev(1), rev(2), pl.BlockSpec((2, HG_WIDTH), lambda c: (0, 0)),
                           pl.BlockSpec((HG_GROUP, HG_WIDTH, HG_HEAD), lambda c: (ng - 1 - c, 0, 0)), rev(0)],
                 out_specs=[pl.BlockSpec((rows, 3 * HG_WIDTH), lambda c: (ng - 1 - c, 0)), _vec(HG_WIDTH)],
                 out_shape=[_sds((t, 3 * HG_WIDTH), BF16), _sds((1, HG_WIDTH))],
                 scratch_shapes=[pltpu.VMEM((nh, HG_HEAD, HG_HEAD), F32)])(proj, proj, proj, lb_logits, states, do)


def _to_sub(a, dil):
    t, w = a.shape
    return a if dil == 1 else a.reshape(t // dil, dil, w).transpose(1, 0, 2).reshape(t, w)


def _from_sub(a, dil):
    t, w = a.shape
    return a if dil == 1 else a.reshape(dil, t // dil, w).transpose(1, 0, 2).reshape(t, w)


def _att_mask(has_prev, seg):
    def place(v):
        v = v % ATT_BLOCK
        return v if seg == 1 else seg * (v % (ATT_BLOCK // seg)) + v // (ATT_BLOCK // seg)

    row = lax.broadcasted_iota(jnp.int32, (2 * ATT_BLOCK, 2 * ATT_BLOCK), 0)
    col = lax.broadcasted_iota(jnp.int32, (2 * ATT_BLOCK, 2 * ATT_BLOCK), 1)
    qi, kj = place(row), place(col)
    prev = jnp.logical_and(jnp.logical_and(col < ATT_BLOCK, kj >= qi), has_prev)
    cur = jnp.logical_and(col >= ATT_BLOCK, kj <= qi)
    return jnp.logical_or(prev, cur), lax.broadcasted_iota(jnp.int32, (1, 128), 1)


def _get(ref, sl):
    if len(ref.shape) == 2:
        return ref[:, sl]
    v = ref[:, :, sl]
    return v.reshape(ATT_BLOCK, v.shape[2])


def _put(ref, sl, val):
    if len(ref.shape) == 2:
        ref[:, sl] = val
    else:
        ref[:, :, sl] = val.reshape(ref.shape[0], ref.shape[1], val.shape[1])


def _att_spec(nb, dil, seg, width, col, back):
    bps = nb // dil

    def plain(n):
        return jnp.clip(n - back, 0, nb - 1), col

    def segmented(n):
        m = jnp.clip(n - back, 0, nb - 1)
        return 0, m // bps, m % bps, 0, col

    if seg == 1:
        return pl.BlockSpec((ATT_BLOCK, width), plain)
    return pl.BlockSpec((seg, None, None, ATT_BLOCK // seg, width), segmented)


def _att_shape(nb, dil, seg, width):
    t = nb * ATT_BLOCK
    return (t, width) if seg == 1 else (seg, dil, nb // dil, ATT_BLOCK // seg, width)


def _att_view(a, nb, dil, seg):
    return a.reshape(_att_shape(nb, dil, seg, a.shape[1]))


def _attn_fwd_block(q_ref, kc_ref, kp_ref, vc_ref, vp_ref, o_ref, l_ref, has_prev, seg):
    mask, lane = _att_mask(has_prev, seg)
    lo = lane < 64
    nq = ATT_BLOCK
    lse_all = jnp.zeros((nq, 128), F32)
    for hp in range(ATT_HEADS // 2):
        sl = slice(hp * 128, (hp + 1) * 128)
        q2 = _get(q_ref, sl)
        zero = jnp.zeros_like(q2)
        qs = jnp.concatenate([jnp.where(lo, q2, zero), jnp.where(lo, zero, q2)], axis=0)
        kk = jnp.concatenate([_get(kp_ref, sl), _get(kc_ref, sl)], axis=0)
        vv = jnp.concatenate([_get(vp_ref, sl), _get(vc_ref, sl)], axis=0)
        s = jnp.where(mask, _dot_nt(qs, kk) * 0.125, NEG)
        mx = jnp.max(s, axis=-1, keepdims=True)
        p = jnp.exp(s - mx)
        l = jnp.sum(p, axis=-1, keepdims=True)
        o = _dot(p.astype(BF16), vv) * (1.0 / l)
        _put(o_ref, sl, jnp.where(lo, o[:nq], o[nq:]).astype(BF16))
        lse = mx + jnp.log(l)
        lse_all = jnp.where(lane == 2 * hp, lse[:nq], lse_all)
        lse_all = jnp.where(lane == 2 * hp + 1, lse[nq:], lse_all)
    _put(l_ref, slice(None), lse_all)


def _attn_fwd(branches, name):
    t = branches[0][0].shape[0]
    nb = t // ATT_BLOCK
    nbr = len(branches)

    def body(*refs):
        n = pl.program_id(0)
        for i, (_, dil, seg) in enumerate(branches):
            _attn_fwd_block(*refs[5 * i:5 * i + 5], *refs[5 * nbr + 2 * i:5 * nbr + 2 * i + 2],
                            (n % (nb // dil)) != 0, seg)

    in_specs, args, out_specs, out_shape = [], [], [], []
    for qkv, dil, seg in branches:
        c0 = qkv.shape[1] // ATT_WIDTH - 3
        in_specs += [_att_spec(nb, dil, seg, ATT_WIDTH, c0 + j, back) for j, back in [(0, 0), (1, 0), (1, 1), (2, 0), (2, 1)]]
        args += [_att_view(qkv, nb, dil, seg)] * 5
        out_specs += [_att_spec(nb, dil, seg, ATT_WIDTH, 0, 0), _att_spec(nb, dil, seg, 128, 0, 0)]
        out_shape += [_sds(_att_shape(nb, dil, seg, ATT_WIDTH), BF16), _sds(_att_shape(nb, dil, seg, 128))]
    out = _call(body, name=name, grid=(nb,), in_specs=in_specs, out_specs=out_specs, out_shape=out_shape)(*args)
    return [(out[2 * i].reshape(t, ATT_WIDTH), out[2 * i + 1].reshape(t, 128)) for i in range(nbr)]


def _attn_combine(os_, ls_, name, tr=256):
    t = os_[0].shape[0]
    nbr = len(os_)

    def body(*refs):
        o_refs, l_refs, (a_ref, lt_ref) = refs[:nbr], refs[nbr:2 * nbr], refs[2 * nbr:]
        lane = lax.broadcasted_iota(jnp.int32, (1, 128), 1)
        ls = [r[...] for r in l_refs]
        mx = functools.reduce(jnp.maximum, ls)
        tot = mx + jnp.log(sum(jnp.exp(l - mx) for l in ls))
        lt_ref[...] = tot
        ws = [jnp.exp(l - tot) for l in ls]
        for hp in range(ATT_HEADS // 2):
            sl = slice(hp * 128, (hp + 1) * 128)
            acc = jnp.zeros((tr, 128), F32)
            for w, o_ref in zip(ws, o_refs):
                wf = jnp.where(lane < 64, w[:, 2 * hp:2 * hp + 1], w[:, 2 * hp + 1:2 * hp + 2])
                acc = acc + wf * o_ref[:, sl]
            a_ref[:, sl] = acc

    return _call(body, name=name, grid=(t // tr,),
                 in_specs=[_rows(tr, ATT_WIDTH)] * nbr + [_rows(tr, 128)] * nbr,
                 out_specs=[_rows(tr, ATT_WIDTH), _rows(tr, 128)],
                 out_shape=[_sds((t, ATT_WIDTH)), _sds((t, 128))])(*os_, *ls_)


def _attn_bwd_block(q_ref, kc_ref, kp_ref, vc_ref, vp_ref, do_ref, l_ref, d_ref, dq_ref, dkv_ref, carry, has_prev, seg):
    w = ATT_WIDTH
    nq = ATT_BLOCK
    mask, lane = _att_mask(has_prev, seg)
    lo = lane < 64
    lse, ddv = _get(l_ref, slice(None)), _get(d_ref, slice(None))
    for hp in range(ATT_HEADS // 2):
        sl = slice(hp * 128, (hp + 1) * 128)
        sv = slice(w + hp * 128, w + (hp + 1) * 128)
        q2, do2 = _get(q_ref, sl), _get(do_ref, sl)
        zero = jnp.zeros_like(q2)
        qs = jnp.concatenate([jnp.where(lo, q2, zero), jnp.where(lo, zero, q2)], axis=0)
        dos = jnp.concatenate([jnp.where(lo, do2, zero), jnp.where(lo, zero, do2)], axis=0)
        kk = jnp.concatenate([_get(kp_ref, sl), _get(kc_ref, sl)], axis=0)
        vv = jnp.concatenate([_get(vp_ref, sl), _get(vc_ref, sl)], axis=0)
        ls = jnp.concatenate([lse[:, 2 * hp:2 * hp + 1], lse[:, 2 * hp + 1:2 * hp + 2]], axis=0)
        dh = jnp.concatenate([ddv[:, 2 * hp:2 * hp + 1], ddv[:, 2 * hp + 1:2 * hp + 2]], axis=0)
        p = jnp.exp(jnp.where(mask, _dot_nt(qs, kk) * 0.125 - ls, NEG))
        ds = (p * (_dot_nt(dos, vv) - dh)).astype(BF16)
        dq = _dot(ds, kk) * 0.125
        _put(dq_ref, sl, jnp.where(lo, dq[:nq], dq[nq:]).astype(BF16))
        dk = _dot_tn(ds, qs) * 0.125
        dv = _dot_tn(p.astype(BF16), dos)
        _put(dkv_ref, sl, (carry[:, sl] + dk[:nq]).astype(BF16))
        _put(dkv_ref, sv, (carry[:, sv] + dv[:nq]).astype(BF16))
        carry[:, sl] = dk[nq:]
        carry[:, sv] = dv[nq:]


def _attn_bwd(branches, name):
    t = branches[0][0].shape[0]
    nb = t // ATT_BLOCK
    nbr = len(branches)
    w = ATT_WIDTH

    def body(*refs):
        ins, outs, carries = refs[:8 * nbr], refs[8 * nbr:10 * nbr], refs[10 * nbr:]
        n = pl.program_id(0)

        @pl.when(n == 0)
        def _():
            for carry in carries:
                carry[...] = jnp.zeros_like(carry)

        @pl.when(n < nb)
        def _():
            for i, branch in enumerate(branches):
                dil, seg = branch[4:]
                _attn_bwd_block(*ins[8 * i:8 * i + 8], *outs[2 * i:2 * i + 2], carries[i], (n % (nb // dil)) != 0, seg)

        @pl.when(n == nb)
        def _():
            for i in range(nbr):
                _put(outs[2 * i + 1], slice(None), carries[i][...].astype(BF16))

    in_specs, args, out_specs, out_shape = [], [], [], []
    for qkv, dout, lse, dd, dil, seg in branches:
        c0 = qkv.shape[1] // w - 3
        in_specs += [_att_spec(nb, dil, seg, w, c0 + j, back) for j, back in [(0, 0), (1, 0), (1, 1), (2, 0), (2, 1)]]
        in_specs += [_att_spec(nb, dil, seg, w, 0, 0), _att_spec(nb, dil, seg, 128, 0, 0), _att_spec(nb, dil, seg, 128, 0, 0)]
        args += [_att_view(a, nb, dil, seg) for a in [qkv] * 5 + [dout, lse, dd]]
        out_specs += [_att_spec(nb, dil, seg, w, 0, 0), _att_spec(nb, dil, seg, 2 * w, 0, 1)]
        out_shape += [_sds(_att_shape(nb, dil, seg, w), BF16), _sds(_att_shape(nb, dil, seg, 2 * w), BF16)]
    out = _call(body, name=name, grid=(nb + 1,), in_specs=in_specs, out_specs=out_specs, out_shape=out_shape,
                scratch_shapes=[pltpu.VMEM((ATT_BLOCK, 2 * w), F32)] * nbr)(*args)
    return [(out[2 * i].reshape(t, w), out[2 * i + 1].reshape(t, 2 * w)) for i in range(nbr)]


def _local_step(x, tgt, mod, norm1_g, lb_logits, og, ag, norm2_g, fg, get_w, put_g, late=lambda a: a):
    shift1, scale1, gate1, shift2, scale2, gate2 = [mod[:, i * D_MODEL:(i + 1) * D_MODEL] for i in range(6)]
    fg = fg.reshape(1, D_MODEL)

    h1 = _norm_mod(x, norm1_g, scale1, shift1, "norm_mod1")
    w_in = get_w("w_in", h1)
    proj = _mm_nn(h1, w_in, "mm_in", out_dtype=BF16)
    o_hg, states = _hgrn_fwd(proj, lb_logits, "hgrn_fwd")
    fine = DILATIONS[-1]
    layouts = [(d, 1 if d == 1 else fine // d) for d in DILATIONS]
    qkv_fine = _to_sub(proj[:, 4 * HG_WIDTH:], fine)
    qkvs = [proj if d == 1 else qkv_fine for d in DILATIONS]
    natural = lambda a, d: a if d == 1 else _from_sub(a, fine)
    outs = _attn_fwd([(q, d, seg) for q, (d, seg) in zip(qkvs, layouts)], "attn_fwd")
    att, lse = _attn_combine([natural(o, d) for (o, _), d in zip(outs, DILATIONS)],
                             [natural(l, d) for (_, l), d in zip(outs, DILATIONS)], "attn_combine")
    mixin = _mix_in(o_hg, proj, att, og, ag, "mix_in")
    w_out = get_w("w_out", mixin)
    mix = _mm_nn(mixin, w_out, "mm_out")
    x2, h2 = _resid_norm_mod(x, mix, gate1, norm2_g, scale2, shift2, "resid_norm_mod2")
    w_gu = get_w("w_gu", h2)
    a_ff, u_ff, act = _mm_gate_up(h2, w_gu, "mm_gu")
    w_down = get_w("w_down", act)
    ffn = _mm_nn(act, w_down, "mm_down")
    dx3, dffn, loss_v, dfg, dgate2 = _final_loss(x2, ffn, gate2, fg, tgt, "final_loss")

    dffn = put_g("w_down", *_mm_tn(act, dffn, 1, "mm_down_dw", tm=2048, tk=D_FF // 2), dffn)
    dau = _mm_down_dx(dffn, w_down, a_ff, u_ff, "mm_down_dx")
    dau = put_g("w_gu", *_mm_tn(h2, dau, N_SHARD, "mm_gu_dw", tm=x.shape[0], tk=512), dau)
    dh2 = _mm_nt(dau, w_gu, "mm_gu_dx")
    dx2, dshift2, dscale2, dg2, dgate1, dmix = _norm_mod_bwd(
        dh2, x2, norm2_g, scale2, dx3, "norm_mod2_bwd", gate=gate1, mix=mix)
    dmix = put_g("w_out", *_mm_tn(mixin, dmix, 1, "mm_out_dw", tm=x.shape[0], tk=512), dmix)
    dmixin = _mm_nt(dmix, w_out, "mm_out_dx", tm=1024)
    do_hg, dg_raw, datt, dd, dog, dag = _mix_in_bwd(dmixin, o_hg, proj, att, og, ag, "mix_in_bwd")
    datt_b = datt
    reordered = [_to_sub(a, fine) for a in (datt_b, lse, dd)]
    datts = _attn_bwd([(q,) + tuple((datt_b, lse, dd) if d == 1 else reordered) + (d, seg)
                       for q, (d, seg) in zip(qkvs, layouts)], "attn_bwd")
    dhg, dlb = _hgrn_bwd(proj, lb_logits, states, do_hg, "hgrn_bwd")
    dhg = late(dhg)
    dproj = _dproj(dhg, dg_raw, [natural(dq, d) for (dq, _), d in zip(datts, DILATIONS)],
                   [natural(dkv, d) for (_, dkv), d in zip(datts, DILATIONS)], "dproj")
    dproj = put_g("w_in", *_mm_tn(h1, dproj, N_SHARD, "mm_in_dw", tm=x.shape[0], tk=512), dproj)
    dh1 = _mm_nt(dproj, w_in, "mm_in_dx", tm=1024)
    dx, dshift1, dscale1, dg1 = _norm_mod_bwd(dh1, x, norm1_g, scale1, dx2, "norm_mod1_bwd")

    stats = jnp.concatenate([loss_v, dfg, dg2, dg1, dlb, dag, dog,
                             dshift1, dscale1, dgate1, dshift2, dscale2, dgate2], axis=1)
    return dx, stats


def _place():
    x, y, c = lax.axis_index("x"), lax.axis_index("y"), lax.axis_index("c")
    return x, y, c


def _chip_peers(x, y, c):
    return [(1 - x, y, c), (x, 1 - y, c), (1 - x, 1 - y, c)]


def _comm_call(body, name, n_in, out_shape, scratch_shapes):
    hbm = pl.BlockSpec(memory_space=pl.ANY)
    return pl.pallas_call(body, name=name, in_specs=[hbm] * n_in, out_specs=[hbm] * len(out_shape),
                          out_shape=out_shape, scratch_shapes=scratch_shapes)


_HBM = pl.BlockSpec(memory_space=pltpu.HBM)
_SEM = pl.BlockSpec(memory_space=pltpu.SEMAPHORE)
_EFFECT = pltpu.SideEffectType.DATAFLOW_SIDE_EFFECTING


def _exchange_copy(bufs, send, recv, j, peer, place, kind):
    x, y, c = place
    target = peer
    if kind == "gather":
        src = dst = bufs[0].at[2 * x + y]
    elif kind == "scatter":
        src, dst = bufs[0].at[2 * peer[0] + peer[1]], bufs[1].at[j]
    else:
        half = bufs[0].shape[1] // 2
        rows = pl.ds(c * half, half)
        if kind == "half":
            src = dst = bufs[0].at[2 * x + y, rows]
        else:
            src = dst = bufs[0].at[2 * peer[0] + peer[1], rows]
            target = (x, y, 1 - c)
    return pltpu.make_async_remote_copy(src_ref=src, dst_ref=dst, send_sem=send.at[j], recv_sem=recv.at[j],
                                        device_id=target, device_id_type=MESH)


def _exchange_start(groups, after, kind, name):
    sizes = [len(g) for g in groups]
    flat = [b for g in groups for b in g]
    ng, nb = len(groups), len(flat)

    def body(*refs):
        bufs, sems = refs[:nb], refs[nb + 1:nb + 1 + 2 * ng]
        x, y, c = _place()
        for j, peer in enumerate(_chip_peers(x, y, c)):
            at = 0
            for i, size in enumerate(sizes):
                _exchange_copy(bufs[at:at + size], sems[2 * i], sems[2 * i + 1], j, peer, (x, y, c), kind).start()
                at += size

    any_space = pl.BlockSpec(memory_space=pl.ANY)
    out = pl.pallas_call(
        body, name=name, in_specs=[_HBM] * nb + [any_space],
        out_specs=[_SEM] * (2 * ng) + [_HBM] * nb + [any_space],
        out_shape=[pltpu.SemaphoreType.DMA((3,))] * (2 * ng) + [pltpu.HBM(b.shape, b.dtype) for b in flat]
        + [_sds(after.shape, after.dtype)],
        input_output_aliases={i: 2 * ng + i for i in range(nb + 1)},
        compiler_params=pltpu.CompilerParams(has_side_effects=_EFFECT),
    )(*[pltpu.with_memory_space_constraint(b, pltpu.HBM) for b in flat], after)
    started, at = [], 2 * ng
    for i, size in enumerate(sizes):
        started.append((out[2 * i], out[2 * i + 1], tuple(out[at:at + size])))
        at += size
    return started, out[-1]


def _exchange_wait(started, after, kind, name):
    send, recv, bufs = started
    nb = len(bufs)

    def body(*refs):
        x, y, c = _place()
        for j, peer in enumerate(_chip_peers(x, y, c)):
            cp = _exchange_copy(refs[:nb], refs[nb], refs[nb + 1], j, peer, (x, y, c), kind)
            cp.wait_send()
            cp.wait_recv()

    return pl.pallas_call(
        body, name=name, in_specs=[_HBM] * nb + [_SEM, _SEM, pl.BlockSpec(memory_space=pl.ANY)],
        out_specs=[_HBM] * nb, out_shape=[pltpu.HBM(b.shape, b.dtype) for b in bufs],
        input_output_aliases={i: i for i in range(nb)},
        compiler_params=pltpu.CompilerParams(has_side_effects=_EFFECT),
    )(*bufs, send, recv, after)


def _sibling_copies(v_refs, l_refs, send, recv):
    x, y, c = _place()
    return [pltpu.make_async_remote_copy(src_ref=v, dst_ref=l, send_sem=send.at[a], recv_sem=recv.at[a],
                                         device_id=(x, y, 1 - c), device_id_type=MESH)
            for a, (v, l) in enumerate(zip(v_refs, l_refs))]


def _sibling_start(vs, after, name):
    n = len(vs)
    lands = [lax.empty(v.shape, v.dtype) for v in vs]

    def body(*refs):
        for cp in _sibling_copies(refs[:n], refs[n:2 * n], refs[2 * n + 1], refs[2 * n + 2]):
            cp.start()

    any_space = pl.BlockSpec(memory_space=pl.ANY)
    out = pl.pallas_call(
        body, name=name, in_specs=[_HBM] * (2 * n) + [any_space],
        out_specs=[_SEM, _SEM] + [_HBM] * (2 * n) + [any_space],
        out_shape=[pltpu.SemaphoreType.DMA((n,))] * 2 + [pltpu.HBM(b.shape, b.dtype) for b in vs + lands]
        + [_sds(after.shape, after.dtype)],
        input_output_aliases={i: 2 + i for i in range(2 * n + 1)},
        compiler_params=pltpu.CompilerParams(has_side_effects=_EFFECT),
    )(*[pltpu.with_memory_space_constraint(b, pltpu.HBM) for b in vs + lands], after)
    return (out[0], out[1], tuple(out[2:2 + n]), tuple(out[2 + n:2 + 2 * n])), out[-1]


def _sibling_wait(started, after, name):
    send, recv, vs, lands = started
    n = len(vs)

    def body(*refs):
        for cp in _sibling_copies(refs[:n], refs[n:2 * n], refs[2 * n], refs[2 * n + 1]):
            cp.wait_send()
            cp.wait_recv()

    out = pl.pallas_call(
        body, name=name, in_specs=[_HBM] * (2 * n) + [_SEM, _SEM, pl.BlockSpec(memory_space=pl.ANY)],
        out_specs=[_HBM] * (2 * n), out_shape=[pltpu.HBM(b.shape, b.dtype) for b in vs + lands],
        input_output_aliases={i: i for i in range(2 * n)},
        compiler_params=pltpu.CompilerParams(has_side_effects=_EFFECT),
    )(*vs, *lands, send, recv, after)
    return out[:n], out[n:]


def _swap_sibling(vs, name):
    n = len(vs)

    def body(*refs):
        v_refs, o_refs, (send, recv) = refs[:n], refs[n:2 * n], refs[2 * n:]
        x, y, c = _place()
        cps = [pltpu.make_async_remote_copy(
            src_ref=v_refs[a], dst_ref=o_refs[a], send_sem=send.at[a], recv_sem=recv.at[a],
            device_id=(x, y, 1 - c), device_id_type=MESH) for a in range(n)]
        for cp in cps:
            cp.start()
        for cp in cps:
            cp.wait()

    return _comm_call(body, name, n, [_sds(v.shape, v.dtype) for v in vs],
                      [pltpu.SemaphoreType.DMA((n,)), pltpu.SemaphoreType.DMA((n,))])(*vs)


def _gather_rows(v, name):
    r, n = v.shape

    def body(v_ref, o_ref, send, recv, loc):
        x, y, c = _place()
        me = 4 * x + 2 * y + c
        own = pltpu.make_async_copy(v_ref, o_ref.at[me], loc)
        own.start()
        peers = []
        for k in range(1, 8):
            px = 1 - x if k & 4 else x
            py = 1 - y if k & 2 else y
            pc = 1 - c if k & 1 else c
            peers.append((px, py, pc))
        sends = []
        for k, peer in enumerate(peers):
            cp = pltpu.make_async_remote_copy(src_ref=v_ref, dst_ref=o_ref.at[me], send_sem=send.at[k],
                                              recv_sem=recv.at[k], device_id=peer, device_id_type=MESH)
            cp.start()
            sends.append(cp)
        for k, peer in enumerate(peers):
            pltpu.make_async_remote_copy(src_ref=v_ref, dst_ref=o_ref.at[4 * peer[0] + 2 * peer[1] + peer[2]],
                                         send_sem=send.at[k], recv_sem=recv.at[k], device_id=peer,
                                         device_id_type=MESH).wait_recv()
        for cp in sends:
            cp.wait_send()
        own.wait()

    vmem = pl.BlockSpec(memory_space=pltpu.VMEM)
    return pl.pallas_call(body, name=name, in_specs=[vmem], out_specs=vmem, out_shape=_sds((8, r, n), v.dtype),
                          scratch_shapes=[pltpu.SemaphoreType.DMA((7,)), pltpu.SemaphoreType.DMA((7,)),
                                          pltpu.SemaphoreType.DMA])(v)


def _cast_place(ws, shard, name):
    n = len(ws)

    def body(s_ref, *refs):
        for w_ref, o_ref in zip(refs[:n], refs[n:]):
            o_ref[0] = w_ref[...].astype(BF16)

    return pl.pallas_call(
        body, name=name, out_shape=[_sds((N_SHARD,) + w.shape, BF16) for w in ws],
        grid_spec=pltpu.PrefetchScalarGridSpec(
            num_scalar_prefetch=1, grid=(4,),
            in_specs=[pl.BlockSpec((w.shape[0] // 4, w.shape[1]), lambda i, s: (i, 0)) for w in ws],
            out_specs=[pl.BlockSpec((1, w.shape[0] // 4, w.shape[1]), lambda i, s: (s[0], i, 0)) for w in ws]),
        compiler_params=pltpu.CompilerParams(dimension_semantics=("arbitrary",), vmem_limit_bytes=VMEM_LIMIT),
    )(shard.reshape(1).astype(jnp.int32), *ws)


def _mod_part(c_all, w_ada, b_ada, name):
    n = w_ada.shape[1]

    def body(c_ref, w_ref, b_ref, a_ref, p_ref):
        cv = c_ref[...]
        ca = cv * _sigmoid(cv)
        a_ref[...] = ca
        p_ref[...] = jnp.dot(ca, w_ref[...], precision=lax.Precision.HIGHEST, preferred_element_type=F32) + b_ref[...]

    full = lambda a: pl.BlockSpec(a.shape, lambda i: (0, 0))
    return _call(body, name=name, grid=(1,), in_specs=[full(c_all), full(w_ada), full(b_ada)],
                 out_specs=[pl.BlockSpec((8, D_MODEL), lambda i: (0, 0)), pl.BlockSpec((8, n), lambda i: (0, 0))],
                 out_shape=[_sds((8, D_MODEL)), _sds((8, n))])(c_all, w_ada, b_ada)


def _sum_received(g, shard, land, name):
    _, r, c = g.shape
    tr = r // 4

    def body(s_ref, g_ref, l_ref, o_ref):
        o_ref[...] = ((g_ref[0] + l_ref[0].astype(F32)) + l_ref[1].astype(F32)) + l_ref[2].astype(F32)

    return pl.pallas_call(
        body, name=name, out_shape=_sds((r, c)),
        grid_spec=pltpu.PrefetchScalarGridSpec(
            num_scalar_prefetch=1, grid=(4,),
            in_specs=[pl.BlockSpec((1, tr, c), lambda i, s: (s[0], i, 0)),
                      pl.BlockSpec((3, tr, c), lambda i, s: (0, i, 0))],
            out_specs=pl.BlockSpec((tr, c), lambda i, s: (i, 0))),
        compiler_params=pltpu.CompilerParams(dimension_semantics=("arbitrary",), vmem_limit_bytes=VMEM_LIMIT),
    )(shard.reshape(1).astype(jnp.int32), g, land)


def _adamw_outer(w, ct, dm, m, v, name):
    k, n = w.shape
    tr = k // 4

    def body(w_ref, c_ref, d_ref, m_ref, v_ref, g_out, d_out, m_out, v_out):
        cv = c_ref[...]
        dv = d_ref[...]
        g = cv[:, 0:1] * dv[0:1, :]
        for i in range(1, 8):
            g = g + cv[:, i:i + 1] * dv[i:i + 1, :]
        g_out[...] = g
        d_out[...], m_out[...], v_out[...] = _adamw_math(w_ref[...], g, m_ref[...], v_ref[...])

    row = _rows(tr, n)
    return _call(body, name=name, grid=(4,),
                 in_specs=[row, _rows(tr, 8), pl.BlockSpec((8, n), lambda i: (0, 0)), row, row],
                 out_specs=[row] * 4, out_shape=[_sds((k, n))] * 4)(w, ct, dm, m, v)


def _adamw_math(w, g, m, v):
    m_new = ADAM_B1 * m + (1.0 - ADAM_B1) * g
    v_new = ADAM_B2 * v + (1.0 - ADAM_B2) * (g * g)
    m_hat = m_new / (1.0 - ADAM_B1 ** ADAM_STEP)
    v_hat = v_new / (1.0 - ADAM_B2 ** ADAM_STEP)
    return -ADAM_LR * (m_hat / (jnp.sqrt(v_hat) + ADAM_EPS) + ADAM_WD * w), m_new, v_new


def _small_update(stats, smalls, name):
    offsets = [ST_DMOD, ST_DG1, ST_DLB, ST_DOG, ST_DAG, ST_DG2, ST_DFG]
    lb_index = 2

    def body(*refs):
        s_ref, ins, l_ref, outs = refs[0], refs[1:22], refs[22], refs[23:]
        tot = s_ref[0:1, :]
        for i in range(1, 8):
            tot = tot + s_ref[i:i + 1, :]
        l_ref[...] = jnp.zeros((1, 128), F32) + (0.5 / D_MODEL) * jnp.sum(tot[:, ST_LOSS:ST_LOSS + D_MODEL])
        for p, off in enumerate(offsets):
            w_ref, m_ref, v_ref = ins[3 * p:3 * p + 3]
            g_out, d_out, m_out, v_out = outs[4 * p:4 * p + 4]
            g = tot[:, off:off + w_ref.shape[1]]
            if p == lb_index:
                lg = w_ref[...]
                lb = _sigmoid(lg[0:1] - lg[1:2])
                g = g * lb * (1.0 - lb)
            for r in range(w_ref.shape[0]):
                rows = slice(r, r + 1)
                gr = g if r == 0 else -g
                delta, m_new, v_new = _adamw_math(w_ref[rows, :], gr, m_ref[rows, :], v_ref[rows, :])
                g_out[rows, :] = gr
                d_out[rows, :] = delta
                m_out[rows, :] = m_new
                v_out[rows, :] = v_new

    full = lambda a: pl.BlockSpec(a.shape, lambda i: (0, 0))
    flat = [a for t in smalls for a in t]
    return _call(body, name=name, grid=(1,),
                 in_specs=[full(stats)] + [full(a) for a in flat],
                 out_specs=[pl.BlockSpec((1, 128), lambda i: (0, 0))] + [full(t[0]) for t in smalls for _ in range(4)],
                 out_shape=[_sds((1, 128))] + [_sds(t[0].shape) for t in smalls for _ in range(4)])(stats, *flat)


def _adamw(w, gs, m, v, name, steps=4):
    r, c = w.shape
    tr = r // steps
    ng = len(gs)

    def body(*refs):
        w_ref, g_refs, (m_ref, v_ref, g_out, d_out, m_out, v_out) = refs[0], refs[1:1 + ng], refs[1 + ng:]
        g = g_refs[0][...]
        for g_ref in g_refs[1:]:
            g = g + g_ref[...]
        g_out[...] = g
        d_out[...], m_out[...], v_out[...] = _adamw_math(w_ref[...], g, m_ref[...], v_ref[...])

    row = _rows(tr, c)
    return _call(body, name=name, grid=(steps,), in_specs=[row] * (3 + ng), out_specs=[row] * 4,
                 out_shape=[_sds((r, c))] * 4)(w, *gs, m, v)


def kernel(x, c, w_ada, b_ada, norm1_g, w_in, hg_lb_logits, hg_onorm_g, att_onorm_g, w_out, norm2_g, w_gate_up, w_down, final_g, loss_target, m_w_ada, m_b_ada, m_norm1_g, m_w_in, m_hg_lb_logits, m_hg_onorm_g, m_att_onorm_g, m_w_out, m_norm2_g, m_w_gate_up, m_w_down, m_final_g, v_w_ada, v_b_ada, v_norm1_g, v_w_in, v_hg_lb_logits, v_hg_onorm_g, v_att_onorm_g, v_w_out, v_norm2_g, v_w_gate_up, v_w_down, v_final_g):
    ix, iy, ic = _place()
    shard = 2 * ix + iy
    sample = 4 * ix + 2 * iy + ic
    n_ada = w_ada.shape[2]

    shards = [w_in[0], w_out[0], w_gate_up[0], w_down[0]]
    names = ["w_in", "w_out", "w_gu", "w_down"]
    shapes = [(N_SHARD,) + w.shape for w in shards]
    placed = [(_cast_place(shards[:1], shard, "place_w_in")[0],)]
    placed += [(p,) for p in _cast_place(shards[1:], shard, "place_rest")]

    c_all = _gather_rows(c, "gather_c").reshape(8, D_MODEL)
    b_part = lax.dynamic_slice(b_ada, (0, shard * n_ada), (1, n_ada))
    c_act, part = _mod_part(c_all, w_ada[0], b_part, "mod_part")
    parts = _gather_rows(part, "gather_mod")[::2]
    mod = lax.dynamic_index_in_dim(parts, sample, axis=1, keepdims=False).reshape(1, 6 * D_MODEL)
    (first,), mod = _exchange_start(placed[:1], mod, "half", "gather_start_w_in")
    gathering = {}

    def get_w(name, after):
        if name == "w_in":
            halves = _exchange_wait(first, after, "half", "gather_wait_w_in")
            (passing,), token = _exchange_start([tuple(halves)], mod, "forward", "forward_start_w_in")
            (full,) = _exchange_wait(passing, token, "forward", "forward_wait_w_in")
            rest, full = _exchange_start(placed[1:], full, "gather", "gather_start_rest")
            gathering.update(zip(names[1:], rest))
            return full
        (full,) = _exchange_wait(gathering[name], after, "gather", "gather_wait_" + name)
        return full if name == "w_gu" else full.reshape(1, -1, D_MODEL)

    scattering = {}

    def put_g(name, g, g_bf16, then):
        shape = shapes[names.index(name)]
        land = lax.empty((3,) + shape[1:], BF16)
        (started,), then = _exchange_start([(g_bf16.reshape(shape), land)], then, "scatter", "scatter_start_" + name)
        scattering[name] = (g.reshape(shape), started)
        return then

    def summed(name, after):
        g, started = scattering[name]
        _, land = _exchange_wait(started, after, "scatter", "scatter_wait_" + name)
        return _sum_received(g, shard, land, "sum_" + name)

    early = ["w_down", "w_gu", "w_out"]
    swapping = []

    def late(a):
        started, a = _sibling_start([summed(nm, a) for nm in early], a, "swap_start")
        swapping.append(started)
        return a

    dx, stats = _local_step(x[0], loss_target[0], mod, norm1_g, hg_lb_logits, hg_onorm_g, att_onorm_g,
                            norm2_g, final_g, get_w, put_g, late)

    stats_all = _gather_rows(stats, "gather_stats").reshape(8, ST_WIDTH)
    dmod = lax.dynamic_slice(stats_all, (0, ST_DMOD + shard * n_ada), (8, n_ada))

    as_row = lambda a: a.reshape(1, -1) if a.ndim == 1 else a
    smalls = [tuple(as_row(a) for a in t) for t in [
        (b_ada, m_b_ada, v_b_ada), (norm1_g, m_norm1_g, v_norm1_g),
        (hg_lb_logits, m_hg_lb_logits, v_hg_lb_logits), (hg_onorm_g, m_hg_onorm_g, v_hg_onorm_g),
        (att_onorm_g, m_att_onorm_g, v_att_onorm_g), (norm2_g, m_norm2_g, v_norm2_g),
        (final_g, m_final_g, v_final_g)]]
    loss, *small_out = _small_update(stats_all, smalls, "small_update")
    shapes_out = [b_ada.shape, norm1_g.shape, hg_lb_logits.shape, hg_onorm_g.shape, att_onorm_g.shape,
                  norm2_g.shape, final_g.shape]
    sg, sd, sm, sv = [[small_out[4 * p + i].reshape(shapes_out[p]) for p in range(7)] for i in range(4)]

    ada = _adamw_outer(w_ada[0], c_act.T, dmod, m_w_ada[0], v_w_ada[0], "adamw_w_ada")
    moments = [(m_w_in, v_w_in), (m_w_out, v_w_out), (m_w_gate_up, v_w_gate_up), (m_w_down, v_w_down)]

    def update(group, sums, other):
        return {nm: _adamw(shards[names.index(nm)], [s, o], moments[names.index(nm)][0][0],
                           moments[names.index(nm)][1][0], "adamw_" + nm) for nm, s, o in zip(group, sums, other)}

    sums, other = _sibling_wait(swapping[0], ada[1], "swap_wait")
    done = update(early, sums, other)
    sum_in = summed("w_in", done["w_out"][1])
    done.update(update(["w_in"], [sum_in], _swap_sibling([sum_in], "swap_sum_in")))
    big = [ada] + [done[nm] for nm in names]
    bg, bd, bm, bv = [[t[i][None] for t in big] for i in range(4)]

    def order(b, s):
        return [b[0], s[0], s[1], b[1], s[2], s[3], s[4], b[2], s[5], b[3], b[4], s[6]]

    return (loss[0, 0], dx[None], *order(bg, sg), *order(bd, sd), *order(bm, sm), *order(bv, sv))
```

```python
import functools

import jax
import jax.numpy as jnp
from jax import lax
from jax.experimental import pallas as pl
from jax.experimental.pallas import tpu as pltpu

F32 = jnp.float32
BF16 = jnp.bfloat16
MESH = pl.DeviceIdType.MESH

D_MODEL = 1024
HG_WIDTH = 512
HG_HEAD = 128
HG_CHUNK = 64
HG_GROUP = 4
ATT_WIDTH = 512
ATT_HEADS = 8
ATT_BLOCK = 128
DILATIONS = (1, 4, 16)
D_FF = 2816
IN_WIDTH = 3584
N_SHARD = 4
RMS_EPS = 1e-6
NEG = -1e30

ADAM_LR = 0.001
ADAM_B1 = 0.9
ADAM_B2 = 0.999
ADAM_EPS = 1e-08
ADAM_WD = 0.01
ADAM_STEP = 10

VMEM_LIMIT = 56 * 2**20

ST_LOSS, ST_DFG, ST_DG2, ST_DG1 = 0, 1024, 2048, 3072
ST_DLB, ST_DAG, ST_DOG, ST_DMOD = 4096, 4608, 5120, 5248
ST_WIDTH = 5248 + 6144
SP_BADA, SP_N1, SP_LB, SP_OG, SP_AG, SP_N2, SP_FG = 0, 6144, 7168, 8192, 8320, 8832, 9856
SP_WIDTH = 10880


def _call(body, *, name, grid, in_specs, out_specs, out_shape, scratch_shapes=()):
    return pl.pallas_call(
        body, name=name, grid=grid, in_specs=in_specs, out_specs=out_specs, out_shape=out_shape,
        scratch_shapes=list(scratch_shapes),
        compiler_params=pltpu.CompilerParams(
            dimension_semantics=("arbitrary",) * len(grid), vmem_limit_bytes=VMEM_LIMIT))


def _sds(shape, dtype=F32):
    return jax.ShapeDtypeStruct(shape, dtype)


def _dot(a, b):
    return jnp.dot(a, b, preferred_element_type=F32)


def _dot_nt(a, b):
    return lax.dot_general(a, b, (((1,), (1,)), ((), ())), preferred_element_type=F32)


def _dot_tn(a, b):
    return lax.dot_general(a, b, (((0,), (0,)), ((), ())), preferred_element_type=F32)


def _sigmoid(x):
    return 1.0 / (1.0 + jnp.exp(-x))


def _rows(tr, width):
    return pl.BlockSpec((tr, width), lambda i: (i, 0))


def _vec(width):
    return pl.BlockSpec((1, width), lambda i: (0, 0))


def _acc(ref, val, first):
    @pl.when(first)
    def _():
        ref[...] = val

    @pl.when(jnp.logical_not(first))
    def _():
        ref[...] += val


def _mm_nn(a, b3, name, tm=1024, out_dtype=F32):
    m, k = a.shape
    s, _, n = b3.shape

    def body(a_ref, b_ref, o_ref):
        o_ref[...] = _dot(a_ref[...], b_ref[0]).astype(out_dtype)

    return _call(
        body, name=name, grid=(s, m // tm),
        in_specs=[pl.BlockSpec((tm, k), lambda j, i: (i, 0)), pl.BlockSpec((1, k, n), lambda j, i: (j, 0, 0))],
        out_specs=pl.BlockSpec((tm, n), lambda j, i: (i, j)), out_shape=_sds((m, s * n), out_dtype))(a, b3)


def _mm_nt(dy, b3, name, tm=512):
    m = dy.shape[0]
    s, k, n = b3.shape

    def body(dy_ref, b_ref, o_ref):
        acc = _dot_nt(dy_ref[:, 0:n], b_ref[0])
        for j in range(1, s):
            acc = acc + _dot_nt(dy_ref[:, j * n:(j + 1) * n], b_ref[j])
        o_ref[...] = acc.astype(BF16)

    return _call(
        body, name=name, grid=(m // tm,),
        in_specs=[_rows(tm, s * n), pl.BlockSpec((s, k, n), lambda i: (0, 0, 0))],
        out_specs=_rows(tm, k), out_shape=_sds((m, k), BF16))(dy, b3)


def _mm_tn(a, dy, s, name, tm, tk):
    m, k = a.shape
    n = dy.shape[1] // s
    steps = m // tm

    def body(a_ref, dy_ref, o_ref, ob_ref):
        p = _dot_tn(a_ref[...], dy_ref[...])[None]
        if steps == 1:
            o_ref[...] = p
            ob_ref[...] = p.astype(BF16)
        else:
            i = pl.program_id(2)
            _acc(o_ref, p, i == 0)

            @pl.when(i == steps - 1)
            def _():
                ob_ref[...] = o_ref[...].astype(BF16)

    out = pl.BlockSpec((1, tk, n), lambda kk, j, i: (j, kk, 0))
    return _call(
        body, name=name, grid=(k // tk, s, steps),
        in_specs=[pl.BlockSpec((tm, tk), lambda kk, j, i: (i, kk)), pl.BlockSpec((tm, n), lambda kk, j, i: (i, j))],
        out_specs=[out, out], out_shape=[_sds((s, k, n)), _sds((s, k, n), BF16)])(a, dy)


def _rms(x):
    return lax.rsqrt(jnp.mean(x * x, axis=-1, keepdims=True) + RMS_EPS)


def _rms_bwd(dxh, xh, r):
    return r * (dxh - xh * jnp.mean(dxh * xh, axis=-1, keepdims=True))


def _norm_mod(x, g, scale, shift, name, tr=512):
    t = x.shape[0]

    def body(x_ref, g_ref, sc_ref, sh_ref, h_ref):
        xv = x_ref[...]
        n = xv * _rms(xv) * g_ref[...]
        h_ref[...] = (n * (1.0 + sc_ref[...]) + sh_ref[...]).astype(BF16)

    return _call(body, name=name, grid=(t // tr,),
                 in_specs=[_rows(tr, D_MODEL), _vec(D_MODEL), _vec(D_MODEL), _vec(D_MODEL)],
                 out_specs=_rows(tr, D_MODEL), out_shape=_sds((t, D_MODEL), BF16))(x, g, scale, shift)


def _mix_in(o_hg, proj, att, og, ag, name, tr=512):
    t = o_hg.shape[0]

    def body(o_ref, g_ref, a_ref, og_ref, ag_ref, m_ref):
        for h in range(HG_WIDTH // HG_HEAD):
            sl = slice(h * HG_HEAD, (h + 1) * HG_HEAD)
            oh = o_ref[:, sl].astype(F32)
            gv = g_ref[:, sl].astype(F32)
            m_ref[:, sl] = (oh * _rms(oh) * og_ref[...] * (gv * _sigmoid(gv))).astype(BF16)
        av = a_ref[...]
        m_ref[:, HG_WIDTH:] = (av * _rms(av) * ag_ref[...]).astype(BF16)

    return _call(body, name=name, grid=(t // tr,),
                 in_specs=[_rows(tr, HG_WIDTH), pl.BlockSpec((tr, HG_WIDTH), lambda i: (i, 3)), _rows(tr, ATT_WIDTH),
                           _vec(HG_HEAD), _vec(ATT_WIDTH)],
                 out_specs=_rows(tr, D_MODEL), out_shape=_sds((t, D_MODEL), BF16))(o_hg, proj, att, og, ag)


def _resid_norm_mod(x, mix, gate, g, scale, shift, name, tr=512):
    t = x.shape[0]

    def body(x_ref, m_ref, gt_ref, g_ref, sc_ref, sh_ref, x2_ref, h_ref):
        x2 = x_ref[...] + gt_ref[...] * m_ref[...]
        x2_ref[...] = x2
        n = x2 * _rms(x2) * g_ref[...]
        h_ref[...] = (n * (1.0 + sc_ref[...]) + sh_ref[...]).astype(BF16)

    return _call(body, name=name, grid=(t // tr,),
                 in_specs=[_rows(tr, D_MODEL), _rows(tr, D_MODEL)] + [_vec(D_MODEL)] * 4,
                 out_specs=[_rows(tr, D_MODEL), _rows(tr, D_MODEL)],
                 out_shape=[_sds((t, D_MODEL)), _sds((t, D_MODEL), BF16)])(x, mix, gate, g, scale, shift)


def _mm_gate_up(h, w_gu, name, tm=1024):
    m, k = h.shape
    n = w_gu.shape[2]

    def body(h_ref, wa_ref, wu_ref, da_ref, du_ref, o_ref):
        hv = h_ref[...]
        a = _dot(hv, wa_ref[0])
        u = _dot(hv, wu_ref[0])
        sg = _sigmoid(a)
        silu = a * sg
        da_ref[...] = (u * sg * (1.0 + a * (1.0 - sg))).astype(BF16)
        du_ref[...] = silu.astype(BF16)
        o_ref[...] = (silu * u).astype(BF16)

    out = pl.BlockSpec((tm, n), lambda j, i: (i, j))
    return _call(body, name=name, grid=(2, m // tm),
                 in_specs=[pl.BlockSpec((tm, k), lambda j, i: (i, 0)), pl.BlockSpec((1, k, n), lambda j, i: (j, 0, 0)),
                           pl.BlockSpec((1, k, n), lambda j, i: (j + 2, 0, 0))],
                 out_specs=[out, out, out], out_shape=[_sds((m, 2 * n), BF16)] * 3)(h, w_gu, w_gu)


def _mm_down_dx(dffn, w_down, act_da, act_du, name, tm=512):
    m = dffn.shape[0]
    _, k, n = w_down.shape

    def body(d_ref, w_ref, da_ref, du_ref, o_ref):
        dact = _dot_nt(d_ref[...], w_ref[0])
        o_ref[:, :k] = (dact * da_ref[...].astype(F32)).astype(BF16)
        o_ref[:, k:] = (dact * du_ref[...].astype(F32)).astype(BF16)

    return _call(body, name=name, grid=(m // tm,),
                 in_specs=[_rows(tm, n), pl.BlockSpec((1, k, n), lambda i: (0, 0, 0)), _rows(tm, k), _rows(tm, k)],
                 out_specs=_rows(tm, 2 * k), out_shape=_sds((m, 2 * k), BF16))(dffn, w_down, act_da, act_du)


def _final_loss(x2, ffn, gate, fg, tgt, name, tr=512):
    t = x2.shape[0]

    def body(x_ref, f_ref, gt_ref, fg_ref, t_ref, dx_ref, df_ref, l_ref, dfg_ref, dgt_ref):
        first = pl.program_id(0) == 0
        ffn_v = f_ref[...]
        x3 = x_ref[...] + gt_ref[...] * ffn_v
        r = _rms(x3)
        xh = x3 * r
        err = xh * fg_ref[...] - t_ref[...]
        dy = err * (1.0 / D_MODEL)
        dx3 = _rms_bwd(dy * fg_ref[...], xh, r)
        dx_ref[...] = dx3
        df_ref[...] = (dx3 * gt_ref[...]).astype(BF16)
        _acc(l_ref, jnp.sum(err * err, axis=0, keepdims=True), first)
        _acc(dfg_ref, jnp.sum(dy * xh, axis=0, keepdims=True), first)
        _acc(dgt_ref, jnp.sum(dx3 * ffn_v, axis=0, keepdims=True), first)

    row, vec = _rows(tr, D_MODEL), _vec(D_MODEL)
    return _call(body, name=name, grid=(t // tr,), in_specs=[row, row, vec, vec, row],
                 out_specs=[row, row, vec, vec, vec],
                 out_shape=[_sds((t, D_MODEL)), _sds((t, D_MODEL), BF16)] + [_sds((1, D_MODEL))] * 3)(
                     x2, ffn, gate, fg, tgt)


def _norm_mod_bwd(dh, x, g, scale, dres, name, gate=None, mix=None, tr=512):
    t = x.shape[0]
    below = gate is not None

    def body(*refs):
        if below:
            dh_ref, x_ref, g_ref, sc_ref, dr_ref, gt_ref, m_ref, dx_ref, dsh_ref, dsc_ref, dg_ref, dgt_ref, dm_ref = refs
        else:
            dh_ref, x_ref, g_ref, sc_ref, dr_ref, dx_ref, dsh_ref, dsc_ref, dg_ref = refs
        first = pl.program_id(0) == 0
        xv = x_ref[...]
        dhv = dh_ref[...].astype(F32)
        r = _rms(xv)
        xh = xv * r
        dn = dhv * (1.0 + sc_ref[...])
        dx = dr_ref[...] + _rms_bwd(dn * g_ref[...], xh, r)
        dx_ref[...] = dx
        _acc(dsh_ref, jnp.sum(dhv, axis=0, keepdims=True), first)
        _acc(dsc_ref, jnp.sum(dhv * xh * g_ref[...], axis=0, keepdims=True), first)
        _acc(dg_ref, jnp.sum(dn * xh, axis=0, keepdims=True), first)
        if below:
            _acc(dgt_ref, jnp.sum(dx * m_ref[...], axis=0, keepdims=True), first)
            dm_ref[...] = (dx * gt_ref[...]).astype(BF16)

    row, vec = _rows(tr, D_MODEL), _vec(D_MODEL)
    in_specs = [row, row, vec, vec, row] + ([vec, row] if below else [])
    out_specs = [row, vec, vec, vec] + ([vec, row] if below else [])
    out_shape = [_sds((t, D_MODEL))] + [_sds((1, D_MODEL))] * 3 + ([_sds((1, D_MODEL)), _sds((t, D_MODEL), BF16)] if below else [])
    args = (dh, x, g, scale, dres) + ((gate, mix) if below else ())
    return _call(body, name=name, grid=(t // tr,), in_specs=in_specs, out_specs=out_specs, out_shape=out_shape)(*args)


def _mix_in_bwd(dmi, o_hg, proj, att, og, ag, name, tr=512):
    t = o_hg.shape[0]

    def body(d_ref, o_ref, g_ref, a_ref, og_ref, ag_ref, do_ref, dg_ref, da_ref, dd_ref, dog_ref, dag_ref):
        first = pl.program_id(0) == 0
        dog = jnp.zeros((1, HG_HEAD), F32)
        for h in range(HG_WIDTH // HG_HEAD):
            sl = slice(h * HG_HEAD, (h + 1) * HG_HEAD)
            oh = o_ref[:, sl].astype(F32)
            gv = g_ref[:, sl].astype(F32)
            dv = d_ref[:, sl].astype(F32)
            r = _rms(oh)
            xh = oh * r
            sg = _sigmoid(gv)
            dno = dv * gv * sg
            dg_ref[:, sl] = (dv * xh * og_ref[...] * sg * (1.0 + gv * (1.0 - sg))).astype(BF16)
            dog = dog + jnp.sum(dno * xh, axis=0, keepdims=True)
            do_ref[:, sl] = _rms_bwd(dno * og_ref[...], xh, r).astype(BF16)
        _acc(dog_ref, dog, first)
        av = a_ref[...]
        dav = d_ref[:, HG_WIDTH:].astype(F32)
        r = _rms(av)
        xa = av * r
        _acc(dag_ref, jnp.sum(dav * xa, axis=0, keepdims=True), first)
        datt = _rms_bwd(dav * ag_ref[...], xa, r)
        da_ref[...] = datt.astype(BF16)
        prod = datt * av
        lane = lax.broadcasted_iota(jnp.int32, (1, 128), 1)
        dd = jnp.zeros((tr, 128), F32)
        for hp in range(ATT_HEADS // 2):
            pp = prod[:, hp * 128:(hp + 1) * 128]
            lo = jnp.sum(jnp.where(lane < 64, pp, 0.0), axis=-1, keepdims=True)
            hi = jnp.sum(jnp.where(lane >= 64, pp, 0.0), axis=-1, keepdims=True)
            dd = jnp.where(lane == 2 * hp, lo, dd)
            dd = jnp.where(lane == 2 * hp + 1, hi, dd)
        dd_ref[...] = dd

    half = _rows(tr, HG_WIDTH)
    return _call(body, name=name, grid=(t // tr,),
                 in_specs=[_rows(tr, D_MODEL), half, pl.BlockSpec((tr, HG_WIDTH), lambda i: (i, 3)), half,
                           _vec(HG_HEAD), _vec(ATT_WIDTH)],
                 out_specs=[half, half, half, _rows(tr, 128), _vec(HG_HEAD), _vec(ATT_WIDTH)],
                 out_shape=[_sds((t, HG_WIDTH), BF16)] * 3 + [_sds((t, 128)), _sds((1, HG_HEAD)), _sds((1, ATT_WIDTH))])(
                     dmi, o_hg, proj, att, og, ag)


def _dproj(dhg, dg, dqs, dkvs, name, tr=512):
    t = dhg.shape[0]
    w3 = 3 * HG_WIDTH
    w4 = w3 + HG_WIDTH
    nbr = len(dqs)

    def body(*refs):
        h_ref, g_ref, q_refs, kv_refs, o_ref = refs[0], refs[1], refs[2:2 + nbr], refs[2 + nbr:2 + 2 * nbr], refs[-1]
        o_ref[:, :w3] = h_ref[...]
        o_ref[:, w3:w4] = g_ref[...].astype(BF16)
        o_ref[:, w4:w4 + ATT_WIDTH] = sum(r[...].astype(F32) for r in q_refs).astype(BF16)
        o_ref[:, w4 + ATT_WIDTH:] = sum(r[...].astype(F32) for r in kv_refs).astype(BF16)

    return _call(body, name=name, grid=(t // tr,),
                 in_specs=[_rows(tr, w3), _rows(tr, HG_WIDTH)] + [_rows(tr, ATT_WIDTH)] * nbr
                 + [_rows(tr, 2 * ATT_WIDTH)] * nbr,
                 out_specs=_rows(tr, IN_WIDTH), out_shape=_sds((t, IN_WIDTH), BF16))(dhg, dg, *dqs, *dkvs)


def _chunk_tri(upper):
    row = lax.broadcasted_iota(jnp.int32, (HG_GROUP, HG_CHUNK, HG_CHUNK), 1)
    col = lax.broadcasted_iota(jnp.int32, (HG_GROUP, HG_CHUNK, HG_CHUNK), 2)
    return (row <= col if upper else row >= col).astype(BF16)


def _chunk_cumsum(x, tri):
    x3 = x.reshape(HG_GROUP, HG_CHUNK, x.shape[1])
    dims = (((2,), (1,)), ((0,), (0,)))
    out = None
    for _ in range(3):
        part = x3.astype(BF16)
        x3 = x3 - part.astype(F32)
        term = lax.dot_general(tri, part, dims, preferred_element_type=F32)
        out = term if out is None else out + term
    return out.reshape(x.shape)


def _hg_gates(f_raw, q_raw, lb, tri):
    sg = _sigmoid(f_raw)
    f = lb + (1.0 - lb) * sg
    k = 1.0 - f
    b = _chunk_cumsum(jnp.log(f), tri)
    sq = _sigmoid(q_raw)
    return sg, f, k, b, sq


def _hg_masks(rows):
    row = lax.broadcasted_iota(jnp.int32, (rows, rows), 0)
    col = lax.broadcasted_iota(jnp.int32, (rows, rows), 1)
    same = (row // HG_CHUNK) == (col // HG_CHUNK)
    return jnp.logical_and(row >= col, same), jnp.logical_and(row <= col, same)


def _per_chunk(rows_of):
    return jnp.concatenate([jnp.broadcast_to(r, (HG_CHUNK, r.shape[1])) for r in rows_of], axis=0)


def _hgrn_fwd(proj, lb_logits, name):
    t = proj.shape[0]
    nc = t // HG_CHUNK
    nh = HG_WIDTH // HG_HEAD
    rows = HG_GROUP * HG_CHUNK

    def body(q_ref, f_ref, i_ref, lg_ref, o_ref, st_ref, s_scr):
        @pl.when(pl.program_id(0) == 0)
        def _():
            s_scr[...] = jnp.zeros_like(s_scr)

        lg = lg_ref[...]
        lb_all = _sigmoid(lg[0:1] - lg[1:2])
        causal, _ = _hg_masks(rows)
        tri = _chunk_tri(False)
        for h in range(nh):
            sl = slice(h * HG_HEAD, (h + 1) * HG_HEAD)
            q_raw = q_ref[:, sl].astype(F32)
            _, _, k, b, sq = _hg_gates(f_ref[:, sl].astype(F32), q_raw, lb_all[:, sl], tri)
            v = i_ref[:, sl].astype(BF16)
            gls = [b[(g + 1) * HG_CHUNK - 1:(g + 1) * HG_CHUNK] for g in range(HG_GROUP)]
            bm = _per_chunk([b[g * HG_CHUNK + HG_CHUNK // 2 - 1:g * HG_CHUNK + HG_CHUNK // 2] for g in range(HG_GROUP)])
            qd = (q_raw * sq * jnp.exp(b)).astype(BF16)
            qm = (q_raw * sq * jnp.exp(b - bm)).astype(BF16)
            km = (k * jnp.exp(bm - b)).astype(BF16)
            ke = (k * jnp.exp(_per_chunk(gls) - b)).astype(BF16)
            a = jnp.where(causal, _dot_nt(qm, km), 0.0).astype(BF16)
            o_intra = _dot(a, v)
            st = s_scr[h]
            o_inter = []
            for g in range(HG_GROUP):
                rs = slice(g * HG_CHUNK, (g + 1) * HG_CHUNK)
                st_ref[g, sl, :] = st
                o_inter.append(_dot_nt(qd[rs], st.astype(BF16)))
                st = st * jnp.exp(gls[g]) + _dot_tn(v[rs], ke[rs])
            s_scr[h] = st
            o_ref[:, sl] = (o_intra + jnp.concatenate(o_inter, axis=0)).astype(BF16)

    blk = lambda j: pl.BlockSpec((rows, HG_WIDTH), lambda c: (c, j))
    return _call(body, name=name, grid=(nc // HG_GROUP,),
                 in_specs=[blk(0), blk(1), blk(2), pl.BlockSpec((2, HG_WIDTH), lambda c: (0, 0))],
                 out_specs=[blk(0), pl.BlockSpec((HG_GROUP, HG_WIDTH, HG_HEAD), lambda c: (c, 0, 0))],
                 out_shape=[_sds((t, HG_WIDTH), BF16), _sds((nc, HG_WIDTH, HG_HEAD))],
                 scratch_shapes=[pltpu.VMEM((nh, HG_HEAD, HG_HEAD), F32)])(proj, proj, proj, lb_logits)


def _hgrn_bwd(proj, lb_logits, states, do, name):
    t = proj.shape[0]
    ng = t // (HG_GROUP * HG_CHUNK)
    nh = HG_WIDTH // HG_HEAD
    rows = HG_GROUP * HG_CHUNK

    def body(q_ref, f_ref, i_ref, lg_ref, st_ref, do_ref, d_ref, dlb_ref, ds_scr):
        first = pl.program_id(0) == 0

        @pl.when(first)
        def _():
            ds_scr[...] = jnp.zeros_like(ds_scr)

        lg = lg_ref[...]
        lb_all = _sigmoid(lg[0:1] - lg[1:2])
        causal, _ = _hg_masks(rows)
        tri = _chunk_tri(False)
        tri_t = _chunk_tri(True)
        dlb = []
        for h in range(nh):
            sl = slice(h * HG_HEAD, (h + 1) * HG_HEAD)
            q_raw = q_ref[:, sl].astype(F32)
            lb = lb_all[:, sl]
            sg, f, k, b, sq = _hg_gates(f_ref[:, sl].astype(F32), q_raw, lb, tri)
            v = i_ref[:, sl].astype(BF16)
            gls = [b[(g + 1) * HG_CHUNK - 1:(g + 1) * HG_CHUNK] for g in range(HG_GROUP)]
            bm = _per_chunk([b[g * HG_CHUNK + HG_CHUNK // 2 - 1:g * HG_CHUNK + HG_CHUNK // 2] for g in range(HG_GROUP)])
            eb = jnp.exp(b)
            ebm = jnp.exp(b - bm)
            emb = jnp.exp(bm - b)
            egb = jnp.exp(_per_chunk(gls) - b)
            ke = k * egb
            qd_b, qm_b = (q_raw * sq * eb).astype(BF16), (q_raw * sq * ebm).astype(BF16)
            km_b, ke_b = (k * emb).astype(BF16), ke.astype(BF16)
            dov = do_ref[:, sl].astype(BF16)
            a = jnp.where(causal, _dot_nt(qm_b, km_b), 0.0).astype(BF16)
            da = jnp.where(causal, _dot_nt(dov, v), 0.0).astype(BF16)
            dkm = _dot_tn(da, qm_b)
            dst = ds_scr[h]
            dqd_s, dv_s, dke_s, dgl_s = [None] * HG_GROUP, [None] * HG_GROUP, [None] * HG_GROUP, [None] * HG_GROUP
            for g in reversed(range(HG_GROUP)):
                rs = slice(g * HG_CHUNK, (g + 1) * HG_CHUNK)
                st = st_ref[g, sl, :]
                dst_b = dst.astype(BF16)
                egl = jnp.exp(gls[g])
                dqd_s[g] = _dot(dov[rs], st.astype(BF16))
                dv_s[g] = _dot_nt(ke_b[rs], dst_b)
                dke_s[g] = _dot(v[rs], dst_b)
                dgl_s[g] = jnp.sum(dst * st, axis=0, keepdims=True) * egl
                dst = _dot_tn(dov[rs], qd_b[rs]) + dst * egl
            ds_scr[h] = dst
            dqm = _dot(da, km_b)
            dqd = jnp.concatenate(dqd_s, axis=0)
            dv = _dot_tn(a, dov) + jnp.concatenate(dv_s, axis=0)
            dke = jnp.concatenate(dke_s, axis=0)
            t1 = dke * ke
            db = dqm * qm_b.astype(F32) - dkm * km_b.astype(F32) + dqd * qd_b.astype(F32) - t1
            dgl = _per_chunk([dgl_s[g] + jnp.sum(t1[g * HG_CHUNK:(g + 1) * HG_CHUNK], axis=0, keepdims=True)
                              for g in range(HG_GROUP)])
            dlf = _chunk_cumsum(db, tri_t) + dgl
            df = dlf / f - (dkm * emb + dke * egb)
            d_ref[:, sl] = ((dqm * ebm + dqd * eb) * sq * (1.0 + q_raw * (1.0 - sq))).astype(BF16)
            d_ref[:, HG_WIDTH + h * HG_HEAD:HG_WIDTH + (h + 1) * HG_HEAD] = (
                df * (1.0 - lb) * sg * (1.0 - sg)).astype(BF16)
            d_ref[:, 2 * HG_WIDTH + h * HG_HEAD:2 * HG_WIDTH + (h + 1) * HG_HEAD] = dv.astype(BF16)
            dlb.append(jnp.sum(df * (1.0 - sg), axis=0, keepdims=True))
        _acc(dlb_ref, jnp.concatenate(dlb, axis=1), first)

    rev = lambda j: pl.BlockSpec((rows, HG_WIDTH), lambda c: (ng - 1 - c, j))
    return _call(body, name=name, grid=(ng,),
                 in_specs=[rev(0), rev(1), rev(2), pl.BlockSpec((2, HG_WIDTH), lambda c: (0, 0)),
                           pl.BlockSpec((HG_GROUP, HG_WIDTH, HG_HEAD), lambda c: (ng - 1 - c, 0, 0)), rev(0)],
                 out_specs=[pl.BlockSpec((rows, 3 * HG_WIDTH), lambda c: (ng - 1 - c, 0)), _vec(HG_WIDTH)],
                 out_shape=[_sds((t, 3 * HG_WIDTH), BF16), _sds((1, HG_WIDTH))],
                 scratch_shapes=[pltpu.VMEM((nh, HG_HEAD, HG_HEAD), F32)])(proj, proj, proj, lb_logits, states, do)


def _to_sub(a, dil):
    t, w = a.shape
    return a if dil == 1 else a.reshape(t // dil, dil, w).transpose(1, 0, 2).reshape(t, w)


def _from_sub(a, dil):
    t, w = a.shape
    return a if dil == 1 else a.reshape(dil, t // dil, w).transpose(1, 0, 2).reshape(t, w)


def _att_mask(has_prev, seg):
    def place(v):
        v = v % ATT_BLOCK
        return v if seg == 1 else seg * (v % (ATT_BLOCK // seg)) + v // (ATT_BLOCK // seg)

    row = lax.broadcasted_iota(jnp.int32, (2 * ATT_BLOCK, 2 * ATT_BLOCK), 0)
    col = lax.broadcasted_iota(jnp.int32, (2 * ATT_BLOCK, 2 * ATT_BLOCK), 1)
    qi, kj = place(row), place(col)
    prev = jnp.logical_and(jnp.logical_and(col < ATT_BLOCK, kj >= qi), has_prev)
    cur = jnp.logical_and(col >= ATT_BLOCK, kj <= qi)
    return jnp.logical_or(prev, cur), lax.broadcasted_iota(jnp.int32, (1, 128), 1)


def _get(ref, sl):
    if len(ref.shape) == 2:
        return ref[:, sl]
    v = ref[:, :, sl]
    return v.reshape(ATT_BLOCK, v.shape[2])


def _put(ref, sl, val):
    if len(ref.shape) == 2:
        ref[:, sl] = val
    else:
        ref[:, :, sl] = val.reshape(ref.shape[0], ref.shape[1], val.shape[1])


def _att_spec(nb, dil, seg, width, col, back):
    bps = nb // dil

    def plain(n):
        return jnp.clip(n - back, 0, nb - 1), col

    def segmented(n):
        m = jnp.clip(n - back, 0, nb - 1)
        return 0, m // bps, m % bps, 0, col

    if seg == 1:
        return pl.BlockSpec((ATT_BLOCK, width), plain)
    return pl.BlockSpec((seg, None, None, ATT_BLOCK // seg, width), segmented)


def _att_shape(nb, dil, seg, width):
    t = nb * ATT_BLOCK
    return (t, width) if seg == 1 else (seg, dil, nb // dil, ATT_BLOCK // seg, width)


def _att_view(a, nb, dil, seg):
    return a.reshape(_att_shape(nb, dil, seg, a.shape[1]))


def _attn_fwd_block(q_ref, kc_ref, kp_ref, vc_ref, vp_ref, o_ref, l_ref, has_prev, seg):
    mask, lane = _att_mask(has_prev, seg)
    lo = lane < 64
    nq = ATT_BLOCK
    lse_all = jnp.zeros((nq, 128), F32)
    for hp in range(ATT_HEADS // 2):
        sl = slice(hp * 128, (hp + 1) * 128)
        q2 = _get(q_ref, sl)
        zero = jnp.zeros_like(q2)
        qs = jnp.concatenate([jnp.where(lo, q2, zero), jnp.where(lo, zero, q2)], axis=0)
        kk = jnp.concatenate([_get(kp_ref, sl), _get(kc_ref, sl)], axis=0)
        vv = jnp.concatenate([_get(vp_ref, sl), _get(vc_ref, sl)], axis=0)
        s = jnp.where(mask, _dot_nt(qs, kk) * 0.125, NEG)
        mx = jnp.max(s, axis=-1, keepdims=True)
        p = jnp.exp(s - mx)
        l = jnp.sum(p, axis=-1, keepdims=True)
        o = _dot(p.astype(BF16), vv) * (1.0 / l)
        _put(o_ref, sl, jnp.where(lo, o[:nq], o[nq:]).astype(BF16))
        lse = mx + jnp.log(l)
        lse_all = jnp.where(lane == 2 * hp, lse[:nq], lse_all)
        lse_all = jnp.where(lane == 2 * hp + 1, lse[nq:], lse_all)
    _put(l_ref, slice(None), lse_all)


def _attn_fwd(branches, name):
    t = branches[0][0].shape[0]
    nb = t // ATT_BLOCK
    nbr = len(branches)

    def body(*refs):
        n = pl.program_id(0)
        for i, (_, dil, seg) in enumerate(branches):
            _attn_fwd_block(*refs[5 * i:5 * i + 5], *refs[5 * nbr + 2 * i:5 * nbr + 2 * i + 2],
                            (n % (nb // dil)) != 0, seg)

    in_specs, args, out_specs, out_shape = [], [], [], []
    for qkv, dil, seg in branches:
        c0 = qkv.shape[1] // ATT_WIDTH - 3
        in_specs += [_att_spec(nb, dil, seg, ATT_WIDTH, c0 + j, back) for j, back in [(0, 0), (1, 0), (1, 1), (2, 0), (2, 1)]]
        args += [_att_view(qkv, nb, dil, seg)] * 5
        out_specs += [_att_spec(nb, dil, seg, ATT_WIDTH, 0, 0), _att_spec(nb, dil, seg, 128, 0, 0)]
        out_shape += [_sds(_att_shape(nb, dil, seg, ATT_WIDTH), BF16), _sds(_att_shape(nb, dil, seg, 128))]
    out = _call(body, name=name, grid=(nb,), in_specs=in_specs, out_specs=out_specs, out_shape=out_shape)(*args)
    return [(out[2 * i].reshape(t, ATT_WIDTH), out[2 * i + 1].reshape(t, 128)) for i in range(nbr)]


def _attn_combine(os_, ls_, name, tr=512):
    t = os_[0].shape[0]
    nbr = len(os_)

    def body(*refs):
        o_refs, l_refs, (a_ref, lt_ref) = refs[:nbr], refs[nbr:2 * nbr], refs[2 * nbr:]
        lane = lax.broadcasted_iota(jnp.int32, (1, 128), 1)
        ls = [r[...] for r in l_refs]
        mx = functools.reduce(jnp.maximum, ls)
        tot = mx + jnp.log(sum(jnp.exp(l - mx) for l in ls))
        lt_ref[...] = tot
        ws = [jnp.exp(l - tot) for l in ls]
        for hp in range(ATT_HEADS // 2):
            sl = slice(hp * 128, (hp + 1) * 128)
            acc = jnp.zeros((tr, 128), F32)
            for w, o_ref in zip(ws, o_refs):
                wf = jnp.where(lane < 64, w[:, 2 * hp:2 * hp + 1], w[:, 2 * hp + 1:2 * hp + 2])
                acc = acc + wf * o_ref[:, sl]
            a_ref[:, sl] = acc

    return _call(body, name=name, grid=(t // tr,),
                 in_specs=[_rows(tr, ATT_WIDTH)] * nbr + [_rows(tr, 128)] * nbr,
                 out_specs=[_rows(tr, ATT_WIDTH), _rows(tr, 128)],
                 out_shape=[_sds((t, ATT_WIDTH)), _sds((t, 128))])(*os_, *ls_)


def _attn_bwd_block(q_ref, kc_ref, kp_ref, vc_ref, vp_ref, do_ref, l_ref, d_ref, dq_ref, dkv_ref, carry, has_prev, seg):
    w = ATT_WIDTH
    nq = ATT_BLOCK
    mask, lane = _att_mask(has_prev, seg)
    lo = lane < 64
    lse, ddv = _get(l_ref, slice(None)), _get(d_ref, slice(None))
    for hp in range(ATT_HEADS // 2):
        sl = slice(hp * 128, (hp + 1) * 128)
        sv = slice(w + hp * 128, w + (hp + 1) * 128)
        q2, do2 = _get(q_ref, sl), _get(do_ref, sl)
        zero = jnp.zeros_like(q2)
        qs = jnp.concatenate([jnp.where(lo, q2, zero), jnp.where(lo, zero, q2)], axis=0)
        dos = jnp.concatenate([jnp.where(lo, do2, zero), jnp.where(lo, zero, do2)], axis=0)
        kk = jnp.concatenate([_get(kp_ref, sl), _get(kc_ref, sl)], axis=0)
        vv = jnp.concatenate([_get(vp_ref, sl), _get(vc_ref, sl)], axis=0)
        ls = jnp.concatenate([lse[:, 2 * hp:2 * hp + 1], lse[:, 2 * hp + 1:2 * hp + 2]], axis=0)
        dh = jnp.concatenate([ddv[:, 2 * hp:2 * hp + 1], ddv[:, 2 * hp + 1:2 * hp + 2]], axis=0)
        p = jnp.exp(jnp.where(mask, _dot_nt(qs, kk) * 0.125 - ls, NEG))
        ds = (p * (_dot_nt(dos, vv) - dh)).astype(BF16)
        dq = _dot(ds, kk) * 0.125
        _put(dq_ref, sl, jnp.where(lo, dq[:nq], dq[nq:]).astype(BF16))
        dk = _dot_tn(ds, qs) * 0.125
        dv = _dot_tn(p.astype(BF16), dos)
        _put(dkv_ref, sl, (carry[:, sl] + dk[:nq]).astype(BF16))
        _put(dkv_ref, sv, (carry[:, sv] + dv[:nq]).astype(BF16))
        carry[:, sl] = dk[nq:]
        carry[:, sv] = dv[nq:]


def _attn_bwd(branches, name):
    t = branches[0][0].shape[0]
    nb = t // ATT_BLOCK
    nbr = len(branches)
    w = ATT_WIDTH

    def body(*refs):
        ins, outs, carries = refs[:8 * nbr], refs[8 * nbr:10 * nbr], refs[10 * nbr:]
        n = pl.program_id(0)

        @pl.when(n == 0)
        def _():
            for carry in carries:
                carry[...] = jnp.zeros_like(carry)

        @pl.when(n < nb)
        def _():
            for i, branch in enumerate(branches):
                dil, seg = branch[4:]
                _attn_bwd_block(*ins[8 * i:8 * i + 8], *outs[2 * i:2 * i + 2], carries[i], (n % (nb // dil)) != 0, seg)

        @pl.when(n == nb)
        def _():
            for i in range(nbr):
                _put(outs[2 * i + 1], slice(None), carries[i][...].astype(BF16))

    in_specs, args, out_specs, out_shape = [], [], [], []
    for qkv, dout, lse, dd, dil, seg in branches:
        c0 = qkv.shape[1] // w - 3
        in_specs += [_att_spec(nb, dil, seg, w, c0 + j, back) for j, back in [(0, 0), (1, 0), (1, 1), (2, 0), (2, 1)]]
        in_specs += [_att_spec(nb, dil, seg, w, 0, 0), _att_spec(nb, dil, seg, 128, 0, 0), _att_spec(nb, dil, seg, 128, 0, 0)]
        args += [_att_view(a, nb, dil, seg) for a in [qkv] * 5 + [dout, lse, dd]]
        out_specs += [_att_spec(nb, dil, seg, w, 0, 0), _att_spec(nb, dil, seg, 2 * w, 0, 1)]
        out_shape += [_sds(_att_shape(nb, dil, seg, w), BF16), _sds(_att_shape(nb, dil, seg, 2 * w), BF16)]
    out = _call(body, name=name, grid=(nb + 1,), in_specs=in_specs, out_specs=out_specs, out_shape=out_shape,
                scratch_shapes=[pltpu.VMEM((ATT_BLOCK, 2 * w), F32)] * nbr)(*args)
    return [(out[2 * i].reshape(t, w), out[2 * i + 1].reshape(t, 2 * w)) for i in range(nbr)]


def _local_step(x, tgt, mod, norm1_g, lb_logits, og, ag, norm2_g, fg, get_w, put_g, late=lambda a: a):
    shift1, scale1, gate1, shift2, scale2, gate2 = [mod[:, i * D_MODEL:(i + 1) * D_MODEL] for i in range(6)]
    fg = fg.reshape(1, D_MODEL)

    h1 = _norm_mod(x, norm1_g, scale1, shift1, "norm_mod1")
    w_in = get_w("w_in", h1)
    proj = _mm_nn(h1, w_in, "mm_in", out_dtype=BF16)
    o_hg, states = _hgrn_fwd(proj, lb_logits, "hgrn_fwd")
    fine = DILATIONS[-1]
    layouts = [(d, 1 if d == 1 else fine // d) for d in DILATIONS]
    qkv_fine = _to_sub(proj[:, 4 * HG_WIDTH:], fine)
    qkvs = [proj if d == 1 else qkv_fine for d in DILATIONS]
    natural = lambda a, d: a if d == 1 else _from_sub(a, fine)
    outs = _attn_fwd([(q, d, seg) for q, (d, seg) in zip(qkvs, layouts)], "attn_fwd")
    att, lse = _attn_combine([natural(o, d) for (o, _), d in zip(outs, DILATIONS)],
                             [natural(l, d) for (_, l), d in zip(outs, DILATIONS)], "attn_combine")
    mixin = _mix_in(o_hg, proj, att, og, ag, "mix_in")
    w_out = get_w("w_out", mixin)
    mix = _mm_nn(mixin, w_out, "mm_out")
    x2, h2 = _resid_norm_mod(x, mix, gate1, norm2_g, scale2, shift2, "resid_norm_mod2")
    w_gu = get_w("w_gu", h2)
    a_ff, u_ff, act = _mm_gate_up(h2, w_gu, "mm_gu")
    w_down = get_w("w_down", act)
    ffn = _mm_nn(act, w_down, "mm_down")
    dx3, dffn, loss_v, dfg, dgate2 = _final_loss(x2, ffn, gate2, fg, tgt, "final_loss")

    dffn = put_g("w_down", *_mm_tn(act, dffn, 1, "mm_down_dw", tm=2048, tk=D_FF // 2), dffn)
    dau = _mm_down_dx(dffn, w_down, a_ff, u_ff, "mm_down_dx")
    dau = put_g("w_gu", *_mm_tn(h2, dau, N_SHARD, "mm_gu_dw", tm=x.shape[0], tk=512), dau)
    dh2 = _mm_nt(dau, w_gu, "mm_gu_dx")
    dx2, dshift2, dscale2, dg2, dgate1, dmix = _norm_mod_bwd(
        dh2, x2, norm2_g, scale2, dx3, "norm_mod2_bwd", gate=gate1, mix=mix)
    dmix = put_g("w_out", *_mm_tn(mixin, dmix, 1, "mm_out_dw", tm=x.shape[0], tk=512), dmix)
    dmixin = _mm_nt(dmix, w_out, "mm_out_dx", tm=1024)
    do_hg, dg_raw, datt, dd, dog, dag = _mix_in_bwd(dmixin, o_hg, proj, att, og, ag, "mix_in_bwd")
    datt_b = datt
    reordered = [_to_sub(a, fine) for a in (datt_b, lse, dd)]
    datts = _attn_bwd([(q,) + tuple((datt_b, lse, dd) if d == 1 else reordered) + (d, seg)
                       for q, (d, seg) in zip(qkvs, layouts)], "attn_bwd")
    dhg, dlb = _hgrn_bwd(proj, lb_logits, states, do_hg, "hgrn_bwd")
    dhg = late(dhg)
    dproj = _dproj(dhg, dg_raw, [natural(dq, d) for (dq, _), d in zip(datts, DILATIONS)],
                   [natural(dkv, d) for (_, dkv), d in zip(datts, DILATIONS)], "dproj")
    dproj = put_g("w_in", *_mm_tn(h1, dproj, N_SHARD, "mm_in_dw", tm=x.shape[0], tk=512), dproj)
    dh1 = _mm_nt(dproj, w_in, "mm_in_dx", tm=1024)
    dx, dshift1, dscale1, dg1 = _norm_mod_bwd(dh1, x, norm1_g, scale1, dx2, "norm_mod1_bwd")

    stats = jnp.concatenate([loss_v, dfg, dg2, dg1, dlb, dag, dog,
                             dshift1, dscale1, dgate1, dshift2, dscale2, dgate2], axis=1)
    return dx, stats


def _place():
    x, y, c = lax.axis_index("x"), lax.axis_index("y"), lax.axis_index("c")
    return x, y, c


def _chip_peers(x, y, c):
    return [(1 - x, y, c), (x, 1 - y, c), (1 - x, 1 - y, c)]


def _comm_call(body, name, n_in, out_shape, scratch_shapes):
    hbm = pl.BlockSpec(memory_space=pl.ANY)
    return pl.pallas_call(body, name=name, in_specs=[hbm] * n_in, out_specs=[hbm] * len(out_shape),
                          out_shape=out_shape, scratch_shapes=scratch_shapes)


_HBM = pl.BlockSpec(memory_space=pltpu.HBM)
_SEM = pl.BlockSpec(memory_space=pltpu.SEMAPHORE)
_EFFECT = pltpu.SideEffectType.DATAFLOW_SIDE_EFFECTING


def _exchange_copy(bufs, send, recv, j, peer, place, kind):
    x, y, c = place
    target = peer
    if kind == "gather":
        src = dst = bufs[0].at[2 * x + y]
    elif kind == "scatter":
        src, dst = bufs[0].at[2 * peer[0] + peer[1]], bufs[1].at[j]
    else:
        half = bufs[0].shape[1] // 2
        rows = pl.ds(c * half, half)
        if kind == "half":
            src = dst = bufs[0].at[2 * x + y, rows]
        else:
            src = dst = bufs[0].at[2 * peer[0] + peer[1], rows]
            target = (x, y, 1 - c)
    return pltpu.make_async_remote_copy(src_ref=src, dst_ref=dst, send_sem=send.at[j], recv_sem=recv.at[j],
                                        device_id=target, device_id_type=MESH)


def _exchange_start(groups, after, kind, name):
    sizes = [len(g) for g in groups]
    flat = [b for g in groups for b in g]
    ng, nb = len(groups), len(flat)

    def body(*refs):
        bufs, sems = refs[:nb], refs[nb + 1:nb + 1 + 2 * ng]
        x, y, c = _place()
        for j, peer in enumerate(_chip_peers(x, y, c)):
            at = 0
            for i, size in enumerate(sizes):
                _exchange_copy(bufs[at:at + size], sems[2 * i], sems[2 * i + 1], j, peer, (x, y, c), kind).start()
                at += size

    any_space = pl.BlockSpec(memory_space=pl.ANY)
    out = pl.pallas_call(
        body, name=name, in_specs=[_HBM] * nb + [any_space],
        out_specs=[_SEM] * (2 * ng) + [_HBM] * nb + [any_space],
        out_shape=[pltpu.SemaphoreType.DMA((3,))] * (2 * ng) + [pltpu.HBM(b.shape, b.dtype) for b in flat]
        + [_sds(after.shape, after.dtype)],
        input_output_aliases={i: 2 * ng + i for i in range(nb + 1)},
        compiler_params=pltpu.CompilerParams(has_side_effects=_EFFECT),
    )(*[pltpu.with_memory_space_constraint(b, pltpu.HBM) for b in flat], after)
    started, at = [], 2 * ng
    for i, size in enumerate(sizes):
        started.append((out[2 * i], out[2 * i + 1], tuple(out[at:at + size])))
        at += size
    return started, out[-1]


def _exchange_wait(started, after, kind, name):
    send, recv, bufs = started
    nb = len(bufs)

    def body(*refs):
        x, y, c = _place()
        for j, peer in enumerate(_chip_peers(x, y, c)):
            cp = _exchange_copy(refs[:nb], refs[nb], refs[nb + 1], j, peer, (x, y, c), kind)
            cp.wait_send()
            cp.wait_recv()

    return pl.pallas_call(
        body, name=name, in_specs=[_HBM] * nb + [_SEM, _SEM, pl.BlockSpec(memory_space=pl.ANY)],
        out_specs=[_HBM] * nb, out_shape=[pltpu.HBM(b.shape, b.dtype) for b in bufs],
        input_output_aliases={i: i for i in range(nb)},
        compiler_params=pltpu.CompilerParams(has_side_effects=_EFFECT),
    )(*bufs, send, recv, after)


def _sibling_copies(v_refs, l_refs, send, recv):
    x, y, c = _place()
    return [pltpu.make_async_remote_copy(src_ref=v, dst_ref=l, send_sem=send.at[a], recv_sem=recv.at[a],
                                         device_id=(x, y, 1 - c), device_id_type=MESH)
            for a, (v, l) in enumerate(zip(v_refs, l_refs))]


def _sibling_start(vs, after, name):
    n = len(vs)
    lands = [lax.empty(v.shape, v.dtype) for v in vs]

    def body(*refs):
        for cp in _sibling_copies(refs[:n], refs[n:2 * n], refs[2 * n + 1], refs[2 * n + 2]):
            cp.start()

    any_space = pl.BlockSpec(memory_space=pl.ANY)
    out = pl.pallas_call(
        body, name=name, in_specs=[_HBM] * (2 * n) + [any_space],
        out_specs=[_SEM, _SEM] + [_HBM] * (2 * n) + [any_space],
        out_shape=[pltpu.SemaphoreType.DMA((n,))] * 2 + [pltpu.HBM(b.shape, b.dtype) for b in vs + lands]
        + [_sds(after.shape, after.dtype)],
        input_output_aliases={i: 2 + i for i in range(2 * n + 1)},
        compiler_params=pltpu.CompilerParams(has_side_effects=_EFFECT),
    )(*[pltpu.with_memory_space_constraint(b, pltpu.HBM) for b in vs + lands], after)
    return (out[0], out[1], tuple(out[2:2 + n]), tuple(out[2 + n:2 + 2 * n])), out[-1]


def _sibling_wait(started, after, name):
    send, recv, vs, lands = started
    n = len(vs)

    def body(*refs):
        for cp in _sibling_copies(refs[:n], refs[n:2 * n], refs[2 * n], refs[2 * n + 1]):
            cp.wait_send()
            cp.wait_recv()

    out = pl.pallas_call(
        body, name=name, in_specs=[_HBM] * (2 * n) + [_SEM, _SEM, pl.BlockSpec(memory_space=pl.ANY)],
        out_specs=[_HBM] * (2 * n), out_shape=[pltpu.HBM(b.shape, b.dtype) for b in vs + lands],
        input_output_aliases={i: i for i in range(2 * n)},
        compiler_params=pltpu.CompilerParams(has_side_effects=_EFFECT),
    )(*vs, *lands, send, recv, after)
    return out[:n], out[n:]


def _swap_sibling(vs, name):
    n = len(vs)

    def body(*refs):
        v_refs, o_refs, (send, recv) = refs[:n], refs[n:2 * n], refs[2 * n:]
        x, y, c = _place()
        cps = [pltpu.make_async_remote_copy(
            src_ref=v_refs[a], dst_ref=o_refs[a], send_sem=send.at[a], recv_sem=recv.at[a],
            device_id=(x, y, 1 - c), device_id_type=MESH) for a in range(n)]
        for cp in cps:
            cp.start()
        for cp in cps:
            cp.wait()

    return _comm_call(body, name, n, [_sds(v.shape, v.dtype) for v in vs],
                      [pltpu.SemaphoreType.DMA((n,)), pltpu.SemaphoreType.DMA((n,))])(*vs)


def _gather_rows(v, name):
    r, n = v.shape

    def body(v_ref, o_ref, send, recv, loc):
        x, y, c = _place()
        me = 4 * x + 2 * y + c
        own = pltpu.make_async_copy(v_ref, o_ref.at[me], loc)
        own.start()
        peers = []
        for k in range(1, 8):
            px = 1 - x if k & 4 else x
            py = 1 - y if k & 2 else y
            pc = 1 - c if k & 1 else c
            peers.append((px, py, pc))
        sends = []
        for k, peer in enumerate(peers):
            cp = pltpu.make_async_remote_copy(src_ref=v_ref, dst_ref=o_ref.at[me], send_sem=send.at[k],
                                              recv_sem=recv.at[k], device_id=peer, device_id_type=MESH)
            cp.start()
            sends.append(cp)
        for k, peer in enumerate(peers):
            pltpu.make_async_remote_copy(src_ref=v_ref, dst_ref=o_ref.at[4 * peer[0] + 2 * peer[1] + peer[2]],
                                         send_sem=send.at[k], recv_sem=recv.at[k], device_id=peer,
                                         device_id_type=MESH).wait_recv()
        for cp in sends:
            cp.wait_send()
        own.wait()

    vmem = pl.BlockSpec(memory_space=pltpu.VMEM)
    return pl.pallas_call(body, name=name, in_specs=[vmem], out_specs=vmem, out_shape=_sds((8, r, n), v.dtype),
                          scratch_shapes=[pltpu.SemaphoreType.DMA((7,)), pltpu.SemaphoreType.DMA((7,)),
                                          pltpu.SemaphoreType.DMA])(v)


def _cast_place(ws, shard, name):
    n = len(ws)

    def body(s_ref, *refs):
        for w_ref, o_ref in zip(refs[:n], refs[n:]):
            o_ref[0] = w_ref[...].astype(BF16)

    return pl.pallas_call(
        body, name=name, out_shape=[_sds((N_SHARD,) + w.shape, BF16) for w in ws],
        grid_spec=pltpu.PrefetchScalarGridSpec(
            num_scalar_prefetch=1, grid=(4,),
            in_specs=[pl.BlockSpec((w.shape[0] // 4, w.shape[1]), lambda i, s: (i, 0)) for w in ws],
            out_specs=[pl.BlockSpec((1, w.shape[0] // 4, w.shape[1]), lambda i, s: (s[0], i, 0)) for w in ws]),
        compiler_params=pltpu.CompilerParams(dimension_semantics=("arbitrary",), vmem_limit_bytes=VMEM_LIMIT),
    )(shard.reshape(1).astype(jnp.int32), *ws)


def _mod_part(c_all, w_ada, b_ada, name):
    n = w_ada.shape[1]

    def body(c_ref, w_ref, b_ref, a_ref, p_ref):
        cv = c_ref[...]
        ca = cv * _sigmoid(cv)
        a_ref[...] = ca
        p_ref[...] = jnp.dot(ca, w_ref[...], precision=lax.Precision.HIGHEST, preferred_element_type=F32) + b_ref[...]

    full = lambda a: pl.BlockSpec(a.shape, lambda i: (0, 0))
    return _call(body, name=name, grid=(1,), in_specs=[full(c_all), full(w_ada), full(b_ada)],
                 out_specs=[pl.BlockSpec((8, D_MODEL), lambda i: (0, 0)), pl.BlockSpec((8, n), lambda i: (0, 0))],
                 out_shape=[_sds((8, D_MODEL)), _sds((8, n))])(c_all, w_ada, b_ada)


def _sum_received(g, shard, land, name):
    _, r, c = g.shape
    tr = r // 4

    def body(s_ref, g_ref, l_ref, o_ref):
        o_ref[...] = ((g_ref[0] + l_ref[0].astype(F32)) + l_ref[1].astype(F32)) + l_ref[2].astype(F32)

    return pl.pallas_call(
        body, name=name, out_shape=_sds((r, c)),
        grid_spec=pltpu.PrefetchScalarGridSpec(
            num_scalar_prefetch=1, grid=(4,),
            in_specs=[pl.BlockSpec((1, tr, c), lambda i, s: (s[0], i, 0)),
                      pl.BlockSpec((3, tr, c), lambda i, s: (0, i, 0))],
            out_specs=pl.BlockSpec((tr, c), lambda i, s: (i, 0))),
        compiler_params=pltpu.CompilerParams(dimension_semantics=("arbitrary",), vmem_limit_bytes=VMEM_LIMIT),
    )(shard.reshape(1).astype(jnp.int32), g, land)


def _adamw_outer(w, ct, dm, m, v, name):
    k, n = w.shape
    tr = k // 4

    def body(w_ref, c_ref, d_ref, m_ref, v_ref, g_out, d_out, m_out, v_out):
        cv = c_ref[...]
        dv = d_ref[...]
        g = cv[:, 0:1] * dv[0:1, :]
        for i in range(1, 8):
            g = g + cv[:, i:i + 1] * dv[i:i + 1, :]
        g_out[...] = g
        d_out[...], m_out[...], v_out[...] = _adamw_math(w_ref[...], g, m_ref[...], v_ref[...])

    row = _rows(tr, n)
    return _call(body, name=name, grid=(4,),
                 in_specs=[row, _rows(tr, 8), pl.BlockSpec((8, n), lambda i: (0, 0)), row, row],
                 out_specs=[row] * 4, out_shape=[_sds((k, n))] * 4)(w, ct, dm, m, v)


def _adamw_math(w, g, m, v):
    m_new = ADAM_B1 * m + (1.0 - ADAM_B1) * g
    v_new = ADAM_B2 * v + (1.0 - ADAM_B2) * (g * g)
    m_hat = m_new / (1.0 - ADAM_B1 ** ADAM_STEP)
    v_hat = v_new / (1.0 - ADAM_B2 ** ADAM_STEP)
    return -ADAM_LR * (m_hat / (jnp.sqrt(v_hat) + ADAM_EPS) + ADAM_WD * w), m_new, v_new


def _small_update(stats, smalls, name):
    offsets = [ST_DMOD, ST_DG1, ST_DLB, ST_DOG, ST_DAG, ST_DG2, ST_DFG]
    lb_index = 2

    def body(*refs):
        s_ref, ins, l_ref, outs = refs[0], refs[1:22], refs[22], refs[23:]
        tot = s_ref[0:1, :]
        for i in range(1, 8):
            tot = tot + s_ref[i:i + 1, :]
        l_ref[...] = jnp.zeros((1, 128), F32) + (0.5 / D_MODEL) * jnp.sum(tot[:, ST_LOSS:ST_LOSS + D_MODEL])
        for p, off in enumerate(offsets):
            w_ref, m_ref, v_ref = ins[3 * p:3 * p + 3]
            g_out, d_out, m_out, v_out = outs[4 * p:4 * p + 4]
            g = tot[:, off:off + w_ref.shape[1]]
            if p == lb_index:
                lg = w_ref[...]
                lb = _sigmoid(lg[0:1] - lg[1:2])
                g = g * lb * (1.0 - lb)
            for r in range(w_ref.shape[0]):
                rows = slice(r, r + 1)
                gr = g if r == 0 else -g
                delta, m_new, v_new = _adamw_math(w_ref[rows, :], gr, m_ref[rows, :], v_ref[rows, :])
                g_out[rows, :] = gr
                d_out[rows, :] = delta
                m_out[rows, :] = m_new
                v_out[rows, :] = v_new

    full = lambda a: pl.BlockSpec(a.shape, lambda i: (0, 0))
    flat = [a for t in smalls for a in t]
    return _call(body, name=name, grid=(1,),
                 in_specs=[full(stats)] + [full(a) for a in flat],
                 out_specs=[pl.BlockSpec((1, 128), lambda i: (0, 0))] + [full(t[0]) for t in smalls for _ in range(4)],
                 out_shape=[_sds((1, 128))] + [_sds(t[0].shape) for t in smalls for _ in range(4)])(stats, *flat)


def _adamw(w, gs, m, v, name, steps=4):
    r, c = w.shape
    tr = r // steps
    ng = len(gs)

    def body(*refs):
        w_ref, g_refs, (m_ref, v_ref, g_out, d_out, m_out, v_out) = refs[0], refs[1:1 + ng], refs[1 + ng:]
        g = g_refs[0][...]
        for g_ref in g_refs[1:]:
            g = g + g_ref[...]
        g_out[...] = g
        d_out[...], m_out[...], v_out[...] = _adamw_math(w_ref[...], g, m_ref[...], v_ref[...])

    row = _rows(tr, c)
    return _call(body, name=name, grid=(steps,), in_specs=[row] * (3 + ng), out_specs=[row] * 4,
                 out_shape=[_sds((r, c))] * 4)(w, *gs, m, v)


def kernel(x, c, w_ada, b_ada, norm1_g, w_in, hg_lb_logits, hg_onorm_g, att_onorm_g, w_out, norm2_g, w_gate_up, w_down, final_g, loss_target, m_w_ada, m_b_ada, m_norm1_g, m_w_in, m_hg_lb_logits, m_hg_onorm_g, m_att_onorm_g, m_w_out, m_norm2_g, m_w_gate_up, m_w_down, m_final_g, v_w_ada, v_b_ada, v_norm1_g, v_w_in, v_hg_lb_logits, v_hg_onorm_g, v_att_onorm_g, v_w_out, v_norm2_g, v_w_gate_up, v_w_down, v_final_g):
    ix, iy, ic = _place()
    shard = 2 * ix + iy
    sample = 4 * ix + 2 * iy + ic
    n_ada = w_ada.shape[2]

    shards = [w_in[0], w_out[0], w_gate_up[0], w_down[0]]
    names = ["w_in", "w_out", "w_gu", "w_down"]
    shapes = [(N_SHARD,) + w.shape for w in shards]
    placed = [(_cast_place(shards[:1], shard, "place_w_in")[0],)]
    placed += [(p,) for p in _cast_place(shards[1:], shard, "place_rest")]

    c_all = _gather_rows(c, "gather_c").reshape(8, D_MODEL)
    b_part = lax.dynamic_slice(b_ada, (0, shard * n_ada), (1, n_ada))
    c_act, part = _mod_part(c_all, w_ada[0], b_part, "mod_part")
    parts = _gather_rows(part, "gather_mod")[::2]
    mod = lax.dynamic_index_in_dim(parts, sample, axis=1, keepdims=False).reshape(1, 6 * D_MODEL)
    (first,), mod = _exchange_start(placed[:1], mod, "half", "gather_start_w_in")
    gathering = {}

    def get_w(name, after):
        if name == "w_in":
            halves = _exchange_wait(first, after, "half", "gather_wait_w_in")
            (passing,), token = _exchange_start([tuple(halves)], mod, "forward", "forward_start_w_in")
            (full,) = _exchange_wait(passing, token, "forward", "forward_wait_w_in")
            rest, full = _exchange_start(placed[1:], full, "gather", "gather_start_rest")
            gathering.update(zip(names[1:], rest))
            return full
        (full,) = _exchange_wait(gathering[name], after, "gather", "gather_wait_" + name)
        return full if name == "w_gu" else full.reshape(1, -1, D_MODEL)

    scattering = {}

    def put_g(name, g, g_bf16, then):
        shape = shapes[names.index(name)]
        land = lax.empty((3,) + shape[1:], BF16)
        (started,), then = _exchange_start([(g_bf16.reshape(shape), land)], then, "scatter", "scatter_start_" + name)
        scattering[name] = (g.reshape(shape), started)
        return then

    def summed(name, after):
        g, started = scattering[name]
        _, land = _exchange_wait(started, after, "scatter", "scatter_wait_" + name)
        return _sum_received(g, shard, land, "sum_" + name)

    early = ["w_down", "w_gu", "w_out"]
    swapping = []

    def late(a):
        started, a = _sibling_start([summed(nm, a) for nm in early], a, "swap_start")
        swapping.append(started)
        return a

    dx, stats = _local_step(x[0], loss_target[0], mod, norm1_g, hg_lb_logits, hg_onorm_g, att_onorm_g,
                            norm2_g, final_g, get_w, put_g, late)

    stats_all = _gather_rows(stats, "gather_stats").reshape(8, ST_WIDTH)
    dmod = lax.dynamic_slice(stats_all, (0, ST_DMOD + shard * n_ada), (8, n_ada))

    as_row = lambda a: a.reshape(1, -1) if a.ndim == 1 else a
    smalls = [tuple(as_row(a) for a in t) for t in [
        (b_ada, m_b_ada, v_b_ada), (norm1_g, m_norm1_g, v_norm1_g),
        (hg_lb_logits, m_hg_lb_logits, v_hg_lb_logits), (hg_onorm_g, m_hg_onorm_g, v_hg_onorm_g),
        (att_onorm_g, m_att_onorm_g, v_att_onorm_g), (norm2_g, m_norm2_g, v_norm2_g),
        (final_g, m_final_g, v_final_g)]]
    loss, *small_out = _small_update(stats_all, smalls, "small_update")
    shapes_out = [b_ada.shape, norm1_g.shape, hg_lb_logits.shape, hg_onorm_g.shape, att_onorm_g.shape,
                  norm2_g.shape, final_g.shape]
    sg, sd, sm, sv = [[small_out[4 * p + i].reshape(shapes_out[p]) for p in range(7)] for i in range(4)]

    ada = _adamw_outer(w_ada[0], c_act.T, dmod, m_w_ada[0], v_w_ada[0], "adamw_w_ada")
    moments = [(m_w_in, v_w_in), (m_w_out, v_w_out), (m_w_gate_up, v_w_gate_up), (m_w_down, v_w_down)]

    def update(group, sums, other):
        return {nm: _adamw(shards[names.index(nm)], [s, o], moments[names.index(nm)][0][0],
                           moments[names.index(nm)][1][0], "adamw_" + nm) for nm, s, o in zip(group, sums, other)}

    sums, other = _sibling_wait(swapping[0], ada[1], "swap_wait")
    done = update(early, sums, other)
    sum_in = summed("w_in", done["w_out"][1])
    done.update(update(["w_in"], [sum_in], _swap_sibling([sum_in], "swap_sum_in")))
    big = [ada] + [done[nm] for nm in names]
    bg, bd, bm, bv = [[t[i][None] for t in big] for i in range(4)]

    def order(b, s):
        return [b[0], s[0], s[1], b[1], s[2], s[3], s[4], b[2], s[5], b[3], b[4], s[6]]

    return (loss[0, 0], dx[None], *order(bg, sg), *order(bd, sd), *order(bm, sm), *order(bv, sv))
```

```python
import functools

import jax
import jax.numpy as jnp
from jax import lax
from jax.experimental import pallas as pl
from jax.experimental.pallas import tpu as pltpu

F32 = jnp.float32
BF16 = jnp.bfloat16
MESH = pl.DeviceIdType.MESH

D_MODEL = 1024
HG_WIDTH = 512
HG_HEAD = 128
HG_CHUNK = 64
HG_GROUP = 4
ATT_WIDTH = 512
ATT_HEADS = 8
ATT_BLOCK = 128
DILATIONS = (1, 4, 16)
D_FF = 2816
IN_WIDTH = 3584
N_SHARD = 4
RMS_EPS = 1e-6
NEG = -1e30

ADAM_LR = 0.001
ADAM_B1 = 0.9
ADAM_B2 = 0.999
ADAM_EPS = 1e-08
ADAM_WD = 0.01
ADAM_STEP = 10

VMEM_LIMIT = 56 * 2**20

ST_LOSS, ST_DFG, ST_DG2, ST_DG1 = 0, 1024, 2048, 3072
ST_DLB, ST_DAG, ST_DOG, ST_DMOD = 4096, 4608, 5120, 5248
ST_WIDTH = 5248 + 6144
SP_BADA, SP_N1, SP_LB, SP_OG, SP_AG, SP_N2, SP_FG = 0, 6144, 7168, 8192, 8320, 8832, 9856
SP_WIDTH = 10880


def _call(body, *, name, grid, in_specs, out_specs, out_shape, scratch_shapes=()):
    return pl.pallas_call(
        body, name=name, grid=grid, in_specs=in_specs, out_specs=out_specs, out_shape=out_shape,
        scratch_shapes=list(scratch_shapes),
        compiler_params=pltpu.CompilerParams(
            dimension_semantics=("arbitrary",) * len(grid), vmem_limit_bytes=VMEM_LIMIT))


def _sds(shape, dtype=F32):
    return jax.ShapeDtypeStruct(shape, dtype)


def _dot(a, b):
    return jnp.dot(a, b, preferred_element_type=F32)


def _dot_nt(a, b):
    return lax.dot_general(a, b, (((1,), (1,)), ((), ())), preferred_element_type=F32)


def _dot_tn(a, b):
    return lax.dot_general(a, b, (((0,), (0,)), ((), ())), preferred_element_type=F32)


def _sigmoid(x):
    return 1.0 / (1.0 + jnp.exp(-x))


def _rows(tr, width):
    return pl.BlockSpec((tr, width), lambda i: (i, 0))


def _vec(width):
    return pl.BlockSpec((1, width), lambda i: (0, 0))


def _acc(ref, val, first):
    @pl.when(first)
    def _():
        ref[...] = val

    @pl.when(jnp.logical_not(first))
    def _():
        ref[...] += val


def _mm_nn(a, b3, name, tm=1024, out_dtype=F32):
    m, k = a.shape
    s, _, n = b3.shape

    def body(a_ref, b_ref, o_ref):
        o_ref[...] = _dot(a_ref[...], b_ref[0]).astype(out_dtype)

    return _call(
        body, name=name, grid=(s, m // tm),
        in_specs=[pl.BlockSpec((tm, k), lambda j, i: (i, 0)), pl.BlockSpec((1, k, n), lambda j, i: (j, 0, 0))],
        out_specs=pl.BlockSpec((tm, n), lambda j, i: (i, j)), out_shape=_sds((m, s * n), out_dtype))(a, b3)


def _mm_nt(dy, b3, name, tm=512):
    m = dy.shape[0]
    s, k, n = b3.shape

    def body(dy_ref, b_ref, o_ref):
        acc = _dot_nt(dy_ref[:, 0:n], b_ref[0])
        for j in range(1, s):
            acc = acc + _dot_nt(dy_ref[:, j * n:(j + 1) * n], b_ref[j])
        o_ref[...] = acc.astype(BF16)

    return _call(
        body, name=name, grid=(m // tm,),
        in_specs=[_rows(tm, s * n), pl.BlockSpec((s, k, n), lambda i: (0, 0, 0))],
        out_specs=_rows(tm, k), out_shape=_sds((m, k), BF16))(dy, b3)


def _mm_tn(a, dy, s, name, tm, tk):
    m, k = a.shape
    n = dy.shape[1] // s
    steps = m // tm

    def body(a_ref, dy_ref, o_ref, ob_ref):
        p = _dot_tn(a_ref[...], dy_ref[...])[None]
        if steps == 1:
            o_ref[...] = p
            ob_ref[...] = p.astype(BF16)
        else:
            i = pl.program_id(2)
            _acc(o_ref, p, i == 0)

            @pl.when(i == steps - 1)
            def _():
                ob_ref[...] = o_ref[...].astype(BF16)

    out = pl.BlockSpec((1, tk, n), lambda kk, j, i: (j, kk, 0))
    return _call(
        body, name=name, grid=(k // tk, s, steps),
        in_specs=[pl.BlockSpec((tm, tk), lambda kk, j, i: (i, kk)), pl.BlockSpec((tm, n), lambda kk, j, i: (i, j))],
        out_specs=[out, out], out_shape=[_sds((s, k, n)), _sds((s, k, n), BF16)])(a, dy)


def _rms(x):
    return lax.rsqrt(jnp.mean(x * x, axis=-1, keepdims=True) + RMS_EPS)


def _rms_bwd(dxh, xh, r):
    return r * (dxh - xh * jnp.mean(dxh * xh, axis=-1, keepdims=True))


def _norm_mod(x, g, scale, shift, name, tr=512):
    t = x.shape[0]

    def body(x_ref, g_ref, sc_ref, sh_ref, h_ref):
        xv = x_ref[...]
        n = xv * _rms(xv) * g_ref[...]
        h_ref[...] = (n * (1.0 + sc_ref[...]) + sh_ref[...]).astype(BF16)

    return _call(body, name=name, grid=(t // tr,),
                 in_specs=[_rows(tr, D_MODEL), _vec(D_MODEL), _vec(D_MODEL), _vec(D_MODEL)],
                 out_specs=_rows(tr, D_MODEL), out_shape=_sds((t, D_MODEL), BF16))(x, g, scale, shift)


def _mix_in(o_hg, proj, att, og, ag, name, tr=512):
    t = o_hg.shape[0]

    def body(o_ref, g_ref, a_ref, og_ref, ag_ref, m_ref):
        for h in range(HG_WIDTH // HG_HEAD):
            sl = slice(h * HG_HEAD, (h + 1) * HG_HEAD)
            oh = o_ref[:, sl].astype(F32)
            gv = g_ref[:, sl].astype(F32)
            m_ref[:, sl] = (oh * _rms(oh) * og_ref[...] * (gv * _sigmoid(gv))).astype(BF16)
        av = a_ref[...]
        m_ref[:, HG_WIDTH:] = (av * _rms(av) * ag_ref[...]).astype(BF16)

    return _call(body, name=name, grid=(t // tr,),
                 in_specs=[_rows(tr, HG_WIDTH), pl.BlockSpec((tr, HG_WIDTH), lambda i: (i, 3)), _rows(tr, ATT_WIDTH),
                           _vec(HG_HEAD), _vec(ATT_WIDTH)],
                 out_specs=_rows(tr, D_MODEL), out_shape=_sds((t, D_MODEL), BF16))(o_hg, proj, att, og, ag)


def _resid_norm_mod(x, mix, gate, g, scale, shift, name, tr=1024):
    t = x.shape[0]

    def body(x_ref, m_ref, gt_ref, g_ref, sc_ref, sh_ref, x2_ref, h_ref):
        x2 = x_ref[...] + gt_ref[...] * m_ref[...]
        x2_ref[...] = x2
        n = x2 * _rms(x2) * g_ref[...]
        h_ref[...] = (n * (1.0 + sc_ref[...]) + sh_ref[...]).astype(BF16)

    return _call(body, name=name, grid=(t // tr,),
                 in_specs=[_rows(tr, D_MODEL), _rows(tr, D_MODEL)] + [_vec(D_MODEL)] * 4,
                 out_specs=[_rows(tr, D_MODEL), _rows(tr, D_MODEL)],
                 out_shape=[_sds((t, D_MODEL)), _sds((t, D_MODEL), BF16)])(x, mix, gate, g, scale, shift)


def _mm_gate_up(h, w_gu, name, tm=1024):
    m, k = h.shape
    n = w_gu.shape[2]

    def body(h_ref, wa_ref, wu_ref, da_ref, du_ref, o_ref):
        hv = h_ref[...]
        a = _dot(hv, wa_ref[0])
        u = _dot(hv, wu_ref[0])
        sg = _sigmoid(a)
        silu = a * sg
        da_ref[...] = (u * sg * (1.0 + a * (1.0 - sg))).astype(BF16)
        du_ref[...] = silu.astype(BF16)
        o_ref[...] = (silu * u).astype(BF16)

    out = pl.BlockSpec((tm, n), lambda j, i: (i, j))
    return _call(body, name=name, grid=(2, m // tm),
                 in_specs=[pl.BlockSpec((tm, k), lambda j, i: (i, 0)), pl.BlockSpec((1, k, n), lambda j, i: (j, 0, 0)),
                           pl.BlockSpec((1, k, n), lambda j, i: (j + 2, 0, 0))],
                 out_specs=[out, out, out], out_shape=[_sds((m, 2 * n), BF16)] * 3)(h, w_gu, w_gu)


def _mm_down_dx(dffn, w_down, act_da, act_du, name, tm=512):
    m = dffn.shape[0]
    _, k, n = w_down.shape

    def body(d_ref, w_ref, da_ref, du_ref, o_ref):
        dact = _dot_nt(d_ref[...], w_ref[0])
        o_ref[:, :k] = (dact * da_ref[...].astype(F32)).astype(BF16)
        o_ref[:, k:] = (dact * du_ref[...].astype(F32)).astype(BF16)

    return _call(body, name=name, grid=(m // tm,),
                 in_specs=[_rows(tm, n), pl.BlockSpec((1, k, n), lambda i: (0, 0, 0)), _rows(tm, k), _rows(tm, k)],
                 out_specs=_rows(tm, 2 * k), out_shape=_sds((m, 2 * k), BF16))(dffn, w_down, act_da, act_du)


def _final_loss(x2, ffn, gate, fg, tgt, name, tr=1024):
    t = x2.shape[0]

    def body(x_ref, f_ref, gt_ref, fg_ref, t_ref, dx_ref, df_ref, l_ref, dfg_ref, dgt_ref):
        first = pl.program_id(0) == 0
        ffn_v = f_ref[...]
        x3 = x_ref[...] + gt_ref[...] * ffn_v
        r = _rms(x3)
        xh = x3 * r
        err = xh * fg_ref[...] - t_ref[...]
        dy = err * (1.0 / D_MODEL)
        dx3 = _rms_bwd(dy * fg_ref[...], xh, r)
        dx_ref[...] = dx3
        df_ref[...] = (dx3 * gt_ref[...]).astype(BF16)
        _acc(l_ref, jnp.sum(err * err, axis=0, keepdims=True), first)
        _acc(dfg_ref, jnp.sum(dy * xh, axis=0, keepdims=True), first)
        _acc(dgt_ref, jnp.sum(dx3 * ffn_v, axis=0, keepdims=True), first)

    row, vec = _rows(tr, D_MODEL), _vec(D_MODEL)
    return _call(body, name=name, grid=(t // tr,), in_specs=[row, row, vec, vec, row],
                 out_specs=[row, row, vec, vec, vec],
                 out_shape=[_sds((t, D_MODEL)), _sds((t, D_MODEL), BF16)] + [_sds((1, D_MODEL))] * 3)(
                     x2, ffn, gate, fg, tgt)


def _norm_mod_bwd(dh, x, g, scale, dres, name, gate=None, mix=None, tr=512):
    t = x.shape[0]
    below = gate is not None

    def body(*refs):
        if below:
            dh_ref, x_ref, g_ref, sc_ref, dr_ref, gt_ref, m_ref, dx_ref, dsh_ref, dsc_ref, dg_ref, dgt_ref, dm_ref = refs
        else:
            dh_ref, x_ref, g_ref, sc_ref, dr_ref, dx_ref, dsh_ref, dsc_ref, dg_ref = refs
        first = pl.program_id(0) == 0
        xv = x_ref[...]
        dhv = dh_ref[...].astype(F32)
        r = _rms(xv)
        xh = xv * r
        dn = dhv * (1.0 + sc_ref[...])
        dx = dr_ref[...] + _rms_bwd(dn * g_ref[...], xh, r)
        dx_ref[...] = dx
        _acc(dsh_ref, jnp.sum(dhv, axis=0, keepdims=True), first)
        _acc(dsc_ref, jnp.sum(dhv * xh * g_ref[...], axis=0, keepdims=True), first)
        _acc(dg_ref, jnp.sum(dn * xh, axis=0, keepdims=True), first)
        if below:
            _acc(dgt_ref, jnp.sum(dx * m_ref[...], axis=0, keepdims=True), first)
            dm_ref[...] = (dx * gt_ref[...]).astype(BF16)

    row, vec = _rows(tr, D_MODEL), _vec(D_MODEL)
    in_specs = [row, row, vec, vec, row] + ([vec, row] if below else [])
    out_specs = [row, vec, vec, vec] + ([vec, row] if below else [])
    out_shape = [_sds((t, D_MODEL))] + [_sds((1, D_MODEL))] * 3 + ([_sds((1, D_MODEL)), _sds((t, D_MODEL), BF16)] if below else [])
    args = (dh, x, g, scale, dres) + ((gate, mix) if below else ())
    return _call(body, name=name, grid=(t // tr,), in_specs=in_specs, out_specs=out_specs, out_shape=out_shape)(*args)


def _mix_in_bwd(dmi, o_hg, proj, att, og, ag, name, tr=512):
    t = o_hg.shape[0]

    def body(d_ref, o_ref, g_ref, a_ref, og_ref, ag_ref, do_ref, dg_ref, da_ref, dd_ref, dog_ref, dag_ref):
        first = pl.program_id(0) == 0
        dog = jnp.zeros((1, HG_HEAD), F32)
        for h in range(HG_WIDTH // HG_HEAD):
            sl = slice(h * HG_HEAD, (h + 1) * HG_HEAD)
            oh = o_ref[:, sl].astype(F32)
            gv = g_ref[:, sl].astype(F32)
            dv = d_ref[:, sl].astype(F32)
            r = _rms(oh)
            xh = oh * r
            sg = _sigmoid(gv)
            dno = dv * gv * sg
            dg_ref[:, sl] = (dv * xh * og_ref[...] * sg * (1.0 + gv * (1.0 - sg))).astype(BF16)
            dog = dog + jnp.sum(dno * xh, axis=0, keepdims=True)
            do_ref[:, sl] = _rms_bwd(dno * og_ref[...], xh, r).astype(BF16)
        _acc(dog_ref, dog, first)
        av = a_ref[...]
        dav = d_ref[:, HG_WIDTH:].astype(F32)
        r = _rms(av)
        xa = av * r
        _acc(dag_ref, jnp.sum(dav * xa, axis=0, keepdims=True), first)
        datt = _rms_bwd(dav * ag_ref[...], xa, r)
        da_ref[...] = datt.astype(BF16)
        prod = datt * av
        lane = lax.broadcasted_iota(jnp.int32, (1, 128), 1)
        dd = jnp.zeros((tr, 128), F32)
        for hp in range(ATT_HEADS // 2):
            pp = prod[:, hp * 128:(hp + 1) * 128]
            lo = jnp.sum(jnp.where(lane < 64, pp, 0.0), axis=-1, keepdims=True)
            hi = jnp.sum(jnp.where(lane >= 64, pp, 0.0), axis=-1, keepdims=True)
            dd = jnp.where(lane == 2 * hp, lo, dd)
            dd = jnp.where(lane == 2 * hp + 1, hi, dd)
        dd_ref[...] = dd

    half = _rows(tr, HG_WIDTH)
    return _call(body, name=name, grid=(t // tr,),
                 in_specs=[_rows(tr, D_MODEL), half, pl.BlockSpec((tr, HG_WIDTH), lambda i: (i, 3)), half,
                           _vec(HG_HEAD), _vec(ATT_WIDTH)],
                 out_specs=[half, half, half, _rows(tr, 128), _vec(HG_HEAD), _vec(ATT_WIDTH)],
                 out_shape=[_sds((t, HG_WIDTH), BF16)] * 3 + [_sds((t, 128)), _sds((1, HG_HEAD)), _sds((1, ATT_WIDTH))])(
                     dmi, o_hg, proj, att, og, ag)


def _dproj(dhg, dg, dqs, dkvs, name, tr=1024):
    t = dhg.shape[0]
    w3 = 3 * HG_WIDTH
    w4 = w3 + HG_WIDTH
    nbr = len(dqs)

    def body(*refs):
        h_ref, g_ref, q_refs, kv_refs, o_ref = refs[0], refs[1], refs[2:2 + nbr], refs[2 + nbr:2 + 2 * nbr], refs[-1]
        o_ref[:, :w3] = h_ref[...]
        o_ref[:, w3:w4] = g_ref[...].astype(BF16)
        o_ref[:, w4:w4 + ATT_WIDTH] = sum(r[...].astype(F32) for r in q_refs).astype(BF16)
        o_ref[:, w4 + ATT_WIDTH:] = sum(r[...].astype(F32) for r in kv_refs).astype(BF16)

    return _call(body, name=name, grid=(t // tr,),
                 in_specs=[_rows(tr, w3), _rows(tr, HG_WIDTH)] + [_rows(tr, ATT_WIDTH)] * nbr
                 + [_rows(tr, 2 * ATT_WIDTH)] * nbr,
                 out_specs=_rows(tr, IN_WIDTH), out_shape=_sds((t, IN_WIDTH), BF16))(dhg, dg, *dqs, *dkvs)


def _chunk_tri(upper):
    row = lax.broadcasted_iota(jnp.int32, (HG_GROUP, HG_CHUNK, HG_CHUNK), 1)
    col = lax.broadcasted_iota(jnp.int32, (HG_GROUP, HG_CHUNK, HG_CHUNK), 2)
    return (row <= col if upper else row >= col).astype(BF16)


def _chunk_cumsum(x, tri):
    x3 = x.reshape(HG_GROUP, HG_CHUNK, x.shape[1])
    dims = (((2,), (1,)), ((0,), (0,)))
    out = None
    for _ in range(3):
        part = x3.astype(BF16)
        x3 = x3 - part.astype(F32)
        term = lax.dot_general(tri, part, dims, preferred_element_type=F32)
        out = term if out is None else out + term
    return out.reshape(x.shape)


def _hg_gates(f_raw, q_raw, lb, tri):
    sg = _sigmoid(f_raw)
    f = lb + (1.0 - lb) * sg
    k = 1.0 - f
    b = _chunk_cumsum(jnp.log(f), tri)
    sq = _sigmoid(q_raw)
    return sg, f, k, b, sq


def _hg_masks(rows):
    row = lax.broadcasted_iota(jnp.int32, (rows, rows), 0)
    col = lax.broadcasted_iota(jnp.int32, (rows, rows), 1)
    same = (row // HG_CHUNK) == (col // HG_CHUNK)
    return jnp.logical_and(row >= col, same), jnp.logical_and(row <= col, same)


def _per_chunk(rows_of):
    return jnp.concatenate([jnp.broadcast_to(r, (HG_CHUNK, r.shape[1])) for r in rows_of], axis=0)


def _hgrn_fwd(proj, lb_logits, name):
    t = proj.shape[0]
    nc = t // HG_CHUNK
    nh = HG_WIDTH // HG_HEAD
    rows = HG_GROUP * HG_CHUNK

    def body(q_ref, f_ref, i_ref, lg_ref, o_ref, st_ref, s_scr):
        @pl.when(pl.program_id(0) == 0)
        def _():
            s_scr[...] = jnp.zeros_like(s_scr)

        lg = lg_ref[...]
        lb_all = _sigmoid(lg[0:1] - lg[1:2])
        causal, _ = _hg_masks(rows)
        tri = _chunk_tri(False)
        for h in range(nh):
            sl = slice(h * HG_HEAD, (h + 1) * HG_HEAD)
            q_raw = q_ref[:, sl].astype(F32)
            _, _, k, b, sq = _hg_gates(f_ref[:, sl].astype(F32), q_raw, lb_all[:, sl], tri)
            v = i_ref[:, sl].astype(BF16)
            gls = [b[(g + 1) * HG_CHUNK - 1:(g + 1) * HG_CHUNK] for g in range(HG_GROUP)]
            bm = _per_chunk([b[g * HG_CHUNK + HG_CHUNK // 2 - 1:g * HG_CHUNK + HG_CHUNK // 2] for g in range(HG_GROUP)])
            qd = (q_raw * sq * jnp.exp(b)).astype(BF16)
            qm = (q_raw * sq * jnp.exp(b - bm)).astype(BF16)
            km = (k * jnp.exp(bm - b)).astype(BF16)
            ke = (k * jnp.exp(_per_chunk(gls) - b)).astype(BF16)
            a = jnp.where(causal, _dot_nt(qm, km), 0.0).astype(BF16)
            o_intra = _dot(a, v)
            st = s_scr[h]
            o_inter = []
            for g in range(HG_GROUP):
                rs = slice(g * HG_CHUNK, (g + 1) * HG_CHUNK)
                st_ref[g, sl, :] = st
                o_inter.append(_dot_nt(qd[rs], st.astype(BF16)))
                st = st * jnp.exp(gls[g]) + _dot_tn(v[rs], ke[rs])
            s_scr[h] = st
            o_ref[:, sl] = (o_intra + jnp.concatenate(o_inter, axis=0)).astype(BF16)

    blk = lambda j: pl.BlockSpec((rows, HG_WIDTH), lambda c: (c, j))
    return _call(body, name=name, grid=(nc // HG_GROUP,),
                 in_specs=[blk(0), blk(1), blk(2), pl.BlockSpec((2, HG_WIDTH), lambda c: (0, 0))],
                 out_specs=[blk(0), pl.BlockSpec((HG_GROUP, HG_WIDTH, HG_HEAD), lambda c: (c, 0, 0))],
                 out_shape=[_sds((t, HG_WIDTH), BF16), _sds((nc, HG_WIDTH, HG_HEAD))],
                 scratch_shapes=[pltpu.VMEM((nh, HG_HEAD, HG_HEAD), F32)])(proj, proj, proj, lb_logits)


def _hgrn_bwd(proj, lb_logits, states, do, name):
    t = proj.shape[0]
    ng = t // (HG_GROUP * HG_CHUNK)
    nh = HG_WIDTH // HG_HEAD
    rows = HG_GROUP * HG_CHUNK

    def body(q_ref, f_ref, i_ref, lg_ref, st_ref, do_ref, d_ref, dlb_ref, ds_scr):
        first = pl.program_id(0) == 0

        @pl.when(first)
        def _():
            ds_scr[...] = jnp.zeros_like(ds_scr)

        lg = lg_ref[...]
        lb_all = _sigmoid(lg[0:1] - lg[1:2])
        causal, _ = _hg_masks(rows)
        tri = _chunk_tri(False)
        tri_t = _chunk_tri(True)
        dlb = []
        for h in range(nh):
            sl = slice(h * HG_HEAD, (h + 1) * HG_HEAD)
            q_raw = q_ref[:, sl].astype(F32)
            lb = lb_all[:, sl]
            sg, f, k, b, sq = _hg_gates(f_ref[:, sl].astype(F32), q_raw, lb, tri)
            v = i_ref[:, sl].astype(BF16)
            gls = [b[(g + 1) * HG_CHUNK - 1:(g + 1) * HG_CHUNK] for g in range(HG_GROUP)]
            bm = _per_chunk([b[g * HG_CHUNK + HG_CHUNK // 2 - 1:g * HG_CHUNK + HG_CHUNK // 2] for g in range(HG_GROUP)])
            eb = jnp.exp(b)
            ebm = jnp.exp(b - bm)
            emb = jnp.exp(bm - b)
            egb = jnp.exp(_per_chunk(gls) - b)
            ke = k * egb
            qd_b, qm_b = (q_raw * sq * eb).astype(BF16), (q_raw * sq * ebm).astype(BF16)
            km_b, ke_b = (k * emb).astype(BF16), ke.astype(BF16)
            dov = do_ref[:, sl].astype(BF16)
            a = jnp.where(causal, _dot_nt(qm_b, km_b), 0.0).astype(BF16)
            da = jnp.where(causal, _dot_nt(dov, v), 0.0).astype(BF16)
            dkm = _dot_tn(da, qm_b)
            dst = ds_scr[h]
            dqd_s, dv_s, dke_s, dgl_s = [None] * HG_GROUP, [None] * HG_GROUP, [None] * HG_GROUP, [None] * HG_GROUP
            for g in reversed(range(HG_GROUP)):
                rs = slice(g * HG_CHUNK, (g + 1) * HG_CHUNK)
                st = st_ref[g, sl, :]
                dst_b = dst.astype(BF16)
                egl = jnp.exp(gls[g])
                dqd_s[g] = _dot(dov[rs], st.astype(BF16))
                dv_s[g] = _dot_nt(ke_b[rs], dst_b)
                dke_s[g] = _dot(v[rs], dst_b)
                dgl_s[g] = jnp.sum(dst * st, axis=0, keepdims=True) * egl
                dst = _dot_tn(dov[rs], qd_b[rs]) + dst * egl
            ds_scr[h] = dst
            dqm = _dot(da, km_b)
            dqd = jnp.concatenate(dqd_s, axis=0)
            dv = _dot_tn(a, dov) + jnp.concatenate(dv_s, axis=0)
            dke = jnp.concatenate(dke_s, axis=0)
            t1 = dke * ke
            db = dqm * qm_b.astype(F32) - dkm * km_b.astype(F32) + dqd * qd_b.astype(F32) - t1
            dgl = _per_chunk([dgl_s[g] + jnp.sum(t1[g * HG_CHUNK:(g + 1) * HG_CHUNK], axis=0, keepdims=True)
                              for g in range(HG_GROUP)])
            dlf = _chunk_cumsum(db, tri_t) + dgl
            df = dlf / f - (dkm * emb + dke * egb)
            d_ref[:, sl] = ((dqm * ebm + dqd * eb) * sq * (1.0 + q_raw * (1.0 - sq))).astype(BF16)
            d_ref[:, HG_WIDTH + h * HG_HEAD:HG_WIDTH + (h + 1) * HG_HEAD] = (
                df * (1.0 - lb) * sg * (1.0 - sg)).astype(BF16)
            d_ref[:, 2 * HG_WIDTH + h * HG_HEAD:2 * HG_WIDTH + (h + 1) * HG_HEAD] = dv.astype(BF16)
            dlb.append(jnp.sum(df * (1.0 - sg), axis=0, keepdims=True))
        _acc(dlb_ref, jnp.concatenate(dlb, axis=1), first)

    rev = lambda j: pl.BlockSpec((rows, HG_WIDTH), lambda c: (ng - 1 - c, j))
    return _call(body, name=name, grid=(ng,),
                 in_specs=[rev(0), rev(1), rev(2), pl.BlockSpec((2, HG_WIDTH), lambda c: (0, 0)),
                           pl.BlockSpec((HG_GROUP, HG_WIDTH, HG_HEAD), lambda c: (ng - 1 - c, 0, 0)), rev(0)],
                 out_specs=[pl.BlockSpec((rows, 3 * HG_WIDTH), lambda c: (ng - 1 - c, 0)), _vec(HG_WIDTH)],
                 out_shape=[_sds((t, 3 * HG_WIDTH), BF16), _sds((1, HG_WIDTH))],
                 scratch_shapes=[pltpu.VMEM((nh, HG_HEAD, HG_HEAD), F32)])(proj, proj, proj, lb_logits, states, do)


def _to_sub(a, dil):
    t, w = a.shape
    return a if dil == 1 else a.reshape(t // dil, dil, w).transpose(1, 0, 2).reshape(t, w)


def _from_sub(a, dil):
    t, w = a.shape
    return a if dil == 1 else a.reshape(dil, t // dil, w).transpose(1, 0, 2).reshape(t, w)


def _att_mask(has_prev, seg):
    def place(v):
        v = v % ATT_BLOCK
        return v if seg == 1 else seg * (v % (ATT_BLOCK // seg)) + v // (ATT_BLOCK // seg)

    row = lax.broadcasted_iota(jnp.int32, (2 * ATT_BLOCK, 2 * ATT_BLOCK), 0)
    col = lax.broadcasted_iota(jnp.int32, (2 * ATT_BLOCK, 2 * ATT_BLOCK), 1)
    qi, kj = place(row), place(col)
    prev = jnp.logical_and(jnp.logical_and(col < ATT_BLOCK, kj >= qi), has_prev)
    cur = jnp.logical_and(col >= ATT_BLOCK, kj <= qi)
    return jnp.logical_or(prev, cur), lax.broadcasted_iota(jnp.int32, (1, 128), 1)


def _get(ref, sl):
    if len(ref.shape) == 2:
        return ref[:, sl]
    v = ref[:, :, sl]
    return v.reshape(ATT_BLOCK, v.shape[2])


def _put(ref, sl, val):
    if len(ref.shape) == 2:
        ref[:, sl] = val
    else:
        ref[:, :, sl] = val.reshape(ref.shape[0], ref.shape[1], val.shape[1])


def _att_spec(nb, dil, seg, width, col, back):
    bps = nb // dil

    def plain(n):
        return jnp.clip(n - back, 0, nb - 1), col

    def segmented(n):
        m = jnp.clip(n - back, 0, nb - 1)
        return 0, m // bps, m % bps, 0, col

    if seg == 1:
        return pl.BlockSpec((ATT_BLOCK, width), plain)
    return pl.BlockSpec((seg, None, None, ATT_BLOCK // seg, width), segmented)


def _att_shape(nb, dil, seg, width):
    t = nb * ATT_BLOCK
    return (t, width) if seg == 1 else (seg, dil, nb // dil, ATT_BLOCK // seg, width)


def _att_view(a, nb, dil, seg):
    return a.reshape(_att_shape(nb, dil, seg, a.shape[1]))


def _attn_fwd_block(q_ref, kc_ref, kp_ref, vc_ref, vp_ref, o_ref, l_ref, has_prev, seg):
    mask, lane = _att_mask(has_prev, seg)
    lo = lane < 64
    nq = ATT_BLOCK
    lse_all = jnp.zeros((nq, 128), F32)
    for hp in range(ATT_HEADS // 2):
        sl = slice(hp * 128, (hp + 1) * 128)
        q2 = _get(q_ref, sl)
        zero = jnp.zeros_like(q2)
        q2 = q2 * 0.125
        qs = jnp.concatenate([jnp.where(lo, q2, zero), jnp.where(lo, zero, q2)], axis=0)
        kk = jnp.concatenate([_get(kp_ref, sl), _get(kc_ref, sl)], axis=0)
        vv = jnp.concatenate([_get(vp_ref, sl), _get(vc_ref, sl)], axis=0)
        s = jnp.where(mask, _dot_nt(qs, kk), NEG)
        mx = jnp.max(s, axis=-1, keepdims=True)
        p = jnp.exp(s - mx)
        l = jnp.sum(p, axis=-1, keepdims=True)
        o = _dot(p.astype(BF16), vv) * (1.0 / l)
        _put(o_ref, sl, jnp.where(lo, o[:nq], o[nq:]).astype(BF16))
        lse = mx + jnp.log(l)
        lse_all = jnp.where(lane == 2 * hp, lse[:nq], lse_all)
        lse_all = jnp.where(lane == 2 * hp + 1, lse[nq:], lse_all)
    _put(l_ref, slice(None), lse_all)


def _attn_fwd(branches, name):
    t = branches[0][0].shape[0]
    nb = t // ATT_BLOCK
    nbr = len(branches)

    def body(*refs):
        n = pl.program_id(0)
        for i, (_, dil, seg) in enumerate(branches):
            _attn_fwd_block(*refs[5 * i:5 * i + 5], *refs[5 * nbr + 2 * i:5 * nbr + 2 * i + 2],
                            (n % (nb // dil)) != 0, seg)

    in_specs, args, out_specs, out_shape = [], [], [], []
    for qkv, dil, seg in branches:
        c0 = qkv.shape[1] // ATT_WIDTH - 3
        in_specs += [_att_spec(nb, dil, seg, ATT_WIDTH, c0 + j, back) for j, back in [(0, 0), (1, 0), (1, 1), (2, 0), (2, 1)]]
        args += [_att_view(qkv, nb, dil, seg)] * 5
        out_specs += [_att_spec(nb, dil, seg, ATT_WIDTH, 0, 0), _att_spec(nb, dil, seg, 128, 0, 0)]
        out_shape += [_sds(_att_shape(nb, dil, seg, ATT_WIDTH), BF16), _sds(_att_shape(nb, dil, seg, 128))]
    out = _call(body, name=name, grid=(nb,), in_specs=in_specs, out_specs=out_specs, out_shape=out_shape)(*args)
    return [(out[2 * i].reshape(t, ATT_WIDTH), out[2 * i + 1].reshape(t, 128)) for i in range(nbr)]


def _attn_combine(os_, ls_, name, tr=512):
    t = os_[0].shape[0]
    nbr = len(os_)

    def body(*refs):
        o_refs, l_refs, (a_ref, lt_ref) = refs[:nbr], refs[nbr:2 * nbr], refs[2 * nbr:]
        lane = lax.broadcasted_iota(jnp.int32, (1, 128), 1)
        ls = [r[...] for r in l_refs]
        mx = functools.reduce(jnp.maximum, ls)
        tot = mx + jnp.log(sum(jnp.exp(l - mx) for l in ls))
        lt_ref[...] = tot
        ws = [jnp.exp(l - tot) for l in ls]
        for hp in range(ATT_HEADS // 2):
            sl = slice(hp * 128, (hp + 1) * 128)
            acc = jnp.zeros((tr, 128), F32)
            for w, o_ref in zip(ws, o_refs):
                wf = jnp.where(lane < 64, w[:, 2 * hp:2 * hp + 1], w[:, 2 * hp + 1:2 * hp + 2])
                acc = acc + wf * o_ref[:, sl]
            a_ref[:, sl] = acc

    return _call(body, name=name, grid=(t // tr,),
                 in_specs=[_rows(tr, ATT_WIDTH)] * nbr + [_rows(tr, 128)] * nbr,
                 out_specs=[_rows(tr, ATT_WIDTH), _rows(tr, 128)],
                 out_shape=[_sds((t, ATT_WIDTH)), _sds((t, 128))])(*os_, *ls_)


def _attn_bwd_block(q_ref, kc_ref, kp_ref, vc_ref, vp_ref, do_ref, l_ref, d_ref, dq_ref, dkv_ref, carry, has_prev, seg):
    w = ATT_WIDTH
    nq = ATT_BLOCK
    mask, lane = _att_mask(has_prev, seg)
    lo = lane < 64
    lse, ddv = _get(l_ref, slice(None)), _get(d_ref, slice(None))
    for hp in range(ATT_HEADS // 2):
        sl = slice(hp * 128, (hp + 1) * 128)
        sv = slice(w + hp * 128, w + (hp + 1) * 128)
        q2, do2 = _get(q_ref, sl), _get(do_ref, sl)
        zero = jnp.zeros_like(q2)
        q2 = q2 * 0.125
        qs = jnp.concatenate([jnp.where(lo, q2, zero), jnp.where(lo, zero, q2)], axis=0)
        dos = jnp.concatenate([jnp.where(lo, do2, zero), jnp.where(lo, zero, do2)], axis=0)
        kk = jnp.concatenate([_get(kp_ref, sl), _get(kc_ref, sl)], axis=0)
        vv = jnp.concatenate([_get(vp_ref, sl), _get(vc_ref, sl)], axis=0)
        ls = jnp.concatenate([lse[:, 2 * hp:2 * hp + 1], lse[:, 2 * hp + 1:2 * hp + 2]], axis=0)
        dh = jnp.concatenate([ddv[:, 2 * hp:2 * hp + 1], ddv[:, 2 * hp + 1:2 * hp + 2]], axis=0)
        p = jnp.exp(jnp.where(mask, _dot_nt(qs, kk) - ls, NEG))
        ds = (p * (_dot_nt(dos, vv) - dh)).astype(BF16)
        dq = _dot(ds, kk) * 0.125
        _put(dq_ref, sl, jnp.where(lo, dq[:nq], dq[nq:]).astype(BF16))
        dk = _dot_tn(ds, qs)
        dv = _dot_tn(p.astype(BF16), dos)
        _put(dkv_ref, sl, (carry[:, sl] + dk[:nq]).astype(BF16))
        _put(dkv_ref, sv, (carry[:, sv] + dv[:nq]).astype(BF16))
        carry[:, sl] = dk[nq:]
        carry[:, sv] = dv[nq:]


def _attn_bwd(branches, name):
    t = branches[0][0].shape[0]
    nb = t // ATT_BLOCK
    nbr = len(branches)
    w = ATT_WIDTH

    def body(*refs):
        ins, outs, carries = refs[:8 * nbr], refs[8 * nbr:10 * nbr], refs[10 * nbr:]
        n = pl.program_id(0)

        @pl.when(n == 0)
        def _():
            for carry in carries:
                carry[...] = jnp.zeros_like(carry)

        @pl.when(n < nb)
        def _():
            for i, branch in enumerate(branches):
                dil, seg = branch[4:]
                _attn_bwd_block(*ins[8 * i:8 * i + 8], *outs[2 * i:2 * i + 2], carries[i], (n % (nb // dil)) != 0, seg)

        @pl.when(n == nb)
        def _():
            for i in range(nbr):
                _put(outs[2 * i + 1], slice(None), carries[i][...].astype(BF16))

    in_specs, args, out_specs, out_shape = [], [], [], []
    for qkv, dout, lse, dd, dil, seg in branches:
        c0 = qkv.shape[1] // w - 3
        in_specs += [_att_spec(nb, dil, seg, w, c0 + j, back) for j, back in [(0, 0), (1, 0), (1, 1), (2, 0), (2, 1)]]
        in_specs += [_att_spec(nb, dil, seg, w, 0, 0), _att_spec(nb, dil, seg, 128, 0, 0), _att_spec(nb, dil, seg, 128, 0, 0)]
        args += [_att_view(a, nb, dil, seg) for a in [qkv] * 5 + [dout, lse, dd]]
        out_specs += [_att_spec(nb, dil, seg, w, 0, 0), _att_spec(nb, dil, seg, 2 * w, 0, 1)]
        out_shape += [_sds(_att_shape(nb, dil, seg, w), BF16), _sds(_att_shape(nb, dil, seg, 2 * w), BF16)]
    out = _call(body, name=name, grid=(nb + 1,), in_specs=in_specs, out_specs=out_specs, out_shape=out_shape,
                scratch_shapes=[pltpu.VMEM((ATT_BLOCK, 2 * w), F32)] * nbr)(*args)
    return [(out[2 * i].reshape(t, w), out[2 * i + 1].reshape(t, 2 * w)) for i in range(nbr)]


def _local_step(x, tgt, mod, norm1_g, lb_logits, og, ag, norm2_g, fg, get_w, put_g, late=lambda a: a):
    shift1, scale1, gate1, shift2, scale2, gate2 = [mod[:, i * D_MODEL:(i + 1) * D_MODEL] for i in range(6)]
    fg = fg.reshape(1, D_MODEL)

    h1 = _norm_mod(x, norm1_g, scale1, shift1, "norm_mod1")
    w_in = get_w("w_in", h1)
    proj = _mm_nn(h1, w_in, "mm_in", out_dtype=BF16)
    o_hg, states = _hgrn_fwd(proj, lb_logits, "hgrn_fwd")
    fine = DILATIONS[-1]
    layouts = [(d, 1 if d == 1 else fine // d) for d in DILATIONS]
    qkv_fine = _to_sub(proj[:, 4 * HG_WIDTH:], fine)
    qkvs = [proj if d == 1 else qkv_fine for d in DILATIONS]
    natural = lambda a, d: a if d == 1 else _from_sub(a, fine)
    outs = _attn_fwd([(q, d, seg) for q, (d, seg) in zip(qkvs, layouts)], "attn_fwd")
    att, lse = _attn_combine([natural(o, d) for (o, _), d in zip(outs, DILATIONS)],
                             [natural(l, d) for (_, l), d in zip(outs, DILATIONS)], "attn_combine")
    mixin = _mix_in(o_hg, proj, att, og, ag, "mix_in")
    w_out = get_w("w_out", mixin)
    mix = _mm_nn(mixin, w_out, "mm_out")
    x2, h2 = _resid_norm_mod(x, mix, gate1, norm2_g, scale2, shift2, "resid_norm_mod2")
    w_gu = get_w("w_gu", h2)
    a_ff, u_ff, act = _mm_gate_up(h2, w_gu, "mm_gu")
    w_down = get_w("w_down", act)
    ffn = _mm_nn(act, w_down, "mm_down")
    dx3, dffn, loss_v, dfg, dgate2 = _final_loss(x2, ffn, gate2, fg, tgt, "final_loss")

    dffn = put_g("w_down", *_mm_tn(act, dffn, 1, "mm_down_dw", tm=2048, tk=D_FF // 2), dffn)
    dau = _mm_down_dx(dffn, w_down, a_ff, u_ff, "mm_down_dx")
    dau = put_g("w_gu", *_mm_tn(h2, dau, N_SHARD, "mm_gu_dw", tm=x.shape[0], tk=512), dau)
    dh2 = _mm_nt(dau, w_gu, "mm_gu_dx")
    dx2, dshift2, dscale2, dg2, dgate1, dmix = _norm_mod_bwd(
        dh2, x2, norm2_g, scale2, dx3, "norm_mod2_bwd", gate=gate1, mix=mix)
    dmix = put_g("w_out", *_mm_tn(mixin, dmix, 1, "mm_out_dw", tm=x.shape[0], tk=512), dmix)
    dmixin = _mm_nt(dmix, w_out, "mm_out_dx", tm=1024)
    do_hg, dg_raw, datt, dd, dog, dag = _mix_in_bwd(dmixin, o_hg, proj, att, og, ag, "mix_in_bwd")
    datt_b = datt
    reordered = [_to_sub(a, fine) for a in (datt_b, lse, dd)]
    datts = _attn_bwd([(q,) + tuple((datt_b, lse, dd) if d == 1 else reordered) + (d, seg)
                       for q, (d, seg) in zip(qkvs, layouts)], "attn_bwd")
    dhg, dlb = _hgrn_bwd(proj, lb_logits, states, do_hg, "hgrn_bwd")
    dhg = late(dhg)
    dproj = _dproj(dhg, dg_raw, [natural(dq, d) for (dq, _), d in zip(datts, DILATIONS)],
                   [natural(dkv, d) for (_, dkv), d in zip(datts, DILATIONS)], "dproj")
    dproj = put_g("w_in", *_mm_tn(h1, dproj, N_SHARD, "mm_in_dw", tm=x.shape[0], tk=512), dproj)
    dh1 = _mm_nt(dproj, w_in, "mm_in_dx", tm=1024)
    dx, dshift1, dscale1, dg1 = _norm_mod_bwd(dh1, x, norm1_g, scale1, dx2, "norm_mod1_bwd")

    stats = jnp.concatenate([loss_v, dfg, dg2, dg1, dlb, dag, dog,
                             dshift1, dscale1, dgate1, dshift2, dscale2, dgate2], axis=1)
    return dx, stats


def _place():
    x, y, c = lax.axis_index("x"), lax.axis_index("y"), lax.axis_index("c")
    return x, y, c


def _chip_peers(x, y, c):
    return [(1 - x, y, c), (x, 1 - y, c), (1 - x, 1 - y, c)]


def _comm_call(body, name, n_in, out_shape, scratch_shapes):
    hbm = pl.BlockSpec(memory_space=pl.ANY)
    return pl.pallas_call(body, name=name, in_specs=[hbm] * n_in, out_specs=[hbm] * len(out_shape),
                          out_shape=out_shape, scratch_shapes=scratch_shapes)


_HBM = pl.BlockSpec(memory_space=pltpu.HBM)
_SEM = pl.BlockSpec(memory_space=pltpu.SEMAPHORE)
_EFFECT = pltpu.SideEffectType.DATAFLOW_SIDE_EFFECTING


def _exchange_copy(bufs, send, recv, j, peer, place, kind):
    x, y, c = place
    target = peer
    if kind == "gather":
        src = dst = bufs[0].at[2 * x + y]
    elif kind == "scatter":
        src, dst = bufs[0].at[2 * peer[0] + peer[1]], bufs[1].at[j]
    else:
        half = bufs[0].shape[1] // 2
        rows = pl.ds(c * half, half)
        if kind == "half":
            src = dst = bufs[0].at[2 * x + y, rows]
        else:
            src = dst = bufs[0].at[2 * peer[0] + peer[1], rows]
            target = (x, y, 1 - c)
    return pltpu.make_async_remote_copy(src_ref=src, dst_ref=dst, send_sem=send.at[j], recv_sem=recv.at[j],
                                        device_id=target, device_id_type=MESH)


def _exchange_start(groups, after, kind, name):
    sizes = [len(g) for g in groups]
    flat = [b for g in groups for b in g]
    ng, nb = len(groups), len(flat)

    def body(*refs):
        bufs, sems = refs[:nb], refs[nb + 1:nb + 1 + 2 * ng]
        x, y, c = _place()
        for j, peer in enumerate(_chip_peers(x, y, c)):
            at = 0
            for i, size in enumerate(sizes):
                _exchange_copy(bufs[at:at + size], sems[2 * i], sems[2 * i + 1], j, peer, (x, y, c), kind).start()
                at += size

    any_space = pl.BlockSpec(memory_space=pl.ANY)
    out = pl.pallas_call(
        body, name=name, in_specs=[_HBM] * nb + [any_space],
        out_specs=[_SEM] * (2 * ng) + [_HBM] * nb + [any_space],
        out_shape=[pltpu.SemaphoreType.DMA((3,))] * (2 * ng) + [pltpu.HBM(b.shape, b.dtype) for b in flat]
        + [_sds(after.shape, after.dtype)],
        input_output_aliases={i: 2 * ng + i for i in range(nb + 1)},
        compiler_params=pltpu.CompilerParams(has_side_effects=_EFFECT),
    )(*[pltpu.with_memory_space_constraint(b, pltpu.HBM) for b in flat], after)
    started, at = [], 2 * ng
    for i, size in enumerate(sizes):
        started.append((out[2 * i], out[2 * i + 1], tuple(out[at:at + size])))
        at += size
    return started, out[-1]


def _exchange_wait(started, after, kind, name):
    send, recv, bufs = started
    nb = len(bufs)

    def body(*refs):
        x, y, c = _place()
        for j, peer in enumerate(_chip_peers(x, y, c)):
            cp = _exchange_copy(refs[:nb], refs[nb], refs[nb + 1], j, peer, (x, y, c), kind)
            cp.wait_send()
            cp.wait_recv()

    return pl.pallas_call(
        body, name=name, in_specs=[_HBM] * nb + [_SEM, _SEM, pl.BlockSpec(memory_space=pl.ANY)],
        out_specs=[_HBM] * nb, out_shape=[pltpu.HBM(b.shape, b.dtype) for b in bufs],
        input_output_aliases={i: i for i in range(nb)},
        compiler_params=pltpu.CompilerParams(has_side_effects=_EFFECT),
    )(*bufs, send, recv, after)


def _sibling_copies(v_refs, l_refs, send, recv):
    x, y, c = _place()
    return [pltpu.make_async_remote_copy(src_ref=v, dst_ref=l, send_sem=send.at[a], recv_sem=recv.at[a],
                                         device_id=(x, y, 1 - c), device_id_type=MESH)
            for a, (v, l) in enumerate(zip(v_refs, l_refs))]


def _sibling_start(vs, after, name):
    vs = list(vs)
    n = len(vs)
    lands = [lax.empty(v.shape, v.dtype) for v in vs]

    def body(*refs):
        for cp in _sibling_copies(refs[:n], refs[n:2 * n], refs[2 * n + 1], refs[2 * n + 2]):
            cp.start()

    any_space = pl.BlockSpec(memory_space=pl.ANY)
    out = pl.pallas_call(
        body, name=name, in_specs=[_HBM] * (2 * n) + [any_space],
        out_specs=[_SEM, _SEM] + [_HBM] * (2 * n) + [any_space],
        out_shape=[pltpu.SemaphoreType.DMA((n,))] * 2 + [pltpu.HBM(b.shape, b.dtype) for b in vs + lands]
        + [_sds(after.shape, after.dtype)],
        input_output_aliases={i: 2 + i for i in range(2 * n + 1)},
        compiler_params=pltpu.CompilerParams(has_side_effects=_EFFECT),
    )(*[pltpu.with_memory_space_constraint(b, pltpu.HBM) for b in vs + lands], after)
    return (out[0], out[1], tuple(out[2:2 + n]), tuple(out[2 + n:2 + 2 * n])), out[-1]


def _sibling_wait(started, after, name):
    send, recv, vs, lands = started
    n = len(vs)

    def body(*refs):
        for cp in _sibling_copies(refs[:n], refs[n:2 * n], refs[2 * n], refs[2 * n + 1]):
            cp.wait_send()
            cp.wait_recv()

    out = pl.pallas_call(
        body, name=name, in_specs=[_HBM] * (2 * n) + [_SEM, _SEM, pl.BlockSpec(memory_space=pl.ANY)],
        out_specs=[_HBM] * (2 * n), out_shape=[pltpu.HBM(b.shape, b.dtype) for b in vs + lands],
        input_output_aliases={i: i for i in range(2 * n)},
        compiler_params=pltpu.CompilerParams(has_side_effects=_EFFECT),
    )(*vs, *lands, send, recv, after)
    return out[:n], out[n:]


def _swap_sibling(vs, name):
    n = len(vs)

    def body(*refs):
        v_refs, o_refs, (send, recv) = refs[:n], refs[n:2 * n], refs[2 * n:]
        x, y, c = _place()
        cps = [pltpu.make_async_remote_copy(
            src_ref=v_refs[a], dst_ref=o_refs[a], send_sem=send.at[a], recv_sem=recv.at[a],
            device_id=(x, y, 1 - c), device_id_type=MESH) for a in range(n)]
        for cp in cps:
            cp.start()
        for cp in cps:
            cp.wait()

    return _comm_call(body, name, n, [_sds(v.shape, v.dtype) for v in vs],
                      [pltpu.SemaphoreType.DMA((n,)), pltpu.SemaphoreType.DMA((n,))])(*vs)


def _gather_rows(v, name):
    r, n = v.shape

    def body(v_ref, o_ref, send, recv, loc):
        x, y, c = _place()
        me = 4 * x + 2 * y + c
        own = pltpu.make_async_copy(v_ref, o_ref.at[me], loc)
        own.start()
        peers = []
        for k in range(1, 8):
            px = 1 - x if k & 4 else x
            py = 1 - y if k & 2 else y
            pc = 1 - c if k & 1 else c
            peers.append((px, py, pc))
        sends = []
        for k, peer in enumerate(peers):
            cp = pltpu.make_async_remote_copy(src_ref=v_ref, dst_ref=o_ref.at[me], send_sem=send.at[k],
                                              recv_sem=recv.at[k], device_id=peer, device_id_type=MESH)
            cp.start()
            sends.append(cp)
        for k, peer in enumerate(peers):
            pltpu.make_async_remote_copy(src_ref=v_ref, dst_ref=o_ref.at[4 * peer[0] + 2 * peer[1] + peer[2]],
                                         send_sem=send.at[k], recv_sem=recv.at[k], device_id=peer,
                                         device_id_type=MESH).wait_recv()
        for cp in sends:
            cp.wait_send()
        own.wait()

    vmem = pl.BlockSpec(memory_space=pltpu.VMEM)
    return pl.pallas_call(body, name=name, in_specs=[vmem], out_specs=vmem, out_shape=_sds((8, r, n), v.dtype),
                          scratch_shapes=[pltpu.SemaphoreType.DMA((7,)), pltpu.SemaphoreType.DMA((7,)),
                                          pltpu.SemaphoreType.DMA])(v)


def _cast_place(ws, shard, name):
    n = len(ws)

    def body(s_ref, *refs):
        for w_ref, o_ref in zip(refs[:n], refs[n:]):
            o_ref[0] = w_ref[...].astype(BF16)

    return pl.pallas_call(
        body, name=name, out_shape=[_sds((N_SHARD,) + w.shape, BF16) for w in ws],
        grid_spec=pltpu.PrefetchScalarGridSpec(
            num_scalar_prefetch=1, grid=(4,),
            in_specs=[pl.BlockSpec((w.shape[0] // 4, w.shape[1]), lambda i, s: (i, 0)) for w in ws],
            out_specs=[pl.BlockSpec((1, w.shape[0] // 4, w.shape[1]), lambda i, s: (s[0], i, 0)) for w in ws]),
        compiler_params=pltpu.CompilerParams(dimension_semantics=("arbitrary",), vmem_limit_bytes=VMEM_LIMIT),
    )(shard.reshape(1).astype(jnp.int32), *ws)


def _mod_part(c_all, w_ada, b_ada, name):
    n = w_ada.shape[1]

    def body(c_ref, w_ref, b_ref, a_ref, p_ref):
        cv = c_ref[...]
        ca = cv * _sigmoid(cv)
        a_ref[...] = ca
        p_ref[...] = jnp.dot(ca, w_ref[...], precision=lax.Precision.HIGHEST, preferred_element_type=F32) + b_ref[...]

    full = lambda a: pl.BlockSpec(a.shape, lambda i: (0, 0))
    return _call(body, name=name, grid=(1,), in_specs=[full(c_all), full(w_ada), full(b_ada)],
                 out_specs=[pl.BlockSpec((8, D_MODEL), lambda i: (0, 0)), pl.BlockSpec((8, n), lambda i: (0, 0))],
                 out_shape=[_sds((8, D_MODEL)), _sds((8, n))])(c_all, w_ada, b_ada)


def _sum_received(gs, shard, lands, name):
    n = len(gs)

    def body(s_ref, *refs):
        for g_ref, l_ref, o_ref in zip(refs[:n], refs[n:2 * n], refs[2 * n:]):
            o_ref[...] = ((g_ref[0] + l_ref[0].astype(F32)) + l_ref[1].astype(F32)) + l_ref[2].astype(F32)

    quarter = lambda g: (g.shape[1] // 4, g.shape[2])
    return pl.pallas_call(
        body, name=name, out_shape=[_sds(g.shape[1:]) for g in gs],
        grid_spec=pltpu.PrefetchScalarGridSpec(
            num_scalar_prefetch=1, grid=(4,),
            in_specs=[pl.BlockSpec((1,) + quarter(g), lambda i, s: (s[0], i, 0)) for g in gs]
            + [pl.BlockSpec((3,) + quarter(g), lambda i, s: (0, i, 0)) for g in gs],
            out_specs=[pl.BlockSpec(quarter(g), lambda i, s: (i, 0)) for g in gs]),
        compiler_params=pltpu.CompilerParams(dimension_semantics=("arbitrary",), vmem_limit_bytes=VMEM_LIMIT),
    )(shard.reshape(1).astype(jnp.int32), *gs, *lands)


def _adamw_outer(w, ct, dm, m, v, name):
    k, n = w.shape
    tr = k // 4

    def body(w_ref, c_ref, d_ref, m_ref, v_ref, g_out, d_out, m_out, v_out):
        cv = c_ref[...]
        dv = d_ref[...]
        g = cv[:, 0:1] * dv[0:1, :]
        for i in range(1, 8):
            g = g + cv[:, i:i + 1] * dv[i:i + 1, :]
        g_out[...] = g
        d_out[...], m_out[...], v_out[...] = _adamw_math(w_ref[...], g, m_ref[...], v_ref[...])

    row = _rows(tr, n)
    return _call(body, name=name, grid=(4,),
                 in_specs=[row, _rows(tr, 8), pl.BlockSpec((8, n), lambda i: (0, 0)), row, row],
                 out_specs=[row] * 4, out_shape=[_sds((k, n))] * 4)(w, ct, dm, m, v)


def _adamw_math(w, g, m, v):
    m_new = ADAM_B1 * m + (1.0 - ADAM_B1) * g
    v_new = ADAM_B2 * v + (1.0 - ADAM_B2) * (g * g)
    m_hat = m_new / (1.0 - ADAM_B1 ** ADAM_STEP)
    v_hat = v_new / (1.0 - ADAM_B2 ** ADAM_STEP)
    return -ADAM_LR * (m_hat / (jnp.sqrt(v_hat) + ADAM_EPS) + ADAM_WD * w), m_new, v_new


def _small_update(stats, smalls, name):
    offsets = [ST_DMOD, ST_DG1, ST_DLB, ST_DOG, ST_DAG, ST_DG2, ST_DFG]
    lb_index = 2

    def body(*refs):
        s_ref, ins, l_ref, outs = refs[0], refs[1:22], refs[22], refs[23:]
        tot = s_ref[0:1, :]
        for i in range(1, 8):
            tot = tot + s_ref[i:i + 1, :]
        l_ref[...] = jnp.zeros((1, 128), F32) + (0.5 / D_MODEL) * jnp.sum(tot[:, ST_LOSS:ST_LOSS + D_MODEL])
        for p, off in enumerate(offsets):
            w_ref, m_ref, v_ref = ins[3 * p:3 * p + 3]
            g_out, d_out, m_out, v_out = outs[4 * p:4 * p + 4]
            g = tot[:, off:off + w_ref.shape[1]]
            if p == lb_index:
                lg = w_ref[...]
                lb = _sigmoid(lg[0:1] - lg[1:2])
                g = g * lb * (1.0 - lb)
            for r in range(w_ref.shape[0]):
                rows = slice(r, r + 1)
                gr = g if r == 0 else -g
                delta, m_new, v_new = _adamw_math(w_ref[rows, :], gr, m_ref[rows, :], v_ref[rows, :])
                g_out[rows, :] = gr
                d_out[rows, :] = delta
                m_out[rows, :] = m_new
                v_out[rows, :] = v_new

    full = lambda a: pl.BlockSpec(a.shape, lambda i: (0, 0))
    flat = [a for t in smalls for a in t]
    return _call(body, name=name, grid=(1,),
                 in_specs=[full(stats)] + [full(a) for a in flat],
                 out_specs=[pl.BlockSpec((1, 128), lambda i: (0, 0))] + [full(t[0]) for t in smalls for _ in range(4)],
                 out_shape=[_sds((1, 128))] + [_sds(t[0].shape) for t in smalls for _ in range(4)])(stats, *flat)


def _adamw(params, name):
    n = len(params)

    def body(*refs):
        for p in range(n):
            w_ref, ga_ref, gb_ref, m_ref, v_ref = refs[5 * p:5 * p + 5]
            g_out, d_out, m_out, v_out = refs[5 * n + 4 * p:5 * n + 4 * p + 4]
            g = ga_ref[...] + gb_ref[...]
            g_out[...] = g
            d_out[...], m_out[...], v_out[...] = _adamw_math(w_ref[...], g, m_ref[...], v_ref[...])

    row = lambda w: _rows(w.shape[0] // 4, w.shape[1])
    out = _call(body, name=name, grid=(4,), in_specs=[row(p[0]) for p in params for _ in range(5)],
                out_specs=[row(p[0]) for p in params for _ in range(4)],
                out_shape=[_sds(p[0].shape) for p in params for _ in range(4)])(*[a for p in params for a in p])
    return [tuple(out[4 * p:4 * p + 4]) for p in range(n)]


def kernel(x, c, w_ada, b_ada, norm1_g, w_in, hg_lb_logits, hg_onorm_g, att_onorm_g, w_out, norm2_g, w_gate_up, w_down, final_g, loss_target, m_w_ada, m_b_ada, m_norm1_g, m_w_in, m_hg_lb_logits, m_hg_onorm_g, m_att_onorm_g, m_w_out, m_norm2_g, m_w_gate_up, m_w_down, m_final_g, v_w_ada, v_b_ada, v_norm1_g, v_w_in, v_hg_lb_logits, v_hg_onorm_g, v_att_onorm_g, v_w_out, v_norm2_g, v_w_gate_up, v_w_down, v_final_g):
    ix, iy, ic = _place()
    shard = 2 * ix + iy
    sample = 4 * ix + 2 * iy + ic
    n_ada = w_ada.shape[2]

    shards = [w_in[0], w_out[0], w_gate_up[0], w_down[0]]
    names = ["w_in", "w_out", "w_gu", "w_down"]
    shapes = [(N_SHARD,) + w.shape for w in shards]
    placed = [(_cast_place(shards[:1], shard, "place_w_in")[0],)]
    placed += [(p,) for p in _cast_place(shards[1:], shard, "place_rest")]

    c_all = _gather_rows(c, "gather_c").reshape(8, D_MODEL)
    b_part = lax.dynamic_slice(b_ada, (0, shard * n_ada), (1, n_ada))
    c_act, part = _mod_part(c_all, w_ada[0], b_part, "mod_part")
    parts = _gather_rows(part, "gather_mod")[::2]
    mod = lax.dynamic_index_in_dim(parts, sample, axis=1, keepdims=False).reshape(1, 6 * D_MODEL)
    (first,), mod = _exchange_start(placed[:1], mod, "half", "gather_start_w_in")
    gathering = {}

    def get_w(name, after):
        if name == "w_in":
            halves = _exchange_wait(first, after, "half", "gather_wait_w_in")
            (passing,), token = _exchange_start([tuple(halves)], mod, "forward", "forward_start_w_in")
            (full,) = _exchange_wait(passing, token, "forward", "forward_wait_w_in")
            rest, full = _exchange_start(placed[1:], full, "gather", "gather_start_rest")
            gathering.update(zip(names[1:], rest))
            return full
        (full,) = _exchange_wait(gathering[name], after, "gather", "gather_wait_" + name)
        return full if name == "w_gu" else full.reshape(1, -1, D_MODEL)

    scattering = {}

    def put_g(name, g, g_bf16, then):
        shape = shapes[names.index(name)]
        land = lax.empty((3,) + shape[1:], BF16)
        (started,), then = _exchange_start([(g_bf16.reshape(shape), land)], then, "scatter", "scatter_start_" + name)
        scattering[name] = (g.reshape(shape), started)
        return then

    def summed(group, after, tag):
        lands = [_exchange_wait(scattering[nm][1], after, "scatter", "scatter_wait_" + nm)[1] for nm in group]
        return _sum_received([scattering[nm][0] for nm in group], shard, lands, "sum_" + tag)

    early = ["w_down", "w_gu", "w_out"]
    swapping = []

    def late(a):
        started, a = _sibling_start(summed(early, a, "early"), a, "swap_start")
        swapping.append(started)
        return a

    dx, stats = _local_step(x[0], loss_target[0], mod, norm1_g, hg_lb_logits, hg_onorm_g, att_onorm_g,
                            norm2_g, final_g, get_w, put_g, late)

    stats_all = _gather_rows(stats, "gather_stats").reshape(8, ST_WIDTH)
    dmod = lax.dynamic_slice(stats_all, (0, ST_DMOD + shard * n_ada), (8, n_ada))

    as_row = lambda a: a.reshape(1, -1) if a.ndim == 1 else a
    smalls = [tuple(as_row(a) for a in t) for t in [
        (b_ada, m_b_ada, v_b_ada), (norm1_g, m_norm1_g, v_norm1_g),
        (hg_lb_logits, m_hg_lb_logits, v_hg_lb_logits), (hg_onorm_g, m_hg_onorm_g, v_hg_onorm_g),
        (att_onorm_g, m_att_onorm_g, v_att_onorm_g), (norm2_g, m_norm2_g, v_norm2_g),
        (final_g, m_final_g, v_final_g)]]
    loss, *small_out = _small_update(stats_all, smalls, "small_update")
    shapes_out = [b_ada.shape, norm1_g.shape, hg_lb_logits.shape, hg_onorm_g.shape, att_onorm_g.shape,
                  norm2_g.shape, final_g.shape]
    sg, sd, sm, sv = [[small_out[4 * p + i].reshape(shapes_out[p]) for p in range(7)] for i in range(4)]

    ada = _adamw_outer(w_ada[0], c_act.T, dmod, m_w_ada[0], v_w_ada[0], "adamw_w_ada")
    moments = [(m_w_in, v_w_in), (m_w_out, v_w_out), (m_w_gate_up, v_w_gate_up), (m_w_down, v_w_down)]

    def update(group, sums, other, tag):
        params = [(shards[names.index(nm)], s, o, moments[names.index(nm)][0][0], moments[names.index(nm)][1][0])
                  for nm, s, o in zip(group, sums, other)]
        return dict(zip(group, _adamw(params, "adamw_" + tag)))

    sums, other = _sibling_wait(swapping[0], ada[1], "swap_wait")
    done = update(early, sums, other, "early")
    sum_in = summed(["w_in"], done["w_out"][1], "w_in")
    done.update(update(["w_in"], sum_in, _swap_sibling(sum_in, "swap_sum_in"), "w_in"))
    big = [ada] + [done[nm] for nm in names]
    bg, bd, bm, bv = [[t[i][None] for t in big] for i in range(4)]

    def order(b, s):
        return [b[0], s[0], s[1], b[1], s[2], s[3], s[4], b[2], s[5], b[3], b[4], s[6]]

    return (loss[0, 0], dx[None], *order(bg, sg), *order(bd, sd), *order(bm, sm), *order(bv, sv))
```

```python
import functools

import jax
import jax.numpy as jnp
from jax import lax
from jax.experimental import pallas as pl
from jax.experimental.pallas import tpu as pltpu

F32 = jnp.float32
BF16 = jnp.bfloat16
MESH = pl.DeviceIdType.MESH

D_MODEL = 1024
HG_WIDTH = 512
HG_HEAD = 128
HG_CHUNK = 64
HG_GROUP = 4
ATT_WIDTH = 512
ATT_HEADS = 8
ATT_BLOCK = 128
DILATIONS = (1, 4, 16)
D_FF = 2816
IN_WIDTH = 3584
N_SHARD = 4
RMS_EPS = 1e-6
NEG = -1e30

ADAM_LR = 0.001
ADAM_B1 = 0.9
ADAM_B2 = 0.999
ADAM_EPS = 1e-08
ADAM_WD = 0.01
ADAM_STEP = 10

VMEM_LIMIT = 56 * 2**20

ST_LOSS, ST_DFG, ST_DG2, ST_DG1 = 0, 1024, 2048, 3072
ST_DLB, ST_DAG, ST_DOG, ST_DMOD = 4096, 4608, 5120, 5248
ST_WIDTH = 5248 + 6144


def _call(body, *, name, grid, in_specs, out_specs, out_shape, scratch_shapes=()):
    return pl.pallas_call(
        body, name=name, grid=grid, in_specs=in_specs, out_specs=out_specs, out_shape=out_shape,
        scratch_shapes=list(scratch_shapes),
        compiler_params=pltpu.CompilerParams(
            dimension_semantics=("arbitrary",) * len(grid), vmem_limit_bytes=VMEM_LIMIT))


def _sds(shape, dtype=F32):
    return jax.ShapeDtypeStruct(shape, dtype)


def _dot(a, b):
    return jnp.dot(a, b, preferred_element_type=F32)


def _dot_nt(a, b):
    return lax.dot_general(a, b, (((1,), (1,)), ((), ())), preferred_element_type=F32)


def _dot_tn(a, b):
    return lax.dot_general(a, b, (((0,), (0,)), ((), ())), preferred_element_type=F32)


def _sigmoid(x):
    return 1.0 / (1.0 + jnp.exp(-x))


def _rows(tr, width):
    return pl.BlockSpec((tr, width), lambda i: (i, 0))


def _vec(width):
    return pl.BlockSpec((1, width), lambda i: (0, 0))


def _acc(ref, val, first):
    @pl.when(first)
    def _():
        ref[...] = val

    @pl.when(jnp.logical_not(first))
    def _():
        ref[...] += val


def _mm_nn(a, b3, name, tm=1024, out_dtype=F32):
    m, k = a.shape
    s, _, n = b3.shape

    def body(a_ref, b_ref, o_ref):
        o_ref[...] = _dot(a_ref[...], b_ref[0]).astype(out_dtype)

    return _call(
        body, name=name, grid=(s, m // tm),
        in_specs=[pl.BlockSpec((tm, k), lambda j, i: (i, 0)), pl.BlockSpec((1, k, n), lambda j, i: (j, 0, 0))],
        out_specs=pl.BlockSpec((tm, n), lambda j, i: (i, j)), out_shape=_sds((m, s * n), out_dtype))(a, b3)


def _mm_nt(dy, b3, name, tm=512):
    m = dy.shape[0]
    s, k, n = b3.shape

    def body(dy_ref, b_ref, o_ref):
        acc = _dot_nt(dy_ref[:, 0:n], b_ref[0])
        for j in range(1, s):
            acc = acc + _dot_nt(dy_ref[:, j * n:(j + 1) * n], b_ref[j])
        o_ref[...] = acc.astype(BF16)

    return _call(
        body, name=name, grid=(m // tm,),
        in_specs=[_rows(tm, s * n), pl.BlockSpec((s, k, n), lambda i: (0, 0, 0))],
        out_specs=_rows(tm, k), out_shape=_sds((m, k), BF16))(dy, b3)


def _mm_tn(a, dy, s, name, tm, tk):
    m, k = a.shape
    n = dy.shape[1] // s
    steps = m // tm

    def body(a_ref, dy_ref, o_ref, ob_ref):
        p = _dot_tn(a_ref[...], dy_ref[...])[None]
        if steps == 1:
            o_ref[...] = p
            ob_ref[...] = p.astype(BF16)
        else:
            i = pl.program_id(2)
            _acc(o_ref, p, i == 0)

            @pl.when(i == steps - 1)
            def _():
                ob_ref[...] = o_ref[...].astype(BF16)

    out = pl.BlockSpec((1, tk, n), lambda kk, j, i: (j, kk, 0))
    return _call(
        body, name=name, grid=(k // tk, s, steps),
        in_specs=[pl.BlockSpec((tm, tk), lambda kk, j, i: (i, kk)), pl.BlockSpec((tm, n), lambda kk, j, i: (i, j))],
        out_specs=[out, out], out_shape=[_sds((s, k, n)), _sds((s, k, n), BF16)])(a, dy)


def _rms(x):
    return lax.rsqrt(jnp.mean(x * x, axis=-1, keepdims=True) + RMS_EPS)


def _rms_bwd(dxh, xh, r):
    return r * (dxh - xh * jnp.mean(dxh * xh, axis=-1, keepdims=True))


def _norm_mod(x, g, scale, shift, name, tr=512):
    t = x.shape[0]

    def body(x_ref, g_ref, sc_ref, sh_ref, h_ref):
        xv = x_ref[...]
        n = xv * _rms(xv) * g_ref[...]
        h_ref[...] = (n * (1.0 + sc_ref[...]) + sh_ref[...]).astype(BF16)

    return _call(body, name=name, grid=(t // tr,),
                 in_specs=[_rows(tr, D_MODEL), _vec(D_MODEL), _vec(D_MODEL), _vec(D_MODEL)],
                 out_specs=_rows(tr, D_MODEL), out_shape=_sds((t, D_MODEL), BF16))(x, g, scale, shift)


def _mix_in(o_hg, proj, att, og, ag, name, tr=512):
    t = o_hg.shape[0]

    def body(o_ref, g_ref, a_ref, og_ref, ag_ref, m_ref):
        for h in range(HG_WIDTH // HG_HEAD):
            sl = slice(h * HG_HEAD, (h + 1) * HG_HEAD)
            oh = o_ref[:, sl].astype(F32)
            gv = g_ref[:, sl].astype(F32)
            m_ref[:, sl] = (oh * _rms(oh) * og_ref[...] * (gv * _sigmoid(gv))).astype(BF16)
        av = a_ref[...]
        m_ref[:, HG_WIDTH:] = (av * _rms(av) * ag_ref[...]).astype(BF16)

    return _call(body, name=name, grid=(t // tr,),
                 in_specs=[_rows(tr, HG_WIDTH), pl.BlockSpec((tr, HG_WIDTH), lambda i: (i, 3)), _rows(tr, ATT_WIDTH),
                           _vec(HG_HEAD), _vec(ATT_WIDTH)],
                 out_specs=_rows(tr, D_MODEL), out_shape=_sds((t, D_MODEL), BF16))(o_hg, proj, att, og, ag)


def _resid_norm_mod(x, mix, gate, g, scale, shift, name, tr=1024):
    t = x.shape[0]

    def body(x_ref, m_ref, gt_ref, g_ref, sc_ref, sh_ref, x2_ref, h_ref):
        x2 = x_ref[...] + gt_ref[...] * m_ref[...]
        x2_ref[...] = x2
        n = x2 * _rms(x2) * g_ref[...]
        h_ref[...] = (n * (1.0 + sc_ref[...]) + sh_ref[...]).astype(BF16)

    return _call(body, name=name, grid=(t // tr,),
                 in_specs=[_rows(tr, D_MODEL), _rows(tr, D_MODEL)] + [_vec(D_MODEL)] * 4,
                 out_specs=[_rows(tr, D_MODEL), _rows(tr, D_MODEL)],
                 out_shape=[_sds((t, D_MODEL)), _sds((t, D_MODEL), BF16)])(x, mix, gate, g, scale, shift)


def _mm_gate_up(h, w_gu, name, tm=1024):
    m, k = h.shape
    n = w_gu.shape[2]

    def body(h_ref, wa_ref, wu_ref, da_ref, du_ref, o_ref):
        hv = h_ref[...]
        a = _dot(hv, wa_ref[0])
        u = _dot(hv, wu_ref[0])
        sg = _sigmoid(a)
        silu = a * sg
        da_ref[...] = (u * sg * (1.0 + a * (1.0 - sg))).astype(BF16)
        du_ref[...] = silu.astype(BF16)
        o_ref[...] = (silu * u).astype(BF16)

    out = pl.BlockSpec((tm, n), lambda j, i: (i, j))
    return _call(body, name=name, grid=(2, m // tm),
                 in_specs=[pl.BlockSpec((tm, k), lambda j, i: (i, 0)), pl.BlockSpec((1, k, n), lambda j, i: (j, 0, 0)),
                           pl.BlockSpec((1, k, n), lambda j, i: (j + 2, 0, 0))],
                 out_specs=[out, out, out], out_shape=[_sds((m, 2 * n), BF16)] * 3)(h, w_gu, w_gu)


def _mm_down_dx(dffn, w_down, act_da, act_du, name, tm=512):
    m = dffn.shape[0]
    _, k, n = w_down.shape

    def body(d_ref, w_ref, da_ref, du_ref, o_ref):
        dact = _dot_nt(d_ref[...], w_ref[0])
        o_ref[:, :k] = (dact * da_ref[...].astype(F32)).astype(BF16)
        o_ref[:, k:] = (dact * du_ref[...].astype(F32)).astype(BF16)

    return _call(body, name=name, grid=(m // tm,),
                 in_specs=[_rows(tm, n), pl.BlockSpec((1, k, n), lambda i: (0, 0, 0)), _rows(tm, k), _rows(tm, k)],
                 out_specs=_rows(tm, 2 * k), out_shape=_sds((m, 2 * k), BF16))(dffn, w_down, act_da, act_du)


def _down_loss(x2, act, w_down, gate, fg, tgt, name, tr=512):
    t = x2.shape[0]
    _, k, n = w_down.shape

    def body(x_ref, a_ref, w_ref, gt_ref, fg_ref, t_ref, dx_ref, df_ref, l_ref, dfg_ref, dgt_ref):
        first = pl.program_id(0) == 0
        ffn_v = _dot(a_ref[...], w_ref[0])
        x3 = x_ref[...] + gt_ref[...] * ffn_v
        r = _rms(x3)
        xh = x3 * r
        err = xh * fg_ref[...] - t_ref[...]
        dy = err * (1.0 / D_MODEL)
        dx3 = _rms_bwd(dy * fg_ref[...], xh, r)
        dx_ref[...] = dx3
        df_ref[...] = (dx3 * gt_ref[...]).astype(BF16)
        _acc(l_ref, jnp.sum(err * err, axis=0, keepdims=True), first)
        _acc(dfg_ref, jnp.sum(dy * xh, axis=0, keepdims=True), first)
        _acc(dgt_ref, jnp.sum(dx3 * ffn_v, axis=0, keepdims=True), first)

    row, vec = _rows(tr, D_MODEL), _vec(D_MODEL)
    return _call(body, name=name, grid=(t // tr,),
                 in_specs=[row, _rows(tr, k), pl.BlockSpec((1, k, n), lambda i: (0, 0, 0)), vec, vec, row],
                 out_specs=[row, row, vec, vec, vec],
                 out_shape=[_sds((t, D_MODEL)), _sds((t, D_MODEL), BF16)] + [_sds((1, D_MODEL))] * 3)(
                     x2, act, w_down, gate, fg, tgt)


def _norm_mod_bwd(dh, x, g, scale, dres, name, gate=None, mix=None, w=None, tr=512):
    t = x.shape[0]
    below = gate is not None

    def body(*refs):
        if w is not None:
            w_ref, refs = refs[1], refs[:1] + refs[2:]
        if below:
            dh_ref, x_ref, g_ref, sc_ref, dr_ref, gt_ref, m_ref, dx_ref, dsh_ref, dsc_ref, dg_ref, dgt_ref, dm_ref = refs
        else:
            dh_ref, x_ref, g_ref, sc_ref, dr_ref, dx_ref, dsh_ref, dsc_ref, dg_ref = refs
        first = pl.program_id(0) == 0
        xv = x_ref[...]
        if w is None:
            dhv = dh_ref[...].astype(F32)
        else:
            n = w.shape[2]
            dhv = _dot_nt(dh_ref[:, 0:n], w_ref[0])
            for j in range(1, w.shape[0]):
                dhv = dhv + _dot_nt(dh_ref[:, j * n:(j + 1) * n], w_ref[j])
        r = _rms(xv)
        xh = xv * r
        dn = dhv * (1.0 + sc_ref[...])
        dx = dr_ref[...] + _rms_bwd(dn * g_ref[...], xh, r)
        dx_ref[...] = dx
        _acc(dsh_ref, jnp.sum(dhv, axis=0, keepdims=True), first)
        _acc(dsc_ref, jnp.sum(dhv * xh * g_ref[...], axis=0, keepdims=True), first)
        _acc(dg_ref, jnp.sum(dn * xh, axis=0, keepdims=True), first)
        if below:
            _acc(dgt_ref, jnp.sum(dx * m_ref[...], axis=0, keepdims=True), first)
            dm_ref[...] = (dx * gt_ref[...]).astype(BF16)

    row, vec = _rows(tr, D_MODEL), _vec(D_MODEL)
    first_specs = [row] if w is None else [_rows(tr, dh.shape[1]), pl.BlockSpec(w.shape, lambda i: (0, 0, 0))]
    in_specs = first_specs + [row, vec, vec, row] + ([vec, row] if below else [])
    out_specs = [row, vec, vec, vec] + ([vec, row] if below else [])
    out_shape = [_sds((t, D_MODEL))] + [_sds((1, D_MODEL))] * 3 + ([_sds((1, D_MODEL)), _sds((t, D_MODEL), BF16)] if below else [])
    args = ((dh,) if w is None else (dh, w)) + (x, g, scale, dres) + ((gate, mix) if below else ())
    return _call(body, name=name, grid=(t // tr,), in_specs=in_specs, out_specs=out_specs, out_shape=out_shape)(*args)


def _mix_in_bwd(dmi, o_hg, proj, att, og, ag, name, tr=512):
    t = o_hg.shape[0]

    def body(d_ref, o_ref, g_ref, a_ref, og_ref, ag_ref, do_ref, dg_ref, da_ref, dd_ref, dog_ref, dag_ref):
        first = pl.program_id(0) == 0
        dog = jnp.zeros((1, HG_HEAD), F32)
        for h in range(HG_WIDTH // HG_HEAD):
            sl = slice(h * HG_HEAD, (h + 1) * HG_HEAD)
            oh = o_ref[:, sl].astype(F32)
            gv = g_ref[:, sl].astype(F32)
            dv = d_ref[:, sl].astype(F32)
            r = _rms(oh)
            xh = oh * r
            sg = _sigmoid(gv)
            dno = dv * gv * sg
            dg_ref[:, sl] = (dv * xh * og_ref[...] * sg * (1.0 + gv * (1.0 - sg))).astype(BF16)
            dog = dog + jnp.sum(dno * xh, axis=0, keepdims=True)
            do_ref[:, sl] = _rms_bwd(dno * og_ref[...], xh, r).astype(BF16)
        _acc(dog_ref, dog, first)
        av = a_ref[...]
        dav = d_ref[:, HG_WIDTH:].astype(F32)
        r = _rms(av)
        xa = av * r
        _acc(dag_ref, jnp.sum(dav * xa, axis=0, keepdims=True), first)
        datt = _rms_bwd(dav * ag_ref[...], xa, r)
        da_ref[...] = datt.astype(BF16)
        prod = datt * av
        lane = lax.broadcasted_iota(jnp.int32, (1, 128), 1)
        dd = jnp.zeros((tr, 128), F32)
        for hp in range(ATT_HEADS // 2):
            pp = prod[:, hp * 128:(hp + 1) * 128]
            lo = jnp.sum(jnp.where(lane < 64, pp, 0.0), axis=-1, keepdims=True)
            hi = jnp.sum(jnp.where(lane >= 64, pp, 0.0), axis=-1, keepdims=True)
            dd = jnp.where(lane == 2 * hp, lo, dd)
            dd = jnp.where(lane == 2 * hp + 1, hi, dd)
        dd_ref[...] = dd

    half = _rows(tr, HG_WIDTH)
    return _call(body, name=name, grid=(t // tr,),
                 in_specs=[_rows(tr, D_MODEL), half, pl.BlockSpec((tr, HG_WIDTH), lambda i: (i, 3)), half,
                           _vec(HG_HEAD), _vec(ATT_WIDTH)],
                 out_specs=[half, half, half, _rows(tr, 128), _vec(HG_HEAD), _vec(ATT_WIDTH)],
                 out_shape=[_sds((t, HG_WIDTH), BF16)] * 3 + [_sds((t, 128)), _sds((1, HG_HEAD)), _sds((1, ATT_WIDTH))])(
                     dmi, o_hg, proj, att, og, ag)


def _dproj(dhg, dg, dqs, dkvs, name, tr=1024):
    t = dhg.shape[0]
    w3 = 3 * HG_WIDTH
    w4 = w3 + HG_WIDTH
    nbr = len(dqs)

    def body(*refs):
        h_ref, g_ref, q_refs, kv_refs, o_ref = refs[0], refs[1], refs[2:2 + nbr], refs[2 + nbr:2 + 2 * nbr], refs[-1]
        o_ref[:, :w3] = h_ref[...]
        o_ref[:, w3:w4] = g_ref[...].astype(BF16)
        o_ref[:, w4:w4 + ATT_WIDTH] = sum(r[...].astype(F32) for r in q_refs).astype(BF16)
        o_ref[:, w4 + ATT_WIDTH:] = sum(r[...].astype(F32) for r in kv_refs).astype(BF16)

    return _call(body, name=name, grid=(t // tr,),
                 in_specs=[_rows(tr, w3), _rows(tr, HG_WIDTH)] + [_rows(tr, ATT_WIDTH)] * nbr
                 + [_rows(tr, 2 * ATT_WIDTH)] * nbr,
                 out_specs=_rows(tr, IN_WIDTH), out_shape=_sds((t, IN_WIDTH), BF16))(dhg, dg, *dqs, *dkvs)


def _chunk_tri(upper):
    row = lax.broadcasted_iota(jnp.int32, (HG_GROUP, HG_CHUNK, HG_CHUNK), 1)
    col = lax.broadcasted_iota(jnp.int32, (HG_GROUP, HG_CHUNK, HG_CHUNK), 2)
    return (row <= col if upper else row >= col).astype(BF16)


def _chunk_cumsum(x, tri):
    x3 = x.reshape(HG_GROUP, HG_CHUNK, x.shape[1])
    dims = (((2,), (1,)), ((0,), (0,)))
    out = None
    for _ in range(3):
        part = x3.astype(BF16)
        x3 = x3 - part.astype(F32)
        term = lax.dot_general(tri, part, dims, preferred_element_type=F32)
        out = term if out is None else out + term
    return out.reshape(x.shape)


def _hg_gates(f_raw, q_raw, lb, tri):
    sg = _sigmoid(f_raw)
    f = lb + (1.0 - lb) * sg
    k = 1.0 - f
    b = _chunk_cumsum(jnp.log(f), tri)
    sq = _sigmoid(q_raw)
    return sg, f, k, b, sq


def _hg_masks(rows):
    row = lax.broadcasted_iota(jnp.int32, (rows, rows), 0)
    col = lax.broadcasted_iota(jnp.int32, (rows, rows), 1)
    same = (row // HG_CHUNK) == (col // HG_CHUNK)
    return jnp.logical_and(row >= col, same), jnp.logical_and(row <= col, same)


def _per_chunk(rows_of):
    return jnp.concatenate([jnp.broadcast_to(r, (HG_CHUNK, r.shape[1])) for r in rows_of], axis=0)


def _hgrn_fwd(proj, lb_logits, name):
    t = proj.shape[0]
    nc = t // HG_CHUNK
    nh = HG_WIDTH // HG_HEAD
    rows = HG_GROUP * HG_CHUNK

    def body(q_ref, f_ref, i_ref, lg_ref, o_ref, st_ref, s_scr):
        @pl.when(pl.program_id(0) == 0)
        def _():
            s_scr[...] = jnp.zeros_like(s_scr)

        lg = lg_ref[...]
        lb_all = _sigmoid(lg[0:1] - lg[1:2])
        causal, _ = _hg_masks(rows)
        tri = _chunk_tri(False)
        for h in range(nh):
            sl = slice(h * HG_HEAD, (h + 1) * HG_HEAD)
            q_raw = q_ref[:, sl].astype(F32)
            _, _, k, b, sq = _hg_gates(f_ref[:, sl].astype(F32), q_raw, lb_all[:, sl], tri)
            v = i_ref[:, sl].astype(BF16)
            gls = [b[(g + 1) * HG_CHUNK - 1:(g + 1) * HG_CHUNK] for g in range(HG_GROUP)]
            bm = _per_chunk([b[g * HG_CHUNK + HG_CHUNK // 2 - 1:g * HG_CHUNK + HG_CHUNK // 2] for g in range(HG_GROUP)])
            qd = (q_raw * sq * jnp.exp(b)).astype(BF16)
            qm = (q_raw * sq * jnp.exp(b - bm)).astype(BF16)
            km = (k * jnp.exp(bm - b)).astype(BF16)
            ke = (k * jnp.exp(_per_chunk(gls) - b)).astype(BF16)
            a = jnp.where(causal, _dot_nt(qm, km), 0.0).astype(BF16)
            o_intra = _dot(a, v)
            st = s_scr[h]
            o_inter = []
            for g in range(HG_GROUP):
                rs = slice(g * HG_CHUNK, (g + 1) * HG_CHUNK)
                st_ref[g, sl, :] = st
                o_inter.append(_dot_nt(qd[rs], st.astype(BF16)))
                st = st * jnp.exp(gls[g]) + _dot_tn(v[rs], ke[rs])
            s_scr[h] = st
            o_ref[:, sl] = (o_intra + jnp.concatenate(o_inter, axis=0)).astype(BF16)

    blk = lambda j: pl.BlockSpec((rows, HG_WIDTH), lambda c: (c, j))
    return _call(body, name=name, grid=(nc // HG_GROUP,),
                 in_specs=[blk(0), blk(1), blk(2), pl.BlockSpec((2, HG_WIDTH), lambda c: (0, 0))],
                 out_specs=[blk(0), pl.BlockSpec((HG_GROUP, HG_WIDTH, HG_HEAD), lambda c: (c, 0, 0))],
                 out_shape=[_sds((t, HG_WIDTH), BF16), _sds((nc, HG_WIDTH, HG_HEAD))],
                 scratch_shapes=[pltpu.VMEM((nh, HG_HEAD, HG_HEAD), F32)])(proj, proj, proj, lb_logits)


def _hgrn_bwd(proj, lb_logits, states, do, name):
    t = proj.shape[0]
    ng = t // (HG_GROUP * HG_CHUNK)
    nh = HG_WIDTH // HG_HEAD
    rows = HG_GROUP * HG_CHUNK

    def body(q_ref, f_ref, i_ref, lg_ref, st_ref, do_ref, d_ref, dlb_ref, ds_scr):
        first = pl.program_id(0) == 0

        @pl.when(first)
        def _():
            ds_scr[...] = jnp.zeros_like(ds_scr)

        lg = lg_ref[...]
        lb_all = _sigmoid(lg[0:1] - lg[1:2])
        causal, _ = _hg_masks(rows)
        tri = _chunk_tri(False)
        tri_t = _chunk_tri(True)
        dlb = []
        for h in range(nh):
            sl = slice(h * HG_HEAD, (h + 1) * HG_HEAD)
            q_raw = q_ref[:, sl].astype(F32)
            lb = lb_all[:, sl]
            sg, f, k, b, sq = _hg_gates(f_ref[:, sl].astype(F32), q_raw, lb, tri)
            v = i_ref[:, sl].astype(BF16)
            gls = [b[(g + 1) * HG_CHUNK - 1:(g + 1) * HG_CHUNK] for g in range(HG_GROUP)]
            bm = _per_chunk([b[g * HG_CHUNK + HG_CHUNK // 2 - 1:g * HG_CHUNK + HG_CHUNK // 2] for g in range(HG_GROUP)])
            eb = jnp.exp(b)
            ebm = jnp.exp(b - bm)
            emb = jnp.exp(bm - b)
            egb = jnp.exp(_per_chunk(gls) - b)
            ke = k * egb
            qd_b, qm_b = (q_raw * sq * eb).astype(BF16), (q_raw * sq * ebm).astype(BF16)
            km_b, ke_b = (k * emb).astype(BF16), ke.astype(BF16)
            dov = do_ref[:, sl].astype(BF16)
            a = jnp.where(causal, _dot_nt(qm_b, km_b), 0.0).astype(BF16)
            da = jnp.where(causal, _dot_nt(dov, v), 0.0).astype(BF16)
            dkm = _dot_tn(da, qm_b)
            dst = ds_scr[h]
            dqd_s, dv_s, dke_s, dgl_s = [None] * HG_GROUP, [None] * HG_GROUP, [None] * HG_GROUP, [None] * HG_GROUP
            for g in reversed(range(HG_GROUP)):
                rs = slice(g * HG_CHUNK, (g + 1) * HG_CHUNK)
                st = st_ref[g, sl, :]
                dst_b = dst.astype(BF16)
                egl = jnp.exp(gls[g])
                dqd_s[g] = _dot(dov[rs], st.astype(BF16))
                dv_s[g] = _dot_nt(ke_b[rs], dst_b)
                dke_s[g] = _dot(v[rs], dst_b)
                dgl_s[g] = jnp.sum(dst * st, axis=0, keepdims=True) * egl
                dst = _dot_tn(dov[rs], qd_b[rs]) + dst * egl
            ds_scr[h] = dst
            dqm = _dot(da, km_b)
            dqd = jnp.concatenate(dqd_s, axis=0)
            dv = _dot_tn(a, dov) + jnp.concatenate(dv_s, axis=0)
            dke = jnp.concatenate(dke_s, axis=0)
            t1 = dke * ke
            db = dqm * qm_b.astype(F32) - dkm * km_b.astype(F32) + dqd * qd_b.astype(F32) - t1
            dgl = _per_chunk([dgl_s[g] + jnp.sum(t1[g * HG_CHUNK:(g + 1) * HG_CHUNK], axis=0, keepdims=True)
                              for g in range(HG_GROUP)])
            dlf = _chunk_cumsum(db, tri_t) + dgl
            df = dlf / f - (dkm * emb + dke * egb)
            d_ref[:, sl] = ((dqm * ebm + dqd * eb) * sq * (1.0 + q_raw * (1.0 - sq))).astype(BF16)
            d_ref[:, HG_WIDTH + h * HG_HEAD:HG_WIDTH + (h + 1) * HG_HEAD] = (
                df * (1.0 - lb) * sg * (1.0 - sg)).astype(BF16)
            d_ref[:, 2 * HG_WIDTH + h * HG_HEAD:2 * HG_WIDTH + (h + 1) * HG_HEAD] = dv.astype(BF16)
            dlb.append(jnp.sum(df * (1.0 - sg), axis=0, keepdims=True))
        _acc(dlb_ref, jnp.concatenate(dlb, axis=1), first)

    rev = lambda j: pl.BlockSpec((rows, HG_WIDTH), lambda c: (ng - 1 - c, j))
    return _call(body, name=name, grid=(ng,),
                 in_specs=[rev(0), rev(1), rev(2), pl.BlockSpec((2, HG_WIDTH), lambda c: (0, 0)),
                           pl.BlockSpec((HG_GROUP, HG_WIDTH, HG_HEAD), lambda c: (ng - 1 - c, 0, 0)), rev(0)],
                 out_specs=[pl.BlockSpec((rows, 3 * HG_WIDTH), lambda c: (ng - 1 - c, 0)), _vec(HG_WIDTH)],
                 out_shape=[_sds((t, 3 * HG_WIDTH), BF16), _sds((1, HG_WIDTH))],
                 scratch_shapes=[pltpu.VMEM((nh, HG_HEAD, HG_HEAD), F32)])(proj, proj, proj, lb_logits, states, do)


def _to_sub(a, dil):
    t, w = a.shape
    return a if dil == 1 else a.reshape(t // dil, dil, w).transpose(1, 0, 2).reshape(t, w)


def _from_sub(a, dil):
    t, w = a.shape
    return a if dil == 1 else a.reshape(dil, t // dil, w).transpose(1, 0, 2).reshape(t, w)


def _att_mask(has_prev, seg):
    def place(v):
        v = v % ATT_BLOCK
        return v if seg == 1 else seg * (v % (ATT_BLOCK // seg)) + v // (ATT_BLOCK // seg)

    row = lax.broadcasted_iota(jnp.int32, (2 * ATT_BLOCK, 2 * ATT_BLOCK), 0)
    col = lax.broadcasted_iota(jnp.int32, (2 * ATT_BLOCK, 2 * ATT_BLOCK), 1)
    qi, kj = place(row), place(col)
    prev = jnp.logical_and(jnp.logical_and(col < ATT_BLOCK, kj >= qi), has_prev)
    cur = jnp.logical_and(col >= ATT_BLOCK, kj <= qi)
    return jnp.logical_or(prev, cur), lax.broadcasted_iota(jnp.int32, (1, 128), 1)


def _get(ref, sl):
    if len(ref.shape) == 2:
        return ref[:, sl]
    v = ref[:, :, sl]
    return v.reshape(ATT_BLOCK, v.shape[2])


def _put(ref, sl, val):
    if len(ref.shape) == 2:
        ref[:, sl] = val
    else:
        ref[:, :, sl] = val.reshape(ref.shape[0], ref.shape[1], val.shape[1])


def _att_spec(nb, dil, seg, width, col, back):
    bps = nb // dil

    def plain(n):
        return jnp.clip(n - back, 0, nb - 1), col

    def segmented(n):
        m = jnp.clip(n - back, 0, nb - 1)
        return 0, m // bps, m % bps, 0, col

    if seg == 1:
        return pl.BlockSpec((ATT_BLOCK, width), plain)
    return pl.BlockSpec((seg, None, None, ATT_BLOCK // seg, width), segmented)


def _att_shape(nb, dil, seg, width):
    t = nb * ATT_BLOCK
    return (t, width) if seg == 1 else (seg, dil, nb // dil, ATT_BLOCK // seg, width)


def _att_view(a, nb, dil, seg):
    return a.reshape(_att_shape(nb, dil, seg, a.shape[1]))


def _attn_fwd_block(q_ref, kc_ref, kp_ref, vc_ref, vp_ref, o_ref, l_ref, has_prev, seg):
    mask, lane = _att_mask(has_prev, seg)
    lo = lane < 64
    nq = ATT_BLOCK
    lse_all = jnp.zeros((nq, 128), F32)
    for hp in range(ATT_HEADS // 2):
        sl = slice(hp * 128, (hp + 1) * 128)
        q2 = _get(q_ref, sl)
        zero = jnp.zeros_like(q2)
        q2 = q2 * 0.125
        qs = jnp.concatenate([jnp.where(lo, q2, zero), jnp.where(lo, zero, q2)], axis=0)
        kk = jnp.concatenate([_get(kp_ref, sl), _get(kc_ref, sl)], axis=0)
        vv = jnp.concatenate([_get(vp_ref, sl), _get(vc_ref, sl)], axis=0)
        s = jnp.where(mask, _dot_nt(qs, kk), NEG)
        mx = jnp.max(s, axis=-1, keepdims=True)
        p = jnp.exp(s - mx)
        l = jnp.sum(p, axis=-1, keepdims=True)
        o = _dot(p.astype(BF16), vv) * (1.0 / l)
        _put(o_ref, sl, jnp.where(lo, o[:nq], o[nq:]).astype(BF16))
        lse = mx + jnp.log(l)
        lse_all = jnp.where(lane == 2 * hp, lse[:nq], lse_all)
        lse_all = jnp.where(lane == 2 * hp + 1, lse[nq:], lse_all)
    _put(l_ref, slice(None), lse_all)


def _attn_fwd(branches, name):
    t = branches[0][0].shape[0]
    nb = t // ATT_BLOCK
    nbr = len(branches)

    def body(*refs):
        n = pl.program_id(0)
        for i, (_, dil, seg) in enumerate(branches):
            _attn_fwd_block(*refs[5 * i:5 * i + 5], *refs[5 * nbr + 2 * i:5 * nbr + 2 * i + 2],
                            (n % (nb // dil)) != 0, seg)

    in_specs, args, out_specs, out_shape = [], [], [], []
    for qkv, dil, seg in branches:
        c0 = qkv.shape[1] // ATT_WIDTH - 3
        in_specs += [_att_spec(nb, dil, seg, ATT_WIDTH, c0 + j, back) for j, back in [(0, 0), (1, 0), (1, 1), (2, 0), (2, 1)]]
        args += [_att_view(qkv, nb, dil, seg)] * 5
        out_specs += [_att_spec(nb, dil, seg, ATT_WIDTH, 0, 0), _att_spec(nb, dil, seg, 128, 0, 0)]
        out_shape += [_sds(_att_shape(nb, dil, seg, ATT_WIDTH), BF16), _sds(_att_shape(nb, dil, seg, 128))]
    out = _call(body, name=name, grid=(nb,), in_specs=in_specs, out_specs=out_specs, out_shape=out_shape)(*args)
    return [(out[2 * i].reshape(t, ATT_WIDTH), out[2 * i + 1].reshape(t, 128)) for i in range(nbr)]


def _attn_combine(os_, ls_, name, tr=512):
    t = os_[0].shape[0]
    nbr = len(os_)

    def body(*refs):
        o_refs, l_refs, (a_ref, lt_ref) = refs[:nbr], refs[nbr:2 * nbr], refs[2 * nbr:]
        lane = lax.broadcasted_iota(jnp.int32, (1, 128), 1)
        ls = [r[...] for r in l_refs]
        mx = functools.reduce(jnp.maximum, ls)
        tot = mx + jnp.log(sum(jnp.exp(l - mx) for l in ls))
        lt_ref[...] = tot
        ws = [jnp.exp(l - tot) for l in ls]
        for hp in range(ATT_HEADS // 2):
            sl = slice(hp * 128, (hp + 1) * 128)
            acc = jnp.zeros((tr, 128), F32)
            for w, o_ref in zip(ws, o_refs):
                wf = jnp.where(lane < 64, w[:, 2 * hp:2 * hp + 1], w[:, 2 * hp + 1:2 * hp + 2])
                acc = acc + wf * o_ref[:, sl]
            a_ref[:, sl] = acc

    return _call(body, name=name, grid=(t // tr,),
                 in_specs=[_rows(tr, ATT_WIDTH)] * nbr + [_rows(tr, 128)] * nbr,
                 out_specs=[_rows(tr, ATT_WIDTH), _rows(tr, 128)],
                 out_shape=[_sds((t, ATT_WIDTH)), _sds((t, 128))])(*os_, *ls_)


def _attn_bwd_block(q_ref, kc_ref, kp_ref, vc_ref, vp_ref, do_ref, l_ref, d_ref, dq_ref, dkv_ref, carry, has_prev, seg):
    w = ATT_WIDTH
    nq = ATT_BLOCK
    mask, lane = _att_mask(has_prev, seg)
    lo = lane < 64
    lse, ddv = _get(l_ref, slice(None)), _get(d_ref, slice(None))
    for hp in range(ATT_HEADS // 2):
        sl = slice(hp * 128, (hp + 1) * 128)
        sv = slice(w + hp * 128, w + (hp + 1) * 128)
        q2, do2 = _get(q_ref, sl), _get(do_ref, sl)
        zero = jnp.zeros_like(q2)
        q2 = q2 * 0.125
        qs = jnp.concatenate([jnp.where(lo, q2, zero), jnp.where(lo, zero, q2)], axis=0)
        dos = jnp.concatenate([jnp.where(lo, do2, zero), jnp.where(lo, zero, do2)], axis=0)
        kk = jnp.concatenate([_get(kp_ref, sl), _get(kc_ref, sl)], axis=0)
        vv = jnp.concatenate([_get(vp_ref, sl), _get(vc_ref, sl)], axis=0)
        ls = jnp.concatenate([lse[:, 2 * hp:2 * hp + 1], lse[:, 2 * hp + 1:2 * hp + 2]], axis=0)
        dh = jnp.concatenate([ddv[:, 2 * hp:2 * hp + 1], ddv[:, 2 * hp + 1:2 * hp + 2]], axis=0)
        p = jnp.exp(jnp.where(mask, _dot_nt(qs, kk) - ls, NEG))
        ds = (p * (_dot_nt(dos, vv) - dh)).astype(BF16)
        dq = _dot(ds, kk) * 0.125
        _put(dq_ref, sl, jnp.where(lo, dq[:nq], dq[nq:]).astype(BF16))
        dk = _dot_tn(ds, qs)
        dv = _dot_tn(p.astype(BF16), dos)
        _put(dkv_ref, sl, (carry[:, sl] + dk[:nq]).astype(BF16))
        _put(dkv_ref, sv, (carry[:, sv] + dv[:nq]).astype(BF16))
        carry[:, sl] = dk[nq:]
        carry[:, sv] = dv[nq:]


def _attn_bwd(branches, name):
    t = branches[0][0].shape[0]
    nb = t // ATT_BLOCK
    nbr = len(branches)
    w = ATT_WIDTH

    def body(*refs):
        ins, outs, carries = refs[:8 * nbr], refs[8 * nbr:10 * nbr], refs[10 * nbr:]
        n = pl.program_id(0)

        @pl.when(n == 0)
        def _():
            for carry in carries:
                carry[...] = jnp.zeros_like(carry)

        @pl.when(n < nb)
        def _():
            for i, branch in enumerate(branches):
                dil, seg = branch[4:]
                _attn_bwd_block(*ins[8 * i:8 * i + 8], *outs[2 * i:2 * i + 2], carries[i], (n % (nb // dil)) != 0, seg)

        @pl.when(n == nb)
        def _():
            for i in range(nbr):
                _put(outs[2 * i + 1], slice(None), carries[i][...].astype(BF16))

    in_specs, args, out_specs, out_shape = [], [], [], []
    for qkv, dout, lse, dd, dil, seg in branches:
        c0 = qkv.shape[1] // w - 3
        in_specs += [_att_spec(nb, dil, seg, w, c0 + j, back) for j, back in [(0, 0), (1, 0), (1, 1), (2, 0), (2, 1)]]
        in_specs += [_att_spec(nb, dil, seg, w, 0, 0), _att_spec(nb, dil, seg, 128, 0, 0), _att_spec(nb, dil, seg, 128, 0, 0)]
        args += [_att_view(a, nb, dil, seg) for a in [qkv] * 5 + [dout, lse, dd]]
        out_specs += [_att_spec(nb, dil, seg, w, 0, 0), _att_spec(nb, dil, seg, 2 * w, 0, 1)]
        out_shape += [_sds(_att_shape(nb, dil, seg, w), BF16), _sds(_att_shape(nb, dil, seg, 2 * w), BF16)]
    out = _call(body, name=name, grid=(nb + 1,), in_specs=in_specs, out_specs=out_specs, out_shape=out_shape,
                scratch_shapes=[pltpu.VMEM((ATT_BLOCK, 2 * w), F32)] * nbr)(*args)
    return [(out[2 * i].reshape(t, w), out[2 * i + 1].reshape(t, 2 * w)) for i in range(nbr)]


def _local_step(x, tgt, mod, norm1_g, lb_logits, og, ag, norm2_g, fg, get_w, put_g, late=lambda a: a):
    shift1, scale1, gate1, shift2, scale2, gate2 = [mod[:, i * D_MODEL:(i + 1) * D_MODEL] for i in range(6)]
    fg = fg.reshape(1, D_MODEL)

    h1 = _norm_mod(x, norm1_g, scale1, shift1, "norm_mod1")
    w_in = get_w("w_in", h1)
    proj = _mm_nn(h1, w_in, "mm_in", out_dtype=BF16)
    o_hg, states = _hgrn_fwd(proj, lb_logits, "hgrn_fwd")
    fine = DILATIONS[-1]
    layouts = [(d, 1 if d == 1 else fine // d) for d in DILATIONS]
    qkv_fine = _to_sub(proj[:, 4 * HG_WIDTH:], fine)
    qkvs = [proj if d == 1 else qkv_fine for d in DILATIONS]
    natural = lambda a, d: a if d == 1 else _from_sub(a, fine)
    outs = _attn_fwd([(q, d, seg) for q, (d, seg) in zip(qkvs, layouts)], "attn_fwd")
    att, lse = _attn_combine([natural(o, d) for (o, _), d in zip(outs, DILATIONS)],
                             [natural(l, d) for (_, l), d in zip(outs, DILATIONS)], "attn_combine")
    mixin = _mix_in(o_hg, proj, att, og, ag, "mix_in")
    w_out = get_w("w_out", mixin)
    mix = _mm_nn(mixin, w_out, "mm_out")
    x2, h2 = _resid_norm_mod(x, mix, gate1, norm2_g, scale2, shift2, "resid_norm_mod2")
    w_gu = get_w("w_gu", h2)
    a_ff, u_ff, act = _mm_gate_up(h2, w_gu, "mm_gu")
    w_down = get_w("w_down", act)
    dx3, dffn, loss_v, dfg, dgate2 = _down_loss(x2, act, w_down, gate2, fg, tgt, "mm_down_loss")

    dffn = put_g("w_down", *_mm_tn(act, dffn, 1, "mm_down_dw", tm=2048, tk=D_FF // 2), dffn)
    dau = _mm_down_dx(dffn, w_down, a_ff, u_ff, "mm_down_dx")
    dau = put_g("w_gu", *_mm_tn(h2, dau, N_SHARD, "mm_gu_dw", tm=x.shape[0], tk=512), dau)
    dh2 = _mm_nt(dau, w_gu, "mm_gu_dx")
    dx2, dshift2, dscale2, dg2, dgate1, dmix = _norm_mod_bwd(
        dh2, x2, norm2_g, scale2, dx3, "norm_mod2_bwd", gate=gate1, mix=mix)
    dmix = put_g("w_out", *_mm_tn(mixin, dmix, 1, "mm_out_dw", tm=x.shape[0], tk=512), dmix)
    dmixin = _mm_nt(dmix, w_out, "mm_out_dx", tm=1024)
    do_hg, dg_raw, datt, dd, dog, dag = _mix_in_bwd(dmixin, o_hg, proj, att, og, ag, "mix_in_bwd")
    datt_b = datt
    reordered = [_to_sub(a, fine) for a in (datt_b, lse, dd)]
    datts = _attn_bwd([(q,) + tuple((datt_b, lse, dd) if d == 1 else reordered) + (d, seg)
                       for q, (d, seg) in zip(qkvs, layouts)], "attn_bwd")
    dhg, dlb = _hgrn_bwd(proj, lb_logits, states, do_hg, "hgrn_bwd")
    dhg = late(dhg)
    dproj = _dproj(dhg, dg_raw, [natural(dq, d) for (dq, _), d in zip(datts, DILATIONS)],
                   [natural(dkv, d) for (_, dkv), d in zip(datts, DILATIONS)], "dproj")
    dproj = put_g("w_in", *_mm_tn(h1, dproj, N_SHARD, "mm_in_dw", tm=x.shape[0], tk=512), dproj)
    dx, dshift1, dscale1, dg1 = _norm_mod_bwd(dproj, x, norm1_g, scale1, dx2, "mm_in_dx_norm_bwd", w=w_in)

    stats = jnp.concatenate([loss_v, dfg, dg2, dg1, dlb, dag, dog,
                             dshift1, dscale1, dgate1, dshift2, dscale2, dgate2], axis=1)
    return dx, stats


def _place():
    x, y, c = lax.axis_index("x"), lax.axis_index("y"), lax.axis_index("c")
    return x, y, c


def _chip_peers(x, y, c):
    return [(1 - x, y, c), (x, 1 - y, c), (1 - x, 1 - y, c)]


def _comm_call(body, name, n_in, out_shape, scratch_shapes):
    hbm = pl.BlockSpec(memory_space=pl.ANY)
    return pl.pallas_call(body, name=name, in_specs=[hbm] * n_in, out_specs=[hbm] * len(out_shape),
                          out_shape=out_shape, scratch_shapes=scratch_shapes)


_HBM = pl.BlockSpec(memory_space=pltpu.HBM)
_SEM = pl.BlockSpec(memory_space=pltpu.SEMAPHORE)
_EFFECT = pltpu.SideEffectType.DATAFLOW_SIDE_EFFECTING


def _exchange_copy(bufs, send, recv, j, peer, place, kind):
    x, y, c = place
    target = peer
    if kind == "gather":
        src = dst = bufs[0].at[2 * x + y]
    elif kind == "scatter":
        src, dst = bufs[0].at[2 * peer[0] + peer[1]], bufs[1].at[j]
    else:
        half = bufs[0].shape[1] // 2
        rows = pl.ds(c * half, half)
        if kind == "half":
            src = dst = bufs[0].at[2 * x + y, rows]
        else:
            src = dst = bufs[0].at[2 * peer[0] + peer[1], rows]
            target = (x, y, 1 - c)
    return pltpu.make_async_remote_copy(src_ref=src, dst_ref=dst, send_sem=send.at[j], recv_sem=recv.at[j],
                                        device_id=target, device_id_type=MESH)


def _exchange_start(groups, after, kind, name):
    sizes = [len(g) for g in groups]
    flat = [b for g in groups for b in g]
    ng, nb = len(groups), len(flat)

    def body(*refs):
        bufs, sems = refs[:nb], refs[nb + 1:nb + 1 + 2 * ng]
        x, y, c = _place()
        for j, peer in enumerate(_chip_peers(x, y, c)):
            at = 0
            for i, size in enumerate(sizes):
                _exchange_copy(bufs[at:at + size], sems[2 * i], sems[2 * i + 1], j, peer, (x, y, c), kind).start()
                at += size

    any_space = pl.BlockSpec(memory_space=pl.ANY)
    out = pl.pallas_call(
        body, name=name, in_specs=[_HBM] * nb + [any_space],
        out_specs=[_SEM] * (2 * ng) + [_HBM] * nb + [any_space],
        out_shape=[pltpu.SemaphoreType.DMA((3,))] * (2 * ng) + [pltpu.HBM(b.shape, b.dtype) for b in flat]
        + [_sds(after.shape, after.dtype)],
        input_output_aliases={i: 2 * ng + i for i in range(nb + 1)},
        compiler_params=pltpu.CompilerParams(has_side_effects=_EFFECT),
    )(*[pltpu.with_memory_space_constraint(b, pltpu.HBM) for b in flat], after)
    started, at = [], 2 * ng
    for i, size in enumerate(sizes):
        started.append((out[2 * i], out[2 * i + 1], tuple(out[at:at + size])))
        at += size
    return started, out[-1]


def _exchange_wait(started, after, kind, name):
    send, recv, bufs = started
    nb = len(bufs)

    def body(*refs):
        x, y, c = _place()
        for j, peer in enumerate(_chip_peers(x, y, c)):
            cp = _exchange_copy(refs[:nb], refs[nb], refs[nb + 1], j, peer, (x, y, c), kind)
            cp.wait_send()
            cp.wait_recv()

    return pl.pallas_call(
        body, name=name, in_specs=[_HBM] * nb + [_SEM, _SEM, pl.BlockSpec(memory_space=pl.ANY)],
        out_specs=[_HBM] * nb, out_shape=[pltpu.HBM(b.shape, b.dtype) for b in bufs],
        input_output_aliases={i: i for i in range(nb)},
        compiler_params=pltpu.CompilerParams(has_side_effects=_EFFECT),
    )(*bufs, send, recv, after)


def _sibling_copies(v_refs, l_refs, send, recv):
    x, y, c = _place()
    return [pltpu.make_async_remote_copy(src_ref=v, dst_ref=l, send_sem=send.at[a], recv_sem=recv.at[a],
                                         device_id=(x, y, 1 - c), device_id_type=MESH)
            for a, (v, l) in enumerate(zip(v_refs, l_refs))]


def _sibling_start(vs, after, name):
    vs = list(vs)
    n = len(vs)
    lands = [lax.empty(v.shape, v.dtype) for v in vs]

    def body(*refs):
        for cp in _sibling_copies(refs[:n], refs[n:2 * n], refs[2 * n + 1], refs[2 * n + 2]):
            cp.start()

    any_space = pl.BlockSpec(memory_space=pl.ANY)
    out = pl.pallas_call(
        body, name=name, in_specs=[_HBM] * (2 * n) + [any_space],
        out_specs=[_SEM, _SEM] + [_HBM] * (2 * n) + [any_space],
        out_shape=[pltpu.SemaphoreType.DMA((n,))] * 2 + [pltpu.HBM(b.shape, b.dtype) for b in vs + lands]
        + [_sds(after.shape, after.dtype)],
        input_output_aliases={i: 2 + i for i in range(2 * n + 1)},
        compiler_params=pltpu.CompilerParams(has_side_effects=_EFFECT),
    )(*[pltpu.with_memory_space_constraint(b, pltpu.HBM) for b in vs + lands], after)
    return (out[0], out[1], tuple(out[2:2 + n]), tuple(out[2 + n:2 + 2 * n])), out[-1]


def _sibling_wait(started, after, name):
    send, recv, vs, lands = started
    n = len(vs)

    def body(*refs):
        for cp in _sibling_copies(refs[:n], refs[n:2 * n], refs[2 * n], refs[2 * n + 1]):
            cp.wait_send()
            cp.wait_recv()

    out = pl.pallas_call(
        body, name=name, in_specs=[_HBM] * (2 * n) + [_SEM, _SEM, pl.BlockSpec(memory_space=pl.ANY)],
        out_specs=[_HBM] * (2 * n), out_shape=[pltpu.HBM(b.shape, b.dtype) for b in vs + lands],
        input_output_aliases={i: i for i in range(2 * n)},
        compiler_params=pltpu.CompilerParams(has_side_effects=_EFFECT),
    )(*vs, *lands, send, recv, after)
    return out[:n], out[n:]


def _swap_sibling(vs, name):
    n = len(vs)

    def body(*refs):
        v_refs, o_refs, (send, recv) = refs[:n], refs[n:2 * n], refs[2 * n:]
        x, y, c = _place()
        cps = [pltpu.make_async_remote_copy(
            src_ref=v_refs[a], dst_ref=o_refs[a], send_sem=send.at[a], recv_sem=recv.at[a],
            device_id=(x, y, 1 - c), device_id_type=MESH) for a in range(n)]
        for cp in cps:
            cp.start()
        for cp in cps:
            cp.wait()

    return _comm_call(body, name, n, [_sds(v.shape, v.dtype) for v in vs],
                      [pltpu.SemaphoreType.DMA((n,)), pltpu.SemaphoreType.DMA((n,))])(*vs)


def _everyone(x, y, c):
    return [(1 - x if k & 4 else x, 1 - y if k & 2 else y, 1 - c if k & 1 else c) for k in range(1, 8)]


def _all_gather_copies(land_ref, send, recv, arriving):
    x, y, c = _place()
    me = 4 * x + 2 * y + c
    return [pltpu.make_async_remote_copy(
        src_ref=land_ref.at[me], dst_ref=land_ref.at[4 * p[0] + 2 * p[1] + p[2] if arriving else me],
        send_sem=send.at[k], recv_sem=recv.at[k], device_id=p, device_id_type=MESH)
        for k, p in enumerate(_everyone(x, y, c))]


def _all_gather_start(v, name):
    x, y, c = _place()
    land = lax.dynamic_update_slice(lax.empty((8,) + v.shape, v.dtype), v[None], (4 * x + 2 * y + c, 0, 0))

    def body(land_ref, v_ref, send, recv, land_out, v_out):
        for cp in _all_gather_copies(land_ref, send, recv, False):
            cp.start()

    any_space = pl.BlockSpec(memory_space=pl.ANY)
    out = pl.pallas_call(
        body, name=name, in_specs=[_HBM, any_space], out_specs=[_SEM, _SEM, _HBM, any_space],
        out_shape=[pltpu.SemaphoreType.DMA((7,))] * 2 + [pltpu.HBM(land.shape, land.dtype), _sds(v.shape, v.dtype)],
        input_output_aliases={0: 2, 1: 3}, compiler_params=pltpu.CompilerParams(has_side_effects=_EFFECT),
    )(pltpu.with_memory_space_constraint(land, pltpu.HBM), v)
    return tuple(out[:3]), out[3]


def _all_gather_wait(started, after, name):
    send, recv, land = started

    def body(land_ref, send, recv, after_ref, land_out):
        for cp in _all_gather_copies(land_ref, send, recv, True):
            cp.wait_send()
            cp.wait_recv()

    return pl.pallas_call(
        body, name=name, in_specs=[_HBM, _SEM, _SEM, pl.BlockSpec(memory_space=pl.ANY)], out_specs=_HBM,
        out_shape=pltpu.HBM(land.shape, land.dtype), input_output_aliases={0: 0},
        compiler_params=pltpu.CompilerParams(has_side_effects=_EFFECT),
    )(land, send, recv, after)


def _gather_rows(v, name):
    r, n = v.shape

    def body(v_ref, o_ref, send, recv, loc):
        x, y, c = _place()
        me = 4 * x + 2 * y + c
        own = pltpu.make_async_copy(v_ref, o_ref.at[me], loc)
        own.start()
        peers = []
        for k in range(1, 8):
            px = 1 - x if k & 4 else x
            py = 1 - y if k & 2 else y
            pc = 1 - c if k & 1 else c
            peers.append((px, py, pc))
        sends = []
        for k, peer in enumerate(peers):
            cp = pltpu.make_async_remote_copy(src_ref=v_ref, dst_ref=o_ref.at[me], send_sem=send.at[k],
                                              recv_sem=recv.at[k], device_id=peer, device_id_type=MESH)
            cp.start()
            sends.append(cp)
        for k, peer in enumerate(peers):
            pltpu.make_async_remote_copy(src_ref=v_ref, dst_ref=o_ref.at[4 * peer[0] + 2 * peer[1] + peer[2]],
                                         send_sem=send.at[k], recv_sem=recv.at[k], device_id=peer,
                                         device_id_type=MESH).wait_recv()
        for cp in sends:
            cp.wait_send()
        own.wait()

    vmem = pl.BlockSpec(memory_space=pltpu.VMEM)
    return pl.pallas_call(body, name=name, in_specs=[vmem], out_specs=vmem, out_shape=_sds((8, r, n), v.dtype),
                          scratch_shapes=[pltpu.SemaphoreType.DMA((7,)), pltpu.SemaphoreType.DMA((7,)),
                                          pltpu.SemaphoreType.DMA])(v)


def _cast_place(ws, shard, name):
    n = len(ws)

    def body(s_ref, *refs):
        for w_ref, o_ref in zip(refs[:n], refs[n:]):
            o_ref[0] = w_ref[...].astype(BF16)

    return pl.pallas_call(
        body, name=name, out_shape=[_sds((N_SHARD,) + w.shape, BF16) for w in ws],
        grid_spec=pltpu.PrefetchScalarGridSpec(
            num_scalar_prefetch=1, grid=(4,),
            in_specs=[pl.BlockSpec((w.shape[0] // 4, w.shape[1]), lambda i, s: (i, 0)) for w in ws],
            out_specs=[pl.BlockSpec((1, w.shape[0] // 4, w.shape[1]), lambda i, s: (s[0], i, 0)) for w in ws]),
        compiler_params=pltpu.CompilerParams(dimension_semantics=("arbitrary",), vmem_limit_bytes=VMEM_LIMIT),
    )(shard.reshape(1).astype(jnp.int32), *ws)


def _mod_part(c_all, w_ada, b_ada, name):
    n = w_ada.shape[1]

    def body(c_ref, w_ref, b_ref, a_ref, p_ref):
        cv = c_ref[...]
        ca = cv * _sigmoid(cv)
        a_ref[...] = ca
        p_ref[...] = jnp.dot(ca, w_ref[...], precision=lax.Precision.HIGHEST, preferred_element_type=F32) + b_ref[...]

    full = lambda a: pl.BlockSpec(a.shape, lambda i: (0, 0))
    return _call(body, name=name, grid=(1,), in_specs=[full(c_all), full(w_ada), full(b_ada)],
                 out_specs=[pl.BlockSpec((8, D_MODEL), lambda i: (0, 0)), pl.BlockSpec((8, n), lambda i: (0, 0))],
                 out_shape=[_sds((8, D_MODEL)), _sds((8, n))])(c_all, w_ada, b_ada)


def _sum_received(gs, shard, lands, name):
    n = len(gs)

    def body(s_ref, *refs):
        for g_ref, l_ref, o_ref in zip(refs[:n], refs[n:2 * n], refs[2 * n:]):
            o_ref[...] = ((g_ref[0] + l_ref[0].astype(F32)) + l_ref[1].astype(F32)) + l_ref[2].astype(F32)

    quarter = lambda g: (g.shape[1] // 4, g.shape[2])
    return pl.pallas_call(
        body, name=name, out_shape=[_sds(g.shape[1:]) for g in gs],
        grid_spec=pltpu.PrefetchScalarGridSpec(
            num_scalar_prefetch=1, grid=(4,),
            in_specs=[pl.BlockSpec((1,) + quarter(g), lambda i, s: (s[0], i, 0)) for g in gs]
            + [pl.BlockSpec((3,) + quarter(g), lambda i, s: (0, i, 0)) for g in gs],
            out_specs=[pl.BlockSpec(quarter(g), lambda i, s: (i, 0)) for g in gs]),
        compiler_params=pltpu.CompilerParams(dimension_semantics=("arbitrary",), vmem_limit_bytes=VMEM_LIMIT),
    )(shard.reshape(1).astype(jnp.int32), *gs, *lands)


def _adamw_outer(w, ct, dm, m, v, name):
    k, n = w.shape
    tr = k // 4

    def body(w_ref, c_ref, d_ref, m_ref, v_ref, g_out, d_out, m_out, v_out):
        cv = c_ref[...]
        dv = d_ref[...]
        g = cv[:, 0:1] * dv[0:1, :]
        for i in range(1, 8):
            g = g + cv[:, i:i + 1] * dv[i:i + 1, :]
        g_out[...] = g
        d_out[...], m_out[...], v_out[...] = _adamw_math(w_ref[...], g, m_ref[...], v_ref[...])

    row = _rows(tr, n)
    return _call(body, name=name, grid=(4,),
                 in_specs=[row, _rows(tr, 8), pl.BlockSpec((8, n), lambda i: (0, 0)), row, row],
                 out_specs=[row] * 4, out_shape=[_sds((k, n))] * 4)(w, ct, dm, m, v)


def _adamw_math(w, g, m, v):
    m_new = ADAM_B1 * m + (1.0 - ADAM_B1) * g
    v_new = ADAM_B2 * v + (1.0 - ADAM_B2) * (g * g)
    m_hat = m_new / (1.0 - ADAM_B1 ** ADAM_STEP)
    v_hat = v_new / (1.0 - ADAM_B2 ** ADAM_STEP)
    return -ADAM_LR * (m_hat / (jnp.sqrt(v_hat) + ADAM_EPS) + ADAM_WD * w), m_new, v_new


def _small_update(stats, smalls, name):
    offsets = [ST_DMOD, ST_DG1, ST_DLB, ST_DOG, ST_DAG, ST_DG2, ST_DFG]
    lb_index = 2

    def body(*refs):
        s_ref, ins, l_ref, outs = refs[0], refs[1:22], refs[22], refs[23:]
        tot = s_ref[0:1, :]
        for i in range(1, 8):
            tot = tot + s_ref[i:i + 1, :]
        l_ref[...] = jnp.zeros((1, 128), F32) + (0.5 / D_MODEL) * jnp.sum(tot[:, ST_LOSS:ST_LOSS + D_MODEL])
        for p, off in enumerate(offsets):
            w_ref, m_ref, v_ref = ins[3 * p:3 * p + 3]
            g_out, d_out, m_out, v_out = outs[4 * p:4 * p + 4]
            g = tot[:, off:off + w_ref.shape[1]]
            if p == lb_index:
                lg = w_ref[...]
                lb = _sigmoid(lg[0:1] - lg[1:2])
                g = g * lb * (1.0 - lb)
            for r in range(w_ref.shape[0]):
                rows = slice(r, r + 1)
                gr = g if r == 0 else -g
                delta, m_new, v_new = _adamw_math(w_ref[rows, :], gr, m_ref[rows, :], v_ref[rows, :])
                g_out[rows, :] = gr
                d_out[rows, :] = delta
                m_out[rows, :] = m_new
                v_out[rows, :] = v_new

    full = lambda a: pl.BlockSpec(a.shape, lambda i: (0, 0))
    flat = [a for t in smalls for a in t]
    return _call(body, name=name, grid=(1,),
                 in_specs=[full(stats)] + [full(a) for a in flat],
                 out_specs=[pl.BlockSpec((1, 128), lambda i: (0, 0))] + [full(t[0]) for t in smalls for _ in range(4)],
                 out_shape=[_sds((1, 128))] + [_sds(t[0].shape) for t in smalls for _ in range(4)])(stats, *flat)


def _adamw(params, name):
    n = len(params)

    def body(*refs):
        for p in range(n):
            w_ref, ga_ref, gb_ref, m_ref, v_ref = refs[5 * p:5 * p + 5]
            g_out, d_out, m_out, v_out = refs[5 * n + 4 * p:5 * n + 4 * p + 4]
            g = ga_ref[...] + gb_ref[...]
            g_out[...] = g
            d_out[...], m_out[...], v_out[...] = _adamw_math(w_ref[...], g, m_ref[...], v_ref[...])

    row = lambda w: _rows(w.shape[0] // 4, w.shape[1])
    out = _call(body, name=name, grid=(4,), in_specs=[row(p[0]) for p in params for _ in range(5)],
                out_specs=[row(p[0]) for p in params for _ in range(4)],
                out_shape=[_sds(p[0].shape) for p in params for _ in range(4)])(*[a for p in params for a in p])
    return [tuple(out[4 * p:4 * p + 4]) for p in range(n)]


def kernel(x, c, w_ada, b_ada, norm1_g, w_in, hg_lb_logits, hg_onorm_g, att_onorm_g, w_out, norm2_g, w_gate_up, w_down, final_g, loss_target, m_w_ada, m_b_ada, m_norm1_g, m_w_in, m_hg_lb_logits, m_hg_onorm_g, m_att_onorm_g, m_w_out, m_norm2_g, m_w_gate_up, m_w_down, m_final_g, v_w_ada, v_b_ada, v_norm1_g, v_w_in, v_hg_lb_logits, v_hg_onorm_g, v_att_onorm_g, v_w_out, v_norm2_g, v_w_gate_up, v_w_down, v_final_g):
    ix, iy, ic = _place()
    shard = 2 * ix + iy
    sample = 4 * ix + 2 * iy + ic
    n_ada = w_ada.shape[2]

    shards = [w_in[0], w_out[0], w_gate_up[0], w_down[0]]
    names = ["w_in", "w_out", "w_gu", "w_down"]
    shapes = [(N_SHARD,) + w.shape for w in shards]
    placed = [(_cast_place(shards[:1], shard, "place_w_in")[0],)]
    placed += [(p,) for p in _cast_place(shards[1:], shard, "place_rest")]

    c_all = _gather_rows(c, "gather_c").reshape(8, D_MODEL)
    b_part = lax.dynamic_slice(b_ada, (0, shard * n_ada), (1, n_ada))
    c_act, part = _mod_part(c_all, w_ada[0], b_part, "mod_part")
    parts = _gather_rows(part, "gather_mod")[::2]
    mod = lax.dynamic_index_in_dim(parts, sample, axis=1, keepdims=False).reshape(1, 6 * D_MODEL)
    (first,), mod = _exchange_start(placed[:1], mod, "half", "gather_start_w_in")
    gathering = {}

    def get_w(name, after):
        if name == "w_in":
            halves = _exchange_wait(first, after, "half", "gather_wait_w_in")
            (passing,), token = _exchange_start([tuple(halves)], mod, "forward", "forward_start_w_in")
            (full,) = _exchange_wait(passing, token, "forward", "forward_wait_w_in")
            rest, full = _exchange_start(placed[1:], full, "gather", "gather_start_rest")
            gathering.update(zip(names[1:], rest))
            return full
        (full,) = _exchange_wait(gathering[name], after, "gather", "gather_wait_" + name)
        return full if name == "w_gu" else full.reshape(1, -1, D_MODEL)

    scattering = {}

    def put_g(name, g, g_bf16, then):
        shape = shapes[names.index(name)]
        land = lax.empty((3,) + shape[1:], BF16)
        (started,), then = _exchange_start([(g_bf16.reshape(shape), land)], then, "scatter", "scatter_start_" + name)
        scattering[name] = (g.reshape(shape), started)
        return then

    def summed(group, after, tag):
        lands = [_exchange_wait(scattering[nm][1], after, "scatter", "scatter_wait_" + nm)[1] for nm in group]
        return _sum_received([scattering[nm][0] for nm in group], shard, lands, "sum_" + tag)

    early = ["w_down", "w_gu", "w_out"]
    swapping = []

    def late(a):
        started, a = _sibling_start(summed(early, a, "early"), a, "swap_start")
        swapping.append(started)
        return a

    dx, stats = _local_step(x[0], loss_target[0], mod, norm1_g, hg_lb_logits, hg_onorm_g, att_onorm_g,
                            norm2_g, final_g, get_w, put_g, late)

    gathering_stats, stats = _all_gather_start(stats, "stats_start")
    moments = [(m_w_in, v_w_in), (m_w_out, v_w_out), (m_w_gate_up, v_w_gate_up), (m_w_down, v_w_down)]

    def update(group, sums, other, tag):
        params = [(shards[names.index(nm)], s, o, moments[names.index(nm)][0][0], moments[names.index(nm)][1][0])
                  for nm, s, o in zip(group, sums, other)]
        return dict(zip(group, _adamw(params, "adamw_" + tag)))

    sums, other = _sibling_wait(swapping[0], stats, "swap_wait")
    done = update(early, sums, other, "early")
    sum_in = summed(["w_in"], done["w_out"][1], "w_in")
    done.update(update(["w_in"], sum_in, _swap_sibling(sum_in, "swap_sum_in"), "w_in"))

    stats_all = _all_gather_wait(gathering_stats, done["w_in"][1], "stats_wait").reshape(8, ST_WIDTH)
    dmod = lax.dynamic_slice(stats_all, (0, ST_DMOD + shard * n_ada), (8, n_ada))

    as_row = lambda a: a.reshape(1, -1) if a.ndim == 1 else a
    smalls = [tuple(as_row(a) for a in t) for t in [
        (b_ada, m_b_ada, v_b_ada), (norm1_g, m_norm1_g, v_norm1_g),
        (hg_lb_logits, m_hg_lb_logits, v_hg_lb_logits), (hg_onorm_g, m_hg_onorm_g, v_hg_onorm_g),
        (att_onorm_g, m_att_onorm_g, v_att_onorm_g), (norm2_g, m_norm2_g, v_norm2_g),
        (final_g, m_final_g, v_final_g)]]
    loss, *small_out = _small_update(stats_all, smalls, "small_update")
    shapes_out = [b_ada.shape, norm1_g.shape, hg_lb_logits.shape, hg_onorm_g.shape, att_onorm_g.shape,
                  norm2_g.shape, final_g.shape]
    sg, sd, sm, sv = [[small_out[4 * p + i].reshape(shapes_out[p]) for p in range(7)] for i in range(4)]

    ada = _adamw_outer(w_ada[0], c_act.T, dmod, m_w_ada[0], v_w_ada[0], "adamw_w_ada")
    big = [ada] + [done[nm] for nm in names]
    bg, bd, bm, bv = [[t[i][None] for t in big] for i in range(4)]

    def order(b, s):
        return [b[0], s[0], s[1], b[1], s[2], s[3], s[4], b[2], s[5], b[3], b[4], s[6]]

    return (loss[0, 0], dx[None], *order(bg, sg), *order(bd, sd), *order(bm, sm), *order(bv, sv))
```

```python
import functools

import jax
import jax.numpy as jnp
from jax import lax
from jax.experimental import pallas as pl
from jax.experimental.pallas import tpu as pltpu

F32 = jnp.float32
BF16 = jnp.bfloat16
MESH = pl.DeviceIdType.MESH

D_MODEL = 1024
HG_WIDTH = 512
HG_HEAD = 128
HG_CHUNK = 64
HG_GROUP = 4
ATT_WIDTH = 512
ATT_HEADS = 8
ATT_BLOCK = 128
DILATIONS = (1, 4, 16)
D_FF = 2816
IN_WIDTH = 3584
N_SHARD = 4
RMS_EPS = 1e-6
NEG = -1e30

ADAM_LR = 0.001
ADAM_B1 = 0.9
ADAM_B2 = 0.999
ADAM_EPS = 1e-08
ADAM_WD = 0.01
ADAM_STEP = 10

VMEM_LIMIT = 56 * 2**20

ST_LOSS, ST_DFG, ST_DG2, ST_DG1 = 0, 1024, 2048, 3072
ST_DLB, ST_DAG, ST_DOG, ST_DMOD = 4096, 4608, 5120, 5248
ST_WIDTH = 5248 + 6144


def _call(body, *, name, grid, in_specs, out_specs, out_shape, scratch_shapes=()):
    return pl.pallas_call(
        body, name=name, grid=grid, in_specs=in_specs, out_specs=out_specs, out_shape=out_shape,
        scratch_shapes=list(scratch_shapes),
        compiler_params=pltpu.CompilerParams(
            dimension_semantics=("arbitrary",) * len(grid), vmem_limit_bytes=VMEM_LIMIT))


def _sds(shape, dtype=F32):
    return jax.ShapeDtypeStruct(shape, dtype)


def _dot(a, b):
    return jnp.dot(a, b, preferred_element_type=F32)


def _dot_nt(a, b):
    return lax.dot_general(a, b, (((1,), (1,)), ((), ())), preferred_element_type=F32)


def _dot_tn(a, b):
    return lax.dot_general(a, b, (((0,), (0,)), ((), ())), preferred_element_type=F32)


def _sigmoid(x):
    return 1.0 / (1.0 + jnp.exp(-x))


def _rows(tr, width):
    return pl.BlockSpec((tr, width), lambda i: (i, 0))


def _vec(width):
    return pl.BlockSpec((1, width), lambda i: (0, 0))


def _acc(ref, val, first):
    @pl.when(first)
    def _():
        ref[...] = val

    @pl.when(jnp.logical_not(first))
    def _():
        ref[...] += val


def _mm_nn(a, b3, name, tm=1024, out_dtype=F32):
    m, k = a.shape
    s, _, n = b3.shape

    def body(a_ref, b_ref, o_ref):
        o_ref[...] = _dot(a_ref[...], b_ref[0]).astype(out_dtype)

    return _call(
        body, name=name, grid=(s, m // tm),
        in_specs=[pl.BlockSpec((tm, k), lambda j, i: (i, 0)), pl.BlockSpec((1, k, n), lambda j, i: (j, 0, 0))],
        out_specs=pl.BlockSpec((tm, n), lambda j, i: (i, j)), out_shape=_sds((m, s * n), out_dtype))(a, b3)


def _mm_tn(a, dy, s, name, tm, tk):
    m, k = a.shape
    n = dy.shape[1] // s
    steps = m // tm

    def body(a_ref, dy_ref, o_ref, ob_ref):
        p = _dot_tn(a_ref[...], dy_ref[...])[None]
        if steps == 1:
            o_ref[...] = p
            ob_ref[...] = p.astype(BF16)
        else:
            i = pl.program_id(2)
            _acc(o_ref, p, i == 0)

            @pl.when(i == steps - 1)
            def _():
                ob_ref[...] = o_ref[...].astype(BF16)

    out = pl.BlockSpec((1, tk, n), lambda kk, j, i: (j, kk, 0))
    return _call(
        body, name=name, grid=(k // tk, s, steps),
        in_specs=[pl.BlockSpec((tm, tk), lambda kk, j, i: (i, kk)), pl.BlockSpec((tm, n), lambda kk, j, i: (i, j))],
        out_specs=[out, out], out_shape=[_sds((s, k, n)), _sds((s, k, n), BF16)])(a, dy)


def _rms(x):
    return lax.rsqrt(jnp.mean(x * x, axis=-1, keepdims=True) + RMS_EPS)


def _rms_bwd(dxh, xh, r):
    return r * (dxh - xh * jnp.mean(dxh * xh, axis=-1, keepdims=True))


def _norm_mod(x, g, scale, shift, name, tr=512):
    t = x.shape[0]

    def body(x_ref, g_ref, sc_ref, sh_ref, h_ref):
        xv = x_ref[...]
        n = xv * _rms(xv) * g_ref[...]
        h_ref[...] = (n * (1.0 + sc_ref[...]) + sh_ref[...]).astype(BF16)

    return _call(body, name=name, grid=(t // tr,),
                 in_specs=[_rows(tr, D_MODEL), _vec(D_MODEL), _vec(D_MODEL), _vec(D_MODEL)],
                 out_specs=_rows(tr, D_MODEL), out_shape=_sds((t, D_MODEL), BF16))(x, g, scale, shift)


def _out_resid_norm_mod(x, mixin, w_out, gate, g, scale, shift, name, tr=512):
    t = x.shape[0]

    def body(x_ref, mi_ref, w_ref, gt_ref, g_ref, sc_ref, sh_ref, m_ref, x2_ref, h_ref):
        mix = _dot(mi_ref[...], w_ref[0])
        m_ref[...] = mix
        x2 = x_ref[...] + gt_ref[...] * mix
        x2_ref[...] = x2
        n = x2 * _rms(x2) * g_ref[...]
        h_ref[...] = (n * (1.0 + sc_ref[...]) + sh_ref[...]).astype(BF16)

    row = _rows(tr, D_MODEL)
    return _call(body, name=name, grid=(t // tr,),
                 in_specs=[row, row, pl.BlockSpec(w_out.shape, lambda i: (0, 0, 0))] + [_vec(D_MODEL)] * 4,
                 out_specs=[row, row, row],
                 out_shape=[_sds((t, D_MODEL)), _sds((t, D_MODEL)), _sds((t, D_MODEL), BF16)])(
                     x, mixin, w_out, gate, g, scale, shift)


def _mm_gate_up(h, w_gu, name, tm=1024):
    m, k = h.shape
    n = w_gu.shape[2]

    def body(h_ref, wa_ref, wu_ref, da_ref, du_ref, o_ref):
        hv = h_ref[...]
        a = _dot(hv, wa_ref[0])
        u = _dot(hv, wu_ref[0])
        sg = _sigmoid(a)
        silu = a * sg
        da_ref[...] = (u * sg * (1.0 + a * (1.0 - sg))).astype(BF16)
        du_ref[...] = silu.astype(BF16)
        o_ref[...] = (silu * u).astype(BF16)

    out = pl.BlockSpec((tm, n), lambda j, i: (i, j))
    return _call(body, name=name, grid=(2, m // tm),
                 in_specs=[pl.BlockSpec((tm, k), lambda j, i: (i, 0)), pl.BlockSpec((1, k, n), lambda j, i: (j, 0, 0)),
                           pl.BlockSpec((1, k, n), lambda j, i: (j + 2, 0, 0))],
                 out_specs=[out, out, out], out_shape=[_sds((m, 2 * n), BF16)] * 3)(h, w_gu, w_gu)


def _mm_down_dx(dffn, w_down, act_da, act_du, name, tm=512):
    m = dffn.shape[0]
    _, k, n = w_down.shape

    def body(d_ref, w_ref, da_ref, du_ref, o_ref):
        dact = _dot_nt(d_ref[...], w_ref[0])
        o_ref[:, :k] = (dact * da_ref[...].astype(F32)).astype(BF16)
        o_ref[:, k:] = (dact * du_ref[...].astype(F32)).astype(BF16)

    return _call(body, name=name, grid=(m // tm,),
                 in_specs=[_rows(tm, n), pl.BlockSpec((1, k, n), lambda i: (0, 0, 0)), _rows(tm, k), _rows(tm, k)],
                 out_specs=_rows(tm, 2 * k), out_shape=_sds((m, 2 * k), BF16))(dffn, w_down, act_da, act_du)


def _down_loss(x2, act, w_down, gate, fg, tgt, name, tr=512):
    t = x2.shape[0]
    _, k, n = w_down.shape

    def body(x_ref, a_ref, w_ref, gt_ref, fg_ref, t_ref, dx_ref, df_ref, l_ref, dfg_ref, dgt_ref):
        first = pl.program_id(0) == 0
        ffn_v = _dot(a_ref[...], w_ref[0])
        x3 = x_ref[...] + gt_ref[...] * ffn_v
        r = _rms(x3)
        xh = x3 * r
        err = xh * fg_ref[...] - t_ref[...]
        dy = err * (1.0 / D_MODEL)
        dx3 = _rms_bwd(dy * fg_ref[...], xh, r)
        dx_ref[...] = dx3
        df_ref[...] = (dx3 * gt_ref[...]).astype(BF16)
        _acc(l_ref, jnp.sum(err * err, axis=0, keepdims=True), first)
        _acc(dfg_ref, jnp.sum(dy * xh, axis=0, keepdims=True), first)
        _acc(dgt_ref, jnp.sum(dx3 * ffn_v, axis=0, keepdims=True), first)

    row, vec = _rows(tr, D_MODEL), _vec(D_MODEL)
    return _call(body, name=name, grid=(t // tr,),
                 in_specs=[row, _rows(tr, k), pl.BlockSpec((1, k, n), lambda i: (0, 0, 0)), vec, vec, row],
                 out_specs=[row, row, vec, vec, vec],
                 out_shape=[_sds((t, D_MODEL)), _sds((t, D_MODEL), BF16)] + [_sds((1, D_MODEL))] * 3)(
                     x2, act, w_down, gate, fg, tgt)


def _norm_mod_bwd(dh, x, g, scale, dres, name, gate=None, mix=None, w=None, tr=512):
    t = x.shape[0]
    below = gate is not None

    def body(*refs):
        if w is not None:
            w_ref, refs = refs[1], refs[:1] + refs[2:]
        if below:
            dh_ref, x_ref, g_ref, sc_ref, dr_ref, gt_ref, m_ref, dx_ref, dsh_ref, dsc_ref, dg_ref, dgt_ref, dm_ref = refs
        else:
            dh_ref, x_ref, g_ref, sc_ref, dr_ref, dx_ref, dsh_ref, dsc_ref, dg_ref = refs
        first = pl.program_id(0) == 0
        xv = x_ref[...]
        if w is None:
            dhv = dh_ref[...].astype(F32)
        else:
            n = w.shape[2]
            dhv = _dot_nt(dh_ref[:, 0:n], w_ref[0])
            for j in range(1, w.shape[0]):
                dhv = dhv + _dot_nt(dh_ref[:, j * n:(j + 1) * n], w_ref[j])
        r = _rms(xv)
        xh = xv * r
        dn = dhv * (1.0 + sc_ref[...])
        dx = dr_ref[...] + _rms_bwd(dn * g_ref[...], xh, r)
        dx_ref[...] = dx
        _acc(dsh_ref, jnp.sum(dhv, axis=0, keepdims=True), first)
        _acc(dsc_ref, jnp.sum(dhv * xh * g_ref[...], axis=0, keepdims=True), first)
        _acc(dg_ref, jnp.sum(dn * xh, axis=0, keepdims=True), first)
        if below:
            _acc(dgt_ref, jnp.sum(dx * m_ref[...], axis=0, keepdims=True), first)
            dm_ref[...] = (dx * gt_ref[...]).astype(BF16)

    row, vec = _rows(tr, D_MODEL), _vec(D_MODEL)
    first_specs = [row] if w is None else [
        _rows(tr, dh.shape[1]), pl.BlockSpec(w.shape, lambda i: (0, 0, 0), pipeline_mode=pl.Buffered(1))]
    in_specs = first_specs + [row, vec, vec, row] + ([vec, row] if below else [])
    out_specs = [row, vec, vec, vec] + ([vec, row] if below else [])
    out_shape = [_sds((t, D_MODEL))] + [_sds((1, D_MODEL))] * 3 + ([_sds((1, D_MODEL)), _sds((t, D_MODEL), BF16)] if below else [])
    args = ((dh,) if w is None else (dh, w)) + (x, g, scale, dres) + ((gate, mix) if below else ())
    return _call(body, name=name, grid=(t // tr,), in_specs=in_specs, out_specs=out_specs, out_shape=out_shape)(*args)


def _mix_in_bwd(dmix, w_out, o_hg, proj, att, og, ag, name, tr=512):
    t = o_hg.shape[0]

    def body(dy_ref, w_ref, o_ref, g_ref, a_ref, og_ref, ag_ref, do_ref, dg_ref, da_ref, dd_ref, dog_ref, dag_ref):
        first = pl.program_id(0) == 0
        dmi = _dot_nt(dy_ref[...], w_ref[0])
        dog = jnp.zeros((1, HG_HEAD), F32)
        for h in range(HG_WIDTH // HG_HEAD):
            sl = slice(h * HG_HEAD, (h + 1) * HG_HEAD)
            oh = o_ref[:, sl].astype(F32)
            gv = g_ref[:, sl].astype(F32)
            dv = dmi[:, sl]
            r = _rms(oh)
            xh = oh * r
            sg = _sigmoid(gv)
            dno = dv * gv * sg
            dg_ref[:, sl] = (dv * xh * og_ref[...] * sg * (1.0 + gv * (1.0 - sg))).astype(BF16)
            dog = dog + jnp.sum(dno * xh, axis=0, keepdims=True)
            do_ref[:, sl] = _rms_bwd(dno * og_ref[...], xh, r).astype(BF16)
        _acc(dog_ref, dog, first)
        av = a_ref[...]
        dav = dmi[:, HG_WIDTH:]
        r = _rms(av)
        xa = av * r
        _acc(dag_ref, jnp.sum(dav * xa, axis=0, keepdims=True), first)
        datt = _rms_bwd(dav * ag_ref[...], xa, r)
        da_ref[...] = datt.astype(BF16)
        prod = datt * av
        lane = lax.broadcasted_iota(jnp.int32, (1, 128), 1)
        dd = jnp.zeros((tr, 128), F32)
        for hp in range(ATT_HEADS // 2):
            pp = prod[:, hp * 128:(hp + 1) * 128]
            lo = jnp.sum(jnp.where(lane < 64, pp, 0.0), axis=-1, keepdims=True)
            hi = jnp.sum(jnp.where(lane >= 64, pp, 0.0), axis=-1, keepdims=True)
            dd = jnp.where(lane == 2 * hp, lo, dd)
            dd = jnp.where(lane == 2 * hp + 1, hi, dd)
        dd_ref[...] = dd

    half = _rows(tr, HG_WIDTH)
    return _call(body, name=name, grid=(t // tr,),
                 in_specs=[_rows(tr, D_MODEL), pl.BlockSpec(w_out.shape, lambda i: (0, 0, 0)), half,
                           pl.BlockSpec((tr, HG_WIDTH), lambda i: (i, 3)), half, _vec(HG_HEAD), _vec(ATT_WIDTH)],
                 out_specs=[half, half, half, _rows(tr, 128), _vec(HG_HEAD), _vec(ATT_WIDTH)],
                 out_shape=[_sds((t, HG_WIDTH), BF16)] * 3 + [_sds((t, 128)), _sds((1, HG_HEAD)), _sds((1, ATT_WIDTH))])(
                     dmix, w_out, o_hg, proj, att, og, ag)


def _dproj(dhg, dg, dqs, dkvs, name, tr=1024):
    t = dhg.shape[0]
    w3 = 3 * HG_WIDTH
    w4 = w3 + HG_WIDTH
    nbr = len(dqs)

    def body(*refs):
        h_ref, g_ref, q_refs, kv_refs, o_ref = refs[0], refs[1], refs[2:2 + nbr], refs[2 + nbr:2 + 2 * nbr], refs[-1]
        o_ref[:, :w3] = h_ref[...]
        o_ref[:, w3:w4] = g_ref[...].astype(BF16)
        o_ref[:, w4:w4 + ATT_WIDTH] = sum(r[...].astype(F32) for r in q_refs).astype(BF16)
        o_ref[:, w4 + ATT_WIDTH:] = sum(r[...].astype(F32) for r in kv_refs).astype(BF16)

    return _call(body, name=name, grid=(t // tr,),
                 in_specs=[_rows(tr, w3), _rows(tr, HG_WIDTH)] + [_rows(tr, ATT_WIDTH)] * nbr
                 + [_rows(tr, 2 * ATT_WIDTH)] * nbr,
                 out_specs=_rows(tr, IN_WIDTH), out_shape=_sds((t, IN_WIDTH), BF16))(dhg, dg, *dqs, *dkvs)


def _chunk_tri(upper):
    row = lax.broadcasted_iota(jnp.int32, (HG_GROUP, HG_CHUNK, HG_CHUNK), 1)
    col = lax.broadcasted_iota(jnp.int32, (HG_GROUP, HG_CHUNK, HG_CHUNK), 2)
    return (row <= col if upper else row >= col).astype(BF16)


def _chunk_cumsum(x, tri):
    x3 = x.reshape(HG_GROUP, HG_CHUNK, x.shape[1])
    dims = (((2,), (1,)), ((0,), (0,)))
    out = None
    for _ in range(3):
        part = x3.astype(BF16)
        x3 = x3 - part.astype(F32)
        term = lax.dot_general(tri, part, dims, preferred_element_type=F32)
        out = term if out is None else out + term
    return out.reshape(x.shape)


def _hg_gates(f_raw, q_raw, lb, tri):
    sg = _sigmoid(f_raw)
    f = lb + (1.0 - lb) * sg
    k = 1.0 - f
    b = _chunk_cumsum(jnp.log(f), tri)
    sq = _sigmoid(q_raw)
    return sg, f, k, b, sq


def _hg_masks(rows):
    row = lax.broadcasted_iota(jnp.int32, (rows, rows), 0)
    col = lax.broadcasted_iota(jnp.int32, (rows, rows), 1)
    same = (row // HG_CHUNK) == (col // HG_CHUNK)
    return jnp.logical_and(row >= col, same), jnp.logical_and(row <= col, same)


def _per_chunk(rows_of):
    return jnp.concatenate([jnp.broadcast_to(r, (HG_CHUNK, r.shape[1])) for r in rows_of], axis=0)


def _hgrn_fwd(proj, lb_logits, name):
    t = proj.shape[0]
    nc = t // HG_CHUNK
    nh = HG_WIDTH // HG_HEAD
    rows = HG_GROUP * HG_CHUNK

    def body(q_ref, f_ref, i_ref, lg_ref, o_ref, st_ref, s_scr):
        @pl.when(pl.program_id(0) == 0)
        def _():
            s_scr[...] = jnp.zeros_like(s_scr)

        lg = lg_ref[...]
        lb_all = _sigmoid(lg[0:1] - lg[1:2])
        causal, _ = _hg_masks(rows)
        tri = _chunk_tri(False)
        for h in range(nh):
            sl = slice(h * HG_HEAD, (h + 1) * HG_HEAD)
            q_raw = q_ref[:, sl].astype(F32)
            _, _, k, b, sq = _hg_gates(f_ref[:, sl].astype(F32), q_raw, lb_all[:, sl], tri)
            v = i_ref[:, sl].astype(BF16)
            gls = [b[(g + 1) * HG_CHUNK - 1:(g + 1) * HG_CHUNK] for g in range(HG_GROUP)]
            bm = _per_chunk([b[g * HG_CHUNK + HG_CHUNK // 2 - 1:g * HG_CHUNK + HG_CHUNK // 2] for g in range(HG_GROUP)])
            qd = (q_raw * sq * jnp.exp(b)).astype(BF16)
            qm = (q_raw * sq * jnp.exp(b - bm)).astype(BF16)
            km = (k * jnp.exp(bm - b)).astype(BF16)
            ke = (k * jnp.exp(_per_chunk(gls) - b)).astype(BF16)
            a = jnp.where(causal, _dot_nt(qm, km), 0.0).astype(BF16)
            o_intra = _dot(a, v)
            st = s_scr[h]
            o_inter = []
            for g in range(HG_GROUP):
                rs = slice(g * HG_CHUNK, (g + 1) * HG_CHUNK)
                st_ref[g, sl, :] = st
                o_inter.append(_dot_nt(qd[rs], st.astype(BF16)))
                st = st * jnp.exp(gls[g]) + _dot_tn(v[rs], ke[rs])
            s_scr[h] = st
            o_ref[:, sl] = (o_intra + jnp.concatenate(o_inter, axis=0)).astype(BF16)

    blk = lambda j: pl.BlockSpec((rows, HG_WIDTH), lambda c: (c, j))
    return _call(body, name=name, grid=(nc // HG_GROUP,),
                 in_specs=[blk(0), blk(1), blk(2), pl.BlockSpec((2, HG_WIDTH), lambda c: (0, 0))],
                 out_specs=[blk(0), pl.BlockSpec((HG_GROUP, HG_WIDTH, HG_HEAD), lambda c: (c, 0, 0))],
                 out_shape=[_sds((t, HG_WIDTH), BF16), _sds((nc, HG_WIDTH, HG_HEAD))],
                 scratch_shapes=[pltpu.VMEM((nh, HG_HEAD, HG_HEAD), F32)])(proj, proj, proj, lb_logits)


def _hgrn_bwd(proj, lb_logits, states, do, name):
    t = proj.shape[0]
    ng = t // (HG_GROUP * HG_CHUNK)
    nh = HG_WIDTH // HG_HEAD
    rows = HG_GROUP * HG_CHUNK

    def body(q_ref, f_ref, i_ref, lg_ref, st_ref, do_ref, d_ref, dlb_ref, ds_scr):
        first = pl.program_id(0) == 0

        @pl.when(first)
        def _():
            ds_scr[...] = jnp.zeros_like(ds_scr)

        lg = lg_ref[...]
        lb_all = _sigmoid(lg[0:1] - lg[1:2])
        causal, _ = _hg_masks(rows)
        tri = _chunk_tri(False)
        tri_t = _chunk_tri(True)
        dlb = []
        for h in range(nh):
            sl = slice(h * HG_HEAD, (h + 1) * HG_HEAD)
            q_raw = q_ref[:, sl].astype(F32)
            lb = lb_all[:, sl]
            sg, f, k, b, sq = _hg_gates(f_ref[:, sl].astype(F32), q_raw, lb, tri)
            v = i_ref[:, sl].astype(BF16)
            gls = [b[(g + 1) * HG_CHUNK - 1:(g + 1) * HG_CHUNK] for g in range(HG_GROUP)]
            bm = _per_chunk([b[g * HG_CHUNK + HG_CHUNK // 2 - 1:g * HG_CHUNK + HG_CHUNK // 2] for g in range(HG_GROUP)])
            eb = jnp.exp(b)
            ebm = jnp.exp(b - bm)
            emb = jnp.exp(bm - b)
            egb = jnp.exp(_per_chunk(gls) - b)
            ke = k * egb
            qd_b, qm_b = (q_raw * sq * eb).astype(BF16), (q_raw * sq * ebm).astype(BF16)
            km_b, ke_b = (k * emb).astype(BF16), ke.astype(BF16)
            dov = do_ref[:, sl].astype(BF16)
            a = jnp.where(causal, _dot_nt(qm_b, km_b), 0.0).astype(BF16)
            da = jnp.where(causal, _dot_nt(dov, v), 0.0).astype(BF16)
            dkm = _dot_tn(da, qm_b)
            dst = ds_scr[h]
            dqd_s, dv_s, dke_s, dgl_s = [None] * HG_GROUP, [None] * HG_GROUP, [None] * HG_GROUP, [None] * HG_GROUP
            for g in reversed(range(HG_GROUP)):
                rs = slice(g * HG_CHUNK, (g + 1) * HG_CHUNK)
                st = st_ref[g, sl, :]
                dst_b = dst.astype(BF16)
                egl = jnp.exp(gls[g])
                dqd_s[g] = _dot(dov[rs], st.astype(BF16))
                dv_s[g] = _dot_nt(ke_b[rs], dst_b)
                dke_s[g] = _dot(v[rs], dst_b)
                dgl_s[g] = jnp.sum(dst * st, axis=0, keepdims=True) * egl
                dst = _dot_tn(dov[rs], qd_b[rs]) + dst * egl
            ds_scr[h] = dst
            dqm = _dot(da, km_b)
            dqd = jnp.concatenate(dqd_s, axis=0)
            dv = _dot_tn(a, dov) + jnp.concatenate(dv_s, axis=0)
            dke = jnp.concatenate(dke_s, axis=0)
            t1 = dke * ke
            db = dqm * qm_b.astype(F32) - dkm * km_b.astype(F32) + dqd * qd_b.astype(F32) - t1
            dgl = _per_chunk([dgl_s[g] + jnp.sum(t1[g * HG_CHUNK:(g + 1) * HG_CHUNK], axis=0, keepdims=True)
                              for g in range(HG_GROUP)])
            dlf = _chunk_cumsum(db, tri_t) + dgl
            df = dlf / f - (dkm * emb + dke * egb)
            d_ref[:, sl] = ((dqm * ebm + dqd * eb) * sq * (1.0 + q_raw * (1.0 - sq))).astype(BF16)
            d_ref[:, HG_WIDTH + h * HG_HEAD:HG_WIDTH + (h + 1) * HG_HEAD] = (
                df * (1.0 - lb) * sg * (1.0 - sg)).astype(BF16)
            d_ref[:, 2 * HG_WIDTH + h * HG_HEAD:2 * HG_WIDTH + (h + 1) * HG_HEAD] = dv.astype(BF16)
            dlb.append(jnp.sum(df * (1.0 - sg), axis=0, keepdims=True))
        _acc(dlb_ref, jnp.concatenate(dlb, axis=1), first)

    rev = lambda j: pl.BlockSpec((rows, HG_WIDTH), lambda c: (ng - 1 - c, j))
    return _call(body, name=name, grid=(ng,),
                 in_specs=[rev(0), rev(1), rev(2), pl.BlockSpec((2, HG_WIDTH), lambda c: (0, 0)),
                           pl.BlockSpec((HG_GROUP, HG_WIDTH, HG_HEAD), lambda c: (ng - 1 - c, 0, 0)), rev(0)],
                 out_specs=[pl.BlockSpec((rows, 3 * HG_WIDTH), lambda c: (ng - 1 - c, 0)), _vec(HG_WIDTH)],
                 out_shape=[_sds((t, 3 * HG_WIDTH), BF16), _sds((1, HG_WIDTH))],
                 scratch_shapes=[pltpu.VMEM((nh, HG_HEAD, HG_HEAD), F32)])(proj, proj, proj, lb_logits, states, do)


def _to_sub(a, dil):
    t, w = a.shape
    return a if dil == 1 else a.reshape(t // dil, dil, w).transpose(1, 0, 2).reshape(t, w)


def _from_sub(a, dil):
    t, w = a.shape
    return a if dil == 1 else a.reshape(dil, t // dil, w).transpose(1, 0, 2).reshape(t, w)


def _att_mask(has_prev, seg):
    def place(v):
        v = v % ATT_BLOCK
        return v if seg == 1 else seg * (v % (ATT_BLOCK // seg)) + v // (ATT_BLOCK // seg)

    row = lax.broadcasted_iota(jnp.int32, (2 * ATT_BLOCK, 2 * ATT_BLOCK), 0)
    col = lax.broadcasted_iota(jnp.int32, (2 * ATT_BLOCK, 2 * ATT_BLOCK), 1)
    qi, kj = place(row), place(col)
    prev = jnp.logical_and(jnp.logical_and(col < ATT_BLOCK, kj >= qi), has_prev)
    cur = jnp.logical_and(col >= ATT_BLOCK, kj <= qi)
    return jnp.logical_or(prev, cur), lax.broadcasted_iota(jnp.int32, (1, 128), 1)


def _get(ref, sl):
    if len(ref.shape) == 2:
        return ref[:, sl]
    v = ref[:, :, sl]
    return v.reshape(ATT_BLOCK, v.shape[2])


def _put(ref, sl, val):
    if len(ref.shape) == 2:
        ref[:, sl] = val
    else:
        ref[:, :, sl] = val.reshape(ref.shape[0], ref.shape[1], val.shape[1])


def _att_spec(nb, dil, seg, width, col, back):
    bps = nb // dil

    def plain(n):
        return jnp.clip(n - back, 0, nb - 1), col

    def segmented(n):
        m = jnp.clip(n - back, 0, nb - 1)
        return 0, m // bps, m % bps, 0, col

    if seg == 1:
        return pl.BlockSpec((ATT_BLOCK, width), plain)
    return pl.BlockSpec((seg, None, None, ATT_BLOCK // seg, width), segmented)


def _att_shape(nb, dil, seg, width):
    t = nb * ATT_BLOCK
    return (t, width) if seg == 1 else (seg, dil, nb // dil, ATT_BLOCK // seg, width)


def _att_view(a, nb, dil, seg):
    return a.reshape(_att_shape(nb, dil, seg, a.shape[1]))


def _attn_fwd_block(q_ref, kc_ref, kp_ref, vc_ref, vp_ref, o_ref, l_ref, has_prev, seg):
    mask, lane = _att_mask(has_prev, seg)
    lo = lane < 64
    nq = ATT_BLOCK
    lse_all = jnp.zeros((nq, 128), F32)
    for hp in range(ATT_HEADS // 2):
        sl = slice(hp * 128, (hp + 1) * 128)
        q2 = _get(q_ref, sl)
        zero = jnp.zeros_like(q2)
        q2 = q2 * 0.125
        qs = jnp.concatenate([jnp.where(lo, q2, zero), jnp.where(lo, zero, q2)], axis=0)
        kk = jnp.concatenate([_get(kp_ref, sl), _get(kc_ref, sl)], axis=0)
        vv = jnp.concatenate([_get(vp_ref, sl), _get(vc_ref, sl)], axis=0)
        s = jnp.where(mask, _dot_nt(qs, kk), NEG)
        mx = jnp.max(s, axis=-1, keepdims=True)
        p = jnp.exp(s - mx)
        l = jnp.sum(p, axis=-1, keepdims=True)
        o = _dot(p.astype(BF16), vv) * (1.0 / l)
        _put(o_ref, sl, jnp.where(lo, o[:nq], o[nq:]).astype(BF16))
        lse = mx + jnp.log(l)
        lse_all = jnp.where(lane == 2 * hp, lse[:nq], lse_all)
        lse_all = jnp.where(lane == 2 * hp + 1, lse[nq:], lse_all)
    _put(l_ref, slice(None), lse_all)


def _attn_fwd(branches, name):
    t = branches[0][0].shape[0]
    nb = t // ATT_BLOCK
    nbr = len(branches)

    def body(*refs):
        n = pl.program_id(0)
        for i, (_, dil, seg) in enumerate(branches):
            _attn_fwd_block(*refs[5 * i:5 * i + 5], *refs[5 * nbr + 2 * i:5 * nbr + 2 * i + 2],
                            (n % (nb // dil)) != 0, seg)

    in_specs, args, out_specs, out_shape = [], [], [], []
    for qkv, dil, seg in branches:
        c0 = qkv.shape[1] // ATT_WIDTH - 3
        in_specs += [_att_spec(nb, dil, seg, ATT_WIDTH, c0 + j, back) for j, back in [(0, 0), (1, 0), (1, 1), (2, 0), (2, 1)]]
        args += [_att_view(qkv, nb, dil, seg)] * 5
        out_specs += [_att_spec(nb, dil, seg, ATT_WIDTH, 0, 0), _att_spec(nb, dil, seg, 128, 0, 0)]
        out_shape += [_sds(_att_shape(nb, dil, seg, ATT_WIDTH), BF16), _sds(_att_shape(nb, dil, seg, 128))]
    out = _call(body, name=name, grid=(nb,), in_specs=in_specs, out_specs=out_specs, out_shape=out_shape)(*args)
    return [(out[2 * i].reshape(t, ATT_WIDTH), out[2 * i + 1].reshape(t, 128)) for i in range(nbr)]


def _combine_mix_in(os_, ls_, o_hg, proj, og, ag, name, tr=512):
    t = os_[0].shape[0]
    nbr = len(os_)

    def body(*refs):
        o_refs, l_refs = refs[:nbr], refs[nbr:2 * nbr]
        oh_ref, g_ref, og_ref, ag_ref, a_ref, lt_ref, m_ref = refs[2 * nbr:]
        lane = lax.broadcasted_iota(jnp.int32, (1, 128), 1)
        ls = [r[...] for r in l_refs]
        mx = functools.reduce(jnp.maximum, ls)
        tot = mx + jnp.log(sum(jnp.exp(l - mx) for l in ls))
        lt_ref[...] = tot
        ws = [jnp.exp(l - tot) for l in ls]
        pairs = []
        for hp in range(ATT_HEADS // 2):
            sl = slice(hp * 128, (hp + 1) * 128)
            acc = jnp.zeros((tr, 128), F32)
            for w, o_ref in zip(ws, o_refs):
                wf = jnp.where(lane < 64, w[:, 2 * hp:2 * hp + 1], w[:, 2 * hp + 1:2 * hp + 2])
                acc = acc + wf * o_ref[:, sl]
            pairs.append(acc)
        av = jnp.concatenate(pairs, axis=1)
        a_ref[...] = av
        m_ref[:, HG_WIDTH:] = (av * _rms(av) * ag_ref[...]).astype(BF16)
        for h in range(HG_WIDTH // HG_HEAD):
            sl = slice(h * HG_HEAD, (h + 1) * HG_HEAD)
            oh = oh_ref[:, sl].astype(F32)
            gv = g_ref[:, sl].astype(F32)
            m_ref[:, sl] = (oh * _rms(oh) * og_ref[...] * (gv * _sigmoid(gv))).astype(BF16)

    half = _rows(tr, ATT_WIDTH)
    return _call(body, name=name, grid=(t // tr,),
                 in_specs=[half] * nbr + [_rows(tr, 128)] * nbr
                 + [half, pl.BlockSpec((tr, HG_WIDTH), lambda i: (i, 3)), _vec(HG_HEAD), _vec(ATT_WIDTH)],
                 out_specs=[half, _rows(tr, 128), _rows(tr, D_MODEL)],
                 out_shape=[_sds((t, ATT_WIDTH)), _sds((t, 128)), _sds((t, D_MODEL), BF16)])(
                     *os_, *ls_, o_hg, proj, og, ag)


def _attn_bwd_block(q_ref, kc_ref, kp_ref, vc_ref, vp_ref, do_ref, l_ref, d_ref, dq_ref, dkv_ref, carry, has_prev, seg):
    w = ATT_WIDTH
    nq = ATT_BLOCK
    mask, lane = _att_mask(has_prev, seg)
    lo = lane < 64
    lse, ddv = _get(l_ref, slice(None)), _get(d_ref, slice(None))
    for hp in range(ATT_HEADS // 2):
        sl = slice(hp * 128, (hp + 1) * 128)
        sv = slice(w + hp * 128, w + (hp + 1) * 128)
        q2, do2 = _get(q_ref, sl), _get(do_ref, sl)
        zero = jnp.zeros_like(q2)
        q2 = q2 * 0.125
        qs = jnp.concatenate([jnp.where(lo, q2, zero), jnp.where(lo, zero, q2)], axis=0)
        dos = jnp.concatenate([jnp.where(lo, do2, zero), jnp.where(lo, zero, do2)], axis=0)
        kk = jnp.concatenate([_get(kp_ref, sl), _get(kc_ref, sl)], axis=0)
        vv = jnp.concatenate([_get(vp_ref, sl), _get(vc_ref, sl)], axis=0)
        ls = jnp.concatenate([lse[:, 2 * hp:2 * hp + 1], lse[:, 2 * hp + 1:2 * hp + 2]], axis=0)
        dh = jnp.concatenate([ddv[:, 2 * hp:2 * hp + 1], ddv[:, 2 * hp + 1:2 * hp + 2]], axis=0)
        p = jnp.exp(jnp.where(mask, _dot_nt(qs, kk) - ls, NEG))
        ds = (p * (_dot_nt(dos, vv) - dh)).astype(BF16)
        dq = _dot(ds, kk) * 0.125
        _put(dq_ref, sl, jnp.where(lo, dq[:nq], dq[nq:]).astype(BF16))
        dk = _dot_tn(ds, qs)
        dv = _dot_tn(p.astype(BF16), dos)
        _put(dkv_ref, sl, (carry[:, sl] + dk[:nq]).astype(BF16))
        _put(dkv_ref, sv, (carry[:, sv] + dv[:nq]).astype(BF16))
        carry[:, sl] = dk[nq:]
        carry[:, sv] = dv[nq:]


def _attn_bwd(branches, name):
    t = branches[0][0].shape[0]
    nb = t // ATT_BLOCK
    nbr = len(branches)
    w = ATT_WIDTH

    def body(*refs):
        ins, outs, carries = refs[:8 * nbr], refs[8 * nbr:10 * nbr], refs[10 * nbr:]
        n = pl.program_id(0)

        @pl.when(n == 0)
        def _():
            for carry in carries:
                carry[...] = jnp.zeros_like(carry)

        @pl.when(n < nb)
        def _():
            for i, branch in enumerate(branches):
                dil, seg = branch[4:]
                _attn_bwd_block(*ins[8 * i:8 * i + 8], *outs[2 * i:2 * i + 2], carries[i], (n % (nb // dil)) != 0, seg)

        @pl.when(n == nb)
        def _():
            for i in range(nbr):
                _put(outs[2 * i + 1], slice(None), carries[i][...].astype(BF16))

    in_specs, args, out_specs, out_shape = [], [], [], []
    for qkv, dout, lse, dd, dil, seg in branches:
        c0 = qkv.shape[1] // w - 3
        in_specs += [_att_spec(nb, dil, seg, w, c0 + j, back) for j, back in [(0, 0), (1, 0), (1, 1), (2, 0), (2, 1)]]
        in_specs += [_att_spec(nb, dil, seg, w, 0, 0), _att_spec(nb, dil, seg, 128, 0, 0), _att_spec(nb, dil, seg, 128, 0, 0)]
        args += [_att_view(a, nb, dil, seg) for a in [qkv] * 5 + [dout, lse, dd]]
        out_specs += [_att_spec(nb, dil, seg, w, 0, 0), _att_spec(nb, dil, seg, 2 * w, 0, 1)]
        out_shape += [_sds(_att_shape(nb, dil, seg, w), BF16), _sds(_att_shape(nb, dil, seg, 2 * w), BF16)]
    out = _call(body, name=name, grid=(nb + 1,), in_specs=in_specs, out_specs=out_specs, out_shape=out_shape,
                scratch_shapes=[pltpu.VMEM((ATT_BLOCK, 2 * w), F32)] * nbr)(*args)
    return [(out[2 * i].reshape(t, w), out[2 * i + 1].reshape(t, 2 * w)) for i in range(nbr)]


def _local_step(x, tgt, mod, norm1_g, lb_logits, og, ag, norm2_g, fg, get_w, put_g, late=lambda a: a):
    shift1, scale1, gate1, shift2, scale2, gate2 = [mod[:, i * D_MODEL:(i + 1) * D_MODEL] for i in range(6)]
    fg = fg.reshape(1, D_MODEL)

    h1 = _norm_mod(x, norm1_g, scale1, shift1, "norm_mod1")
    w_in = get_w("w_in", h1)
    proj = _mm_nn(h1, w_in, "mm_in", out_dtype=BF16)
    o_hg, states = _hgrn_fwd(proj, lb_logits, "hgrn_fwd")
    fine = DILATIONS[-1]
    layouts = [(d, 1 if d == 1 else fine // d) for d in DILATIONS]
    qkv_fine = _to_sub(proj[:, 4 * HG_WIDTH:], fine)
    qkvs = [proj if d == 1 else qkv_fine for d in DILATIONS]
    natural = lambda a, d: a if d == 1 else _from_sub(a, fine)
    outs = _attn_fwd([(q, d, seg) for q, (d, seg) in zip(qkvs, layouts)], "attn_fwd")
    att, lse, mixin = _combine_mix_in([natural(o, d) for (o, _), d in zip(outs, DILATIONS)],
                                      [natural(l, d) for (_, l), d in zip(outs, DILATIONS)],
                                      o_hg, proj, og, ag, "attn_combine_mix_in")
    w_out = get_w("w_out", mixin)
    mix, x2, h2 = _out_resid_norm_mod(x, mixin, w_out, gate1, norm2_g, scale2, shift2, "mm_out_resid_norm_mod2")
    w_gu = get_w("w_gu", h2)
    a_ff, u_ff, act = _mm_gate_up(h2, w_gu, "mm_gu")
    w_down = get_w("w_down", act)
    dx3, dffn, loss_v, dfg, dgate2 = _down_loss(x2, act, w_down, gate2, fg, tgt, "mm_down_loss")

    dffn = put_g("w_down", *_mm_tn(act, dffn, 1, "mm_down_dw", tm=2048, tk=D_FF // 2), dffn)
    dau = _mm_down_dx(dffn, w_down, a_ff, u_ff, "mm_down_dx")
    dau = put_g("w_gu", *_mm_tn(h2, dau, N_SHARD, "mm_gu_dw", tm=x.shape[0], tk=512), dau)
    dx2, dshift2, dscale2, dg2, dgate1, dmix = _norm_mod_bwd(
        dau, x2, norm2_g, scale2, dx3, "mm_gu_dx_norm_bwd", gate=gate1, mix=mix, w=w_gu)
    dmix = put_g("w_out", *_mm_tn(mixin, dmix, 1, "mm_out_dw", tm=x.shape[0], tk=512), dmix)
    do_hg, dg_raw, datt, dd, dog, dag = _mix_in_bwd(dmix, w_out, o_hg, proj, att, og, ag, "mm_out_dx_mix_in_bwd")
    datt_b = datt
    reordered = [_to_sub(a, fine) for a in (datt_b, lse, dd)]
    datts = _attn_bwd([(q,) + tuple((datt_b, lse, dd) if d == 1 else reordered) + (d, seg)
                       for q, (d, seg) in zip(qkvs, layouts)], "attn_bwd")
    dhg, dlb = _hgrn_bwd(proj, lb_logits, states, do_hg, "hgrn_bwd")
    dhg = late(dhg)
    dproj = _dproj(dhg, dg_raw, [natural(dq, d) for (dq, _), d in zip(datts, DILATIONS)],
                   [natural(dkv, d) for (_, dkv), d in zip(datts, DILATIONS)], "dproj")
    dproj = put_g("w_in", *_mm_tn(h1, dproj, N_SHARD, "mm_in_dw", tm=x.shape[0], tk=512), dproj)
    dx, dshift1, dscale1, dg1 = _norm_mod_bwd(dproj, x, norm1_g, scale1, dx2, "mm_in_dx_norm_bwd", w=w_in)

    stats = jnp.concatenate([loss_v, dfg, dg2, dg1, dlb, dag, dog,
                             dshift1, dscale1, dgate1, dshift2, dscale2, dgate2], axis=1)
    return dx, stats


def _place():
    x, y, c = lax.axis_index("x"), lax.axis_index("y"), lax.axis_index("c")
    return x, y, c


def _chip_peers(x, y, c):
    return [(1 - x, y, c), (x, 1 - y, c), (1 - x, 1 - y, c)]


def _comm_call(body, name, n_in, out_shape, scratch_shapes):
    hbm = pl.BlockSpec(memory_space=pl.ANY)
    return pl.pallas_call(body, name=name, in_specs=[hbm] * n_in, out_specs=[hbm] * len(out_shape),
                          out_shape=out_shape, scratch_shapes=scratch_shapes)


_HBM = pl.BlockSpec(memory_space=pltpu.HBM)
_SEM = pl.BlockSpec(memory_space=pltpu.SEMAPHORE)
_EFFECT = pltpu.SideEffectType.DATAFLOW_SIDE_EFFECTING


def _exchange_copy(bufs, send, recv, j, peer, place, kind):
    x, y, c = place
    target = peer
    if kind == "gather":
        src = dst = bufs[0].at[2 * x + y]
    elif kind == "scatter":
        src, dst = bufs[0].at[2 * peer[0] + peer[1]], bufs[1].at[j]
    else:
        half = bufs[0].shape[1] // 2
        rows = pl.ds(c * half, half)
        if kind == "half":
            src = dst = bufs[0].at[2 * x + y, rows]
        else:
            src = dst = bufs[0].at[2 * peer[0] + peer[1], rows]
            target = (x, y, 1 - c)
    return pltpu.make_async_remote_copy(src_ref=src, dst_ref=dst, send_sem=send.at[j], recv_sem=recv.at[j],
                                        device_id=target, device_id_type=MESH)


def _exchange_start(groups, after, kind, name):
    sizes = [len(g) for g in groups]
    flat = [b for g in groups for b in g]
    ng, nb = len(groups), len(flat)

    def body(*refs):
        bufs, sems = refs[:nb], refs[nb + 1:nb + 1 + 2 * ng]
        x, y, c = _place()
        for j, peer in enumerate(_chip_peers(x, y, c)):
            at = 0
            for i, size in enumerate(sizes):
                _exchange_copy(bufs[at:at + size], sems[2 * i], sems[2 * i + 1], j, peer, (x, y, c), kind).start()
                at += size

    any_space = pl.BlockSpec(memory_space=pl.ANY)
    out = pl.pallas_call(
        body, name=name, in_specs=[_HBM] * nb + [any_space],
        out_specs=[_SEM] * (2 * ng) + [_HBM] * nb + [any_space],
        out_shape=[pltpu.SemaphoreType.DMA((3,))] * (2 * ng) + [pltpu.HBM(b.shape, b.dtype) for b in flat]
        + [_sds(after.shape, after.dtype)],
        input_output_aliases={i: 2 * ng + i for i in range(nb + 1)},
        compiler_params=pltpu.CompilerParams(has_side_effects=_EFFECT),
    )(*[pltpu.with_memory_space_constraint(b, pltpu.HBM) for b in flat], after)
    started, at = [], 2 * ng
    for i, size in enumerate(sizes):
        started.append((out[2 * i], out[2 * i + 1], tuple(out[at:at + size])))
        at += size
    return started, out[-1]


def _exchange_wait(started, after, kind, name):
    send, recv, bufs = started
    nb = len(bufs)

    def body(*refs):
        x, y, c = _place()
        for j, peer in enumerate(_chip_peers(x, y, c)):
            cp = _exchange_copy(refs[:nb], refs[nb], refs[nb + 1], j, peer, (x, y, c), kind)
            cp.wait_send()
            cp.wait_recv()

    return pl.pallas_call(
        body, name=name, in_specs=[_HBM] * nb + [_SEM, _SEM, pl.BlockSpec(memory_space=pl.ANY)],
        out_specs=[_HBM] * nb, out_shape=[pltpu.HBM(b.shape, b.dtype) for b in bufs],
        input_output_aliases={i: i for i in range(nb)},
        compiler_params=pltpu.CompilerParams(has_side_effects=_EFFECT),
    )(*bufs, send, recv, after)


def _sibling_copies(v_refs, l_refs, send, recv):
    x, y, c = _place()
    return [pltpu.make_async_remote_copy(src_ref=v, dst_ref=l, send_sem=send.at[a], recv_sem=recv.at[a],
                                         device_id=(x, y, 1 - c), device_id_type=MESH)
            for a, (v, l) in enumerate(zip(v_refs, l_refs))]


def _sibling_start(vs, after, name):
    vs = list(vs)
    n = len(vs)
    lands = [lax.empty(v.shape, v.dtype) for v in vs]

    def body(*refs):
        for cp in _sibling_copies(refs[:n], refs[n:2 * n], refs[2 * n + 1], refs[2 * n + 2]):
            cp.start()

    any_space = pl.BlockSpec(memory_space=pl.ANY)
    out = pl.pallas_call(
        body, name=name, in_specs=[_HBM] * (2 * n) + [any_space],
        out_specs=[_SEM, _SEM] + [_HBM] * (2 * n) + [any_space],
        out_shape=[pltpu.SemaphoreType.DMA((n,))] * 2 + [pltpu.HBM(b.shape, b.dtype) for b in vs + lands]
        + [_sds(after.shape, after.dtype)],
        input_output_aliases={i: 2 + i for i in range(2 * n + 1)},
        compiler_params=pltpu.CompilerParams(has_side_effects=_EFFECT),
    )(*[pltpu.with_memory_space_constraint(b, pltpu.HBM) for b in vs + lands], after)
    return (out[0], out[1], tuple(out[2:2 + n]), tuple(out[2 + n:2 + 2 * n])), out[-1]


def _sibling_wait(started, after, name):
    send, recv, vs, lands = started
    n = len(vs)

    def body(*refs):
        for cp in _sibling_copies(refs[:n], refs[n:2 * n], refs[2 * n], refs[2 * n + 1]):
            cp.wait_send()
            cp.wait_recv()

    out = pl.pallas_call(
        body, name=name, in_specs=[_HBM] * (2 * n) + [_SEM, _SEM, pl.BlockSpec(memory_space=pl.ANY)],
        out_specs=[_HBM] * (2 * n), out_shape=[pltpu.HBM(b.shape, b.dtype) for b in vs + lands],
        input_output_aliases={i: i for i in range(2 * n)},
        compiler_params=pltpu.CompilerParams(has_side_effects=_EFFECT),
    )(*vs, *lands, send, recv, after)
    return out[:n], out[n:]


def _swap_sibling(vs, name):
    n = len(vs)

    def body(*refs):
        v_refs, o_refs, (send, recv) = refs[:n], refs[n:2 * n], refs[2 * n:]
        x, y, c = _place()
        cps = [pltpu.make_async_remote_copy(
            src_ref=v_refs[a], dst_ref=o_refs[a], send_sem=send.at[a], recv_sem=recv.at[a],
            device_id=(x, y, 1 - c), device_id_type=MESH) for a in range(n)]
        for cp in cps:
            cp.start()
        for cp in cps:
            cp.wait()

    return _comm_call(body, name, n, [_sds(v.shape, v.dtype) for v in vs],
                      [pltpu.SemaphoreType.DMA((n,)), pltpu.SemaphoreType.DMA((n,))])(*vs)


def _everyone(x, y, c):
    return [(1 - x if k & 4 else x, 1 - y if k & 2 else y, 1 - c if k & 1 else c) for k in range(1, 8)]


def _all_gather_copies(land_ref, send, recv, arriving):
    x, y, c = _place()
    me = 4 * x + 2 * y + c
    return [pltpu.make_async_remote_copy(
        src_ref=land_ref.at[me], dst_ref=land_ref.at[4 * p[0] + 2 * p[1] + p[2] if arriving else me],
        send_sem=send.at[k], recv_sem=recv.at[k], device_id=p, device_id_type=MESH)
        for k, p in enumerate(_everyone(x, y, c))]


def _all_gather_start(v, name):
    x, y, c = _place()
    land = lax.dynamic_update_slice(lax.empty((8,) + v.shape, v.dtype), v[None], (4 * x + 2 * y + c, 0, 0))

    def body(land_ref, v_ref, send, recv, land_out, v_out):
        for cp in _all_gather_copies(land_ref, send, recv, False):
            cp.start()

    any_space = pl.BlockSpec(memory_space=pl.ANY)
    out = pl.pallas_call(
        body, name=name, in_specs=[_HBM, any_space], out_specs=[_SEM, _SEM, _HBM, any_space],
        out_shape=[pltpu.SemaphoreType.DMA((7,))] * 2 + [pltpu.HBM(land.shape, land.dtype), _sds(v.shape, v.dtype)],
        input_output_aliases={0: 2, 1: 3}, compiler_params=pltpu.CompilerParams(has_side_effects=_EFFECT),
    )(pltpu.with_memory_space_constraint(land, pltpu.HBM), v)
    return tuple(out[:3]), out[3]


def _all_gather_wait(started, after, name):
    send, recv, land = started

    def body(land_ref, send, recv, after_ref, land_out):
        for cp in _all_gather_copies(land_ref, send, recv, True):
            cp.wait_send()
            cp.wait_recv()

    return pl.pallas_call(
        body, name=name, in_specs=[_HBM, _SEM, _SEM, pl.BlockSpec(memory_space=pl.ANY)], out_specs=_HBM,
        out_shape=pltpu.HBM(land.shape, land.dtype), input_output_aliases={0: 0},
        compiler_params=pltpu.CompilerParams(has_side_effects=_EFFECT),
    )(land, send, recv, after)


def _gather_rows(v, name):
    r, n = v.shape

    def body(v_ref, o_ref, send, recv, loc):
        x, y, c = _place()
        me = 4 * x + 2 * y + c
        own = pltpu.make_async_copy(v_ref, o_ref.at[me], loc)
        own.start()
        peers = []
        for k in range(1, 8):
            px = 1 - x if k & 4 else x
            py = 1 - y if k & 2 else y
            pc = 1 - c if k & 1 else c
            peers.append((px, py, pc))
        sends = []
        for k, peer in enumerate(peers):
            cp = pltpu.make_async_remote_copy(src_ref=v_ref, dst_ref=o_ref.at[me], send_sem=send.at[k],
                                              recv_sem=recv.at[k], device_id=peer, device_id_type=MESH)
            cp.start()
            sends.append(cp)
        for k, peer in enumerate(peers):
            pltpu.make_async_remote_copy(src_ref=v_ref, dst_ref=o_ref.at[4 * peer[0] + 2 * peer[1] + peer[2]],
                                         send_sem=send.at[k], recv_sem=recv.at[k], device_id=peer,
                                         device_id_type=MESH).wait_recv()
        for cp in sends:
            cp.wait_send()
        own.wait()

    vmem = pl.BlockSpec(memory_space=pltpu.VMEM)
    return pl.pallas_call(body, name=name, in_specs=[vmem], out_specs=vmem, out_shape=_sds((8, r, n), v.dtype),
                          scratch_shapes=[pltpu.SemaphoreType.DMA((7,)), pltpu.SemaphoreType.DMA((7,)),
                                          pltpu.SemaphoreType.DMA])(v)


def _cast_place(ws, shard, name):
    n = len(ws)

    def body(s_ref, *refs):
        for w_ref, o_ref in zip(refs[:n], refs[n:]):
            o_ref[0] = w_ref[...].astype(BF16)

    return pl.pallas_call(
        body, name=name, out_shape=[_sds((N_SHARD,) + w.shape, BF16) for w in ws],
        grid_spec=pltpu.PrefetchScalarGridSpec(
            num_scalar_prefetch=1, grid=(4,),
            in_specs=[pl.BlockSpec((w.shape[0] // 4, w.shape[1]), lambda i, s: (i, 0)) for w in ws],
            out_specs=[pl.BlockSpec((1, w.shape[0] // 4, w.shape[1]), lambda i, s: (s[0], i, 0)) for w in ws]),
        compiler_params=pltpu.CompilerParams(dimension_semantics=("arbitrary",), vmem_limit_bytes=VMEM_LIMIT),
    )(shard.reshape(1).astype(jnp.int32), *ws)


def _mod_part(c_all, w_ada, b_ada, name):
    n = w_ada.shape[1]

    def body(c_ref, w_ref, b_ref, a_ref, p_ref):
        cv = c_ref[...]
        ca = cv * _sigmoid(cv)
        a_ref[...] = ca
        p_ref[...] = jnp.dot(ca, w_ref[...], precision=lax.Precision.HIGHEST, preferred_element_type=F32) + b_ref[...]

    full = lambda a: pl.BlockSpec(a.shape, lambda i: (0, 0))
    return _call(body, name=name, grid=(1,), in_specs=[full(c_all), full(w_ada), full(b_ada)],
                 out_specs=[pl.BlockSpec((8, D_MODEL), lambda i: (0, 0)), pl.BlockSpec((8, n), lambda i: (0, 0))],
                 out_shape=[_sds((8, D_MODEL)), _sds((8, n))])(c_all, w_ada, b_ada)


def _sum_received(gs, shard, lands, name):
    n = len(gs)

    def body(s_ref, *refs):
        for g_ref, l_ref, o_ref in zip(refs[:n], refs[n:2 * n], refs[2 * n:]):
            o_ref[...] = ((g_ref[0] + l_ref[0].astype(F32)) + l_ref[1].astype(F32)) + l_ref[2].astype(F32)

    quarter = lambda g: (g.shape[1] // 4, g.shape[2])
    return pl.pallas_call(
        body, name=name, out_shape=[_sds(g.shape[1:]) for g in gs],
        grid_spec=pltpu.PrefetchScalarGridSpec(
            num_scalar_prefetch=1, grid=(4,),
            in_specs=[pl.BlockSpec((1,) + quarter(g), lambda i, s: (s[0], i, 0)) for g in gs]
            + [pl.BlockSpec((3,) + quarter(g), lambda i, s: (0, i, 0)) for g in gs],
            out_specs=[pl.BlockSpec(quarter(g), lambda i, s: (i, 0)) for g in gs]),
        compiler_params=pltpu.CompilerParams(dimension_semantics=("arbitrary",), vmem_limit_bytes=VMEM_LIMIT),
    )(shard.reshape(1).astype(jnp.int32), *gs, *lands)


def _adamw_outer(w, ct, dm, m, v, name):
    k, n = w.shape
    tr = k // 4

    def body(w_ref, c_ref, d_ref, m_ref, v_ref, g_out, d_out, m_out, v_out):
        cv = c_ref[...]
        dv = d_ref[...]
        g = cv[:, 0:1] * dv[0:1, :]
        for i in range(1, 8):
            g = g + cv[:, i:i + 1] * dv[i:i + 1, :]
        g_out[...] = g
        d_out[...], m_out[...], v_out[...] = _adamw_math(w_ref[...], g, m_ref[...], v_ref[...])

    row = _rows(tr, n)
    return _call(body, name=name, grid=(4,),
                 in_specs=[row, _rows(tr, 8), pl.BlockSpec((8, n), lambda i: (0, 0)), row, row],
                 out_specs=[row] * 4, out_shape=[_sds((k, n))] * 4)(w, ct, dm, m, v)


def _adamw_math(w, g, m, v):
    m_new = ADAM_B1 * m + (1.0 - ADAM_B1) * g
    v_new = ADAM_B2 * v + (1.0 - ADAM_B2) * (g * g)
    m_hat = m_new / (1.0 - ADAM_B1 ** ADAM_STEP)
    v_hat = v_new / (1.0 - ADAM_B2 ** ADAM_STEP)
    return -ADAM_LR * (m_hat / (jnp.sqrt(v_hat) + ADAM_EPS) + ADAM_WD * w), m_new, v_new


def _small_update(stats, smalls, name):
    offsets = [ST_DMOD, ST_DG1, ST_DLB, ST_DOG, ST_DAG, ST_DG2, ST_DFG]
    lb_index = 2

    def body(*refs):
        s_ref, ins, l_ref, outs = refs[0], refs[1:22], refs[22], refs[23:]
        tot = s_ref[0:1, :]
        for i in range(1, 8):
            tot = tot + s_ref[i:i + 1, :]
        l_ref[...] = jnp.zeros((1, 128), F32) + (0.5 / D_MODEL) * jnp.sum(tot[:, ST_LOSS:ST_LOSS + D_MODEL])
        for p, off in enumerate(offsets):
            w_ref, m_ref, v_ref = ins[3 * p:3 * p + 3]
            g_out, d_out, m_out, v_out = outs[4 * p:4 * p + 4]
            g = tot[:, off:off + w_ref.shape[1]]
            if p == lb_index:
                lg = w_ref[...]
                lb = _sigmoid(lg[0:1] - lg[1:2])
                g = g * lb * (1.0 - lb)
            for r in range(w_ref.shape[0]):
                rows = slice(r, r + 1)
                gr = g if r == 0 else -g
                delta, m_new, v_new = _adamw_math(w_ref[rows, :], gr, m_ref[rows, :], v_ref[rows, :])
                g_out[rows, :] = gr
                d_out[rows, :] = delta
                m_out[rows, :] = m_new
                v_out[rows, :] = v_new

    full = lambda a: pl.BlockSpec(a.shape, lambda i: (0, 0))
    flat = [a for t in smalls for a in t]
    return _call(body, name=name, grid=(1,),
                 in_specs=[full(stats)] + [full(a) for a in flat],
                 out_specs=[pl.BlockSpec((1, 128), lambda i: (0, 0))] + [full(t[0]) for t in smalls for _ in range(4)],
                 out_shape=[_sds((1, 128))] + [_sds(t[0].shape) for t in smalls for _ in range(4)])(stats, *flat)


def _adamw(params, name):
    n = len(params)

    def body(*refs):
        for p in range(n):
            w_ref, ga_ref, gb_ref, m_ref, v_ref = refs[5 * p:5 * p + 5]
            g_out, d_out, m_out, v_out = refs[5 * n + 4 * p:5 * n + 4 * p + 4]
            g = ga_ref[...] + gb_ref[...]
            g_out[...] = g
            d_out[...], m_out[...], v_out[...] = _adamw_math(w_ref[...], g, m_ref[...], v_ref[...])

    row = lambda w: _rows(w.shape[0] // 4, w.shape[1])
    out = _call(body, name=name, grid=(4,), in_specs=[row(p[0]) for p in params for _ in range(5)],
                out_specs=[row(p[0]) for p in params for _ in range(4)],
                out_shape=[_sds(p[0].shape) for p in params for _ in range(4)])(*[a for p in params for a in p])
    return [tuple(out[4 * p:4 * p + 4]) for p in range(n)]


def kernel(x, c, w_ada, b_ada, norm1_g, w_in, hg_lb_logits, hg_onorm_g, att_onorm_g, w_out, norm2_g, w_gate_up, w_down, final_g, loss_target, m_w_ada, m_b_ada, m_norm1_g, m_w_in, m_hg_lb_logits, m_hg_onorm_g, m_att_onorm_g, m_w_out, m_norm2_g, m_w_gate_up, m_w_down, m_final_g, v_w_ada, v_b_ada, v_norm1_g, v_w_in, v_hg_lb_logits, v_hg_onorm_g, v_att_onorm_g, v_w_out, v_norm2_g, v_w_gate_up, v_w_down, v_final_g):
    ix, iy, ic = _place()
    shard = 2 * ix + iy
    sample = 4 * ix + 2 * iy + ic
    n_ada = w_ada.shape[2]

    shards = [w_in[0], w_out[0], w_gate_up[0], w_down[0]]
    names = ["w_in", "w_out", "w_gu", "w_down"]
    shapes = [(N_SHARD,) + w.shape for w in shards]
    placed = [(_cast_place(shards[:1], shard, "place_w_in")[0],)]
    placed += [(p,) for p in _cast_place(shards[1:], shard, "place_rest")]

    c_all = _gather_rows(c, "gather_c").reshape(8, D_MODEL)
    b_part = lax.dynamic_slice(b_ada, (0, shard * n_ada), (1, n_ada))
    c_act, part = _mod_part(c_all, w_ada[0], b_part, "mod_part")
    parts = _gather_rows(part, "gather_mod")[::2]
    mod = lax.dynamic_index_in_dim(parts, sample, axis=1, keepdims=False).reshape(1, 6 * D_MODEL)
    (first,), mod = _exchange_start(placed[:1], mod, "half", "gather_start_w_in")
    gathering = {}

    def get_w(name, after):
        if name == "w_in":
            halves = _exchange_wait(first, after, "half", "gather_wait_w_in")
            (passing,), token = _exchange_start([tuple(halves)], mod, "forward", "forward_start_w_in")
            (full,) = _exchange_wait(passing, token, "forward", "forward_wait_w_in")
            rest, full = _exchange_start(placed[1:], full, "gather", "gather_start_rest")
            gathering.update(zip(names[1:], rest))
            return full
        (full,) = _exchange_wait(gathering[name], after, "gather", "gather_wait_" + name)
        return full if name == "w_gu" else full.reshape(1, -1, D_MODEL)

    scattering = {}

    def put_g(name, g, g_bf16, then):
        shape = shapes[names.index(name)]
        land = lax.empty((3,) + shape[1:], BF16)
        (started,), then = _exchange_start([(g_bf16.reshape(shape), land)], then, "scatter", "scatter_start_" + name)
        scattering[name] = (g.reshape(shape), started)
        return then

    def summed(group, after, tag):
        lands = [_exchange_wait(scattering[nm][1], after, "scatter", "scatter_wait_" + nm)[1] for nm in group]
        return _sum_received([scattering[nm][0] for nm in group], shard, lands, "sum_" + tag)

    early = ["w_down", "w_gu", "w_out"]
    swapping = []

    def late(a):
        started, a = _sibling_start(summed(early, a, "early"), a, "swap_start")
        swapping.append(started)
        return a

    dx, stats = _local_step(x[0], loss_target[0], mod, norm1_g, hg_lb_logits, hg_onorm_g, att_onorm_g,
                            norm2_g, final_g, get_w, put_g, late)

    gathering_stats, stats = _all_gather_start(stats, "stats_start")
    moments = [(m_w_in, v_w_in), (m_w_out, v_w_out), (m_w_gate_up, v_w_gate_up), (m_w_down, v_w_down)]

    def update(group, sums, other, tag):
        params = [(shards[names.index(nm)], s, o, moments[names.index(nm)][0][0], moments[names.index(nm)][1][0])
                  for nm, s, o in zip(group, sums, other)]
        return dict(zip(group, _adamw(params, "adamw_" + tag)))

    sums, other = _sibling_wait(swapping[0], stats, "swap_wait")
    done = update(early, sums, other, "early")
    sum_in = summed(["w_in"], done["w_out"][1], "w_in")
    done.update(update(["w_in"], sum_in, _swap_sibling(sum_in, "swap_sum_in"), "w_in"))

    stats_all = _all_gather_wait(gathering_stats, done["w_in"][1], "stats_wait").reshape(8, ST_WIDTH)
    dmod = lax.dynamic_slice(stats_all, (0, ST_DMOD + shard * n_ada), (8, n_ada))

    as_row = lambda a: a.reshape(1, -1) if a.ndim == 1 else a
    smalls = [tuple(as_row(a) for a in t) for t in [
        (b_ada, m_b_ada, v_b_ada), (norm1_g, m_norm1_g, v_norm1_g),
        (hg_lb_logits, m_hg_lb_logits, v_hg_lb_logits), (hg_onorm_g, m_hg_onorm_g, v_hg_onorm_g),
        (att_onorm_g, m_att_onorm_g, v_att_onorm_g), (norm2_g, m_norm2_g, v_norm2_g),
        (final_g, m_final_g, v_final_g)]]
    loss, *small_out = _small_update(stats_all, smalls, "small_update")
    shapes_out = [b_ada.shape, norm1_g.shape, hg_lb_logits.shape, hg_onorm_g.shape, att_onorm_g.shape,
                  norm2_g.shape, final_g.shape]
    sg, sd, sm, sv = [[small_out[4 * p + i].reshape(shapes_out[p]) for p in range(7)] for i in range(4)]

    ada = _adamw_outer(w_ada[0], c_act.T, dmod, m_w_ada[0], v_w_ada[0], "adamw_w_ada")
    big = [ada] + [done[nm] for nm in names]
    bg, bd, bm, bv = [[t[i][None] for t in big] for i in range(4)]

    def order(b, s):
        return [b[0], s[0], s[1], b[1], s[2], s[3], s[4], b[2], s[5], b[3], b[4], s[6]]

    return (loss[0, 0], dx[None], *order(bg, sg), *order(bd, sd), *order(bm, sm), *order(bv, sv))
```

```python
import functools

import jax
import jax.numpy as jnp
from jax import lax
from jax.experimental import pallas as pl
from jax.experimental.pallas import tpu as pltpu

F32 = jnp.float32
BF16 = jnp.bfloat16
MESH = pl.DeviceIdType.MESH

D_MODEL = 1024
HG_WIDTH = 512
HG_HEAD = 128
HG_CHUNK = 64
HG_GROUP = 4
ATT_WIDTH = 512
ATT_HEADS = 8
ATT_BLOCK = 128
DILATIONS = (1, 4, 16)
D_FF = 2816
IN_WIDTH = 3584
N_SHARD = 4
RMS_EPS = 1e-6
NEG = -1e30

ADAM_LR = 0.001
ADAM_B1 = 0.9
ADAM_B2 = 0.999
ADAM_EPS = 1e-08
ADAM_WD = 0.01
ADAM_STEP = 10

VMEM_LIMIT = 56 * 2**20

ST_LOSS, ST_DFG, ST_DG2, ST_DG1 = 0, 1024, 2048, 3072
ST_DLB, ST_DAG, ST_DOG, ST_DMOD = 4096, 4608, 5120, 5248
ST_WIDTH = 5248 + 6144


def _call(body, *, name, grid, in_specs, out_specs, out_shape, scratch_shapes=()):
    return pl.pallas_call(
        body, name=name, grid=grid, in_specs=in_specs, out_specs=out_specs, out_shape=out_shape,
        scratch_shapes=list(scratch_shapes),
        compiler_params=pltpu.CompilerParams(
            dimension_semantics=("arbitrary",) * len(grid), vmem_limit_bytes=VMEM_LIMIT))


def _sds(shape, dtype=F32):
    return jax.ShapeDtypeStruct(shape, dtype)


def _dot(a, b):
    return jnp.dot(a, b, preferred_element_type=F32)


def _dot_nt(a, b):
    return lax.dot_general(a, b, (((1,), (1,)), ((), ())), preferred_element_type=F32)


def _dot_tn(a, b):
    return lax.dot_general(a, b, (((0,), (0,)), ((), ())), preferred_element_type=F32)


def _sigmoid(x):
    return 1.0 / (1.0 + jnp.exp(-x))


def _rows(tr, width):
    return pl.BlockSpec((tr, width), lambda i: (i, 0))


def _vec(width):
    return pl.BlockSpec((1, width), lambda i: (0, 0))


def _acc(ref, val, first):
    @pl.when(first)
    def _():
        ref[...] = val

    @pl.when(jnp.logical_not(first))
    def _():
        ref[...] += val


def _mm_nn(a, b3, name, tm=1024, out_dtype=F32):
    m, k = a.shape
    s, _, n = b3.shape

    def body(a_ref, b_ref, o_ref):
        o_ref[...] = _dot(a_ref[...], b_ref[0]).astype(out_dtype)

    return _call(
        body, name=name, grid=(s, m // tm),
        in_specs=[pl.BlockSpec((tm, k), lambda j, i: (i, 0)), pl.BlockSpec((1, k, n), lambda j, i: (j, 0, 0))],
        out_specs=pl.BlockSpec((tm, n), lambda j, i: (i, j)), out_shape=_sds((m, s * n), out_dtype))(a, b3)


def _mm_own_shard(a, w, shard, s, name, tm=1024):
    m, k = a.shape
    n = w.shape[1]

    def body(s_ref, a_ref, w_ref, o_ref):
        o_ref[...] = _dot(a_ref[...], w_ref[...].astype(BF16)).astype(BF16)

    return pl.pallas_call(
        body, name=name, out_shape=_sds((m, s * n), BF16),
        grid_spec=pltpu.PrefetchScalarGridSpec(
            num_scalar_prefetch=1, grid=(m // tm,),
            in_specs=[pl.BlockSpec((tm, k), lambda i, sh: (i, 0)), pl.BlockSpec((k, n), lambda i, sh: (0, 0))],
            out_specs=pl.BlockSpec((tm, n), lambda i, sh: (i, sh[0]))),
        compiler_params=pltpu.CompilerParams(dimension_semantics=("arbitrary",), vmem_limit_bytes=VMEM_LIMIT),
    )(shard.reshape(1).astype(jnp.int32), a, w)


def _mm_other_shards(a, b3, partial, shard, name, tm=1024):
    m, k = a.shape
    s, _, n = b3.shape
    which = lambda j, sh: (sh[0] + 1 + j) % s

    def body(s_ref, a_ref, b_ref, p_ref, o_ref):
        o_ref[...] = _dot(a_ref[...], b_ref[0]).astype(BF16)

    return pl.pallas_call(
        body, name=name, out_shape=_sds(partial.shape, BF16),
        grid_spec=pltpu.PrefetchScalarGridSpec(
            num_scalar_prefetch=1, grid=(s - 1, m // tm),
            in_specs=[pl.BlockSpec((tm, k), lambda j, i, sh: (i, 0)),
                      pl.BlockSpec((1, k, n), lambda j, i, sh: (which(j, sh), 0, 0)),
                      pl.BlockSpec(memory_space=pl.ANY)],
            out_specs=pl.BlockSpec((tm, n), lambda j, i, sh: (i, which(j, sh)))),
        input_output_aliases={3: 0},
        compiler_params=pltpu.CompilerParams(dimension_semantics=("arbitrary",) * 2, vmem_limit_bytes=VMEM_LIMIT),
    )(shard.reshape(1).astype(jnp.int32), a, b3, partial)


def _mm_tn(a, dy, s, name, tm, tk):
    m, k = a.shape
    n = dy.shape[1] // s
    steps = m // tm

    def body(a_ref, dy_ref, o_ref, ob_ref):
        p = _dot_tn(a_ref[...], dy_ref[...])[None]
        if steps == 1:
            o_ref[...] = p
            ob_ref[...] = p.astype(BF16)
        else:
            i = pl.program_id(2)
            _acc(o_ref, p, i == 0)

            @pl.when(i == steps - 1)
            def _():
                ob_ref[...] = o_ref[...].astype(BF16)

    out = pl.BlockSpec((1, tk, n), lambda kk, j, i: (j, kk, 0))
    return _call(
        body, name=name, grid=(k // tk, s, steps),
        in_specs=[pl.BlockSpec((tm, tk), lambda kk, j, i: (i, kk)), pl.BlockSpec((tm, n), lambda kk, j, i: (i, j))],
        out_specs=[out, out], out_shape=[_sds((s, k, n)), _sds((s, k, n), BF16)])(a, dy)


def _rms(x):
    return lax.rsqrt(jnp.mean(x * x, axis=-1, keepdims=True) + RMS_EPS)


def _rms_bwd(dxh, xh, r):
    return r * (dxh - xh * jnp.mean(dxh * xh, axis=-1, keepdims=True))


def _norm_mod(x, g, scale, shift, name, tr=512):
    t = x.shape[0]

    def body(x_ref, g_ref, sc_ref, sh_ref, h_ref):
        xv = x_ref[...]
        n = xv * _rms(xv) * g_ref[...]
        h_ref[...] = (n * (1.0 + sc_ref[...]) + sh_ref[...]).astype(BF16)

    return _call(body, name=name, grid=(t // tr,),
                 in_specs=[_rows(tr, D_MODEL), _vec(D_MODEL), _vec(D_MODEL), _vec(D_MODEL)],
                 out_specs=_rows(tr, D_MODEL), out_shape=_sds((t, D_MODEL), BF16))(x, g, scale, shift)


def _out_resid_norm_mod(x, mixin, w_out, gate, g, scale, shift, name, tr=512):
    t = x.shape[0]

    def body(x_ref, mi_ref, w_ref, gt_ref, g_ref, sc_ref, sh_ref, m_ref, x2_ref, h_ref):
        mix = _dot(mi_ref[...], w_ref[0])
        m_ref[...] = mix
        x2 = x_ref[...] + gt_ref[...] * mix
        x2_ref[...] = x2
        n = x2 * _rms(x2) * g_ref[...]
        h_ref[...] = (n * (1.0 + sc_ref[...]) + sh_ref[...]).astype(BF16)

    row = _rows(tr, D_MODEL)
    return _call(body, name=name, grid=(t // tr,),
                 in_specs=[row, row, pl.BlockSpec(w_out.shape, lambda i: (0, 0, 0))] + [_vec(D_MODEL)] * 4,
                 out_specs=[row, row, row],
                 out_shape=[_sds((t, D_MODEL)), _sds((t, D_MODEL)), _sds((t, D_MODEL), BF16)])(
                     x, mixin, w_out, gate, g, scale, shift)


def _mm_gate_up(h, w_gu, name, tm=1024):
    m, k = h.shape
    n = w_gu.shape[2]

    def body(h_ref, wa_ref, wu_ref, da_ref, du_ref, o_ref):
        hv = h_ref[...]
        a = _dot(hv, wa_ref[0])
        u = _dot(hv, wu_ref[0])
        sg = _sigmoid(a)
        silu = a * sg
        da_ref[...] = (u * sg * (1.0 + a * (1.0 - sg))).astype(BF16)
        du_ref[...] = silu.astype(BF16)
        o_ref[...] = (silu * u).astype(BF16)

    out = pl.BlockSpec((tm, n), lambda j, i: (i, j))
    return _call(body, name=name, grid=(2, m // tm),
                 in_specs=[pl.BlockSpec((tm, k), lambda j, i: (i, 0)), pl.BlockSpec((1, k, n), lambda j, i: (j, 0, 0)),
                           pl.BlockSpec((1, k, n), lambda j, i: (j + 2, 0, 0))],
                 out_specs=[out, out, out], out_shape=[_sds((m, 2 * n), BF16)] * 3)(h, w_gu, w_gu)


def _mm_down_dx(dffn, w_down, act_da, act_du, name, tm=512):
    m = dffn.shape[0]
    _, k, n = w_down.shape

    def body(d_ref, w_ref, da_ref, du_ref, o_ref):
        dact = _dot_nt(d_ref[...], w_ref[0])
        o_ref[:, :k] = (dact * da_ref[...].astype(F32)).astype(BF16)
        o_ref[:, k:] = (dact * du_ref[...].astype(F32)).astype(BF16)

    return _call(body, name=name, grid=(m // tm,),
                 in_specs=[_rows(tm, n), pl.BlockSpec((1, k, n), lambda i: (0, 0, 0)), _rows(tm, k), _rows(tm, k)],
                 out_specs=_rows(tm, 2 * k), out_shape=_sds((m, 2 * k), BF16))(dffn, w_down, act_da, act_du)


def _down_loss(x2, act, w_down, gate, fg, tgt, name, tr=512):
    t = x2.shape[0]
    _, k, n = w_down.shape

    def body(x_ref, a_ref, w_ref, gt_ref, fg_ref, t_ref, dx_ref, df_ref, l_ref, dfg_ref, dgt_ref):
        first = pl.program_id(0) == 0
        ffn_v = _dot(a_ref[...], w_ref[0])
        x3 = x_ref[...] + gt_ref[...] * ffn_v
        r = _rms(x3)
        xh = x3 * r
        err = xh * fg_ref[...] - t_ref[...]
        dy = err * (1.0 / D_MODEL)
        dx3 = _rms_bwd(dy * fg_ref[...], xh, r)
        dx_ref[...] = dx3
        df_ref[...] = (dx3 * gt_ref[...]).astype(BF16)
        _acc(l_ref, jnp.sum(err * err, axis=0, keepdims=True), first)
        _acc(dfg_ref, jnp.sum(dy * xh, axis=0, keepdims=True), first)
        _acc(dgt_ref, jnp.sum(dx3 * ffn_v, axis=0, keepdims=True), first)

    row, vec = _rows(tr, D_MODEL), _vec(D_MODEL)
    return _call(body, name=name, grid=(t // tr,),
                 in_specs=[row, _rows(tr, k), pl.BlockSpec((1, k, n), lambda i: (0, 0, 0)), vec, vec, row],
                 out_specs=[row, row, vec, vec, vec],
                 out_shape=[_sds((t, D_MODEL)), _sds((t, D_MODEL), BF16)] + [_sds((1, D_MODEL))] * 3)(
                     x2, act, w_down, gate, fg, tgt)


def _norm_mod_bwd(dh, x, g, scale, dres, name, gate=None, mix=None, w=None, tr=512):
    t = x.shape[0]
    below = gate is not None

    def body(*refs):
        if w is not None:
            w_ref, refs = refs[1], refs[:1] + refs[2:]
        if below:
            dh_ref, x_ref, g_ref, sc_ref, dr_ref, gt_ref, m_ref, dx_ref, dsh_ref, dsc_ref, dg_ref, dgt_ref, dm_ref = refs
        else:
            dh_ref, x_ref, g_ref, sc_ref, dr_ref, dx_ref, dsh_ref, dsc_ref, dg_ref = refs
        first = pl.program_id(0) == 0
        xv = x_ref[...]
        if w is None:
            dhv = dh_ref[...].astype(F32)
        else:
            n = w.shape[2]
            dhv = _dot_nt(dh_ref[:, 0:n], w_ref[0])
            for j in range(1, w.shape[0]):
                dhv = dhv + _dot_nt(dh_ref[:, j * n:(j + 1) * n], w_ref[j])
        r = _rms(xv)
        xh = xv * r
        dn = dhv * (1.0 + sc_ref[...])
        dx = dr_ref[...] + _rms_bwd(dn * g_ref[...], xh, r)
        dx_ref[...] = dx
        _acc(dsh_ref, jnp.sum(dhv, axis=0, keepdims=True), first)
        _acc(dsc_ref, jnp.sum(dhv * xh * g_ref[...], axis=0, keepdims=True), first)
        _acc(dg_ref, jnp.sum(dn * xh, axis=0, keepdims=True), first)
        if below:
            _acc(dgt_ref, jnp.sum(dx * m_ref[...], axis=0, keepdims=True), first)
            dm_ref[...] = (dx * gt_ref[...]).astype(BF16)

    row, vec = _rows(tr, D_MODEL), _vec(D_MODEL)
    first_specs = [row] if w is None else [
        _rows(tr, dh.shape[1]), pl.BlockSpec(w.shape, lambda i: (0, 0, 0), pipeline_mode=pl.Buffered(1))]
    in_specs = first_specs + [row, vec, vec, row] + ([vec, row] if below else [])
    out_specs = [row, vec, vec, vec] + ([vec, row] if below else [])
    out_shape = [_sds((t, D_MODEL))] + [_sds((1, D_MODEL))] * 3 + ([_sds((1, D_MODEL)), _sds((t, D_MODEL), BF16)] if below else [])
    args = ((dh,) if w is None else (dh, w)) + (x, g, scale, dres) + ((gate, mix) if below else ())
    return _call(body, name=name, grid=(t // tr,), in_specs=in_specs, out_specs=out_specs, out_shape=out_shape)(*args)


def _mix_in_bwd(dmix, w_out, o_hg, proj, att, og, ag, name, tr=512):
    t = o_hg.shape[0]

    def body(dy_ref, w_ref, o_ref, g_ref, a_ref, og_ref, ag_ref, do_ref, dg_ref, da_ref, dd_ref, dog_ref, dag_ref):
        first = pl.program_id(0) == 0
        dmi = _dot_nt(dy_ref[...], w_ref[0])
        dog = jnp.zeros((1, HG_HEAD), F32)
        for h in range(HG_WIDTH // HG_HEAD):
            sl = slice(h * HG_HEAD, (h + 1) * HG_HEAD)
            oh = o_ref[:, sl].astype(F32)
            gv = g_ref[:, sl].astype(F32)
            dv = dmi[:, sl]
            r = _rms(oh)
            xh = oh * r
            sg = _sigmoid(gv)
            dno = dv * gv * sg
            dg_ref[:, sl] = (dv * xh * og_ref[...] * sg * (1.0 + gv * (1.0 - sg))).astype(BF16)
            dog = dog + jnp.sum(dno * xh, axis=0, keepdims=True)
            do_ref[:, sl] = _rms_bwd(dno * og_ref[...], xh, r).astype(BF16)
        _acc(dog_ref, dog, first)
        av = a_ref[...]
        dav = dmi[:, HG_WIDTH:]
        r = _rms(av)
        xa = av * r
        _acc(dag_ref, jnp.sum(dav * xa, axis=0, keepdims=True), first)
        datt = _rms_bwd(dav * ag_ref[...], xa, r)
        da_ref[...] = datt.astype(BF16)
        prod = datt * av
        lane = lax.broadcasted_iota(jnp.int32, (1, 128), 1)
        dd = jnp.zeros((tr, 128), F32)
        for hp in range(ATT_HEADS // 2):
            pp = prod[:, hp * 128:(hp + 1) * 128]
            lo = jnp.sum(jnp.where(lane < 64, pp, 0.0), axis=-1, keepdims=True)
            hi = jnp.sum(jnp.where(lane >= 64, pp, 0.0), axis=-1, keepdims=True)
            dd = jnp.where(lane == 2 * hp, lo, dd)
            dd = jnp.where(lane == 2 * hp + 1, hi, dd)
        dd_ref[...] = dd

    half = _rows(tr, HG_WIDTH)
    return _call(body, name=name, grid=(t // tr,),
                 in_specs=[_rows(tr, D_MODEL), pl.BlockSpec(w_out.shape, lambda i: (0, 0, 0)), half,
                           pl.BlockSpec((tr, HG_WIDTH), lambda i: (i, 3)), half, _vec(HG_HEAD), _vec(ATT_WIDTH)],
                 out_specs=[half, half, half, _rows(tr, 128), _vec(HG_HEAD), _vec(ATT_WIDTH)],
                 out_shape=[_sds((t, HG_WIDTH), BF16)] * 3 + [_sds((t, 128)), _sds((1, HG_HEAD)), _sds((1, ATT_WIDTH))])(
                     dmix, w_out, o_hg, proj, att, og, ag)


def _dproj(dhg, dg, dqs, dkvs, name, tr=1024):
    t = dhg.shape[0]
    w3 = 3 * HG_WIDTH
    w4 = w3 + HG_WIDTH
    nbr = len(dqs)

    def body(*refs):
        h_ref, g_ref, q_refs, kv_refs, o_ref = refs[0], refs[1], refs[2:2 + nbr], refs[2 + nbr:2 + 2 * nbr], refs[-1]
        o_ref[:, :w3] = h_ref[...]
        o_ref[:, w3:w4] = g_ref[...].astype(BF16)
        o_ref[:, w4:w4 + ATT_WIDTH] = sum(r[...].astype(F32) for r in q_refs).astype(BF16)
        o_ref[:, w4 + ATT_WIDTH:] = sum(r[...].astype(F32) for r in kv_refs).astype(BF16)

    return _call(body, name=name, grid=(t // tr,),
                 in_specs=[_rows(tr, w3), _rows(tr, HG_WIDTH)] + [_rows(tr, ATT_WIDTH)] * nbr
                 + [_rows(tr, 2 * ATT_WIDTH)] * nbr,
                 out_specs=_rows(tr, IN_WIDTH), out_shape=_sds((t, IN_WIDTH), BF16))(dhg, dg, *dqs, *dkvs)


def _chunk_tri(upper):
    row = lax.broadcasted_iota(jnp.int32, (HG_GROUP, HG_CHUNK, HG_CHUNK), 1)
    col = lax.broadcasted_iota(jnp.int32, (HG_GROUP, HG_CHUNK, HG_CHUNK), 2)
    return (row <= col if upper else row >= col).astype(BF16)


def _chunk_cumsum(x, tri):
    x3 = x.reshape(HG_GROUP, HG_CHUNK, x.shape[1])
    dims = (((2,), (1,)), ((0,), (0,)))
    out = None
    for _ in range(3):
        part = x3.astype(BF16)
        x3 = x3 - part.astype(F32)
        term = lax.dot_general(tri, part, dims, preferred_element_type=F32)
        out = term if out is None else out + term
    return out.reshape(x.shape)


def _hg_gates(f_raw, q_raw, lb, tri):
    sg = _sigmoid(f_raw)
    f = lb + (1.0 - lb) * sg
    k = 1.0 - f
    b = _chunk_cumsum(jnp.log(f), tri)
    sq = _sigmoid(q_raw)
    return sg, f, k, b, sq


def _hg_masks(rows):
    row = lax.broadcasted_iota(jnp.int32, (rows, rows), 0)
    col = lax.broadcasted_iota(jnp.int32, (rows, rows), 1)
    same = (row // HG_CHUNK) == (col // HG_CHUNK)
    return jnp.logical_and(row >= col, same), jnp.logical_and(row <= col, same)


def _per_chunk(rows_of):
    return jnp.concatenate([jnp.broadcast_to(r, (HG_CHUNK, r.shape[1])) for r in rows_of], axis=0)


def _hgrn_fwd(proj, lb_logits, name):
    t = proj.shape[0]
    nc = t // HG_CHUNK
    nh = HG_WIDTH // HG_HEAD
    rows = HG_GROUP * HG_CHUNK

    def body(q_ref, f_ref, i_ref, lg_ref, o_ref, st_ref, s_scr):
        @pl.when(pl.program_id(0) == 0)
        def _():
            s_scr[...] = jnp.zeros_like(s_scr)

        lg = lg_ref[...]
        lb_all = _sigmoid(lg[0:1] - lg[1:2])
        causal, _ = _hg_masks(rows)
        tri = _chunk_tri(False)
        for h in range(nh):
            sl = slice(h * HG_HEAD, (h + 1) * HG_HEAD)
            q_raw = q_ref[:, sl].astype(F32)
            _, _, k, b, sq = _hg_gates(f_ref[:, sl].astype(F32), q_raw, lb_all[:, sl], tri)
            v = i_ref[:, sl].astype(BF16)
            gls = [b[(g + 1) * HG_CHUNK - 1:(g + 1) * HG_CHUNK] for g in range(HG_GROUP)]
            bm = _per_chunk([b[g * HG_CHUNK + HG_CHUNK // 2 - 1:g * HG_CHUNK + HG_CHUNK // 2] for g in range(HG_GROUP)])
            qd = (q_raw * sq * jnp.exp(b)).astype(BF16)
            qm = (q_raw * sq * jnp.exp(b - bm)).astype(BF16)
            km = (k * jnp.exp(bm - b)).astype(BF16)
            ke = (k * jnp.exp(_per_chunk(gls) - b)).astype(BF16)
            a = jnp.where(causal, _dot_nt(qm, km), 0.0).astype(BF16)
            o_intra = _dot(a, v)
            st = s_scr[h]
            o_inter = []
            for g in range(HG_GROUP):
                rs = slice(g * HG_CHUNK, (g + 1) * HG_CHUNK)
                st_ref[g, sl, :] = st
                o_inter.append(_dot_nt(qd[rs], st.astype(BF16)))
                st = st * jnp.exp(gls[g]) + _dot_tn(v[rs], ke[rs])
            s_scr[h] = st
            o_ref[:, sl] = (o_intra + jnp.concatenate(o_inter, axis=0)).astype(BF16)

    blk = lambda j: pl.BlockSpec((rows, HG_WIDTH), lambda c: (c, j))
    return _call(body, name=name, grid=(nc // HG_GROUP,),
                 in_specs=[blk(0), blk(1), blk(2), pl.BlockSpec((2, HG_WIDTH), lambda c: (0, 0))],
                 out_specs=[blk(0), pl.BlockSpec((HG_GROUP, HG_WIDTH, HG_HEAD), lambda c: (c, 0, 0))],
                 out_shape=[_sds((t, HG_WIDTH), BF16), _sds((nc, HG_WIDTH, HG_HEAD))],
                 scratch_shapes=[pltpu.VMEM((nh, HG_HEAD, HG_HEAD), F32)])(proj, proj, proj, lb_logits)


def _hgrn_bwd(proj, lb_logits, states, do, name):
    t = proj.shape[0]
    ng = t // (HG_GROUP * HG_CHUNK)
    nh = HG_WIDTH // HG_HEAD
    rows = HG_GROUP * HG_CHUNK

    def body(q_ref, f_ref, i_ref, lg_ref, st_ref, do_ref, d_ref, dlb_ref, ds_scr):
        first = pl.program_id(0) == 0

        @pl.when(first)
        def _():
            ds_scr[...] = jnp.zeros_like(ds_scr)

        lg = lg_ref[...]
        lb_all = _sigmoid(lg[0:1] - lg[1:2])
        causal, _ = _hg_masks(rows)
        tri = _chunk_tri(False)
        tri_t = _chunk_tri(True)
        dlb = []
        for h in range(nh):
            sl = slice(h * HG_HEAD, (h + 1) * HG_HEAD)
            q_raw = q_ref[:, sl].astype(F32)
            lb = lb_all[:, sl]
            sg, f, k, b, sq = _hg_gates(f_ref[:, sl].astype(F32), q_raw, lb, tri)
            v = i_ref[:, sl].astype(BF16)
            gls = [b[(g + 1) * HG_CHUNK - 1:(g + 1) * HG_CHUNK] for g in range(HG_GROUP)]
            bm = _per_chunk([b[g * HG_CHUNK + HG_CHUNK // 2 - 1:g * HG_CHUNK + HG_CHUNK // 2] for g in range(HG_GROUP)])
            eb = jnp.exp(b)
            ebm = jnp.exp(b - bm)
            emb = jnp.exp(bm - b)
            egb = jnp.exp(_per_chunk(gls) - b)
            ke = k * egb
            qd_b, qm_b = (q_raw * sq * eb).astype(BF16), (q_raw * sq * ebm).astype(BF16)
            km_b, ke_b = (k * emb).astype(BF16), ke.astype(BF16)
            dov = do_ref[:, sl].astype(BF16)
            a = jnp.where(causal, _dot_nt(qm_b, km_b), 0.0).astype(BF16)
            da = jnp.where(causal, _dot_nt(dov, v), 0.0).astype(BF16)
            dkm = _dot_tn(da, qm_b)
            dst = ds_scr[h]
            dqd_s, dv_s, dke_s, dgl_s = [None] * HG_GROUP, [None] * HG_GROUP, [None] * HG_GROUP, [None] * HG_GROUP
            for g in reversed(range(HG_GROUP)):
                rs = slice(g * HG_CHUNK, (g + 1) * HG_CHUNK)
                st = st_ref[g, sl, :]
                dst_b = dst.astype(BF16)
                egl = jnp.exp(gls[g])
                dqd_s[g] = _dot(dov[rs], st.astype(BF16))
                dv_s[g] = _dot_nt(ke_b[rs], dst_b)
                dke_s[g] = _dot(v[rs], dst_b)
                dgl_s[g] = jnp.sum(dst * st, axis=0, keepdims=True) * egl
                dst = _dot_tn(dov[rs], qd_b[rs]) + dst * egl
            ds_scr[h] = dst
            dqm = _dot(da, km_b)
            dqd = jnp.concatenate(dqd_s, axis=0)
            dv = _dot_tn(a, dov) + jnp.concatenate(dv_s, axis=0)
            dke = jnp.concatenate(dke_s, axis=0)
            t1 = dke * ke
            db = dqm * qm_b.astype(F32) - dkm * km_b.astype(F32) + dqd * qd_b.astype(F32) - t1
            dgl = _per_chunk([dgl_s[g] + jnp.sum(t1[g * HG_CHUNK:(g + 1) * HG_CHUNK], axis=0, keepdims=True)
                              for g in range(HG_GROUP)])
            dlf = _chunk_cumsum(db, tri_t) + dgl
            df = dlf / f - (dkm * emb + dke * egb)
            d_ref[:, sl] = ((dqm * ebm + dqd * eb) * sq * (1.0 + q_raw * (1.0 - sq))).astype(BF16)
            d_ref[:, HG_WIDTH + h * HG_HEAD:HG_WIDTH + (h + 1) * HG_HEAD] = (
                df * (1.0 - lb) * sg * (1.0 - sg)).astype(BF16)
            d_ref[:, 2 * HG_WIDTH + h * HG_HEAD:2 * HG_WIDTH + (h + 1) * HG_HEAD] = dv.astype(BF16)
            dlb.append(jnp.sum(df * (1.0 - sg), axis=0, keepdims=True))
        _acc(dlb_ref, jnp.concatenate(dlb, axis=1), first)

    rev = lambda j: pl.BlockSpec((rows, HG_WIDTH), lambda c: (ng - 1 - c, j))
    return _call(body, name=name, grid=(ng,),
                 in_specs=[rev(0), rev(1), rev(2), pl.BlockSpec((2, HG_WIDTH), lambda c: (0, 0)),
                           pl.BlockSpec((HG_GROUP, HG_WIDTH, HG_HEAD), lambda c: (ng - 1 - c, 0, 0)), rev(0)],
                 out_specs=[pl.BlockSpec((rows, 3 * HG_WIDTH), lambda c: (ng - 1 - c, 0)), _vec(HG_WIDTH)],
                 out_shape=[_sds((t, 3 * HG_WIDTH), BF16), _sds((1, HG_WIDTH))],
                 scratch_shapes=[pltpu.VMEM((nh, HG_HEAD, HG_HEAD), F32)])(proj, proj, proj, lb_logits, states, do)


def _to_sub(a, dil):
    t, w = a.shape
    return a if dil == 1 else a.reshape(t // dil, dil, w).transpose(1, 0, 2).reshape(t, w)


def _from_sub(a, dil):
    t, w = a.shape
    return a if dil == 1 else a.reshape(dil, t // dil, w).transpose(1, 0, 2).reshape(t, w)


def _att_mask(has_prev, seg):
    def place(v):
        v = v % ATT_BLOCK
        return v if seg == 1 else seg * (v % (ATT_BLOCK // seg)) + v // (ATT_BLOCK // seg)

    row = lax.broadcasted_iota(jnp.int32, (2 * ATT_BLOCK, 2 * ATT_BLOCK), 0)
    col = lax.broadcasted_iota(jnp.int32, (2 * ATT_BLOCK, 2 * ATT_BLOCK), 1)
    qi, kj = place(row), place(col)
    prev = jnp.logical_and(jnp.logical_and(col < ATT_BLOCK, kj >= qi), has_prev)
    cur = jnp.logical_and(col >= ATT_BLOCK, kj <= qi)
    return jnp.logical_or(prev, cur), lax.broadcasted_iota(jnp.int32, (1, 128), 1)


def _get(ref, sl):
    if len(ref.shape) == 2:
        return ref[:, sl]
    v = ref[:, :, sl]
    return v.reshape(ATT_BLOCK, v.shape[2])


def _put(ref, sl, val):
    if len(ref.shape) == 2:
        ref[:, sl] = val
    else:
        ref[:, :, sl] = val.reshape(ref.shape[0], ref.shape[1], val.shape[1])


def _att_spec(nb, dil, seg, width, col, back):
    bps = nb // dil

    def plain(n):
        return jnp.clip(n - back, 0, nb - 1), col

    def segmented(n):
        m = jnp.clip(n - back, 0, nb - 1)
        return 0, m // bps, m % bps, 0, col

    if seg == 1:
        return pl.BlockSpec((ATT_BLOCK, width), plain)
    return pl.BlockSpec((seg, None, None, ATT_BLOCK // seg, width), segmented)


def _att_shape(nb, dil, seg, width):
    t = nb * ATT_BLOCK
    return (t, width) if seg == 1 else (seg, dil, nb // dil, ATT_BLOCK // seg, width)


def _att_view(a, nb, dil, seg):
    return a.reshape(_att_shape(nb, dil, seg, a.shape[1]))


def _attn_fwd_block(q_ref, kc_ref, kp_ref, vc_ref, vp_ref, o_ref, l_ref, has_prev, seg):
    mask, lane = _att_mask(has_prev, seg)
    lo = lane < 64
    nq = ATT_BLOCK
    lse_all = jnp.zeros((nq, 128), F32)
    for hp in range(ATT_HEADS // 2):
        sl = slice(hp * 128, (hp + 1) * 128)
        q2 = _get(q_ref, sl)
        zero = jnp.zeros_like(q2)
        q2 = q2 * 0.125
        qs = jnp.concatenate([jnp.where(lo, q2, zero), jnp.where(lo, zero, q2)], axis=0)
        kk = jnp.concatenate([_get(kp_ref, sl), _get(kc_ref, sl)], axis=0)
        vv = jnp.concatenate([_get(vp_ref, sl), _get(vc_ref, sl)], axis=0)
        s = jnp.where(mask, _dot_nt(qs, kk), NEG)
        mx = jnp.max(s, axis=-1, keepdims=True)
        p = jnp.exp(s - mx)
        l = jnp.sum(p, axis=-1, keepdims=True)
        o = _dot(p.astype(BF16), vv) * (1.0 / l)
        _put(o_ref, sl, jnp.where(lo, o[:nq], o[nq:]).astype(BF16))
        lse = mx + jnp.log(l)
        lse_all = jnp.where(lane == 2 * hp, lse[:nq], lse_all)
        lse_all = jnp.where(lane == 2 * hp + 1, lse[nq:], lse_all)
    _put(l_ref, slice(None), lse_all)


def _attn_fwd(branches, name):
    t = branches[0][0].shape[0]
    nb = t // ATT_BLOCK
    nbr = len(branches)

    def body(*refs):
        n = pl.program_id(0)
        for i, (_, dil, seg) in enumerate(branches):
            _attn_fwd_block(*refs[5 * i:5 * i + 5], *refs[5 * nbr + 2 * i:5 * nbr + 2 * i + 2],
                            (n % (nb // dil)) != 0, seg)

    in_specs, args, out_specs, out_shape = [], [], [], []
    for qkv, dil, seg in branches:
        c0 = qkv.shape[1] // ATT_WIDTH - 3
        in_specs += [_att_spec(nb, dil, seg, ATT_WIDTH, c0 + j, back) for j, back in [(0, 0), (1, 0), (1, 1), (2, 0), (2, 1)]]
        args += [_att_view(qkv, nb, dil, seg)] * 5
        out_specs += [_att_spec(nb, dil, seg, ATT_WIDTH, 0, 0), _att_spec(nb, dil, seg, 128, 0, 0)]
        out_shape += [_sds(_att_shape(nb, dil, seg, ATT_WIDTH), BF16), _sds(_att_shape(nb, dil, seg, 128))]
    out = _call(body, name=name, grid=(nb,), in_specs=in_specs, out_specs=out_specs, out_shape=out_shape)(*args)
    return [(out[2 * i].reshape(t, ATT_WIDTH), out[2 * i + 1].reshape(t, 128)) for i in range(nbr)]


def _combine_mix_in(os_, ls_, o_hg, proj, og, ag, name, tr=512):
    t = os_[0].shape[0]
    nbr = len(os_)

    def body(*refs):
        o_refs, l_refs = refs[:nbr], refs[nbr:2 * nbr]
        oh_ref, g_ref, og_ref, ag_ref, a_ref, lt_ref, m_ref = refs[2 * nbr:]
        lane = lax.broadcasted_iota(jnp.int32, (1, 128), 1)
        ls = [r[...] for r in l_refs]
        mx = functools.reduce(jnp.maximum, ls)
        tot = mx + jnp.log(sum(jnp.exp(l - mx) for l in ls))
        lt_ref[...] = tot
        ws = [jnp.exp(l - tot) for l in ls]
        pairs = []
        for hp in range(ATT_HEADS // 2):
            sl = slice(hp * 128, (hp + 1) * 128)
            acc = jnp.zeros((tr, 128), F32)
            for w, o_ref in zip(ws, o_refs):
                wf = jnp.where(lane < 64, w[:, 2 * hp:2 * hp + 1], w[:, 2 * hp + 1:2 * hp + 2])
                acc = acc + wf * o_ref[:, sl]
            pairs.append(acc)
        av = jnp.concatenate(pairs, axis=1)
        a_ref[...] = av
        m_ref[:, HG_WIDTH:] = (av * _rms(av) * ag_ref[...]).astype(BF16)
        for h in range(HG_WIDTH // HG_HEAD):
            sl = slice(h * HG_HEAD, (h + 1) * HG_HEAD)
            oh = oh_ref[:, sl].astype(F32)
            gv = g_ref[:, sl].astype(F32)
            m_ref[:, sl] = (oh * _rms(oh) * og_ref[...] * (gv * _sigmoid(gv))).astype(BF16)

    half = _rows(tr, ATT_WIDTH)
    return _call(body, name=name, grid=(t // tr,),
                 in_specs=[half] * nbr + [_rows(tr, 128)] * nbr
                 + [half, pl.BlockSpec((tr, HG_WIDTH), lambda i: (i, 3)), _vec(HG_HEAD), _vec(ATT_WIDTH)],
                 out_specs=[half, _rows(tr, 128), _rows(tr, D_MODEL)],
                 out_shape=[_sds((t, ATT_WIDTH)), _sds((t, 128)), _sds((t, D_MODEL), BF16)])(
                     *os_, *ls_, o_hg, proj, og, ag)


def _attn_bwd_block(q_ref, kc_ref, kp_ref, vc_ref, vp_ref, do_ref, l_ref, d_ref, dq_ref, dkv_ref, carry, has_prev, seg):
    w = ATT_WIDTH
    nq = ATT_BLOCK
    mask, lane = _att_mask(has_prev, seg)
    lo = lane < 64
    lse, ddv = _get(l_ref, slice(None)), _get(d_ref, slice(None))
    for hp in range(ATT_HEADS // 2):
        sl = slice(hp * 128, (hp + 1) * 128)
        sv = slice(w + hp * 128, w + (hp + 1) * 128)
        q2, do2 = _get(q_ref, sl), _get(do_ref, sl)
        zero = jnp.zeros_like(q2)
        q2 = q2 * 0.125
        qs = jnp.concatenate([jnp.where(lo, q2, zero), jnp.where(lo, zero, q2)], axis=0)
        dos = jnp.concatenate([jnp.where(lo, do2, zero), jnp.where(lo, zero, do2)], axis=0)
        kk = jnp.concatenate([_get(kp_ref, sl), _get(kc_ref, sl)], axis=0)
        vv = jnp.concatenate([_get(vp_ref, sl), _get(vc_ref, sl)], axis=0)
        ls = jnp.concatenate([lse[:, 2 * hp:2 * hp + 1], lse[:, 2 * hp + 1:2 * hp + 2]], axis=0)
        dh = jnp.concatenate([ddv[:, 2 * hp:2 * hp + 1], ddv[:, 2 * hp + 1:2 * hp + 2]], axis=0)
        p = jnp.exp(jnp.where(mask, _dot_nt(qs, kk) - ls, NEG))
        ds = (p * (_dot_nt(dos, vv) - dh)).astype(BF16)
        dq = _dot(ds, kk) * 0.125
        _put(dq_ref, sl, jnp.where(lo, dq[:nq], dq[nq:]).astype(BF16))
        dk = _dot_tn(ds, qs)
        dv = _dot_tn(p.astype(BF16), dos)
        _put(dkv_ref, sl, (carry[:, sl] + dk[:nq]).astype(BF16))
        _put(dkv_ref, sv, (carry[:, sv] + dv[:nq]).astype(BF16))
        carry[:, sl] = dk[nq:]
        carry[:, sv] = dv[nq:]


def _attn_bwd(branches, name):
    t = branches[0][0].shape[0]
    nb = t // ATT_BLOCK
    nbr = len(branches)
    w = ATT_WIDTH

    def body(*refs):
        ins, outs, carries = refs[:8 * nbr], refs[8 * nbr:10 * nbr], refs[10 * nbr:]
        n = pl.program_id(0)

        @pl.when(n == 0)
        def _():
            for carry in carries:
                carry[...] = jnp.zeros_like(carry)

        @pl.when(n < nb)
        def _():
            for i, branch in enumerate(branches):
                dil, seg = branch[4:]
                _attn_bwd_block(*ins[8 * i:8 * i + 8], *outs[2 * i:2 * i + 2], carries[i], (n % (nb // dil)) != 0, seg)

        @pl.when(n == nb)
        def _():
            for i in range(nbr):
                _put(outs[2 * i + 1], slice(None), carries[i][...].astype(BF16))

    in_specs, args, out_specs, out_shape = [], [], [], []
    for qkv, dout, lse, dd, dil, seg in branches:
        c0 = qkv.shape[1] // w - 3
        in_specs += [_att_spec(nb, dil, seg, w, c0 + j, back) for j, back in [(0, 0), (1, 0), (1, 1), (2, 0), (2, 1)]]
        in_specs += [_att_spec(nb, dil, seg, w, 0, 0), _att_spec(nb, dil, seg, 128, 0, 0), _att_spec(nb, dil, seg, 128, 0, 0)]
        args += [_att_view(a, nb, dil, seg) for a in [qkv] * 5 + [dout, lse, dd]]
        out_specs += [_att_spec(nb, dil, seg, w, 0, 0), _att_spec(nb, dil, seg, 2 * w, 0, 1)]
        out_shape += [_sds(_att_shape(nb, dil, seg, w), BF16), _sds(_att_shape(nb, dil, seg, 2 * w), BF16)]
    out = _call(body, name=name, grid=(nb + 1,), in_specs=in_specs, out_specs=out_specs, out_shape=out_shape,
                scratch_shapes=[pltpu.VMEM((ATT_BLOCK, 2 * w), F32)] * nbr)(*args)
    return [(out[2 * i].reshape(t, w), out[2 * i + 1].reshape(t, 2 * w)) for i in range(nbr)]


def _local_step(x, tgt, mod, norm1_g, lb_logits, og, ag, norm2_g, fg, get_w, put_g, late=lambda a: a, project=None):
    shift1, scale1, gate1, shift2, scale2, gate2 = [mod[:, i * D_MODEL:(i + 1) * D_MODEL] for i in range(6)]
    fg = fg.reshape(1, D_MODEL)

    h1 = _norm_mod(x, norm1_g, scale1, shift1, "norm_mod1")
    if project is None:
        w_in = get_w("w_in", h1)
        proj = _mm_nn(h1, w_in, "mm_in", out_dtype=BF16)
    else:
        proj, w_in = project(h1)
    o_hg, states = _hgrn_fwd(proj, lb_logits, "hgrn_fwd")
    fine = DILATIONS[-1]
    layouts = [(d, 1 if d == 1 else fine // d) for d in DILATIONS]
    qkv_fine = _to_sub(proj[:, 4 * HG_WIDTH:], fine)
    qkvs = [proj if d == 1 else qkv_fine for d in DILATIONS]
    natural = lambda a, d: a if d == 1 else _from_sub(a, fine)
    outs = _attn_fwd([(q, d, seg) for q, (d, seg) in zip(qkvs, layouts)], "attn_fwd")
    att, lse, mixin = _combine_mix_in([natural(o, d) for (o, _), d in zip(outs, DILATIONS)],
                                      [natural(l, d) for (_, l), d in zip(outs, DILATIONS)],
                                      o_hg, proj, og, ag, "attn_combine_mix_in")
    w_out = get_w("w_out", mixin)
    mix, x2, h2 = _out_resid_norm_mod(x, mixin, w_out, gate1, norm2_g, scale2, shift2, "mm_out_resid_norm_mod2")
    w_gu = get_w("w_gu", h2)
    a_ff, u_ff, act = _mm_gate_up(h2, w_gu, "mm_gu")
    w_down = get_w("w_down", act)
    dx3, dffn, loss_v, dfg, dgate2 = _down_loss(x2, act, w_down, gate2, fg, tgt, "mm_down_loss")

    dffn = put_g("w_down", *_mm_tn(act, dffn, 1, "mm_down_dw", tm=2048, tk=D_FF // 2), dffn)
    dau = _mm_down_dx(dffn, w_down, a_ff, u_ff, "mm_down_dx")
    dau = put_g("w_gu", *_mm_tn(h2, dau, N_SHARD, "mm_gu_dw", tm=x.shape[0], tk=512), dau)
    dx2, dshift2, dscale2, dg2, dgate1, dmix = _norm_mod_bwd(
        dau, x2, norm2_g, scale2, dx3, "mm_gu_dx_norm_bwd", gate=gate1, mix=mix, w=w_gu)
    dmix = put_g("w_out", *_mm_tn(mixin, dmix, 1, "mm_out_dw", tm=x.shape[0], tk=512), dmix)
    do_hg, dg_raw, datt, dd, dog, dag = _mix_in_bwd(dmix, w_out, o_hg, proj, att, og, ag, "mm_out_dx_mix_in_bwd")
    datt_b = datt
    reordered = [_to_sub(a, fine) for a in (datt_b, lse, dd)]
    datts = _attn_bwd([(q,) + tuple((datt_b, lse, dd) if d == 1 else reordered) + (d, seg)
                       for q, (d, seg) in zip(qkvs, layouts)], "attn_bwd")
    dhg, dlb = _hgrn_bwd(proj, lb_logits, states, do_hg, "hgrn_bwd")
    dhg = late(dhg)
    dproj = _dproj(dhg, dg_raw, [natural(dq, d) for (dq, _), d in zip(datts, DILATIONS)],
                   [natural(dkv, d) for (_, dkv), d in zip(datts, DILATIONS)], "dproj")
    dproj = put_g("w_in", *_mm_tn(h1, dproj, N_SHARD, "mm_in_dw", tm=x.shape[0], tk=512), dproj)
    dx, dshift1, dscale1, dg1 = _norm_mod_bwd(dproj, x, norm1_g, scale1, dx2, "mm_in_dx_norm_bwd", w=w_in)

    stats = jnp.concatenate([loss_v, dfg, dg2, dg1, dlb, dag, dog,
                             dshift1, dscale1, dgate1, dshift2, dscale2, dgate2], axis=1)
    return dx, stats


def _place():
    x, y, c = lax.axis_index("x"), lax.axis_index("y"), lax.axis_index("c")
    return x, y, c


def _chip_peers(x, y, c):
    return [(1 - x, y, c), (x, 1 - y, c), (1 - x, 1 - y, c)]


def _comm_call(body, name, n_in, out_shape, scratch_shapes):
    hbm = pl.BlockSpec(memory_space=pl.ANY)
    return pl.pallas_call(body, name=name, in_specs=[hbm] * n_in, out_specs=[hbm] * len(out_shape),
                          out_shape=out_shape, scratch_shapes=scratch_shapes)


_HBM = pl.BlockSpec(memory_space=pltpu.HBM)
_SEM = pl.BlockSpec(memory_space=pltpu.SEMAPHORE)
_EFFECT = pltpu.SideEffectType.DATAFLOW_SIDE_EFFECTING


def _exchange_copy(bufs, send, recv, j, peer, place, kind):
    x, y, c = place
    target = peer
    if kind == "gather":
        src = dst = bufs[0].at[2 * x + y]
    elif kind == "scatter":
        src, dst = bufs[0].at[2 * peer[0] + peer[1]], bufs[1].at[j]
    else:
        half = bufs[0].shape[1] // 2
        rows = pl.ds(c * half, half)
        if kind == "half":
            src = dst = bufs[0].at[2 * x + y, rows]
        else:
            src = dst = bufs[0].at[2 * peer[0] + peer[1], rows]
            target = (x, y, 1 - c)
    return pltpu.make_async_remote_copy(src_ref=src, dst_ref=dst, send_sem=send.at[j], recv_sem=recv.at[j],
                                        device_id=target, device_id_type=MESH)


def _exchange_start(groups, after, kind, name):
    sizes = [len(g) for g in groups]
    flat = [b for g in groups for b in g]
    ng, nb = len(groups), len(flat)

    def body(*refs):
        bufs, sems = refs[:nb], refs[nb + 1:nb + 1 + 2 * ng]
        x, y, c = _place()
        for j, peer in enumerate(_chip_peers(x, y, c)):
            at = 0
            for i, size in enumerate(sizes):
                _exchange_copy(bufs[at:at + size], sems[2 * i], sems[2 * i + 1], j, peer, (x, y, c), kind).start()
                at += size

    any_space = pl.BlockSpec(memory_space=pl.ANY)
    out = pl.pallas_call(
        body, name=name, in_specs=[_HBM] * nb + [any_space],
        out_specs=[_SEM] * (2 * ng) + [_HBM] * nb + [any_space],
        out_shape=[pltpu.SemaphoreType.DMA((3,))] * (2 * ng) + [pltpu.HBM(b.shape, b.dtype) for b in flat]
        + [_sds(after.shape, after.dtype)],
        input_output_aliases={i: 2 * ng + i for i in range(nb + 1)},
        compiler_params=pltpu.CompilerParams(has_side_effects=_EFFECT),
    )(*[pltpu.with_memory_space_constraint(b, pltpu.HBM) for b in flat], after)
    started, at = [], 2 * ng
    for i, size in enumerate(sizes):
        started.append((out[2 * i], out[2 * i + 1], tuple(out[at:at + size])))
        at += size
    return started, out[-1]


def _exchange_wait(started, after, kind, name):
    send, recv, bufs = started
    nb = len(bufs)

    def body(*refs):
        x, y, c = _place()
        for j, peer in enumerate(_chip_peers(x, y, c)):
            cp = _exchange_copy(refs[:nb], refs[nb], refs[nb + 1], j, peer, (x, y, c), kind)
            cp.wait_send()
            cp.wait_recv()

    return pl.pallas_call(
        body, name=name, in_specs=[_HBM] * nb + [_SEM, _SEM, pl.BlockSpec(memory_space=pl.ANY)],
        out_specs=[_HBM] * nb, out_shape=[pltpu.HBM(b.shape, b.dtype) for b in bufs],
        input_output_aliases={i: i for i in range(nb)},
        compiler_params=pltpu.CompilerParams(has_side_effects=_EFFECT),
    )(*bufs, send, recv, after)


def _sibling_copies(v_refs, l_refs, send, recv):
    x, y, c = _place()
    return [pltpu.make_async_remote_copy(src_ref=v, dst_ref=l, send_sem=send.at[a], recv_sem=recv.at[a],
                                         device_id=(x, y, 1 - c), device_id_type=MESH)
            for a, (v, l) in enumerate(zip(v_refs, l_refs))]


def _sibling_start(vs, after, name):
    vs = list(vs)
    n = len(vs)
    lands = [lax.empty(v.shape, v.dtype) for v in vs]

    def body(*refs):
        for cp in _sibling_copies(refs[:n], refs[n:2 * n], refs[2 * n + 1], refs[2 * n + 2]):
            cp.start()

    any_space = pl.BlockSpec(memory_space=pl.ANY)
    out = pl.pallas_call(
        body, name=name, in_specs=[_HBM] * (2 * n) + [any_space],
        out_specs=[_SEM, _SEM] + [_HBM] * (2 * n) + [any_space],
        out_shape=[pltpu.SemaphoreType.DMA((n,))] * 2 + [pltpu.HBM(b.shape, b.dtype) for b in vs + lands]
        + [_sds(after.shape, after.dtype)],
        input_output_aliases={i: 2 + i for i in range(2 * n + 1)},
        compiler_params=pltpu.CompilerParams(has_side_effects=_EFFECT),
    )(*[pltpu.with_memory_space_constraint(b, pltpu.HBM) for b in vs + lands], after)
    return (out[0], out[1], tuple(out[2:2 + n]), tuple(out[2 + n:2 + 2 * n])), out[-1]


def _sibling_wait(started, after, name):
    send, recv, vs, lands = started
    n = len(vs)

    def body(*refs):
        for cp in _sibling_copies(refs[:n], refs[n:2 * n], refs[2 * n], refs[2 * n + 1]):
            cp.wait_send()
            cp.wait_recv()

    out = pl.pallas_call(
        body, name=name, in_specs=[_HBM] * (2 * n) + [_SEM, _SEM, pl.BlockSpec(memory_space=pl.ANY)],
        out_specs=[_HBM] * (2 * n), out_shape=[pltpu.HBM(b.shape, b.dtype) for b in vs + lands],
        input_output_aliases={i: i for i in range(2 * n)},
        compiler_params=pltpu.CompilerParams(has_side_effects=_EFFECT),
    )(*vs, *lands, send, recv, after)
    return out[:n], out[n:]


def _swap_sibling(vs, name):
    n = len(vs)

    def body(*refs):
        v_refs, o_refs, (send, recv) = refs[:n], refs[n:2 * n], refs[2 * n:]
        x, y, c = _place()
        cps = [pltpu.make_async_remote_copy(
            src_ref=v_refs[a], dst_ref=o_refs[a], send_sem=send.at[a], recv_sem=recv.at[a],
            device_id=(x, y, 1 - c), device_id_type=MESH) for a in range(n)]
        for cp in cps:
            cp.start()
        for cp in cps:
            cp.wait()

    return _comm_call(body, name, n, [_sds(v.shape, v.dtype) for v in vs],
                      [pltpu.SemaphoreType.DMA((n,)), pltpu.SemaphoreType.DMA((n,))])(*vs)


def _everyone(x, y, c):
    return [(1 - x if k & 4 else x, 1 - y if k & 2 else y, 1 - c if k & 1 else c) for k in range(1, 8)]


def _all_gather_copies(land_ref, send, recv, arriving):
    x, y, c = _place()
    me = 4 * x + 2 * y + c
    return [pltpu.make_async_remote_copy(
        src_ref=land_ref.at[me], dst_ref=land_ref.at[4 * p[0] + 2 * p[1] + p[2] if arriving else me],
        send_sem=send.at[k], recv_sem=recv.at[k], device_id=p, device_id_type=MESH)
        for k, p in enumerate(_everyone(x, y, c))]


def _all_gather_start(v, name):
    x, y, c = _place()
    land = lax.dynamic_update_slice(lax.empty((8,) + v.shape, v.dtype), v[None], (4 * x + 2 * y + c, 0, 0))

    def body(land_ref, v_ref, send, recv, land_out, v_out):
        for cp in _all_gather_copies(land_ref, send, recv, False):
            cp.start()

    any_space = pl.BlockSpec(memory_space=pl.ANY)
    out = pl.pallas_call(
        body, name=name, in_specs=[_HBM, any_space], out_specs=[_SEM, _SEM, _HBM, any_space],
        out_shape=[pltpu.SemaphoreType.DMA((7,))] * 2 + [pltpu.HBM(land.shape, land.dtype), _sds(v.shape, v.dtype)],
        input_output_aliases={0: 2, 1: 3}, compiler_params=pltpu.CompilerParams(has_side_effects=_EFFECT),
    )(pltpu.with_memory_space_constraint(land, pltpu.HBM), v)
    return tuple(out[:3]), out[3]


def _all_gather_wait(started, after, name):
    send, recv, land = started

    def body(land_ref, send, recv, after_ref, land_out):
        for cp in _all_gather_copies(land_ref, send, recv, True):
            cp.wait_send()
            cp.wait_recv()

    return pl.pallas_call(
        body, name=name, in_specs=[_HBM, _SEM, _SEM, pl.BlockSpec(memory_space=pl.ANY)], out_specs=_HBM,
        out_shape=pltpu.HBM(land.shape, land.dtype), input_output_aliases={0: 0},
        compiler_params=pltpu.CompilerParams(has_side_effects=_EFFECT),
    )(land, send, recv, after)


def _gather_rows(v, name):
    r, n = v.shape

    def body(v_ref, o_ref, send, recv, loc):
        x, y, c = _place()
        me = 4 * x + 2 * y + c
        own = pltpu.make_async_copy(v_ref, o_ref.at[me], loc)
        own.start()
        peers = []
        for k in range(1, 8):
            px = 1 - x if k & 4 else x
            py = 1 - y if k & 2 else y
            pc = 1 - c if k & 1 else c
            peers.append((px, py, pc))
        sends = []
        for k, peer in enumerate(peers):
            cp = pltpu.make_async_remote_copy(src_ref=v_ref, dst_ref=o_ref.at[me], send_sem=send.at[k],
                                              recv_sem=recv.at[k], device_id=peer, device_id_type=MESH)
            cp.start()
            sends.append(cp)
        for k, peer in enumerate(peers):
            pltpu.make_async_remote_copy(src_ref=v_ref, dst_ref=o_ref.at[4 * peer[0] + 2 * peer[1] + peer[2]],
                                         send_sem=send.at[k], recv_sem=recv.at[k], device_id=peer,
                                         device_id_type=MESH).wait_recv()
        for cp in sends:
            cp.wait_send()
        own.wait()

    vmem = pl.BlockSpec(memory_space=pltpu.VMEM)
    return pl.pallas_call(body, name=name, in_specs=[vmem], out_specs=vmem, out_shape=_sds((8, r, n), v.dtype),
                          scratch_shapes=[pltpu.SemaphoreType.DMA((7,)), pltpu.SemaphoreType.DMA((7,)),
                                          pltpu.SemaphoreType.DMA])(v)


def _cast_place(ws, shard, name):
    n = len(ws)

    def body(s_ref, *refs):
        for w_ref, o_ref in zip(refs[:n], refs[n:]):
            o_ref[0] = w_ref[...].astype(BF16)

    return pl.pallas_call(
        body, name=name, out_shape=[_sds((N_SHARD,) + w.shape, BF16) for w in ws],
        grid_spec=pltpu.PrefetchScalarGridSpec(
            num_scalar_prefetch=1, grid=(4,),
            in_specs=[pl.BlockSpec((w.shape[0] // 4, w.shape[1]), lambda i, s: (i, 0)) for w in ws],
            out_specs=[pl.BlockSpec((1, w.shape[0] // 4, w.shape[1]), lambda i, s: (s[0], i, 0)) for w in ws]),
        compiler_params=pltpu.CompilerParams(dimension_semantics=("arbitrary",), vmem_limit_bytes=VMEM_LIMIT),
    )(shard.reshape(1).astype(jnp.int32), *ws)


def _mod_part(c_all, w_ada, b_ada, name):
    n = w_ada.shape[1]

    def body(c_ref, w_ref, b_ref, a_ref, p_ref):
        cv = c_ref[...]
        ca = cv * _sigmoid(cv)
        a_ref[...] = ca
        p_ref[...] = jnp.dot(ca, w_ref[...], precision=lax.Precision.HIGHEST, preferred_element_type=F32) + b_ref[...]

    full = lambda a: pl.BlockSpec(a.shape, lambda i: (0, 0))
    return _call(body, name=name, grid=(1,), in_specs=[full(c_all), full(w_ada), full(b_ada)],
                 out_specs=[pl.BlockSpec((8, D_MODEL), lambda i: (0, 0)), pl.BlockSpec((8, n), lambda i: (0, 0))],
                 out_shape=[_sds((8, D_MODEL)), _sds((8, n))])(c_all, w_ada, b_ada)


def _sum_received(gs, shard, lands, name):
    n = len(gs)

    def body(s_ref, *refs):
        for g_ref, l_ref, o_ref in zip(refs[:n], refs[n:2 * n], refs[2 * n:]):
            o_ref[...] = ((g_ref[0] + l_ref[0].astype(F32)) + l_ref[1].astype(F32)) + l_ref[2].astype(F32)

    quarter = lambda g: (g.shape[1] // 4, g.shape[2])
    return pl.pallas_call(
        body, name=name, out_shape=[_sds(g.shape[1:]) for g in gs],
        grid_spec=pltpu.PrefetchScalarGridSpec(
            num_scalar_prefetch=1, grid=(4,),
            in_specs=[pl.BlockSpec((1,) + quarter(g), lambda i, s: (s[0], i, 0)) for g in gs]
            + [pl.BlockSpec((3,) + quarter(g), lambda i, s: (0, i, 0)) for g in gs],
            out_specs=[pl.BlockSpec(quarter(g), lambda i, s: (i, 0)) for g in gs]),
        compiler_params=pltpu.CompilerParams(dimension_semantics=("arbitrary",), vmem_limit_bytes=VMEM_LIMIT),
    )(shard.reshape(1).astype(jnp.int32), *gs, *lands)


def _adamw_outer(w, ct, dm, m, v, name):
    k, n = w.shape
    tr = k // 4

    def body(w_ref, c_ref, d_ref, m_ref, v_ref, g_out, d_out, m_out, v_out):
        cv = c_ref[...]
        dv = d_ref[...]
        g = cv[:, 0:1] * dv[0:1, :]
        for i in range(1, 8):
            g = g + cv[:, i:i + 1] * dv[i:i + 1, :]
        g_out[...] = g
        d_out[...], m_out[...], v_out[...] = _adamw_math(w_ref[...], g, m_ref[...], v_ref[...])

    row = _rows(tr, n)
    return _call(body, name=name, grid=(4,),
                 in_specs=[row, _rows(tr, 8), pl.BlockSpec((8, n), lambda i: (0, 0)), row, row],
                 out_specs=[row] * 4, out_shape=[_sds((k, n))] * 4)(w, ct, dm, m, v)


def _adamw_math(w, g, m, v):
    m_new = ADAM_B1 * m + (1.0 - ADAM_B1) * g
    v_new = ADAM_B2 * v + (1.0 - ADAM_B2) * (g * g)
    m_hat = m_new / (1.0 - ADAM_B1 ** ADAM_STEP)
    v_hat = v_new / (1.0 - ADAM_B2 ** ADAM_STEP)
    return -ADAM_LR * (m_hat / (jnp.sqrt(v_hat) + ADAM_EPS) + ADAM_WD * w), m_new, v_new


def _small_update(stats, smalls, name):
    offsets = [ST_DMOD, ST_DG1, ST_DLB, ST_DOG, ST_DAG, ST_DG2, ST_DFG]
    lb_index = 2

    def body(*refs):
        s_ref, ins, l_ref, outs = refs[0], refs[1:22], refs[22], refs[23:]
        tot = s_ref[0:1, :]
        for i in range(1, 8):
            tot = tot + s_ref[i:i + 1, :]
        l_ref[...] = jnp.zeros((1, 128), F32) + (0.5 / D_MODEL) * jnp.sum(tot[:, ST_LOSS:ST_LOSS + D_MODEL])
        for p, off in enumerate(offsets):
            w_ref, m_ref, v_ref = ins[3 * p:3 * p + 3]
            g_out, d_out, m_out, v_out = outs[4 * p:4 * p + 4]
            g = tot[:, off:off + w_ref.shape[1]]
            if p == lb_index:
                lg = w_ref[...]
                lb = _sigmoid(lg[0:1] - lg[1:2])
                g = g * lb * (1.0 - lb)
            for r in range(w_ref.shape[0]):
                rows = slice(r, r + 1)
                gr = g if r == 0 else -g
                delta, m_new, v_new = _adamw_math(w_ref[rows, :], gr, m_ref[rows, :], v_ref[rows, :])
                g_out[rows, :] = gr
                d_out[rows, :] = delta
                m_out[rows, :] = m_new
                v_out[rows, :] = v_new

    full = lambda a: pl.BlockSpec(a.shape, lambda i: (0, 0))
    flat = [a for t in smalls for a in t]
    return _call(body, name=name, grid=(1,),
                 in_specs=[full(stats)] + [full(a) for a in flat],
                 out_specs=[pl.BlockSpec((1, 128), lambda i: (0, 0))] + [full(t[0]) for t in smalls for _ in range(4)],
                 out_shape=[_sds((1, 128))] + [_sds(t[0].shape) for t in smalls for _ in range(4)])(stats, *flat)


def _adamw(params, name):
    n = len(params)

    def body(*refs):
        for p in range(n):
            w_ref, ga_ref, gb_ref, m_ref, v_ref = refs[5 * p:5 * p + 5]
            g_out, d_out, m_out, v_out = refs[5 * n + 4 * p:5 * n + 4 * p + 4]
            g = ga_ref[...] + gb_ref[...]
            g_out[...] = g
            d_out[...], m_out[...], v_out[...] = _adamw_math(w_ref[...], g, m_ref[...], v_ref[...])

    row = lambda w: _rows(w.shape[0] // 4, w.shape[1])
    out = _call(body, name=name, grid=(4,), in_specs=[row(p[0]) for p in params for _ in range(5)],
                out_specs=[row(p[0]) for p in params for _ in range(4)],
                out_shape=[_sds(p[0].shape) for p in params for _ in range(4)])(*[a for p in params for a in p])
    return [tuple(out[4 * p:4 * p + 4]) for p in range(n)]


def kernel(x, c, w_ada, b_ada, norm1_g, w_in, hg_lb_logits, hg_onorm_g, att_onorm_g, w_out, norm2_g, w_gate_up, w_down, final_g, loss_target, m_w_ada, m_b_ada, m_norm1_g, m_w_in, m_hg_lb_logits, m_hg_onorm_g, m_att_onorm_g, m_w_out, m_norm2_g, m_w_gate_up, m_w_down, m_final_g, v_w_ada, v_b_ada, v_norm1_g, v_w_in, v_hg_lb_logits, v_hg_onorm_g, v_att_onorm_g, v_w_out, v_norm2_g, v_w_gate_up, v_w_down, v_final_g):
    ix, iy, ic = _place()
    shard = 2 * ix + iy
    sample = 4 * ix + 2 * iy + ic
    n_ada = w_ada.shape[2]

    shards = [w_in[0], w_out[0], w_gate_up[0], w_down[0]]
    names = ["w_in", "w_out", "w_gu", "w_down"]
    shapes = [(N_SHARD,) + w.shape for w in shards]
    placed = [(_cast_place(shards[:1], shard, "place_w_in")[0],)]
    placed += [(p,) for p in _cast_place(shards[1:], shard, "place_rest")]

    c_all = _gather_rows(c, "gather_c").reshape(8, D_MODEL)
    b_part = lax.dynamic_slice(b_ada, (0, shard * n_ada), (1, n_ada))
    c_act, part = _mod_part(c_all, w_ada[0], b_part, "mod_part")
    parts = _gather_rows(part, "gather_mod")[::2]
    mod = lax.dynamic_index_in_dim(parts, sample, axis=1, keepdims=False).reshape(1, 6 * D_MODEL)
    (first,), mod = _exchange_start(placed[:1], mod, "half", "gather_start_w_in")
    gathering = {}

    def get_w(name, after):
        if name == "w_in":
            halves = _exchange_wait(first, after, "half", "gather_wait_w_in")
            (passing,), token = _exchange_start([tuple(halves)], mod, "forward", "forward_start_w_in")
            (full,) = _exchange_wait(passing, token, "forward", "forward_wait_w_in")
            rest, full = _exchange_start(placed[1:], full, "gather", "gather_start_rest")
            gathering.update(zip(names[1:], rest))
            return full
        (full,) = _exchange_wait(gathering[name], after, "gather", "gather_wait_" + name)
        return full if name == "w_gu" else full.reshape(1, -1, D_MODEL)

    scattering = {}

    def put_g(name, g, g_bf16, then):
        shape = shapes[names.index(name)]
        land = lax.empty((3,) + shape[1:], BF16)
        (started,), then = _exchange_start([(g_bf16.reshape(shape), land)], then, "scatter", "scatter_start_" + name)
        scattering[name] = (g.reshape(shape), started)
        return then

    def summed(group, after, tag):
        lands = [_exchange_wait(scattering[nm][1], after, "scatter", "scatter_wait_" + nm)[1] for nm in group]
        return _sum_received([scattering[nm][0] for nm in group], shard, lands, "sum_" + tag)

    early = ["w_down", "w_gu", "w_out"]
    swapping = []

    def late(a):
        started, a = _sibling_start(summed(early, a, "early"), a, "swap_start")
        swapping.append(started)
        return a

    def project(h1):
        own = _mm_own_shard(h1, shards[0], shard, N_SHARD, "mm_in_own")
        w_full = get_w("w_in", own)
        return _mm_other_shards(h1, w_full, own, shard, "mm_in_rest"), w_full

    dx, stats = _local_step(x[0], loss_target[0], mod, norm1_g, hg_lb_logits, hg_onorm_g, att_onorm_g,
                            norm2_g, final_g, get_w, put_g, late, project)

    gathering_stats, stats = _all_gather_start(stats, "stats_start")
    moments = [(m_w_in, v_w_in), (m_w_out, v_w_out), (m_w_gate_up, v_w_gate_up), (m_w_down, v_w_down)]

    def update(group, sums, other, tag):
        params = [(shards[names.index(nm)], s, o, moments[names.index(nm)][0][0], moments[names.index(nm)][1][0])
                  for nm, s, o in zip(group, sums, other)]
        return dict(zip(group, _adamw(params, "adamw_" + tag)))

    sums, other = _sibling_wait(swapping[0], stats, "swap_wait")
    done = update(early, sums, other, "early")
    sum_in = summed(["w_in"], done["w_out"][1], "w_in")
    done.update(update(["w_in"], sum_in, _swap_sibling(sum_in, "swap_sum_in"), "w_in"))

    stats_all = _all_gather_wait(gathering_stats, done["w_in"][1], "stats_wait").reshape(8, ST_WIDTH)
    dmod = lax.dynamic_slice(stats_all, (0, ST_DMOD + shard * n_ada), (8, n_ada))

    as_row = lambda a: a.reshape(1, -1) if a.ndim == 1 else a
    smalls = [tuple(as_row(a) for a in t) for t in [
        (b_ada, m_b_ada, v_b_ada), (norm1_g, m_norm1_g, v_norm1_g),
        (hg_lb_logits, m_hg_lb_logits, v_hg_lb_logits), (hg_onorm_g, m_hg_onorm_g, v_hg_onorm_g),
        (att_onorm_g, m_att_onorm_g, v_att_onorm_g), (norm2_g, m_norm2_g, v_norm2_g),
        (final_g, m_final_g, v_final_g)]]
    loss, *small_out = _small_update(stats_all, smalls, "small_update")
    shapes_out = [b_ada.shape, norm1_g.shape, hg_lb_logits.shape, hg_onorm_g.shape, att_onorm_g.shape,
                  norm2_g.shape, final_g.shape]
    sg, sd, sm, sv = [[small_out[4 * p + i].reshape(shapes_out[p]) for p in range(7)] for i in range(4)]

    ada = _adamw_outer(w_ada[0], c_act.T, dmod, m_w_ada[0], v_w_ada[0], "adamw_w_ada")
    big = [ada] + [done[nm] for nm in names]
    bg, bd, bm, bv = [[t[i][None] for t in big] for i in range(4)]

    def order(b, s):
        return [b[0], s[0], s[1], b[1], s[2], s[3], s[4], b[2], s[5], b[3], b[4], s[6]]

    return (loss[0, 0], dx[None], *order(bg, sg), *order(bd, sd), *order(bm, sm), *order(bv, sv))
```

```python
import functools

import jax
import jax.numpy as jnp
from jax import lax
from jax.experimental import pallas as pl
from jax.experimental.pallas import tpu as pltpu

F32 = jnp.float32
BF16 = jnp.bfloat16
MESH = pl.DeviceIdType.MESH

D_MODEL = 1024
HG_WIDTH = 512
HG_HEAD = 128
HG_CHUNK = 64
HG_GROUP = 4
ATT_WIDTH = 512
ATT_HEADS = 8
ATT_BLOCK = 128
DILATIONS = (1, 4, 16)
D_FF = 2816
IN_WIDTH = 3584
N_SHARD = 4
RMS_EPS = 1e-6
NEG = -1e30

ADAM_LR = 0.001
ADAM_B1 = 0.9
ADAM_B2 = 0.999
ADAM_EPS = 1e-08
ADAM_WD = 0.01
ADAM_STEP = 10

VMEM_LIMIT = 56 * 2**20

ST_LOSS, ST_DFG, ST_DG2, ST_DG1 = 0, 1024, 2048, 3072
ST_DLB, ST_DAG, ST_DOG, ST_DMOD = 4096, 4608, 5120, 5248
ST_WIDTH = 5248 + 6144


def _call(body, *, name, grid, in_specs, out_specs, out_shape, scratch_shapes=()):
    return pl.pallas_call(
        body, name=name, grid=grid, in_specs=in_specs, out_specs=out_specs, out_shape=out_shape,
        scratch_shapes=list(scratch_shapes),
        compiler_params=pltpu.CompilerParams(
            dimension_semantics=("arbitrary",) * len(grid), vmem_limit_bytes=VMEM_LIMIT))


def _sds(shape, dtype=F32):
    return jax.ShapeDtypeStruct(shape, dtype)


def _dot(a, b):
    return jnp.dot(a, b, preferred_element_type=F32)


def _dot_nt(a, b):
    return lax.dot_general(a, b, (((1,), (1,)), ((), ())), preferred_element_type=F32)


def _dot_tn(a, b):
    return lax.dot_general(a, b, (((0,), (0,)), ((), ())), preferred_element_type=F32)


def _sigmoid(x):
    return 1.0 / (1.0 + jnp.exp(-x))


def _rows(tr, width):
    return pl.BlockSpec((tr, width), lambda i: (i, 0))


def _vec(width):
    return pl.BlockSpec((1, width), lambda i: (0, 0))


def _acc(ref, val, first):
    @pl.when(first)
    def _():
        ref[...] = val

    @pl.when(jnp.logical_not(first))
    def _():
        ref[...] += val


def _mm_nn(a, b3, name, tm=1024, out_dtype=F32):
    m, k = a.shape
    s, _, n = b3.shape

    def body(a_ref, b_ref, o_ref):
        o_ref[...] = _dot(a_ref[...], b_ref[0]).astype(out_dtype)

    return _call(
        body, name=name, grid=(s, m // tm),
        in_specs=[pl.BlockSpec((tm, k), lambda j, i: (i, 0)), pl.BlockSpec((1, k, n), lambda j, i: (j, 0, 0))],
        out_specs=pl.BlockSpec((tm, n), lambda j, i: (i, j)), out_shape=_sds((m, s * n), out_dtype))(a, b3)


def _mm_own_shard(a, w, shard, s, name, tm=1024):
    m, k = a.shape
    n = w.shape[1]

    def body(s_ref, a_ref, w_ref, o_ref):
        o_ref[...] = _dot(a_ref[...], w_ref[...].astype(BF16)).astype(BF16)

    return pl.pallas_call(
        body, name=name, out_shape=_sds((m, s * n), BF16),
        grid_spec=pltpu.PrefetchScalarGridSpec(
            num_scalar_prefetch=1, grid=(m // tm,),
            in_specs=[pl.BlockSpec((tm, k), lambda i, sh: (i, 0)), pl.BlockSpec((k, n), lambda i, sh: (0, 0))],
            out_specs=pl.BlockSpec((tm, n), lambda i, sh: (i, sh[0]))),
        compiler_params=pltpu.CompilerParams(dimension_semantics=("arbitrary",), vmem_limit_bytes=VMEM_LIMIT),
    )(shard.reshape(1).astype(jnp.int32), a, w)


def _mm_other_shards(a, b3, partial, shard, name, tm=1024):
    m, k = a.shape
    s, _, n = b3.shape
    which = lambda j, sh: (sh[0] + 1 + j) % s

    def body(s_ref, a_ref, b_ref, p_ref, o_ref):
        o_ref[...] = _dot(a_ref[...], b_ref[0]).astype(BF16)

    return pl.pallas_call(
        body, name=name, out_shape=_sds(partial.shape, BF16),
        grid_spec=pltpu.PrefetchScalarGridSpec(
            num_scalar_prefetch=1, grid=(s - 1, m // tm),
            in_specs=[pl.BlockSpec((tm, k), lambda j, i, sh: (i, 0)),
                      pl.BlockSpec((1, k, n), lambda j, i, sh: (which(j, sh), 0, 0)),
                      pl.BlockSpec(memory_space=pl.ANY)],
            out_specs=pl.BlockSpec((tm, n), lambda j, i, sh: (i, which(j, sh)))),
        input_output_aliases={3: 0},
        compiler_params=pltpu.CompilerParams(dimension_semantics=("arbitrary",) * 2, vmem_limit_bytes=VMEM_LIMIT),
    )(shard.reshape(1).astype(jnp.int32), a, b3, partial)


def _mm_tn(a, dy, s, name, tm, tk):
    m, k = a.shape
    n = dy.shape[1] // s
    steps = m // tm

    def body(a_ref, dy_ref, o_ref, ob_ref):
        p = _dot_tn(a_ref[...], dy_ref[...])[None]
        if steps == 1:
            o_ref[...] = p
            ob_ref[...] = p.astype(BF16)
        else:
            i = pl.program_id(2)
            _acc(o_ref, p, i == 0)

            @pl.when(i == steps - 1)
            def _():
                ob_ref[...] = o_ref[...].astype(BF16)

    out = pl.BlockSpec((1, tk, n), lambda kk, j, i: (j, kk, 0))
    return _call(
        body, name=name, grid=(k // tk, s, steps),
        in_specs=[pl.BlockSpec((tm, tk), lambda kk, j, i: (i, kk)), pl.BlockSpec((tm, n), lambda kk, j, i: (i, j))],
        out_specs=[out, out], out_shape=[_sds((s, k, n)), _sds((s, k, n), BF16)])(a, dy)


def _rms(x):
    return lax.rsqrt(jnp.mean(x * x, axis=-1, keepdims=True) + RMS_EPS)


def _rms_bwd(dxh, xh, r):
    return r * (dxh - xh * jnp.mean(dxh * xh, axis=-1, keepdims=True))


def _norm_mod(x, g, scale, shift, name, tr=512):
    t = x.shape[0]

    def body(x_ref, g_ref, sc_ref, sh_ref, h_ref):
        xv = x_ref[...]
        n = xv * _rms(xv) * g_ref[...]
        h_ref[...] = (n * (1.0 + sc_ref[...]) + sh_ref[...]).astype(BF16)

    return _call(body, name=name, grid=(t // tr,),
                 in_specs=[_rows(tr, D_MODEL), _vec(D_MODEL), _vec(D_MODEL), _vec(D_MODEL)],
                 out_specs=_rows(tr, D_MODEL), out_shape=_sds((t, D_MODEL), BF16))(x, g, scale, shift)


def _out_resid_norm_mod(x, mixin, w_out, gate, g, scale, shift, name, tr=512):
    t = x.shape[0]

    def body(x_ref, mi_ref, w_ref, gt_ref, g_ref, sc_ref, sh_ref, m_ref, x2_ref, h_ref):
        mix = _dot(mi_ref[...], w_ref[0])
        m_ref[...] = mix
        x2 = x_ref[...] + gt_ref[...] * mix
        x2_ref[...] = x2
        n = x2 * _rms(x2) * g_ref[...]
        h_ref[...] = (n * (1.0 + sc_ref[...]) + sh_ref[...]).astype(BF16)

    row = _rows(tr, D_MODEL)
    return _call(body, name=name, grid=(t // tr,),
                 in_specs=[row, row, pl.BlockSpec(w_out.shape, lambda i: (0, 0, 0))] + [_vec(D_MODEL)] * 4,
                 out_specs=[row, row, row],
                 out_shape=[_sds((t, D_MODEL)), _sds((t, D_MODEL)), _sds((t, D_MODEL), BF16)])(
                     x, mixin, w_out, gate, g, scale, shift)


def _mm_gate_up(h, w_gu, name, tm=1024):
    m, k = h.shape
    n = w_gu.shape[2]

    def body(h_ref, wa_ref, wu_ref, da_ref, du_ref, o_ref):
        hv = h_ref[...]
        a = _dot(hv, wa_ref[0])
        u = _dot(hv, wu_ref[0])
        sg = _sigmoid(a)
        silu = a * sg
        da_ref[...] = (u * sg * (1.0 + a * (1.0 - sg))).astype(BF16)
        du_ref[...] = silu.astype(BF16)
        o_ref[...] = (silu * u).astype(BF16)

    out = pl.BlockSpec((tm, n), lambda j, i: (i, j))
    return _call(body, name=name, grid=(2, m // tm),
                 in_specs=[pl.BlockSpec((tm, k), lambda j, i: (i, 0)), pl.BlockSpec((1, k, n), lambda j, i: (j, 0, 0)),
                           pl.BlockSpec((1, k, n), lambda j, i: (j + 2, 0, 0))],
                 out_specs=[out, out, out], out_shape=[_sds((m, 2 * n), BF16)] * 3)(h, w_gu, w_gu)


def _mm_down_dx(dffn, w_down, act_da, act_du, name, tm=512):
    m = dffn.shape[0]
    _, k, n = w_down.shape

    def body(d_ref, w_ref, da_ref, du_ref, o_ref):
        dact = _dot_nt(d_ref[...], w_ref[0])
        o_ref[:, :k] = (dact * da_ref[...].astype(F32)).astype(BF16)
        o_ref[:, k:] = (dact * du_ref[...].astype(F32)).astype(BF16)

    return _call(body, name=name, grid=(m // tm,),
                 in_specs=[_rows(tm, n), pl.BlockSpec((1, k, n), lambda i: (0, 0, 0)), _rows(tm, k), _rows(tm, k)],
                 out_specs=_rows(tm, 2 * k), out_shape=_sds((m, 2 * k), BF16))(dffn, w_down, act_da, act_du)


def _down_loss(x2, act, w_down, gate, fg, tgt, name, tr=512):
    t = x2.shape[0]
    _, k, n = w_down.shape

    def body(x_ref, a_ref, w_ref, gt_ref, fg_ref, t_ref, dx_ref, df_ref, l_ref, dfg_ref, dgt_ref):
        first = pl.program_id(0) == 0
        ffn_v = _dot(a_ref[...], w_ref[0])
        x3 = x_ref[...] + gt_ref[...] * ffn_v
        r = _rms(x3)
        xh = x3 * r
        err = xh * fg_ref[...] - t_ref[...]
        dy = err * (1.0 / D_MODEL)
        dx3 = _rms_bwd(dy * fg_ref[...], xh, r)
        dx_ref[...] = dx3
        df_ref[...] = (dx3 * gt_ref[...]).astype(BF16)
        _acc(l_ref, jnp.sum(err * err, axis=0, keepdims=True), first)
        _acc(dfg_ref, jnp.sum(dy * xh, axis=0, keepdims=True), first)
        _acc(dgt_ref, jnp.sum(dx3 * ffn_v, axis=0, keepdims=True), first)

    row, vec = _rows(tr, D_MODEL), _vec(D_MODEL)
    return _call(body, name=name, grid=(t // tr,),
                 in_specs=[row, _rows(tr, k), pl.BlockSpec((1, k, n), lambda i: (0, 0, 0)), vec, vec, row],
                 out_specs=[row, row, vec, vec, vec],
                 out_shape=[_sds((t, D_MODEL)), _sds((t, D_MODEL), BF16)] + [_sds((1, D_MODEL))] * 3)(
                     x2, act, w_down, gate, fg, tgt)


def _norm_mod_bwd(dh, x, g, scale, dres, name, gate=None, mix=None, w=None, tr=512):
    t = x.shape[0]
    below = gate is not None

    def body(*refs):
        if w is not None:
            w_ref, refs = refs[1], refs[:1] + refs[2:]
        if below:
            dh_ref, x_ref, g_ref, sc_ref, dr_ref, gt_ref, m_ref, dx_ref, dsh_ref, dsc_ref, dg_ref, dgt_ref, dm_ref = refs
        else:
            dh_ref, x_ref, g_ref, sc_ref, dr_ref, dx_ref, dsh_ref, dsc_ref, dg_ref = refs
        first = pl.program_id(0) == 0
        xv = x_ref[...]
        if w is None:
            dhv = dh_ref[...].astype(F32)
        else:
            n = w.shape[2]
            dhv = _dot_nt(dh_ref[:, 0:n], w_ref[0])
            for j in range(1, w.shape[0]):
                dhv = dhv + _dot_nt(dh_ref[:, j * n:(j + 1) * n], w_ref[j])
        r = _rms(xv)
        xh = xv * r
        dn = dhv * (1.0 + sc_ref[...])
        dx = dr_ref[...] + _rms_bwd(dn * g_ref[...], xh, r)
        dx_ref[...] = dx
        _acc(dsh_ref, jnp.sum(dhv, axis=0, keepdims=True), first)
        _acc(dsc_ref, jnp.sum(dhv * xh * g_ref[...], axis=0, keepdims=True), first)
        _acc(dg_ref, jnp.sum(dn * xh, axis=0, keepdims=True), first)
        if below:
            _acc(dgt_ref, jnp.sum(dx * m_ref[...], axis=0, keepdims=True), first)
            dm_ref[...] = (dx * gt_ref[...]).astype(BF16)

    row, vec = _rows(tr, D_MODEL), _vec(D_MODEL)
    first_specs = [row] if w is None else [
        _rows(tr, dh.shape[1]), pl.BlockSpec(w.shape, lambda i: (0, 0, 0), pipeline_mode=pl.Buffered(1))]
    in_specs = first_specs + [row, vec, vec, row] + ([vec, row] if below else [])
    out_specs = [row, vec, vec, vec] + ([vec, row] if below else [])
    out_shape = [_sds((t, D_MODEL))] + [_sds((1, D_MODEL))] * 3 + ([_sds((1, D_MODEL)), _sds((t, D_MODEL), BF16)] if below else [])
    args = ((dh,) if w is None else (dh, w)) + (x, g, scale, dres) + ((gate, mix) if below else ())
    return _call(body, name=name, grid=(t // tr,), in_specs=in_specs, out_specs=out_specs, out_shape=out_shape)(*args)


def _mix_in_bwd(dmix, w_out, o_hg, proj, att, og, ag, name, tr=512):
    t = o_hg.shape[0]

    def body(dy_ref, w_ref, o_ref, g_ref, a_ref, og_ref, ag_ref, do_ref, dg_ref, da_ref, dd_ref, dog_ref, dag_ref):
        first = pl.program_id(0) == 0
        dmi = _dot_nt(dy_ref[...], w_ref[0])
        dog = jnp.zeros((1, HG_HEAD), F32)
        for h in range(HG_WIDTH // HG_HEAD):
            sl = slice(h * HG_HEAD, (h + 1) * HG_HEAD)
            oh = o_ref[:, sl].astype(F32)
            gv = g_ref[:, sl].astype(F32)
            dv = dmi[:, sl]
            r = _rms(oh)
            xh = oh * r
            sg = _sigmoid(gv)
            dno = dv * gv * sg
            dg_ref[:, sl] = (dv * xh * og_ref[...] * sg * (1.0 + gv * (1.0 - sg))).astype(BF16)
            dog = dog + jnp.sum(dno * xh, axis=0, keepdims=True)
            do_ref[:, sl] = _rms_bwd(dno * og_ref[...], xh, r).astype(BF16)
        _acc(dog_ref, dog, first)
        av = a_ref[...]
        dav = dmi[:, HG_WIDTH:]
        r = _rms(av)
        xa = av * r
        _acc(dag_ref, jnp.sum(dav * xa, axis=0, keepdims=True), first)
        datt = _rms_bwd(dav * ag_ref[...], xa, r)
        da_ref[...] = datt.astype(BF16)
        prod = datt * av
        lane = lax.broadcasted_iota(jnp.int32, (1, 128), 1)
        dd = jnp.zeros((tr, 128), F32)
        for hp in range(ATT_HEADS // 2):
            pp = prod[:, hp * 128:(hp + 1) * 128]
            lo = jnp.sum(jnp.where(lane < 64, pp, 0.0), axis=-1, keepdims=True)
            hi = jnp.sum(jnp.where(lane >= 64, pp, 0.0), axis=-1, keepdims=True)
            dd = jnp.where(lane == 2 * hp, lo, dd)
            dd = jnp.where(lane == 2 * hp + 1, hi, dd)
        dd_ref[...] = dd

    half = _rows(tr, HG_WIDTH)
    return _call(body, name=name, grid=(t // tr,),
                 in_specs=[_rows(tr, D_MODEL), pl.BlockSpec(w_out.shape, lambda i: (0, 0, 0)), half,
                           pl.BlockSpec((tr, HG_WIDTH), lambda i: (i, 3)), half, _vec(HG_HEAD), _vec(ATT_WIDTH)],
                 out_specs=[half, half, half, _rows(tr, 128), _vec(HG_HEAD), _vec(ATT_WIDTH)],
                 out_shape=[_sds((t, HG_WIDTH), BF16)] * 3 + [_sds((t, 128)), _sds((1, HG_HEAD)), _sds((1, ATT_WIDTH))])(
                     dmix, w_out, o_hg, proj, att, og, ag)


def _dproj(dhg, dg, dqs, dkvs, name, tr=1024):
    t = dhg.shape[0]
    w3 = 3 * HG_WIDTH
    w4 = w3 + HG_WIDTH
    nbr = len(dqs)

    def body(*refs):
        h_ref, g_ref, q_refs, kv_refs, o_ref = refs[0], refs[1], refs[2:2 + nbr], refs[2 + nbr:2 + 2 * nbr], refs[-1]
        o_ref[:, :w3] = h_ref[...]
        o_ref[:, w3:w4] = g_ref[...].astype(BF16)
        o_ref[:, w4:w4 + ATT_WIDTH] = sum(r[...].astype(F32) for r in q_refs).astype(BF16)
        o_ref[:, w4 + ATT_WIDTH:] = sum(r[...].astype(F32) for r in kv_refs).astype(BF16)

    return _call(body, name=name, grid=(t // tr,),
                 in_specs=[_rows(tr, w3), _rows(tr, HG_WIDTH)] + [_rows(tr, ATT_WIDTH)] * nbr
                 + [_rows(tr, 2 * ATT_WIDTH)] * nbr,
                 out_specs=_rows(tr, IN_WIDTH), out_shape=_sds((t, IN_WIDTH), BF16))(dhg, dg, *dqs, *dkvs)


def _chunk_tri(upper):
    row = lax.broadcasted_iota(jnp.int32, (HG_GROUP, HG_CHUNK, HG_CHUNK), 1)
    col = lax.broadcasted_iota(jnp.int32, (HG_GROUP, HG_CHUNK, HG_CHUNK), 2)
    return (row <= col if upper else row >= col).astype(BF16)


def _chunk_cumsum(x, tri):
    x3 = x.reshape(HG_GROUP, HG_CHUNK, x.shape[1])
    dims = (((2,), (1,)), ((0,), (0,)))
    out = None
    for _ in range(3):
        part = x3.astype(BF16)
        x3 = x3 - part.astype(F32)
        term = lax.dot_general(tri, part, dims, preferred_element_type=F32)
        out = term if out is None else out + term
    return out.reshape(x.shape)


def _hg_gates(f_raw, q_raw, lb, tri):
    sg = _sigmoid(f_raw)
    f = lb + (1.0 - lb) * sg
    k = 1.0 - f
    b = _chunk_cumsum(jnp.log(f), tri)
    sq = _sigmoid(q_raw)
    return sg, f, k, b, sq


def _hg_masks(rows):
    row = lax.broadcasted_iota(jnp.int32, (rows, rows), 0)
    col = lax.broadcasted_iota(jnp.int32, (rows, rows), 1)
    same = (row // HG_CHUNK) == (col // HG_CHUNK)
    return jnp.logical_and(row >= col, same), jnp.logical_and(row <= col, same)


def _per_chunk(rows_of):
    return jnp.concatenate([jnp.broadcast_to(r, (HG_CHUNK, r.shape[1])) for r in rows_of], axis=0)


def _hgrn_fwd(proj, lb_logits, name):
    t = proj.shape[0]
    nc = t // HG_CHUNK
    nh = HG_WIDTH // HG_HEAD
    rows = HG_GROUP * HG_CHUNK

    def body(q_ref, f_ref, i_ref, lg_ref, o_ref, st_ref, s_scr):
        @pl.when(pl.program_id(0) == 0)
        def _():
            s_scr[...] = jnp.zeros_like(s_scr)

        lg = lg_ref[...]
        lb_all = _sigmoid(lg[0:1] - lg[1:2])
        causal, _ = _hg_masks(rows)
        tri = _chunk_tri(False)
        for h in range(nh):
            sl = slice(h * HG_HEAD, (h + 1) * HG_HEAD)
            q_raw = q_ref[:, sl].astype(F32)
            _, _, k, b, sq = _hg_gates(f_ref[:, sl].astype(F32), q_raw, lb_all[:, sl], tri)
            v = i_ref[:, sl].astype(BF16)
            gls = [b[(g + 1) * HG_CHUNK - 1:(g + 1) * HG_CHUNK] for g in range(HG_GROUP)]
            bm = _per_chunk([b[g * HG_CHUNK + HG_CHUNK // 2 - 1:g * HG_CHUNK + HG_CHUNK // 2] for g in range(HG_GROUP)])
            qd = (q_raw * sq * jnp.exp(b)).astype(BF16)
            qm = (q_raw * sq * jnp.exp(b - bm)).astype(BF16)
            km = (k * jnp.exp(bm - b)).astype(BF16)
            ke = (k * jnp.exp(_per_chunk(gls) - b)).astype(BF16)
            a = jnp.where(causal, _dot_nt(qm, km), 0.0).astype(BF16)
            o_intra = _dot(a, v)
            st = s_scr[h]
            o_inter = []
            for g in range(HG_GROUP):
                rs = slice(g * HG_CHUNK, (g + 1) * HG_CHUNK)
                st_ref[g, sl, :] = st
                o_inter.append(_dot_nt(qd[rs], st.astype(BF16)))
                st = st * jnp.exp(gls[g]) + _dot_tn(v[rs], ke[rs])
            s_scr[h] = st
            o_ref[:, sl] = (o_intra + jnp.concatenate(o_inter, axis=0)).astype(BF16)

    blk = lambda j: pl.BlockSpec((rows, HG_WIDTH), lambda c: (c, j))
    return _call(body, name=name, grid=(nc // HG_GROUP,),
                 in_specs=[blk(0), blk(1), blk(2), pl.BlockSpec((2, HG_WIDTH), lambda c: (0, 0))],
                 out_specs=[blk(0), pl.BlockSpec((HG_GROUP, HG_WIDTH, HG_HEAD), lambda c: (c, 0, 0))],
                 out_shape=[_sds((t, HG_WIDTH), BF16), _sds((nc, HG_WIDTH, HG_HEAD))],
                 scratch_shapes=[pltpu.VMEM((nh, HG_HEAD, HG_HEAD), F32)])(proj, proj, proj, lb_logits)


def _hgrn_bwd(proj, lb_logits, states, do, name):
    t = proj.shape[0]
    ng = t // (HG_GROUP * HG_CHUNK)
    nh = HG_WIDTH // HG_HEAD
    rows = HG_GROUP * HG_CHUNK

    def body(q_ref, f_ref, i_ref, lg_ref, st_ref, do_ref, d_ref, dlb_ref, ds_scr):
        first = pl.program_id(0) == 0

        @pl.when(first)
        def _():
            ds_scr[...] = jnp.zeros_like(ds_scr)

        lg = lg_ref[...]
        lb_all = _sigmoid(lg[0:1] - lg[1:2])
        causal, _ = _hg_masks(rows)
        tri = _chunk_tri(False)
        tri_t = _chunk_tri(True)
        dlb = []
        for h in range(nh):
            sl = slice(h * HG_HEAD, (h + 1) * HG_HEAD)
            q_raw = q_ref[:, sl].astype(F32)
            lb = lb_all[:, sl]
            sg, f, k, b, sq = _hg_gates(f_ref[:, sl].astype(F32), q_raw, lb, tri)
            v = i_ref[:, sl].astype(BF16)
            gls = [b[(g + 1) * HG_CHUNK - 1:(g + 1) * HG_CHUNK] for g in range(HG_GROUP)]
            bm = _per_chunk([b[g * HG_CHUNK + HG_CHUNK // 2 - 1:g * HG_CHUNK + HG_CHUNK // 2] for g in range(HG_GROUP)])
            eb = jnp.exp(b)
            ebm = jnp.exp(b - bm)
            emb = jnp.exp(bm - b)
            egb = jnp.exp(_per_chunk(gls) - b)
            ke = k * egb
            qd_b, qm_b = (q_raw * sq * eb).astype(BF16), (q_raw * sq * ebm).astype(BF16)
            km_b, ke_b = (k * emb).astype(BF16), ke.astype(BF16)
            dov = do_ref[:, sl].astype(BF16)
            a = jnp.where(causal, _dot_nt(qm_b, km_b), 0.0).astype(BF16)
            da = jnp.where(causal, _dot_nt(dov, v), 0.0).astype(BF16)
            dkm = _dot_tn(da, qm_b)
            dst = ds_scr[h]
            dqd_s, dv_s, dke_s, dgl_s = [None] * HG_GROUP, [None] * HG_GROUP, [None] * HG_GROUP, [None] * HG_GROUP
            for g in reversed(range(HG_GROUP)):
                rs = slice(g * HG_CHUNK, (g + 1) * HG_CHUNK)
                st = st_ref[g, sl, :]
                dst_b = dst.astype(BF16)
                egl = jnp.exp(gls[g])
                dqd_s[g] = _dot(dov[rs], st.astype(BF16))
                dv_s[g] = _dot_nt(ke_b[rs], dst_b)
                dke_s[g] = _dot(v[rs], dst_b)
                dgl_s[g] = jnp.sum(dst * st, axis=0, keepdims=True) * egl
                dst = _dot_tn(dov[rs], qd_b[rs]) + dst * egl
            ds_scr[h] = dst
            dqm = _dot(da, km_b)
            dqd = jnp.concatenate(dqd_s, axis=0)
            dv = _dot_tn(a, dov) + jnp.concatenate(dv_s, axis=0)
            dke = jnp.concatenate(dke_s, axis=0)
            t1 = dke * ke
            db = dqm * qm_b.astype(F32) - dkm * km_b.astype(F32) + dqd * qd_b.astype(F32) - t1
            dgl = _per_chunk([dgl_s[g] + jnp.sum(t1[g * HG_CHUNK:(g + 1) * HG_CHUNK], axis=0, keepdims=True)
                              for g in range(HG_GROUP)])
            dlf = _chunk_cumsum(db, tri_t) + dgl
            df = dlf / f - (dkm * emb + dke * egb)
            d_ref[:, sl] = ((dqm * ebm + dqd * eb) * sq * (1.0 + q_raw * (1.0 - sq))).astype(BF16)
            d_ref[:, HG_WIDTH + h * HG_HEAD:HG_WIDTH + (h + 1) * HG_HEAD] = (
                df * (1.0 - lb) * sg * (1.0 - sg)).astype(BF16)
            d_ref[:, 2 * HG_WIDTH + h * HG_HEAD:2 * HG_WIDTH + (h + 1) * HG_HEAD] = dv.astype(BF16)
            dlb.append(jnp.sum(df * (1.0 - sg), axis=0, keepdims=True))
        _acc(dlb_ref, jnp.concatenate(dlb, axis=1), first)

    rev = lambda j: pl.BlockSpec((rows, HG_WIDTH), lambda c: (ng - 1 - c, j))
    return _call(body, name=name, grid=(ng,),
                 in_specs=[rev(0), rev(1), rev(2), pl.BlockSpec((2, HG_WIDTH), lambda c: (0, 0)),
                           pl.BlockSpec((HG_GROUP, HG_WIDTH, HG_HEAD), lambda c: (ng - 1 - c, 0, 0)), rev(0)],
                 out_specs=[pl.BlockSpec((rows, 3 * HG_WIDTH), lambda c: (ng - 1 - c, 0)), _vec(HG_WIDTH)],
                 out_shape=[_sds((t, 3 * HG_WIDTH), BF16), _sds((1, HG_WIDTH))],
                 scratch_shapes=[pltpu.VMEM((nh, HG_HEAD, HG_HEAD), F32)])(proj, proj, proj, lb_logits, states, do)


def _to_sub(a, dil):
    t, w = a.shape
    return a if dil == 1 else a.reshape(t // dil, dil, w).transpose(1, 0, 2).reshape(t, w)


def _from_sub(a, dil):
    t, w = a.shape
    return a if dil == 1 else a.reshape(dil, t // dil, w).transpose(1, 0, 2).reshape(t, w)


def _att_mask(has_prev, seg):
    def place(v):
        v = v % ATT_BLOCK
        return v if seg == 1 else seg * (v % (ATT_BLOCK // seg)) + v // (ATT_BLOCK // seg)

    row = lax.broadcasted_iota(jnp.int32, (2 * ATT_BLOCK, 2 * ATT_BLOCK), 0)
    col = lax.broadcasted_iota(jnp.int32, (2 * ATT_BLOCK, 2 * ATT_BLOCK), 1)
    qi, kj = place(row), place(col)
    prev = jnp.logical_and(jnp.logical_and(col < ATT_BLOCK, kj >= qi), has_prev)
    cur = jnp.logical_and(col >= ATT_BLOCK, kj <= qi)
    return jnp.logical_or(prev, cur), lax.broadcasted_iota(jnp.int32, (1, 128), 1)


def _get(ref, sl):
    if len(ref.shape) == 2:
        return ref[:, sl]
    v = ref[:, :, sl]
    return v.reshape(ATT_BLOCK, v.shape[2])


def _put(ref, sl, val):
    if len(ref.shape) == 2:
        ref[:, sl] = val
    else:
        ref[:, :, sl] = val.reshape(ref.shape[0], ref.shape[1], val.shape[1])


def _att_spec(nb, dil, seg, width, col, back):
    bps = nb // dil

    def plain(n):
        return jnp.clip(n - back, 0, nb - 1), col

    def segmented(n):
        m = jnp.clip(n - back, 0, nb - 1)
        return 0, m // bps, m % bps, 0, col

    if seg == 1:
        return pl.BlockSpec((ATT_BLOCK, width), plain)
    return pl.BlockSpec((seg, None, None, ATT_BLOCK // seg, width), segmented)


def _att_shape(nb, dil, seg, width):
    t = nb * ATT_BLOCK
    return (t, width) if seg == 1 else (seg, dil, nb // dil, ATT_BLOCK // seg, width)


def _att_view(a, nb, dil, seg):
    return a.reshape(_att_shape(nb, dil, seg, a.shape[1]))


def _attn_fwd_block(q_ref, kc_ref, kp_ref, vc_ref, vp_ref, o_ref, l_ref, has_prev, seg):
    mask, lane = _att_mask(has_prev, seg)
    lo = lane < 64
    nq = ATT_BLOCK
    lse_all = jnp.zeros((nq, 128), F32)
    for hp in range(ATT_HEADS // 2):
        sl = slice(hp * 128, (hp + 1) * 128)
        q2 = _get(q_ref, sl)
        zero = jnp.zeros_like(q2)
        q2 = q2 * 0.125
        qs = jnp.concatenate([jnp.where(lo, q2, zero), jnp.where(lo, zero, q2)], axis=0)
        kk = jnp.concatenate([_get(kp_ref, sl), _get(kc_ref, sl)], axis=0)
        vv = jnp.concatenate([_get(vp_ref, sl), _get(vc_ref, sl)], axis=0)
        s = jnp.where(mask, _dot_nt(qs, kk), NEG)
        mx = jnp.max(s, axis=-1, keepdims=True)
        p = jnp.exp(s - mx)
        l = jnp.sum(p, axis=-1, keepdims=True)
        o = _dot(p.astype(BF16), vv) * (1.0 / l)
        _put(o_ref, sl, jnp.where(lo, o[:nq], o[nq:]).astype(BF16))
        lse = mx + jnp.log(l)
        lse_all = jnp.where(lane == 2 * hp, lse[:nq], lse_all)
        lse_all = jnp.where(lane == 2 * hp + 1, lse[nq:], lse_all)
    _put(l_ref, slice(None), lse_all)


def _attn_fwd(branches, name):
    t = branches[0][0].shape[0]
    nb = t // ATT_BLOCK
    nbr = len(branches)

    def body(*refs):
        n = pl.program_id(0)
        for i, (_, dil, seg) in enumerate(branches):
            _attn_fwd_block(*refs[5 * i:5 * i + 5], *refs[5 * nbr + 2 * i:5 * nbr + 2 * i + 2],
                            (n % (nb // dil)) != 0, seg)

    in_specs, args, out_specs, out_shape = [], [], [], []
    for qkv, dil, seg in branches:
        c0 = qkv.shape[1] // ATT_WIDTH - 3
        in_specs += [_att_spec(nb, dil, seg, ATT_WIDTH, c0 + j, back) for j, back in [(0, 0), (1, 0), (1, 1), (2, 0), (2, 1)]]
        args += [_att_view(qkv, nb, dil, seg)] * 5
        out_specs += [_att_spec(nb, dil, seg, ATT_WIDTH, 0, 0), _att_spec(nb, dil, seg, 128, 0, 0)]
        out_shape += [_sds(_att_shape(nb, dil, seg, ATT_WIDTH), BF16), _sds(_att_shape(nb, dil, seg, 128))]
    out = _call(body, name=name, grid=(nb,), in_specs=in_specs, out_specs=out_specs, out_shape=out_shape)(*args)
    return [(out[2 * i].reshape(t, ATT_WIDTH), out[2 * i + 1].reshape(t, 128)) for i in range(nbr)]


def _combine_mix_in(os_, ls_, o_hg, proj, og, ag, name, tr=512):
    t = os_[0].shape[0]
    nbr = len(os_)

    def body(*refs):
        o_refs, l_refs = refs[:nbr], refs[nbr:2 * nbr]
        oh_ref, g_ref, og_ref, ag_ref, a_ref, lt_ref, m_ref = refs[2 * nbr:]
        lane = lax.broadcasted_iota(jnp.int32, (1, 128), 1)
        ls = [r[...] for r in l_refs]
        mx = functools.reduce(jnp.maximum, ls)
        tot = mx + jnp.log(sum(jnp.exp(l - mx) for l in ls))
        lt_ref[...] = tot
        ws = [jnp.exp(l - tot) for l in ls]
        pairs = []
        for hp in range(ATT_HEADS // 2):
            sl = slice(hp * 128, (hp + 1) * 128)
            acc = jnp.zeros((tr, 128), F32)
            for w, o_ref in zip(ws, o_refs):
                wf = jnp.where(lane < 64, w[:, 2 * hp:2 * hp + 1], w[:, 2 * hp + 1:2 * hp + 2])
                acc = acc + wf * o_ref[:, sl]
            pairs.append(acc)
        av = jnp.concatenate(pairs, axis=1)
        a_ref[...] = av
        m_ref[:, HG_WIDTH:] = (av * _rms(av) * ag_ref[...]).astype(BF16)
        for h in range(HG_WIDTH // HG_HEAD):
            sl = slice(h * HG_HEAD, (h + 1) * HG_HEAD)
            oh = oh_ref[:, sl].astype(F32)
            gv = g_ref[:, sl].astype(F32)
            m_ref[:, sl] = (oh * _rms(oh) * og_ref[...] * (gv * _sigmoid(gv))).astype(BF16)

    half = _rows(tr, ATT_WIDTH)
    return _call(body, name=name, grid=(t // tr,),
                 in_specs=[half] * nbr + [_rows(tr, 128)] * nbr
                 + [half, pl.BlockSpec((tr, HG_WIDTH), lambda i: (i, 3)), _vec(HG_HEAD), _vec(ATT_WIDTH)],
                 out_specs=[half, _rows(tr, 128), _rows(tr, D_MODEL)],
                 out_shape=[_sds((t, ATT_WIDTH)), _sds((t, 128)), _sds((t, D_MODEL), BF16)])(
                     *os_, *ls_, o_hg, proj, og, ag)


def _attn_bwd_block(q_ref, kc_ref, kp_ref, vc_ref, vp_ref, do_ref, l_ref, d_ref, dq_ref, dkv_ref, carry, has_prev, seg):
    w = ATT_WIDTH
    nq = ATT_BLOCK
    mask, lane = _att_mask(has_prev, seg)
    lo = lane < 64
    lse, ddv = _get(l_ref, slice(None)), _get(d_ref, slice(None))
    for hp in range(ATT_HEADS // 2):
        sl = slice(hp * 128, (hp + 1) * 128)
        sv = slice(w + hp * 128, w + (hp + 1) * 128)
        q2, do2 = _get(q_ref, sl), _get(do_ref, sl)
        zero = jnp.zeros_like(q2)
        q2 = q2 * 0.125
        qs = jnp.concatenate([jnp.where(lo, q2, zero), jnp.where(lo, zero, q2)], axis=0)
        dos = jnp.concatenate([jnp.where(lo, do2, zero), jnp.where(lo, zero, do2)], axis=0)
        kk = jnp.concatenate([_get(kp_ref, sl), _get(kc_ref, sl)], axis=0)
        vv = jnp.concatenate([_get(vp_ref, sl), _get(vc_ref, sl)], axis=0)
        ls = jnp.concatenate([lse[:, 2 * hp:2 * hp + 1], lse[:, 2 * hp + 1:2 * hp + 2]], axis=0)
        dh = jnp.concatenate([ddv[:, 2 * hp:2 * hp + 1], ddv[:, 2 * hp + 1:2 * hp + 2]], axis=0)
        p = jnp.exp(jnp.where(mask, _dot_nt(qs, kk) - ls, NEG))
        ds = (p * (_dot_nt(dos, vv) - dh)).astype(BF16)
        dq = _dot(ds, kk) * 0.125
        _put(dq_ref, sl, jnp.where(lo, dq[:nq], dq[nq:]).astype(BF16))
        dk = _dot_tn(ds, qs)
        dv = _dot_tn(p.astype(BF16), dos)
        _put(dkv_ref, sl, (carry[:, sl] + dk[:nq]).astype(BF16))
        _put(dkv_ref, sv, (carry[:, sv] + dv[:nq]).astype(BF16))
        carry[:, sl] = dk[nq:]
        carry[:, sv] = dv[nq:]


def _attn_bwd(branches, name):
    t = branches[0][0].shape[0]
    nb = t // ATT_BLOCK
    nbr = len(branches)
    w = ATT_WIDTH

    def body(*refs):
        ins, outs, carries = refs[:8 * nbr], refs[8 * nbr:10 * nbr], refs[10 * nbr:]
        n = pl.program_id(0)

        @pl.when(n == 0)
        def _():
            for carry in carries:
                carry[...] = jnp.zeros_like(carry)

        @pl.when(n < nb)
        def _():
            for i, branch in enumerate(branches):
                dil, seg = branch[4:]
                _attn_bwd_block(*ins[8 * i:8 * i + 8], *outs[2 * i:2 * i + 2], carries[i], (n % (nb // dil)) != 0, seg)

        @pl.when(n == nb)
        def _():
            for i in range(nbr):
                _put(outs[2 * i + 1], slice(None), carries[i][...].astype(BF16))

    in_specs, args, out_specs, out_shape = [], [], [], []
    for qkv, dout, lse, dd, dil, seg in branches:
        c0 = qkv.shape[1] // w - 3
        in_specs += [_att_spec(nb, dil, seg, w, c0 + j, back) for j, back in [(0, 0), (1, 0), (1, 1), (2, 0), (2, 1)]]
        in_specs += [_att_spec(nb, dil, seg, w, 0, 0), _att_spec(nb, dil, seg, 128, 0, 0), _att_spec(nb, dil, seg, 128, 0, 0)]
        args += [_att_view(a, nb, dil, seg) for a in [qkv] * 5 + [dout, lse, dd]]
        out_specs += [_att_spec(nb, dil, seg, w, 0, 0), _att_spec(nb, dil, seg, 2 * w, 0, 1)]
        out_shape += [_sds(_att_shape(nb, dil, seg, w), BF16), _sds(_att_shape(nb, dil, seg, 2 * w), BF16)]
    out = _call(body, name=name, grid=(nb + 1,), in_specs=in_specs, out_specs=out_specs, out_shape=out_shape,
                scratch_shapes=[pltpu.VMEM((ATT_BLOCK, 2 * w), F32)] * nbr)(*args)
    return [(out[2 * i].reshape(t, w), out[2 * i + 1].reshape(t, 2 * w)) for i in range(nbr)]


def _local_step(x, tgt, mod, norm1_g, lb_logits, og, ag, norm2_g, fg, get_w, put_g, late=lambda a: a, project=None):
    shift1, scale1, gate1, shift2, scale2, gate2 = [mod[:, i * D_MODEL:(i + 1) * D_MODEL] for i in range(6)]
    fg = fg.reshape(1, D_MODEL)

    h1 = _norm_mod(x, norm1_g, scale1, shift1, "norm_mod1")
    if project is None:
        w_in = get_w("w_in", h1)
        proj = _mm_nn(h1, w_in, "mm_in", out_dtype=BF16)
    else:
        proj, w_in = project(h1)
    o_hg, states = _hgrn_fwd(proj, lb_logits, "hgrn_fwd")
    fine = DILATIONS[-1]
    layouts = [(d, 1 if d == 1 else fine // d) for d in DILATIONS]
    qkv_fine = _to_sub(proj[:, 4 * HG_WIDTH:], fine)
    qkvs = [proj if d == 1 else qkv_fine for d in DILATIONS]
    natural = lambda a, d: a if d == 1 else _from_sub(a, fine)
    outs = _attn_fwd([(q, d, seg) for q, (d, seg) in zip(qkvs, layouts)], "attn_fwd")
    att, lse, mixin = _combine_mix_in([natural(o, d) for (o, _), d in zip(outs, DILATIONS)],
                                      [natural(l, d) for (_, l), d in zip(outs, DILATIONS)],
                                      o_hg, proj, og, ag, "attn_combine_mix_in")
    w_out = get_w("w_out", mixin)
    mix, x2, h2 = _out_resid_norm_mod(x, mixin, w_out, gate1, norm2_g, scale2, shift2, "mm_out_resid_norm_mod2")
    w_gu = get_w("w_gu", h2)
    a_ff, u_ff, act = _mm_gate_up(h2, w_gu, "mm_gu")
    w_down = get_w("w_down", act)
    dx3, dffn, loss_v, dfg, dgate2 = _down_loss(x2, act, w_down, gate2, fg, tgt, "mm_down_loss")

    dffn = put_g("w_down", *_mm_tn(act, dffn, 1, "mm_down_dw", tm=2048, tk=D_FF // 2), dffn)
    dau = _mm_down_dx(dffn, w_down, a_ff, u_ff, "mm_down_dx")
    dau = put_g("w_gu", *_mm_tn(h2, dau, N_SHARD, "mm_gu_dw", tm=x.shape[0], tk=512), dau)
    dx2, dshift2, dscale2, dg2, dgate1, dmix = _norm_mod_bwd(
        dau, x2, norm2_g, scale2, dx3, "mm_gu_dx_norm_bwd", gate=gate1, mix=mix, w=w_gu)
    dmix = put_g("w_out", *_mm_tn(mixin, dmix, 1, "mm_out_dw", tm=x.shape[0], tk=512), dmix)
    do_hg, dg_raw, datt, dd, dog, dag = _mix_in_bwd(dmix, w_out, o_hg, proj, att, og, ag, "mm_out_dx_mix_in_bwd")
    datt_b = datt
    reordered = [_to_sub(a, fine) for a in (datt_b, lse, dd)]
    datts = _attn_bwd([(q,) + tuple((datt_b, lse, dd) if d == 1 else reordered) + (d, seg)
                       for q, (d, seg) in zip(qkvs, layouts)], "attn_bwd")
    dhg, dlb = _hgrn_bwd(proj, lb_logits, states, do_hg, "hgrn_bwd")
    dhg = late(dhg)
    dproj = _dproj(dhg, dg_raw, [natural(dq, d) for (dq, _), d in zip(datts, DILATIONS)],
                   [natural(dkv, d) for (_, dkv), d in zip(datts, DILATIONS)], "dproj")
    dproj = put_g("w_in", *_mm_tn(h1, dproj, N_SHARD, "mm_in_dw", tm=x.shape[0], tk=512), dproj)
    dx, dshift1, dscale1, dg1 = _norm_mod_bwd(dproj, x, norm1_g, scale1, dx2, "mm_in_dx_norm_bwd", w=w_in)

    stats = jnp.concatenate([loss_v, dfg, dg2, dg1, dlb, dag, dog,
                             dshift1, dscale1, dgate1, dshift2, dscale2, dgate2], axis=1)
    return dx, stats


def _place():
    x, y, c = lax.axis_index("x"), lax.axis_index("y"), lax.axis_index("c")
    return x, y, c


def _chip_peers(x, y, c):
    return [(1 - x, y, c), (x, 1 - y, c), (1 - x, 1 - y, c)]


def _comm_call(body, name, n_in, out_shape, scratch_shapes):
    hbm = pl.BlockSpec(memory_space=pl.ANY)
    return pl.pallas_call(body, name=name, in_specs=[hbm] * n_in, out_specs=[hbm] * len(out_shape),
                          out_shape=out_shape, scratch_shapes=scratch_shapes)


_HBM = pl.BlockSpec(memory_space=pltpu.HBM)
_SEM = pl.BlockSpec(memory_space=pltpu.SEMAPHORE)
_EFFECT = pltpu.SideEffectType.DATAFLOW_SIDE_EFFECTING


def _exchange_copy(bufs, send, recv, j, peer, place, kind):
    x, y, c = place
    target = peer
    if kind == "gather":
        src = dst = bufs[0].at[2 * x + y]
    elif kind == "scatter":
        src, dst = bufs[0].at[2 * peer[0] + peer[1]], bufs[1].at[j]
    else:
        half = bufs[0].shape[1] // 2
        rows = pl.ds(c * half, half)
        if kind == "half":
            src = dst = bufs[0].at[2 * x + y, rows]
        else:
            src = dst = bufs[0].at[2 * peer[0] + peer[1], rows]
            target = (x, y, 1 - c)
    return pltpu.make_async_remote_copy(src_ref=src, dst_ref=dst, send_sem=send.at[j], recv_sem=recv.at[j],
                                        device_id=target, device_id_type=MESH)


def _exchange_start(groups, after, kind, name):
    sizes = [len(g) for g in groups]
    flat = [b for g in groups for b in g]
    ng, nb = len(groups), len(flat)

    def body(*refs):
        bufs, sems = refs[:nb], refs[nb + 1:nb + 1 + 2 * ng]
        x, y, c = _place()
        for j, peer in enumerate(_chip_peers(x, y, c)):
            at = 0
            for i, size in enumerate(sizes):
                _exchange_copy(bufs[at:at + size], sems[2 * i], sems[2 * i + 1], j, peer, (x, y, c), kind).start()
                at += size

    any_space = pl.BlockSpec(memory_space=pl.ANY)
    out = pl.pallas_call(
        body, name=name, in_specs=[_HBM] * nb + [any_space],
        out_specs=[_SEM] * (2 * ng) + [_HBM] * nb + [any_space],
        out_shape=[pltpu.SemaphoreType.DMA((3,))] * (2 * ng) + [pltpu.HBM(b.shape, b.dtype) for b in flat]
        + [_sds(after.shape, after.dtype)],
        input_output_aliases={i: 2 * ng + i for i in range(nb + 1)},
        compiler_params=pltpu.CompilerParams(has_side_effects=_EFFECT),
    )(*[pltpu.with_memory_space_constraint(b, pltpu.HBM) for b in flat], after)
    started, at = [], 2 * ng
    for i, size in enumerate(sizes):
        started.append((out[2 * i], out[2 * i + 1], tuple(out[at:at + size])))
        at += size
    return started, out[-1]


def _exchange_wait(started, after, kind, name):
    send, recv, bufs = started
    nb = len(bufs)

    def body(*refs):
        x, y, c = _place()
        for j, peer in enumerate(_chip_peers(x, y, c)):
            cp = _exchange_copy(refs[:nb], refs[nb], refs[nb + 1], j, peer, (x, y, c), kind)
            cp.wait_send()
            cp.wait_recv()

    return pl.pallas_call(
        body, name=name, in_specs=[_HBM] * nb + [_SEM, _SEM, pl.BlockSpec(memory_space=pl.ANY)],
        out_specs=[_HBM] * nb, out_shape=[pltpu.HBM(b.shape, b.dtype) for b in bufs],
        input_output_aliases={i: i for i in range(nb)},
        compiler_params=pltpu.CompilerParams(has_side_effects=_EFFECT),
    )(*bufs, send, recv, after)


def _sibling_copies(v_refs, l_refs, send, recv):
    x, y, c = _place()
    return [pltpu.make_async_remote_copy(src_ref=v, dst_ref=l, send_sem=send.at[a], recv_sem=recv.at[a],
                                         device_id=(x, y, 1 - c), device_id_type=MESH)
            for a, (v, l) in enumerate(zip(v_refs, l_refs))]


def _sibling_start(vs, after, name):
    vs = list(vs)
    n = len(vs)
    lands = [lax.empty(v.shape, v.dtype) for v in vs]

    def body(*refs):
        for cp in _sibling_copies(refs[:n], refs[n:2 * n], refs[2 * n + 1], refs[2 * n + 2]):
            cp.start()

    any_space = pl.BlockSpec(memory_space=pl.ANY)
    out = pl.pallas_call(
        body, name=name, in_specs=[_HBM] * (2 * n) + [any_space],
        out_specs=[_SEM, _SEM] + [_HBM] * (2 * n) + [any_space],
        out_shape=[pltpu.SemaphoreType.DMA((n,))] * 2 + [pltpu.HBM(b.shape, b.dtype) for b in vs + lands]
        + [_sds(after.shape, after.dtype)],
        input_output_aliases={i: 2 + i for i in range(2 * n + 1)},
        compiler_params=pltpu.CompilerParams(has_side_effects=_EFFECT),
    )(*[pltpu.with_memory_space_constraint(b, pltpu.HBM) for b in vs + lands], after)
    return (out[0], out[1], tuple(out[2:2 + n]), tuple(out[2 + n:2 + 2 * n])), out[-1]


def _sibling_wait(started, after, name):
    send, recv, vs, lands = started
    n = len(vs)

    def body(*refs):
        for cp in _sibling_copies(refs[:n], refs[n:2 * n], refs[2 * n], refs[2 * n + 1]):
            cp.wait_send()
            cp.wait_recv()

    out = pl.pallas_call(
        body, name=name, in_specs=[_HBM] * (2 * n) + [_SEM, _SEM, pl.BlockSpec(memory_space=pl.ANY)],
        out_specs=[_HBM] * (2 * n), out_shape=[pltpu.HBM(b.shape, b.dtype) for b in vs + lands],
        input_output_aliases={i: i for i in range(2 * n)},
        compiler_params=pltpu.CompilerParams(has_side_effects=_EFFECT),
    )(*vs, *lands, send, recv, after)
    return out[:n], out[n:]


def _swap_sibling(vs, name):
    n = len(vs)

    def body(*refs):
        v_refs, o_refs, (send, recv) = refs[:n], refs[n:2 * n], refs[2 * n:]
        x, y, c = _place()
        cps = [pltpu.make_async_remote_copy(
            src_ref=v_refs[a], dst_ref=o_refs[a], send_sem=send.at[a], recv_sem=recv.at[a],
            device_id=(x, y, 1 - c), device_id_type=MESH) for a in range(n)]
        for cp in cps:
            cp.start()
        for cp in cps:
            cp.wait()

    return _comm_call(body, name, n, [_sds(v.shape, v.dtype) for v in vs],
                      [pltpu.SemaphoreType.DMA((n,)), pltpu.SemaphoreType.DMA((n,))])(*vs)


def _everyone(x, y, c):
    return [(1 - x if k & 4 else x, 1 - y if k & 2 else y, 1 - c if k & 1 else c) for k in range(1, 8)]


def _all_gather_copies(land_ref, send, recv, arriving):
    x, y, c = _place()
    me = 4 * x + 2 * y + c
    return [pltpu.make_async_remote_copy(
        src_ref=land_ref.at[me], dst_ref=land_ref.at[4 * p[0] + 2 * p[1] + p[2] if arriving else me],
        send_sem=send.at[k], recv_sem=recv.at[k], device_id=p, device_id_type=MESH)
        for k, p in enumerate(_everyone(x, y, c))]


def _all_gather_start(v, name):
    x, y, c = _place()
    land = lax.dynamic_update_slice(lax.empty((8,) + v.shape, v.dtype), v[None], (4 * x + 2 * y + c, 0, 0))

    def body(land_ref, v_ref, send, recv, land_out, v_out):
        for cp in _all_gather_copies(land_ref, send, recv, False):
            cp.start()

    any_space = pl.BlockSpec(memory_space=pl.ANY)
    out = pl.pallas_call(
        body, name=name, in_specs=[_HBM, any_space], out_specs=[_SEM, _SEM, _HBM, any_space],
        out_shape=[pltpu.SemaphoreType.DMA((7,))] * 2 + [pltpu.HBM(land.shape, land.dtype), _sds(v.shape, v.dtype)],
        input_output_aliases={0: 2, 1: 3}, compiler_params=pltpu.CompilerParams(has_side_effects=_EFFECT),
    )(pltpu.with_memory_space_constraint(land, pltpu.HBM), v)
    return tuple(out[:3]), out[3]


def _all_gather_wait(started, after, name):
    send, recv, land = started

    def body(land_ref, send, recv, after_ref, land_out):
        for cp in _all_gather_copies(land_ref, send, recv, True):
            cp.wait_send()
            cp.wait_recv()

    return pl.pallas_call(
        body, name=name, in_specs=[_HBM, _SEM, _SEM, pl.BlockSpec(memory_space=pl.ANY)], out_specs=_HBM,
        out_shape=pltpu.HBM(land.shape, land.dtype), input_output_aliases={0: 0},
        compiler_params=pltpu.CompilerParams(has_side_effects=_EFFECT),
    )(land, send, recv, after)


def _cast_place(ws, shard, name):
    n = len(ws)

    def body(s_ref, *refs):
        for w_ref, o_ref in zip(refs[:n], refs[n:]):
            o_ref[0] = w_ref[...].astype(BF16)

    return pl.pallas_call(
        body, name=name, out_shape=[_sds((N_SHARD,) + w.shape, BF16) for w in ws],
        grid_spec=pltpu.PrefetchScalarGridSpec(
            num_scalar_prefetch=1, grid=(4,),
            in_specs=[pl.BlockSpec((w.shape[0] // 4, w.shape[1]), lambda i, s: (i, 0)) for w in ws],
            out_specs=[pl.BlockSpec((1, w.shape[0] // 4, w.shape[1]), lambda i, s: (s[0], i, 0)) for w in ws]),
        compiler_params=pltpu.CompilerParams(dimension_semantics=("arbitrary",), vmem_limit_bytes=VMEM_LIMIT),
    )(shard.reshape(1).astype(jnp.int32), *ws)


def _mod_rows(c8, w_ada, b_ada, name):
    n = w_ada.shape[1]

    def gather(src_ref, dst_ref, send, recv, loc, base):
        x, y, c = _place()
        me = 4 * x + 2 * y + c
        own = pltpu.make_async_copy(src_ref, dst_ref.at[me], loc)
        own.start()
        peers = _everyone(x, y, c)
        sends = [pltpu.make_async_remote_copy(src_ref=src_ref, dst_ref=dst_ref.at[me], send_sem=send.at[base + k],
                                              recv_sem=recv.at[base + k], device_id=p, device_id_type=MESH)
                 for k, p in enumerate(peers)]
        for cp in sends:
            cp.start()
        for k, p in enumerate(peers):
            pltpu.make_async_remote_copy(src_ref=src_ref, dst_ref=dst_ref.at[4 * p[0] + 2 * p[1] + p[2]],
                                         send_sem=send.at[base + k], recv_sem=recv.at[base + k], device_id=p,
                                         device_id_type=MESH).wait_recv()
        for cp in sends:
            cp.wait_send()
        own.wait()

    def body(c_ref, w_ref, b_ref, a_ref, parts_ref, c_all, part, send, recv, loc):
        gather(c_ref, c_all, send, recv, loc.at[0], 0)
        cv = jnp.max(c_all[...], axis=1)
        ca = cv * _sigmoid(cv)
        a_ref[...] = ca
        part[...] = jnp.dot(ca, w_ref[...], precision=lax.Precision.HIGHEST, preferred_element_type=F32) + b_ref[...]
        gather(part, parts_ref, send, recv, loc.at[1], 7)

    vmem = pl.BlockSpec(memory_space=pltpu.VMEM)
    return pl.pallas_call(
        body, name=name, in_specs=[vmem] * 3, out_specs=[vmem, vmem],
        out_shape=[_sds((8, D_MODEL)), _sds((8, 8, n))],
        scratch_shapes=[pltpu.VMEM((8, 8, D_MODEL), F32), pltpu.VMEM((8, n), F32), pltpu.SemaphoreType.DMA((14,)),
                        pltpu.SemaphoreType.DMA((14,)), pltpu.SemaphoreType.DMA((2,))],
        compiler_params=pltpu.CompilerParams(vmem_limit_bytes=VMEM_LIMIT))(c8, w_ada, b_ada)


def _sum_received(gs, shard, lands, name):
    n = len(gs)

    def body(s_ref, *refs):
        for g_ref, l_ref, o_ref in zip(refs[:n], refs[n:2 * n], refs[2 * n:]):
            o_ref[...] = ((g_ref[0] + l_ref[0].astype(F32)) + l_ref[1].astype(F32)) + l_ref[2].astype(F32)

    quarter = lambda g: (g.shape[1] // 4, g.shape[2])
    return pl.pallas_call(
        body, name=name, out_shape=[_sds(g.shape[1:]) for g in gs],
        grid_spec=pltpu.PrefetchScalarGridSpec(
            num_scalar_prefetch=1, grid=(4,),
            in_specs=[pl.BlockSpec((1,) + quarter(g), lambda i, s: (s[0], i, 0)) for g in gs]
            + [pl.BlockSpec((3,) + quarter(g), lambda i, s: (0, i, 0)) for g in gs],
            out_specs=[pl.BlockSpec(quarter(g), lambda i, s: (i, 0)) for g in gs]),
        compiler_params=pltpu.CompilerParams(dimension_semantics=("arbitrary",), vmem_limit_bytes=VMEM_LIMIT),
    )(shard.reshape(1).astype(jnp.int32), *gs, *lands)


def _adamw_outer(w, ct, dm, m, v, name):
    k, n = w.shape
    tr = k // 4

    def body(w_ref, c_ref, d_ref, m_ref, v_ref, g_out, d_out, m_out, v_out):
        cv = c_ref[...]
        dv = d_ref[...]
        g = cv[:, 0:1] * dv[0:1, :]
        for i in range(1, 8):
            g = g + cv[:, i:i + 1] * dv[i:i + 1, :]
        g_out[...] = g
        d_out[...], m_out[...], v_out[...] = _adamw_math(w_ref[...], g, m_ref[...], v_ref[...])

    row = _rows(tr, n)
    return _call(body, name=name, grid=(4,),
                 in_specs=[row, _rows(tr, 8), pl.BlockSpec((8, n), lambda i: (0, 0)), row, row],
                 out_specs=[row] * 4, out_shape=[_sds((k, n))] * 4)(w, ct, dm, m, v)


def _adamw_math(w, g, m, v):
    m_new = ADAM_B1 * m + (1.0 - ADAM_B1) * g
    v_new = ADAM_B2 * v + (1.0 - ADAM_B2) * (g * g)
    m_hat = m_new / (1.0 - ADAM_B1 ** ADAM_STEP)
    v_hat = v_new / (1.0 - ADAM_B2 ** ADAM_STEP)
    return -ADAM_LR * (m_hat / (jnp.sqrt(v_hat) + ADAM_EPS) + ADAM_WD * w), m_new, v_new


def _small_update(stats, smalls, name):
    offsets = [ST_DMOD, ST_DG1, ST_DLB, ST_DOG, ST_DAG, ST_DG2, ST_DFG]
    lb_index = 2

    def body(*refs):
        s_ref, ins, l_ref, outs = refs[0], refs[1:22], refs[22], refs[23:]
        tot = s_ref[0:1, :]
        for i in range(1, 8):
            tot = tot + s_ref[i:i + 1, :]
        l_ref[...] = jnp.zeros((1, 128), F32) + (0.5 / D_MODEL) * jnp.sum(tot[:, ST_LOSS:ST_LOSS + D_MODEL])
        for p, off in enumerate(offsets):
            w_ref, m_ref, v_ref = ins[3 * p:3 * p + 3]
            g_out, d_out, m_out, v_out = outs[4 * p:4 * p + 4]
            g = tot[:, off:off + w_ref.shape[1]]
            if p == lb_index:
                lg = w_ref[...]
                lb = _sigmoid(lg[0:1] - lg[1:2])
                g = g * lb * (1.0 - lb)
            for r in range(w_ref.shape[0]):
                rows = slice(r, r + 1)
                gr = g if r == 0 else -g
                delta, m_new, v_new = _adamw_math(w_ref[rows, :], gr, m_ref[rows, :], v_ref[rows, :])
                g_out[rows, :] = gr
                d_out[rows, :] = delta
                m_out[rows, :] = m_new
                v_out[rows, :] = v_new

    full = lambda a: pl.BlockSpec(a.shape, lambda i: (0, 0))
    flat = [a for t in smalls for a in t]
    return _call(body, name=name, grid=(1,),
                 in_specs=[full(stats)] + [full(a) for a in flat],
                 out_specs=[pl.BlockSpec((1, 128), lambda i: (0, 0))] + [full(t[0]) for t in smalls for _ in range(4)],
                 out_shape=[_sds((1, 128))] + [_sds(t[0].shape) for t in smalls for _ in range(4)])(stats, *flat)


def _adamw(params, name):
    n = len(params)

    def body(*refs):
        for p in range(n):
            w_ref, ga_ref, gb_ref, m_ref, v_ref = refs[5 * p:5 * p + 5]
            g_out, d_out, m_out, v_out = refs[5 * n + 4 * p:5 * n + 4 * p + 4]
            g = ga_ref[...] + gb_ref[...]
            g_out[...] = g
            d_out[...], m_out[...], v_out[...] = _adamw_math(w_ref[...], g, m_ref[...], v_ref[...])

    row = lambda w: _rows(w.shape[0] // 4, w.shape[1])
    out = _call(body, name=name, grid=(4,), in_specs=[row(p[0]) for p in params for _ in range(5)],
                out_specs=[row(p[0]) for p in params for _ in range(4)],
                out_shape=[_sds(p[0].shape) for p in params for _ in range(4)])(*[a for p in params for a in p])
    return [tuple(out[4 * p:4 * p + 4]) for p in range(n)]


def kernel(x, c, w_ada, b_ada, norm1_g, w_in, hg_lb_logits, hg_onorm_g, att_onorm_g, w_out, norm2_g, w_gate_up, w_down, final_g, loss_target, m_w_ada, m_b_ada, m_norm1_g, m_w_in, m_hg_lb_logits, m_hg_onorm_g, m_att_onorm_g, m_w_out, m_norm2_g, m_w_gate_up, m_w_down, m_final_g, v_w_ada, v_b_ada, v_norm1_g, v_w_in, v_hg_lb_logits, v_hg_onorm_g, v_att_onorm_g, v_w_out, v_norm2_g, v_w_gate_up, v_w_down, v_final_g):
    ix, iy, ic = _place()
    shard = 2 * ix + iy
    sample = 4 * ix + 2 * iy + ic
    n_ada = w_ada.shape[2]

    shards = [w_in[0], w_out[0], w_gate_up[0], w_down[0]]
    names = ["w_in", "w_out", "w_gu", "w_down"]
    shapes = [(N_SHARD,) + w.shape for w in shards]
    placed = [(_cast_place(shards[:1], shard, "place_w_in")[0],)]
    placed += [(p,) for p in _cast_place(shards[1:], shard, "place_rest")]

    b_part = lax.dynamic_slice(b_ada, (0, shard * n_ada), (1, n_ada))
    c_act, parts = _mod_rows(jnp.broadcast_to(c, (8, D_MODEL)), w_ada[0], b_part, "mod_rows")
    parts = parts[::2]
    mod = lax.dynamic_index_in_dim(parts, sample, axis=1, keepdims=False).reshape(1, 6 * D_MODEL)
    (first,), mod = _exchange_start(placed[:1], mod, "half", "gather_start_w_in")
    gathering = {}

    def get_w(name, after):
        if name == "w_in":
            halves = _exchange_wait(first, after, "half", "gather_wait_w_in")
            (passing,), token = _exchange_start([tuple(halves)], mod, "forward", "forward_start_w_in")
            (full,) = _exchange_wait(passing, token, "forward", "forward_wait_w_in")
            rest, full = _exchange_start(placed[1:], full, "gather", "gather_start_rest")
            gathering.update(zip(names[1:], rest))
            return full
        (full,) = _exchange_wait(gathering[name], after, "gather", "gather_wait_" + name)
        return full if name == "w_gu" else full.reshape(1, -1, D_MODEL)

    scattering = {}

    def put_g(name, g, g_bf16, then):
        shape = shapes[names.index(name)]
        land = lax.empty((3,) + shape[1:], BF16)
        (started,), then = _exchange_start([(g_bf16.reshape(shape), land)], then, "scatter", "scatter_start_" + name)
        scattering[name] = (g.reshape(shape), started)
        return then

    def summed(group, after, tag):
        lands = [_exchange_wait(scattering[nm][1], after, "scatter", "scatter_wait_" + nm)[1] for nm in group]
        return _sum_received([scattering[nm][0] for nm in group], shard, lands, "sum_" + tag)

    early = ["w_down", "w_gu", "w_out"]
    swapping = []

    def late(a):
        started, a = _sibling_start(summed(early, a, "early"), a, "swap_start")
        swapping.append(started)
        return a

    def project(h1):
        own = _mm_own_shard(h1, shards[0], shard, N_SHARD, "mm_in_own")
        w_full = get_w("w_in", own)
        return _mm_other_shards(h1, w_full, own, shard, "mm_in_rest"), w_full

    dx, stats = _local_step(x[0], loss_target[0], mod, norm1_g, hg_lb_logits, hg_onorm_g, att_onorm_g,
                            norm2_g, final_g, get_w, put_g, late, project)

    gathering_stats, stats = _all_gather_start(stats, "stats_start")
    moments = [(m_w_in, v_w_in), (m_w_out, v_w_out), (m_w_gate_up, v_w_gate_up), (m_w_down, v_w_down)]

    def update(group, sums, other, tag):
        params = [(shards[names.index(nm)], s, o, moments[names.index(nm)][0][0], moments[names.index(nm)][1][0])
                  for nm, s, o in zip(group, sums, other)]
        return dict(zip(group, _adamw(params, "adamw_" + tag)))

    sums, other = _sibling_wait(swapping[0], stats, "swap_wait")
    done = update(early, sums, other, "early")
    sum_in = summed(["w_in"], done["w_out"][1], "w_in")
    done.update(update(["w_in"], sum_in, _swap_sibling(sum_in, "swap_sum_in"), "w_in"))

    stats_all = _all_gather_wait(gathering_stats, done["w_in"][1], "stats_wait").reshape(8, ST_WIDTH)
    dmod = lax.dynamic_slice(stats_all, (0, ST_DMOD + shard * n_ada), (8, n_ada))

    as_row = lambda a: a.reshape(1, -1) if a.ndim == 1 else a
    smalls = [tuple(as_row(a) for a in t) for t in [
        (b_ada, m_b_ada, v_b_ada), (norm1_g, m_norm1_g, v_norm1_g),
        (hg_lb_logits, m_hg_lb_logits, v_hg_lb_logits), (hg_onorm_g, m_hg_onorm_g, v_hg_onorm_g),
        (att_onorm_g, m_att_onorm_g, v_att_onorm_g), (norm2_g, m_norm2_g, v_norm2_g),
        (final_g, m_final_g, v_final_g)]]
    loss, *small_out = _small_update(stats_all, smalls, "small_update")
    shapes_out = [b_ada.shape, norm1_g.shape, hg_lb_logits.shape, hg_onorm_g.shape, att_onorm_g.shape,
                  norm2_g.shape, final_g.shape]
    sg, sd, sm, sv = [[small_out[4 * p + i].reshape(shapes_out[p]) for p in range(7)] for i in range(4)]

    ada = _adamw_outer(w_ada[0], c_act.T, dmod, m_w_ada[0], v_w_ada[0], "adamw_w_ada")
    big = [ada] + [done[nm] for nm in names]
    bg, bd, bm, bv = [[t[i][None] for t in big] for i in range(4)]

    def order(b, s):
        return [b[0], s[0], s[1], b[1], s[2], s[3], s[4], b[2], s[5], b[3], b[4], s[6]]

    return (loss[0, 0], dx[None], *order(bg, sg), *order(bd, sd), *order(bm, sm), *order(bv, sv))
```

```python
import functools

import jax
import jax.numpy as jnp
from jax import lax
from jax.experimental import pallas as pl
from jax.experimental.pallas import tpu as pltpu

F32 = jnp.float32
BF16 = jnp.bfloat16
MESH = pl.DeviceIdType.MESH

D_MODEL = 1024
HG_WIDTH = 512
HG_HEAD = 128
HG_CHUNK = 64
HG_GROUP = 4
ATT_WIDTH = 512
ATT_HEADS = 8
ATT_BLOCK = 128
DILATIONS = (1, 4, 16)
D_FF = 2816
IN_WIDTH = 3584
N_SHARD = 4
RMS_EPS = 1e-6
NEG = -1e30

ADAM_LR = 0.001
ADAM_B1 = 0.9
ADAM_B2 = 0.999
ADAM_EPS = 1e-08
ADAM_WD = 0.01
ADAM_STEP = 10

VMEM_LIMIT = 56 * 2**20

ST_LOSS, ST_DFG, ST_DG2, ST_DG1 = 0, 1024, 2048, 3072
ST_DLB, ST_DAG, ST_DOG, ST_DMOD = 4096, 4608, 5120, 5248
ST_WIDTH = 5248 + 6144


def _call(body, *, name, grid, in_specs, out_specs, out_shape, scratch_shapes=()):
    return pl.pallas_call(
        body, name=name, grid=grid, in_specs=in_specs, out_specs=out_specs, out_shape=out_shape,
        scratch_shapes=list(scratch_shapes),
        compiler_params=pltpu.CompilerParams(
            dimension_semantics=("arbitrary",) * len(grid), vmem_limit_bytes=VMEM_LIMIT))


def _sds(shape, dtype=F32):
    return jax.ShapeDtypeStruct(shape, dtype)


def _dot(a, b):
    return jnp.dot(a, b, preferred_element_type=F32)


def _dot_nt(a, b):
    return lax.dot_general(a, b, (((1,), (1,)), ((), ())), preferred_element_type=F32)


def _dot_tn(a, b):
    return lax.dot_general(a, b, (((0,), (0,)), ((), ())), preferred_element_type=F32)


def _sigmoid(x):
    return 1.0 / (1.0 + jnp.exp(-x))


def _rows(tr, width):
    return pl.BlockSpec((tr, width), lambda i: (i, 0))


def _vec(width):
    return pl.BlockSpec((1, width), lambda i: (0, 0))


def _acc(ref, val, first):
    @pl.when(first)
    def _():
        ref[...] = val

    @pl.when(jnp.logical_not(first))
    def _():
        ref[...] += val


def _sub_rows(tr, fine, width):
    return pl.BlockSpec((fine, tr // fine, width), lambda i: (0, i, 0))


def _regroup_matrix(tr, groups):
    a = lax.broadcasted_iota(jnp.int32, (tr, tr), 0)
    b = lax.broadcasted_iota(jnp.int32, (tr, tr), 1)
    return (b == (a % groups) * (tr // groups) + a // groups).astype(BF16)


def _regroup(m, v):
    if v.dtype == BF16:
        return _dot(m, v)
    out = None
    for _ in range(3):
        part = v.astype(BF16)
        v = v - part.astype(F32)
        out = _dot(m, part) if out is None else out + _dot(m, part)
    return out


def _mm_nn(a, b3, name, tm=1024, out_dtype=F32):
    m, k = a.shape
    s, _, n = b3.shape

    def body(a_ref, b_ref, o_ref):
        o_ref[...] = _dot(a_ref[...], b_ref[0]).astype(out_dtype)

    return _call(
        body, name=name, grid=(s, m // tm),
        in_specs=[pl.BlockSpec((tm, k), lambda j, i: (i, 0)), pl.BlockSpec((1, k, n), lambda j, i: (j, 0, 0))],
        out_specs=pl.BlockSpec((tm, n), lambda j, i: (i, j)), out_shape=_sds((m, s * n), out_dtype))(a, b3)


def _mm_own_shard(a, w, shard, s, name, tm=1024):
    m, k = a.shape
    n = w.shape[1]

    def body(s_ref, a_ref, w_ref, o_ref):
        o_ref[...] = _dot(a_ref[...], w_ref[...].astype(BF16)).astype(BF16)

    return pl.pallas_call(
        body, name=name, out_shape=_sds((m, s * n), BF16),
        grid_spec=pltpu.PrefetchScalarGridSpec(
            num_scalar_prefetch=1, grid=(m // tm,),
            in_specs=[pl.BlockSpec((tm, k), lambda i, sh: (i, 0)), pl.BlockSpec((k, n), lambda i, sh: (0, 0))],
            out_specs=pl.BlockSpec((tm, n), lambda i, sh: (i, sh[0]))),
        compiler_params=pltpu.CompilerParams(dimension_semantics=("arbitrary",), vmem_limit_bytes=VMEM_LIMIT),
    )(shard.reshape(1).astype(jnp.int32), a, w)


def _mm_other_shards(a, b3, partial, shard, name, tm=1024):
    m, k = a.shape
    s, _, n = b3.shape
    which = lambda j, sh: (sh[0] + 1 + j) % s

    def body(s_ref, a_ref, b_ref, p_ref, o_ref):
        o_ref[...] = _dot(a_ref[...], b_ref[0]).astype(BF16)

    return pl.pallas_call(
        body, name=name, out_shape=_sds(partial.shape, BF16),
        grid_spec=pltpu.PrefetchScalarGridSpec(
            num_scalar_prefetch=1, grid=(s - 1, m // tm),
            in_specs=[pl.BlockSpec((tm, k), lambda j, i, sh: (i, 0)),
                      pl.BlockSpec((1, k, n), lambda j, i, sh: (which(j, sh), 0, 0)),
                      pl.BlockSpec(memory_space=pl.ANY)],
            out_specs=pl.BlockSpec((tm, n), lambda j, i, sh: (i, which(j, sh)))),
        input_output_aliases={3: 0},
        compiler_params=pltpu.CompilerParams(dimension_semantics=("arbitrary",) * 2, vmem_limit_bytes=VMEM_LIMIT),
    )(shard.reshape(1).astype(jnp.int32), a, b3, partial)


def _mm_tn(a, dy, s, name, tm, tk):
    m, k = a.shape
    n = dy.shape[1] // s
    steps = m // tm

    def body(a_ref, dy_ref, o_ref, ob_ref):
        p = _dot_tn(a_ref[...], dy_ref[...])[None]
        if steps == 1:
            o_ref[...] = p
            ob_ref[...] = p.astype(BF16)
        else:
            i = pl.program_id(2)
            _acc(o_ref, p, i == 0)

            @pl.when(i == steps - 1)
            def _():
                ob_ref[...] = o_ref[...].astype(BF16)

    out = pl.BlockSpec((1, tk, n), lambda kk, j, i: (j, kk, 0))
    return _call(
        body, name=name, grid=(k // tk, s, steps),
        in_specs=[pl.BlockSpec((tm, tk), lambda kk, j, i: (i, kk)), pl.BlockSpec((tm, n), lambda kk, j, i: (i, j))],
        out_specs=[out, out], out_shape=[_sds((s, k, n)), _sds((s, k, n), BF16)])(a, dy)


def _rms(x):
    return lax.rsqrt(jnp.mean(x * x, axis=-1, keepdims=True) + RMS_EPS)


def _rms_bwd(dxh, xh, r):
    return r * (dxh - xh * jnp.mean(dxh * xh, axis=-1, keepdims=True))


def _norm_mod(x, g, scale, shift, name, tr=512):
    t = x.shape[0]

    def body(x_ref, g_ref, sc_ref, sh_ref, h_ref):
        xv = x_ref[...]
        n = xv * _rms(xv) * g_ref[...]
        h_ref[...] = (n * (1.0 + sc_ref[...]) + sh_ref[...]).astype(BF16)

    return _call(body, name=name, grid=(t // tr,),
                 in_specs=[_rows(tr, D_MODEL), _vec(D_MODEL), _vec(D_MODEL), _vec(D_MODEL)],
                 out_specs=_rows(tr, D_MODEL), out_shape=_sds((t, D_MODEL), BF16))(x, g, scale, shift)


def _out_resid_norm_mod(x, mixin, w_out, gate, g, scale, shift, name, tr=512):
    t = x.shape[0]

    def body(x_ref, mi_ref, w_ref, gt_ref, g_ref, sc_ref, sh_ref, m_ref, x2_ref, h_ref):
        mix = _dot(mi_ref[...], w_ref[0])
        m_ref[...] = mix
        x2 = x_ref[...] + gt_ref[...] * mix
        x2_ref[...] = x2
        n = x2 * _rms(x2) * g_ref[...]
        h_ref[...] = (n * (1.0 + sc_ref[...]) + sh_ref[...]).astype(BF16)

    row = _rows(tr, D_MODEL)
    return _call(body, name=name, grid=(t // tr,),
                 in_specs=[row, row, pl.BlockSpec(w_out.shape, lambda i: (0, 0, 0))] + [_vec(D_MODEL)] * 4,
                 out_specs=[row, row, row],
                 out_shape=[_sds((t, D_MODEL)), _sds((t, D_MODEL)), _sds((t, D_MODEL), BF16)])(
                     x, mixin, w_out, gate, g, scale, shift)


def _mm_gate_up(h, w_gu, name, tm=1024):
    m, k = h.shape
    n = w_gu.shape[2]

    def body(h_ref, wa_ref, wu_ref, da_ref, du_ref, o_ref):
        hv = h_ref[...]
        a = _dot(hv, wa_ref[0])
        u = _dot(hv, wu_ref[0])
        sg = _sigmoid(a)
        silu = a * sg
        da_ref[...] = (u * sg * (1.0 + a * (1.0 - sg))).astype(BF16)
        du_ref[...] = silu.astype(BF16)
        o_ref[...] = (silu * u).astype(BF16)

    out = pl.BlockSpec((tm, n), lambda j, i: (i, j))
    return _call(body, name=name, grid=(2, m // tm),
                 in_specs=[pl.BlockSpec((tm, k), lambda j, i: (i, 0)), pl.BlockSpec((1, k, n), lambda j, i: (j, 0, 0)),
                           pl.BlockSpec((1, k, n), lambda j, i: (j + 2, 0, 0))],
                 out_specs=[out, out, out], out_shape=[_sds((m, 2 * n), BF16)] * 3)(h, w_gu, w_gu)


def _mm_down_dx(dffn, w_down, act_da, act_du, name, tm=512):
    m = dffn.shape[0]
    _, k, n = w_down.shape

    def body(d_ref, w_ref, da_ref, du_ref, o_ref):
        dact = _dot_nt(d_ref[...], w_ref[0])
        o_ref[:, :k] = (dact * da_ref[...].astype(F32)).astype(BF16)
        o_ref[:, k:] = (dact * du_ref[...].astype(F32)).astype(BF16)

    return _call(body, name=name, grid=(m // tm,),
                 in_specs=[_rows(tm, n), pl.BlockSpec((1, k, n), lambda i: (0, 0, 0)), _rows(tm, k), _rows(tm, k)],
                 out_specs=_rows(tm, 2 * k), out_shape=_sds((m, 2 * k), BF16))(dffn, w_down, act_da, act_du)


def _down_loss(x2, act, w_down, gate, fg, tgt, name, tr=512):
    t = x2.shape[0]
    _, k, n = w_down.shape

    def body(x_ref, a_ref, w_ref, gt_ref, fg_ref, t_ref, dx_ref, df_ref, l_ref, dfg_ref, dgt_ref):
        first = pl.program_id(0) == 0
        ffn_v = _dot(a_ref[...], w_ref[0])
        x3 = x_ref[...] + gt_ref[...] * ffn_v
        r = _rms(x3)
        xh = x3 * r
        err = xh * fg_ref[...] - t_ref[...]
        dy = err * (1.0 / D_MODEL)
        dx3 = _rms_bwd(dy * fg_ref[...], xh, r)
        dx_ref[...] = dx3
        df_ref[...] = (dx3 * gt_ref[...]).astype(BF16)
        _acc(l_ref, jnp.sum(err * err, axis=0, keepdims=True), first)
        _acc(dfg_ref, jnp.sum(dy * xh, axis=0, keepdims=True), first)
        _acc(dgt_ref, jnp.sum(dx3 * ffn_v, axis=0, keepdims=True), first)

    row, vec = _rows(tr, D_MODEL), _vec(D_MODEL)
    return _call(body, name=name, grid=(t // tr,),
                 in_specs=[row, _rows(tr, k), pl.BlockSpec((1, k, n), lambda i: (0, 0, 0)), vec, vec, row],
                 out_specs=[row, row, vec, vec, vec],
                 out_shape=[_sds((t, D_MODEL)), _sds((t, D_MODEL), BF16)] + [_sds((1, D_MODEL))] * 3)(
                     x2, act, w_down, gate, fg, tgt)


def _norm_mod_bwd(dh, x, g, scale, dres, name, gate=None, mix=None, w=None, tr=512):
    t = x.shape[0]
    below = gate is not None

    def body(*refs):
        if w is not None:
            w_ref, refs = refs[1], refs[:1] + refs[2:]
        if below:
            dh_ref, x_ref, g_ref, sc_ref, dr_ref, gt_ref, m_ref, dx_ref, dsh_ref, dsc_ref, dg_ref, dgt_ref, dm_ref = refs
        else:
            dh_ref, x_ref, g_ref, sc_ref, dr_ref, dx_ref, dsh_ref, dsc_ref, dg_ref = refs
        first = pl.program_id(0) == 0
        xv = x_ref[...]
        if w is None:
            dhv = dh_ref[...].astype(F32)
        else:
            n = w.shape[2]
            dhv = _dot_nt(dh_ref[:, 0:n], w_ref[0])
            for j in range(1, w.shape[0]):
                dhv = dhv + _dot_nt(dh_ref[:, j * n:(j + 1) * n], w_ref[j])
        r = _rms(xv)
        xh = xv * r
        dn = dhv * (1.0 + sc_ref[...])
        dx = dr_ref[...] + _rms_bwd(dn * g_ref[...], xh, r)
        dx_ref[...] = dx
        _acc(dsh_ref, jnp.sum(dhv, axis=0, keepdims=True), first)
        _acc(dsc_ref, jnp.sum(dhv * xh * g_ref[...], axis=0, keepdims=True), first)
        _acc(dg_ref, jnp.sum(dn * xh, axis=0, keepdims=True), first)
        if below:
            _acc(dgt_ref, jnp.sum(dx * m_ref[...], axis=0, keepdims=True), first)
            dm_ref[...] = (dx * gt_ref[...]).astype(BF16)

    row, vec = _rows(tr, D_MODEL), _vec(D_MODEL)
    first_specs = [row] if w is None else [
        _rows(tr, dh.shape[1]), pl.BlockSpec(w.shape, lambda i: (0, 0, 0), pipeline_mode=pl.Buffered(1))]
    in_specs = first_specs + [row, vec, vec, row] + ([vec, row] if below else [])
    out_specs = [row, vec, vec, vec] + ([vec, row] if below else [])
    out_shape = [_sds((t, D_MODEL))] + [_sds((1, D_MODEL))] * 3 + ([_sds((1, D_MODEL)), _sds((t, D_MODEL), BF16)] if below else [])
    args = ((dh,) if w is None else (dh, w)) + (x, g, scale, dres) + ((gate, mix) if below else ())
    return _call(body, name=name, grid=(t // tr,), in_specs=in_specs, out_specs=out_specs, out_shape=out_shape)(*args)


def _mix_in_bwd(dmix, w_out, o_hg, proj, att, og, ag, fine, name, tr=512):
    t = o_hg.shape[0]

    def body(dy_ref, w_ref, o_ref, g_ref, a_ref, og_ref, ag_ref,
             do_ref, dg_ref, da_ref, dd_ref, das_ref, dds_ref, dog_ref, dag_ref, to_sub):
        first = pl.program_id(0) == 0

        @pl.when(first)
        def _():
            to_sub[...] = _regroup_matrix(tr, tr // fine)

        dmi = _dot_nt(dy_ref[...], w_ref[0])
        dog = jnp.zeros((1, HG_HEAD), F32)
        for h in range(HG_WIDTH // HG_HEAD):
            sl = slice(h * HG_HEAD, (h + 1) * HG_HEAD)
            oh = o_ref[:, sl].astype(F32)
            gv = g_ref[:, sl].astype(F32)
            dv = dmi[:, sl]
            r = _rms(oh)
            xh = oh * r
            sg = _sigmoid(gv)
            dno = dv * gv * sg
            dg_ref[:, sl] = (dv * xh * og_ref[...] * sg * (1.0 + gv * (1.0 - sg))).astype(BF16)
            dog = dog + jnp.sum(dno * xh, axis=0, keepdims=True)
            do_ref[:, sl] = _rms_bwd(dno * og_ref[...], xh, r).astype(BF16)
        _acc(dog_ref, dog, first)
        av = a_ref[...]
        dav = dmi[:, HG_WIDTH:]
        r = _rms(av)
        xa = av * r
        _acc(dag_ref, jnp.sum(dav * xa, axis=0, keepdims=True), first)
        datt = _rms_bwd(dav * ag_ref[...], xa, r)
        da_ref[...] = datt.astype(BF16)
        das_ref[...] = _regroup(to_sub[...], datt.astype(BF16)).astype(BF16).reshape(das_ref.shape)
        prod = datt * av
        lane = lax.broadcasted_iota(jnp.int32, (1, 128), 1)
        dd = jnp.zeros((tr, 128), F32)
        for hp in range(ATT_HEADS // 2):
            pp = prod[:, hp * 128:(hp + 1) * 128]
            lo = jnp.sum(jnp.where(lane < 64, pp, 0.0), axis=-1, keepdims=True)
            hi = jnp.sum(jnp.where(lane >= 64, pp, 0.0), axis=-1, keepdims=True)
            dd = jnp.where(lane == 2 * hp, lo, dd)
            dd = jnp.where(lane == 2 * hp + 1, hi, dd)
        dd_ref[...] = dd
        dds_ref[...] = _regroup(to_sub[...], dd).reshape(dds_ref.shape)

    half = _rows(tr, HG_WIDTH)
    out = _call(body, name=name, grid=(t // tr,),
                in_specs=[_rows(tr, D_MODEL), pl.BlockSpec(w_out.shape, lambda i: (0, 0, 0)), half,
                          pl.BlockSpec((tr, HG_WIDTH), lambda i: (i, 3)), half, _vec(HG_HEAD), _vec(ATT_WIDTH)],
                out_specs=[half, half, half, _rows(tr, 128), _sub_rows(tr, fine, ATT_WIDTH), _sub_rows(tr, fine, 128),
                           _vec(HG_HEAD), _vec(ATT_WIDTH)],
                out_shape=[_sds((t, HG_WIDTH), BF16)] * 3 + [_sds((t, 128)), _sds((fine, t // fine, ATT_WIDTH), BF16),
                           _sds((fine, t // fine, 128)), _sds((1, HG_HEAD)), _sds((1, ATT_WIDTH))],
                scratch_shapes=[pltpu.VMEM((tr, tr), BF16)])(dmix, w_out, o_hg, proj, att, og, ag)
    out = list(out)
    return out[:4] + [out[4].reshape(t, ATT_WIDTH), out[5].reshape(t, 128)] + out[6:]


def _dproj(dhg, dg, dqs, dkvs, name, tr=1024):
    t = dhg.shape[0]
    w3 = 3 * HG_WIDTH
    w4 = w3 + HG_WIDTH
    nbr = len(dqs)

    def body(*refs):
        h_ref, g_ref, q_refs, kv_refs, o_ref = refs[0], refs[1], refs[2:2 + nbr], refs[2 + nbr:2 + 2 * nbr], refs[-1]
        o_ref[:, :w3] = h_ref[...]
        o_ref[:, w3:w4] = g_ref[...].astype(BF16)
        o_ref[:, w4:w4 + ATT_WIDTH] = sum(r[...].astype(F32) for r in q_refs).astype(BF16)
        o_ref[:, w4 + ATT_WIDTH:] = sum(r[...].astype(F32) for r in kv_refs).astype(BF16)

    return _call(body, name=name, grid=(t // tr,),
                 in_specs=[_rows(tr, w3), _rows(tr, HG_WIDTH)] + [_rows(tr, ATT_WIDTH)] * nbr
                 + [_rows(tr, 2 * ATT_WIDTH)] * nbr,
                 out_specs=_rows(tr, IN_WIDTH), out_shape=_sds((t, IN_WIDTH), BF16))(dhg, dg, *dqs, *dkvs)


def _chunk_tri(upper):
    row = lax.broadcasted_iota(jnp.int32, (HG_GROUP, HG_CHUNK, HG_CHUNK), 1)
    col = lax.broadcasted_iota(jnp.int32, (HG_GROUP, HG_CHUNK, HG_CHUNK), 2)
    return (row <= col if upper else row >= col).astype(BF16)


def _chunk_cumsum(x, tri):
    x3 = x.reshape(HG_GROUP, HG_CHUNK, x.shape[1])
    dims = (((2,), (1,)), ((0,), (0,)))
    out = None
    for _ in range(3):
        part = x3.astype(BF16)
        x3 = x3 - part.astype(F32)
        term = lax.dot_general(tri, part, dims, preferred_element_type=F32)
        out = term if out is None else out + term
    return out.reshape(x.shape)


def _hg_gates(f_raw, q_raw, lb, tri):
    sg = _sigmoid(f_raw)
    f = lb + (1.0 - lb) * sg
    k = 1.0 - f
    b = _chunk_cumsum(jnp.log(f), tri)
    sq = _sigmoid(q_raw)
    return sg, f, k, b, sq


def _hg_masks(rows):
    row = lax.broadcasted_iota(jnp.int32, (rows, rows), 0)
    col = lax.broadcasted_iota(jnp.int32, (rows, rows), 1)
    same = (row // HG_CHUNK) == (col // HG_CHUNK)
    return jnp.logical_and(row >= col, same), jnp.logical_and(row <= col, same)


def _per_chunk(rows_of):
    return jnp.concatenate([jnp.broadcast_to(r, (HG_CHUNK, r.shape[1])) for r in rows_of], axis=0)


def _hgrn_fwd(proj, lb_logits, name):
    t = proj.shape[0]
    nc = t // HG_CHUNK
    nh = HG_WIDTH // HG_HEAD
    rows = HG_GROUP * HG_CHUNK

    def body(q_ref, f_ref, i_ref, lg_ref, o_ref, st_ref, s_scr):
        @pl.when(pl.program_id(0) == 0)
        def _():
            s_scr[...] = jnp.zeros_like(s_scr)

        lg = lg_ref[...]
        lb_all = _sigmoid(lg[0:1] - lg[1:2])
        causal, _ = _hg_masks(rows)
        tri = _chunk_tri(False)
        for h in range(nh):
            sl = slice(h * HG_HEAD, (h + 1) * HG_HEAD)
            q_raw = q_ref[:, sl].astype(F32)
            _, _, k, b, sq = _hg_gates(f_ref[:, sl].astype(F32), q_raw, lb_all[:, sl], tri)
            v = i_ref[:, sl].astype(BF16)
            gls = [b[(g + 1) * HG_CHUNK - 1:(g + 1) * HG_CHUNK] for g in range(HG_GROUP)]
            bm = _per_chunk([b[g * HG_CHUNK + HG_CHUNK // 2 - 1:g * HG_CHUNK + HG_CHUNK // 2] for g in range(HG_GROUP)])
            qd = (q_raw * sq * jnp.exp(b)).astype(BF16)
            qm = (q_raw * sq * jnp.exp(b - bm)).astype(BF16)
            km = (k * jnp.exp(bm - b)).astype(BF16)
            ke = (k * jnp.exp(_per_chunk(gls) - b)).astype(BF16)
            a = jnp.where(causal, _dot_nt(qm, km), 0.0).astype(BF16)
            o_intra = _dot(a, v)
            st = s_scr[h]
            o_inter = []
            for g in range(HG_GROUP):
                rs = slice(g * HG_CHUNK, (g + 1) * HG_CHUNK)
                st_ref[g, sl, :] = st
                o_inter.append(_dot_nt(qd[rs], st.astype(BF16)))
                st = st * jnp.exp(gls[g]) + _dot_tn(v[rs], ke[rs])
            s_scr[h] = st
            o_ref[:, sl] = (o_intra + jnp.concatenate(o_inter, axis=0)).astype(BF16)

    blk = lambda j: pl.BlockSpec((rows, HG_WIDTH), lambda c: (c, j))
    return _call(body, name=name, grid=(nc // HG_GROUP,),
                 in_specs=[blk(0), blk(1), blk(2), pl.BlockSpec((2, HG_WIDTH), lambda c: (0, 0))],
                 out_specs=[blk(0), pl.BlockSpec((HG_GROUP, HG_WIDTH, HG_HEAD), lambda c: (c, 0, 0))],
                 out_shape=[_sds((t, HG_WIDTH), BF16), _sds((nc, HG_WIDTH, HG_HEAD))],
                 scratch_shapes=[pltpu.VMEM((nh, HG_HEAD, HG_HEAD), F32)])(proj, proj, proj, lb_logits)


def _hgrn_bwd(proj, lb_logits, states, do, name):
    t = proj.shape[0]
    ng = t // (HG_GROUP * HG_CHUNK)
    nh = HG_WIDTH // HG_HEAD
    rows = HG_GROUP * HG_CHUNK

    def body(q_ref, f_ref, i_ref, lg_ref, st_ref, do_ref, d_ref, dlb_ref, ds_scr):
        first = pl.program_id(0) == 0

        @pl.when(first)
        def _():
            ds_scr[...] = jnp.zeros_like(ds_scr)

        lg = lg_ref[...]
        lb_all = _sigmoid(lg[0:1] - lg[1:2])
        causal, _ = _hg_masks(rows)
        tri = _chunk_tri(False)
        tri_t = _chunk_tri(True)
        dlb = []
        for h in range(nh):
            sl = slice(h * HG_HEAD, (h + 1) * HG_HEAD)
            q_raw = q_ref[:, sl].astype(F32)
            lb = lb_all[:, sl]
            sg, f, k, b, sq = _hg_gates(f_ref[:, sl].astype(F32), q_raw, lb, tri)
            v = i_ref[:, sl].astype(BF16)
            gls = [b[(g + 1) * HG_CHUNK - 1:(g + 1) * HG_CHUNK] for g in range(HG_GROUP)]
            bm = _per_chunk([b[g * HG_CHUNK + HG_CHUNK // 2 - 1:g * HG_CHUNK + HG_CHUNK // 2] for g in range(HG_GROUP)])
            eb = jnp.exp(b)
            ebm = jnp.exp(b - bm)
            emb = jnp.exp(bm - b)
            egb = jnp.exp(_per_chunk(gls) - b)
            ke = k * egb
            qd_b, qm_b = (q_raw * sq * eb).astype(BF16), (q_raw * sq * ebm).astype(BF16)
            km_b, ke_b = (k * emb).astype(BF16), ke.astype(BF16)
            dov = do_ref[:, sl].astype(BF16)
            a = jnp.where(causal, _dot_nt(qm_b, km_b), 0.0).astype(BF16)
            da = jnp.where(causal, _dot_nt(dov, v), 0.0).astype(BF16)
            dkm = _dot_tn(da, qm_b)
            dst = ds_scr[h]
            dqd_s, dv_s, dke_s, dgl_s = [None] * HG_GROUP, [None] * HG_GROUP, [None] * HG_GROUP, [None] * HG_GROUP
            for g in reversed(range(HG_GROUP)):
                rs = slice(g * HG_CHUNK, (g + 1) * HG_CHUNK)
                st = st_ref[g, sl, :]
                dst_b = dst.astype(BF16)
                egl = jnp.exp(gls[g])
                dqd_s[g] = _dot(dov[rs], st.astype(BF16))
                dv_s[g] = _dot_nt(ke_b[rs], dst_b)
                dke_s[g] = _dot(v[rs], dst_b)
                dgl_s[g] = jnp.sum(dst * st, axis=0, keepdims=True) * egl
                dst = _dot_tn(dov[rs], qd_b[rs]) + dst * egl
            ds_scr[h] = dst
            dqm = _dot(da, km_b)
            dqd = jnp.concatenate(dqd_s, axis=0)
            dv = _dot_tn(a, dov) + jnp.concatenate(dv_s, axis=0)
            dke = jnp.concatenate(dke_s, axis=0)
            t1 = dke * ke
            db = dqm * qm_b.astype(F32) - dkm * km_b.astype(F32) + dqd * qd_b.astype(F32) - t1
            dgl = _per_chunk([dgl_s[g] + jnp.sum(t1[g * HG_CHUNK:(g + 1) * HG_CHUNK], axis=0, keepdims=True)
                              for g in range(HG_GROUP)])
            dlf = _chunk_cumsum(db, tri_t) + dgl
            df = dlf / f - (dkm * emb + dke * egb)
            d_ref[:, sl] = ((dqm * ebm + dqd * eb) * sq * (1.0 + q_raw * (1.0 - sq))).astype(BF16)
            d_ref[:, HG_WIDTH + h * HG_HEAD:HG_WIDTH + (h + 1) * HG_HEAD] = (
                df * (1.0 - lb) * sg * (1.0 - sg)).astype(BF16)
            d_ref[:, 2 * HG_WIDTH + h * HG_HEAD:2 * HG_WIDTH + (h + 1) * HG_HEAD] = dv.astype(BF16)
            dlb.append(jnp.sum(df * (1.0 - sg), axis=0, keepdims=True))
        _acc(dlb_ref, jnp.concatenate(dlb, axis=1), first)

    rev = lambda j: pl.BlockSpec((rows, HG_WIDTH), lambda c: (ng - 1 - c, j))
    return _call(body, name=name, grid=(ng,),
                 in_specs=[rev(0), rev(1), rev(2), pl.BlockSpec((2, HG_WIDTH), lambda c: (0, 0)),
                           pl.BlockSpec((HG_GROUP, HG_WIDTH, HG_HEAD), lambda c: (ng - 1 - c, 0, 0)), rev(0)],
                 out_specs=[pl.BlockSpec((rows, 3 * HG_WIDTH), lambda c: (ng - 1 - c, 0)), _vec(HG_WIDTH)],
                 out_shape=[_sds((t, 3 * HG_WIDTH), BF16), _sds((1, HG_WIDTH))],
                 scratch_shapes=[pltpu.VMEM((nh, HG_HEAD, HG_HEAD), F32)])(proj, proj, proj, lb_logits, states, do)


def _to_sub(a, dil):
    t, w = a.shape
    return a if dil == 1 else a.reshape(t // dil, dil, w).transpose(1, 0, 2).reshape(t, w)


def _from_sub(a, dil):
    t, w = a.shape
    return a if dil == 1 else a.reshape(dil, t // dil, w).transpose(1, 0, 2).reshape(t, w)


def _att_mask(has_prev, seg):
    def place(v):
        v = v % ATT_BLOCK
        return v if seg == 1 else seg * (v % (ATT_BLOCK // seg)) + v // (ATT_BLOCK // seg)

    row = lax.broadcasted_iota(jnp.int32, (2 * ATT_BLOCK, 2 * ATT_BLOCK), 0)
    col = lax.broadcasted_iota(jnp.int32, (2 * ATT_BLOCK, 2 * ATT_BLOCK), 1)
    qi, kj = place(row), place(col)
    prev = jnp.logical_and(jnp.logical_and(col < ATT_BLOCK, kj >= qi), has_prev)
    cur = jnp.logical_and(col >= ATT_BLOCK, kj <= qi)
    return jnp.logical_or(prev, cur), lax.broadcasted_iota(jnp.int32, (1, 128), 1)


def _get(ref, sl):
    if len(ref.shape) == 2:
        return ref[:, sl]
    v = ref[:, :, sl]
    return v.reshape(ATT_BLOCK, v.shape[2])


def _put(ref, sl, val):
    if len(ref.shape) == 2:
        ref[:, sl] = val
    else:
        ref[:, :, sl] = val.reshape(ref.shape[0], ref.shape[1], val.shape[1])


def _att_spec(nb, dil, seg, width, col, back):
    bps = nb // dil

    def plain(n):
        return jnp.clip(n - back, 0, nb - 1), col

    def segmented(n):
        m = jnp.clip(n - back, 0, nb - 1)
        return 0, m // bps, m % bps, 0, col

    if seg == 1:
        return pl.BlockSpec((ATT_BLOCK, width), plain)
    return pl.BlockSpec((seg, None, None, ATT_BLOCK // seg, width), segmented)


def _att_shape(nb, dil, seg, width):
    t = nb * ATT_BLOCK
    return (t, width) if seg == 1 else (seg, dil, nb // dil, ATT_BLOCK // seg, width)


def _att_view(a, nb, dil, seg):
    return a.reshape(_att_shape(nb, dil, seg, a.shape[1]))


def _attn_fwd_block(q_ref, kc_ref, kp_ref, vc_ref, vp_ref, o_ref, l_ref, has_prev, seg, lane0):
    mask, lane = _att_mask(has_prev, seg)
    lo = lane < 64
    nq = ATT_BLOCK
    lse_all = jnp.zeros((nq, 128), F32)
    for hp in range(ATT_HEADS // 2):
        sl = slice(hp * 128, (hp + 1) * 128)
        q2 = _get(q_ref, sl)
        zero = jnp.zeros_like(q2)
        q2 = q2 * 0.125
        qs = jnp.concatenate([jnp.where(lo, q2, zero), jnp.where(lo, zero, q2)], axis=0)
        kk = jnp.concatenate([_get(kp_ref, sl), _get(kc_ref, sl)], axis=0)
        vv = jnp.concatenate([_get(vp_ref, sl), _get(vc_ref, sl)], axis=0)
        s = jnp.where(mask, _dot_nt(qs, kk), NEG)
        mx = jnp.max(s, axis=-1, keepdims=True)
        p = jnp.exp(s - mx)
        l = jnp.sum(p, axis=-1, keepdims=True)
        o = _dot(p.astype(BF16), vv) * (1.0 / l)
        _put(o_ref, sl, jnp.where(lo, o[:nq], o[nq:]).astype(BF16))
        lse = mx + jnp.log(l)
        lse_all = jnp.where(lane == lane0 + 2 * hp, lse[:nq], lse_all)
        lse_all = jnp.where(lane == lane0 + 2 * hp + 1, lse[nq:], lse_all)
    _put(l_ref, slice(None), lse_all)


def _attn_fwd(branches, name):
    t = branches[0][0].shape[0]
    nb = t // ATT_BLOCK
    nbr = len(branches)

    def body(*refs):
        n = pl.program_id(0)
        for i, (_, dil, seg) in enumerate(branches):
            _attn_fwd_block(*refs[5 * i:5 * i + 5], *refs[5 * nbr + 2 * i:5 * nbr + 2 * i + 2],
                            (n % (nb // dil)) != 0, seg, ATT_HEADS * i)

    in_specs, args, out_specs, out_shape = [], [], [], []
    for qkv, dil, seg in branches:
        c0 = qkv.shape[1] // ATT_WIDTH - 3
        in_specs += [_att_spec(nb, dil, seg, ATT_WIDTH, c0 + j, back) for j, back in [(0, 0), (1, 0), (1, 1), (2, 0), (2, 1)]]
        args += [_att_view(qkv, nb, dil, seg)] * 5
        out_specs += [_att_spec(nb, dil, seg, ATT_WIDTH, 0, 0), _att_spec(nb, dil, seg, 128, 0, 0)]
        out_shape += [_sds(_att_shape(nb, dil, seg, ATT_WIDTH), BF16), _sds(_att_shape(nb, dil, seg, 128))]
    out = _call(body, name=name, grid=(nb,), in_specs=in_specs, out_specs=out_specs, out_shape=out_shape)(*args)
    return [(out[2 * i].reshape(t, ATT_WIDTH), out[2 * i + 1].reshape(t, 128)) for i in range(nbr)]


def _combine_mix_in(os_, ls_, fine, o_hg, proj, og, ag, name, tr=512):
    t = os_[0].shape[0]
    nbr = len(os_)

    def body(*refs):
        o_refs, l_refs = refs[:nbr], refs[nbr:2 * nbr]
        oh_ref, g_ref, og_ref, ag_ref, a_ref, lt_ref, lts_ref, m_ref, to_natural, to_sub = refs[2 * nbr:]

        @pl.when(pl.program_id(0) == 0)
        def _():
            to_natural[...] = _regroup_matrix(tr, fine)
            to_sub[...] = _regroup_matrix(tr, tr // fine)

        lane = lax.broadcasted_iota(jnp.int32, (1, 128), 1)
        packed = l_refs[0][...] + _regroup(to_natural[...], sum(r[...] for r in l_refs[1:]).reshape(tr, 128))
        ls = [packed if i == 0 else pltpu.roll(packed, 128 - ATT_HEADS * i, 1) for i in range(nbr)]
        mx = functools.reduce(jnp.maximum, ls)
        tot = mx + jnp.log(sum(jnp.exp(l - mx) for l in ls))
        ws = [jnp.exp(l - tot) for l in ls]
        tot = jnp.where(lane < ATT_HEADS, tot, 0.0)
        lt_ref[...] = tot
        lts_ref[...] = _regroup(to_sub[...], tot).reshape(lts_ref.shape)
        o_vals = [o_refs[0]] + [_regroup(to_natural[...], r[...].reshape(tr, ATT_WIDTH)) for r in o_refs[1:]]
        pairs = []
        for hp in range(ATT_HEADS // 2):
            sl = slice(hp * 128, (hp + 1) * 128)
            acc = jnp.zeros((tr, 128), F32)
            for w, o in zip(ws, o_vals):
                wf = jnp.where(lane < 64, w[:, 2 * hp:2 * hp + 1], w[:, 2 * hp + 1:2 * hp + 2])
                acc = acc + wf * o[:, sl]
            pairs.append(acc)
        av = jnp.concatenate(pairs, axis=1)
        a_ref[...] = av
        m_ref[:, HG_WIDTH:] = (av * _rms(av) * ag_ref[...]).astype(BF16)
        for h in range(HG_WIDTH // HG_HEAD):
            sl = slice(h * HG_HEAD, (h + 1) * HG_HEAD)
            oh = oh_ref[:, sl].astype(F32)
            gv = g_ref[:, sl].astype(F32)
            m_ref[:, sl] = (oh * _rms(oh) * og_ref[...] * (gv * _sigmoid(gv))).astype(BF16)

    half = _rows(tr, ATT_WIDTH)
    sub = lambda a: a.reshape(fine, t // fine, a.shape[1])
    att, lse, lse_sub, mixin = _call(
        body, name=name, grid=(t // tr,),
        in_specs=[half] + [_sub_rows(tr, fine, ATT_WIDTH)] * (nbr - 1) + [_rows(tr, 128)] + [_sub_rows(tr, fine, 128)] * (nbr - 1)
        + [half, pl.BlockSpec((tr, HG_WIDTH), lambda i: (i, 3)), _vec(HG_HEAD), _vec(ATT_WIDTH)],
        out_specs=[half, _rows(tr, 128), _sub_rows(tr, fine, 128), _rows(tr, D_MODEL)],
        out_shape=[_sds((t, ATT_WIDTH)), _sds((t, 128)), _sds((fine, t // fine, 128)), _sds((t, D_MODEL), BF16)],
        scratch_shapes=[pltpu.VMEM((tr, tr), BF16)] * 2)(
            os_[0], *map(sub, os_[1:]), ls_[0], *map(sub, ls_[1:]), o_hg, proj, og, ag)
    return att, lse, lse_sub.reshape(t, 128), mixin


def _attn_bwd_block(q_ref, kc_ref, kp_ref, vc_ref, vp_ref, do_ref, l_ref, d_ref, dq_ref, dkv_ref, carry, has_prev, seg):
    w = ATT_WIDTH
    nq = ATT_BLOCK
    mask, lane = _att_mask(has_prev, seg)
    lo = lane < 64
    lse, ddv = _get(l_ref, slice(None)), _get(d_ref, slice(None))
    for hp in range(ATT_HEADS // 2):
        sl = slice(hp * 128, (hp + 1) * 128)
        sv = slice(w + hp * 128, w + (hp + 1) * 128)
        q2, do2 = _get(q_ref, sl), _get(do_ref, sl)
        zero = jnp.zeros_like(q2)
        q2 = q2 * 0.125
        qs = jnp.concatenate([jnp.where(lo, q2, zero), jnp.where(lo, zero, q2)], axis=0)
        dos = jnp.concatenate([jnp.where(lo, do2, zero), jnp.where(lo, zero, do2)], axis=0)
        kk = jnp.concatenate([_get(kp_ref, sl), _get(kc_ref, sl)], axis=0)
        vv = jnp.concatenate([_get(vp_ref, sl), _get(vc_ref, sl)], axis=0)
        ls = jnp.concatenate([lse[:, 2 * hp:2 * hp + 1], lse[:, 2 * hp + 1:2 * hp + 2]], axis=0)
        dh = jnp.concatenate([ddv[:, 2 * hp:2 * hp + 1], ddv[:, 2 * hp + 1:2 * hp + 2]], axis=0)
        p = jnp.exp(jnp.where(mask, _dot_nt(qs, kk) - ls, NEG))
        ds = (p * (_dot_nt(dos, vv) - dh)).astype(BF16)
        dq = _dot(ds, kk) * 0.125
        _put(dq_ref, sl, jnp.where(lo, dq[:nq], dq[nq:]).astype(BF16))
        dk = _dot_tn(ds, qs)
        dv = _dot_tn(p.astype(BF16), dos)
        _put(dkv_ref, sl, (carry[:, sl] + dk[:nq]).astype(BF16))
        _put(dkv_ref, sv, (carry[:, sv] + dv[:nq]).astype(BF16))
        carry[:, sl] = dk[nq:]
        carry[:, sv] = dv[nq:]


def _attn_bwd(branches, name):
    t = branches[0][0].shape[0]
    nb = t // ATT_BLOCK
    nbr = len(branches)
    w = ATT_WIDTH

    def body(*refs):
        ins, outs, carries = refs[:8 * nbr], refs[8 * nbr:10 * nbr], refs[10 * nbr:]
        n = pl.program_id(0)

        @pl.when(n == 0)
        def _():
            for carry in carries:
                carry[...] = jnp.zeros_like(carry)

        @pl.when(n < nb)
        def _():
            for i, branch in enumerate(branches):
                dil, seg = branch[4:]
                _attn_bwd_block(*ins[8 * i:8 * i + 8], *outs[2 * i:2 * i + 2], carries[i], (n % (nb // dil)) != 0, seg)

        @pl.when(n == nb)
        def _():
            for i in range(nbr):
                _put(outs[2 * i + 1], slice(None), carries[i][...].astype(BF16))

    in_specs, args, out_specs, out_shape = [], [], [], []
    for qkv, dout, lse, dd, dil, seg in branches:
        c0 = qkv.shape[1] // w - 3
        in_specs += [_att_spec(nb, dil, seg, w, c0 + j, back) for j, back in [(0, 0), (1, 0), (1, 1), (2, 0), (2, 1)]]
        in_specs += [_att_spec(nb, dil, seg, w, 0, 0), _att_spec(nb, dil, seg, 128, 0, 0), _att_spec(nb, dil, seg, 128, 0, 0)]
        args += [_att_view(a, nb, dil, seg) for a in [qkv] * 5 + [dout, lse, dd]]
        out_specs += [_att_spec(nb, dil, seg, w, 0, 0), _att_spec(nb, dil, seg, 2 * w, 0, 1)]
        out_shape += [_sds(_att_shape(nb, dil, seg, w), BF16), _sds(_att_shape(nb, dil, seg, 2 * w), BF16)]
    out = _call(body, name=name, grid=(nb + 1,), in_specs=in_specs, out_specs=out_specs, out_shape=out_shape,
                scratch_shapes=[pltpu.VMEM((ATT_BLOCK, 2 * w), F32)] * nbr)(*args)
    return [(out[2 * i].reshape(t, w), out[2 * i + 1].reshape(t, 2 * w)) for i in range(nbr)]


def _local_step(x, tgt, mod, norm1_g, lb_logits, og, ag, norm2_g, fg, get_w, put_g, late=lambda a: a, project=None):
    shift1, scale1, gate1, shift2, scale2, gate2 = [mod[:, i * D_MODEL:(i + 1) * D_MODEL] for i in range(6)]
    fg = fg.reshape(1, D_MODEL)

    h1 = _norm_mod(x, norm1_g, scale1, shift1, "norm_mod1")
    if project is None:
        w_in = get_w("w_in", h1)
        proj = _mm_nn(h1, w_in, "mm_in", out_dtype=BF16)
    else:
        proj, w_in = project(h1)
    o_hg, states = _hgrn_fwd(proj, lb_logits, "hgrn_fwd")
    fine = DILATIONS[-1]
    layouts = [(d, 1 if d == 1 else fine // d) for d in DILATIONS]
    qkv_fine = _to_sub(proj, fine)
    qkvs = [proj if d == 1 else qkv_fine for d in DILATIONS]
    natural = lambda a, d: a if d == 1 else _from_sub(a, fine)
    outs = _attn_fwd([(q, d, seg) for q, (d, seg) in zip(qkvs, layouts)], "attn_fwd")
    att, lse, lse_fine, mixin = _combine_mix_in([o for o, _ in outs], [l for _, l in outs], fine,
                                                o_hg, proj, og, ag, "attn_combine_mix_in")
    w_out = get_w("w_out", mixin)
    mix, x2, h2 = _out_resid_norm_mod(x, mixin, w_out, gate1, norm2_g, scale2, shift2, "mm_out_resid_norm_mod2")
    w_gu = get_w("w_gu", h2)
    a_ff, u_ff, act = _mm_gate_up(h2, w_gu, "mm_gu")
    w_down = get_w("w_down", act)
    dx3, dffn, loss_v, dfg, dgate2 = _down_loss(x2, act, w_down, gate2, fg, tgt, "mm_down_loss")

    dffn = put_g("w_down", *_mm_tn(act, dffn, 1, "mm_down_dw", tm=2048, tk=D_FF // 2), dffn)
    dau = _mm_down_dx(dffn, w_down, a_ff, u_ff, "mm_down_dx")
    dau = put_g("w_gu", *_mm_tn(h2, dau, N_SHARD, "mm_gu_dw", tm=x.shape[0], tk=512), dau)
    dx2, dshift2, dscale2, dg2, dgate1, dmix = _norm_mod_bwd(
        dau, x2, norm2_g, scale2, dx3, "mm_gu_dx_norm_bwd", gate=gate1, mix=mix, w=w_gu)
    dmix = put_g("w_out", *_mm_tn(mixin, dmix, 1, "mm_out_dw", tm=x.shape[0], tk=512), dmix)
    do_hg, dg_raw, datt, dd, datt_fine, dd_fine, dog, dag = _mix_in_bwd(
        dmix, w_out, o_hg, proj, att, og, ag, fine, "mm_out_dx_mix_in_bwd")
    datts = _attn_bwd([(q,) + ((datt, lse, dd) if d == 1 else (datt_fine, lse_fine, dd_fine)) + (d, seg)
                       for q, (d, seg) in zip(qkvs, layouts)], "attn_bwd")
    dhg, dlb = _hgrn_bwd(proj, lb_logits, states, do_hg, "hgrn_bwd")
    dhg = late(dhg)
    dproj = _dproj(dhg, dg_raw, [natural(dq, d) for (dq, _), d in zip(datts, DILATIONS)],
                   [natural(dkv, d) for (_, dkv), d in zip(datts, DILATIONS)], "dproj")
    dproj = put_g("w_in", *_mm_tn(h1, dproj, N_SHARD, "mm_in_dw", tm=x.shape[0], tk=512), dproj)
    dx, dshift1, dscale1, dg1 = _norm_mod_bwd(dproj, x, norm1_g, scale1, dx2, "mm_in_dx_norm_bwd", w=w_in)

    stats = jnp.concatenate([loss_v, dfg, dg2, dg1, dlb, dag, dog,
                             dshift1, dscale1, dgate1, dshift2, dscale2, dgate2], axis=1)
    return dx, stats


def _place():
    x, y, c = lax.axis_index("x"), lax.axis_index("y"), lax.axis_index("c")
    return x, y, c


def _chip_peers(x, y, c):
    return [(1 - x, y, c), (x, 1 - y, c), (1 - x, 1 - y, c)]


def _comm_call(body, name, n_in, out_shape, scratch_shapes):
    hbm = pl.BlockSpec(memory_space=pl.ANY)
    return pl.pallas_call(body, name=name, in_specs=[hbm] * n_in, out_specs=[hbm] * len(out_shape),
                          out_shape=out_shape, scratch_shapes=scratch_shapes)


_HBM = pl.BlockSpec(memory_space=pltpu.HBM)
_SEM = pl.BlockSpec(memory_space=pltpu.SEMAPHORE)
_EFFECT = pltpu.SideEffectType.DATAFLOW_SIDE_EFFECTING


def _exchange_copy(bufs, send, recv, j, peer, place, kind):
    x, y, c = place
    target = peer
    if kind == "gather":
        src = dst = bufs[0].at[2 * x + y]
    elif kind == "scatter":
        src, dst = bufs[0].at[2 * peer[0] + peer[1]], bufs[1].at[j]
    else:
        half = bufs[0].shape[1] // 2
        rows = pl.ds(c * half, half)
        if kind == "half":
            src = dst = bufs[0].at[2 * x + y, rows]
        else:
            src = dst = bufs[0].at[2 * peer[0] + peer[1], rows]
            target = (x, y, 1 - c)
    return pltpu.make_async_remote_copy(src_ref=src, dst_ref=dst, send_sem=send.at[j], recv_sem=recv.at[j],
                                        device_id=target, device_id_type=MESH)


def _exchange_start(groups, after, kind, name):
    sizes = [len(g) for g in groups]
    flat = [b for g in groups for b in g]
    ng, nb = len(groups), len(flat)

    def body(*refs):
        bufs, sems = refs[:nb], refs[nb + 1:nb + 1 + 2 * ng]
        x, y, c = _place()
        for j, peer in enumerate(_chip_peers(x, y, c)):
            at = 0
            for i, size in enumerate(sizes):
                _exchange_copy(bufs[at:at + size], sems[2 * i], sems[2 * i + 1], j, peer, (x, y, c), kind).start()
                at += size

    any_space = pl.BlockSpec(memory_space=pl.ANY)
    out = pl.pallas_call(
        body, name=name, in_specs=[_HBM] * nb + [any_space],
        out_specs=[_SEM] * (2 * ng) + [_HBM] * nb + [any_space],
        out_shape=[pltpu.SemaphoreType.DMA((3,))] * (2 * ng) + [pltpu.HBM(b.shape, b.dtype) for b in flat]
        + [_sds(after.shape, after.dtype)],
        input_output_aliases={i: 2 * ng + i for i in range(nb + 1)},
        compiler_params=pltpu.CompilerParams(has_side_effects=_EFFECT),
    )(*[pltpu.with_memory_space_constraint(b, pltpu.HBM) for b in flat], after)
    started, at = [], 2 * ng
    for i, size in enumerate(sizes):
        started.append((out[2 * i], out[2 * i + 1], tuple(out[at:at + size])))
        at += size
    return started, out[-1]


def _exchange_wait(started, after, kind, name):
    send, recv, bufs = started
    nb = len(bufs)

    def body(*refs):
        x, y, c = _place()
        for j, peer in enumerate(_chip_peers(x, y, c)):
            cp = _exchange_copy(refs[:nb], refs[nb], refs[nb + 1], j, peer, (x, y, c), kind)
            cp.wait_send()
            cp.wait_recv()

    return pl.pallas_call(
        body, name=name, in_specs=[_HBM] * nb + [_SEM, _SEM, pl.BlockSpec(memory_space=pl.ANY)],
        out_specs=[_HBM] * nb, out_shape=[pltpu.HBM(b.shape, b.dtype) for b in bufs],
        input_output_aliases={i: i for i in range(nb)},
        compiler_params=pltpu.CompilerParams(has_side_effects=_EFFECT),
    )(*bufs, send, recv, after)


def _sibling_copies(v_refs, l_refs, send, recv):
    x, y, c = _place()
    return [pltpu.make_async_remote_copy(src_ref=v, dst_ref=l, send_sem=send.at[a], recv_sem=recv.at[a],
                                         device_id=(x, y, 1 - c), device_id_type=MESH)
            for a, (v, l) in enumerate(zip(v_refs, l_refs))]


def _sibling_start(vs, after, name):
    vs = list(vs)
    n = len(vs)
    lands = [lax.empty(v.shape, v.dtype) for v in vs]

    def body(*refs):
        for cp in _sibling_copies(refs[:n], refs[n:2 * n], refs[2 * n + 1], refs[2 * n + 2]):
            cp.start()

    any_space = pl.BlockSpec(memory_space=pl.ANY)
    out = pl.pallas_call(
        body, name=name, in_specs=[_HBM] * (2 * n) + [any_space],
        out_specs=[_SEM, _SEM] + [_HBM] * (2 * n) + [any_space],
        out_shape=[pltpu.SemaphoreType.DMA((n,))] * 2 + [pltpu.HBM(b.shape, b.dtype) for b in vs + lands]
        + [_sds(after.shape, after.dtype)],
        input_output_aliases={i: 2 + i for i in range(2 * n + 1)},
        compiler_params=pltpu.CompilerParams(has_side_effects=_EFFECT),
    )(*[pltpu.with_memory_space_constraint(b, pltpu.HBM) for b in vs + lands], after)
    return (out[0], out[1], tuple(out[2:2 + n]), tuple(out[2 + n:2 + 2 * n])), out[-1]


def _sibling_wait(started, after, name):
    send, recv, vs, lands = started
    n = len(vs)

    def body(*refs):
        for cp in _sibling_copies(refs[:n], refs[n:2 * n], refs[2 * n], refs[2 * n + 1]):
            cp.wait_send()
            cp.wait_recv()

    out = pl.pallas_call(
        body, name=name, in_specs=[_HBM] * (2 * n) + [_SEM, _SEM, pl.BlockSpec(memory_space=pl.ANY)],
        out_specs=[_HBM] * (2 * n), out_shape=[pltpu.HBM(b.shape, b.dtype) for b in vs + lands],
        input_output_aliases={i: i for i in range(2 * n)},
        compiler_params=pltpu.CompilerParams(has_side_effects=_EFFECT),
    )(*vs, *lands, send, recv, after)
    return out[:n], out[n:]


def _swap_sibling(vs, name):
    n = len(vs)

    def body(*refs):
        v_refs, o_refs, (send, recv) = refs[:n], refs[n:2 * n], refs[2 * n:]
        x, y, c = _place()
        cps = [pltpu.make_async_remote_copy(
            src_ref=v_refs[a], dst_ref=o_refs[a], send_sem=send.at[a], recv_sem=recv.at[a],
            device_id=(x, y, 1 - c), device_id_type=MESH) for a in range(n)]
        for cp in cps:
            cp.start()
        for cp in cps:
            cp.wait()

    return _comm_call(body, name, n, [_sds(v.shape, v.dtype) for v in vs],
                      [pltpu.SemaphoreType.DMA((n,)), pltpu.SemaphoreType.DMA((n,))])(*vs)


def _everyone(x, y, c):
    return [(1 - x if k & 4 else x, 1 - y if k & 2 else y, 1 - c if k & 1 else c) for k in range(1, 8)]


def _all_gather_copies(land_ref, send, recv, arriving):
    x, y, c = _place()
    me = 4 * x + 2 * y + c
    return [pltpu.make_async_remote_copy(
        src_ref=land_ref.at[me], dst_ref=land_ref.at[4 * p[0] + 2 * p[1] + p[2] if arriving else me],
        send_sem=send.at[k], recv_sem=recv.at[k], device_id=p, device_id_type=MESH)
        for k, p in enumerate(_everyone(x, y, c))]


def _all_gather_start(v, name):
    x, y, c = _place()
    land = lax.dynamic_update_slice(lax.empty((8,) + v.shape, v.dtype), v[None], (4 * x + 2 * y + c, 0, 0))

    def body(land_ref, v_ref, send, recv, land_out, v_out):
        for cp in _all_gather_copies(land_ref, send, recv, False):
            cp.start()

    any_space = pl.BlockSpec(memory_space=pl.ANY)
    out = pl.pallas_call(
        body, name=name, in_specs=[_HBM, any_space], out_specs=[_SEM, _SEM, _HBM, any_space],
        out_shape=[pltpu.SemaphoreType.DMA((7,))] * 2 + [pltpu.HBM(land.shape, land.dtype), _sds(v.shape, v.dtype)],
        input_output_aliases={0: 2, 1: 3}, compiler_params=pltpu.CompilerParams(has_side_effects=_EFFECT),
    )(pltpu.with_memory_space_constraint(land, pltpu.HBM), v)
    return tuple(out[:3]), out[3]


def _all_gather_wait(started, after, name):
    send, recv, land = started

    def body(land_ref, send, recv, after_ref, land_out):
        for cp in _all_gather_copies(land_ref, send, recv, True):
            cp.wait_send()
            cp.wait_recv()

    return pl.pallas_call(
        body, name=name, in_specs=[_HBM, _SEM, _SEM, pl.BlockSpec(memory_space=pl.ANY)], out_specs=_HBM,
        out_shape=pltpu.HBM(land.shape, land.dtype), input_output_aliases={0: 0},
        compiler_params=pltpu.CompilerParams(has_side_effects=_EFFECT),
    )(land, send, recv, after)


def _cast_place(ws, shard, name):
    n = len(ws)

    def body(s_ref, *refs):
        for w_ref, o_ref in zip(refs[:n], refs[n:]):
            o_ref[0] = w_ref[...].astype(BF16)

    return pl.pallas_call(
        body, name=name, out_shape=[_sds((N_SHARD,) + w.shape, BF16) for w in ws],
        grid_spec=pltpu.PrefetchScalarGridSpec(
            num_scalar_prefetch=1, grid=(4,),
            in_specs=[pl.BlockSpec((w.shape[0] // 4, w.shape[1]), lambda i, s: (i, 0)) for w in ws],
            out_specs=[pl.BlockSpec((1, w.shape[0] // 4, w.shape[1]), lambda i, s: (s[0], i, 0)) for w in ws]),
        compiler_params=pltpu.CompilerParams(dimension_semantics=("arbitrary",), vmem_limit_bytes=VMEM_LIMIT),
    )(shard.reshape(1).astype(jnp.int32), *ws)


def _mod_rows(c8, w_ada, b_ada, name):
    n = w_ada.shape[1]

    def gather(src_ref, dst_ref, send, recv, loc, base):
        x, y, c = _place()
        me = 4 * x + 2 * y + c
        own = pltpu.make_async_copy(src_ref, dst_ref.at[me], loc)
        own.start()
        peers = _everyone(x, y, c)
        sends = [pltpu.make_async_remote_copy(src_ref=src_ref, dst_ref=dst_ref.at[me], send_sem=send.at[base + k],
                                              recv_sem=recv.at[base + k], device_id=p, device_id_type=MESH)
                 for k, p in enumerate(peers)]
        for cp in sends:
            cp.start()
        for k, p in enumerate(peers):
            pltpu.make_async_remote_copy(src_ref=src_ref, dst_ref=dst_ref.at[4 * p[0] + 2 * p[1] + p[2]],
                                         send_sem=send.at[base + k], recv_sem=recv.at[base + k], device_id=p,
                                         device_id_type=MESH).wait_recv()
        for cp in sends:
            cp.wait_send()
        own.wait()

    def body(c_ref, w_ref, b_ref, a_ref, parts_ref, c_all, part, send, recv, loc):
        gather(c_ref, c_all, send, recv, loc.at[0], 0)
        cv = jnp.max(c_all[...], axis=1)
        ca = cv * _sigmoid(cv)
        a_ref[...] = ca
        part[...] = jnp.dot(ca, w_ref[...], precision=lax.Precision.HIGHEST, preferred_element_type=F32) + b_ref[...]
        gather(part, parts_ref, send, recv, loc.at[1], 7)

    vmem = pl.BlockSpec(memory_space=pltpu.VMEM)
    return pl.pallas_call(
        body, name=name, in_specs=[vmem] * 3, out_specs=[vmem, vmem],
        out_shape=[_sds((8, D_MODEL)), _sds((8, 8, n))],
        scratch_shapes=[pltpu.VMEM((8, 8, D_MODEL), F32), pltpu.VMEM((8, n), F32), pltpu.SemaphoreType.DMA((14,)),
                        pltpu.SemaphoreType.DMA((14,)), pltpu.SemaphoreType.DMA((2,))],
        compiler_params=pltpu.CompilerParams(vmem_limit_bytes=VMEM_LIMIT))(c8, w_ada, b_ada)


def _sum_received(gs, shard, lands, name):
    n = len(gs)

    def body(s_ref, *refs):
        for g_ref, l_ref, o_ref in zip(refs[:n], refs[n:2 * n], refs[2 * n:]):
            o_ref[...] = ((g_ref[0] + l_ref[0].astype(F32)) + l_ref[1].astype(F32)) + l_ref[2].astype(F32)

    quarter = lambda g: (g.shape[1] // 4, g.shape[2])
    return pl.pallas_call(
        body, name=name, out_shape=[_sds(g.shape[1:]) for g in gs],
        grid_spec=pltpu.PrefetchScalarGridSpec(
            num_scalar_prefetch=1, grid=(4,),
            in_specs=[pl.BlockSpec((1,) + quarter(g), lambda i, s: (s[0], i, 0)) for g in gs]
            + [pl.BlockSpec((3,) + quarter(g), lambda i, s: (0, i, 0)) for g in gs],
            out_specs=[pl.BlockSpec(quarter(g), lambda i, s: (i, 0)) for g in gs]),
        compiler_params=pltpu.CompilerParams(dimension_semantics=("arbitrary",), vmem_limit_bytes=VMEM_LIMIT),
    )(shard.reshape(1).astype(jnp.int32), *gs, *lands)


def _adamw_outer(w, ct, dm, m, v, name):
    k, n = w.shape
    tr = k // 4

    def body(w_ref, c_ref, d_ref, m_ref, v_ref, g_out, d_out, m_out, v_out):
        cv = c_ref[...]
        dv = d_ref[...]
        g = cv[:, 0:1] * dv[0:1, :]
        for i in range(1, 8):
            g = g + cv[:, i:i + 1] * dv[i:i + 1, :]
        g_out[...] = g
        d_out[...], m_out[...], v_out[...] = _adamw_math(w_ref[...], g, m_ref[...], v_ref[...])

    row = _rows(tr, n)
    return _call(body, name=name, grid=(4,),
                 in_specs=[row, _rows(tr, 8), pl.BlockSpec((8, n), lambda i: (0, 0)), row, row],
                 out_specs=[row] * 4, out_shape=[_sds((k, n))] * 4)(w, ct, dm, m, v)


def _adamw_math(w, g, m, v):
    m_new = ADAM_B1 * m + (1.0 - ADAM_B1) * g
    v_new = ADAM_B2 * v + (1.0 - ADAM_B2) * (g * g)
    m_hat = m_new / (1.0 - ADAM_B1 ** ADAM_STEP)
    v_hat = v_new / (1.0 - ADAM_B2 ** ADAM_STEP)
    return -ADAM_LR * (m_hat / (jnp.sqrt(v_hat) + ADAM_EPS) + ADAM_WD * w), m_new, v_new


def _small_update(stats, smalls, name):
    offsets = [ST_DMOD, ST_DG1, ST_DLB, ST_DOG, ST_DAG, ST_DG2, ST_DFG]
    lb_index = 2

    def body(*refs):
        s_ref, ins, l_ref, outs = refs[0], refs[1:22], refs[22], refs[23:]
        tot = s_ref[0:1, :]
        for i in range(1, 8):
            tot = tot + s_ref[i:i + 1, :]
        l_ref[...] = jnp.zeros((1, 128), F32) + (0.5 / D_MODEL) * jnp.sum(tot[:, ST_LOSS:ST_LOSS + D_MODEL])
        for p, off in enumerate(offsets):
            w_ref, m_ref, v_ref = ins[3 * p:3 * p + 3]
            g_out, d_out, m_out, v_out = outs[4 * p:4 * p + 4]
            g = tot[:, off:off + w_ref.shape[1]]
            if p == lb_index:
                lg = w_ref[...]
                lb = _sigmoid(lg[0:1] - lg[1:2])
                g = g * lb * (1.0 - lb)
            for r in range(w_ref.shape[0]):
                rows = slice(r, r + 1)
                gr = g if r == 0 else -g
                delta, m_new, v_new = _adamw_math(w_ref[rows, :], gr, m_ref[rows, :], v_ref[rows, :])
                g_out[rows, :] = gr
                d_out[rows, :] = delta
                m_out[rows, :] = m_new
                v_out[rows, :] = v_new

    full = lambda a: pl.BlockSpec(a.shape, lambda i: (0, 0))
    flat = [a for t in smalls for a in t]
    return _call(body, name=name, grid=(1,),
                 in_specs=[full(stats)] + [full(a) for a in flat],
                 out_specs=[pl.BlockSpec((1, 128), lambda i: (0, 0))] + [full(t[0]) for t in smalls for _ in range(4)],
                 out_shape=[_sds((1, 128))] + [_sds(t[0].shape) for t in smalls for _ in range(4)])(stats, *flat)


def _adamw(params, name):
    n = len(params)

    def body(*refs):
        for p in range(n):
            w_ref, ga_ref, gb_ref, m_ref, v_ref = refs[5 * p:5 * p + 5]
            g_out, d_out, m_out, v_out = refs[5 * n + 4 * p:5 * n + 4 * p + 4]
            g = ga_ref[...] + gb_ref[...]
            g_out[...] = g
            d_out[...], m_out[...], v_out[...] = _adamw_math(w_ref[...], g, m_ref[...], v_ref[...])

    row = lambda w: _rows(w.shape[0] // 4, w.shape[1])
    out = _call(body, name=name, grid=(4,), in_specs=[row(p[0]) for p in params for _ in range(5)],
                out_specs=[row(p[0]) for p in params for _ in range(4)],
                out_shape=[_sds(p[0].shape) for p in params for _ in range(4)])(*[a for p in params for a in p])
    return [tuple(out[4 * p:4 * p + 4]) for p in range(n)]


def kernel(x, c, w_ada, b_ada, norm1_g, w_in, hg_lb_logits, hg_onorm_g, att_onorm_g, w_out, norm2_g, w_gate_up, w_down, final_g, loss_target, m_w_ada, m_b_ada, m_norm1_g, m_w_in, m_hg_lb_logits, m_hg_onorm_g, m_att_onorm_g, m_w_out, m_norm2_g, m_w_gate_up, m_w_down, m_final_g, v_w_ada, v_b_ada, v_norm1_g, v_w_in, v_hg_lb_logits, v_hg_onorm_g, v_att_onorm_g, v_w_out, v_norm2_g, v_w_gate_up, v_w_down, v_final_g):
    ix, iy, ic = _place()
    shard = 2 * ix + iy
    sample = 4 * ix + 2 * iy + ic
    n_ada = w_ada.shape[2]

    shards = [w_in[0], w_out[0], w_gate_up[0], w_down[0]]
    names = ["w_in", "w_out", "w_gu", "w_down"]
    shapes = [(N_SHARD,) + w.shape for w in shards]
    placed = [(_cast_place(shards[:1], shard, "place_w_in")[0],)]
    placed += [(p,) for p in _cast_place(shards[1:], shard, "place_rest")]

    b_part = lax.dynamic_slice(b_ada, (0, shard * n_ada), (1, n_ada))
    c_act, parts = _mod_rows(jnp.broadcast_to(c, (8, D_MODEL)), w_ada[0], b_part, "mod_rows")
    parts = parts[::2]
    mod = lax.dynamic_index_in_dim(parts, sample, axis=1, keepdims=False).reshape(1, 6 * D_MODEL)
    (first,), mod = _exchange_start(placed[:1], mod, "half", "gather_start_w_in")
    gathering = {}

    def get_w(name, after):
        if name == "w_in":
            halves = _exchange_wait(first, after, "half", "gather_wait_w_in")
            (passing,), token = _exchange_start([tuple(halves)], mod, "forward", "forward_start_w_in")
            (full,) = _exchange_wait(passing, token, "forward", "forward_wait_w_in")
            rest, full = _exchange_start(placed[1:], full, "gather", "gather_start_rest")
            gathering.update(zip(names[1:], rest))
            return full
        (full,) = _exchange_wait(gathering[name], after, "gather", "gather_wait_" + name)
        return full if name == "w_gu" else full.reshape(1, -1, D_MODEL)

    scattering = {}

    def put_g(name, g, g_bf16, then):
        shape = shapes[names.index(name)]
        land = lax.empty((3,) + shape[1:], BF16)
        (started,), then = _exchange_start([(g_bf16.reshape(shape), land)], then, "scatter", "scatter_start_" + name)
        scattering[name] = (g.reshape(shape), started)
        return then

    def summed(group, after, tag):
        lands = [_exchange_wait(scattering[nm][1], after, "scatter", "scatter_wait_" + nm)[1] for nm in group]
        return _sum_received([scattering[nm][0] for nm in group], shard, lands, "sum_" + tag)

    early = ["w_down", "w_gu", "w_out"]
    swapping = []

    def late(a):
        started, a = _sibling_start(summed(early, a, "early"), a, "swap_start")
        swapping.append(started)
        return a

    def project(h1):
        own = _mm_own_shard(h1, shards[0], shard, N_SHARD, "mm_in_own")
        w_full = get_w("w_in", own)
        return _mm_other_shards(h1, w_full, own, shard, "mm_in_rest"), w_full

    dx, stats = _local_step(x[0], loss_target[0], mod, norm1_g, hg_lb_logits, hg_onorm_g, att_onorm_g,
                            norm2_g, final_g, get_w, put_g, late, project)

    gathering_stats, stats = _all_gather_start(stats, "stats_start")
    moments = [(m_w_in, v_w_in), (m_w_out, v_w_out), (m_w_gate_up, v_w_gate_up), (m_w_down, v_w_down)]

    def update(group, sums, other, tag):
        params = [(shards[names.index(nm)], s, o, moments[names.index(nm)][0][0], moments[names.index(nm)][1][0])
                  for nm, s, o in zip(group, sums, other)]
        return dict(zip(group, _adamw(params, "adamw_" + tag)))

    sums, other = _sibling_wait(swapping[0], stats, "swap_wait")
    done = update(early, sums, other, "early")
    sum_in = summed(["w_in"], done["w_out"][1], "w_in")
    done.update(update(["w_in"], sum_in, _swap_sibling(sum_in, "swap_sum_in"), "w_in"))

    stats_all = _all_gather_wait(gathering_stats, done["w_in"][1], "stats_wait").reshape(8, ST_WIDTH)
    dmod = lax.dynamic_slice(stats_all, (0, ST_DMOD + shard * n_ada), (8, n_ada))

    as_row = lambda a: a.reshape(1, -1) if a.ndim == 1 else a
    smalls = [tuple(as_row(a) for a in t) for t in [
        (b_ada, m_b_ada, v_b_ada), (norm1_g, m_norm1_g, v_norm1_g),
        (hg_lb_logits, m_hg_lb_logits, v_hg_lb_logits), (hg_onorm_g, m_hg_onorm_g, v_hg_onorm_g),
        (att_onorm_g, m_att_onorm_g, v_att_onorm_g), (norm2_g, m_norm2_g, v_norm2_g),
        (final_g, m_final_g, v_final_g)]]
    loss, *small_out = _small_update(stats_all, smalls, "small_update")
    shapes_out = [b_ada.shape, norm1_g.shape, hg_lb_logits.shape, hg_onorm_g.shape, att_onorm_g.shape,
                  norm2_g.shape, final_g.shape]
    sg, sd, sm, sv = [[small_out[4 * p + i].reshape(shapes_out[p]) for p in range(7)] for i in range(4)]

    ada = _adamw_outer(w_ada[0], c_act.T, dmod, m_w_ada[0], v_w_ada[0], "adamw_w_ada")
    big = [ada] + [done[nm] for nm in names]
    bg, bd, bm, bv = [[t[i][None] for t in big] for i in range(4)]

    def order(b, s):
        return [b[0], s[0], s[1], b[1], s[2], s[3], s[4], b[2], s[5], b[3], b[4], s[6]]

    return (loss[0, 0], dx[None], *order(bg, sg), *order(bd, sd), *order(bm, sm), *order(bv, sv))
```

```python
import functools

import jax
import jax.numpy as jnp
from jax import lax
from jax.experimental import pallas as pl
from jax.experimental.pallas import tpu as pltpu

F32 = jnp.float32
BF16 = jnp.bfloat16
MESH = pl.DeviceIdType.MESH

D_MODEL = 1024
HG_WIDTH = 512
HG_HEAD = 128
HG_CHUNK = 64
HG_GROUP = 4
ATT_WIDTH = 512
ATT_HEADS = 8
ATT_BLOCK = 128
DILATIONS = (1, 4, 16)
D_FF = 2816
IN_WIDTH = 3584
N_SHARD = 4
RMS_EPS = 1e-6
NEG = -1e30

ADAM_LR = 0.001
ADAM_B1 = 0.9
ADAM_B2 = 0.999
ADAM_EPS = 1e-08
ADAM_WD = 0.01
ADAM_STEP = 10

VMEM_LIMIT = 56 * 2**20

ST_LOSS, ST_DFG, ST_DG2, ST_DG1 = 0, 1024, 2048, 3072
ST_DLB, ST_DAG, ST_DOG, ST_DMOD = 4096, 4608, 5120, 5248
ST_WIDTH = 5248 + 6144


def _call(body, *, name, grid, in_specs, out_specs, out_shape, scratch_shapes=()):
    return pl.pallas_call(
        body, name=name, grid=grid, in_specs=in_specs, out_specs=out_specs, out_shape=out_shape,
        scratch_shapes=list(scratch_shapes),
        compiler_params=pltpu.CompilerParams(
            dimension_semantics=("arbitrary",) * len(grid), vmem_limit_bytes=VMEM_LIMIT))


def _sds(shape, dtype=F32):
    return jax.ShapeDtypeStruct(shape, dtype)


def _dot(a, b):
    return jnp.dot(a, b, preferred_element_type=F32)


def _dot_nt(a, b):
    return lax.dot_general(a, b, (((1,), (1,)), ((), ())), preferred_element_type=F32)


def _dot_tn(a, b):
    return lax.dot_general(a, b, (((0,), (0,)), ((), ())), preferred_element_type=F32)


def _sigmoid(x):
    return 1.0 / (1.0 + jnp.exp(-x))


def _rows(tr, width):
    return pl.BlockSpec((tr, width), lambda i: (i, 0))


def _vec(width):
    return pl.BlockSpec((1, width), lambda i: (0, 0))


def _acc(ref, val, first):
    @pl.when(first)
    def _():
        ref[...] = val

    @pl.when(jnp.logical_not(first))
    def _():
        ref[...] += val


def _sub_rows(tr, fine, width):
    return pl.BlockSpec((fine, tr // fine, width), lambda i: (0, i, 0))


def _regroup_matrix(tr, groups):
    a = lax.broadcasted_iota(jnp.int32, (tr, tr), 0)
    b = lax.broadcasted_iota(jnp.int32, (tr, tr), 1)
    return (b == (a % groups) * (tr // groups) + a // groups).astype(BF16)


def _regroup(m, v):
    if v.dtype == BF16:
        return _dot(m, v)
    out = None
    for _ in range(3):
        part = v.astype(BF16)
        v = v - part.astype(F32)
        out = _dot(m, part) if out is None else out + _dot(m, part)
    return out


def _mm_nn(a, b3, name, tm=1024, out_dtype=F32):
    m, k = a.shape
    s, _, n = b3.shape

    def body(a_ref, b_ref, o_ref):
        o_ref[...] = _dot(a_ref[...], b_ref[0]).astype(out_dtype)

    return _call(
        body, name=name, grid=(s, m // tm),
        in_specs=[pl.BlockSpec((tm, k), lambda j, i: (i, 0)), pl.BlockSpec((1, k, n), lambda j, i: (j, 0, 0))],
        out_specs=pl.BlockSpec((tm, n), lambda j, i: (i, j)), out_shape=_sds((m, s * n), out_dtype))(a, b3)


def _mm_own_shard(a, w, shard, s, name, tm=1024):
    m, k = a.shape
    n = w.shape[1]

    def body(s_ref, a_ref, w_ref, o_ref):
        o_ref[...] = _dot(a_ref[...], w_ref[...].astype(BF16)).astype(BF16)

    return pl.pallas_call(
        body, name=name, out_shape=_sds((m, s * n), BF16),
        grid_spec=pltpu.PrefetchScalarGridSpec(
            num_scalar_prefetch=1, grid=(m // tm,),
            in_specs=[pl.BlockSpec((tm, k), lambda i, sh: (i, 0)), pl.BlockSpec((k, n), lambda i, sh: (0, 0))],
            out_specs=pl.BlockSpec((tm, n), lambda i, sh: (i, sh[0]))),
        compiler_params=pltpu.CompilerParams(dimension_semantics=("arbitrary",), vmem_limit_bytes=VMEM_LIMIT),
    )(shard.reshape(1).astype(jnp.int32), a, w)


def _mm_other_shards(a, b3, partial, shard, name, tm=1024):
    m, k = a.shape
    s, _, n = b3.shape
    which = lambda j, sh: (sh[0] + 1 + j) % s

    def body(s_ref, a_ref, b_ref, p_ref, o_ref):
        o_ref[...] = _dot(a_ref[...], b_ref[0]).astype(BF16)

    return pl.pallas_call(
        body, name=name, out_shape=_sds(partial.shape, BF16),
        grid_spec=pltpu.PrefetchScalarGridSpec(
            num_scalar_prefetch=1, grid=(s - 1, m // tm),
            in_specs=[pl.BlockSpec((tm, k), lambda j, i, sh: (i, 0)),
                      pl.BlockSpec((1, k, n), lambda j, i, sh: (which(j, sh), 0, 0)),
                      pl.BlockSpec(memory_space=pl.ANY)],
            out_specs=pl.BlockSpec((tm, n), lambda j, i, sh: (i, which(j, sh)))),
        input_output_aliases={3: 0},
        compiler_params=pltpu.CompilerParams(dimension_semantics=("arbitrary",) * 2, vmem_limit_bytes=VMEM_LIMIT),
    )(shard.reshape(1).astype(jnp.int32), a, b3, partial)


def _mm_tn(a, dy, s, name, tm, tk):
    m, k = a.shape
    n = dy.shape[1] // s
    steps = m // tm

    def body(a_ref, dy_ref, o_ref, ob_ref):
        p = _dot_tn(a_ref[...], dy_ref[...])[None]
        if steps == 1:
            o_ref[...] = p
            ob_ref[...] = p.astype(BF16)
        else:
            i = pl.program_id(2)
            _acc(o_ref, p, i == 0)

            @pl.when(i == steps - 1)
            def _():
                ob_ref[...] = o_ref[...].astype(BF16)

    out = pl.BlockSpec((1, tk, n), lambda kk, j, i: (j, kk, 0))
    return _call(
        body, name=name, grid=(k // tk, s, steps),
        in_specs=[pl.BlockSpec((tm, tk), lambda kk, j, i: (i, kk)), pl.BlockSpec((tm, n), lambda kk, j, i: (i, j))],
        out_specs=[out, out], out_shape=[_sds((s, k, n)), _sds((s, k, n), BF16)])(a, dy)


def _rms(x):
    return lax.rsqrt(jnp.mean(x * x, axis=-1, keepdims=True) + RMS_EPS)


def _rms_bwd(dxh, xh, r):
    return r * (dxh - xh * jnp.mean(dxh * xh, axis=-1, keepdims=True))


def _norm_mod(x, g, scale, shift, name, tr=512):
    t = x.shape[0]

    def body(x_ref, g_ref, sc_ref, sh_ref, h_ref):
        xv = x_ref[...]
        n = xv * _rms(xv) * g_ref[...]
        h_ref[...] = (n * (1.0 + sc_ref[...]) + sh_ref[...]).astype(BF16)

    return _call(body, name=name, grid=(t // tr,),
                 in_specs=[_rows(tr, D_MODEL), _vec(D_MODEL), _vec(D_MODEL), _vec(D_MODEL)],
                 out_specs=_rows(tr, D_MODEL), out_shape=_sds((t, D_MODEL), BF16))(x, g, scale, shift)


def _out_resid_norm_mod(x, mixin, w_out, gate, g, scale, shift, name, tr=512):
    t = x.shape[0]

    def body(x_ref, mi_ref, w_ref, gt_ref, g_ref, sc_ref, sh_ref, m_ref, x2_ref, h_ref):
        mix = _dot(mi_ref[...], w_ref[0])
        m_ref[...] = mix
        x2 = x_ref[...] + gt_ref[...] * mix
        x2_ref[...] = x2
        n = x2 * _rms(x2) * g_ref[...]
        h_ref[...] = (n * (1.0 + sc_ref[...]) + sh_ref[...]).astype(BF16)

    row = _rows(tr, D_MODEL)
    return _call(body, name=name, grid=(t // tr,),
                 in_specs=[row, row, pl.BlockSpec(w_out.shape, lambda i: (0, 0, 0))] + [_vec(D_MODEL)] * 4,
                 out_specs=[row, row, row],
                 out_shape=[_sds((t, D_MODEL)), _sds((t, D_MODEL)), _sds((t, D_MODEL), BF16)])(
                     x, mixin, w_out, gate, g, scale, shift)


def _mm_gate_up(h, w_gu, name, tm=1024):
    m, k = h.shape
    n = w_gu.shape[2]

    def body(h_ref, wa_ref, wu_ref, da_ref, du_ref, o_ref):
        hv = h_ref[...]
        a = _dot(hv, wa_ref[0])
        u = _dot(hv, wu_ref[0])
        sg = _sigmoid(a)
        silu = a * sg
        da_ref[...] = (u * sg * (1.0 + a * (1.0 - sg))).astype(BF16)
        du_ref[...] = silu.astype(BF16)
        o_ref[...] = (silu * u).astype(BF16)

    out = pl.BlockSpec((tm, n), lambda j, i: (i, j))
    return _call(body, name=name, grid=(2, m // tm),
                 in_specs=[pl.BlockSpec((tm, k), lambda j, i: (i, 0)), pl.BlockSpec((1, k, n), lambda j, i: (j, 0, 0)),
                           pl.BlockSpec((1, k, n), lambda j, i: (j + 2, 0, 0))],
                 out_specs=[out, out, out], out_shape=[_sds((m, 2 * n), BF16)] * 3)(h, w_gu, w_gu)


def _mm_down_dx(dffn, w_down, act_da, act_du, name, tm=512):
    m = dffn.shape[0]
    _, k, n = w_down.shape

    def body(d_ref, w_ref, da_ref, du_ref, o_ref):
        dact = _dot_nt(d_ref[...], w_ref[0])
        o_ref[:, :k] = (dact * da_ref[...].astype(F32)).astype(BF16)
        o_ref[:, k:] = (dact * du_ref[...].astype(F32)).astype(BF16)

    return _call(body, name=name, grid=(m // tm,),
                 in_specs=[_rows(tm, n), pl.BlockSpec((1, k, n), lambda i: (0, 0, 0)), _rows(tm, k), _rows(tm, k)],
                 out_specs=_rows(tm, 2 * k), out_shape=_sds((m, 2 * k), BF16))(dffn, w_down, act_da, act_du)


def _down_loss(x2, act, w_down, gate, fg, tgt, name, tr=512):
    t = x2.shape[0]
    _, k, n = w_down.shape

    def body(x_ref, a_ref, w_ref, gt_ref, fg_ref, t_ref, dx_ref, df_ref, l_ref, dfg_ref, dgt_ref):
        first = pl.program_id(0) == 0
        ffn_v = _dot(a_ref[...], w_ref[0])
        x3 = x_ref[...] + gt_ref[...] * ffn_v
        r = _rms(x3)
        xh = x3 * r
        err = xh * fg_ref[...] - t_ref[...]
        dy = err * (1.0 / D_MODEL)
        dx3 = _rms_bwd(dy * fg_ref[...], xh, r)
        dx_ref[...] = dx3
        df_ref[...] = (dx3 * gt_ref[...]).astype(BF16)
        _acc(l_ref, jnp.sum(err * err, axis=0, keepdims=True), first)
        _acc(dfg_ref, jnp.sum(dy * xh, axis=0, keepdims=True), first)
        _acc(dgt_ref, jnp.sum(dx3 * ffn_v, axis=0, keepdims=True), first)

    row, vec = _rows(tr, D_MODEL), _vec(D_MODEL)
    return _call(body, name=name, grid=(t // tr,),
                 in_specs=[row, _rows(tr, k), pl.BlockSpec((1, k, n), lambda i: (0, 0, 0)), vec, vec, row],
                 out_specs=[row, row, vec, vec, vec],
                 out_shape=[_sds((t, D_MODEL)), _sds((t, D_MODEL), BF16)] + [_sds((1, D_MODEL))] * 3)(
                     x2, act, w_down, gate, fg, tgt)


def _norm_mod_bwd(dh, x, g, scale, dres, name, gate=None, mix=None, w=None, tr=512):
    t = x.shape[0]
    below = gate is not None

    def body(*refs):
        if w is not None:
            w_ref, refs = refs[1], refs[:1] + refs[2:]
        if below:
            dh_ref, x_ref, g_ref, sc_ref, dr_ref, gt_ref, m_ref, dx_ref, dsh_ref, dsc_ref, dg_ref, dgt_ref, dm_ref = refs
        else:
            dh_ref, x_ref, g_ref, sc_ref, dr_ref, dx_ref, dsh_ref, dsc_ref, dg_ref = refs
        first = pl.program_id(0) == 0
        xv = x_ref[...]
        if w is None:
            dhv = dh_ref[...].astype(F32)
        else:
            n = w.shape[2]
            dhv = _dot_nt(dh_ref[:, 0:n], w_ref[0])
            for j in range(1, w.shape[0]):
                dhv = dhv + _dot_nt(dh_ref[:, j * n:(j + 1) * n], w_ref[j])
        r = _rms(xv)
        xh = xv * r
        dn = dhv * (1.0 + sc_ref[...])
        dx = dr_ref[...] + _rms_bwd(dn * g_ref[...], xh, r)
        dx_ref[...] = dx
        _acc(dsh_ref, jnp.sum(dhv, axis=0, keepdims=True), first)
        _acc(dsc_ref, jnp.sum(dhv * xh * g_ref[...], axis=0, keepdims=True), first)
        _acc(dg_ref, jnp.sum(dn * xh, axis=0, keepdims=True), first)
        if below:
            _acc(dgt_ref, jnp.sum(dx * m_ref[...], axis=0, keepdims=True), first)
            dm_ref[...] = (dx * gt_ref[...]).astype(BF16)

    row, vec = _rows(tr, D_MODEL), _vec(D_MODEL)
    first_specs = [row] if w is None else [
        _rows(tr, dh.shape[1]), pl.BlockSpec(w.shape, lambda i: (0, 0, 0), pipeline_mode=pl.Buffered(1))]
    in_specs = first_specs + [row, vec, vec, row] + ([vec, row] if below else [])
    out_specs = [row, vec, vec, vec] + ([vec, row] if below else [])
    out_shape = [_sds((t, D_MODEL))] + [_sds((1, D_MODEL))] * 3 + ([_sds((1, D_MODEL)), _sds((t, D_MODEL), BF16)] if below else [])
    args = ((dh,) if w is None else (dh, w)) + (x, g, scale, dres) + ((gate, mix) if below else ())
    return _call(body, name=name, grid=(t // tr,), in_specs=in_specs, out_specs=out_specs, out_shape=out_shape)(*args)


def _mix_in_bwd(dmix, w_out, o_hg, proj, att, og, ag, fine, name, tr=512):
    t = o_hg.shape[0]

    def body(dy_ref, w_ref, o_ref, g_ref, a_ref, og_ref, ag_ref,
             do_ref, dg_ref, da_ref, dd_ref, das_ref, dds_ref, dog_ref, dag_ref, to_sub):
        first = pl.program_id(0) == 0

        @pl.when(first)
        def _():
            to_sub[...] = _regroup_matrix(tr, tr // fine)

        dmi = _dot_nt(dy_ref[...], w_ref[0])
        dog = jnp.zeros((1, HG_HEAD), F32)
        for h in range(HG_WIDTH // HG_HEAD):
            sl = slice(h * HG_HEAD, (h + 1) * HG_HEAD)
            oh = o_ref[:, sl].astype(F32)
            gv = g_ref[:, sl].astype(F32)
            dv = dmi[:, sl]
            r = _rms(oh)
            xh = oh * r
            sg = _sigmoid(gv)
            dno = dv * gv * sg
            dg_ref[:, sl] = (dv * xh * og_ref[...] * sg * (1.0 + gv * (1.0 - sg))).astype(BF16)
            dog = dog + jnp.sum(dno * xh, axis=0, keepdims=True)
            do_ref[:, sl] = _rms_bwd(dno * og_ref[...], xh, r).astype(BF16)
        _acc(dog_ref, dog, first)
        av = a_ref[...]
        dav = dmi[:, HG_WIDTH:]
        r = _rms(av)
        xa = av * r
        _acc(dag_ref, jnp.sum(dav * xa, axis=0, keepdims=True), first)
        datt = _rms_bwd(dav * ag_ref[...], xa, r)
        da_ref[...] = datt.astype(BF16)
        das_ref[...] = _regroup(to_sub[...], datt.astype(BF16)).astype(BF16).reshape(das_ref.shape)
        prod = datt * av
        lane = lax.broadcasted_iota(jnp.int32, (1, 128), 1)
        dd = jnp.zeros((tr, 128), F32)
        for hp in range(ATT_HEADS // 2):
            pp = prod[:, hp * 128:(hp + 1) * 128]
            lo = jnp.sum(jnp.where(lane < 64, pp, 0.0), axis=-1, keepdims=True)
            hi = jnp.sum(jnp.where(lane >= 64, pp, 0.0), axis=-1, keepdims=True)
            dd = jnp.where(lane == 2 * hp, lo, dd)
            dd = jnp.where(lane == 2 * hp + 1, hi, dd)
        dd_ref[...] = dd
        dds_ref[...] = _regroup(to_sub[...], dd).reshape(dds_ref.shape)

    half = _rows(tr, HG_WIDTH)
    out = _call(body, name=name, grid=(t // tr,),
                in_specs=[_rows(tr, D_MODEL), pl.BlockSpec(w_out.shape, lambda i: (0, 0, 0)), half,
                          pl.BlockSpec((tr, HG_WIDTH), lambda i: (i, 3)), half, _vec(HG_HEAD), _vec(ATT_WIDTH)],
                out_specs=[half, half, half, _rows(tr, 128), _sub_rows(tr, fine, ATT_WIDTH), _sub_rows(tr, fine, 128),
                           _vec(HG_HEAD), _vec(ATT_WIDTH)],
                out_shape=[_sds((t, HG_WIDTH), BF16)] * 3 + [_sds((t, 128)), _sds((fine, t // fine, ATT_WIDTH), BF16),
                           _sds((fine, t // fine, 128)), _sds((1, HG_HEAD)), _sds((1, ATT_WIDTH))],
                scratch_shapes=[pltpu.VMEM((tr, tr), BF16)])(dmix, w_out, o_hg, proj, att, og, ag)
    out = list(out)
    return out[:4] + [out[4].reshape(t, ATT_WIDTH), out[5].reshape(t, 128)] + out[6:]


def _dproj(dhg, dg, dqkvs, name, tr=1024):
    t = dhg.shape[0]
    w3 = 3 * HG_WIDTH
    w4 = w3 + HG_WIDTH

    def body(h_ref, g_ref, *refs):
        o_ref = refs[-1]
        o_ref[:, :w3] = h_ref[...]
        o_ref[:, w3:w4] = g_ref[...]
        o_ref[:, w4:] = sum(r[...].astype(F32) for r in refs[:-1]).astype(BF16)

    return _call(body, name=name, grid=(t // tr,),
                 in_specs=[_rows(tr, w3), _rows(tr, HG_WIDTH)] + [_rows(tr, 3 * ATT_WIDTH)] * len(dqkvs),
                 out_specs=_rows(tr, IN_WIDTH), out_shape=_sds((t, IN_WIDTH), BF16))(dhg, dg, *dqkvs)


def _chunk_tri(upper):
    row = lax.broadcasted_iota(jnp.int32, (HG_GROUP, HG_CHUNK, HG_CHUNK), 1)
    col = lax.broadcasted_iota(jnp.int32, (HG_GROUP, HG_CHUNK, HG_CHUNK), 2)
    return (row <= col if upper else row >= col).astype(BF16)


def _chunk_cumsum(x, tri):
    x3 = x.reshape(HG_GROUP, HG_CHUNK, x.shape[1])
    dims = (((2,), (1,)), ((0,), (0,)))
    out = None
    for _ in range(3):
        part = x3.astype(BF16)
        x3 = x3 - part.astype(F32)
        term = lax.dot_general(tri, part, dims, preferred_element_type=F32)
        out = term if out is None else out + term
    return out.reshape(x.shape)


def _hg_gates(f_raw, q_raw, lb, tri):
    sg = _sigmoid(f_raw)
    f = lb + (1.0 - lb) * sg
    k = 1.0 - f
    b = _chunk_cumsum(jnp.log(f), tri)
    sq = _sigmoid(q_raw)
    return sg, f, k, b, sq


def _hg_masks(rows):
    row = lax.broadcasted_iota(jnp.int32, (rows, rows), 0)
    col = lax.broadcasted_iota(jnp.int32, (rows, rows), 1)
    same = (row // HG_CHUNK) == (col // HG_CHUNK)
    return jnp.logical_and(row >= col, same), jnp.logical_and(row <= col, same)


def _per_chunk(rows_of):
    return jnp.concatenate([jnp.broadcast_to(r, (HG_CHUNK, r.shape[1])) for r in rows_of], axis=0)


def _hgrn_fwd(proj, lb_logits, name):
    t = proj.shape[0]
    nc = t // HG_CHUNK
    nh = HG_WIDTH // HG_HEAD
    rows = HG_GROUP * HG_CHUNK

    def body(q_ref, f_ref, i_ref, lg_ref, o_ref, st_ref, s_scr):
        @pl.when(pl.program_id(0) == 0)
        def _():
            s_scr[...] = jnp.zeros_like(s_scr)

        lg = lg_ref[...]
        lb_all = _sigmoid(lg[0:1] - lg[1:2])
        causal, _ = _hg_masks(rows)
        tri = _chunk_tri(False)
        for h in range(nh):
            sl = slice(h * HG_HEAD, (h + 1) * HG_HEAD)
            q_raw = q_ref[:, sl].astype(F32)
            _, _, k, b, sq = _hg_gates(f_ref[:, sl].astype(F32), q_raw, lb_all[:, sl], tri)
            v = i_ref[:, sl].astype(BF16)
            gls = [b[(g + 1) * HG_CHUNK - 1:(g + 1) * HG_CHUNK] for g in range(HG_GROUP)]
            bm = _per_chunk([b[g * HG_CHUNK + HG_CHUNK // 2 - 1:g * HG_CHUNK + HG_CHUNK // 2] for g in range(HG_GROUP)])
            qd = (q_raw * sq * jnp.exp(b)).astype(BF16)
            qm = (q_raw * sq * jnp.exp(b - bm)).astype(BF16)
            km = (k * jnp.exp(bm - b)).astype(BF16)
            ke = (k * jnp.exp(_per_chunk(gls) - b)).astype(BF16)
            a = jnp.where(causal, _dot_nt(qm, km), 0.0).astype(BF16)
            o_intra = _dot(a, v)
            st = s_scr[h]
            o_inter = []
            for g in range(HG_GROUP):
                rs = slice(g * HG_CHUNK, (g + 1) * HG_CHUNK)
                st_ref[g, sl, :] = st
                o_inter.append(_dot_nt(qd[rs], st.astype(BF16)))
                st = st * jnp.exp(gls[g]) + _dot_tn(v[rs], ke[rs])
            s_scr[h] = st
            o_ref[:, sl] = (o_intra + jnp.concatenate(o_inter, axis=0)).astype(BF16)

    blk = lambda j: pl.BlockSpec((rows, HG_WIDTH), lambda c: (c, j))
    return _call(body, name=name, grid=(nc // HG_GROUP,),
                 in_specs=[blk(0), blk(1), blk(2), pl.BlockSpec((2, HG_WIDTH), lambda c: (0, 0))],
                 out_specs=[blk(0), pl.BlockSpec((HG_GROUP, HG_WIDTH, HG_HEAD), lambda c: (c, 0, 0))],
                 out_shape=[_sds((t, HG_WIDTH), BF16), _sds((nc, HG_WIDTH, HG_HEAD))],
                 scratch_shapes=[pltpu.VMEM((nh, HG_HEAD, HG_HEAD), F32)])(proj, proj, proj, lb_logits)


def _hgrn_bwd(proj, lb_logits, states, do, name):
    t = proj.shape[0]
    ng = t // (HG_GROUP * HG_CHUNK)
    nh = HG_WIDTH // HG_HEAD
    rows = HG_GROUP * HG_CHUNK

    def body(q_ref, f_ref, i_ref, lg_ref, st_ref, do_ref, d_ref, dlb_ref, ds_scr):
        first = pl.program_id(0) == 0

        @pl.when(first)
        def _():
            ds_scr[...] = jnp.zeros_like(ds_scr)

        lg = lg_ref[...]
        lb_all = _sigmoid(lg[0:1] - lg[1:2])
        causal, _ = _hg_masks(rows)
        tri = _chunk_tri(False)
        tri_t = _chunk_tri(True)
        dlb = []
        for h in range(nh):
            sl = slice(h * HG_HEAD, (h + 1) * HG_HEAD)
            q_raw = q_ref[:, sl].astype(F32)
            lb = lb_all[:, sl]
            sg, f, k, b, sq = _hg_gates(f_ref[:, sl].astype(F32), q_raw, lb, tri)
            v = i_ref[:, sl].astype(BF16)
            gls = [b[(g + 1) * HG_CHUNK - 1:(g + 1) * HG_CHUNK] for g in range(HG_GROUP)]
            bm = _per_chunk([b[g * HG_CHUNK + HG_CHUNK // 2 - 1:g * HG_CHUNK + HG_CHUNK // 2] for g in range(HG_GROUP)])
            eb = jnp.exp(b)
            ebm = jnp.exp(b - bm)
            emb = jnp.exp(bm - b)
            egb = jnp.exp(_per_chunk(gls) - b)
            ke = k * egb
            qd_b, qm_b = (q_raw * sq * eb).astype(BF16), (q_raw * sq * ebm).astype(BF16)
            km_b, ke_b = (k * emb).astype(BF16), ke.astype(BF16)
            dov = do_ref[:, sl].astype(BF16)
            a = jnp.where(causal, _dot_nt(qm_b, km_b), 0.0).astype(BF16)
            da = jnp.where(causal, _dot_nt(dov, v), 0.0).astype(BF16)
            dkm = _dot_tn(da, qm_b)
            dst = ds_scr[h]
            dqd_s, dv_s, dke_s, dgl_s = [None] * HG_GROUP, [None] * HG_GROUP, [None] * HG_GROUP, [None] * HG_GROUP
            for g in reversed(range(HG_GROUP)):
                rs = slice(g * HG_CHUNK, (g + 1) * HG_CHUNK)
                st = st_ref[g, sl, :]
                dst_b = dst.astype(BF16)
                egl = jnp.exp(gls[g])
                dqd_s[g] = _dot(dov[rs], st.astype(BF16))
                dv_s[g] = _dot_nt(ke_b[rs], dst_b)
                dke_s[g] = _dot(v[rs], dst_b)
                dgl_s[g] = jnp.sum(dst * st, axis=0, keepdims=True) * egl
                dst = _dot_tn(dov[rs], qd_b[rs]) + dst * egl
            ds_scr[h] = dst
            dqm = _dot(da, km_b)
            dqd = jnp.concatenate(dqd_s, axis=0)
            dv = _dot_tn(a, dov) + jnp.concatenate(dv_s, axis=0)
            dke = jnp.concatenate(dke_s, axis=0)
            t1 = dke * ke
            db = dqm * qm_b.astype(F32) - dkm * km_b.astype(F32) + dqd * qd_b.astype(F32) - t1
            dgl = _per_chunk([dgl_s[g] + jnp.sum(t1[g * HG_CHUNK:(g + 1) * HG_CHUNK], axis=0, keepdims=True)
                              for g in range(HG_GROUP)])
            dlf = _chunk_cumsum(db, tri_t) + dgl
            df = dlf / f - (dkm * emb + dke * egb)
            d_ref[:, sl] = ((dqm * ebm + dqd * eb) * sq * (1.0 + q_raw * (1.0 - sq))).astype(BF16)
            d_ref[:, HG_WIDTH + h * HG_HEAD:HG_WIDTH + (h + 1) * HG_HEAD] = (
                df * (1.0 - lb) * sg * (1.0 - sg)).astype(BF16)
            d_ref[:, 2 * HG_WIDTH + h * HG_HEAD:2 * HG_WIDTH + (h + 1) * HG_HEAD] = dv.astype(BF16)
            dlb.append(jnp.sum(df * (1.0 - sg), axis=0, keepdims=True))
        _acc(dlb_ref, jnp.concatenate(dlb, axis=1), first)

    rev = lambda j: pl.BlockSpec((rows, HG_WIDTH), lambda c: (ng - 1 - c, j))
    return _call(body, name=name, grid=(ng,),
                 in_specs=[rev(0), rev(1), rev(2), pl.BlockSpec((2, HG_WIDTH), lambda c: (0, 0)),
                           pl.BlockSpec((HG_GROUP, HG_WIDTH, HG_HEAD), lambda c: (ng - 1 - c, 0, 0)), rev(0)],
                 out_specs=[pl.BlockSpec((rows, 3 * HG_WIDTH), lambda c: (ng - 1 - c, 0)), _vec(HG_WIDTH)],
                 out_shape=[_sds((t, 3 * HG_WIDTH), BF16), _sds((1, HG_WIDTH))],
                 scratch_shapes=[pltpu.VMEM((nh, HG_HEAD, HG_HEAD), F32)])(proj, proj, proj, lb_logits, states, do)


def _to_sub(a, dil):
    t, w = a.shape
    return a if dil == 1 else a.reshape(t // dil, dil, w).transpose(1, 0, 2).reshape(t, w)


def _from_sub(a, dil):
    t, w = a.shape
    return a if dil == 1 else a.reshape(dil, t // dil, w).transpose(1, 0, 2).reshape(t, w)


def _att_mask(has_prev, seg):
    def place(v):
        v = v % ATT_BLOCK
        return v if seg == 1 else seg * (v % (ATT_BLOCK // seg)) + v // (ATT_BLOCK // seg)

    row = lax.broadcasted_iota(jnp.int32, (2 * ATT_BLOCK, 2 * ATT_BLOCK), 0)
    col = lax.broadcasted_iota(jnp.int32, (2 * ATT_BLOCK, 2 * ATT_BLOCK), 1)
    qi, kj = place(row), place(col)
    prev = jnp.logical_and(jnp.logical_and(col < ATT_BLOCK, kj >= qi), has_prev)
    cur = jnp.logical_and(col >= ATT_BLOCK, kj <= qi)
    return jnp.logical_or(prev, cur), lax.broadcasted_iota(jnp.int32, (1, 128), 1)


def _get(ref, sl):
    if len(ref.shape) == 2:
        return ref[:, sl]
    v = ref[:, :, sl]
    return v.reshape(ATT_BLOCK, v.shape[2])


def _put(ref, sl, val):
    if len(ref.shape) == 2:
        ref[:, sl] = val
    else:
        ref[:, :, sl] = val.reshape(ref.shape[0], ref.shape[1], val.shape[1])


def _att_spec(nb, dil, seg, width, col, back):
    bps = nb // dil

    def plain(n):
        return jnp.clip(n - back, 0, nb - 1), col

    def segmented(n):
        m = jnp.clip(n - back, 0, nb - 1)
        return 0, m // bps, m % bps, 0, col

    if seg == 1:
        return pl.BlockSpec((ATT_BLOCK, width), plain)
    return pl.BlockSpec((seg, None, None, ATT_BLOCK // seg, width), segmented)


def _att_shape(nb, dil, seg, width):
    t = nb * ATT_BLOCK
    return (t, width) if seg == 1 else (seg, dil, nb // dil, ATT_BLOCK // seg, width)


def _att_view(a, nb, dil, seg):
    return a.reshape(_att_shape(nb, dil, seg, a.shape[1]))


def _attn_fwd_block(q_ref, kc_ref, kp_ref, vc_ref, vp_ref, o_ref, l_ref, has_prev, seg, lane0):
    mask, lane = _att_mask(has_prev, seg)
    lo = lane < 64
    nq = ATT_BLOCK
    lse_all = jnp.zeros((nq, 128), F32)
    for hp in range(ATT_HEADS // 2):
        sl = slice(hp * 128, (hp + 1) * 128)
        q2 = _get(q_ref, sl)
        zero = jnp.zeros_like(q2)
        q2 = q2 * 0.125
        qs = jnp.concatenate([jnp.where(lo, q2, zero), jnp.where(lo, zero, q2)], axis=0)
        kk = jnp.concatenate([_get(kp_ref, sl), _get(kc_ref, sl)], axis=0)
        vv = jnp.concatenate([_get(vp_ref, sl), _get(vc_ref, sl)], axis=0)
        s = jnp.where(mask, _dot_nt(qs, kk), NEG)
        mx = jnp.max(s, axis=-1, keepdims=True)
        p = jnp.exp(s - mx)
        l = jnp.sum(p, axis=-1, keepdims=True)
        o = _dot(p.astype(BF16), vv) * (1.0 / l)
        _put(o_ref, sl, jnp.where(lo, o[:nq], o[nq:]).astype(BF16))
        lse = mx + jnp.log(l)
        lse_all = jnp.where(lane == lane0 + 2 * hp, lse[:nq], lse_all)
        lse_all = jnp.where(lane == lane0 + 2 * hp + 1, lse[nq:], lse_all)
    _put(l_ref, slice(None), lse_all)


def _attn_fwd(branches, name):
    t = branches[0][0].shape[0]
    nb = t // ATT_BLOCK
    nbr = len(branches)

    def body(*refs):
        n = pl.program_id(0)
        for i, (_, dil, seg) in enumerate(branches):
            _attn_fwd_block(*refs[5 * i:5 * i + 5], *refs[5 * nbr + 2 * i:5 * nbr + 2 * i + 2],
                            (n % (nb // dil)) != 0, seg, ATT_HEADS * i)

    in_specs, args, out_specs, out_shape = [], [], [], []
    for qkv, dil, seg in branches:
        c0 = qkv.shape[1] // ATT_WIDTH - 3
        in_specs += [_att_spec(nb, dil, seg, ATT_WIDTH, c0 + j, back) for j, back in [(0, 0), (1, 0), (1, 1), (2, 0), (2, 1)]]
        args += [_att_view(qkv, nb, dil, seg)] * 5
        out_specs += [_att_spec(nb, dil, seg, ATT_WIDTH, 0, 0), _att_spec(nb, dil, seg, 128, 0, 0)]
        out_shape += [_sds(_att_shape(nb, dil, seg, ATT_WIDTH), BF16), _sds(_att_shape(nb, dil, seg, 128))]
    out = _call(body, name=name, grid=(nb,), in_specs=in_specs, out_specs=out_specs, out_shape=out_shape)(*args)
    return [(out[2 * i].reshape(t, ATT_WIDTH), out[2 * i + 1].reshape(t, 128)) for i in range(nbr)]


def _combine_mix_in(os_, ls_, fine, o_hg, proj, og, ag, name, tr=512):
    t = os_[0].shape[0]
    nbr = len(os_)

    def body(*refs):
        o_refs, l_refs = refs[:nbr], refs[nbr:2 * nbr]
        oh_ref, g_ref, og_ref, ag_ref, a_ref, lt_ref, lts_ref, m_ref, to_natural, to_sub = refs[2 * nbr:]

        @pl.when(pl.program_id(0) == 0)
        def _():
            to_natural[...] = _regroup_matrix(tr, fine)
            to_sub[...] = _regroup_matrix(tr, tr // fine)

        lane = lax.broadcasted_iota(jnp.int32, (1, 128), 1)
        packed = l_refs[0][...] + _regroup(to_natural[...], sum(r[...] for r in l_refs[1:]).reshape(tr, 128))
        ls = [packed if i == 0 else pltpu.roll(packed, 128 - ATT_HEADS * i, 1) for i in range(nbr)]
        mx = functools.reduce(jnp.maximum, ls)
        tot = mx + jnp.log(sum(jnp.exp(l - mx) for l in ls))
        ws = [jnp.exp(l - tot) for l in ls]
        tot = jnp.where(lane < ATT_HEADS, tot, 0.0)
        lt_ref[...] = tot
        lts_ref[...] = _regroup(to_sub[...], tot).reshape(lts_ref.shape)
        o_vals = [o_refs[0]] + [_regroup(to_natural[...], r[...].reshape(tr, ATT_WIDTH)) for r in o_refs[1:]]
        pairs = []
        for hp in range(ATT_HEADS // 2):
            sl = slice(hp * 128, (hp + 1) * 128)
            acc = jnp.zeros((tr, 128), F32)
            for w, o in zip(ws, o_vals):
                wf = jnp.where(lane < 64, w[:, 2 * hp:2 * hp + 1], w[:, 2 * hp + 1:2 * hp + 2])
                acc = acc + wf * o[:, sl]
            pairs.append(acc)
        av = jnp.concatenate(pairs, axis=1)
        a_ref[...] = av
        m_ref[:, HG_WIDTH:] = (av * _rms(av) * ag_ref[...]).astype(BF16)
        for h in range(HG_WIDTH // HG_HEAD):
            sl = slice(h * HG_HEAD, (h + 1) * HG_HEAD)
            oh = oh_ref[:, sl].astype(F32)
            gv = g_ref[:, sl].astype(F32)
            m_ref[:, sl] = (oh * _rms(oh) * og_ref[...] * (gv * _sigmoid(gv))).astype(BF16)

    half = _rows(tr, ATT_WIDTH)
    sub = lambda a: a.reshape(fine, t // fine, a.shape[1])
    att, lse, lse_sub, mixin = _call(
        body, name=name, grid=(t // tr,),
        in_specs=[half] + [_sub_rows(tr, fine, ATT_WIDTH)] * (nbr - 1) + [_rows(tr, 128)] + [_sub_rows(tr, fine, 128)] * (nbr - 1)
        + [half, pl.BlockSpec((tr, HG_WIDTH), lambda i: (i, 3)), _vec(HG_HEAD), _vec(ATT_WIDTH)],
        out_specs=[half, _rows(tr, 128), _sub_rows(tr, fine, 128), _rows(tr, D_MODEL)],
        out_shape=[_sds((t, ATT_WIDTH)), _sds((t, 128)), _sds((fine, t // fine, 128)), _sds((t, D_MODEL), BF16)],
        scratch_shapes=[pltpu.VMEM((tr, tr), BF16)] * 2)(
            os_[0], *map(sub, os_[1:]), ls_[0], *map(sub, ls_[1:]), o_hg, proj, og, ag)
    return att, lse, lse_sub.reshape(t, 128), mixin


def _attn_bwd_block(q_ref, kc_ref, kp_ref, vc_ref, vp_ref, do_ref, l_ref, d_ref, out_ref, carry, has_prev, seg):
    w = ATT_WIDTH
    nq = ATT_BLOCK
    mask, lane = _att_mask(has_prev, seg)
    lo = lane < 64
    lse, ddv = _get(l_ref, slice(None)), _get(d_ref, slice(None))
    for hp in range(ATT_HEADS // 2):
        sl = slice(hp * 128, (hp + 1) * 128)
        sk = slice(w + hp * 128, w + (hp + 1) * 128)
        sv = slice(2 * w + hp * 128, 2 * w + (hp + 1) * 128)
        q2, do2 = _get(q_ref, sl), _get(do_ref, sl)
        zero = jnp.zeros_like(q2)
        q2 = q2 * 0.125
        qs = jnp.concatenate([jnp.where(lo, q2, zero), jnp.where(lo, zero, q2)], axis=0)
        dos = jnp.concatenate([jnp.where(lo, do2, zero), jnp.where(lo, zero, do2)], axis=0)
        kk = jnp.concatenate([_get(kp_ref, sl), _get(kc_ref, sl)], axis=0)
        vv = jnp.concatenate([_get(vp_ref, sl), _get(vc_ref, sl)], axis=0)
        ls = jnp.concatenate([lse[:, 2 * hp:2 * hp + 1], lse[:, 2 * hp + 1:2 * hp + 2]], axis=0)
        dh = jnp.concatenate([ddv[:, 2 * hp:2 * hp + 1], ddv[:, 2 * hp + 1:2 * hp + 2]], axis=0)
        p = jnp.exp(jnp.where(mask, _dot_nt(qs, kk) - ls, NEG))
        ds = (p * (_dot_nt(dos, vv) - dh)).astype(BF16)
        dq = _dot(ds, kk) * 0.125
        dk = _dot_tn(ds, qs)
        dv = _dot_tn(p.astype(BF16), dos)
        _put(out_ref, sl, carry[:, sl].astype(BF16))
        _put(out_ref, sk, (carry[:, sk] + dk[:nq]).astype(BF16))
        _put(out_ref, sv, (carry[:, sv] + dv[:nq]).astype(BF16))
        carry[:, sl] = jnp.where(lo, dq[:nq], dq[nq:])
        carry[:, sk] = dk[nq:]
        carry[:, sv] = dv[nq:]


def _attn_bwd(branches, name):
    t = branches[0][0].shape[0]
    nb = t // ATT_BLOCK
    nbr = len(branches)
    w = ATT_WIDTH

    def body(*refs):
        ins, outs, carries = refs[:8 * nbr], refs[8 * nbr:9 * nbr], refs[9 * nbr:]
        n = pl.program_id(0)

        @pl.when(n == 0)
        def _():
            for carry in carries:
                carry[...] = jnp.zeros_like(carry)

        @pl.when(n < nb)
        def _():
            for i, branch in enumerate(branches):
                dil, seg = branch[4:]
                _attn_bwd_block(*ins[8 * i:8 * i + 8], outs[i], carries[i], (n % (nb // dil)) != 0, seg)

        @pl.when(n == nb)
        def _():
            for i in range(nbr):
                _put(outs[i], slice(None), carries[i][...].astype(BF16))

    in_specs, args, out_specs, out_shape = [], [], [], []
    for qkv, dout, lse, dd, dil, seg in branches:
        c0 = qkv.shape[1] // w - 3
        in_specs += [_att_spec(nb, dil, seg, w, c0 + j, back) for j, back in [(0, 0), (1, 0), (1, 1), (2, 0), (2, 1)]]
        in_specs += [_att_spec(nb, dil, seg, w, 0, 0), _att_spec(nb, dil, seg, 128, 0, 0), _att_spec(nb, dil, seg, 128, 0, 0)]
        args += [_att_view(a, nb, dil, seg) for a in [qkv] * 5 + [dout, lse, dd]]
        out_specs += [_att_spec(nb, dil, seg, 3 * w, 0, 1)]
        out_shape += [_sds(_att_shape(nb, dil, seg, 3 * w), BF16)]
    out = _call(body, name=name, grid=(nb + 1,), in_specs=in_specs, out_specs=out_specs, out_shape=out_shape,
                scratch_shapes=[pltpu.VMEM((ATT_BLOCK, 3 * w), F32)] * nbr)(*args)
    return [o.reshape(t, 3 * w) for o in out]


def _local_step(x, tgt, mod, norm1_g, lb_logits, og, ag, norm2_g, fg, get_w, put_g, late=lambda a: a, project=None):
    shift1, scale1, gate1, shift2, scale2, gate2 = [mod[:, i * D_MODEL:(i + 1) * D_MODEL] for i in range(6)]
    fg = fg.reshape(1, D_MODEL)

    h1 = _norm_mod(x, norm1_g, scale1, shift1, "norm_mod1")
    if project is None:
        w_in = get_w("w_in", h1)
        proj = _mm_nn(h1, w_in, "mm_in", out_dtype=BF16)
    else:
        proj, w_in = project(h1)
    o_hg, states = _hgrn_fwd(proj, lb_logits, "hgrn_fwd")
    fine = DILATIONS[-1]
    layouts = [(d, 1 if d == 1 else fine // d) for d in DILATIONS]
    qkv_fine = _to_sub(proj, fine)
    qkvs = [proj if d == 1 else qkv_fine for d in DILATIONS]
    natural = lambda a, d: a if d == 1 else _from_sub(a, fine)
    outs = _attn_fwd([(q, d, seg) for q, (d, seg) in zip(qkvs, layouts)], "attn_fwd")
    att, lse, lse_fine, mixin = _combine_mix_in([o for o, _ in outs], [l for _, l in outs], fine,
                                                o_hg, proj, og, ag, "attn_combine_mix_in")
    w_out = get_w("w_out", mixin)
    mix, x2, h2 = _out_resid_norm_mod(x, mixin, w_out, gate1, norm2_g, scale2, shift2, "mm_out_resid_norm_mod2")
    w_gu = get_w("w_gu", h2)
    a_ff, u_ff, act = _mm_gate_up(h2, w_gu, "mm_gu")
    w_down = get_w("w_down", act)
    dx3, dffn, loss_v, dfg, dgate2 = _down_loss(x2, act, w_down, gate2, fg, tgt, "mm_down_loss")

    dffn = put_g("w_down", *_mm_tn(act, dffn, 1, "mm_down_dw", tm=2048, tk=D_FF // 2), dffn)
    dau = _mm_down_dx(dffn, w_down, a_ff, u_ff, "mm_down_dx")
    dau = put_g("w_gu", *_mm_tn(h2, dau, N_SHARD, "mm_gu_dw", tm=x.shape[0], tk=512), dau)
    dx2, dshift2, dscale2, dg2, dgate1, dmix = _norm_mod_bwd(
        dau, x2, norm2_g, scale2, dx3, "mm_gu_dx_norm_bwd", gate=gate1, mix=mix, w=w_gu)
    dmix = put_g("w_out", *_mm_tn(mixin, dmix, 1, "mm_out_dw", tm=x.shape[0], tk=512), dmix)
    do_hg, dg_raw, datt, dd, datt_fine, dd_fine, dog, dag = _mix_in_bwd(
        dmix, w_out, o_hg, proj, att, og, ag, fine, "mm_out_dx_mix_in_bwd")
    datts = _attn_bwd([(q,) + ((datt, lse, dd) if d == 1 else (datt_fine, lse_fine, dd_fine)) + (d, seg)
                       for q, (d, seg) in zip(qkvs, layouts)], "attn_bwd")
    dhg, dlb = _hgrn_bwd(proj, lb_logits, states, do_hg, "hgrn_bwd")
    dhg = late(dhg)
    dproj = _dproj(dhg, dg_raw, [natural(a, d) for a, d in zip(datts, DILATIONS)], "dproj")
    dproj = put_g("w_in", *_mm_tn(h1, dproj, N_SHARD, "mm_in_dw", tm=x.shape[0], tk=512), dproj)
    dx, dshift1, dscale1, dg1 = _norm_mod_bwd(dproj, x, norm1_g, scale1, dx2, "mm_in_dx_norm_bwd", w=w_in)

    stats = jnp.concatenate([loss_v, dfg, dg2, dg1, dlb, dag, dog,
                             dshift1, dscale1, dgate1, dshift2, dscale2, dgate2], axis=1)
    return dx, stats


def _place():
    x, y, c = lax.axis_index("x"), lax.axis_index("y"), lax.axis_index("c")
    return x, y, c


def _chip_peers(x, y, c):
    return [(1 - x, y, c), (x, 1 - y, c), (1 - x, 1 - y, c)]


def _comm_call(body, name, n_in, out_shape, scratch_shapes):
    hbm = pl.BlockSpec(memory_space=pl.ANY)
    return pl.pallas_call(body, name=name, in_specs=[hbm] * n_in, out_specs=[hbm] * len(out_shape),
                          out_shape=out_shape, scratch_shapes=scratch_shapes)


_HBM = pl.BlockSpec(memory_space=pltpu.HBM)
_SEM = pl.BlockSpec(memory_space=pltpu.SEMAPHORE)
_EFFECT = pltpu.SideEffectType.DATAFLOW_SIDE_EFFECTING


def _exchange_copy(bufs, send, recv, j, peer, place, kind):
    x, y, c = place
    target = peer
    if kind == "gather":
        src = dst = bufs[0].at[2 * x + y]
    elif kind == "scatter":
        src, dst = bufs[0].at[2 * peer[0] + peer[1]], bufs[1].at[j]
    else:
        half = bufs[0].shape[1] // 2
        rows = pl.ds(c * half, half)
        if kind == "half":
            src = dst = bufs[0].at[2 * x + y, rows]
        else:
            src = dst = bufs[0].at[2 * peer[0] + peer[1], rows]
            target = (x, y, 1 - c)
    return pltpu.make_async_remote_copy(src_ref=src, dst_ref=dst, send_sem=send.at[j], recv_sem=recv.at[j],
                                        device_id=target, device_id_type=MESH)


def _exchange_start(groups, after, kind, name):
    sizes = [len(g) for g in groups]
    flat = [b for g in groups for b in g]
    ng, nb = len(groups), len(flat)

    def body(*refs):
        bufs, sems = refs[:nb], refs[nb + 1:nb + 1 + 2 * ng]
        x, y, c = _place()
        for j, peer in enumerate(_chip_peers(x, y, c)):
            at = 0
            for i, size in enumerate(sizes):
                _exchange_copy(bufs[at:at + size], sems[2 * i], sems[2 * i + 1], j, peer, (x, y, c), kind).start()
                at += size

    any_space = pl.BlockSpec(memory_space=pl.ANY)
    out = pl.pallas_call(
        body, name=name, in_specs=[_HBM] * nb + [any_space],
        out_specs=[_SEM] * (2 * ng) + [_HBM] * nb + [any_space],
        out_shape=[pltpu.SemaphoreType.DMA((3,))] * (2 * ng) + [pltpu.HBM(b.shape, b.dtype) for b in flat]
        + [_sds(after.shape, after.dtype)],
        input_output_aliases={i: 2 * ng + i for i in range(nb + 1)},
        compiler_params=pltpu.CompilerParams(has_side_effects=_EFFECT),
    )(*[pltpu.with_memory_space_constraint(b, pltpu.HBM) for b in flat], after)
    started, at = [], 2 * ng
    for i, size in enumerate(sizes):
        started.append((out[2 * i], out[2 * i + 1], tuple(out[at:at + size])))
        at += size
    return started, out[-1]


def _exchange_wait(started, after, kind, name):
    send, recv, bufs = started
    nb = len(bufs)

    def body(*refs):
        x, y, c = _place()
        for j, peer in enumerate(_chip_peers(x, y, c)):
            cp = _exchange_copy(refs[:nb], refs[nb], refs[nb + 1], j, peer, (x, y, c), kind)
            cp.wait_send()
            cp.wait_recv()

    return pl.pallas_call(
        body, name=name, in_specs=[_HBM] * nb + [_SEM, _SEM, pl.BlockSpec(memory_space=pl.ANY)],
        out_specs=[_HBM] * nb, out_shape=[pltpu.HBM(b.shape, b.dtype) for b in bufs],
        input_output_aliases={i: i for i in range(nb)},
        compiler_params=pltpu.CompilerParams(has_side_effects=_EFFECT),
    )(*bufs, send, recv, after)


def _sibling_copies(v_refs, l_refs, send, recv):
    x, y, c = _place()
    return [pltpu.make_async_remote_copy(src_ref=v, dst_ref=l, send_sem=send.at[a], recv_sem=recv.at[a],
                                         device_id=(x, y, 1 - c), device_id_type=MESH)
            for a, (v, l) in enumerate(zip(v_refs, l_refs))]


def _sibling_start(vs, after, name):
    vs = list(vs)
    n = len(vs)
    lands = [lax.empty(v.shape, v.dtype) for v in vs]

    def body(*refs):
        for cp in _sibling_copies(refs[:n], refs[n:2 * n], refs[2 * n + 1], refs[2 * n + 2]):
            cp.start()

    any_space = pl.BlockSpec(memory_space=pl.ANY)
    out = pl.pallas_call(
        body, name=name, in_specs=[_HBM] * (2 * n) + [any_space],
        out_specs=[_SEM, _SEM] + [_HBM] * (2 * n) + [any_space],
        out_shape=[pltpu.SemaphoreType.DMA((n,))] * 2 + [pltpu.HBM(b.shape, b.dtype) for b in vs + lands]
        + [_sds(after.shape, after.dtype)],
        input_output_aliases={i: 2 + i for i in range(2 * n + 1)},
        compiler_params=pltpu.CompilerParams(has_side_effects=_EFFECT),
    )(*[pltpu.with_memory_space_constraint(b, pltpu.HBM) for b in vs + lands], after)
    return (out[0], out[1], tuple(out[2:2 + n]), tuple(out[2 + n:2 + 2 * n])), out[-1]


def _sibling_wait(started, after, name):
    send, recv, vs, lands = started
    n = len(vs)

    def body(*refs):
        for cp in _sibling_copies(refs[:n], refs[n:2 * n], refs[2 * n], refs[2 * n + 1]):
            cp.wait_send()
            cp.wait_recv()

    out = pl.pallas_call(
        body, name=name, in_specs=[_HBM] * (2 * n) + [_SEM, _SEM, pl.BlockSpec(memory_space=pl.ANY)],
        out_specs=[_HBM] * (2 * n), out_shape=[pltpu.HBM(b.shape, b.dtype) for b in vs + lands],
        input_output_aliases={i: i for i in range(2 * n)},
        compiler_params=pltpu.CompilerParams(has_side_effects=_EFFECT),
    )(*vs, *lands, send, recv, after)
    return out[:n], out[n:]


def _swap_sibling(vs, name):
    n = len(vs)

    def body(*refs):
        v_refs, o_refs, (send, recv) = refs[:n], refs[n:2 * n], refs[2 * n:]
        x, y, c = _place()
        cps = [pltpu.make_async_remote_copy(
            src_ref=v_refs[a], dst_ref=o_refs[a], send_sem=send.at[a], recv_sem=recv.at[a],
            device_id=(x, y, 1 - c), device_id_type=MESH) for a in range(n)]
        for cp in cps:
            cp.start()
        for cp in cps:
            cp.wait()

    return _comm_call(body, name, n, [_sds(v.shape, v.dtype) for v in vs],
                      [pltpu.SemaphoreType.DMA((n,)), pltpu.SemaphoreType.DMA((n,))])(*vs)


def _everyone(x, y, c):
    return [(1 - x if k & 4 else x, 1 - y if k & 2 else y, 1 - c if k & 1 else c) for k in range(1, 8)]


def _all_gather_copies(land_ref, send, recv, arriving):
    x, y, c = _place()
    me = 4 * x + 2 * y + c
    return [pltpu.make_async_remote_copy(
        src_ref=land_ref.at[me], dst_ref=land_ref.at[4 * p[0] + 2 * p[1] + p[2] if arriving else me],
        send_sem=send.at[k], recv_sem=recv.at[k], device_id=p, device_id_type=MESH)
        for k, p in enumerate(_everyone(x, y, c))]


def _all_gather_start(v, name):
    x, y, c = _place()
    land = lax.dynamic_update_slice(lax.empty((8,) + v.shape, v.dtype), v[None], (4 * x + 2 * y + c, 0, 0))

    def body(land_ref, v_ref, send, recv, land_out, v_out):
        for cp in _all_gather_copies(land_ref, send, recv, False):
            cp.start()

    any_space = pl.BlockSpec(memory_space=pl.ANY)
    out = pl.pallas_call(
        body, name=name, in_specs=[_HBM, any_space], out_specs=[_SEM, _SEM, _HBM, any_space],
        out_shape=[pltpu.SemaphoreType.DMA((7,))] * 2 + [pltpu.HBM(land.shape, land.dtype), _sds(v.shape, v.dtype)],
        input_output_aliases={0: 2, 1: 3}, compiler_params=pltpu.CompilerParams(has_side_effects=_EFFECT),
    )(pltpu.with_memory_space_constraint(land, pltpu.HBM), v)
    return tuple(out[:3]), out[3]


def _all_gather_wait(started, after, name):
    send, recv, land = started

    def body(land_ref, send, recv, after_ref, land_out):
        for cp in _all_gather_copies(land_ref, send, recv, True):
            cp.wait_send()
            cp.wait_recv()

    return pl.pallas_call(
        body, name=name, in_specs=[_HBM, _SEM, _SEM, pl.BlockSpec(memory_space=pl.ANY)], out_specs=_HBM,
        out_shape=pltpu.HBM(land.shape, land.dtype), input_output_aliases={0: 0},
        compiler_params=pltpu.CompilerParams(has_side_effects=_EFFECT),
    )(land, send, recv, after)


def _cast_place(ws, shard, name):
    n = len(ws)

    def body(s_ref, *refs):
        for w_ref, o_ref in zip(refs[:n], refs[n:]):
            o_ref[0] = w_ref[...].astype(BF16)

    return pl.pallas_call(
        body, name=name, out_shape=[_sds((N_SHARD,) + w.shape, BF16) for w in ws],
        grid_spec=pltpu.PrefetchScalarGridSpec(
            num_scalar_prefetch=1, grid=(4,),
            in_specs=[pl.BlockSpec((w.shape[0] // 4, w.shape[1]), lambda i, s: (i, 0)) for w in ws],
            out_specs=[pl.BlockSpec((1, w.shape[0] // 4, w.shape[1]), lambda i, s: (s[0], i, 0)) for w in ws]),
        compiler_params=pltpu.CompilerParams(dimension_semantics=("arbitrary",), vmem_limit_bytes=VMEM_LIMIT),
    )(shard.reshape(1).astype(jnp.int32), *ws)


def _mod_rows(c8, w_ada, b_ada, name):
    n = w_ada.shape[1]

    def gather(src_ref, dst_ref, send, recv, loc, base):
        x, y, c = _place()
        me = 4 * x + 2 * y + c
        own = pltpu.make_async_copy(src_ref, dst_ref.at[me], loc)
        own.start()
        peers = _everyone(x, y, c)
        sends = [pltpu.make_async_remote_copy(src_ref=src_ref, dst_ref=dst_ref.at[me], send_sem=send.at[base + k],
                                              recv_sem=recv.at[base + k], device_id=p, device_id_type=MESH)
                 for k, p in enumerate(peers)]
        for cp in sends:
            cp.start()
        for k, p in enumerate(peers):
            pltpu.make_async_remote_copy(src_ref=src_ref, dst_ref=dst_ref.at[4 * p[0] + 2 * p[1] + p[2]],
                                         send_sem=send.at[base + k], recv_sem=recv.at[base + k], device_id=p,
                                         device_id_type=MESH).wait_recv()
        for cp in sends:
            cp.wait_send()
        own.wait()

    def body(c_ref, w_ref, b_ref, a_ref, parts_ref, c_all, part, send, recv, loc):
        gather(c_ref, c_all, send, recv, loc.at[0], 0)
        cv = jnp.max(c_all[...], axis=1)
        ca = cv * _sigmoid(cv)
        a_ref[...] = ca
        part[...] = jnp.dot(ca, w_ref[...], precision=lax.Precision.HIGHEST, preferred_element_type=F32) + b_ref[...]
        gather(part, parts_ref, send, recv, loc.at[1], 7)

    vmem = pl.BlockSpec(memory_space=pltpu.VMEM)
    return pl.pallas_call(
        body, name=name, in_specs=[vmem] * 3, out_specs=[vmem, vmem],
        out_shape=[_sds((8, D_MODEL)), _sds((8, 8, n))],
        scratch_shapes=[pltpu.VMEM((8, 8, D_MODEL), F32), pltpu.VMEM((8, n), F32), pltpu.SemaphoreType.DMA((14,)),
                        pltpu.SemaphoreType.DMA((14,)), pltpu.SemaphoreType.DMA((2,))],
        compiler_params=pltpu.CompilerParams(vmem_limit_bytes=VMEM_LIMIT))(c8, w_ada, b_ada)


def _sum_received(gs, shard, lands, name):
    n = len(gs)

    def body(s_ref, *refs):
        for g_ref, l_ref, o_ref in zip(refs[:n], refs[n:2 * n], refs[2 * n:]):
            o_ref[...] = ((g_ref[0] + l_ref[0].astype(F32)) + l_ref[1].astype(F32)) + l_ref[2].astype(F32)

    quarter = lambda g: (g.shape[1] // 4, g.shape[2])
    return pl.pallas_call(
        body, name=name, out_shape=[_sds(g.shape[1:]) for g in gs],
        grid_spec=pltpu.PrefetchScalarGridSpec(
            num_scalar_prefetch=1, grid=(4,),
            in_specs=[pl.BlockSpec((1,) + quarter(g), lambda i, s: (s[0], i, 0)) for g in gs]
            + [pl.BlockSpec((3,) + quarter(g), lambda i, s: (0, i, 0)) for g in gs],
            out_specs=[pl.BlockSpec(quarter(g), lambda i, s: (i, 0)) for g in gs]),
        compiler_params=pltpu.CompilerParams(dimension_semantics=("arbitrary",), vmem_limit_bytes=VMEM_LIMIT),
    )(shard.reshape(1).astype(jnp.int32), *gs, *lands)


def _adamw_outer(w, ct, dm, m, v, name):
    k, n = w.shape
    tr = k // 4

    def body(w_ref, c_ref, d_ref, m_ref, v_ref, g_out, d_out, m_out, v_out):
        cv = c_ref[...]
        dv = d_ref[...]
        g = cv[:, 0:1] * dv[0:1, :]
        for i in range(1, 8):
            g = g + cv[:, i:i + 1] * dv[i:i + 1, :]
        g_out[...] = g
        d_out[...], m_out[...], v_out[...] = _adamw_math(w_ref[...], g, m_ref[...], v_ref[...])

    row = _rows(tr, n)
    return _call(body, name=name, grid=(4,),
                 in_specs=[row, _rows(tr, 8), pl.BlockSpec((8, n), lambda i: (0, 0)), row, row],
                 out_specs=[row] * 4, out_shape=[_sds((k, n))] * 4)(w, ct, dm, m, v)


def _adamw_math(w, g, m, v):
    m_new = ADAM_B1 * m + (1.0 - ADAM_B1) * g
    v_new = ADAM_B2 * v + (1.0 - ADAM_B2) * (g * g)
    m_hat = m_new / (1.0 - ADAM_B1 ** ADAM_STEP)
    v_hat = v_new / (1.0 - ADAM_B2 ** ADAM_STEP)
    return -ADAM_LR * (m_hat / (jnp.sqrt(v_hat) + ADAM_EPS) + ADAM_WD * w), m_new, v_new


def _small_update(stats, smalls, name):
    offsets = [ST_DMOD, ST_DG1, ST_DLB, ST_DOG, ST_DAG, ST_DG2, ST_DFG]
    lb_index = 2

    def body(*refs):
        s_ref, ins, l_ref, outs = refs[0], refs[1:22], refs[22], refs[23:]
        tot = s_ref[0:1, :]
        for i in range(1, 8):
            tot = tot + s_ref[i:i + 1, :]
        l_ref[...] = jnp.zeros((1, 128), F32) + (0.5 / D_MODEL) * jnp.sum(tot[:, ST_LOSS:ST_LOSS + D_MODEL])
        for p, off in enumerate(offsets):
            w_ref, m_ref, v_ref = ins[3 * p:3 * p + 3]
            g_out, d_out, m_out, v_out = outs[4 * p:4 * p + 4]
            g = tot[:, off:off + w_ref.shape[1]]
            if p == lb_index:
                lg = w_ref[...]
                lb = _sigmoid(lg[0:1] - lg[1:2])
                g = g * lb * (1.0 - lb)
            for r in range(w_ref.shape[0]):
                rows = slice(r, r + 1)
                gr = g if r == 0 else -g
                delta, m_new, v_new = _adamw_math(w_ref[rows, :], gr, m_ref[rows, :], v_ref[rows, :])
                g_out[rows, :] = gr
                d_out[rows, :] = delta
                m_out[rows, :] = m_new
                v_out[rows, :] = v_new

    full = lambda a: pl.BlockSpec(a.shape, lambda i: (0, 0))
    flat = [a for t in smalls for a in t]
    return _call(body, name=name, grid=(1,),
                 in_specs=[full(stats)] + [full(a) for a in flat],
                 out_specs=[pl.BlockSpec((1, 128), lambda i: (0, 0))] + [full(t[0]) for t in smalls for _ in range(4)],
                 out_shape=[_sds((1, 128))] + [_sds(t[0].shape) for t in smalls for _ in range(4)])(stats, *flat)


def _adamw(params, name):
    n = len(params)

    def body(*refs):
        for p in range(n):
            w_ref, ga_ref, gb_ref, m_ref, v_ref = refs[5 * p:5 * p + 5]
            g_out, d_out, m_out, v_out = refs[5 * n + 4 * p:5 * n + 4 * p + 4]
            g = ga_ref[...] + gb_ref[...]
            g_out[...] = g
            d_out[...], m_out[...], v_out[...] = _adamw_math(w_ref[...], g, m_ref[...], v_ref[...])

    row = lambda w: _rows(w.shape[0] // 4, w.shape[1])
    out = _call(body, name=name, grid=(4,), in_specs=[row(p[0]) for p in params for _ in range(5)],
                out_specs=[row(p[0]) for p in params for _ in range(4)],
                out_shape=[_sds(p[0].shape) for p in params for _ in range(4)])(*[a for p in params for a in p])
    return [tuple(out[4 * p:4 * p + 4]) for p in range(n)]


def kernel(x, c, w_ada, b_ada, norm1_g, w_in, hg_lb_logits, hg_onorm_g, att_onorm_g, w_out, norm2_g, w_gate_up, w_down, final_g, loss_target, m_w_ada, m_b_ada, m_norm1_g, m_w_in, m_hg_lb_logits, m_hg_onorm_g, m_att_onorm_g, m_w_out, m_norm2_g, m_w_gate_up, m_w_down, m_final_g, v_w_ada, v_b_ada, v_norm1_g, v_w_in, v_hg_lb_logits, v_hg_onorm_g, v_att_onorm_g, v_w_out, v_norm2_g, v_w_gate_up, v_w_down, v_final_g):
    ix, iy, ic = _place()
    shard = 2 * ix + iy
    sample = 4 * ix + 2 * iy + ic
    n_ada = w_ada.shape[2]

    shards = [w_in[0], w_out[0], w_gate_up[0], w_down[0]]
    names = ["w_in", "w_out", "w_gu", "w_down"]
    shapes = [(N_SHARD,) + w.shape for w in shards]
    placed = [(_cast_place(shards[:1], shard, "place_w_in")[0],)]
    placed += [(p,) for p in _cast_place(shards[1:], shard, "place_rest")]

    b_part = lax.dynamic_slice(b_ada, (0, shard * n_ada), (1, n_ada))
    c_act, parts = _mod_rows(jnp.broadcast_to(c, (8, D_MODEL)), w_ada[0], b_part, "mod_rows")
    parts = parts[::2]
    mod = lax.dynamic_index_in_dim(parts, sample, axis=1, keepdims=False).reshape(1, 6 * D_MODEL)
    (first,), mod = _exchange_start(placed[:1], mod, "half", "gather_start_w_in")
    gathering = {}

    def get_w(name, after):
        if name == "w_in":
            halves = _exchange_wait(first, after, "half", "gather_wait_w_in")
            (passing,), token = _exchange_start([tuple(halves)], mod, "forward", "forward_start_w_in")
            (full,) = _exchange_wait(passing, token, "forward", "forward_wait_w_in")
            rest, full = _exchange_start(placed[1:], full, "gather", "gather_start_rest")
            gathering.update(zip(names[1:], rest))
            return full
        (full,) = _exchange_wait(gathering[name], after, "gather", "gather_wait_" + name)
        return full if name == "w_gu" else full.reshape(1, -1, D_MODEL)

    scattering = {}

    def put_g(name, g, g_bf16, then):
        shape = shapes[names.index(name)]
        land = lax.empty((3,) + shape[1:], BF16)
        (started,), then = _exchange_start([(g_bf16.reshape(shape), land)], then, "scatter", "scatter_start_" + name)
        scattering[name] = (g.reshape(shape), started)
        return then

    def summed(group, after, tag):
        lands = [_exchange_wait(scattering[nm][1], after, "scatter", "scatter_wait_" + nm)[1] for nm in group]
        return _sum_received([scattering[nm][0] for nm in group], shard, lands, "sum_" + tag)

    early = ["w_down", "w_gu", "w_out"]
    swapping = []

    def late(a):
        started, a = _sibling_start(summed(early, a, "early"), a, "swap_start")
        swapping.append(started)
        return a

    def project(h1):
        own = _mm_own_shard(h1, shards[0], shard, N_SHARD, "mm_in_own")
        w_full = get_w("w_in", own)
        return _mm_other_shards(h1, w_full, own, shard, "mm_in_rest"), w_full

    dx, stats = _local_step(x[0], loss_target[0], mod, norm1_g, hg_lb_logits, hg_onorm_g, att_onorm_g,
                            norm2_g, final_g, get_w, put_g, late, project)

    gathering_stats, stats = _all_gather_start(stats, "stats_start")
    moments = [(m_w_in, v_w_in), (m_w_out, v_w_out), (m_w_gate_up, v_w_gate_up), (m_w_down, v_w_down)]

    def update(group, sums, other, tag):
        params = [(shards[names.index(nm)], s, o, moments[names.index(nm)][0][0], moments[names.index(nm)][1][0])
                  for nm, s, o in zip(group, sums, other)]
        return dict(zip(group, _adamw(params, "adamw_" + tag)))

    sums, other = _sibling_wait(swapping[0], stats, "swap_wait")
    done = update(early, sums, other, "early")
    sum_in = summed(["w_in"], done["w_out"][1], "w_in")
    done.update(update(["w_in"], sum_in, _swap_sibling(sum_in, "swap_sum_in"), "w_in"))

    stats_all = _all_gather_wait(gathering_stats, done["w_in"][1], "stats_wait").reshape(8, ST_WIDTH)
    dmod = lax.dynamic_slice(stats_all, (0, ST_DMOD + shard * n_ada), (8, n_ada))

    as_row = lambda a: a.reshape(1, -1) if a.ndim == 1 else a
    smalls = [tuple(as_row(a) for a in t) for t in [
        (b_ada, m_b_ada, v_b_ada), (norm1_g, m_norm1_g, v_norm1_g),
        (hg_lb_logits, m_hg_lb_logits, v_hg_lb_logits), (hg_onorm_g, m_hg_onorm_g, v_hg_onorm_g),
        (att_onorm_g, m_att_onorm_g, v_att_onorm_g), (norm2_g, m_norm2_g, v_norm2_g),
        (final_g, m_final_g, v_final_g)]]
    loss, *small_out = _small_update(stats_all, smalls, "small_update")
    shapes_out = [b_ada.shape, norm1_g.shape, hg_lb_logits.shape, hg_onorm_g.shape, att_onorm_g.shape,
                  norm2_g.shape, final_g.shape]
    sg, sd, sm, sv = [[small_out[4 * p + i].reshape(shapes_out[p]) for p in range(7)] for i in range(4)]

    ada = _adamw_outer(w_ada[0], c_act.T, dmod, m_w_ada[0], v_w_ada[0], "adamw_w_ada")
    big = [ada] + [done[nm] for nm in names]
    bg, bd, bm, bv = [[t[i][None] for t in big] for i in range(4)]

    def order(b, s):
        return [b[0], s[0], s[1], b[1], s[2], s[3], s[4], b[2], s[5], b[3], b[4], s[6]]

    return (loss[0, 0], dx[None], *order(bg, sg), *order(bd, sd), *order(bm, sm), *order(bv, sv))
```

```python
import functools

import jax
import jax.numpy as jnp
from jax import lax
from jax.experimental import pallas as pl
from jax.experimental.pallas import tpu as pltpu

F32 = jnp.float32
BF16 = jnp.bfloat16
MESH = pl.DeviceIdType.MESH

D_MODEL = 1024
HG_WIDTH = 512
HG_HEAD = 128
HG_CHUNK = 64
HG_GROUP = 4
ATT_WIDTH = 512
ATT_HEADS = 8
ATT_BLOCK = 128
DILATIONS = (1, 4, 16)
D_FF = 2816
IN_WIDTH = 3584
N_SHARD = 4
RMS_EPS = 1e-6
NEG = -1e30

ADAM_LR = 0.001
ADAM_B1 = 0.9
ADAM_B2 = 0.999
ADAM_EPS = 1e-08
ADAM_WD = 0.01
ADAM_STEP = 10

VMEM_LIMIT = 56 * 2**20

ST_LOSS, ST_DFG, ST_DG2, ST_DG1 = 0, 1024, 2048, 3072
ST_DLB, ST_DAG, ST_DOG, ST_DMOD = 4096, 4608, 5120, 5248
ST_WIDTH = 5248 + 6144


def _call(body, *, name, grid, in_specs, out_specs, out_shape, scratch_shapes=()):
    return pl.pallas_call(
        body, name=name, grid=grid, in_specs=in_specs, out_specs=out_specs, out_shape=out_shape,
        scratch_shapes=list(scratch_shapes),
        compiler_params=pltpu.CompilerParams(
            dimension_semantics=("arbitrary",) * len(grid), vmem_limit_bytes=VMEM_LIMIT))


def _sds(shape, dtype=F32):
    return jax.ShapeDtypeStruct(shape, dtype)


def _dot(a, b):
    return jnp.dot(a, b, preferred_element_type=F32)


def _dot_nt(a, b):
    return lax.dot_general(a, b, (((1,), (1,)), ((), ())), preferred_element_type=F32)


def _dot_tn(a, b):
    return lax.dot_general(a, b, (((0,), (0,)), ((), ())), preferred_element_type=F32)


def _sigmoid(x):
    return 1.0 / (1.0 + jnp.exp(-x))


def _rows(tr, width):
    return pl.BlockSpec((tr, width), lambda i: (i, 0))


def _vec(width):
    return pl.BlockSpec((1, width), lambda i: (0, 0))


def _acc(ref, val, first):
    @pl.when(first)
    def _():
        ref[...] = val

    @pl.when(jnp.logical_not(first))
    def _():
        ref[...] += val


def _sub_rows(tr, fine, width):
    return pl.BlockSpec((fine, tr // fine, width), lambda i: (0, i, 0))


def _regroup_matrix(tr, groups):
    a = lax.broadcasted_iota(jnp.int32, (tr, tr), 0)
    b = lax.broadcasted_iota(jnp.int32, (tr, tr), 1)
    return (b == (a % groups) * (tr // groups) + a // groups).astype(BF16)


def _regroup(m, v):
    if v.dtype == BF16:
        return _dot(m, v)
    out = None
    for _ in range(3):
        part = v.astype(BF16)
        v = v - part.astype(F32)
        out = _dot(m, part) if out is None else out + _dot(m, part)
    return out


def _mm_nn(a, b3, name, tm=1024, out_dtype=F32):
    m, k = a.shape
    s, _, n = b3.shape

    def body(a_ref, b_ref, o_ref):
        o_ref[...] = _dot(a_ref[...], b_ref[0]).astype(out_dtype)

    return _call(
        body, name=name, grid=(s, m // tm),
        in_specs=[pl.BlockSpec((tm, k), lambda j, i: (i, 0)), pl.BlockSpec((1, k, n), lambda j, i: (j, 0, 0))],
        out_specs=pl.BlockSpec((tm, n), lambda j, i: (i, j)), out_shape=_sds((m, s * n), out_dtype))(a, b3)


def _mm_own_shard(a, w, shard, s, name, tm=1024):
    m, k = a.shape
    n = w.shape[1]

    def body(s_ref, a_ref, w_ref, o_ref):
        o_ref[...] = _dot(a_ref[...], w_ref[...].astype(BF16)).astype(BF16)

    return pl.pallas_call(
        body, name=name, out_shape=_sds((m, s * n), BF16),
        grid_spec=pltpu.PrefetchScalarGridSpec(
            num_scalar_prefetch=1, grid=(m // tm,),
            in_specs=[pl.BlockSpec((tm, k), lambda i, sh: (i, 0)), pl.BlockSpec((k, n), lambda i, sh: (0, 0))],
            out_specs=pl.BlockSpec((tm, n), lambda i, sh: (i, sh[0]))),
        compiler_params=pltpu.CompilerParams(dimension_semantics=("arbitrary",), vmem_limit_bytes=VMEM_LIMIT),
    )(shard.reshape(1).astype(jnp.int32), a, w)


def _mm_other_shards(a, b3, partial, shard, name, tm=1024):
    m, k = a.shape
    s, _, n = b3.shape
    which = lambda j, sh: (sh[0] + 1 + j) % s

    def body(s_ref, a_ref, b_ref, p_ref, o_ref):
        o_ref[...] = _dot(a_ref[...], b_ref[0]).astype(BF16)

    return pl.pallas_call(
        body, name=name, out_shape=_sds(partial.shape, BF16),
        grid_spec=pltpu.PrefetchScalarGridSpec(
            num_scalar_prefetch=1, grid=(s - 1, m // tm),
            in_specs=[pl.BlockSpec((tm, k), lambda j, i, sh: (i, 0)),
                      pl.BlockSpec((1, k, n), lambda j, i, sh: (which(j, sh), 0, 0)),
                      pl.BlockSpec(memory_space=pl.ANY)],
            out_specs=pl.BlockSpec((tm, n), lambda j, i, sh: (i, which(j, sh)))),
        input_output_aliases={3: 0},
        compiler_params=pltpu.CompilerParams(dimension_semantics=("arbitrary",) * 2, vmem_limit_bytes=VMEM_LIMIT),
    )(shard.reshape(1).astype(jnp.int32), a, b3, partial)


def _mm_tn(a, dy, s, name, tm, tk, group=1):
    m, k = a.shape
    n = dy.shape[1] // s
    steps = m // tm

    def body(a_ref, dy_ref, o_ref, ob_ref):
        p = _dot_tn(a_ref[...], dy_ref[...])
        for g in range(group):
            pg = p[:, g * n:(g + 1) * n]
            if steps == 1:
                o_ref[g] = pg
                ob_ref[g] = pg.astype(BF16)
            else:
                _acc(o_ref.at[g], pg, pl.program_id(2) == 0)
        if steps > 1:
            @pl.when(pl.program_id(2) == steps - 1)
            def _():
                ob_ref[...] = o_ref[...].astype(BF16)

    out = pl.BlockSpec((group, tk, n), lambda kk, j, i: (j, kk, 0))
    return _call(
        body, name=name, grid=(k // tk, s // group, steps),
        in_specs=[pl.BlockSpec((tm, tk), lambda kk, j, i: (i, kk)),
                  pl.BlockSpec((tm, group * n), lambda kk, j, i: (i, j))],
        out_specs=[out, out], out_shape=[_sds((s, k, n)), _sds((s, k, n), BF16)])(a, dy)


def _rms(x):
    return lax.rsqrt(jnp.mean(x * x, axis=-1, keepdims=True) + RMS_EPS)


def _rms_bwd(dxh, xh, r):
    return r * (dxh - xh * jnp.mean(dxh * xh, axis=-1, keepdims=True))


def _norm_mod(x, g, scale, shift, name, tr=512):
    t = x.shape[0]

    def body(x_ref, g_ref, sc_ref, sh_ref, h_ref):
        xv = x_ref[...]
        n = xv * _rms(xv) * g_ref[...]
        h_ref[...] = (n * (1.0 + sc_ref[...]) + sh_ref[...]).astype(BF16)

    return _call(body, name=name, grid=(t // tr,),
                 in_specs=[_rows(tr, D_MODEL), _vec(D_MODEL), _vec(D_MODEL), _vec(D_MODEL)],
                 out_specs=_rows(tr, D_MODEL), out_shape=_sds((t, D_MODEL), BF16))(x, g, scale, shift)


def _out_resid_norm_mod(x, mixin, w_out, gate, g, scale, shift, name, tr=512):
    t = x.shape[0]

    def body(x_ref, mi_ref, w_ref, gt_ref, g_ref, sc_ref, sh_ref, m_ref, x2_ref, h_ref):
        mix = _dot(mi_ref[...], w_ref[0])
        m_ref[...] = mix
        x2 = x_ref[...] + gt_ref[...] * mix
        x2_ref[...] = x2
        n = x2 * _rms(x2) * g_ref[...]
        h_ref[...] = (n * (1.0 + sc_ref[...]) + sh_ref[...]).astype(BF16)

    row = _rows(tr, D_MODEL)
    return _call(body, name=name, grid=(t // tr,),
                 in_specs=[row, row, pl.BlockSpec(w_out.shape, lambda i: (0, 0, 0))] + [_vec(D_MODEL)] * 4,
                 out_specs=[row, row, row],
                 out_shape=[_sds((t, D_MODEL)), _sds((t, D_MODEL)), _sds((t, D_MODEL), BF16)])(
                     x, mixin, w_out, gate, g, scale, shift)


def _mm_gate_up(h, w_gu, name, tm=1024):
    m, k = h.shape
    n = w_gu.shape[2]

    def body(h_ref, wa_ref, wu_ref, da_ref, du_ref, o_ref):
        hv = h_ref[...]
        a = _dot(hv, wa_ref[0])
        u = _dot(hv, wu_ref[0])
        sg = _sigmoid(a)
        silu = a * sg
        da_ref[...] = (u * sg * (1.0 + a * (1.0 - sg))).astype(BF16)
        du_ref[...] = silu.astype(BF16)
        o_ref[...] = (silu * u).astype(BF16)

    out = pl.BlockSpec((tm, n), lambda j, i: (i, j))
    return _call(body, name=name, grid=(2, m // tm),
                 in_specs=[pl.BlockSpec((tm, k), lambda j, i: (i, 0)), pl.BlockSpec((1, k, n), lambda j, i: (j, 0, 0)),
                           pl.BlockSpec((1, k, n), lambda j, i: (j + 2, 0, 0))],
                 out_specs=[out, out, out], out_shape=[_sds((m, 2 * n), BF16)] * 3)(h, w_gu, w_gu)


def _mm_down_dx(dffn, w_down, act_da, act_du, name, tm=512):
    m = dffn.shape[0]
    _, k, n = w_down.shape

    def body(d_ref, w_ref, da_ref, du_ref, o_ref):
        dact = _dot_nt(d_ref[...], w_ref[0])
        o_ref[:, :k] = (dact * da_ref[...].astype(F32)).astype(BF16)
        o_ref[:, k:] = (dact * du_ref[...].astype(F32)).astype(BF16)

    return _call(body, name=name, grid=(m // tm,),
                 in_specs=[_rows(tm, n), pl.BlockSpec((1, k, n), lambda i: (0, 0, 0)), _rows(tm, k), _rows(tm, k)],
                 out_specs=_rows(tm, 2 * k), out_shape=_sds((m, 2 * k), BF16))(dffn, w_down, act_da, act_du)


def _down_loss(x2, act, w_down, gate, fg, tgt, name, tr=512):
    t = x2.shape[0]
    _, k, n = w_down.shape

    def body(x_ref, a_ref, w_ref, gt_ref, fg_ref, t_ref, dx_ref, df_ref, l_ref, dfg_ref, dgt_ref):
        first = pl.program_id(0) == 0
        ffn_v = _dot(a_ref[...], w_ref[0])
        x3 = x_ref[...] + gt_ref[...] * ffn_v
        r = _rms(x3)
        xh = x3 * r
        err = xh * fg_ref[...] - t_ref[...]
        dy = err * (1.0 / D_MODEL)
        dx3 = _rms_bwd(dy * fg_ref[...], xh, r)
        dx_ref[...] = dx3
        df_ref[...] = (dx3 * gt_ref[...]).astype(BF16)
        _acc(l_ref, jnp.sum(err * err, axis=0, keepdims=True), first)
        _acc(dfg_ref, jnp.sum(dy * xh, axis=0, keepdims=True), first)
        _acc(dgt_ref, jnp.sum(dx3 * ffn_v, axis=0, keepdims=True), first)

    row, vec = _rows(tr, D_MODEL), _vec(D_MODEL)
    return _call(body, name=name, grid=(t // tr,),
                 in_specs=[row, _rows(tr, k), pl.BlockSpec((1, k, n), lambda i: (0, 0, 0)), vec, vec, row],
                 out_specs=[row, row, vec, vec, vec],
                 out_shape=[_sds((t, D_MODEL)), _sds((t, D_MODEL), BF16)] + [_sds((1, D_MODEL))] * 3)(
                     x2, act, w_down, gate, fg, tgt)


def _norm_mod_bwd(dh, x, g, scale, dres, name, gate=None, mix=None, w=None, tr=512):
    t = x.shape[0]
    below = gate is not None

    def body(*refs):
        if w is not None:
            w_ref, w_full, sem, refs = refs[1], refs[-2], refs[-1], refs[:1] + refs[2:-2]

            @pl.when(pl.program_id(0) == 0)
            def _():
                n = w.shape[2]
                copies = [pltpu.make_async_copy(w_ref.at[j], w_full.at[:, pl.ds(j * n, n)], sem.at[j])
                          for j in range(w.shape[0])]
                for cp in copies:
                    cp.start()
                for cp in copies:
                    cp.wait()

        if below:
            dh_ref, x_ref, g_ref, sc_ref, dr_ref, gt_ref, m_ref, dx_ref, dsh_ref, dsc_ref, dg_ref, dgt_ref, dm_ref = refs
        else:
            dh_ref, x_ref, g_ref, sc_ref, dr_ref, dx_ref, dsh_ref, dsc_ref, dg_ref = refs
        first = pl.program_id(0) == 0
        xv = x_ref[...]
        if w is None:
            dhv = dh_ref[...].astype(F32)
        else:
            dhv = _dot_nt(dh_ref[...], w_full[...])
        r = _rms(xv)
        xh = xv * r
        dn = dhv * (1.0 + sc_ref[...])
        dx = dr_ref[...] + _rms_bwd(dn * g_ref[...], xh, r)
        dx_ref[...] = dx
        _acc(dsh_ref, jnp.sum(dhv, axis=0, keepdims=True), first)
        _acc(dsc_ref, jnp.sum(dhv * xh * g_ref[...], axis=0, keepdims=True), first)
        _acc(dg_ref, jnp.sum(dn * xh, axis=0, keepdims=True), first)
        if below:
            _acc(dgt_ref, jnp.sum(dx * m_ref[...], axis=0, keepdims=True), first)
            dm_ref[...] = (dx * gt_ref[...]).astype(BF16)

    row, vec = _rows(tr, D_MODEL), _vec(D_MODEL)
    first_specs = [row] if w is None else [_rows(tr, dh.shape[1]), pl.BlockSpec(memory_space=pl.ANY)]
    scratch = [] if w is None else [pltpu.VMEM((w.shape[1], dh.shape[1]), BF16), pltpu.SemaphoreType.DMA((w.shape[0],))]
    in_specs = first_specs + [row, vec, vec, row] + ([vec, row] if below else [])
    out_specs = [row, vec, vec, vec] + ([vec, row] if below else [])
    out_shape = [_sds((t, D_MODEL))] + [_sds((1, D_MODEL))] * 3 + ([_sds((1, D_MODEL)), _sds((t, D_MODEL), BF16)] if below else [])
    args = ((dh,) if w is None else (dh, w)) + (x, g, scale, dres) + ((gate, mix) if below else ())
    return _call(body, name=name, grid=(t // tr,), in_specs=in_specs, out_specs=out_specs, out_shape=out_shape,
                 scratch_shapes=scratch)(*args)


def _mix_in_bwd(dmix, w_out, o_hg, proj, att, og, ag, fine, name, tr=512):
    t = o_hg.shape[0]

    def body(dy_ref, w_ref, o_ref, g_ref, a_ref, og_ref, ag_ref,
             do_ref, dg_ref, da_ref, dd_ref, das_ref, dds_ref, dog_ref, dag_ref, to_sub):
        first = pl.program_id(0) == 0

        @pl.when(first)
        def _():
            to_sub[...] = _regroup_matrix(tr, tr // fine)

        dmi = _dot_nt(dy_ref[...], w_ref[0])
        dog = jnp.zeros((1, HG_HEAD), F32)
        for h in range(HG_WIDTH // HG_HEAD):
            sl = slice(h * HG_HEAD, (h + 1) * HG_HEAD)
            oh = o_ref[:, sl].astype(F32)
            gv = g_ref[:, sl].astype(F32)
            dv = dmi[:, sl]
            r = _rms(oh)
            xh = oh * r
            sg = _sigmoid(gv)
            dno = dv * gv * sg
            dg_ref[:, sl] = (dv * xh * og_ref[...] * sg * (1.0 + gv * (1.0 - sg))).astype(BF16)
            dog = dog + jnp.sum(dno * xh, axis=0, keepdims=True)
            do_ref[:, sl] = _rms_bwd(dno * og_ref[...], xh, r).astype(BF16)
        _acc(dog_ref, dog, first)
        av = a_ref[...]
        dav = dmi[:, HG_WIDTH:]
        r = _rms(av)
        xa = av * r
        _acc(dag_ref, jnp.sum(dav * xa, axis=0, keepdims=True), first)
        datt = _rms_bwd(dav * ag_ref[...], xa, r)
        da_ref[...] = datt.astype(BF16)
        das_ref[...] = _regroup(to_sub[...], datt.astype(BF16)).astype(BF16).reshape(das_ref.shape)
        prod = datt * av
        lane = lax.broadcasted_iota(jnp.int32, (1, 128), 1)
        dd = jnp.zeros((tr, 128), F32)
        for hp in range(ATT_HEADS // 2):
            pp = prod[:, hp * 128:(hp + 1) * 128]
            lo = jnp.sum(jnp.where(lane < 64, pp, 0.0), axis=-1, keepdims=True)
            hi = jnp.sum(jnp.where(lane >= 64, pp, 0.0), axis=-1, keepdims=True)
            dd = jnp.where(lane == 2 * hp, lo, dd)
            dd = jnp.where(lane == 2 * hp + 1, hi, dd)
        dd_ref[...] = dd
        dds_ref[...] = _regroup(to_sub[...], dd).reshape(dds_ref.shape)

    half = _rows(tr, HG_WIDTH)
    out = _call(body, name=name, grid=(t // tr,),
                in_specs=[_rows(tr, D_MODEL), pl.BlockSpec(w_out.shape, lambda i: (0, 0, 0)), half,
                          pl.BlockSpec((tr, HG_WIDTH), lambda i: (i, 3)), half, _vec(HG_HEAD), _vec(ATT_WIDTH)],
                out_specs=[half, half, half, _rows(tr, 128), _sub_rows(tr, fine, ATT_WIDTH), _sub_rows(tr, fine, 128),
                           _vec(HG_HEAD), _vec(ATT_WIDTH)],
                out_shape=[_sds((t, HG_WIDTH), BF16)] * 3 + [_sds((t, 128)), _sds((fine, t // fine, ATT_WIDTH), BF16),
                           _sds((fine, t // fine, 128)), _sds((1, HG_HEAD)), _sds((1, ATT_WIDTH))],
                scratch_shapes=[pltpu.VMEM((tr, tr), BF16)])(dmix, w_out, o_hg, proj, att, og, ag)
    out = list(out)
    return out[:4] + [out[4].reshape(t, ATT_WIDTH), out[5].reshape(t, 128)] + out[6:]


def _dproj(dhg, dg, dqkvs, name, tr=1024):
    t = dhg.shape[0]
    w3 = 3 * HG_WIDTH
    w4 = w3 + HG_WIDTH

    def body(h_ref, g_ref, *refs):
        o_ref = refs[-1]
        o_ref[:, :w3] = h_ref[...]
        o_ref[:, w3:w4] = g_ref[...]
        o_ref[:, w4:] = sum(r[...].astype(F32) for r in refs[:-1]).astype(BF16)

    return _call(body, name=name, grid=(t // tr,),
                 in_specs=[_rows(tr, w3), _rows(tr, HG_WIDTH)] + [_rows(tr, 3 * ATT_WIDTH)] * len(dqkvs),
                 out_specs=_rows(tr, IN_WIDTH), out_shape=_sds((t, IN_WIDTH), BF16))(dhg, dg, *dqkvs)


def _chunk_tri(upper):
    row = lax.broadcasted_iota(jnp.int32, (HG_GROUP, HG_CHUNK, HG_CHUNK), 1)
    col = lax.broadcasted_iota(jnp.int32, (HG_GROUP, HG_CHUNK, HG_CHUNK), 2)
    return (row <= col if upper else row >= col).astype(BF16)


def _chunk_cumsum(x, tri):
    x3 = x.reshape(HG_GROUP, HG_CHUNK, x.shape[1])
    dims = (((2,), (1,)), ((0,), (0,)))
    out = None
    for _ in range(3):
        part = x3.astype(BF16)
        x3 = x3 - part.astype(F32)
        term = lax.dot_general(tri, part, dims, preferred_element_type=F32)
        out = term if out is None else out + term
    return out.reshape(x.shape)


def _hg_gates(f_raw, q_raw, lb, tri):
    sg = _sigmoid(f_raw)
    f = lb + (1.0 - lb) * sg
    k = 1.0 - f
    b = _chunk_cumsum(jnp.log(f), tri)
    sq = _sigmoid(q_raw)
    return sg, f, k, b, sq


def _hg_masks(rows):
    row = lax.broadcasted_iota(jnp.int32, (rows, rows), 0)
    col = lax.broadcasted_iota(jnp.int32, (rows, rows), 1)
    same = (row // HG_CHUNK) == (col // HG_CHUNK)
    return jnp.logical_and(row >= col, same), jnp.logical_and(row <= col, same)


def _per_chunk(rows_of):
    return jnp.concatenate([jnp.broadcast_to(r, (HG_CHUNK, r.shape[1])) for r in rows_of], axis=0)


def _hgrn_fwd(proj, lb_logits, name):
    t = proj.shape[0]
    nc = t // HG_CHUNK
    nh = HG_WIDTH // HG_HEAD
    rows = HG_GROUP * HG_CHUNK

    def body(q_ref, f_ref, i_ref, lg_ref, o_ref, st_ref, s_scr):
        @pl.when(pl.program_id(0) == 0)
        def _():
            s_scr[...] = jnp.zeros_like(s_scr)

        lg = lg_ref[...]
        lb_all = _sigmoid(lg[0:1] - lg[1:2])
        causal, _ = _hg_masks(rows)
        tri = _chunk_tri(False)
        for h in range(nh):
            sl = slice(h * HG_HEAD, (h + 1) * HG_HEAD)
            q_raw = q_ref[:, sl].astype(F32)
            _, _, k, b, sq = _hg_gates(f_ref[:, sl].astype(F32), q_raw, lb_all[:, sl], tri)
            v = i_ref[:, sl].astype(BF16)
            gls = [b[(g + 1) * HG_CHUNK - 1:(g + 1) * HG_CHUNK] for g in range(HG_GROUP)]
            bm = _per_chunk([b[g * HG_CHUNK + HG_CHUNK // 2 - 1:g * HG_CHUNK + HG_CHUNK // 2] for g in range(HG_GROUP)])
            qd = (q_raw * sq * jnp.exp(b)).astype(BF16)
            qm = (q_raw * sq * jnp.exp(b - bm)).astype(BF16)
            km = (k * jnp.exp(bm - b)).astype(BF16)
            ke = (k * jnp.exp(_per_chunk(gls) - b)).astype(BF16)
            a = jnp.where(causal, _dot_nt(qm, km), 0.0).astype(BF16)
            o_intra = _dot(a, v)
            st = s_scr[h]
            o_inter = []
            for g in range(HG_GROUP):
                rs = slice(g * HG_CHUNK, (g + 1) * HG_CHUNK)
                st_ref[g, sl, :] = st
                o_inter.append(_dot_nt(qd[rs], st.astype(BF16)))
                st = st * jnp.exp(gls[g]) + _dot_tn(v[rs], ke[rs])
            s_scr[h] = st
            o_ref[:, sl] = (o_intra + jnp.concatenate(o_inter, axis=0)).astype(BF16)

    blk = lambda j: pl.BlockSpec((rows, HG_WIDTH), lambda c: (c, j))
    return _call(body, name=name, grid=(nc // HG_GROUP,),
                 in_specs=[blk(0), blk(1), blk(2), pl.BlockSpec((2, HG_WIDTH), lambda c: (0, 0))],
                 out_specs=[blk(0), pl.BlockSpec((HG_GROUP, HG_WIDTH, HG_HEAD), lambda c: (c, 0, 0))],
                 out_shape=[_sds((t, HG_WIDTH), BF16), _sds((nc, HG_WIDTH, HG_HEAD))],
                 scratch_shapes=[pltpu.VMEM((nh, HG_HEAD, HG_HEAD), F32)])(proj, proj, proj, lb_logits)


def _hgrn_bwd(proj, lb_logits, states, do, name):
    t = proj.shape[0]
    ng = t // (HG_GROUP * HG_CHUNK)
    nh = HG_WIDTH // HG_HEAD
    rows = HG_GROUP * HG_CHUNK

    def body(q_ref, f_ref, i_ref, lg_ref, st_ref, do_ref, d_ref, dlb_ref, ds_scr):
        first = pl.program_id(0) == 0

        @pl.when(first)
        def _():
            ds_scr[...] = jnp.zeros_like(ds_scr)

        lg = lg_ref[...]
        lb_all = _sigmoid(lg[0:1] - lg[1:2])
        causal, _ = _hg_masks(rows)
        tri = _chunk_tri(False)
        tri_t = _chunk_tri(True)
        dlb = []
        for h in range(nh):
            sl = slice(h * HG_HEAD, (h + 1) * HG_HEAD)
            q_raw = q_ref[:, sl].astype(F32)
            lb = lb_all[:, sl]
            sg, f, k, b, sq = _hg_gates(f_ref[:, sl].astype(F32), q_raw, lb, tri)
            v = i_ref[:, sl].astype(BF16)
            gls = [b[(g + 1) * HG_CHUNK - 1:(g + 1) * HG_CHUNK] for g in range(HG_GROUP)]
            bm = _per_chunk([b[g * HG_CHUNK + HG_CHUNK // 2 - 1:g * HG_CHUNK + HG_CHUNK // 2] for g in range(HG_GROUP)])
            eb = jnp.exp(b)
            ebm = jnp.exp(b - bm)
            emb = jnp.exp(bm - b)
            egb = jnp.exp(_per_chunk(gls) - b)
            ke = k * egb
            qd_b, qm_b = (q_raw * sq * eb).astype(BF16), (q_raw * sq * ebm).astype(BF16)
            km_b, ke_b = (k * emb).astype(BF16), ke.astype(BF16)
            dov = do_ref[:, sl].astype(BF16)
            a = jnp.where(causal, _dot_nt(qm_b, km_b), 0.0).astype(BF16)
            da = jnp.where(causal, _dot_nt(dov, v), 0.0).astype(BF16)
            dkm = _dot_tn(da, qm_b)
            dst = ds_scr[h]
            dqd_s, dv_s, dke_s, dgl_s = [None] * HG_GROUP, [None] * HG_GROUP, [None] * HG_GROUP, [None] * HG_GROUP
            for g in reversed(range(HG_GROUP)):
                rs = slice(g * HG_CHUNK, (g + 1) * HG_CHUNK)
                st = st_ref[g, sl, :]
                dst_b = dst.astype(BF16)
                egl = jnp.exp(gls[g])
                dqd_s[g] = _dot(dov[rs], st.astype(BF16))
                dv_s[g] = _dot_nt(ke_b[rs], dst_b)
                dke_s[g] = _dot(v[rs], dst_b)
                dgl_s[g] = jnp.sum(dst * st, axis=0, keepdims=True) * egl
                dst = _dot_tn(dov[rs], qd_b[rs]) + dst * egl
            ds_scr[h] = dst
            dqm = _dot(da, km_b)
            dqd = jnp.concatenate(dqd_s, axis=0)
            dv = _dot_tn(a, dov) + jnp.concatenate(dv_s, axis=0)
            dke = jnp.concatenate(dke_s, axis=0)
            t1 = dke * ke
            db = dqm * qm_b.astype(F32) - dkm * km_b.astype(F32) + dqd * qd_b.astype(F32) - t1
            dgl = _per_chunk([dgl_s[g] + jnp.sum(t1[g * HG_CHUNK:(g + 1) * HG_CHUNK], axis=0, keepdims=True)
                              for g in range(HG_GROUP)])
            dlf = _chunk_cumsum(db, tri_t) + dgl
            df = dlf / f - (dkm * emb + dke * egb)
            d_ref[:, sl] = ((dqm * ebm + dqd * eb) * sq * (1.0 + q_raw * (1.0 - sq))).astype(BF16)
            d_ref[:, HG_WIDTH + h * HG_HEAD:HG_WIDTH + (h + 1) * HG_HEAD] = (
                df * (1.0 - lb) * sg * (1.0 - sg)).astype(BF16)
            d_ref[:, 2 * HG_WIDTH + h * HG_HEAD:2 * HG_WIDTH + (h + 1) * HG_HEAD] = dv.astype(BF16)
            dlb.append(jnp.sum(df * (1.0 - sg), axis=0, keepdims=True))
        _acc(dlb_ref, jnp.concatenate(dlb, axis=1), first)

    rev = lambda j: pl.BlockSpec((rows, HG_WIDTH), lambda c: (ng - 1 - c, j))
    return _call(body, name=name, grid=(ng,),
                 in_specs=[rev(0), rev(1), rev(2), pl.BlockSpec((2, HG_WIDTH), lambda c: (0, 0)),
                           pl.BlockSpec((HG_GROUP, HG_WIDTH, HG_HEAD), lambda c: (ng - 1 - c, 0, 0)), rev(0)],
                 out_specs=[pl.BlockSpec((rows, 3 * HG_WIDTH), lambda c: (ng - 1 - c, 0)), _vec(HG_WIDTH)],
                 out_shape=[_sds((t, 3 * HG_WIDTH), BF16), _sds((1, HG_WIDTH))],
                 scratch_shapes=[pltpu.VMEM((nh, HG_HEAD, HG_HEAD), F32)])(proj, proj, proj, lb_logits, states, do)


def _to_sub(a, dil):
    t, w = a.shape
    return a if dil == 1 else a.reshape(t // dil, dil, w).transpose(1, 0, 2).reshape(t, w)


def _from_sub(a, dil):
    t, w = a.shape
    return a if dil == 1 else a.reshape(dil, t // dil, w).transpose(1, 0, 2).reshape(t, w)


def _att_mask(has_prev, seg):
    def place(v):
        v = v % ATT_BLOCK
        return v if seg == 1 else seg * (v % (ATT_BLOCK // seg)) + v // (ATT_BLOCK // seg)

    row = lax.broadcasted_iota(jnp.int32, (2 * ATT_BLOCK, 2 * ATT_BLOCK), 0)
    col = lax.broadcasted_iota(jnp.int32, (2 * ATT_BLOCK, 2 * ATT_BLOCK), 1)
    qi, kj = place(row), place(col)
    prev = jnp.logical_and(jnp.logical_and(col < ATT_BLOCK, kj >= qi), has_prev)
    cur = jnp.logical_and(col >= ATT_BLOCK, kj <= qi)
    return jnp.logical_or(prev, cur), lax.broadcasted_iota(jnp.int32, (1, 128), 1)


def _get(ref, sl):
    if len(ref.shape) == 2:
        return ref[:, sl]
    v = ref[:, :, sl]
    return v.reshape(ATT_BLOCK, v.shape[2])


def _put(ref, sl, val):
    if len(ref.shape) == 2:
        ref[:, sl] = val
    else:
        ref[:, :, sl] = val.reshape(ref.shape[0], ref.shape[1], val.shape[1])


def _att_spec(nb, dil, seg, width, col, back):
    bps = nb // dil

    def plain(n):
        return jnp.clip(n - back, 0, nb - 1), col

    def segmented(n):
        m = jnp.clip(n - back, 0, nb - 1)
        return 0, m // bps, m % bps, 0, col

    if seg == 1:
        return pl.BlockSpec((ATT_BLOCK, width), plain)
    return pl.BlockSpec((seg, None, None, ATT_BLOCK // seg, width), segmented)


def _att_shape(nb, dil, seg, width):
    t = nb * ATT_BLOCK
    return (t, width) if seg == 1 else (seg, dil, nb // dil, ATT_BLOCK // seg, width)


def _att_view(a, nb, dil, seg):
    return a.reshape(_att_shape(nb, dil, seg, a.shape[1]))


def _attn_fwd_block(q_ref, kc_ref, kp_ref, vc_ref, vp_ref, o_ref, l_ref, has_prev, seg, lane0):
    mask, lane = _att_mask(has_prev, seg)
    lo = lane < 64
    nq = ATT_BLOCK
    lse_all = jnp.zeros((nq, 128), F32)
    for hp in range(ATT_HEADS // 2):
        sl = slice(hp * 128, (hp + 1) * 128)
        q2 = _get(q_ref, sl)
        zero = jnp.zeros_like(q2)
        q2 = q2 * 0.125
        qs = jnp.concatenate([jnp.where(lo, q2, zero), jnp.where(lo, zero, q2)], axis=0)
        kk = jnp.concatenate([_get(kp_ref, sl), _get(kc_ref, sl)], axis=0)
        vv = jnp.concatenate([_get(vp_ref, sl), _get(vc_ref, sl)], axis=0)
        s = jnp.where(mask, _dot_nt(qs, kk), NEG)
        mx = jnp.max(s, axis=-1, keepdims=True)
        p = jnp.exp(s - mx)
        l = jnp.sum(p, axis=-1, keepdims=True)
        o = _dot(p.astype(BF16), vv) * (1.0 / l)
        _put(o_ref, sl, jnp.where(lo, o[:nq], o[nq:]).astype(BF16))
        lse = mx + jnp.log(l)
        lse_all = jnp.where(lane == lane0 + 2 * hp, lse[:nq], lse_all)
        lse_all = jnp.where(lane == lane0 + 2 * hp + 1, lse[nq:], lse_all)
    _put(l_ref, slice(None), lse_all)


def _attn_fwd(branches, name):
    t = branches[0][0].shape[0]
    nb = t // ATT_BLOCK
    nbr = len(branches)

    def body(*refs):
        n = pl.program_id(0)
        for i, (_, dil, seg) in enumerate(branches):
            _attn_fwd_block(*refs[5 * i:5 * i + 5], *refs[5 * nbr + 2 * i:5 * nbr + 2 * i + 2],
                            (n % (nb // dil)) != 0, seg, ATT_HEADS * i)

    in_specs, args, out_specs, out_shape = [], [], [], []
    for qkv, dil, seg in branches:
        c0 = qkv.shape[1] // ATT_WIDTH - 3
        in_specs += [_att_spec(nb, dil, seg, ATT_WIDTH, c0 + j, back) for j, back in [(0, 0), (1, 0), (1, 1), (2, 0), (2, 1)]]
        args += [_att_view(qkv, nb, dil, seg)] * 5
        out_specs += [_att_spec(nb, dil, seg, ATT_WIDTH, 0, 0), _att_spec(nb, dil, seg, 128, 0, 0)]
        out_shape += [_sds(_att_shape(nb, dil, seg, ATT_WIDTH), BF16), _sds(_att_shape(nb, dil, seg, 128))]
    out = _call(body, name=name, grid=(nb,), in_specs=in_specs, out_specs=out_specs, out_shape=out_shape)(*args)
    return [(out[2 * i].reshape(t, ATT_WIDTH), out[2 * i + 1].reshape(t, 128)) for i in range(nbr)]


def _combine_mix_in(os_, ls_, fine, o_hg, proj, og, ag, name, tr=512):
    t = os_[0].shape[0]
    nbr = len(os_)

    def body(*refs):
        o_refs, l_refs = refs[:nbr], refs[nbr:2 * nbr]
        oh_ref, g_ref, og_ref, ag_ref, a_ref, lt_ref, lts_ref, m_ref, to_natural, to_sub = refs[2 * nbr:]

        @pl.when(pl.program_id(0) == 0)
        def _():
            to_natural[...] = _regroup_matrix(tr, fine)
            to_sub[...] = _regroup_matrix(tr, tr // fine)

        lane = lax.broadcasted_iota(jnp.int32, (1, 128), 1)
        packed = l_refs[0][...] + _regroup(to_natural[...], sum(r[...] for r in l_refs[1:]).reshape(tr, 128))
        ls = [packed if i == 0 else pltpu.roll(packed, 128 - ATT_HEADS * i, 1) for i in range(nbr)]
        mx = functools.reduce(jnp.maximum, ls)
        tot = mx + jnp.log(sum(jnp.exp(l - mx) for l in ls))
        ws = [jnp.exp(l - tot) for l in ls]
        tot = jnp.where(lane < ATT_HEADS, tot, 0.0)
        lt_ref[...] = tot
        lts_ref[...] = _regroup(to_sub[...], tot).reshape(lts_ref.shape)
        o_vals = [o_refs[0]] + [_regroup(to_natural[...], r[...].reshape(tr, ATT_WIDTH)) for r in o_refs[1:]]
        pairs = []
        for hp in range(ATT_HEADS // 2):
            sl = slice(hp * 128, (hp + 1) * 128)
            acc = jnp.zeros((tr, 128), F32)
            for w, o in zip(ws, o_vals):
                wf = jnp.where(lane < 64, w[:, 2 * hp:2 * hp + 1], w[:, 2 * hp + 1:2 * hp + 2])
                acc = acc + wf * o[:, sl]
            pairs.append(acc)
        av = jnp.concatenate(pairs, axis=1)
        a_ref[...] = av
        m_ref[:, HG_WIDTH:] = (av * _rms(av) * ag_ref[...]).astype(BF16)
        for h in range(HG_WIDTH // HG_HEAD):
            sl = slice(h * HG_HEAD, (h + 1) * HG_HEAD)
            oh = oh_ref[:, sl].astype(F32)
            gv = g_ref[:, sl].astype(F32)
            m_ref[:, sl] = (oh * _rms(oh) * og_ref[...] * (gv * _sigmoid(gv))).astype(BF16)

    half = _rows(tr, ATT_WIDTH)
    sub = lambda a: a.reshape(fine, t // fine, a.shape[1])
    att, lse, lse_sub, mixin = _call(
        body, name=name, grid=(t // tr,),
        in_specs=[half] + [_sub_rows(tr, fine, ATT_WIDTH)] * (nbr - 1) + [_rows(tr, 128)] + [_sub_rows(tr, fine, 128)] * (nbr - 1)
        + [half, pl.BlockSpec((tr, HG_WIDTH), lambda i: (i, 3)), _vec(HG_HEAD), _vec(ATT_WIDTH)],
        out_specs=[half, _rows(tr, 128), _sub_rows(tr, fine, 128), _rows(tr, D_MODEL)],
        out_shape=[_sds((t, ATT_WIDTH)), _sds((t, 128)), _sds((fine, t // fine, 128)), _sds((t, D_MODEL), BF16)],
        scratch_shapes=[pltpu.VMEM((tr, tr), BF16)] * 2)(
            os_[0], *map(sub, os_[1:]), ls_[0], *map(sub, ls_[1:]), o_hg, proj, og, ag)
    return att, lse, lse_sub.reshape(t, 128), mixin


def _attn_bwd_block(q_ref, kc_ref, kp_ref, vc_ref, vp_ref, do_ref, l_ref, d_ref, out_ref, carry, has_prev, seg):
    w = ATT_WIDTH
    nq = ATT_BLOCK
    mask, lane = _att_mask(has_prev, seg)
    lo = lane < 64
    lse, ddv = _get(l_ref, slice(None)), _get(d_ref, slice(None))
    for hp in range(ATT_HEADS // 2):
        sl = slice(hp * 128, (hp + 1) * 128)
        sk = slice(w + hp * 128, w + (hp + 1) * 128)
        sv = slice(2 * w + hp * 128, 2 * w + (hp + 1) * 128)
        q2, do2 = _get(q_ref, sl), _get(do_ref, sl)
        zero = jnp.zeros_like(q2)
        q2 = q2 * 0.125
        qs = jnp.concatenate([jnp.where(lo, q2, zero), jnp.where(lo, zero, q2)], axis=0)
        dos = jnp.concatenate([jnp.where(lo, do2, zero), jnp.where(lo, zero, do2)], axis=0)
        kk = jnp.concatenate([_get(kp_ref, sl), _get(kc_ref, sl)], axis=0)
        vv = jnp.concatenate([_get(vp_ref, sl), _get(vc_ref, sl)], axis=0)
        ls = jnp.concatenate([lse[:, 2 * hp:2 * hp + 1], lse[:, 2 * hp + 1:2 * hp + 2]], axis=0)
        dh = jnp.concatenate([ddv[:, 2 * hp:2 * hp + 1], ddv[:, 2 * hp + 1:2 * hp + 2]], axis=0)
        p = jnp.exp(jnp.where(mask, _dot_nt(qs, kk) - ls, NEG))
        ds = (p * (_dot_nt(dos, vv) - dh)).astype(BF16)
        dq = _dot(ds, kk) * 0.125
        dk = _dot_tn(ds, qs)
        dv = _dot_tn(p.astype(BF16), dos)
        _put(out_ref, sl, carry[:, sl].astype(BF16))
        _put(out_ref, sk, (carry[:, sk] + dk[:nq]).astype(BF16))
        _put(out_ref, sv, (carry[:, sv] + dv[:nq]).astype(BF16))
        carry[:, sl] = jnp.where(lo, dq[:nq], dq[nq:])
        carry[:, sk] = dk[nq:]
        carry[:, sv] = dv[nq:]


def _attn_bwd(branches, name):
    t = branches[0][0].shape[0]
    nb = t // ATT_BLOCK
    nbr = len(branches)
    w = ATT_WIDTH

    def body(*refs):
        ins, outs, carries = refs[:8 * nbr], refs[8 * nbr:9 * nbr], refs[9 * nbr:]
        n = pl.program_id(0)

        @pl.when(n == 0)
        def _():
            for carry in carries:
                carry[...] = jnp.zeros_like(carry)

        @pl.when(n < nb)
        def _():
            for i, branch in enumerate(branches):
                dil, seg = branch[4:]
                _attn_bwd_block(*ins[8 * i:8 * i + 8], outs[i], carries[i], (n % (nb // dil)) != 0, seg)

        @pl.when(n == nb)
        def _():
            for i in range(nbr):
                _put(outs[i], slice(None), carries[i][...].astype(BF16))

    in_specs, args, out_specs, out_shape = [], [], [], []
    for qkv, dout, lse, dd, dil, seg in branches:
        c0 = qkv.shape[1] // w - 3
        in_specs += [_att_spec(nb, dil, seg, w, c0 + j, back) for j, back in [(0, 0), (1, 0), (1, 1), (2, 0), (2, 1)]]
        in_specs += [_att_spec(nb, dil, seg, w, 0, 0), _att_spec(nb, dil, seg, 128, 0, 0), _att_spec(nb, dil, seg, 128, 0, 0)]
        args += [_att_view(a, nb, dil, seg) for a in [qkv] * 5 + [dout, lse, dd]]
        out_specs += [_att_spec(nb, dil, seg, 3 * w, 0, 1)]
        out_shape += [_sds(_att_shape(nb, dil, seg, 3 * w), BF16)]
    out = _call(body, name=name, grid=(nb + 1,), in_specs=in_specs, out_specs=out_specs, out_shape=out_shape,
                scratch_shapes=[pltpu.VMEM((ATT_BLOCK, 3 * w), F32)] * nbr)(*args)
    return [o.reshape(t, 3 * w) for o in out]


def _local_step(x, tgt, mod, norm1_g, lb_logits, og, ag, norm2_g, fg, get_w, put_g, late=lambda a: a, project=None):
    shift1, scale1, gate1, shift2, scale2, gate2 = [mod[:, i * D_MODEL:(i + 1) * D_MODEL] for i in range(6)]
    fg = fg.reshape(1, D_MODEL)

    h1 = _norm_mod(x, norm1_g, scale1, shift1, "norm_mod1")
    if project is None:
        w_in = get_w("w_in", h1)
        proj = _mm_nn(h1, w_in, "mm_in", out_dtype=BF16)
    else:
        proj, w_in = project(h1)
    o_hg, states = _hgrn_fwd(proj, lb_logits, "hgrn_fwd")
    fine = DILATIONS[-1]
    layouts = [(d, 1 if d == 1 else fine // d) for d in DILATIONS]
    qkv_fine = _to_sub(proj, fine)
    qkvs = [proj if d == 1 else qkv_fine for d in DILATIONS]
    natural = lambda a, d: a if d == 1 else _from_sub(a, fine)
    outs = _attn_fwd([(q, d, seg) for q, (d, seg) in zip(qkvs, layouts)], "attn_fwd")
    att, lse, lse_fine, mixin = _combine_mix_in([o for o, _ in outs], [l for _, l in outs], fine,
                                                o_hg, proj, og, ag, "attn_combine_mix_in")
    w_out = get_w("w_out", mixin)
    mix, x2, h2 = _out_resid_norm_mod(x, mixin, w_out, gate1, norm2_g, scale2, shift2, "mm_out_resid_norm_mod2")
    w_gu = get_w("w_gu", h2)
    a_ff, u_ff, act = _mm_gate_up(h2, w_gu, "mm_gu")
    w_down = get_w("w_down", act)
    dx3, dffn, loss_v, dfg, dgate2 = _down_loss(x2, act, w_down, gate2, fg, tgt, "mm_down_loss")

    dffn = put_g("w_down", *_mm_tn(act, dffn, 1, "mm_down_dw", tm=2048, tk=D_FF // 2), dffn)
    dau = _mm_down_dx(dffn, w_down, a_ff, u_ff, "mm_down_dx")
    dau = put_g("w_gu", *_mm_tn(h2, dau, N_SHARD, "mm_gu_dw", tm=x.shape[0], tk=512), dau)
    dx2, dshift2, dscale2, dg2, dgate1, dmix = _norm_mod_bwd(
        dau, x2, norm2_g, scale2, dx3, "mm_gu_dx_norm_bwd", gate=gate1, mix=mix, w=w_gu)
    dmix = put_g("w_out", *_mm_tn(mixin, dmix, 1, "mm_out_dw", tm=x.shape[0], tk=512), dmix)
    do_hg, dg_raw, datt, dd, datt_fine, dd_fine, dog, dag = _mix_in_bwd(
        dmix, w_out, o_hg, proj, att, og, ag, fine, "mm_out_dx_mix_in_bwd")
    datts = _attn_bwd([(q,) + ((datt, lse, dd) if d == 1 else (datt_fine, lse_fine, dd_fine)) + (d, seg)
                       for q, (d, seg) in zip(qkvs, layouts)], "attn_bwd")
    dhg, dlb = _hgrn_bwd(proj, lb_logits, states, do_hg, "hgrn_bwd")
    dhg = late(dhg)
    dproj = _dproj(dhg, dg_raw, [natural(a, d) for a, d in zip(datts, DILATIONS)], "dproj")
    dproj = put_g("w_in", *_mm_tn(h1, dproj, N_SHARD, "mm_in_dw", tm=x.shape[0], tk=512, group=2), dproj)
    dx, dshift1, dscale1, dg1 = _norm_mod_bwd(dproj, x, norm1_g, scale1, dx2, "mm_in_dx_norm_bwd", w=w_in)

    stats = jnp.concatenate([loss_v, dfg, dg2, dg1, dlb, dag, dog,
                             dshift1, dscale1, dgate1, dshift2, dscale2, dgate2], axis=1)
    return dx, stats


def _place():
    x, y, c = lax.axis_index("x"), lax.axis_index("y"), lax.axis_index("c")
    return x, y, c


def _chip_peers(x, y, c):
    return [(1 - x, y, c), (x, 1 - y, c), (1 - x, 1 - y, c)]


def _comm_call(body, name, n_in, out_shape, scratch_shapes):
    hbm = pl.BlockSpec(memory_space=pl.ANY)
    return pl.pallas_call(body, name=name, in_specs=[hbm] * n_in, out_specs=[hbm] * len(out_shape),
                          out_shape=out_shape, scratch_shapes=scratch_shapes)


_HBM = pl.BlockSpec(memory_space=pltpu.HBM)
_SEM = pl.BlockSpec(memory_space=pltpu.SEMAPHORE)
_EFFECT = pltpu.SideEffectType.DATAFLOW_SIDE_EFFECTING


def _exchange_copy(bufs, send, recv, j, peer, place, kind):
    x, y, c = place
    target = peer
    if kind == "gather":
        src = dst = bufs[0].at[2 * x + y]
    elif kind == "scatter":
        src, dst = bufs[0].at[2 * peer[0] + peer[1]], bufs[1].at[j]
    else:
        half = bufs[0].shape[1] // 2
        rows = pl.ds(c * half, half)
        if kind == "half":
            src = dst = bufs[0].at[2 * x + y, rows]
        else:
            src = dst = bufs[0].at[2 * peer[0] + peer[1], rows]
            target = (x, y, 1 - c)
    return pltpu.make_async_remote_copy(src_ref=src, dst_ref=dst, send_sem=send.at[j], recv_sem=recv.at[j],
                                        device_id=target, device_id_type=MESH)


def _exchange_start(groups, after, kind, name):
    sizes = [len(g) for g in groups]
    flat = [b for g in groups for b in g]
    ng, nb = len(groups), len(flat)

    def body(*refs):
        bufs, sems = refs[:nb], refs[nb + 1:nb + 1 + 2 * ng]
        x, y, c = _place()
        for j, peer in enumerate(_chip_peers(x, y, c)):
            at = 0
            for i, size in enumerate(sizes):
                _exchange_copy(bufs[at:at + size], sems[2 * i], sems[2 * i + 1], j, peer, (x, y, c), kind).start()
                at += size

    any_space = pl.BlockSpec(memory_space=pl.ANY)
    out = pl.pallas_call(
        body, name=name, in_specs=[_HBM] * nb + [any_space],
        out_specs=[_SEM] * (2 * ng) + [_HBM] * nb + [any_space],
        out_shape=[pltpu.SemaphoreType.DMA((3,))] * (2 * ng) + [pltpu.HBM(b.shape, b.dtype) for b in flat]
        + [_sds(after.shape, after.dtype)],
        input_output_aliases={i: 2 * ng + i for i in range(nb + 1)},
        compiler_params=pltpu.CompilerParams(has_side_effects=_EFFECT),
    )(*[pltpu.with_memory_space_constraint(b, pltpu.HBM) for b in flat], after)
    started, at = [], 2 * ng
    for i, size in enumerate(sizes):
        started.append((out[2 * i], out[2 * i + 1], tuple(out[at:at + size])))
        at += size
    return started, out[-1]


def _exchange_wait(started, after, kind, name):
    send, recv, bufs = started
    nb = len(bufs)

    def body(*refs):
        x, y, c = _place()
        for j, peer in enumerate(_chip_peers(x, y, c)):
            cp = _exchange_copy(refs[:nb], refs[nb], refs[nb + 1], j, peer, (x, y, c), kind)
            cp.wait_send()
            cp.wait_recv()

    return pl.pallas_call(
        body, name=name, in_specs=[_HBM] * nb + [_SEM, _SEM, pl.BlockSpec(memory_space=pl.ANY)],
        out_specs=[_HBM] * nb, out_shape=[pltpu.HBM(b.shape, b.dtype) for b in bufs],
        input_output_aliases={i: i for i in range(nb)},
        compiler_params=pltpu.CompilerParams(has_side_effects=_EFFECT),
    )(*bufs, send, recv, after)


def _sibling_copies(v_refs, l_refs, send, recv):
    x, y, c = _place()
    return [pltpu.make_async_remote_copy(src_ref=v, dst_ref=l, send_sem=send.at[a], recv_sem=recv.at[a],
                                         device_id=(x, y, 1 - c), device_id_type=MESH)
            for a, (v, l) in enumerate(zip(v_refs, l_refs))]


def _sibling_start(vs, after, name):
    vs = list(vs)
    n = len(vs)
    lands = [lax.empty(v.shape, v.dtype) for v in vs]

    def body(*refs):
        for cp in _sibling_copies(refs[:n], refs[n:2 * n], refs[2 * n + 1], refs[2 * n + 2]):
            cp.start()

    any_space = pl.BlockSpec(memory_space=pl.ANY)
    out = pl.pallas_call(
        body, name=name, in_specs=[_HBM] * (2 * n) + [any_space],
        out_specs=[_SEM, _SEM] + [_HBM] * (2 * n) + [any_space],
        out_shape=[pltpu.SemaphoreType.DMA((n,))] * 2 + [pltpu.HBM(b.shape, b.dtype) for b in vs + lands]
        + [_sds(after.shape, after.dtype)],
        input_output_aliases={i: 2 + i for i in range(2 * n + 1)},
        compiler_params=pltpu.CompilerParams(has_side_effects=_EFFECT),
    )(*[pltpu.with_memory_space_constraint(b, pltpu.HBM) for b in vs + lands], after)
    return (out[0], out[1], tuple(out[2:2 + n]), tuple(out[2 + n:2 + 2 * n])), out[-1]


def _sibling_wait(started, after, name):
    send, recv, vs, lands = started
    n = len(vs)

    def body(*refs):
        for cp in _sibling_copies(refs[:n], refs[n:2 * n], refs[2 * n], refs[2 * n + 1]):
            cp.wait_send()
            cp.wait_recv()

    out = pl.pallas_call(
        body, name=name, in_specs=[_HBM] * (2 * n) + [_SEM, _SEM, pl.BlockSpec(memory_space=pl.ANY)],
        out_specs=[_HBM] * (2 * n), out_shape=[pltpu.HBM(b.shape, b.dtype) for b in vs + lands],
        input_output_aliases={i: i for i in range(2 * n)},
        compiler_params=pltpu.CompilerParams(has_side_effects=_EFFECT),
    )(*vs, *lands, send, recv, after)
    return out[:n], out[n:]


def _swap_sibling(vs, name):
    n = len(vs)

    def body(*refs):
        v_refs, o_refs, (send, recv) = refs[:n], refs[n:2 * n], refs[2 * n:]
        x, y, c = _place()
        cps = [pltpu.make_async_remote_copy(
            src_ref=v_refs[a], dst_ref=o_refs[a], send_sem=send.at[a], recv_sem=recv.at[a],
            device_id=(x, y, 1 - c), device_id_type=MESH) for a in range(n)]
        for cp in cps:
            cp.start()
        for cp in cps:
            cp.wait()

    return _comm_call(body, name, n, [_sds(v.shape, v.dtype) for v in vs],
                      [pltpu.SemaphoreType.DMA((n,)), pltpu.SemaphoreType.DMA((n,))])(*vs)


def _everyone(x, y, c):
    return [(1 - x if k & 4 else x, 1 - y if k & 2 else y, 1 - c if k & 1 else c) for k in range(1, 8)]


def _all_gather_copies(land_ref, send, recv, arriving):
    x, y, c = _place()
    me = 4 * x + 2 * y + c
    return [pltpu.make_async_remote_copy(
        src_ref=land_ref.at[me], dst_ref=land_ref.at[4 * p[0] + 2 * p[1] + p[2] if arriving else me],
        send_sem=send.at[k], recv_sem=recv.at[k], device_id=p, device_id_type=MESH)
        for k, p in enumerate(_everyone(x, y, c))]


def _all_gather_start(v, name):
    x, y, c = _place()
    land = lax.dynamic_update_slice(lax.empty((8,) + v.shape, v.dtype), v[None], (4 * x + 2 * y + c, 0, 0))

    def body(land_ref, v_ref, send, recv, land_out, v_out):
        for cp in _all_gather_copies(land_ref, send, recv, False):
            cp.start()

    any_space = pl.BlockSpec(memory_space=pl.ANY)
    out = pl.pallas_call(
        body, name=name, in_specs=[_HBM, any_space], out_specs=[_SEM, _SEM, _HBM, any_space],
        out_shape=[pltpu.SemaphoreType.DMA((7,))] * 2 + [pltpu.HBM(land.shape, land.dtype), _sds(v.shape, v.dtype)],
        input_output_aliases={0: 2, 1: 3}, compiler_params=pltpu.CompilerParams(has_side_effects=_EFFECT),
    )(pltpu.with_memory_space_constraint(land, pltpu.HBM), v)
    return tuple(out[:3]), out[3]


def _all_gather_wait(started, after, name):
    send, recv, land = started

    def body(land_ref, send, recv, after_ref, land_out):
        for cp in _all_gather_copies(land_ref, send, recv, True):
            cp.wait_send()
            cp.wait_recv()

    return pl.pallas_call(
        body, name=name, in_specs=[_HBM, _SEM, _SEM, pl.BlockSpec(memory_space=pl.ANY)], out_specs=_HBM,
        out_shape=pltpu.HBM(land.shape, land.dtype), input_output_aliases={0: 0},
        compiler_params=pltpu.CompilerParams(has_side_effects=_EFFECT),
    )(land, send, recv, after)


def _cast_place(ws, shard, name):
    n = len(ws)

    def body(s_ref, *refs):
        for w_ref, o_ref in zip(refs[:n], refs[n:]):
            o_ref[0] = w_ref[...].astype(BF16)

    return pl.pallas_call(
        body, name=name, out_shape=[_sds((N_SHARD,) + w.shape, BF16) for w in ws],
        grid_spec=pltpu.PrefetchScalarGridSpec(
            num_scalar_prefetch=1, grid=(4,),
            in_specs=[pl.BlockSpec((w.shape[0] // 4, w.shape[1]), lambda i, s: (i, 0)) for w in ws],
            out_specs=[pl.BlockSpec((1, w.shape[0] // 4, w.shape[1]), lambda i, s: (s[0], i, 0)) for w in ws]),
        compiler_params=pltpu.CompilerParams(dimension_semantics=("arbitrary",), vmem_limit_bytes=VMEM_LIMIT),
    )(shard.reshape(1).astype(jnp.int32), *ws)


def _mod_rows(c8, w_ada, b_ada, name):
    n = w_ada.shape[1]

    def gather(src_ref, dst_ref, send, recv, loc, base):
        x, y, c = _place()
        me = 4 * x + 2 * y + c
        own = pltpu.make_async_copy(src_ref, dst_ref.at[me], loc)
        own.start()
        peers = _everyone(x, y, c)
        sends = [pltpu.make_async_remote_copy(src_ref=src_ref, dst_ref=dst_ref.at[me], send_sem=send.at[base + k],
                                              recv_sem=recv.at[base + k], device_id=p, device_id_type=MESH)
                 for k, p in enumerate(peers)]
        for cp in sends:
            cp.start()
        for k, p in enumerate(peers):
            pltpu.make_async_remote_copy(src_ref=src_ref, dst_ref=dst_ref.at[4 * p[0] + 2 * p[1] + p[2]],
                                         send_sem=send.at[base + k], recv_sem=recv.at[base + k], device_id=p,
                                         device_id_type=MESH).wait_recv()
        for cp in sends:
            cp.wait_send()
        own.wait()

    def body(c_ref, w_ref, b_ref, a_ref, parts_ref, c_all, part, send, recv, loc):
        gather(c_ref, c_all, send, recv, loc.at[0], 0)
        cv = jnp.max(c_all[...], axis=1)
        ca = cv * _sigmoid(cv)
        a_ref[...] = ca
        part[...] = jnp.dot(ca, w_ref[...], precision=lax.Precision.HIGHEST, preferred_element_type=F32) + b_ref[...]
        gather(part, parts_ref, send, recv, loc.at[1], 7)

    vmem = pl.BlockSpec(memory_space=pltpu.VMEM)
    return pl.pallas_call(
        body, name=name, in_specs=[vmem] * 3, out_specs=[vmem, vmem],
        out_shape=[_sds((8, D_MODEL)), _sds((8, 8, n))],
        scratch_shapes=[pltpu.VMEM((8, 8, D_MODEL), F32), pltpu.VMEM((8, n), F32), pltpu.SemaphoreType.DMA((14,)),
                        pltpu.SemaphoreType.DMA((14,)), pltpu.SemaphoreType.DMA((2,))],
        compiler_params=pltpu.CompilerParams(vmem_limit_bytes=VMEM_LIMIT))(c8, w_ada, b_ada)


def _sum_received(gs, shard, lands, name):
    n = len(gs)

    def body(s_ref, *refs):
        for g_ref, l_ref, o_ref in zip(refs[:n], refs[n:2 * n], refs[2 * n:]):
            o_ref[...] = ((g_ref[0] + l_ref[0].astype(F32)) + l_ref[1].astype(F32)) + l_ref[2].astype(F32)

    quarter = lambda g: (g.shape[1] // 4, g.shape[2])
    return pl.pallas_call(
        body, name=name, out_shape=[_sds(g.shape[1:]) for g in gs],
        grid_spec=pltpu.PrefetchScalarGridSpec(
            num_scalar_prefetch=1, grid=(4,),
            in_specs=[pl.BlockSpec((1,) + quarter(g), lambda i, s: (s[0], i, 0)) for g in gs]
            + [pl.BlockSpec((3,) + quarter(g), lambda i, s: (0, i, 0)) for g in gs],
            out_specs=[pl.BlockSpec(quarter(g), lambda i, s: (i, 0)) for g in gs]),
        compiler_params=pltpu.CompilerParams(dimension_semantics=("arbitrary",), vmem_limit_bytes=VMEM_LIMIT),
    )(shard.reshape(1).astype(jnp.int32), *gs, *lands)


def _adamw_outer(w, ct, dm, m, v, name):
    k, n = w.shape
    tr = k // 4

    def body(w_ref, c_ref, d_ref, m_ref, v_ref, g_out, d_out, m_out, v_out):
        cv = c_ref[...]
        dv = d_ref[...]
        g = cv[:, 0:1] * dv[0:1, :]
        for i in range(1, 8):
            g = g + cv[:, i:i + 1] * dv[i:i + 1, :]
        g_out[...] = g
        d_out[...], m_out[...], v_out[...] = _adamw_math(w_ref[...], g, m_ref[...], v_ref[...])

    row = _rows(tr, n)
    return _call(body, name=name, grid=(4,),
                 in_specs=[row, _rows(tr, 8), pl.BlockSpec((8, n), lambda i: (0, 0)), row, row],
                 out_specs=[row] * 4, out_shape=[_sds((k, n))] * 4)(w, ct, dm, m, v)


def _adamw_math(w, g, m, v):
    m_new = ADAM_B1 * m + (1.0 - ADAM_B1) * g
    v_new = ADAM_B2 * v + (1.0 - ADAM_B2) * (g * g)
    m_hat = m_new / (1.0 - ADAM_B1 ** ADAM_STEP)
    v_hat = v_new / (1.0 - ADAM_B2 ** ADAM_STEP)
    return -ADAM_LR * (m_hat / (jnp.sqrt(v_hat) + ADAM_EPS) + ADAM_WD * w), m_new, v_new


def _small_update(stats, smalls, name):
    offsets = [ST_DMOD, ST_DG1, ST_DLB, ST_DOG, ST_DAG, ST_DG2, ST_DFG]
    lb_index = 2

    def body(*refs):
        s_ref, ins, l_ref, outs = refs[0], refs[1:22], refs[22], refs[23:]
        tot = s_ref[0:1, :]
        for i in range(1, 8):
            tot = tot + s_ref[i:i + 1, :]
        l_ref[...] = jnp.zeros((1, 128), F32) + (0.5 / D_MODEL) * jnp.sum(tot[:, ST_LOSS:ST_LOSS + D_MODEL])
        for p, off in enumerate(offsets):
            w_ref, m_ref, v_ref = ins[3 * p:3 * p + 3]
            g_out, d_out, m_out, v_out = outs[4 * p:4 * p + 4]
            g = tot[:, off:off + w_ref.shape[1]]
            if p == lb_index:
                lg = w_ref[...]
                lb = _sigmoid(lg[0:1] - lg[1:2])
                g = g * lb * (1.0 - lb)
            for r in range(w_ref.shape[0]):
                rows = slice(r, r + 1)
                gr = g if r == 0 else -g
                delta, m_new, v_new = _adamw_math(w_ref[rows, :], gr, m_ref[rows, :], v_ref[rows, :])
                g_out[rows, :] = gr
                d_out[rows, :] = delta
                m_out[rows, :] = m_new
                v_out[rows, :] = v_new

    full = lambda a: pl.BlockSpec(a.shape, lambda i: (0, 0))
    flat = [a for t in smalls for a in t]
    return _call(body, name=name, grid=(1,),
                 in_specs=[full(stats)] + [full(a) for a in flat],
                 out_specs=[pl.BlockSpec((1, 128), lambda i: (0, 0))] + [full(t[0]) for t in smalls for _ in range(4)],
                 out_shape=[_sds((1, 128))] + [_sds(t[0].shape) for t in smalls for _ in range(4)])(stats, *flat)


def _adamw(params, name):
    n = len(params)

    def body(*refs):
        for p in range(n):
            w_ref, ga_ref, gb_ref, m_ref, v_ref = refs[5 * p:5 * p + 5]
            g_out, d_out, m_out, v_out = refs[5 * n + 4 * p:5 * n + 4 * p + 4]
            g = ga_ref[...] + gb_ref[...]
            g_out[...] = g
            d_out[...], m_out[...], v_out[...] = _adamw_math(w_ref[...], g, m_ref[...], v_ref[...])

    row = lambda w: _rows(w.shape[0] // 4, w.shape[1])
    out = _call(body, name=name, grid=(4,), in_specs=[row(p[0]) for p in params for _ in range(5)],
                out_specs=[row(p[0]) for p in params for _ in range(4)],
                out_shape=[_sds(p[0].shape) for p in params for _ in range(4)])(*[a for p in params for a in p])
    return [tuple(out[4 * p:4 * p + 4]) for p in range(n)]


def kernel(x, c, w_ada, b_ada, norm1_g, w_in, hg_lb_logits, hg_onorm_g, att_onorm_g, w_out, norm2_g, w_gate_up, w_down, final_g, loss_target, m_w_ada, m_b_ada, m_norm1_g, m_w_in, m_hg_lb_logits, m_hg_onorm_g, m_att_onorm_g, m_w_out, m_norm2_g, m_w_gate_up, m_w_down, m_final_g, v_w_ada, v_b_ada, v_norm1_g, v_w_in, v_hg_lb_logits, v_hg_onorm_g, v_att_onorm_g, v_w_out, v_norm2_g, v_w_gate_up, v_w_down, v_final_g):
    ix, iy, ic = _place()
    shard = 2 * ix + iy
    sample = 4 * ix + 2 * iy + ic
    n_ada = w_ada.shape[2]

    shards = [w_in[0], w_out[0], w_gate_up[0], w_down[0]]
    names = ["w_in", "w_out", "w_gu", "w_down"]
    shapes = [(N_SHARD,) + w.shape for w in shards]
    placed = [(_cast_place(shards[:1], shard, "place_w_in")[0],)]
    placed += [(p,) for p in _cast_place(shards[1:], shard, "place_rest")]

    b_part = lax.dynamic_slice(b_ada, (0, shard * n_ada), (1, n_ada))
    c_act, parts = _mod_rows(jnp.broadcast_to(c, (8, D_MODEL)), w_ada[0], b_part, "mod_rows")
    parts = parts[::2]
    mod = lax.dynamic_index_in_dim(parts, sample, axis=1, keepdims=False).reshape(1, 6 * D_MODEL)
    (first,), mod = _exchange_start(placed[:1], mod, "half", "gather_start_w_in")
    gathering = {}

    def get_w(name, after):
        if name == "w_in":
            halves = _exchange_wait(first, after, "half", "gather_wait_w_in")
            (passing,), token = _exchange_start([tuple(halves)], mod, "forward", "forward_start_w_in")
            (full,) = _exchange_wait(passing, token, "forward", "forward_wait_w_in")
            rest, full = _exchange_start(placed[1:], full, "gather", "gather_start_rest")
            gathering.update(zip(names[1:], rest))
            return full
        (full,) = _exchange_wait(gathering[name], after, "gather", "gather_wait_" + name)
        return full if name == "w_gu" else full.reshape(1, -1, D_MODEL)

    scattering = {}

    def put_g(name, g, g_bf16, then):
        shape = shapes[names.index(name)]
        land = lax.empty((3,) + shape[1:], BF16)
        (started,), then = _exchange_start([(g_bf16.reshape(shape), land)], then, "scatter", "scatter_start_" + name)
        scattering[name] = (g.reshape(shape), started)
        return then

    def summed(group, after, tag):
        lands = [_exchange_wait(scattering[nm][1], after, "scatter", "scatter_wait_" + nm)[1] for nm in group]
        return _sum_received([scattering[nm][0] for nm in group], shard, lands, "sum_" + tag)

    early = ["w_down", "w_gu", "w_out"]
    swapping = []

    def late(a):
        started, a = _sibling_start(summed(early, a, "early"), a, "swap_start")
        swapping.append(started)
        return a

    def project(h1):
        own = _mm_own_shard(h1, shards[0], shard, N_SHARD, "mm_in_own")
        w_full = get_w("w_in", own)
        return _mm_other_shards(h1, w_full, own, shard, "mm_in_rest"), w_full

    dx, stats = _local_step(x[0], loss_target[0], mod, norm1_g, hg_lb_logits, hg_onorm_g, att_onorm_g,
                            norm2_g, final_g, get_w, put_g, late, project)

    gathering_stats, stats = _all_gather_start(stats, "stats_start")
    moments = [(m_w_in, v_w_in), (m_w_out, v_w_out), (m_w_gate_up, v_w_gate_up), (m_w_down, v_w_down)]

    def update(group, sums, other, tag):
        params = [(shards[names.index(nm)], s, o, moments[names.index(nm)][0][0], moments[names.index(nm)][1][0])
                  for nm, s, o in zip(group, sums, other)]
        return dict(zip(group, _adamw(params, "adamw_" + tag)))

    sums, other = _sibling_wait(swapping[0], stats, "swap_wait")
    done = update(early, sums, other, "early")
    sum_in = summed(["w_in"], done["w_out"][1], "w_in")
    done.update(update(["w_in"], sum_in, _swap_sibling(sum_in, "swap_sum_in"), "w_in"))

    stats_all = _all_gather_wait(gathering_stats, done["w_in"][1], "stats_wait").reshape(8, ST_WIDTH)
    dmod = lax.dynamic_slice(stats_all, (0, ST_DMOD + shard * n_ada), (8, n_ada))

    as_row = lambda a: a.reshape(1, -1) if a.ndim == 1 else a
    smalls = [tuple(as_row(a) for a in t) for t in [
        (b_ada, m_b_ada, v_b_ada), (norm1_g, m_norm1_g, v_norm1_g),
        (hg_lb_logits, m_hg_lb_logits, v_hg_lb_logits), (hg_onorm_g, m_hg_onorm_g, v_hg_onorm_g),
        (att_onorm_g, m_att_onorm_g, v_att_onorm_g), (norm2_g, m_norm2_g, v_norm2_g),
        (final_g, m_final_g, v_final_g)]]
    loss, *small_out = _small_update(stats_all, smalls, "small_update")
    shapes_out = [b_ada.shape, norm1_g.shape, hg_lb_logits.shape, hg_onorm_g.shape, att_onorm_g.shape,
                  norm2_g.shape, final_g.shape]
    sg, sd, sm, sv = [[small_out[4 * p + i].reshape(shapes_out[p]) for p in range(7)] for i in range(4)]

    ada = _adamw_outer(w_ada[0], c_act.T, dmod, m_w_ada[0], v_w_ada[0], "adamw_w_ada")
    big = [ada] + [done[nm] for nm in names]
    bg, bd, bm, bv = [[t[i][None] for t in big] for i in range(4)]

    def order(b, s):
        return [b[0], s[0], s[1], b[1], s[2], s[3], s[4], b[2], s[5], b[3], b[4], s[6]]

    return (loss[0, 0], dx[None], *order(bg, sg), *order(bd, sd), *order(bm, sm), *order(bv, sv))
```

```python
import functools

import jax
import jax.numpy as jnp
from jax import lax
from jax.experimental import pallas as pl
from jax.experimental.pallas import tpu as pltpu

F32 = jnp.float32
BF16 = jnp.bfloat16
MESH = pl.DeviceIdType.MESH

D_MODEL = 1024
HG_WIDTH = 512
HG_HEAD = 128
HG_CHUNK = 64
HG_GROUP = 4
ATT_WIDTH = 512
ATT_HEADS = 8
ATT_BLOCK = 128
DILATIONS = (1, 4, 16)
D_FF = 2816
IN_WIDTH = 3584
N_SHARD = 4
RMS_EPS = 1e-6
NEG = -1e30

ADAM_LR = 0.001
ADAM_B1 = 0.9
ADAM_B2 = 0.999
ADAM_EPS = 1e-08
ADAM_WD = 0.01
ADAM_STEP = 10

VMEM_LIMIT = 56 * 2**20

ST_LOSS, ST_DFG, ST_DG2, ST_DG1 = 0, 1024, 2048, 3072
ST_DLB, ST_DAG, ST_DOG, ST_DMOD = 4096, 4608, 5120, 5248
ST_WIDTH = 5248 + 6144


def _call(body, *, name, grid, in_specs, out_specs, out_shape, scratch_shapes=(), aliases=None):
    return pl.pallas_call(
        body, name=name, grid=grid, in_specs=in_specs, out_specs=out_specs, out_shape=out_shape,
        scratch_shapes=list(scratch_shapes), input_output_aliases=aliases or {},
        compiler_params=pltpu.CompilerParams(
            dimension_semantics=("arbitrary",) * len(grid), vmem_limit_bytes=VMEM_LIMIT))


def _sds(shape, dtype=F32):
    return jax.ShapeDtypeStruct(shape, dtype)


def _dot(a, b):
    return jnp.dot(a, b, preferred_element_type=F32)


def _dot_nt(a, b):
    return lax.dot_general(a, b, (((1,), (1,)), ((), ())), preferred_element_type=F32)


def _dot_tn(a, b):
    return lax.dot_general(a, b, (((0,), (0,)), ((), ())), preferred_element_type=F32)


def _sigmoid(x):
    return 1.0 / (1.0 + jnp.exp(-x))


def _rows(tr, width):
    return pl.BlockSpec((tr, width), lambda i: (i, 0))


def _vec(width):
    return pl.BlockSpec((1, width), lambda i: (0, 0))


def _acc(ref, val, first):
    @pl.when(first)
    def _():
        ref[...] = val

    @pl.when(jnp.logical_not(first))
    def _():
        ref[...] += val


def _sub_rows(tr, fine, width):
    return pl.BlockSpec((fine, tr // fine, width), lambda i: (0, i, 0))


def _regroup_matrix(tr, groups):
    a = lax.broadcasted_iota(jnp.int32, (tr, tr), 0)
    b = lax.broadcasted_iota(jnp.int32, (tr, tr), 1)
    return (b == (a % groups) * (tr // groups) + a // groups).astype(BF16)


def _regroup(m, v):
    if v.dtype == BF16:
        return _dot(m, v)
    out = None
    for _ in range(3):
        part = v.astype(BF16)
        v = v - part.astype(F32)
        out = _dot(m, part) if out is None else out + _dot(m, part)
    return out


def _mm_nn(a, b3, name, tm=1024, out_dtype=F32):
    m, k = a.shape
    s, _, n = b3.shape

    def body(a_ref, b_ref, o_ref):
        o_ref[...] = _dot(a_ref[...], b_ref[0]).astype(out_dtype)

    return _call(
        body, name=name, grid=(s, m // tm),
        in_specs=[pl.BlockSpec((tm, k), lambda j, i: (i, 0)), pl.BlockSpec((1, k, n), lambda j, i: (j, 0, 0))],
        out_specs=pl.BlockSpec((tm, n), lambda j, i: (i, j)), out_shape=_sds((m, s * n), out_dtype))(a, b3)


def _mm_own_shard(a, w, shard, s, name, tm=1024):
    m, k = a.shape
    n = w.shape[1]

    def body(s_ref, a_ref, w_ref, o_ref):
        o_ref[...] = _dot(a_ref[...], w_ref[...].astype(BF16)).astype(BF16)

    return pl.pallas_call(
        body, name=name, out_shape=_sds((m, s * n), BF16),
        grid_spec=pltpu.PrefetchScalarGridSpec(
            num_scalar_prefetch=1, grid=(m // tm,),
            in_specs=[pl.BlockSpec((tm, k), lambda i, sh: (i, 0)), pl.BlockSpec((k, n), lambda i, sh: (0, 0))],
            out_specs=pl.BlockSpec((tm, n), lambda i, sh: (i, sh[0]))),
        compiler_params=pltpu.CompilerParams(dimension_semantics=("arbitrary",), vmem_limit_bytes=VMEM_LIMIT),
    )(shard.reshape(1).astype(jnp.int32), a, w)


def _mm_other_shards(a, b3, partial, shard, name, tm=1024):
    m, k = a.shape
    s, _, n = b3.shape
    which = lambda j, sh: (sh[0] + 1 + j) % s

    def body(s_ref, a_ref, b_ref, p_ref, o_ref):
        o_ref[...] = _dot(a_ref[...], b_ref[0]).astype(BF16)

    return pl.pallas_call(
        body, name=name, out_shape=_sds(partial.shape, BF16),
        grid_spec=pltpu.PrefetchScalarGridSpec(
            num_scalar_prefetch=1, grid=(s - 1, m // tm),
            in_specs=[pl.BlockSpec((tm, k), lambda j, i, sh: (i, 0)),
                      pl.BlockSpec((1, k, n), lambda j, i, sh: (which(j, sh), 0, 0)),
                      pl.BlockSpec(memory_space=pl.ANY)],
            out_specs=pl.BlockSpec((tm, n), lambda j, i, sh: (i, which(j, sh)))),
        input_output_aliases={3: 0},
        compiler_params=pltpu.CompilerParams(dimension_semantics=("arbitrary",) * 2, vmem_limit_bytes=VMEM_LIMIT),
    )(shard.reshape(1).astype(jnp.int32), a, b3, partial)


def _mm_tn(a, dy, s, name, tm, tk, group=1):
    m, k = a.shape
    n = dy.shape[1] // s
    steps = m // tm

    def body(a_ref, dy_ref, o_ref, ob_ref):
        p = _dot_tn(a_ref[...], dy_ref[...])
        for g in range(group):
            pg = p[:, g * n:(g + 1) * n]
            if steps == 1:
                o_ref[g] = pg
                ob_ref[g] = pg.astype(BF16)
            else:
                _acc(o_ref.at[g], pg, pl.program_id(2) == 0)
        if steps > 1:
            @pl.when(pl.program_id(2) == steps - 1)
            def _():
                ob_ref[...] = o_ref[...].astype(BF16)

    out = pl.BlockSpec((group, tk, n), lambda kk, j, i: (j, kk, 0))
    return _call(
        body, name=name, grid=(k // tk, s // group, steps),
        in_specs=[pl.BlockSpec((tm, tk), lambda kk, j, i: (i, kk)),
                  pl.BlockSpec((tm, group * n), lambda kk, j, i: (i, j))],
        out_specs=[out, out], out_shape=[_sds((s, k, n)), _sds((s, k, n), BF16)])(a, dy)


def _rms(x):
    return lax.rsqrt(jnp.mean(x * x, axis=-1, keepdims=True) + RMS_EPS)


def _rms_bwd(dxh, xh, r):
    return r * (dxh - xh * jnp.mean(dxh * xh, axis=-1, keepdims=True))


def _norm_mod(x, g, scale, shift, name, tr=512):
    t = x.shape[0]

    def body(x_ref, g_ref, sc_ref, sh_ref, h_ref):
        xv = x_ref[...]
        n = xv * _rms(xv) * g_ref[...]
        h_ref[...] = (n * (1.0 + sc_ref[...]) + sh_ref[...]).astype(BF16)

    return _call(body, name=name, grid=(t // tr,),
                 in_specs=[_rows(tr, D_MODEL), _vec(D_MODEL), _vec(D_MODEL), _vec(D_MODEL)],
                 out_specs=_rows(tr, D_MODEL), out_shape=_sds((t, D_MODEL), BF16))(x, g, scale, shift)


def _out_resid_norm_mod(x, mixin, w_out, gate, g, scale, shift, name, tr=512):
    t = x.shape[0]

    def body(x_ref, mi_ref, w_ref, gt_ref, g_ref, sc_ref, sh_ref, m_ref, x2_ref, h_ref):
        mix = _dot(mi_ref[...], w_ref[0])
        m_ref[...] = mix
        x2 = x_ref[...] + gt_ref[...] * mix
        x2_ref[...] = x2
        n = x2 * _rms(x2) * g_ref[...]
        h_ref[...] = (n * (1.0 + sc_ref[...]) + sh_ref[...]).astype(BF16)

    row = _rows(tr, D_MODEL)
    return _call(body, name=name, grid=(t // tr,),
                 in_specs=[row, row, pl.BlockSpec(w_out.shape, lambda i: (0, 0, 0))] + [_vec(D_MODEL)] * 4,
                 out_specs=[row, row, row],
                 out_shape=[_sds((t, D_MODEL)), _sds((t, D_MODEL)), _sds((t, D_MODEL), BF16)])(
                     x, mixin, w_out, gate, g, scale, shift)


def _mm_gate_up(h, w_gu, name, tm=1024):
    m, k = h.shape
    n = w_gu.shape[2]

    def body(h_ref, wa_ref, wu_ref, da_ref, du_ref, o_ref):
        hv = h_ref[...]
        a = _dot(hv, wa_ref[0])
        u = _dot(hv, wu_ref[0])
        sg = _sigmoid(a)
        silu = a * sg
        da_ref[...] = (u * sg * (1.0 + a * (1.0 - sg))).astype(BF16)
        du_ref[...] = silu.astype(BF16)
        o_ref[...] = (silu * u).astype(BF16)

    out = pl.BlockSpec((tm, n), lambda j, i: (i, j))
    return _call(body, name=name, grid=(2, m // tm),
                 in_specs=[pl.BlockSpec((tm, k), lambda j, i: (i, 0)), pl.BlockSpec((1, k, n), lambda j, i: (j, 0, 0)),
                           pl.BlockSpec((1, k, n), lambda j, i: (j + 2, 0, 0))],
                 out_specs=[out, out, out], out_shape=[_sds((m, 2 * n), BF16)] * 3)(h, w_gu, w_gu)


def _mm_down_dx(dffn, w_down, act_da, act_du, name, tm=512):
    m = dffn.shape[0]
    _, k, n = w_down.shape

    def body(d_ref, w_ref, da_ref, du_ref, o_ref):
        dact = _dot_nt(d_ref[...], w_ref[0])
        o_ref[:, :k] = (dact * da_ref[...].astype(F32)).astype(BF16)
        o_ref[:, k:] = (dact * du_ref[...].astype(F32)).astype(BF16)

    return _call(body, name=name, grid=(m // tm,),
                 in_specs=[_rows(tm, n), pl.BlockSpec((1, k, n), lambda i: (0, 0, 0)), _rows(tm, k), _rows(tm, k)],
                 out_specs=_rows(tm, 2 * k), out_shape=_sds((m, 2 * k), BF16))(dffn, w_down, act_da, act_du)


def _down_loss(x2, act, w_down, gate, fg, tgt, name, tr=512):
    t = x2.shape[0]
    _, k, n = w_down.shape

    def body(x_ref, a_ref, w_ref, gt_ref, fg_ref, t_ref, dx_ref, df_ref, l_ref, dfg_ref, dgt_ref):
        first = pl.program_id(0) == 0
        ffn_v = _dot(a_ref[...], w_ref[0])
        x3 = x_ref[...] + gt_ref[...] * ffn_v
        r = _rms(x3)
        xh = x3 * r
        err = xh * fg_ref[...] - t_ref[...]
        dy = err * (1.0 / D_MODEL)
        dx3 = _rms_bwd(dy * fg_ref[...], xh, r)
        dx_ref[...] = dx3
        df_ref[...] = (dx3 * gt_ref[...]).astype(BF16)
        _acc(l_ref, jnp.sum(err * err, axis=0, keepdims=True), first)
        _acc(dfg_ref, jnp.sum(dy * xh, axis=0, keepdims=True), first)
        _acc(dgt_ref, jnp.sum(dx3 * ffn_v, axis=0, keepdims=True), first)

    row, vec = _rows(tr, D_MODEL), _vec(D_MODEL)
    return _call(body, name=name, grid=(t // tr,),
                 in_specs=[row, _rows(tr, k), pl.BlockSpec((1, k, n), lambda i: (0, 0, 0)), vec, vec, row],
                 out_specs=[row, row, vec, vec, vec],
                 out_shape=[_sds((t, D_MODEL)), _sds((t, D_MODEL), BF16)] + [_sds((1, D_MODEL))] * 3)(
                     x2, act, w_down, gate, fg, tgt)


def _norm_mod_bwd(dh, x, g, scale, dres, name, gate=None, mix=None, w=None, tr=512):
    t = x.shape[0]
    below = gate is not None

    def body(*refs):
        if w is not None:
            w_ref, w_full, sem, refs = refs[1], refs[-2], refs[-1], refs[:1] + refs[2:-2]

            @pl.when(pl.program_id(0) == 0)
            def _():
                n = w.shape[2]
                copies = [pltpu.make_async_copy(w_ref.at[j], w_full.at[:, pl.ds(j * n, n)], sem.at[j])
                          for j in range(w.shape[0])]
                for cp in copies:
                    cp.start()
                for cp in copies:
                    cp.wait()

        if below:
            dh_ref, x_ref, g_ref, sc_ref, dr_ref, gt_ref, m_ref, dx_ref, dsh_ref, dsc_ref, dg_ref, dgt_ref, dm_ref = refs
        else:
            dh_ref, x_ref, g_ref, sc_ref, dr_ref, dx_ref, dsh_ref, dsc_ref, dg_ref = refs
        first = pl.program_id(0) == 0
        xv = x_ref[...]
        if w is None:
            dhv = dh_ref[...].astype(F32)
        else:
            dhv = _dot_nt(dh_ref[...], w_full[...])
        r = _rms(xv)
        xh = xv * r
        dn = dhv * (1.0 + sc_ref[...])
        dx = dr_ref[...] + _rms_bwd(dn * g_ref[...], xh, r)
        dx_ref[...] = dx
        _acc(dsh_ref, jnp.sum(dhv, axis=0, keepdims=True), first)
        _acc(dsc_ref, jnp.sum(dhv * xh * g_ref[...], axis=0, keepdims=True), first)
        _acc(dg_ref, jnp.sum(dn * xh, axis=0, keepdims=True), first)
        if below:
            _acc(dgt_ref, jnp.sum(dx * m_ref[...], axis=0, keepdims=True), first)
            dm_ref[...] = (dx * gt_ref[...]).astype(BF16)

    row, vec = _rows(tr, D_MODEL), _vec(D_MODEL)
    first_specs = [row] if w is None else [_rows(tr, dh.shape[1]), pl.BlockSpec(memory_space=pl.ANY)]
    scratch = [] if w is None else [pltpu.VMEM((w.shape[1], dh.shape[1]), BF16), pltpu.SemaphoreType.DMA((w.shape[0],))]
    in_specs = first_specs + [row, vec, vec, row] + ([vec, row] if below else [])
    out_specs = [row, vec, vec, vec] + ([vec, row] if below else [])
    out_shape = [_sds((t, D_MODEL))] + [_sds((1, D_MODEL))] * 3 + ([_sds((1, D_MODEL)), _sds((t, D_MODEL), BF16)] if below else [])
    args = ((dh,) if w is None else (dh, w)) + (x, g, scale, dres) + ((gate, mix) if below else ())
    return _call(body, name=name, grid=(t // tr,), in_specs=in_specs, out_specs=out_specs, out_shape=out_shape,
                 scratch_shapes=scratch)(*args)


def _mix_in_bwd(dmix, w_out, o_hg, proj, att, og, ag, fine, name, tr=512):
    t = o_hg.shape[0]

    def body(dy_ref, w_ref, o_ref, g_ref, a_ref, og_ref, ag_ref,
             do_ref, dg_ref, da_ref, dd_ref, das_ref, dds_ref, dog_ref, dag_ref, to_sub):
        first = pl.program_id(0) == 0

        @pl.when(first)
        def _():
            to_sub[...] = _regroup_matrix(tr, tr // fine)

        dmi = _dot_nt(dy_ref[...], w_ref[0])
        dog = jnp.zeros((1, HG_HEAD), F32)
        for h in range(HG_WIDTH // HG_HEAD):
            sl = slice(h * HG_HEAD, (h + 1) * HG_HEAD)
            oh = o_ref[:, sl].astype(F32)
            gv = g_ref[:, sl].astype(F32)
            dv = dmi[:, sl]
            r = _rms(oh)
            xh = oh * r
            sg = _sigmoid(gv)
            dno = dv * gv * sg
            dg_ref[:, sl] = (dv * xh * og_ref[...] * sg * (1.0 + gv * (1.0 - sg))).astype(BF16)
            dog = dog + jnp.sum(dno * xh, axis=0, keepdims=True)
            do_ref[:, sl] = _rms_bwd(dno * og_ref[...], xh, r).astype(BF16)
        _acc(dog_ref, dog, first)
        av = a_ref[...]
        dav = dmi[:, HG_WIDTH:]
        r = _rms(av)
        xa = av * r
        _acc(dag_ref, jnp.sum(dav * xa, axis=0, keepdims=True), first)
        datt = _rms_bwd(dav * ag_ref[...], xa, r)
        da_ref[...] = datt.astype(BF16)
        das_ref[...] = _regroup(to_sub[...], datt.astype(BF16)).astype(BF16).reshape(das_ref.shape)
        prod = datt * av
        lane = lax.broadcasted_iota(jnp.int32, (1, 128), 1)
        dd = jnp.zeros((tr, 128), F32)
        for hp in range(ATT_HEADS // 2):
            pp = prod[:, hp * 128:(hp + 1) * 128]
            lo = jnp.sum(jnp.where(lane < 64, pp, 0.0), axis=-1, keepdims=True)
            hi = jnp.sum(jnp.where(lane >= 64, pp, 0.0), axis=-1, keepdims=True)
            dd = jnp.where(lane == 2 * hp, lo, dd)
            dd = jnp.where(lane == 2 * hp + 1, hi, dd)
        dd_ref[...] = dd
        dds_ref[...] = _regroup(to_sub[...], dd).reshape(dds_ref.shape)

    half = _rows(tr, HG_WIDTH)
    out = _call(body, name=name, grid=(t // tr,),
                in_specs=[_rows(tr, D_MODEL), pl.BlockSpec(w_out.shape, lambda i: (0, 0, 0)), half,
                          pl.BlockSpec((tr, HG_WIDTH), lambda i: (i, 3)), half, _vec(HG_HEAD), _vec(ATT_WIDTH)],
                out_specs=[half, pl.BlockSpec((tr, HG_WIDTH), lambda i: (i, 3)), half, _rows(tr, 128),
                           _sub_rows(tr, fine, ATT_WIDTH), _sub_rows(tr, fine, 128), _vec(HG_HEAD), _vec(ATT_WIDTH)],
                out_shape=[_sds((t, HG_WIDTH), BF16), _sds((t, IN_WIDTH), BF16), _sds((t, HG_WIDTH), BF16),
                           _sds((t, 128)), _sds((fine, t // fine, ATT_WIDTH), BF16),
                           _sds((fine, t // fine, 128)), _sds((1, HG_HEAD)), _sds((1, ATT_WIDTH))],
                scratch_shapes=[pltpu.VMEM((tr, tr), BF16)])(dmix, w_out, o_hg, proj, att, og, ag)
    out = list(out)
    return out[:4] + [out[4].reshape(t, ATT_WIDTH), out[5].reshape(t, 128)] + out[6:]


def _dproj(dproj, dqkvs, name, tr=1024):
    t = dproj.shape[0]
    nbr = len(dqkvs)
    first = IN_WIDTH // ATT_WIDTH - 3

    def body(*refs):
        refs[-1][...] = sum(r[...].astype(F32) for r in refs[:nbr]).astype(BF16)

    return _call(body, name=name, grid=(t // tr, 3),
                 in_specs=[pl.BlockSpec((tr, ATT_WIDTH), lambda i, j: (i, j))] * nbr + [pl.BlockSpec(memory_space=pl.ANY)],
                 out_specs=pl.BlockSpec((tr, ATT_WIDTH), lambda i, j: (i, first + j)),
                 out_shape=_sds(dproj.shape, BF16), aliases={nbr: 0})(*dqkvs, dproj)


def _chunk_tri(upper):
    row = lax.broadcasted_iota(jnp.int32, (HG_GROUP, HG_CHUNK, HG_CHUNK), 1)
    col = lax.broadcasted_iota(jnp.int32, (HG_GROUP, HG_CHUNK, HG_CHUNK), 2)
    return (row <= col if upper else row >= col).astype(BF16)


def _chunk_cumsum(x, tri):
    x3 = x.reshape(HG_GROUP, HG_CHUNK, x.shape[1])
    dims = (((2,), (1,)), ((0,), (0,)))
    out = None
    for _ in range(3):
        part = x3.astype(BF16)
        x3 = x3 - part.astype(F32)
        term = lax.dot_general(tri, part, dims, preferred_element_type=F32)
        out = term if out is None else out + term
    return out.reshape(x.shape)


def _hg_gates(f_raw, q_raw, lb, tri):
    sg = _sigmoid(f_raw)
    f = lb + (1.0 - lb) * sg
    k = 1.0 - f
    b = _chunk_cumsum(jnp.log(f), tri)
    sq = _sigmoid(q_raw)
    return sg, f, k, b, sq


def _hg_masks(rows):
    row = lax.broadcasted_iota(jnp.int32, (rows, rows), 0)
    col = lax.broadcasted_iota(jnp.int32, (rows, rows), 1)
    same = (row // HG_CHUNK) == (col // HG_CHUNK)
    return jnp.logical_and(row >= col, same), jnp.logical_and(row <= col, same)


def _per_chunk(rows_of):
    return jnp.concatenate([jnp.broadcast_to(r, (HG_CHUNK, r.shape[1])) for r in rows_of], axis=0)


def _hgrn_fwd(proj, lb_logits, name):
    t = proj.shape[0]
    nc = t // HG_CHUNK
    nh = HG_WIDTH // HG_HEAD
    rows = HG_GROUP * HG_CHUNK

    def body(q_ref, f_ref, i_ref, lg_ref, o_ref, st_ref, s_scr):
        @pl.when(pl.program_id(0) == 0)
        def _():
            s_scr[...] = jnp.zeros_like(s_scr)

        lg = lg_ref[...]
        lb_all = _sigmoid(lg[0:1] - lg[1:2])
        causal, _ = _hg_masks(rows)
        tri = _chunk_tri(False)
        for h in range(nh):
            sl = slice(h * HG_HEAD, (h + 1) * HG_HEAD)
            q_raw = q_ref[:, sl].astype(F32)
            _, _, k, b, sq = _hg_gates(f_ref[:, sl].astype(F32), q_raw, lb_all[:, sl], tri)
            v = i_ref[:, sl].astype(BF16)
            gls = [b[(g + 1) * HG_CHUNK - 1:(g + 1) * HG_CHUNK] for g in range(HG_GROUP)]
            bm = _per_chunk([b[g * HG_CHUNK + HG_CHUNK // 2 - 1:g * HG_CHUNK + HG_CHUNK // 2] for g in range(HG_GROUP)])
            qd = (q_raw * sq * jnp.exp(b)).astype(BF16)
            qm = (q_raw * sq * jnp.exp(b - bm)).astype(BF16)
            km = (k * jnp.exp(bm - b)).astype(BF16)
            ke = (k * jnp.exp(_per_chunk(gls) - b)).astype(BF16)
            a = jnp.where(causal, _dot_nt(qm, km), 0.0).astype(BF16)
            o_intra = _dot(a, v)
            st = s_scr[h]
            o_inter = []
            for g in range(HG_GROUP):
                rs = slice(g * HG_CHUNK, (g + 1) * HG_CHUNK)
                st_ref[g, sl, :] = st
                o_inter.append(_dot_nt(qd[rs], st.astype(BF16)))
                st = st * jnp.exp(gls[g]) + _dot_tn(v[rs], ke[rs])
            s_scr[h] = st
            o_ref[:, sl] = (o_intra + jnp.concatenate(o_inter, axis=0)).astype(BF16)

    blk = lambda j: pl.BlockSpec((rows, HG_WIDTH), lambda c: (c, j))
    return _call(body, name=name, grid=(nc // HG_GROUP,),
                 in_specs=[blk(0), blk(1), blk(2), pl.BlockSpec((2, HG_WIDTH), lambda c: (0, 0))],
                 out_specs=[blk(0), pl.BlockSpec((HG_GROUP, HG_WIDTH, HG_HEAD), lambda c: (c, 0, 0))],
                 out_shape=[_sds((t, HG_WIDTH), BF16), _sds((nc, HG_WIDTH, HG_HEAD))],
                 scratch_shapes=[pltpu.VMEM((nh, HG_HEAD, HG_HEAD), F32)])(proj, proj, proj, lb_logits)


def _hgrn_bwd(proj, lb_logits, states, do, dproj, name):
    t = proj.shape[0]
    ng = t // (HG_GROUP * HG_CHUNK)
    nh = HG_WIDTH // HG_HEAD
    rows = HG_GROUP * HG_CHUNK

    def body(q_ref, f_ref, i_ref, lg_ref, st_ref, do_ref, _, d_ref, dlb_ref, ds_scr):
        first = pl.program_id(0) == 0

        @pl.when(first)
        def _():
            ds_scr[...] = jnp.zeros_like(ds_scr)

        lg = lg_ref[...]
        lb_all = _sigmoid(lg[0:1] - lg[1:2])
        causal, _ = _hg_masks(rows)
        tri = _chunk_tri(False)
        tri_t = _chunk_tri(True)
        dlb = []
        for h in range(nh):
            sl = slice(h * HG_HEAD, (h + 1) * HG_HEAD)
            q_raw = q_ref[:, sl].astype(F32)
            lb = lb_all[:, sl]
            sg, f, k, b, sq = _hg_gates(f_ref[:, sl].astype(F32), q_raw, lb, tri)
            v = i_ref[:, sl].astype(BF16)
            gls = [b[(g + 1) * HG_CHUNK - 1:(g + 1) * HG_CHUNK] for g in range(HG_GROUP)]
            bm = _per_chunk([b[g * HG_CHUNK + HG_CHUNK // 2 - 1:g * HG_CHUNK + HG_CHUNK // 2] for g in range(HG_GROUP)])
            eb = jnp.exp(b)
            ebm = jnp.exp(b - bm)
            emb = jnp.exp(bm - b)
            egb = jnp.exp(_per_chunk(gls) - b)
            ke = k * egb
            qd_b, qm_b = (q_raw * sq * eb).astype(BF16), (q_raw * sq * ebm).astype(BF16)
            km_b, ke_b = (k * emb).astype(BF16), ke.astype(BF16)
            dov = do_ref[:, sl].astype(BF16)
            a = jnp.where(causal, _dot_nt(qm_b, km_b), 0.0).astype(BF16)
            da = jnp.where(causal, _dot_nt(dov, v), 0.0).astype(BF16)
            dkm = _dot_tn(da, qm_b)
            dst = ds_scr[h]
            dqd_s, dv_s, dke_s, dgl_s = [None] * HG_GROUP, [None] * HG_GROUP, [None] * HG_GROUP, [None] * HG_GROUP
            for g in reversed(range(HG_GROUP)):
                rs = slice(g * HG_CHUNK, (g + 1) * HG_CHUNK)
                st = st_ref[g, sl, :]
                dst_b = dst.astype(BF16)
                egl = jnp.exp(gls[g])
                dqd_s[g] = _dot(dov[rs], st.astype(BF16))
                dv_s[g] = _dot_nt(ke_b[rs], dst_b)
                dke_s[g] = _dot(v[rs], dst_b)
                dgl_s[g] = jnp.sum(dst * st, axis=0, keepdims=True) * egl
                dst = _dot_tn(dov[rs], qd_b[rs]) + dst * egl
            ds_scr[h] = dst
            dqm = _dot(da, km_b)
            dqd = jnp.concatenate(dqd_s, axis=0)
            dv = _dot_tn(a, dov) + jnp.concatenate(dv_s, axis=0)
            dke = jnp.concatenate(dke_s, axis=0)
            t1 = dke * ke
            db = dqm * qm_b.astype(F32) - dkm * km_b.astype(F32) + dqd * qd_b.astype(F32) - t1
            dgl = _per_chunk([dgl_s[g] + jnp.sum(t1[g * HG_CHUNK:(g + 1) * HG_CHUNK], axis=0, keepdims=True)
                              for g in range(HG_GROUP)])
            dlf = _chunk_cumsum(db, tri_t) + dgl
            df = dlf / f - (dkm * emb + dke * egb)
            d_ref[:, sl] = ((dqm * ebm + dqd * eb) * sq * (1.0 + q_raw * (1.0 - sq))).astype(BF16)
            d_ref[:, HG_WIDTH + h * HG_HEAD:HG_WIDTH + (h + 1) * HG_HEAD] = (
                df * (1.0 - lb) * sg * (1.0 - sg)).astype(BF16)
            d_ref[:, 2 * HG_WIDTH + h * HG_HEAD:2 * HG_WIDTH + (h + 1) * HG_HEAD] = dv.astype(BF16)
            dlb.append(jnp.sum(df * (1.0 - sg), axis=0, keepdims=True))
        _acc(dlb_ref, jnp.concatenate(dlb, axis=1), first)

    rev = lambda j: pl.BlockSpec((rows, HG_WIDTH), lambda c: (ng - 1 - c, j))
    return _call(body, name=name, grid=(ng,),
                 in_specs=[rev(0), rev(1), rev(2), pl.BlockSpec((2, HG_WIDTH), lambda c: (0, 0)),
                           pl.BlockSpec((HG_GROUP, HG_WIDTH, HG_HEAD), lambda c: (ng - 1 - c, 0, 0)), rev(0),
                           pl.BlockSpec(memory_space=pl.ANY)],
                 out_specs=[pl.BlockSpec((rows, 3 * HG_WIDTH), lambda c: (ng - 1 - c, 0)), _vec(HG_WIDTH)],
                 out_shape=[_sds(dproj.shape, BF16), _sds((1, HG_WIDTH))], aliases={6: 0},
                 scratch_shapes=[pltpu.VMEM((nh, HG_HEAD, HG_HEAD), F32)])(
                     proj, proj, proj, lb_logits, states, do, dproj)


def _to_sub(a, dil):
    t, w = a.shape
    return a if dil == 1 else a.reshape(t // dil, dil, w).transpose(1, 0, 2).reshape(t, w)


def _from_sub(a, dil):
    t, w = a.shape
    return a if dil == 1 else a.reshape(dil, t // dil, w).transpose(1, 0, 2).reshape(t, w)


def _att_mask(has_prev, seg):
    def place(v):
        v = v % ATT_BLOCK
        return v if seg == 1 else seg * (v % (ATT_BLOCK // seg)) + v // (ATT_BLOCK // seg)

    row = lax.broadcasted_iota(jnp.int32, (2 * ATT_BLOCK, 2 * ATT_BLOCK), 0)
    col = lax.broadcasted_iota(jnp.int32, (2 * ATT_BLOCK, 2 * ATT_BLOCK), 1)
    qi, kj = place(row), place(col)
    prev = jnp.logical_and(jnp.logical_and(col < ATT_BLOCK, kj >= qi), has_prev)
    cur = jnp.logical_and(col >= ATT_BLOCK, kj <= qi)
    return jnp.logical_or(prev, cur), lax.broadcasted_iota(jnp.int32, (1, 128), 1)


def _get(ref, sl):
    if len(ref.shape) == 2:
        return ref[:, sl]
    v = ref[:, :, sl]
    return v.reshape(ATT_BLOCK, v.shape[2])


def _put(ref, sl, val):
    if len(ref.shape) == 2:
        ref[:, sl] = val
    else:
        ref[:, :, sl] = val.reshape(ref.shape[0], ref.shape[1], val.shape[1])


def _att_spec(nb, dil, seg, width, col, back):
    bps = nb // dil

    def plain(n):
        return jnp.clip(n - back, 0, nb - 1), col

    def segmented(n):
        m = jnp.clip(n - back, 0, nb - 1)
        return 0, m // bps, m % bps, 0, col

    if seg == 1:
        return pl.BlockSpec((ATT_BLOCK, width), plain)
    return pl.BlockSpec((seg, None, None, ATT_BLOCK // seg, width), segmented)


def _att_shape(nb, dil, seg, width):
    t = nb * ATT_BLOCK
    return (t, width) if seg == 1 else (seg, dil, nb // dil, ATT_BLOCK // seg, width)


def _att_view(a, nb, dil, seg):
    return a.reshape(_att_shape(nb, dil, seg, a.shape[1]))


def _attn_fwd_block(q_ref, kc_ref, kp_ref, vc_ref, vp_ref, o_ref, l_ref, has_prev, seg, lane0):
    mask, lane = _att_mask(has_prev, seg)
    lo = lane < 64
    nq = ATT_BLOCK
    lse_all = jnp.zeros((nq, 128), F32)
    for hp in range(ATT_HEADS // 2):
        sl = slice(hp * 128, (hp + 1) * 128)
        q2 = _get(q_ref, sl)
        zero = jnp.zeros_like(q2)
        q2 = q2 * 0.125
        qs = jnp.concatenate([jnp.where(lo, q2, zero), jnp.where(lo, zero, q2)], axis=0)
        kk = jnp.concatenate([_get(kp_ref, sl), _get(kc_ref, sl)], axis=0)
        vv = jnp.concatenate([_get(vp_ref, sl), _get(vc_ref, sl)], axis=0)
        s = jnp.where(mask, _dot_nt(qs, kk), NEG)
        mx = jnp.max(s, axis=-1, keepdims=True)
        p = jnp.exp(s - mx)
        l = jnp.sum(p, axis=-1, keepdims=True)
        o = _dot(p.astype(BF16), vv) * (1.0 / l)
        _put(o_ref, sl, jnp.where(lo, o[:nq], o[nq:]).astype(BF16))
        lse = mx + jnp.log(l)
        lse_all = jnp.where(lane == lane0 + 2 * hp, lse[:nq], lse_all)
        lse_all = jnp.where(lane == lane0 + 2 * hp + 1, lse[nq:], lse_all)
    _put(l_ref, slice(None), lse_all)


def _attn_fwd(branches, name):
    t = branches[0][0].shape[0]
    nb = t // ATT_BLOCK
    nbr = len(branches)

    def body(*refs):
        n = pl.program_id(0)
        for i, (_, dil, seg) in enumerate(branches):
            _attn_fwd_block(*refs[5 * i:5 * i + 5], *refs[5 * nbr + 2 * i:5 * nbr + 2 * i + 2],
                            (n % (nb // dil)) != 0, seg, ATT_HEADS * i)

    in_specs, args, out_specs, out_shape = [], [], [], []
    for qkv, dil, seg in branches:
        c0 = qkv.shape[1] // ATT_WIDTH - 3
        in_specs += [_att_spec(nb, dil, seg, ATT_WIDTH, c0 + j, back) for j, back in [(0, 0), (1, 0), (1, 1), (2, 0), (2, 1)]]
        args += [_att_view(qkv, nb, dil, seg)] * 5
        out_specs += [_att_spec(nb, dil, seg, ATT_WIDTH, 0, 0), _att_spec(nb, dil, seg, 128, 0, 0)]
        out_shape += [_sds(_att_shape(nb, dil, seg, ATT_WIDTH), BF16), _sds(_att_shape(nb, dil, seg, 128))]
    out = _call(body, name=name, grid=(nb,), in_specs=in_specs, out_specs=out_specs, out_shape=out_shape)(*args)
    return [(out[2 * i].reshape(t, ATT_WIDTH), out[2 * i + 1].reshape(t, 128)) for i in range(nbr)]


def _combine_mix_in(os_, ls_, fine, o_hg, proj, og, ag, name, tr=512):
    t = os_[0].shape[0]
    nbr = len(os_)

    def body(*refs):
        o_refs, l_refs = refs[:nbr], refs[nbr:2 * nbr]
        oh_ref, g_ref, og_ref, ag_ref, a_ref, lt_ref, lts_ref, m_ref, to_natural, to_sub = refs[2 * nbr:]

        @pl.when(pl.program_id(0) == 0)
        def _():
            to_natural[...] = _regroup_matrix(tr, fine)
            to_sub[...] = _regroup_matrix(tr, tr // fine)

        lane = lax.broadcasted_iota(jnp.int32, (1, 128), 1)
        packed = l_refs[0][...] + _regroup(to_natural[...], sum(r[...] for r in l_refs[1:]).reshape(tr, 128))
        ls = [packed if i == 0 else pltpu.roll(packed, 128 - ATT_HEADS * i, 1) for i in range(nbr)]
        mx = functools.reduce(jnp.maximum, ls)
        tot = mx + jnp.log(sum(jnp.exp(l - mx) for l in ls))
        ws = [jnp.exp(l - tot) for l in ls]
        tot = jnp.where(lane < ATT_HEADS, tot, 0.0)
        lt_ref[...] = tot
        lts_ref[...] = _regroup(to_sub[...], tot).reshape(lts_ref.shape)
        o_vals = [o_refs[0]] + [_regroup(to_natural[...], r[...].reshape(tr, ATT_WIDTH)) for r in o_refs[1:]]
        pairs = []
        for hp in range(ATT_HEADS // 2):
            sl = slice(hp * 128, (hp + 1) * 128)
            acc = jnp.zeros((tr, 128), F32)
            for w, o in zip(ws, o_vals):
                wf = jnp.where(lane < 64, w[:, 2 * hp:2 * hp + 1], w[:, 2 * hp + 1:2 * hp + 2])
                acc = acc + wf * o[:, sl]
            pairs.append(acc)
        av = jnp.concatenate(pairs, axis=1)
        a_ref[...] = av
        m_ref[:, HG_WIDTH:] = (av * _rms(av) * ag_ref[...]).astype(BF16)
        for h in range(HG_WIDTH // HG_HEAD):
            sl = slice(h * HG_HEAD, (h + 1) * HG_HEAD)
            oh = oh_ref[:, sl].astype(F32)
            gv = g_ref[:, sl].astype(F32)
            m_ref[:, sl] = (oh * _rms(oh) * og_ref[...] * (gv * _sigmoid(gv))).astype(BF16)

    half = _rows(tr, ATT_WIDTH)
    sub = lambda a: a.reshape(fine, t // fine, a.shape[1])
    att, lse, lse_sub, mixin = _call(
        body, name=name, grid=(t // tr,),
        in_specs=[half] + [_sub_rows(tr, fine, ATT_WIDTH)] * (nbr - 1) + [_rows(tr, 128)] + [_sub_rows(tr, fine, 128)] * (nbr - 1)
        + [half, pl.BlockSpec((tr, HG_WIDTH), lambda i: (i, 3)), _vec(HG_HEAD), _vec(ATT_WIDTH)],
        out_specs=[half, _rows(tr, 128), _sub_rows(tr, fine, 128), _rows(tr, D_MODEL)],
        out_shape=[_sds((t, ATT_WIDTH)), _sds((t, 128)), _sds((fine, t // fine, 128)), _sds((t, D_MODEL), BF16)],
        scratch_shapes=[pltpu.VMEM((tr, tr), BF16)] * 2)(
            os_[0], *map(sub, os_[1:]), ls_[0], *map(sub, ls_[1:]), o_hg, proj, og, ag)
    return att, lse, lse_sub.reshape(t, 128), mixin


def _attn_bwd_block(q_ref, kc_ref, kp_ref, vc_ref, vp_ref, do_ref, l_ref, d_ref, out_ref, carry, has_prev, seg):
    w = ATT_WIDTH
    nq = ATT_BLOCK
    mask, lane = _att_mask(has_prev, seg)
    lo = lane < 64
    lse, ddv = _get(l_ref, slice(None)), _get(d_ref, slice(None))
    for hp in range(ATT_HEADS // 2):
        sl = slice(hp * 128, (hp + 1) * 128)
        sk = slice(w + hp * 128, w + (hp + 1) * 128)
        sv = slice(2 * w + hp * 128, 2 * w + (hp + 1) * 128)
        q2, do2 = _get(q_ref, sl), _get(do_ref, sl)
        zero = jnp.zeros_like(q2)
        q2 = q2 * 0.125
        qs = jnp.concatenate([jnp.where(lo, q2, zero), jnp.where(lo, zero, q2)], axis=0)
        dos = jnp.concatenate([jnp.where(lo, do2, zero), jnp.where(lo, zero, do2)], axis=0)
        kk = jnp.concatenate([_get(kp_ref, sl), _get(kc_ref, sl)], axis=0)
        vv = jnp.concatenate([_get(vp_ref, sl), _get(vc_ref, sl)], axis=0)
        ls = jnp.concatenate([lse[:, 2 * hp:2 * hp + 1], lse[:, 2 * hp + 1:2 * hp + 2]], axis=0)
        dh = jnp.concatenate([ddv[:, 2 * hp:2 * hp + 1], ddv[:, 2 * hp + 1:2 * hp + 2]], axis=0)
        p = jnp.exp(jnp.where(mask, _dot_nt(qs, kk) - ls, NEG))
        ds = (p * (_dot_nt(dos, vv) - dh)).astype(BF16)
        dq = _dot(ds, kk) * 0.125
        dk = _dot_tn(ds, qs)
        dv = _dot_tn(p.astype(BF16), dos)
        _put(out_ref, sl, carry[:, sl].astype(BF16))
        _put(out_ref, sk, (carry[:, sk] + dk[:nq]).astype(BF16))
        _put(out_ref, sv, (carry[:, sv] + dv[:nq]).astype(BF16))
        carry[:, sl] = jnp.where(lo, dq[:nq], dq[nq:])
        carry[:, sk] = dk[nq:]
        carry[:, sv] = dv[nq:]


def _attn_bwd(branches, name):
    t = branches[0][0].shape[0]
    nb = t // ATT_BLOCK
    nbr = len(branches)
    w = ATT_WIDTH

    def body(*refs):
        ins, outs, carries = refs[:8 * nbr], refs[8 * nbr:9 * nbr], refs[9 * nbr:]
        n = pl.program_id(0)

        @pl.when(n == 0)
        def _():
            for carry in carries:
                carry[...] = jnp.zeros_like(carry)

        @pl.when(n < nb)
        def _():
            for i, branch in enumerate(branches):
                dil, seg = branch[4:]
                _attn_bwd_block(*ins[8 * i:8 * i + 8], outs[i], carries[i], (n % (nb // dil)) != 0, seg)

        @pl.when(n == nb)
        def _():
            for i in range(nbr):
                _put(outs[i], slice(None), carries[i][...].astype(BF16))

    in_specs, args, out_specs, out_shape = [], [], [], []
    for qkv, dout, lse, dd, dil, seg in branches:
        c0 = qkv.shape[1] // w - 3
        in_specs += [_att_spec(nb, dil, seg, w, c0 + j, back) for j, back in [(0, 0), (1, 0), (1, 1), (2, 0), (2, 1)]]
        in_specs += [_att_spec(nb, dil, seg, w, 0, 0), _att_spec(nb, dil, seg, 128, 0, 0), _att_spec(nb, dil, seg, 128, 0, 0)]
        args += [_att_view(a, nb, dil, seg) for a in [qkv] * 5 + [dout, lse, dd]]
        out_specs += [_att_spec(nb, dil, seg, 3 * w, 0, 1)]
        out_shape += [_sds(_att_shape(nb, dil, seg, 3 * w), BF16)]
    out = _call(body, name=name, grid=(nb + 1,), in_specs=in_specs, out_specs=out_specs, out_shape=out_shape,
                scratch_shapes=[pltpu.VMEM((ATT_BLOCK, 3 * w), F32)] * nbr)(*args)
    return [o.reshape(t, 3 * w) for o in out]


def _local_step(x, tgt, mod, norm1_g, lb_logits, og, ag, norm2_g, fg, get_w, put_g, late=lambda a: a, project=None):
    shift1, scale1, gate1, shift2, scale2, gate2 = [mod[:, i * D_MODEL:(i + 1) * D_MODEL] for i in range(6)]
    fg = fg.reshape(1, D_MODEL)

    h1 = _norm_mod(x, norm1_g, scale1, shift1, "norm_mod1")
    if project is None:
        w_in = get_w("w_in", h1)
        proj = _mm_nn(h1, w_in, "mm_in", out_dtype=BF16)
    else:
        proj, w_in = project(h1)
    o_hg, states = _hgrn_fwd(proj, lb_logits, "hgrn_fwd")
    fine = DILATIONS[-1]
    layouts = [(d, 1 if d == 1 else fine // d) for d in DILATIONS]
    qkv_fine = _to_sub(proj, fine)
    qkvs = [proj if d == 1 else qkv_fine for d in DILATIONS]
    natural = lambda a, d: a if d == 1 else _from_sub(a, fine)
    outs = _attn_fwd([(q, d, seg) for q, (d, seg) in zip(qkvs, layouts)], "attn_fwd")
    att, lse, lse_fine, mixin = _combine_mix_in([o for o, _ in outs], [l for _, l in outs], fine,
                                                o_hg, proj, og, ag, "attn_combine_mix_in")
    w_out = get_w("w_out", mixin)
    mix, x2, h2 = _out_resid_norm_mod(x, mixin, w_out, gate1, norm2_g, scale2, shift2, "mm_out_resid_norm_mod2")
    w_gu = get_w("w_gu", h2)
    a_ff, u_ff, act = _mm_gate_up(h2, w_gu, "mm_gu")
    w_down = get_w("w_down", act)
    dx3, dffn, loss_v, dfg, dgate2 = _down_loss(x2, act, w_down, gate2, fg, tgt, "mm_down_loss")

    dffn = put_g("w_down", *_mm_tn(act, dffn, 1, "mm_down_dw", tm=2048, tk=D_FF // 2), dffn)
    dau = _mm_down_dx(dffn, w_down, a_ff, u_ff, "mm_down_dx")
    dau = put_g("w_gu", *_mm_tn(h2, dau, N_SHARD, "mm_gu_dw", tm=x.shape[0], tk=512), dau)
    dx2, dshift2, dscale2, dg2, dgate1, dmix = _norm_mod_bwd(
        dau, x2, norm2_g, scale2, dx3, "mm_gu_dx_norm_bwd", gate=gate1, mix=mix, w=w_gu)
    dmix = put_g("w_out", *_mm_tn(mixin, dmix, 1, "mm_out_dw", tm=x.shape[0], tk=512), dmix)
    do_hg, dproj, datt, dd, datt_fine, dd_fine, dog, dag = _mix_in_bwd(
        dmix, w_out, o_hg, proj, att, og, ag, fine, "mm_out_dx_mix_in_bwd")
    datts = _attn_bwd([(q,) + ((datt, lse, dd) if d == 1 else (datt_fine, lse_fine, dd_fine)) + (d, seg)
                       for q, (d, seg) in zip(qkvs, layouts)], "attn_bwd")
    dproj, dlb = _hgrn_bwd(proj, lb_logits, states, do_hg, dproj, "hgrn_bwd")
    dproj = late(dproj)
    dproj = _dproj(dproj, [natural(a, d) for a, d in zip(datts, DILATIONS)], "dproj")
    dproj = put_g("w_in", *_mm_tn(h1, dproj, N_SHARD, "mm_in_dw", tm=x.shape[0], tk=512, group=2), dproj)
    dx, dshift1, dscale1, dg1 = _norm_mod_bwd(dproj, x, norm1_g, scale1, dx2, "mm_in_dx_norm_bwd", w=w_in)

    stats = jnp.concatenate([loss_v, dfg, dg2, dg1, dlb, dag, dog,
                             dshift1, dscale1, dgate1, dshift2, dscale2, dgate2], axis=1)
    return dx, stats


def _place():
    x, y, c = lax.axis_index("x"), lax.axis_index("y"), lax.axis_index("c")
    return x, y, c


def _chip_peers(x, y, c):
    return [(1 - x, y, c), (x, 1 - y, c), (1 - x, 1 - y, c)]


def _comm_call(body, name, n_in, out_shape, scratch_shapes):
    hbm = pl.BlockSpec(memory_space=pl.ANY)
    return pl.pallas_call(body, name=name, in_specs=[hbm] * n_in, out_specs=[hbm] * len(out_shape),
                          out_shape=out_shape, scratch_shapes=scratch_shapes)


_HBM = pl.BlockSpec(memory_space=pltpu.HBM)
_SEM = pl.BlockSpec(memory_space=pltpu.SEMAPHORE)
_EFFECT = pltpu.SideEffectType.DATAFLOW_SIDE_EFFECTING


def _exchange_copy(bufs, send, recv, j, peer, place, kind):
    x, y, c = place
    target = peer
    if kind == "gather":
        src = dst = bufs[0].at[2 * x + y]
    elif kind == "scatter":
        src, dst = bufs[0].at[2 * peer[0] + peer[1]], bufs[1].at[j]
    else:
        half = bufs[0].shape[1] // 2
        rows = pl.ds(c * half, half)
        if kind == "half":
            src = dst = bufs[0].at[2 * x + y, rows]
        else:
            src = dst = bufs[0].at[2 * peer[0] + peer[1], rows]
            target = (x, y, 1 - c)
    return pltpu.make_async_remote_copy(src_ref=src, dst_ref=dst, send_sem=send.at[j], recv_sem=recv.at[j],
                                        device_id=target, device_id_type=MESH)


def _exchange_start(groups, after, kind, name):
    sizes = [len(g) for g in groups]
    flat = [b for g in groups for b in g]
    ng, nb = len(groups), len(flat)

    def body(*refs):
        bufs, sems = refs[:nb], refs[nb + 1:nb + 1 + 2 * ng]
        x, y, c = _place()
        for j, peer in enumerate(_chip_peers(x, y, c)):
            at = 0
            for i, size in enumerate(sizes):
                _exchange_copy(bufs[at:at + size], sems[2 * i], sems[2 * i + 1], j, peer, (x, y, c), kind).start()
                at += size

    any_space = pl.BlockSpec(memory_space=pl.ANY)
    out = pl.pallas_call(
        body, name=name, in_specs=[_HBM] * nb + [any_space],
        out_specs=[_SEM] * (2 * ng) + [_HBM] * nb + [any_space],
        out_shape=[pltpu.SemaphoreType.DMA((3,))] * (2 * ng) + [pltpu.HBM(b.shape, b.dtype) for b in flat]
        + [_sds(after.shape, after.dtype)],
        input_output_aliases={i: 2 * ng + i for i in range(nb + 1)},
        compiler_params=pltpu.CompilerParams(has_side_effects=_EFFECT),
    )(*[pltpu.with_memory_space_constraint(b, pltpu.HBM) for b in flat], after)
    started, at = [], 2 * ng
    for i, size in enumerate(sizes):
        started.append((out[2 * i], out[2 * i + 1], tuple(out[at:at + size])))
        at += size
    return started, out[-1]


def _exchange_wait(started, after, kind, name):
    send, recv, bufs = started
    nb = len(bufs)

    def body(*refs):
        x, y, c = _place()
        for j, peer in enumerate(_chip_peers(x, y, c)):
            cp = _exchange_copy(refs[:nb], refs[nb], refs[nb + 1], j, peer, (x, y, c), kind)
            cp.wait_send()
            cp.wait_recv()

    return pl.pallas_call(
        body, name=name, in_specs=[_HBM] * nb + [_SEM, _SEM, pl.BlockSpec(memory_space=pl.ANY)],
        out_specs=[_HBM] * nb, out_shape=[pltpu.HBM(b.shape, b.dtype) for b in bufs],
        input_output_aliases={i: i for i in range(nb)},
        compiler_params=pltpu.CompilerParams(has_side_effects=_EFFECT),
    )(*bufs, send, recv, after)


def _sibling_copies(v_refs, l_refs, send, recv):
    x, y, c = _place()
    return [pltpu.make_async_remote_copy(src_ref=v, dst_ref=l, send_sem=send.at[a], recv_sem=recv.at[a],
                                         device_id=(x, y, 1 - c), device_id_type=MESH)
            for a, (v, l) in enumerate(zip(v_refs, l_refs))]


def _sibling_start(vs, after, name):
    vs = list(vs)
    n = len(vs)
    lands = [lax.empty(v.shape, v.dtype) for v in vs]

    def body(*refs):
        for cp in _sibling_copies(refs[:n], refs[n:2 * n], refs[2 * n + 1], refs[2 * n + 2]):
            cp.start()

    any_space = pl.BlockSpec(memory_space=pl.ANY)
    out = pl.pallas_call(
        body, name=name, in_specs=[_HBM] * (2 * n) + [any_space],
        out_specs=[_SEM, _SEM] + [_HBM] * (2 * n) + [any_space],
        out_shape=[pltpu.SemaphoreType.DMA((n,))] * 2 + [pltpu.HBM(b.shape, b.dtype) for b in vs + lands]
        + [_sds(after.shape, after.dtype)],
        input_output_aliases={i: 2 + i for i in range(2 * n + 1)},
        compiler_params=pltpu.CompilerParams(has_side_effects=_EFFECT),
    )(*[pltpu.with_memory_space_constraint(b, pltpu.HBM) for b in vs + lands], after)
    return (out[0], out[1], tuple(out[2:2 + n]), tuple(out[2 + n:2 + 2 * n])), out[-1]


def _sibling_wait(started, after, name):
    send, recv, vs, lands = started
    n = len(vs)

    def body(*refs):
        for cp in _sibling_copies(refs[:n], refs[n:2 * n], refs[2 * n], refs[2 * n + 1]):
            cp.wait_send()
            cp.wait_recv()

    out = pl.pallas_call(
        body, name=name, in_specs=[_HBM] * (2 * n) + [_SEM, _SEM, pl.BlockSpec(memory_space=pl.ANY)],
        out_specs=[_HBM] * (2 * n), out_shape=[pltpu.HBM(b.shape, b.dtype) for b in vs + lands],
        input_output_aliases={i: i for i in range(2 * n)},
        compiler_params=pltpu.CompilerParams(has_side_effects=_EFFECT),
    )(*vs, *lands, send, recv, after)
    return out[:n], out[n:]


def _swap_sibling(vs, name):
    n = len(vs)

    def body(*refs):
        v_refs, o_refs, (send, recv) = refs[:n], refs[n:2 * n], refs[2 * n:]
        x, y, c = _place()
        cps = [pltpu.make_async_remote_copy(
            src_ref=v_refs[a], dst_ref=o_refs[a], send_sem=send.at[a], recv_sem=recv.at[a],
            device_id=(x, y, 1 - c), device_id_type=MESH) for a in range(n)]
        for cp in cps:
            cp.start()
        for cp in cps:
            cp.wait()

    return _comm_call(body, name, n, [_sds(v.shape, v.dtype) for v in vs],
                      [pltpu.SemaphoreType.DMA((n,)), pltpu.SemaphoreType.DMA((n,))])(*vs)


def _everyone(x, y, c):
    return [(1 - x if k & 4 else x, 1 - y if k & 2 else y, 1 - c if k & 1 else c) for k in range(1, 8)]


def _all_gather_copies(land_ref, send, recv, arriving):
    x, y, c = _place()
    me = 4 * x + 2 * y + c
    return [pltpu.make_async_remote_copy(
        src_ref=land_ref.at[me], dst_ref=land_ref.at[4 * p[0] + 2 * p[1] + p[2] if arriving else me],
        send_sem=send.at[k], recv_sem=recv.at[k], device_id=p, device_id_type=MESH)
        for k, p in enumerate(_everyone(x, y, c))]


def _all_gather_start(v, name):
    x, y, c = _place()
    land = lax.dynamic_update_slice(lax.empty((8,) + v.shape, v.dtype), v[None], (4 * x + 2 * y + c, 0, 0))

    def body(land_ref, v_ref, send, recv, land_out, v_out):
        for cp in _all_gather_copies(land_ref, send, recv, False):
            cp.start()

    any_space = pl.BlockSpec(memory_space=pl.ANY)
    out = pl.pallas_call(
        body, name=name, in_specs=[_HBM, any_space], out_specs=[_SEM, _SEM, _HBM, any_space],
        out_shape=[pltpu.SemaphoreType.DMA((7,))] * 2 + [pltpu.HBM(land.shape, land.dtype), _sds(v.shape, v.dtype)],
        input_output_aliases={0: 2, 1: 3}, compiler_params=pltpu.CompilerParams(has_side_effects=_EFFECT),
    )(pltpu.with_memory_space_constraint(land, pltpu.HBM), v)
    return tuple(out[:3]), out[3]


def _all_gather_wait(started, after, name):
    send, recv, land = started

    def body(land_ref, send, recv, after_ref, land_out):
        for cp in _all_gather_copies(land_ref, send, recv, True):
            cp.wait_send()
            cp.wait_recv()

    return pl.pallas_call(
        body, name=name, in_specs=[_HBM, _SEM, _SEM, pl.BlockSpec(memory_space=pl.ANY)], out_specs=_HBM,
        out_shape=pltpu.HBM(land.shape, land.dtype), input_output_aliases={0: 0},
        compiler_params=pltpu.CompilerParams(has_side_effects=_EFFECT),
    )(land, send, recv, after)


def _cast_place(ws, shard, name):
    n = len(ws)

    def body(s_ref, *refs):
        for w_ref, o_ref in zip(refs[:n], refs[n:]):
            o_ref[0] = w_ref[...].astype(BF16)

    return pl.pallas_call(
        body, name=name, out_shape=[_sds((N_SHARD,) + w.shape, BF16) for w in ws],
        grid_spec=pltpu.PrefetchScalarGridSpec(
            num_scalar_prefetch=1, grid=(4,),
            in_specs=[pl.BlockSpec((w.shape[0] // 4, w.shape[1]), lambda i, s: (i, 0)) for w in ws],
            out_specs=[pl.BlockSpec((1, w.shape[0] // 4, w.shape[1]), lambda i, s: (s[0], i, 0)) for w in ws]),
        compiler_params=pltpu.CompilerParams(dimension_semantics=("arbitrary",), vmem_limit_bytes=VMEM_LIMIT),
    )(shard.reshape(1).astype(jnp.int32), *ws)


def _mod_rows(c8, w_ada, b_ada, name):
    n = w_ada.shape[1]

    def gather(src_ref, dst_ref, send, recv, loc, base):
        x, y, c = _place()
        me = 4 * x + 2 * y + c
        own = pltpu.make_async_copy(src_ref, dst_ref.at[me], loc)
        own.start()
        peers = _everyone(x, y, c)
        sends = [pltpu.make_async_remote_copy(src_ref=src_ref, dst_ref=dst_ref.at[me], send_sem=send.at[base + k],
                                              recv_sem=recv.at[base + k], device_id=p, device_id_type=MESH)
                 for k, p in enumerate(peers)]
        for cp in sends:
            cp.start()
        for k, p in enumerate(peers):
            pltpu.make_async_remote_copy(src_ref=src_ref, dst_ref=dst_ref.at[4 * p[0] + 2 * p[1] + p[2]],
                                         send_sem=send.at[base + k], recv_sem=recv.at[base + k], device_id=p,
                                         device_id_type=MESH).wait_recv()
        for cp in sends:
            cp.wait_send()
        own.wait()

    def body(c_ref, w_ref, b_ref, a_ref, parts_ref, c_all, part, send, recv, loc):
        gather(c_ref, c_all, send, recv, loc.at[0], 0)
        cv = jnp.max(c_all[...], axis=1)
        ca = cv * _sigmoid(cv)
        a_ref[...] = ca
        part[...] = jnp.dot(ca, w_ref[...], precision=lax.Precision.HIGHEST, preferred_element_type=F32) + b_ref[...]
        gather(part, parts_ref, send, recv, loc.at[1], 7)

    vmem = pl.BlockSpec(memory_space=pltpu.VMEM)
    return pl.pallas_call(
        body, name=name, in_specs=[vmem] * 3, out_specs=[vmem, vmem],
        out_shape=[_sds((8, D_MODEL)), _sds((8, 8, n))],
        scratch_shapes=[pltpu.VMEM((8, 8, D_MODEL), F32), pltpu.VMEM((8, n), F32), pltpu.SemaphoreType.DMA((14,)),
                        pltpu.SemaphoreType.DMA((14,)), pltpu.SemaphoreType.DMA((2,))],
        compiler_params=pltpu.CompilerParams(vmem_limit_bytes=VMEM_LIMIT))(c8, w_ada, b_ada)


def _sum_received(gs, shard, lands, name):
    n = len(gs)

    def body(s_ref, *refs):
        for g_ref, l_ref, o_ref in zip(refs[:n], refs[n:2 * n], refs[2 * n:]):
            o_ref[...] = ((g_ref[0] + l_ref[0].astype(F32)) + l_ref[1].astype(F32)) + l_ref[2].astype(F32)

    quarter = lambda g: (g.shape[1] // 4, g.shape[2])
    return pl.pallas_call(
        body, name=name, out_shape=[_sds(g.shape[1:]) for g in gs],
        grid_spec=pltpu.PrefetchScalarGridSpec(
            num_scalar_prefetch=1, grid=(4,),
            in_specs=[pl.BlockSpec((1,) + quarter(g), lambda i, s: (s[0], i, 0)) for g in gs]
            + [pl.BlockSpec((3,) + quarter(g), lambda i, s: (0, i, 0)) for g in gs],
            out_specs=[pl.BlockSpec(quarter(g), lambda i, s: (i, 0)) for g in gs]),
        compiler_params=pltpu.CompilerParams(dimension_semantics=("arbitrary",), vmem_limit_bytes=VMEM_LIMIT),
    )(shard.reshape(1).astype(jnp.int32), *gs, *lands)


def _adamw_outer(w, ct, dm, m, v, name):
    k, n = w.shape
    tr = k // 4

    def body(w_ref, c_ref, d_ref, m_ref, v_ref, g_out, d_out, m_out, v_out):
        cv = c_ref[...]
        dv = d_ref[...]
        g = cv[:, 0:1] * dv[0:1, :]
        for i in range(1, 8):
            g = g + cv[:, i:i + 1] * dv[i:i + 1, :]
        g_out[...] = g
        d_out[...], m_out[...], v_out[...] = _adamw_math(w_ref[...], g, m_ref[...], v_ref[...])

    row = _rows(tr, n)
    return _call(body, name=name, grid=(4,),
                 in_specs=[row, _rows(tr, 8), pl.BlockSpec((8, n), lambda i: (0, 0)), row, row],
                 out_specs=[row] * 4, out_shape=[_sds((k, n))] * 4)(w, ct, dm, m, v)


def _adamw_math(w, g, m, v):
    m_new = ADAM_B1 * m + (1.0 - ADAM_B1) * g
    v_new = ADAM_B2 * v + (1.0 - ADAM_B2) * (g * g)
    m_hat = m_new / (1.0 - ADAM_B1 ** ADAM_STEP)
    v_hat = v_new / (1.0 - ADAM_B2 ** ADAM_STEP)
    return -ADAM_LR * (m_hat / (jnp.sqrt(v_hat) + ADAM_EPS) + ADAM_WD * w), m_new, v_new


def _small_update(stats, smalls, name):
    offsets = [ST_DMOD, ST_DG1, ST_DLB, ST_DOG, ST_DAG, ST_DG2, ST_DFG]
    lb_index = 2

    def body(*refs):
        s_ref, ins, l_ref, outs = refs[0], refs[1:22], refs[22], refs[23:]
        tot = s_ref[0:1, :]
        for i in range(1, 8):
            tot = tot + s_ref[i:i + 1, :]
        l_ref[...] = jnp.zeros((1, 128), F32) + (0.5 / D_MODEL) * jnp.sum(tot[:, ST_LOSS:ST_LOSS + D_MODEL])
        for p, off in enumerate(offsets):
            w_ref, m_ref, v_ref = ins[3 * p:3 * p + 3]
            g_out, d_out, m_out, v_out = outs[4 * p:4 * p + 4]
            g = tot[:, off:off + w_ref.shape[1]]
            if p == lb_index:
                lg = w_ref[...]
                lb = _sigmoid(lg[0:1] - lg[1:2])
                g = g * lb * (1.0 - lb)
            for r in range(w_ref.shape[0]):
                rows = slice(r, r + 1)
                gr = g if r == 0 else -g
                delta, m_new, v_new = _adamw_math(w_ref[rows, :], gr, m_ref[rows, :], v_ref[rows, :])
                g_out[rows, :] = gr
                d_out[rows, :] = delta
                m_out[rows, :] = m_new
                v_out[rows, :] = v_new

    full = lambda a: pl.BlockSpec(a.shape, lambda i: (0, 0))
    flat = [a for t in smalls for a in t]
    return _call(body, name=name, grid=(1,),
                 in_specs=[full(stats)] + [full(a) for a in flat],
                 out_specs=[pl.BlockSpec((1, 128), lambda i: (0, 0))] + [full(t[0]) for t in smalls for _ in range(4)],
                 out_shape=[_sds((1, 128))] + [_sds(t[0].shape) for t in smalls for _ in range(4)])(stats, *flat)


def _adamw(params, name):
    n = len(params)

    def body(*refs):
        for p in range(n):
            w_ref, ga_ref, gb_ref, m_ref, v_ref = refs[5 * p:5 * p + 5]
            g_out, d_out, m_out, v_out = refs[5 * n + 4 * p:5 * n + 4 * p + 4]
            g = ga_ref[...] + gb_ref[...]
            g_out[...] = g
            d_out[...], m_out[...], v_out[...] = _adamw_math(w_ref[...], g, m_ref[...], v_ref[...])

    row = lambda w: _rows(w.shape[0] // 4, w.shape[1])
    out = _call(body, name=name, grid=(4,), in_specs=[row(p[0]) for p in params for _ in range(5)],
                out_specs=[row(p[0]) for p in params for _ in range(4)],
                out_shape=[_sds(p[0].shape) for p in params for _ in range(4)])(*[a for p in params for a in p])
    return [tuple(out[4 * p:4 * p + 4]) for p in range(n)]


def kernel(x, c, w_ada, b_ada, norm1_g, w_in, hg_lb_logits, hg_onorm_g, att_onorm_g, w_out, norm2_g, w_gate_up, w_down, final_g, loss_target, m_w_ada, m_b_ada, m_norm1_g, m_w_in, m_hg_lb_logits, m_hg_onorm_g, m_att_onorm_g, m_w_out, m_norm2_g, m_w_gate_up, m_w_down, m_final_g, v_w_ada, v_b_ada, v_norm1_g, v_w_in, v_hg_lb_logits, v_hg_onorm_g, v_att_onorm_g, v_w_out, v_norm2_g, v_w_gate_up, v_w_down, v_final_g):
    ix, iy, ic = _place()
    shard = 2 * ix + iy
    sample = 4 * ix + 2 * iy + ic
    n_ada = w_ada.shape[2]

    shards = [w_in[0], w_out[0], w_gate_up[0], w_down[0]]
    names = ["w_in", "w_out", "w_gu", "w_down"]
    shapes = [(N_SHARD,) + w.shape for w in shards]
    placed = [(_cast_place(shards[:1], shard, "place_w_in")[0],)]
    placed += [(p,) for p in _cast_place(shards[1:], shard, "place_rest")]

    b_part = lax.dynamic_slice(b_ada, (0, shard * n_ada), (1, n_ada))
    c_act, parts = _mod_rows(jnp.broadcast_to(c, (8, D_MODEL)), w_ada[0], b_part, "mod_rows")
    parts = parts[::2]
    mod = lax.dynamic_index_in_dim(parts, sample, axis=1, keepdims=False).reshape(1, 6 * D_MODEL)
    (first,), mod = _exchange_start(placed[:1], mod, "half", "gather_start_w_in")
    gathering = {}

    def get_w(name, after):
        if name == "w_in":
            halves = _exchange_wait(first, after, "half", "gather_wait_w_in")
            (passing,), token = _exchange_start([tuple(halves)], mod, "forward", "forward_start_w_in")
            (full,) = _exchange_wait(passing, token, "forward", "forward_wait_w_in")
            rest, full = _exchange_start(placed[1:], full, "gather", "gather_start_rest")
            gathering.update(zip(names[1:], rest))
            return full
        (full,) = _exchange_wait(gathering[name], after, "gather", "gather_wait_" + name)
        return full if name == "w_gu" else full.reshape(1, -1, D_MODEL)

    scattering = {}

    def put_g(name, g, g_bf16, then):
        shape = shapes[names.index(name)]
        land = lax.empty((3,) + shape[1:], BF16)
        (started,), then = _exchange_start([(g_bf16.reshape(shape), land)], then, "scatter", "scatter_start_" + name)
        scattering[name] = (g.reshape(shape), started)
        return then

    def summed(group, after, tag):
        lands = [_exchange_wait(scattering[nm][1], after, "scatter", "scatter_wait_" + nm)[1] for nm in group]
        return _sum_received([scattering[nm][0] for nm in group], shard, lands, "sum_" + tag)

    early = ["w_down", "w_gu", "w_out"]
    swapping = []

    def late(a):
        started, a = _sibling_start(summed(early, a, "early"), a, "swap_start")
        swapping.append(started)
        return a

    def project(h1):
        own = _mm_own_shard(h1, shards[0], shard, N_SHARD, "mm_in_own")
        w_full = get_w("w_in", own)
        return _mm_other_shards(h1, w_full, own, shard, "mm_in_rest"), w_full

    dx, stats = _local_step(x[0], loss_target[0], mod, norm1_g, hg_lb_logits, hg_onorm_g, att_onorm_g,
                            norm2_g, final_g, get_w, put_g, late, project)

    gathering_stats, stats = _all_gather_start(stats, "stats_start")
    moments = [(m_w_in, v_w_in), (m_w_out, v_w_out), (m_w_gate_up, v_w_gate_up), (m_w_down, v_w_down)]

    def update(group, sums, other, tag):
        params = [(shards[names.index(nm)], s, o, moments[names.index(nm)][0][0], moments[names.index(nm)][1][0])
                  for nm, s, o in zip(group, sums, other)]
        return dict(zip(group, _adamw(params, "adamw_" + tag)))

    sums, other = _sibling_wait(swapping[0], stats, "swap_wait")
    done = update(early, sums, other, "early")
    sum_in = summed(["w_in"], done["w_out"][1], "w_in")
    done.update(update(["w_in"], sum_in, _swap_sibling(sum_in, "swap_sum_in"), "w_in"))

    stats_all = _all_gather_wait(gathering_stats, done["w_in"][1], "stats_wait").reshape(8, ST_WIDTH)
    dmod = lax.dynamic_slice(stats_all, (0, ST_DMOD + shard * n_ada), (8, n_ada))

    as_row = lambda a: a.reshape(1, -1) if a.ndim == 1 else a
    smalls = [tuple(as_row(a) for a in t) for t in [
        (b_ada, m_b_ada, v_b_ada), (norm1_g, m_norm1_g, v_norm1_g),
        (hg_lb_logits, m_hg_lb_logits, v_hg_lb_logits), (hg_onorm_g, m_hg_onorm_g, v_hg_onorm_g),
        (att_onorm_g, m_att_onorm_g, v_att_onorm_g), (norm2_g, m_norm2_g, v_norm2_g),
        (final_g, m_final_g, v_final_g)]]
    loss, *small_out = _small_update(stats_all, smalls, "small_update")
    shapes_out = [b_ada.shape, norm1_g.shape, hg_lb_logits.shape, hg_onorm_g.shape, att_onorm_g.shape,
                  norm2_g.shape, final_g.shape]
    sg, sd, sm, sv = [[small_out[4 * p + i].reshape(shapes_out[p]) for p in range(7)] for i in range(4)]

    ada = _adamw_outer(w_ada[0], c_act.T, dmod, m_w_ada[0], v_w_ada[0], "adamw_w_ada")
    big = [ada] + [done[nm] for nm in names]
    bg, bd, bm, bv = [[t[i][None] for t in big] for i in range(4)]

    def order(b, s):
        return [b[0], s[0], s[1], b[1], s[2], s[3], s[4], b[2], s[5], b[3], b[4], s[6]]

    return (loss[0, 0], dx[None], *order(bg, sg), *order(bd, sd), *order(bm, sm), *order(bv, sv))
```

```python
import functools

import jax
import jax.numpy as jnp
from jax import lax
from jax.experimental import pallas as pl
from jax.experimental.pallas import tpu as pltpu

F32 = jnp.float32
BF16 = jnp.bfloat16
MESH = pl.DeviceIdType.MESH

D_MODEL = 1024
HG_WIDTH = 512
HG_HEAD = 128
HG_CHUNK = 64
HG_GROUP = 4
ATT_WIDTH = 512
ATT_HEADS = 8
ATT_BLOCK = 128
DILATIONS = (1, 4, 16)
D_FF = 2816
IN_WIDTH = 3584
N_SHARD = 4
RMS_EPS = 1e-6
NEG = -1e30

ADAM_LR = 0.001
ADAM_B1 = 0.9
ADAM_B2 = 0.999
ADAM_EPS = 1e-08
ADAM_WD = 0.01
ADAM_STEP = 10

VMEM_LIMIT = 56 * 2**20

ST_LOSS, ST_DFG, ST_DG2, ST_DG1 = 0, 1024, 2048, 3072
ST_DLB, ST_DAG, ST_DOG, ST_DMOD = 4096, 4608, 5120, 5248
ST_WIDTH = 5248 + 6144


def _call(body, *, name, grid, in_specs, out_specs, out_shape, scratch_shapes=(), aliases=None):
    return pl.pallas_call(
        body, name=name, grid=grid, in_specs=in_specs, out_specs=out_specs, out_shape=out_shape,
        scratch_shapes=list(scratch_shapes), input_output_aliases=aliases or {},
        compiler_params=pltpu.CompilerParams(
            dimension_semantics=("arbitrary",) * len(grid), vmem_limit_bytes=VMEM_LIMIT))


def _sds(shape, dtype=F32):
    return jax.ShapeDtypeStruct(shape, dtype)


def _dot(a, b):
    return jnp.dot(a, b, preferred_element_type=F32)


def _dot_nt(a, b):
    return lax.dot_general(a, b, (((1,), (1,)), ((), ())), preferred_element_type=F32)


def _dot_tn(a, b):
    return lax.dot_general(a, b, (((0,), (0,)), ((), ())), preferred_element_type=F32)


def _sigmoid(x):
    return 1.0 / (1.0 + jnp.exp(-x))


def _rows(tr, width):
    return pl.BlockSpec((tr, width), lambda i: (i, 0))


def _vec(width):
    return pl.BlockSpec((1, width), lambda i: (0, 0))


def _acc(ref, val, first):
    @pl.when(first)
    def _():
        ref[...] = val

    @pl.when(jnp.logical_not(first))
    def _():
        ref[...] += val


def _sub_rows(tr, fine, width):
    return pl.BlockSpec((fine, tr // fine, width), lambda i: (0, i, 0))


def _regroup_matrix(tr, groups):
    a = lax.broadcasted_iota(jnp.int32, (tr, tr), 0)
    b = lax.broadcasted_iota(jnp.int32, (tr, tr), 1)
    return (b == (a % groups) * (tr // groups) + a // groups).astype(BF16)


def _regroup(m, v):
    if v.dtype == BF16:
        return _dot(m, v)
    out = None
    for _ in range(3):
        part = v.astype(BF16)
        v = v - part.astype(F32)
        out = _dot(m, part) if out is None else out + _dot(m, part)
    return out


def _mm_nn(a, b3, name, tm=1024, out_dtype=F32):
    m, k = a.shape
    s, _, n = b3.shape

    def body(a_ref, b_ref, o_ref):
        o_ref[...] = _dot(a_ref[...], b_ref[0]).astype(out_dtype)

    return _call(
        body, name=name, grid=(s, m // tm),
        in_specs=[pl.BlockSpec((tm, k), lambda j, i: (i, 0)), pl.BlockSpec((1, k, n), lambda j, i: (j, 0, 0))],
        out_specs=pl.BlockSpec((tm, n), lambda j, i: (i, j)), out_shape=_sds((m, s * n), out_dtype))(a, b3)


def _mm_own_shard(a, w, shard, s, name, tm=1024):
    m, k = a.shape
    n = w.shape[1]

    def body(s_ref, a_ref, w_ref, o_ref):
        o_ref[...] = _dot(a_ref[...], w_ref[...].astype(BF16)).astype(BF16)

    return pl.pallas_call(
        body, name=name, out_shape=_sds((m, s * n), BF16),
        grid_spec=pltpu.PrefetchScalarGridSpec(
            num_scalar_prefetch=1, grid=(m // tm,),
            in_specs=[pl.BlockSpec((tm, k), lambda i, sh: (i, 0)), pl.BlockSpec((k, n), lambda i, sh: (0, 0))],
            out_specs=pl.BlockSpec((tm, n), lambda i, sh: (i, sh[0]))),
        compiler_params=pltpu.CompilerParams(dimension_semantics=("arbitrary",), vmem_limit_bytes=VMEM_LIMIT),
    )(shard.reshape(1).astype(jnp.int32), a, w)


def _mm_other_shards(a, b3, partial, shard, name, tm=1024):
    m, k = a.shape
    s, _, n = b3.shape
    which = lambda j, sh: (sh[0] + 1 + j) % s

    def body(s_ref, a_ref, b_ref, p_ref, o_ref):
        o_ref[...] = _dot(a_ref[...], b_ref[0]).astype(BF16)

    return pl.pallas_call(
        body, name=name, out_shape=_sds(partial.shape, BF16),
        grid_spec=pltpu.PrefetchScalarGridSpec(
            num_scalar_prefetch=1, grid=(s - 1, m // tm),
            in_specs=[pl.BlockSpec((tm, k), lambda j, i, sh: (i, 0)),
                      pl.BlockSpec((1, k, n), lambda j, i, sh: (which(j, sh), 0, 0)),
                      pl.BlockSpec(memory_space=pl.ANY)],
            out_specs=pl.BlockSpec((tm, n), lambda j, i, sh: (i, which(j, sh)))),
        input_output_aliases={3: 0},
        compiler_params=pltpu.CompilerParams(dimension_semantics=("arbitrary",) * 2, vmem_limit_bytes=VMEM_LIMIT),
    )(shard.reshape(1).astype(jnp.int32), a, b3, partial)


def _mm_tn(a, dy, s, name, tm, tk, group=1):
    m, k = a.shape
    n = dy.shape[1] // s
    steps = m // tm

    def body(a_ref, dy_ref, o_ref, ob_ref):
        p = _dot_tn(a_ref[...], dy_ref[...])
        for g in range(group):
            pg = p[:, g * n:(g + 1) * n]
            if steps == 1:
                o_ref[g] = pg
                ob_ref[g] = pg.astype(BF16)
            else:
                _acc(o_ref.at[g], pg, pl.program_id(2) == 0)
        if steps > 1:
            @pl.when(pl.program_id(2) == steps - 1)
            def _():
                ob_ref[...] = o_ref[...].astype(BF16)

    out = pl.BlockSpec((group, tk, n), lambda kk, j, i: (j, kk, 0))
    return _call(
        body, name=name, grid=(k // tk, s // group, steps),
        in_specs=[pl.BlockSpec((tm, tk), lambda kk, j, i: (i, kk)),
                  pl.BlockSpec((tm, group * n), lambda kk, j, i: (i, j))],
        out_specs=[out, out], out_shape=[_sds((s, k, n)), _sds((s, k, n), BF16)])(a, dy)


def _rms(x):
    return lax.rsqrt(jnp.mean(x * x, axis=-1, keepdims=True) + RMS_EPS)


def _rms_bwd(dxh, xh, r):
    return r * (dxh - xh * jnp.mean(dxh * xh, axis=-1, keepdims=True))


def _norm_mod(x, g, scale, shift, name, tr=512):
    t = x.shape[0]

    def body(x_ref, g_ref, sc_ref, sh_ref, h_ref):
        xv = x_ref[...]
        n = xv * _rms(xv) * g_ref[...]
        h_ref[...] = (n * (1.0 + sc_ref[...]) + sh_ref[...]).astype(BF16)

    return _call(body, name=name, grid=(t // tr,),
                 in_specs=[_rows(tr, D_MODEL), _vec(D_MODEL), _vec(D_MODEL), _vec(D_MODEL)],
                 out_specs=_rows(tr, D_MODEL), out_shape=_sds((t, D_MODEL), BF16))(x, g, scale, shift)


def _out_resid_norm_mod(x, mixin, w_out, gate, g, scale, shift, name, tr=512):
    t = x.shape[0]

    def body(x_ref, mi_ref, w_ref, gt_ref, g_ref, sc_ref, sh_ref, m_ref, x2_ref, h_ref):
        mix = _dot(mi_ref[...], w_ref[0])
        m_ref[...] = mix
        x2 = x_ref[...] + gt_ref[...] * mix
        x2_ref[...] = x2
        n = x2 * _rms(x2) * g_ref[...]
        h_ref[...] = (n * (1.0 + sc_ref[...]) + sh_ref[...]).astype(BF16)

    row = _rows(tr, D_MODEL)
    return _call(body, name=name, grid=(t // tr,),
                 in_specs=[row, row, pl.BlockSpec(w_out.shape, lambda i: (0, 0, 0))] + [_vec(D_MODEL)] * 4,
                 out_specs=[row, row, row],
                 out_shape=[_sds((t, D_MODEL)), _sds((t, D_MODEL)), _sds((t, D_MODEL), BF16)])(
                     x, mixin, w_out, gate, g, scale, shift)


def _mm_gate_up(h, w_gu, name, tm=1024):
    m, k = h.shape
    n = w_gu.shape[2]

    def body(h_ref, wa_ref, wu_ref, da_ref, du_ref, o_ref):
        hv = h_ref[...]
        a = _dot(hv, wa_ref[0])
        u = _dot(hv, wu_ref[0])
        sg = _sigmoid(a)
        silu = a * sg
        da_ref[...] = (u * sg * (1.0 + a * (1.0 - sg))).astype(BF16)
        du_ref[...] = silu.astype(BF16)
        o_ref[...] = (silu * u).astype(BF16)

    out = pl.BlockSpec((tm, n), lambda j, i: (i, j))
    return _call(body, name=name, grid=(2, m // tm),
                 in_specs=[pl.BlockSpec((tm, k), lambda j, i: (i, 0)), pl.BlockSpec((1, k, n), lambda j, i: (j, 0, 0)),
                           pl.BlockSpec((1, k, n), lambda j, i: (j + 2, 0, 0))],
                 out_specs=[out, out, out], out_shape=[_sds((m, 2 * n), BF16)] * 3)(h, w_gu, w_gu)


def _mm_down_dx(dffn, w_down, act_da, act_du, name, tm=512):
    m = dffn.shape[0]
    _, k, n = w_down.shape

    def body(d_ref, w_ref, da_ref, du_ref, o_ref):
        dact = _dot_nt(d_ref[...], w_ref[0])
        o_ref[:, :k] = (dact * da_ref[...].astype(F32)).astype(BF16)
        o_ref[:, k:] = (dact * du_ref[...].astype(F32)).astype(BF16)

    return _call(body, name=name, grid=(m // tm,),
                 in_specs=[_rows(tm, n), pl.BlockSpec((1, k, n), lambda i: (0, 0, 0)), _rows(tm, k), _rows(tm, k)],
                 out_specs=_rows(tm, 2 * k), out_shape=_sds((m, 2 * k), BF16))(dffn, w_down, act_da, act_du)


def _down_loss(x2, act, w_down, gate, fg, tgt, name, tr=512):
    t = x2.shape[0]
    _, k, n = w_down.shape

    def body(x_ref, a_ref, w_ref, gt_ref, fg_ref, t_ref, dx_ref, df_ref, l_ref, dfg_ref, dgt_ref):
        first = pl.program_id(0) == 0
        ffn_v = _dot(a_ref[...], w_ref[0])
        x3 = x_ref[...] + gt_ref[...] * ffn_v
        r = _rms(x3)
        xh = x3 * r
        err = xh * fg_ref[...] - t_ref[...]
        dy = err * (1.0 / D_MODEL)
        dx3 = _rms_bwd(dy * fg_ref[...], xh, r)
        dx_ref[...] = dx3
        df_ref[...] = (dx3 * gt_ref[...]).astype(BF16)
        _acc(l_ref, jnp.sum(err * err, axis=0, keepdims=True), first)
        _acc(dfg_ref, jnp.sum(dy * xh, axis=0, keepdims=True), first)
        _acc(dgt_ref, jnp.sum(dx3 * ffn_v, axis=0, keepdims=True), first)

    row, vec = _rows(tr, D_MODEL), _vec(D_MODEL)
    return _call(body, name=name, grid=(t // tr,),
                 in_specs=[row, _rows(tr, k), pl.BlockSpec((1, k, n), lambda i: (0, 0, 0)), vec, vec, row],
                 out_specs=[row, row, vec, vec, vec],
                 out_shape=[_sds((t, D_MODEL)), _sds((t, D_MODEL), BF16)] + [_sds((1, D_MODEL))] * 3)(
                     x2, act, w_down, gate, fg, tgt)


def _norm_mod_bwd(dh, x, g, scale, dres, name, gate=None, mix=None, w=None, tr=512):
    t = x.shape[0]
    below = gate is not None

    def body(*refs):
        if w is not None:
            w_ref, w_full, sem, refs = refs[1], refs[-2], refs[-1], refs[:1] + refs[2:-2]

            @pl.when(pl.program_id(0) == 0)
            def _():
                n = w.shape[2]
                copies = [pltpu.make_async_copy(w_ref.at[j], w_full.at[:, pl.ds(j * n, n)], sem.at[j])
                          for j in range(w.shape[0])]
                for cp in copies:
                    cp.start()
                for cp in copies:
                    cp.wait()

        if below:
            dh_ref, x_ref, g_ref, sc_ref, dr_ref, gt_ref, m_ref, dx_ref, dsh_ref, dsc_ref, dg_ref, dgt_ref, dm_ref = refs
        else:
            dh_ref, x_ref, g_ref, sc_ref, dr_ref, dx_ref, dsh_ref, dsc_ref, dg_ref = refs
        first = pl.program_id(0) == 0
        xv = x_ref[...]
        if w is None:
            dhv = dh_ref[...].astype(F32)
        else:
            dhv = _dot_nt(dh_ref[...], w_full[...])
        r = _rms(xv)
        xh = xv * r
        dn = dhv * (1.0 + sc_ref[...])
        dx = dr_ref[...] + _rms_bwd(dn * g_ref[...], xh, r)
        dx_ref[...] = dx
        _acc(dsh_ref, jnp.sum(dhv, axis=0, keepdims=True), first)
        _acc(dsc_ref, jnp.sum(dhv * xh * g_ref[...], axis=0, keepdims=True), first)
        _acc(dg_ref, jnp.sum(dn * xh, axis=0, keepdims=True), first)
        if below:
            _acc(dgt_ref, jnp.sum(dx * m_ref[...], axis=0, keepdims=True), first)
            dm_ref[...] = (dx * gt_ref[...]).astype(BF16)

    row, vec = _rows(tr, D_MODEL), _vec(D_MODEL)
    first_specs = [row] if w is None else [_rows(tr, dh.shape[1]), pl.BlockSpec(memory_space=pl.ANY)]
    scratch = [] if w is None else [pltpu.VMEM((w.shape[1], dh.shape[1]), BF16), pltpu.SemaphoreType.DMA((w.shape[0],))]
    in_specs = first_specs + [row, vec, vec, row] + ([vec, row] if below else [])
    out_specs = [row, vec, vec, vec] + ([vec, row] if below else [])
    out_shape = [_sds((t, D_MODEL))] + [_sds((1, D_MODEL))] * 3 + ([_sds((1, D_MODEL)), _sds((t, D_MODEL), BF16)] if below else [])
    args = ((dh,) if w is None else (dh, w)) + (x, g, scale, dres) + ((gate, mix) if below else ())
    return _call(body, name=name, grid=(t // tr,), in_specs=in_specs, out_specs=out_specs, out_shape=out_shape,
                 scratch_shapes=scratch)(*args)


def _mix_in_bwd(dmix, w_out, o_hg, proj, att, og, ag, fine, name, tr=512):
    t = o_hg.shape[0]

    def body(dy_ref, w_ref, o_ref, g_ref, a_ref, og_ref, ag_ref,
             do_ref, dg_ref, da_ref, dd_ref, das_ref, dds_ref, dog_ref, dag_ref, to_sub):
        first = pl.program_id(0) == 0

        @pl.when(first)
        def _():
            to_sub[...] = _regroup_matrix(tr, tr // fine)

        dmi = _dot_nt(dy_ref[...], w_ref[0])
        dog = jnp.zeros((1, HG_HEAD), F32)
        for h in range(HG_WIDTH // HG_HEAD):
            sl = slice(h * HG_HEAD, (h + 1) * HG_HEAD)
            oh = o_ref[:, sl].astype(F32)
            gv = g_ref[:, sl].astype(F32)
            dv = dmi[:, sl]
            r = _rms(oh)
            xh = oh * r
            sg = _sigmoid(gv)
            dno = dv * gv * sg
            dg_ref[:, sl] = (dv * xh * og_ref[...] * sg * (1.0 + gv * (1.0 - sg))).astype(BF16)
            dog = dog + jnp.sum(dno * xh, axis=0, keepdims=True)
            do_ref[:, sl] = _rms_bwd(dno * og_ref[...], xh, r).astype(BF16)
        _acc(dog_ref, dog, first)
        av = a_ref[...]
        dav = dmi[:, HG_WIDTH:]
        r = _rms(av)
        xa = av * r
        _acc(dag_ref, jnp.sum(dav * xa, axis=0, keepdims=True), first)
        datt = _rms_bwd(dav * ag_ref[...], xa, r)
        da_ref[...] = datt.astype(BF16)
        das_ref[...] = _regroup(to_sub[...], datt.astype(BF16)).astype(BF16).reshape(das_ref.shape)
        prod = datt * av
        lane = lax.broadcasted_iota(jnp.int32, (1, 128), 1)
        dd = jnp.zeros((tr, 128), F32)
        for hp in range(ATT_HEADS // 2):
            pp = prod[:, hp * 128:(hp + 1) * 128]
            lo = jnp.sum(jnp.where(lane < 64, pp, 0.0), axis=-1, keepdims=True)
            hi = jnp.sum(jnp.where(lane >= 64, pp, 0.0), axis=-1, keepdims=True)
            dd = jnp.where(lane == 2 * hp, lo, dd)
            dd = jnp.where(lane == 2 * hp + 1, hi, dd)
        dd_ref[...] = dd
        dds_ref[...] = _regroup(to_sub[...], dd).reshape(dds_ref.shape)

    half = _rows(tr, HG_WIDTH)
    out = _call(body, name=name, grid=(t // tr,),
                in_specs=[_rows(tr, D_MODEL), pl.BlockSpec(w_out.shape, lambda i: (0, 0, 0)), half,
                          pl.BlockSpec((tr, HG_WIDTH), lambda i: (i, 3)), half, _vec(HG_HEAD), _vec(ATT_WIDTH)],
                out_specs=[half, pl.BlockSpec((tr, HG_WIDTH), lambda i: (i, 3)), half, _rows(tr, 128),
                           _sub_rows(tr, fine, ATT_WIDTH), _sub_rows(tr, fine, 128), _vec(HG_HEAD), _vec(ATT_WIDTH)],
                out_shape=[_sds((t, HG_WIDTH), BF16), _sds((t, IN_WIDTH), BF16), _sds((t, HG_WIDTH), BF16),
                           _sds((t, 128)), _sds((fine, t // fine, ATT_WIDTH), BF16),
                           _sds((fine, t // fine, 128)), _sds((1, HG_HEAD)), _sds((1, ATT_WIDTH))],
                scratch_shapes=[pltpu.VMEM((tr, tr), BF16)])(dmix, w_out, o_hg, proj, att, og, ag)
    out = list(out)
    return out[:4] + [out[4].reshape(t, ATT_WIDTH), out[5].reshape(t, 128)] + out[6:]


def _dproj(dproj, dqkvs, name, tr=1024):
    t = dproj.shape[0]
    nbr = len(dqkvs)
    first = IN_WIDTH // ATT_WIDTH - 3

    def body(*refs):
        refs[-1][...] = sum(r[...].astype(F32) for r in refs[:nbr]).astype(BF16)

    return _call(body, name=name, grid=(t // tr, 3),
                 in_specs=[pl.BlockSpec((tr, ATT_WIDTH), lambda i, j: (i, j))] * nbr + [pl.BlockSpec(memory_space=pl.ANY)],
                 out_specs=pl.BlockSpec((tr, ATT_WIDTH), lambda i, j: (i, first + j)),
                 out_shape=_sds(dproj.shape, BF16), aliases={nbr: 0})(*dqkvs, dproj)


def _chunk_tri(upper):
    row = lax.broadcasted_iota(jnp.int32, (HG_GROUP, HG_CHUNK, HG_CHUNK), 1)
    col = lax.broadcasted_iota(jnp.int32, (HG_GROUP, HG_CHUNK, HG_CHUNK), 2)
    return (row <= col if upper else row >= col).astype(BF16)


def _chunk_cumsum(x, tri):
    x3 = x.reshape(HG_GROUP, HG_CHUNK, x.shape[1])
    dims = (((2,), (1,)), ((0,), (0,)))
    out = None
    for _ in range(3):
        part = x3.astype(BF16)
        x3 = x3 - part.astype(F32)
        term = lax.dot_general(tri, part, dims, preferred_element_type=F32)
        out = term if out is None else out + term
    return out.reshape(x.shape)


def _hg_gates(f_raw, q_raw, lb, tri):
    sg = _sigmoid(f_raw)
    f = lb + (1.0 - lb) * sg
    k = 1.0 - f
    b = _chunk_cumsum(jnp.log(f), tri)
    sq = _sigmoid(q_raw)
    return sg, f, k, b, sq


def _hg_masks(rows):
    row = lax.broadcasted_iota(jnp.int32, (rows, rows), 0)
    col = lax.broadcasted_iota(jnp.int32, (rows, rows), 1)
    same = (row // HG_CHUNK) == (col // HG_CHUNK)
    return jnp.logical_and(row >= col, same), jnp.logical_and(row <= col, same)


def _per_chunk(rows_of):
    return jnp.concatenate([jnp.broadcast_to(r, (HG_CHUNK, r.shape[1])) for r in rows_of], axis=0)


def _hgrn_fwd(proj, lb_logits, name):
    t = proj.shape[0]
    nc = t // HG_CHUNK
    nh = HG_WIDTH // HG_HEAD
    rows = HG_GROUP * HG_CHUNK

    def body(q_ref, f_ref, i_ref, lg_ref, o_ref, st_ref, s_scr):
        @pl.when(pl.program_id(0) == 0)
        def _():
            s_scr[...] = jnp.zeros_like(s_scr)

        lg = lg_ref[...]
        lb_all = _sigmoid(lg[0:1] - lg[1:2])
        causal, _ = _hg_masks(rows)
        tri = _chunk_tri(False)
        for h in range(nh):
            sl = slice(h * HG_HEAD, (h + 1) * HG_HEAD)
            q_raw = q_ref[:, sl].astype(F32)
            _, _, k, b, sq = _hg_gates(f_ref[:, sl].astype(F32), q_raw, lb_all[:, sl], tri)
            v = i_ref[:, sl].astype(BF16)
            gls = [b[(g + 1) * HG_CHUNK - 1:(g + 1) * HG_CHUNK] for g in range(HG_GROUP)]
            bm = _per_chunk([b[g * HG_CHUNK + HG_CHUNK // 2 - 1:g * HG_CHUNK + HG_CHUNK // 2] for g in range(HG_GROUP)])
            qd = (q_raw * sq * jnp.exp(b)).astype(BF16)
            qm = (q_raw * sq * jnp.exp(b - bm)).astype(BF16)
            km = (k * jnp.exp(bm - b)).astype(BF16)
            ke = (k * jnp.exp(_per_chunk(gls) - b)).astype(BF16)
            a = jnp.where(causal, _dot_nt(qm, km), 0.0).astype(BF16)
            o_intra = _dot(a, v)
            st = s_scr[h]
            o_inter = []
            for g in range(HG_GROUP):
                rs = slice(g * HG_CHUNK, (g + 1) * HG_CHUNK)
                st_ref[g, sl, :] = st
                o_inter.append(_dot_nt(qd[rs], st.astype(BF16)))
                st = st * jnp.exp(gls[g]) + _dot_tn(v[rs], ke[rs])
            s_scr[h] = st
            o_ref[:, sl] = (o_intra + jnp.concatenate(o_inter, axis=0)).astype(BF16)

    blk = lambda j: pl.BlockSpec((rows, HG_WIDTH), lambda c: (c, j))
    return _call(body, name=name, grid=(nc // HG_GROUP,),
                 in_specs=[blk(0), blk(1), blk(2), pl.BlockSpec((2, HG_WIDTH), lambda c: (0, 0))],
                 out_specs=[blk(0), pl.BlockSpec((HG_GROUP, HG_WIDTH, HG_HEAD), lambda c: (c, 0, 0))],
                 out_shape=[_sds((t, HG_WIDTH), BF16), _sds((nc, HG_WIDTH, HG_HEAD))],
                 scratch_shapes=[pltpu.VMEM((nh, HG_HEAD, HG_HEAD), F32)])(proj, proj, proj, lb_logits)


def _hgrn_bwd(proj, lb_logits, states, do, dproj, name):
    t = proj.shape[0]
    ng = t // (HG_GROUP * HG_CHUNK)
    nh = HG_WIDTH // HG_HEAD
    rows = HG_GROUP * HG_CHUNK

    def body(q_ref, f_ref, i_ref, lg_ref, st_ref, do_ref, _, d_ref, dlb_ref, ds_scr):
        first = pl.program_id(0) == 0

        @pl.when(first)
        def _():
            ds_scr[...] = jnp.zeros_like(ds_scr)

        lg = lg_ref[...]
        lb_all = _sigmoid(lg[0:1] - lg[1:2])
        causal, _ = _hg_masks(rows)
        tri = _chunk_tri(False)
        tri_t = _chunk_tri(True)
        dlb = []
        for h in range(nh):
            sl = slice(h * HG_HEAD, (h + 1) * HG_HEAD)
            q_raw = q_ref[:, sl].astype(F32)
            lb = lb_all[:, sl]
            sg, f, k, b, sq = _hg_gates(f_ref[:, sl].astype(F32), q_raw, lb, tri)
            v = i_ref[:, sl].astype(BF16)
            gls = [b[(g + 1) * HG_CHUNK - 1:(g + 1) * HG_CHUNK] for g in range(HG_GROUP)]
            bm = _per_chunk([b[g * HG_CHUNK + HG_CHUNK // 2 - 1:g * HG_CHUNK + HG_CHUNK // 2] for g in range(HG_GROUP)])
            eb = jnp.exp(b)
            ebm = jnp.exp(b - bm)
            emb = jnp.exp(bm - b)
            egb = jnp.exp(_per_chunk(gls) - b)
            ke = k * egb
            qd_b, qm_b = (q_raw * sq * eb).astype(BF16), (q_raw * sq * ebm).astype(BF16)
            km_b, ke_b = (k * emb).astype(BF16), ke.astype(BF16)
            dov = do_ref[:, sl].astype(BF16)
            a = jnp.where(causal, _dot_nt(qm_b, km_b), 0.0).astype(BF16)
            da = jnp.where(causal, _dot_nt(dov, v), 0.0).astype(BF16)
            dkm = _dot_tn(da, qm_b)
            dst = ds_scr[h]
            dqd_s, dv_s, dke_s, dgl_s = [None] * HG_GROUP, [None] * HG_GROUP, [None] * HG_GROUP, [None] * HG_GROUP
            for g in reversed(range(HG_GROUP)):
                rs = slice(g * HG_CHUNK, (g + 1) * HG_CHUNK)
                st = st_ref[g, sl, :]
                dst_b = dst.astype(BF16)
                egl = jnp.exp(gls[g])
                dqd_s[g] = _dot(dov[rs], st.astype(BF16))
                dv_s[g] = _dot_nt(ke_b[rs], dst_b)
                dke_s[g] = _dot(v[rs], dst_b)
                dgl_s[g] = jnp.sum(dst * st, axis=0, keepdims=True) * egl
                dst = _dot_tn(dov[rs], qd_b[rs]) + dst * egl
            ds_scr[h] = dst
            dqm = _dot(da, km_b)
            dqd = jnp.concatenate(dqd_s, axis=0)
            dv = _dot_tn(a, dov) + jnp.concatenate(dv_s, axis=0)
            dke = jnp.concatenate(dke_s, axis=0)
            t1 = dke * ke
            db = dqm * qm_b.astype(F32) - dkm * km_b.astype(F32) + dqd * qd_b.astype(F32) - t1
            dgl = _per_chunk([dgl_s[g] + jnp.sum(t1[g * HG_CHUNK:(g + 1) * HG_CHUNK], axis=0, keepdims=True)
                              for g in range(HG_GROUP)])
            dlf = _chunk_cumsum(db, tri_t) + dgl
            df = dlf / f - (dkm * emb + dke * egb)
            d_ref[:, sl] = ((dqm * ebm + dqd * eb) * sq * (1.0 + q_raw * (1.0 - sq))).astype(BF16)
            d_ref[:, HG_WIDTH + h * HG_HEAD:HG_WIDTH + (h + 1) * HG_HEAD] = (
                df * (1.0 - lb) * sg * (1.0 - sg)).astype(BF16)
            d_ref[:, 2 * HG_WIDTH + h * HG_HEAD:2 * HG_WIDTH + (h + 1) * HG_HEAD] = dv.astype(BF16)
            dlb.append(jnp.sum(df * (1.0 - sg), axis=0, keepdims=True))
        _acc(dlb_ref, jnp.concatenate(dlb, axis=1), first)

    rev = lambda j: pl.BlockSpec((rows, HG_WIDTH), lambda c: (ng - 1 - c, j))
    return _call(body, name=name, grid=(ng,),
                 in_specs=[rev(0), rev(1), rev(2), pl.BlockSpec((2, HG_WIDTH), lambda c: (0, 0)),
                           pl.BlockSpec((HG_GROUP, HG_WIDTH, HG_HEAD), lambda c: (ng - 1 - c, 0, 0)), rev(0),
                           pl.BlockSpec(memory_space=pl.ANY)],
                 out_specs=[pl.BlockSpec((rows, 3 * HG_WIDTH), lambda c: (ng - 1 - c, 0)), _vec(HG_WIDTH)],
                 out_shape=[_sds(dproj.shape, BF16), _sds((1, HG_WIDTH))], aliases={6: 0},
                 scratch_shapes=[pltpu.VMEM((nh, HG_HEAD, HG_HEAD), F32)])(
                     proj, proj, proj, lb_logits, states, do, dproj)


def _to_sub(a, dil):
    t, w = a.shape
    return a if dil == 1 else a.reshape(t // dil, dil, w).transpose(1, 0, 2).reshape(t, w)


def _from_sub(a, dil):
    t, w = a.shape
    return a if dil == 1 else a.reshape(dil, t // dil, w).transpose(1, 0, 2).reshape(t, w)


def _att_mask(has_prev, seg):
    def place(v):
        v = v % ATT_BLOCK
        return v if seg == 1 else seg * (v % (ATT_BLOCK // seg)) + v // (ATT_BLOCK // seg)

    row = lax.broadcasted_iota(jnp.int32, (2 * ATT_BLOCK, 2 * ATT_BLOCK), 0)
    col = lax.broadcasted_iota(jnp.int32, (2 * ATT_BLOCK, 2 * ATT_BLOCK), 1)
    qi, kj = place(row), place(col)
    prev = jnp.logical_and(jnp.logical_and(col < ATT_BLOCK, kj >= qi), has_prev)
    cur = jnp.logical_and(col >= ATT_BLOCK, kj <= qi)
    return jnp.logical_or(prev, cur), lax.broadcasted_iota(jnp.int32, (1, 128), 1)


def _get(ref, sl):
    if len(ref.shape) == 2:
        return ref[:, sl]
    v = ref[:, :, sl]
    return v.reshape(ATT_BLOCK, v.shape[2])


def _put(ref, sl, val):
    if len(ref.shape) == 2:
        ref[:, sl] = val
    else:
        ref[:, :, sl] = val.reshape(ref.shape[0], ref.shape[1], val.shape[1])


def _att_spec(nb, dil, seg, width, col, back):
    bps = nb // dil

    def plain(n):
        return jnp.clip(n - back, 0, nb - 1), col

    def segmented(n):
        m = jnp.clip(n - back, 0, nb - 1)
        return 0, m // bps, m % bps, 0, col

    if seg == 1:
        return pl.BlockSpec((ATT_BLOCK, width), plain)
    return pl.BlockSpec((seg, None, None, ATT_BLOCK // seg, width), segmented)


def _att_shape(nb, dil, seg, width):
    t = nb * ATT_BLOCK
    return (t, width) if seg == 1 else (seg, dil, nb // dil, ATT_BLOCK // seg, width)


def _att_view(a, nb, dil, seg):
    return a.reshape(_att_shape(nb, dil, seg, a.shape[1]))


def _attn_fwd_block(q_ref, k_ref, v_ref, o_ref, l_ref, prev, has_prev, seg, lane0):
    w = ATT_WIDTH
    mask, lane = _att_mask(has_prev, seg)
    lo = lane < 64
    nq = ATT_BLOCK
    lse_all = jnp.zeros((nq, 128), F32)
    for hp in range(ATT_HEADS // 2):
        sl = slice(hp * 128, (hp + 1) * 128)
        q2 = _get(q_ref, sl)
        zero = jnp.zeros_like(q2)
        q2 = q2 * 0.125
        qs = jnp.concatenate([jnp.where(lo, q2, zero), jnp.where(lo, zero, q2)], axis=0)
        sv = slice(w + hp * 128, w + (hp + 1) * 128)
        kc, vc = _get(k_ref, sl), _get(v_ref, sl)
        kk = jnp.concatenate([prev[:, sl], kc], axis=0)
        vv = jnp.concatenate([prev[:, sv], vc], axis=0)
        prev[:, sl] = kc
        prev[:, sv] = vc
        s = jnp.where(mask, _dot_nt(qs, kk), NEG)
        mx = jnp.max(s, axis=-1, keepdims=True)
        p = jnp.exp(s - mx)
        l = jnp.sum(p, axis=-1, keepdims=True)
        o = _dot(p.astype(BF16), vv) * (1.0 / l)
        _put(o_ref, sl, jnp.where(lo, o[:nq], o[nq:]).astype(BF16))
        lse = mx + jnp.log(l)
        lse_all = jnp.where(lane == lane0 + 2 * hp, lse[:nq], lse_all)
        lse_all = jnp.where(lane == lane0 + 2 * hp + 1, lse[nq:], lse_all)
    _put(l_ref, slice(None), lse_all)


def _attn_fwd(branches, name):
    t = branches[0][0].shape[0]
    nb = t // ATT_BLOCK
    nbr = len(branches)

    def body(*refs):
        ins, outs, prevs = refs[:3 * nbr], refs[3 * nbr:5 * nbr], refs[5 * nbr:]
        n = pl.program_id(0)

        @pl.when(n == 0)
        def _():
            for prev in prevs:
                prev[...] = jnp.zeros_like(prev)

        for i, (_, dil, seg) in enumerate(branches):
            _attn_fwd_block(*ins[3 * i:3 * i + 3], *outs[2 * i:2 * i + 2], prevs[i],
                            (n % (nb // dil)) != 0, seg, ATT_HEADS * i)

    in_specs, args, out_specs, out_shape = [], [], [], []
    for qkv, dil, seg in branches:
        c0 = qkv.shape[1] // ATT_WIDTH - 3
        in_specs += [_att_spec(nb, dil, seg, ATT_WIDTH, c0 + j, 0) for j in range(3)]
        args += [_att_view(qkv, nb, dil, seg)] * 3
        out_specs += [_att_spec(nb, dil, seg, ATT_WIDTH, 0, 0), _att_spec(nb, dil, seg, 128, 0, 0)]
        out_shape += [_sds(_att_shape(nb, dil, seg, ATT_WIDTH), BF16), _sds(_att_shape(nb, dil, seg, 128))]
    out = _call(body, name=name, grid=(nb,), in_specs=in_specs, out_specs=out_specs, out_shape=out_shape,
                scratch_shapes=[pltpu.VMEM((ATT_BLOCK, 2 * ATT_WIDTH), BF16)] * nbr)(*args)
    return [(out[2 * i].reshape(t, ATT_WIDTH), out[2 * i + 1].reshape(t, 128)) for i in range(nbr)]


def _combine_mix_in(os_, ls_, fine, o_hg, proj, og, ag, name, tr=512):
    t = os_[0].shape[0]
    nbr = len(os_)

    def body(*refs):
        o_refs, l_refs = refs[:nbr], refs[nbr:2 * nbr]
        oh_ref, g_ref, og_ref, ag_ref, a_ref, lt_ref, lts_ref, m_ref, to_natural, to_sub = refs[2 * nbr:]

        @pl.when(pl.program_id(0) == 0)
        def _():
            to_natural[...] = _regroup_matrix(tr, fine)
            to_sub[...] = _regroup_matrix(tr, tr // fine)

        lane = lax.broadcasted_iota(jnp.int32, (1, 128), 1)
        packed = l_refs[0][...] + _regroup(to_natural[...], sum(r[...] for r in l_refs[1:]).reshape(tr, 128))
        ls = [packed if i == 0 else pltpu.roll(packed, 128 - ATT_HEADS * i, 1) for i in range(nbr)]
        mx = functools.reduce(jnp.maximum, ls)
        tot = mx + jnp.log(sum(jnp.exp(l - mx) for l in ls))
        ws = [jnp.exp(l - tot) for l in ls]
        tot = jnp.where(lane < ATT_HEADS, tot, 0.0)
        lt_ref[...] = tot
        lts_ref[...] = _regroup(to_sub[...], tot).reshape(lts_ref.shape)
        o_vals = [o_refs[0]] + [_regroup(to_natural[...], r[...].reshape(tr, ATT_WIDTH)) for r in o_refs[1:]]
        pairs = []
        for hp in range(ATT_HEADS // 2):
            sl = slice(hp * 128, (hp + 1) * 128)
            acc = jnp.zeros((tr, 128), F32)
            for w, o in zip(ws, o_vals):
                wf = jnp.where(lane < 64, w[:, 2 * hp:2 * hp + 1], w[:, 2 * hp + 1:2 * hp + 2])
                acc = acc + wf * o[:, sl]
            pairs.append(acc)
        av = jnp.concatenate(pairs, axis=1)
        a_ref[...] = av
        m_ref[:, HG_WIDTH:] = (av * _rms(av) * ag_ref[...]).astype(BF16)
        for h in range(HG_WIDTH // HG_HEAD):
            sl = slice(h * HG_HEAD, (h + 1) * HG_HEAD)
            oh = oh_ref[:, sl].astype(F32)
            gv = g_ref[:, sl].astype(F32)
            m_ref[:, sl] = (oh * _rms(oh) * og_ref[...] * (gv * _sigmoid(gv))).astype(BF16)

    half = _rows(tr, ATT_WIDTH)
    sub = lambda a: a.reshape(fine, t // fine, a.shape[1])
    att, lse, lse_sub, mixin = _call(
        body, name=name, grid=(t // tr,),
        in_specs=[half] + [_sub_rows(tr, fine, ATT_WIDTH)] * (nbr - 1) + [_rows(tr, 128)] + [_sub_rows(tr, fine, 128)] * (nbr - 1)
        + [half, pl.BlockSpec((tr, HG_WIDTH), lambda i: (i, 3)), _vec(HG_HEAD), _vec(ATT_WIDTH)],
        out_specs=[half, _rows(tr, 128), _sub_rows(tr, fine, 128), _rows(tr, D_MODEL)],
        out_shape=[_sds((t, ATT_WIDTH)), _sds((t, 128)), _sds((fine, t // fine, 128)), _sds((t, D_MODEL), BF16)],
        scratch_shapes=[pltpu.VMEM((tr, tr), BF16)] * 2)(
            os_[0], *map(sub, os_[1:]), ls_[0], *map(sub, ls_[1:]), o_hg, proj, og, ag)
    return att, lse, lse_sub.reshape(t, 128), mixin


def _attn_bwd_block(q_ref, k_ref, v_ref, do_ref, l_ref, d_ref, out_ref, carry, prev, has_prev, seg):
    w = ATT_WIDTH
    nq = ATT_BLOCK
    mask, lane = _att_mask(has_prev, seg)
    lo = lane < 64
    lse, ddv = _get(l_ref, slice(None)), _get(d_ref, slice(None))
    for hp in range(ATT_HEADS // 2):
        sl = slice(hp * 128, (hp + 1) * 128)
        sk = slice(w + hp * 128, w + (hp + 1) * 128)
        sv = slice(2 * w + hp * 128, 2 * w + (hp + 1) * 128)
        q2, do2 = _get(q_ref, sl), _get(do_ref, sl)
        zero = jnp.zeros_like(q2)
        q2 = q2 * 0.125
        qs = jnp.concatenate([jnp.where(lo, q2, zero), jnp.where(lo, zero, q2)], axis=0)
        dos = jnp.concatenate([jnp.where(lo, do2, zero), jnp.where(lo, zero, do2)], axis=0)
        kc, vc = _get(k_ref, sl), _get(v_ref, sl)
        kk = jnp.concatenate([prev[:, sl], kc], axis=0)
        vv = jnp.concatenate([prev[:, sk], vc], axis=0)
        prev[:, sl] = kc
        prev[:, sk] = vc
        ls = jnp.concatenate([lse[:, 2 * hp:2 * hp + 1], lse[:, 2 * hp + 1:2 * hp + 2]], axis=0)
        dh = jnp.concatenate([ddv[:, 2 * hp:2 * hp + 1], ddv[:, 2 * hp + 1:2 * hp + 2]], axis=0)
        p = jnp.exp(jnp.where(mask, _dot_nt(qs, kk) - ls, NEG))
        ds = (p * (_dot_nt(dos, vv) - dh)).astype(BF16)
        dq = _dot(ds, kk) * 0.125
        dk = _dot_tn(ds, qs)
        dv = _dot_tn(p.astype(BF16), dos)
        _put(out_ref, sl, carry[:, sl].astype(BF16))
        _put(out_ref, sk, (carry[:, sk] + dk[:nq]).astype(BF16))
        _put(out_ref, sv, (carry[:, sv] + dv[:nq]).astype(BF16))
        carry[:, sl] = jnp.where(lo, dq[:nq], dq[nq:])
        carry[:, sk] = dk[nq:]
        carry[:, sv] = dv[nq:]


def _attn_bwd(branches, name):
    t = branches[0][0].shape[0]
    nb = t // ATT_BLOCK
    nbr = len(branches)
    w = ATT_WIDTH

    def body(*refs):
        ins, outs, carries, prevs = refs[:6 * nbr], refs[6 * nbr:7 * nbr], refs[7 * nbr:8 * nbr], refs[8 * nbr:]
        n = pl.program_id(0)

        @pl.when(n == 0)
        def _():
            for scratch in carries + prevs:
                scratch[...] = jnp.zeros_like(scratch)

        @pl.when(n < nb)
        def _():
            for i, branch in enumerate(branches):
                dil, seg = branch[4:]
                _attn_bwd_block(*ins[6 * i:6 * i + 6], outs[i], carries[i], prevs[i], (n % (nb // dil)) != 0, seg)

        @pl.when(n == nb)
        def _():
            for i in range(nbr):
                _put(outs[i], slice(None), carries[i][...].astype(BF16))

    in_specs, args, out_specs, out_shape = [], [], [], []
    for qkv, dout, lse, dd, dil, seg in branches:
        c0 = qkv.shape[1] // w - 3
        in_specs += [_att_spec(nb, dil, seg, w, c0 + j, 0) for j in range(3)]
        in_specs += [_att_spec(nb, dil, seg, w, 0, 0), _att_spec(nb, dil, seg, 128, 0, 0), _att_spec(nb, dil, seg, 128, 0, 0)]
        args += [_att_view(a, nb, dil, seg) for a in [qkv] * 3 + [dout, lse, dd]]
        out_specs += [_att_spec(nb, dil, seg, 3 * w, 0, 1)]
        out_shape += [_sds(_att_shape(nb, dil, seg, 3 * w), BF16)]
    out = _call(body, name=name, grid=(nb + 1,), in_specs=in_specs, out_specs=out_specs, out_shape=out_shape,
                scratch_shapes=[pltpu.VMEM((ATT_BLOCK, 3 * w), F32)] * nbr + [pltpu.VMEM((ATT_BLOCK, 2 * w), BF16)] * nbr)(*args)
    return [o.reshape(t, 3 * w) for o in out]


def _local_step(x, tgt, mod, norm1_g, lb_logits, og, ag, norm2_g, fg, get_w, put_g, late=lambda a: a, project=None):
    shift1, scale1, gate1, shift2, scale2, gate2 = [mod[:, i * D_MODEL:(i + 1) * D_MODEL] for i in range(6)]
    fg = fg.reshape(1, D_MODEL)

    h1 = _norm_mod(x, norm1_g, scale1, shift1, "norm_mod1")
    if project is None:
        w_in = get_w("w_in", h1)
        proj = _mm_nn(h1, w_in, "mm_in", out_dtype=BF16)
    else:
        proj, w_in = project(h1)
    o_hg, states = _hgrn_fwd(proj, lb_logits, "hgrn_fwd")
    fine = DILATIONS[-1]
    layouts = [(d, 1 if d == 1 else fine // d) for d in DILATIONS]
    qkv_fine = _to_sub(proj, fine)
    qkvs = [proj if d == 1 else qkv_fine for d in DILATIONS]
    natural = lambda a, d: a if d == 1 else _from_sub(a, fine)
    outs = _attn_fwd([(q, d, seg) for q, (d, seg) in zip(qkvs, layouts)], "attn_fwd")
    att, lse, lse_fine, mixin = _combine_mix_in([o for o, _ in outs], [l for _, l in outs], fine,
                                                o_hg, proj, og, ag, "attn_combine_mix_in")
    w_out = get_w("w_out", mixin)
    mix, x2, h2 = _out_resid_norm_mod(x, mixin, w_out, gate1, norm2_g, scale2, shift2, "mm_out_resid_norm_mod2")
    w_gu = get_w("w_gu", h2)
    a_ff, u_ff, act = _mm_gate_up(h2, w_gu, "mm_gu")
    w_down = get_w("w_down", act)
    dx3, dffn, loss_v, dfg, dgate2 = _down_loss(x2, act, w_down, gate2, fg, tgt, "mm_down_loss")

    dffn = put_g("w_down", *_mm_tn(act, dffn, 1, "mm_down_dw", tm=2048, tk=D_FF // 2), dffn)
    dau = _mm_down_dx(dffn, w_down, a_ff, u_ff, "mm_down_dx")
    dau = put_g("w_gu", *_mm_tn(h2, dau, N_SHARD, "mm_gu_dw", tm=x.shape[0], tk=512), dau)
    dx2, dshift2, dscale2, dg2, dgate1, dmix = _norm_mod_bwd(
        dau, x2, norm2_g, scale2, dx3, "mm_gu_dx_norm_bwd", gate=gate1, mix=mix, w=w_gu)
    dmix = put_g("w_out", *_mm_tn(mixin, dmix, 1, "mm_out_dw", tm=x.shape[0], tk=512), dmix)
    do_hg, dproj, datt, dd, datt_fine, dd_fine, dog, dag = _mix_in_bwd(
        dmix, w_out, o_hg, proj, att, og, ag, fine, "mm_out_dx_mix_in_bwd")
    datts = _attn_bwd([(q,) + ((datt, lse, dd) if d == 1 else (datt_fine, lse_fine, dd_fine)) + (d, seg)
                       for q, (d, seg) in zip(qkvs, layouts)], "attn_bwd")
    dproj, dlb = _hgrn_bwd(proj, lb_logits, states, do_hg, dproj, "hgrn_bwd")
    dproj = late(dproj)
    dproj = _dproj(dproj, [natural(a, d) for a, d in zip(datts, DILATIONS)], "dproj")
    dproj = put_g("w_in", *_mm_tn(h1, dproj, N_SHARD, "mm_in_dw", tm=x.shape[0], tk=512, group=2), dproj)
    dx, dshift1, dscale1, dg1 = _norm_mod_bwd(dproj, x, norm1_g, scale1, dx2, "mm_in_dx_norm_bwd", w=w_in)

    stats = jnp.concatenate([loss_v, dfg, dg2, dg1, dlb, dag, dog,
                             dshift1, dscale1, dgate1, dshift2, dscale2, dgate2], axis=1)
    return dx, stats


def _place():
    x, y, c = lax.axis_index("x"), lax.axis_index("y"), lax.axis_index("c")
    return x, y, c


def _chip_peers(x, y, c):
    return [(1 - x, y, c), (x, 1 - y, c), (1 - x, 1 - y, c)]


def _comm_call(body, name, n_in, out_shape, scratch_shapes):
    hbm = pl.BlockSpec(memory_space=pl.ANY)
    return pl.pallas_call(body, name=name, in_specs=[hbm] * n_in, out_specs=[hbm] * len(out_shape),
                          out_shape=out_shape, scratch_shapes=scratch_shapes)


_HBM = pl.BlockSpec(memory_space=pltpu.HBM)
_SEM = pl.BlockSpec(memory_space=pltpu.SEMAPHORE)
_EFFECT = pltpu.SideEffectType.DATAFLOW_SIDE_EFFECTING


def _exchange_copy(bufs, send, recv, j, peer, place, kind):
    x, y, c = place
    target = peer
    if kind == "gather":
        src = dst = bufs[0].at[2 * x + y]
    elif kind == "scatter":
        src, dst = bufs[0].at[2 * peer[0] + peer[1]], bufs[1].at[j]
    else:
        half = bufs[0].shape[1] // 2
        rows = pl.ds(c * half, half)
        if kind == "half":
            src = dst = bufs[0].at[2 * x + y, rows]
        else:
            src = dst = bufs[0].at[2 * peer[0] + peer[1], rows]
            target = (x, y, 1 - c)
    return pltpu.make_async_remote_copy(src_ref=src, dst_ref=dst, send_sem=send.at[j], recv_sem=recv.at[j],
                                        device_id=target, device_id_type=MESH)


def _exchange_start(groups, after, kind, name):
    sizes = [len(g) for g in groups]
    flat = [b for g in groups for b in g]
    ng, nb = len(groups), len(flat)

    def body(*refs):
        bufs, sems = refs[:nb], refs[nb + 1:nb + 1 + 2 * ng]
        x, y, c = _place()
        for j, peer in enumerate(_chip_peers(x, y, c)):
            at = 0
            for i, size in enumerate(sizes):
                _exchange_copy(bufs[at:at + size], sems[2 * i], sems[2 * i + 1], j, peer, (x, y, c), kind).start()
                at += size

    any_space = pl.BlockSpec(memory_space=pl.ANY)
    out = pl.pallas_call(
        body, name=name, in_specs=[_HBM] * nb + [any_space],
        out_specs=[_SEM] * (2 * ng) + [_HBM] * nb + [any_space],
        out_shape=[pltpu.SemaphoreType.DMA((3,))] * (2 * ng) + [pltpu.HBM(b.shape, b.dtype) for b in flat]
        + [_sds(after.shape, after.dtype)],
        input_output_aliases={i: 2 * ng + i for i in range(nb + 1)},
        compiler_params=pltpu.CompilerParams(has_side_effects=_EFFECT),
    )(*[pltpu.with_memory_space_constraint(b, pltpu.HBM) for b in flat], after)
    started, at = [], 2 * ng
    for i, size in enumerate(sizes):
        started.append((out[2 * i], out[2 * i + 1], tuple(out[at:at + size])))
        at += size
    return started, out[-1]


def _exchange_wait(started, after, kind, name):
    send, recv, bufs = started
    nb = len(bufs)

    def body(*refs):
        x, y, c = _place()
        for j, peer in enumerate(_chip_peers(x, y, c)):
            cp = _exchange_copy(refs[:nb], refs[nb], refs[nb + 1], j, peer, (x, y, c), kind)
            cp.wait_send()
            cp.wait_recv()

    return pl.pallas_call(
        body, name=name, in_specs=[_HBM] * nb + [_SEM, _SEM, pl.BlockSpec(memory_space=pl.ANY)],
        out_specs=[_HBM] * nb, out_shape=[pltpu.HBM(b.shape, b.dtype) for b in bufs],
        input_output_aliases={i: i for i in range(nb)},
        compiler_params=pltpu.CompilerParams(has_side_effects=_EFFECT),
    )(*bufs, send, recv, after)


def _sibling_copies(v_refs, l_refs, send, recv):
    x, y, c = _place()
    return [pltpu.make_async_remote_copy(src_ref=v, dst_ref=l, send_sem=send.at[a], recv_sem=recv.at[a],
                                         device_id=(x, y, 1 - c), device_id_type=MESH)
            for a, (v, l) in enumerate(zip(v_refs, l_refs))]


def _sibling_start(vs, after, name):
    vs = list(vs)
    n = len(vs)
    lands = [lax.empty(v.shape, v.dtype) for v in vs]

    def body(*refs):
        for cp in _sibling_copies(refs[:n], refs[n:2 * n], refs[2 * n + 1], refs[2 * n + 2]):
            cp.start()

    any_space = pl.BlockSpec(memory_space=pl.ANY)
    out = pl.pallas_call(
        body, name=name, in_specs=[_HBM] * (2 * n) + [any_space],
        out_specs=[_SEM, _SEM] + [_HBM] * (2 * n) + [any_space],
        out_shape=[pltpu.SemaphoreType.DMA((n,))] * 2 + [pltpu.HBM(b.shape, b.dtype) for b in vs + lands]
        + [_sds(after.shape, after.dtype)],
        input_output_aliases={i: 2 + i for i in range(2 * n + 1)},
        compiler_params=pltpu.CompilerParams(has_side_effects=_EFFECT),
    )(*[pltpu.with_memory_space_constraint(b, pltpu.HBM) for b in vs + lands], after)
    return (out[0], out[1], tuple(out[2:2 + n]), tuple(out[2 + n:2 + 2 * n])), out[-1]


def _sibling_wait(started, after, name):
    send, recv, vs, lands = started
    n = len(vs)

    def body(*refs):
        for cp in _sibling_copies(refs[:n], refs[n:2 * n], refs[2 * n], refs[2 * n + 1]):
            cp.wait_send()
            cp.wait_recv()

    out = pl.pallas_call(
        body, name=name, in_specs=[_HBM] * (2 * n) + [_SEM, _SEM, pl.BlockSpec(memory_space=pl.ANY)],
        out_specs=[_HBM] * (2 * n), out_shape=[pltpu.HBM(b.shape, b.dtype) for b in vs + lands],
        input_output_aliases={i: i for i in range(2 * n)},
        compiler_params=pltpu.CompilerParams(has_side_effects=_EFFECT),
    )(*vs, *lands, send, recv, after)
    return out[:n], out[n:]


def _swap_sibling(vs, name):
    n = len(vs)

    def body(*refs):
        v_refs, o_refs, (send, recv) = refs[:n], refs[n:2 * n], refs[2 * n:]
        x, y, c = _place()
        cps = [pltpu.make_async_remote_copy(
            src_ref=v_refs[a], dst_ref=o_refs[a], send_sem=send.at[a], recv_sem=recv.at[a],
            device_id=(x, y, 1 - c), device_id_type=MESH) for a in range(n)]
        for cp in cps:
            cp.start()
        for cp in cps:
            cp.wait()

    return _comm_call(body, name, n, [_sds(v.shape, v.dtype) for v in vs],
                      [pltpu.SemaphoreType.DMA((n,)), pltpu.SemaphoreType.DMA((n,))])(*vs)


def _everyone(x, y, c):
    return [(1 - x if k & 4 else x, 1 - y if k & 2 else y, 1 - c if k & 1 else c) for k in range(1, 8)]


def _all_gather_copies(land_ref, send, recv, arriving):
    x, y, c = _place()
    me = 4 * x + 2 * y + c
    return [pltpu.make_async_remote_copy(
        src_ref=land_ref.at[me], dst_ref=land_ref.at[4 * p[0] + 2 * p[1] + p[2] if arriving else me],
        send_sem=send.at[k], recv_sem=recv.at[k], device_id=p, device_id_type=MESH)
        for k, p in enumerate(_everyone(x, y, c))]


def _all_gather_start(v, name):
    x, y, c = _place()
    land = lax.dynamic_update_slice(lax.empty((8,) + v.shape, v.dtype), v[None], (4 * x + 2 * y + c, 0, 0))

    def body(land_ref, v_ref, send, recv, land_out, v_out):
        for cp in _all_gather_copies(land_ref, send, recv, False):
            cp.start()

    any_space = pl.BlockSpec(memory_space=pl.ANY)
    out = pl.pallas_call(
        body, name=name, in_specs=[_HBM, any_space], out_specs=[_SEM, _SEM, _HBM, any_space],
        out_shape=[pltpu.SemaphoreType.DMA((7,))] * 2 + [pltpu.HBM(land.shape, land.dtype), _sds(v.shape, v.dtype)],
        input_output_aliases={0: 2, 1: 3}, compiler_params=pltpu.CompilerParams(has_side_effects=_EFFECT),
    )(pltpu.with_memory_space_constraint(land, pltpu.HBM), v)
    return tuple(out[:3]), out[3]


def _all_gather_wait(started, after, name):
    send, recv, land = started

    def body(land_ref, send, recv, after_ref, land_out):
        for cp in _all_gather_copies(land_ref, send, recv, True):
            cp.wait_send()
            cp.wait_recv()

    return pl.pallas_call(
        body, name=name, in_specs=[_HBM, _SEM, _SEM, pl.BlockSpec(memory_space=pl.ANY)], out_specs=_HBM,
        out_shape=pltpu.HBM(land.shape, land.dtype), input_output_aliases={0: 0},
        compiler_params=pltpu.CompilerParams(has_side_effects=_EFFECT),
    )(land, send, recv, after)


def _cast_place(ws, shard, name):
    n = len(ws)

    def body(s_ref, *refs):
        for w_ref, o_ref in zip(refs[:n], refs[n:]):
            o_ref[0] = w_ref[...].astype(BF16)

    return pl.pallas_call(
        body, name=name, out_shape=[_sds((N_SHARD,) + w.shape, BF16) for w in ws],
        grid_spec=pltpu.PrefetchScalarGridSpec(
            num_scalar_prefetch=1, grid=(4,),
            in_specs=[pl.BlockSpec((w.shape[0] // 4, w.shape[1]), lambda i, s: (i, 0)) for w in ws],
            out_specs=[pl.BlockSpec((1, w.shape[0] // 4, w.shape[1]), lambda i, s: (s[0], i, 0)) for w in ws]),
        compiler_params=pltpu.CompilerParams(dimension_semantics=("arbitrary",), vmem_limit_bytes=VMEM_LIMIT),
    )(shard.reshape(1).astype(jnp.int32), *ws)


def _mod_rows(c8, w_ada, b_ada, name):
    n = w_ada.shape[1]

    def gather(src_ref, dst_ref, send, recv, loc, base):
        x, y, c = _place()
        me = 4 * x + 2 * y + c
        own = pltpu.make_async_copy(src_ref, dst_ref.at[me], loc)
        own.start()
        peers = _everyone(x, y, c)
        sends = [pltpu.make_async_remote_copy(src_ref=src_ref, dst_ref=dst_ref.at[me], send_sem=send.at[base + k],
                                              recv_sem=recv.at[base + k], device_id=p, device_id_type=MESH)
                 for k, p in enumerate(peers)]
        for cp in sends:
            cp.start()
        for k, p in enumerate(peers):
            pltpu.make_async_remote_copy(src_ref=src_ref, dst_ref=dst_ref.at[4 * p[0] + 2 * p[1] + p[2]],
                                         send_sem=send.at[base + k], recv_sem=recv.at[base + k], device_id=p,
                                         device_id_type=MESH).wait_recv()
        for cp in sends:
            cp.wait_send()
        own.wait()

    def body(c_ref, w_ref, b_ref, a_ref, parts_ref, c_all, part, send, recv, loc):
        gather(c_ref, c_all, send, recv, loc.at[0], 0)
        cv = jnp.max(c_all[...], axis=1)
        ca = cv * _sigmoid(cv)
        a_ref[...] = ca
        part[...] = jnp.dot(ca, w_ref[...], precision=lax.Precision.HIGHEST, preferred_element_type=F32) + b_ref[...]
        gather(part, parts_ref, send, recv, loc.at[1], 7)

    vmem = pl.BlockSpec(memory_space=pltpu.VMEM)
    return pl.pallas_call(
        body, name=name, in_specs=[vmem] * 3, out_specs=[vmem, vmem],
        out_shape=[_sds((8, D_MODEL)), _sds((8, 8, n))],
        scratch_shapes=[pltpu.VMEM((8, 8, D_MODEL), F32), pltpu.VMEM((8, n), F32), pltpu.SemaphoreType.DMA((14,)),
                        pltpu.SemaphoreType.DMA((14,)), pltpu.SemaphoreType.DMA((2,))],
        compiler_params=pltpu.CompilerParams(vmem_limit_bytes=VMEM_LIMIT))(c8, w_ada, b_ada)


def _sum_received(gs, shard, lands, name):
    n = len(gs)

    def body(s_ref, *refs):
        for g_ref, l_ref, o_ref in zip(refs[:n], refs[n:2 * n], refs[2 * n:]):
            o_ref[...] = ((g_ref[0] + l_ref[0].astype(F32)) + l_ref[1].astype(F32)) + l_ref[2].astype(F32)

    quarter = lambda g: (g.shape[1] // 4, g.shape[2])
    return pl.pallas_call(
        body, name=name, out_shape=[_sds(g.shape[1:]) for g in gs],
        grid_spec=pltpu.PrefetchScalarGridSpec(
            num_scalar_prefetch=1, grid=(4,),
            in_specs=[pl.BlockSpec((1,) + quarter(g), lambda i, s: (s[0], i, 0)) for g in gs]
            + [pl.BlockSpec((3,) + quarter(g), lambda i, s: (0, i, 0)) for g in gs],
            out_specs=[pl.BlockSpec(quarter(g), lambda i, s: (i, 0)) for g in gs]),
        compiler_params=pltpu.CompilerParams(dimension_semantics=("arbitrary",), vmem_limit_bytes=VMEM_LIMIT),
    )(shard.reshape(1).astype(jnp.int32), *gs, *lands)


def _adamw_outer(w, ct, dm, m, v, name):
    k, n = w.shape
    tr = k // 4

    def body(w_ref, c_ref, d_ref, m_ref, v_ref, g_out, d_out, m_out, v_out):
        cv = c_ref[...]
        dv = d_ref[...]
        g = cv[:, 0:1] * dv[0:1, :]
        for i in range(1, 8):
            g = g + cv[:, i:i + 1] * dv[i:i + 1, :]
        g_out[...] = g
        d_out[...], m_out[...], v_out[...] = _adamw_math(w_ref[...], g, m_ref[...], v_ref[...])

    row = _rows(tr, n)
    return _call(body, name=name, grid=(4,),
                 in_specs=[row, _rows(tr, 8), pl.BlockSpec((8, n), lambda i: (0, 0)), row, row],
                 out_specs=[row] * 4, out_shape=[_sds((k, n))] * 4)(w, ct, dm, m, v)


def _adamw_math(w, g, m, v):
    m_new = ADAM_B1 * m + (1.0 - ADAM_B1) * g
    v_new = ADAM_B2 * v + (1.0 - ADAM_B2) * (g * g)
    m_hat = m_new / (1.0 - ADAM_B1 ** ADAM_STEP)
    v_hat = v_new / (1.0 - ADAM_B2 ** ADAM_STEP)
    return -ADAM_LR * (m_hat / (jnp.sqrt(v_hat) + ADAM_EPS) + ADAM_WD * w), m_new, v_new


def _small_update(stats, smalls, name):
    offsets = [ST_DMOD, ST_DG1, ST_DLB, ST_DOG, ST_DAG, ST_DG2, ST_DFG]
    lb_index = 2

    def body(*refs):
        s_ref, ins, l_ref, outs = refs[0], refs[1:22], refs[22], refs[23:]
        tot = s_ref[0:1, :]
        for i in range(1, 8):
            tot = tot + s_ref[i:i + 1, :]
        l_ref[...] = jnp.zeros((1, 128), F32) + (0.5 / D_MODEL) * jnp.sum(tot[:, ST_LOSS:ST_LOSS + D_MODEL])
        for p, off in enumerate(offsets):
            w_ref, m_ref, v_ref = ins[3 * p:3 * p + 3]
            g_out, d_out, m_out, v_out = outs[4 * p:4 * p + 4]
            g = tot[:, off:off + w_ref.shape[1]]
            if p == lb_index:
                lg = w_ref[...]
                lb = _sigmoid(lg[0:1] - lg[1:2])
                g = g * lb * (1.0 - lb)
            for r in range(w_ref.shape[0]):
                rows = slice(r, r + 1)
                gr = g if r == 0 else -g
                delta, m_new, v_new = _adamw_math(w_ref[rows, :], gr, m_ref[rows, :], v_ref[rows, :])
                g_out[rows, :] = gr
                d_out[rows, :] = delta
                m_out[rows, :] = m_new
                v_out[rows, :] = v_new

    full = lambda a: pl.BlockSpec(a.shape, lambda i: (0, 0))
    flat = [a for t in smalls for a in t]
    return _call(body, name=name, grid=(1,),
                 in_specs=[full(stats)] + [full(a) for a in flat],
                 out_specs=[pl.BlockSpec((1, 128), lambda i: (0, 0))] + [full(t[0]) for t in smalls for _ in range(4)],
                 out_shape=[_sds((1, 128))] + [_sds(t[0].shape) for t in smalls for _ in range(4)])(stats, *flat)


def _adamw(params, name):
    n = len(params)

    def body(*refs):
        for p in range(n):
            w_ref, ga_ref, gb_ref, m_ref, v_ref = refs[5 * p:5 * p + 5]
            g_out, d_out, m_out, v_out = refs[5 * n + 4 * p:5 * n + 4 * p + 4]
            g = ga_ref[...] + gb_ref[...]
            g_out[...] = g
            d_out[...], m_out[...], v_out[...] = _adamw_math(w_ref[...], g, m_ref[...], v_ref[...])

    row = lambda w: _rows(w.shape[0] // 4, w.shape[1])
    out = _call(body, name=name, grid=(4,), in_specs=[row(p[0]) for p in params for _ in range(5)],
                out_specs=[row(p[0]) for p in params for _ in range(4)],
                out_shape=[_sds(p[0].shape) for p in params for _ in range(4)])(*[a for p in params for a in p])
    return [tuple(out[4 * p:4 * p + 4]) for p in range(n)]


def kernel(x, c, w_ada, b_ada, norm1_g, w_in, hg_lb_logits, hg_onorm_g, att_onorm_g, w_out, norm2_g, w_gate_up, w_down, final_g, loss_target, m_w_ada, m_b_ada, m_norm1_g, m_w_in, m_hg_lb_logits, m_hg_onorm_g, m_att_onorm_g, m_w_out, m_norm2_g, m_w_gate_up, m_w_down, m_final_g, v_w_ada, v_b_ada, v_norm1_g, v_w_in, v_hg_lb_logits, v_hg_onorm_g, v_att_onorm_g, v_w_out, v_norm2_g, v_w_gate_up, v_w_down, v_final_g):
    ix, iy, ic = _place()
    shard = 2 * ix + iy
    sample = 4 * ix + 2 * iy + ic
    n_ada = w_ada.shape[2]

    shards = [w_in[0], w_out[0], w_gate_up[0], w_down[0]]
    names = ["w_in", "w_out", "w_gu", "w_down"]
    shapes = [(N_SHARD,) + w.shape for w in shards]
    placed = [(_cast_place(shards[:1], shard, "place_w_in")[0],)]
    placed += [(p,) for p in _cast_place(shards[1:], shard, "place_rest")]

    b_part = lax.dynamic_slice(b_ada, (0, shard * n_ada), (1, n_ada))
    c_act, parts = _mod_rows(jnp.broadcast_to(c, (8, D_MODEL)), w_ada[0], b_part, "mod_rows")
    parts = parts[::2]
    mod = lax.dynamic_index_in_dim(parts, sample, axis=1, keepdims=False).reshape(1, 6 * D_MODEL)
    (first,), mod = _exchange_start(placed[:1], mod, "half", "gather_start_w_in")
    gathering = {}

    def get_w(name, after):
        if name == "w_in":
            halves = _exchange_wait(first, after, "half", "gather_wait_w_in")
            (passing,), token = _exchange_start([tuple(halves)], mod, "forward", "forward_start_w_in")
            (full,) = _exchange_wait(passing, token, "forward", "forward_wait_w_in")
            rest, full = _exchange_start(placed[1:], full, "gather", "gather_start_rest")
            gathering.update(zip(names[1:], rest))
            return full
        (full,) = _exchange_wait(gathering[name], after, "gather", "gather_wait_" + name)
        return full if name == "w_gu" else full.reshape(1, -1, D_MODEL)

    scattering = {}

    def put_g(name, g, g_bf16, then):
        shape = shapes[names.index(name)]
        land = lax.empty((3,) + shape[1:], BF16)
        (started,), then = _exchange_start([(g_bf16.reshape(shape), land)], then, "scatter", "scatter_start_" + name)
        scattering[name] = (g.reshape(shape), started)
        return then

    def summed(group, after, tag):
        lands = [_exchange_wait(scattering[nm][1], after, "scatter", "scatter_wait_" + nm)[1] for nm in group]
        return _sum_received([scattering[nm][0] for nm in group], shard, lands, "sum_" + tag)

    early = ["w_down", "w_gu", "w_out"]
    swapping = []

    def late(a):
        started, a = _sibling_start(summed(early, a, "early"), a, "swap_start")
        swapping.append(started)
        return a

    def project(h1):
        own = _mm_own_shard(h1, shards[0], shard, N_SHARD, "mm_in_own")
        w_full = get_w("w_in", own)
        return _mm_other_shards(h1, w_full, own, shard, "mm_in_rest"), w_full

    dx, stats = _local_step(x[0], loss_target[0], mod, norm1_g, hg_lb_logits, hg_onorm_g, att_onorm_g,
                            norm2_g, final_g, get_w, put_g, late, project)

    gathering_stats, stats = _all_gather_start(stats, "stats_start")
    moments = [(m_w_in, v_w_in), (m_w_out, v_w_out), (m_w_gate_up, v_w_gate_up), (m_w_down, v_w_down)]

    def update(group, sums, other, tag):
        params = [(shards[names.index(nm)], s, o, moments[names.index(nm)][0][0], moments[names.index(nm)][1][0])
                  for nm, s, o in zip(group, sums, other)]
        return dict(zip(group, _adamw(params, "adamw_" + tag)))

    sums, other = _sibling_wait(swapping[0], stats, "swap_wait")
    done = update(early, sums, other, "early")
    sum_in = summed(["w_in"], done["w_out"][1], "w_in")
    done.update(update(["w_in"], sum_in, _swap_sibling(sum_in, "swap_sum_in"), "w_in"))

    stats_all = _all_gather_wait(gathering_stats, done["w_in"][1], "stats_wait").reshape(8, ST_WIDTH)
    dmod = lax.dynamic_slice(stats_all, (0, ST_DMOD + shard * n_ada), (8, n_ada))

    as_row = lambda a: a.reshape(1, -1) if a.ndim == 1 else a
    smalls = [tuple(as_row(a) for a in t) for t in [
        (b_ada, m_b_ada, v_b_ada), (norm1_g, m_norm1_g, v_norm1_g),
        (hg_lb_logits, m_hg_lb_logits, v_hg_lb_logits), (hg_onorm_g, m_hg_onorm_g, v_hg_onorm_g),
        (att_onorm_g, m_att_onorm_g, v_att_onorm_g), (norm2_g, m_norm2_g, v_norm2_g),
        (final_g, m_final_g, v_final_g)]]
    loss, *small_out = _small_update(stats_all, smalls, "small_update")
    shapes_out = [b_ada.shape, norm1_g.shape, hg_lb_logits.shape, hg_onorm_g.shape, att_onorm_g.shape,
                  norm2_g.shape, final_g.shape]
    sg, sd, sm, sv = [[small_out[4 * p + i].reshape(shapes_out[p]) for p in range(7)] for i in range(4)]

    ada = _adamw_outer(w_ada[0], c_act.T, dmod, m_w_ada[0], v_w_ada[0], "adamw_w_ada")
    big = [ada] + [done[nm] for nm in names]
    bg, bd, bm, bv = [[t[i][None] for t in big] for i in range(4)]

    def order(b, s):
        return [b[0], s[0], s[1], b[1], s[2], s[3], s[4], b[2], s[5], b[3], b[4], s[6]]

    return (loss[0, 0], dx[None], *order(bg, sg), *order(bd, sd), *order(bm, sm), *order(bv, sv))
```

```python
import functools

import jax
import jax.numpy as jnp
from jax import lax
from jax.experimental import pallas as pl
from jax.experimental.pallas import tpu as pltpu

F32 = jnp.float32
BF16 = jnp.bfloat16
MESH = pl.DeviceIdType.MESH

D_MODEL = 1024
HG_WIDTH = 512
HG_HEAD = 128
HG_CHUNK = 64
HG_GROUP = 4
ATT_WIDTH = 512
ATT_HEADS = 8
ATT_BLOCK = 128
DILATIONS = (1, 4, 16)
D_FF = 2816
IN_WIDTH = 3584
N_SHARD = 4
RMS_EPS = 1e-6
NEG = -1e30

ADAM_LR = 0.001
ADAM_B1 = 0.9
ADAM_B2 = 0.999
ADAM_EPS = 1e-08
ADAM_WD = 0.01
ADAM_STEP = 10

VMEM_LIMIT = 56 * 2**20

ST_LOSS, ST_DFG, ST_DG2, ST_DG1 = 0, 1024, 2048, 3072
ST_DLB, ST_DAG, ST_DOG, ST_DMOD = 4096, 4608, 5120, 5248
ST_WIDTH = 5248 + 6144


def _call(body, *, name, grid, in_specs, out_specs, out_shape, scratch_shapes=(), aliases=None):
    return pl.pallas_call(
        body, name=name, grid=grid, in_specs=in_specs, out_specs=out_specs, out_shape=out_shape,
        scratch_shapes=list(scratch_shapes), input_output_aliases=aliases or {},
        compiler_params=pltpu.CompilerParams(
            dimension_semantics=("arbitrary",) * len(grid), vmem_limit_bytes=VMEM_LIMIT))


def _sds(shape, dtype=F32):
    return jax.ShapeDtypeStruct(shape, dtype)


def _dot(a, b):
    return jnp.dot(a, b, preferred_element_type=F32)


def _dot_nt(a, b):
    return lax.dot_general(a, b, (((1,), (1,)), ((), ())), preferred_element_type=F32)


def _dot_tn(a, b):
    return lax.dot_general(a, b, (((0,), (0,)), ((), ())), preferred_element_type=F32)


def _sigmoid(x):
    return 1.0 / (1.0 + jnp.exp(-x))


def _rows(tr, width):
    return pl.BlockSpec((tr, width), lambda i: (i, 0))


def _vec(width):
    return pl.BlockSpec((1, width), lambda i: (0, 0))


def _acc(ref, val, first):
    @pl.when(first)
    def _():
        ref[...] = val

    @pl.when(jnp.logical_not(first))
    def _():
        ref[...] += val


def _sub_rows(tr, fine, width):
    return pl.BlockSpec((fine, tr // fine, width), lambda i: (0, i, 0))


def _regroup_matrix(tr, groups):
    a = lax.broadcasted_iota(jnp.int32, (tr, tr), 0)
    b = lax.broadcasted_iota(jnp.int32, (tr, tr), 1)
    return (b == (a % groups) * (tr // groups) + a // groups).astype(BF16)


def _regroup(m, v, lanes=None):
    if v.dtype == BF16:
        return _dot(m, v)
    width = v.shape[1]
    packed, out = None, None
    for i in range(3):
        part = v.astype(BF16).astype(F32)
        v = v - part
        if lanes is None:
            out = _dot(m, part.astype(BF16)) if i == 0 else out + _dot(m, part.astype(BF16))
        else:
            packed = part if i == 0 else packed + pltpu.roll(part, i * lanes, 1)
    if lanes is None:
        return out
    out = _dot(m, packed.astype(BF16))
    out = out + pltpu.roll(out, width - lanes, 1) + pltpu.roll(out, width - 2 * lanes, 1)
    return jnp.where(lax.broadcasted_iota(jnp.int32, (1, width), 1) < lanes, out, 0.0)


def _mm_nn(a, b3, name, tm=1024, out_dtype=F32):
    m, k = a.shape
    s, _, n = b3.shape

    def body(a_ref, b_ref, o_ref):
        o_ref[...] = _dot(a_ref[...], b_ref[0]).astype(out_dtype)

    return _call(
        body, name=name, grid=(s, m // tm),
        in_specs=[pl.BlockSpec((tm, k), lambda j, i: (i, 0)), pl.BlockSpec((1, k, n), lambda j, i: (j, 0, 0))],
        out_specs=pl.BlockSpec((tm, n), lambda j, i: (i, j)), out_shape=_sds((m, s * n), out_dtype))(a, b3)


def _mm_own_shard(a, w, shard, s, name, tm=1024):
    m, k = a.shape
    n = w.shape[1]

    def body(s_ref, a_ref, w_ref, o_ref):
        o_ref[...] = _dot(a_ref[...], w_ref[...].astype(BF16)).astype(BF16)

    return pl.pallas_call(
        body, name=name, out_shape=_sds((m, s * n), BF16),
        grid_spec=pltpu.PrefetchScalarGridSpec(
            num_scalar_prefetch=1, grid=(m // tm,),
            in_specs=[pl.BlockSpec((tm, k), lambda i, sh: (i, 0)), pl.BlockSpec((k, n), lambda i, sh: (0, 0))],
            out_specs=pl.BlockSpec((tm, n), lambda i, sh: (i, sh[0]))),
        compiler_params=pltpu.CompilerParams(dimension_semantics=("arbitrary",), vmem_limit_bytes=VMEM_LIMIT),
    )(shard.reshape(1).astype(jnp.int32), a, w)


def _mm_other_shards(a, b3, partial, shard, name, tm=1024):
    m, k = a.shape
    s, _, n = b3.shape
    which = lambda j, sh: (sh[0] + 1 + j) % s

    def body(s_ref, a_ref, b_ref, p_ref, o_ref):
        o_ref[...] = _dot(a_ref[...], b_ref[0]).astype(BF16)

    return pl.pallas_call(
        body, name=name, out_shape=_sds(partial.shape, BF16),
        grid_spec=pltpu.PrefetchScalarGridSpec(
            num_scalar_prefetch=1, grid=(s - 1, m // tm),
            in_specs=[pl.BlockSpec((tm, k), lambda j, i, sh: (i, 0)),
                      pl.BlockSpec((1, k, n), lambda j, i, sh: (which(j, sh), 0, 0)),
                      pl.BlockSpec(memory_space=pl.ANY)],
            out_specs=pl.BlockSpec((tm, n), lambda j, i, sh: (i, which(j, sh)))),
        input_output_aliases={3: 0},
        compiler_params=pltpu.CompilerParams(dimension_semantics=("arbitrary",) * 2, vmem_limit_bytes=VMEM_LIMIT),
    )(shard.reshape(1).astype(jnp.int32), a, b3, partial)


def _mm_tn(a, dy, s, name, tm, tk, group=1):
    m, k = a.shape
    n = dy.shape[1] // s
    steps = m // tm

    def body(a_ref, dy_ref, o_ref, ob_ref):
        p = _dot_tn(a_ref[...], dy_ref[...])
        for g in range(group):
            pg = p[:, g * n:(g + 1) * n]
            if steps == 1:
                o_ref[g] = pg
                ob_ref[g] = pg.astype(BF16)
            else:
                _acc(o_ref.at[g], pg, pl.program_id(2) == 0)
        if steps > 1:
            @pl.when(pl.program_id(2) == steps - 1)
            def _():
                ob_ref[...] = o_ref[...].astype(BF16)

    out = pl.BlockSpec((group, tk, n), lambda kk, j, i: (j, kk, 0))
    return _call(
        body, name=name, grid=(k // tk, s // group, steps),
        in_specs=[pl.BlockSpec((tm, tk), lambda kk, j, i: (i, kk)),
                  pl.BlockSpec((tm, group * n), lambda kk, j, i: (i, j))],
        out_specs=[out, out], out_shape=[_sds((s, k, n)), _sds((s, k, n), BF16)])(a, dy)


def _rms(x):
    return lax.rsqrt(jnp.mean(x * x, axis=-1, keepdims=True) + RMS_EPS)


def _rms_bwd(dxh, xh, r):
    return r * (dxh - xh * jnp.mean(dxh * xh, axis=-1, keepdims=True))


def _norm_mod(x, g, scale, shift, name, tr=512):
    t = x.shape[0]

    def body(x_ref, g_ref, sc_ref, sh_ref, h_ref):
        xv = x_ref[...]
        n = xv * _rms(xv) * g_ref[...]
        h_ref[...] = (n * (1.0 + sc_ref[...]) + sh_ref[...]).astype(BF16)

    return _call(body, name=name, grid=(t // tr,),
                 in_specs=[_rows(tr, D_MODEL), _vec(D_MODEL), _vec(D_MODEL), _vec(D_MODEL)],
                 out_specs=_rows(tr, D_MODEL), out_shape=_sds((t, D_MODEL), BF16))(x, g, scale, shift)


def _out_resid_norm_mod(x, mixin, w_out, gate, g, scale, shift, name, tr=512):
    t = x.shape[0]

    def body(x_ref, mi_ref, w_ref, gt_ref, g_ref, sc_ref, sh_ref, m_ref, x2_ref, h_ref):
        mix = _dot(mi_ref[...], w_ref[0])
        m_ref[...] = mix
        x2 = x_ref[...] + gt_ref[...] * mix
        x2_ref[...] = x2
        n = x2 * _rms(x2) * g_ref[...]
        h_ref[...] = (n * (1.0 + sc_ref[...]) + sh_ref[...]).astype(BF16)

    row = _rows(tr, D_MODEL)
    return _call(body, name=name, grid=(t // tr,),
                 in_specs=[row, row, pl.BlockSpec(w_out.shape, lambda i: (0, 0, 0))] + [_vec(D_MODEL)] * 4,
                 out_specs=[row, row, row],
                 out_shape=[_sds((t, D_MODEL)), _sds((t, D_MODEL)), _sds((t, D_MODEL), BF16)])(
                     x, mixin, w_out, gate, g, scale, shift)


def _mm_gate_up(h, w_gu, name, tm=1024):
    m, k = h.shape
    n = w_gu.shape[2]

    def body(h_ref, wa_ref, wu_ref, da_ref, du_ref, o_ref, w_au):
        @pl.when(pl.program_id(1) == 0)
        def _():
            w_au[:, :n] = wa_ref[0]
            w_au[:, n:] = wu_ref[0]

        au = _dot(h_ref[...], w_au[...])
        a, u = au[:, :n], au[:, n:]
        sg = _sigmoid(a)
        silu = a * sg
        da_ref[...] = (u * sg * (1.0 + a * (1.0 - sg))).astype(BF16)
        du_ref[...] = silu.astype(BF16)
        o_ref[...] = (silu * u).astype(BF16)

    out = pl.BlockSpec((tm, n), lambda j, i: (i, j))
    return _call(body, name=name, grid=(2, m // tm),
                 in_specs=[pl.BlockSpec((tm, k), lambda j, i: (i, 0)), pl.BlockSpec((1, k, n), lambda j, i: (j, 0, 0)),
                           pl.BlockSpec((1, k, n), lambda j, i: (j + 2, 0, 0))],
                 out_specs=[out, out, out], out_shape=[_sds((m, 2 * n), BF16)] * 3,
                 scratch_shapes=[pltpu.VMEM((k, 2 * n), BF16)])(h, w_gu, w_gu)


def _mm_down_dx(dffn, w_down, act_da, act_du, name, tm=512):
    m = dffn.shape[0]
    _, k, n = w_down.shape

    def body(d_ref, w_ref, da_ref, du_ref, o_ref):
        dact = _dot_nt(d_ref[...], w_ref[0])
        o_ref[:, :k] = (dact * da_ref[...].astype(F32)).astype(BF16)
        o_ref[:, k:] = (dact * du_ref[...].astype(F32)).astype(BF16)

    return _call(body, name=name, grid=(m // tm,),
                 in_specs=[_rows(tm, n), pl.BlockSpec((1, k, n), lambda i: (0, 0, 0)), _rows(tm, k), _rows(tm, k)],
                 out_specs=_rows(tm, 2 * k), out_shape=_sds((m, 2 * k), BF16))(dffn, w_down, act_da, act_du)


def _down_loss(x2, act, w_down, gate, fg, tgt, name, tr=512):
    t = x2.shape[0]
    _, k, n = w_down.shape

    def body(x_ref, a_ref, w_ref, gt_ref, fg_ref, t_ref, dx_ref, df_ref, l_ref, dfg_ref, dgt_ref):
        first = pl.program_id(0) == 0
        ffn_v = _dot(a_ref[...], w_ref[0])
        x3 = x_ref[...] + gt_ref[...] * ffn_v
        r = _rms(x3)
        xh = x3 * r
        err = xh * fg_ref[...] - t_ref[...]
        dy = err * (1.0 / D_MODEL)
        dx3 = _rms_bwd(dy * fg_ref[...], xh, r)
        dx_ref[...] = dx3
        df_ref[...] = (dx3 * gt_ref[...]).astype(BF16)
        _acc(l_ref, jnp.sum(err * err, axis=0, keepdims=True), first)
        _acc(dfg_ref, jnp.sum(dy * xh, axis=0, keepdims=True), first)
        _acc(dgt_ref, jnp.sum(dx3 * ffn_v, axis=0, keepdims=True), first)

    row, vec = _rows(tr, D_MODEL), _vec(D_MODEL)
    return _call(body, name=name, grid=(t // tr,),
                 in_specs=[row, _rows(tr, k), pl.BlockSpec((1, k, n), lambda i: (0, 0, 0)), vec, vec, row],
                 out_specs=[row, row, vec, vec, vec],
                 out_shape=[_sds((t, D_MODEL)), _sds((t, D_MODEL), BF16)] + [_sds((1, D_MODEL))] * 3)(
                     x2, act, w_down, gate, fg, tgt)


def _norm_mod_bwd(dh, x, g, scale, dres, name, gate=None, mix=None, w=None, tr=512):
    t = x.shape[0]
    below = gate is not None

    def body(*refs):
        if w is not None:
            w_ref, w_full, sem, refs = refs[1], refs[-2], refs[-1], refs[:1] + refs[2:-2]

            @pl.when(pl.program_id(0) == 0)
            def _():
                n = w.shape[2]
                copies = [pltpu.make_async_copy(w_ref.at[j], w_full.at[:, pl.ds(j * n, n)], sem.at[j])
                          for j in range(w.shape[0])]
                for cp in copies:
                    cp.start()
                for cp in copies:
                    cp.wait()

        if below:
            dh_ref, x_ref, g_ref, sc_ref, dr_ref, gt_ref, m_ref, dx_ref, dsh_ref, dsc_ref, dg_ref, dgt_ref, dm_ref = refs
        else:
            dh_ref, x_ref, g_ref, sc_ref, dr_ref, dx_ref, dsh_ref, dsc_ref, dg_ref = refs
        first = pl.program_id(0) == 0
        xv = x_ref[...]
        if w is None:
            dhv = dh_ref[...].astype(F32)
        else:
            dhv = _dot_nt(dh_ref[...], w_full[...])
        r = _rms(xv)
        xh = xv * r
        dn = dhv * (1.0 + sc_ref[...])
        dx = dr_ref[...] + _rms_bwd(dn * g_ref[...], xh, r)
        dx_ref[...] = dx
        _acc(dsh_ref, jnp.sum(dhv, axis=0, keepdims=True), first)
        _acc(dsc_ref, jnp.sum(dhv * xh * g_ref[...], axis=0, keepdims=True), first)
        _acc(dg_ref, jnp.sum(dn * xh, axis=0, keepdims=True), first)
        if below:
            _acc(dgt_ref, jnp.sum(dx * m_ref[...], axis=0, keepdims=True), first)
            dm_ref[...] = (dx * gt_ref[...]).astype(BF16)

    row, vec = _rows(tr, D_MODEL), _vec(D_MODEL)
    first_specs = [row] if w is None else [_rows(tr, dh.shape[1]), pl.BlockSpec(memory_space=pl.ANY)]
    scratch = [] if w is None else [pltpu.VMEM((w.shape[1], dh.shape[1]), BF16), pltpu.SemaphoreType.DMA((w.shape[0],))]
    in_specs = first_specs + [row, vec, vec, row] + ([vec, row] if below else [])
    out_specs = [row, vec, vec, vec] + ([vec, row] if below else [])
    out_shape = [_sds((t, D_MODEL))] + [_sds((1, D_MODEL))] * 3 + ([_sds((1, D_MODEL)), _sds((t, D_MODEL), BF16)] if below else [])
    args = ((dh,) if w is None else (dh, w)) + (x, g, scale, dres) + ((gate, mix) if below else ())
    return _call(body, name=name, grid=(t // tr,), in_specs=in_specs, out_specs=out_specs, out_shape=out_shape,
                 scratch_shapes=scratch)(*args)


def _mix_in_bwd(dmix, w_out, o_hg, proj, att, og, ag, fine, name, tr=512):
    t = o_hg.shape[0]

    def body(dy_ref, w_ref, o_ref, g_ref, a_ref, og_ref, ag_ref,
             do_ref, dg_ref, da_ref, dd_ref, das_ref, dds_ref, dog_ref, dag_ref, to_sub):
        first = pl.program_id(0) == 0

        @pl.when(first)
        def _():
            to_sub[...] = _regroup_matrix(tr, tr // fine)

        dmi = _dot_nt(dy_ref[...], w_ref[0])
        dog = jnp.zeros((1, HG_HEAD), F32)
        for h in range(HG_WIDTH // HG_HEAD):
            sl = slice(h * HG_HEAD, (h + 1) * HG_HEAD)
            oh = o_ref[:, sl].astype(F32)
            gv = g_ref[:, sl].astype(F32)
            dv = dmi[:, sl]
            r = _rms(oh)
            xh = oh * r
            sg = _sigmoid(gv)
            dno = dv * gv * sg
            dg_ref[:, sl] = (dv * xh * og_ref[...] * sg * (1.0 + gv * (1.0 - sg))).astype(BF16)
            dog = dog + jnp.sum(dno * xh, axis=0, keepdims=True)
            do_ref[:, sl] = _rms_bwd(dno * og_ref[...], xh, r).astype(BF16)
        _acc(dog_ref, dog, first)
        av = a_ref[...]
        dav = dmi[:, HG_WIDTH:]
        r = _rms(av)
        xa = av * r
        _acc(dag_ref, jnp.sum(dav * xa, axis=0, keepdims=True), first)
        datt = _rms_bwd(dav * ag_ref[...], xa, r)
        da_ref[...] = datt.astype(BF16)
        das_ref[...] = _regroup(to_sub[...], datt.astype(BF16)).astype(BF16).reshape(das_ref.shape)
        prod = datt * av
        lane = lax.broadcasted_iota(jnp.int32, (1, 128), 1)
        dd = jnp.zeros((tr, 128), F32)
        for hp in range(ATT_HEADS // 2):
            pp = prod[:, hp * 128:(hp + 1) * 128]
            lo = jnp.sum(jnp.where(lane < 64, pp, 0.0), axis=-1, keepdims=True)
            hi = jnp.sum(jnp.where(lane >= 64, pp, 0.0), axis=-1, keepdims=True)
            dd = jnp.where(lane == 2 * hp, lo, dd)
            dd = jnp.where(lane == 2 * hp + 1, hi, dd)
        dd_ref[...] = dd
        dds_ref[...] = _regroup(to_sub[...], dd, ATT_HEADS).reshape(dds_ref.shape)

    half = _rows(tr, HG_WIDTH)
    out = _call(body, name=name, grid=(t // tr,),
                in_specs=[_rows(tr, D_MODEL), pl.BlockSpec(w_out.shape, lambda i: (0, 0, 0)), half,
                          pl.BlockSpec((tr, HG_WIDTH), lambda i: (i, 3)), half, _vec(HG_HEAD), _vec(ATT_WIDTH)],
                out_specs=[half, pl.BlockSpec((tr, HG_WIDTH), lambda i: (i, 3)), half, _rows(tr, 128),
                           _sub_rows(tr, fine, ATT_WIDTH), _sub_rows(tr, fine, 128), _vec(HG_HEAD), _vec(ATT_WIDTH)],
                out_shape=[_sds((t, HG_WIDTH), BF16), _sds((t, IN_WIDTH), BF16), _sds((t, HG_WIDTH), BF16),
                           _sds((t, 128)), _sds((fine, t // fine, ATT_WIDTH), BF16),
                           _sds((fine, t // fine, 128)), _sds((1, HG_HEAD)), _sds((1, ATT_WIDTH))],
                scratch_shapes=[pltpu.VMEM((tr, tr), BF16)])(dmix, w_out, o_hg, proj, att, og, ag)
    out = list(out)
    return out[:4] + [out[4].reshape(t, ATT_WIDTH), out[5].reshape(t, 128)] + out[6:]


def _dproj(dproj, dqkvs, name, tr=1024):
    t = dproj.shape[0]
    nbr = len(dqkvs)
    first = IN_WIDTH // ATT_WIDTH - 3

    def body(*refs):
        refs[-1][...] = sum(r[...].astype(F32) for r in refs[:nbr]).astype(BF16)

    return _call(body, name=name, grid=(t // tr, 3),
                 in_specs=[pl.BlockSpec((tr, ATT_WIDTH), lambda i, j: (i, j))] * nbr + [pl.BlockSpec(memory_space=pl.ANY)],
                 out_specs=pl.BlockSpec((tr, ATT_WIDTH), lambda i, j: (i, first + j)),
                 out_shape=_sds(dproj.shape, BF16), aliases={nbr: 0})(*dqkvs, dproj)


def _chunk_tri(upper):
    row = lax.broadcasted_iota(jnp.int32, (HG_GROUP, HG_CHUNK, HG_CHUNK), 1)
    col = lax.broadcasted_iota(jnp.int32, (HG_GROUP, HG_CHUNK, HG_CHUNK), 2)
    return (row <= col if upper else row >= col).astype(BF16)


def _chunk_cumsum(x, tri):
    x3 = x.reshape(HG_GROUP, HG_CHUNK, x.shape[1])
    dims = (((2,), (1,)), ((0,), (0,)))
    out = None
    for _ in range(3):
        part = x3.astype(BF16)
        x3 = x3 - part.astype(F32)
        term = lax.dot_general(tri, part, dims, preferred_element_type=F32)
        out = term if out is None else out + term
    return out.reshape(x.shape)


def _hg_gates(f_raw, q_raw, lb, tri):
    sg = _sigmoid(f_raw)
    f = lb + (1.0 - lb) * sg
    k = 1.0 - f
    b = _chunk_cumsum(jnp.log(f), tri)
    sq = _sigmoid(q_raw)
    return sg, f, k, b, sq


def _hg_masks(rows):
    row = lax.broadcasted_iota(jnp.int32, (rows, rows), 0)
    col = lax.broadcasted_iota(jnp.int32, (rows, rows), 1)
    same = (row // HG_CHUNK) == (col // HG_CHUNK)
    return jnp.logical_and(row >= col, same), jnp.logical_and(row <= col, same)


def _per_chunk(rows_of):
    return jnp.concatenate([jnp.broadcast_to(r, (HG_CHUNK, r.shape[1])) for r in rows_of], axis=0)


def _hgrn_fwd(proj, lb_logits, name):
    t = proj.shape[0]
    nc = t // HG_CHUNK
    nh = HG_WIDTH // HG_HEAD
    rows = HG_GROUP * HG_CHUNK

    def body(q_ref, f_ref, i_ref, lg_ref, o_ref, st_ref, s_scr):
        @pl.when(pl.program_id(0) == 0)
        def _():
            s_scr[...] = jnp.zeros_like(s_scr)

        lg = lg_ref[...]
        lb_all = _sigmoid(lg[0:1] - lg[1:2])
        causal, _ = _hg_masks(rows)
        tri = _chunk_tri(False)
        for h in range(nh):
            sl = slice(h * HG_HEAD, (h + 1) * HG_HEAD)
            q_raw = q_ref[:, sl].astype(F32)
            _, _, k, b, sq = _hg_gates(f_ref[:, sl].astype(F32), q_raw, lb_all[:, sl], tri)
            v = i_ref[:, sl].astype(BF16)
            gls = [b[(g + 1) * HG_CHUNK - 1:(g + 1) * HG_CHUNK] for g in range(HG_GROUP)]
            bm = _per_chunk([b[g * HG_CHUNK + HG_CHUNK // 2 - 1:g * HG_CHUNK + HG_CHUNK // 2] for g in range(HG_GROUP)])
            qd = (q_raw * sq * jnp.exp(b)).astype(BF16)
            qm = (q_raw * sq * jnp.exp(b - bm)).astype(BF16)
            km = (k * jnp.exp(bm - b)).astype(BF16)
            ke = (k * jnp.exp(_per_chunk(gls) - b)).astype(BF16)
            a = jnp.where(causal, _dot_nt(qm, km), 0.0).astype(BF16)
            o_intra = _dot(a, v)
            st = s_scr[h]
            o_inter = []
            for g in range(HG_GROUP):
                rs = slice(g * HG_CHUNK, (g + 1) * HG_CHUNK)
                st_ref[g, sl, :] = st
                o_inter.append(_dot_nt(qd[rs], st.astype(BF16)))
                st = st * jnp.exp(gls[g]) + _dot_tn(v[rs], ke[rs])
            s_scr[h] = st
            o_ref[:, sl] = (o_intra + jnp.concatenate(o_inter, axis=0)).astype(BF16)

    blk = lambda j: pl.BlockSpec((rows, HG_WIDTH), lambda c: (c, j))
    return _call(body, name=name, grid=(nc // HG_GROUP,),
                 in_specs=[blk(0), blk(1), blk(2), pl.BlockSpec((2, HG_WIDTH), lambda c: (0, 0))],
                 out_specs=[blk(0), pl.BlockSpec((HG_GROUP, HG_WIDTH, HG_HEAD), lambda c: (c, 0, 0))],
                 out_shape=[_sds((t, HG_WIDTH), BF16), _sds((nc, HG_WIDTH, HG_HEAD))],
                 scratch_shapes=[pltpu.VMEM((nh, HG_HEAD, HG_HEAD), F32)])(proj, proj, proj, lb_logits)


def _hgrn_bwd(proj, lb_logits, states, do, dproj, name):
    t = proj.shape[0]
    ng = t // (HG_GROUP * HG_CHUNK)
    nh = HG_WIDTH // HG_HEAD
    rows = HG_GROUP * HG_CHUNK

    def body(q_ref, f_ref, i_ref, lg_ref, st_ref, do_ref, _, d_ref, dlb_ref, ds_scr):
        first = pl.program_id(0) == 0

        @pl.when(first)
        def _():
            ds_scr[...] = jnp.zeros_like(ds_scr)

        lg = lg_ref[...]
        lb_all = _sigmoid(lg[0:1] - lg[1:2])
        causal, _ = _hg_masks(rows)
        tri = _chunk_tri(False)
        tri_t = _chunk_tri(True)
        dlb = []
        for h in range(nh):
            sl = slice(h * HG_HEAD, (h + 1) * HG_HEAD)
            q_raw = q_ref[:, sl].astype(F32)
            lb = lb_all[:, sl]
            sg, f, k, b, sq = _hg_gates(f_ref[:, sl].astype(F32), q_raw, lb, tri)
            v = i_ref[:, sl].astype(BF16)
            gls = [b[(g + 1) * HG_CHUNK - 1:(g + 1) * HG_CHUNK] for g in range(HG_GROUP)]
            bm = _per_chunk([b[g * HG_CHUNK + HG_CHUNK // 2 - 1:g * HG_CHUNK + HG_CHUNK // 2] for g in range(HG_GROUP)])
            eb = jnp.exp(b)
            ebm = jnp.exp(b - bm)
            emb = jnp.exp(bm - b)
            egb = jnp.exp(_per_chunk(gls) - b)
            ke = k * egb
            qd_b, qm_b = (q_raw * sq * eb).astype(BF16), (q_raw * sq * ebm).astype(BF16)
            km_b, ke_b = (k * emb).astype(BF16), ke.astype(BF16)
            dov = do_ref[:, sl].astype(BF16)
            a = jnp.where(causal, _dot_nt(qm_b, km_b), 0.0).astype(BF16)
            da = jnp.where(causal, _dot_nt(dov, v), 0.0).astype(BF16)
            dkm = _dot_tn(da, qm_b)
            dst = ds_scr[h]
            dqd_s, dv_s, dke_s, dgl_s = [None] * HG_GROUP, [None] * HG_GROUP, [None] * HG_GROUP, [None] * HG_GROUP
            for g in reversed(range(HG_GROUP)):
                rs = slice(g * HG_CHUNK, (g + 1) * HG_CHUNK)
                st = st_ref[g, sl, :]
                dst_b = dst.astype(BF16)
                egl = jnp.exp(gls[g])
                dqd_s[g] = _dot(dov[rs], st.astype(BF16))
                dv_s[g] = _dot_nt(ke_b[rs], dst_b)
                dke_s[g] = _dot(v[rs], dst_b)
                dgl_s[g] = jnp.sum(dst * st, axis=0, keepdims=True) * egl
                dst = _dot_tn(dov[rs], qd_b[rs]) + dst * egl
            ds_scr[h] = dst
            dqm = _dot(da, km_b)
            dqd = jnp.concatenate(dqd_s, axis=0)
            dv = _dot_tn(a, dov) + jnp.concatenate(dv_s, axis=0)
            dke = jnp.concatenate(dke_s, axis=0)
            t1 = dke * ke
            db = dqm * qm_b.astype(F32) - dkm * km_b.astype(F32) + dqd * qd_b.astype(F32) - t1
            dgl = _per_chunk([dgl_s[g] + jnp.sum(t1[g * HG_CHUNK:(g + 1) * HG_CHUNK], axis=0, keepdims=True)
                              for g in range(HG_GROUP)])
            dlf = _chunk_cumsum(db, tri_t) + dgl
            df = dlf / f - (dkm * emb + dke * egb)
            d_ref[:, sl] = ((dqm * ebm + dqd * eb) * sq * (1.0 + q_raw * (1.0 - sq))).astype(BF16)
            d_ref[:, HG_WIDTH + h * HG_HEAD:HG_WIDTH + (h + 1) * HG_HEAD] = (
                df * (1.0 - lb) * sg * (1.0 - sg)).astype(BF16)
            d_ref[:, 2 * HG_WIDTH + h * HG_HEAD:2 * HG_WIDTH + (h + 1) * HG_HEAD] = dv.astype(BF16)
            dlb.append(jnp.sum(df * (1.0 - sg), axis=0, keepdims=True))
        _acc(dlb_ref, jnp.concatenate(dlb, axis=1), first)

    rev = lambda j: pl.BlockSpec((rows, HG_WIDTH), lambda c: (ng - 1 - c, j))
    return _call(body, name=name, grid=(ng,),
                 in_specs=[rev(0), rev(1), rev(2), pl.BlockSpec((2, HG_WIDTH), lambda c: (0, 0)),
                           pl.BlockSpec((HG_GROUP, HG_WIDTH, HG_HEAD), lambda c: (ng - 1 - c, 0, 0)), rev(0),
                           pl.BlockSpec(memory_space=pl.ANY)],
                 out_specs=[pl.BlockSpec((rows, 3 * HG_WIDTH), lambda c: (ng - 1 - c, 0)), _vec(HG_WIDTH)],
                 out_shape=[_sds(dproj.shape, BF16), _sds((1, HG_WIDTH))], aliases={6: 0},
                 scratch_shapes=[pltpu.VMEM((nh, HG_HEAD, HG_HEAD), F32)])(
                     proj, proj, proj, lb_logits, states, do, dproj)


def _to_sub(a, dil):
    t, w = a.shape
    return a if dil == 1 else a.reshape(t // dil, dil, w).transpose(1, 0, 2).reshape(t, w)


def _from_sub(a, dil):
    t, w = a.shape
    return a if dil == 1 else a.reshape(dil, t // dil, w).transpose(1, 0, 2).reshape(t, w)


def _att_mask(has_prev, seg):
    def place(v):
        v = v % ATT_BLOCK
        return v if seg == 1 else seg * (v % (ATT_BLOCK // seg)) + v // (ATT_BLOCK // seg)

    row = lax.broadcasted_iota(jnp.int32, (2 * ATT_BLOCK, 2 * ATT_BLOCK), 0)
    col = lax.broadcasted_iota(jnp.int32, (2 * ATT_BLOCK, 2 * ATT_BLOCK), 1)
    qi, kj = place(row), place(col)
    prev = jnp.logical_and(jnp.logical_and(col < ATT_BLOCK, kj >= qi), has_prev)
    cur = jnp.logical_and(col >= ATT_BLOCK, kj <= qi)
    return jnp.logical_or(prev, cur), lax.broadcasted_iota(jnp.int32, (1, 128), 1)


def _get(ref, sl):
    if len(ref.shape) == 2:
        return ref[:, sl]
    v = ref[:, :, sl]
    return v.reshape(ATT_BLOCK, v.shape[2])


def _put(ref, sl, val):
    if len(ref.shape) == 2:
        ref[:, sl] = val
    else:
        ref[:, :, sl] = val.reshape(ref.shape[0], ref.shape[1], val.shape[1])


def _att_spec(nb, dil, seg, width, col, back):
    bps = nb // dil

    def plain(n):
        return jnp.clip(n - back, 0, nb - 1), col

    def segmented(n):
        m = jnp.clip(n - back, 0, nb - 1)
        return 0, m // bps, m % bps, 0, col

    if seg == 1:
        return pl.BlockSpec((ATT_BLOCK, width), plain)
    return pl.BlockSpec((seg, None, None, ATT_BLOCK // seg, width), segmented)


def _att_shape(nb, dil, seg, width):
    t = nb * ATT_BLOCK
    return (t, width) if seg == 1 else (seg, dil, nb // dil, ATT_BLOCK // seg, width)


def _att_view(a, nb, dil, seg):
    return a.reshape(_att_shape(nb, dil, seg, a.shape[1]))


def _attn_fwd_block(q_ref, kc_ref, kp_ref, vc_ref, vp_ref, o_ref, l_ref, has_prev, seg, lane0):
    mask, lane = _att_mask(has_prev, seg)
    lo = lane < 64
    nq = ATT_BLOCK
    lse_all = jnp.zeros((nq, 128), F32)
    for hp in range(ATT_HEADS // 2):
        sl = slice(hp * 128, (hp + 1) * 128)
        q2 = _get(q_ref, sl)
        zero = jnp.zeros_like(q2)
        q2 = q2 * 0.125
        qs = jnp.concatenate([jnp.where(lo, q2, zero), jnp.where(lo, zero, q2)], axis=0)
        kk = jnp.concatenate([_get(kp_ref, sl), _get(kc_ref, sl)], axis=0)
        vv = jnp.concatenate([_get(vp_ref, sl), _get(vc_ref, sl)], axis=0)
        s = jnp.where(mask, _dot_nt(qs, kk), NEG)
        mx = jnp.max(s, axis=-1, keepdims=True)
        p = jnp.exp(s - mx)
        l = jnp.sum(p, axis=-1, keepdims=True)
        o = _dot(p.astype(BF16), vv) * (1.0 / l)
        _put(o_ref, sl, jnp.where(lo, o[:nq], o[nq:]).astype(BF16))
        lse = mx + jnp.log(l)
        lse_all = jnp.where(lane == lane0 + 2 * hp, lse[:nq], lse_all)
        lse_all = jnp.where(lane == lane0 + 2 * hp + 1, lse[nq:], lse_all)
    _put(l_ref, slice(None), lse_all)


def _attn_fwd(branches, name):
    t = branches[0][0].shape[0]
    nb = t // ATT_BLOCK
    nbr = len(branches)

    def body(*refs):
        n = pl.program_id(0)
        for i, (_, dil, seg) in enumerate(branches):
            _attn_fwd_block(*refs[5 * i:5 * i + 5], *refs[5 * nbr + 2 * i:5 * nbr + 2 * i + 2],
                            (n % (nb // dil)) != 0, seg, ATT_HEADS * i)

    in_specs, args, out_specs, out_shape = [], [], [], []
    for qkv, dil, seg in branches:
        c0 = qkv.shape[1] // ATT_WIDTH - 3
        in_specs += [_att_spec(nb, dil, seg, ATT_WIDTH, c0 + j, back) for j, back in [(0, 0), (1, 0), (1, 1), (2, 0), (2, 1)]]
        args += [_att_view(qkv, nb, dil, seg)] * 5
        out_specs += [_att_spec(nb, dil, seg, ATT_WIDTH, 0, 0), _att_spec(nb, dil, seg, 128, 0, 0)]
        out_shape += [_sds(_att_shape(nb, dil, seg, ATT_WIDTH), BF16), _sds(_att_shape(nb, dil, seg, 128))]
    out = _call(body, name=name, grid=(nb,), in_specs=in_specs, out_specs=out_specs, out_shape=out_shape)(*args)
    return [(out[2 * i].reshape(t, ATT_WIDTH), out[2 * i + 1].reshape(t, 128)) for i in range(nbr)]


def _combine_mix_in(os_, ls_, fine, o_hg, proj, og, ag, name, tr=512):
    t = os_[0].shape[0]
    nbr = len(os_)

    def body(*refs):
        o_refs, l_refs = refs[:nbr], refs[nbr:2 * nbr]
        oh_ref, g_ref, og_ref, ag_ref, a_ref, lt_ref, lts_ref, m_ref, to_natural, to_sub = refs[2 * nbr:]

        @pl.when(pl.program_id(0) == 0)
        def _():
            to_natural[...] = _regroup_matrix(tr, fine)
            to_sub[...] = _regroup_matrix(tr, tr // fine)

        lane = lax.broadcasted_iota(jnp.int32, (1, 128), 1)
        packed = l_refs[0][...] + _regroup(to_natural[...], sum(r[...] for r in l_refs[1:]).reshape(tr, 128))
        ls = [packed if i == 0 else pltpu.roll(packed, 128 - ATT_HEADS * i, 1) for i in range(nbr)]
        mx = functools.reduce(jnp.maximum, ls)
        tot = mx + jnp.log(sum(jnp.exp(l - mx) for l in ls))
        ws = [jnp.exp(l - tot) for l in ls]
        tot = jnp.where(lane < ATT_HEADS, tot, 0.0)
        lt_ref[...] = tot
        lts_ref[...] = _regroup(to_sub[...], tot).reshape(lts_ref.shape)
        o_vals = [o_refs[0]] + [_regroup(to_natural[...], r[...].reshape(tr, ATT_WIDTH)) for r in o_refs[1:]]
        pairs = []
        for hp in range(ATT_HEADS // 2):
            sl = slice(hp * 128, (hp + 1) * 128)
            acc = jnp.zeros((tr, 128), F32)
            for w, o in zip(ws, o_vals):
                wf = jnp.where(lane < 64, w[:, 2 * hp:2 * hp + 1], w[:, 2 * hp + 1:2 * hp + 2])
                acc = acc + wf * o[:, sl]
            pairs.append(acc)
        av = jnp.concatenate(pairs, axis=1)
        a_ref[...] = av
        m_ref[:, HG_WIDTH:] = (av * _rms(av) * ag_ref[...]).astype(BF16)
        for h in range(HG_WIDTH // HG_HEAD):
            sl = slice(h * HG_HEAD, (h + 1) * HG_HEAD)
            oh = oh_ref[:, sl].astype(F32)
            gv = g_ref[:, sl].astype(F32)
            m_ref[:, sl] = (oh * _rms(oh) * og_ref[...] * (gv * _sigmoid(gv))).astype(BF16)

    half = _rows(tr, ATT_WIDTH)
    sub = lambda a: a.reshape(fine, t // fine, a.shape[1])
    att, lse, lse_sub, mixin = _call(
        body, name=name, grid=(t // tr,),
        in_specs=[half] + [_sub_rows(tr, fine, ATT_WIDTH)] * (nbr - 1) + [_rows(tr, 128)] + [_sub_rows(tr, fine, 128)] * (nbr - 1)
        + [half, pl.BlockSpec((tr, HG_WIDTH), lambda i: (i, 3)), _vec(HG_HEAD), _vec(ATT_WIDTH)],
        out_specs=[half, _rows(tr, 128), _sub_rows(tr, fine, 128), _rows(tr, D_MODEL)],
        out_shape=[_sds((t, ATT_WIDTH)), _sds((t, 128)), _sds((fine, t // fine, 128)), _sds((t, D_MODEL), BF16)],
        scratch_shapes=[pltpu.VMEM((tr, tr), BF16)] * 2)(
            os_[0], *map(sub, os_[1:]), ls_[0], *map(sub, ls_[1:]), o_hg, proj, og, ag)
    return att, lse, lse_sub.reshape(t, 128), mixin


def _attn_bwd_block(q_ref, k_ref, v_ref, do_ref, l_ref, d_ref, out_ref, carry, prev, has_prev, seg):
    w = ATT_WIDTH
    nq = ATT_BLOCK
    mask, lane = _att_mask(has_prev, seg)
    lo = lane < 64
    lse, ddv = _get(l_ref, slice(None)), _get(d_ref, slice(None))
    for hp in range(ATT_HEADS // 2):
        sl = slice(hp * 128, (hp + 1) * 128)
        sk = slice(w + hp * 128, w + (hp + 1) * 128)
        sv = slice(2 * w + hp * 128, 2 * w + (hp + 1) * 128)
        q2, do2 = _get(q_ref, sl), _get(do_ref, sl)
        zero = jnp.zeros_like(q2)
        q2 = q2 * 0.125
        qs = jnp.concatenate([jnp.where(lo, q2, zero), jnp.where(lo, zero, q2)], axis=0)
        dos = jnp.concatenate([jnp.where(lo, do2, zero), jnp.where(lo, zero, do2)], axis=0)
        kc, vc = _get(k_ref, sl), _get(v_ref, sl)
        kk = jnp.concatenate([prev[:, sl], kc], axis=0)
        vv = jnp.concatenate([prev[:, sk], vc], axis=0)
        prev[:, sl] = kc
        prev[:, sk] = vc
        ls = jnp.concatenate([lse[:, 2 * hp:2 * hp + 1], lse[:, 2 * hp + 1:2 * hp + 2]], axis=0)
        dh = jnp.concatenate([ddv[:, 2 * hp:2 * hp + 1], ddv[:, 2 * hp + 1:2 * hp + 2]], axis=0)
        p = jnp.exp(jnp.where(mask, _dot_nt(qs, kk) - ls, NEG))
        ds = (p * (_dot_nt(dos, vv) - dh)).astype(BF16)
        dq = _dot(ds, kk) * 0.125
        dk = _dot_tn(ds, qs)
        dv = _dot_tn(p.astype(BF16), dos)
        _put(out_ref, sl, carry[:, sl].astype(BF16))
        _put(out_ref, sk, (carry[:, sk] + dk[:nq]).astype(BF16))
        _put(out_ref, sv, (carry[:, sv] + dv[:nq]).astype(BF16))
        carry[:, sl] = jnp.where(lo, dq[:nq], dq[nq:])
        carry[:, sk] = dk[nq:]
        carry[:, sv] = dv[nq:]


def _attn_bwd(branches, name):
    t = branches[0][0].shape[0]
    nb = t // ATT_BLOCK
    nbr = len(branches)
    w = ATT_WIDTH

    def body(*refs):
        ins, outs, carries, prevs = refs[:6 * nbr], refs[6 * nbr:7 * nbr], refs[7 * nbr:8 * nbr], refs[8 * nbr:]
        n = pl.program_id(0)

        @pl.when(n == 0)
        def _():
            for scratch in carries + prevs:
                scratch[...] = jnp.zeros_like(scratch)

        @pl.when(n < nb)
        def _():
            for i, branch in enumerate(branches):
                dil, seg = branch[4:]
                _attn_bwd_block(*ins[6 * i:6 * i + 6], outs[i], carries[i], prevs[i], (n % (nb // dil)) != 0, seg)

        @pl.when(n == nb)
        def _():
            for i in range(nbr):
                _put(outs[i], slice(None), carries[i][...].astype(BF16))

    in_specs, args, out_specs, out_shape = [], [], [], []
    for qkv, dout, lse, dd, dil, seg in branches:
        c0 = qkv.shape[1] // w - 3
        in_specs += [_att_spec(nb, dil, seg, w, c0 + j, 0) for j in range(3)]
        in_specs += [_att_spec(nb, dil, seg, w, 0, 0), _att_spec(nb, dil, seg, 128, 0, 0), _att_spec(nb, dil, seg, 128, 0, 0)]
        args += [_att_view(a, nb, dil, seg) for a in [qkv] * 3 + [dout, lse, dd]]
        out_specs += [_att_spec(nb, dil, seg, 3 * w, 0, 1)]
        out_shape += [_sds(_att_shape(nb, dil, seg, 3 * w), BF16)]
    out = _call(body, name=name, grid=(nb + 1,), in_specs=in_specs, out_specs=out_specs, out_shape=out_shape,
                scratch_shapes=[pltpu.VMEM((ATT_BLOCK, 3 * w), F32)] * nbr + [pltpu.VMEM((ATT_BLOCK, 2 * w), BF16)] * nbr)(*args)
    return [o.reshape(t, 3 * w) for o in out]


def _local_step(x, tgt, mod, norm1_g, lb_logits, og, ag, norm2_g, fg, get_w, put_g, late=lambda a: a, project=None):
    shift1, scale1, gate1, shift2, scale2, gate2 = [mod[:, i * D_MODEL:(i + 1) * D_MODEL] for i in range(6)]
    fg = fg.reshape(1, D_MODEL)

    h1 = _norm_mod(x, norm1_g, scale1, shift1, "norm_mod1")
    if project is None:
        w_in = get_w("w_in", h1)
        proj = _mm_nn(h1, w_in, "mm_in", out_dtype=BF16)
    else:
        proj, w_in = project(h1)
    o_hg, states = _hgrn_fwd(proj, lb_logits, "hgrn_fwd")
    fine = DILATIONS[-1]
    layouts = [(d, 1 if d == 1 else fine // d) for d in DILATIONS]
    qkv_fine = _to_sub(proj, fine)
    qkvs = [proj if d == 1 else qkv_fine for d in DILATIONS]
    natural = lambda a, d: a if d == 1 else _from_sub(a, fine)
    outs = _attn_fwd([(q, d, seg) for q, (d, seg) in zip(qkvs, layouts)], "attn_fwd")
    att, lse, lse_fine, mixin = _combine_mix_in([o for o, _ in outs], [l for _, l in outs], fine,
                                                o_hg, proj, og, ag, "attn_combine_mix_in")
    w_out = get_w("w_out", mixin)
    mix, x2, h2 = _out_resid_norm_mod(x, mixin, w_out, gate1, norm2_g, scale2, shift2, "mm_out_resid_norm_mod2")
    w_gu = get_w("w_gu", h2)
    a_ff, u_ff, act = _mm_gate_up(h2, w_gu, "mm_gu")
    w_down = get_w("w_down", act)
    dx3, dffn, loss_v, dfg, dgate2 = _down_loss(x2, act, w_down, gate2, fg, tgt, "mm_down_loss")

    dffn = put_g("w_down", *_mm_tn(act, dffn, 1, "mm_down_dw", tm=2048, tk=D_FF // 2), dffn)
    dau = _mm_down_dx(dffn, w_down, a_ff, u_ff, "mm_down_dx")
    dau = put_g("w_gu", *_mm_tn(h2, dau, N_SHARD, "mm_gu_dw", tm=x.shape[0], tk=512), dau)
    dx2, dshift2, dscale2, dg2, dgate1, dmix = _norm_mod_bwd(
        dau, x2, norm2_g, scale2, dx3, "mm_gu_dx_norm_bwd", gate=gate1, mix=mix, w=w_gu)
    dmix = put_g("w_out", *_mm_tn(mixin, dmix, 1, "mm_out_dw", tm=x.shape[0], tk=512), dmix)
    do_hg, dproj, datt, dd, datt_fine, dd_fine, dog, dag = _mix_in_bwd(
        dmix, w_out, o_hg, proj, att, og, ag, fine, "mm_out_dx_mix_in_bwd")
    datts = _attn_bwd([(q,) + ((datt, lse, dd) if d == 1 else (datt_fine, lse_fine, dd_fine)) + (d, seg)
                       for q, (d, seg) in zip(qkvs, layouts)], "attn_bwd")
    dproj, dlb = _hgrn_bwd(proj, lb_logits, states, do_hg, dproj, "hgrn_bwd")
    dproj = late(dproj)
    dproj = _dproj(dproj, [natural(a, d) for a, d in zip(datts, DILATIONS)], "dproj")
    dproj = put_g("w_in", *_mm_tn(h1, dproj, N_SHARD, "mm_in_dw", tm=x.shape[0], tk=512, group=2), dproj)
    dx, dshift1, dscale1, dg1 = _norm_mod_bwd(dproj, x, norm1_g, scale1, dx2, "mm_in_dx_norm_bwd", w=w_in)

    stats = jnp.concatenate([loss_v, dfg, dg2, dg1, dlb, dag, dog,
                             dshift1, dscale1, dgate1, dshift2, dscale2, dgate2], axis=1)
    return dx, stats


def _place():
    x, y, c = lax.axis_index("x"), lax.axis_index("y"), lax.axis_index("c")
    return x, y, c


def _chip_peers(x, y, c):
    return [(1 - x, y, c), (x, 1 - y, c), (1 - x, 1 - y, c)]


def _comm_call(body, name, n_in, out_shape, scratch_shapes):
    hbm = pl.BlockSpec(memory_space=pl.ANY)
    return pl.pallas_call(body, name=name, in_specs=[hbm] * n_in, out_specs=[hbm] * len(out_shape),
                          out_shape=out_shape, scratch_shapes=scratch_shapes)


_HBM = pl.BlockSpec(memory_space=pltpu.HBM)
_SEM = pl.BlockSpec(memory_space=pltpu.SEMAPHORE)
_EFFECT = pltpu.SideEffectType.DATAFLOW_SIDE_EFFECTING


def _exchange_copy(bufs, send, recv, j, peer, place, kind):
    x, y, c = place
    target = peer
    if kind == "gather":
        src = dst = bufs[0].at[2 * x + y]
    elif kind == "scatter":
        src, dst = bufs[0].at[2 * peer[0] + peer[1]], bufs[1].at[j]
    else:
        half = bufs[0].shape[1] // 2
        rows = pl.ds(c * half, half)
        if kind == "half":
            src = dst = bufs[0].at[2 * x + y, rows]
        else:
            src = dst = bufs[0].at[2 * peer[0] + peer[1], rows]
            target = (x, y, 1 - c)
    return pltpu.make_async_remote_copy(src_ref=src, dst_ref=dst, send_sem=send.at[j], recv_sem=recv.at[j],
                                        device_id=target, device_id_type=MESH)


def _exchange_start(groups, after, kind, name):
    sizes = [len(g) for g in groups]
    flat = [b for g in groups for b in g]
    ng, nb = len(groups), len(flat)

    def body(*refs):
        bufs, sems = refs[:nb], refs[nb + 1:nb + 1 + 2 * ng]
        x, y, c = _place()
        for j, peer in enumerate(_chip_peers(x, y, c)):
            at = 0
            for i, size in enumerate(sizes):
                _exchange_copy(bufs[at:at + size], sems[2 * i], sems[2 * i + 1], j, peer, (x, y, c), kind).start()
                at += size

    any_space = pl.BlockSpec(memory_space=pl.ANY)
    out = pl.pallas_call(
        body, name=name, in_specs=[_HBM] * nb + [any_space],
        out_specs=[_SEM] * (2 * ng) + [_HBM] * nb + [any_space],
        out_shape=[pltpu.SemaphoreType.DMA((3,))] * (2 * ng) + [pltpu.HBM(b.shape, b.dtype) for b in flat]
        + [_sds(after.shape, after.dtype)],
        input_output_aliases={i: 2 * ng + i for i in range(nb + 1)},
        compiler_params=pltpu.CompilerParams(has_side_effects=_EFFECT),
    )(*[pltpu.with_memory_space_constraint(b, pltpu.HBM) for b in flat], after)
    started, at = [], 2 * ng
    for i, size in enumerate(sizes):
        started.append((out[2 * i], out[2 * i + 1], tuple(out[at:at + size])))
        at += size
    return started, out[-1]


def _exchange_wait(started, after, kind, name):
    send, recv, bufs = started
    nb = len(bufs)

    def body(*refs):
        x, y, c = _place()
        for j, peer in enumerate(_chip_peers(x, y, c)):
            cp = _exchange_copy(refs[:nb], refs[nb], refs[nb + 1], j, peer, (x, y, c), kind)
            cp.wait_send()
            cp.wait_recv()

    return pl.pallas_call(
        body, name=name, in_specs=[_HBM] * nb + [_SEM, _SEM, pl.BlockSpec(memory_space=pl.ANY)],
        out_specs=[_HBM] * nb, out_shape=[pltpu.HBM(b.shape, b.dtype) for b in bufs],
        input_output_aliases={i: i for i in range(nb)},
        compiler_params=pltpu.CompilerParams(has_side_effects=_EFFECT),
    )(*bufs, send, recv, after)


def _sibling_copies(v_refs, l_refs, send, recv):
    x, y, c = _place()
    return [pltpu.make_async_remote_copy(src_ref=v, dst_ref=l, send_sem=send.at[a], recv_sem=recv.at[a],
                                         device_id=(x, y, 1 - c), device_id_type=MESH)
            for a, (v, l) in enumerate(zip(v_refs, l_refs))]


def _sibling_start(vs, after, name):
    vs = list(vs)
    n = len(vs)
    lands = [lax.empty(v.shape, v.dtype) for v in vs]

    def body(*refs):
        for cp in _sibling_copies(refs[:n], refs[n:2 * n], refs[2 * n + 1], refs[2 * n + 2]):
            cp.start()

    any_space = pl.BlockSpec(memory_space=pl.ANY)
    out = pl.pallas_call(
        body, name=name, in_specs=[_HBM] * (2 * n) + [any_space],
        out_specs=[_SEM, _SEM] + [_HBM] * (2 * n) + [any_space],
        out_shape=[pltpu.SemaphoreType.DMA((n,))] * 2 + [pltpu.HBM(b.shape, b.dtype) for b in vs + lands]
        + [_sds(after.shape, after.dtype)],
        input_output_aliases={i: 2 + i for i in range(2 * n + 1)},
        compiler_params=pltpu.CompilerParams(has_side_effects=_EFFECT),
    )(*[pltpu.with_memory_space_constraint(b, pltpu.HBM) for b in vs + lands], after)
    return (out[0], out[1], tuple(out[2:2 + n]), tuple(out[2 + n:2 + 2 * n])), out[-1]


def _sibling_wait(started, after, name):
    send, recv, vs, lands = started
    n = len(vs)

    def body(*refs):
        for cp in _sibling_copies(refs[:n], refs[n:2 * n], refs[2 * n], refs[2 * n + 1]):
            cp.wait_send()
            cp.wait_recv()

    out = pl.pallas_call(
        body, name=name, in_specs=[_HBM] * (2 * n) + [_SEM, _SEM, pl.BlockSpec(memory_space=pl.ANY)],
        out_specs=[_HBM] * (2 * n), out_shape=[pltpu.HBM(b.shape, b.dtype) for b in vs + lands],
        input_output_aliases={i: i for i in range(2 * n)},
        compiler_params=pltpu.CompilerParams(has_side_effects=_EFFECT),
    )(*vs, *lands, send, recv, after)
    return out[:n], out[n:]


def _swap_sibling(vs, name):
    n = len(vs)

    def body(*refs):
        v_refs, o_refs, (send, recv) = refs[:n], refs[n:2 * n], refs[2 * n:]
        x, y, c = _place()
        cps = [pltpu.make_async_remote_copy(
            src_ref=v_refs[a], dst_ref=o_refs[a], send_sem=send.at[a], recv_sem=recv.at[a],
            device_id=(x, y, 1 - c), device_id_type=MESH) for a in range(n)]
        for cp in cps:
            cp.start()
        for cp in cps:
            cp.wait()

    return _comm_call(body, name, n, [_sds(v.shape, v.dtype) for v in vs],
                      [pltpu.SemaphoreType.DMA((n,)), pltpu.SemaphoreType.DMA((n,))])(*vs)


def _everyone(x, y, c):
    return [(1 - x if k & 4 else x, 1 - y if k & 2 else y, 1 - c if k & 1 else c) for k in range(1, 8)]


def _all_gather_copies(land_ref, send, recv, arriving):
    x, y, c = _place()
    me = 4 * x + 2 * y + c
    return [pltpu.make_async_remote_copy(
        src_ref=land_ref.at[me], dst_ref=land_ref.at[4 * p[0] + 2 * p[1] + p[2] if arriving else me],
        send_sem=send.at[k], recv_sem=recv.at[k], device_id=p, device_id_type=MESH)
        for k, p in enumerate(_everyone(x, y, c))]


def _all_gather_start(v, name):
    x, y, c = _place()
    land = lax.dynamic_update_slice(lax.empty((8,) + v.shape, v.dtype), v[None], (4 * x + 2 * y + c, 0, 0))

    def body(land_ref, v_ref, send, recv, land_out, v_out):
        for cp in _all_gather_copies(land_ref, send, recv, False):
            cp.start()

    any_space = pl.BlockSpec(memory_space=pl.ANY)
    out = pl.pallas_call(
        body, name=name, in_specs=[_HBM, any_space], out_specs=[_SEM, _SEM, _HBM, any_space],
        out_shape=[pltpu.SemaphoreType.DMA((7,))] * 2 + [pltpu.HBM(land.shape, land.dtype), _sds(v.shape, v.dtype)],
        input_output_aliases={0: 2, 1: 3}, compiler_params=pltpu.CompilerParams(has_side_effects=_EFFECT),
    )(pltpu.with_memory_space_constraint(land, pltpu.HBM), v)
    return tuple(out[:3]), out[3]


def _all_gather_wait(started, after, name):
    send, recv, land = started

    def body(land_ref, send, recv, after_ref, land_out):
        for cp in _all_gather_copies(land_ref, send, recv, True):
            cp.wait_send()
            cp.wait_recv()

    return pl.pallas_call(
        body, name=name, in_specs=[_HBM, _SEM, _SEM, pl.BlockSpec(memory_space=pl.ANY)], out_specs=_HBM,
        out_shape=pltpu.HBM(land.shape, land.dtype), input_output_aliases={0: 0},
        compiler_params=pltpu.CompilerParams(has_side_effects=_EFFECT),
    )(land, send, recv, after)


def _cast_place(ws, shard, name):
    n = len(ws)

    def body(s_ref, *refs):
        for w_ref, o_ref in zip(refs[:n], refs[n:]):
            o_ref[0] = w_ref[...].astype(BF16)

    return pl.pallas_call(
        body, name=name, out_shape=[_sds((N_SHARD,) + w.shape, BF16) for w in ws],
        grid_spec=pltpu.PrefetchScalarGridSpec(
            num_scalar_prefetch=1, grid=(4,),
            in_specs=[pl.BlockSpec((w.shape[0] // 4, w.shape[1]), lambda i, s: (i, 0)) for w in ws],
            out_specs=[pl.BlockSpec((1, w.shape[0] // 4, w.shape[1]), lambda i, s: (s[0], i, 0)) for w in ws]),
        compiler_params=pltpu.CompilerParams(dimension_semantics=("arbitrary",), vmem_limit_bytes=VMEM_LIMIT),
    )(shard.reshape(1).astype(jnp.int32), *ws)


def _mod_rows(c8, w_ada, b_ada, name):
    n = w_ada.shape[1]

    def gather(src_ref, dst_ref, send, recv, loc, base):
        x, y, c = _place()
        me = 4 * x + 2 * y + c
        own = pltpu.make_async_copy(src_ref, dst_ref.at[me], loc)
        own.start()
        peers = _everyone(x, y, c)
        sends = [pltpu.make_async_remote_copy(src_ref=src_ref, dst_ref=dst_ref.at[me], send_sem=send.at[base + k],
                                              recv_sem=recv.at[base + k], device_id=p, device_id_type=MESH)
                 for k, p in enumerate(peers)]
        for cp in sends:
            cp.start()
        for k, p in enumerate(peers):
            pltpu.make_async_remote_copy(src_ref=src_ref, dst_ref=dst_ref.at[4 * p[0] + 2 * p[1] + p[2]],
                                         send_sem=send.at[base + k], recv_sem=recv.at[base + k], device_id=p,
                                         device_id_type=MESH).wait_recv()
        for cp in sends:
            cp.wait_send()
        own.wait()

    def body(c_ref, w_ref, b_ref, a_ref, parts_ref, c_all, part, send, recv, loc):
        gather(c_ref, c_all, send, recv, loc.at[0], 0)
        cv = jnp.max(c_all[...], axis=1)
        ca = cv * _sigmoid(cv)
        a_ref[...] = ca
        part[...] = jnp.dot(ca, w_ref[...], precision=lax.Precision.HIGHEST, preferred_element_type=F32) + b_ref[...]
        gather(part, parts_ref, send, recv, loc.at[1], 7)

    vmem = pl.BlockSpec(memory_space=pltpu.VMEM)
    return pl.pallas_call(
        body, name=name, in_specs=[vmem] * 3, out_specs=[vmem, vmem],
        out_shape=[_sds((8, D_MODEL)), _sds((8, 8, n))],
        scratch_shapes=[pltpu.VMEM((8, 8, D_MODEL), F32), pltpu.VMEM((8, n), F32), pltpu.SemaphoreType.DMA((14,)),
                        pltpu.SemaphoreType.DMA((14,)), pltpu.SemaphoreType.DMA((2,))],
        compiler_params=pltpu.CompilerParams(vmem_limit_bytes=VMEM_LIMIT))(c8, w_ada, b_ada)


def _sum_received(gs, shard, lands, name):
    n = len(gs)

    def body(s_ref, *refs):
        for g_ref, l_ref, o_ref in zip(refs[:n], refs[n:2 * n], refs[2 * n:]):
            o_ref[...] = ((g_ref[0] + l_ref[0].astype(F32)) + l_ref[1].astype(F32)) + l_ref[2].astype(F32)

    quarter = lambda g: (g.shape[1] // 4, g.shape[2])
    return pl.pallas_call(
        body, name=name, out_shape=[_sds(g.shape[1:]) for g in gs],
        grid_spec=pltpu.PrefetchScalarGridSpec(
            num_scalar_prefetch=1, grid=(4,),
            in_specs=[pl.BlockSpec((1,) + quarter(g), lambda i, s: (s[0], i, 0)) for g in gs]
            + [pl.BlockSpec((3,) + quarter(g), lambda i, s: (0, i, 0)) for g in gs],
            out_specs=[pl.BlockSpec(quarter(g), lambda i, s: (i, 0)) for g in gs]),
        compiler_params=pltpu.CompilerParams(dimension_semantics=("arbitrary",), vmem_limit_bytes=VMEM_LIMIT),
    )(shard.reshape(1).astype(jnp.int32), *gs, *lands)


def _adamw_outer(w, ct, dm, m, v, name):
    k, n = w.shape
    tr = k // 4

    def body(w_ref, c_ref, d_ref, m_ref, v_ref, g_out, d_out, m_out, v_out):
        cv = c_ref[...]
        dv = d_ref[...]
        g = cv[:, 0:1] * dv[0:1, :]
        for i in range(1, 8):
            g = g + cv[:, i:i + 1] * dv[i:i + 1, :]
        g_out[...] = g
        d_out[...], m_out[...], v_out[...] = _adamw_math(w_ref[...], g, m_ref[...], v_ref[...])

    row = _rows(tr, n)
    return _call(body, name=name, grid=(4,),
                 in_specs=[row, _rows(tr, 8), pl.BlockSpec((8, n), lambda i: (0, 0)), row, row],
                 out_specs=[row] * 4, out_shape=[_sds((k, n))] * 4)(w, ct, dm, m, v)


def _adamw_math(w, g, m, v):
    m_new = ADAM_B1 * m + (1.0 - ADAM_B1) * g
    v_new = ADAM_B2 * v + (1.0 - ADAM_B2) * (g * g)
    m_hat = m_new / (1.0 - ADAM_B1 ** ADAM_STEP)
    v_hat = v_new / (1.0 - ADAM_B2 ** ADAM_STEP)
    return -ADAM_LR * (m_hat / (jnp.sqrt(v_hat) + ADAM_EPS) + ADAM_WD * w), m_new, v_new


def _small_update(stats, smalls, name):
    offsets = [ST_DMOD, ST_DG1, ST_DLB, ST_DOG, ST_DAG, ST_DG2, ST_DFG]
    lb_index = 2

    def body(*refs):
        s_ref, ins, l_ref, outs = refs[0], refs[1:22], refs[22], refs[23:]
        tot = s_ref[0:1, :]
        for i in range(1, 8):
            tot = tot + s_ref[i:i + 1, :]
        l_ref[...] = jnp.zeros((1, 128), F32) + (0.5 / D_MODEL) * jnp.sum(tot[:, ST_LOSS:ST_LOSS + D_MODEL])
        for p, off in enumerate(offsets):
            w_ref, m_ref, v_ref = ins[3 * p:3 * p + 3]
            g_out, d_out, m_out, v_out = outs[4 * p:4 * p + 4]
            g = tot[:, off:off + w_ref.shape[1]]
            if p == lb_index:
                lg = w_ref[...]
                lb = _sigmoid(lg[0:1] - lg[1:2])
                g = g * lb * (1.0 - lb)
            for r in range(w_ref.shape[0]):
                rows = slice(r, r + 1)
                gr = g if r == 0 else -g
                delta, m_new, v_new = _adamw_math(w_ref[rows, :], gr, m_ref[rows, :], v_ref[rows, :])
                g_out[rows, :] = gr
                d_out[rows, :] = delta
                m_out[rows, :] = m_new
                v_out[rows, :] = v_new

    full = lambda a: pl.BlockSpec(a.shape, lambda i: (0, 0))
    flat = [a for t in smalls for a in t]
    return _call(body, name=name, grid=(1,),
                 in_specs=[full(stats)] + [full(a) for a in flat],
                 out_specs=[pl.BlockSpec((1, 128), lambda i: (0, 0))] + [full(t[0]) for t in smalls for _ in range(4)],
                 out_shape=[_sds((1, 128))] + [_sds(t[0].shape) for t in smalls for _ in range(4)])(stats, *flat)


def _adamw(params, name):
    n = len(params)

    def body(*refs):
        for p in range(n):
            w_ref, ga_ref, gb_ref, m_ref, v_ref = refs[5 * p:5 * p + 5]
            g_out, d_out, m_out, v_out = refs[5 * n + 4 * p:5 * n + 4 * p + 4]
            g = ga_ref[...] + gb_ref[...]
            g_out[...] = g
            d_out[...], m_out[...], v_out[...] = _adamw_math(w_ref[...], g, m_ref[...], v_ref[...])

    row = lambda w: _rows(w.shape[0] // 4, w.shape[1])
    out = _call(body, name=name, grid=(4,), in_specs=[row(p[0]) for p in params for _ in range(5)],
                out_specs=[row(p[0]) for p in params for _ in range(4)],
                out_shape=[_sds(p[0].shape) for p in params for _ in range(4)])(*[a for p in params for a in p])
    return [tuple(out[4 * p:4 * p + 4]) for p in range(n)]


def kernel(x, c, w_ada, b_ada, norm1_g, w_in, hg_lb_logits, hg_onorm_g, att_onorm_g, w_out, norm2_g, w_gate_up, w_down, final_g, loss_target, m_w_ada, m_b_ada, m_norm1_g, m_w_in, m_hg_lb_logits, m_hg_onorm_g, m_att_onorm_g, m_w_out, m_norm2_g, m_w_gate_up, m_w_down, m_final_g, v_w_ada, v_b_ada, v_norm1_g, v_w_in, v_hg_lb_logits, v_hg_onorm_g, v_att_onorm_g, v_w_out, v_norm2_g, v_w_gate_up, v_w_down, v_final_g):
    ix, iy, ic = _place()
    shard = 2 * ix + iy
    sample = 4 * ix + 2 * iy + ic
    n_ada = w_ada.shape[2]

    shards = [w_in[0], w_out[0], w_gate_up[0], w_down[0]]
    names = ["w_in", "w_out", "w_gu", "w_down"]
    shapes = [(N_SHARD,) + w.shape for w in shards]
    placed = [(_cast_place(shards[:1], shard, "place_w_in")[0],)]
    placed += [(p,) for p in _cast_place(shards[1:], shard, "place_rest")]

    b_part = lax.dynamic_slice(b_ada, (0, shard * n_ada), (1, n_ada))
    c_act, parts = _mod_rows(jnp.broadcast_to(c, (8, D_MODEL)), w_ada[0], b_part, "mod_rows")
    parts = parts[::2]
    mod = lax.dynamic_index_in_dim(parts, sample, axis=1, keepdims=False).reshape(1, 6 * D_MODEL)
    (first,), mod = _exchange_start(placed[:1], mod, "half", "gather_start_w_in")
    gathering = {}

    def get_w(name, after):
        if name == "w_in":
            halves = _exchange_wait(first, after, "half", "gather_wait_w_in")
            (passing,), token = _exchange_start([tuple(halves)], mod, "forward", "forward_start_w_in")
            (full,) = _exchange_wait(passing, token, "forward", "forward_wait_w_in")
            rest, full = _exchange_start(placed[1:], full, "gather", "gather_start_rest")
            gathering.update(zip(names[1:], rest))
            return full
        (full,) = _exchange_wait(gathering[name], after, "gather", "gather_wait_" + name)
        return full if name == "w_gu" else full.reshape(1, -1, D_MODEL)

    scattering = {}

    def put_g(name, g, g_bf16, then):
        shape = shapes[names.index(name)]
        land = lax.empty((3,) + shape[1:], BF16)
        (started,), then = _exchange_start([(g_bf16.reshape(shape), land)], then, "scatter", "scatter_start_" + name)
        scattering[name] = (g.reshape(shape), started)
        return then

    def summed(group, after, tag):
        lands = [_exchange_wait(scattering[nm][1], after, "scatter", "scatter_wait_" + nm)[1] for nm in group]
        return _sum_received([scattering[nm][0] for nm in group], shard, lands, "sum_" + tag)

    early = ["w_down", "w_gu", "w_out"]
    swapping = []

    def late(a):
        started, a = _sibling_start(summed(early, a, "early"), a, "swap_start")
        swapping.append(started)
        return a

    def project(h1):
        own = _mm_own_shard(h1, shards[0], shard, N_SHARD, "mm_in_own")
        w_full = get_w("w_in", own)
        return _mm_other_shards(h1, w_full, own, shard, "mm_in_rest"), w_full

    dx, stats = _local_step(x[0], loss_target[0], mod, norm1_g, hg_lb_logits, hg_onorm_g, att_onorm_g,
                            norm2_g, final_g, get_w, put_g, late, project)

    gathering_stats, stats = _all_gather_start(stats, "stats_start")
    moments = [(m_w_in, v_w_in), (m_w_out, v_w_out), (m_w_gate_up, v_w_gate_up), (m_w_down, v_w_down)]

    def update(group, sums, other, tag):
        params = [(shards[names.index(nm)], s, o, moments[names.index(nm)][0][0], moments[names.index(nm)][1][0])
                  for nm, s, o in zip(group, sums, other)]
        return dict(zip(group, _adamw(params, "adamw_" + tag)))

    sums, other = _sibling_wait(swapping[0], stats, "swap_wait")
    done = update(early, sums, other, "early")
    sum_in = summed(["w_in"], done["w_out"][1], "w_in")
    done.update(update(["w_in"], sum_in, _swap_sibling(sum_in, "swap_sum_in"), "w_in"))

    stats_all = _all_gather_wait(gathering_stats, done["w_in"][1], "stats_wait").reshape(8, ST_WIDTH)
    dmod = lax.dynamic_slice(stats_all, (0, ST_DMOD + shard * n_ada), (8, n_ada))

    as_row = lambda a: a.reshape(1, -1) if a.ndim == 1 else a
    smalls = [tuple(as_row(a) for a in t) for t in [
        (b_ada, m_b_ada, v_b_ada), (norm1_g, m_norm1_g, v_norm1_g),
        (hg_lb_logits, m_hg_lb_logits, v_hg_lb_logits), (hg_onorm_g, m_hg_onorm_g, v_hg_onorm_g),
        (att_onorm_g, m_att_onorm_g, v_att_onorm_g), (norm2_g, m_norm2_g, v_norm2_g),
        (final_g, m_final_g, v_final_g)]]
    loss, *small_out = _small_update(stats_all, smalls, "small_update")
    shapes_out = [b_ada.shape, norm1_g.shape, hg_lb_logits.shape, hg_onorm_g.shape, att_onorm_g.shape,
                  norm2_g.shape, final_g.shape]
    sg, sd, sm, sv = [[small_out[4 * p + i].reshape(shapes_out[p]) for p in range(7)] for i in range(4)]

    ada = _adamw_outer(w_ada[0], c_act.T, dmod, m_w_ada[0], v_w_ada[0], "adamw_w_ada")
    big = [ada] + [done[nm] for nm in names]
    bg, bd, bm, bv = [[t[i][None] for t in big] for i in range(4)]

    def order(b, s):
        return [b[0], s[0], s[1], b[1], s[2], s[3], s[4], b[2], s[5], b[3], b[4], s[6]]

    return (loss[0, 0], dx[None], *order(bg, sg), *order(bd, sd), *order(bm, sm), *order(bv, sv))
```

```python
import functools

import jax
import jax.numpy as jnp
from jax import lax
from jax.experimental import pallas as pl
from jax.experimental.pallas import tpu as pltpu

F32 = jnp.float32
BF16 = jnp.bfloat16
MESH = pl.DeviceIdType.MESH

D_MODEL = 1024
HG_WIDTH = 512
HG_HEAD = 128
HG_CHUNK = 64
HG_GROUP = 4
ATT_WIDTH = 512
ATT_HEADS = 8
ATT_BLOCK = 128
DILATIONS = (1, 4, 16)
D_FF = 2816
IN_WIDTH = 3584
N_SHARD = 4
RMS_EPS = 1e-6
NEG = -1e30

ADAM_LR = 0.001
ADAM_B1 = 0.9
ADAM_B2 = 0.999
ADAM_EPS = 1e-08
ADAM_WD = 0.01
ADAM_STEP = 10

VMEM_LIMIT = 56 * 2**20

ST_LOSS, ST_DFG, ST_DG2, ST_DG1 = 0, 1024, 2048, 3072
ST_DLB, ST_DAG, ST_DOG, ST_DMOD = 4096, 4608, 5120, 5248
ST_WIDTH = 5248 + 6144


def _call(body, *, name, grid, in_specs, out_specs, out_shape, scratch_shapes=(), aliases=None):
    return pl.pallas_call(
        body, name=name, grid=grid, in_specs=in_specs, out_specs=out_specs, out_shape=out_shape,
        scratch_shapes=list(scratch_shapes), input_output_aliases=aliases or {},
        compiler_params=pltpu.CompilerParams(
            dimension_semantics=("arbitrary",) * len(grid), vmem_limit_bytes=VMEM_LIMIT))


def _sds(shape, dtype=F32):
    return jax.ShapeDtypeStruct(shape, dtype)


def _dot(a, b):
    return jnp.dot(a, b, preferred_element_type=F32)


def _dot_nt(a, b):
    return lax.dot_general(a, b, (((1,), (1,)), ((), ())), preferred_element_type=F32)


def _dot_tn(a, b):
    return lax.dot_general(a, b, (((0,), (0,)), ((), ())), preferred_element_type=F32)


def _sigmoid(x):
    return 1.0 / (1.0 + jnp.exp(-x))


def _rows(tr, width):
    return pl.BlockSpec((tr, width), lambda i: (i, 0))


def _vec(width):
    return pl.BlockSpec((1, width), lambda i: (0, 0))


def _acc(ref, val, first):
    @pl.when(first)
    def _():
        ref[...] = val

    @pl.when(jnp.logical_not(first))
    def _():
        ref[...] += val


def _sub_rows(tr, fine, width):
    return pl.BlockSpec((fine, tr // fine, width), lambda i: (0, i, 0))


def _regroup_matrix(tr, groups):
    a = lax.broadcasted_iota(jnp.int32, (tr, tr), 0)
    b = lax.broadcasted_iota(jnp.int32, (tr, tr), 1)
    return (b == (a % groups) * (tr // groups) + a // groups).astype(BF16)


def _regroup(m, v, lanes=None):
    if v.dtype == BF16:
        return _dot(m, v)
    width = v.shape[1]
    packed, out = None, None
    for i in range(3):
        part = v.astype(BF16).astype(F32)
        v = v - part
        if lanes is None:
            out = _dot(m, part.astype(BF16)) if i == 0 else out + _dot(m, part.astype(BF16))
        else:
            packed = part if i == 0 else packed + pltpu.roll(part, i * lanes, 1)
    if lanes is None:
        return out
    out = _dot(m, packed.astype(BF16))
    out = out + pltpu.roll(out, width - lanes, 1) + pltpu.roll(out, width - 2 * lanes, 1)
    return jnp.where(lax.broadcasted_iota(jnp.int32, (1, width), 1) < lanes, out, 0.0)


def _mm_nn(a, b3, name, tm=1024, out_dtype=F32):
    m, k = a.shape
    s, _, n = b3.shape

    def body(a_ref, b_ref, o_ref):
        o_ref[...] = _dot(a_ref[...], b_ref[0]).astype(out_dtype)

    return _call(
        body, name=name, grid=(s, m // tm),
        in_specs=[pl.BlockSpec((tm, k), lambda j, i: (i, 0)), pl.BlockSpec((1, k, n), lambda j, i: (j, 0, 0))],
        out_specs=pl.BlockSpec((tm, n), lambda j, i: (i, j)), out_shape=_sds((m, s * n), out_dtype))(a, b3)


def _mm_own_shard(a, w, shard, s, name, tm=1024):
    m, k = a.shape
    n = w.shape[1]

    def body(s_ref, a_ref, w_ref, o_ref):
        o_ref[...] = _dot(a_ref[...], w_ref[...].astype(BF16)).astype(BF16)

    return pl.pallas_call(
        body, name=name, out_shape=_sds((m, s * n), BF16),
        grid_spec=pltpu.PrefetchScalarGridSpec(
            num_scalar_prefetch=1, grid=(m // tm,),
            in_specs=[pl.BlockSpec((tm, k), lambda i, sh: (i, 0)), pl.BlockSpec((k, n), lambda i, sh: (0, 0))],
            out_specs=pl.BlockSpec((tm, n), lambda i, sh: (i, sh[0]))),
        compiler_params=pltpu.CompilerParams(dimension_semantics=("arbitrary",), vmem_limit_bytes=VMEM_LIMIT),
    )(shard.reshape(1).astype(jnp.int32), a, w)


def _mm_other_shards(a, b3, partial, shard, name, tm=1024):
    m, k = a.shape
    s, _, n = b3.shape
    which = lambda j, sh: (sh[0] + 1 + j) % s

    def body(s_ref, a_ref, b_ref, p_ref, o_ref):
        o_ref[...] = _dot(a_ref[...], b_ref[0]).astype(BF16)

    return pl.pallas_call(
        body, name=name, out_shape=_sds(partial.shape, BF16),
        grid_spec=pltpu.PrefetchScalarGridSpec(
            num_scalar_prefetch=1, grid=(s - 1, m // tm),
            in_specs=[pl.BlockSpec((tm, k), lambda j, i, sh: (i, 0)),
                      pl.BlockSpec((1, k, n), lambda j, i, sh: (which(j, sh), 0, 0)),
                      pl.BlockSpec(memory_space=pl.ANY)],
            out_specs=pl.BlockSpec((tm, n), lambda j, i, sh: (i, which(j, sh)))),
        input_output_aliases={3: 0},
        compiler_params=pltpu.CompilerParams(dimension_semantics=("arbitrary",) * 2, vmem_limit_bytes=VMEM_LIMIT),
    )(shard.reshape(1).astype(jnp.int32), a, b3, partial)


def _mm_tn(a, dy, s, name, tm, tk, group=1):
    m, k = a.shape
    n = dy.shape[1] // s
    steps = m // tm

    def body(a_ref, dy_ref, o_ref, ob_ref):
        p = _dot_tn(a_ref[...], dy_ref[...])
        for g in range(group):
            pg = p[:, g * n:(g + 1) * n]
            if steps == 1:
                o_ref[g] = pg
                ob_ref[g] = pg.astype(BF16)
            else:
                _acc(o_ref.at[g], pg, pl.program_id(2) == 0)
        if steps > 1:
            @pl.when(pl.program_id(2) == steps - 1)
            def _():
                ob_ref[...] = o_ref[...].astype(BF16)

    out = pl.BlockSpec((group, tk, n), lambda kk, j, i: (j, kk, 0))
    return _call(
        body, name=name, grid=(k // tk, s // group, steps),
        in_specs=[pl.BlockSpec((tm, tk), lambda kk, j, i: (i, kk)),
                  pl.BlockSpec((tm, group * n), lambda kk, j, i: (i, j))],
        out_specs=[out, out], out_shape=[_sds((s, k, n)), _sds((s, k, n), BF16)])(a, dy)


def _rms(x):
    return lax.rsqrt(jnp.mean(x * x, axis=-1, keepdims=True) + RMS_EPS)


def _rms_bwd(dxh, xh, r):
    return r * (dxh - xh * jnp.mean(dxh * xh, axis=-1, keepdims=True))


def _norm_mod(x, g, scale, shift, name, tr=512):
    t = x.shape[0]

    def body(x_ref, g_ref, sc_ref, sh_ref, h_ref):
        xv = x_ref[...]
        n = xv * _rms(xv) * g_ref[...]
        h_ref[...] = (n * (1.0 + sc_ref[...]) + sh_ref[...]).astype(BF16)

    return _call(body, name=name, grid=(t // tr,),
                 in_specs=[_rows(tr, D_MODEL), _vec(D_MODEL), _vec(D_MODEL), _vec(D_MODEL)],
                 out_specs=_rows(tr, D_MODEL), out_shape=_sds((t, D_MODEL), BF16))(x, g, scale, shift)


def _out_resid_norm_mod(x, mixin, w_out, gate, g, scale, shift, name, tr=512):
    t = x.shape[0]

    def body(x_ref, mi_ref, w_ref, gt_ref, g_ref, sc_ref, sh_ref, m_ref, x2_ref, h_ref):
        mix = _dot(mi_ref[...], w_ref[0])
        m_ref[...] = mix
        x2 = x_ref[...] + gt_ref[...] * mix
        x2_ref[...] = x2
        n = x2 * _rms(x2) * g_ref[...]
        h_ref[...] = (n * (1.0 + sc_ref[...]) + sh_ref[...]).astype(BF16)

    row = _rows(tr, D_MODEL)
    return _call(body, name=name, grid=(t // tr,),
                 in_specs=[row, row, pl.BlockSpec(w_out.shape, lambda i: (0, 0, 0))] + [_vec(D_MODEL)] * 4,
                 out_specs=[row, row, row],
                 out_shape=[_sds((t, D_MODEL)), _sds((t, D_MODEL)), _sds((t, D_MODEL), BF16)])(
                     x, mixin, w_out, gate, g, scale, shift)


def _mm_gate_up(h, w_gu, name, tm=1024):
    m, k = h.shape
    n = w_gu.shape[2]

    def body(h_ref, wa_ref, wu_ref, da_ref, du_ref, o_ref, w_au):
        @pl.when(pl.program_id(1) == 0)
        def _():
            w_au[:, :n] = wa_ref[0]
            w_au[:, n:] = wu_ref[0]

        au = _dot(h_ref[...], w_au[...])
        a, u = au[:, :n], au[:, n:]
        sg = _sigmoid(a)
        silu = a * sg
        da_ref[...] = (u * sg * (1.0 + a * (1.0 - sg))).astype(BF16)
        du_ref[...] = silu.astype(BF16)
        o_ref[...] = (silu * u).astype(BF16)

    out = pl.BlockSpec((tm, n), lambda j, i: (i, j))
    return _call(body, name=name, grid=(2, m // tm),
                 in_specs=[pl.BlockSpec((tm, k), lambda j, i: (i, 0)), pl.BlockSpec((1, k, n), lambda j, i: (j, 0, 0)),
                           pl.BlockSpec((1, k, n), lambda j, i: (j + 2, 0, 0))],
                 out_specs=[out, out, out], out_shape=[_sds((m, 2 * n), BF16)] * 3,
                 scratch_shapes=[pltpu.VMEM((k, 2 * n), BF16)])(h, w_gu, w_gu)


def _mm_down_dx(dffn, w_down, act_da, act_du, name, tm=512):
    m = dffn.shape[0]
    _, k, n = w_down.shape

    def body(d_ref, w_ref, da_ref, du_ref, o_ref):
        dact = _dot_nt(d_ref[...], w_ref[0])
        o_ref[:, :k] = (dact * da_ref[...].astype(F32)).astype(BF16)
        o_ref[:, k:] = (dact * du_ref[...].astype(F32)).astype(BF16)

    return _call(body, name=name, grid=(m // tm,),
                 in_specs=[_rows(tm, n), pl.BlockSpec((1, k, n), lambda i: (0, 0, 0)), _rows(tm, k), _rows(tm, k)],
                 out_specs=_rows(tm, 2 * k), out_shape=_sds((m, 2 * k), BF16))(dffn, w_down, act_da, act_du)


def _down_loss(x2, act, w_down, gate, fg, tgt, name, tr=512):
    t = x2.shape[0]
    _, k, n = w_down.shape

    def body(x_ref, a_ref, w_ref, gt_ref, fg_ref, t_ref, dx_ref, df_ref, l_ref, dfg_ref, dgt_ref):
        first = pl.program_id(0) == 0
        ffn_v = _dot(a_ref[...], w_ref[0])
        x3 = x_ref[...] + gt_ref[...] * ffn_v
        r = _rms(x3)
        xh = x3 * r
        err = xh * fg_ref[...] - t_ref[...]
        dy = err * (1.0 / D_MODEL)
        dx3 = _rms_bwd(dy * fg_ref[...], xh, r)
        dx_ref[...] = dx3
        df_ref[...] = (dx3 * gt_ref[...]).astype(BF16)
        _acc(l_ref, jnp.sum(err * err, axis=0, keepdims=True), first)
        _acc(dfg_ref, jnp.sum(dy * xh, axis=0, keepdims=True), first)
        _acc(dgt_ref, jnp.sum(dx3 * ffn_v, axis=0, keepdims=True), first)

    row, vec = _rows(tr, D_MODEL), _vec(D_MODEL)
    return _call(body, name=name, grid=(t // tr,),
                 in_specs=[row, _rows(tr, k), pl.BlockSpec((1, k, n), lambda i: (0, 0, 0)), vec, vec, row],
                 out_specs=[row, row, vec, vec, vec],
                 out_shape=[_sds((t, D_MODEL)), _sds((t, D_MODEL), BF16)] + [_sds((1, D_MODEL))] * 3)(
                     x2, act, w_down, gate, fg, tgt)


def _norm_mod_bwd(dh, x, g, scale, dres, name, gate=None, mix=None, w=None, tr=512):
    t = x.shape[0]
    below = gate is not None

    def body(*refs):
        if w is not None:
            w_ref, w_full, sem, refs = refs[1], refs[-2], refs[-1], refs[:1] + refs[2:-2]

            @pl.when(pl.program_id(0) == 0)
            def _():
                n = w.shape[2]
                copies = [pltpu.make_async_copy(w_ref.at[j], w_full.at[:, pl.ds(j * n, n)], sem.at[j])
                          for j in range(w.shape[0])]
                for cp in copies:
                    cp.start()
                for cp in copies:
                    cp.wait()

        if below:
            dh_ref, x_ref, g_ref, sc_ref, dr_ref, gt_ref, m_ref, dx_ref, dsh_ref, dsc_ref, dg_ref, dgt_ref, dm_ref = refs
        else:
            dh_ref, x_ref, g_ref, sc_ref, dr_ref, dx_ref, dsh_ref, dsc_ref, dg_ref = refs
        first = pl.program_id(0) == 0
        xv = x_ref[...]
        if w is None:
            dhv = dh_ref[...].astype(F32)
        else:
            dhv = _dot_nt(dh_ref[...], w_full[...])
        r = _rms(xv)
        xh = xv * r
        dn = dhv * (1.0 + sc_ref[...])
        dx = dr_ref[...] + _rms_bwd(dn * g_ref[...], xh, r)
        dx_ref[...] = dx
        _acc(dsh_ref, jnp.sum(dhv, axis=0, keepdims=True), first)
        _acc(dsc_ref, jnp.sum(dhv * xh * g_ref[...], axis=0, keepdims=True), first)
        _acc(dg_ref, jnp.sum(dn * xh, axis=0, keepdims=True), first)
        if below:
            _acc(dgt_ref, jnp.sum(dx * m_ref[...], axis=0, keepdims=True), first)
            dm_ref[...] = (dx * gt_ref[...]).astype(BF16)

    row, vec = _rows(tr, D_MODEL), _vec(D_MODEL)
    first_specs = [row] if w is None else [_rows(tr, dh.shape[1]), pl.BlockSpec(memory_space=pl.ANY)]
    scratch = [] if w is None else [pltpu.VMEM((w.shape[1], dh.shape[1]), BF16), pltpu.SemaphoreType.DMA((w.shape[0],))]
    in_specs = first_specs + [row, vec, vec, row] + ([vec, row] if below else [])
    out_specs = [row, vec, vec, vec] + ([vec, row] if below else [])
    out_shape = [_sds((t, D_MODEL))] + [_sds((1, D_MODEL))] * 3 + ([_sds((1, D_MODEL)), _sds((t, D_MODEL), BF16)] if below else [])
    args = ((dh,) if w is None else (dh, w)) + (x, g, scale, dres) + ((gate, mix) if below else ())
    return _call(body, name=name, grid=(t // tr,), in_specs=in_specs, out_specs=out_specs, out_shape=out_shape,
                 scratch_shapes=scratch)(*args)


def _mix_in_bwd(dmix, w_out, o_hg, proj, att, og, ag, fine, name, tr=512):
    t = o_hg.shape[0]

    def body(dy_ref, w_ref, o_ref, g_ref, a_ref, og_ref, ag_ref,
             do_ref, dg_ref, da_ref, dd_ref, das_ref, dds_ref, dog_ref, dag_ref, to_sub):
        first = pl.program_id(0) == 0

        @pl.when(first)
        def _():
            to_sub[...] = _regroup_matrix(tr, tr // fine)

        dmi = _dot_nt(dy_ref[...], w_ref[0])
        dog = jnp.zeros((1, HG_HEAD), F32)
        for h in range(HG_WIDTH // HG_HEAD):
            sl = slice(h * HG_HEAD, (h + 1) * HG_HEAD)
            oh = o_ref[:, sl].astype(F32)
            gv = g_ref[:, sl].astype(F32)
            dv = dmi[:, sl]
            r = _rms(oh)
            xh = oh * r
            sg = _sigmoid(gv)
            dno = dv * gv * sg
            dg_ref[:, sl] = (dv * xh * og_ref[...] * sg * (1.0 + gv * (1.0 - sg))).astype(BF16)
            dog = dog + jnp.sum(dno * xh, axis=0, keepdims=True)
            do_ref[:, sl] = _rms_bwd(dno * og_ref[...], xh, r).astype(BF16)
        _acc(dog_ref, dog, first)
        av = a_ref[...]
        dav = dmi[:, HG_WIDTH:]
        r = _rms(av)
        xa = av * r
        _acc(dag_ref, jnp.sum(dav * xa, axis=0, keepdims=True), first)
        datt = _rms_bwd(dav * ag_ref[...], xa, r)
        da_ref[...] = datt.astype(BF16)
        das_ref[...] = _regroup(to_sub[...], datt.astype(BF16)).astype(BF16).reshape(das_ref.shape)
        prod = datt * av
        lane = lax.broadcasted_iota(jnp.int32, (1, 128), 1)
        dd = jnp.zeros((tr, 128), F32)
        for hp in range(ATT_HEADS // 2):
            pp = prod[:, hp * 128:(hp + 1) * 128]
            lo = jnp.sum(jnp.where(lane < 64, pp, 0.0), axis=-1, keepdims=True)
            hi = jnp.sum(jnp.where(lane >= 64, pp, 0.0), axis=-1, keepdims=True)
            dd = jnp.where(lane == 2 * hp, lo, dd)
            dd = jnp.where(lane == 2 * hp + 1, hi, dd)
        dd_ref[...] = dd
        dds_ref[...] = _regroup(to_sub[...], dd, ATT_HEADS).reshape(dds_ref.shape)

    half = _rows(tr, HG_WIDTH)
    out = _call(body, name=name, grid=(t // tr,),
                in_specs=[_rows(tr, D_MODEL), pl.BlockSpec(w_out.shape, lambda i: (0, 0, 0)), half,
                          pl.BlockSpec((tr, HG_WIDTH), lambda i: (i, 3)), half, _vec(HG_HEAD), _vec(ATT_WIDTH)],
                out_specs=[half, pl.BlockSpec((tr, HG_WIDTH), lambda i: (i, 3)), half, _rows(tr, 128),
                           _sub_rows(tr, fine, ATT_WIDTH), _sub_rows(tr, fine, 128), _vec(HG_HEAD), _vec(ATT_WIDTH)],
                out_shape=[_sds((t, HG_WIDTH), BF16), _sds((t, IN_WIDTH), BF16), _sds((t, HG_WIDTH), BF16),
                           _sds((t, 128)), _sds((fine, t // fine, ATT_WIDTH), BF16),
                           _sds((fine, t // fine, 128)), _sds((1, HG_HEAD)), _sds((1, ATT_WIDTH))],
                scratch_shapes=[pltpu.VMEM((tr, tr), BF16)])(dmix, w_out, o_hg, proj, att, og, ag)
    out = list(out)
    return out[:4] + [out[4].reshape(t, ATT_WIDTH), out[5].reshape(t, 128)] + out[6:]


def _dproj(dproj, dqkvs, name, tr=1024):
    t = dproj.shape[0]
    nbr = len(dqkvs)
    first = IN_WIDTH // ATT_WIDTH - 3

    def body(*refs):
        refs[-1][...] = sum(r[...].astype(F32) for r in refs[:nbr]).astype(BF16)

    return _call(body, name=name, grid=(t // tr, 3),
                 in_specs=[pl.BlockSpec((tr, ATT_WIDTH), lambda i, j: (i, j))] * nbr + [pl.BlockSpec(memory_space=pl.ANY)],
                 out_specs=pl.BlockSpec((tr, ATT_WIDTH), lambda i, j: (i, first + j)),
                 out_shape=_sds(dproj.shape, BF16), aliases={nbr: 0})(*dqkvs, dproj)


def _chunk_tri(upper):
    row = lax.broadcasted_iota(jnp.int32, (HG_GROUP, HG_CHUNK, HG_CHUNK), 1)
    col = lax.broadcasted_iota(jnp.int32, (HG_GROUP, HG_CHUNK, HG_CHUNK), 2)
    return (row <= col if upper else row >= col).astype(BF16)


def _chunk_cumsum(x, tri):
    x3 = x.reshape(HG_GROUP, HG_CHUNK, x.shape[1])
    dims = (((2,), (1,)), ((0,), (0,)))
    out = None
    for _ in range(3):
        part = x3.astype(BF16)
        x3 = x3 - part.astype(F32)
        term = lax.dot_general(tri, part, dims, preferred_element_type=F32)
        out = term if out is None else out + term
    return out.reshape(x.shape)


def _hg_gates(f_raw, q_raw, lb, tri):
    sg = _sigmoid(f_raw)
    f = lb + (1.0 - lb) * sg
    k = 1.0 - f
    b = _chunk_cumsum(jnp.log(f), tri)
    sq = _sigmoid(q_raw)
    return sg, f, k, b, sq


def _hg_masks(rows):
    row = lax.broadcasted_iota(jnp.int32, (rows, rows), 0)
    col = lax.broadcasted_iota(jnp.int32, (rows, rows), 1)
    same = (row // HG_CHUNK) == (col // HG_CHUNK)
    return jnp.logical_and(row >= col, same), jnp.logical_and(row <= col, same)


def _per_chunk(rows_of):
    return jnp.concatenate([jnp.broadcast_to(r, (HG_CHUNK, r.shape[1])) for r in rows_of], axis=0)


def _hgrn_fwd(proj, lb_logits, name):
    t = proj.shape[0]
    nc = t // HG_CHUNK
    nh = HG_WIDTH // HG_HEAD
    rows = HG_GROUP * HG_CHUNK

    def body(q_ref, f_ref, i_ref, lg_ref, o_ref, st_ref, s_scr):
        @pl.when(pl.program_id(0) == 0)
        def _():
            s_scr[...] = jnp.zeros_like(s_scr)

        lg = lg_ref[...]
        lb_all = _sigmoid(lg[0:1] - lg[1:2])
        causal, _ = _hg_masks(rows)
        tri = _chunk_tri(False)
        for h in range(nh):
            sl = slice(h * HG_HEAD, (h + 1) * HG_HEAD)
            q_raw = q_ref[:, sl].astype(F32)
            _, _, k, b, sq = _hg_gates(f_ref[:, sl].astype(F32), q_raw, lb_all[:, sl], tri)
            v = i_ref[:, sl].astype(BF16)
            gls = [b[(g + 1) * HG_CHUNK - 1:(g + 1) * HG_CHUNK] for g in range(HG_GROUP)]
            bm = _per_chunk([b[g * HG_CHUNK + HG_CHUNK // 2 - 1:g * HG_CHUNK + HG_CHUNK // 2] for g in range(HG_GROUP)])
            qd = (q_raw * sq * jnp.exp(b)).astype(BF16)
            qm = (q_raw * sq * jnp.exp(b - bm)).astype(BF16)
            km = (k * jnp.exp(bm - b)).astype(BF16)
            ke = (k * jnp.exp(_per_chunk(gls) - b)).astype(BF16)
            a = jnp.where(causal, _dot_nt(qm, km), 0.0).astype(BF16)
            o_intra = _dot(a, v)
            st = s_scr[h]
            o_inter = []
            for g in range(HG_GROUP):
                rs = slice(g * HG_CHUNK, (g + 1) * HG_CHUNK)
                st_ref[g, sl, :] = st
                o_inter.append(_dot_nt(qd[rs], st.astype(BF16)))
                st = st * jnp.exp(gls[g]) + _dot_tn(v[rs], ke[rs])
            s_scr[h] = st
            o_ref[:, sl] = (o_intra + jnp.concatenate(o_inter, axis=0)).astype(BF16)

    blk = lambda j: pl.BlockSpec((rows, HG_WIDTH), lambda c: (c, j))
    return _call(body, name=name, grid=(nc // HG_GROUP,),
                 in_specs=[blk(0), blk(1), blk(2), pl.BlockSpec((2, HG_WIDTH), lambda c: (0, 0))],
                 out_specs=[blk(0), pl.BlockSpec((HG_GROUP, HG_WIDTH, HG_HEAD), lambda c: (c, 0, 0))],
                 out_shape=[_sds((t, HG_WIDTH), BF16), _sds((nc, HG_WIDTH, HG_HEAD))],
                 scratch_shapes=[pltpu.VMEM((nh, HG_HEAD, HG_HEAD), F32)])(proj, proj, proj, lb_logits)


def _hgrn_bwd(proj, lb_logits, states, do, dproj, name):
    t = proj.shape[0]
    ng = t // (HG_GROUP * HG_CHUNK)
    nh = HG_WIDTH // HG_HEAD
    rows = HG_GROUP * HG_CHUNK

    def body(q_ref, f_ref, i_ref, lg_ref, st_ref, do_ref, _, d_ref, dlb_ref, ds_scr):
        first = pl.program_id(0) == 0

        @pl.when(first)
        def _():
            ds_scr[...] = jnp.zeros_like(ds_scr)

        lg = lg_ref[...]
        lb_all = _sigmoid(lg[0:1] - lg[1:2])
        causal, _ = _hg_masks(rows)
        tri = _chunk_tri(False)
        tri_t = _chunk_tri(True)
        dlb = []
        for h in range(nh):
            sl = slice(h * HG_HEAD, (h + 1) * HG_HEAD)
            q_raw = q_ref[:, sl].astype(F32)
            lb = lb_all[:, sl]
            sg, f, k, b, sq = _hg_gates(f_ref[:, sl].astype(F32), q_raw, lb, tri)
            v = i_ref[:, sl].astype(BF16)
            gls = [b[(g + 1) * HG_CHUNK - 1:(g + 1) * HG_CHUNK] for g in range(HG_GROUP)]
            bm = _per_chunk([b[g * HG_CHUNK + HG_CHUNK // 2 - 1:g * HG_CHUNK + HG_CHUNK // 2] for g in range(HG_GROUP)])
            eb = jnp.exp(b)
            ebm = jnp.exp(b - bm)
            emb = jnp.exp(bm - b)
            egb = jnp.exp(_per_chunk(gls) - b)
            ke = k * egb
            qd_b, qm_b = (q_raw * sq * eb).astype(BF16), (q_raw * sq * ebm).astype(BF16)
            km_b, ke_b = (k * emb).astype(BF16), ke.astype(BF16)
            dov = do_ref[:, sl].astype(BF16)
            a = jnp.where(causal, _dot_nt(qm_b, km_b), 0.0).astype(BF16)
            da = jnp.where(causal, _dot_nt(dov, v), 0.0).astype(BF16)
            dkm = _dot_tn(da, qm_b)
            dst = ds_scr[h]
            dqd_s, dv_s, dke_s, dgl_s = [None] * HG_GROUP, [None] * HG_GROUP, [None] * HG_GROUP, [None] * HG_GROUP
            for g in reversed(range(HG_GROUP)):
                rs = slice(g * HG_CHUNK, (g + 1) * HG_CHUNK)
                st = st_ref[g, sl, :]
                dst_b = dst.astype(BF16)
                egl = jnp.exp(gls[g])
                dqd_s[g] = _dot(dov[rs], st.astype(BF16))
                dv_s[g] = _dot_nt(ke_b[rs], dst_b)
                dke_s[g] = _dot(v[rs], dst_b)
                dgl_s[g] = jnp.sum(dst * st, axis=0, keepdims=True) * egl
                dst = _dot_tn(dov[rs], qd_b[rs]) + dst * egl
            ds_scr[h] = dst
            dqm = _dot(da, km_b)
            dqd = jnp.concatenate(dqd_s, axis=0)
            dv = _dot_tn(a, dov) + jnp.concatenate(dv_s, axis=0)
            dke = jnp.concatenate(dke_s, axis=0)
            t1 = dke * ke
            db = dqm * qm_b.astype(F32) - dkm * km_b.astype(F32) + dqd * qd_b.astype(F32) - t1
            dgl = _per_chunk([dgl_s[g] + jnp.sum(t1[g * HG_CHUNK:(g + 1) * HG_CHUNK], axis=0, keepdims=True)
                              for g in range(HG_GROUP)])
            dlf = _chunk_cumsum(db, tri_t) + dgl
            df = dlf / f - (dkm * emb + dke * egb)
            d_ref[:, sl] = ((dqm * ebm + dqd * eb) * sq * (1.0 + q_raw * (1.0 - sq))).astype(BF16)
            d_ref[:, HG_WIDTH + h * HG_HEAD:HG_WIDTH + (h + 1) * HG_HEAD] = (
                df * (1.0 - lb) * sg * (1.0 - sg)).astype(BF16)
            d_ref[:, 2 * HG_WIDTH + h * HG_HEAD:2 * HG_WIDTH + (h + 1) * HG_HEAD] = dv.astype(BF16)
            dlb.append(jnp.sum(df * (1.0 - sg), axis=0, keepdims=True))
        _acc(dlb_ref, jnp.concatenate(dlb, axis=1), first)

    rev = lambda j: pl.BlockSpec((rows, HG_WIDTH), lambda c: (ng - 1 - c, j))
    return _call(body, name=name, grid=(ng,),
                 in_specs=[rev(0), rev(1), rev(2), pl.BlockSpec((2, HG_WIDTH), lambda c: (0, 0)),
                           pl.BlockSpec((HG_GROUP, HG_WIDTH, HG_HEAD), lambda c: (ng - 1 - c, 0, 0)), rev(0),
                           pl.BlockSpec(memory_space=pl.ANY)],
                 out_specs=[pl.BlockSpec((rows, 3 * HG_WIDTH), lambda c: (ng - 1 - c, 0)), _vec(HG_WIDTH)],
                 out_shape=[_sds(dproj.shape, BF16), _sds((1, HG_WIDTH))], aliases={6: 0},
                 scratch_shapes=[pltpu.VMEM((nh, HG_HEAD, HG_HEAD), F32)])(
                     proj, proj, proj, lb_logits, states, do, dproj)


def _to_sub(a, dil):
    t, w = a.shape
    return a if dil == 1 else a.reshape(t // dil, dil, w).transpose(1, 0, 2).reshape(t, w)


def _from_sub(a, dil):
    t, w = a.shape
    return a if dil == 1 else a.reshape(dil, t // dil, w).transpose(1, 0, 2).reshape(t, w)


def _att_mask(has_prev, seg):
    def place(v):
        v = v % ATT_BLOCK
        return v if seg == 1 else seg * (v % (ATT_BLOCK // seg)) + v // (ATT_BLOCK // seg)

    row = lax.broadcasted_iota(jnp.int32, (2 * ATT_BLOCK, 2 * ATT_BLOCK), 0)
    col = lax.broadcasted_iota(jnp.int32, (2 * ATT_BLOCK, 2 * ATT_BLOCK), 1)
    qi, kj = place(row), place(col)
    prev = jnp.logical_and(jnp.logical_and(col < ATT_BLOCK, kj >= qi), has_prev)
    cur = jnp.logical_and(col >= ATT_BLOCK, kj <= qi)
    return jnp.logical_or(prev, cur), lax.broadcasted_iota(jnp.int32, (1, 128), 1)


def _get(ref, sl):
    if len(ref.shape) == 2:
        return ref[:, sl]
    v = ref[:, :, sl]
    return v.reshape(ATT_BLOCK, v.shape[2])


def _put(ref, sl, val):
    if len(ref.shape) == 2:
        ref[:, sl] = val
    else:
        ref[:, :, sl] = val.reshape(ref.shape[0], ref.shape[1], val.shape[1])


def _att_spec(nb, dil, seg, width, col, back):
    bps = nb // dil

    def plain(n):
        return jnp.clip(n - back, 0, nb - 1), col

    def segmented(n):
        m = jnp.clip(n - back, 0, nb - 1)
        return 0, m // bps, m % bps, 0, col

    if seg == 1:
        return pl.BlockSpec((ATT_BLOCK, width), plain)
    return pl.BlockSpec((seg, None, None, ATT_BLOCK // seg, width), segmented)


def _att_shape(nb, dil, seg, width):
    t = nb * ATT_BLOCK
    return (t, width) if seg == 1 else (seg, dil, nb // dil, ATT_BLOCK // seg, width)


def _att_view(a, nb, dil, seg):
    return a.reshape(_att_shape(nb, dil, seg, a.shape[1]))


def _attn_fwd_block(q_ref, kc_ref, kp_ref, vc_ref, vp_ref, o_ref, l_ref, has_prev, seg, lane0):
    mask, lane = _att_mask(has_prev, seg)
    lo = lane < 64
    nq = ATT_BLOCK
    lse_all = jnp.zeros((nq, 128), F32)
    for hp in range(ATT_HEADS // 2):
        sl = slice(hp * 128, (hp + 1) * 128)
        q2 = _get(q_ref, sl)
        zero = jnp.zeros_like(q2)
        q2 = q2 * 0.125
        qs = jnp.concatenate([jnp.where(lo, q2, zero), jnp.where(lo, zero, q2)], axis=0)
        kk = jnp.concatenate([_get(kp_ref, sl), _get(kc_ref, sl)], axis=0)
        vv = jnp.concatenate([_get(vp_ref, sl), _get(vc_ref, sl)], axis=0)
        s = jnp.where(mask, _dot_nt(qs, kk), NEG)
        mx = jnp.max(s, axis=-1, keepdims=True)
        p = jnp.exp(s - mx)
        l = jnp.sum(p, axis=-1, keepdims=True)
        o = _dot(p.astype(BF16), vv) * (1.0 / l)
        _put(o_ref, sl, jnp.where(lo, o[:nq], o[nq:]).astype(BF16))
        lse = mx + jnp.log(l)
        lse_all = jnp.where(lane == lane0 + 2 * hp, lse[:nq], lse_all)
        lse_all = jnp.where(lane == lane0 + 2 * hp + 1, lse[nq:], lse_all)
    _put(l_ref, slice(None), lse_all)


def _attn_fwd(branches, name):
    t = branches[0][0].shape[0]
    nb = t // ATT_BLOCK
    nbr = len(branches)

    def body(*refs):
        n = pl.program_id(0)
        for i, (_, dil, seg) in enumerate(branches):
            _attn_fwd_block(*refs[5 * i:5 * i + 5], *refs[5 * nbr + 2 * i:5 * nbr + 2 * i + 2],
                            (n % (nb // dil)) != 0, seg, ATT_HEADS * i)

    in_specs, args, out_specs, out_shape = [], [], [], []
    for qkv, dil, seg in branches:
        c0 = qkv.shape[1] // ATT_WIDTH - 3
        in_specs += [_att_spec(nb, dil, seg, ATT_WIDTH, c0 + j, back) for j, back in [(0, 0), (1, 0), (1, 1), (2, 0), (2, 1)]]
        args += [_att_view(qkv, nb, dil, seg)] * 5
        out_specs += [_att_spec(nb, dil, seg, ATT_WIDTH, 0, 0), _att_spec(nb, dil, seg, 128, 0, 0)]
        out_shape += [_sds(_att_shape(nb, dil, seg, ATT_WIDTH), BF16), _sds(_att_shape(nb, dil, seg, 128))]
    out = _call(body, name=name, grid=(nb,), in_specs=in_specs, out_specs=out_specs, out_shape=out_shape)(*args)
    return [(out[2 * i].reshape(t, ATT_WIDTH), out[2 * i + 1].reshape(t, 128)) for i in range(nbr)]


def _combine_mix_in(os_, ls_, fine, o_hg, proj, og, ag, name, tr=512):
    t = os_[0].shape[0]
    nbr = len(os_)

    def body(*refs):
        o_refs, l_refs = refs[:nbr], refs[nbr:2 * nbr]
        oh_ref, g_ref, og_ref, ag_ref, a_ref, lt_ref, lts_ref, m_ref, to_natural, to_sub = refs[2 * nbr:]

        @pl.when(pl.program_id(0) == 0)
        def _():
            to_natural[...] = _regroup_matrix(tr, fine)
            to_sub[...] = _regroup_matrix(tr, tr // fine)

        lane = lax.broadcasted_iota(jnp.int32, (1, 128), 1)
        packed = l_refs[0][...] + _regroup(to_natural[...], sum(r[...] for r in l_refs[1:]).reshape(tr, 128))
        ls = [packed if i == 0 else pltpu.roll(packed, 128 - ATT_HEADS * i, 1) for i in range(nbr)]
        mx = functools.reduce(jnp.maximum, ls)
        tot = mx + jnp.log(sum(jnp.exp(l - mx) for l in ls))
        ws = [jnp.exp(l - tot) for l in ls]
        tot = jnp.where(lane < ATT_HEADS, tot, 0.0)
        lt_ref[...] = tot
        lts_ref[...] = _regroup(to_sub[...], tot).reshape(lts_ref.shape)
        o_vals = [o_refs[0]] + [_regroup(to_natural[...], r[...].reshape(tr, ATT_WIDTH)) for r in o_refs[1:]]
        pairs = []
        for hp in range(ATT_HEADS // 2):
            sl = slice(hp * 128, (hp + 1) * 128)
            acc = jnp.zeros((tr, 128), F32)
            for w, o in zip(ws, o_vals):
                wf = jnp.where(lane < 64, w[:, 2 * hp:2 * hp + 1], w[:, 2 * hp + 1:2 * hp + 2])
                acc = acc + wf * o[:, sl]
            pairs.append(acc)
        av = jnp.concatenate(pairs, axis=1)
        a_ref[...] = av
        m_ref[:, HG_WIDTH:] = (av * _rms(av) * ag_ref[...]).astype(BF16)
        for h in range(HG_WIDTH // HG_HEAD):
            sl = slice(h * HG_HEAD, (h + 1) * HG_HEAD)
            oh = oh_ref[:, sl].astype(F32)
            gv = g_ref[:, sl].astype(F32)
            m_ref[:, sl] = (oh * _rms(oh) * og_ref[...] * (gv * _sigmoid(gv))).astype(BF16)

    half = _rows(tr, ATT_WIDTH)
    sub = lambda a: a.reshape(fine, t // fine, a.shape[1])
    att, lse, lse_sub, mixin = _call(
        body, name=name, grid=(t // tr,),
        in_specs=[half] + [_sub_rows(tr, fine, ATT_WIDTH)] * (nbr - 1) + [_rows(tr, 128)] + [_sub_rows(tr, fine, 128)] * (nbr - 1)
        + [half, pl.BlockSpec((tr, HG_WIDTH), lambda i: (i, 3)), _vec(HG_HEAD), _vec(ATT_WIDTH)],
        out_specs=[half, _rows(tr, 128), _sub_rows(tr, fine, 128), _rows(tr, D_MODEL)],
        out_shape=[_sds((t, ATT_WIDTH)), _sds((t, 128)), _sds((fine, t // fine, 128)), _sds((t, D_MODEL), BF16)],
        scratch_shapes=[pltpu.VMEM((tr, tr), BF16)] * 2)(
            os_[0], *map(sub, os_[1:]), ls_[0], *map(sub, ls_[1:]), o_hg, proj, og, ag)
    return att, lse, lse_sub.reshape(t, 128), mixin


def _attn_bwd_block(q_ref, k_ref, v_ref, do_ref, l_ref, d_ref, out_ref, carry, prev, has_prev, seg):
    w = ATT_WIDTH
    nq = ATT_BLOCK
    mask, lane = _att_mask(has_prev, seg)
    lo = lane < 64
    lse, ddv = _get(l_ref, slice(None)), _get(d_ref, slice(None))
    for hp in range(ATT_HEADS // 2):
        sl = slice(hp * 128, (hp + 1) * 128)
        sk = slice(w + hp * 128, w + (hp + 1) * 128)
        sv = slice(2 * w + hp * 128, 2 * w + (hp + 1) * 128)
        q2, do2 = _get(q_ref, sl), _get(do_ref, sl)
        zero = jnp.zeros_like(q2)
        q2 = q2 * 0.125
        qs = jnp.concatenate([jnp.where(lo, q2, zero), jnp.where(lo, zero, q2)], axis=0)
        dos = jnp.concatenate([jnp.where(lo, do2, zero), jnp.where(lo, zero, do2)], axis=0)
        kc, vc = _get(k_ref, sl), _get(v_ref, sl)
        kk = jnp.concatenate([prev[:, sl], kc], axis=0)
        vv = jnp.concatenate([prev[:, sk], vc], axis=0)
        prev[:, sl] = kc
        prev[:, sk] = vc
        ls = jnp.concatenate([lse[:, 2 * hp:2 * hp + 1], lse[:, 2 * hp + 1:2 * hp + 2]], axis=0)
        dh = jnp.concatenate([ddv[:, 2 * hp:2 * hp + 1], ddv[:, 2 * hp + 1:2 * hp + 2]], axis=0)
        p = jnp.exp(jnp.where(mask, _dot_nt(qs, kk) - ls, NEG))
        ds = (p * (_dot_nt(dos, vv) - dh)).astype(BF16)
        dq = _dot(ds, kk) * 0.125
        dk = _dot_tn(ds, qs)
        dv = _dot_tn(p.astype(BF16), dos)
        _put(out_ref, sl, carry[:, sl].astype(BF16))
        _put(out_ref, sk, (carry[:, sk] + dk[:nq]).astype(BF16))
        _put(out_ref, sv, (carry[:, sv] + dv[:nq]).astype(BF16))
        carry[:, sl] = jnp.where(lo, dq[:nq], dq[nq:])
        carry[:, sk] = dk[nq:]
        carry[:, sv] = dv[nq:]


def _attn_bwd(branches, name):
    t = branches[0][0].shape[0]
    nb = t // ATT_BLOCK
    nbr = len(branches)
    w = ATT_WIDTH

    def body(*refs):
        ins, outs, carries, prevs = refs[:6 * nbr], refs[6 * nbr:7 * nbr], refs[7 * nbr:8 * nbr], refs[8 * nbr:]
        n = pl.program_id(0)

        @pl.when(n == 0)
        def _():
            for scratch in carries + prevs:
                scratch[...] = jnp.zeros_like(scratch)

        @pl.when(n < nb)
        def _():
            for i, branch in enumerate(branches):
                dil, seg = branch[4:]
                _attn_bwd_block(*ins[6 * i:6 * i + 6], outs[i], carries[i], prevs[i], (n % (nb // dil)) != 0, seg)

        @pl.when(n == nb)
        def _():
            for i in range(nbr):
                _put(outs[i], slice(None), carries[i][...].astype(BF16))

    in_specs, args, out_specs, out_shape = [], [], [], []
    for qkv, dout, lse, dd, dil, seg in branches:
        c0 = qkv.shape[1] // w - 3
        in_specs += [_att_spec(nb, dil, seg, w, c0 + j, 0) for j in range(3)]
        in_specs += [_att_spec(nb, dil, seg, w, 0, 0), _att_spec(nb, dil, seg, 128, 0, 0), _att_spec(nb, dil, seg, 128, 0, 0)]
        args += [_att_view(a, nb, dil, seg) for a in [qkv] * 3 + [dout, lse, dd]]
        out_specs += [_att_spec(nb, dil, seg, 3 * w, 0, 1)]
        out_shape += [_sds(_att_shape(nb, dil, seg, 3 * w), BF16)]
    out = _call(body, name=name, grid=(nb + 1,), in_specs=in_specs, out_specs=out_specs, out_shape=out_shape,
                scratch_shapes=[pltpu.VMEM((ATT_BLOCK, 3 * w), F32)] * nbr + [pltpu.VMEM((ATT_BLOCK, 2 * w), BF16)] * nbr)(*args)
    return [o.reshape(t, 3 * w) for o in out]


def _local_step(x, tgt, mod, norm1_g, lb_logits, og, ag, norm2_g, fg, get_w, put_g, late=lambda a: a, project=None):
    shift1, scale1, gate1, shift2, scale2, gate2 = [mod[:, i * D_MODEL:(i + 1) * D_MODEL] for i in range(6)]
    fg = fg.reshape(1, D_MODEL)

    h1 = _norm_mod(x, norm1_g, scale1, shift1, "norm_mod1")
    if project is None:
        w_in = get_w("w_in", h1)
        proj = _mm_nn(h1, w_in, "mm_in", out_dtype=BF16)
    else:
        proj, w_in = project(h1)
    o_hg, states = _hgrn_fwd(proj, lb_logits, "hgrn_fwd")
    fine = DILATIONS[-1]
    layouts = [(d, 1 if d == 1 else fine // d) for d in DILATIONS]
    qkv_fine = _to_sub(proj, fine)
    qkvs = [proj if d == 1 else qkv_fine for d in DILATIONS]
    natural = lambda a, d: a if d == 1 else _from_sub(a, fine)
    outs = _attn_fwd([(q, d, seg) for q, (d, seg) in zip(qkvs, layouts)], "attn_fwd")
    att, lse, lse_fine, mixin = _combine_mix_in([o for o, _ in outs], [l for _, l in outs], fine,
                                                o_hg, proj, og, ag, "attn_combine_mix_in")
    w_out = get_w("w_out", mixin)
    mix, x2, h2 = _out_resid_norm_mod(x, mixin, w_out, gate1, norm2_g, scale2, shift2, "mm_out_resid_norm_mod2")
    w_gu = get_w("w_gu", h2)
    a_ff, u_ff, act = _mm_gate_up(h2, w_gu, "mm_gu")
    w_down = get_w("w_down", act)
    dx3, dffn, loss_v, dfg, dgate2 = _down_loss(x2, act, w_down, gate2, fg, tgt, "mm_down_loss")

    dffn = put_g("w_down", *_mm_tn(act, dffn, 1, "mm_down_dw", tm=2048, tk=D_FF // 2), dffn)
    dau = _mm_down_dx(dffn, w_down, a_ff, u_ff, "mm_down_dx")
    dau = put_g("w_gu", *_mm_tn(h2, dau, N_SHARD, "mm_gu_dw", tm=x.shape[0], tk=512), dau)
    dx2, dshift2, dscale2, dg2, dgate1, dmix = _norm_mod_bwd(
        dau, x2, norm2_g, scale2, dx3, "mm_gu_dx_norm_bwd", gate=gate1, mix=mix, w=w_gu)
    dmix = put_g("w_out", *_mm_tn(mixin, dmix, 1, "mm_out_dw", tm=x.shape[0], tk=512), dmix)
    do_hg, dproj, datt, dd, datt_fine, dd_fine, dog, dag = _mix_in_bwd(
        dmix, w_out, o_hg, proj, att, og, ag, fine, "mm_out_dx_mix_in_bwd")
    datts = _attn_bwd([(q,) + ((datt, lse, dd) if d == 1 else (datt_fine, lse_fine, dd_fine)) + (d, seg)
                       for q, (d, seg) in zip(qkvs, layouts)], "attn_bwd")
    dproj, dlb = _hgrn_bwd(proj, lb_logits, states, do_hg, dproj, "hgrn_bwd")
    dproj = late(dproj)
    dproj = _dproj(dproj, [natural(a, d) for a, d in zip(datts, DILATIONS)], "dproj")
    dproj = put_g("w_in", *_mm_tn(h1, dproj, N_SHARD, "mm_in_dw", tm=x.shape[0], tk=512, group=2), dproj)
    dx, dshift1, dscale1, dg1 = _norm_mod_bwd(dproj, x, norm1_g, scale1, dx2, "mm_in_dx_norm_bwd", w=w_in)

    stats = jnp.concatenate([loss_v, dfg, dg2, dg1, dlb, dag, dog,
                             dshift1, dscale1, dgate1, dshift2, dscale2, dgate2], axis=1)
    return dx, stats


def _place():
    x, y, c = lax.axis_index("x"), lax.axis_index("y"), lax.axis_index("c")
    return x, y, c


def _chip_peers(x, y, c):
    return [(1 - x, y, c), (x, 1 - y, c), (1 - x, 1 - y, c)]


def _comm_call(body, name, n_in, out_shape, scratch_shapes):
    hbm = pl.BlockSpec(memory_space=pl.ANY)
    return pl.pallas_call(body, name=name, in_specs=[hbm] * n_in, out_specs=[hbm] * len(out_shape),
                          out_shape=out_shape, scratch_shapes=scratch_shapes)


_HBM = pl.BlockSpec(memory_space=pltpu.HBM)
_SEM = pl.BlockSpec(memory_space=pltpu.SEMAPHORE)
_EFFECT = pltpu.SideEffectType.DATAFLOW_SIDE_EFFECTING


def _exchange_copy(bufs, send, recv, j, peer, place, kind):
    x, y, c = place
    target = peer
    if kind == "gather":
        src = dst = bufs[0].at[2 * x + y]
    elif kind == "scatter":
        src, dst = bufs[0].at[2 * peer[0] + peer[1]], bufs[1].at[j]
    else:
        half = bufs[0].shape[1] // 2
        rows = pl.ds(c * half, half)
        if kind == "half":
            src = dst = bufs[0].at[2 * x + y, rows]
        else:
            src = dst = bufs[0].at[2 * peer[0] + peer[1], rows]
            target = (x, y, 1 - c)
    return pltpu.make_async_remote_copy(src_ref=src, dst_ref=dst, send_sem=send.at[j], recv_sem=recv.at[j],
                                        device_id=target, device_id_type=MESH)


def _exchange_start(groups, after, kind, name):
    sizes = [len(g) for g in groups]
    flat = [b for g in groups for b in g]
    ng, nb = len(groups), len(flat)

    def body(*refs):
        bufs, sems = refs[:nb], refs[nb + 1:nb + 1 + 2 * ng]
        x, y, c = _place()
        for j, peer in enumerate(_chip_peers(x, y, c)):
            at = 0
            for i, size in enumerate(sizes):
                _exchange_copy(bufs[at:at + size], sems[2 * i], sems[2 * i + 1], j, peer, (x, y, c), kind).start()
                at += size

    any_space = pl.BlockSpec(memory_space=pl.ANY)
    out = pl.pallas_call(
        body, name=name, in_specs=[_HBM] * nb + [any_space],
        out_specs=[_SEM] * (2 * ng) + [_HBM] * nb + [any_space],
        out_shape=[pltpu.SemaphoreType.DMA((3,))] * (2 * ng) + [pltpu.HBM(b.shape, b.dtype) for b in flat]
        + [_sds(after.shape, after.dtype)],
        input_output_aliases={i: 2 * ng + i for i in range(nb + 1)},
        compiler_params=pltpu.CompilerParams(has_side_effects=_EFFECT),
    )(*[pltpu.with_memory_space_constraint(b, pltpu.HBM) for b in flat], after)
    started, at = [], 2 * ng
    for i, size in enumerate(sizes):
        started.append((out[2 * i], out[2 * i + 1], tuple(out[at:at + size])))
        at += size
    return started, out[-1]


def _exchange_wait(started, after, kind, name):
    send, recv, bufs = started
    nb = len(bufs)

    def body(*refs):
        x, y, c = _place()
        for j, peer in enumerate(_chip_peers(x, y, c)):
            cp = _exchange_copy(refs[:nb], refs[nb], refs[nb + 1], j, peer, (x, y, c), kind)
            cp.wait_send()
            cp.wait_recv()

    return pl.pallas_call(
        body, name=name, in_specs=[_HBM] * nb + [_SEM, _SEM, pl.BlockSpec(memory_space=pl.ANY)],
        out_specs=[_HBM] * nb, out_shape=[pltpu.HBM(b.shape, b.dtype) for b in bufs],
        input_output_aliases={i: i for i in range(nb)},
        compiler_params=pltpu.CompilerParams(has_side_effects=_EFFECT),
    )(*bufs, send, recv, after)


def _sibling_copies(v_refs, l_refs, send, recv):
    x, y, c = _place()
    return [pltpu.make_async_remote_copy(src_ref=v, dst_ref=l, send_sem=send.at[a], recv_sem=recv.at[a],
                                         device_id=(x, y, 1 - c), device_id_type=MESH)
            for a, (v, l) in enumerate(zip(v_refs, l_refs))]


def _sibling_start(vs, after, name):
    vs = list(vs)
    n = len(vs)
    lands = [lax.empty(v.shape, v.dtype) for v in vs]

    def body(*refs):
        for cp in _sibling_copies(refs[:n], refs[n:2 * n], refs[2 * n + 1], refs[2 * n + 2]):
            cp.start()

    any_space = pl.BlockSpec(memory_space=pl.ANY)
    out = pl.pallas_call(
        body, name=name, in_specs=[_HBM] * (2 * n) + [any_space],
        out_specs=[_SEM, _SEM] + [_HBM] * (2 * n) + [any_space],
        out_shape=[pltpu.SemaphoreType.DMA((n,))] * 2 + [pltpu.HBM(b.shape, b.dtype) for b in vs + lands]
        + [_sds(after.shape, after.dtype)],
        input_output_aliases={i: 2 + i for i in range(2 * n + 1)},
        compiler_params=pltpu.CompilerParams(has_side_effects=_EFFECT),
    )(*[pltpu.with_memory_space_constraint(b, pltpu.HBM) for b in vs + lands], after)
    return (out[0], out[1], tuple(out[2:2 + n]), tuple(out[2 + n:2 + 2 * n])), out[-1]


def _sibling_wait(started, after, name):
    send, recv, vs, lands = started
    n = len(vs)

    def body(*refs):
        for cp in _sibling_copies(refs[:n], refs[n:2 * n], refs[2 * n], refs[2 * n + 1]):
            cp.wait_send()
            cp.wait_recv()

    out = pl.pallas_call(
        body, name=name, in_specs=[_HBM] * (2 * n) + [_SEM, _SEM, pl.BlockSpec(memory_space=pl.ANY)],
        out_specs=[_HBM] * (2 * n), out_shape=[pltpu.HBM(b.shape, b.dtype) for b in vs + lands],
        input_output_aliases={i: i for i in range(2 * n)},
        compiler_params=pltpu.CompilerParams(has_side_effects=_EFFECT),
    )(*vs, *lands, send, recv, after)
    return out[:n], out[n:]


def _swap_sibling(vs, name):
    n = len(vs)

    def body(*refs):
        v_refs, o_refs, (send, recv) = refs[:n], refs[n:2 * n], refs[2 * n:]
        x, y, c = _place()
        cps = [pltpu.make_async_remote_copy(
            src_ref=v_refs[a], dst_ref=o_refs[a], send_sem=send.at[a], recv_sem=recv.at[a],
            device_id=(x, y, 1 - c), device_id_type=MESH) for a in range(n)]
        for cp in cps:
            cp.start()
        for cp in cps:
            cp.wait()

    return _comm_call(body, name, n, [_sds(v.shape, v.dtype) for v in vs],
                      [pltpu.SemaphoreType.DMA((n,)), pltpu.SemaphoreType.DMA((n,))])(*vs)


def _everyone(x, y, c):
    return [(1 - x if k & 4 else x, 1 - y if k & 2 else y, 1 - c if k & 1 else c) for k in range(1, 8)]


def _all_gather_copies(land_ref, send, recv, arriving):
    x, y, c = _place()
    me = 4 * x + 2 * y + c
    return [pltpu.make_async_remote_copy(
        src_ref=land_ref.at[me], dst_ref=land_ref.at[4 * p[0] + 2 * p[1] + p[2] if arriving else me],
        send_sem=send.at[k], recv_sem=recv.at[k], device_id=p, device_id_type=MESH)
        for k, p in enumerate(_everyone(x, y, c))]


def _all_gather_start(v, name):
    x, y, c = _place()
    land = lax.dynamic_update_slice(lax.empty((8,) + v.shape, v.dtype), v[None], (4 * x + 2 * y + c, 0, 0))

    def body(land_ref, v_ref, send, recv, land_out, v_out):
        for cp in _all_gather_copies(land_ref, send, recv, False):
            cp.start()

    any_space = pl.BlockSpec(memory_space=pl.ANY)
    out = pl.pallas_call(
        body, name=name, in_specs=[_HBM, any_space], out_specs=[_SEM, _SEM, _HBM, any_space],
        out_shape=[pltpu.SemaphoreType.DMA((7,))] * 2 + [pltpu.HBM(land.shape, land.dtype), _sds(v.shape, v.dtype)],
        input_output_aliases={0: 2, 1: 3}, compiler_params=pltpu.CompilerParams(has_side_effects=_EFFECT),
    )(pltpu.with_memory_space_constraint(land, pltpu.HBM), v)
    return tuple(out[:3]), out[3]


def _all_gather_wait(started, after, name):
    send, recv, land = started

    def body(land_ref, send, recv, after_ref, land_out):
        for cp in _all_gather_copies(land_ref, send, recv, True):
            cp.wait_send()
            cp.wait_recv()

    return pl.pallas_call(
        body, name=name, in_specs=[_HBM, _SEM, _SEM, pl.BlockSpec(memory_space=pl.ANY)], out_specs=_HBM,
        out_shape=pltpu.HBM(land.shape, land.dtype), input_output_aliases={0: 0},
        compiler_params=pltpu.CompilerParams(has_side_effects=_EFFECT),
    )(land, send, recv, after)


def _cast_place(ws, shard, name, after=()):
    n = len(ws)

    def body(s_ref, *refs):
        for w_ref, o_ref in zip(refs[:n], refs[-n:]):
            o_ref[0] = w_ref[...].astype(BF16)

    return pl.pallas_call(
        body, name=name, out_shape=[_sds((N_SHARD,) + w.shape, BF16) for w in ws],
        grid_spec=pltpu.PrefetchScalarGridSpec(
            num_scalar_prefetch=1, grid=(4,),
            in_specs=[pl.BlockSpec((w.shape[0] // 4, w.shape[1]), lambda i, s: (i, 0)) for w in ws]
            + [pl.BlockSpec(memory_space=pl.ANY)] * len(after),
            out_specs=[pl.BlockSpec((1, w.shape[0] // 4, w.shape[1]), lambda i, s: (s[0], i, 0)) for w in ws]),
        compiler_params=pltpu.CompilerParams(dimension_semantics=("arbitrary",), vmem_limit_bytes=VMEM_LIMIT),
    )(shard.reshape(1).astype(jnp.int32), *ws, *after)


def _mod_rows(c8, w_ada, b_ada, name):
    n = w_ada.shape[1]

    def gather(src_ref, dst_ref, send, recv, loc, base):
        x, y, c = _place()
        me = 4 * x + 2 * y + c
        own = pltpu.make_async_copy(src_ref, dst_ref.at[me], loc)
        own.start()
        peers = _everyone(x, y, c)
        sends = [pltpu.make_async_remote_copy(src_ref=src_ref, dst_ref=dst_ref.at[me], send_sem=send.at[base + k],
                                              recv_sem=recv.at[base + k], device_id=p, device_id_type=MESH)
                 for k, p in enumerate(peers)]
        for cp in sends:
            cp.start()
        for k, p in enumerate(peers):
            pltpu.make_async_remote_copy(src_ref=src_ref, dst_ref=dst_ref.at[4 * p[0] + 2 * p[1] + p[2]],
                                         send_sem=send.at[base + k], recv_sem=recv.at[base + k], device_id=p,
                                         device_id_type=MESH).wait_recv()
        for cp in sends:
            cp.wait_send()
        own.wait()

    def body(c_ref, w_ref, b_ref, a_ref, parts_ref, c_all, part, send, recv, loc):
        gather(c_ref, c_all, send, recv, loc.at[0], 0)
        cv = jnp.max(c_all[...], axis=1)
        ca = cv * _sigmoid(cv)
        a_ref[...] = ca
        part[...] = jnp.dot(ca, w_ref[...], precision=lax.Precision.HIGHEST, preferred_element_type=F32) + b_ref[...]
        gather(part, parts_ref, send, recv, loc.at[1], 7)

    vmem = pl.BlockSpec(memory_space=pltpu.VMEM)
    return pl.pallas_call(
        body, name=name, in_specs=[vmem] * 3, out_specs=[vmem, vmem],
        out_shape=[_sds((8, D_MODEL)), _sds((8, 8, n))],
        scratch_shapes=[pltpu.VMEM((8, 8, D_MODEL), F32), pltpu.VMEM((8, n), F32), pltpu.SemaphoreType.DMA((14,)),
                        pltpu.SemaphoreType.DMA((14,)), pltpu.SemaphoreType.DMA((2,))],
        compiler_params=pltpu.CompilerParams(vmem_limit_bytes=VMEM_LIMIT))(c8, w_ada, b_ada)


def _sum_received(gs, shard, lands, name):
    n = len(gs)

    def body(s_ref, *refs):
        for g_ref, l_ref, o_ref in zip(refs[:n], refs[n:2 * n], refs[2 * n:]):
            o_ref[...] = ((g_ref[0] + l_ref[0].astype(F32)) + l_ref[1].astype(F32)) + l_ref[2].astype(F32)

    quarter = lambda g: (g.shape[1] // 4, g.shape[2])
    return pl.pallas_call(
        body, name=name, out_shape=[_sds(g.shape[1:]) for g in gs],
        grid_spec=pltpu.PrefetchScalarGridSpec(
            num_scalar_prefetch=1, grid=(4,),
            in_specs=[pl.BlockSpec((1,) + quarter(g), lambda i, s: (s[0], i, 0)) for g in gs]
            + [pl.BlockSpec((3,) + quarter(g), lambda i, s: (0, i, 0)) for g in gs],
            out_specs=[pl.BlockSpec(quarter(g), lambda i, s: (i, 0)) for g in gs]),
        compiler_params=pltpu.CompilerParams(dimension_semantics=("arbitrary",), vmem_limit_bytes=VMEM_LIMIT),
    )(shard.reshape(1).astype(jnp.int32), *gs, *lands)


def _adamw_outer(w, ct, dm, m, v, name):
    k, n = w.shape
    tr = k // 4

    def body(w_ref, c_ref, d_ref, m_ref, v_ref, g_out, d_out, m_out, v_out):
        cv = c_ref[...]
        dv = d_ref[...]
        g = cv[:, 0:1] * dv[0:1, :]
        for i in range(1, 8):
            g = g + cv[:, i:i + 1] * dv[i:i + 1, :]
        g_out[...] = g
        d_out[...], m_out[...], v_out[...] = _adamw_math(w_ref[...], g, m_ref[...], v_ref[...])

    row = _rows(tr, n)
    return _call(body, name=name, grid=(4,),
                 in_specs=[row, _rows(tr, 8), pl.BlockSpec((8, n), lambda i: (0, 0)), row, row],
                 out_specs=[row] * 4, out_shape=[_sds((k, n))] * 4)(w, ct, dm, m, v)


def _adamw_math(w, g, m, v):
    m_new = ADAM_B1 * m + (1.0 - ADAM_B1) * g
    v_new = ADAM_B2 * v + (1.0 - ADAM_B2) * (g * g)
    m_hat = m_new / (1.0 - ADAM_B1 ** ADAM_STEP)
    v_hat = v_new / (1.0 - ADAM_B2 ** ADAM_STEP)
    return -ADAM_LR * (m_hat / (jnp.sqrt(v_hat) + ADAM_EPS) + ADAM_WD * w), m_new, v_new


def _small_update(stats, smalls, name):
    offsets = [ST_DMOD, ST_DG1, ST_DLB, ST_DOG, ST_DAG, ST_DG2, ST_DFG]
    lb_index = 2

    def body(*refs):
        s_ref, ins, l_ref, outs = refs[0], refs[1:22], refs[22], refs[23:]
        tot = s_ref[0:1, :]
        for i in range(1, 8):
            tot = tot + s_ref[i:i + 1, :]
        l_ref[...] = jnp.zeros((1, 128), F32) + (0.5 / D_MODEL) * jnp.sum(tot[:, ST_LOSS:ST_LOSS + D_MODEL])
        for p, off in enumerate(offsets):
            w_ref, m_ref, v_ref = ins[3 * p:3 * p + 3]
            g_out, d_out, m_out, v_out = outs[4 * p:4 * p + 4]
            g = tot[:, off:off + w_ref.shape[1]]
            if p == lb_index:
                lg = w_ref[...]
                lb = _sigmoid(lg[0:1] - lg[1:2])
                g = g * lb * (1.0 - lb)
            for r in range(w_ref.shape[0]):
                rows = slice(r, r + 1)
                gr = g if r == 0 else -g
                delta, m_new, v_new = _adamw_math(w_ref[rows, :], gr, m_ref[rows, :], v_ref[rows, :])
                g_out[rows, :] = gr
                d_out[rows, :] = delta
                m_out[rows, :] = m_new
                v_out[rows, :] = v_new

    full = lambda a: pl.BlockSpec(a.shape, lambda i: (0, 0))
    flat = [a for t in smalls for a in t]
    return _call(body, name=name, grid=(1,),
                 in_specs=[full(stats)] + [full(a) for a in flat],
                 out_specs=[pl.BlockSpec((1, 128), lambda i: (0, 0))] + [full(t[0]) for t in smalls for _ in range(4)],
                 out_shape=[_sds((1, 128))] + [_sds(t[0].shape) for t in smalls for _ in range(4)])(stats, *flat)


def _adamw(params, name):
    n = len(params)

    def body(*refs):
        for p in range(n):
            w_ref, ga_ref, gb_ref, m_ref, v_ref = refs[5 * p:5 * p + 5]
            g_out, d_out, m_out, v_out = refs[5 * n + 4 * p:5 * n + 4 * p + 4]
            g = ga_ref[...] + gb_ref[...]
            g_out[...] = g
            d_out[...], m_out[...], v_out[...] = _adamw_math(w_ref[...], g, m_ref[...], v_ref[...])

    row = lambda w: _rows(w.shape[0] // 4, w.shape[1])
    out = _call(body, name=name, grid=(4,), in_specs=[row(p[0]) for p in params for _ in range(5)],
                out_specs=[row(p[0]) for p in params for _ in range(4)],
                out_shape=[_sds(p[0].shape) for p in params for _ in range(4)])(*[a for p in params for a in p])
    return [tuple(out[4 * p:4 * p + 4]) for p in range(n)]


def kernel(x, c, w_ada, b_ada, norm1_g, w_in, hg_lb_logits, hg_onorm_g, att_onorm_g, w_out, norm2_g, w_gate_up, w_down, final_g, loss_target, m_w_ada, m_b_ada, m_norm1_g, m_w_in, m_hg_lb_logits, m_hg_onorm_g, m_att_onorm_g, m_w_out, m_norm2_g, m_w_gate_up, m_w_down, m_final_g, v_w_ada, v_b_ada, v_norm1_g, v_w_in, v_hg_lb_logits, v_hg_onorm_g, v_att_onorm_g, v_w_out, v_norm2_g, v_w_gate_up, v_w_down, v_final_g):
    ix, iy, ic = _place()
    shard = 2 * ix + iy
    sample = 4 * ix + 2 * iy + ic
    n_ada = w_ada.shape[2]

    shards = [w_in[0], w_out[0], w_gate_up[0], w_down[0]]
    names = ["w_in", "w_out", "w_gu", "w_down"]
    shapes = [(N_SHARD,) + w.shape for w in shards]
    placed = [(_cast_place(shards[:1], shard, "place_w_in")[0],)]

    b_part = lax.dynamic_slice(b_ada, (0, shard * n_ada), (1, n_ada))
    c_act, parts = _mod_rows(jnp.broadcast_to(c, (8, D_MODEL)), w_ada[0], b_part, "mod_rows")
    parts = parts[::2]
    mod = lax.dynamic_index_in_dim(parts, sample, axis=1, keepdims=False).reshape(1, 6 * D_MODEL)
    (first,), mod = _exchange_start(placed[:1], mod, "half", "gather_start_w_in")
    gathering = {}

    def get_w(name, after):
        if name == "w_in":
            placed_rest = [(p,) for p in _cast_place(shards[1:], shard, "place_rest", (after,))]
            halves = _exchange_wait(first, placed_rest[0][0], "half", "gather_wait_w_in")
            (passing,), token = _exchange_start([tuple(halves)], mod, "forward", "forward_start_w_in")
            rest, token = _exchange_start(placed_rest, token, "gather", "gather_start_rest")
            (full,) = _exchange_wait(passing, token, "forward", "forward_wait_w_in")
            gathering.update(zip(names[1:], rest))
            return full
        (full,) = _exchange_wait(gathering[name], after, "gather", "gather_wait_" + name)
        return full if name == "w_gu" else full.reshape(1, -1, D_MODEL)

    scattering = {}

    def put_g(name, g, g_bf16, then):
        shape = shapes[names.index(name)]
        land = lax.empty((3,) + shape[1:], BF16)
        (started,), then = _exchange_start([(g_bf16.reshape(shape), land)], then, "scatter", "scatter_start_" + name)
        scattering[name] = (g.reshape(shape), started)
        return then

    def summed(group, after, tag):
        lands = [_exchange_wait(scattering[nm][1], after, "scatter", "scatter_wait_" + nm)[1] for nm in group]
        return _sum_received([scattering[nm][0] for nm in group], shard, lands, "sum_" + tag)

    early = ["w_down", "w_gu", "w_out"]
    swapping = []

    def late(a):
        started, a = _sibling_start(summed(early, a, "early"), a, "swap_start")
        swapping.append(started)
        return a

    def project(h1):
        own = _mm_own_shard(h1, shards[0], shard, N_SHARD, "mm_in_own")
        w_full = get_w("w_in", own)
        return _mm_other_shards(h1, w_full, own, shard, "mm_in_rest"), w_full

    dx, stats = _local_step(x[0], loss_target[0], mod, norm1_g, hg_lb_logits, hg_onorm_g, att_onorm_g,
                            norm2_g, final_g, get_w, put_g, late, project)

    gathering_stats, stats = _all_gather_start(stats, "stats_start")
    moments = [(m_w_in, v_w_in), (m_w_out, v_w_out), (m_w_gate_up, v_w_gate_up), (m_w_down, v_w_down)]

    def update(group, sums, other, tag):
        params = [(shards[names.index(nm)], s, o, moments[names.index(nm)][0][0], moments[names.index(nm)][1][0])
                  for nm, s, o in zip(group, sums, other)]
        return dict(zip(group, _adamw(params, "adamw_" + tag)))

    sums, other = _sibling_wait(swapping[0], stats, "swap_wait")
    done = update(early, sums, other, "early")
    sum_in = summed(["w_in"], done["w_out"][1], "w_in")
    done.update(update(["w_in"], sum_in, _swap_sibling(sum_in, "swap_sum_in"), "w_in"))

    stats_all = _all_gather_wait(gathering_stats, done["w_in"][1], "stats_wait").reshape(8, ST_WIDTH)
    dmod = lax.dynamic_slice(stats_all, (0, ST_DMOD + shard * n_ada), (8, n_ada))

    as_row = lambda a: a.reshape(1, -1) if a.ndim == 1 else a
    smalls = [tuple(as_row(a) for a in t) for t in [
        (b_ada, m_b_ada, v_b_ada), (norm1_g, m_norm1_g, v_norm1_g),
        (hg_lb_logits, m_hg_lb_logits, v_hg_lb_logits), (hg_onorm_g, m_hg_onorm_g, v_hg_onorm_g),
        (att_onorm_g, m_att_onorm_g, v_att_onorm_g), (norm2_g, m_norm2_g, v_norm2_g),
        (final_g, m_final_g, v_final_g)]]
    loss, *small_out = _small_update(stats_all, smalls, "small_update")
    shapes_out = [b_ada.shape, norm1_g.shape, hg_lb_logits.shape, hg_onorm_g.shape, att_onorm_g.shape,
                  norm2_g.shape, final_g.shape]
    sg, sd, sm, sv = [[small_out[4 * p + i].reshape(shapes_out[p]) for p in range(7)] for i in range(4)]

    ada = _adamw_outer(w_ada[0], c_act.T, dmod, m_w_ada[0], v_w_ada[0], "adamw_w_ada")
    big = [ada] + [done[nm] for nm in names]
    bg, bd, bm, bv = [[t[i][None] for t in big] for i in range(4)]

    def order(b, s):
        return [b[0], s[0], s[1], b[1], s[2], s[3], s[4], b[2], s[5], b[3], b[4], s[6]]

    return (loss[0, 0], dx[None], *order(bg, sg), *order(bd, sd), *order(bm, sm), *order(bv, sv))
```

```python
import functools

import jax
import jax.numpy as jnp
from jax import lax
from jax.experimental import pallas as pl
from jax.experimental.pallas import tpu as pltpu

F32 = jnp.float32
BF16 = jnp.bfloat16
MESH = pl.DeviceIdType.MESH

D_MODEL = 1024
HG_WIDTH = 512
HG_HEAD = 128
HG_CHUNK = 64
HG_GROUP = 4
ATT_WIDTH = 512
ATT_HEADS = 8
ATT_BLOCK = 128
DILATIONS = (1, 4, 16)
D_FF = 2816
IN_WIDTH = 3584
N_SHARD = 4
RMS_EPS = 1e-6
NEG = -1e30

ADAM_LR = 0.001
ADAM_B1 = 0.9
ADAM_B2 = 0.999
ADAM_EPS = 1e-08
ADAM_WD = 0.01
ADAM_STEP = 10

VMEM_LIMIT = 56 * 2**20

ST_LOSS, ST_DFG, ST_DG2, ST_DG1 = 0, 1024, 2048, 3072
ST_DLB, ST_DAG, ST_DOG, ST_DMOD = 4096, 4608, 5120, 5248
ST_WIDTH = 5248 + 6144


def _call(body, *, name, grid, in_specs, out_specs, out_shape, scratch_shapes=(), aliases=None):
    return pl.pallas_call(
        body, name=name, grid=grid, in_specs=in_specs, out_specs=out_specs, out_shape=out_shape,
        scratch_shapes=list(scratch_shapes), input_output_aliases=aliases or {},
        compiler_params=pltpu.CompilerParams(
            dimension_semantics=("arbitrary",) * len(grid), vmem_limit_bytes=VMEM_LIMIT))


def _sds(shape, dtype=F32):
    return jax.ShapeDtypeStruct(shape, dtype)


def _dot(a, b):
    return jnp.dot(a, b, preferred_element_type=F32)


def _dot_nt(a, b):
    return lax.dot_general(a, b, (((1,), (1,)), ((), ())), preferred_element_type=F32)


def _dot_tn(a, b):
    return lax.dot_general(a, b, (((0,), (0,)), ((), ())), preferred_element_type=F32)


def _sigmoid(x):
    return 1.0 / (1.0 + jnp.exp(-x))


def _rows(tr, width):
    return pl.BlockSpec((tr, width), lambda i: (i, 0))


def _vec(width):
    return pl.BlockSpec((1, width), lambda i: (0, 0))


def _acc(ref, val, first):
    @pl.when(first)
    def _():
        ref[...] = val

    @pl.when(jnp.logical_not(first))
    def _():
        ref[...] += val


def _sub_rows(tr, fine, width):
    return pl.BlockSpec((fine, tr // fine, width), lambda i: (0, i, 0))


def _regroup_matrix(tr, groups):
    a = lax.broadcasted_iota(jnp.int32, (tr, tr), 0)
    b = lax.broadcasted_iota(jnp.int32, (tr, tr), 1)
    return (b == (a % groups) * (tr // groups) + a // groups).astype(BF16)


def _regroup(m, v, lanes=None):
    if v.dtype == BF16:
        return _dot(m, v)
    width = v.shape[1]
    packed, out = None, None
    for i in range(3):
        part = v.astype(BF16).astype(F32)
        v = v - part
        if lanes is None:
            out = _dot(m, part.astype(BF16)) if i == 0 else out + _dot(m, part.astype(BF16))
        else:
            packed = part if i == 0 else packed + pltpu.roll(part, i * lanes, 1)
    if lanes is None:
        return out
    out = _dot(m, packed.astype(BF16))
    out = out + pltpu.roll(out, width - lanes, 1) + pltpu.roll(out, width - 2 * lanes, 1)
    return jnp.where(lax.broadcasted_iota(jnp.int32, (1, width), 1) < lanes, out, 0.0)


def _mm_nn(a, b3, name, tm=1024, out_dtype=F32):
    m, k = a.shape
    s, _, n = b3.shape

    def body(a_ref, b_ref, o_ref):
        o_ref[...] = _dot(a_ref[...], b_ref[0]).astype(out_dtype)

    return _call(
        body, name=name, grid=(s, m // tm),
        in_specs=[pl.BlockSpec((tm, k), lambda j, i: (i, 0)), pl.BlockSpec((1, k, n), lambda j, i: (j, 0, 0))],
        out_specs=pl.BlockSpec((tm, n), lambda j, i: (i, j)), out_shape=_sds((m, s * n), out_dtype))(a, b3)


def _mm_own_shard(a, w, shard, s, name, tm=1024):
    m, k = a.shape
    n = w.shape[1]

    def body(s_ref, a_ref, w_ref, o_ref):
        o_ref[...] = _dot(a_ref[...], w_ref[...].astype(BF16)).astype(BF16)

    return pl.pallas_call(
        body, name=name, out_shape=_sds((m, s * n), BF16),
        grid_spec=pltpu.PrefetchScalarGridSpec(
            num_scalar_prefetch=1, grid=(m // tm,),
            in_specs=[pl.BlockSpec((tm, k), lambda i, sh: (i, 0)), pl.BlockSpec((k, n), lambda i, sh: (0, 0))],
            out_specs=pl.BlockSpec((tm, n), lambda i, sh: (i, sh[0]))),
        compiler_params=pltpu.CompilerParams(dimension_semantics=("arbitrary",), vmem_limit_bytes=VMEM_LIMIT),
    )(shard.reshape(1).astype(jnp.int32), a, w)


def _mm_other_shards(a, b3, partial, shard, name, tm=1024):
    m, k = a.shape
    s, _, n = b3.shape
    which = lambda j, sh: (sh[0] + 1 + j) % s

    def body(s_ref, a_ref, b_ref, p_ref, o_ref):
        o_ref[...] = _dot(a_ref[...], b_ref[0]).astype(BF16)

    return pl.pallas_call(
        body, name=name, out_shape=_sds(partial.shape, BF16),
        grid_spec=pltpu.PrefetchScalarGridSpec(
            num_scalar_prefetch=1, grid=(s - 1, m // tm),
            in_specs=[pl.BlockSpec((tm, k), lambda j, i, sh: (i, 0)),
                      pl.BlockSpec((1, k, n), lambda j, i, sh: (which(j, sh), 0, 0)),
                      pl.BlockSpec(memory_space=pl.ANY)],
            out_specs=pl.BlockSpec((tm, n), lambda j, i, sh: (i, which(j, sh)))),
        input_output_aliases={3: 0},
        compiler_params=pltpu.CompilerParams(dimension_semantics=("arbitrary",) * 2, vmem_limit_bytes=VMEM_LIMIT),
    )(shard.reshape(1).astype(jnp.int32), a, b3, partial)


def _mm_tn(a, dy, s, name, tm, tk, group=1):
    m, k = a.shape
    n = dy.shape[1] // s
    steps = m // tm

    def body(a_ref, dy_ref, o_ref, ob_ref):
        p = _dot_tn(a_ref[...], dy_ref[...])
        for g in range(group):
            pg = p[:, g * n:(g + 1) * n]
            if steps == 1:
                o_ref[g] = pg
                ob_ref[g] = pg.astype(BF16)
            else:
                _acc(o_ref.at[g], pg, pl.program_id(2) == 0)
        if steps > 1:
            @pl.when(pl.program_id(2) == steps - 1)
            def _():
                ob_ref[...] = o_ref[...].astype(BF16)

    out = pl.BlockSpec((group, tk, n), lambda kk, j, i: (j, kk, 0))
    return _call(
        body, name=name, grid=(k // tk, s // group, steps),
        in_specs=[pl.BlockSpec((tm, tk), lambda kk, j, i: (i, kk)),
                  pl.BlockSpec((tm, group * n), lambda kk, j, i: (i, j))],
        out_specs=[out, out], out_shape=[_sds((s, k, n)), _sds((s, k, n), BF16)])(a, dy)


def _rms(x):
    return lax.rsqrt(jnp.mean(x * x, axis=-1, keepdims=True) + RMS_EPS)


def _rms_bwd(dxh, xh, r):
    return r * (dxh - xh * jnp.mean(dxh * xh, axis=-1, keepdims=True))


def _norm_mod(x, g, scale, shift, name, tr=512):
    t = x.shape[0]

    def body(x_ref, g_ref, sc_ref, sh_ref, h_ref):
        xv = x_ref[...]
        n = xv * _rms(xv) * g_ref[...]
        h_ref[...] = (n * (1.0 + sc_ref[...]) + sh_ref[...]).astype(BF16)

    return _call(body, name=name, grid=(t // tr,),
                 in_specs=[_rows(tr, D_MODEL), _vec(D_MODEL), _vec(D_MODEL), _vec(D_MODEL)],
                 out_specs=_rows(tr, D_MODEL), out_shape=_sds((t, D_MODEL), BF16))(x, g, scale, shift)


def _out_resid_norm_mod(x, mixin, w_out, gate, g, scale, shift, name, tr=512):
    t = x.shape[0]

    def body(x_ref, mi_ref, w_ref, gt_ref, g_ref, sc_ref, sh_ref, m_ref, x2_ref, h_ref):
        mix = _dot(mi_ref[...], w_ref[0])
        m_ref[...] = mix
        x2 = x_ref[...] + gt_ref[...] * mix
        x2_ref[...] = x2
        n = x2 * _rms(x2) * g_ref[...]
        h_ref[...] = (n * (1.0 + sc_ref[...]) + sh_ref[...]).astype(BF16)

    row = _rows(tr, D_MODEL)
    return _call(body, name=name, grid=(t // tr,),
                 in_specs=[row, row, pl.BlockSpec(w_out.shape, lambda i: (0, 0, 0))] + [_vec(D_MODEL)] * 4,
                 out_specs=[row, row, row],
                 out_shape=[_sds((t, D_MODEL)), _sds((t, D_MODEL)), _sds((t, D_MODEL), BF16)])(
                     x, mixin, w_out, gate, g, scale, shift)


def _mm_gate_up(h, w_gu, name, tm=1024):
    m, k = h.shape
    n = w_gu.shape[2]

    def body(h_ref, wa_ref, wu_ref, da_ref, du_ref, o_ref, w_au):
        @pl.when(pl.program_id(1) == 0)
        def _():
            w_au[:, :n] = wa_ref[0]
            w_au[:, n:] = wu_ref[0]

        au = _dot(h_ref[...], w_au[...])
        a, u = au[:, :n], au[:, n:]
        sg = _sigmoid(a)
        silu = a * sg
        da_ref[...] = (u * sg * (1.0 + a * (1.0 - sg))).astype(BF16)
        du_ref[...] = silu.astype(BF16)
        o_ref[...] = (silu * u).astype(BF16)

    out = pl.BlockSpec((tm, n), lambda j, i: (i, j))
    return _call(body, name=name, grid=(2, m // tm),
                 in_specs=[pl.BlockSpec((tm, k), lambda j, i: (i, 0)), pl.BlockSpec((1, k, n), lambda j, i: (j, 0, 0)),
                           pl.BlockSpec((1, k, n), lambda j, i: (j + 2, 0, 0))],
                 out_specs=[out, out, out], out_shape=[_sds((m, 2 * n), BF16)] * 3,
                 scratch_shapes=[pltpu.VMEM((k, 2 * n), BF16)])(h, w_gu, w_gu)


def _mm_down_dx(dffn, w_down, act_da, act_du, name, tm=512):
    m = dffn.shape[0]
    _, k, n = w_down.shape

    def body(d_ref, w_ref, da_ref, du_ref, o_ref):
        dact = _dot_nt(d_ref[...], w_ref[0])
        o_ref[:, :k] = (dact * da_ref[...].astype(F32)).astype(BF16)
        o_ref[:, k:] = (dact * du_ref[...].astype(F32)).astype(BF16)

    return _call(body, name=name, grid=(m // tm,),
                 in_specs=[_rows(tm, n), pl.BlockSpec((1, k, n), lambda i: (0, 0, 0)), _rows(tm, k), _rows(tm, k)],
                 out_specs=_rows(tm, 2 * k), out_shape=_sds((m, 2 * k), BF16))(dffn, w_down, act_da, act_du)


def _down_loss(x2, act, w_down, gate, fg, tgt, name, tr=512):
    t = x2.shape[0]
    _, k, n = w_down.shape

    def body(x_ref, a_ref, w_ref, gt_ref, fg_ref, t_ref, dx_ref, df_ref, l_ref, dfg_ref, dgt_ref):
        first = pl.program_id(0) == 0
        ffn_v = _dot(a_ref[...], w_ref[0])
        x3 = x_ref[...] + gt_ref[...] * ffn_v
        r = _rms(x3)
        xh = x3 * r
        err = xh * fg_ref[...] - t_ref[...]
        dy = err * (1.0 / D_MODEL)
        dx3 = _rms_bwd(dy * fg_ref[...], xh, r)
        dx_ref[...] = dx3
        df_ref[...] = (dx3 * gt_ref[...]).astype(BF16)
        _acc(l_ref, jnp.sum(err * err, axis=0, keepdims=True), first)
        _acc(dfg_ref, jnp.sum(dy * xh, axis=0, keepdims=True), first)
        _acc(dgt_ref, jnp.sum(dx3 * ffn_v, axis=0, keepdims=True), first)

    row, vec = _rows(tr, D_MODEL), _vec(D_MODEL)
    return _call(body, name=name, grid=(t // tr,),
                 in_specs=[row, _rows(tr, k), pl.BlockSpec((1, k, n), lambda i: (0, 0, 0)), vec, vec, row],
                 out_specs=[row, row, vec, vec, vec],
                 out_shape=[_sds((t, D_MODEL)), _sds((t, D_MODEL), BF16)] + [_sds((1, D_MODEL))] * 3)(
                     x2, act, w_down, gate, fg, tgt)


def _norm_mod_bwd(dh, x, g, scale, dres, name, gate=None, mix=None, w=None, tr=512):
    t = x.shape[0]
    below = gate is not None

    def body(*refs):
        if w is not None:
            w_ref, w_full, sem, refs = refs[1], refs[-2], refs[-1], refs[:1] + refs[2:-2]

            @pl.when(pl.program_id(0) == 0)
            def _():
                n = w.shape[2]
                copies = [pltpu.make_async_copy(w_ref.at[j], w_full.at[:, pl.ds(j * n, n)], sem.at[j])
                          for j in range(w.shape[0])]
                for cp in copies:
                    cp.start()
                for cp in copies:
                    cp.wait()

        if below:
            dh_ref, x_ref, g_ref, sc_ref, dr_ref, gt_ref, m_ref, dx_ref, dsh_ref, dsc_ref, dg_ref, dgt_ref, dm_ref = refs
        else:
            dh_ref, x_ref, g_ref, sc_ref, dr_ref, dx_ref, dsh_ref, dsc_ref, dg_ref = refs
        first = pl.program_id(0) == 0
        xv = x_ref[...]
        if w is None:
            dhv = dh_ref[...].astype(F32)
        else:
            dhv = _dot_nt(dh_ref[...], w_full[...])
        r = _rms(xv)
        xh = xv * r
        dn = dhv * (1.0 + sc_ref[...])
        dx = dr_ref[...] + _rms_bwd(dn * g_ref[...], xh, r)
        dx_ref[...] = dx
        _acc(dsh_ref, jnp.sum(dhv, axis=0, keepdims=True), first)
        _acc(dsc_ref, jnp.sum(dhv * xh * g_ref[...], axis=0, keepdims=True), first)
        _acc(dg_ref, jnp.sum(dn * xh, axis=0, keepdims=True), first)
        if below:
            _acc(dgt_ref, jnp.sum(dx * m_ref[...], axis=0, keepdims=True), first)
            dm_ref[...] = (dx * gt_ref[...]).astype(BF16)

    row, vec = _rows(tr, D_MODEL), _vec(D_MODEL)
    first_specs = [row] if w is None else [_rows(tr, dh.shape[1]), pl.BlockSpec(memory_space=pl.ANY)]
    scratch = [] if w is None else [pltpu.VMEM((w.shape[1], dh.shape[1]), BF16), pltpu.SemaphoreType.DMA((w.shape[0],))]
    in_specs = first_specs + [row, vec, vec, row] + ([vec, row] if below else [])
    out_specs = [row, vec, vec, vec] + ([vec, row] if below else [])
    out_shape = [_sds((t, D_MODEL))] + [_sds((1, D_MODEL))] * 3 + ([_sds((1, D_MODEL)), _sds((t, D_MODEL), BF16)] if below else [])
    args = ((dh,) if w is None else (dh, w)) + (x, g, scale, dres) + ((gate, mix) if below else ())
    return _call(body, name=name, grid=(t // tr,), in_specs=in_specs, out_specs=out_specs, out_shape=out_shape,
                 scratch_shapes=scratch)(*args)


def _mix_in_bwd(dmix, w_out, o_hg, proj, att, og, ag, fine, name, tr=512):
    t = o_hg.shape[0]

    def body(dy_ref, w_ref, o_ref, g_ref, a_ref, og_ref, ag_ref,
             do_ref, dg_ref, da_ref, dd_ref, das_ref, dds_ref, dog_ref, dag_ref, to_sub):
        first = pl.program_id(0) == 0

        @pl.when(first)
        def _():
            to_sub[...] = _regroup_matrix(tr, tr // fine)

        dmi = _dot_nt(dy_ref[...], w_ref[0])
        dog = jnp.zeros((1, HG_HEAD), F32)
        for h in range(HG_WIDTH // HG_HEAD):
            sl = slice(h * HG_HEAD, (h + 1) * HG_HEAD)
            oh = o_ref[:, sl].astype(F32)
            gv = g_ref[:, sl].astype(F32)
            dv = dmi[:, sl]
            r = _rms(oh)
            xh = oh * r
            sg = _sigmoid(gv)
            dno = dv * gv * sg
            dg_ref[:, sl] = (dv * xh * og_ref[...] * sg * (1.0 + gv * (1.0 - sg))).astype(BF16)
            dog = dog + jnp.sum(dno * xh, axis=0, keepdims=True)
            do_ref[:, sl] = _rms_bwd(dno * og_ref[...], xh, r).astype(BF16)
        _acc(dog_ref, dog, first)
        av = a_ref[...]
        dav = dmi[:, HG_WIDTH:]
        r = _rms(av)
        xa = av * r
        _acc(dag_ref, jnp.sum(dav * xa, axis=0, keepdims=True), first)
        datt = _rms_bwd(dav * ag_ref[...], xa, r)
        da_ref[...] = datt.astype(BF16)
        das_ref[...] = _regroup(to_sub[...], datt.astype(BF16)).astype(BF16).reshape(das_ref.shape)
        prod = datt * av
        lane = lax.broadcasted_iota(jnp.int32, (1, 128), 1)
        dd = jnp.zeros((tr, 128), F32)
        for hp in range(ATT_HEADS // 2):
            pp = prod[:, hp * 128:(hp + 1) * 128]
            lo = jnp.sum(jnp.where(lane < 64, pp, 0.0), axis=-1, keepdims=True)
            hi = jnp.sum(jnp.where(lane >= 64, pp, 0.0), axis=-1, keepdims=True)
            dd = jnp.where(lane == 2 * hp, lo, dd)
            dd = jnp.where(lane == 2 * hp + 1, hi, dd)
        dd_ref[...] = dd
        dds_ref[...] = _regroup(to_sub[...], dd, ATT_HEADS).reshape(dds_ref.shape)

    half = _rows(tr, HG_WIDTH)
    out = _call(body, name=name, grid=(t // tr,),
                in_specs=[_rows(tr, D_MODEL), pl.BlockSpec(w_out.shape, lambda i: (0, 0, 0)), half,
                          pl.BlockSpec((tr, HG_WIDTH), lambda i: (i, 3)), half, _vec(HG_HEAD), _vec(ATT_WIDTH)],
                out_specs=[half, pl.BlockSpec((tr, HG_WIDTH), lambda i: (i, 3)), half, _rows(tr, 128),
                           _sub_rows(tr, fine, ATT_WIDTH), _sub_rows(tr, fine, 128), _vec(HG_HEAD), _vec(ATT_WIDTH)],
                out_shape=[_sds((t, HG_WIDTH), BF16), _sds((t, IN_WIDTH), BF16), _sds((t, HG_WIDTH), BF16),
                           _sds((t, 128)), _sds((fine, t // fine, ATT_WIDTH), BF16),
                           _sds((fine, t // fine, 128)), _sds((1, HG_HEAD)), _sds((1, ATT_WIDTH))],
                scratch_shapes=[pltpu.VMEM((tr, tr), BF16)])(dmix, w_out, o_hg, proj, att, og, ag)
    out = list(out)
    return out[:4] + [out[4].reshape(t, ATT_WIDTH), out[5].reshape(t, 128)] + out[6:]


def _dproj(dproj, dqkvs, name, tr=1024):
    t = dproj.shape[0]
    nbr = len(dqkvs)
    first = IN_WIDTH // ATT_WIDTH - 3

    def body(*refs):
        refs[-1][...] = sum(r[...].astype(F32) for r in refs[:nbr]).astype(BF16)

    return _call(body, name=name, grid=(t // tr, 3),
                 in_specs=[pl.BlockSpec((tr, ATT_WIDTH), lambda i, j: (i, j))] * nbr + [pl.BlockSpec(memory_space=pl.ANY)],
                 out_specs=pl.BlockSpec((tr, ATT_WIDTH), lambda i, j: (i, first + j)),
                 out_shape=_sds(dproj.shape, BF16), aliases={nbr: 0})(*dqkvs, dproj)


def _chunk_tri(upper):
    row = lax.broadcasted_iota(jnp.int32, (HG_GROUP, HG_CHUNK, HG_CHUNK), 1)
    col = lax.broadcasted_iota(jnp.int32, (HG_GROUP, HG_CHUNK, HG_CHUNK), 2)
    return (row <= col if upper else row >= col).astype(BF16)


def _chunk_cumsum(x, tri):
    x3 = x.reshape(HG_GROUP, HG_CHUNK, x.shape[1])
    dims = (((2,), (1,)), ((0,), (0,)))
    out = None
    for _ in range(3):
        part = x3.astype(BF16)
        x3 = x3 - part.astype(F32)
        term = lax.dot_general(tri, part, dims, preferred_element_type=F32)
        out = term if out is None else out + term
    return out.reshape(x.shape)


def _hg_gates(f_raw, q_raw, lb, tri):
    sg = _sigmoid(f_raw)
    f = lb + (1.0 - lb) * sg
    k = 1.0 - f
    b = _chunk_cumsum(jnp.log(f), tri)
    sq = _sigmoid(q_raw)
    return sg, f, k, b, sq


def _hg_masks(rows):
    row = lax.broadcasted_iota(jnp.int32, (rows, rows), 0)
    col = lax.broadcasted_iota(jnp.int32, (rows, rows), 1)
    same = (row // HG_CHUNK) == (col // HG_CHUNK)
    return jnp.logical_and(row >= col, same), jnp.logical_and(row <= col, same)


def _per_chunk(rows_of):
    return jnp.concatenate([jnp.broadcast_to(r, (HG_CHUNK, r.shape[1])) for r in rows_of], axis=0)


def _hgrn_fwd(proj, lb_logits, name):
    t = proj.shape[0]
    nc = t // HG_CHUNK
    nh = HG_WIDTH // HG_HEAD
    rows = HG_GROUP * HG_CHUNK

    def body(q_ref, f_ref, i_ref, lg_ref, o_ref, st_ref, s_scr):
        @pl.when(pl.program_id(0) == 0)
        def _():
            s_scr[...] = jnp.zeros_like(s_scr)

        lg = lg_ref[...]
        lb_all = _sigmoid(lg[0:1] - lg[1:2])
        causal, _ = _hg_masks(rows)
        tri = _chunk_tri(False)
        for h in range(nh):
            sl = slice(h * HG_HEAD, (h + 1) * HG_HEAD)
            q_raw = q_ref[:, sl].astype(F32)
            _, _, k, b, sq = _hg_gates(f_ref[:, sl].astype(F32), q_raw, lb_all[:, sl], tri)
            v = i_ref[:, sl].astype(BF16)
            gls = [b[(g + 1) * HG_CHUNK - 1:(g + 1) * HG_CHUNK] for g in range(HG_GROUP)]
            bm = _per_chunk([b[g * HG_CHUNK + HG_CHUNK // 2 - 1:g * HG_CHUNK + HG_CHUNK // 2] for g in range(HG_GROUP)])
            qd = (q_raw * sq * jnp.exp(b)).astype(BF16)
            qm = (q_raw * sq * jnp.exp(b - bm)).astype(BF16)
            km = (k * jnp.exp(bm - b)).astype(BF16)
            ke = (k * jnp.exp(_per_chunk(gls) - b)).astype(BF16)
            a = jnp.where(causal, _dot_nt(qm, km), 0.0).astype(BF16)
            o_intra = _dot(a, v)
            st = s_scr[h]
            o_inter = []
            for g in range(HG_GROUP):
                rs = slice(g * HG_CHUNK, (g + 1) * HG_CHUNK)
                st_ref[g, sl, :] = st
                o_inter.append(_dot_nt(qd[rs], st.astype(BF16)))
                st = st * jnp.exp(gls[g]) + _dot_tn(v[rs], ke[rs])
            s_scr[h] = st
            o_ref[:, sl] = (o_intra + jnp.concatenate(o_inter, axis=0)).astype(BF16)

    blk = lambda j: pl.BlockSpec((rows, HG_WIDTH), lambda c: (c, j))
    return _call(body, name=name, grid=(nc // HG_GROUP,),
                 in_specs=[blk(0), blk(1), blk(2), pl.BlockSpec((2, HG_WIDTH), lambda c: (0, 0))],
                 out_specs=[blk(0), pl.BlockSpec((HG_GROUP, HG_WIDTH, HG_HEAD), lambda c: (c, 0, 0))],
                 out_shape=[_sds((t, HG_WIDTH), BF16), _sds((nc, HG_WIDTH, HG_HEAD))],
                 scratch_shapes=[pltpu.VMEM((nh, HG_HEAD, HG_HEAD), F32)])(proj, proj, proj, lb_logits)


def _hgrn_bwd(proj, lb_logits, states, do, dproj, name):
    t = proj.shape[0]
    ng = t // (HG_GROUP * HG_CHUNK)
    nh = HG_WIDTH // HG_HEAD
    rows = HG_GROUP * HG_CHUNK

    def body(q_ref, f_ref, i_ref, lg_ref, st_ref, do_ref, _, d_ref, dlb_ref, ds_scr):
        first = pl.program_id(0) == 0

        @pl.when(first)
        def _():
            ds_scr[...] = jnp.zeros_like(ds_scr)

        lg = lg_ref[...]
        lb_all = _sigmoid(lg[0:1] - lg[1:2])
        causal, _ = _hg_masks(rows)
        tri = _chunk_tri(False)
        tri_t = _chunk_tri(True)
        dlb = []
        for h in range(nh):
            sl = slice(h * HG_HEAD, (h + 1) * HG_HEAD)
            q_raw = q_ref[:, sl].astype(F32)
            lb = lb_all[:, sl]
            sg, f, k, b, sq = _hg_gates(f_ref[:, sl].astype(F32), q_raw, lb, tri)
            v = i_ref[:, sl].astype(BF16)
            gls = [b[(g + 1) * HG_CHUNK - 1:(g + 1) * HG_CHUNK] for g in range(HG_GROUP)]
            bm = _per_chunk([b[g * HG_CHUNK + HG_CHUNK // 2 - 1:g * HG_CHUNK + HG_CHUNK // 2] for g in range(HG_GROUP)])
            eb = jnp.exp(b)
            ebm = jnp.exp(b - bm)
            emb = jnp.exp(bm - b)
            egb = jnp.exp(_per_chunk(gls) - b)
            ke = k * egb
            qd_b, qm_b = (q_raw * sq * eb).astype(BF16), (q_raw * sq * ebm).astype(BF16)
            km_b, ke_b = (k * emb).astype(BF16), ke.astype(BF16)
            dov = do_ref[:, sl].astype(BF16)
            a = jnp.where(causal, _dot_nt(qm_b, km_b), 0.0).astype(BF16)
            da = jnp.where(causal, _dot_nt(dov, v), 0.0).astype(BF16)
            dkm = _dot_tn(da, qm_b)
            dst = ds_scr[h]
            dqd_s, dv_s, dke_s, dgl_s = [None] * HG_GROUP, [None] * HG_GROUP, [None] * HG_GROUP, [None] * HG_GROUP
            for g in reversed(range(HG_GROUP)):
                rs = slice(g * HG_CHUNK, (g + 1) * HG_CHUNK)
                st = st_ref[g, sl, :]
                dst_b = dst.astype(BF16)
                egl = jnp.exp(gls[g])
                dqd_s[g] = _dot(dov[rs], st.astype(BF16))
                dv_s[g] = _dot_nt(ke_b[rs], dst_b)
                dke_s[g] = _dot(v[rs], dst_b)
                dgl_s[g] = jnp.sum(dst * st, axis=0, keepdims=True) * egl
                dst = _dot_tn(dov[rs], qd_b[rs]) + dst * egl
            ds_scr[h] = dst
            dqm = _dot(da, km_b)
            dqd = jnp.concatenate(dqd_s, axis=0)
            dv = _dot_tn(a, dov) + jnp.concatenate(dv_s, axis=0)
            dke = jnp.concatenate(dke_s, axis=0)
            t1 = dke * ke
            db = dqm * qm_b.astype(F32) - dkm * km_b.astype(F32) + dqd * qd_b.astype(F32) - t1
            dgl = _per_chunk([dgl_s[g] + jnp.sum(t1[g * HG_CHUNK:(g + 1) * HG_CHUNK], axis=0, keepdims=True)
                              for g in range(HG_GROUP)])
            dlf = _chunk_cumsum(db, tri_t) + dgl
            df = dlf / f - (dkm * emb + dke * egb)
            d_ref[:, sl] = ((dqm * ebm + dqd * eb) * sq * (1.0 + q_raw * (1.0 - sq))).astype(BF16)
            d_ref[:, HG_WIDTH + h * HG_HEAD:HG_WIDTH + (h + 1) * HG_HEAD] = (
                df * (1.0 - lb) * sg * (1.0 - sg)).astype(BF16)
            d_ref[:, 2 * HG_WIDTH + h * HG_HEAD:2 * HG_WIDTH + (h + 1) * HG_HEAD] = dv.astype(BF16)
            dlb.append(jnp.sum(df * (1.0 - sg), axis=0, keepdims=True))
        _acc(dlb_ref, jnp.concatenate(dlb, axis=1), first)

    rev = lambda j: pl.BlockSpec((rows, HG_WIDTH), lambda c: (ng - 1 - c, j))
    return _call(body, name=name, grid=(ng,),
                 in_specs=[rev(0), rev(1), rev(2), pl.BlockSpec((2, HG_WIDTH), lambda c: (0, 0)),
                           pl.BlockSpec((HG_GROUP, HG_WIDTH, HG_HEAD), lambda c: (ng - 1 - c, 0, 0)), rev(0),
                           pl.BlockSpec(memory_space=pl.ANY)],
                 out_specs=[pl.BlockSpec((rows, 3 * HG_WIDTH), lambda c: (ng - 1 - c, 0)), _vec(HG_WIDTH)],
                 out_shape=[_sds(dproj.shape, BF16), _sds((1, HG_WIDTH))], aliases={6: 0},
                 scratch_shapes=[pltpu.VMEM((nh, HG_HEAD, HG_HEAD), F32)])(
                     proj, proj, proj, lb_logits, states, do, dproj)


def _to_sub(a, dil):
    t, w = a.shape
    return a if dil == 1 else a.reshape(t // dil, dil, w).transpose(1, 0, 2).reshape(t, w)


def _from_sub(a, dil):
    t, w = a.shape
    return a if dil == 1 else a.reshape(dil, t // dil, w).transpose(1, 0, 2).reshape(t, w)


def _att_mask(has_prev, seg):
    def place(v):
        v = v % ATT_BLOCK
        return v if seg == 1 else seg * (v % (ATT_BLOCK // seg)) + v // (ATT_BLOCK // seg)

    row = lax.broadcasted_iota(jnp.int32, (2 * ATT_BLOCK, 2 * ATT_BLOCK), 0)
    col = lax.broadcasted_iota(jnp.int32, (2 * ATT_BLOCK, 2 * ATT_BLOCK), 1)
    qi, kj = place(row), place(col)
    prev = jnp.logical_and(jnp.logical_and(col < ATT_BLOCK, kj >= qi), has_prev)
    cur = jnp.logical_and(col >= ATT_BLOCK, kj <= qi)
    return jnp.logical_or(prev, cur), lax.broadcasted_iota(jnp.int32, (1, 128), 1)


def _get(ref, sl):
    if len(ref.shape) == 2:
        return ref[:, sl]
    v = ref[:, :, sl]
    return v.reshape(ATT_BLOCK, v.shape[2])


def _put(ref, sl, val):
    if len(ref.shape) == 2:
        ref[:, sl] = val
    else:
        ref[:, :, sl] = val.reshape(ref.shape[0], ref.shape[1], val.shape[1])


def _att_spec(nb, dil, seg, width, col, back):
    bps = nb // dil

    def plain(n):
        return jnp.clip(n - back, 0, nb - 1), col

    def segmented(n):
        m = jnp.clip(n - back, 0, nb - 1)
        return 0, m // bps, m % bps, 0, col

    if seg == 1:
        return pl.BlockSpec((ATT_BLOCK, width), plain)
    return pl.BlockSpec((seg, None, None, ATT_BLOCK // seg, width), segmented)


def _att_shape(nb, dil, seg, width):
    t = nb * ATT_BLOCK
    return (t, width) if seg == 1 else (seg, dil, nb // dil, ATT_BLOCK // seg, width)


def _att_view(a, nb, dil, seg):
    return a.reshape(_att_shape(nb, dil, seg, a.shape[1]))


def _attn_fwd_block(q_ref, kc_ref, kp_ref, vc_ref, vp_ref, o_ref, l_ref, has_prev, seg, lane0):
    mask, lane = _att_mask(has_prev, seg)
    lo = lane < 64
    nq = ATT_BLOCK
    lse_all = jnp.zeros((nq, 128), F32)
    for hp in range(ATT_HEADS // 2):
        sl = slice(hp * 128, (hp + 1) * 128)
        q2 = _get(q_ref, sl)
        zero = jnp.zeros_like(q2)
        q2 = q2 * 0.125
        qs = jnp.concatenate([jnp.where(lo, q2, zero), jnp.where(lo, zero, q2)], axis=0)
        kk = jnp.concatenate([_get(kp_ref, sl), _get(kc_ref, sl)], axis=0)
        vv = jnp.concatenate([_get(vp_ref, sl), _get(vc_ref, sl)], axis=0)
        s = jnp.where(mask, _dot_nt(qs, kk), NEG)
        mx = jnp.max(s, axis=-1, keepdims=True)
        p = jnp.exp(s - mx)
        l = jnp.sum(p, axis=-1, keepdims=True)
        o = _dot(p.astype(BF16), vv) * (1.0 / l)
        _put(o_ref, sl, jnp.where(lo, o[:nq], o[nq:]).astype(BF16))
        lse = mx + jnp.log(l)
        lse_all = jnp.where(lane == lane0 + 2 * hp, lse[:nq], lse_all)
        lse_all = jnp.where(lane == lane0 + 2 * hp + 1, lse[nq:], lse_all)
    _put(l_ref, slice(None), lse_all)


def _attn_fwd(branches, name):
    t = branches[0][0].shape[0]
    nb = t // ATT_BLOCK
    nbr = len(branches)

    def body(*refs):
        n = pl.program_id(0)
        for i, (_, dil, seg) in enumerate(branches):
            _attn_fwd_block(*refs[5 * i:5 * i + 5], *refs[5 * nbr + 2 * i:5 * nbr + 2 * i + 2],
                            (n % (nb // dil)) != 0, seg, ATT_HEADS * i)

    in_specs, args, out_specs, out_shape = [], [], [], []
    for qkv, dil, seg in branches:
        c0 = qkv.shape[1] // ATT_WIDTH - 3
        in_specs += [_att_spec(nb, dil, seg, ATT_WIDTH, c0 + j, back) for j, back in [(0, 0), (1, 0), (1, 1), (2, 0), (2, 1)]]
        args += [_att_view(qkv, nb, dil, seg)] * 5
        out_specs += [_att_spec(nb, dil, seg, ATT_WIDTH, 0, 0), _att_spec(nb, dil, seg, 128, 0, 0)]
        out_shape += [_sds(_att_shape(nb, dil, seg, ATT_WIDTH), BF16), _sds(_att_shape(nb, dil, seg, 128))]
    out = _call(body, name=name, grid=(nb,), in_specs=in_specs, out_specs=out_specs, out_shape=out_shape)(*args)
    return [(out[2 * i].reshape(t, ATT_WIDTH), out[2 * i + 1].reshape(t, 128)) for i in range(nbr)]


def _combine_mix_in(os_, ls_, fine, o_hg, proj, og, ag, name, tr=512):
    t = os_[0].shape[0]
    nbr = len(os_)

    def body(*refs):
        o_refs, l_refs = refs[:nbr], refs[nbr:2 * nbr]
        oh_ref, g_ref, og_ref, ag_ref, a_ref, lt_ref, lts_ref, m_ref, to_natural, to_sub = refs[2 * nbr:]

        @pl.when(pl.program_id(0) == 0)
        def _():
            to_natural[...] = _regroup_matrix(tr, fine)
            to_sub[...] = _regroup_matrix(tr, tr // fine)

        lane = lax.broadcasted_iota(jnp.int32, (1, 128), 1)
        packed = l_refs[0][...] + _regroup(to_natural[...], sum(r[...] for r in l_refs[1:]).reshape(tr, 128))
        ls = [packed if i == 0 else pltpu.roll(packed, 128 - ATT_HEADS * i, 1) for i in range(nbr)]
        mx = functools.reduce(jnp.maximum, ls)
        tot = mx + jnp.log(sum(jnp.exp(l - mx) for l in ls))
        ws = [jnp.exp(l - tot) for l in ls]
        tot = jnp.where(lane < ATT_HEADS, tot, 0.0)
        lt_ref[...] = tot
        lts_ref[...] = _regroup(to_sub[...], tot).reshape(lts_ref.shape)
        o_vals = [o_refs[0]] + [_regroup(to_natural[...], r[...].reshape(tr, ATT_WIDTH)) for r in o_refs[1:]]
        pairs = []
        for hp in range(ATT_HEADS // 2):
            sl = slice(hp * 128, (hp + 1) * 128)
            acc = jnp.zeros((tr, 128), F32)
            for w, o in zip(ws, o_vals):
                wf = jnp.where(lane < 64, w[:, 2 * hp:2 * hp + 1], w[:, 2 * hp + 1:2 * hp + 2])
                acc = acc + wf * o[:, sl]
            pairs.append(acc)
        av = jnp.concatenate(pairs, axis=1)
        a_ref[...] = av
        m_ref[:, HG_WIDTH:] = (av * _rms(av) * ag_ref[...]).astype(BF16)
        for h in range(HG_WIDTH // HG_HEAD):
            sl = slice(h * HG_HEAD, (h + 1) * HG_HEAD)
            oh = oh_ref[:, sl].astype(F32)
            gv = g_ref[:, sl].astype(F32)
            m_ref[:, sl] = (oh * _rms(oh) * og_ref[...] * (gv * _sigmoid(gv))).astype(BF16)

    half = _rows(tr, ATT_WIDTH)
    sub = lambda a: a.reshape(fine, t // fine, a.shape[1])
    att, lse, lse_sub, mixin = _call(
        body, name=name, grid=(t // tr,),
        in_specs=[half] + [_sub_rows(tr, fine, ATT_WIDTH)] * (nbr - 1) + [_rows(tr, 128)] + [_sub_rows(tr, fine, 128)] * (nbr - 1)
        + [half, pl.BlockSpec((tr, HG_WIDTH), lambda i: (i, 3)), _vec(HG_HEAD), _vec(ATT_WIDTH)],
        out_specs=[half, _rows(tr, 128), _sub_rows(tr, fine, 128), _rows(tr, D_MODEL)],
        out_shape=[_sds((t, ATT_WIDTH)), _sds((t, 128)), _sds((fine, t // fine, 128)), _sds((t, D_MODEL), BF16)],
        scratch_shapes=[pltpu.VMEM((tr, tr), BF16)] * 2)(
            os_[0], *map(sub, os_[1:]), ls_[0], *map(sub, ls_[1:]), o_hg, proj, og, ag)
    return att, lse, lse_sub.reshape(t, 128), mixin


def _attn_bwd_block(q_ref, k_ref, v_ref, do_ref, l_ref, d_ref, out_ref, carry, prev, has_prev, seg):
    w = ATT_WIDTH
    nq = ATT_BLOCK
    mask, lane = _att_mask(has_prev, seg)
    lo = lane < 64
    lse, ddv = _get(l_ref, slice(None)), _get(d_ref, slice(None))
    for hp in range(ATT_HEADS // 2):
        sl = slice(hp * 128, (hp + 1) * 128)
        sk = slice(w + hp * 128, w + (hp + 1) * 128)
        sv = slice(2 * w + hp * 128, 2 * w + (hp + 1) * 128)
        q2, do2 = _get(q_ref, sl), _get(do_ref, sl)
        zero = jnp.zeros_like(q2)
        q2 = q2 * 0.125
        qs = jnp.concatenate([jnp.where(lo, q2, zero), jnp.where(lo, zero, q2)], axis=0)
        dos = jnp.concatenate([jnp.where(lo, do2, zero), jnp.where(lo, zero, do2)], axis=0)
        kc, vc = _get(k_ref, sl), _get(v_ref, sl)
        kk = jnp.concatenate([prev[:, sl], kc], axis=0)
        vv = jnp.concatenate([prev[:, sk], vc], axis=0)
        prev[:, sl] = kc
        prev[:, sk] = vc
        ls = jnp.concatenate([lse[:, 2 * hp:2 * hp + 1], lse[:, 2 * hp + 1:2 * hp + 2]], axis=0)
        dh = jnp.concatenate([ddv[:, 2 * hp:2 * hp + 1], ddv[:, 2 * hp + 1:2 * hp + 2]], axis=0)
        p = jnp.exp(jnp.where(mask, _dot_nt(qs, kk) - ls, NEG))
        ds = (p * (_dot_nt(dos, vv) - dh)).astype(BF16)
        dq = _dot(ds, kk) * 0.125
        dk = _dot_tn(ds, qs)
        dv = _dot_tn(p.astype(BF16), dos)
        _put(out_ref, sl, carry[:, sl].astype(BF16))
        _put(out_ref, sk, (carry[:, sk] + dk[:nq]).astype(BF16))
        _put(out_ref, sv, (carry[:, sv] + dv[:nq]).astype(BF16))
        carry[:, sl] = jnp.where(lo, dq[:nq], dq[nq:])
        carry[:, sk] = dk[nq:]
        carry[:, sv] = dv[nq:]


def _attn_bwd(branches, name):
    t = branches[0][0].shape[0]
    nb = t // ATT_BLOCK
    nbr = len(branches)
    w = ATT_WIDTH

    def body(*refs):
        ins, outs, carries, prevs = refs[:6 * nbr], refs[6 * nbr:7 * nbr], refs[7 * nbr:8 * nbr], refs[8 * nbr:]
        n = pl.program_id(0)

        @pl.when(n == 0)
        def _():
            for scratch in carries + prevs:
                scratch[...] = jnp.zeros_like(scratch)

        @pl.when(n < nb)
        def _():
            for i, branch in enumerate(branches):
                dil, seg = branch[4:]
                _attn_bwd_block(*ins[6 * i:6 * i + 6], outs[i], carries[i], prevs[i], (n % (nb // dil)) != 0, seg)

        @pl.when(n == nb)
        def _():
            for i in range(nbr):
                _put(outs[i], slice(None), carries[i][...].astype(BF16))

    in_specs, args, out_specs, out_shape = [], [], [], []
    for qkv, dout, lse, dd, dil, seg in branches:
        c0 = qkv.shape[1] // w - 3
        in_specs += [_att_spec(nb, dil, seg, w, c0 + j, 0) for j in range(3)]
        in_specs += [_att_spec(nb, dil, seg, w, 0, 0), _att_spec(nb, dil, seg, 128, 0, 0), _att_spec(nb, dil, seg, 128, 0, 0)]
        args += [_att_view(a, nb, dil, seg) for a in [qkv] * 3 + [dout, lse, dd]]
        out_specs += [_att_spec(nb, dil, seg, 3 * w, 0, 1)]
        out_shape += [_sds(_att_shape(nb, dil, seg, 3 * w), BF16)]
    out = _call(body, name=name, grid=(nb + 1,), in_specs=in_specs, out_specs=out_specs, out_shape=out_shape,
                scratch_shapes=[pltpu.VMEM((ATT_BLOCK, 3 * w), F32)] * nbr + [pltpu.VMEM((ATT_BLOCK, 2 * w), BF16)] * nbr)(*args)
    return [o.reshape(t, 3 * w) for o in out]


def _local_step(x, tgt, mod, norm1_g, lb_logits, og, ag, norm2_g, fg, get_w, put_g, late=lambda a: a, project=None):
    shift1, scale1, gate1, shift2, scale2, gate2 = [mod[:, i * D_MODEL:(i + 1) * D_MODEL] for i in range(6)]
    fg = fg.reshape(1, D_MODEL)

    h1 = _norm_mod(x, norm1_g, scale1, shift1, "norm_mod1")
    if project is None:
        w_in = get_w("w_in", h1)
        proj = _mm_nn(h1, w_in, "mm_in", out_dtype=BF16)
    else:
        proj, w_in = project(h1)
    o_hg, states = _hgrn_fwd(proj, lb_logits, "hgrn_fwd")
    fine = DILATIONS[-1]
    layouts = [(d, 1 if d == 1 else fine // d) for d in DILATIONS]
    qkv_fine = _to_sub(proj, fine)
    qkvs = [proj if d == 1 else qkv_fine for d in DILATIONS]
    natural = lambda a, d: a if d == 1 else _from_sub(a, fine)
    outs = _attn_fwd([(q, d, seg) for q, (d, seg) in zip(qkvs, layouts)], "attn_fwd")
    att, lse, lse_fine, mixin = _combine_mix_in([o for o, _ in outs], [l for _, l in outs], fine,
                                                o_hg, proj, og, ag, "attn_combine_mix_in")
    w_out = get_w("w_out", mixin)
    mix, x2, h2 = _out_resid_norm_mod(x, mixin, w_out, gate1, norm2_g, scale2, shift2, "mm_out_resid_norm_mod2")
    w_gu = get_w("w_gu", h2)
    a_ff, u_ff, act = _mm_gate_up(h2, w_gu, "mm_gu")
    w_down = get_w("w_down", act)
    dx3, dffn, loss_v, dfg, dgate2 = _down_loss(x2, act, w_down, gate2, fg, tgt, "mm_down_loss")

    dffn = put_g("w_down", *_mm_tn(act, dffn, 1, "mm_down_dw", tm=2048, tk=D_FF // 2), dffn)
    dau = _mm_down_dx(dffn, w_down, a_ff, u_ff, "mm_down_dx")
    dau = put_g("w_gu", *_mm_tn(h2, dau, N_SHARD, "mm_gu_dw", tm=x.shape[0], tk=512), dau)
    dx2, dshift2, dscale2, dg2, dgate1, dmix = _norm_mod_bwd(
        dau, x2, norm2_g, scale2, dx3, "mm_gu_dx_norm_bwd", gate=gate1, mix=mix, w=w_gu)
    dmix = put_g("w_out", *_mm_tn(mixin, dmix, 1, "mm_out_dw", tm=x.shape[0], tk=512), dmix)
    do_hg, dproj, datt, dd, datt_fine, dd_fine, dog, dag = _mix_in_bwd(
        dmix, w_out, o_hg, proj, att, og, ag, fine, "mm_out_dx_mix_in_bwd")
    datts = _attn_bwd([(q,) + ((datt, lse, dd) if d == 1 else (datt_fine, lse_fine, dd_fine)) + (d, seg)
                       for q, (d, seg) in zip(qkvs, layouts)], "attn_bwd")
    dproj, dlb = _hgrn_bwd(proj, lb_logits, states, do_hg, dproj, "hgrn_bwd")
    dproj = late(dproj)
    dproj = _dproj(dproj, [natural(a, d) for a, d in zip(datts, DILATIONS)], "dproj")
    dproj = put_g("w_in", *_mm_tn(h1, dproj, N_SHARD, "mm_in_dw", tm=x.shape[0], tk=512, group=2), dproj)
    dx, dshift1, dscale1, dg1 = _norm_mod_bwd(dproj, x, norm1_g, scale1, dx2, "mm_in_dx_norm_bwd", w=w_in)

    stats = jnp.concatenate([loss_v, dfg, dg2, dg1, dlb, dag, dog,
                             dshift1, dscale1, dgate1, dshift2, dscale2, dgate2], axis=1)
    return dx, stats


def _place():
    x, y, c = lax.axis_index("x"), lax.axis_index("y"), lax.axis_index("c")
    return x, y, c


def _chip_peers(x, y, c):
    return [(1 - x, y, c), (x, 1 - y, c), (1 - x, 1 - y, c)]


_HBM = pl.BlockSpec(memory_space=pltpu.HBM)
_SEM = pl.BlockSpec(memory_space=pltpu.SEMAPHORE)
_EFFECT = pltpu.SideEffectType.DATAFLOW_SIDE_EFFECTING


def _exchange_copy(bufs, send, recv, j, peer, place, kind):
    x, y, c = place
    target = peer
    if kind == "gather":
        src = dst = bufs[0].at[2 * x + y]
    elif kind == "scatter":
        src, dst = bufs[0].at[2 * peer[0] + peer[1]], bufs[1].at[j]
    else:
        half = bufs[0].shape[1] // 2
        rows = pl.ds(c * half, half)
        if kind == "half":
            src = dst = bufs[0].at[2 * x + y, rows]
        else:
            src = dst = bufs[0].at[2 * peer[0] + peer[1], rows]
            target = (x, y, 1 - c)
    return pltpu.make_async_remote_copy(src_ref=src, dst_ref=dst, send_sem=send.at[j], recv_sem=recv.at[j],
                                        device_id=target, device_id_type=MESH)


def _exchange_start(groups, after, kind, name):
    sizes = [len(g) for g in groups]
    flat = [b for g in groups for b in g]
    ng, nb = len(groups), len(flat)

    def body(*refs):
        bufs, sems = refs[:nb], refs[nb + 1:nb + 1 + 2 * ng]
        x, y, c = _place()
        for j, peer in enumerate(_chip_peers(x, y, c)):
            at = 0
            for i, size in enumerate(sizes):
                _exchange_copy(bufs[at:at + size], sems[2 * i], sems[2 * i + 1], j, peer, (x, y, c), kind).start()
                at += size

    any_space = pl.BlockSpec(memory_space=pl.ANY)
    out = pl.pallas_call(
        body, name=name, in_specs=[_HBM] * nb + [any_space],
        out_specs=[_SEM] * (2 * ng) + [_HBM] * nb + [any_space],
        out_shape=[pltpu.SemaphoreType.DMA((3,))] * (2 * ng) + [pltpu.HBM(b.shape, b.dtype) for b in flat]
        + [_sds(after.shape, after.dtype)],
        input_output_aliases={i: 2 * ng + i for i in range(nb + 1)},
        compiler_params=pltpu.CompilerParams(has_side_effects=_EFFECT),
    )(*[pltpu.with_memory_space_constraint(b, pltpu.HBM) for b in flat], after)
    started, at = [], 2 * ng
    for i, size in enumerate(sizes):
        started.append((out[2 * i], out[2 * i + 1], tuple(out[at:at + size])))
        at += size
    return started, out[-1]


def _exchange_wait(started, after, kind, name):
    send, recv, bufs = started
    nb = len(bufs)

    def body(*refs):
        x, y, c = _place()
        for j, peer in enumerate(_chip_peers(x, y, c)):
            cp = _exchange_copy(refs[:nb], refs[nb], refs[nb + 1], j, peer, (x, y, c), kind)
            cp.wait_send()
            cp.wait_recv()

    return pl.pallas_call(
        body, name=name, in_specs=[_HBM] * nb + [_SEM, _SEM, pl.BlockSpec(memory_space=pl.ANY)],
        out_specs=[_HBM] * nb, out_shape=[pltpu.HBM(b.shape, b.dtype) for b in bufs],
        input_output_aliases={i: i for i in range(nb)},
        compiler_params=pltpu.CompilerParams(has_side_effects=_EFFECT),
    )(*bufs, send, recv, after)


def _sibling_copies(v_refs, l_refs, send, recv):
    x, y, c = _place()
    return [pltpu.make_async_remote_copy(src_ref=v, dst_ref=l, send_sem=send.at[a], recv_sem=recv.at[a],
                                         device_id=(x, y, 1 - c), device_id_type=MESH)
            for a, (v, l) in enumerate(zip(v_refs, l_refs))]


def _sibling_start(vs, after, name):
    vs = list(vs)
    n = len(vs)
    lands = [lax.empty(v.shape, v.dtype) for v in vs]

    def body(*refs):
        for cp in _sibling_copies(refs[:n], refs[n:2 * n], refs[2 * n + 1], refs[2 * n + 2]):
            cp.start()

    any_space = pl.BlockSpec(memory_space=pl.ANY)
    out = pl.pallas_call(
        body, name=name, in_specs=[_HBM] * (2 * n) + [any_space],
        out_specs=[_SEM, _SEM] + [_HBM] * (2 * n) + [any_space],
        out_shape=[pltpu.SemaphoreType.DMA((n,))] * 2 + [pltpu.HBM(b.shape, b.dtype) for b in vs + lands]
        + [_sds(after.shape, after.dtype)],
        input_output_aliases={i: 2 + i for i in range(2 * n + 1)},
        compiler_params=pltpu.CompilerParams(has_side_effects=_EFFECT),
    )(*[pltpu.with_memory_space_constraint(b, pltpu.HBM) for b in vs + lands], after)
    return (out[0], out[1], tuple(out[2:2 + n]), tuple(out[2 + n:2 + 2 * n])), out[-1]


def _sibling_wait(started, after, name):
    send, recv, vs, lands = started
    n = len(vs)

    def body(*refs):
        for cp in _sibling_copies(refs[:n], refs[n:2 * n], refs[2 * n], refs[2 * n + 1]):
            cp.wait_send()
            cp.wait_recv()

    out = pl.pallas_call(
        body, name=name, in_specs=[_HBM] * (2 * n) + [_SEM, _SEM, pl.BlockSpec(memory_space=pl.ANY)],
        out_specs=[_HBM] * (2 * n), out_shape=[pltpu.HBM(b.shape, b.dtype) for b in vs + lands],
        input_output_aliases={i: i for i in range(2 * n)},
        compiler_params=pltpu.CompilerParams(has_side_effects=_EFFECT),
    )(*vs, *lands, send, recv, after)
    return out[:n], out[n:]


def _everyone(x, y, c):
    return [(1 - x if k & 4 else x, 1 - y if k & 2 else y, 1 - c if k & 1 else c) for k in range(1, 8)]


def _all_gather_copies(land_ref, send, recv, arriving):
    x, y, c = _place()
    me = 4 * x + 2 * y + c
    return [pltpu.make_async_remote_copy(
        src_ref=land_ref.at[me], dst_ref=land_ref.at[4 * p[0] + 2 * p[1] + p[2] if arriving else me],
        send_sem=send.at[k], recv_sem=recv.at[k], device_id=p, device_id_type=MESH)
        for k, p in enumerate(_everyone(x, y, c))]


def _all_gather_start(v, name):
    x, y, c = _place()
    land = lax.dynamic_update_slice(lax.empty((8,) + v.shape, v.dtype), v[None], (4 * x + 2 * y + c, 0, 0))

    def body(land_ref, v_ref, send, recv, land_out, v_out):
        for cp in _all_gather_copies(land_ref, send, recv, False):
            cp.start()

    any_space = pl.BlockSpec(memory_space=pl.ANY)
    out = pl.pallas_call(
        body, name=name, in_specs=[_HBM, any_space], out_specs=[_SEM, _SEM, _HBM, any_space],
        out_shape=[pltpu.SemaphoreType.DMA((7,))] * 2 + [pltpu.HBM(land.shape, land.dtype), _sds(v.shape, v.dtype)],
        input_output_aliases={0: 2, 1: 3}, compiler_params=pltpu.CompilerParams(has_side_effects=_EFFECT),
    )(pltpu.with_memory_space_constraint(land, pltpu.HBM), v)
    return tuple(out[:3]), out[3]


def _all_gather_wait(started, after, name):
    send, recv, land = started

    def body(land_ref, send, recv, after_ref, land_out):
        for cp in _all_gather_copies(land_ref, send, recv, True):
            cp.wait_send()
            cp.wait_recv()

    return pl.pallas_call(
        body, name=name, in_specs=[_HBM, _SEM, _SEM, pl.BlockSpec(memory_space=pl.ANY)], out_specs=_HBM,
        out_shape=pltpu.HBM(land.shape, land.dtype), input_output_aliases={0: 0},
        compiler_params=pltpu.CompilerParams(has_side_effects=_EFFECT),
    )(land, send, recv, after)


def _cast_place(ws, shard, name, after=()):
    n = len(ws)

    def body(s_ref, *refs):
        for w_ref, o_ref in zip(refs[:n], refs[-n:]):
            o_ref[0] = w_ref[...].astype(BF16)

    return pl.pallas_call(
        body, name=name, out_shape=[_sds((N_SHARD,) + w.shape, BF16) for w in ws],
        grid_spec=pltpu.PrefetchScalarGridSpec(
            num_scalar_prefetch=1, grid=(4,),
            in_specs=[pl.BlockSpec((w.shape[0] // 4, w.shape[1]), lambda i, s: (i, 0)) for w in ws]
            + [pl.BlockSpec(memory_space=pl.ANY)] * len(after),
            out_specs=[pl.BlockSpec((1, w.shape[0] // 4, w.shape[1]), lambda i, s: (s[0], i, 0)) for w in ws]),
        compiler_params=pltpu.CompilerParams(dimension_semantics=("arbitrary",), vmem_limit_bytes=VMEM_LIMIT),
    )(shard.reshape(1).astype(jnp.int32), *ws, *after)


def _mod_rows(c8, w_ada, b_ada, name):
    n = w_ada.shape[1]

    def gather(src_ref, dst_ref, send, recv, loc, base):
        x, y, c = _place()
        me = 4 * x + 2 * y + c
        own = pltpu.make_async_copy(src_ref, dst_ref.at[me], loc)
        own.start()
        peers = _everyone(x, y, c)
        sends = [pltpu.make_async_remote_copy(src_ref=src_ref, dst_ref=dst_ref.at[me], send_sem=send.at[base + k],
                                              recv_sem=recv.at[base + k], device_id=p, device_id_type=MESH)
                 for k, p in enumerate(peers)]
        for cp in sends:
            cp.start()
        for k, p in enumerate(peers):
            pltpu.make_async_remote_copy(src_ref=src_ref, dst_ref=dst_ref.at[4 * p[0] + 2 * p[1] + p[2]],
                                         send_sem=send.at[base + k], recv_sem=recv.at[base + k], device_id=p,
                                         device_id_type=MESH).wait_recv()
        for cp in sends:
            cp.wait_send()
        own.wait()

    def body(c_ref, w_ref, b_ref, a_ref, parts_ref, c_all, part, send, recv, loc):
        gather(c_ref, c_all, send, recv, loc.at[0], 0)
        cv = jnp.max(c_all[...], axis=1)
        ca = cv * _sigmoid(cv)
        a_ref[...] = ca
        part[...] = jnp.dot(ca, w_ref[...], precision=lax.Precision.HIGHEST, preferred_element_type=F32) + b_ref[...]
        gather(part, parts_ref, send, recv, loc.at[1], 7)

    vmem = pl.BlockSpec(memory_space=pltpu.VMEM)
    return pl.pallas_call(
        body, name=name, in_specs=[vmem] * 3, out_specs=[vmem, vmem],
        out_shape=[_sds((8, D_MODEL)), _sds((8, 8, n))],
        scratch_shapes=[pltpu.VMEM((8, 8, D_MODEL), F32), pltpu.VMEM((8, n), F32), pltpu.SemaphoreType.DMA((14,)),
                        pltpu.SemaphoreType.DMA((14,)), pltpu.SemaphoreType.DMA((2,))],
        compiler_params=pltpu.CompilerParams(vmem_limit_bytes=VMEM_LIMIT))(c8, w_ada, b_ada)


def _sum_received(gs, shard, lands, name):
    n = len(gs)

    def body(s_ref, *refs):
        for g_ref, l_ref, o_ref in zip(refs[:n], refs[n:2 * n], refs[2 * n:]):
            o_ref[...] = ((g_ref[0] + l_ref[0].astype(F32)) + l_ref[1].astype(F32)) + l_ref[2].astype(F32)

    quarter = lambda g: (g.shape[1] // 4, g.shape[2])
    return pl.pallas_call(
        body, name=name, out_shape=[_sds(g.shape[1:]) for g in gs],
        grid_spec=pltpu.PrefetchScalarGridSpec(
            num_scalar_prefetch=1, grid=(4,),
            in_specs=[pl.BlockSpec((1,) + quarter(g), lambda i, s: (s[0], i, 0)) for g in gs]
            + [pl.BlockSpec((3,) + quarter(g), lambda i, s: (0, i, 0)) for g in gs],
            out_specs=[pl.BlockSpec(quarter(g), lambda i, s: (i, 0)) for g in gs]),
        compiler_params=pltpu.CompilerParams(dimension_semantics=("arbitrary",), vmem_limit_bytes=VMEM_LIMIT),
    )(shard.reshape(1).astype(jnp.int32), *gs, *lands)


def _adamw_outer(w, ct, dm, m, v, name):
    k, n = w.shape
    tr = k // 4

    def body(w_ref, c_ref, d_ref, m_ref, v_ref, g_out, d_out, m_out, v_out):
        cv = c_ref[...]
        dv = d_ref[...]
        g = cv[:, 0:1] * dv[0:1, :]
        for i in range(1, 8):
            g = g + cv[:, i:i + 1] * dv[i:i + 1, :]
        g_out[...] = g
        d_out[...], m_out[...], v_out[...] = _adamw_math(w_ref[...], g, m_ref[...], v_ref[...])

    row = _rows(tr, n)
    return _call(body, name=name, grid=(4,),
                 in_specs=[row, _rows(tr, 8), pl.BlockSpec((8, n), lambda i: (0, 0)), row, row],
                 out_specs=[row] * 4, out_shape=[_sds((k, n))] * 4)(w, ct, dm, m, v)


def _adamw_math(w, g, m, v):
    m_new = ADAM_B1 * m + (1.0 - ADAM_B1) * g
    v_new = ADAM_B2 * v + (1.0 - ADAM_B2) * (g * g)
    m_hat = m_new / (1.0 - ADAM_B1 ** ADAM_STEP)
    v_hat = v_new / (1.0 - ADAM_B2 ** ADAM_STEP)
    return -ADAM_LR * (m_hat / (jnp.sqrt(v_hat) + ADAM_EPS) + ADAM_WD * w), m_new, v_new


def _small_update(stats, smalls, name):
    offsets = [ST_DMOD, ST_DG1, ST_DLB, ST_DOG, ST_DAG, ST_DG2, ST_DFG]
    lb_index = 2

    def body(*refs):
        s_ref, ins, l_ref, outs = refs[0], refs[1:22], refs[22], refs[23:]
        tot = s_ref[0:1, :]
        for i in range(1, 8):
            tot = tot + s_ref[i:i + 1, :]
        l_ref[...] = jnp.zeros((1, 128), F32) + (0.5 / D_MODEL) * jnp.sum(tot[:, ST_LOSS:ST_LOSS + D_MODEL])
        for p, off in enumerate(offsets):
            w_ref, m_ref, v_ref = ins[3 * p:3 * p + 3]
            g_out, d_out, m_out, v_out = outs[4 * p:4 * p + 4]
            g = tot[:, off:off + w_ref.shape[1]]
            if p == lb_index:
                lg = w_ref[...]
                lb = _sigmoid(lg[0:1] - lg[1:2])
                g = g * lb * (1.0 - lb)
            for r in range(w_ref.shape[0]):
                rows = slice(r, r + 1)
                gr = g if r == 0 else -g
                delta, m_new, v_new = _adamw_math(w_ref[rows, :], gr, m_ref[rows, :], v_ref[rows, :])
                g_out[rows, :] = gr
                d_out[rows, :] = delta
                m_out[rows, :] = m_new
                v_out[rows, :] = v_new

    full = lambda a: pl.BlockSpec(a.shape, lambda i: (0, 0))
    flat = [a for t in smalls for a in t]
    return _call(body, name=name, grid=(1,),
                 in_specs=[full(stats)] + [full(a) for a in flat],
                 out_specs=[pl.BlockSpec((1, 128), lambda i: (0, 0))] + [full(t[0]) for t in smalls for _ in range(4)],
                 out_shape=[_sds((1, 128))] + [_sds(t[0].shape) for t in smalls for _ in range(4)])(stats, *flat)


def _adamw(params, name):
    n = len(params)

    def body(*refs):
        for p in range(n):
            w_ref, ga_ref, gb_ref, m_ref, v_ref = refs[5 * p:5 * p + 5]
            g_out, d_out, m_out, v_out = refs[5 * n + 4 * p:5 * n + 4 * p + 4]
            g = ga_ref[...] + gb_ref[...]
            g_out[...] = g
            d_out[...], m_out[...], v_out[...] = _adamw_math(w_ref[...], g, m_ref[...], v_ref[...])

    row = lambda w: _rows(w.shape[0] // 4, w.shape[1])
    out = _call(body, name=name, grid=(4,), in_specs=[row(p[0]) for p in params for _ in range(5)],
                out_specs=[row(p[0]) for p in params for _ in range(4)],
                out_shape=[_sds(p[0].shape) for p in params for _ in range(4)])(*[a for p in params for a in p])
    return [tuple(out[4 * p:4 * p + 4]) for p in range(n)]


def kernel(x, c, w_ada, b_ada, norm1_g, w_in, hg_lb_logits, hg_onorm_g, att_onorm_g, w_out, norm2_g, w_gate_up, w_down, final_g, loss_target, m_w_ada, m_b_ada, m_norm1_g, m_w_in, m_hg_lb_logits, m_hg_onorm_g, m_att_onorm_g, m_w_out, m_norm2_g, m_w_gate_up, m_w_down, m_final_g, v_w_ada, v_b_ada, v_norm1_g, v_w_in, v_hg_lb_logits, v_hg_onorm_g, v_att_onorm_g, v_w_out, v_norm2_g, v_w_gate_up, v_w_down, v_final_g):
    ix, iy, ic = _place()
    shard = 2 * ix + iy
    sample = 4 * ix + 2 * iy + ic
    n_ada = w_ada.shape[2]

    shards = [w_in[0], w_out[0], w_gate_up[0], w_down[0]]
    names = ["w_in", "w_out", "w_gu", "w_down"]
    shapes = [(N_SHARD,) + w.shape for w in shards]
    placed = [(_cast_place(shards[:1], shard, "place_w_in")[0],)]

    b_part = lax.dynamic_slice(b_ada, (0, shard * n_ada), (1, n_ada))
    c_act, parts = _mod_rows(jnp.broadcast_to(c, (8, D_MODEL)), w_ada[0], b_part, "mod_rows")
    parts = parts[::2]
    mod = lax.dynamic_index_in_dim(parts, sample, axis=1, keepdims=False).reshape(1, 6 * D_MODEL)
    (first,), mod = _exchange_start(placed[:1], mod, "half", "gather_start_w_in")
    gathering = {}

    def get_w(name, after):
        if name == "w_in":
            placed_rest = [(p,) for p in _cast_place(shards[1:], shard, "place_rest", (after,))]
            halves = _exchange_wait(first, placed_rest[0][0], "half", "gather_wait_w_in")
            (passing,), token = _exchange_start([tuple(halves)], mod, "forward", "forward_start_w_in")
            rest, token = _exchange_start(placed_rest, token, "gather", "gather_start_rest")
            (full,) = _exchange_wait(passing, token, "forward", "forward_wait_w_in")
            gathering.update(zip(names[1:], rest))
            return full
        (full,) = _exchange_wait(gathering[name], after, "gather", "gather_wait_" + name)
        return full if name == "w_gu" else full.reshape(1, -1, D_MODEL)

    scattering = {}

    def put_g(name, g, g_bf16, then):
        shape = shapes[names.index(name)]
        land = lax.empty((3,) + shape[1:], BF16)
        (started,), then = _exchange_start([(g_bf16.reshape(shape), land)], then, "scatter", "scatter_start_" + name)
        scattering[name] = (g.reshape(shape), started)
        return then

    def summed(group, after, tag):
        lands = [_exchange_wait(scattering[nm][1], after, "scatter", "scatter_wait_" + nm)[1] for nm in group]
        return _sum_received([scattering[nm][0] for nm in group], shard, lands, "sum_" + tag)

    early = ["w_down", "w_gu", "w_out"]
    swapping = []

    def late(a):
        started, a = _sibling_start(summed(early, a, "early"), a, "swap_start")
        swapping.append(started)
        return a

    def project(h1):
        own = _mm_own_shard(h1, shards[0], shard, N_SHARD, "mm_in_own")
        w_full = get_w("w_in", own)
        return _mm_other_shards(h1, w_full, own, shard, "mm_in_rest"), w_full

    dx, stats = _local_step(x[0], loss_target[0], mod, norm1_g, hg_lb_logits, hg_onorm_g, att_onorm_g,
                            norm2_g, final_g, get_w, put_g, late, project)

    gathering_stats, stats = _all_gather_start(stats, "stats_start")
    moments = [(m_w_in, v_w_in), (m_w_out, v_w_out), (m_w_gate_up, v_w_gate_up), (m_w_down, v_w_down)]

    def update(group, sums, other, tag):
        params = [(shards[names.index(nm)], s, o, moments[names.index(nm)][0][0], moments[names.index(nm)][1][0])
                  for nm, s, o in zip(group, sums, other)]
        return dict(zip(group, _adamw(params, "adamw_" + tag)))

    swapping_in, stats = _sibling_start(summed(["w_in"], stats, "w_in"), stats, "swap_start_w_in")
    sums, other = _sibling_wait(swapping[0], stats, "swap_wait")
    done = update(early, sums, other, "early")
    sum_in, other_in = _sibling_wait(swapping_in, done["w_out"][1], "swap_wait_w_in")
    done.update(update(["w_in"], sum_in, other_in, "w_in"))

    stats_all = _all_gather_wait(gathering_stats, done["w_in"][1], "stats_wait").reshape(8, ST_WIDTH)
    dmod = lax.dynamic_slice(stats_all, (0, ST_DMOD + shard * n_ada), (8, n_ada))

    as_row = lambda a: a.reshape(1, -1) if a.ndim == 1 else a
    smalls = [tuple(as_row(a) for a in t) for t in [
        (b_ada, m_b_ada, v_b_ada), (norm1_g, m_norm1_g, v_norm1_g),
        (hg_lb_logits, m_hg_lb_logits, v_hg_lb_logits), (hg_onorm_g, m_hg_onorm_g, v_hg_onorm_g),
        (att_onorm_g, m_att_onorm_g, v_att_onorm_g), (norm2_g, m_norm2_g, v_norm2_g),
        (final_g, m_final_g, v_final_g)]]
    loss, *small_out = _small_update(stats_all, smalls, "small_update")
    shapes_out = [b_ada.shape, norm1_g.shape, hg_lb_logits.shape, hg_onorm_g.shape, att_onorm_g.shape,
                  norm2_g.shape, final_g.shape]
    sg, sd, sm, sv = [[small_out[4 * p + i].reshape(shapes_out[p]) for p in range(7)] for i in range(4)]

    ada = _adamw_outer(w_ada[0], c_act.T, dmod, m_w_ada[0], v_w_ada[0], "adamw_w_ada")
    big = [ada] + [done[nm] for nm in names]
    bg, bd, bm, bv = [[t[i][None] for t in big] for i in range(4)]

    def order(b, s):
        return [b[0], s[0], s[1], b[1], s[2], s[3], s[4], b[2], s[5], b[3], b[4], s[6]]

    return (loss[0, 0], dx[None], *order(bg, sg), *order(bd, sd), *order(bm, sm), *order(bv, sv))
```

```python
import functools

import jax
import jax.numpy as jnp
from jax import lax
from jax.experimental import pallas as pl
from jax.experimental.pallas import tpu as pltpu

F32 = jnp.float32
BF16 = jnp.bfloat16
MESH = pl.DeviceIdType.MESH

D_MODEL = 1024
HG_WIDTH = 512
HG_HEAD = 128
HG_CHUNK = 64
HG_GROUP = 4
ATT_WIDTH = 512
ATT_HEADS = 8
ATT_BLOCK = 128
DILATIONS = (1, 4, 16)
D_FF = 2816
IN_WIDTH = 3584
N_SHARD = 4
RMS_EPS = 1e-6
NEG = -1e30

ADAM_LR = 0.001
ADAM_B1 = 0.9
ADAM_B2 = 0.999
ADAM_EPS = 1e-08
ADAM_WD = 0.01
ADAM_STEP = 10

VMEM_LIMIT = 56 * 2**20

ST_LOSS, ST_DFG, ST_DG2, ST_DG1 = 0, 1024, 2048, 3072
ST_DLB, ST_DAG, ST_DOG, ST_DMOD = 4096, 4608, 5120, 5248
ST_WIDTH = 5248 + 6144


def _call(body, *, name, grid, in_specs, out_specs, out_shape, scratch_shapes=(), aliases=None):
    return pl.pallas_call(
        body, name=name, grid=grid, in_specs=in_specs, out_specs=out_specs, out_shape=out_shape,
        scratch_shapes=list(scratch_shapes), input_output_aliases=aliases or {},
        compiler_params=pltpu.CompilerParams(
            dimension_semantics=("arbitrary",) * len(grid), vmem_limit_bytes=VMEM_LIMIT))


def _sds(shape, dtype=F32):
    return jax.ShapeDtypeStruct(shape, dtype)


def _dot(a, b):
    return jnp.dot(a, b, preferred_element_type=F32)


def _dot_nt(a, b):
    return lax.dot_general(a, b, (((1,), (1,)), ((), ())), preferred_element_type=F32)


def _dot_tn(a, b):
    return lax.dot_general(a, b, (((0,), (0,)), ((), ())), preferred_element_type=F32)


def _sigmoid(x):
    return 1.0 / (1.0 + jnp.exp(-x))


def _rows(tr, width):
    return pl.BlockSpec((tr, width), lambda i: (i, 0))


def _vec(width):
    return pl.BlockSpec((1, width), lambda i: (0, 0))


def _acc(ref, val, first):
    @pl.when(first)
    def _():
        ref[...] = val

    @pl.when(jnp.logical_not(first))
    def _():
        ref[...] += val


def _sub_rows(tr, fine, width):
    return pl.BlockSpec((fine, tr // fine, width), lambda i: (0, i, 0))


def _regroup_matrix(tr, groups):
    a = lax.broadcasted_iota(jnp.int32, (tr, tr), 0)
    b = lax.broadcasted_iota(jnp.int32, (tr, tr), 1)
    return (b == (a % groups) * (tr // groups) + a // groups).astype(BF16)


def _regroup(m, v, lanes=None):
    if v.dtype == BF16:
        return _dot(m, v)
    width = v.shape[1]
    packed, out = None, None
    for i in range(3):
        part = v.astype(BF16).astype(F32)
        v = v - part
        if lanes is None:
            out = _dot(m, part.astype(BF16)) if i == 0 else out + _dot(m, part.astype(BF16))
        else:
            packed = part if i == 0 else packed + pltpu.roll(part, i * lanes, 1)
    if lanes is None:
        return out
    out = _dot(m, packed.astype(BF16))
    out = out + pltpu.roll(out, width - lanes, 1) + pltpu.roll(out, width - 2 * lanes, 1)
    return jnp.where(lax.broadcasted_iota(jnp.int32, (1, width), 1) < lanes, out, 0.0)


def _mm_nn(a, b3, name, tm=1024, out_dtype=F32):
    m, k = a.shape
    s, _, n = b3.shape

    def body(a_ref, b_ref, o_ref):
        o_ref[...] = _dot(a_ref[...], b_ref[0]).astype(out_dtype)

    return _call(
        body, name=name, grid=(s, m // tm),
        in_specs=[pl.BlockSpec((tm, k), lambda j, i: (i, 0)), pl.BlockSpec((1, k, n), lambda j, i: (j, 0, 0))],
        out_specs=pl.BlockSpec((tm, n), lambda j, i: (i, j)), out_shape=_sds((m, s * n), out_dtype))(a, b3)


def _mm_own_shard(a, w, shard, s, name, tm=1024):
    m, k = a.shape
    n = w.shape[1]

    def body(s_ref, a_ref, w_ref, o_ref):
        o_ref[...] = _dot(a_ref[...], w_ref[...].astype(BF16)).astype(BF16)

    return pl.pallas_call(
        body, name=name, out_shape=_sds((m, s * n), BF16),
        grid_spec=pltpu.PrefetchScalarGridSpec(
            num_scalar_prefetch=1, grid=(m // tm,),
            in_specs=[pl.BlockSpec((tm, k), lambda i, sh: (i, 0)), pl.BlockSpec((k, n), lambda i, sh: (0, 0))],
            out_specs=pl.BlockSpec((tm, n), lambda i, sh: (i, sh[0]))),
        compiler_params=pltpu.CompilerParams(dimension_semantics=("arbitrary",), vmem_limit_bytes=VMEM_LIMIT),
    )(shard.reshape(1).astype(jnp.int32), a, w)


def _mm_other_shards(a, b3, partial, shard, name, tm=1024):
    m, k = a.shape
    s, _, n = b3.shape
    which = lambda j, sh: (sh[0] + 1 + j) % s

    def body(s_ref, a_ref, b_ref, p_ref, o_ref):
        o_ref[...] = _dot(a_ref[...], b_ref[0]).astype(BF16)

    return pl.pallas_call(
        body, name=name, out_shape=_sds(partial.shape, BF16),
        grid_spec=pltpu.PrefetchScalarGridSpec(
            num_scalar_prefetch=1, grid=(s - 1, m // tm),
            in_specs=[pl.BlockSpec((tm, k), lambda j, i, sh: (i, 0)),
                      pl.BlockSpec((1, k, n), lambda j, i, sh: (which(j, sh), 0, 0)),
                      pl.BlockSpec(memory_space=pl.ANY)],
            out_specs=pl.BlockSpec((tm, n), lambda j, i, sh: (i, which(j, sh)))),
        input_output_aliases={3: 0},
        compiler_params=pltpu.CompilerParams(dimension_semantics=("arbitrary",) * 2, vmem_limit_bytes=VMEM_LIMIT),
    )(shard.reshape(1).astype(jnp.int32), a, b3, partial)


def _mm_tn(a, dy, s, name, tm, tk, group=1):
    m, k = a.shape
    n = dy.shape[1] // s
    steps = m // tm

    def body(a_ref, dy_ref, o_ref, ob_ref):
        p = _dot_tn(a_ref[...], dy_ref[...])
        for g in range(group):
            pg = p[:, g * n:(g + 1) * n]
            if steps == 1:
                o_ref[g] = pg
                ob_ref[g] = pg.astype(BF16)
            else:
                _acc(o_ref.at[g], pg, pl.program_id(2) == 0)
        if steps > 1:
            @pl.when(pl.program_id(2) == steps - 1)
            def _():
                ob_ref[...] = o_ref[...].astype(BF16)

    out = pl.BlockSpec((group, tk, n), lambda kk, j, i: (j, kk, 0))
    return _call(
        body, name=name, grid=(k // tk, s // group, steps),
        in_specs=[pl.BlockSpec((tm, tk), lambda kk, j, i: (i, kk)),
                  pl.BlockSpec((tm, group * n), lambda kk, j, i: (i, j))],
        out_specs=[out, out], out_shape=[_sds((s, k, n)), _sds((s, k, n), BF16)])(a, dy)


def _rms(x):
    return lax.rsqrt(jnp.mean(x * x, axis=-1, keepdims=True) + RMS_EPS)


def _rms_bwd(dxh, xh, r):
    return r * (dxh - xh * jnp.mean(dxh * xh, axis=-1, keepdims=True))


def _norm_mod(x, g, scale, shift, name, tr=512):
    t = x.shape[0]

    def body(x_ref, g_ref, sc_ref, sh_ref, h_ref):
        xv = x_ref[...]
        n = xv * _rms(xv) * g_ref[...]
        h_ref[...] = (n * (1.0 + sc_ref[...]) + sh_ref[...]).astype(BF16)

    return _call(body, name=name, grid=(t // tr,),
                 in_specs=[_rows(tr, D_MODEL), _vec(D_MODEL), _vec(D_MODEL), _vec(D_MODEL)],
                 out_specs=_rows(tr, D_MODEL), out_shape=_sds((t, D_MODEL), BF16))(x, g, scale, shift)


def _out_resid_norm_mod(x, mixin, w_out, gate, g, scale, shift, name, tr=512):
    t = x.shape[0]

    def body(x_ref, mi_ref, w_ref, gt_ref, g_ref, sc_ref, sh_ref, m_ref, x2_ref, h_ref):
        mix = _dot(mi_ref[...], w_ref[0])
        m_ref[...] = mix
        x2 = x_ref[...] + gt_ref[...] * mix
        x2_ref[...] = x2
        n = x2 * _rms(x2) * g_ref[...]
        h_ref[...] = (n * (1.0 + sc_ref[...]) + sh_ref[...]).astype(BF16)

    row = _rows(tr, D_MODEL)
    return _call(body, name=name, grid=(t // tr,),
                 in_specs=[row, row, pl.BlockSpec(w_out.shape, lambda i: (0, 0, 0))] + [_vec(D_MODEL)] * 4,
                 out_specs=[row, row, row],
                 out_shape=[_sds((t, D_MODEL)), _sds((t, D_MODEL)), _sds((t, D_MODEL), BF16)])(
                     x, mixin, w_out, gate, g, scale, shift)


def _mm_gate_up(h, w_gu, name, tm=1024):
    m, k = h.shape
    n = w_gu.shape[2]

    def body(h_ref, wa_ref, wu_ref, da_ref, du_ref, o_ref, w_au):
        @pl.when(pl.program_id(1) == 0)
        def _():
            w_au[:, :n] = wa_ref[0]
            w_au[:, n:] = wu_ref[0]

        au = _dot(h_ref[...], w_au[...])
        a, u = au[:, :n], au[:, n:]
        sg = _sigmoid(a)
        silu = a * sg
        da_ref[...] = (u * sg * (1.0 + a * (1.0 - sg))).astype(BF16)
        du_ref[...] = silu.astype(BF16)
        o_ref[...] = (silu * u).astype(BF16)

    out = pl.BlockSpec((tm, n), lambda j, i: (i, j))
    return _call(body, name=name, grid=(2, m // tm),
                 in_specs=[pl.BlockSpec((tm, k), lambda j, i: (i, 0)), pl.BlockSpec((1, k, n), lambda j, i: (j, 0, 0)),
                           pl.BlockSpec((1, k, n), lambda j, i: (j + 2, 0, 0))],
                 out_specs=[out, out, out], out_shape=[_sds((m, 2 * n), BF16)] * 3,
                 scratch_shapes=[pltpu.VMEM((k, 2 * n), BF16)])(h, w_gu, w_gu)


def _mm_down_dx(dffn, w_down, act_da, act_du, name, tm=512):
    m = dffn.shape[0]
    _, k, n = w_down.shape

    def body(d_ref, w_ref, da_ref, du_ref, o_ref):
        dact = _dot_nt(d_ref[...], w_ref[0])
        o_ref[:, :k] = (dact * da_ref[...].astype(F32)).astype(BF16)
        o_ref[:, k:] = (dact * du_ref[...].astype(F32)).astype(BF16)

    return _call(body, name=name, grid=(m // tm,),
                 in_specs=[_rows(tm, n), pl.BlockSpec((1, k, n), lambda i: (0, 0, 0)), _rows(tm, k), _rows(tm, k)],
                 out_specs=_rows(tm, 2 * k), out_shape=_sds((m, 2 * k), BF16))(dffn, w_down, act_da, act_du)


def _down_loss(x2, act, w_down, gate, fg, tgt, name, tr=512):
    t = x2.shape[0]
    _, k, n = w_down.shape

    def body(x_ref, a_ref, w_ref, gt_ref, fg_ref, t_ref, dx_ref, df_ref, l_ref, dfg_ref, dgt_ref):
        first = pl.program_id(0) == 0
        ffn_v = _dot(a_ref[...], w_ref[0])
        x3 = x_ref[...] + gt_ref[...] * ffn_v
        r = _rms(x3)
        xh = x3 * r
        err = xh * fg_ref[...] - t_ref[...]
        dy = err * (1.0 / D_MODEL)
        dx3 = _rms_bwd(dy * fg_ref[...], xh, r)
        dx_ref[...] = dx3
        df_ref[...] = (dx3 * gt_ref[...]).astype(BF16)
        _acc(l_ref, jnp.sum(err * err, axis=0, keepdims=True), first)
        _acc(dfg_ref, jnp.sum(dy * xh, axis=0, keepdims=True), first)
        _acc(dgt_ref, jnp.sum(dx3 * ffn_v, axis=0, keepdims=True), first)

    row, vec = _rows(tr, D_MODEL), _vec(D_MODEL)
    return _call(body, name=name, grid=(t // tr,),
                 in_specs=[row, _rows(tr, k), pl.BlockSpec((1, k, n), lambda i: (0, 0, 0)), vec, vec, row],
                 out_specs=[row, row, vec, vec, vec],
                 out_shape=[_sds((t, D_MODEL)), _sds((t, D_MODEL), BF16)] + [_sds((1, D_MODEL))] * 3)(
                     x2, act, w_down, gate, fg, tgt)


def _norm_mod_bwd(dh, x, g, scale, dres, name, gate=None, mix=None, w=None, tr=512):
    t = x.shape[0]
    below = gate is not None

    def body(*refs):
        if w is not None:
            w_ref, w_full, sem, refs = refs[1], refs[-2], refs[-1], refs[:1] + refs[2:-2]

            @pl.when(pl.program_id(0) == 0)
            def _():
                n = w.shape[2]
                copies = [pltpu.make_async_copy(w_ref.at[j], w_full.at[:, pl.ds(j * n, n)], sem.at[j])
                          for j in range(w.shape[0])]
                for cp in copies:
                    cp.start()
                for cp in copies:
                    cp.wait()

        if below:
            dh_ref, x_ref, g_ref, sc_ref, dr_ref, gt_ref, m_ref, dx_ref, dsh_ref, dsc_ref, dg_ref, dgt_ref, dm_ref = refs
        else:
            dh_ref, x_ref, g_ref, sc_ref, dr_ref, dx_ref, dsh_ref, dsc_ref, dg_ref = refs
        first = pl.program_id(0) == 0
        xv = x_ref[...]
        if w is None:
            dhv = dh_ref[...].astype(F32)
        else:
            dhv = _dot_nt(dh_ref[...], w_full[...])
        r = _rms(xv)
        xh = xv * r
        dn = dhv * (1.0 + sc_ref[...])
        dx = dr_ref[...] + _rms_bwd(dn * g_ref[...], xh, r)
        dx_ref[...] = dx
        _acc(dsh_ref, jnp.sum(dhv, axis=0, keepdims=True), first)
        _acc(dsc_ref, jnp.sum(dhv * xh * g_ref[...], axis=0, keepdims=True), first)
        _acc(dg_ref, jnp.sum(dn * xh, axis=0, keepdims=True), first)
        if below:
            _acc(dgt_ref, jnp.sum(dx * m_ref[...], axis=0, keepdims=True), first)
            dm_ref[...] = (dx * gt_ref[...]).astype(BF16)

    row, vec = _rows(tr, D_MODEL), _vec(D_MODEL)
    first_specs = [row] if w is None else [_rows(tr, dh.shape[1]), pl.BlockSpec(memory_space=pl.ANY)]
    scratch = [] if w is None else [pltpu.VMEM((w.shape[1], dh.shape[1]), BF16), pltpu.SemaphoreType.DMA((w.shape[0],))]
    in_specs = first_specs + [row, vec, vec, row] + ([vec, row] if below else [])
    out_specs = [row, vec, vec, vec] + ([vec, row] if below else [])
    out_shape = [_sds((t, D_MODEL))] + [_sds((1, D_MODEL))] * 3 + ([_sds((1, D_MODEL)), _sds((t, D_MODEL), BF16)] if below else [])
    args = ((dh,) if w is None else (dh, w)) + (x, g, scale, dres) + ((gate, mix) if below else ())
    return _call(body, name=name, grid=(t // tr,), in_specs=in_specs, out_specs=out_specs, out_shape=out_shape,
                 scratch_shapes=scratch)(*args)


def _mix_in_bwd(dmix, w_out, o_hg, proj, att, og, ag, fine, name, tr=512):
    t = o_hg.shape[0]

    def body(dy_ref, w_ref, o_ref, g_ref, a_ref, og_ref, ag_ref,
             do_ref, dg_ref, da_ref, dd_ref, das_ref, dds_ref, dog_ref, dag_ref, to_sub):
        first = pl.program_id(0) == 0

        @pl.when(first)
        def _():
            to_sub[...] = _regroup_matrix(tr, tr // fine)

        dmi = _dot_nt(dy_ref[...], w_ref[0])
        dog = jnp.zeros((1, HG_HEAD), F32)
        for h in range(HG_WIDTH // HG_HEAD):
            sl = slice(h * HG_HEAD, (h + 1) * HG_HEAD)
            oh = o_ref[:, sl].astype(F32)
            gv = g_ref[:, sl].astype(F32)
            dv = dmi[:, sl]
            r = _rms(oh)
            xh = oh * r
            sg = _sigmoid(gv)
            dno = dv * gv * sg
            dg_ref[:, sl] = (dv * xh * og_ref[...] * sg * (1.0 + gv * (1.0 - sg))).astype(BF16)
            dog = dog + jnp.sum(dno * xh, axis=0, keepdims=True)
            do_ref[:, sl] = _rms_bwd(dno * og_ref[...], xh, r).astype(BF16)
        _acc(dog_ref, dog, first)
        av = a_ref[...]
        dav = dmi[:, HG_WIDTH:]
        r = _rms(av)
        xa = av * r
        _acc(dag_ref, jnp.sum(dav * xa, axis=0, keepdims=True), first)
        datt = _rms_bwd(dav * ag_ref[...], xa, r)
        da_ref[...] = datt.astype(BF16)
        das_ref[...] = _regroup(to_sub[...], datt.astype(BF16)).astype(BF16).reshape(das_ref.shape)
        prod = datt * av
        lane = lax.broadcasted_iota(jnp.int32, (1, 128), 1)
        dd = jnp.zeros((tr, 128), F32)
        for hp in range(ATT_HEADS // 2):
            pp = prod[:, hp * 128:(hp + 1) * 128]
            lo = jnp.sum(jnp.where(lane < 64, pp, 0.0), axis=-1, keepdims=True)
            hi = jnp.sum(jnp.where(lane >= 64, pp, 0.0), axis=-1, keepdims=True)
            dd = jnp.where(lane == 2 * hp, lo, dd)
            dd = jnp.where(lane == 2 * hp + 1, hi, dd)
        dd_ref[...] = dd
        dds_ref[...] = _regroup(to_sub[...], dd, ATT_HEADS).reshape(dds_ref.shape)

    half = _rows(tr, HG_WIDTH)
    out = _call(body, name=name, grid=(t // tr,),
                in_specs=[_rows(tr, D_MODEL), pl.BlockSpec(w_out.shape, lambda i: (0, 0, 0)), half,
                          pl.BlockSpec((tr, HG_WIDTH), lambda i: (i, 3)), half, _vec(HG_HEAD), _vec(ATT_WIDTH)],
                out_specs=[half, pl.BlockSpec((tr, HG_WIDTH), lambda i: (i, 3)), half, _rows(tr, 128),
                           _sub_rows(tr, fine, ATT_WIDTH), _sub_rows(tr, fine, 128), _vec(HG_HEAD), _vec(ATT_WIDTH)],
                out_shape=[_sds((t, HG_WIDTH), BF16), _sds((t, IN_WIDTH), BF16), _sds((t, HG_WIDTH), BF16),
                           _sds((t, 128)), _sds((fine, t // fine, ATT_WIDTH), BF16),
                           _sds((fine, t // fine, 128)), _sds((1, HG_HEAD)), _sds((1, ATT_WIDTH))],
                scratch_shapes=[pltpu.VMEM((tr, tr), BF16)])(dmix, w_out, o_hg, proj, att, og, ag)
    out = list(out)
    return out[:4] + [out[4].reshape(t, ATT_WIDTH), out[5].reshape(t, 128)] + out[6:]


def _dproj(dproj, dqkvs, name, tr=1024):
    t = dproj.shape[0]
    nbr = len(dqkvs)
    first = IN_WIDTH // ATT_WIDTH - 3

    def body(*refs):
        refs[-1][...] = sum(r[...].astype(F32) for r in refs[:nbr]).astype(BF16)

    return _call(body, name=name, grid=(t // tr, 3),
                 in_specs=[pl.BlockSpec((tr, ATT_WIDTH), lambda i, j: (i, j))] * nbr + [pl.BlockSpec(memory_space=pl.ANY)],
                 out_specs=pl.BlockSpec((tr, ATT_WIDTH), lambda i, j: (i, first + j)),
                 out_shape=_sds(dproj.shape, BF16), aliases={nbr: 0})(*dqkvs, dproj)


def _chunk_tri(upper):
    row = lax.broadcasted_iota(jnp.int32, (HG_GROUP, HG_CHUNK, HG_CHUNK), 1)
    col = lax.broadcasted_iota(jnp.int32, (HG_GROUP, HG_CHUNK, HG_CHUNK), 2)
    return (row <= col if upper else row >= col).astype(BF16)


def _chunk_cumsum(x, tri):
    x3 = x.reshape(HG_GROUP, HG_CHUNK, x.shape[1])
    dims = (((2,), (1,)), ((0,), (0,)))
    out = None
    for _ in range(3):
        part = x3.astype(BF16)
        x3 = x3 - part.astype(F32)
        term = lax.dot_general(tri, part, dims, preferred_element_type=F32)
        out = term if out is None else out + term
    return out.reshape(x.shape)


def _hg_gates(f_raw, q_raw, lb, tri):
    sg = _sigmoid(f_raw)
    f = lb + (1.0 - lb) * sg
    k = 1.0 - f
    b = _chunk_cumsum(jnp.log(f), tri)
    sq = _sigmoid(q_raw)
    return sg, f, k, b, sq


def _hg_masks(rows):
    row = lax.broadcasted_iota(jnp.int32, (rows, rows), 0)
    col = lax.broadcasted_iota(jnp.int32, (rows, rows), 1)
    same = (row // HG_CHUNK) == (col // HG_CHUNK)
    return jnp.logical_and(row >= col, same), jnp.logical_and(row <= col, same)


def _per_chunk(rows_of):
    return jnp.concatenate([jnp.broadcast_to(r, (HG_CHUNK, r.shape[1])) for r in rows_of], axis=0)


def _hgrn_fwd(proj, lb_logits, name):
    t = proj.shape[0]
    nc = t // HG_CHUNK
    nh = HG_WIDTH // HG_HEAD
    rows = HG_GROUP * HG_CHUNK

    def body(q_ref, f_ref, i_ref, lg_ref, o_ref, st_ref, s_scr):
        @pl.when(pl.program_id(0) == 0)
        def _():
            s_scr[...] = jnp.zeros_like(s_scr)

        lg = lg_ref[...]
        lb_all = _sigmoid(lg[0:1] - lg[1:2])
        causal, _ = _hg_masks(rows)
        tri = _chunk_tri(False)
        for h in range(nh):
            sl = slice(h * HG_HEAD, (h + 1) * HG_HEAD)
            q_raw = q_ref[:, sl].astype(F32)
            _, _, k, b, sq = _hg_gates(f_ref[:, sl].astype(F32), q_raw, lb_all[:, sl], tri)
            v = i_ref[:, sl].astype(BF16)
            gls = [b[(g + 1) * HG_CHUNK - 1:(g + 1) * HG_CHUNK] for g in range(HG_GROUP)]
            bm = _per_chunk([b[g * HG_CHUNK + HG_CHUNK // 2 - 1:g * HG_CHUNK + HG_CHUNK // 2] for g in range(HG_GROUP)])
            qd = (q_raw * sq * jnp.exp(b)).astype(BF16)
            qm = (q_raw * sq * jnp.exp(b - bm)).astype(BF16)
            km = (k * jnp.exp(bm - b)).astype(BF16)
            ke = (k * jnp.exp(_per_chunk(gls) - b)).astype(BF16)
            a = jnp.where(causal, _dot_nt(qm, km), 0.0).astype(BF16)
            o_intra = _dot(a, v)
            st = s_scr[h]
            o_inter = []
            for g in range(HG_GROUP):
                rs = slice(g * HG_CHUNK, (g + 1) * HG_CHUNK)
                st_ref[g, sl, :] = st
                o_inter.append(_dot_nt(qd[rs], st.astype(BF16)))
                st = st * jnp.exp(gls[g]) + _dot_tn(v[rs], ke[rs])
            s_scr[h] = st
            o_ref[:, sl] = (o_intra + jnp.concatenate(o_inter, axis=0)).astype(BF16)

    blk = lambda j: pl.BlockSpec((rows, HG_WIDTH), lambda c: (c, j))
    return _call(body, name=name, grid=(nc // HG_GROUP,),
                 in_specs=[blk(0), blk(1), blk(2), pl.BlockSpec((2, HG_WIDTH), lambda c: (0, 0))],
                 out_specs=[blk(0), pl.BlockSpec((HG_GROUP, HG_WIDTH, HG_HEAD), lambda c: (c, 0, 0))],
                 out_shape=[_sds((t, HG_WIDTH), BF16), _sds((nc, HG_WIDTH, HG_HEAD))],
                 scratch_shapes=[pltpu.VMEM((nh, HG_HEAD, HG_HEAD), F32)])(proj, proj, proj, lb_logits)


def _hgrn_bwd(proj, lb_logits, states, do, dproj, name):
    t = proj.shape[0]
    ng = t // (HG_GROUP * HG_CHUNK)
    nh = HG_WIDTH // HG_HEAD
    rows = HG_GROUP * HG_CHUNK

    def body(q_ref, f_ref, i_ref, lg_ref, st_ref, do_ref, _, d_ref, dlb_ref, ds_scr):
        first = pl.program_id(0) == 0

        @pl.when(first)
        def _():
            ds_scr[...] = jnp.zeros_like(ds_scr)

        lg = lg_ref[...]
        lb_all = _sigmoid(lg[0:1] - lg[1:2])
        causal, _ = _hg_masks(rows)
        tri = _chunk_tri(False)
        tri_t = _chunk_tri(True)
        dlb = []
        for h in range(nh):
            sl = slice(h * HG_HEAD, (h + 1) * HG_HEAD)
            q_raw = q_ref[:, sl].astype(F32)
            lb = lb_all[:, sl]
            sg, f, k, b, sq = _hg_gates(f_ref[:, sl].astype(F32), q_raw, lb, tri)
            v = i_ref[:, sl].astype(BF16)
            gls = [b[(g + 1) * HG_CHUNK - 1:(g + 1) * HG_CHUNK] for g in range(HG_GROUP)]
            bm = _per_chunk([b[g * HG_CHUNK + HG_CHUNK // 2 - 1:g * HG_CHUNK + HG_CHUNK // 2] for g in range(HG_GROUP)])
            eb = jnp.exp(b)
            ebm = jnp.exp(b - bm)
            emb = jnp.exp(bm - b)
            egb = jnp.exp(_per_chunk(gls) - b)
            ke = k * egb
            qd_b, qm_b = (q_raw * sq * eb).astype(BF16), (q_raw * sq * ebm).astype(BF16)
            km_b, ke_b = (k * emb).astype(BF16), ke.astype(BF16)
            dov = do_ref[:, sl].astype(BF16)
            a = jnp.where(causal, _dot_nt(qm_b, km_b), 0.0).astype(BF16)
            da = jnp.where(causal, _dot_nt(dov, v), 0.0).astype(BF16)
            dkm = _dot_tn(da, qm_b)
            dst = ds_scr[h]
            dqd_s, dv_s, dke_s, dgl_s = [None] * HG_GROUP, [None] * HG_GROUP, [None] * HG_GROUP, [None] * HG_GROUP
            for g in reversed(range(HG_GROUP)):
                rs = slice(g * HG_CHUNK, (g + 1) * HG_CHUNK)
                st = st_ref[g, sl, :]
                dst_b = dst.astype(BF16)
                egl = jnp.exp(gls[g])
                dqd_s[g] = _dot(dov[rs], st.astype(BF16))
                dv_s[g] = _dot_nt(ke_b[rs], dst_b)
                dke_s[g] = _dot(v[rs], dst_b)
                dgl_s[g] = jnp.sum(dst * st, axis=0, keepdims=True) * egl
                dst = _dot_tn(dov[rs], qd_b[rs]) + dst * egl
            ds_scr[h] = dst
            dqm = _dot(da, km_b)
            dqd = jnp.concatenate(dqd_s, axis=0)
            dv = _dot_tn(a, dov) + jnp.concatenate(dv_s, axis=0)
            dke = jnp.concatenate(dke_s, axis=0)
            t1 = dke * ke
            db = dqm * qm_b.astype(F32) - dkm * km_b.astype(F32) + dqd * qd_b.astype(F32) - t1
            dgl = _per_chunk([dgl_s[g] + jnp.sum(t1[g * HG_CHUNK:(g + 1) * HG_CHUNK], axis=0, keepdims=True)
                              for g in range(HG_GROUP)])
            dlf = _chunk_cumsum(db, tri_t) + dgl
            df = dlf / f - (dkm * emb + dke * egb)
            d_ref[:, sl] = ((dqm * ebm + dqd * eb) * sq * (1.0 + q_raw * (1.0 - sq))).astype(BF16)
            d_ref[:, HG_WIDTH + h * HG_HEAD:HG_WIDTH + (h + 1) * HG_HEAD] = (
                df * (1.0 - lb) * sg * (1.0 - sg)).astype(BF16)
            d_ref[:, 2 * HG_WIDTH + h * HG_HEAD:2 * HG_WIDTH + (h + 1) * HG_HEAD] = dv.astype(BF16)
            dlb.append(jnp.sum(df * (1.0 - sg), axis=0, keepdims=True))
        _acc(dlb_ref, jnp.concatenate(dlb, axis=1), first)

    rev = lambda j: pl.BlockSpec((rows, HG_WIDTH), lambda c: (ng - 1 - c, j))
    return _call(body, name=name, grid=(ng,),
                 in_specs=[rev(0), rev(1), rev(2), pl.BlockSpec((2, HG_WIDTH), lambda c: (0, 0)),
                           pl.BlockSpec((HG_GROUP, HG_WIDTH, HG_HEAD), lambda c: (ng - 1 - c, 0, 0)), rev(0),
                           pl.BlockSpec(memory_space=pl.ANY)],
                 out_specs=[pl.BlockSpec((rows, 3 * HG_WIDTH), lambda c: (ng - 1 - c, 0)), _vec(HG_WIDTH)],
                 out_shape=[_sds(dproj.shape, BF16), _sds((1, HG_WIDTH))], aliases={6: 0},
                 scratch_shapes=[pltpu.VMEM((nh, HG_HEAD, HG_HEAD), F32)])(
                     proj, proj, proj, lb_logits, states, do, dproj)


def _to_sub(a, dil):
    t, w = a.shape
    return a if dil == 1 else a.reshape(t // dil, dil, w).transpose(1, 0, 2).reshape(t, w)


def _from_sub(a, dil):
    t, w = a.shape
    return a if dil == 1 else a.reshape(dil, t // dil, w).transpose(1, 0, 2).reshape(t, w)


def _att_mask(has_prev, seg):
    def place(v):
        v = v % ATT_BLOCK
        return v if seg == 1 else seg * (v % (ATT_BLOCK // seg)) + v // (ATT_BLOCK // seg)

    row = lax.broadcasted_iota(jnp.int32, (2 * ATT_BLOCK, 2 * ATT_BLOCK), 0)
    col = lax.broadcasted_iota(jnp.int32, (2 * ATT_BLOCK, 2 * ATT_BLOCK), 1)
    qi, kj = place(row), place(col)
    prev = jnp.logical_and(jnp.logical_and(col < ATT_BLOCK, kj >= qi), has_prev)
    cur = jnp.logical_and(col >= ATT_BLOCK, kj <= qi)
    return jnp.logical_or(prev, cur), lax.broadcasted_iota(jnp.int32, (1, 128), 1)


def _get(ref, sl):
    if len(ref.shape) == 2:
        return ref[:, sl]
    v = ref[:, :, sl]
    return v.reshape(ATT_BLOCK, v.shape[2])


def _put(ref, sl, val):
    if len(ref.shape) == 2:
        ref[:, sl] = val
    else:
        ref[:, :, sl] = val.reshape(ref.shape[0], ref.shape[1], val.shape[1])


def _att_spec(nb, dil, seg, width, col, back):
    bps = nb // dil

    def plain(n):
        return jnp.clip(n - back, 0, nb - 1), col

    def segmented(n):
        m = jnp.clip(n - back, 0, nb - 1)
        return 0, m // bps, m % bps, 0, col

    if seg == 1:
        return pl.BlockSpec((ATT_BLOCK, width), plain)
    return pl.BlockSpec((seg, None, None, ATT_BLOCK // seg, width), segmented)


def _att_shape(nb, dil, seg, width):
    t = nb * ATT_BLOCK
    return (t, width) if seg == 1 else (seg, dil, nb // dil, ATT_BLOCK // seg, width)


def _att_view(a, nb, dil, seg):
    return a.reshape(_att_shape(nb, dil, seg, a.shape[1]))


def _attn_fwd_block(q_ref, kc_ref, kp_ref, vc_ref, vp_ref, o_ref, l_ref, has_prev, seg, lane0):
    mask, lane = _att_mask(has_prev, seg)
    lo = lane < 64
    nq = ATT_BLOCK
    lse_all = jnp.zeros((nq, 128), F32)
    for hp in range(ATT_HEADS // 2):
        sl = slice(hp * 128, (hp + 1) * 128)
        q2 = _get(q_ref, sl)
        zero = jnp.zeros_like(q2)
        q2 = q2 * 0.125
        qs = jnp.concatenate([jnp.where(lo, q2, zero), jnp.where(lo, zero, q2)], axis=0)
        kk = jnp.concatenate([_get(kp_ref, sl), _get(kc_ref, sl)], axis=0)
        vv = jnp.concatenate([_get(vp_ref, sl), _get(vc_ref, sl)], axis=0)
        s = jnp.where(mask, _dot_nt(qs, kk), NEG)
        mx = jnp.max(s, axis=-1, keepdims=True)
        p = jnp.exp(s - mx)
        l = jnp.sum(p, axis=-1, keepdims=True)
        o = _dot(p.astype(BF16), vv) * (1.0 / l)
        _put(o_ref, sl, jnp.where(lo, o[:nq], o[nq:]).astype(BF16))
        lse = mx + jnp.log(l)
        lse_all = jnp.where(lane == lane0 + 2 * hp, lse[:nq], lse_all)
        lse_all = jnp.where(lane == lane0 + 2 * hp + 1, lse[nq:], lse_all)
    _put(l_ref, slice(None), lse_all)


def _attn_fwd(branches, name):
    t = branches[0][0].shape[0]
    nb = t // ATT_BLOCK
    nbr = len(branches)

    def body(*refs):
        n = pl.program_id(0)
        for i, (_, dil, seg) in enumerate(branches):
            _attn_fwd_block(*refs[5 * i:5 * i + 5], *refs[5 * nbr + 2 * i:5 * nbr + 2 * i + 2],
                            (n % (nb // dil)) != 0, seg, ATT_HEADS * i)

    in_specs, args, out_specs, out_shape = [], [], [], []
    for qkv, dil, seg in branches:
        c0 = qkv.shape[1] // ATT_WIDTH - 3
        in_specs += [_att_spec(nb, dil, seg, ATT_WIDTH, c0 + j, back) for j, back in [(0, 0), (1, 0), (1, 1), (2, 0), (2, 1)]]
        args += [_att_view(qkv, nb, dil, seg)] * 5
        out_specs += [_att_spec(nb, dil, seg, ATT_WIDTH, 0, 0), _att_spec(nb, dil, seg, 128, 0, 0)]
        out_shape += [_sds(_att_shape(nb, dil, seg, ATT_WIDTH), BF16), _sds(_att_shape(nb, dil, seg, 128))]
    out = _call(body, name=name, grid=(nb,), in_specs=in_specs, out_specs=out_specs, out_shape=out_shape)(*args)
    return [(out[2 * i].reshape(t, ATT_WIDTH), out[2 * i + 1].reshape(t, 128)) for i in range(nbr)]


def _combine_mix_in(os_, ls_, fine, o_hg, proj, og, ag, name, tr=512):
    t = os_[0].shape[0]
    nbr = len(os_)

    def body(*refs):
        o_refs, l_refs = refs[:nbr], refs[nbr:2 * nbr]
        oh_ref, g_ref, og_ref, ag_ref, a_ref, lt_ref, lts_ref, m_ref, to_natural, to_sub = refs[2 * nbr:]

        @pl.when(pl.program_id(0) == 0)
        def _():
            to_natural[...] = _regroup_matrix(tr, fine)
            to_sub[...] = _regroup_matrix(tr, tr // fine)

        lane = lax.broadcasted_iota(jnp.int32, (1, 128), 1)
        packed = l_refs[0][...] + _regroup(to_natural[...], sum(r[...] for r in l_refs[1:]).reshape(tr, 128))
        ls = [packed if i == 0 else pltpu.roll(packed, 128 - ATT_HEADS * i, 1) for i in range(nbr)]
        mx = functools.reduce(jnp.maximum, ls)
        tot = mx + jnp.log(sum(jnp.exp(l - mx) for l in ls))
        ws = [jnp.exp(l - tot) for l in ls]
        tot = jnp.where(lane < ATT_HEADS, tot, 0.0)
        lt_ref[...] = tot
        lts_ref[...] = _regroup(to_sub[...], tot).reshape(lts_ref.shape)
        o_vals = [o_refs[0]] + [_regroup(to_natural[...], r[...].reshape(tr, ATT_WIDTH)) for r in o_refs[1:]]
        pairs = []
        for hp in range(ATT_HEADS // 2):
            sl = slice(hp * 128, (hp + 1) * 128)
            acc = jnp.zeros((tr, 128), F32)
            for w, o in zip(ws, o_vals):
                wf = jnp.where(lane < 64, w[:, 2 * hp:2 * hp + 1], w[:, 2 * hp + 1:2 * hp + 2])
                acc = acc + wf * o[:, sl]
            pairs.append(acc)
        av = jnp.concatenate(pairs, axis=1)
        a_ref[...] = av
        m_ref[:, HG_WIDTH:] = (av * _rms(av) * ag_ref[...]).astype(BF16)
        for h in range(HG_WIDTH // HG_HEAD):
            sl = slice(h * HG_HEAD, (h + 1) * HG_HEAD)
            oh = oh_ref[:, sl].astype(F32)
            gv = g_ref[:, sl].astype(F32)
            m_ref[:, sl] = (oh * _rms(oh) * og_ref[...] * (gv * _sigmoid(gv))).astype(BF16)

    half = _rows(tr, ATT_WIDTH)
    sub = lambda a: a.reshape(fine, t // fine, a.shape[1])
    att, lse, lse_sub, mixin = _call(
        body, name=name, grid=(t // tr,),
        in_specs=[half] + [_sub_rows(tr, fine, ATT_WIDTH)] * (nbr - 1) + [_rows(tr, 128)] + [_sub_rows(tr, fine, 128)] * (nbr - 1)
        + [half, pl.BlockSpec((tr, HG_WIDTH), lambda i: (i, 3)), _vec(HG_HEAD), _vec(ATT_WIDTH)],
        out_specs=[half, _rows(tr, 128), _sub_rows(tr, fine, 128), _rows(tr, D_MODEL)],
        out_shape=[_sds((t, ATT_WIDTH)), _sds((t, 128)), _sds((fine, t // fine, 128)), _sds((t, D_MODEL), BF16)],
        scratch_shapes=[pltpu.VMEM((tr, tr), BF16)] * 2)(
            os_[0], *map(sub, os_[1:]), ls_[0], *map(sub, ls_[1:]), o_hg, proj, og, ag)
    return att, lse, lse_sub.reshape(t, 128), mixin


def _attn_bwd_block(q_ref, k_ref, v_ref, do_ref, l_ref, d_ref, out_ref, carry, prev, has_prev, seg):
    w = ATT_WIDTH
    nq = ATT_BLOCK
    mask, lane = _att_mask(has_prev, seg)
    lo = lane < 64
    lse, ddv = _get(l_ref, slice(None)), _get(d_ref, slice(None))
    for hp in range(ATT_HEADS // 2):
        sl = slice(hp * 128, (hp + 1) * 128)
        sk = slice(w + hp * 128, w + (hp + 1) * 128)
        sv = slice(2 * w + hp * 128, 2 * w + (hp + 1) * 128)
        q2, do2 = _get(q_ref, sl), _get(do_ref, sl)
        zero = jnp.zeros_like(q2)
        q2 = q2 * 0.125
        qs = jnp.concatenate([jnp.where(lo, q2, zero), jnp.where(lo, zero, q2)], axis=0)
        dos = jnp.concatenate([jnp.where(lo, do2, zero), jnp.where(lo, zero, do2)], axis=0)
        kc, vc = _get(k_ref, sl), _get(v_ref, sl)
        kk = jnp.concatenate([prev[:, sl], kc], axis=0)
        vv = jnp.concatenate([prev[:, sk], vc], axis=0)
        prev[:, sl] = kc
        prev[:, sk] = vc
        ls = jnp.concatenate([lse[:, 2 * hp:2 * hp + 1], lse[:, 2 * hp + 1:2 * hp + 2]], axis=0)
        dh = jnp.concatenate([ddv[:, 2 * hp:2 * hp + 1], ddv[:, 2 * hp + 1:2 * hp + 2]], axis=0)
        p = jnp.exp(jnp.where(mask, _dot_nt(qs, kk) - ls, NEG))
        ds = (p * (_dot_nt(dos, vv) - dh)).astype(BF16)
        dq = _dot(ds, kk) * 0.125
        dk = _dot_tn(ds, qs)
        dv = _dot_tn(p.astype(BF16), dos)
        _put(out_ref, sl, carry[:, sl].astype(BF16))
        _put(out_ref, sk, (carry[:, sk] + dk[:nq]).astype(BF16))
        _put(out_ref, sv, (carry[:, sv] + dv[:nq]).astype(BF16))
        carry[:, sl] = jnp.where(lo, dq[:nq], dq[nq:])
        carry[:, sk] = dk[nq:]
        carry[:, sv] = dv[nq:]


def _attn_bwd(branches, name):
    t = branches[0][0].shape[0]
    nb = t // ATT_BLOCK
    nbr = len(branches)
    w = ATT_WIDTH

    def body(*refs):
        ins, outs, carries, prevs = refs[:6 * nbr], refs[6 * nbr:7 * nbr], refs[7 * nbr:8 * nbr], refs[8 * nbr:]
        n = pl.program_id(0)

        @pl.when(n == 0)
        def _():
            for scratch in carries + prevs:
                scratch[...] = jnp.zeros_like(scratch)

        @pl.when(n < nb)
        def _():
            for i, branch in enumerate(branches):
                dil, seg = branch[4:]
                _attn_bwd_block(*ins[6 * i:6 * i + 6], outs[i], carries[i], prevs[i], (n % (nb // dil)) != 0, seg)

        @pl.when(n == nb)
        def _():
            for i in range(nbr):
                _put(outs[i], slice(None), carries[i][...].astype(BF16))

    in_specs, args, out_specs, out_shape = [], [], [], []
    for qkv, dout, lse, dd, dil, seg in branches:
        c0 = qkv.shape[1] // w - 3
        in_specs += [_att_spec(nb, dil, seg, w, c0 + j, 0) for j in range(3)]
        in_specs += [_att_spec(nb, dil, seg, w, 0, 0), _att_spec(nb, dil, seg, 128, 0, 0), _att_spec(nb, dil, seg, 128, 0, 0)]
        args += [_att_view(a, nb, dil, seg) for a in [qkv] * 3 + [dout, lse, dd]]
        out_specs += [_att_spec(nb, dil, seg, 3 * w, 0, 1)]
        out_shape += [_sds(_att_shape(nb, dil, seg, 3 * w), BF16)]
    out = _call(body, name=name, grid=(nb + 1,), in_specs=in_specs, out_specs=out_specs, out_shape=out_shape,
                scratch_shapes=[pltpu.VMEM((ATT_BLOCK, 3 * w), F32)] * nbr + [pltpu.VMEM((ATT_BLOCK, 2 * w), BF16)] * nbr)(*args)
    return [o.reshape(t, 3 * w) for o in out]


def _local_step(x, tgt, mod, norm1_g, lb_logits, og, ag, norm2_g, fg, get_w, put_g, late=lambda a: a, project=None):
    shift1, scale1, gate1, shift2, scale2, gate2 = [mod[:, i * D_MODEL:(i + 1) * D_MODEL] for i in range(6)]
    fg = fg.reshape(1, D_MODEL)

    h1 = _norm_mod(x, norm1_g, scale1, shift1, "norm_mod1")
    if project is None:
        w_in = get_w("w_in", h1)
        proj = _mm_nn(h1, w_in, "mm_in", out_dtype=BF16)
    else:
        proj, w_in = project(h1)
    o_hg, states = _hgrn_fwd(proj, lb_logits, "hgrn_fwd")
    fine = DILATIONS[-1]
    layouts = [(d, 1 if d == 1 else fine // d) for d in DILATIONS]
    qkv_fine = _to_sub(proj, fine)
    qkvs = [proj if d == 1 else qkv_fine for d in DILATIONS]
    natural = lambda a, d: a if d == 1 else _from_sub(a, fine)
    outs = _attn_fwd([(q, d, seg) for q, (d, seg) in zip(qkvs, layouts)], "attn_fwd")
    att, lse, lse_fine, mixin = _combine_mix_in([o for o, _ in outs], [l for _, l in outs], fine,
                                                o_hg, proj, og, ag, "attn_combine_mix_in")
    w_out = get_w("w_out", mixin)
    mix, x2, h2 = _out_resid_norm_mod(x, mixin, w_out, gate1, norm2_g, scale2, shift2, "mm_out_resid_norm_mod2")
    w_gu = get_w("w_gu", h2)
    a_ff, u_ff, act = _mm_gate_up(h2, w_gu, "mm_gu")
    w_down = get_w("w_down", act)
    dx3, dffn, loss_v, dfg, dgate2 = _down_loss(x2, act, w_down, gate2, fg, tgt, "mm_down_loss")

    dffn = put_g("w_down", *_mm_tn(act, dffn, 1, "mm_down_dw", tm=2048, tk=D_FF // 2), dffn)
    dau = _mm_down_dx(dffn, w_down, a_ff, u_ff, "mm_down_dx")
    dau = put_g("w_gu", *_mm_tn(h2, dau, N_SHARD, "mm_gu_dw", tm=x.shape[0], tk=512), dau)
    dx2, dshift2, dscale2, dg2, dgate1, dmix = _norm_mod_bwd(
        dau, x2, norm2_g, scale2, dx3, "mm_gu_dx_norm_bwd", gate=gate1, mix=mix, w=w_gu)
    dmix = put_g("w_out", *_mm_tn(mixin, dmix, 1, "mm_out_dw", tm=x.shape[0], tk=512), dmix)
    do_hg, dproj, datt, dd, datt_fine, dd_fine, dog, dag = _mix_in_bwd(
        dmix, w_out, o_hg, proj, att, og, ag, fine, "mm_out_dx_mix_in_bwd")
    datts = _attn_bwd([(q,) + ((datt, lse, dd) if d == 1 else (datt_fine, lse_fine, dd_fine)) + (d, seg)
                       for q, (d, seg) in zip(qkvs, layouts)], "attn_bwd")
    dproj, dlb = _hgrn_bwd(proj, lb_logits, states, do_hg, dproj, "hgrn_bwd")
    dproj = late(dproj)
    dproj = _dproj(dproj, [natural(a, d) for a, d in zip(datts, DILATIONS)], "dproj")
    dproj = put_g("w_in", *_mm_tn(h1, dproj, N_SHARD, "mm_in_dw", tm=x.shape[0], tk=512, group=2), dproj)
    dx, dshift1, dscale1, dg1 = _norm_mod_bwd(dproj, x, norm1_g, scale1, dx2, "mm_in_dx_norm_bwd", w=w_in)

    stats = jnp.concatenate([loss_v, dfg, dg2, dg1, dlb, dag, dog,
                             dshift1, dscale1, dgate1, dshift2, dscale2, dgate2], axis=1)
    return dx, stats


def _place():
    x, y, c = lax.axis_index("x"), lax.axis_index("y"), lax.axis_index("c")
    return x, y, c


def _chip_peers(x, y, c):
    return [(1 - x, y, c), (x, 1 - y, c), (1 - x, 1 - y, c)]


_HBM = pl.BlockSpec(memory_space=pltpu.HBM)
_SEM = pl.BlockSpec(memory_space=pltpu.SEMAPHORE)
_EFFECT = pltpu.SideEffectType.DATAFLOW_SIDE_EFFECTING


def _exchange_copy(bufs, send, recv, j, peer, place, kind):
    x, y, c = place
    target = peer
    if kind == "gather":
        src = dst = bufs[0].at[2 * x + y]
    elif kind == "scatter":
        src, dst = bufs[0].at[2 * peer[0] + peer[1]], bufs[1].at[j]
    else:
        half = bufs[0].shape[1] // 2
        rows = pl.ds(c * half, half)
        if kind == "half":
            src = dst = bufs[0].at[2 * x + y, rows]
        else:
            src = dst = bufs[0].at[2 * peer[0] + peer[1], rows]
            target = (x, y, 1 - c)
    return pltpu.make_async_remote_copy(src_ref=src, dst_ref=dst, send_sem=send.at[j], recv_sem=recv.at[j],
                                        device_id=target, device_id_type=MESH)


def _exchange_start(groups, after, kind, name):
    sizes = [len(g) for g in groups]
    flat = [b for g in groups for b in g]
    ng, nb = len(groups), len(flat)

    def body(*refs):
        bufs, sems = refs[:nb], refs[nb + 1:nb + 1 + 2 * ng]
        x, y, c = _place()
        for j, peer in enumerate(_chip_peers(x, y, c)):
            at = 0
            for i, size in enumerate(sizes):
                _exchange_copy(bufs[at:at + size], sems[2 * i], sems[2 * i + 1], j, peer, (x, y, c), kind).start()
                at += size

    any_space = pl.BlockSpec(memory_space=pl.ANY)
    out = pl.pallas_call(
        body, name=name, in_specs=[_HBM] * nb + [any_space],
        out_specs=[_SEM] * (2 * ng) + [_HBM] * nb + [any_space],
        out_shape=[pltpu.SemaphoreType.DMA((3,))] * (2 * ng) + [pltpu.HBM(b.shape, b.dtype) for b in flat]
        + [_sds(after.shape, after.dtype)],
        input_output_aliases={i: 2 * ng + i for i in range(nb + 1)},
        compiler_params=pltpu.CompilerParams(has_side_effects=_EFFECT),
    )(*[pltpu.with_memory_space_constraint(b, pltpu.HBM) for b in flat], after)
    started, at = [], 2 * ng
    for i, size in enumerate(sizes):
        started.append((out[2 * i], out[2 * i + 1], tuple(out[at:at + size])))
        at += size
    return started, out[-1]


def _exchange_wait(started, after, kind, name):
    send, recv, bufs = started
    nb = len(bufs)

    def body(*refs):
        x, y, c = _place()
        for j, peer in enumerate(_chip_peers(x, y, c)):
            cp = _exchange_copy(refs[:nb], refs[nb], refs[nb + 1], j, peer, (x, y, c), kind)
            cp.wait_send()
            cp.wait_recv()

    return pl.pallas_call(
        body, name=name, in_specs=[_HBM] * nb + [_SEM, _SEM, pl.BlockSpec(memory_space=pl.ANY)],
        out_specs=[_HBM] * nb, out_shape=[pltpu.HBM(b.shape, b.dtype) for b in bufs],
        input_output_aliases={i: i for i in range(nb)},
        compiler_params=pltpu.CompilerParams(has_side_effects=_EFFECT),
    )(*bufs, send, recv, after)


def _sibling_copies(v_refs, l_refs, send, recv):
    x, y, c = _place()
    return [pltpu.make_async_remote_copy(src_ref=v, dst_ref=l, send_sem=send.at[a], recv_sem=recv.at[a],
                                         device_id=(x, y, 1 - c), device_id_type=MESH)
            for a, (v, l) in enumerate(zip(v_refs, l_refs))]


def _sibling_start(vs, after, name):
    vs = list(vs)
    n = len(vs)
    lands = [lax.empty(v.shape, v.dtype) for v in vs]

    def body(*refs):
        for cp in _sibling_copies(refs[:n], refs[n:2 * n], refs[2 * n + 1], refs[2 * n + 2]):
            cp.start()

    any_space = pl.BlockSpec(memory_space=pl.ANY)
    out = pl.pallas_call(
        body, name=name, in_specs=[_HBM] * (2 * n) + [any_space],
        out_specs=[_SEM, _SEM] + [_HBM] * (2 * n) + [any_space],
        out_shape=[pltpu.SemaphoreType.DMA((n,))] * 2 + [pltpu.HBM(b.shape, b.dtype) for b in vs + lands]
        + [_sds(after.shape, after.dtype)],
        input_output_aliases={i: 2 + i for i in range(2 * n + 1)},
        compiler_params=pltpu.CompilerParams(has_side_effects=_EFFECT),
    )(*[pltpu.with_memory_space_constraint(b, pltpu.HBM) for b in vs + lands], after)
    return (out[0], out[1], tuple(out[2:2 + n]), tuple(out[2 + n:2 + 2 * n])), out[-1]


def _sibling_wait(started, after, name):
    send, recv, vs, lands = started
    n = len(vs)

    def body(*refs):
        for cp in _sibling_copies(refs[:n], refs[n:2 * n], refs[2 * n], refs[2 * n + 1]):
            cp.wait_send()
            cp.wait_recv()

    out = pl.pallas_call(
        body, name=name, in_specs=[_HBM] * (2 * n) + [_SEM, _SEM, pl.BlockSpec(memory_space=pl.ANY)],
        out_specs=[_HBM] * (2 * n), out_shape=[pltpu.HBM(b.shape, b.dtype) for b in vs + lands],
        input_output_aliases={i: i for i in range(2 * n)},
        compiler_params=pltpu.CompilerParams(has_side_effects=_EFFECT),
    )(*vs, *lands, send, recv, after)
    return out[:n], out[n:]


def _everyone(x, y, c):
    return [(1 - x if k & 4 else x, 1 - y if k & 2 else y, 1 - c if k & 1 else c) for k in range(1, 8)]


def _all_gather_copies(land_ref, send, recv, arriving):
    x, y, c = _place()
    me = 4 * x + 2 * y + c
    return [pltpu.make_async_remote_copy(
        src_ref=land_ref.at[me], dst_ref=land_ref.at[4 * p[0] + 2 * p[1] + p[2] if arriving else me],
        send_sem=send.at[k], recv_sem=recv.at[k], device_id=p, device_id_type=MESH)
        for k, p in enumerate(_everyone(x, y, c))]


def _all_gather_start(v, name):
    x, y, c = _place()
    land = lax.dynamic_update_slice(lax.empty((8,) + v.shape, v.dtype), v[None], (4 * x + 2 * y + c, 0, 0))

    def body(land_ref, v_ref, send, recv, land_out, v_out):
        for cp in _all_gather_copies(land_ref, send, recv, False):
            cp.start()

    any_space = pl.BlockSpec(memory_space=pl.ANY)
    out = pl.pallas_call(
        body, name=name, in_specs=[_HBM, any_space], out_specs=[_SEM, _SEM, _HBM, any_space],
        out_shape=[pltpu.SemaphoreType.DMA((7,))] * 2 + [pltpu.HBM(land.shape, land.dtype), _sds(v.shape, v.dtype)],
        input_output_aliases={0: 2, 1: 3}, compiler_params=pltpu.CompilerParams(has_side_effects=_EFFECT),
    )(pltpu.with_memory_space_constraint(land, pltpu.HBM), v)
    return tuple(out[:3]), out[3]


def _all_gather_wait(started, after, name):
    send, recv, land = started

    def body(land_ref, send, recv, after_ref, land_out):
        for cp in _all_gather_copies(land_ref, send, recv, True):
            cp.wait_send()
            cp.wait_recv()

    return pl.pallas_call(
        body, name=name, in_specs=[_HBM, _SEM, _SEM, pl.BlockSpec(memory_space=pl.ANY)], out_specs=_HBM,
        out_shape=pltpu.HBM(land.shape, land.dtype), input_output_aliases={0: 0},
        compiler_params=pltpu.CompilerParams(has_side_effects=_EFFECT),
    )(land, send, recv, after)


def _cast_place(ws, shard, name, after=()):
    n = len(ws)

    def body(s_ref, *refs):
        for w_ref, o_ref in zip(refs[:n], refs[-n:]):
            o_ref[0] = w_ref[...].astype(BF16)

    return pl.pallas_call(
        body, name=name, out_shape=[_sds((N_SHARD,) + w.shape, BF16) for w in ws],
        grid_spec=pltpu.PrefetchScalarGridSpec(
            num_scalar_prefetch=1, grid=(4,),
            in_specs=[pl.BlockSpec((w.shape[0] // 4, w.shape[1]), lambda i, s: (i, 0)) for w in ws]
            + [pl.BlockSpec(memory_space=pl.ANY)] * len(after),
            out_specs=[pl.BlockSpec((1, w.shape[0] // 4, w.shape[1]), lambda i, s: (s[0], i, 0)) for w in ws]),
        compiler_params=pltpu.CompilerParams(dimension_semantics=("arbitrary",), vmem_limit_bytes=VMEM_LIMIT),
    )(shard.reshape(1).astype(jnp.int32), *ws, *after)


def _mod_rows(c8, w_ada, b_ada, name):
    n = w_ada.shape[1]

    def gather(src_ref, dst_ref, send, recv, loc, base):
        x, y, c = _place()
        me = 4 * x + 2 * y + c
        own = pltpu.make_async_copy(src_ref, dst_ref.at[me], loc)
        own.start()
        peers = _everyone(x, y, c)
        sends = [pltpu.make_async_remote_copy(src_ref=src_ref, dst_ref=dst_ref.at[me], send_sem=send.at[base + k],
                                              recv_sem=recv.at[base + k], device_id=p, device_id_type=MESH)
                 for k, p in enumerate(peers)]
        for cp in sends:
            cp.start()
        for k, p in enumerate(peers):
            pltpu.make_async_remote_copy(src_ref=src_ref, dst_ref=dst_ref.at[4 * p[0] + 2 * p[1] + p[2]],
                                         send_sem=send.at[base + k], recv_sem=recv.at[base + k], device_id=p,
                                         device_id_type=MESH).wait_recv()
        for cp in sends:
            cp.wait_send()
        own.wait()

    def body(c_ref, w_ref, b_ref, a_ref, parts_ref, c_all, part, send, recv, loc):
        gather(c_ref, c_all, send, recv, loc.at[0], 0)
        cv = jnp.max(c_all[...], axis=1)
        ca = cv * _sigmoid(cv)
        a_ref[...] = ca
        part[...] = jnp.dot(ca, w_ref[...], precision=lax.Precision.HIGHEST, preferred_element_type=F32) + b_ref[...]
        gather(part, parts_ref, send, recv, loc.at[1], 7)

    vmem = pl.BlockSpec(memory_space=pltpu.VMEM)
    return pl.pallas_call(
        body, name=name, in_specs=[vmem] * 3, out_specs=[vmem, vmem],
        out_shape=[_sds((8, D_MODEL)), _sds((8, 8, n))],
        scratch_shapes=[pltpu.VMEM((8, 8, D_MODEL), F32), pltpu.VMEM((8, n), F32), pltpu.SemaphoreType.DMA((14,)),
                        pltpu.SemaphoreType.DMA((14,)), pltpu.SemaphoreType.DMA((2,))],
        compiler_params=pltpu.CompilerParams(vmem_limit_bytes=VMEM_LIMIT))(c8, w_ada, b_ada)


def _sum_received(gs, shard, lands, name):
    n = len(gs)

    def body(s_ref, *refs):
        for g_ref, l_ref, o_ref in zip(refs[:n], refs[n:2 * n], refs[2 * n:]):
            o_ref[...] = ((g_ref[0] + l_ref[0].astype(F32)) + l_ref[1].astype(F32)) + l_ref[2].astype(F32)

    quarter = lambda g: (g.shape[1] // 4, g.shape[2])
    return pl.pallas_call(
        body, name=name, out_shape=[_sds(g.shape[1:]) for g in gs],
        grid_spec=pltpu.PrefetchScalarGridSpec(
            num_scalar_prefetch=1, grid=(4,),
            in_specs=[pl.BlockSpec((1,) + quarter(g), lambda i, s: (s[0], i, 0)) for g in gs]
            + [pl.BlockSpec((3,) + quarter(g), lambda i, s: (0, i, 0)) for g in gs],
            out_specs=[pl.BlockSpec(quarter(g), lambda i, s: (i, 0)) for g in gs]),
        compiler_params=pltpu.CompilerParams(dimension_semantics=("arbitrary",), vmem_limit_bytes=VMEM_LIMIT),
    )(shard.reshape(1).astype(jnp.int32), *gs, *lands)


def _adamw_outer(w, ct, dm, m, v, name):
    k, n = w.shape
    tr = k // 4

    def body(w_ref, c_ref, d_ref, m_ref, v_ref, g_out, d_out, m_out, v_out):
        cv = c_ref[...]
        dv = d_ref[...]
        g = cv[:, 0:1] * dv[0:1, :]
        for i in range(1, 8):
            g = g + cv[:, i:i + 1] * dv[i:i + 1, :]
        g_out[...] = g
        d_out[...], m_out[...], v_out[...] = _adamw_math(w_ref[...], g, m_ref[...], v_ref[...])

    row = _rows(tr, n)
    return _call(body, name=name, grid=(4,),
                 in_specs=[row, _rows(tr, 8), pl.BlockSpec((8, n), lambda i: (0, 0)), row, row],
                 out_specs=[row] * 4, out_shape=[_sds((k, n))] * 4)(w, ct, dm, m, v)


def _adamw_math(w, g, m, v):
    m_new = ADAM_B1 * m + (1.0 - ADAM_B1) * g
    v_new = ADAM_B2 * v + (1.0 - ADAM_B2) * (g * g)
    m_hat = m_new / (1.0 - ADAM_B1 ** ADAM_STEP)
    v_hat = v_new / (1.0 - ADAM_B2 ** ADAM_STEP)
    return -ADAM_LR * (m_hat / (jnp.sqrt(v_hat) + ADAM_EPS) + ADAM_WD * w), m_new, v_new


def _small_update(stats, smalls, name):
    offsets = [ST_DMOD, ST_DG1, ST_DLB, ST_DOG, ST_DAG, ST_DG2, ST_DFG]
    lb_index = 2

    def body(*refs):
        s_ref, ins, l_ref, outs = refs[0], refs[1:22], refs[22], refs[23:]
        tot = s_ref[0:1, :]
        for i in range(1, 8):
            tot = tot + s_ref[i:i + 1, :]
        l_ref[...] = jnp.zeros((1, 128), F32) + (0.5 / D_MODEL) * jnp.sum(tot[:, ST_LOSS:ST_LOSS + D_MODEL])
        for p, off in enumerate(offsets):
            w_ref, m_ref, v_ref = ins[3 * p:3 * p + 3]
            g_out, d_out, m_out, v_out = outs[4 * p:4 * p + 4]
            g = tot[:, off:off + w_ref.shape[1]]
            if p == lb_index:
                lg = w_ref[...]
                lb = _sigmoid(lg[0:1] - lg[1:2])
                g = g * lb * (1.0 - lb)
            for r in range(w_ref.shape[0]):
                rows = slice(r, r + 1)
                gr = g if r == 0 else -g
                delta, m_new, v_new = _adamw_math(w_ref[rows, :], gr, m_ref[rows, :], v_ref[rows, :])
                g_out[rows, :] = gr
                d_out[rows, :] = delta
                m_out[rows, :] = m_new
                v_out[rows, :] = v_new

    full = lambda a: pl.BlockSpec(a.shape, lambda i: (0, 0))
    flat = [a for t in smalls for a in t]
    return _call(body, name=name, grid=(1,),
                 in_specs=[full(stats)] + [full(a) for a in flat],
                 out_specs=[pl.BlockSpec((1, 128), lambda i: (0, 0))] + [full(t[0]) for t in smalls for _ in range(4)],
                 out_shape=[_sds((1, 128))] + [_sds(t[0].shape) for t in smalls for _ in range(4)])(stats, *flat)


def _adamw(params, name):
    n = len(params)

    def body(*refs):
        for p in range(n):
            w_ref, ga_ref, gb_ref, m_ref, v_ref = refs[5 * p:5 * p + 5]
            g_out, d_out, m_out, v_out = refs[5 * n + 4 * p:5 * n + 4 * p + 4]
            g = ga_ref[...] + gb_ref[...]
            g_out[...] = g
            d_out[...], m_out[...], v_out[...] = _adamw_math(w_ref[...], g, m_ref[...], v_ref[...])

    row = lambda w: _rows(w.shape[0] // 4, w.shape[1])
    out = _call(body, name=name, grid=(4,), in_specs=[row(p[0]) for p in params for _ in range(5)],
                out_specs=[row(p[0]) for p in params for _ in range(4)],
                out_shape=[_sds(p[0].shape) for p in params for _ in range(4)])(*[a for p in params for a in p])
    return [tuple(out[4 * p:4 * p + 4]) for p in range(n)]


def kernel(x, c, w_ada, b_ada, norm1_g, w_in, hg_lb_logits, hg_onorm_g, att_onorm_g, w_out, norm2_g, w_gate_up, w_down, final_g, loss_target, m_w_ada, m_b_ada, m_norm1_g, m_w_in, m_hg_lb_logits, m_hg_onorm_g, m_att_onorm_g, m_w_out, m_norm2_g, m_w_gate_up, m_w_down, m_final_g, v_w_ada, v_b_ada, v_norm1_g, v_w_in, v_hg_lb_logits, v_hg_onorm_g, v_att_onorm_g, v_w_out, v_norm2_g, v_w_gate_up, v_w_down, v_final_g):
    ix, iy, ic = _place()
    shard = 2 * ix + iy
    sample = 4 * ix + 2 * iy + ic
    n_ada = w_ada.shape[2]

    shards = [w_in[0], w_out[0], w_gate_up[0], w_down[0]]
    names = ["w_in", "w_out", "w_gu", "w_down"]
    shapes = [(N_SHARD,) + w.shape for w in shards]
    placed = [(_cast_place(shards[:1], shard, "place_w_in")[0],)]

    b_part = lax.dynamic_slice(b_ada, (0, shard * n_ada), (1, n_ada))
    c_act, parts = _mod_rows(jnp.broadcast_to(c, (8, D_MODEL)), w_ada[0], b_part, "mod_rows")
    parts = parts[::2]
    mod = lax.dynamic_index_in_dim(parts, sample, axis=1, keepdims=False).reshape(1, 6 * D_MODEL)
    (first,), mod = _exchange_start(placed[:1], mod, "half", "gather_start_w_in")
    gathering = {}

    def get_w(name, after):
        if name == "w_in":
            placed_rest = [(p,) for p in _cast_place(shards[1:], shard, "place_rest", (after,))]
            halves = _exchange_wait(first, placed_rest[0][0], "half", "gather_wait_w_in")
            (passing,), token = _exchange_start([tuple(halves)], mod, "forward", "forward_start_w_in")
            rest, token = _exchange_start(placed_rest, token, "gather", "gather_start_rest")
            (full,) = _exchange_wait(passing, token, "forward", "forward_wait_w_in")
            gathering.update(zip(names[1:], rest))
            return full
        (full,) = _exchange_wait(gathering[name], after, "gather", "gather_wait_" + name)
        return full if name == "w_gu" else full.reshape(1, -1, D_MODEL)

    scattering = {}

    def put_g(name, g, g_bf16, then):
        shape = shapes[names.index(name)]
        land = lax.empty((3,) + shape[1:], BF16)
        (started,), then = _exchange_start([(g_bf16.reshape(shape), land)], then, "scatter", "scatter_start_" + name)
        scattering[name] = (g.reshape(shape), started)
        return then

    def summed(group, after, tag):
        lands = [_exchange_wait(scattering[nm][1], after, "scatter", "scatter_wait_" + nm)[1] for nm in group]
        return _sum_received([scattering[nm][0] for nm in group], shard, lands, "sum_" + tag)

    early = ["w_down", "w_gu", "w_out"]
    swapping = []

    def late(a):
        started, a = _sibling_start(summed(early, a, "early"), a, "swap_start")
        swapping.append(started)
        return a

    def project(h1):
        own = _mm_own_shard(h1, shards[0], shard, N_SHARD, "mm_in_own")
        w_full = get_w("w_in", own)
        return _mm_other_shards(h1, w_full, own, shard, "mm_in_rest"), w_full

    dx, stats = _local_step(x[0], loss_target[0], mod, norm1_g, hg_lb_logits, hg_onorm_g, att_onorm_g,
                            norm2_g, final_g, get_w, put_g, late, project)

    gathering_stats, stats = _all_gather_start(stats, "stats_start")
    moments = [(m_w_in, v_w_in), (m_w_out, v_w_out), (m_w_gate_up, v_w_gate_up), (m_w_down, v_w_down)]

    def update(group, sums, other, tag):
        params = [(shards[names.index(nm)], s, o, moments[names.index(nm)][0][0], moments[names.index(nm)][1][0])
                  for nm, s, o in zip(group, sums, other)]
        return dict(zip(group, _adamw(params, "adamw_" + tag)))

    sums, other = _sibling_wait(swapping[0], stats, "swap_wait")
    done = update(early, sums, other, "early")
    swapping_in, stats = _sibling_start(summed(["w_in"], done["w_out"][1], "w_in"), stats, "swap_start_w_in")

    stats_all = _all_gather_wait(gathering_stats, stats, "stats_wait").reshape(8, ST_WIDTH)
    dmod = lax.dynamic_slice(stats_all, (0, ST_DMOD + shard * n_ada), (8, n_ada))

    as_row = lambda a: a.reshape(1, -1) if a.ndim == 1 else a
    smalls = [tuple(as_row(a) for a in t) for t in [
        (b_ada, m_b_ada, v_b_ada), (norm1_g, m_norm1_g, v_norm1_g),
        (hg_lb_logits, m_hg_lb_logits, v_hg_lb_logits), (hg_onorm_g, m_hg_onorm_g, v_hg_onorm_g),
        (att_onorm_g, m_att_onorm_g, v_att_onorm_g), (norm2_g, m_norm2_g, v_norm2_g),
        (final_g, m_final_g, v_final_g)]]
    loss, *small_out = _small_update(stats_all, smalls, "small_update")
    shapes_out = [b_ada.shape, norm1_g.shape, hg_lb_logits.shape, hg_onorm_g.shape, att_onorm_g.shape,
                  norm2_g.shape, final_g.shape]
    sg, sd, sm, sv = [[small_out[4 * p + i].reshape(shapes_out[p]) for p in range(7)] for i in range(4)]

    ada = _adamw_outer(w_ada[0], c_act.T, dmod, m_w_ada[0], v_w_ada[0], "adamw_w_ada")
    sum_in, other_in = _sibling_wait(swapping_in, ada[1], "swap_wait_w_in")
    done.update(update(["w_in"], sum_in, other_in, "w_in"))
    big = [ada] + [done[nm] for nm in names]
    bg, bd, bm, bv = [[t[i][None] for t in big] for i in range(4)]

    def order(b, s):
        return [b[0], s[0], s[1], b[1], s[2], s[3], s[4], b[2], s[5], b[3], b[4], s[6]]

    return (loss[0, 0], dx[None], *order(bg, sg), *order(bd, sd), *order(bm, sm), *order(bv, sv))
```

```python
import functools

import jax
import jax.numpy as jnp
from jax import lax
from jax.experimental import pallas as pl
from jax.experimental.pallas import tpu as pltpu

F32 = jnp.float32
BF16 = jnp.bfloat16
MESH = pl.DeviceIdType.MESH

D_MODEL = 1024
HG_WIDTH = 512
HG_HEAD = 128
HG_CHUNK = 64
HG_GROUP = 4
ATT_WIDTH = 512
ATT_HEADS = 8
ATT_BLOCK = 128
DILATIONS = (1, 4, 16)
D_FF = 2816
IN_WIDTH = 3584
N_SHARD = 4
RMS_EPS = 1e-6
NEG = -1e30

ADAM_LR = 0.001
ADAM_B1 = 0.9
ADAM_B2 = 0.999
ADAM_EPS = 1e-08
ADAM_WD = 0.01
ADAM_STEP = 10

VMEM_LIMIT = 56 * 2**20

ST_LOSS, ST_DFG, ST_DG2, ST_DG1 = 0, 1024, 2048, 3072
ST_DLB, ST_DAG, ST_DOG, ST_DMOD = 4096, 4608, 5120, 5248
ST_WIDTH = 5248 + 6144


def _call(body, *, name, grid, in_specs, out_specs, out_shape, scratch_shapes=(), aliases=None):
    return pl.pallas_call(
        body, name=name, grid=grid, in_specs=in_specs, out_specs=out_specs, out_shape=out_shape,
        scratch_shapes=list(scratch_shapes), input_output_aliases=aliases or {},
        compiler_params=pltpu.CompilerParams(
            dimension_semantics=("arbitrary",) * len(grid), vmem_limit_bytes=VMEM_LIMIT))


def _sds(shape, dtype=F32):
    return jax.ShapeDtypeStruct(shape, dtype)


def _dot(a, b):
    return jnp.dot(a, b, preferred_element_type=F32)


def _dot_nt(a, b):
    return lax.dot_general(a, b, (((1,), (1,)), ((), ())), preferred_element_type=F32)


def _dot_tn(a, b):
    return lax.dot_general(a, b, (((0,), (0,)), ((), ())), preferred_element_type=F32)


def _sigmoid(x):
    return 1.0 / (1.0 + jnp.exp(-x))


def _rows(tr, width):
    return pl.BlockSpec((tr, width), lambda i: (i, 0))


def _vec(width):
    return pl.BlockSpec((1, width), lambda i: (0, 0))


def _acc(ref, val, first):
    @pl.when(first)
    def _():
        ref[...] = val

    @pl.when(jnp.logical_not(first))
    def _():
        ref[...] += val


def _sub_rows(tr, fine, width):
    return pl.BlockSpec((fine, tr // fine, width), lambda i: (0, i, 0))


def _regroup_matrix(tr, groups):
    a = lax.broadcasted_iota(jnp.int32, (tr, tr), 0)
    b = lax.broadcasted_iota(jnp.int32, (tr, tr), 1)
    return (b == (a % groups) * (tr // groups) + a // groups).astype(BF16)


def _regroup(m, v, lanes=None):
    if v.dtype == BF16:
        return _dot(m, v)
    width = v.shape[1]
    packed, out = None, None
    for i in range(3):
        part = v.astype(BF16).astype(F32)
        v = v - part
        if lanes is None:
            out = _dot(m, part.astype(BF16)) if i == 0 else out + _dot(m, part.astype(BF16))
        else:
            packed = part if i == 0 else packed + pltpu.roll(part, i * lanes, 1)
    if lanes is None:
        return out
    out = _dot(m, packed.astype(BF16))
    out = out + pltpu.roll(out, width - lanes, 1) + pltpu.roll(out, width - 2 * lanes, 1)
    return jnp.where(lax.broadcasted_iota(jnp.int32, (1, width), 1) < lanes, out, 0.0)


def _mm_nn(a, b3, name, tm=1024, out_dtype=F32):
    m, k = a.shape
    s, _, n = b3.shape

    def body(a_ref, b_ref, o_ref):
        o_ref[...] = _dot(a_ref[...], b_ref[0]).astype(out_dtype)

    return _call(
        body, name=name, grid=(s, m // tm),
        in_specs=[pl.BlockSpec((tm, k), lambda j, i: (i, 0)), pl.BlockSpec((1, k, n), lambda j, i: (j, 0, 0))],
        out_specs=pl.BlockSpec((tm, n), lambda j, i: (i, j)), out_shape=_sds((m, s * n), out_dtype))(a, b3)


def _mm_own_shard(a, w, shard, s, name, tm=1024):
    m, k = a.shape
    n = w.shape[1]

    def body(s_ref, a_ref, w_ref, o_ref):
        o_ref[...] = _dot(a_ref[...], w_ref[...].astype(BF16)).astype(BF16)

    return pl.pallas_call(
        body, name=name, out_shape=_sds((m, s * n), BF16),
        grid_spec=pltpu.PrefetchScalarGridSpec(
            num_scalar_prefetch=1, grid=(m // tm,),
            in_specs=[pl.BlockSpec((tm, k), lambda i, sh: (i, 0)), pl.BlockSpec((k, n), lambda i, sh: (0, 0))],
            out_specs=pl.BlockSpec((tm, n), lambda i, sh: (i, sh[0]))),
        compiler_params=pltpu.CompilerParams(dimension_semantics=("arbitrary",), vmem_limit_bytes=VMEM_LIMIT),
    )(shard.reshape(1).astype(jnp.int32), a, w)


def _mm_other_shards(a, b3, partial, shard, name, tm=1024):
    m, k = a.shape
    s, _, n = b3.shape
    which = lambda j, sh: (sh[0] + 1 + j) % s

    def body(s_ref, a_ref, b_ref, p_ref, o_ref):
        o_ref[...] = _dot(a_ref[...], b_ref[0]).astype(BF16)

    return pl.pallas_call(
        body, name=name, out_shape=_sds(partial.shape, BF16),
        grid_spec=pltpu.PrefetchScalarGridSpec(
            num_scalar_prefetch=1, grid=(s - 1, m // tm),
            in_specs=[pl.BlockSpec((tm, k), lambda j, i, sh: (i, 0)),
                      pl.BlockSpec((1, k, n), lambda j, i, sh: (which(j, sh), 0, 0)),
                      pl.BlockSpec(memory_space=pl.ANY)],
            out_specs=pl.BlockSpec((tm, n), lambda j, i, sh: (i, which(j, sh)))),
        input_output_aliases={3: 0},
        compiler_params=pltpu.CompilerParams(dimension_semantics=("arbitrary",) * 2, vmem_limit_bytes=VMEM_LIMIT),
    )(shard.reshape(1).astype(jnp.int32), a, b3, partial)


def _mm_tn(a, dy, s, name, tm, tk, group=1):
    m, k = a.shape
    n = dy.shape[1] // s
    steps = m // tm

    def body(a_ref, dy_ref, o_ref, ob_ref):
        p = _dot_tn(a_ref[...], dy_ref[...])
        for g in range(group):
            pg = p[:, g * n:(g + 1) * n]
            if steps == 1:
                o_ref[g] = pg
                ob_ref[g] = pg.astype(BF16)
            else:
                _acc(o_ref.at[g], pg, pl.program_id(2) == 0)
        if steps > 1:
            @pl.when(pl.program_id(2) == steps - 1)
            def _():
                ob_ref[...] = o_ref[...].astype(BF16)

    out = pl.BlockSpec((group, tk, n), lambda kk, j, i: (j, kk, 0))
    return _call(
        body, name=name, grid=(k // tk, s // group, steps),
        in_specs=[pl.BlockSpec((tm, tk), lambda kk, j, i: (i, kk)),
                  pl.BlockSpec((tm, group * n), lambda kk, j, i: (i, j))],
        out_specs=[out, out], out_shape=[_sds((s, k, n)), _sds((s, k, n), BF16)])(a, dy)


def _rms(x):
    return lax.rsqrt(jnp.mean(x * x, axis=-1, keepdims=True) + RMS_EPS)


def _rms_bwd(dxh, xh, r):
    return r * (dxh - xh * jnp.mean(dxh * xh, axis=-1, keepdims=True))


def _norm_mod(x, g, scale, shift, name, tr=512):
    t = x.shape[0]

    def body(x_ref, g_ref, sc_ref, sh_ref, h_ref):
        xv = x_ref[...]
        n = xv * _rms(xv) * g_ref[...]
        h_ref[...] = (n * (1.0 + sc_ref[...]) + sh_ref[...]).astype(BF16)

    return _call(body, name=name, grid=(t // tr,),
                 in_specs=[_rows(tr, D_MODEL), _vec(D_MODEL), _vec(D_MODEL), _vec(D_MODEL)],
                 out_specs=_rows(tr, D_MODEL), out_shape=_sds((t, D_MODEL), BF16))(x, g, scale, shift)


def _out_resid_norm_mod(x, mixin, w_out, gate, g, scale, shift, name, tr=512):
    t = x.shape[0]

    def body(x_ref, mi_ref, w_ref, gt_ref, g_ref, sc_ref, sh_ref, m_ref, x2_ref, h_ref):
        mix = _dot(mi_ref[...], w_ref[0])
        m_ref[...] = mix
        x2 = x_ref[...] + gt_ref[...] * mix
        x2_ref[...] = x2
        n = x2 * _rms(x2) * g_ref[...]
        h_ref[...] = (n * (1.0 + sc_ref[...]) + sh_ref[...]).astype(BF16)

    row = _rows(tr, D_MODEL)
    return _call(body, name=name, grid=(t // tr,),
                 in_specs=[row, row, pl.BlockSpec(w_out.shape, lambda i: (0, 0, 0))] + [_vec(D_MODEL)] * 4,
                 out_specs=[row, row, row],
                 out_shape=[_sds((t, D_MODEL)), _sds((t, D_MODEL)), _sds((t, D_MODEL), BF16)])(
                     x, mixin, w_out, gate, g, scale, shift)


def _mm_gate_up(h, w_gu, name, tm=1024):
    m, k = h.shape
    n = w_gu.shape[2]

    def body(h_ref, wa_ref, wu_ref, da_ref, du_ref, o_ref, w_au):
        @pl.when(pl.program_id(1) == 0)
        def _():
            w_au[:, :n] = wa_ref[0]
            w_au[:, n:] = wu_ref[0]

        au = _dot(h_ref[...], w_au[...])
        a, u = au[:, :n], au[:, n:]
        sg = _sigmoid(a)
        silu = a * sg
        da_ref[...] = (u * sg * (1.0 + a * (1.0 - sg))).astype(BF16)
        du_ref[...] = silu.astype(BF16)
        o_ref[...] = (silu * u).astype(BF16)

    out = pl.BlockSpec((tm, n), lambda j, i: (i, j))
    return _call(body, name=name, grid=(2, m // tm),
                 in_specs=[pl.BlockSpec((tm, k), lambda j, i: (i, 0)), pl.BlockSpec((1, k, n), lambda j, i: (j, 0, 0)),
                           pl.BlockSpec((1, k, n), lambda j, i: (j + 2, 0, 0))],
                 out_specs=[out, out, out], out_shape=[_sds((m, 2 * n), BF16)] * 3,
                 scratch_shapes=[pltpu.VMEM((k, 2 * n), BF16)])(h, w_gu, w_gu)


def _mm_down_dx(dffn, w_down, act_da, act_du, name, tm=512):
    m = dffn.shape[0]
    _, k, n = w_down.shape

    def body(d_ref, w_ref, da_ref, du_ref, o_ref):
        dact = _dot_nt(d_ref[...], w_ref[0])
        o_ref[:, :k] = (dact * da_ref[...].astype(F32)).astype(BF16)
        o_ref[:, k:] = (dact * du_ref[...].astype(F32)).astype(BF16)

    return _call(body, name=name, grid=(m // tm,),
                 in_specs=[_rows(tm, n), pl.BlockSpec((1, k, n), lambda i: (0, 0, 0)), _rows(tm, k), _rows(tm, k)],
                 out_specs=_rows(tm, 2 * k), out_shape=_sds((m, 2 * k), BF16))(dffn, w_down, act_da, act_du)


def _down_loss(x2, act, w_down, gate, fg, tgt, name, tr=512):
    t = x2.shape[0]
    _, k, n = w_down.shape

    def body(x_ref, a_ref, w_ref, gt_ref, fg_ref, t_ref, dx_ref, df_ref, l_ref, dfg_ref, dgt_ref):
        first = pl.program_id(0) == 0
        ffn_v = _dot(a_ref[...], w_ref[0])
        x3 = x_ref[...] + gt_ref[...] * ffn_v
        r = _rms(x3)
        xh = x3 * r
        err = xh * fg_ref[...] - t_ref[...]
        dy = err * (1.0 / D_MODEL)
        dx3 = _rms_bwd(dy * fg_ref[...], xh, r)
        dx_ref[...] = dx3
        df_ref[...] = (dx3 * gt_ref[...]).astype(BF16)
        _acc(l_ref, jnp.sum(err * err, axis=0, keepdims=True), first)
        _acc(dfg_ref, jnp.sum(dy * xh, axis=0, keepdims=True), first)
        _acc(dgt_ref, jnp.sum(dx3 * ffn_v, axis=0, keepdims=True), first)

    row, vec = _rows(tr, D_MODEL), _vec(D_MODEL)
    return _call(body, name=name, grid=(t // tr,),
                 in_specs=[row, _rows(tr, k), pl.BlockSpec((1, k, n), lambda i: (0, 0, 0)), vec, vec, row],
                 out_specs=[row, row, vec, vec, vec],
                 out_shape=[_sds((t, D_MODEL)), _sds((t, D_MODEL), BF16)] + [_sds((1, D_MODEL))] * 3)(
                     x2, act, w_down, gate, fg, tgt)


def _norm_mod_bwd(dh, x, g, scale, dres, name, gate=None, mix=None, w=None, tr=512):
    t = x.shape[0]
    below = gate is not None

    def body(*refs):
        if w is not None:
            w_ref, w_full, sem, refs = refs[1], refs[-2], refs[-1], refs[:1] + refs[2:-2]

            @pl.when(pl.program_id(0) == 0)
            def _():
                n = w.shape[2]
                copies = [pltpu.make_async_copy(w_ref.at[j], w_full.at[:, pl.ds(j * n, n)], sem.at[j])
                          for j in range(w.shape[0])]
                for cp in copies:
                    cp.start()
                for cp in copies:
                    cp.wait()

        if below:
            dh_ref, x_ref, g_ref, sc_ref, dr_ref, gt_ref, m_ref, dx_ref, dsh_ref, dsc_ref, dg_ref, dgt_ref, dm_ref = refs
        else:
            dh_ref, x_ref, g_ref, sc_ref, dr_ref, dx_ref, dsh_ref, dsc_ref, dg_ref = refs
        first = pl.program_id(0) == 0
        xv = x_ref[...]
        if w is None:
            dhv = dh_ref[...].astype(F32)
        else:
            dhv = _dot_nt(dh_ref[...], w_full[...])
        r = _rms(xv)
        xh = xv * r
        dn = dhv * (1.0 + sc_ref[...])
        dx = dr_ref[...] + _rms_bwd(dn * g_ref[...], xh, r)
        dx_ref[...] = dx
        _acc(dsh_ref, jnp.sum(dhv, axis=0, keepdims=True), first)
        _acc(dsc_ref, jnp.sum(dhv * xh * g_ref[...], axis=0, keepdims=True), first)
        _acc(dg_ref, jnp.sum(dn * xh, axis=0, keepdims=True), first)
        if below:
            _acc(dgt_ref, jnp.sum(dx * m_ref[...], axis=0, keepdims=True), first)
            dm_ref[...] = (dx * gt_ref[...]).astype(BF16)

    row, vec = _rows(tr, D_MODEL), _vec(D_MODEL)
    first_specs = [row] if w is None else [_rows(tr, dh.shape[1]), pl.BlockSpec(memory_space=pl.ANY)]
    scratch = [] if w is None else [pltpu.VMEM((w.shape[1], dh.shape[1]), BF16), pltpu.SemaphoreType.DMA((w.shape[0],))]
    in_specs = first_specs + [row, vec, vec, row] + ([vec, row] if below else [])
    out_specs = [row, vec, vec, vec] + ([vec, row] if below else [])
    out_shape = [_sds((t, D_MODEL))] + [_sds((1, D_MODEL))] * 3 + ([_sds((1, D_MODEL)), _sds((t, D_MODEL), BF16)] if below else [])
    args = ((dh,) if w is None else (dh, w)) + (x, g, scale, dres) + ((gate, mix) if below else ())
    return _call(body, name=name, grid=(t // tr,), in_specs=in_specs, out_specs=out_specs, out_shape=out_shape,
                 scratch_shapes=scratch)(*args)


def _mix_in_bwd(dmix, w_out, o_hg, proj, att, og, ag, fine, name, tr=512):
    t = o_hg.shape[0]

    def body(dy_ref, w_ref, o_ref, g_ref, a_ref, og_ref, ag_ref,
             do_ref, dg_ref, da_ref, dd_ref, das_ref, dds_ref, dog_ref, dag_ref, to_sub):
        first = pl.program_id(0) == 0

        @pl.when(first)
        def _():
            to_sub[...] = _regroup_matrix(tr, tr // fine)

        dmi = _dot_nt(dy_ref[...], w_ref[0])
        dog = jnp.zeros((1, HG_HEAD), F32)
        for h in range(HG_WIDTH // HG_HEAD):
            sl = slice(h * HG_HEAD, (h + 1) * HG_HEAD)
            oh = o_ref[:, sl].astype(F32)
            gv = g_ref[:, sl].astype(F32)
            dv = dmi[:, sl]
            r = _rms(oh)
            xh = oh * r
            sg = _sigmoid(gv)
            dno = dv * gv * sg
            dg_ref[:, sl] = (dv * xh * og_ref[...] * sg * (1.0 + gv * (1.0 - sg))).astype(BF16)
            dog = dog + jnp.sum(dno * xh, axis=0, keepdims=True)
            do_ref[:, sl] = _rms_bwd(dno * og_ref[...], xh, r).astype(BF16)
        _acc(dog_ref, dog, first)
        av = a_ref[...]
        dav = dmi[:, HG_WIDTH:]
        r = _rms(av)
        xa = av * r
        _acc(dag_ref, jnp.sum(dav * xa, axis=0, keepdims=True), first)
        datt = _rms_bwd(dav * ag_ref[...], xa, r)
        da_ref[...] = datt.astype(BF16)
        das_ref[...] = _regroup(to_sub[...], datt.astype(BF16)).astype(BF16).reshape(das_ref.shape)
        prod = datt * av
        lane = lax.broadcasted_iota(jnp.int32, (1, 128), 1)
        dd = jnp.zeros((tr, 128), F32)
        for hp in range(ATT_HEADS // 2):
            pp = prod[:, hp * 128:(hp + 1) * 128]
            lo = jnp.sum(jnp.where(lane < 64, pp, 0.0), axis=-1, keepdims=True)
            hi = jnp.sum(jnp.where(lane >= 64, pp, 0.0), axis=-1, keepdims=True)
            dd = jnp.where(lane == 2 * hp, lo, dd)
            dd = jnp.where(lane == 2 * hp + 1, hi, dd)
        dd_ref[...] = dd
        dds_ref[...] = _regroup(to_sub[...], dd, ATT_HEADS).reshape(dds_ref.shape)

    half = _rows(tr, HG_WIDTH)
    out = _call(body, name=name, grid=(t // tr,),
                in_specs=[_rows(tr, D_MODEL), pl.BlockSpec(w_out.shape, lambda i: (0, 0, 0)), half,
                          pl.BlockSpec((tr, HG_WIDTH), lambda i: (i, 3)), half, _vec(HG_HEAD), _vec(ATT_WIDTH)],
                out_specs=[half, pl.BlockSpec((tr, HG_WIDTH), lambda i: (i, 3)), half, _rows(tr, 128),
                           _sub_rows(tr, fine, ATT_WIDTH), _sub_rows(tr, fine, 128), _vec(HG_HEAD), _vec(ATT_WIDTH)],
                out_shape=[_sds((t, HG_WIDTH), BF16), _sds((t, IN_WIDTH), BF16), _sds((t, HG_WIDTH), BF16),
                           _sds((t, 128)), _sds((fine, t // fine, ATT_WIDTH), BF16),
                           _sds((fine, t // fine, 128)), _sds((1, HG_HEAD)), _sds((1, ATT_WIDTH))],
                scratch_shapes=[pltpu.VMEM((tr, tr), BF16)])(dmix, w_out, o_hg, proj, att, og, ag)
    out = list(out)
    return out[:4] + [out[4].reshape(t, ATT_WIDTH), out[5].reshape(t, 128)] + out[6:]


def _dproj(dproj, dqkvs, name, tr=1024):
    t = dproj.shape[0]
    nbr = len(dqkvs)
    first = IN_WIDTH // ATT_WIDTH - 3

    def body(*refs):
        refs[-1][...] = sum(r[...].astype(F32) for r in refs[:nbr]).astype(BF16)

    return _call(body, name=name, grid=(t // tr, 3),
                 in_specs=[pl.BlockSpec((tr, ATT_WIDTH), lambda i, j: (i, j))] * nbr + [pl.BlockSpec(memory_space=pl.ANY)],
                 out_specs=pl.BlockSpec((tr, ATT_WIDTH), lambda i, j: (i, first + j)),
                 out_shape=_sds(dproj.shape, BF16), aliases={nbr: 0})(*dqkvs, dproj)


def _chunk_tri(upper):
    row = lax.broadcasted_iota(jnp.int32, (HG_GROUP, HG_CHUNK, HG_CHUNK), 1)
    col = lax.broadcasted_iota(jnp.int32, (HG_GROUP, HG_CHUNK, HG_CHUNK), 2)
    return (row <= col if upper else row >= col).astype(BF16)


def _chunk_cumsum(x, tri):
    x3 = x.reshape(HG_GROUP, HG_CHUNK, x.shape[1])
    dims = (((2,), (1,)), ((0,), (0,)))
    out = None
    for _ in range(3):
        part = x3.astype(BF16)
        x3 = x3 - part.astype(F32)
        term = lax.dot_general(tri, part, dims, preferred_element_type=F32)
        out = term if out is None else out + term
    return out.reshape(x.shape)


def _hg_gates(f_raw, q_raw, lb, tri):
    sg = _sigmoid(f_raw)
    f = lb + (1.0 - lb) * sg
    k = 1.0 - f
    b = _chunk_cumsum(jnp.log(f), tri)
    sq = _sigmoid(q_raw)
    return sg, f, k, b, sq


def _hg_masks(rows):
    row = lax.broadcasted_iota(jnp.int32, (rows, rows), 0)
    col = lax.broadcasted_iota(jnp.int32, (rows, rows), 1)
    same = (row // HG_CHUNK) == (col // HG_CHUNK)
    return jnp.logical_and(row >= col, same), jnp.logical_and(row <= col, same)


def _per_chunk(rows_of):
    return jnp.concatenate([jnp.broadcast_to(r, (HG_CHUNK, r.shape[1])) for r in rows_of], axis=0)


def _hgrn_fwd(proj, lb_logits, name):
    t = proj.shape[0]
    nc = t // HG_CHUNK
    nh = HG_WIDTH // HG_HEAD
    rows = HG_GROUP * HG_CHUNK

    def body(q_ref, f_ref, i_ref, lg_ref, o_ref, st_ref, s_scr):
        @pl.when(pl.program_id(0) == 0)
        def _():
            s_scr[...] = jnp.zeros_like(s_scr)

        lg = lg_ref[...]
        lb_all = _sigmoid(lg[0:1] - lg[1:2])
        causal, _ = _hg_masks(rows)
        tri = _chunk_tri(False)
        for h in range(nh):
            sl = slice(h * HG_HEAD, (h + 1) * HG_HEAD)
            q_raw = q_ref[:, sl].astype(F32)
            _, _, k, b, sq = _hg_gates(f_ref[:, sl].astype(F32), q_raw, lb_all[:, sl], tri)
            v = i_ref[:, sl].astype(BF16)
            gls = [b[(g + 1) * HG_CHUNK - 1:(g + 1) * HG_CHUNK] for g in range(HG_GROUP)]
            bm = _per_chunk([b[g * HG_CHUNK + HG_CHUNK // 2 - 1:g * HG_CHUNK + HG_CHUNK // 2] for g in range(HG_GROUP)])
            qd = (q_raw * sq * jnp.exp(b)).astype(BF16)
            qm = (q_raw * sq * jnp.exp(b - bm)).astype(BF16)
            km = (k * jnp.exp(bm - b)).astype(BF16)
            ke = (k * jnp.exp(_per_chunk(gls) - b)).astype(BF16)
            a = jnp.where(causal, _dot_nt(qm, km), 0.0).astype(BF16)
            o_intra = _dot(a, v)
            st = s_scr[h]
            o_inter = []
            for g in range(HG_GROUP):
                rs = slice(g * HG_CHUNK, (g + 1) * HG_CHUNK)
                st_ref[g, sl, :] = st
                o_inter.append(_dot_nt(qd[rs], st.astype(BF16)))
                st = st * jnp.exp(gls[g]) + _dot_tn(v[rs], ke[rs])
            s_scr[h] = st
            o_ref[:, sl] = (o_intra + jnp.concatenate(o_inter, axis=0)).astype(BF16)

    blk = lambda j: pl.BlockSpec((rows, HG_WIDTH), lambda c: (c, j))
    return _call(body, name=name, grid=(nc // HG_GROUP,),
                 in_specs=[blk(0), blk(1), blk(2), pl.BlockSpec((2, HG_WIDTH), lambda c: (0, 0))],
                 out_specs=[blk(0), pl.BlockSpec((HG_GROUP, HG_WIDTH, HG_HEAD), lambda c: (c, 0, 0))],
                 out_shape=[_sds((t, HG_WIDTH), BF16), _sds((nc, HG_WIDTH, HG_HEAD))],
                 scratch_shapes=[pltpu.VMEM((nh, HG_HEAD, HG_HEAD), F32)])(proj, proj, proj, lb_logits)


def _hgrn_bwd(proj, lb_logits, states, do, dproj, name):
    t = proj.shape[0]
    ng = t // (HG_GROUP * HG_CHUNK)
    nh = HG_WIDTH // HG_HEAD
    rows = HG_GROUP * HG_CHUNK

    def body(q_ref, f_ref, i_ref, lg_ref, st_ref, do_ref, _, d_ref, dlb_ref, ds_scr):
        first = pl.program_id(0) == 0

        @pl.when(first)
        def _():
            ds_scr[...] = jnp.zeros_like(ds_scr)

        lg = lg_ref[...]
        lb_all = _sigmoid(lg[0:1] - lg[1:2])
        causal, _ = _hg_masks(rows)
        tri = _chunk_tri(False)
        tri_t = _chunk_tri(True)
        dlb = []
        for h in range(nh):
            sl = slice(h * HG_HEAD, (h + 1) * HG_HEAD)
            q_raw = q_ref[:, sl].astype(F32)
            lb = lb_all[:, sl]
            sg, f, k, b, sq = _hg_gates(f_ref[:, sl].astype(F32), q_raw, lb, tri)
            v = i_ref[:, sl].astype(BF16)
            gls = [b[(g + 1) * HG_CHUNK - 1:(g + 1) * HG_CHUNK] for g in range(HG_GROUP)]
            bm = _per_chunk([b[g * HG_CHUNK + HG_CHUNK // 2 - 1:g * HG_CHUNK + HG_CHUNK // 2] for g in range(HG_GROUP)])
            eb = jnp.exp(b)
            ebm = jnp.exp(b - bm)
            emb = jnp.exp(bm - b)
            egb = jnp.exp(_per_chunk(gls) - b)
            ke = k * egb
            qd_b, qm_b = (q_raw * sq * eb).astype(BF16), (q_raw * sq * ebm).astype(BF16)
            km_b, ke_b = (k * emb).astype(BF16), ke.astype(BF16)
            dov = do_ref[:, sl].astype(BF16)
            a = jnp.where(causal, _dot_nt(qm_b, km_b), 0.0).astype(BF16)
            da = jnp.where(causal, _dot_nt(dov, v), 0.0).astype(BF16)
            dkm = _dot_tn(da, qm_b)
            dst = ds_scr[h]
            dqd_s, dv_s, dke_s, dgl_s = [None] * HG_GROUP, [None] * HG_GROUP, [None] * HG_GROUP, [None] * HG_GROUP
            for g in reversed(range(HG_GROUP)):
                rs = slice(g * HG_CHUNK, (g + 1) * HG_CHUNK)
                st = st_ref[g, sl, :]
                dst_b = dst.astype(BF16)
                egl = jnp.exp(gls[g])
                dqd_s[g] = _dot(dov[rs], st.astype(BF16))
                dv_s[g] = _dot_nt(ke_b[rs], dst_b)
                dke_s[g] = _dot(v[rs], dst_b)
                dgl_s[g] = jnp.sum(dst * st, axis=0, keepdims=True) * egl
                dst = _dot_tn(dov[rs], qd_b[rs]) + dst * egl
            ds_scr[h] = dst
            dqm = _dot(da, km_b)
            dqd = jnp.concatenate(dqd_s, axis=0)
            dv = _dot_tn(a, dov) + jnp.concatenate(dv_s, axis=0)
            dke = jnp.concatenate(dke_s, axis=0)
            t1 = dke * ke
            db = dqm * qm_b.astype(F32) - dkm * km_b.astype(F32) + dqd * qd_b.astype(F32) - t1
            dgl = _per_chunk([dgl_s[g] + jnp.sum(t1[g * HG_CHUNK:(g + 1) * HG_CHUNK], axis=0, keepdims=True)
                              for g in range(HG_GROUP)])
            dlf = _chunk_cumsum(db, tri_t) + dgl
            df = dlf / f - (dkm * emb + dke * egb)
            d_ref[:, sl] = ((dqm * ebm + dqd * eb) * sq * (1.0 + q_raw * (1.0 - sq))).astype(BF16)
            d_ref[:, HG_WIDTH + h * HG_HEAD:HG_WIDTH + (h + 1) * HG_HEAD] = (
                df * (1.0 - lb) * sg * (1.0 - sg)).astype(BF16)
            d_ref[:, 2 * HG_WIDTH + h * HG_HEAD:2 * HG_WIDTH + (h + 1) * HG_HEAD] = dv.astype(BF16)
            dlb.append(jnp.sum(df * (1.0 - sg), axis=0, keepdims=True))
        _acc(dlb_ref, jnp.concatenate(dlb, axis=1), first)

    rev = lambda j: pl.BlockSpec((rows, HG_WIDTH), lambda c: (ng - 1 - c, j))
    return _call(body, name=name, grid=(ng,),
                 in_specs=[rev(0), rev(1), rev(2), pl.BlockSpec((2, HG_WIDTH), lambda c: (0, 0)),
                           pl.BlockSpec((HG_GROUP, HG_WIDTH, HG_HEAD), lambda c: (ng - 1 - c, 0, 0)), rev(0),
                           pl.BlockSpec(memory_space=pl.ANY)],
                 out_specs=[pl.BlockSpec((rows, 3 * HG_WIDTH), lambda c: (ng - 1 - c, 0)), _vec(HG_WIDTH)],
                 out_shape=[_sds(dproj.shape, BF16), _sds((1, HG_WIDTH))], aliases={6: 0},
                 scratch_shapes=[pltpu.VMEM((nh, HG_HEAD, HG_HEAD), F32)])(
                     proj, proj, proj, lb_logits, states, do, dproj)


def _to_sub(a, dil):
    t, w = a.shape
    return a if dil == 1 else a.reshape(t // dil, dil, w).transpose(1, 0, 2).reshape(t, w)


def _from_sub(a, dil):
    t, w = a.shape
    return a if dil == 1 else a.reshape(dil, t // dil, w).transpose(1, 0, 2).reshape(t, w)


def _att_mask(has_prev, seg):
    def place(v):
        v = v % ATT_BLOCK
        return v if seg == 1 else seg * (v % (ATT_BLOCK // seg)) + v // (ATT_BLOCK // seg)

    row = lax.broadcasted_iota(jnp.int32, (2 * ATT_BLOCK, 2 * ATT_BLOCK), 0)
    col = lax.broadcasted_iota(jnp.int32, (2 * ATT_BLOCK, 2 * ATT_BLOCK), 1)
    qi, kj = place(row), place(col)
    prev = jnp.logical_and(jnp.logical_and(col < ATT_BLOCK, kj >= qi), has_prev)
    cur = jnp.logical_and(col >= ATT_BLOCK, kj <= qi)
    return jnp.logical_or(prev, cur), lax.broadcasted_iota(jnp.int32, (1, 128), 1)


def _get(ref, sl):
    if len(ref.shape) == 2:
        return ref[:, sl]
    v = ref[:, :, sl]
    return v.reshape(ATT_BLOCK, v.shape[2])


def _put(ref, sl, val):
    if len(ref.shape) == 2:
        ref[:, sl] = val
    else:
        ref[:, :, sl] = val.reshape(ref.shape[0], ref.shape[1], val.shape[1])


def _att_spec(nb, dil, seg, width, col, back):
    bps = nb // dil

    def plain(n):
        return jnp.clip(n - back, 0, nb - 1), col

    def segmented(n):
        m = jnp.clip(n - back, 0, nb - 1)
        return 0, m // bps, m % bps, 0, col

    if seg == 1:
        return pl.BlockSpec((ATT_BLOCK, width), plain)
    return pl.BlockSpec((seg, None, None, ATT_BLOCK // seg, width), segmented)


def _att_shape(nb, dil, seg, width):
    t = nb * ATT_BLOCK
    return (t, width) if seg == 1 else (seg, dil, nb // dil, ATT_BLOCK // seg, width)


def _att_view(a, nb, dil, seg):
    return a.reshape(_att_shape(nb, dil, seg, a.shape[1]))


def _attn_fwd_block(q_ref, kc_ref, kp_ref, vc_ref, vp_ref, o_ref, l_ref, has_prev, seg, lane0):
    mask, lane = _att_mask(has_prev, seg)
    lo = lane < 64
    nq = ATT_BLOCK
    lse_all = jnp.zeros((nq, 128), F32)
    for hp in range(ATT_HEADS // 2):
        sl = slice(hp * 128, (hp + 1) * 128)
        q2 = _get(q_ref, sl)
        zero = jnp.zeros_like(q2)
        q2 = q2 * 0.125
        qs = jnp.concatenate([jnp.where(lo, q2, zero), jnp.where(lo, zero, q2)], axis=0)
        kk = jnp.concatenate([_get(kp_ref, sl), _get(kc_ref, sl)], axis=0)
        vv = jnp.concatenate([_get(vp_ref, sl), _get(vc_ref, sl)], axis=0)
        s = jnp.where(mask, _dot_nt(qs, kk), NEG)
        mx = jnp.max(s, axis=-1, keepdims=True)
        p = jnp.exp(s - mx)
        l = jnp.sum(p, axis=-1, keepdims=True)
        o = _dot(p.astype(BF16), vv) * (1.0 / l)
        _put(o_ref, sl, jnp.where(lo, o[:nq], o[nq:]).astype(BF16))
        lse = mx + jnp.log(l)
        lse_all = jnp.where(lane == lane0 + 2 * hp, lse[:nq], lse_all)
        lse_all = jnp.where(lane == lane0 + 2 * hp + 1, lse[nq:], lse_all)
    _put(l_ref, slice(None), lse_all)


def _attn_fwd(branches, name):
    t = branches[0][0].shape[0]
    nb = t // ATT_BLOCK
    nbr = len(branches)

    def body(*refs):
        n = pl.program_id(0)
        for i, (_, dil, seg) in enumerate(branches):
            _attn_fwd_block(*refs[5 * i:5 * i + 5], *refs[5 * nbr + 2 * i:5 * nbr + 2 * i + 2],
                            (n % (nb // dil)) != 0, seg, ATT_HEADS * i)

    in_specs, args, out_specs, out_shape = [], [], [], []
    for qkv, dil, seg in branches:
        c0 = qkv.shape[1] // ATT_WIDTH - 3
        in_specs += [_att_spec(nb, dil, seg, ATT_WIDTH, c0 + j, back) for j, back in [(0, 0), (1, 0), (1, 1), (2, 0), (2, 1)]]
        args += [_att_view(qkv, nb, dil, seg)] * 5
        out_specs += [_att_spec(nb, dil, seg, ATT_WIDTH, 0, 0), _att_spec(nb, dil, seg, 128, 0, 0)]
        out_shape += [_sds(_att_shape(nb, dil, seg, ATT_WIDTH), BF16), _sds(_att_shape(nb, dil, seg, 128))]
    out = _call(body, name=name, grid=(nb,), in_specs=in_specs, out_specs=out_specs, out_shape=out_shape)(*args)
    return [(out[2 * i].reshape(t, ATT_WIDTH), out[2 * i + 1].reshape(t, 128)) for i in range(nbr)]


def _combine_mix_in(os_, ls_, fine, o_hg, proj, og, ag, name, tr=512):
    t = os_[0].shape[0]
    nbr = len(os_)

    def body(*refs):
        o_refs, l_refs = refs[:nbr], refs[nbr:2 * nbr]
        oh_ref, g_ref, og_ref, ag_ref, a_ref, lt_ref, lts_ref, m_ref, to_natural, to_sub = refs[2 * nbr:]

        @pl.when(pl.program_id(0) == 0)
        def _():
            to_natural[...] = _regroup_matrix(tr, fine)
            to_sub[...] = _regroup_matrix(tr, tr // fine)

        lane = lax.broadcasted_iota(jnp.int32, (1, 128), 1)
        packed = l_refs[0][...] + _regroup(to_natural[...], sum(r[...] for r in l_refs[1:]).reshape(tr, 128))
        ls = [packed if i == 0 else pltpu.roll(packed, 128 - ATT_HEADS * i, 1) for i in range(nbr)]
        mx = functools.reduce(jnp.maximum, ls)
        tot = mx + jnp.log(sum(jnp.exp(l - mx) for l in ls))
        ws = [jnp.exp(l - tot) for l in ls]
        tot = jnp.where(lane < ATT_HEADS, tot, 0.0)
        lt_ref[...] = tot
        lts_ref[...] = _regroup(to_sub[...], tot).reshape(lts_ref.shape)
        o_vals = [o_refs[0]] + [_regroup(to_natural[...], r[...].reshape(tr, ATT_WIDTH)) for r in o_refs[1:]]
        pairs = []
        for hp in range(ATT_HEADS // 2):
            sl = slice(hp * 128, (hp + 1) * 128)
            acc = jnp.zeros((tr, 128), F32)
            for w, o in zip(ws, o_vals):
                wf = jnp.where(lane < 64, w[:, 2 * hp:2 * hp + 1], w[:, 2 * hp + 1:2 * hp + 2])
                acc = acc + wf * o[:, sl]
            pairs.append(acc)
        av = jnp.concatenate(pairs, axis=1)
        a_ref[...] = av
        m_ref[:, HG_WIDTH:] = (av * _rms(av) * ag_ref[...]).astype(BF16)
        for h in range(HG_WIDTH // HG_HEAD):
            sl = slice(h * HG_HEAD, (h + 1) * HG_HEAD)
            oh = oh_ref[:, sl].astype(F32)
            gv = g_ref[:, sl].astype(F32)
            m_ref[:, sl] = (oh * _rms(oh) * og_ref[...] * (gv * _sigmoid(gv))).astype(BF16)

    half = _rows(tr, ATT_WIDTH)
    sub = lambda a: a.reshape(fine, t // fine, a.shape[1])
    att, lse, lse_sub, mixin = _call(
        body, name=name, grid=(t // tr,),
        in_specs=[half] + [_sub_rows(tr, fine, ATT_WIDTH)] * (nbr - 1) + [_rows(tr, 128)] + [_sub_rows(tr, fine, 128)] * (nbr - 1)
        + [half, pl.BlockSpec((tr, HG_WIDTH), lambda i: (i, 3)), _vec(HG_HEAD), _vec(ATT_WIDTH)],
        out_specs=[half, _rows(tr, 128), _sub_rows(tr, fine, 128), _rows(tr, D_MODEL)],
        out_shape=[_sds((t, ATT_WIDTH)), _sds((t, 128)), _sds((fine, t // fine, 128)), _sds((t, D_MODEL), BF16)],
        scratch_shapes=[pltpu.VMEM((tr, tr), BF16)] * 2)(
            os_[0], *map(sub, os_[1:]), ls_[0], *map(sub, ls_[1:]), o_hg, proj, og, ag)
    return att, lse, lse_sub.reshape(t, 128), mixin


def _attn_bwd_block(q_ref, k_ref, v_ref, do_ref, l_ref, d_ref, out_ref, carry, prev, has_prev, seg):
    w = ATT_WIDTH
    nq = ATT_BLOCK
    mask, lane = _att_mask(has_prev, seg)
    lo = lane < 64
    lse, ddv = _get(l_ref, slice(None)), _get(d_ref, slice(None))
    for hp in range(ATT_HEADS // 2):
        sl = slice(hp * 128, (hp + 1) * 128)
        sk = slice(w + hp * 128, w + (hp + 1) * 128)
        sv = slice(2 * w + hp * 128, 2 * w + (hp + 1) * 128)
        q2, do2 = _get(q_ref, sl), _get(do_ref, sl)
        zero = jnp.zeros_like(q2)
        q2 = q2 * 0.125
        qs = jnp.concatenate([jnp.where(lo, q2, zero), jnp.where(lo, zero, q2)], axis=0)
        dos = jnp.concatenate([jnp.where(lo, do2, zero), jnp.where(lo, zero, do2)], axis=0)
        kc, vc = _get(k_ref, sl), _get(v_ref, sl)
        kk = jnp.concatenate([prev[:, sl], kc], axis=0)
        vv = jnp.concatenate([prev[:, sk], vc], axis=0)
        prev[:, sl] = kc
        prev[:, sk] = vc
        ls = jnp.concatenate([lse[:, 2 * hp:2 * hp + 1], lse[:, 2 * hp + 1:2 * hp + 2]], axis=0)
        dh = jnp.concatenate([ddv[:, 2 * hp:2 * hp + 1], ddv[:, 2 * hp + 1:2 * hp + 2]], axis=0)
        p = jnp.exp(jnp.where(mask, _dot_nt(qs, kk) - ls, NEG))
        ds = (p * (_dot_nt(dos, vv) - dh)).astype(BF16)
        dq = _dot(ds, kk) * 0.125
        dk = _dot_tn(ds, qs)
        dv = _dot_tn(p.astype(BF16), dos)
        _put(out_ref, sl, carry[:, sl].astype(BF16))
        _put(out_ref, sk, (carry[:, sk] + dk[:nq]).astype(BF16))
        _put(out_ref, sv, (carry[:, sv] + dv[:nq]).astype(BF16))
        carry[:, sl] = jnp.where(lo, dq[:nq], dq[nq:])
        carry[:, sk] = dk[nq:]
        carry[:, sv] = dv[nq:]


def _attn_bwd(branches, name):
    t = branches[0][0].shape[0]
    nb = t // ATT_BLOCK
    nbr = len(branches)
    w = ATT_WIDTH

    def body(*refs):
        ins, outs, carries, prevs = refs[:6 * nbr], refs[6 * nbr:7 * nbr], refs[7 * nbr:8 * nbr], refs[8 * nbr:]
        n = pl.program_id(0)

        @pl.when(n == 0)
        def _():
            for scratch in carries + prevs:
                scratch[...] = jnp.zeros_like(scratch)

        @pl.when(n < nb)
        def _():
            for i, branch in enumerate(branches):
                dil, seg = branch[4:]
                _attn_bwd_block(*ins[6 * i:6 * i + 6], outs[i], carries[i], prevs[i], (n % (nb // dil)) != 0, seg)

        @pl.when(n == nb)
        def _():
            for i in range(nbr):
                _put(outs[i], slice(None), carries[i][...].astype(BF16))

    in_specs, args, out_specs, out_shape = [], [], [], []
    for qkv, dout, lse, dd, dil, seg in branches:
        c0 = qkv.shape[1] // w - 3
        in_specs += [_att_spec(nb, dil, seg, w, c0 + j, 0) for j in range(3)]
        in_specs += [_att_spec(nb, dil, seg, w, 0, 0), _att_spec(nb, dil, seg, 128, 0, 0), _att_spec(nb, dil, seg, 128, 0, 0)]
        args += [_att_view(a, nb, dil, seg) for a in [qkv] * 3 + [dout, lse, dd]]
        out_specs += [_att_spec(nb, dil, seg, 3 * w, 0, 1)]
        out_shape += [_sds(_att_shape(nb, dil, seg, 3 * w), BF16)]
    out = _call(body, name=name, grid=(nb + 1,), in_specs=in_specs, out_specs=out_specs, out_shape=out_shape,
                scratch_shapes=[pltpu.VMEM((ATT_BLOCK, 3 * w), F32)] * nbr + [pltpu.VMEM((ATT_BLOCK, 2 * w), BF16)] * nbr)(*args)
    return [o.reshape(t, 3 * w) for o in out]


def _local_step(x, tgt, mod, norm1_g, lb_logits, og, ag, norm2_g, fg, get_w, put_g, late=lambda a: a, project=None):
    shift1, scale1, gate1, shift2, scale2, gate2 = [mod[:, i * D_MODEL:(i + 1) * D_MODEL] for i in range(6)]
    fg = fg.reshape(1, D_MODEL)

    h1 = _norm_mod(x, norm1_g, scale1, shift1, "norm_mod1")
    if project is None:
        w_in = get_w("w_in", h1)
        proj = _mm_nn(h1, w_in, "mm_in", out_dtype=BF16)
    else:
        proj, w_in = project(h1)
    o_hg, states = _hgrn_fwd(proj, lb_logits, "hgrn_fwd")
    fine = DILATIONS[-1]
    layouts = [(d, 1 if d == 1 else fine // d) for d in DILATIONS]
    qkv_fine = _to_sub(proj, fine)
    qkvs = [proj if d == 1 else qkv_fine for d in DILATIONS]
    natural = lambda a, d: a if d == 1 else _from_sub(a, fine)
    outs = _attn_fwd([(q, d, seg) for q, (d, seg) in zip(qkvs, layouts)], "attn_fwd")
    att, lse, lse_fine, mixin = _combine_mix_in([o for o, _ in outs], [l for _, l in outs], fine,
                                                o_hg, proj, og, ag, "attn_combine_mix_in")
    w_out = get_w("w_out", mixin)
    mix, x2, h2 = _out_resid_norm_mod(x, mixin, w_out, gate1, norm2_g, scale2, shift2, "mm_out_resid_norm_mod2")
    w_gu = get_w("w_gu", h2)
    a_ff, u_ff, act = _mm_gate_up(h2, w_gu, "mm_gu")
    w_down = get_w("w_down", act)
    dx3, dffn, loss_v, dfg, dgate2 = _down_loss(x2, act, w_down, gate2, fg, tgt, "mm_down_loss")

    dffn = put_g("w_down", *_mm_tn(act, dffn, 1, "mm_down_dw", tm=x.shape[0], tk=256), dffn)
    dau = _mm_down_dx(dffn, w_down, a_ff, u_ff, "mm_down_dx")
    dau = put_g("w_gu", *_mm_tn(h2, dau, N_SHARD, "mm_gu_dw", tm=x.shape[0], tk=512), dau)
    dx2, dshift2, dscale2, dg2, dgate1, dmix = _norm_mod_bwd(
        dau, x2, norm2_g, scale2, dx3, "mm_gu_dx_norm_bwd", gate=gate1, mix=mix, w=w_gu)
    dmix = put_g("w_out", *_mm_tn(mixin, dmix, 1, "mm_out_dw", tm=x.shape[0], tk=512), dmix)
    do_hg, dproj, datt, dd, datt_fine, dd_fine, dog, dag = _mix_in_bwd(
        dmix, w_out, o_hg, proj, att, og, ag, fine, "mm_out_dx_mix_in_bwd")
    datts = _attn_bwd([(q,) + ((datt, lse, dd) if d == 1 else (datt_fine, lse_fine, dd_fine)) + (d, seg)
                       for q, (d, seg) in zip(qkvs, layouts)], "attn_bwd")
    dproj, dlb = _hgrn_bwd(proj, lb_logits, states, do_hg, dproj, "hgrn_bwd")
    dproj = late(dproj)
    dproj = _dproj(dproj, [natural(a, d) for a, d in zip(datts, DILATIONS)], "dproj")
    dproj = put_g("w_in", *_mm_tn(h1, dproj, N_SHARD, "mm_in_dw", tm=x.shape[0], tk=512, group=2), dproj)
    dx, dshift1, dscale1, dg1 = _norm_mod_bwd(dproj, x, norm1_g, scale1, dx2, "mm_in_dx_norm_bwd", w=w_in)

    stats = jnp.concatenate([loss_v, dfg, dg2, dg1, dlb, dag, dog,
                             dshift1, dscale1, dgate1, dshift2, dscale2, dgate2], axis=1)
    return dx, stats


def _place():
    x, y, c = lax.axis_index("x"), lax.axis_index("y"), lax.axis_index("c")
    return x, y, c


def _chip_peers(x, y, c):
    return [(1 - x, y, c), (x, 1 - y, c), (1 - x, 1 - y, c)]


_HBM = pl.BlockSpec(memory_space=pltpu.HBM)
_SEM = pl.BlockSpec(memory_space=pltpu.SEMAPHORE)
_EFFECT = pltpu.SideEffectType.DATAFLOW_SIDE_EFFECTING


def _exchange_copy(bufs, send, recv, j, peer, place, kind):
    x, y, c = place
    target = peer
    if kind == "gather":
        src = dst = bufs[0].at[2 * x + y]
    elif kind == "scatter":
        src, dst = bufs[0].at[2 * peer[0] + peer[1]], bufs[1].at[j]
    else:
        half = bufs[0].shape[1] // 2
        rows = pl.ds(c * half, half)
        if kind == "half":
            src = dst = bufs[0].at[2 * x + y, rows]
        else:
            src = dst = bufs[0].at[2 * peer[0] + peer[1], rows]
            target = (x, y, 1 - c)
    return pltpu.make_async_remote_copy(src_ref=src, dst_ref=dst, send_sem=send.at[j], recv_sem=recv.at[j],
                                        device_id=target, device_id_type=MESH)


def _exchange_start(groups, after, kind, name):
    sizes = [len(g) for g in groups]
    flat = [b for g in groups for b in g]
    ng, nb = len(groups), len(flat)

    def body(*refs):
        bufs, sems = refs[:nb], refs[nb + 1:nb + 1 + 2 * ng]
        x, y, c = _place()
        for j, peer in enumerate(_chip_peers(x, y, c)):
            at = 0
            for i, size in enumerate(sizes):
                _exchange_copy(bufs[at:at + size], sems[2 * i], sems[2 * i + 1], j, peer, (x, y, c), kind).start()
                at += size

    any_space = pl.BlockSpec(memory_space=pl.ANY)
    out = pl.pallas_call(
        body, name=name, in_specs=[_HBM] * nb + [any_space],
        out_specs=[_SEM] * (2 * ng) + [_HBM] * nb + [any_space],
        out_shape=[pltpu.SemaphoreType.DMA((3,))] * (2 * ng) + [pltpu.HBM(b.shape, b.dtype) for b in flat]
        + [_sds(after.shape, after.dtype)],
        input_output_aliases={i: 2 * ng + i for i in range(nb + 1)},
        compiler_params=pltpu.CompilerParams(has_side_effects=_EFFECT),
    )(*[pltpu.with_memory_space_constraint(b, pltpu.HBM) for b in flat], after)
    started, at = [], 2 * ng
    for i, size in enumerate(sizes):
        started.append((out[2 * i], out[2 * i + 1], tuple(out[at:at + size])))
        at += size
    return started, out[-1]


def _exchange_wait(started, after, kind, name):
    send, recv, bufs = started
    nb = len(bufs)

    def body(*refs):
        x, y, c = _place()
        for j, peer in enumerate(_chip_peers(x, y, c)):
            cp = _exchange_copy(refs[:nb], refs[nb], refs[nb + 1], j, peer, (x, y, c), kind)
            cp.wait_send()
            cp.wait_recv()

    return pl.pallas_call(
        body, name=name, in_specs=[_HBM] * nb + [_SEM, _SEM, pl.BlockSpec(memory_space=pl.ANY)],
        out_specs=[_HBM] * nb, out_shape=[pltpu.HBM(b.shape, b.dtype) for b in bufs],
        input_output_aliases={i: i for i in range(nb)},
        compiler_params=pltpu.CompilerParams(has_side_effects=_EFFECT),
    )(*bufs, send, recv, after)


def _sibling_copies(v_refs, l_refs, send, recv):
    x, y, c = _place()
    return [pltpu.make_async_remote_copy(src_ref=v, dst_ref=l, send_sem=send.at[a], recv_sem=recv.at[a],
                                         device_id=(x, y, 1 - c), device_id_type=MESH)
            for a, (v, l) in enumerate(zip(v_refs, l_refs))]


def _sibling_start(vs, after, name):
    vs = list(vs)
    n = len(vs)
    lands = [lax.empty(v.shape, v.dtype) for v in vs]

    def body(*refs):
        for cp in _sibling_copies(refs[:n], refs[n:2 * n], refs[2 * n + 1], refs[2 * n + 2]):
            cp.start()

    any_space = pl.BlockSpec(memory_space=pl.ANY)
    out = pl.pallas_call(
        body, name=name, in_specs=[_HBM] * (2 * n) + [any_space],
        out_specs=[_SEM, _SEM] + [_HBM] * (2 * n) + [any_space],
        out_shape=[pltpu.SemaphoreType.DMA((n,))] * 2 + [pltpu.HBM(b.shape, b.dtype) for b in vs + lands]
        + [_sds(after.shape, after.dtype)],
        input_output_aliases={i: 2 + i for i in range(2 * n + 1)},
        compiler_params=pltpu.CompilerParams(has_side_effects=_EFFECT),
    )(*[pltpu.with_memory_space_constraint(b, pltpu.HBM) for b in vs + lands], after)
    return (out[0], out[1], tuple(out[2:2 + n]), tuple(out[2 + n:2 + 2 * n])), out[-1]


def _sibling_wait(started, after, name):
    send, recv, vs, lands = started
    n = len(vs)

    def body(*refs):
        for cp in _sibling_copies(refs[:n], refs[n:2 * n], refs[2 * n], refs[2 * n + 1]):
            cp.wait_send()
            cp.wait_recv()

    out = pl.pallas_call(
        body, name=name, in_specs=[_HBM] * (2 * n) + [_SEM, _SEM, pl.BlockSpec(memory_space=pl.ANY)],
        out_specs=[_HBM] * (2 * n), out_shape=[pltpu.HBM(b.shape, b.dtype) for b in vs + lands],
        input_output_aliases={i: i for i in range(2 * n)},
        compiler_params=pltpu.CompilerParams(has_side_effects=_EFFECT),
    )(*vs, *lands, send, recv, after)
    return out[:n], out[n:]


def _everyone(x, y, c):
    return [(1 - x if k & 4 else x, 1 - y if k & 2 else y, 1 - c if k & 1 else c) for k in range(1, 8)]


def _all_gather_copies(land_ref, send, recv, arriving):
    x, y, c = _place()
    me = 4 * x + 2 * y + c
    return [pltpu.make_async_remote_copy(
        src_ref=land_ref.at[me], dst_ref=land_ref.at[4 * p[0] + 2 * p[1] + p[2] if arriving else me],
        send_sem=send.at[k], recv_sem=recv.at[k], device_id=p, device_id_type=MESH)
        for k, p in enumerate(_everyone(x, y, c))]


def _all_gather_start(v, name):
    x, y, c = _place()
    land = lax.dynamic_update_slice(lax.empty((8,) + v.shape, v.dtype), v[None], (4 * x + 2 * y + c, 0, 0))

    def body(land_ref, v_ref, send, recv, land_out, v_out):
        for cp in _all_gather_copies(land_ref, send, recv, False):
            cp.start()

    any_space = pl.BlockSpec(memory_space=pl.ANY)
    out = pl.pallas_call(
        body, name=name, in_specs=[_HBM, any_space], out_specs=[_SEM, _SEM, _HBM, any_space],
        out_shape=[pltpu.SemaphoreType.DMA((7,))] * 2 + [pltpu.HBM(land.shape, land.dtype), _sds(v.shape, v.dtype)],
        input_output_aliases={0: 2, 1: 3}, compiler_params=pltpu.CompilerParams(has_side_effects=_EFFECT),
    )(pltpu.with_memory_space_constraint(land, pltpu.HBM), v)
    return tuple(out[:3]), out[3]


def _all_gather_wait(started, after, name):
    send, recv, land = started

    def body(land_ref, send, recv, after_ref, land_out):
        for cp in _all_gather_copies(land_ref, send, recv, True):
            cp.wait_send()
            cp.wait_recv()

    return pl.pallas_call(
        body, name=name, in_specs=[_HBM, _SEM, _SEM, pl.BlockSpec(memory_space=pl.ANY)], out_specs=_HBM,
        out_shape=pltpu.HBM(land.shape, land.dtype), input_output_aliases={0: 0},
        compiler_params=pltpu.CompilerParams(has_side_effects=_EFFECT),
    )(land, send, recv, after)


def _cast_place(ws, shard, name, after=()):
    n = len(ws)

    def body(s_ref, *refs):
        for w_ref, o_ref in zip(refs[:n], refs[-n:]):
            o_ref[0] = w_ref[...].astype(BF16)

    return pl.pallas_call(
        body, name=name, out_shape=[_sds((N_SHARD,) + w.shape, BF16) for w in ws],
        grid_spec=pltpu.PrefetchScalarGridSpec(
            num_scalar_prefetch=1, grid=(4,),
            in_specs=[pl.BlockSpec((w.shape[0] // 4, w.shape[1]), lambda i, s: (i, 0)) for w in ws]
            + [pl.BlockSpec(memory_space=pl.ANY)] * len(after),
            out_specs=[pl.BlockSpec((1, w.shape[0] // 4, w.shape[1]), lambda i, s: (s[0], i, 0)) for w in ws]),
        compiler_params=pltpu.CompilerParams(dimension_semantics=("arbitrary",), vmem_limit_bytes=VMEM_LIMIT),
    )(shard.reshape(1).astype(jnp.int32), *ws, *after)


def _mod_rows(c8, w_ada, b_ada, name):
    n = w_ada.shape[1]

    def gather(src_ref, dst_ref, send, recv, loc, base):
        x, y, c = _place()
        me = 4 * x + 2 * y + c
        own = pltpu.make_async_copy(src_ref, dst_ref.at[me], loc)
        own.start()
        peers = _everyone(x, y, c)
        sends = [pltpu.make_async_remote_copy(src_ref=src_ref, dst_ref=dst_ref.at[me], send_sem=send.at[base + k],
                                              recv_sem=recv.at[base + k], device_id=p, device_id_type=MESH)
                 for k, p in enumerate(peers)]
        for cp in sends:
            cp.start()
        for k, p in enumerate(peers):
            pltpu.make_async_remote_copy(src_ref=src_ref, dst_ref=dst_ref.at[4 * p[0] + 2 * p[1] + p[2]],
                                         send_sem=send.at[base + k], recv_sem=recv.at[base + k], device_id=p,
                                         device_id_type=MESH).wait_recv()
        for cp in sends:
            cp.wait_send()
        own.wait()

    def body(c_ref, w_ref, b_ref, a_ref, parts_ref, c_all, part, send, recv, loc):
        gather(c_ref, c_all, send, recv, loc.at[0], 0)
        cv = jnp.max(c_all[...], axis=1)
        ca = cv * _sigmoid(cv)
        a_ref[...] = ca
        part[...] = jnp.dot(ca, w_ref[...], precision=lax.Precision.HIGHEST, preferred_element_type=F32) + b_ref[...]
        gather(part, parts_ref, send, recv, loc.at[1], 7)

    vmem = pl.BlockSpec(memory_space=pltpu.VMEM)
    return pl.pallas_call(
        body, name=name, in_specs=[vmem] * 3, out_specs=[vmem, vmem],
        out_shape=[_sds((8, D_MODEL)), _sds((8, 8, n))],
        scratch_shapes=[pltpu.VMEM((8, 8, D_MODEL), F32), pltpu.VMEM((8, n), F32), pltpu.SemaphoreType.DMA((14,)),
                        pltpu.SemaphoreType.DMA((14,)), pltpu.SemaphoreType.DMA((2,))],
        compiler_params=pltpu.CompilerParams(vmem_limit_bytes=VMEM_LIMIT))(c8, w_ada, b_ada)


def _sum_received(gs, shard, lands, name):
    n = len(gs)

    def body(s_ref, *refs):
        for g_ref, l_ref, o_ref in zip(refs[:n], refs[n:2 * n], refs[2 * n:]):
            o_ref[...] = ((g_ref[0] + l_ref[0].astype(F32)) + l_ref[1].astype(F32)) + l_ref[2].astype(F32)

    quarter = lambda g: (g.shape[1] // 4, g.shape[2])
    return pl.pallas_call(
        body, name=name, out_shape=[_sds(g.shape[1:]) for g in gs],
        grid_spec=pltpu.PrefetchScalarGridSpec(
            num_scalar_prefetch=1, grid=(4,),
            in_specs=[pl.BlockSpec((1,) + quarter(g), lambda i, s: (s[0], i, 0)) for g in gs]
            + [pl.BlockSpec((3,) + quarter(g), lambda i, s: (0, i, 0)) for g in gs],
            out_specs=[pl.BlockSpec(quarter(g), lambda i, s: (i, 0)) for g in gs]),
        compiler_params=pltpu.CompilerParams(dimension_semantics=("arbitrary",), vmem_limit_bytes=VMEM_LIMIT),
    )(shard.reshape(1).astype(jnp.int32), *gs, *lands)


def _adamw_outer(w, ct, dm, m, v, name):
    k, n = w.shape
    tr = k // 4

    def body(w_ref, c_ref, d_ref, m_ref, v_ref, g_out, d_out, m_out, v_out):
        cv = c_ref[...]
        dv = d_ref[...]
        g = cv[:, 0:1] * dv[0:1, :]
        for i in range(1, 8):
            g = g + cv[:, i:i + 1] * dv[i:i + 1, :]
        g_out[...] = g
        d_out[...], m_out[...], v_out[...] = _adamw_math(w_ref[...], g, m_ref[...], v_ref[...])

    row = _rows(tr, n)
    return _call(body, name=name, grid=(4,),
                 in_specs=[row, _rows(tr, 8), pl.BlockSpec((8, n), lambda i: (0, 0)), row, row],
                 out_specs=[row] * 4, out_shape=[_sds((k, n))] * 4)(w, ct, dm, m, v)


def _adamw_math(w, g, m, v):
    m_new = ADAM_B1 * m + (1.0 - ADAM_B1) * g
    v_new = ADAM_B2 * v + (1.0 - ADAM_B2) * (g * g)
    m_hat = m_new / (1.0 - ADAM_B1 ** ADAM_STEP)
    v_hat = v_new / (1.0 - ADAM_B2 ** ADAM_STEP)
    return -ADAM_LR * (m_hat / (jnp.sqrt(v_hat) + ADAM_EPS) + ADAM_WD * w), m_new, v_new


def _small_update(stats, smalls, name):
    offsets = [ST_DMOD, ST_DG1, ST_DLB, ST_DOG, ST_DAG, ST_DG2, ST_DFG]
    lb_index = 2

    def body(*refs):
        s_ref, ins, l_ref, outs = refs[0], refs[1:22], refs[22], refs[23:]
        tot = s_ref[0:1, :]
        for i in range(1, 8):
            tot = tot + s_ref[i:i + 1, :]
        l_ref[...] = jnp.zeros((1, 128), F32) + (0.5 / D_MODEL) * jnp.sum(tot[:, ST_LOSS:ST_LOSS + D_MODEL])
        for p, off in enumerate(offsets):
            w_ref, m_ref, v_ref = ins[3 * p:3 * p + 3]
            g_out, d_out, m_out, v_out = outs[4 * p:4 * p + 4]
            g = tot[:, off:off + w_ref.shape[1]]
            if p == lb_index:
                lg = w_ref[...]
                lb = _sigmoid(lg[0:1] - lg[1:2])
                g = g * lb * (1.0 - lb)
            for r in range(w_ref.shape[0]):
                rows = slice(r, r + 1)
                gr = g if r == 0 else -g
                delta, m_new, v_new = _adamw_math(w_ref[rows, :], gr, m_ref[rows, :], v_ref[rows, :])
                g_out[rows, :] = gr
                d_out[rows, :] = delta
                m_out[rows, :] = m_new
                v_out[rows, :] = v_new

    full = lambda a: pl.BlockSpec(a.shape, lambda i: (0, 0))
    flat = [a for t in smalls for a in t]
    return _call(body, name=name, grid=(1,),
                 in_specs=[full(stats)] + [full(a) for a in flat],
                 out_specs=[pl.BlockSpec((1, 128), lambda i: (0, 0))] + [full(t[0]) for t in smalls for _ in range(4)],
                 out_shape=[_sds((1, 128))] + [_sds(t[0].shape) for t in smalls for _ in range(4)])(stats, *flat)


def _adamw(params, name):
    n = len(params)

    def body(*refs):
        for p in range(n):
            w_ref, ga_ref, gb_ref, m_ref, v_ref = refs[5 * p:5 * p + 5]
            g_out, d_out, m_out, v_out = refs[5 * n + 4 * p:5 * n + 4 * p + 4]
            g = ga_ref[...] + gb_ref[...]
            g_out[...] = g
            d_out[...], m_out[...], v_out[...] = _adamw_math(w_ref[...], g, m_ref[...], v_ref[...])

    row = lambda w: _rows(w.shape[0] // 4, w.shape[1])
    out = _call(body, name=name, grid=(4,), in_specs=[row(p[0]) for p in params for _ in range(5)],
                out_specs=[row(p[0]) for p in params for _ in range(4)],
                out_shape=[_sds(p[0].shape) for p in params for _ in range(4)])(*[a for p in params for a in p])
    return [tuple(out[4 * p:4 * p + 4]) for p in range(n)]


def kernel(x, c, w_ada, b_ada, norm1_g, w_in, hg_lb_logits, hg_onorm_g, att_onorm_g, w_out, norm2_g, w_gate_up, w_down, final_g, loss_target, m_w_ada, m_b_ada, m_norm1_g, m_w_in, m_hg_lb_logits, m_hg_onorm_g, m_att_onorm_g, m_w_out, m_norm2_g, m_w_gate_up, m_w_down, m_final_g, v_w_ada, v_b_ada, v_norm1_g, v_w_in, v_hg_lb_logits, v_hg_onorm_g, v_att_onorm_g, v_w_out, v_norm2_g, v_w_gate_up, v_w_down, v_final_g):
    ix, iy, ic = _place()
    shard = 2 * ix + iy
    sample = 4 * ix + 2 * iy + ic
    n_ada = w_ada.shape[2]

    shards = [w_in[0], w_out[0], w_gate_up[0], w_down[0]]
    names = ["w_in", "w_out", "w_gu", "w_down"]
    shapes = [(N_SHARD,) + w.shape for w in shards]
    placed = [(_cast_place(shards[:1], shard, "place_w_in")[0],)]

    b_part = lax.dynamic_slice(b_ada, (0, shard * n_ada), (1, n_ada))
    c_act, parts = _mod_rows(jnp.broadcast_to(c, (8, D_MODEL)), w_ada[0], b_part, "mod_rows")
    parts = parts[::2]
    mod = lax.dynamic_index_in_dim(parts, sample, axis=1, keepdims=False).reshape(1, 6 * D_MODEL)
    (first,), mod = _exchange_start(placed[:1], mod, "half", "gather_start_w_in")
    gathering = {}

    def get_w(name, after):
        if name == "w_in":
            placed_rest = [(p,) for p in _cast_place(shards[1:], shard, "place_rest", (after,))]
            halves = _exchange_wait(first, placed_rest[0][0], "half", "gather_wait_w_in")
            (passing,), token = _exchange_start([tuple(halves)], mod, "forward", "forward_start_w_in")
            rest, token = _exchange_start(placed_rest, token, "gather", "gather_start_rest")
            (full,) = _exchange_wait(passing, token, "forward", "forward_wait_w_in")
            gathering.update(zip(names[1:], rest))
            return full
        (full,) = _exchange_wait(gathering[name], after, "gather", "gather_wait_" + name)
        return full if name == "w_gu" else full.reshape(1, -1, D_MODEL)

    scattering = {}

    def put_g(name, g, g_bf16, then):
        shape = shapes[names.index(name)]
        land = lax.empty((3,) + shape[1:], BF16)
        (started,), then = _exchange_start([(g_bf16.reshape(shape), land)], then, "scatter", "scatter_start_" + name)
        scattering[name] = (g.reshape(shape), started)
        return then

    def summed(group, after, tag):
        lands = [_exchange_wait(scattering[nm][1], after, "scatter", "scatter_wait_" + nm)[1] for nm in group]
        return _sum_received([scattering[nm][0] for nm in group], shard, lands, "sum_" + tag)

    early = ["w_down", "w_gu", "w_out"]
    swapping = []

    def late(a):
        started, a = _sibling_start(summed(early, a, "early"), a, "swap_start")
        swapping.append(started)
        return a

    def project(h1):
        own = _mm_own_shard(h1, shards[0], shard, N_SHARD, "mm_in_own")
        w_full = get_w("w_in", own)
        return _mm_other_shards(h1, w_full, own, shard, "mm_in_rest"), w_full

    dx, stats = _local_step(x[0], loss_target[0], mod, norm1_g, hg_lb_logits, hg_onorm_g, att_onorm_g,
                            norm2_g, final_g, get_w, put_g, late, project)

    gathering_stats, stats = _all_gather_start(stats, "stats_start")
    moments = [(m_w_in, v_w_in), (m_w_out, v_w_out), (m_w_gate_up, v_w_gate_up), (m_w_down, v_w_down)]

    def update(group, sums, other, tag):
        params = [(shards[names.index(nm)], s, o, moments[names.index(nm)][0][0], moments[names.index(nm)][1][0])
                  for nm, s, o in zip(group, sums, other)]
        return dict(zip(group, _adamw(params, "adamw_" + tag)))

    sums, other = _sibling_wait(swapping[0], stats, "swap_wait")
    done = update(early, sums, other, "early")
    swapping_in, stats = _sibling_start(summed(["w_in"], done["w_out"][1], "w_in"), stats, "swap_start_w_in")

    stats_all = _all_gather_wait(gathering_stats, stats, "stats_wait").reshape(8, ST_WIDTH)
    dmod = lax.dynamic_slice(stats_all, (0, ST_DMOD + shard * n_ada), (8, n_ada))

    as_row = lambda a: a.reshape(1, -1) if a.ndim == 1 else a
    smalls = [tuple(as_row(a) for a in t) for t in [
        (b_ada, m_b_ada, v_b_ada), (norm1_g, m_norm1_g, v_norm1_g),
        (hg_lb_logits, m_hg_lb_logits, v_hg_lb_logits), (hg_onorm_g, m_hg_onorm_g, v_hg_onorm_g),
        (att_onorm_g, m_att_onorm_g, v_att_onorm_g), (norm2_g, m_norm2_g, v_norm2_g),
        (final_g, m_final_g, v_final_g)]]
    loss, *small_out = _small_update(stats_all, smalls, "small_update")
    shapes_out = [b_ada.shape, norm1_g.shape, hg_lb_logits.shape, hg_onorm_g.shape, att_onorm_g.shape,
                  norm2_g.shape, final_g.shape]
    sg, sd, sm, sv = [[small_out[4 * p + i].reshape(shapes_out[p]) for p in range(7)] for i in range(4)]

    ada = _adamw_outer(w_ada[0], c_act.T, dmod, m_w_ada[0], v_w_ada[0], "adamw_w_ada")
    sum_in, other_in = _sibling_wait(swapping_in, ada[1], "swap_wait_w_in")
    done.update(update(["w_in"], sum_in, other_in, "w_in"))
    big = [ada] + [done[nm] for nm in names]
    bg, bd, bm, bv = [[t[i][None] for t in big] for i in range(4)]

    def order(b, s):
        return [b[0], s[0], s[1], b[1], s[2], s[3], s[4], b[2], s[5], b[3], b[4], s[6]]

    return (loss[0, 0], dx[None], *order(bg, sg), *order(bd, sd), *order(bm, sm), *order(bv, sv))
```

```python
import functools

import jax
import jax.numpy as jnp
from jax import lax
from jax.experimental import pallas as pl
from jax.experimental.pallas import tpu as pltpu

F32 = jnp.float32
BF16 = jnp.bfloat16
MESH = pl.DeviceIdType.MESH

D_MODEL = 1024
HG_WIDTH = 512
HG_HEAD = 128
HG_CHUNK = 64
HG_GROUP = 4
ATT_WIDTH = 512
ATT_HEADS = 8
ATT_BLOCK = 128
DILATIONS = (1, 4, 16)
D_FF = 2816
IN_WIDTH = 3584
N_SHARD = 4
RMS_EPS = 1e-6
NEG = -1e30

ADAM_LR = 0.001
ADAM_B1 = 0.9
ADAM_B2 = 0.999
ADAM_EPS = 1e-08
ADAM_WD = 0.01
ADAM_STEP = 10

VMEM_LIMIT = 56 * 2**20

ST_LOSS, ST_DFG, ST_DG2, ST_DG1 = 0, 1024, 2048, 3072
ST_DLB, ST_DAG, ST_DOG, ST_DMOD = 4096, 4608, 5120, 5248
ST_WIDTH = 5248 + 6144


def _call(body, *, name, grid, in_specs, out_specs, out_shape, scratch_shapes=(), aliases=None):
    return pl.pallas_call(
        body, name=name, grid=grid, in_specs=in_specs, out_specs=out_specs, out_shape=out_shape,
        scratch_shapes=list(scratch_shapes), input_output_aliases=aliases or {},
        compiler_params=pltpu.CompilerParams(
            dimension_semantics=("arbitrary",) * len(grid), vmem_limit_bytes=VMEM_LIMIT))


def _sds(shape, dtype=F32):
    return jax.ShapeDtypeStruct(shape, dtype)


def _dot(a, b):
    return jnp.dot(a, b, preferred_element_type=F32)


def _dot_nt(a, b):
    return lax.dot_general(a, b, (((1,), (1,)), ((), ())), preferred_element_type=F32)


def _dot_tn(a, b):
    return lax.dot_general(a, b, (((0,), (0,)), ((), ())), preferred_element_type=F32)


def _sigmoid(x):
    return 1.0 / (1.0 + jnp.exp(-x))


def _rows(tr, width):
    return pl.BlockSpec((tr, width), lambda i: (i, 0))


def _vec(width):
    return pl.BlockSpec((1, width), lambda i: (0, 0))


def _acc(ref, val, first):
    @pl.when(first)
    def _():
        ref[...] = val

    @pl.when(jnp.logical_not(first))
    def _():
        ref[...] += val


def _sub_rows(tr, fine, width):
    return pl.BlockSpec((fine, tr // fine, width), lambda i: (0, i, 0))


def _regroup_matrix(tr, groups):
    a = lax.broadcasted_iota(jnp.int32, (tr, tr), 0)
    b = lax.broadcasted_iota(jnp.int32, (tr, tr), 1)
    return (b == (a % groups) * (tr // groups) + a // groups).astype(BF16)


def _regroup(m, v, lanes=None):
    if v.dtype == BF16:
        return _dot(m, v)
    width = v.shape[1]
    packed, out = None, None
    for i in range(3):
        part = v.astype(BF16).astype(F32)
        v = v - part
        if lanes is None:
            out = _dot(m, part.astype(BF16)) if i == 0 else out + _dot(m, part.astype(BF16))
        else:
            packed = part if i == 0 else packed + pltpu.roll(part, i * lanes, 1)
    if lanes is None:
        return out
    out = _dot(m, packed.astype(BF16))
    out = out + pltpu.roll(out, width - lanes, 1) + pltpu.roll(out, width - 2 * lanes, 1)
    return jnp.where(lax.broadcasted_iota(jnp.int32, (1, width), 1) < lanes, out, 0.0)


def _mm_nn(a, b3, name, tm=1024, out_dtype=F32):
    m, k = a.shape
    s, _, n = b3.shape

    def body(a_ref, b_ref, o_ref):
        o_ref[...] = _dot(a_ref[...], b_ref[0]).astype(out_dtype)

    return _call(
        body, name=name, grid=(s, m // tm),
        in_specs=[pl.BlockSpec((tm, k), lambda j, i: (i, 0)), pl.BlockSpec((1, k, n), lambda j, i: (j, 0, 0))],
        out_specs=pl.BlockSpec((tm, n), lambda j, i: (i, j)), out_shape=_sds((m, s * n), out_dtype))(a, b3)


def _mm_own_shard(a, w, shard, s, name, tm=1024):
    m, k = a.shape
    n = w.shape[1]

    def body(s_ref, a_ref, w_ref, o_ref):
        o_ref[...] = _dot(a_ref[...], w_ref[...].astype(BF16)).astype(BF16)

    return pl.pallas_call(
        body, name=name, out_shape=_sds((m, s * n), BF16),
        grid_spec=pltpu.PrefetchScalarGridSpec(
            num_scalar_prefetch=1, grid=(m // tm,),
            in_specs=[pl.BlockSpec((tm, k), lambda i, sh: (i, 0)), pl.BlockSpec((k, n), lambda i, sh: (0, 0))],
            out_specs=pl.BlockSpec((tm, n), lambda i, sh: (i, sh[0]))),
        compiler_params=pltpu.CompilerParams(dimension_semantics=("arbitrary",), vmem_limit_bytes=VMEM_LIMIT),
    )(shard.reshape(1).astype(jnp.int32), a, w)


def _mm_other_shards(a, b3, partial, shard, name, tm=1024):
    m, k = a.shape
    s, _, n = b3.shape
    which = lambda j, sh: (sh[0] + 1 + j) % s

    def body(s_ref, a_ref, b_ref, p_ref, o_ref):
        o_ref[...] = _dot(a_ref[...], b_ref[0]).astype(BF16)

    return pl.pallas_call(
        body, name=name, out_shape=_sds(partial.shape, BF16),
        grid_spec=pltpu.PrefetchScalarGridSpec(
            num_scalar_prefetch=1, grid=(s - 1, m // tm),
            in_specs=[pl.BlockSpec((tm, k), lambda j, i, sh: (i, 0)),
                      pl.BlockSpec((1, k, n), lambda j, i, sh: (which(j, sh), 0, 0)),
                      pl.BlockSpec(memory_space=pl.ANY)],
            out_specs=pl.BlockSpec((tm, n), lambda j, i, sh: (i, which(j, sh)))),
        input_output_aliases={3: 0},
        compiler_params=pltpu.CompilerParams(dimension_semantics=("arbitrary",) * 2, vmem_limit_bytes=VMEM_LIMIT),
    )(shard.reshape(1).astype(jnp.int32), a, b3, partial)


def _mm_tn(a, dy, s, name, tm, tk, group=1):
    m, k = a.shape
    n = dy.shape[1] // s
    steps = m // tm

    def body(a_ref, dy_ref, o_ref, ob_ref):
        p = _dot_tn(a_ref[...], dy_ref[...])
        for g in range(group):
            pg = p[:, g * n:(g + 1) * n]
            if steps == 1:
                o_ref[g] = pg
                ob_ref[g] = pg.astype(BF16)
            else:
                _acc(o_ref.at[g], pg, pl.program_id(2) == 0)
        if steps > 1:
            @pl.when(pl.program_id(2) == steps - 1)
            def _():
                ob_ref[...] = o_ref[...].astype(BF16)

    out = pl.BlockSpec((group, tk, n), lambda kk, j, i: (j, kk, 0))
    return _call(
        body, name=name, grid=(k // tk, s // group, steps),
        in_specs=[pl.BlockSpec((tm, tk), lambda kk, j, i: (i, kk)),
                  pl.BlockSpec((tm, group * n), lambda kk, j, i: (i, j))],
        out_specs=[out, out], out_shape=[_sds((s, k, n)), _sds((s, k, n), BF16)])(a, dy)


def _rms(x):
    return lax.rsqrt(jnp.mean(x * x, axis=-1, keepdims=True) + RMS_EPS)


def _rms_bwd(dxh, xh, r):
    return r * (dxh - xh * jnp.mean(dxh * xh, axis=-1, keepdims=True))


def _norm_mod(x, g, scale, shift, name, tr=512):
    t = x.shape[0]

    def body(x_ref, g_ref, sc_ref, sh_ref, h_ref):
        xv = x_ref[...]
        n = xv * _rms(xv) * g_ref[...]
        h_ref[...] = (n * (1.0 + sc_ref[...]) + sh_ref[...]).astype(BF16)

    return _call(body, name=name, grid=(t // tr,),
                 in_specs=[_rows(tr, D_MODEL), _vec(D_MODEL), _vec(D_MODEL), _vec(D_MODEL)],
                 out_specs=_rows(tr, D_MODEL), out_shape=_sds((t, D_MODEL), BF16))(x, g, scale, shift)


def _out_resid_norm_mod(x, mixin, w_out, gate, g, scale, shift, name, tr=512):
    t = x.shape[0]

    def body(x_ref, mi_ref, w_ref, gt_ref, g_ref, sc_ref, sh_ref, m_ref, x2_ref, h_ref):
        mix = _dot(mi_ref[...], w_ref[0])
        m_ref[...] = mix
        x2 = x_ref[...] + gt_ref[...] * mix
        x2_ref[...] = x2
        n = x2 * _rms(x2) * g_ref[...]
        h_ref[...] = (n * (1.0 + sc_ref[...]) + sh_ref[...]).astype(BF16)

    row = _rows(tr, D_MODEL)
    return _call(body, name=name, grid=(t // tr,),
                 in_specs=[row, row, pl.BlockSpec(w_out.shape, lambda i: (0, 0, 0))] + [_vec(D_MODEL)] * 4,
                 out_specs=[row, row, row],
                 out_shape=[_sds((t, D_MODEL)), _sds((t, D_MODEL)), _sds((t, D_MODEL), BF16)])(
                     x, mixin, w_out, gate, g, scale, shift)


def _mm_gate_up(h, w_gu, name, tm=1024):
    m, k = h.shape
    n = w_gu.shape[2]

    def body(h_ref, wa_ref, wu_ref, da_ref, du_ref, o_ref, w_au):
        @pl.when(pl.program_id(1) == 0)
        def _():
            w_au[:, :n] = wa_ref[0]
            w_au[:, n:] = wu_ref[0]

        au = _dot(h_ref[...], w_au[...])
        a, u = au[:, :n], au[:, n:]
        sg = _sigmoid(a)
        silu = a * sg
        da_ref[...] = (u * sg * (1.0 + a * (1.0 - sg))).astype(BF16)
        du_ref[...] = silu.astype(BF16)
        o_ref[...] = (silu * u).astype(BF16)

    out = pl.BlockSpec((tm, n), lambda j, i: (i, j))
    return _call(body, name=name, grid=(2, m // tm),
                 in_specs=[pl.BlockSpec((tm, k), lambda j, i: (i, 0)), pl.BlockSpec((1, k, n), lambda j, i: (j, 0, 0)),
                           pl.BlockSpec((1, k, n), lambda j, i: (j + 2, 0, 0))],
                 out_specs=[out, out, out], out_shape=[_sds((m, 2 * n), BF16)] * 3,
                 scratch_shapes=[pltpu.VMEM((k, 2 * n), BF16)])(h, w_gu, w_gu)


def _mm_down_dx(dffn, w_down, act_da, act_du, name, tm=512):
    m = dffn.shape[0]
    _, k, n = w_down.shape

    def body(d_ref, w_ref, da_ref, du_ref, o_ref):
        dact = _dot_nt(d_ref[...], w_ref[0])
        o_ref[:, :k] = (dact * da_ref[...].astype(F32)).astype(BF16)
        o_ref[:, k:] = (dact * du_ref[...].astype(F32)).astype(BF16)

    return _call(body, name=name, grid=(m // tm,),
                 in_specs=[_rows(tm, n), pl.BlockSpec((1, k, n), lambda i: (0, 0, 0)), _rows(tm, k), _rows(tm, k)],
                 out_specs=_rows(tm, 2 * k), out_shape=_sds((m, 2 * k), BF16))(dffn, w_down, act_da, act_du)


def _down_loss(x2, act, w_down, gate, fg, tgt, name, tr=512):
    t = x2.shape[0]
    _, k, n = w_down.shape

    def body(x_ref, a_ref, w_ref, gt_ref, fg_ref, t_ref, dx_ref, df_ref, l_ref, dfg_ref, dgt_ref):
        first = pl.program_id(0) == 0
        ffn_v = _dot(a_ref[...], w_ref[0])
        x3 = x_ref[...] + gt_ref[...] * ffn_v
        r = _rms(x3)
        xh = x3 * r
        err = xh * fg_ref[...] - t_ref[...]
        dy = err * (1.0 / D_MODEL)
        dx3 = _rms_bwd(dy * fg_ref[...], xh, r)
        dx_ref[...] = dx3
        df_ref[...] = (dx3 * gt_ref[...]).astype(BF16)
        _acc(l_ref, jnp.sum(err * err, axis=0, keepdims=True), first)
        _acc(dfg_ref, jnp.sum(dy * xh, axis=0, keepdims=True), first)
        _acc(dgt_ref, jnp.sum(dx3 * ffn_v, axis=0, keepdims=True), first)

    row, vec = _rows(tr, D_MODEL), _vec(D_MODEL)
    return _call(body, name=name, grid=(t // tr,),
                 in_specs=[row, _rows(tr, k), pl.BlockSpec((1, k, n), lambda i: (0, 0, 0)), vec, vec, row],
                 out_specs=[row, row, vec, vec, vec],
                 out_shape=[_sds((t, D_MODEL)), _sds((t, D_MODEL), BF16)] + [_sds((1, D_MODEL))] * 3)(
                     x2, act, w_down, gate, fg, tgt)


def _norm_mod_bwd(dh, x, g, scale, dres, name, gate=None, mix=None, w=None, tr=512):
    t = x.shape[0]
    below = gate is not None

    def body(*refs):
        if w is not None:
            w_ref, w_full, sem, refs = refs[1], refs[-2], refs[-1], refs[:1] + refs[2:-2]

            @pl.when(pl.program_id(0) == 0)
            def _():
                n = w.shape[2]
                copies = [pltpu.make_async_copy(w_ref.at[j], w_full.at[:, pl.ds(j * n, n)], sem.at[j])
                          for j in range(w.shape[0])]
                for cp in copies:
                    cp.start()
                for cp in copies:
                    cp.wait()

        if below:
            dh_ref, x_ref, g_ref, sc_ref, dr_ref, gt_ref, m_ref, dx_ref, dsh_ref, dsc_ref, dg_ref, dgt_ref, dm_ref = refs
        else:
            dh_ref, x_ref, g_ref, sc_ref, dr_ref, dx_ref, dsh_ref, dsc_ref, dg_ref = refs
        first = pl.program_id(0) == 0
        xv = x_ref[...]
        if w is None:
            dhv = dh_ref[...].astype(F32)
        else:
            dhv = _dot_nt(dh_ref[...], w_full[...])
        r = _rms(xv)
        xh = xv * r
        dn = dhv * (1.0 + sc_ref[...])
        dx = dr_ref[...] + _rms_bwd(dn * g_ref[...], xh, r)
        dx_ref[...] = dx
        _acc(dsh_ref, jnp.sum(dhv, axis=0, keepdims=True), first)
        _acc(dsc_ref, jnp.sum(dhv * xh * g_ref[...], axis=0, keepdims=True), first)
        _acc(dg_ref, jnp.sum(dn * xh, axis=0, keepdims=True), first)
        if below:
            _acc(dgt_ref, jnp.sum(dx * m_ref[...], axis=0, keepdims=True), first)
            dm_ref[...] = (dx * gt_ref[...]).astype(BF16)

    row, vec = _rows(tr, D_MODEL), _vec(D_MODEL)
    first_specs = [row] if w is None else [_rows(tr, dh.shape[1]), pl.BlockSpec(memory_space=pl.ANY)]
    scratch = [] if w is None else [pltpu.VMEM((w.shape[1], dh.shape[1]), BF16), pltpu.SemaphoreType.DMA((w.shape[0],))]
    in_specs = first_specs + [row, vec, vec, row] + ([vec, row] if below else [])
    out_specs = [row, vec, vec, vec] + ([vec, row] if below else [])
    out_shape = [_sds((t, D_MODEL))] + [_sds((1, D_MODEL))] * 3 + ([_sds((1, D_MODEL)), _sds((t, D_MODEL), BF16)] if below else [])
    args = ((dh,) if w is None else (dh, w)) + (x, g, scale, dres) + ((gate, mix) if below else ())
    return _call(body, name=name, grid=(t // tr,), in_specs=in_specs, out_specs=out_specs, out_shape=out_shape,
                 scratch_shapes=scratch)(*args)


def _mix_in_bwd(dmix, w_out, o_hg, proj, att, og, ag, fine, name, tr=512):
    t = o_hg.shape[0]

    def body(dy_ref, w_ref, o_ref, g_ref, a_ref, og_ref, ag_ref,
             do_ref, dg_ref, da_ref, dd_ref, das_ref, dds_ref, dog_ref, dag_ref, to_sub):
        first = pl.program_id(0) == 0

        @pl.when(first)
        def _():
            to_sub[...] = _regroup_matrix(tr, tr // fine)

        dmi = _dot_nt(dy_ref[...], w_ref[0])
        dog = jnp.zeros((1, HG_HEAD), F32)
        for h in range(HG_WIDTH // HG_HEAD):
            sl = slice(h * HG_HEAD, (h + 1) * HG_HEAD)
            oh = o_ref[:, sl].astype(F32)
            gv = g_ref[:, sl].astype(F32)
            dv = dmi[:, sl]
            r = _rms(oh)
            xh = oh * r
            sg = _sigmoid(gv)
            dno = dv * gv * sg
            dg_ref[:, sl] = (dv * xh * og_ref[...] * sg * (1.0 + gv * (1.0 - sg))).astype(BF16)
            dog = dog + jnp.sum(dno * xh, axis=0, keepdims=True)
            do_ref[:, sl] = _rms_bwd(dno * og_ref[...], xh, r).astype(BF16)
        _acc(dog_ref, dog, first)
        av = a_ref[...]
        dav = dmi[:, HG_WIDTH:]
        r = _rms(av)
        xa = av * r
        _acc(dag_ref, jnp.sum(dav * xa, axis=0, keepdims=True), first)
        datt = _rms_bwd(dav * ag_ref[...], xa, r)
        da_ref[...] = datt.astype(BF16)
        das_ref[...] = _regroup(to_sub[...], datt.astype(BF16)).astype(BF16).reshape(das_ref.shape)
        prod = datt * av
        lane = lax.broadcasted_iota(jnp.int32, (1, 128), 1)
        dd = jnp.zeros((tr, 128), F32)
        for hp in range(ATT_HEADS // 2):
            pp = prod[:, hp * 128:(hp + 1) * 128]
            lo = jnp.sum(jnp.where(lane < 64, pp, 0.0), axis=-1, keepdims=True)
            hi = jnp.sum(jnp.where(lane >= 64, pp, 0.0), axis=-1, keepdims=True)
            dd = jnp.where(lane == 2 * hp, lo, dd)
            dd = jnp.where(lane == 2 * hp + 1, hi, dd)
        dd_ref[...] = dd
        dds_ref[...] = _regroup(to_sub[...], dd, ATT_HEADS).reshape(dds_ref.shape)

    half = _rows(tr, HG_WIDTH)
    out = _call(body, name=name, grid=(t // tr,),
                in_specs=[_rows(tr, D_MODEL), pl.BlockSpec(w_out.shape, lambda i: (0, 0, 0)), half,
                          pl.BlockSpec((tr, HG_WIDTH), lambda i: (i, 3)), half, _vec(HG_HEAD), _vec(ATT_WIDTH)],
                out_specs=[half, pl.BlockSpec((tr, HG_WIDTH), lambda i: (i, 3)), half, _rows(tr, 128),
                           _sub_rows(tr, fine, ATT_WIDTH), _sub_rows(tr, fine, 128), _vec(HG_HEAD), _vec(ATT_WIDTH)],
                out_shape=[_sds((t, HG_WIDTH), BF16), _sds((t, IN_WIDTH), BF16), _sds((t, HG_WIDTH), BF16),
                           _sds((t, 128)), _sds((fine, t // fine, ATT_WIDTH), BF16),
                           _sds((fine, t // fine, 128)), _sds((1, HG_HEAD)), _sds((1, ATT_WIDTH))],
                scratch_shapes=[pltpu.VMEM((tr, tr), BF16)])(dmix, w_out, o_hg, proj, att, og, ag)
    out = list(out)
    return out[:4] + [out[4].reshape(t, ATT_WIDTH), out[5].reshape(t, 128)] + out[6:]


def _dproj(dproj, dqkvs, name, tr=1024):
    t = dproj.shape[0]
    nbr = len(dqkvs)
    first = IN_WIDTH // ATT_WIDTH - 3

    def body(*refs):
        refs[-1][...] = sum(r[...].astype(F32) for r in refs[:nbr]).astype(BF16)

    return _call(body, name=name, grid=(t // tr, 3),
                 in_specs=[pl.BlockSpec((tr, ATT_WIDTH), lambda i, j: (i, j))] * nbr + [pl.BlockSpec(memory_space=pl.ANY)],
                 out_specs=pl.BlockSpec((tr, ATT_WIDTH), lambda i, j: (i, first + j)),
                 out_shape=_sds(dproj.shape, BF16), aliases={nbr: 0})(*dqkvs, dproj)


def _chunk_tri(upper):
    row = lax.broadcasted_iota(jnp.int32, (HG_GROUP, HG_CHUNK, HG_CHUNK), 1)
    col = lax.broadcasted_iota(jnp.int32, (HG_GROUP, HG_CHUNK, HG_CHUNK), 2)
    return (row <= col if upper else row >= col).astype(BF16)


def _chunk_cumsum(x, tri):
    x3 = x.reshape(HG_GROUP, HG_CHUNK, x.shape[1])
    dims = (((2,), (1,)), ((0,), (0,)))
    out = None
    for _ in range(3):
        part = x3.astype(BF16)
        x3 = x3 - part.astype(F32)
        term = lax.dot_general(tri, part, dims, preferred_element_type=F32)
        out = term if out is None else out + term
    return out.reshape(x.shape)


def _hg_gates(f_raw, q_raw, lb, tri):
    sg = _sigmoid(f_raw)
    f = lb + (1.0 - lb) * sg
    k = 1.0 - f
    b = _chunk_cumsum(jnp.log(f), tri)
    sq = _sigmoid(q_raw)
    return sg, f, k, b, sq


def _hg_masks(rows):
    row = lax.broadcasted_iota(jnp.int32, (rows, rows), 0)
    col = lax.broadcasted_iota(jnp.int32, (rows, rows), 1)
    same = (row // HG_CHUNK) == (col // HG_CHUNK)
    return jnp.logical_and(row >= col, same), jnp.logical_and(row <= col, same)


def _per_chunk(rows_of):
    return jnp.concatenate([jnp.broadcast_to(r, (HG_CHUNK, r.shape[1])) for r in rows_of], axis=0)


def _hgrn_fwd(proj, lb_logits, name):
    t = proj.shape[0]
    nc = t // HG_CHUNK
    nh = HG_WIDTH // HG_HEAD
    rows = HG_GROUP * HG_CHUNK

    def body(q_ref, f_ref, i_ref, lg_ref, o_ref, st_ref, s_scr):
        @pl.when(pl.program_id(0) == 0)
        def _():
            s_scr[...] = jnp.zeros_like(s_scr)

        lg = lg_ref[...]
        lb_all = _sigmoid(lg[0:1] - lg[1:2])
        causal, _ = _hg_masks(rows)
        tri = _chunk_tri(False)
        for h in range(nh):
            sl = slice(h * HG_HEAD, (h + 1) * HG_HEAD)
            q_raw = q_ref[:, sl].astype(F32)
            _, _, k, b, sq = _hg_gates(f_ref[:, sl].astype(F32), q_raw, lb_all[:, sl], tri)
            v = i_ref[:, sl].astype(BF16)
            gls = [b[(g + 1) * HG_CHUNK - 1:(g + 1) * HG_CHUNK] for g in range(HG_GROUP)]
            bm = _per_chunk([b[g * HG_CHUNK + HG_CHUNK // 2 - 1:g * HG_CHUNK + HG_CHUNK // 2] for g in range(HG_GROUP)])
            qd = (q_raw * sq * jnp.exp(b)).astype(BF16)
            qm = (q_raw * sq * jnp.exp(b - bm)).astype(BF16)
            km = (k * jnp.exp(bm - b)).astype(BF16)
            ke = (k * jnp.exp(_per_chunk(gls) - b)).astype(BF16)
            a = jnp.where(causal, _dot_nt(qm, km), 0.0).astype(BF16)
            o_intra = _dot(a, v)
            st = s_scr[h]
            o_inter = []
            for g in range(HG_GROUP):
                rs = slice(g * HG_CHUNK, (g + 1) * HG_CHUNK)
                st_ref[g, sl, :] = st
                o_inter.append(_dot_nt(qd[rs], st.astype(BF16)))
                st = st * jnp.exp(gls[g]) + _dot_tn(v[rs], ke[rs])
            s_scr[h] = st
            o_ref[:, sl] = (o_intra + jnp.concatenate(o_inter, axis=0)).astype(BF16)

    blk = lambda j: pl.BlockSpec((rows, HG_WIDTH), lambda c: (c, j))
    return _call(body, name=name, grid=(nc // HG_GROUP,),
                 in_specs=[blk(0), blk(1), blk(2), pl.BlockSpec((2, HG_WIDTH), lambda c: (0, 0))],
                 out_specs=[blk(0), pl.BlockSpec((HG_GROUP, HG_WIDTH, HG_HEAD), lambda c: (c, 0, 0))],
                 out_shape=[_sds((t, HG_WIDTH), BF16), _sds((nc, HG_WIDTH, HG_HEAD))],
                 scratch_shapes=[pltpu.VMEM((nh, HG_HEAD, HG_HEAD), F32)])(proj, proj, proj, lb_logits)


def _hgrn_bwd(proj, lb_logits, states, do, dproj, name):
    t = proj.shape[0]
    ng = t // (HG_GROUP * HG_CHUNK)
    nh = HG_WIDTH // HG_HEAD
    rows = HG_GROUP * HG_CHUNK

    def body(q_ref, f_ref, i_ref, lg_ref, st_ref, do_ref, _, d_ref, dlb_ref, ds_scr):
        first = pl.program_id(0) == 0

        @pl.when(first)
        def _():
            ds_scr[...] = jnp.zeros_like(ds_scr)

        lg = lg_ref[...]
        lb_all = _sigmoid(lg[0:1] - lg[1:2])
        causal, _ = _hg_masks(rows)
        tri = _chunk_tri(False)
        tri_t = _chunk_tri(True)
        dlb = []
        for h in range(nh):
            sl = slice(h * HG_HEAD, (h + 1) * HG_HEAD)
            q_raw = q_ref[:, sl].astype(F32)
            lb = lb_all[:, sl]
            sg, f, k, b, sq = _hg_gates(f_ref[:, sl].astype(F32), q_raw, lb, tri)
            v = i_ref[:, sl].astype(BF16)
            gls = [b[(g + 1) * HG_CHUNK - 1:(g + 1) * HG_CHUNK] for g in range(HG_GROUP)]
            bm = _per_chunk([b[g * HG_CHUNK + HG_CHUNK // 2 - 1:g * HG_CHUNK + HG_CHUNK // 2] for g in range(HG_GROUP)])
            eb = jnp.exp(b)
            ebm = jnp.exp(b - bm)
            emb = jnp.exp(bm - b)
            egb = jnp.exp(_per_chunk(gls) - b)
            ke = k * egb
            qd_b, qm_b = (q_raw * sq * eb).astype(BF16), (q_raw * sq * ebm).astype(BF16)
            km_b, ke_b = (k * emb).astype(BF16), ke.astype(BF16)
            dov = do_ref[:, sl].astype(BF16)
            a = jnp.where(causal, _dot_nt(qm_b, km_b), 0.0).astype(BF16)
            da = jnp.where(causal, _dot_nt(dov, v), 0.0).astype(BF16)
            dkm = _dot_tn(da, qm_b)
            dst = ds_scr[h]
            dqd_s, dv_s, dke_s, dgl_s = [None] * HG_GROUP, [None] * HG_GROUP, [None] * HG_GROUP, [None] * HG_GROUP
            for g in reversed(range(HG_GROUP)):
                rs = slice(g * HG_CHUNK, (g + 1) * HG_CHUNK)
                st = st_ref[g, sl, :]
                dst_b = dst.astype(BF16)
                egl = jnp.exp(gls[g])
                dqd_s[g] = _dot(dov[rs], st.astype(BF16))
                dv_s[g] = _dot_nt(ke_b[rs], dst_b)
                dke_s[g] = _dot(v[rs], dst_b)
                dgl_s[g] = jnp.sum(dst * st, axis=0, keepdims=True) * egl
                dst = _dot_tn(dov[rs], qd_b[rs]) + dst * egl
            ds_scr[h] = dst
            dqm = _dot(da, km_b)
            dqd = jnp.concatenate(dqd_s, axis=0)
            dv = _dot_tn(a, dov) + jnp.concatenate(dv_s, axis=0)
            dke = jnp.concatenate(dke_s, axis=0)
            t1 = dke * ke
            db = dqm * qm_b.astype(F32) - dkm * km_b.astype(F32) + dqd * qd_b.astype(F32) - t1
            dgl = _per_chunk([dgl_s[g] + jnp.sum(t1[g * HG_CHUNK:(g + 1) * HG_CHUNK], axis=0, keepdims=True)
                              for g in range(HG_GROUP)])
            dlf = _chunk_cumsum(db, tri_t) + dgl
            df = dlf / f - (dkm * emb + dke * egb)
            d_ref[:, sl] = ((dqm * ebm + dqd * eb) * sq * (1.0 + q_raw * (1.0 - sq))).astype(BF16)
            d_ref[:, HG_WIDTH + h * HG_HEAD:HG_WIDTH + (h + 1) * HG_HEAD] = (
                df * (1.0 - lb) * sg * (1.0 - sg)).astype(BF16)
            d_ref[:, 2 * HG_WIDTH + h * HG_HEAD:2 * HG_WIDTH + (h + 1) * HG_HEAD] = dv.astype(BF16)
            dlb.append(jnp.sum(df * (1.0 - sg), axis=0, keepdims=True))
        _acc(dlb_ref, jnp.concatenate(dlb, axis=1), first)

    rev = lambda j: pl.BlockSpec((rows, HG_WIDTH), lambda c: (ng - 1 - c, j))
    return _call(body, name=name, grid=(ng,),
                 in_specs=[rev(0), rev(1), rev(2), pl.BlockSpec((2, HG_WIDTH), lambda c: (0, 0)),
                           pl.BlockSpec((HG_GROUP, HG_WIDTH, HG_HEAD), lambda c: (ng - 1 - c, 0, 0)), rev(0),
                           pl.BlockSpec(memory_space=pl.ANY)],
                 out_specs=[pl.BlockSpec((rows, 3 * HG_WIDTH), lambda c: (ng - 1 - c, 0)), _vec(HG_WIDTH)],
                 out_shape=[_sds(dproj.shape, BF16), _sds((1, HG_WIDTH))], aliases={6: 0},
                 scratch_shapes=[pltpu.VMEM((nh, HG_HEAD, HG_HEAD), F32)])(
                     proj, proj, proj, lb_logits, states, do, dproj)


def _to_sub(a, dil):
    t, w = a.shape
    return a if dil == 1 else a.reshape(t // dil, dil, w).transpose(1, 0, 2).reshape(t, w)


def _from_sub(a, dil):
    t, w = a.shape
    return a if dil == 1 else a.reshape(dil, t // dil, w).transpose(1, 0, 2).reshape(t, w)


def _att_bias(seg):
    def place(v):
        v = v % ATT_BLOCK
        return v if seg == 1 else seg * (v % (ATT_BLOCK // seg)) + v // (ATT_BLOCK // seg)

    row = lax.broadcasted_iota(jnp.int32, (2 * ATT_BLOCK, 2 * ATT_BLOCK), 0)
    col = lax.broadcasted_iota(jnp.int32, (2 * ATT_BLOCK, 2 * ATT_BLOCK), 1)
    qi, kj = place(row), place(col)
    prev = jnp.logical_and(col < ATT_BLOCK, kj >= qi)
    cur = jnp.logical_and(col >= ATT_BLOCK, kj <= qi)
    return jnp.stack([jnp.where(cur, 0.0, NEG), jnp.where(jnp.logical_or(prev, cur), 0.0, NEG)])


def _get(ref, sl):
    if len(ref.shape) == 2:
        return ref[:, sl]
    v = ref[:, :, sl]
    return v.reshape(ATT_BLOCK, v.shape[2])


def _put(ref, sl, val):
    if len(ref.shape) == 2:
        ref[:, sl] = val
    else:
        ref[:, :, sl] = val.reshape(ref.shape[0], ref.shape[1], val.shape[1])


def _att_spec(nb, dil, seg, width, col, back):
    bps = nb // dil

    def plain(n):
        return jnp.clip(n - back, 0, nb - 1), col

    def segmented(n):
        m = jnp.clip(n - back, 0, nb - 1)
        return 0, m // bps, m % bps, 0, col

    if seg == 1:
        return pl.BlockSpec((ATT_BLOCK, width), plain)
    return pl.BlockSpec((seg, None, None, ATT_BLOCK // seg, width), segmented)


def _att_shape(nb, dil, seg, width):
    t = nb * ATT_BLOCK
    return (t, width) if seg == 1 else (seg, dil, nb // dil, ATT_BLOCK // seg, width)


def _att_view(a, nb, dil, seg):
    return a.reshape(_att_shape(nb, dil, seg, a.shape[1]))


def _attn_fwd_block(q_ref, kc_ref, kp_ref, vc_ref, vp_ref, o_ref, l_ref, bias_ref, has_prev, lane0):
    bias = bias_ref[has_prev.astype(jnp.int32)]
    lane = lax.broadcasted_iota(jnp.int32, (1, 128), 1)
    lo = lane < 64
    nq = ATT_BLOCK
    lse_all = jnp.zeros((nq, 128), F32)
    for hp in range(ATT_HEADS // 2):
        sl = slice(hp * 128, (hp + 1) * 128)
        q2 = _get(q_ref, sl)
        zero = jnp.zeros_like(q2)
        q2 = q2 * 0.125
        qs = jnp.concatenate([jnp.where(lo, q2, zero), jnp.where(lo, zero, q2)], axis=0)
        kk = jnp.concatenate([_get(kp_ref, sl), _get(kc_ref, sl)], axis=0)
        vv = jnp.concatenate([_get(vp_ref, sl), _get(vc_ref, sl)], axis=0)
        s = _dot_nt(qs, kk) + bias
        mx = jnp.max(s, axis=-1, keepdims=True)
        p = jnp.exp(s - mx)
        l = jnp.sum(p, axis=-1, keepdims=True)
        o = _dot(p.astype(BF16), vv) * (1.0 / l)
        _put(o_ref, sl, jnp.where(lo, o[:nq], o[nq:]).astype(BF16))
        lse = mx + jnp.log(l)
        lse_all = jnp.where(lane == lane0 + 2 * hp, lse[:nq], lse_all)
        lse_all = jnp.where(lane == lane0 + 2 * hp + 1, lse[nq:], lse_all)
    _put(l_ref, slice(None), lse_all)


def _attn_fwd(branches, name):
    t = branches[0][0].shape[0]
    nb = t // ATT_BLOCK
    nbr = len(branches)

    def body(*refs):
        n = pl.program_id(0)
        biases = refs[7 * nbr:]

        @pl.when(n == 0)
        def _():
            for bias_ref, (_, _, seg) in zip(biases, branches):
                bias_ref[...] = _att_bias(seg)

        for i, (_, dil, seg) in enumerate(branches):
            _attn_fwd_block(*refs[5 * i:5 * i + 5], *refs[5 * nbr + 2 * i:5 * nbr + 2 * i + 2], biases[i],
                            (n % (nb // dil)) != 0, ATT_HEADS * i)

    in_specs, args, out_specs, out_shape = [], [], [], []
    for qkv, dil, seg in branches:
        c0 = qkv.shape[1] // ATT_WIDTH - 3
        in_specs += [_att_spec(nb, dil, seg, ATT_WIDTH, c0 + j, back) for j, back in [(0, 0), (1, 0), (1, 1), (2, 0), (2, 1)]]
        args += [_att_view(qkv, nb, dil, seg)] * 5
        out_specs += [_att_spec(nb, dil, seg, ATT_WIDTH, 0, 0), _att_spec(nb, dil, seg, 128, 0, 0)]
        out_shape += [_sds(_att_shape(nb, dil, seg, ATT_WIDTH), BF16), _sds(_att_shape(nb, dil, seg, 128))]
    out = _call(body, name=name, grid=(nb,), in_specs=in_specs, out_specs=out_specs, out_shape=out_shape,
                scratch_shapes=[pltpu.VMEM((2, 2 * ATT_BLOCK, 2 * ATT_BLOCK), F32)] * nbr)(*args)
    return [(out[2 * i].reshape(t, ATT_WIDTH), out[2 * i + 1].reshape(t, 128)) for i in range(nbr)]


def _combine_mix_in(os_, ls_, fine, o_hg, proj, og, ag, name, tr=512):
    t = os_[0].shape[0]
    nbr = len(os_)

    def body(*refs):
        o_refs, l_refs = refs[:nbr], refs[nbr:2 * nbr]
        oh_ref, g_ref, og_ref, ag_ref, a_ref, lt_ref, lts_ref, m_ref, to_natural, to_sub = refs[2 * nbr:]

        @pl.when(pl.program_id(0) == 0)
        def _():
            to_natural[...] = _regroup_matrix(tr, fine)
            to_sub[...] = _regroup_matrix(tr, tr // fine)

        lane = lax.broadcasted_iota(jnp.int32, (1, 128), 1)
        packed = l_refs[0][...] + _regroup(to_natural[...], sum(r[...] for r in l_refs[1:]).reshape(tr, 128))
        ls = [packed if i == 0 else pltpu.roll(packed, 128 - ATT_HEADS * i, 1) for i in range(nbr)]
        mx = functools.reduce(jnp.maximum, ls)
        tot = mx + jnp.log(sum(jnp.exp(l - mx) for l in ls))
        ws = [jnp.exp(l - tot) for l in ls]
        tot = jnp.where(lane < ATT_HEADS, tot, 0.0)
        lt_ref[...] = tot
        lts_ref[...] = _regroup(to_sub[...], tot).reshape(lts_ref.shape)
        o_vals = [o_refs[0]] + [_regroup(to_natural[...], r[...].reshape(tr, ATT_WIDTH)) for r in o_refs[1:]]
        pairs = []
        for hp in range(ATT_HEADS // 2):
            sl = slice(hp * 128, (hp + 1) * 128)
            acc = jnp.zeros((tr, 128), F32)
            for w, o in zip(ws, o_vals):
                wf = jnp.where(lane < 64, w[:, 2 * hp:2 * hp + 1], w[:, 2 * hp + 1:2 * hp + 2])
                acc = acc + wf * o[:, sl]
            pairs.append(acc)
        av = jnp.concatenate(pairs, axis=1)
        a_ref[...] = av
        m_ref[:, HG_WIDTH:] = (av * _rms(av) * ag_ref[...]).astype(BF16)
        for h in range(HG_WIDTH // HG_HEAD):
            sl = slice(h * HG_HEAD, (h + 1) * HG_HEAD)
            oh = oh_ref[:, sl].astype(F32)
            gv = g_ref[:, sl].astype(F32)
            m_ref[:, sl] = (oh * _rms(oh) * og_ref[...] * (gv * _sigmoid(gv))).astype(BF16)

    half = _rows(tr, ATT_WIDTH)
    sub = lambda a: a.reshape(fine, t // fine, a.shape[1])
    att, lse, lse_sub, mixin = _call(
        body, name=name, grid=(t // tr,),
        in_specs=[half] + [_sub_rows(tr, fine, ATT_WIDTH)] * (nbr - 1) + [_rows(tr, 128)] + [_sub_rows(tr, fine, 128)] * (nbr - 1)
        + [half, pl.BlockSpec((tr, HG_WIDTH), lambda i: (i, 3)), _vec(HG_HEAD), _vec(ATT_WIDTH)],
        out_specs=[half, _rows(tr, 128), _sub_rows(tr, fine, 128), _rows(tr, D_MODEL)],
        out_shape=[_sds((t, ATT_WIDTH)), _sds((t, 128)), _sds((fine, t // fine, 128)), _sds((t, D_MODEL), BF16)],
        scratch_shapes=[pltpu.VMEM((tr, tr), BF16)] * 2)(
            os_[0], *map(sub, os_[1:]), ls_[0], *map(sub, ls_[1:]), o_hg, proj, og, ag)
    return att, lse, lse_sub.reshape(t, 128), mixin


def _attn_bwd_block(q_ref, k_ref, v_ref, do_ref, l_ref, d_ref, out_ref, carry, prev, bias_ref, has_prev):
    w = ATT_WIDTH
    nq = ATT_BLOCK
    bias = bias_ref[has_prev.astype(jnp.int32)]
    lo = lax.broadcasted_iota(jnp.int32, (1, 128), 1) < 64
    lse, ddv = _get(l_ref, slice(None)), _get(d_ref, slice(None))
    for hp in range(ATT_HEADS // 2):
        sl = slice(hp * 128, (hp + 1) * 128)
        sk = slice(w + hp * 128, w + (hp + 1) * 128)
        sv = slice(2 * w + hp * 128, 2 * w + (hp + 1) * 128)
        q2, do2 = _get(q_ref, sl), _get(do_ref, sl)
        zero = jnp.zeros_like(q2)
        q2 = q2 * 0.125
        qs = jnp.concatenate([jnp.where(lo, q2, zero), jnp.where(lo, zero, q2)], axis=0)
        dos = jnp.concatenate([jnp.where(lo, do2, zero), jnp.where(lo, zero, do2)], axis=0)
        kc, vc = _get(k_ref, sl), _get(v_ref, sl)
        kk = jnp.concatenate([prev[:, sl], kc], axis=0)
        vv = jnp.concatenate([prev[:, sk], vc], axis=0)
        prev[:, sl] = kc
        prev[:, sk] = vc
        ls = jnp.concatenate([lse[:, 2 * hp:2 * hp + 1], lse[:, 2 * hp + 1:2 * hp + 2]], axis=0)
        dh = jnp.concatenate([ddv[:, 2 * hp:2 * hp + 1], ddv[:, 2 * hp + 1:2 * hp + 2]], axis=0)
        p = jnp.exp(_dot_nt(qs, kk) - ls + bias)
        ds = (p * (_dot_nt(dos, vv) - dh)).astype(BF16)
        dq = _dot(ds, kk) * 0.125
        dk = _dot_tn(ds, qs)
        dv = _dot_tn(p.astype(BF16), dos)
        _put(out_ref, sl, carry[:, sl].astype(BF16))
        _put(out_ref, sk, (carry[:, sk] + dk[:nq]).astype(BF16))
        _put(out_ref, sv, (carry[:, sv] + dv[:nq]).astype(BF16))
        carry[:, sl] = jnp.where(lo, dq[:nq], dq[nq:])
        carry[:, sk] = dk[nq:]
        carry[:, sv] = dv[nq:]


def _attn_bwd(branches, name):
    t = branches[0][0].shape[0]
    nb = t // ATT_BLOCK
    nbr = len(branches)
    w = ATT_WIDTH

    def body(*refs):
        ins, outs, carries, prevs = refs[:6 * nbr], refs[6 * nbr:7 * nbr], refs[7 * nbr:8 * nbr], refs[8 * nbr:9 * nbr]
        biases = refs[9 * nbr:]
        n = pl.program_id(0)

        @pl.when(n == 0)
        def _():
            for scratch in carries + prevs:
                scratch[...] = jnp.zeros_like(scratch)
            for bias_ref, branch in zip(biases, branches):
                bias_ref[...] = _att_bias(branch[5])

        @pl.when(n < nb)
        def _():
            for i, branch in enumerate(branches):
                dil, seg = branch[4:]
                _attn_bwd_block(*ins[6 * i:6 * i + 6], outs[i], carries[i], prevs[i], biases[i], (n % (nb // dil)) != 0)

        @pl.when(n == nb)
        def _():
            for i in range(nbr):
                _put(outs[i], slice(None), carries[i][...].astype(BF16))

    in_specs, args, out_specs, out_shape = [], [], [], []
    for qkv, dout, lse, dd, dil, seg in branches:
        c0 = qkv.shape[1] // w - 3
        in_specs += [_att_spec(nb, dil, seg, w, c0 + j, 0) for j in range(3)]
        in_specs += [_att_spec(nb, dil, seg, w, 0, 0), _att_spec(nb, dil, seg, 128, 0, 0), _att_spec(nb, dil, seg, 128, 0, 0)]
        args += [_att_view(a, nb, dil, seg) for a in [qkv] * 3 + [dout, lse, dd]]
        out_specs += [_att_spec(nb, dil, seg, 3 * w, 0, 1)]
        out_shape += [_sds(_att_shape(nb, dil, seg, 3 * w), BF16)]
    out = _call(body, name=name, grid=(nb + 1,), in_specs=in_specs, out_specs=out_specs, out_shape=out_shape,
                scratch_shapes=[pltpu.VMEM((ATT_BLOCK, 3 * w), F32)] * nbr + [pltpu.VMEM((ATT_BLOCK, 2 * w), BF16)] * nbr
                + [pltpu.VMEM((2, 2 * ATT_BLOCK, 2 * ATT_BLOCK), F32)] * nbr)(*args)
    return [o.reshape(t, 3 * w) for o in out]


def _local_step(x, tgt, mod, norm1_g, lb_logits, og, ag, norm2_g, fg, get_w, put_g, late=lambda a: a, project=None):
    shift1, scale1, gate1, shift2, scale2, gate2 = [mod[:, i * D_MODEL:(i + 1) * D_MODEL] for i in range(6)]
    fg = fg.reshape(1, D_MODEL)

    h1 = _norm_mod(x, norm1_g, scale1, shift1, "norm_mod1")
    if project is None:
        w_in = get_w("w_in", h1)
        proj = _mm_nn(h1, w_in, "mm_in", out_dtype=BF16)
    else:
        proj, w_in = project(h1)
    o_hg, states = _hgrn_fwd(proj, lb_logits, "hgrn_fwd")
    fine = DILATIONS[-1]
    layouts = [(d, 1 if d == 1 else fine // d) for d in DILATIONS]
    qkv_fine = _to_sub(proj, fine)
    qkvs = [proj if d == 1 else qkv_fine for d in DILATIONS]
    natural = lambda a, d: a if d == 1 else _from_sub(a, fine)
    outs = _attn_fwd([(q, d, seg) for q, (d, seg) in zip(qkvs, layouts)], "attn_fwd")
    att, lse, lse_fine, mixin = _combine_mix_in([o for o, _ in outs], [l for _, l in outs], fine,
                                                o_hg, proj, og, ag, "attn_combine_mix_in")
    w_out = get_w("w_out", mixin)
    mix, x2, h2 = _out_resid_norm_mod(x, mixin, w_out, gate1, norm2_g, scale2, shift2, "mm_out_resid_norm_mod2")
    w_gu = get_w("w_gu", h2)
    a_ff, u_ff, act = _mm_gate_up(h2, w_gu, "mm_gu")
    w_down = get_w("w_down", act)
    dx3, dffn, loss_v, dfg, dgate2 = _down_loss(x2, act, w_down, gate2, fg, tgt, "mm_down_loss")

    dffn = put_g("w_down", *_mm_tn(act, dffn, 1, "mm_down_dw", tm=x.shape[0], tk=256), dffn)
    dau = _mm_down_dx(dffn, w_down, a_ff, u_ff, "mm_down_dx")
    dau = put_g("w_gu", *_mm_tn(h2, dau, N_SHARD, "mm_gu_dw", tm=x.shape[0], tk=512), dau)
    dx2, dshift2, dscale2, dg2, dgate1, dmix = _norm_mod_bwd(
        dau, x2, norm2_g, scale2, dx3, "mm_gu_dx_norm_bwd", gate=gate1, mix=mix, w=w_gu)
    dmix = put_g("w_out", *_mm_tn(mixin, dmix, 1, "mm_out_dw", tm=x.shape[0], tk=512), dmix)
    do_hg, dproj, datt, dd, datt_fine, dd_fine, dog, dag = _mix_in_bwd(
        dmix, w_out, o_hg, proj, att, og, ag, fine, "mm_out_dx_mix_in_bwd")
    datts = _attn_bwd([(q,) + ((datt, lse, dd) if d == 1 else (datt_fine, lse_fine, dd_fine)) + (d, seg)
                       for q, (d, seg) in zip(qkvs, layouts)], "attn_bwd")
    dproj, dlb = _hgrn_bwd(proj, lb_logits, states, do_hg, dproj, "hgrn_bwd")
    dproj = late(dproj)
    dproj = _dproj(dproj, [natural(a, d) for a, d in zip(datts, DILATIONS)], "dproj")
    dproj = put_g("w_in", *_mm_tn(h1, dproj, N_SHARD, "mm_in_dw", tm=x.shape[0], tk=512, group=2), dproj)
    dx, dshift1, dscale1, dg1 = _norm_mod_bwd(dproj, x, norm1_g, scale1, dx2, "mm_in_dx_norm_bwd", w=w_in)

    stats = jnp.concatenate([loss_v, dfg, dg2, dg1, dlb, dag, dog,
                             dshift1, dscale1, dgate1, dshift2, dscale2, dgate2], axis=1)
    return dx, stats


def _place():
    x, y, c = lax.axis_index("x"), lax.axis_index("y"), lax.axis_index("c")
    return x, y, c


def _chip_peers(x, y, c):
    return [(1 - x, y, c), (x, 1 - y, c), (1 - x, 1 - y, c)]


_HBM = pl.BlockSpec(memory_space=pltpu.HBM)
_SEM = pl.BlockSpec(memory_space=pltpu.SEMAPHORE)
_EFFECT = pltpu.SideEffectType.DATAFLOW_SIDE_EFFECTING


def _exchange_copy(bufs, send, recv, j, peer, place, kind):
    x, y, c = place
    target = peer
    if kind == "gather":
        src = dst = bufs[0].at[2 * x + y]
    elif kind == "scatter":
        src, dst = bufs[0].at[2 * peer[0] + peer[1]], bufs[1].at[j]
    else:
        half = bufs[0].shape[1] // 2
        rows = pl.ds(c * half, half)
        if kind == "half":
            src = dst = bufs[0].at[2 * x + y, rows]
        else:
            src = dst = bufs[0].at[2 * peer[0] + peer[1], rows]
            target = (x, y, 1 - c)
    return pltpu.make_async_remote_copy(src_ref=src, dst_ref=dst, send_sem=send.at[j], recv_sem=recv.at[j],
                                        device_id=target, device_id_type=MESH)


def _exchange_start(groups, after, kind, name):
    sizes = [len(g) for g in groups]
    flat = [b for g in groups for b in g]
    ng, nb = len(groups), len(flat)

    def body(*refs):
        bufs, sems = refs[:nb], refs[nb + 1:nb + 1 + 2 * ng]
        x, y, c = _place()
        for j, peer in enumerate(_chip_peers(x, y, c)):
            at = 0
            for i, size in enumerate(sizes):
                _exchange_copy(bufs[at:at + size], sems[2 * i], sems[2 * i + 1], j, peer, (x, y, c), kind).start()
                at += size

    any_space = pl.BlockSpec(memory_space=pl.ANY)
    out = pl.pallas_call(
        body, name=name, in_specs=[_HBM] * nb + [any_space],
        out_specs=[_SEM] * (2 * ng) + [_HBM] * nb + [any_space],
        out_shape=[pltpu.SemaphoreType.DMA((3,))] * (2 * ng) + [pltpu.HBM(b.shape, b.dtype) for b in flat]
        + [_sds(after.shape, after.dtype)],
        input_output_aliases={i: 2 * ng + i for i in range(nb + 1)},
        compiler_params=pltpu.CompilerParams(has_side_effects=_EFFECT),
    )(*[pltpu.with_memory_space_constraint(b, pltpu.HBM) for b in flat], after)
    started, at = [], 2 * ng
    for i, size in enumerate(sizes):
        started.append((out[2 * i], out[2 * i + 1], tuple(out[at:at + size])))
        at += size
    return started, out[-1]


def _exchange_wait(started, after, kind, name):
    send, recv, bufs = started
    nb = len(bufs)

    def body(*refs):
        x, y, c = _place()
        for j, peer in enumerate(_chip_peers(x, y, c)):
            cp = _exchange_copy(refs[:nb], refs[nb], refs[nb + 1], j, peer, (x, y, c), kind)
            cp.wait_send()
            cp.wait_recv()

    return pl.pallas_call(
        body, name=name, in_specs=[_HBM] * nb + [_SEM, _SEM, pl.BlockSpec(memory_space=pl.ANY)],
        out_specs=[_HBM] * nb, out_shape=[pltpu.HBM(b.shape, b.dtype) for b in bufs],
        input_output_aliases={i: i for i in range(nb)},
        compiler_params=pltpu.CompilerParams(has_side_effects=_EFFECT),
    )(*bufs, send, recv, after)


def _sibling_copies(v_refs, l_refs, send, recv):
    x, y, c = _place()
    return [pltpu.make_async_remote_copy(src_ref=v, dst_ref=l, send_sem=send.at[a], recv_sem=recv.at[a],
                                         device_id=(x, y, 1 - c), device_id_type=MESH)
            for a, (v, l) in enumerate(zip(v_refs, l_refs))]


def _sibling_start(vs, after, name):
    vs = list(vs)
    n = len(vs)
    lands = [lax.empty(v.shape, v.dtype) for v in vs]

    def body(*refs):
        for cp in _sibling_copies(refs[:n], refs[n:2 * n], refs[2 * n + 1], refs[2 * n + 2]):
            cp.start()

    any_space = pl.BlockSpec(memory_space=pl.ANY)
    out = pl.pallas_call(
        body, name=name, in_specs=[_HBM] * (2 * n) + [any_space],
        out_specs=[_SEM, _SEM] + [_HBM] * (2 * n) + [any_space],
        out_shape=[pltpu.SemaphoreType.DMA((n,))] * 2 + [pltpu.HBM(b.shape, b.dtype) for b in vs + lands]
        + [_sds(after.shape, after.dtype)],
        input_output_aliases={i: 2 + i for i in range(2 * n + 1)},
        compiler_params=pltpu.CompilerParams(has_side_effects=_EFFECT),
    )(*[pltpu.with_memory_space_constraint(b, pltpu.HBM) for b in vs + lands], after)
    return (out[0], out[1], tuple(out[2:2 + n]), tuple(out[2 + n:2 + 2 * n])), out[-1]


def _sibling_wait(started, after, name):
    send, recv, vs, lands = started
    n = len(vs)

    def body(*refs):
        for cp in _sibling_copies(refs[:n], refs[n:2 * n], refs[2 * n], refs[2 * n + 1]):
            cp.wait_send()
            cp.wait_recv()

    out = pl.pallas_call(
        body, name=name, in_specs=[_HBM] * (2 * n) + [_SEM, _SEM, pl.BlockSpec(memory_space=pl.ANY)],
        out_specs=[_HBM] * (2 * n), out_shape=[pltpu.HBM(b.shape, b.dtype) for b in vs + lands],
        input_output_aliases={i: i for i in range(2 * n)},
        compiler_params=pltpu.CompilerParams(has_side_effects=_EFFECT),
    )(*vs, *lands, send, recv, after)
    return out[:n], out[n:]


def _everyone(x, y, c):
    return [(1 - x if k & 4 else x, 1 - y if k & 2 else y, 1 - c if k & 1 else c) for k in range(1, 8)]


def _all_gather_copies(land_ref, send, recv, arriving):
    x, y, c = _place()
    me = 4 * x + 2 * y + c
    return [pltpu.make_async_remote_copy(
        src_ref=land_ref.at[me], dst_ref=land_ref.at[4 * p[0] + 2 * p[1] + p[2] if arriving else me],
        send_sem=send.at[k], recv_sem=recv.at[k], device_id=p, device_id_type=MESH)
        for k, p in enumerate(_everyone(x, y, c))]


def _all_gather_start(v, name):
    x, y, c = _place()
    land = lax.dynamic_update_slice(lax.empty((8,) + v.shape, v.dtype), v[None], (4 * x + 2 * y + c, 0, 0))

    def body(land_ref, v_ref, send, recv, land_out, v_out):
        for cp in _all_gather_copies(land_ref, send, recv, False):
            cp.start()

    any_space = pl.BlockSpec(memory_space=pl.ANY)
    out = pl.pallas_call(
        body, name=name, in_specs=[_HBM, any_space], out_specs=[_SEM, _SEM, _HBM, any_space],
        out_shape=[pltpu.SemaphoreType.DMA((7,))] * 2 + [pltpu.HBM(land.shape, land.dtype), _sds(v.shape, v.dtype)],
        input_output_aliases={0: 2, 1: 3}, compiler_params=pltpu.CompilerParams(has_side_effects=_EFFECT),
    )(pltpu.with_memory_space_constraint(land, pltpu.HBM), v)
    return tuple(out[:3]), out[3]


def _all_gather_wait(started, after, name):
    send, recv, land = started

    def body(land_ref, send, recv, after_ref, land_out):
        for cp in _all_gather_copies(land_ref, send, recv, True):
            cp.wait_send()
            cp.wait_recv()

    return pl.pallas_call(
        body, name=name, in_specs=[_HBM, _SEM, _SEM, pl.BlockSpec(memory_space=pl.ANY)], out_specs=_HBM,
        out_shape=pltpu.HBM(land.shape, land.dtype), input_output_aliases={0: 0},
        compiler_params=pltpu.CompilerParams(has_side_effects=_EFFECT),
    )(land, send, recv, after)


def _cast_place(ws, shard, name, after=()):
    n = len(ws)

    def body(s_ref, *refs):
        for w_ref, o_ref in zip(refs[:n], refs[-n:]):
            o_ref[0] = w_ref[...].astype(BF16)

    return pl.pallas_call(
        body, name=name, out_shape=[_sds((N_SHARD,) + w.shape, BF16) for w in ws],
        grid_spec=pltpu.PrefetchScalarGridSpec(
            num_scalar_prefetch=1, grid=(4,),
            in_specs=[pl.BlockSpec((w.shape[0] // 4, w.shape[1]), lambda i, s: (i, 0)) for w in ws]
            + [pl.BlockSpec(memory_space=pl.ANY)] * len(after),
            out_specs=[pl.BlockSpec((1, w.shape[0] // 4, w.shape[1]), lambda i, s: (s[0], i, 0)) for w in ws]),
        compiler_params=pltpu.CompilerParams(dimension_semantics=("arbitrary",), vmem_limit_bytes=VMEM_LIMIT),
    )(shard.reshape(1).astype(jnp.int32), *ws, *after)


def _mod_rows(c8, w_ada, b_ada, name):
    n = w_ada.shape[1]

    def gather(src_ref, dst_ref, send, recv, loc, base):
        x, y, c = _place()
        me = 4 * x + 2 * y + c
        own = pltpu.make_async_copy(src_ref, dst_ref.at[me], loc)
        own.start()
        peers = _everyone(x, y, c)
        sends = [pltpu.make_async_remote_copy(src_ref=src_ref, dst_ref=dst_ref.at[me], send_sem=send.at[base + k],
                                              recv_sem=recv.at[base + k], device_id=p, device_id_type=MESH)
                 for k, p in enumerate(peers)]
        for cp in sends:
            cp.start()
        for k, p in enumerate(peers):
            pltpu.make_async_remote_copy(src_ref=src_ref, dst_ref=dst_ref.at[4 * p[0] + 2 * p[1] + p[2]],
                                         send_sem=send.at[base + k], recv_sem=recv.at[base + k], device_id=p,
                                         device_id_type=MESH).wait_recv()
        for cp in sends:
            cp.wait_send()
        own.wait()

    def body(c_ref, w_ref, b_ref, a_ref, parts_ref, c_all, part, send, recv, loc):
        gather(c_ref, c_all, send, recv, loc.at[0], 0)
        cv = jnp.max(c_all[...], axis=1)
        ca = cv * _sigmoid(cv)
        a_ref[...] = ca
        part[...] = jnp.dot(ca, w_ref[...], precision=lax.Precision.HIGHEST, preferred_element_type=F32) + b_ref[...]
        gather(part, parts_ref, send, recv, loc.at[1], 7)

    vmem = pl.BlockSpec(memory_space=pltpu.VMEM)
    return pl.pallas_call(
        body, name=name, in_specs=[vmem] * 3, out_specs=[vmem, vmem],
        out_shape=[_sds((8, D_MODEL)), _sds((8, 8, n))],
        scratch_shapes=[pltpu.VMEM((8, 8, D_MODEL), F32), pltpu.VMEM((8, n), F32), pltpu.SemaphoreType.DMA((14,)),
                        pltpu.SemaphoreType.DMA((14,)), pltpu.SemaphoreType.DMA((2,))],
        compiler_params=pltpu.CompilerParams(vmem_limit_bytes=VMEM_LIMIT))(c8, w_ada, b_ada)


def _sum_received(gs, shard, lands, name):
    n = len(gs)

    def body(s_ref, *refs):
        for g_ref, l_ref, o_ref in zip(refs[:n], refs[n:2 * n], refs[2 * n:]):
            o_ref[...] = ((g_ref[0] + l_ref[0].astype(F32)) + l_ref[1].astype(F32)) + l_ref[2].astype(F32)

    quarter = lambda g: (g.shape[1] // 4, g.shape[2])
    return pl.pallas_call(
        body, name=name, out_shape=[_sds(g.shape[1:]) for g in gs],
        grid_spec=pltpu.PrefetchScalarGridSpec(
            num_scalar_prefetch=1, grid=(4,),
            in_specs=[pl.BlockSpec((1,) + quarter(g), lambda i, s: (s[0], i, 0)) for g in gs]
            + [pl.BlockSpec((3,) + quarter(g), lambda i, s: (0, i, 0)) for g in gs],
            out_specs=[pl.BlockSpec(quarter(g), lambda i, s: (i, 0)) for g in gs]),
        compiler_params=pltpu.CompilerParams(dimension_semantics=("arbitrary",), vmem_limit_bytes=VMEM_LIMIT),
    )(shard.reshape(1).astype(jnp.int32), *gs, *lands)


def _adamw_outer(w, ct, dm, m, v, name):
    k, n = w.shape
    tr = k // 4

    def body(w_ref, c_ref, d_ref, m_ref, v_ref, g_out, d_out, m_out, v_out):
        cv = c_ref[...]
        dv = d_ref[...]
        g = cv[:, 0:1] * dv[0:1, :]
        for i in range(1, 8):
            g = g + cv[:, i:i + 1] * dv[i:i + 1, :]
        g_out[...] = g
        d_out[...], m_out[...], v_out[...] = _adamw_math(w_ref[...], g, m_ref[...], v_ref[...])

    row = _rows(tr, n)
    return _call(body, name=name, grid=(4,),
                 in_specs=[row, _rows(tr, 8), pl.BlockSpec((8, n), lambda i: (0, 0)), row, row],
                 out_specs=[row] * 4, out_shape=[_sds((k, n))] * 4)(w, ct, dm, m, v)


def _adamw_math(w, g, m, v):
    m_new = ADAM_B1 * m + (1.0 - ADAM_B1) * g
    v_new = ADAM_B2 * v + (1.0 - ADAM_B2) * (g * g)
    m_hat = m_new / (1.0 - ADAM_B1 ** ADAM_STEP)
    v_hat = v_new / (1.0 - ADAM_B2 ** ADAM_STEP)
    return -ADAM_LR * (m_hat / (jnp.sqrt(v_hat) + ADAM_EPS) + ADAM_WD * w), m_new, v_new


def _small_update(stats, smalls, name):
    offsets = [ST_DMOD, ST_DG1, ST_DLB, ST_DOG, ST_DAG, ST_DG2, ST_DFG]
    lb_index = 2

    def body(*refs):
        s_ref, ins, l_ref, outs = refs[0], refs[1:22], refs[22], refs[23:]
        tot = s_ref[0:1, :]
        for i in range(1, 8):
            tot = tot + s_ref[i:i + 1, :]
        l_ref[...] = jnp.zeros((1, 128), F32) + (0.5 / D_MODEL) * jnp.sum(tot[:, ST_LOSS:ST_LOSS + D_MODEL])
        for p, off in enumerate(offsets):
            w_ref, m_ref, v_ref = ins[3 * p:3 * p + 3]
            g_out, d_out, m_out, v_out = outs[4 * p:4 * p + 4]
            g = tot[:, off:off + w_ref.shape[1]]
            if p == lb_index:
                lg = w_ref[...]
                lb = _sigmoid(lg[0:1] - lg[1:2])
                g = g * lb * (1.0 - lb)
            for r in range(w_ref.shape[0]):
                rows = slice(r, r + 1)
                gr = g if r == 0 else -g
                delta, m_new, v_new = _adamw_math(w_ref[rows, :], gr, m_ref[rows, :], v_ref[rows, :])
                g_out[rows, :] = gr
                d_out[rows, :] = delta
                m_out[rows, :] = m_new
                v_out[rows, :] = v_new

    full = lambda a: pl.BlockSpec(a.shape, lambda i: (0, 0))
    flat = [a for t in smalls for a in t]
    return _call(body, name=name, grid=(1,),
                 in_specs=[full(stats)] + [full(a) for a in flat],
                 out_specs=[pl.BlockSpec((1, 128), lambda i: (0, 0))] + [full(t[0]) for t in smalls for _ in range(4)],
                 out_shape=[_sds((1, 128))] + [_sds(t[0].shape) for t in smalls for _ in range(4)])(stats, *flat)


def _adamw(params, name):
    n = len(params)

    def body(*refs):
        for p in range(n):
            w_ref, ga_ref, gb_ref, m_ref, v_ref = refs[5 * p:5 * p + 5]
            g_out, d_out, m_out, v_out = refs[5 * n + 4 * p:5 * n + 4 * p + 4]
            g = ga_ref[...] + gb_ref[...]
            g_out[...] = g
            d_out[...], m_out[...], v_out[...] = _adamw_math(w_ref[...], g, m_ref[...], v_ref[...])

    row = lambda w: _rows(w.shape[0] // 4, w.shape[1])
    out = _call(body, name=name, grid=(4,), in_specs=[row(p[0]) for p in params for _ in range(5)],
                out_specs=[row(p[0]) for p in params for _ in range(4)],
                out_shape=[_sds(p[0].shape) for p in params for _ in range(4)])(*[a for p in params for a in p])
    return [tuple(out[4 * p:4 * p + 4]) for p in range(n)]


def kernel(x, c, w_ada, b_ada, norm1_g, w_in, hg_lb_logits, hg_onorm_g, att_onorm_g, w_out, norm2_g, w_gate_up, w_down, final_g, loss_target, m_w_ada, m_b_ada, m_norm1_g, m_w_in, m_hg_lb_logits, m_hg_onorm_g, m_att_onorm_g, m_w_out, m_norm2_g, m_w_gate_up, m_w_down, m_final_g, v_w_ada, v_b_ada, v_norm1_g, v_w_in, v_hg_lb_logits, v_hg_onorm_g, v_att_onorm_g, v_w_out, v_norm2_g, v_w_gate_up, v_w_down, v_final_g):
    ix, iy, ic = _place()
    shard = 2 * ix + iy
    sample = 4 * ix + 2 * iy + ic
    n_ada = w_ada.shape[2]

    shards = [w_in[0], w_out[0], w_gate_up[0], w_down[0]]
    names = ["w_in", "w_out", "w_gu", "w_down"]
    shapes = [(N_SHARD,) + w.shape for w in shards]
    placed = [(_cast_place(shards[:1], shard, "place_w_in")[0],)]

    b_part = lax.dynamic_slice(b_ada, (0, shard * n_ada), (1, n_ada))
    c_act, parts = _mod_rows(jnp.broadcast_to(c, (8, D_MODEL)), w_ada[0], b_part, "mod_rows")
    parts = parts[::2]
    mod = lax.dynamic_index_in_dim(parts, sample, axis=1, keepdims=False).reshape(1, 6 * D_MODEL)
    (first,), mod = _exchange_start(placed[:1], mod, "half", "gather_start_w_in")
    gathering = {}

    def get_w(name, after):
        if name == "w_in":
            placed_rest = [(p,) for p in _cast_place(shards[1:], shard, "place_rest", (after,))]
            halves = _exchange_wait(first, placed_rest[0][0], "half", "gather_wait_w_in")
            (passing,), token = _exchange_start([tuple(halves)], mod, "forward", "forward_start_w_in")
            rest, token = _exchange_start(placed_rest, token, "gather", "gather_start_rest")
            (full,) = _exchange_wait(passing, token, "forward", "forward_wait_w_in")
            gathering.update(zip(names[1:], rest))
            return full
        (full,) = _exchange_wait(gathering[name], after, "gather", "gather_wait_" + name)
        return full if name == "w_gu" else full.reshape(1, -1, D_MODEL)

    scattering = {}

    def put_g(name, g, g_bf16, then):
        shape = shapes[names.index(name)]
        land = lax.empty((3,) + shape[1:], BF16)
        (started,), then = _exchange_start([(g_bf16.reshape(shape), land)], then, "scatter", "scatter_start_" + name)
        scattering[name] = (g.reshape(shape), started)
        return then

    def summed(group, after, tag):
        lands = [_exchange_wait(scattering[nm][1], after, "scatter", "scatter_wait_" + nm)[1] for nm in group]
        return _sum_received([scattering[nm][0] for nm in group], shard, lands, "sum_" + tag)

    early = ["w_down", "w_gu", "w_out"]
    swapping = []

    def late(a):
        started, a = _sibling_start(summed(early, a, "early"), a, "swap_start")
        swapping.append(started)
        return a

    def project(h1):
        own = _mm_own_shard(h1, shards[0], shard, N_SHARD, "mm_in_own")
        w_full = get_w("w_in", own)
        return _mm_other_shards(h1, w_full, own, shard, "mm_in_rest"), w_full

    dx, stats = _local_step(x[0], loss_target[0], mod, norm1_g, hg_lb_logits, hg_onorm_g, att_onorm_g,
                            norm2_g, final_g, get_w, put_g, late, project)

    gathering_stats, stats = _all_gather_start(stats, "stats_start")
    moments = [(m_w_in, v_w_in), (m_w_out, v_w_out), (m_w_gate_up, v_w_gate_up), (m_w_down, v_w_down)]

    def update(group, sums, other, tag):
        params = [(shards[names.index(nm)], s, o, moments[names.index(nm)][0][0], moments[names.index(nm)][1][0])
                  for nm, s, o in zip(group, sums, other)]
        return dict(zip(group, _adamw(params, "adamw_" + tag)))

    sums, other = _sibling_wait(swapping[0], stats, "swap_wait")
    done = update(early, sums, other, "early")
    swapping_in, stats = _sibling_start(summed(["w_in"], done["w_out"][1], "w_in"), stats, "swap_start_w_in")

    stats_all = _all_gather_wait(gathering_stats, stats, "stats_wait").reshape(8, ST_WIDTH)
    dmod = lax.dynamic_slice(stats_all, (0, ST_DMOD + shard * n_ada), (8, n_ada))

    as_row = lambda a: a.reshape(1, -1) if a.ndim == 1 else a
    smalls = [tuple(as_row(a) for a in t) for t in [
        (b_ada, m_b_ada, v_b_ada), (norm1_g, m_norm1_g, v_norm1_g),
        (hg_lb_logits, m_hg_lb_logits, v_hg_lb_logits), (hg_onorm_g, m_hg_onorm_g, v_hg_onorm_g),
        (att_onorm_g, m_att_onorm_g, v_att_onorm_g), (norm2_g, m_norm2_g, v_norm2_g),
        (final_g, m_final_g, v_final_g)]]
    loss, *small_out = _small_update(stats_all, smalls, "small_update")
    shapes_out = [b_ada.shape, norm1_g.shape, hg_lb_logits.shape, hg_onorm_g.shape, att_onorm_g.shape,
                  norm2_g.shape, final_g.shape]
    sg, sd, sm, sv = [[small_out[4 * p + i].reshape(shapes_out[p]) for p in range(7)] for i in range(4)]

    ada = _adamw_outer(w_ada[0], c_act.T, dmod, m_w_ada[0], v_w_ada[0], "adamw_w_ada")
    sum_in, other_in = _sibling_wait(swapping_in, ada[1], "swap_wait_w_in")
    done.update(update(["w_in"], sum_in, other_in, "w_in"))
    big = [ada] + [done[nm] for nm in names]
    bg, bd, bm, bv = [[t[i][None] for t in big] for i in range(4)]

    def order(b, s):
        return [b[0], s[0], s[1], b[1], s[2], s[3], s[4], b[2], s[5], b[3], b[4], s[6]]

    return (loss[0, 0], dx[None], *order(bg, sg), *order(bd, sd), *order(bm, sm), *order(bv, sv))
```

```python
import functools

import jax
import jax.numpy as jnp
from jax import lax
from jax.experimental import pallas as pl
from jax.experimental.pallas import tpu as pltpu

F32 = jnp.float32
BF16 = jnp.bfloat16
MESH = pl.DeviceIdType.MESH

D_MODEL = 1024
HG_WIDTH = 512
HG_HEAD = 128
HG_CHUNK = 64
HG_GROUP = 4
ATT_WIDTH = 512
ATT_HEADS = 8
ATT_BLOCK = 128
DILATIONS = (1, 4, 16)
D_FF = 2816
IN_WIDTH = 3584
N_SHARD = 4
RMS_EPS = 1e-6
NEG = -1e30

ADAM_LR = 0.001
ADAM_B1 = 0.9
ADAM_B2 = 0.999
ADAM_EPS = 1e-08
ADAM_WD = 0.01
ADAM_STEP = 10

VMEM_LIMIT = 56 * 2**20

ST_LOSS, ST_DFG, ST_DG2, ST_DG1 = 0, 1024, 2048, 3072
ST_DLB, ST_DAG, ST_DOG, ST_DMOD = 4096, 4608, 5120, 5248
ST_WIDTH = 5248 + 6144


def _call(body, *, name, grid, in_specs, out_specs, out_shape, scratch_shapes=(), aliases=None):
    return pl.pallas_call(
        body, name=name, grid=grid, in_specs=in_specs, out_specs=out_specs, out_shape=out_shape,
        scratch_shapes=list(scratch_shapes), input_output_aliases=aliases or {},
        compiler_params=pltpu.CompilerParams(
            dimension_semantics=("arbitrary",) * len(grid), vmem_limit_bytes=VMEM_LIMIT))


def _sds(shape, dtype=F32):
    return jax.ShapeDtypeStruct(shape, dtype)


def _dot(a, b):
    return jnp.dot(a, b, preferred_element_type=F32)


def _dot_nt(a, b):
    return lax.dot_general(a, b, (((1,), (1,)), ((), ())), preferred_element_type=F32)


def _dot_tn(a, b):
    return lax.dot_general(a, b, (((0,), (0,)), ((), ())), preferred_element_type=F32)


def _sigmoid(x):
    return 1.0 / (1.0 + jnp.exp(-x))


def _rows(tr, width):
    return pl.BlockSpec((tr, width), lambda i: (i, 0))


def _vec(width):
    return pl.BlockSpec((1, width), lambda i: (0, 0))


def _acc(ref, val, first):
    @pl.when(first)
    def _():
        ref[...] = val

    @pl.when(jnp.logical_not(first))
    def _():
        ref[...] += val


def _sub_rows(tr, fine, width):
    return pl.BlockSpec((fine, tr // fine, width), lambda i: (0, i, 0))


def _regroup_matrix(tr, groups):
    a = lax.broadcasted_iota(jnp.int32, (tr, tr), 0)
    b = lax.broadcasted_iota(jnp.int32, (tr, tr), 1)
    return (b == (a % groups) * (tr // groups) + a // groups).astype(BF16)


def _regroup(m, v, lanes=None):
    if v.dtype == BF16:
        return _dot(m, v)
    width = v.shape[1]
    packed, out = None, None
    for i in range(3):
        part = v.astype(BF16).astype(F32)
        v = v - part
        if lanes is None:
            out = _dot(m, part.astype(BF16)) if i == 0 else out + _dot(m, part.astype(BF16))
        else:
            packed = part if i == 0 else packed + pltpu.roll(part, i * lanes, 1)
    if lanes is None:
        return out
    out = _dot(m, packed.astype(BF16))
    out = out + pltpu.roll(out, width - lanes, 1) + pltpu.roll(out, width - 2 * lanes, 1)
    return jnp.where(lax.broadcasted_iota(jnp.int32, (1, width), 1) < lanes, out, 0.0)


def _mm_nn(a, b3, name, tm=1024, out_dtype=F32):
    m, k = a.shape
    s, _, n = b3.shape

    def body(a_ref, b_ref, o_ref):
        o_ref[...] = _dot(a_ref[...], b_ref[0]).astype(out_dtype)

    return _call(
        body, name=name, grid=(s, m // tm),
        in_specs=[pl.BlockSpec((tm, k), lambda j, i: (i, 0)), pl.BlockSpec((1, k, n), lambda j, i: (j, 0, 0))],
        out_specs=pl.BlockSpec((tm, n), lambda j, i: (i, j)), out_shape=_sds((m, s * n), out_dtype))(a, b3)


def _mm_own_shard(a, w, shard, s, name, tm=1024):
    m, k = a.shape
    n = w.shape[1]

    def body(s_ref, a_ref, w_ref, o_ref):
        o_ref[...] = _dot(a_ref[...], w_ref[...].astype(BF16)).astype(BF16)

    return pl.pallas_call(
        body, name=name, out_shape=_sds((m, s * n), BF16),
        grid_spec=pltpu.PrefetchScalarGridSpec(
            num_scalar_prefetch=1, grid=(m // tm,),
            in_specs=[pl.BlockSpec((tm, k), lambda i, sh: (i, 0)), pl.BlockSpec((k, n), lambda i, sh: (0, 0))],
            out_specs=pl.BlockSpec((tm, n), lambda i, sh: (i, sh[0]))),
        compiler_params=pltpu.CompilerParams(dimension_semantics=("arbitrary",), vmem_limit_bytes=VMEM_LIMIT),
    )(shard.reshape(1).astype(jnp.int32), a, w)


def _mm_other_shards(a, b3, partial, shard, name, tm=1024):
    m, k = a.shape
    s, _, n = b3.shape
    which = lambda j, sh: (sh[0] + 1 + j) % s

    def body(s_ref, a_ref, b_ref, p_ref, o_ref):
        o_ref[...] = _dot(a_ref[...], b_ref[0]).astype(BF16)

    return pl.pallas_call(
        body, name=name, out_shape=_sds(partial.shape, BF16),
        grid_spec=pltpu.PrefetchScalarGridSpec(
            num_scalar_prefetch=1, grid=(s - 1, m // tm),
            in_specs=[pl.BlockSpec((tm, k), lambda j, i, sh: (i, 0)),
                      pl.BlockSpec((1, k, n), lambda j, i, sh: (which(j, sh), 0, 0)),
                      pl.BlockSpec(memory_space=pl.ANY)],
            out_specs=pl.BlockSpec((tm, n), lambda j, i, sh: (i, which(j, sh)))),
        input_output_aliases={3: 0},
        compiler_params=pltpu.CompilerParams(dimension_semantics=("arbitrary",) * 2, vmem_limit_bytes=VMEM_LIMIT),
    )(shard.reshape(1).astype(jnp.int32), a, b3, partial)


def _mm_tn(a, dy, s, name, tm, tk, group=1):
    m, k = a.shape
    n = dy.shape[1] // s
    steps = m // tm

    def body(a_ref, dy_ref, o_ref, ob_ref):
        p = _dot_tn(a_ref[...], dy_ref[...])
        for g in range(group):
            pg = p[:, g * n:(g + 1) * n]
            if steps == 1:
                o_ref[g] = pg
                ob_ref[g] = pg.astype(BF16)
            else:
                _acc(o_ref.at[g], pg, pl.program_id(2) == 0)
        if steps > 1:
            @pl.when(pl.program_id(2) == steps - 1)
            def _():
                ob_ref[...] = o_ref[...].astype(BF16)

    out = pl.BlockSpec((group, tk, n), lambda kk, j, i: (j, kk, 0))
    return _call(
        body, name=name, grid=(k // tk, s // group, steps),
        in_specs=[pl.BlockSpec((tm, tk), lambda kk, j, i: (i, kk)),
                  pl.BlockSpec((tm, group * n), lambda kk, j, i: (i, j))],
        out_specs=[out, out], out_shape=[_sds((s, k, n)), _sds((s, k, n), BF16)])(a, dy)


def _rms(x):
    return lax.rsqrt(jnp.mean(x * x, axis=-1, keepdims=True) + RMS_EPS)


def _rms_bwd(dxh, xh, r):
    return r * (dxh - xh * jnp.mean(dxh * xh, axis=-1, keepdims=True))


def _norm_mod(x, g, scale, shift, name, tr=512):
    t = x.shape[0]

    def body(x_ref, g_ref, sc_ref, sh_ref, h_ref):
        xv = x_ref[...]
        n = xv * _rms(xv) * g_ref[...]
        h_ref[...] = (n * (1.0 + sc_ref[...]) + sh_ref[...]).astype(BF16)

    return _call(body, name=name, grid=(t // tr,),
                 in_specs=[_rows(tr, D_MODEL), _vec(D_MODEL), _vec(D_MODEL), _vec(D_MODEL)],
                 out_specs=_rows(tr, D_MODEL), out_shape=_sds((t, D_MODEL), BF16))(x, g, scale, shift)


def _out_resid_norm_mod(x, mixin, w_out, gate, g, scale, shift, name, tr=512):
    t = x.shape[0]

    def body(x_ref, mi_ref, w_ref, gt_ref, g_ref, sc_ref, sh_ref, m_ref, x2_ref, h_ref):
        mix = _dot(mi_ref[...], w_ref[0])
        m_ref[...] = mix
        x2 = x_ref[...] + gt_ref[...] * mix
        x2_ref[...] = x2
        n = x2 * _rms(x2) * g_ref[...]
        h_ref[...] = (n * (1.0 + sc_ref[...]) + sh_ref[...]).astype(BF16)

    row = _rows(tr, D_MODEL)
    return _call(body, name=name, grid=(t // tr,),
                 in_specs=[row, row, pl.BlockSpec(w_out.shape, lambda i: (0, 0, 0))] + [_vec(D_MODEL)] * 4,
                 out_specs=[row, row, row],
                 out_shape=[_sds((t, D_MODEL)), _sds((t, D_MODEL)), _sds((t, D_MODEL), BF16)])(
                     x, mixin, w_out, gate, g, scale, shift)


def _mm_gate_up(h, w_gu, name, tm=1024):
    m, k = h.shape
    n = w_gu.shape[2]

    def body(h_ref, wa_ref, wu_ref, da_ref, du_ref, o_ref, w_au):
        @pl.when(pl.program_id(1) == 0)
        def _():
            w_au[:, :n] = wa_ref[0]
            w_au[:, n:] = wu_ref[0]

        au = _dot(h_ref[...], w_au[...])
        a, u = au[:, :n], au[:, n:]
        sg = _sigmoid(a)
        silu = a * sg
        da_ref[...] = (u * sg * (1.0 + a * (1.0 - sg))).astype(BF16)
        du_ref[...] = silu.astype(BF16)
        o_ref[...] = (silu * u).astype(BF16)

    out = pl.BlockSpec((tm, n), lambda j, i: (i, j))
    return _call(body, name=name, grid=(2, m // tm),
                 in_specs=[pl.BlockSpec((tm, k), lambda j, i: (i, 0)), pl.BlockSpec((1, k, n), lambda j, i: (j, 0, 0)),
                           pl.BlockSpec((1, k, n), lambda j, i: (j + 2, 0, 0))],
                 out_specs=[out, out, out], out_shape=[_sds((m, 2 * n), BF16)] * 3,
                 scratch_shapes=[pltpu.VMEM((k, 2 * n), BF16)])(h, w_gu, w_gu)


def _mm_down_dx(dffn, w_down, act_da, act_du, name, tm=512):
    m = dffn.shape[0]
    _, k, n = w_down.shape

    steps = m // tm
    slots = 3

    def body(d_ref, w_ref, da_hbm, du_hbm, o_ref, da_buf, du_buf, sem):
        s = pl.program_id(0)

        def copies(step):
            slot = step % slots
            return [pltpu.make_async_copy(src.at[pl.ds(step * tm, tm)], dst.at[slot], sem.at[j, slot])
                    for j, (src, dst) in enumerate([(da_hbm, da_buf), (du_hbm, du_buf)])]

        @pl.when(s == 0)
        def _():
            for step in range(min(slots - 1, steps)):
                for cp in copies(step):
                    cp.start()

        @pl.when(s + slots - 1 < steps)
        def _():
            for cp in copies(s + slots - 1):
                cp.start()

        for cp in copies(s):
            cp.wait()
        slot = s % slots
        dact = _dot_nt(d_ref[...], w_ref[0])
        o_ref[:, :k] = (dact * da_buf[slot].astype(F32)).astype(BF16)
        o_ref[:, k:] = (dact * du_buf[slot].astype(F32)).astype(BF16)

    any_space = pl.BlockSpec(memory_space=pl.ANY)
    return _call(body, name=name, grid=(steps,),
                 in_specs=[_rows(tm, n), pl.BlockSpec((1, k, n), lambda i: (0, 0, 0)), any_space, any_space],
                 out_specs=_rows(tm, 2 * k), out_shape=_sds((m, 2 * k), BF16),
                 scratch_shapes=[pltpu.VMEM((slots, tm, k), BF16)] * 2 + [pltpu.SemaphoreType.DMA((2, slots))])(
                     dffn, w_down, act_da, act_du)


def _down_loss(x2, act, w_down, gate, fg, tgt, name, tr=512):
    t = x2.shape[0]
    _, k, n = w_down.shape

    def body(x_ref, a_ref, w_ref, gt_ref, fg_ref, t_ref, dx_ref, df_ref, l_ref, dfg_ref, dgt_ref):
        first = pl.program_id(0) == 0
        ffn_v = _dot(a_ref[...], w_ref[0])
        x3 = x_ref[...] + gt_ref[...] * ffn_v
        r = _rms(x3)
        xh = x3 * r
        err = xh * fg_ref[...] - t_ref[...]
        dy = err * (1.0 / D_MODEL)
        dx3 = _rms_bwd(dy * fg_ref[...], xh, r)
        dx_ref[...] = dx3
        df_ref[...] = (dx3 * gt_ref[...]).astype(BF16)
        _acc(l_ref, jnp.sum(err * err, axis=0, keepdims=True), first)
        _acc(dfg_ref, jnp.sum(dy * xh, axis=0, keepdims=True), first)
        _acc(dgt_ref, jnp.sum(dx3 * ffn_v, axis=0, keepdims=True), first)

    row, vec = _rows(tr, D_MODEL), _vec(D_MODEL)
    return _call(body, name=name, grid=(t // tr,),
                 in_specs=[row, _rows(tr, k), pl.BlockSpec((1, k, n), lambda i: (0, 0, 0)), vec, vec, row],
                 out_specs=[row, row, vec, vec, vec],
                 out_shape=[_sds((t, D_MODEL)), _sds((t, D_MODEL), BF16)] + [_sds((1, D_MODEL))] * 3)(
                     x2, act, w_down, gate, fg, tgt)


def _norm_mod_bwd(dh, x, g, scale, dres, name, gate=None, mix=None, w=None, tr=512):
    t = x.shape[0]
    below = gate is not None

    def body(*refs):
        if w is not None:
            w_ref, w_full, sem, refs = refs[1], refs[-2], refs[-1], refs[:1] + refs[2:-2]

            @pl.when(pl.program_id(0) == 0)
            def _():
                n = w.shape[2]
                copies = [pltpu.make_async_copy(w_ref.at[j], w_full.at[:, pl.ds(j * n, n)], sem.at[j])
                          for j in range(w.shape[0])]
                for cp in copies:
                    cp.start()
                for cp in copies:
                    cp.wait()

        if below:
            dh_ref, x_ref, g_ref, sc_ref, dr_ref, gt_ref, m_ref, dx_ref, dsh_ref, dsc_ref, dg_ref, dgt_ref, dm_ref = refs
        else:
            dh_ref, x_ref, g_ref, sc_ref, dr_ref, dx_ref, dsh_ref, dsc_ref, dg_ref = refs
        first = pl.program_id(0) == 0
        xv = x_ref[...]
        if w is None:
            dhv = dh_ref[...].astype(F32)
        else:
            dhv = _dot_nt(dh_ref[...], w_full[...])
        r = _rms(xv)
        xh = xv * r
        dn = dhv * (1.0 + sc_ref[...])
        dx = dr_ref[...] + _rms_bwd(dn * g_ref[...], xh, r)
        dx_ref[...] = dx
        _acc(dsh_ref, jnp.sum(dhv, axis=0, keepdims=True), first)
        _acc(dsc_ref, jnp.sum(dhv * xh * g_ref[...], axis=0, keepdims=True), first)
        _acc(dg_ref, jnp.sum(dn * xh, axis=0, keepdims=True), first)
        if below:
            _acc(dgt_ref, jnp.sum(dx * m_ref[...], axis=0, keepdims=True), first)
            dm_ref[...] = (dx * gt_ref[...]).astype(BF16)

    row, vec = _rows(tr, D_MODEL), _vec(D_MODEL)
    first_specs = [row] if w is None else [_rows(tr, dh.shape[1]), pl.BlockSpec(memory_space=pl.ANY)]
    scratch = [] if w is None else [pltpu.VMEM((w.shape[1], dh.shape[1]), BF16), pltpu.SemaphoreType.DMA((w.shape[0],))]
    in_specs = first_specs + [row, vec, vec, row] + ([vec, row] if below else [])
    out_specs = [row, vec, vec, vec] + ([vec, row] if below else [])
    out_shape = [_sds((t, D_MODEL))] + [_sds((1, D_MODEL))] * 3 + ([_sds((1, D_MODEL)), _sds((t, D_MODEL), BF16)] if below else [])
    args = ((dh,) if w is None else (dh, w)) + (x, g, scale, dres) + ((gate, mix) if below else ())
    return _call(body, name=name, grid=(t // tr,), in_specs=in_specs, out_specs=out_specs, out_shape=out_shape,
                 scratch_shapes=scratch)(*args)


def _mix_in_bwd(dmix, w_out, o_hg, proj, att, og, ag, fine, name, tr=512):
    t = o_hg.shape[0]

    def body(dy_ref, w_ref, o_ref, g_ref, a_ref, og_ref, ag_ref,
             do_ref, dg_ref, da_ref, dd_ref, das_ref, dds_ref, dog_ref, dag_ref, to_sub):
        first = pl.program_id(0) == 0

        @pl.when(first)
        def _():
            to_sub[...] = _regroup_matrix(tr, tr // fine)

        dmi = _dot_nt(dy_ref[...], w_ref[0])
        dog = jnp.zeros((1, HG_HEAD), F32)
        for h in range(HG_WIDTH // HG_HEAD):
            sl = slice(h * HG_HEAD, (h + 1) * HG_HEAD)
            oh = o_ref[:, sl].astype(F32)
            gv = g_ref[:, sl].astype(F32)
            dv = dmi[:, sl]
            r = _rms(oh)
            xh = oh * r
            sg = _sigmoid(gv)
            dno = dv * gv * sg
            dg_ref[:, sl] = (dv * xh * og_ref[...] * sg * (1.0 + gv * (1.0 - sg))).astype(BF16)
            dog = dog + jnp.sum(dno * xh, axis=0, keepdims=True)
            do_ref[:, sl] = _rms_bwd(dno * og_ref[...], xh, r).astype(BF16)
        _acc(dog_ref, dog, first)
        av = a_ref[...]
        dav = dmi[:, HG_WIDTH:]
        r = _rms(av)
        xa = av * r
        _acc(dag_ref, jnp.sum(dav * xa, axis=0, keepdims=True), first)
        datt = _rms_bwd(dav * ag_ref[...], xa, r)
        da_ref[...] = datt.astype(BF16)
        das_ref[...] = _regroup(to_sub[...], datt.astype(BF16)).astype(BF16).reshape(das_ref.shape)
        prod = datt * av
        lane = lax.broadcasted_iota(jnp.int32, (1, 128), 1)
        dd = jnp.zeros((tr, 128), F32)
        for hp in range(ATT_HEADS // 2):
            pp = prod[:, hp * 128:(hp + 1) * 128]
            lo = jnp.sum(jnp.where(lane < 64, pp, 0.0), axis=-1, keepdims=True)
            hi = jnp.sum(jnp.where(lane >= 64, pp, 0.0), axis=-1, keepdims=True)
            dd = jnp.where(lane == 2 * hp, lo, dd)
            dd = jnp.where(lane == 2 * hp + 1, hi, dd)
        dd_ref[...] = dd
        dds_ref[...] = _regroup(to_sub[...], dd, ATT_HEADS).reshape(dds_ref.shape)

    half = _rows(tr, HG_WIDTH)
    out = _call(body, name=name, grid=(t // tr,),
                in_specs=[_rows(tr, D_MODEL), pl.BlockSpec(w_out.shape, lambda i: (0, 0, 0)), half,
                          pl.BlockSpec((tr, HG_WIDTH), lambda i: (i, 3)), half, _vec(HG_HEAD), _vec(ATT_WIDTH)],
                out_specs=[half, pl.BlockSpec((tr, HG_WIDTH), lambda i: (i, 3)), half, _rows(tr, 128),
                           _sub_rows(tr, fine, ATT_WIDTH), _sub_rows(tr, fine, 128), _vec(HG_HEAD), _vec(ATT_WIDTH)],
                out_shape=[_sds((t, HG_WIDTH), BF16), _sds((t, IN_WIDTH), BF16), _sds((t, HG_WIDTH), BF16),
                           _sds((t, 128)), _sds((fine, t // fine, ATT_WIDTH), BF16),
                           _sds((fine, t // fine, 128)), _sds((1, HG_HEAD)), _sds((1, ATT_WIDTH))],
                scratch_shapes=[pltpu.VMEM((tr, tr), BF16)])(dmix, w_out, o_hg, proj, att, og, ag)
    out = list(out)
    return out[:4] + [out[4].reshape(t, ATT_WIDTH), out[5].reshape(t, 128)] + out[6:]


def _dproj(dproj, dqkvs, name, tr=1024):
    t = dproj.shape[0]
    nbr = len(dqkvs)
    first = IN_WIDTH // ATT_WIDTH - 3

    def body(*refs):
        refs[-1][...] = sum(r[...].astype(F32) for r in refs[:nbr]).astype(BF16)

    return _call(body, name=name, grid=(t // tr, 3),
                 in_specs=[pl.BlockSpec((tr, ATT_WIDTH), lambda i, j: (i, j))] * nbr + [pl.BlockSpec(memory_space=pl.ANY)],
                 out_specs=pl.BlockSpec((tr, ATT_WIDTH), lambda i, j: (i, first + j)),
                 out_shape=_sds(dproj.shape, BF16), aliases={nbr: 0})(*dqkvs, dproj)


def _chunk_tri(upper):
    row = lax.broadcasted_iota(jnp.int32, (HG_GROUP, HG_CHUNK, HG_CHUNK), 1)
    col = lax.broadcasted_iota(jnp.int32, (HG_GROUP, HG_CHUNK, HG_CHUNK), 2)
    return (row <= col if upper else row >= col).astype(BF16)


def _chunk_cumsum(x, tri):
    x3 = x.reshape(HG_GROUP, HG_CHUNK, x.shape[1])
    dims = (((2,), (1,)), ((0,), (0,)))
    out = None
    for _ in range(3):
        part = x3.astype(BF16)
        x3 = x3 - part.astype(F32)
        term = lax.dot_general(tri, part, dims, preferred_element_type=F32)
        out = term if out is None else out + term
    return out.reshape(x.shape)


def _hg_gates(f_raw, q_raw, lb, tri):
    sg = _sigmoid(f_raw)
    f = lb + (1.0 - lb) * sg
    k = 1.0 - f
    b = _chunk_cumsum(jnp.log(f), tri)
    sq = _sigmoid(q_raw)
    return sg, f, k, b, sq


def _hg_masks(rows):
    row = lax.broadcasted_iota(jnp.int32, (rows, rows), 0)
    col = lax.broadcasted_iota(jnp.int32, (rows, rows), 1)
    same = (row // HG_CHUNK) == (col // HG_CHUNK)
    return jnp.logical_and(row >= col, same), jnp.logical_and(row <= col, same)


def _per_chunk(rows_of):
    return jnp.concatenate([jnp.broadcast_to(r, (HG_CHUNK, r.shape[1])) for r in rows_of], axis=0)


def _hgrn_fwd(proj, lb_logits, name):
    t = proj.shape[0]
    nc = t // HG_CHUNK
    nh = HG_WIDTH // HG_HEAD
    rows = HG_GROUP * HG_CHUNK

    def body(q_ref, f_ref, i_ref, lg_ref, o_ref, st_ref, s_scr):
        @pl.when(pl.program_id(0) == 0)
        def _():
            s_scr[...] = jnp.zeros_like(s_scr)

        lg = lg_ref[...]
        lb_all = _sigmoid(lg[0:1] - lg[1:2])
        causal, _ = _hg_masks(rows)
        tri = _chunk_tri(False)
        for h in range(nh):
            sl = slice(h * HG_HEAD, (h + 1) * HG_HEAD)
            q_raw = q_ref[:, sl].astype(F32)
            _, _, k, b, sq = _hg_gates(f_ref[:, sl].astype(F32), q_raw, lb_all[:, sl], tri)
            v = i_ref[:, sl].astype(BF16)
            gls = [b[(g + 1) * HG_CHUNK - 1:(g + 1) * HG_CHUNK] for g in range(HG_GROUP)]
            bm = _per_chunk([b[g * HG_CHUNK + HG_CHUNK // 2 - 1:g * HG_CHUNK + HG_CHUNK // 2] for g in range(HG_GROUP)])
            qd = (q_raw * sq * jnp.exp(b)).astype(BF16)
            qm = (q_raw * sq * jnp.exp(b - bm)).astype(BF16)
            km = (k * jnp.exp(bm - b)).astype(BF16)
            ke = (k * jnp.exp(_per_chunk(gls) - b)).astype(BF16)
            a = jnp.where(causal, _dot_nt(qm, km), 0.0).astype(BF16)
            o_intra = _dot(a, v)
            st = s_scr[h]
            o_inter = []
            for g in range(HG_GROUP):
                rs = slice(g * HG_CHUNK, (g + 1) * HG_CHUNK)
                st_ref[g, sl, :] = st
                o_inter.append(_dot_nt(qd[rs], st.astype(BF16)))
                st = st * jnp.exp(gls[g]) + _dot_tn(v[rs], ke[rs])
            s_scr[h] = st
            o_ref[:, sl] = (o_intra + jnp.concatenate(o_inter, axis=0)).astype(BF16)

    blk = lambda j: pl.BlockSpec((rows, HG_WIDTH), lambda c: (c, j))
    return _call(body, name=name, grid=(nc // HG_GROUP,),
                 in_specs=[blk(0), blk(1), blk(2), pl.BlockSpec((2, HG_WIDTH), lambda c: (0, 0))],
                 out_specs=[blk(0), pl.BlockSpec((HG_GROUP, HG_WIDTH, HG_HEAD), lambda c: (c, 0, 0))],
                 out_shape=[_sds((t, HG_WIDTH), BF16), _sds((nc, HG_WIDTH, HG_HEAD))],
                 scratch_shapes=[pltpu.VMEM((nh, HG_HEAD, HG_HEAD), F32)])(proj, proj, proj, lb_logits)


def _hgrn_bwd(proj, lb_logits, states, do, dproj, name):
    t = proj.shape[0]
    ng = t // (HG_GROUP * HG_CHUNK)
    nh = HG_WIDTH // HG_HEAD
    rows = HG_GROUP * HG_CHUNK

    def body(q_ref, f_ref, i_ref, lg_ref, st_ref, do_ref, _, d_ref, dlb_ref, ds_scr):
        first = pl.program_id(0) == 0

        @pl.when(first)
        def _():
            ds_scr[...] = jnp.zeros_like(ds_scr)

        lg = lg_ref[...]
        lb_all = _sigmoid(lg[0:1] - lg[1:2])
        causal, _ = _hg_masks(rows)
        tri = _chunk_tri(False)
        tri_t = _chunk_tri(True)
        dlb = []
        for h in range(nh):
            sl = slice(h * HG_HEAD, (h + 1) * HG_HEAD)
            q_raw = q_ref[:, sl].astype(F32)
            lb = lb_all[:, sl]
            sg, f, k, b, sq = _hg_gates(f_ref[:, sl].astype(F32), q_raw, lb, tri)
            v = i_ref[:, sl].astype(BF16)
            gls = [b[(g + 1) * HG_CHUNK - 1:(g + 1) * HG_CHUNK] for g in range(HG_GROUP)]
            bm = _per_chunk([b[g * HG_CHUNK + HG_CHUNK // 2 - 1:g * HG_CHUNK + HG_CHUNK // 2] for g in range(HG_GROUP)])
            eb = jnp.exp(b)
            ebm = jnp.exp(b - bm)
            emb = jnp.exp(bm - b)
            egb = jnp.exp(_per_chunk(gls) - b)
            ke = k * egb
            qd_b, qm_b = (q_raw * sq * eb).astype(BF16), (q_raw * sq * ebm).astype(BF16)
            km_b, ke_b = (k * emb).astype(BF16), ke.astype(BF16)
            dov = do_ref[:, sl].astype(BF16)
            a = jnp.where(causal, _dot_nt(qm_b, km_b), 0.0).astype(BF16)
            da = jnp.where(causal, _dot_nt(dov, v), 0.0).astype(BF16)
            dkm = _dot_tn(da, qm_b)
            dst = ds_scr[h]
            dqd_s, dv_s, dke_s, dgl_s = [None] * HG_GROUP, [None] * HG_GROUP, [None] * HG_GROUP, [None] * HG_GROUP
            for g in reversed(range(HG_GROUP)):
                rs = slice(g * HG_CHUNK, (g + 1) * HG_CHUNK)
                st = st_ref[g, sl, :]
                dst_b = dst.astype(BF16)
                egl = jnp.exp(gls[g])
                dqd_s[g] = _dot(dov[rs], st.astype(BF16))
                dv_s[g] = _dot_nt(ke_b[rs], dst_b)
                dke_s[g] = _dot(v[rs], dst_b)
                dgl_s[g] = jnp.sum(dst * st, axis=0, keepdims=True) * egl
                dst = _dot_tn(dov[rs], qd_b[rs]) + dst * egl
            ds_scr[h] = dst
            dqm = _dot(da, km_b)
            dqd = jnp.concatenate(dqd_s, axis=0)
            dv = _dot_tn(a, dov) + jnp.concatenate(dv_s, axis=0)
            dke = jnp.concatenate(dke_s, axis=0)
            t1 = dke * ke
            db = dqm * qm_b.astype(F32) - dkm * km_b.astype(F32) + dqd * qd_b.astype(F32) - t1
            dgl = _per_chunk([dgl_s[g] + jnp.sum(t1[g * HG_CHUNK:(g + 1) * HG_CHUNK], axis=0, keepdims=True)
                              for g in range(HG_GROUP)])
            dlf = _chunk_cumsum(db, tri_t) + dgl
            df = dlf / f - (dkm * emb + dke * egb)
            d_ref[:, sl] = ((dqm * ebm + dqd * eb) * sq * (1.0 + q_raw * (1.0 - sq))).astype(BF16)
            d_ref[:, HG_WIDTH + h * HG_HEAD:HG_WIDTH + (h + 1) * HG_HEAD] = (
                df * (1.0 - lb) * sg * (1.0 - sg)).astype(BF16)
            d_ref[:, 2 * HG_WIDTH + h * HG_HEAD:2 * HG_WIDTH + (h + 1) * HG_HEAD] = dv.astype(BF16)
            dlb.append(jnp.sum(df * (1.0 - sg), axis=0, keepdims=True))
        _acc(dlb_ref, jnp.concatenate(dlb, axis=1), first)

    rev = lambda j: pl.BlockSpec((rows, HG_WIDTH), lambda c: (ng - 1 - c, j))
    return _call(body, name=name, grid=(ng,),
                 in_specs=[rev(0), rev(1), rev(2), pl.BlockSpec((2, HG_WIDTH), lambda c: (0, 0)),
                           pl.BlockSpec((HG_GROUP, HG_WIDTH, HG_HEAD), lambda c: (ng - 1 - c, 0, 0)), rev(0),
                           pl.BlockSpec(memory_space=pl.ANY)],
                 out_specs=[pl.BlockSpec((rows, 3 * HG_WIDTH), lambda c: (ng - 1 - c, 0)), _vec(HG_WIDTH)],
                 out_shape=[_sds(dproj.shape, BF16), _sds((1, HG_WIDTH))], aliases={6: 0},
                 scratch_shapes=[pltpu.VMEM((nh, HG_HEAD, HG_HEAD), F32)])(
                     proj, proj, proj, lb_logits, states, do, dproj)


def _to_sub(a, dil):
    t, w = a.shape
    return a if dil == 1 else a.reshape(t // dil, dil, w).transpose(1, 0, 2).reshape(t, w)


def _from_sub(a, dil):
    t, w = a.shape
    return a if dil == 1 else a.reshape(dil, t // dil, w).transpose(1, 0, 2).reshape(t, w)


def _att_bias(seg):
    def place(v):
        v = v % ATT_BLOCK
        return v if seg == 1 else seg * (v % (ATT_BLOCK // seg)) + v // (ATT_BLOCK // seg)

    row = lax.broadcasted_iota(jnp.int32, (2 * ATT_BLOCK, 2 * ATT_BLOCK), 0)
    col = lax.broadcasted_iota(jnp.int32, (2 * ATT_BLOCK, 2 * ATT_BLOCK), 1)
    qi, kj = place(row), place(col)
    prev = jnp.logical_and(col < ATT_BLOCK, kj >= qi)
    cur = jnp.logical_and(col >= ATT_BLOCK, kj <= qi)
    return jnp.stack([jnp.where(cur, 0.0, NEG), jnp.where(jnp.logical_or(prev, cur), 0.0, NEG)])


def _get(ref, sl):
    if len(ref.shape) == 2:
        return ref[:, sl]
    v = ref[:, :, sl]
    return v.reshape(ATT_BLOCK, v.shape[2])


def _put(ref, sl, val):
    if len(ref.shape) == 2:
        ref[:, sl] = val
    else:
        ref[:, :, sl] = val.reshape(ref.shape[0], ref.shape[1], val.shape[1])


def _att_spec(nb, dil, seg, width, col, back):
    bps = nb // dil

    def plain(n):
        return jnp.clip(n - back, 0, nb - 1), col

    def segmented(n):
        m = jnp.clip(n - back, 0, nb - 1)
        return 0, m // bps, m % bps, 0, col

    if seg == 1:
        return pl.BlockSpec((ATT_BLOCK, width), plain)
    return pl.BlockSpec((seg, None, None, ATT_BLOCK // seg, width), segmented)


def _att_shape(nb, dil, seg, width):
    t = nb * ATT_BLOCK
    return (t, width) if seg == 1 else (seg, dil, nb // dil, ATT_BLOCK // seg, width)


def _att_view(a, nb, dil, seg):
    return a.reshape(_att_shape(nb, dil, seg, a.shape[1]))


def _attn_fwd_block(q_ref, kc_ref, kp_ref, vc_ref, vp_ref, o_ref, l_ref, bias_ref, has_prev, lane0):
    bias = bias_ref[has_prev.astype(jnp.int32)]
    lane = lax.broadcasted_iota(jnp.int32, (1, 128), 1)
    lo = lane < 64
    nq = ATT_BLOCK
    lse_all = jnp.zeros((nq, 128), F32)
    for hp in range(ATT_HEADS // 2):
        sl = slice(hp * 128, (hp + 1) * 128)
        q2 = _get(q_ref, sl)
        zero = jnp.zeros_like(q2)
        q2 = q2 * 0.125
        qs = jnp.concatenate([jnp.where(lo, q2, zero), jnp.where(lo, zero, q2)], axis=0)
        kk = jnp.concatenate([_get(kp_ref, sl), _get(kc_ref, sl)], axis=0)
        vv = jnp.concatenate([_get(vp_ref, sl), _get(vc_ref, sl)], axis=0)
        s = _dot_nt(qs, kk) + bias
        mx = jnp.max(s, axis=-1, keepdims=True)
        p = jnp.exp(s - mx)
        l = jnp.sum(p, axis=-1, keepdims=True)
        o = _dot(p.astype(BF16), vv) * (1.0 / l)
        _put(o_ref, sl, jnp.where(lo, o[:nq], o[nq:]).astype(BF16))
        lse = mx + jnp.log(l)
        lse_all = jnp.where(lane == lane0 + 2 * hp, lse[:nq], lse_all)
        lse_all = jnp.where(lane == lane0 + 2 * hp + 1, lse[nq:], lse_all)
    _put(l_ref, slice(None), lse_all)


def _attn_fwd(branches, name):
    t = branches[0][0].shape[0]
    nb = t // ATT_BLOCK
    nbr = len(branches)

    def body(*refs):
        n = pl.program_id(0)
        biases = refs[7 * nbr:]

        @pl.when(n == 0)
        def _():
            for bias_ref, (_, _, seg) in zip(biases, branches):
                bias_ref[...] = _att_bias(seg)

        for i, (_, dil, seg) in enumerate(branches):
            _attn_fwd_block(*refs[5 * i:5 * i + 5], *refs[5 * nbr + 2 * i:5 * nbr + 2 * i + 2], biases[i],
                            (n % (nb // dil)) != 0, ATT_HEADS * i)

    in_specs, args, out_specs, out_shape = [], [], [], []
    for qkv, dil, seg in branches:
        c0 = qkv.shape[1] // ATT_WIDTH - 3
        in_specs += [_att_spec(nb, dil, seg, ATT_WIDTH, c0 + j, back) for j, back in [(0, 0), (1, 0), (1, 1), (2, 0), (2, 1)]]
        args += [_att_view(qkv, nb, dil, seg)] * 5
        out_specs += [_att_spec(nb, dil, seg, ATT_WIDTH, 0, 0), _att_spec(nb, dil, seg, 128, 0, 0)]
        out_shape += [_sds(_att_shape(nb, dil, seg, ATT_WIDTH), BF16), _sds(_att_shape(nb, dil, seg, 128))]
    out = _call(body, name=name, grid=(nb,), in_specs=in_specs, out_specs=out_specs, out_shape=out_shape,
                scratch_shapes=[pltpu.VMEM((2, 2 * ATT_BLOCK, 2 * ATT_BLOCK), F32)] * nbr)(*args)
    return [(out[2 * i].reshape(t, ATT_WIDTH), out[2 * i + 1].reshape(t, 128)) for i in range(nbr)]


def _combine_mix_in(os_, ls_, fine, o_hg, proj, og, ag, name, tr=512):
    t = os_[0].shape[0]
    nbr = len(os_)

    def body(*refs):
        o_refs, l_refs = refs[:nbr], refs[nbr:2 * nbr]
        oh_ref, g_ref, og_ref, ag_ref, a_ref, lt_ref, lts_ref, m_ref, to_natural, to_sub = refs[2 * nbr:]

        @pl.when(pl.program_id(0) == 0)
        def _():
            to_natural[...] = _regroup_matrix(tr, fine)
            to_sub[...] = _regroup_matrix(tr, tr // fine)

        lane = lax.broadcasted_iota(jnp.int32, (1, 128), 1)
        packed = l_refs[0][...] + _regroup(to_natural[...], sum(r[...] for r in l_refs[1:]).reshape(tr, 128))
        ls = [packed if i == 0 else pltpu.roll(packed, 128 - ATT_HEADS * i, 1) for i in range(nbr)]
        mx = functools.reduce(jnp.maximum, ls)
        tot = mx + jnp.log(sum(jnp.exp(l - mx) for l in ls))
        ws = [jnp.exp(l - tot) for l in ls]
        tot = jnp.where(lane < ATT_HEADS, tot, 0.0)
        lt_ref[...] = tot
        lts_ref[...] = _regroup(to_sub[...], tot).reshape(lts_ref.shape)
        o_vals = [o_refs[0]] + [_regroup(to_natural[...], r[...].reshape(tr, ATT_WIDTH)) for r in o_refs[1:]]
        pairs = []
        for hp in range(ATT_HEADS // 2):
            sl = slice(hp * 128, (hp + 1) * 128)
            acc = jnp.zeros((tr, 128), F32)
            for w, o in zip(ws, o_vals):
                wf = jnp.where(lane < 64, w[:, 2 * hp:2 * hp + 1], w[:, 2 * hp + 1:2 * hp + 2])
                acc = acc + wf * o[:, sl]
            pairs.append(acc)
        av = jnp.concatenate(pairs, axis=1)
        a_ref[...] = av
        m_ref[:, HG_WIDTH:] = (av * _rms(av) * ag_ref[...]).astype(BF16)
        for h in range(HG_WIDTH // HG_HEAD):
            sl = slice(h * HG_HEAD, (h + 1) * HG_HEAD)
            oh = oh_ref[:, sl].astype(F32)
            gv = g_ref[:, sl].astype(F32)
            m_ref[:, sl] = (oh * _rms(oh) * og_ref[...] * (gv * _sigmoid(gv))).astype(BF16)

    half = _rows(tr, ATT_WIDTH)
    sub = lambda a: a.reshape(fine, t // fine, a.shape[1])
    att, lse, lse_sub, mixin = _call(
        body, name=name, grid=(t // tr,),
        in_specs=[half] + [_sub_rows(tr, fine, ATT_WIDTH)] * (nbr - 1) + [_rows(tr, 128)] + [_sub_rows(tr, fine, 128)] * (nbr - 1)
        + [half, pl.BlockSpec((tr, HG_WIDTH), lambda i: (i, 3)), _vec(HG_HEAD), _vec(ATT_WIDTH)],
        out_specs=[half, _rows(tr, 128), _sub_rows(tr, fine, 128), _rows(tr, D_MODEL)],
        out_shape=[_sds((t, ATT_WIDTH)), _sds((t, 128)), _sds((fine, t // fine, 128)), _sds((t, D_MODEL), BF16)],
        scratch_shapes=[pltpu.VMEM((tr, tr), BF16)] * 2)(
            os_[0], *map(sub, os_[1:]), ls_[0], *map(sub, ls_[1:]), o_hg, proj, og, ag)
    return att, lse, lse_sub.reshape(t, 128), mixin


def _attn_bwd_block(q_ref, k_ref, v_ref, do_ref, l_ref, d_ref, out_ref, carry, prev, bias_ref, has_prev):
    w = ATT_WIDTH
    nq = ATT_BLOCK
    bias = bias_ref[has_prev.astype(jnp.int32)]
    lo = lax.broadcasted_iota(jnp.int32, (1, 128), 1) < 64
    lse, ddv = _get(l_ref, slice(None)), _get(d_ref, slice(None))
    for hp in range(ATT_HEADS // 2):
        sl = slice(hp * 128, (hp + 1) * 128)
        sk = slice(w + hp * 128, w + (hp + 1) * 128)
        sv = slice(2 * w + hp * 128, 2 * w + (hp + 1) * 128)
        q2, do2 = _get(q_ref, sl), _get(do_ref, sl)
        zero = jnp.zeros_like(q2)
        q2 = q2 * 0.125
        qs = jnp.concatenate([jnp.where(lo, q2, zero), jnp.where(lo, zero, q2)], axis=0)
        dos = jnp.concatenate([jnp.where(lo, do2, zero), jnp.where(lo, zero, do2)], axis=0)
        kc, vc = _get(k_ref, sl), _get(v_ref, sl)
        kk = jnp.concatenate([prev[:, sl], kc], axis=0)
        vv = jnp.concatenate([prev[:, sk], vc], axis=0)
        prev[:, sl] = kc
        prev[:, sk] = vc
        ls = jnp.concatenate([lse[:, 2 * hp:2 * hp + 1], lse[:, 2 * hp + 1:2 * hp + 2]], axis=0)
        dh = jnp.concatenate([ddv[:, 2 * hp:2 * hp + 1], ddv[:, 2 * hp + 1:2 * hp + 2]], axis=0)
        p = jnp.exp(_dot_nt(qs, kk) - ls + bias)
        ds = (p * (_dot_nt(dos, vv) - dh)).astype(BF16)
        dq = _dot(ds, kk) * 0.125
        dk = _dot_tn(ds, qs)
        dv = _dot_tn(p.astype(BF16), dos)
        _put(out_ref, sl, carry[:, sl].astype(BF16))
        _put(out_ref, sk, (carry[:, sk] + dk[:nq]).astype(BF16))
        _put(out_ref, sv, (carry[:, sv] + dv[:nq]).astype(BF16))
        carry[:, sl] = jnp.where(lo, dq[:nq], dq[nq:])
        carry[:, sk] = dk[nq:]
        carry[:, sv] = dv[nq:]


def _attn_bwd(branches, name):
    t = branches[0][0].shape[0]
    nb = t // ATT_BLOCK
    nbr = len(branches)
    w = ATT_WIDTH

    def body(*refs):
        ins, outs, carries, prevs = refs[:6 * nbr], refs[6 * nbr:7 * nbr], refs[7 * nbr:8 * nbr], refs[8 * nbr:9 * nbr]
        biases = refs[9 * nbr:]
        n = pl.program_id(0)

        @pl.when(n == 0)
        def _():
            for scratch in carries + prevs:
                scratch[...] = jnp.zeros_like(scratch)
            for bias_ref, branch in zip(biases, branches):
                bias_ref[...] = _att_bias(branch[5])

        @pl.when(n < nb)
        def _():
            for i, branch in enumerate(branches):
                dil, seg = branch[4:]
                _attn_bwd_block(*ins[6 * i:6 * i + 6], outs[i], carries[i], prevs[i], biases[i], (n % (nb // dil)) != 0)

        @pl.when(n == nb)
        def _():
            for i in range(nbr):
                _put(outs[i], slice(None), carries[i][...].astype(BF16))

    in_specs, args, out_specs, out_shape = [], [], [], []
    for qkv, dout, lse, dd, dil, seg in branches:
        c0 = qkv.shape[1] // w - 3
        in_specs += [_att_spec(nb, dil, seg, w, c0 + j, 0) for j in range(3)]
        in_specs += [_att_spec(nb, dil, seg, w, 0, 0), _att_spec(nb, dil, seg, 128, 0, 0), _att_spec(nb, dil, seg, 128, 0, 0)]
        args += [_att_view(a, nb, dil, seg) for a in [qkv] * 3 + [dout, lse, dd]]
        out_specs += [_att_spec(nb, dil, seg, 3 * w, 0, 1)]
        out_shape += [_sds(_att_shape(nb, dil, seg, 3 * w), BF16)]
    out = _call(body, name=name, grid=(nb + 1,), in_specs=in_specs, out_specs=out_specs, out_shape=out_shape,
                scratch_shapes=[pltpu.VMEM((ATT_BLOCK, 3 * w), F32)] * nbr + [pltpu.VMEM((ATT_BLOCK, 2 * w), BF16)] * nbr
                + [pltpu.VMEM((2, 2 * ATT_BLOCK, 2 * ATT_BLOCK), F32)] * nbr)(*args)
    return [o.reshape(t, 3 * w) for o in out]


def _local_step(x, tgt, mod, norm1_g, lb_logits, og, ag, norm2_g, fg, get_w, put_g, late=lambda a: a, project=None):
    shift1, scale1, gate1, shift2, scale2, gate2 = [mod[:, i * D_MODEL:(i + 1) * D_MODEL] for i in range(6)]
    fg = fg.reshape(1, D_MODEL)

    h1 = _norm_mod(x, norm1_g, scale1, shift1, "norm_mod1")
    if project is None:
        w_in = get_w("w_in", h1)
        proj = _mm_nn(h1, w_in, "mm_in", out_dtype=BF16)
    else:
        proj, w_in = project(h1)
    o_hg, states = _hgrn_fwd(proj, lb_logits, "hgrn_fwd")
    fine = DILATIONS[-1]
    layouts = [(d, 1 if d == 1 else fine // d) for d in DILATIONS]
    qkv_fine = _to_sub(proj, fine)
    qkvs = [proj if d == 1 else qkv_fine for d in DILATIONS]
    natural = lambda a, d: a if d == 1 else _from_sub(a, fine)
    outs = _attn_fwd([(q, d, seg) for q, (d, seg) in zip(qkvs, layouts)], "attn_fwd")
    att, lse, lse_fine, mixin = _combine_mix_in([o for o, _ in outs], [l for _, l in outs], fine,
                                                o_hg, proj, og, ag, "attn_combine_mix_in")
    w_out = get_w("w_out", mixin)
    mix, x2, h2 = _out_resid_norm_mod(x, mixin, w_out, gate1, norm2_g, scale2, shift2, "mm_out_resid_norm_mod2")
    w_gu = get_w("w_gu", h2)
    a_ff, u_ff, act = _mm_gate_up(h2, w_gu, "mm_gu")
    w_down = get_w("w_down", act)
    dx3, dffn, loss_v, dfg, dgate2 = _down_loss(x2, act, w_down, gate2, fg, tgt, "mm_down_loss")

    dffn = put_g("w_down", *_mm_tn(act, dffn, 1, "mm_down_dw", tm=x.shape[0], tk=256), dffn)
    dau = _mm_down_dx(dffn, w_down, a_ff, u_ff, "mm_down_dx")
    dau = put_g("w_gu", *_mm_tn(h2, dau, N_SHARD, "mm_gu_dw", tm=x.shape[0], tk=512), dau)
    dx2, dshift2, dscale2, dg2, dgate1, dmix = _norm_mod_bwd(
        dau, x2, norm2_g, scale2, dx3, "mm_gu_dx_norm_bwd", gate=gate1, mix=mix, w=w_gu)
    dmix = put_g("w_out", *_mm_tn(mixin, dmix, 1, "mm_out_dw", tm=x.shape[0], tk=512), dmix)
    do_hg, dproj, datt, dd, datt_fine, dd_fine, dog, dag = _mix_in_bwd(
        dmix, w_out, o_hg, proj, att, og, ag, fine, "mm_out_dx_mix_in_bwd")
    datts = _attn_bwd([(q,) + ((datt, lse, dd) if d == 1 else (datt_fine, lse_fine, dd_fine)) + (d, seg)
                       for q, (d, seg) in zip(qkvs, layouts)], "attn_bwd")
    dproj, dlb = _hgrn_bwd(proj, lb_logits, states, do_hg, dproj, "hgrn_bwd")
    dproj = late(dproj)
    dproj = _dproj(dproj, [natural(a, d) for a, d in zip(datts, DILATIONS)], "dproj")
    dproj = put_g("w_in", *_mm_tn(h1, dproj, N_SHARD, "mm_in_dw", tm=x.shape[0], tk=512, group=2), dproj)
    dx, dshift1, dscale1, dg1 = _norm_mod_bwd(dproj, x, norm1_g, scale1, dx2, "mm_in_dx_norm_bwd", w=w_in)

    stats = jnp.concatenate([loss_v, dfg, dg2, dg1, dlb, dag, dog,
                             dshift1, dscale1, dgate1, dshift2, dscale2, dgate2], axis=1)
    return dx, stats


def _place():
    x, y, c = lax.axis_index("x"), lax.axis_index("y"), lax.axis_index("c")
    return x, y, c


def _chip_peers(x, y, c):
    return [(1 - x, y, c), (x, 1 - y, c), (1 - x, 1 - y, c)]


_HBM = pl.BlockSpec(memory_space=pltpu.HBM)
_SEM = pl.BlockSpec(memory_space=pltpu.SEMAPHORE)
_EFFECT = pltpu.SideEffectType.DATAFLOW_SIDE_EFFECTING


def _exchange_copy(bufs, send, recv, j, peer, place, kind):
    x, y, c = place
    target = peer
    if kind == "gather":
        src = dst = bufs[0].at[2 * x + y]
    elif kind == "scatter":
        src, dst = bufs[0].at[2 * peer[0] + peer[1]], bufs[1].at[j]
    else:
        half = bufs[0].shape[1] // 2
        rows = pl.ds(c * half, half)
        if kind == "half":
            src = dst = bufs[0].at[2 * x + y, rows]
        else:
            src = dst = bufs[0].at[2 * peer[0] + peer[1], rows]
            target = (x, y, 1 - c)
    return pltpu.make_async_remote_copy(src_ref=src, dst_ref=dst, send_sem=send.at[j], recv_sem=recv.at[j],
                                        device_id=target, device_id_type=MESH)


def _exchange_start(groups, after, kind, name):
    sizes = [len(g) for g in groups]
    flat = [b for g in groups for b in g]
    ng, nb = len(groups), len(flat)

    def body(*refs):
        bufs, sems = refs[:nb], refs[nb + 1:nb + 1 + 2 * ng]
        x, y, c = _place()
        for j, peer in enumerate(_chip_peers(x, y, c)):
            at = 0
            for i, size in enumerate(sizes):
                _exchange_copy(bufs[at:at + size], sems[2 * i], sems[2 * i + 1], j, peer, (x, y, c), kind).start()
                at += size

    any_space = pl.BlockSpec(memory_space=pl.ANY)
    out = pl.pallas_call(
        body, name=name, in_specs=[_HBM] * nb + [any_space],
        out_specs=[_SEM] * (2 * ng) + [_HBM] * nb + [any_space],
        out_shape=[pltpu.SemaphoreType.DMA((3,))] * (2 * ng) + [pltpu.HBM(b.shape, b.dtype) for b in flat]
        + [_sds(after.shape, after.dtype)],
        input_output_aliases={i: 2 * ng + i for i in range(nb + 1)},
        compiler_params=pltpu.CompilerParams(has_side_effects=_EFFECT),
    )(*[pltpu.with_memory_space_constraint(b, pltpu.HBM) for b in flat], after)
    started, at = [], 2 * ng
    for i, size in enumerate(sizes):
        started.append((out[2 * i], out[2 * i + 1], tuple(out[at:at + size])))
        at += size
    return started, out[-1]


def _exchange_wait(started, after, kind, name):
    send, recv, bufs = started
    nb = len(bufs)

    def body(*refs):
        x, y, c = _place()
        for j, peer in enumerate(_chip_peers(x, y, c)):
            cp = _exchange_copy(refs[:nb], refs[nb], refs[nb + 1], j, peer, (x, y, c), kind)
            cp.wait_send()
            cp.wait_recv()

    return pl.pallas_call(
        body, name=name, in_specs=[_HBM] * nb + [_SEM, _SEM, pl.BlockSpec(memory_space=pl.ANY)],
        out_specs=[_HBM] * nb, out_shape=[pltpu.HBM(b.shape, b.dtype) for b in bufs],
        input_output_aliases={i: i for i in range(nb)},
        compiler_params=pltpu.CompilerParams(has_side_effects=_EFFECT),
    )(*bufs, send, recv, after)


def _sibling_copies(v_refs, l_refs, send, recv):
    x, y, c = _place()
    return [pltpu.make_async_remote_copy(src_ref=v, dst_ref=l, send_sem=send.at[a], recv_sem=recv.at[a],
                                         device_id=(x, y, 1 - c), device_id_type=MESH)
            for a, (v, l) in enumerate(zip(v_refs, l_refs))]


def _sibling_start(vs, after, name):
    vs = list(vs)
    n = len(vs)
    lands = [lax.empty(v.shape, v.dtype) for v in vs]

    def body(*refs):
        for cp in _sibling_copies(refs[:n], refs[n:2 * n], refs[2 * n + 1], refs[2 * n + 2]):
            cp.start()

    any_space = pl.BlockSpec(memory_space=pl.ANY)
    out = pl.pallas_call(
        body, name=name, in_specs=[_HBM] * (2 * n) + [any_space],
        out_specs=[_SEM, _SEM] + [_HBM] * (2 * n) + [any_space],
        out_shape=[pltpu.SemaphoreType.DMA((n,))] * 2 + [pltpu.HBM(b.shape, b.dtype) for b in vs + lands]
        + [_sds(after.shape, after.dtype)],
        input_output_aliases={i: 2 + i for i in range(2 * n + 1)},
        compiler_params=pltpu.CompilerParams(has_side_effects=_EFFECT),
    )(*[pltpu.with_memory_space_constraint(b, pltpu.HBM) for b in vs + lands], after)
    return (out[0], out[1], tuple(out[2:2 + n]), tuple(out[2 + n:2 + 2 * n])), out[-1]


def _sibling_wait(started, after, name):
    send, recv, vs, lands = started
    n = len(vs)

    def body(*refs):
        for cp in _sibling_copies(refs[:n], refs[n:2 * n], refs[2 * n], refs[2 * n + 1]):
            cp.wait_send()
            cp.wait_recv()

    out = pl.pallas_call(
        body, name=name, in_specs=[_HBM] * (2 * n) + [_SEM, _SEM, pl.BlockSpec(memory_space=pl.ANY)],
        out_specs=[_HBM] * (2 * n), out_shape=[pltpu.HBM(b.shape, b.dtype) for b in vs + lands],
        input_output_aliases={i: i for i in range(2 * n)},
        compiler_params=pltpu.CompilerParams(has_side_effects=_EFFECT),
    )(*vs, *lands, send, recv, after)
    return out[:n], out[n:]


def _everyone(x, y, c):
    return [(1 - x if k & 4 else x, 1 - y if k & 2 else y, 1 - c if k & 1 else c) for k in range(1, 8)]


def _all_gather_copies(land_ref, send, recv, arriving):
    x, y, c = _place()
    me = 4 * x + 2 * y + c
    return [pltpu.make_async_remote_copy(
        src_ref=land_ref.at[me], dst_ref=land_ref.at[4 * p[0] + 2 * p[1] + p[2] if arriving else me],
        send_sem=send.at[k], recv_sem=recv.at[k], device_id=p, device_id_type=MESH)
        for k, p in enumerate(_everyone(x, y, c))]


def _all_gather_start(v, name):
    x, y, c = _place()
    land = lax.dynamic_update_slice(lax.empty((8,) + v.shape, v.dtype), v[None], (4 * x + 2 * y + c, 0, 0))

    def body(land_ref, v_ref, send, recv, land_out, v_out):
        for cp in _all_gather_copies(land_ref, send, recv, False):
            cp.start()

    any_space = pl.BlockSpec(memory_space=pl.ANY)
    out = pl.pallas_call(
        body, name=name, in_specs=[_HBM, any_space], out_specs=[_SEM, _SEM, _HBM, any_space],
        out_shape=[pltpu.SemaphoreType.DMA((7,))] * 2 + [pltpu.HBM(land.shape, land.dtype), _sds(v.shape, v.dtype)],
        input_output_aliases={0: 2, 1: 3}, compiler_params=pltpu.CompilerParams(has_side_effects=_EFFECT),
    )(pltpu.with_memory_space_constraint(land, pltpu.HBM), v)
    return tuple(out[:3]), out[3]


def _all_gather_wait(started, after, name):
    send, recv, land = started

    def body(land_ref, send, recv, after_ref, land_out):
        for cp in _all_gather_copies(land_ref, send, recv, True):
            cp.wait_send()
            cp.wait_recv()

    return pl.pallas_call(
        body, name=name, in_specs=[_HBM, _SEM, _SEM, pl.BlockSpec(memory_space=pl.ANY)], out_specs=_HBM,
        out_shape=pltpu.HBM(land.shape, land.dtype), input_output_aliases={0: 0},
        compiler_params=pltpu.CompilerParams(has_side_effects=_EFFECT),
    )(land, send, recv, after)


def _cast_place(ws, shard, name, after=()):
    n = len(ws)

    def body(s_ref, *refs):
        for w_ref, o_ref in zip(refs[:n], refs[-n:]):
            o_ref[0] = w_ref[...].astype(BF16)

    return pl.pallas_call(
        body, name=name, out_shape=[_sds((N_SHARD,) + w.shape, BF16) for w in ws],
        grid_spec=pltpu.PrefetchScalarGridSpec(
            num_scalar_prefetch=1, grid=(4,),
            in_specs=[pl.BlockSpec((w.shape[0] // 4, w.shape[1]), lambda i, s: (i, 0)) for w in ws]
            + [pl.BlockSpec(memory_space=pl.ANY)] * len(after),
            out_specs=[pl.BlockSpec((1, w.shape[0] // 4, w.shape[1]), lambda i, s: (s[0], i, 0)) for w in ws]),
        compiler_params=pltpu.CompilerParams(dimension_semantics=("arbitrary",), vmem_limit_bytes=VMEM_LIMIT),
    )(shard.reshape(1).astype(jnp.int32), *ws, *after)


def _mod_rows(c8, w_ada, b_ada, name):
    n = w_ada.shape[1]

    def gather(src_ref, dst_ref, send, recv, loc, base):
        x, y, c = _place()
        me = 4 * x + 2 * y + c
        own = pltpu.make_async_copy(src_ref, dst_ref.at[me], loc)
        own.start()
        peers = _everyone(x, y, c)
        sends = [pltpu.make_async_remote_copy(src_ref=src_ref, dst_ref=dst_ref.at[me], send_sem=send.at[base + k],
                                              recv_sem=recv.at[base + k], device_id=p, device_id_type=MESH)
                 for k, p in enumerate(peers)]
        for cp in sends:
            cp.start()
        for k, p in enumerate(peers):
            pltpu.make_async_remote_copy(src_ref=src_ref, dst_ref=dst_ref.at[4 * p[0] + 2 * p[1] + p[2]],
                                         send_sem=send.at[base + k], recv_sem=recv.at[base + k], device_id=p,
                                         device_id_type=MESH).wait_recv()
        for cp in sends:
            cp.wait_send()
        own.wait()

    def body(c_ref, w_ref, b_ref, a_ref, parts_ref, c_all, part, send, recv, loc):
        gather(c_ref, c_all, send, recv, loc.at[0], 0)
        cv = jnp.max(c_all[...], axis=1)
        ca = cv * _sigmoid(cv)
        a_ref[...] = ca
        part[...] = jnp.dot(ca, w_ref[...], precision=lax.Precision.HIGHEST, preferred_element_type=F32) + b_ref[...]
        gather(part, parts_ref, send, recv, loc.at[1], 7)

    vmem = pl.BlockSpec(memory_space=pltpu.VMEM)
    return pl.pallas_call(
        body, name=name, in_specs=[vmem] * 3, out_specs=[vmem, vmem],
        out_shape=[_sds((8, D_MODEL)), _sds((8, 8, n))],
        scratch_shapes=[pltpu.VMEM((8, 8, D_MODEL), F32), pltpu.VMEM((8, n), F32), pltpu.SemaphoreType.DMA((14,)),
                        pltpu.SemaphoreType.DMA((14,)), pltpu.SemaphoreType.DMA((2,))],
        compiler_params=pltpu.CompilerParams(vmem_limit_bytes=VMEM_LIMIT))(c8, w_ada, b_ada)


def _sum_received(gs, shard, lands, name):
    n = len(gs)

    def body(s_ref, *refs):
        for g_ref, l_ref, o_ref in zip(refs[:n], refs[n:2 * n], refs[2 * n:]):
            o_ref[...] = ((g_ref[0] + l_ref[0].astype(F32)) + l_ref[1].astype(F32)) + l_ref[2].astype(F32)

    quarter = lambda g: (g.shape[1] // 4, g.shape[2])
    return pl.pallas_call(
        body, name=name, out_shape=[_sds(g.shape[1:]) for g in gs],
        grid_spec=pltpu.PrefetchScalarGridSpec(
            num_scalar_prefetch=1, grid=(4,),
            in_specs=[pl.BlockSpec((1,) + quarter(g), lambda i, s: (s[0], i, 0)) for g in gs]
            + [pl.BlockSpec((3,) + quarter(g), lambda i, s: (0, i, 0)) for g in gs],
            out_specs=[pl.BlockSpec(quarter(g), lambda i, s: (i, 0)) for g in gs]),
        compiler_params=pltpu.CompilerParams(dimension_semantics=("arbitrary",), vmem_limit_bytes=VMEM_LIMIT),
    )(shard.reshape(1).astype(jnp.int32), *gs, *lands)


def _adamw_outer(w, ct, dm, m, v, name):
    k, n = w.shape
    tr = k // 4

    def body(w_ref, c_ref, d_ref, m_ref, v_ref, g_out, d_out, m_out, v_out):
        cv = c_ref[...]
        dv = d_ref[...]
        g = cv[:, 0:1] * dv[0:1, :]
        for i in range(1, 8):
            g = g + cv[:, i:i + 1] * dv[i:i + 1, :]
        g_out[...] = g
        d_out[...], m_out[...], v_out[...] = _adamw_math(w_ref[...], g, m_ref[...], v_ref[...])

    row = _rows(tr, n)
    return _call(body, name=name, grid=(4,),
                 in_specs=[row, _rows(tr, 8), pl.BlockSpec((8, n), lambda i: (0, 0)), row, row],
                 out_specs=[row] * 4, out_shape=[_sds((k, n))] * 4)(w, ct, dm, m, v)


def _adamw_math(w, g, m, v):
    m_new = ADAM_B1 * m + (1.0 - ADAM_B1) * g
    v_new = ADAM_B2 * v + (1.0 - ADAM_B2) * (g * g)
    m_hat = m_new / (1.0 - ADAM_B1 ** ADAM_STEP)
    v_hat = v_new / (1.0 - ADAM_B2 ** ADAM_STEP)
    return -ADAM_LR * (m_hat / (jnp.sqrt(v_hat) + ADAM_EPS) + ADAM_WD * w), m_new, v_new


def _small_update(stats, smalls, name):
    offsets = [ST_DMOD, ST_DG1, ST_DLB, ST_DOG, ST_DAG, ST_DG2, ST_DFG]
    lb_index = 2

    def body(*refs):
        s_ref, ins, l_ref, outs = refs[0], refs[1:22], refs[22], refs[23:]
        tot = s_ref[0:1, :]
        for i in range(1, 8):
            tot = tot + s_ref[i:i + 1, :]
        l_ref[...] = jnp.zeros((1, 128), F32) + (0.5 / D_MODEL) * jnp.sum(tot[:, ST_LOSS:ST_LOSS + D_MODEL])
        for p, off in enumerate(offsets):
            w_ref, m_ref, v_ref = ins[3 * p:3 * p + 3]
            g_out, d_out, m_out, v_out = outs[4 * p:4 * p + 4]
            g = tot[:, off:off + w_ref.shape[1]]
            if p == lb_index:
                lg = w_ref[...]
                lb = _sigmoid(lg[0:1] - lg[1:2])
                g = g * lb * (1.0 - lb)
            for r in range(w_ref.shape[0]):
                rows = slice(r, r + 1)
                gr = g if r == 0 else -g
                delta, m_new, v_new = _adamw_math(w_ref[rows, :], gr, m_ref[rows, :], v_ref[rows, :])
                g_out[rows, :] = gr
                d_out[rows, :] = delta
                m_out[rows, :] = m_new
                v_out[rows, :] = v_new

    full = lambda a: pl.BlockSpec(a.shape, lambda i: (0, 0))
    flat = [a for t in smalls for a in t]
    return _call(body, name=name, grid=(1,),
                 in_specs=[full(stats)] + [full(a) for a in flat],
                 out_specs=[pl.BlockSpec((1, 128), lambda i: (0, 0))] + [full(t[0]) for t in smalls for _ in range(4)],
                 out_shape=[_sds((1, 128))] + [_sds(t[0].shape) for t in smalls for _ in range(4)])(stats, *flat)


def _adamw(params, name):
    n = len(params)

    def body(*refs):
        for p in range(n):
            w_ref, ga_ref, gb_ref, m_ref, v_ref = refs[5 * p:5 * p + 5]
            g_out, d_out, m_out, v_out = refs[5 * n + 4 * p:5 * n + 4 * p + 4]
            g = ga_ref[...] + gb_ref[...]
            g_out[...] = g
            d_out[...], m_out[...], v_out[...] = _adamw_math(w_ref[...], g, m_ref[...], v_ref[...])

    row = lambda w: _rows(w.shape[0] // 4, w.shape[1])
    out = _call(body, name=name, grid=(4,), in_specs=[row(p[0]) for p in params for _ in range(5)],
                out_specs=[row(p[0]) for p in params for _ in range(4)],
                out_shape=[_sds(p[0].shape) for p in params for _ in range(4)])(*[a for p in params for a in p])
    return [tuple(out[4 * p:4 * p + 4]) for p in range(n)]


def kernel(x, c, w_ada, b_ada, norm1_g, w_in, hg_lb_logits, hg_onorm_g, att_onorm_g, w_out, norm2_g, w_gate_up, w_down, final_g, loss_target, m_w_ada, m_b_ada, m_norm1_g, m_w_in, m_hg_lb_logits, m_hg_onorm_g, m_att_onorm_g, m_w_out, m_norm2_g, m_w_gate_up, m_w_down, m_final_g, v_w_ada, v_b_ada, v_norm1_g, v_w_in, v_hg_lb_logits, v_hg_onorm_g, v_att_onorm_g, v_w_out, v_norm2_g, v_w_gate_up, v_w_down, v_final_g):
    ix, iy, ic = _place()
    shard = 2 * ix + iy
    sample = 4 * ix + 2 * iy + ic
    n_ada = w_ada.shape[2]

    shards = [w_in[0], w_out[0], w_gate_up[0], w_down[0]]
    names = ["w_in", "w_out", "w_gu", "w_down"]
    shapes = [(N_SHARD,) + w.shape for w in shards]
    placed = [(_cast_place(shards[:1], shard, "place_w_in")[0],)]

    b_part = lax.dynamic_slice(b_ada, (0, shard * n_ada), (1, n_ada))
    c_act, parts = _mod_rows(jnp.broadcast_to(c, (8, D_MODEL)), w_ada[0], b_part, "mod_rows")
    parts = parts[::2]
    mod = lax.dynamic_index_in_dim(parts, sample, axis=1, keepdims=False).reshape(1, 6 * D_MODEL)
    (first,), mod = _exchange_start(placed[:1], mod, "half", "gather_start_w_in")
    gathering = {}

    def get_w(name, after):
        if name == "w_in":
            placed_rest = [(p,) for p in _cast_place(shards[1:], shard, "place_rest", (after,))]
            halves = _exchange_wait(first, placed_rest[0][0], "half", "gather_wait_w_in")
            (passing,), token = _exchange_start([tuple(halves)], mod, "forward", "forward_start_w_in")
            rest, token = _exchange_start(placed_rest, token, "gather", "gather_start_rest")
            (full,) = _exchange_wait(passing, token, "forward", "forward_wait_w_in")
            gathering.update(zip(names[1:], rest))
            return full
        (full,) = _exchange_wait(gathering[name], after, "gather", "gather_wait_" + name)
        return full if name == "w_gu" else full.reshape(1, -1, D_MODEL)

    scattering = {}

    def put_g(name, g, g_bf16, then):
        shape = shapes[names.index(name)]
        land = lax.empty((3,) + shape[1:], BF16)
        (started,), then = _exchange_start([(g_bf16.reshape(shape), land)], then, "scatter", "scatter_start_" + name)
        scattering[name] = (g.reshape(shape), started)
        return then

    def summed(group, after, tag):
        lands = [_exchange_wait(scattering[nm][1], after, "scatter", "scatter_wait_" + nm)[1] for nm in group]
        return _sum_received([scattering[nm][0] for nm in group], shard, lands, "sum_" + tag)

    early = ["w_down", "w_gu", "w_out"]
    swapping = []

    def late(a):
        started, a = _sibling_start(summed(early, a, "early"), a, "swap_start")
        swapping.append(started)
        return a

    def project(h1):
        own = _mm_own_shard(h1, shards[0], shard, N_SHARD, "mm_in_own")
        w_full = get_w("w_in", own)
        return _mm_other_shards(h1, w_full, own, shard, "mm_in_rest"), w_full

    dx, stats = _local_step(x[0], loss_target[0], mod, norm1_g, hg_lb_logits, hg_onorm_g, att_onorm_g,
                            norm2_g, final_g, get_w, put_g, late, project)

    gathering_stats, stats = _all_gather_start(stats, "stats_start")
    moments = [(m_w_in, v_w_in), (m_w_out, v_w_out), (m_w_gate_up, v_w_gate_up), (m_w_down, v_w_down)]

    def update(group, sums, other, tag):
        params = [(shards[names.index(nm)], s, o, moments[names.index(nm)][0][0], moments[names.index(nm)][1][0])
                  for nm, s, o in zip(group, sums, other)]
        return dict(zip(group, _adamw(params, "adamw_" + tag)))

    sums, other = _sibling_wait(swapping[0], stats, "swap_wait")
    done = update(early, sums, other, "early")
    swapping_in, stats = _sibling_start(summed(["w_in"], done["w_out"][1], "w_in"), stats, "swap_start_w_in")

    stats_all = _all_gather_wait(gathering_stats, stats, "stats_wait").reshape(8, ST_WIDTH)
    dmod = lax.dynamic_slice(stats_all, (0, ST_DMOD + shard * n_ada), (8, n_ada))

    as_row = lambda a: a.reshape(1, -1) if a.ndim == 1 else a
    smalls = [tuple(as_row(a) for a in t) for t in [
        (b_ada, m_b_ada, v_b_ada), (norm1_g, m_norm1_g, v_norm1_g),
        (hg_lb_logits, m_hg_lb_logits, v_hg_lb_logits), (hg_onorm_g, m_hg_onorm_g, v_hg_onorm_g),
        (att_onorm_g, m_att_onorm_g, v_att_onorm_g), (norm2_g, m_norm2_g, v_norm2_g),
        (final_g, m_final_g, v_final_g)]]
    loss, *small_out = _small_update(stats_all, smalls, "small_update")
    shapes_out = [b_ada.shape, norm1_g.shape, hg_lb_logits.shape, hg_onorm_g.shape, att_onorm_g.shape,
                  norm2_g.shape, final_g.shape]
    sg, sd, sm, sv = [[small_out[4 * p + i].reshape(shapes_out[p]) for p in range(7)] for i in range(4)]

    ada = _adamw_outer(w_ada[0], c_act.T, dmod, m_w_ada[0], v_w_ada[0], "adamw_w_ada")
    sum_in, other_in = _sibling_wait(swapping_in, ada[1], "swap_wait_w_in")
    done.update(update(["w_in"], sum_in, other_in, "w_in"))
    big = [ada] + [done[nm] for nm in names]
    bg, bd, bm, bv = [[t[i][None] for t in big] for i in range(4)]

    def order(b, s):
        return [b[0], s[0], s[1], b[1], s[2], s[3], s[4], b[2], s[5], b[3], b[4], s[6]]

    return (loss[0, 0], dx[None], *order(bg, sg), *order(bd, sd), *order(bm, sm), *order(bv, sv))
```

```python
import functools

import jax
import jax.numpy as jnp
from jax import lax
from jax.experimental import pallas as pl
from jax.experimental.pallas import tpu as pltpu

F32 = jnp.float32
BF16 = jnp.bfloat16
MESH = pl.DeviceIdType.MESH

D_MODEL = 1024
HG_WIDTH = 512
HG_HEAD = 128
HG_CHUNK = 64
HG_GROUP = 4
ATT_WIDTH = 512
ATT_HEADS = 8
ATT_BLOCK = 128
DILATIONS = (1, 4, 16)
D_FF = 2816
IN_WIDTH = 3584
N_SHARD = 4
RMS_EPS = 1e-6
NEG = -1e30

ADAM_LR = 0.001
ADAM_B1 = 0.9
ADAM_B2 = 0.999
ADAM_EPS = 1e-08
ADAM_WD = 0.01
ADAM_STEP = 10

VMEM_LIMIT = 56 * 2**20

ST_LOSS, ST_DFG, ST_DG2, ST_DG1 = 0, 1024, 2048, 3072
ST_DLB, ST_DAG, ST_DOG, ST_DMOD = 4096, 4608, 5120, 5248
ST_WIDTH = 5248 + 6144


def _call(body, *, name, grid, in_specs, out_specs, out_shape, scratch_shapes=(), aliases=None):
    return pl.pallas_call(
        body, name=name, grid=grid, in_specs=in_specs, out_specs=out_specs, out_shape=out_shape,
        scratch_shapes=list(scratch_shapes), input_output_aliases=aliases or {},
        compiler_params=pltpu.CompilerParams(
            dimension_semantics=("arbitrary",) * len(grid), vmem_limit_bytes=VMEM_LIMIT))


def _sds(shape, dtype=F32):
    return jax.ShapeDtypeStruct(shape, dtype)


def _dot(a, b):
    return jnp.dot(a, b, preferred_element_type=F32)


def _dot_nt(a, b):
    return lax.dot_general(a, b, (((1,), (1,)), ((), ())), preferred_element_type=F32)


def _dot_tn(a, b):
    return lax.dot_general(a, b, (((0,), (0,)), ((), ())), preferred_element_type=F32)


def _sigmoid(x):
    return 1.0 / (1.0 + jnp.exp(-x))


def _rows(tr, width):
    return pl.BlockSpec((tr, width), lambda i: (i, 0))


def _vec(width):
    return pl.BlockSpec((1, width), lambda i: (0, 0))


def _acc(ref, val, first):
    @pl.when(first)
    def _():
        ref[...] = val

    @pl.when(jnp.logical_not(first))
    def _():
        ref[...] += val


def _sub_rows(tr, fine, width):
    return pl.BlockSpec((fine, tr // fine, width), lambda i: (0, i, 0))


def _regroup_matrix(tr, groups):
    a = lax.broadcasted_iota(jnp.int32, (tr, tr), 0)
    b = lax.broadcasted_iota(jnp.int32, (tr, tr), 1)
    return (b == (a % groups) * (tr // groups) + a // groups).astype(BF16)


def _regroup(m, v, lanes=None):
    if v.dtype == BF16:
        return _dot(m, v)
    width = v.shape[1]
    packed, out = None, None
    for i in range(3):
        part = v.astype(BF16).astype(F32)
        v = v - part
        if lanes is None:
            out = _dot(m, part.astype(BF16)) if i == 0 else out + _dot(m, part.astype(BF16))
        else:
            packed = part if i == 0 else packed + pltpu.roll(part, i * lanes, 1)
    if lanes is None:
        return out
    out = _dot(m, packed.astype(BF16))
    out = out + pltpu.roll(out, width - lanes, 1) + pltpu.roll(out, width - 2 * lanes, 1)
    return jnp.where(lax.broadcasted_iota(jnp.int32, (1, width), 1) < lanes, out, 0.0)


def _mm_nn(a, b3, name, tm=1024, out_dtype=F32):
    m, k = a.shape
    s, _, n = b3.shape

    def body(a_ref, b_ref, o_ref):
        o_ref[...] = _dot(a_ref[...], b_ref[0]).astype(out_dtype)

    return _call(
        body, name=name, grid=(s, m // tm),
        in_specs=[pl.BlockSpec((tm, k), lambda j, i: (i, 0)), pl.BlockSpec((1, k, n), lambda j, i: (j, 0, 0))],
        out_specs=pl.BlockSpec((tm, n), lambda j, i: (i, j)), out_shape=_sds((m, s * n), out_dtype))(a, b3)


def _mm_own_shard(a, w, shard, s, name, tm=1024):
    m, k = a.shape
    n = w.shape[1]

    def body(s_ref, a_ref, w_ref, o_ref):
        o_ref[...] = _dot(a_ref[...], w_ref[...].astype(BF16)).astype(BF16)

    return pl.pallas_call(
        body, name=name, out_shape=_sds((m, s * n), BF16),
        grid_spec=pltpu.PrefetchScalarGridSpec(
            num_scalar_prefetch=1, grid=(m // tm,),
            in_specs=[pl.BlockSpec((tm, k), lambda i, sh: (i, 0)), pl.BlockSpec((k, n), lambda i, sh: (0, 0))],
            out_specs=pl.BlockSpec((tm, n), lambda i, sh: (i, sh[0]))),
        compiler_params=pltpu.CompilerParams(dimension_semantics=("arbitrary",), vmem_limit_bytes=VMEM_LIMIT),
    )(shard.reshape(1).astype(jnp.int32), a, w)


def _mm_other_shards(a, b3, partial, shard, name, tm=1024):
    m, k = a.shape
    s, _, n = b3.shape
    which = lambda j, sh: (sh[0] + 1 + j) % s

    def body(s_ref, a_ref, b_ref, p_ref, o_ref):
        o_ref[...] = _dot(a_ref[...], b_ref[0]).astype(BF16)

    return pl.pallas_call(
        body, name=name, out_shape=_sds(partial.shape, BF16),
        grid_spec=pltpu.PrefetchScalarGridSpec(
            num_scalar_prefetch=1, grid=(s - 1, m // tm),
            in_specs=[pl.BlockSpec((tm, k), lambda j, i, sh: (i, 0)),
                      pl.BlockSpec((1, k, n), lambda j, i, sh: (which(j, sh), 0, 0)),
                      pl.BlockSpec(memory_space=pl.ANY)],
            out_specs=pl.BlockSpec((tm, n), lambda j, i, sh: (i, which(j, sh)))),
        input_output_aliases={3: 0},
        compiler_params=pltpu.CompilerParams(dimension_semantics=("arbitrary",) * 2, vmem_limit_bytes=VMEM_LIMIT),
    )(shard.reshape(1).astype(jnp.int32), a, b3, partial)


def _mm_tn(a, dy, s, name, tm, tk, group=1):
    m, k = a.shape
    n = dy.shape[1] // s
    steps = m // tm

    def body(a_ref, dy_ref, o_ref, ob_ref):
        p = _dot_tn(a_ref[...], dy_ref[...])
        for g in range(group):
            pg = p[:, g * n:(g + 1) * n]
            if steps == 1:
                o_ref[g] = pg
                ob_ref[g] = pg.astype(BF16)
            else:
                _acc(o_ref.at[g], pg, pl.program_id(2) == 0)
        if steps > 1:
            @pl.when(pl.program_id(2) == steps - 1)
            def _():
                ob_ref[...] = o_ref[...].astype(BF16)

    out = pl.BlockSpec((group, tk, n), lambda kk, j, i: (j, kk, 0))
    return _call(
        body, name=name, grid=(k // tk, s // group, steps),
        in_specs=[pl.BlockSpec((tm, tk), lambda kk, j, i: (i, kk)),
                  pl.BlockSpec((tm, group * n), lambda kk, j, i: (i, j))],
        out_specs=[out, out], out_shape=[_sds((s, k, n)), _sds((s, k, n), BF16)])(a, dy)


def _rms(x):
    return lax.rsqrt(jnp.mean(x * x, axis=-1, keepdims=True) + RMS_EPS)


def _rms_bwd(dxh, xh, r):
    return r * (dxh - xh * jnp.mean(dxh * xh, axis=-1, keepdims=True))


def _norm_mod(x, g, scale, shift, name, tr=512):
    t = x.shape[0]

    def body(x_ref, g_ref, sc_ref, sh_ref, h_ref):
        xv = x_ref[...]
        n = xv * _rms(xv) * g_ref[...]
        h_ref[...] = (n * (1.0 + sc_ref[...]) + sh_ref[...]).astype(BF16)

    return _call(body, name=name, grid=(t // tr,),
                 in_specs=[_rows(tr, D_MODEL), _vec(D_MODEL), _vec(D_MODEL), _vec(D_MODEL)],
                 out_specs=_rows(tr, D_MODEL), out_shape=_sds((t, D_MODEL), BF16))(x, g, scale, shift)


def _out_resid_norm_mod(x, mixin, w_out, gate, g, scale, shift, name, tr=512):
    t = x.shape[0]

    def body(x_ref, mi_ref, w_ref, gt_ref, g_ref, sc_ref, sh_ref, m_ref, x2_ref, h_ref):
        mix = _dot(mi_ref[...], w_ref[0])
        m_ref[...] = mix
        x2 = x_ref[...] + gt_ref[...] * mix
        x2_ref[...] = x2
        n = x2 * _rms(x2) * g_ref[...]
        h_ref[...] = (n * (1.0 + sc_ref[...]) + sh_ref[...]).astype(BF16)

    row = _rows(tr, D_MODEL)
    return _call(body, name=name, grid=(t // tr,),
                 in_specs=[row, row, pl.BlockSpec(w_out.shape, lambda i: (0, 0, 0))] + [_vec(D_MODEL)] * 4,
                 out_specs=[row, row, row],
                 out_shape=[_sds((t, D_MODEL)), _sds((t, D_MODEL)), _sds((t, D_MODEL), BF16)])(
                     x, mixin, w_out, gate, g, scale, shift)


def _mm_gate_up(h, w_gu, name, tm=1024):
    m, k = h.shape
    n = w_gu.shape[2]

    def body(h_ref, wa_ref, wu_ref, da_ref, du_ref, o_ref, w_au):
        @pl.when(pl.program_id(1) == 0)
        def _():
            w_au[:, :n] = wa_ref[0]
            w_au[:, n:] = wu_ref[0]

        au = _dot(h_ref[...], w_au[...])
        a, u = au[:, :n], au[:, n:]
        sg = _sigmoid(a)
        silu = a * sg
        da_ref[...] = (u * sg * (1.0 + a * (1.0 - sg))).astype(BF16)
        du_ref[...] = silu.astype(BF16)
        o_ref[...] = (silu * u).astype(BF16)

    out = pl.BlockSpec((tm, n), lambda j, i: (i, j))
    return _call(body, name=name, grid=(2, m // tm),
                 in_specs=[pl.BlockSpec((tm, k), lambda j, i: (i, 0)), pl.BlockSpec((1, k, n), lambda j, i: (j, 0, 0)),
                           pl.BlockSpec((1, k, n), lambda j, i: (j + 2, 0, 0))],
                 out_specs=[out, out, out], out_shape=[_sds((m, 2 * n), BF16)] * 3,
                 scratch_shapes=[pltpu.VMEM((k, 2 * n), BF16)])(h, w_gu, w_gu)


def _mm_down_dx(dffn, w_down, act_da, act_du, name, tm=512):
    m = dffn.shape[0]
    _, k, n = w_down.shape

    steps = m // tm
    slots = 3

    def body(d_ref, w_ref, da_hbm, du_hbm, o_ref, da_buf, du_buf, sem):
        s = pl.program_id(0)

        def copies(step):
            slot = step % slots
            return [pltpu.make_async_copy(src.at[pl.ds(step * tm, tm)], dst.at[slot], sem.at[j, slot])
                    for j, (src, dst) in enumerate([(da_hbm, da_buf), (du_hbm, du_buf)])]

        @pl.when(s == 0)
        def _():
            for step in range(min(slots - 1, steps)):
                for cp in copies(step):
                    cp.start()

        @pl.when(s + slots - 1 < steps)
        def _():
            for cp in copies(s + slots - 1):
                cp.start()

        for cp in copies(s):
            cp.wait()
        slot = s % slots
        dact = _dot_nt(d_ref[...], w_ref[0])
        o_ref[:, :k] = (dact * da_buf[slot].astype(F32)).astype(BF16)
        o_ref[:, k:] = (dact * du_buf[slot].astype(F32)).astype(BF16)

    any_space = pl.BlockSpec(memory_space=pl.ANY)
    return _call(body, name=name, grid=(steps,),
                 in_specs=[_rows(tm, n), pl.BlockSpec((1, k, n), lambda i: (0, 0, 0)), any_space, any_space],
                 out_specs=_rows(tm, 2 * k), out_shape=_sds((m, 2 * k), BF16),
                 scratch_shapes=[pltpu.VMEM((slots, tm, k), BF16)] * 2 + [pltpu.SemaphoreType.DMA((2, slots))])(
                     dffn, w_down, act_da, act_du)


def _down_loss(x2, act, w_down, gate, fg, tgt, name, tr=512):
    t = x2.shape[0]
    _, k, n = w_down.shape

    def body(x_ref, a_ref, w_ref, gt_ref, fg_ref, t_ref, dx_ref, df_ref, l_ref, dfg_ref, dgt_ref):
        first = pl.program_id(0) == 0
        ffn_v = _dot(a_ref[...], w_ref[0])
        x3 = x_ref[...] + gt_ref[...] * ffn_v
        r = _rms(x3)
        xh = x3 * r
        err = xh * fg_ref[...] - t_ref[...]
        dy = err * (1.0 / D_MODEL)
        dx3 = _rms_bwd(dy * fg_ref[...], xh, r)
        dx_ref[...] = dx3
        df_ref[...] = (dx3 * gt_ref[...]).astype(BF16)
        _acc(l_ref, jnp.sum(err * err, axis=0, keepdims=True), first)
        _acc(dfg_ref, jnp.sum(dy * xh, axis=0, keepdims=True), first)
        _acc(dgt_ref, jnp.sum(dx3 * ffn_v, axis=0, keepdims=True), first)

    row, vec = _rows(tr, D_MODEL), _vec(D_MODEL)
    return _call(body, name=name, grid=(t // tr,),
                 in_specs=[row, _rows(tr, k), pl.BlockSpec((1, k, n), lambda i: (0, 0, 0)), vec, vec, row],
                 out_specs=[row, row, vec, vec, vec],
                 out_shape=[_sds((t, D_MODEL)), _sds((t, D_MODEL), BF16)] + [_sds((1, D_MODEL))] * 3)(
                     x2, act, w_down, gate, fg, tgt)


def _norm_mod_bwd(dh, x, g, scale, dres, name, gate=None, mix=None, w=None, tr=512):
    t = x.shape[0]
    below = gate is not None

    def body(*refs):
        if w is not None:
            w_ref, w_full, sem, dy_buf, dy_sem, refs = refs[1], refs[-4], refs[-3], refs[-2], refs[-1], refs[:1] + refs[2:-4]
            step = pl.program_id(0)

            def dy_copy(s):
                return pltpu.make_async_copy(refs[0].at[pl.ds(s * tr, tr)], dy_buf.at[s % 3], dy_sem.at[s % 3])

            @pl.when(step == 0)
            def _():
                for s in range(min(2, t // tr)):
                    dy_copy(s).start()

            @pl.when(step + 2 < t // tr)
            def _():
                dy_copy(step + 2).start()

            @pl.when(pl.program_id(0) == 0)
            def _():
                n = w.shape[2]
                copies = [pltpu.make_async_copy(w_ref.at[j], w_full.at[:, pl.ds(j * n, n)], sem.at[j])
                          for j in range(w.shape[0])]
                for cp in copies:
                    cp.start()
                for cp in copies:
                    cp.wait()

        if below:
            dh_ref, x_ref, g_ref, sc_ref, dr_ref, gt_ref, m_ref, dx_ref, dsh_ref, dsc_ref, dg_ref, dgt_ref, dm_ref = refs
        else:
            dh_ref, x_ref, g_ref, sc_ref, dr_ref, dx_ref, dsh_ref, dsc_ref, dg_ref = refs
        first = pl.program_id(0) == 0
        xv = x_ref[...]
        if w is None:
            dhv = dh_ref[...].astype(F32)
        else:
            dy_copy(step).wait()
            dhv = _dot_nt(dy_buf[step % 3], w_full[...])
        r = _rms(xv)
        xh = xv * r
        dn = dhv * (1.0 + sc_ref[...])
        dx = dr_ref[...] + _rms_bwd(dn * g_ref[...], xh, r)
        dx_ref[...] = dx
        _acc(dsh_ref, jnp.sum(dhv, axis=0, keepdims=True), first)
        _acc(dsc_ref, jnp.sum(dhv * xh * g_ref[...], axis=0, keepdims=True), first)
        _acc(dg_ref, jnp.sum(dn * xh, axis=0, keepdims=True), first)
        if below:
            _acc(dgt_ref, jnp.sum(dx * m_ref[...], axis=0, keepdims=True), first)
            dm_ref[...] = (dx * gt_ref[...]).astype(BF16)

    row, vec = _rows(tr, D_MODEL), _vec(D_MODEL)
    first_specs = [row] if w is None else [pl.BlockSpec(memory_space=pl.ANY)] * 2
    scratch = [] if w is None else [pltpu.VMEM((w.shape[1], dh.shape[1]), BF16), pltpu.SemaphoreType.DMA((w.shape[0],)),
                                    pltpu.VMEM((3, tr, dh.shape[1]), BF16), pltpu.SemaphoreType.DMA((3,))]
    in_specs = first_specs + [row, vec, vec, row] + ([vec, row] if below else [])
    out_specs = [row, vec, vec, vec] + ([vec, row] if below else [])
    out_shape = [_sds((t, D_MODEL))] + [_sds((1, D_MODEL))] * 3 + ([_sds((1, D_MODEL)), _sds((t, D_MODEL), BF16)] if below else [])
    args = ((dh,) if w is None else (dh, w)) + (x, g, scale, dres) + ((gate, mix) if below else ())
    return _call(body, name=name, grid=(t // tr,), in_specs=in_specs, out_specs=out_specs, out_shape=out_shape,
                 scratch_shapes=scratch)(*args)


def _mix_in_bwd(dmix, w_out, o_hg, proj, att, og, ag, fine, name, tr=512):
    t = o_hg.shape[0]

    def body(dy_ref, w_ref, o_ref, g_ref, a_ref, og_ref, ag_ref,
             do_ref, dg_ref, da_ref, dd_ref, das_ref, dds_ref, dog_ref, dag_ref, to_sub):
        first = pl.program_id(0) == 0

        @pl.when(first)
        def _():
            to_sub[...] = _regroup_matrix(tr, tr // fine)

        dmi = _dot_nt(dy_ref[...], w_ref[0])
        dog = jnp.zeros((1, HG_HEAD), F32)
        for h in range(HG_WIDTH // HG_HEAD):
            sl = slice(h * HG_HEAD, (h + 1) * HG_HEAD)
            oh = o_ref[:, sl].astype(F32)
            gv = g_ref[:, sl].astype(F32)
            dv = dmi[:, sl]
            r = _rms(oh)
            xh = oh * r
            sg = _sigmoid(gv)
            dno = dv * gv * sg
            dg_ref[:, sl] = (dv * xh * og_ref[...] * sg * (1.0 + gv * (1.0 - sg))).astype(BF16)
            dog = dog + jnp.sum(dno * xh, axis=0, keepdims=True)
            do_ref[:, sl] = _rms_bwd(dno * og_ref[...], xh, r).astype(BF16)
        _acc(dog_ref, dog, first)
        av = a_ref[...]
        dav = dmi[:, HG_WIDTH:]
        r = _rms(av)
        xa = av * r
        _acc(dag_ref, jnp.sum(dav * xa, axis=0, keepdims=True), first)
        datt = _rms_bwd(dav * ag_ref[...], xa, r)
        da_ref[...] = datt.astype(BF16)
        das_ref[...] = _regroup(to_sub[...], datt.astype(BF16)).astype(BF16).reshape(das_ref.shape)
        prod = datt * av
        lane = lax.broadcasted_iota(jnp.int32, (1, 128), 1)
        dd = jnp.zeros((tr, 128), F32)
        for hp in range(ATT_HEADS // 2):
            pp = prod[:, hp * 128:(hp + 1) * 128]
            lo = jnp.sum(jnp.where(lane < 64, pp, 0.0), axis=-1, keepdims=True)
            hi = jnp.sum(jnp.where(lane >= 64, pp, 0.0), axis=-1, keepdims=True)
            dd = jnp.where(lane == 2 * hp, lo, dd)
            dd = jnp.where(lane == 2 * hp + 1, hi, dd)
        dd_ref[...] = dd
        dds_ref[...] = _regroup(to_sub[...], dd, ATT_HEADS).reshape(dds_ref.shape)

    half = _rows(tr, HG_WIDTH)
    out = _call(body, name=name, grid=(t // tr,),
                in_specs=[_rows(tr, D_MODEL), pl.BlockSpec(w_out.shape, lambda i: (0, 0, 0)), half,
                          pl.BlockSpec((tr, HG_WIDTH), lambda i: (i, 3)), half, _vec(HG_HEAD), _vec(ATT_WIDTH)],
                out_specs=[half, pl.BlockSpec((tr, HG_WIDTH), lambda i: (i, 3)), half, _rows(tr, 128),
                           _sub_rows(tr, fine, ATT_WIDTH), _sub_rows(tr, fine, 128), _vec(HG_HEAD), _vec(ATT_WIDTH)],
                out_shape=[_sds((t, HG_WIDTH), BF16), _sds((t, IN_WIDTH), BF16), _sds((t, HG_WIDTH), BF16),
                           _sds((t, 128)), _sds((fine, t // fine, ATT_WIDTH), BF16),
                           _sds((fine, t // fine, 128)), _sds((1, HG_HEAD)), _sds((1, ATT_WIDTH))],
                scratch_shapes=[pltpu.VMEM((tr, tr), BF16)])(dmix, w_out, o_hg, proj, att, og, ag)
    out = list(out)
    return out[:4] + [out[4].reshape(t, ATT_WIDTH), out[5].reshape(t, 128)] + out[6:]


def _dproj(dproj, dqkvs, name, tr=1024):
    t = dproj.shape[0]
    nbr = len(dqkvs)
    first = IN_WIDTH // ATT_WIDTH - 3

    def body(*refs):
        refs[-1][...] = sum(r[...].astype(F32) for r in refs[:nbr]).astype(BF16)

    return _call(body, name=name, grid=(t // tr, 3),
                 in_specs=[pl.BlockSpec((tr, ATT_WIDTH), lambda i, j: (i, j))] * nbr + [pl.BlockSpec(memory_space=pl.ANY)],
                 out_specs=pl.BlockSpec((tr, ATT_WIDTH), lambda i, j: (i, first + j)),
                 out_shape=_sds(dproj.shape, BF16), aliases={nbr: 0})(*dqkvs, dproj)


def _chunk_tri(upper):
    row = lax.broadcasted_iota(jnp.int32, (HG_GROUP, HG_CHUNK, HG_CHUNK), 1)
    col = lax.broadcasted_iota(jnp.int32, (HG_GROUP, HG_CHUNK, HG_CHUNK), 2)
    return (row <= col if upper else row >= col).astype(BF16)


def _chunk_cumsum(x, tri):
    x3 = x.reshape(HG_GROUP, HG_CHUNK, x.shape[1])
    dims = (((2,), (1,)), ((0,), (0,)))
    out = None
    for _ in range(3):
        part = x3.astype(BF16)
        x3 = x3 - part.astype(F32)
        term = lax.dot_general(tri, part, dims, preferred_element_type=F32)
        out = term if out is None else out + term
    return out.reshape(x.shape)


def _hg_gates(f_raw, q_raw, lb, tri):
    sg = _sigmoid(f_raw)
    f = lb + (1.0 - lb) * sg
    k = 1.0 - f
    b = _chunk_cumsum(jnp.log(f), tri)
    sq = _sigmoid(q_raw)
    return sg, f, k, b, sq


def _hg_masks(rows):
    row = lax.broadcasted_iota(jnp.int32, (rows, rows), 0)
    col = lax.broadcasted_iota(jnp.int32, (rows, rows), 1)
    same = (row // HG_CHUNK) == (col // HG_CHUNK)
    return jnp.logical_and(row >= col, same), jnp.logical_and(row <= col, same)


def _per_chunk(rows_of):
    return jnp.concatenate([jnp.broadcast_to(r, (HG_CHUNK, r.shape[1])) for r in rows_of], axis=0)


def _hgrn_fwd(proj, lb_logits, name):
    t = proj.shape[0]
    nc = t // HG_CHUNK
    nh = HG_WIDTH // HG_HEAD
    rows = HG_GROUP * HG_CHUNK

    def body(q_ref, f_ref, i_ref, lg_ref, o_ref, st_ref, s_scr):
        @pl.when(pl.program_id(0) == 0)
        def _():
            s_scr[...] = jnp.zeros_like(s_scr)

        lg = lg_ref[...]
        lb_all = _sigmoid(lg[0:1] - lg[1:2])
        causal, _ = _hg_masks(rows)
        tri = _chunk_tri(False)
        for h in range(nh):
            sl = slice(h * HG_HEAD, (h + 1) * HG_HEAD)
            q_raw = q_ref[:, sl].astype(F32)
            _, _, k, b, sq = _hg_gates(f_ref[:, sl].astype(F32), q_raw, lb_all[:, sl], tri)
            v = i_ref[:, sl].astype(BF16)
            gls = [b[(g + 1) * HG_CHUNK - 1:(g + 1) * HG_CHUNK] for g in range(HG_GROUP)]
            bm = _per_chunk([b[g * HG_CHUNK + HG_CHUNK // 2 - 1:g * HG_CHUNK + HG_CHUNK // 2] for g in range(HG_GROUP)])
            qd = (q_raw * sq * jnp.exp(b)).astype(BF16)
            qm = (q_raw * sq * jnp.exp(b - bm)).astype(BF16)
            km = (k * jnp.exp(bm - b)).astype(BF16)
            ke = (k * jnp.exp(_per_chunk(gls) - b)).astype(BF16)
            a = jnp.where(causal, _dot_nt(qm, km), 0.0).astype(BF16)
            o_intra = _dot(a, v)
            st = s_scr[h]
            o_inter = []
            for g in range(HG_GROUP):
                rs = slice(g * HG_CHUNK, (g + 1) * HG_CHUNK)
                st_ref[g, sl, :] = st
                o_inter.append(_dot_nt(qd[rs], st.astype(BF16)))
                st = st * jnp.exp(gls[g]) + _dot_tn(v[rs], ke[rs])
            s_scr[h] = st
            o_ref[:, sl] = (o_intra + jnp.concatenate(o_inter, axis=0)).astype(BF16)

    blk = lambda j: pl.BlockSpec((rows, HG_WIDTH), lambda c: (c, j))
    return _call(body, name=name, grid=(nc // HG_GROUP,),
                 in_specs=[blk(0), blk(1), blk(2), pl.BlockSpec((2, HG_WIDTH), lambda c: (0, 0))],
                 out_specs=[blk(0), pl.BlockSpec((HG_GROUP, HG_WIDTH, HG_HEAD), lambda c: (c, 0, 0))],
                 out_shape=[_sds((t, HG_WIDTH), BF16), _sds((nc, HG_WIDTH, HG_HEAD))],
                 scratch_shapes=[pltpu.VMEM((nh, HG_HEAD, HG_HEAD), F32)])(proj, proj, proj, lb_logits)


def _hgrn_bwd(proj, lb_logits, states, do, dproj, name):
    t = proj.shape[0]
    ng = t // (HG_GROUP * HG_CHUNK)
    nh = HG_WIDTH // HG_HEAD
    rows = HG_GROUP * HG_CHUNK

    def body(q_ref, f_ref, i_ref, lg_ref, st_ref, do_ref, _, d_ref, dlb_ref, ds_scr):
        first = pl.program_id(0) == 0

        @pl.when(first)
        def _():
            ds_scr[...] = jnp.zeros_like(ds_scr)

        lg = lg_ref[...]
        lb_all = _sigmoid(lg[0:1] - lg[1:2])
        causal, _ = _hg_masks(rows)
        tri = _chunk_tri(False)
        tri_t = _chunk_tri(True)
        dlb = []
        for h in range(nh):
            sl = slice(h * HG_HEAD, (h + 1) * HG_HEAD)
            q_raw = q_ref[:, sl].astype(F32)
            lb = lb_all[:, sl]
            sg, f, k, b, sq = _hg_gates(f_ref[:, sl].astype(F32), q_raw, lb, tri)
            v = i_ref[:, sl].astype(BF16)
            gls = [b[(g + 1) * HG_CHUNK - 1:(g + 1) * HG_CHUNK] for g in range(HG_GROUP)]
            bm = _per_chunk([b[g * HG_CHUNK + HG_CHUNK // 2 - 1:g * HG_CHUNK + HG_CHUNK // 2] for g in range(HG_GROUP)])
            eb = jnp.exp(b)
            ebm = jnp.exp(b - bm)
            emb = jnp.exp(bm - b)
            egb = jnp.exp(_per_chunk(gls) - b)
            ke = k * egb
            qd_b, qm_b = (q_raw * sq * eb).astype(BF16), (q_raw * sq * ebm).astype(BF16)
            km_b, ke_b = (k * emb).astype(BF16), ke.astype(BF16)
            dov = do_ref[:, sl].astype(BF16)
            a = jnp.where(causal, _dot_nt(qm_b, km_b), 0.0).astype(BF16)
            da = jnp.where(causal, _dot_nt(dov, v), 0.0).astype(BF16)
            dkm = _dot_tn(da, qm_b)
            dst = ds_scr[h]
            dqd_s, dv_s, dke_s, dgl_s = [None] * HG_GROUP, [None] * HG_GROUP, [None] * HG_GROUP, [None] * HG_GROUP
            for g in reversed(range(HG_GROUP)):
                rs = slice(g * HG_CHUNK, (g + 1) * HG_CHUNK)
                st = st_ref[g, sl, :]
                dst_b = dst.astype(BF16)
                egl = jnp.exp(gls[g])
                dqd_s[g] = _dot(dov[rs], st.astype(BF16))
                dv_s[g] = _dot_nt(ke_b[rs], dst_b)
                dke_s[g] = _dot(v[rs], dst_b)
                dgl_s[g] = jnp.sum(dst * st, axis=0, keepdims=True) * egl
                dst = _dot_tn(dov[rs], qd_b[rs]) + dst * egl
            ds_scr[h] = dst
            dqm = _dot(da, km_b)
            dqd = jnp.concatenate(dqd_s, axis=0)
            dv = _dot_tn(a, dov) + jnp.concatenate(dv_s, axis=0)
            dke = jnp.concatenate(dke_s, axis=0)
            t1 = dke * ke
            db = dqm * qm_b.astype(F32) - dkm * km_b.astype(F32) + dqd * qd_b.astype(F32) - t1
            dgl = _per_chunk([dgl_s[g] + jnp.sum(t1[g * HG_CHUNK:(g + 1) * HG_CHUNK], axis=0, keepdims=True)
                              for g in range(HG_GROUP)])
            dlf = _chunk_cumsum(db, tri_t) + dgl
            df = dlf / f - (dkm * emb + dke * egb)
            d_ref[:, sl] = ((dqm * ebm + dqd * eb) * sq * (1.0 + q_raw * (1.0 - sq))).astype(BF16)
            d_ref[:, HG_WIDTH + h * HG_HEAD:HG_WIDTH + (h + 1) * HG_HEAD] = (
                df * (1.0 - lb) * sg * (1.0 - sg)).astype(BF16)
            d_ref[:, 2 * HG_WIDTH + h * HG_HEAD:2 * HG_WIDTH + (h + 1) * HG_HEAD] = dv.astype(BF16)
            dlb.append(jnp.sum(df * (1.0 - sg), axis=0, keepdims=True))
        _acc(dlb_ref, jnp.concatenate(dlb, axis=1), first)

    rev = lambda j: pl.BlockSpec((rows, HG_WIDTH), lambda c: (ng - 1 - c, j))
    return _call(body, name=name, grid=(ng,),
                 in_specs=[rev(0), rev(1), rev(2), pl.BlockSpec((2, HG_WIDTH), lambda c: (0, 0)),
                           pl.BlockSpec((HG_GROUP, HG_WIDTH, HG_HEAD), lambda c: (ng - 1 - c, 0, 0)), rev(0),
                           pl.BlockSpec(memory_space=pl.ANY)],
                 out_specs=[pl.BlockSpec((rows, 3 * HG_WIDTH), lambda c: (ng - 1 - c, 0)), _vec(HG_WIDTH)],
                 out_shape=[_sds(dproj.shape, BF16), _sds((1, HG_WIDTH))], aliases={6: 0},
                 scratch_shapes=[pltpu.VMEM((nh, HG_HEAD, HG_HEAD), F32)])(
                     proj, proj, proj, lb_logits, states, do, dproj)


def _to_sub(a, dil):
    t, w = a.shape
    return a if dil == 1 else a.reshape(t // dil, dil, w).transpose(1, 0, 2).reshape(t, w)


def _from_sub(a, dil):
    t, w = a.shape
    return a if dil == 1 else a.reshape(dil, t // dil, w).transpose(1, 0, 2).reshape(t, w)


def _att_bias(seg):
    def place(v):
        v = v % ATT_BLOCK
        return v if seg == 1 else seg * (v % (ATT_BLOCK // seg)) + v // (ATT_BLOCK // seg)

    row = lax.broadcasted_iota(jnp.int32, (2 * ATT_BLOCK, 2 * ATT_BLOCK), 0)
    col = lax.broadcasted_iota(jnp.int32, (2 * ATT_BLOCK, 2 * ATT_BLOCK), 1)
    qi, kj = place(row), place(col)
    prev = jnp.logical_and(col < ATT_BLOCK, kj >= qi)
    cur = jnp.logical_and(col >= ATT_BLOCK, kj <= qi)
    return jnp.stack([jnp.where(cur, 0.0, NEG), jnp.where(jnp.logical_or(prev, cur), 0.0, NEG)])


def _get(ref, sl):
    if len(ref.shape) == 2:
        return ref[:, sl]
    v = ref[:, :, sl]
    return v.reshape(ATT_BLOCK, v.shape[2])


def _put(ref, sl, val):
    if len(ref.shape) == 2:
        ref[:, sl] = val
    else:
        ref[:, :, sl] = val.reshape(ref.shape[0], ref.shape[1], val.shape[1])


def _att_spec(nb, dil, seg, width, col, back):
    bps = nb // dil

    def plain(n):
        return jnp.clip(n - back, 0, nb - 1), col

    def segmented(n):
        m = jnp.clip(n - back, 0, nb - 1)
        return 0, m // bps, m % bps, 0, col

    if seg == 1:
        return pl.BlockSpec((ATT_BLOCK, width), plain)
    return pl.BlockSpec((seg, None, None, ATT_BLOCK // seg, width), segmented)


def _att_shape(nb, dil, seg, width):
    t = nb * ATT_BLOCK
    return (t, width) if seg == 1 else (seg, dil, nb // dil, ATT_BLOCK // seg, width)


def _att_view(a, nb, dil, seg):
    return a.reshape(_att_shape(nb, dil, seg, a.shape[1]))


def _attn_fwd_block(q_ref, kc_ref, kp_ref, vc_ref, vp_ref, o_ref, l_ref, bias_ref, has_prev, lane0):
    bias = bias_ref[has_prev.astype(jnp.int32)]
    lane = lax.broadcasted_iota(jnp.int32, (1, 128), 1)
    lo = lane < 64
    nq = ATT_BLOCK
    lse_all = jnp.zeros((nq, 128), F32)
    for hp in range(ATT_HEADS // 2):
        sl = slice(hp * 128, (hp + 1) * 128)
        q2 = _get(q_ref, sl)
        zero = jnp.zeros_like(q2)
        q2 = q2 * 0.125
        qs = jnp.concatenate([jnp.where(lo, q2, zero), jnp.where(lo, zero, q2)], axis=0)
        kk = jnp.concatenate([_get(kp_ref, sl), _get(kc_ref, sl)], axis=0)
        vv = jnp.concatenate([_get(vp_ref, sl), _get(vc_ref, sl)], axis=0)
        s = _dot_nt(qs, kk) + bias
        mx = jnp.max(s, axis=-1, keepdims=True)
        p = jnp.exp(s - mx)
        l = jnp.sum(p, axis=-1, keepdims=True)
        o = _dot(p.astype(BF16), vv) * (1.0 / l)
        _put(o_ref, sl, jnp.where(lo, o[:nq], o[nq:]).astype(BF16))
        lse = mx + jnp.log(l)
        lse_all = jnp.where(lane == lane0 + 2 * hp, lse[:nq], lse_all)
        lse_all = jnp.where(lane == lane0 + 2 * hp + 1, lse[nq:], lse_all)
    _put(l_ref, slice(None), lse_all)


def _attn_fwd(branches, name):
    t = branches[0][0].shape[0]
    nb = t // ATT_BLOCK
    nbr = len(branches)

    def body(*refs):
        n = pl.program_id(0)
        biases = refs[7 * nbr:]

        @pl.when(n == 0)
        def _():
            for bias_ref, (_, _, seg) in zip(biases, branches):
                bias_ref[...] = _att_bias(seg)

        for i, (_, dil, seg) in enumerate(branches):
            _attn_fwd_block(*refs[5 * i:5 * i + 5], *refs[5 * nbr + 2 * i:5 * nbr + 2 * i + 2], biases[i],
                            (n % (nb // dil)) != 0, ATT_HEADS * i)

    in_specs, args, out_specs, out_shape = [], [], [], []
    for qkv, dil, seg in branches:
        c0 = qkv.shape[1] // ATT_WIDTH - 3
        in_specs += [_att_spec(nb, dil, seg, ATT_WIDTH, c0 + j, back) for j, back in [(0, 0), (1, 0), (1, 1), (2, 0), (2, 1)]]
        args += [_att_view(qkv, nb, dil, seg)] * 5
        out_specs += [_att_spec(nb, dil, seg, ATT_WIDTH, 0, 0), _att_spec(nb, dil, seg, 128, 0, 0)]
        out_shape += [_sds(_att_shape(nb, dil, seg, ATT_WIDTH), BF16), _sds(_att_shape(nb, dil, seg, 128))]
    out = _call(body, name=name, grid=(nb,), in_specs=in_specs, out_specs=out_specs, out_shape=out_shape,
                scratch_shapes=[pltpu.VMEM((2, 2 * ATT_BLOCK, 2 * ATT_BLOCK), F32)] * nbr)(*args)
    return [(out[2 * i].reshape(t, ATT_WIDTH), out[2 * i + 1].reshape(t, 128)) for i in range(nbr)]


def _combine_mix_in(os_, ls_, fine, o_hg, proj, og, ag, name, tr=512):
    t = os_[0].shape[0]
    nbr = len(os_)

    def body(*refs):
        o_refs, l_refs = refs[:nbr], refs[nbr:2 * nbr]
        oh_ref, g_ref, og_ref, ag_ref, a_ref, lt_ref, lts_ref, m_ref, to_natural, to_sub = refs[2 * nbr:]

        @pl.when(pl.program_id(0) == 0)
        def _():
            to_natural[...] = _regroup_matrix(tr, fine)
            to_sub[...] = _regroup_matrix(tr, tr // fine)

        lane = lax.broadcasted_iota(jnp.int32, (1, 128), 1)
        packed = l_refs[0][...] + _regroup(to_natural[...], sum(r[...] for r in l_refs[1:]).reshape(tr, 128))
        ls = [packed if i == 0 else pltpu.roll(packed, 128 - ATT_HEADS * i, 1) for i in range(nbr)]
        mx = functools.reduce(jnp.maximum, ls)
        tot = mx + jnp.log(sum(jnp.exp(l - mx) for l in ls))
        ws = [jnp.exp(l - tot) for l in ls]
        tot = jnp.where(lane < ATT_HEADS, tot, 0.0)
        lt_ref[...] = tot
        lts_ref[...] = _regroup(to_sub[...], tot).reshape(lts_ref.shape)
        o_vals = [o_refs[0]] + [_regroup(to_natural[...], r[...].reshape(tr, ATT_WIDTH)) for r in o_refs[1:]]
        pairs = []
        for hp in range(ATT_HEADS // 2):
            sl = slice(hp * 128, (hp + 1) * 128)
            acc = jnp.zeros((tr, 128), F32)
            for w, o in zip(ws, o_vals):
                wf = jnp.where(lane < 64, w[:, 2 * hp:2 * hp + 1], w[:, 2 * hp + 1:2 * hp + 2])
                acc = acc + wf * o[:, sl]
            pairs.append(acc)
        av = jnp.concatenate(pairs, axis=1)
        a_ref[...] = av
        m_ref[:, HG_WIDTH:] = (av * _rms(av) * ag_ref[...]).astype(BF16)
        for h in range(HG_WIDTH // HG_HEAD):
            sl = slice(h * HG_HEAD, (h + 1) * HG_HEAD)
            oh = oh_ref[:, sl].astype(F32)
            gv = g_ref[:, sl].astype(F32)
            m_ref[:, sl] = (oh * _rms(oh) * og_ref[...] * (gv * _sigmoid(gv))).astype(BF16)

    half = _rows(tr, ATT_WIDTH)
    sub = lambda a: a.reshape(fine, t // fine, a.shape[1])
    att, lse, lse_sub, mixin = _call(
        body, name=name, grid=(t // tr,),
        in_specs=[half] + [_sub_rows(tr, fine, ATT_WIDTH)] * (nbr - 1) + [_rows(tr, 128)] + [_sub_rows(tr, fine, 128)] * (nbr - 1)
        + [half, pl.BlockSpec((tr, HG_WIDTH), lambda i: (i, 3)), _vec(HG_HEAD), _vec(ATT_WIDTH)],
        out_specs=[half, _rows(tr, 128), _sub_rows(tr, fine, 128), _rows(tr, D_MODEL)],
        out_shape=[_sds((t, ATT_WIDTH)), _sds((t, 128)), _sds((fine, t // fine, 128)), _sds((t, D_MODEL), BF16)],
        scratch_shapes=[pltpu.VMEM((tr, tr), BF16)] * 2)(
            os_[0], *map(sub, os_[1:]), ls_[0], *map(sub, ls_[1:]), o_hg, proj, og, ag)
    return att, lse, lse_sub.reshape(t, 128), mixin


def _attn_bwd_block(q_ref, k_ref, v_ref, do_ref, l_ref, d_ref, out_ref, carry, prev, bias_ref, has_prev):
    w = ATT_WIDTH
    nq = ATT_BLOCK
    bias = bias_ref[has_prev.astype(jnp.int32)]
    lo = lax.broadcasted_iota(jnp.int32, (1, 128), 1) < 64
    lse, ddv = _get(l_ref, slice(None)), _get(d_ref, slice(None))
    for hp in range(ATT_HEADS // 2):
        sl = slice(hp * 128, (hp + 1) * 128)
        sk = slice(w + hp * 128, w + (hp + 1) * 128)
        sv = slice(2 * w + hp * 128, 2 * w + (hp + 1) * 128)
        q2, do2 = _get(q_ref, sl), _get(do_ref, sl)
        zero = jnp.zeros_like(q2)
        q2 = q2 * 0.125
        qs = jnp.concatenate([jnp.where(lo, q2, zero), jnp.where(lo, zero, q2)], axis=0)
        dos = jnp.concatenate([jnp.where(lo, do2, zero), jnp.where(lo, zero, do2)], axis=0)
        kc, vc = _get(k_ref, sl), _get(v_ref, sl)
        kk = jnp.concatenate([prev[:, sl], kc], axis=0)
        vv = jnp.concatenate([prev[:, sk], vc], axis=0)
        prev[:, sl] = kc
        prev[:, sk] = vc
        ls = jnp.concatenate([lse[:, 2 * hp:2 * hp + 1], lse[:, 2 * hp + 1:2 * hp + 2]], axis=0)
        dh = jnp.concatenate([ddv[:, 2 * hp:2 * hp + 1], ddv[:, 2 * hp + 1:2 * hp + 2]], axis=0)
        p = jnp.exp(_dot_nt(qs, kk) - ls + bias)
        ds = (p * (_dot_nt(dos, vv) - dh)).astype(BF16)
        dq = _dot(ds, kk) * 0.125
        dk = _dot_tn(ds, qs)
        dv = _dot_tn(p.astype(BF16), dos)
        _put(out_ref, sl, carry[:, sl].astype(BF16))
        _put(out_ref, sk, (carry[:, sk] + dk[:nq]).astype(BF16))
        _put(out_ref, sv, (carry[:, sv] + dv[:nq]).astype(BF16))
        carry[:, sl] = jnp.where(lo, dq[:nq], dq[nq:])
        carry[:, sk] = dk[nq:]
        carry[:, sv] = dv[nq:]


def _attn_bwd(branches, name):
    t = branches[0][0].shape[0]
    nb = t // ATT_BLOCK
    nbr = len(branches)
    w = ATT_WIDTH

    def body(*refs):
        ins, outs, carries, prevs = refs[:6 * nbr], refs[6 * nbr:7 * nbr], refs[7 * nbr:8 * nbr], refs[8 * nbr:9 * nbr]
        biases = refs[9 * nbr:]
        n = pl.program_id(0)

        @pl.when(n == 0)
        def _():
            for scratch in carries + prevs:
                scratch[...] = jnp.zeros_like(scratch)
            for bias_ref, branch in zip(biases, branches):
                bias_ref[...] = _att_bias(branch[5])

        @pl.when(n < nb)
        def _():
            for i, branch in enumerate(branches):
                dil, seg = branch[4:]
                _attn_bwd_block(*ins[6 * i:6 * i + 6], outs[i], carries[i], prevs[i], biases[i], (n % (nb // dil)) != 0)

        @pl.when(n == nb)
        def _():
            for i in range(nbr):
                _put(outs[i], slice(None), carries[i][...].astype(BF16))

    in_specs, args, out_specs, out_shape = [], [], [], []
    for qkv, dout, lse, dd, dil, seg in branches:
        c0 = qkv.shape[1] // w - 3
        in_specs += [_att_spec(nb, dil, seg, w, c0 + j, 0) for j in range(3)]
        in_specs += [_att_spec(nb, dil, seg, w, 0, 0), _att_spec(nb, dil, seg, 128, 0, 0), _att_spec(nb, dil, seg, 128, 0, 0)]
        args += [_att_view(a, nb, dil, seg) for a in [qkv] * 3 + [dout, lse, dd]]
        out_specs += [_att_spec(nb, dil, seg, 3 * w, 0, 1)]
        out_shape += [_sds(_att_shape(nb, dil, seg, 3 * w), BF16)]
    out = _call(body, name=name, grid=(nb + 1,), in_specs=in_specs, out_specs=out_specs, out_shape=out_shape,
                scratch_shapes=[pltpu.VMEM((ATT_BLOCK, 3 * w), F32)] * nbr + [pltpu.VMEM((ATT_BLOCK, 2 * w), BF16)] * nbr
                + [pltpu.VMEM((2, 2 * ATT_BLOCK, 2 * ATT_BLOCK), F32)] * nbr)(*args)
    return [o.reshape(t, 3 * w) for o in out]


def _local_step(x, tgt, mod, norm1_g, lb_logits, og, ag, norm2_g, fg, get_w, put_g, late=lambda a: a, project=None):
    shift1, scale1, gate1, shift2, scale2, gate2 = [mod[:, i * D_MODEL:(i + 1) * D_MODEL] for i in range(6)]
    fg = fg.reshape(1, D_MODEL)

    h1 = _norm_mod(x, norm1_g, scale1, shift1, "norm_mod1")
    if project is None:
        w_in = get_w("w_in", h1)
        proj = _mm_nn(h1, w_in, "mm_in", out_dtype=BF16)
    else:
        proj, w_in = project(h1)
    o_hg, states = _hgrn_fwd(proj, lb_logits, "hgrn_fwd")
    fine = DILATIONS[-1]
    layouts = [(d, 1 if d == 1 else fine // d) for d in DILATIONS]
    qkv_fine = _to_sub(proj, fine)
    qkvs = [proj if d == 1 else qkv_fine for d in DILATIONS]
    natural = lambda a, d: a if d == 1 else _from_sub(a, fine)
    outs = _attn_fwd([(q, d, seg) for q, (d, seg) in zip(qkvs, layouts)], "attn_fwd")
    att, lse, lse_fine, mixin = _combine_mix_in([o for o, _ in outs], [l for _, l in outs], fine,
                                                o_hg, proj, og, ag, "attn_combine_mix_in")
    w_out = get_w("w_out", mixin)
    mix, x2, h2 = _out_resid_norm_mod(x, mixin, w_out, gate1, norm2_g, scale2, shift2, "mm_out_resid_norm_mod2")
    w_gu = get_w("w_gu", h2)
    a_ff, u_ff, act = _mm_gate_up(h2, w_gu, "mm_gu")
    w_down = get_w("w_down", act)
    dx3, dffn, loss_v, dfg, dgate2 = _down_loss(x2, act, w_down, gate2, fg, tgt, "mm_down_loss")

    dffn = put_g("w_down", *_mm_tn(act, dffn, 1, "mm_down_dw", tm=x.shape[0], tk=256), dffn)
    dau = _mm_down_dx(dffn, w_down, a_ff, u_ff, "mm_down_dx")
    dau = put_g("w_gu", *_mm_tn(h2, dau, N_SHARD, "mm_gu_dw", tm=x.shape[0], tk=512), dau)
    dx2, dshift2, dscale2, dg2, dgate1, dmix = _norm_mod_bwd(
        dau, x2, norm2_g, scale2, dx3, "mm_gu_dx_norm_bwd", gate=gate1, mix=mix, w=w_gu)
    dmix = put_g("w_out", *_mm_tn(mixin, dmix, 1, "mm_out_dw", tm=x.shape[0], tk=512), dmix)
    do_hg, dproj, datt, dd, datt_fine, dd_fine, dog, dag = _mix_in_bwd(
        dmix, w_out, o_hg, proj, att, og, ag, fine, "mm_out_dx_mix_in_bwd")
    datts = _attn_bwd([(q,) + ((datt, lse, dd) if d == 1 else (datt_fine, lse_fine, dd_fine)) + (d, seg)
                       for q, (d, seg) in zip(qkvs, layouts)], "attn_bwd")
    dproj, dlb = _hgrn_bwd(proj, lb_logits, states, do_hg, dproj, "hgrn_bwd")
    dproj = late(dproj)
    dproj = _dproj(dproj, [natural(a, d) for a, d in zip(datts, DILATIONS)], "dproj")
    dproj = put_g("w_in", *_mm_tn(h1, dproj, N_SHARD, "mm_in_dw", tm=x.shape[0], tk=512, group=2), dproj)
    dx, dshift1, dscale1, dg1 = _norm_mod_bwd(dproj, x, norm1_g, scale1, dx2, "mm_in_dx_norm_bwd", w=w_in)

    stats = jnp.concatenate([loss_v, dfg, dg2, dg1, dlb, dag, dog,
                             dshift1, dscale1, dgate1, dshift2, dscale2, dgate2], axis=1)
    return dx, stats


def _place():
    x, y, c = lax.axis_index("x"), lax.axis_index("y"), lax.axis_index("c")
    return x, y, c


def _chip_peers(x, y, c):
    return [(1 - x, y, c), (x, 1 - y, c), (1 - x, 1 - y, c)]


_HBM = pl.BlockSpec(memory_space=pltpu.HBM)
_SEM = pl.BlockSpec(memory_space=pltpu.SEMAPHORE)
_EFFECT = pltpu.SideEffectType.DATAFLOW_SIDE_EFFECTING


def _exchange_copy(bufs, send, recv, j, peer, place, kind):
    x, y, c = place
    target = peer
    if kind == "gather":
        src = dst = bufs[0].at[2 * x + y]
    elif kind == "scatter":
        src, dst = bufs[0].at[2 * peer[0] + peer[1]], bufs[1].at[j]
    else:
        half = bufs[0].shape[1] // 2
        rows = pl.ds(c * half, half)
        if kind == "half":
            src = dst = bufs[0].at[2 * x + y, rows]
        else:
            src = dst = bufs[0].at[2 * peer[0] + peer[1], rows]
            target = (x, y, 1 - c)
    return pltpu.make_async_remote_copy(src_ref=src, dst_ref=dst, send_sem=send.at[j], recv_sem=recv.at[j],
                                        device_id=target, device_id_type=MESH)


def _exchange_start(groups, after, kind, name):
    sizes = [len(g) for g in groups]
    flat = [b for g in groups for b in g]
    ng, nb = len(groups), len(flat)

    def body(*refs):
        bufs, sems = refs[:nb], refs[nb + 1:nb + 1 + 2 * ng]
        x, y, c = _place()
        for j, peer in enumerate(_chip_peers(x, y, c)):
            at = 0
            for i, size in enumerate(sizes):
                _exchange_copy(bufs[at:at + size], sems[2 * i], sems[2 * i + 1], j, peer, (x, y, c), kind).start()
                at += size

    any_space = pl.BlockSpec(memory_space=pl.ANY)
    out = pl.pallas_call(
        body, name=name, in_specs=[_HBM] * nb + [any_space],
        out_specs=[_SEM] * (2 * ng) + [_HBM] * nb + [any_space],
        out_shape=[pltpu.SemaphoreType.DMA((3,))] * (2 * ng) + [pltpu.HBM(b.shape, b.dtype) for b in flat]
        + [_sds(after.shape, after.dtype)],
        input_output_aliases={i: 2 * ng + i for i in range(nb + 1)},
        compiler_params=pltpu.CompilerParams(has_side_effects=_EFFECT),
    )(*[pltpu.with_memory_space_constraint(b, pltpu.HBM) for b in flat], after)
    started, at = [], 2 * ng
    for i, size in enumerate(sizes):
        started.append((out[2 * i], out[2 * i + 1], tuple(out[at:at + size])))
        at += size
    return started, out[-1]


def _exchange_wait(started, after, kind, name):
    send, recv, bufs = started
    nb = len(bufs)

    def body(*refs):
        x, y, c = _place()
        for j, peer in enumerate(_chip_peers(x, y, c)):
            cp = _exchange_copy(refs[:nb], refs[nb], refs[nb + 1], j, peer, (x, y, c), kind)
            cp.wait_send()
            cp.wait_recv()

    return pl.pallas_call(
        body, name=name, in_specs=[_HBM] * nb + [_SEM, _SEM, pl.BlockSpec(memory_space=pl.ANY)],
        out_specs=[_HBM] * nb, out_shape=[pltpu.HBM(b.shape, b.dtype) for b in bufs],
        input_output_aliases={i: i for i in range(nb)},
        compiler_params=pltpu.CompilerParams(has_side_effects=_EFFECT),
    )(*bufs, send, recv, after)


def _sibling_copies(v_refs, l_refs, send, recv):
    x, y, c = _place()
    return [pltpu.make_async_remote_copy(src_ref=v, dst_ref=l, send_sem=send.at[a], recv_sem=recv.at[a],
                                         device_id=(x, y, 1 - c), device_id_type=MESH)
            for a, (v, l) in enumerate(zip(v_refs, l_refs))]


def _sibling_start(vs, after, name):
    vs = list(vs)
    n = len(vs)
    lands = [lax.empty(v.shape, v.dtype) for v in vs]

    def body(*refs):
        for cp in _sibling_copies(refs[:n], refs[n:2 * n], refs[2 * n + 1], refs[2 * n + 2]):
            cp.start()

    any_space = pl.BlockSpec(memory_space=pl.ANY)
    out = pl.pallas_call(
        body, name=name, in_specs=[_HBM] * (2 * n) + [any_space],
        out_specs=[_SEM, _SEM] + [_HBM] * (2 * n) + [any_space],
        out_shape=[pltpu.SemaphoreType.DMA((n,))] * 2 + [pltpu.HBM(b.shape, b.dtype) for b in vs + lands]
        + [_sds(after.shape, after.dtype)],
        input_output_aliases={i: 2 + i for i in range(2 * n + 1)},
        compiler_params=pltpu.CompilerParams(has_side_effects=_EFFECT),
    )(*[pltpu.with_memory_space_constraint(b, pltpu.HBM) for b in vs + lands], after)
    return (out[0], out[1], tuple(out[2:2 + n]), tuple(out[2 + n:2 + 2 * n])), out[-1]


def _sibling_wait(started, after, name):
    send, recv, vs, lands = started
    n = len(vs)

    def body(*refs):
        for cp in _sibling_copies(refs[:n], refs[n:2 * n], refs[2 * n], refs[2 * n + 1]):
            cp.wait_send()
            cp.wait_recv()

    out = pl.pallas_call(
        body, name=name, in_specs=[_HBM] * (2 * n) + [_SEM, _SEM, pl.BlockSpec(memory_space=pl.ANY)],
        out_specs=[_HBM] * (2 * n), out_shape=[pltpu.HBM(b.shape, b.dtype) for b in vs + lands],
        input_output_aliases={i: i for i in range(2 * n)},
        compiler_params=pltpu.CompilerParams(has_side_effects=_EFFECT),
    )(*vs, *lands, send, recv, after)
    return out[:n], out[n:]


def _everyone(x, y, c):
    return [(1 - x if k & 4 else x, 1 - y if k & 2 else y, 1 - c if k & 1 else c) for k in range(1, 8)]


def _all_gather_copies(land_ref, send, recv, arriving):
    x, y, c = _place()
    me = 4 * x + 2 * y + c
    return [pltpu.make_async_remote_copy(
        src_ref=land_ref.at[me], dst_ref=land_ref.at[4 * p[0] + 2 * p[1] + p[2] if arriving else me],
        send_sem=send.at[k], recv_sem=recv.at[k], device_id=p, device_id_type=MESH)
        for k, p in enumerate(_everyone(x, y, c))]


def _all_gather_start(v, name):
    x, y, c = _place()
    land = lax.dynamic_update_slice(lax.empty((8,) + v.shape, v.dtype), v[None], (4 * x + 2 * y + c, 0, 0))

    def body(land_ref, v_ref, send, recv, land_out, v_out):
        for cp in _all_gather_copies(land_ref, send, recv, False):
            cp.start()

    any_space = pl.BlockSpec(memory_space=pl.ANY)
    out = pl.pallas_call(
        body, name=name, in_specs=[_HBM, any_space], out_specs=[_SEM, _SEM, _HBM, any_space],
        out_shape=[pltpu.SemaphoreType.DMA((7,))] * 2 + [pltpu.HBM(land.shape, land.dtype), _sds(v.shape, v.dtype)],
        input_output_aliases={0: 2, 1: 3}, compiler_params=pltpu.CompilerParams(has_side_effects=_EFFECT),
    )(pltpu.with_memory_space_constraint(land, pltpu.HBM), v)
    return tuple(out[:3]), out[3]


def _all_gather_wait(started, after, name):
    send, recv, land = started

    def body(land_ref, send, recv, after_ref, land_out):
        for cp in _all_gather_copies(land_ref, send, recv, True):
            cp.wait_send()
            cp.wait_recv()

    return pl.pallas_call(
        body, name=name, in_specs=[_HBM, _SEM, _SEM, pl.BlockSpec(memory_space=pl.ANY)], out_specs=_HBM,
        out_shape=pltpu.HBM(land.shape, land.dtype), input_output_aliases={0: 0},
        compiler_params=pltpu.CompilerParams(has_side_effects=_EFFECT),
    )(land, send, recv, after)


def _cast_place(ws, shard, name, after=()):
    n = len(ws)

    def body(s_ref, *refs):
        for w_ref, o_ref in zip(refs[:n], refs[-n:]):
            o_ref[0] = w_ref[...].astype(BF16)

    return pl.pallas_call(
        body, name=name, out_shape=[_sds((N_SHARD,) + w.shape, BF16) for w in ws],
        grid_spec=pltpu.PrefetchScalarGridSpec(
            num_scalar_prefetch=1, grid=(4,),
            in_specs=[pl.BlockSpec((w.shape[0] // 4, w.shape[1]), lambda i, s: (i, 0)) for w in ws]
            + [pl.BlockSpec(memory_space=pl.ANY)] * len(after),
            out_specs=[pl.BlockSpec((1, w.shape[0] // 4, w.shape[1]), lambda i, s: (s[0], i, 0)) for w in ws]),
        compiler_params=pltpu.CompilerParams(dimension_semantics=("arbitrary",), vmem_limit_bytes=VMEM_LIMIT),
    )(shard.reshape(1).astype(jnp.int32), *ws, *after)


def _mod_rows(c8, w_ada, b_ada, name):
    n = w_ada.shape[1]

    def gather(src_ref, dst_ref, send, recv, loc, base):
        x, y, c = _place()
        me = 4 * x + 2 * y + c
        own = pltpu.make_async_copy(src_ref, dst_ref.at[me], loc)
        own.start()
        peers = _everyone(x, y, c)
        sends = [pltpu.make_async_remote_copy(src_ref=src_ref, dst_ref=dst_ref.at[me], send_sem=send.at[base + k],
                                              recv_sem=recv.at[base + k], device_id=p, device_id_type=MESH)
                 for k, p in enumerate(peers)]
        for cp in sends:
            cp.start()
        for k, p in enumerate(peers):
            pltpu.make_async_remote_copy(src_ref=src_ref, dst_ref=dst_ref.at[4 * p[0] + 2 * p[1] + p[2]],
                                         send_sem=send.at[base + k], recv_sem=recv.at[base + k], device_id=p,
                                         device_id_type=MESH).wait_recv()
        for cp in sends:
            cp.wait_send()
        own.wait()

    def body(c_ref, w_ref, b_ref, a_ref, parts_ref, c_all, part, send, recv, loc):
        gather(c_ref, c_all, send, recv, loc.at[0], 0)
        cv = jnp.max(c_all[...], axis=1)
        ca = cv * _sigmoid(cv)
        a_ref[...] = ca
        part[...] = jnp.dot(ca, w_ref[...], precision=lax.Precision.HIGHEST, preferred_element_type=F32) + b_ref[...]
        gather(part, parts_ref, send, recv, loc.at[1], 7)

    vmem = pl.BlockSpec(memory_space=pltpu.VMEM)
    return pl.pallas_call(
        body, name=name, in_specs=[vmem] * 3, out_specs=[vmem, vmem],
        out_shape=[_sds((8, D_MODEL)), _sds((8, 8, n))],
        scratch_shapes=[pltpu.VMEM((8, 8, D_MODEL), F32), pltpu.VMEM((8, n), F32), pltpu.SemaphoreType.DMA((14,)),
                        pltpu.SemaphoreType.DMA((14,)), pltpu.SemaphoreType.DMA((2,))],
        compiler_params=pltpu.CompilerParams(vmem_limit_bytes=VMEM_LIMIT))(c8, w_ada, b_ada)


def _sum_received(gs, shard, lands, name):
    n = len(gs)

    def body(s_ref, *refs):
        for g_ref, l_ref, o_ref in zip(refs[:n], refs[n:2 * n], refs[2 * n:]):
            o_ref[...] = ((g_ref[0] + l_ref[0].astype(F32)) + l_ref[1].astype(F32)) + l_ref[2].astype(F32)

    quarter = lambda g: (g.shape[1] // 4, g.shape[2])
    return pl.pallas_call(
        body, name=name, out_shape=[_sds(g.shape[1:]) for g in gs],
        grid_spec=pltpu.PrefetchScalarGridSpec(
            num_scalar_prefetch=1, grid=(4,),
            in_specs=[pl.BlockSpec((1,) + quarter(g), lambda i, s: (s[0], i, 0)) for g in gs]
            + [pl.BlockSpec((3,) + quarter(g), lambda i, s: (0, i, 0)) for g in gs],
            out_specs=[pl.BlockSpec(quarter(g), lambda i, s: (i, 0)) for g in gs]),
        compiler_params=pltpu.CompilerParams(dimension_semantics=("arbitrary",), vmem_limit_bytes=VMEM_LIMIT),
    )(shard.reshape(1).astype(jnp.int32), *gs, *lands)


def _adamw_outer(w, ct, dm, m, v, name):
    k, n = w.shape
    tr = k // 4

    def body(w_ref, c_ref, d_ref, m_ref, v_ref, g_out, d_out, m_out, v_out):
        cv = c_ref[...]
        dv = d_ref[...]
        g = cv[:, 0:1] * dv[0:1, :]
        for i in range(1, 8):
            g = g + cv[:, i:i + 1] * dv[i:i + 1, :]
        g_out[...] = g
        d_out[...], m_out[...], v_out[...] = _adamw_math(w_ref[...], g, m_ref[...], v_ref[...])

    row = _rows(tr, n)
    return _call(body, name=name, grid=(4,),
                 in_specs=[row, _rows(tr, 8), pl.BlockSpec((8, n), lambda i: (0, 0)), row, row],
                 out_specs=[row] * 4, out_shape=[_sds((k, n))] * 4)(w, ct, dm, m, v)


def _adamw_math(w, g, m, v):
    m_new = ADAM_B1 * m + (1.0 - ADAM_B1) * g
    v_new = ADAM_B2 * v + (1.0 - ADAM_B2) * (g * g)
    m_hat = m_new / (1.0 - ADAM_B1 ** ADAM_STEP)
    v_hat = v_new / (1.0 - ADAM_B2 ** ADAM_STEP)
    return -ADAM_LR * (m_hat / (jnp.sqrt(v_hat) + ADAM_EPS) + ADAM_WD * w), m_new, v_new


def _small_update(stats, smalls, name):
    offsets = [ST_DMOD, ST_DG1, ST_DLB, ST_DOG, ST_DAG, ST_DG2, ST_DFG]
    lb_index = 2

    def body(*refs):
        s_ref, ins, l_ref, outs = refs[0], refs[1:22], refs[22], refs[23:]
        tot = s_ref[0:1, :]
        for i in range(1, 8):
            tot = tot + s_ref[i:i + 1, :]
        l_ref[...] = jnp.zeros((1, 128), F32) + (0.5 / D_MODEL) * jnp.sum(tot[:, ST_LOSS:ST_LOSS + D_MODEL])
        for p, off in enumerate(offsets):
            w_ref, m_ref, v_ref = ins[3 * p:3 * p + 3]
            g_out, d_out, m_out, v_out = outs[4 * p:4 * p + 4]
            g = tot[:, off:off + w_ref.shape[1]]
            if p == lb_index:
                lg = w_ref[...]
                lb = _sigmoid(lg[0:1] - lg[1:2])
                g = g * lb * (1.0 - lb)
            for r in range(w_ref.shape[0]):
                rows = slice(r, r + 1)
                gr = g if r == 0 else -g
                delta, m_new, v_new = _adamw_math(w_ref[rows, :], gr, m_ref[rows, :], v_ref[rows, :])
                g_out[rows, :] = gr
                d_out[rows, :] = delta
                m_out[rows, :] = m_new
                v_out[rows, :] = v_new

    full = lambda a: pl.BlockSpec(a.shape, lambda i: (0, 0))
    flat = [a for t in smalls for a in t]
    return _call(body, name=name, grid=(1,),
                 in_specs=[full(stats)] + [full(a) for a in flat],
                 out_specs=[pl.BlockSpec((1, 128), lambda i: (0, 0))] + [full(t[0]) for t in smalls for _ in range(4)],
                 out_shape=[_sds((1, 128))] + [_sds(t[0].shape) for t in smalls for _ in range(4)])(stats, *flat)


def _adamw(params, name):
    n = len(params)

    def body(*refs):
        for p in range(n):
            w_ref, ga_ref, gb_ref, m_ref, v_ref = refs[5 * p:5 * p + 5]
            g_out, d_out, m_out, v_out = refs[5 * n + 4 * p:5 * n + 4 * p + 4]
            g = ga_ref[...] + gb_ref[...]
            g_out[...] = g
            d_out[...], m_out[...], v_out[...] = _adamw_math(w_ref[...], g, m_ref[...], v_ref[...])

    row = lambda w: _rows(w.shape[0] // 4, w.shape[1])
    out = _call(body, name=name, grid=(4,), in_specs=[row(p[0]) for p in params for _ in range(5)],
                out_specs=[row(p[0]) for p in params for _ in range(4)],
                out_shape=[_sds(p[0].shape) for p in params for _ in range(4)])(*[a for p in params for a in p])
    return [tuple(out[4 * p:4 * p + 4]) for p in range(n)]


def kernel(x, c, w_ada, b_ada, norm1_g, w_in, hg_lb_logits, hg_onorm_g, att_onorm_g, w_out, norm2_g, w_gate_up, w_down, final_g, loss_target, m_w_ada, m_b_ada, m_norm1_g, m_w_in, m_hg_lb_logits, m_hg_onorm_g, m_att_onorm_g, m_w_out, m_norm2_g, m_w_gate_up, m_w_down, m_final_g, v_w_ada, v_b_ada, v_norm1_g, v_w_in, v_hg_lb_logits, v_hg_onorm_g, v_att_onorm_g, v_w_out, v_norm2_g, v_w_gate_up, v_w_down, v_final_g):
    ix, iy, ic = _place()
    shard = 2 * ix + iy
    sample = 4 * ix + 2 * iy + ic
    n_ada = w_ada.shape[2]

    shards = [w_in[0], w_out[0], w_gate_up[0], w_down[0]]
    names = ["w_in", "w_out", "w_gu", "w_down"]
    shapes = [(N_SHARD,) + w.shape for w in shards]
    placed = [(_cast_place(shards[:1], shard, "place_w_in")[0],)]

    b_part = lax.dynamic_slice(b_ada, (0, shard * n_ada), (1, n_ada))
    c_act, parts = _mod_rows(jnp.broadcast_to(c, (8, D_MODEL)), w_ada[0], b_part, "mod_rows")
    parts = parts[::2]
    mod = lax.dynamic_index_in_dim(parts, sample, axis=1, keepdims=False).reshape(1, 6 * D_MODEL)
    (first,), mod = _exchange_start(placed[:1], mod, "half", "gather_start_w_in")
    gathering = {}

    def get_w(name, after):
        if name == "w_in":
            placed_rest = [(p,) for p in _cast_place(shards[1:], shard, "place_rest", (after,))]
            halves = _exchange_wait(first, placed_rest[0][0], "half", "gather_wait_w_in")
            (passing,), token = _exchange_start([tuple(halves)], mod, "forward", "forward_start_w_in")
            rest, token = _exchange_start(placed_rest, token, "gather", "gather_start_rest")
            (full,) = _exchange_wait(passing, token, "forward", "forward_wait_w_in")
            gathering.update(zip(names[1:], rest))
            return full
        (full,) = _exchange_wait(gathering[name], after, "gather", "gather_wait_" + name)
        return full if name == "w_gu" else full.reshape(1, -1, D_MODEL)

    scattering = {}

    def put_g(name, g, g_bf16, then):
        shape = shapes[names.index(name)]
        land = lax.empty((3,) + shape[1:], BF16)
        (started,), then = _exchange_start([(g_bf16.reshape(shape), land)], then, "scatter", "scatter_start_" + name)
        scattering[name] = (g.reshape(shape), started)
        return then

    def summed(group, after, tag):
        lands = [_exchange_wait(scattering[nm][1], after, "scatter", "scatter_wait_" + nm)[1] for nm in group]
        return _sum_received([scattering[nm][0] for nm in group], shard, lands, "sum_" + tag)

    early = ["w_down", "w_gu", "w_out"]
    swapping = []

    def late(a):
        started, a = _sibling_start(summed(early, a, "early"), a, "swap_start")
        swapping.append(started)
        return a

    def project(h1):
        own = _mm_own_shard(h1, shards[0], shard, N_SHARD, "mm_in_own")
        w_full = get_w("w_in", own)
        return _mm_other_shards(h1, w_full, own, shard, "mm_in_rest"), w_full

    dx, stats = _local_step(x[0], loss_target[0], mod, norm1_g, hg_lb_logits, hg_onorm_g, att_onorm_g,
                            norm2_g, final_g, get_w, put_g, late, project)

    gathering_stats, stats = _all_gather_start(stats, "stats_start")
    moments = [(m_w_in, v_w_in), (m_w_out, v_w_out), (m_w_gate_up, v_w_gate_up), (m_w_down, v_w_down)]

    def update(group, sums, other, tag):
        params = [(shards[names.index(nm)], s, o, moments[names.index(nm)][0][0], moments[names.index(nm)][1][0])
                  for nm, s, o in zip(group, sums, other)]
        return dict(zip(group, _adamw(params, "adamw_" + tag)))

    sums, other = _sibling_wait(swapping[0], stats, "swap_wait")
    done = update(early, sums, other, "early")
    swapping_in, stats = _sibling_start(summed(["w_in"], done["w_out"][1], "w_in"), stats, "swap_start_w_in")

    stats_all = _all_gather_wait(gathering_stats, stats, "stats_wait").reshape(8, ST_WIDTH)
    dmod = lax.dynamic_slice(stats_all, (0, ST_DMOD + shard * n_ada), (8, n_ada))

    as_row = lambda a: a.reshape(1, -1) if a.ndim == 1 else a
    smalls = [tuple(as_row(a) for a in t) for t in [
        (b_ada, m_b_ada, v_b_ada), (norm1_g, m_norm1_g, v_norm1_g),
        (hg_lb_logits, m_hg_lb_logits, v_hg_lb_logits), (hg_onorm_g, m_hg_onorm_g, v_hg_onorm_g),
        (att_onorm_g, m_att_onorm_g, v_att_onorm_g), (norm2_g, m_norm2_g, v_norm2_g),
        (final_g, m_final_g, v_final_g)]]
    loss, *small_out = _small_update(stats_all, smalls, "small_update")
    shapes_out = [b_ada.shape, norm1_g.shape, hg_lb_logits.shape, hg_onorm_g.shape, att_onorm_g.shape,
                  norm2_g.shape, final_g.shape]
    sg, sd, sm, sv = [[small_out[4 * p + i].reshape(shapes_out[p]) for p in range(7)] for i in range(4)]

    ada = _adamw_outer(w_ada[0], c_act.T, dmod, m_w_ada[0], v_w_ada[0], "adamw_w_ada")
    sum_in, other_in = _sibling_wait(swapping_in, ada[1], "swap_wait_w_in")
    done.update(update(["w_in"], sum_in, other_in, "w_in"))
    big = [ada] + [done[nm] for nm in names]
    bg, bd, bm, bv = [[t[i][None] for t in big] for i in range(4)]

    def order(b, s):
        return [b[0], s[0], s[1], b[1], s[2], s[3], s[4], b[2], s[5], b[3], b[4], s[6]]

    return (loss[0, 0], dx[None], *order(bg, sg), *order(bd, sd), *order(bm, sm), *order(bv, sv))
```

```python
import functools

import jax
import jax.numpy as jnp
from jax import lax
from jax.experimental import pallas as pl
from jax.experimental.pallas import tpu as pltpu

F32 = jnp.float32
BF16 = jnp.bfloat16
MESH = pl.DeviceIdType.MESH

D_MODEL = 1024
HG_WIDTH = 512
HG_HEAD = 128
HG_CHUNK = 64
HG_GROUP = 4
ATT_WIDTH = 512
ATT_HEADS = 8
ATT_BLOCK = 128
DILATIONS = (1, 4, 16)
D_FF = 2816
IN_WIDTH = 3584
N_SHARD = 4
RMS_EPS = 1e-6
NEG = -1e30

ADAM_LR = 0.001
ADAM_B1 = 0.9
ADAM_B2 = 0.999
ADAM_EPS = 1e-08
ADAM_WD = 0.01
ADAM_STEP = 10

VMEM_LIMIT = 56 * 2**20

ST_LOSS, ST_DFG, ST_DG2, ST_DG1 = 0, 1024, 2048, 3072
ST_DLB, ST_DAG, ST_DOG, ST_DMOD = 4096, 4608, 5120, 5248
ST_WIDTH = 5248 + 6144


def _call(body, *, name, grid, in_specs, out_specs, out_shape, scratch_shapes=(), aliases=None):
    return pl.pallas_call(
        body, name=name, grid=grid, in_specs=in_specs, out_specs=out_specs, out_shape=out_shape,
        scratch_shapes=list(scratch_shapes), input_output_aliases=aliases or {},
        compiler_params=pltpu.CompilerParams(
            dimension_semantics=("arbitrary",) * len(grid), vmem_limit_bytes=VMEM_LIMIT))


def _sds(shape, dtype=F32):
    return jax.ShapeDtypeStruct(shape, dtype)


def _dot(a, b):
    return jnp.dot(a, b, preferred_element_type=F32)


def _dot_nt(a, b):
    return lax.dot_general(a, b, (((1,), (1,)), ((), ())), preferred_element_type=F32)


def _dot_tn(a, b):
    return lax.dot_general(a, b, (((0,), (0,)), ((), ())), preferred_element_type=F32)


def _sigmoid(x):
    return 1.0 / (1.0 + jnp.exp(-x))


def _rows(tr, width):
    return pl.BlockSpec((tr, width), lambda i: (i, 0))


def _vec(width):
    return pl.BlockSpec((1, width), lambda i: (0, 0))


def _acc(ref, val, first):
    @pl.when(first)
    def _():
        ref[...] = val

    @pl.when(jnp.logical_not(first))
    def _():
        ref[...] += val


def _sub_rows(tr, fine, width):
    return pl.BlockSpec((fine, tr // fine, width), lambda i: (0, i, 0))


def _regroup_matrix(tr, groups):
    a = lax.broadcasted_iota(jnp.int32, (tr, tr), 0)
    b = lax.broadcasted_iota(jnp.int32, (tr, tr), 1)
    return (b == (a % groups) * (tr // groups) + a // groups).astype(BF16)


def _regroup(m, v, lanes=None):
    if v.dtype == BF16:
        return _dot(m, v)
    width = v.shape[1]
    packed, out = None, None
    for i in range(3):
        part = v.astype(BF16).astype(F32)
        v = v - part
        if lanes is None:
            out = _dot(m, part.astype(BF16)) if i == 0 else out + _dot(m, part.astype(BF16))
        else:
            packed = part if i == 0 else packed + pltpu.roll(part, i * lanes, 1)
    if lanes is None:
        return out
    out = _dot(m, packed.astype(BF16))
    out = out + pltpu.roll(out, width - lanes, 1) + pltpu.roll(out, width - 2 * lanes, 1)
    return jnp.where(lax.broadcasted_iota(jnp.int32, (1, width), 1) < lanes, out, 0.0)


def _mm_nn(a, b3, name, tm=1024, out_dtype=F32):
    m, k = a.shape
    s, _, n = b3.shape

    def body(a_ref, b_ref, o_ref):
        o_ref[...] = _dot(a_ref[...], b_ref[0]).astype(out_dtype)

    return _call(
        body, name=name, grid=(s, m // tm),
        in_specs=[pl.BlockSpec((tm, k), lambda j, i: (i, 0)), pl.BlockSpec((1, k, n), lambda j, i: (j, 0, 0))],
        out_specs=pl.BlockSpec((tm, n), lambda j, i: (i, j)), out_shape=_sds((m, s * n), out_dtype))(a, b3)


def _mm_own_shard(a, w, shard, s, name, tm=1024):
    m, k = a.shape
    n = w.shape[1]

    def body(s_ref, a_ref, w_ref, o_ref):
        o_ref[...] = _dot(a_ref[...], w_ref[...].astype(BF16)).astype(BF16)

    return pl.pallas_call(
        body, name=name, out_shape=_sds((m, s * n), BF16),
        grid_spec=pltpu.PrefetchScalarGridSpec(
            num_scalar_prefetch=1, grid=(m // tm,),
            in_specs=[pl.BlockSpec((tm, k), lambda i, sh: (i, 0)), pl.BlockSpec((k, n), lambda i, sh: (0, 0))],
            out_specs=pl.BlockSpec((tm, n), lambda i, sh: (i, sh[0]))),
        compiler_params=pltpu.CompilerParams(dimension_semantics=("arbitrary",), vmem_limit_bytes=VMEM_LIMIT),
    )(shard.reshape(1).astype(jnp.int32), a, w)


def _mm_other_shards(a, b3, partial, shard, name, tm=1024):
    m, k = a.shape
    s, _, n = b3.shape
    which = lambda j, sh: (sh[0] + 1 + j) % s

    def body(s_ref, a_ref, b_ref, p_ref, o_ref):
        o_ref[...] = _dot(a_ref[...], b_ref[0]).astype(BF16)

    return pl.pallas_call(
        body, name=name, out_shape=_sds(partial.shape, BF16),
        grid_spec=pltpu.PrefetchScalarGridSpec(
            num_scalar_prefetch=1, grid=(s - 1, m // tm),
            in_specs=[pl.BlockSpec((tm, k), lambda j, i, sh: (i, 0)),
                      pl.BlockSpec((1, k, n), lambda j, i, sh: (which(j, sh), 0, 0)),
                      pl.BlockSpec(memory_space=pl.ANY)],
            out_specs=pl.BlockSpec((tm, n), lambda j, i, sh: (i, which(j, sh)))),
        input_output_aliases={3: 0},
        compiler_params=pltpu.CompilerParams(dimension_semantics=("arbitrary",) * 2, vmem_limit_bytes=VMEM_LIMIT),
    )(shard.reshape(1).astype(jnp.int32), a, b3, partial)


def _mm_tn(a, dy, s, name, tm, tk, group=1):
    m, k = a.shape
    n = dy.shape[1] // s
    steps = m // tm

    def body(a_ref, dy_ref, o_ref, ob_ref):
        p = _dot_tn(a_ref[...], dy_ref[...])
        for g in range(group):
            pg = p[:, g * n:(g + 1) * n]
            if steps == 1:
                o_ref[g] = pg
                ob_ref[g] = pg.astype(BF16)
            else:
                _acc(o_ref.at[g], pg, pl.program_id(2) == 0)
        if steps > 1:
            @pl.when(pl.program_id(2) == steps - 1)
            def _():
                ob_ref[...] = o_ref[...].astype(BF16)

    out = pl.BlockSpec((group, tk, n), lambda kk, j, i: (j, kk, 0))
    return _call(
        body, name=name, grid=(k // tk, s // group, steps),
        in_specs=[pl.BlockSpec((tm, tk), lambda kk, j, i: (i, kk)),
                  pl.BlockSpec((tm, group * n), lambda kk, j, i: (i, j))],
        out_specs=[out, out], out_shape=[_sds((s, k, n)), _sds((s, k, n), BF16)])(a, dy)


def _rms(x):
    return lax.rsqrt(jnp.mean(x * x, axis=-1, keepdims=True) + RMS_EPS)


def _rms_bwd(dxh, xh, r):
    return r * (dxh - xh * jnp.mean(dxh * xh, axis=-1, keepdims=True))


def _norm_mod(x, g, scale, shift, name, tr=512):
    t = x.shape[0]

    def body(x_ref, g_ref, sc_ref, sh_ref, h_ref):
        xv = x_ref[...]
        n = xv * _rms(xv) * g_ref[...]
        h_ref[...] = (n * (1.0 + sc_ref[...]) + sh_ref[...]).astype(BF16)

    return _call(body, name=name, grid=(t // tr,),
                 in_specs=[_rows(tr, D_MODEL), _vec(D_MODEL), _vec(D_MODEL), _vec(D_MODEL)],
                 out_specs=_rows(tr, D_MODEL), out_shape=_sds((t, D_MODEL), BF16))(x, g, scale, shift)


def _out_resid_norm_mod(x, mixin, w_out, gate, g, scale, shift, name, tr=512):
    t = x.shape[0]

    def body(x_ref, mi_ref, w_ref, gt_ref, g_ref, sc_ref, sh_ref, m_ref, x2_ref, h_ref):
        mix = _dot(mi_ref[...], w_ref[0])
        m_ref[...] = mix
        x2 = x_ref[...] + gt_ref[...] * mix
        x2_ref[...] = x2
        n = x2 * _rms(x2) * g_ref[...]
        h_ref[...] = (n * (1.0 + sc_ref[...]) + sh_ref[...]).astype(BF16)

    row = _rows(tr, D_MODEL)
    return _call(body, name=name, grid=(t // tr,),
                 in_specs=[row, row, pl.BlockSpec(w_out.shape, lambda i: (0, 0, 0))] + [_vec(D_MODEL)] * 4,
                 out_specs=[row, row, row],
                 out_shape=[_sds((t, D_MODEL)), _sds((t, D_MODEL)), _sds((t, D_MODEL), BF16)])(
                     x, mixin, w_out, gate, g, scale, shift)


def _mm_gate_up(h, w_gu, name, tm=1024):
    m, k = h.shape
    n = w_gu.shape[2]

    def body(h_ref, wa_ref, wu_ref, da_ref, du_ref, o_ref, w_au):
        @pl.when(pl.program_id(1) == 0)
        def _():
            w_au[:, :n] = wa_ref[0]
            w_au[:, n:] = wu_ref[0]

        au = _dot(h_ref[...], w_au[...])
        a, u = au[:, :n], au[:, n:]
        sg = _sigmoid(a)
        silu = a * sg
        da_ref[...] = (u * sg * (1.0 + a * (1.0 - sg))).astype(BF16)
        du_ref[...] = silu.astype(BF16)
        o_ref[...] = (silu * u).astype(BF16)

    out = pl.BlockSpec((tm, n), lambda j, i: (i, j))
    return _call(body, name=name, grid=(2, m // tm),
                 in_specs=[pl.BlockSpec((tm, k), lambda j, i: (i, 0)), pl.BlockSpec((1, k, n), lambda j, i: (j, 0, 0)),
                           pl.BlockSpec((1, k, n), lambda j, i: (j + 2, 0, 0))],
                 out_specs=[out, out, out], out_shape=[_sds((m, 2 * n), BF16)] * 3,
                 scratch_shapes=[pltpu.VMEM((k, 2 * n), BF16)])(h, w_gu, w_gu)


def _mm_down_dx(dffn, w_down, act_da, act_du, name, tm=512):
    m = dffn.shape[0]
    _, k, n = w_down.shape

    steps = m // tm
    slots = 3

    def body(d_ref, w_ref, da_hbm, du_hbm, o_ref, da_buf, du_buf, sem):
        s = pl.program_id(0)

        def copies(step):
            slot = step % slots
            return [pltpu.make_async_copy(src.at[pl.ds(step * tm, tm)], dst.at[slot], sem.at[j, slot])
                    for j, (src, dst) in enumerate([(da_hbm, da_buf), (du_hbm, du_buf)])]

        @pl.when(s == 0)
        def _():
            for step in range(min(slots - 1, steps)):
                for cp in copies(step):
                    cp.start()

        @pl.when(s + slots - 1 < steps)
        def _():
            for cp in copies(s + slots - 1):
                cp.start()

        for cp in copies(s):
            cp.wait()
        slot = s % slots
        dact = _dot_nt(d_ref[...], w_ref[0])
        o_ref[:, :k] = (dact * da_buf[slot].astype(F32)).astype(BF16)
        o_ref[:, k:] = (dact * du_buf[slot].astype(F32)).astype(BF16)

    any_space = pl.BlockSpec(memory_space=pl.ANY)
    return _call(body, name=name, grid=(steps,),
                 in_specs=[_rows(tm, n), pl.BlockSpec((1, k, n), lambda i: (0, 0, 0)), any_space, any_space],
                 out_specs=_rows(tm, 2 * k), out_shape=_sds((m, 2 * k), BF16),
                 scratch_shapes=[pltpu.VMEM((slots, tm, k), BF16)] * 2 + [pltpu.SemaphoreType.DMA((2, slots))])(
                     dffn, w_down, act_da, act_du)


def _down_loss(x2, act, w_down, gate, fg, tgt, name, tr=512):
    t = x2.shape[0]
    _, k, n = w_down.shape

    def body(x_ref, a_hbm, w_ref, gt_ref, fg_ref, t_ref, dx_ref, df_ref, l_ref, dfg_ref, dgt_ref, a_buf, sem):
        step = pl.program_id(0)
        first = step == 0

        def a_copy(s):
            return pltpu.make_async_copy(a_hbm.at[pl.ds(s * tr, tr)], a_buf.at[s % 3], sem.at[s % 3])

        @pl.when(first)
        def _():
            for s in range(min(2, t // tr)):
                a_copy(s).start()

        @pl.when(step + 2 < t // tr)
        def _():
            a_copy(step + 2).start()

        a_copy(step).wait()
        ffn_v = _dot(a_buf[step % 3], w_ref[0])
        x3 = x_ref[...] + gt_ref[...] * ffn_v
        r = _rms(x3)
        xh = x3 * r
        err = xh * fg_ref[...] - t_ref[...]
        dy = err * (1.0 / D_MODEL)
        dx3 = _rms_bwd(dy * fg_ref[...], xh, r)
        dx_ref[...] = dx3
        df_ref[...] = (dx3 * gt_ref[...]).astype(BF16)
        _acc(l_ref, jnp.sum(err * err, axis=0, keepdims=True), first)
        _acc(dfg_ref, jnp.sum(dy * xh, axis=0, keepdims=True), first)
        _acc(dgt_ref, jnp.sum(dx3 * ffn_v, axis=0, keepdims=True), first)

    row, vec = _rows(tr, D_MODEL), _vec(D_MODEL)
    return _call(body, name=name, grid=(t // tr,),
                 in_specs=[row, pl.BlockSpec(memory_space=pl.ANY), pl.BlockSpec((1, k, n), lambda i: (0, 0, 0)), vec, vec, row],
                 out_specs=[row, row, vec, vec, vec],
                 out_shape=[_sds((t, D_MODEL)), _sds((t, D_MODEL), BF16)] + [_sds((1, D_MODEL))] * 3,
                 scratch_shapes=[pltpu.VMEM((3, tr, k), BF16), pltpu.SemaphoreType.DMA((3,))])(
                     x2, act, w_down, gate, fg, tgt)


def _norm_mod_bwd(dh, x, g, scale, dres, name, gate=None, mix=None, w=None, tr=512):
    t = x.shape[0]
    below = gate is not None

    def body(*refs):
        if w is not None:
            w_ref, w_full, sem, refs = refs[1], refs[-2], refs[-1], refs[:1] + refs[2:-2]

            @pl.when(pl.program_id(0) == 0)
            def _():
                n = w.shape[2]
                copies = [pltpu.make_async_copy(w_ref.at[j], w_full.at[:, pl.ds(j * n, n)], sem.at[j])
                          for j in range(w.shape[0])]
                for cp in copies:
                    cp.start()
                for cp in copies:
                    cp.wait()

        if below:
            dh_ref, x_ref, g_ref, sc_ref, dr_ref, gt_ref, m_ref, dx_ref, dsh_ref, dsc_ref, dg_ref, dgt_ref, dm_ref = refs
        else:
            dh_ref, x_ref, g_ref, sc_ref, dr_ref, dx_ref, dsh_ref, dsc_ref, dg_ref = refs
        first = pl.program_id(0) == 0
        xv = x_ref[...]
        if w is None:
            dhv = dh_ref[...].astype(F32)
        else:
            dhv = _dot_nt(dh_ref[...], w_full[...])
        r = _rms(xv)
        xh = xv * r
        dn = dhv * (1.0 + sc_ref[...])
        dx = dr_ref[...] + _rms_bwd(dn * g_ref[...], xh, r)
        dx_ref[...] = dx
        _acc(dsh_ref, jnp.sum(dhv, axis=0, keepdims=True), first)
        _acc(dsc_ref, jnp.sum(dhv * xh * g_ref[...], axis=0, keepdims=True), first)
        _acc(dg_ref, jnp.sum(dn * xh, axis=0, keepdims=True), first)
        if below:
            _acc(dgt_ref, jnp.sum(dx * m_ref[...], axis=0, keepdims=True), first)
            dm_ref[...] = (dx * gt_ref[...]).astype(BF16)

    row, vec = _rows(tr, D_MODEL), _vec(D_MODEL)
    first_specs = [row] if w is None else [_rows(tr, dh.shape[1]), pl.BlockSpec(memory_space=pl.ANY)]
    scratch = [] if w is None else [pltpu.VMEM((w.shape[1], dh.shape[1]), BF16), pltpu.SemaphoreType.DMA((w.shape[0],))]
    in_specs = first_specs + [row, vec, vec, row] + ([vec, row] if below else [])
    out_specs = [row, vec, vec, vec] + ([vec, row] if below else [])
    out_shape = [_sds((t, D_MODEL))] + [_sds((1, D_MODEL))] * 3 + ([_sds((1, D_MODEL)), _sds((t, D_MODEL), BF16)] if below else [])
    args = ((dh,) if w is None else (dh, w)) + (x, g, scale, dres) + ((gate, mix) if below else ())
    return _call(body, name=name, grid=(t // tr,), in_specs=in_specs, out_specs=out_specs, out_shape=out_shape,
                 scratch_shapes=scratch)(*args)


def _mix_in_bwd(dmix, w_out, o_hg, proj, att, og, ag, fine, name, tr=512):
    t = o_hg.shape[0]

    def body(dy_ref, w_ref, o_ref, g_ref, a_ref, og_ref, ag_ref,
             do_ref, dg_ref, da_ref, dd_ref, das_ref, dds_ref, dog_ref, dag_ref, to_sub):
        first = pl.program_id(0) == 0

        @pl.when(first)
        def _():
            to_sub[...] = _regroup_matrix(tr, tr // fine)

        dmi = _dot_nt(dy_ref[...], w_ref[0])
        dog = jnp.zeros((1, HG_HEAD), F32)
        for h in range(HG_WIDTH // HG_HEAD):
            sl = slice(h * HG_HEAD, (h + 1) * HG_HEAD)
            oh = o_ref[:, sl].astype(F32)
            gv = g_ref[:, sl].astype(F32)
            dv = dmi[:, sl]
            r = _rms(oh)
            xh = oh * r
            sg = _sigmoid(gv)
            dno = dv * gv * sg
            dg_ref[:, sl] = (dv * xh * og_ref[...] * sg * (1.0 + gv * (1.0 - sg))).astype(BF16)
            dog = dog + jnp.sum(dno * xh, axis=0, keepdims=True)
            do_ref[:, sl] = _rms_bwd(dno * og_ref[...], xh, r).astype(BF16)
        _acc(dog_ref, dog, first)
        av = a_ref[...]
        dav = dmi[:, HG_WIDTH:]
        r = _rms(av)
        xa = av * r
        _acc(dag_ref, jnp.sum(dav * xa, axis=0, keepdims=True), first)
        datt = _rms_bwd(dav * ag_ref[...], xa, r)
        da_ref[...] = datt.astype(BF16)
        das_ref[...] = _regroup(to_sub[...], datt.astype(BF16)).astype(BF16).reshape(das_ref.shape)
        prod = datt * av
        lane = lax.broadcasted_iota(jnp.int32, (1, 128), 1)
        dd = jnp.zeros((tr, 128), F32)
        for hp in range(ATT_HEADS // 2):
            pp = prod[:, hp * 128:(hp + 1) * 128]
            lo = jnp.sum(jnp.where(lane < 64, pp, 0.0), axis=-1, keepdims=True)
            hi = jnp.sum(jnp.where(lane >= 64, pp, 0.0), axis=-1, keepdims=True)
            dd = jnp.where(lane == 2 * hp, lo, dd)
            dd = jnp.where(lane == 2 * hp + 1, hi, dd)
        dd_ref[...] = dd
        dds_ref[...] = _regroup(to_sub[...], dd, ATT_HEADS).reshape(dds_ref.shape)

    half = _rows(tr, HG_WIDTH)
    out = _call(body, name=name, grid=(t // tr,),
                in_specs=[_rows(tr, D_MODEL), pl.BlockSpec(w_out.shape, lambda i: (0, 0, 0)), half,
                          pl.BlockSpec((tr, HG_WIDTH), lambda i: (i, 3)), half, _vec(HG_HEAD), _vec(ATT_WIDTH)],
                out_specs=[half, pl.BlockSpec((tr, HG_WIDTH), lambda i: (i, 3)), half, _rows(tr, 128),
                           _sub_rows(tr, fine, ATT_WIDTH), _sub_rows(tr, fine, 128), _vec(HG_HEAD), _vec(ATT_WIDTH)],
                out_shape=[_sds((t, HG_WIDTH), BF16), _sds((t, IN_WIDTH), BF16), _sds((t, HG_WIDTH), BF16),
                           _sds((t, 128)), _sds((fine, t // fine, ATT_WIDTH), BF16),
                           _sds((fine, t // fine, 128)), _sds((1, HG_HEAD)), _sds((1, ATT_WIDTH))],
                scratch_shapes=[pltpu.VMEM((tr, tr), BF16)])(dmix, w_out, o_hg, proj, att, og, ag)
    out = list(out)
    return out[:4] + [out[4].reshape(t, ATT_WIDTH), out[5].reshape(t, 128)] + out[6:]


def _dproj(dproj, dqkvs, name, tr=1024):
    t = dproj.shape[0]
    nbr = len(dqkvs)
    first = IN_WIDTH // ATT_WIDTH - 3

    def body(*refs):
        refs[-1][...] = sum(r[...].astype(F32) for r in refs[:nbr]).astype(BF16)

    return _call(body, name=name, grid=(t // tr, 3),
                 in_specs=[pl.BlockSpec((tr, ATT_WIDTH), lambda i, j: (i, j))] * nbr + [pl.BlockSpec(memory_space=pl.ANY)],
                 out_specs=pl.BlockSpec((tr, ATT_WIDTH), lambda i, j: (i, first + j)),
                 out_shape=_sds(dproj.shape, BF16), aliases={nbr: 0})(*dqkvs, dproj)


def _chunk_tri(upper):
    row = lax.broadcasted_iota(jnp.int32, (HG_GROUP, HG_CHUNK, HG_CHUNK), 1)
    col = lax.broadcasted_iota(jnp.int32, (HG_GROUP, HG_CHUNK, HG_CHUNK), 2)
    return (row <= col if upper else row >= col).astype(BF16)


def _chunk_cumsum(x, tri):
    x3 = x.reshape(HG_GROUP, HG_CHUNK, x.shape[1])
    dims = (((2,), (1,)), ((0,), (0,)))
    out = None
    for _ in range(3):
        part = x3.astype(BF16)
        x3 = x3 - part.astype(F32)
        term = lax.dot_general(tri, part, dims, preferred_element_type=F32)
        out = term if out is None else out + term
    return out.reshape(x.shape)


def _hg_gates(f_raw, q_raw, lb, tri):
    sg = _sigmoid(f_raw)
    f = lb + (1.0 - lb) * sg
    k = 1.0 - f
    b = _chunk_cumsum(jnp.log(f), tri)
    sq = _sigmoid(q_raw)
    return sg, f, k, b, sq


def _hg_masks(rows):
    row = lax.broadcasted_iota(jnp.int32, (rows, rows), 0)
    col = lax.broadcasted_iota(jnp.int32, (rows, rows), 1)
    same = (row // HG_CHUNK) == (col // HG_CHUNK)
    return jnp.logical_and(row >= col, same), jnp.logical_and(row <= col, same)


def _per_chunk(rows_of):
    return jnp.concatenate([jnp.broadcast_to(r, (HG_CHUNK, r.shape[1])) for r in rows_of], axis=0)


def _hgrn_fwd(proj, lb_logits, name):
    t = proj.shape[0]
    nc = t // HG_CHUNK
    nh = HG_WIDTH // HG_HEAD
    rows = HG_GROUP * HG_CHUNK

    def body(q_ref, f_ref, i_ref, lg_ref, o_ref, st_ref, s_scr):
        @pl.when(pl.program_id(0) == 0)
        def _():
            s_scr[...] = jnp.zeros_like(s_scr)

        lg = lg_ref[...]
        lb_all = _sigmoid(lg[0:1] - lg[1:2])
        causal, _ = _hg_masks(rows)
        tri = _chunk_tri(False)
        for h in range(nh):
            sl = slice(h * HG_HEAD, (h + 1) * HG_HEAD)
            q_raw = q_ref[:, sl].astype(F32)
            _, _, k, b, sq = _hg_gates(f_ref[:, sl].astype(F32), q_raw, lb_all[:, sl], tri)
            v = i_ref[:, sl].astype(BF16)
            gls = [b[(g + 1) * HG_CHUNK - 1:(g + 1) * HG_CHUNK] for g in range(HG_GROUP)]
            bm = _per_chunk([b[g * HG_CHUNK + HG_CHUNK // 2 - 1:g * HG_CHUNK + HG_CHUNK // 2] for g in range(HG_GROUP)])
            qd = (q_raw * sq * jnp.exp(b)).astype(BF16)
            qm = (q_raw * sq * jnp.exp(b - bm)).astype(BF16)
            km = (k * jnp.exp(bm - b)).astype(BF16)
            ke = (k * jnp.exp(_per_chunk(gls) - b)).astype(BF16)
            a = jnp.where(causal, _dot_nt(qm, km), 0.0).astype(BF16)
            o_intra = _dot(a, v)
            st = s_scr[h]
            o_inter = []
            for g in range(HG_GROUP):
                rs = slice(g * HG_CHUNK, (g + 1) * HG_CHUNK)
                st_ref[g, sl, :] = st
                o_inter.append(_dot_nt(qd[rs], st.astype(BF16)))
                st = st * jnp.exp(gls[g]) + _dot_tn(v[rs], ke[rs])
            s_scr[h] = st
            o_ref[:, sl] = (o_intra + jnp.concatenate(o_inter, axis=0)).astype(BF16)

    blk = lambda j: pl.BlockSpec((rows, HG_WIDTH), lambda c: (c, j))
    return _call(body, name=name, grid=(nc // HG_GROUP,),
                 in_specs=[blk(0), blk(1), blk(2), pl.BlockSpec((2, HG_WIDTH), lambda c: (0, 0))],
                 out_specs=[blk(0), pl.BlockSpec((HG_GROUP, HG_WIDTH, HG_HEAD), lambda c: (c, 0, 0))],
                 out_shape=[_sds((t, HG_WIDTH), BF16), _sds((nc, HG_WIDTH, HG_HEAD))],
                 scratch_shapes=[pltpu.VMEM((nh, HG_HEAD, HG_HEAD), F32)])(proj, proj, proj, lb_logits)


def _hgrn_bwd(proj, lb_logits, states, do, dproj, name):
    t = proj.shape[0]
    ng = t // (HG_GROUP * HG_CHUNK)
    nh = HG_WIDTH // HG_HEAD
    rows = HG_GROUP * HG_CHUNK

    def body(q_ref, f_ref, i_ref, lg_ref, st_ref, do_ref, _, d_ref, dlb_ref, ds_scr):
        first = pl.program_id(0) == 0

        @pl.when(first)
        def _():
            ds_scr[...] = jnp.zeros_like(ds_scr)

        lg = lg_ref[...]
        lb_all = _sigmoid(lg[0:1] - lg[1:2])
        causal, _ = _hg_masks(rows)
        tri = _chunk_tri(False)
        tri_t = _chunk_tri(True)
        dlb = []
        for h in range(nh):
            sl = slice(h * HG_HEAD, (h + 1) * HG_HEAD)
            q_raw = q_ref[:, sl].astype(F32)
            lb = lb_all[:, sl]
            sg, f, k, b, sq = _hg_gates(f_ref[:, sl].astype(F32), q_raw, lb, tri)
            v = i_ref[:, sl].astype(BF16)
            gls = [b[(g + 1) * HG_CHUNK - 1:(g + 1) * HG_CHUNK] for g in range(HG_GROUP)]
            bm = _per_chunk([b[g * HG_CHUNK + HG_CHUNK // 2 - 1:g * HG_CHUNK + HG_CHUNK // 2] for g in range(HG_GROUP)])
            eb = jnp.exp(b)
            ebm = jnp.exp(b - bm)
            emb = jnp.exp(bm - b)
            egb = jnp.exp(_per_chunk(gls) - b)
            ke = k * egb
            qd_b, qm_b = (q_raw * sq * eb).astype(BF16), (q_raw * sq * ebm).astype(BF16)
            km_b, ke_b = (k * emb).astype(BF16), ke.astype(BF16)
            dov = do_ref[:, sl].astype(BF16)
            a = jnp.where(causal, _dot_nt(qm_b, km_b), 0.0).astype(BF16)
            da = jnp.where(causal, _dot_nt(dov, v), 0.0).astype(BF16)
            dkm = _dot_tn(da, qm_b)
            dst = ds_scr[h]
            dqd_s, dv_s, dke_s, dgl_s = [None] * HG_GROUP, [None] * HG_GROUP, [None] * HG_GROUP, [None] * HG_GROUP
            for g in reversed(range(HG_GROUP)):
                rs = slice(g * HG_CHUNK, (g + 1) * HG_CHUNK)
                st = st_ref[g, sl, :]
                dst_b = dst.astype(BF16)
                egl = jnp.exp(gls[g])
                dqd_s[g] = _dot(dov[rs], st.astype(BF16))
                dv_s[g] = _dot_nt(ke_b[rs], dst_b)
                dke_s[g] = _dot(v[rs], dst_b)
                dgl_s[g] = jnp.sum(dst * st, axis=0, keepdims=True) * egl
                dst = _dot_tn(dov[rs], qd_b[rs]) + dst * egl
            ds_scr[h] = dst
            dqm = _dot(da, km_b)
            dqd = jnp.concatenate(dqd_s, axis=0)
            dv = _dot_tn(a, dov) + jnp.concatenate(dv_s, axis=0)
            dke = jnp.concatenate(dke_s, axis=0)
            t1 = dke * ke
            db = dqm * qm_b.astype(F32) - dkm * km_b.astype(F32) + dqd * qd_b.astype(F32) - t1
            dgl = _per_chunk([dgl_s[g] + jnp.sum(t1[g * HG_CHUNK:(g + 1) * HG_CHUNK], axis=0, keepdims=True)
                              for g in range(HG_GROUP)])
            dlf = _chunk_cumsum(db, tri_t) + dgl
            df = dlf / f - (dkm * emb + dke * egb)
            d_ref[:, sl] = ((dqm * ebm + dqd * eb) * sq * (1.0 + q_raw * (1.0 - sq))).astype(BF16)
            d_ref[:, HG_WIDTH + h * HG_HEAD:HG_WIDTH + (h + 1) * HG_HEAD] = (
                df * (1.0 - lb) * sg * (1.0 - sg)).astype(BF16)
            d_ref[:, 2 * HG_WIDTH + h * HG_HEAD:2 * HG_WIDTH + (h + 1) * HG_HEAD] = dv.astype(BF16)
            dlb.append(jnp.sum(df * (1.0 - sg), axis=0, keepdims=True))
        _acc(dlb_ref, jnp.concatenate(dlb, axis=1), first)

    rev = lambda j: pl.BlockSpec((rows, HG_WIDTH), lambda c: (ng - 1 - c, j))
    return _call(body, name=name, grid=(ng,),
                 in_specs=[rev(0), rev(1), rev(2), pl.BlockSpec((2, HG_WIDTH), lambda c: (0, 0)),
                           pl.BlockSpec((HG_GROUP, HG_WIDTH, HG_HEAD), lambda c: (ng - 1 - c, 0, 0)), rev(0),
                           pl.BlockSpec(memory_space=pl.ANY)],
                 out_specs=[pl.BlockSpec((rows, 3 * HG_WIDTH), lambda c: (ng - 1 - c, 0)), _vec(HG_WIDTH)],
                 out_shape=[_sds(dproj.shape, BF16), _sds((1, HG_WIDTH))], aliases={6: 0},
                 scratch_shapes=[pltpu.VMEM((nh, HG_HEAD, HG_HEAD), F32)])(
                     proj, proj, proj, lb_logits, states, do, dproj)


def _to_sub(a, dil):
    t, w = a.shape
    return a if dil == 1 else a.reshape(t // dil, dil, w).transpose(1, 0, 2).reshape(t, w)


def _from_sub(a, dil):
    t, w = a.shape
    return a if dil == 1 else a.reshape(dil, t // dil, w).transpose(1, 0, 2).reshape(t, w)


def _att_bias(seg):
    def place(v):
        v = v % ATT_BLOCK
        return v if seg == 1 else seg * (v % (ATT_BLOCK // seg)) + v // (ATT_BLOCK // seg)

    row = lax.broadcasted_iota(jnp.int32, (2 * ATT_BLOCK, 2 * ATT_BLOCK), 0)
    col = lax.broadcasted_iota(jnp.int32, (2 * ATT_BLOCK, 2 * ATT_BLOCK), 1)
    qi, kj = place(row), place(col)
    prev = jnp.logical_and(col < ATT_BLOCK, kj >= qi)
    cur = jnp.logical_and(col >= ATT_BLOCK, kj <= qi)
    return jnp.stack([jnp.where(cur, 0.0, NEG), jnp.where(jnp.logical_or(prev, cur), 0.0, NEG)])


def _get(ref, sl):
    if len(ref.shape) == 2:
        return ref[:, sl]
    v = ref[:, :, sl]
    return v.reshape(ATT_BLOCK, v.shape[2])


def _put(ref, sl, val):
    if len(ref.shape) == 2:
        ref[:, sl] = val
    else:
        ref[:, :, sl] = val.reshape(ref.shape[0], ref.shape[1], val.shape[1])


def _att_spec(nb, dil, seg, width, col, back):
    bps = nb // dil

    def plain(n):
        return jnp.clip(n - back, 0, nb - 1), col

    def segmented(n):
        m = jnp.clip(n - back, 0, nb - 1)
        return 0, m // bps, m % bps, 0, col

    if seg == 1:
        return pl.BlockSpec((ATT_BLOCK, width), plain)
    return pl.BlockSpec((seg, None, None, ATT_BLOCK // seg, width), segmented)


def _att_shape(nb, dil, seg, width):
    t = nb * ATT_BLOCK
    return (t, width) if seg == 1 else (seg, dil, nb // dil, ATT_BLOCK // seg, width)


def _att_view(a, nb, dil, seg):
    return a.reshape(_att_shape(nb, dil, seg, a.shape[1]))


def _attn_fwd_block(q_ref, kc_ref, kp_ref, vc_ref, vp_ref, o_ref, l_ref, bias_ref, has_prev, lane0):
    bias = bias_ref[has_prev.astype(jnp.int32)]
    lane = lax.broadcasted_iota(jnp.int32, (1, 128), 1)
    lo = lane < 64
    nq = ATT_BLOCK
    lse_all = jnp.zeros((nq, 128), F32)
    for hp in range(ATT_HEADS // 2):
        sl = slice(hp * 128, (hp + 1) * 128)
        q2 = _get(q_ref, sl)
        zero = jnp.zeros_like(q2)
        q2 = q2 * 0.125
        qs = jnp.concatenate([jnp.where(lo, q2, zero), jnp.where(lo, zero, q2)], axis=0)
        kk = jnp.concatenate([_get(kp_ref, sl), _get(kc_ref, sl)], axis=0)
        vv = jnp.concatenate([_get(vp_ref, sl), _get(vc_ref, sl)], axis=0)
        s = _dot_nt(qs, kk) + bias
        mx = jnp.max(s, axis=-1, keepdims=True)
        p = jnp.exp(s - mx)
        l = jnp.sum(p, axis=-1, keepdims=True)
        o = _dot(p.astype(BF16), vv) * (1.0 / l)
        _put(o_ref, sl, jnp.where(lo, o[:nq], o[nq:]).astype(BF16))
        lse = mx + jnp.log(l)
        lse_all = jnp.where(lane == lane0 + 2 * hp, lse[:nq], lse_all)
        lse_all = jnp.where(lane == lane0 + 2 * hp + 1, lse[nq:], lse_all)
    _put(l_ref, slice(None), lse_all)


def _attn_fwd(branches, name):
    t = branches[0][0].shape[0]
    nb = t // ATT_BLOCK
    nbr = len(branches)

    def body(*refs):
        n = pl.program_id(0)
        biases = refs[7 * nbr:]

        @pl.when(n == 0)
        def _():
            for bias_ref, (_, _, seg) in zip(biases, branches):
                bias_ref[...] = _att_bias(seg)

        for i, (_, dil, seg) in enumerate(branches):
            _attn_fwd_block(*refs[5 * i:5 * i + 5], *refs[5 * nbr + 2 * i:5 * nbr + 2 * i + 2], biases[i],
                            (n % (nb // dil)) != 0, ATT_HEADS * i)

    in_specs, args, out_specs, out_shape = [], [], [], []
    for qkv, dil, seg in branches:
        c0 = qkv.shape[1] // ATT_WIDTH - 3
        in_specs += [_att_spec(nb, dil, seg, ATT_WIDTH, c0 + j, back) for j, back in [(0, 0), (1, 0), (1, 1), (2, 0), (2, 1)]]
        args += [_att_view(qkv, nb, dil, seg)] * 5
        out_specs += [_att_spec(nb, dil, seg, ATT_WIDTH, 0, 0), _att_spec(nb, dil, seg, 128, 0, 0)]
        out_shape += [_sds(_att_shape(nb, dil, seg, ATT_WIDTH), BF16), _sds(_att_shape(nb, dil, seg, 128))]
    out = _call(body, name=name, grid=(nb,), in_specs=in_specs, out_specs=out_specs, out_shape=out_shape,
                scratch_shapes=[pltpu.VMEM((2, 2 * ATT_BLOCK, 2 * ATT_BLOCK), F32)] * nbr)(*args)
    return [(out[2 * i].reshape(t, ATT_WIDTH), out[2 * i + 1].reshape(t, 128)) for i in range(nbr)]


def _combine_mix_in(os_, ls_, fine, o_hg, proj, og, ag, name, tr=512):
    t = os_[0].shape[0]
    nbr = len(os_)

    def body(*refs):
        o_refs, l_refs = refs[:nbr], refs[nbr:2 * nbr]
        oh_ref, g_ref, og_ref, ag_ref, a_ref, lt_ref, lts_ref, m_ref, to_natural, to_sub = refs[2 * nbr:]

        @pl.when(pl.program_id(0) == 0)
        def _():
            to_natural[...] = _regroup_matrix(tr, fine)
            to_sub[...] = _regroup_matrix(tr, tr // fine)

        lane = lax.broadcasted_iota(jnp.int32, (1, 128), 1)
        packed = l_refs[0][...] + _regroup(to_natural[...], sum(r[...] for r in l_refs[1:]).reshape(tr, 128))
        ls = [packed if i == 0 else pltpu.roll(packed, 128 - ATT_HEADS * i, 1) for i in range(nbr)]
        mx = functools.reduce(jnp.maximum, ls)
        tot = mx + jnp.log(sum(jnp.exp(l - mx) for l in ls))
        ws = [jnp.exp(l - tot) for l in ls]
        tot = jnp.where(lane < ATT_HEADS, tot, 0.0)
        lt_ref[...] = tot
        lts_ref[...] = _regroup(to_sub[...], tot).reshape(lts_ref.shape)
        o_vals = [o_refs[0]] + [_regroup(to_natural[...], r[...].reshape(tr, ATT_WIDTH)) for r in o_refs[1:]]
        pairs = []
        for hp in range(ATT_HEADS // 2):
            sl = slice(hp * 128, (hp + 1) * 128)
            acc = jnp.zeros((tr, 128), F32)
            for w, o in zip(ws, o_vals):
                wf = jnp.where(lane < 64, w[:, 2 * hp:2 * hp + 1], w[:, 2 * hp + 1:2 * hp + 2])
                acc = acc + wf * o[:, sl]
            pairs.append(acc)
        av = jnp.concatenate(pairs, axis=1)
        a_ref[...] = av
        m_ref[:, HG_WIDTH:] = (av * _rms(av) * ag_ref[...]).astype(BF16)
        for h in range(HG_WIDTH // HG_HEAD):
            sl = slice(h * HG_HEAD, (h + 1) * HG_HEAD)
            oh = oh_ref[:, sl].astype(F32)
            gv = g_ref[:, sl].astype(F32)
            m_ref[:, sl] = (oh * _rms(oh) * og_ref[...] * (gv * _sigmoid(gv))).astype(BF16)

    half = _rows(tr, ATT_WIDTH)
    sub = lambda a: a.reshape(fine, t // fine, a.shape[1])
    att, lse, lse_sub, mixin = _call(
        body, name=name, grid=(t // tr,),
        in_specs=[half] + [_sub_rows(tr, fine, ATT_WIDTH)] * (nbr - 1) + [_rows(tr, 128)] + [_sub_rows(tr, fine, 128)] * (nbr - 1)
        + [half, pl.BlockSpec((tr, HG_WIDTH), lambda i: (i, 3)), _vec(HG_HEAD), _vec(ATT_WIDTH)],
        out_specs=[half, _rows(tr, 128), _sub_rows(tr, fine, 128), _rows(tr, D_MODEL)],
        out_shape=[_sds((t, ATT_WIDTH)), _sds((t, 128)), _sds((fine, t // fine, 128)), _sds((t, D_MODEL), BF16)],
        scratch_shapes=[pltpu.VMEM((tr, tr), BF16)] * 2)(
            os_[0], *map(sub, os_[1:]), ls_[0], *map(sub, ls_[1:]), o_hg, proj, og, ag)
    return att, lse, lse_sub.reshape(t, 128), mixin


def _attn_bwd_block(q_ref, k_ref, v_ref, do_ref, l_ref, d_ref, out_ref, carry, prev, bias_ref, has_prev):
    w = ATT_WIDTH
    nq = ATT_BLOCK
    bias = bias_ref[has_prev.astype(jnp.int32)]
    lo = lax.broadcasted_iota(jnp.int32, (1, 128), 1) < 64
    lse, ddv = _get(l_ref, slice(None)), _get(d_ref, slice(None))
    for hp in range(ATT_HEADS // 2):
        sl = slice(hp * 128, (hp + 1) * 128)
        sk = slice(w + hp * 128, w + (hp + 1) * 128)
        sv = slice(2 * w + hp * 128, 2 * w + (hp + 1) * 128)
        q2, do2 = _get(q_ref, sl), _get(do_ref, sl)
        zero = jnp.zeros_like(q2)
        q2 = q2 * 0.125
        qs = jnp.concatenate([jnp.where(lo, q2, zero), jnp.where(lo, zero, q2)], axis=0)
        dos = jnp.concatenate([jnp.where(lo, do2, zero), jnp.where(lo, zero, do2)], axis=0)
        kc, vc = _get(k_ref, sl), _get(v_ref, sl)
        kk = jnp.concatenate([prev[:, sl], kc], axis=0)
        vv = jnp.concatenate([prev[:, sk], vc], axis=0)
        prev[:, sl] = kc
        prev[:, sk] = vc
        ls = jnp.concatenate([lse[:, 2 * hp:2 * hp + 1], lse[:, 2 * hp + 1:2 * hp + 2]], axis=0)
        dh = jnp.concatenate([ddv[:, 2 * hp:2 * hp + 1], ddv[:, 2 * hp + 1:2 * hp + 2]], axis=0)
        p = jnp.exp(_dot_nt(qs, kk) - ls + bias)
        ds = (p * (_dot_nt(dos, vv) - dh)).astype(BF16)
        dq = _dot(ds, kk) * 0.125
        dk = _dot_tn(ds, qs)
        dv = _dot_tn(p.astype(BF16), dos)
        _put(out_ref, sl, carry[:, sl].astype(BF16))
        _put(out_ref, sk, (carry[:, sk] + dk[:nq]).astype(BF16))
        _put(out_ref, sv, (carry[:, sv] + dv[:nq]).astype(BF16))
        carry[:, sl] = jnp.where(lo, dq[:nq], dq[nq:])
        carry[:, sk] = dk[nq:]
        carry[:, sv] = dv[nq:]


def _attn_bwd(branches, name):
    t = branches[0][0].shape[0]
    nb = t // ATT_BLOCK
    nbr = len(branches)
    w = ATT_WIDTH

    def body(*refs):
        ins, outs, carries, prevs = refs[:6 * nbr], refs[6 * nbr:7 * nbr], refs[7 * nbr:8 * nbr], refs[8 * nbr:9 * nbr]
        biases = refs[9 * nbr:]
        n = pl.program_id(0)

        @pl.when(n == 0)
        def _():
            for scratch in carries + prevs:
                scratch[...] = jnp.zeros_like(scratch)
            for bias_ref, branch in zip(biases, branches):
                bias_ref[...] = _att_bias(branch[5])

        @pl.when(n < nb)
        def _():
            for i, branch in enumerate(branches):
                dil, seg = branch[4:]
                _attn_bwd_block(*ins[6 * i:6 * i + 6], outs[i], carries[i], prevs[i], biases[i], (n % (nb // dil)) != 0)

        @pl.when(n == nb)
        def _():
            for i in range(nbr):
                _put(outs[i], slice(None), carries[i][...].astype(BF16))

    in_specs, args, out_specs, out_shape = [], [], [], []
    for qkv, dout, lse, dd, dil, seg in branches:
        c0 = qkv.shape[1] // w - 3
        in_specs += [_att_spec(nb, dil, seg, w, c0 + j, 0) for j in range(3)]
        in_specs += [_att_spec(nb, dil, seg, w, 0, 0), _att_spec(nb, dil, seg, 128, 0, 0), _att_spec(nb, dil, seg, 128, 0, 0)]
        args += [_att_view(a, nb, dil, seg) for a in [qkv] * 3 + [dout, lse, dd]]
        out_specs += [_att_spec(nb, dil, seg, 3 * w, 0, 1)]
        out_shape += [_sds(_att_shape(nb, dil, seg, 3 * w), BF16)]
    out = _call(body, name=name, grid=(nb + 1,), in_specs=in_specs, out_specs=out_specs, out_shape=out_shape,
                scratch_shapes=[pltpu.VMEM((ATT_BLOCK, 3 * w), F32)] * nbr + [pltpu.VMEM((ATT_BLOCK, 2 * w), BF16)] * nbr
                + [pltpu.VMEM((2, 2 * ATT_BLOCK, 2 * ATT_BLOCK), F32)] * nbr)(*args)
    return [o.reshape(t, 3 * w) for o in out]


def _local_step(x, tgt, mod, norm1_g, lb_logits, og, ag, norm2_g, fg, get_w, put_g, late=lambda a: a, project=None):
    shift1, scale1, gate1, shift2, scale2, gate2 = [mod[:, i * D_MODEL:(i + 1) * D_MODEL] for i in range(6)]
    fg = fg.reshape(1, D_MODEL)

    h1 = _norm_mod(x, norm1_g, scale1, shift1, "norm_mod1")
    if project is None:
        w_in = get_w("w_in", h1)
        proj = _mm_nn(h1, w_in, "mm_in", out_dtype=BF16)
    else:
        proj, w_in = project(h1)
    o_hg, states = _hgrn_fwd(proj, lb_logits, "hgrn_fwd")
    fine = DILATIONS[-1]
    layouts = [(d, 1 if d == 1 else fine // d) for d in DILATIONS]
    qkv_fine = _to_sub(proj, fine)
    qkvs = [proj if d == 1 else qkv_fine for d in DILATIONS]
    natural = lambda a, d: a if d == 1 else _from_sub(a, fine)
    outs = _attn_fwd([(q, d, seg) for q, (d, seg) in zip(qkvs, layouts)], "attn_fwd")
    att, lse, lse_fine, mixin = _combine_mix_in([o for o, _ in outs], [l for _, l in outs], fine,
                                                o_hg, proj, og, ag, "attn_combine_mix_in")
    w_out = get_w("w_out", mixin)
    mix, x2, h2 = _out_resid_norm_mod(x, mixin, w_out, gate1, norm2_g, scale2, shift2, "mm_out_resid_norm_mod2")
    w_gu = get_w("w_gu", h2)
    a_ff, u_ff, act = _mm_gate_up(h2, w_gu, "mm_gu")
    w_down = get_w("w_down", act)
    dx3, dffn, loss_v, dfg, dgate2 = _down_loss(x2, act, w_down, gate2, fg, tgt, "mm_down_loss")

    dffn = put_g("w_down", *_mm_tn(act, dffn, 1, "mm_down_dw", tm=x.shape[0], tk=256), dffn)
    dau = _mm_down_dx(dffn, w_down, a_ff, u_ff, "mm_down_dx")
    dau = put_g("w_gu", *_mm_tn(h2, dau, N_SHARD, "mm_gu_dw", tm=x.shape[0], tk=512), dau)
    dx2, dshift2, dscale2, dg2, dgate1, dmix = _norm_mod_bwd(
        dau, x2, norm2_g, scale2, dx3, "mm_gu_dx_norm_bwd", gate=gate1, mix=mix, w=w_gu)
    dmix = put_g("w_out", *_mm_tn(mixin, dmix, 1, "mm_out_dw", tm=x.shape[0], tk=512), dmix)
    do_hg, dproj, datt, dd, datt_fine, dd_fine, dog, dag = _mix_in_bwd(
        dmix, w_out, o_hg, proj, att, og, ag, fine, "mm_out_dx_mix_in_bwd")
    datts = _attn_bwd([(q,) + ((datt, lse, dd) if d == 1 else (datt_fine, lse_fine, dd_fine)) + (d, seg)
                       for q, (d, seg) in zip(qkvs, layouts)], "attn_bwd")
    dproj, dlb = _hgrn_bwd(proj, lb_logits, states, do_hg, dproj, "hgrn_bwd")
    dproj = late(dproj)
    dproj = _dproj(dproj, [natural(a, d) for a, d in zip(datts, DILATIONS)], "dproj")
    dproj = put_g("w_in", *_mm_tn(h1, dproj, N_SHARD, "mm_in_dw", tm=x.shape[0], tk=512, group=2), dproj)
    dx, dshift1, dscale1, dg1 = _norm_mod_bwd(dproj, x, norm1_g, scale1, dx2, "mm_in_dx_norm_bwd", w=w_in)

    stats = jnp.concatenate([loss_v, dfg, dg2, dg1, dlb, dag, dog,
                             dshift1, dscale1, dgate1, dshift2, dscale2, dgate2], axis=1)
    return dx, stats


def _place():
    x, y, c = lax.axis_index("x"), lax.axis_index("y"), lax.axis_index("c")
    return x, y, c


def _chip_peers(x, y, c):
    return [(1 - x, y, c), (x, 1 - y, c), (1 - x, 1 - y, c)]


_HBM = pl.BlockSpec(memory_space=pltpu.HBM)
_SEM = pl.BlockSpec(memory_space=pltpu.SEMAPHORE)
_EFFECT = pltpu.SideEffectType.DATAFLOW_SIDE_EFFECTING


def _exchange_copy(bufs, send, recv, j, peer, place, kind):
    x, y, c = place
    target = peer
    if kind == "gather":
        src = dst = bufs[0].at[2 * x + y]
    elif kind == "scatter":
        src, dst = bufs[0].at[2 * peer[0] + peer[1]], bufs[1].at[j]
    else:
        half = bufs[0].shape[1] // 2
        rows = pl.ds(c * half, half)
        if kind == "half":
            src = dst = bufs[0].at[2 * x + y, rows]
        else:
            src = dst = bufs[0].at[2 * peer[0] + peer[1], rows]
            target = (x, y, 1 - c)
    return pltpu.make_async_remote_copy(src_ref=src, dst_ref=dst, send_sem=send.at[j], recv_sem=recv.at[j],
                                        device_id=target, device_id_type=MESH)


def _exchange_start(groups, after, kind, name):
    sizes = [len(g) for g in groups]
    flat = [b for g in groups for b in g]
    ng, nb = len(groups), len(flat)

    def body(*refs):
        bufs, sems = refs[:nb], refs[nb + 1:nb + 1 + 2 * ng]
        x, y, c = _place()
        for j, peer in enumerate(_chip_peers(x, y, c)):
            at = 0
            for i, size in enumerate(sizes):
                _exchange_copy(bufs[at:at + size], sems[2 * i], sems[2 * i + 1], j, peer, (x, y, c), kind).start()
                at += size

    any_space = pl.BlockSpec(memory_space=pl.ANY)
    out = pl.pallas_call(
        body, name=name, in_specs=[_HBM] * nb + [any_space],
        out_specs=[_SEM] * (2 * ng) + [_HBM] * nb + [any_space],
        out_shape=[pltpu.SemaphoreType.DMA((3,))] * (2 * ng) + [pltpu.HBM(b.shape, b.dtype) for b in flat]
        + [_sds(after.shape, after.dtype)],
        input_output_aliases={i: 2 * ng + i for i in range(nb + 1)},
        compiler_params=pltpu.CompilerParams(has_side_effects=_EFFECT),
    )(*[pltpu.with_memory_space_constraint(b, pltpu.HBM) for b in flat], after)
    started, at = [], 2 * ng
    for i, size in enumerate(sizes):
        started.append((out[2 * i], out[2 * i + 1], tuple(out[at:at + size])))
        at += size
    return started, out[-1]


def _exchange_wait(started, after, kind, name):
    send, recv, bufs = started
    nb = len(bufs)

    def body(*refs):
        x, y, c = _place()
        for j, peer in enumerate(_chip_peers(x, y, c)):
            cp = _exchange_copy(refs[:nb], refs[nb], refs[nb + 1], j, peer, (x, y, c), kind)
            cp.wait_send()
            cp.wait_recv()

    return pl.pallas_call(
        body, name=name, in_specs=[_HBM] * nb + [_SEM, _SEM, pl.BlockSpec(memory_space=pl.ANY)],
        out_specs=[_HBM] * nb, out_shape=[pltpu.HBM(b.shape, b.dtype) for b in bufs],
        input_output_aliases={i: i for i in range(nb)},
        compiler_params=pltpu.CompilerParams(has_side_effects=_EFFECT),
    )(*bufs, send, recv, after)


def _sibling_copies(v_refs, l_refs, send, recv):
    x, y, c = _place()
    return [pltpu.make_async_remote_copy(src_ref=v, dst_ref=l, send_sem=send.at[a], recv_sem=recv.at[a],
                                         device_id=(x, y, 1 - c), device_id_type=MESH)
            for a, (v, l) in enumerate(zip(v_refs, l_refs))]


def _sibling_start(vs, after, name):
    vs = list(vs)
    n = len(vs)
    lands = [lax.empty(v.shape, v.dtype) for v in vs]

    def body(*refs):
        for cp in _sibling_copies(refs[:n], refs[n:2 * n], refs[2 * n + 1], refs[2 * n + 2]):
            cp.start()

    any_space = pl.BlockSpec(memory_space=pl.ANY)
    out = pl.pallas_call(
        body, name=name, in_specs=[_HBM] * (2 * n) + [any_space],
        out_specs=[_SEM, _SEM] + [_HBM] * (2 * n) + [any_space],
        out_shape=[pltpu.SemaphoreType.DMA((n,))] * 2 + [pltpu.HBM(b.shape, b.dtype) for b in vs + lands]
        + [_sds(after.shape, after.dtype)],
        input_output_aliases={i: 2 + i for i in range(2 * n + 1)},
        compiler_params=pltpu.CompilerParams(has_side_effects=_EFFECT),
    )(*[pltpu.with_memory_space_constraint(b, pltpu.HBM) for b in vs + lands], after)
    return (out[0], out[1], tuple(out[2:2 + n]), tuple(out[2 + n:2 + 2 * n])), out[-1]


def _sibling_wait(started, after, name):
    send, recv, vs, lands = started
    n = len(vs)

    def body(*refs):
        for cp in _sibling_copies(refs[:n], refs[n:2 * n], refs[2 * n], refs[2 * n + 1]):
            cp.wait_send()
            cp.wait_recv()

    out = pl.pallas_call(
        body, name=name, in_specs=[_HBM] * (2 * n) + [_SEM, _SEM, pl.BlockSpec(memory_space=pl.ANY)],
        out_specs=[_HBM] * (2 * n), out_shape=[pltpu.HBM(b.shape, b.dtype) for b in vs + lands],
        input_output_aliases={i: i for i in range(2 * n)},
        compiler_params=pltpu.CompilerParams(has_side_effects=_EFFECT),
    )(*vs, *lands, send, recv, after)
    return out[:n], out[n:]


def _everyone(x, y, c):
    return [(1 - x if k & 4 else x, 1 - y if k & 2 else y, 1 - c if k & 1 else c) for k in range(1, 8)]


def _all_gather_copies(land_ref, send, recv, arriving):
    x, y, c = _place()
    me = 4 * x + 2 * y + c
    return [pltpu.make_async_remote_copy(
        src_ref=land_ref.at[me], dst_ref=land_ref.at[4 * p[0] + 2 * p[1] + p[2] if arriving else me],
        send_sem=send.at[k], recv_sem=recv.at[k], device_id=p, device_id_type=MESH)
        for k, p in enumerate(_everyone(x, y, c))]


def _all_gather_start(v, name):
    x, y, c = _place()
    land = lax.dynamic_update_slice(lax.empty((8,) + v.shape, v.dtype), v[None], (4 * x + 2 * y + c, 0, 0))

    def body(land_ref, v_ref, send, recv, land_out, v_out):
        for cp in _all_gather_copies(land_ref, send, recv, False):
            cp.start()

    any_space = pl.BlockSpec(memory_space=pl.ANY)
    out = pl.pallas_call(
        body, name=name, in_specs=[_HBM, any_space], out_specs=[_SEM, _SEM, _HBM, any_space],
        out_shape=[pltpu.SemaphoreType.DMA((7,))] * 2 + [pltpu.HBM(land.shape, land.dtype), _sds(v.shape, v.dtype)],
        input_output_aliases={0: 2, 1: 3}, compiler_params=pltpu.CompilerParams(has_side_effects=_EFFECT),
    )(pltpu.with_memory_space_constraint(land, pltpu.HBM), v)
    return tuple(out[:3]), out[3]


def _all_gather_wait(started, after, name):
    send, recv, land = started

    def body(land_ref, send, recv, after_ref, land_out):
        for cp in _all_gather_copies(land_ref, send, recv, True):
            cp.wait_send()
            cp.wait_recv()

    return pl.pallas_call(
        body, name=name, in_specs=[_HBM, _SEM, _SEM, pl.BlockSpec(memory_space=pl.ANY)], out_specs=_HBM,
        out_shape=pltpu.HBM(land.shape, land.dtype), input_output_aliases={0: 0},
        compiler_params=pltpu.CompilerParams(has_side_effects=_EFFECT),
    )(land, send, recv, after)


def _cast_place(ws, shard, name, after=()):
    n = len(ws)

    def body(s_ref, *refs):
        for w_ref, o_ref in zip(refs[:n], refs[-n:]):
            o_ref[0] = w_ref[...].astype(BF16)

    return pl.pallas_call(
        body, name=name, out_shape=[_sds((N_SHARD,) + w.shape, BF16) for w in ws],
        grid_spec=pltpu.PrefetchScalarGridSpec(
            num_scalar_prefetch=1, grid=(4,),
            in_specs=[pl.BlockSpec((w.shape[0] // 4, w.shape[1]), lambda i, s: (i, 0)) for w in ws]
            + [pl.BlockSpec(memory_space=pl.ANY)] * len(after),
            out_specs=[pl.BlockSpec((1, w.shape[0] // 4, w.shape[1]), lambda i, s: (s[0], i, 0)) for w in ws]),
        compiler_params=pltpu.CompilerParams(dimension_semantics=("arbitrary",), vmem_limit_bytes=VMEM_LIMIT),
    )(shard.reshape(1).astype(jnp.int32), *ws, *after)


def _mod_rows(c8, w_ada, b_ada, name):
    n = w_ada.shape[1]

    def gather(src_ref, dst_ref, send, recv, loc, base):
        x, y, c = _place()
        me = 4 * x + 2 * y + c
        own = pltpu.make_async_copy(src_ref, dst_ref.at[me], loc)
        own.start()
        peers = _everyone(x, y, c)
        sends = [pltpu.make_async_remote_copy(src_ref=src_ref, dst_ref=dst_ref.at[me], send_sem=send.at[base + k],
                                              recv_sem=recv.at[base + k], device_id=p, device_id_type=MESH)
                 for k, p in enumerate(peers)]
        for cp in sends:
            cp.start()
        for k, p in enumerate(peers):
            pltpu.make_async_remote_copy(src_ref=src_ref, dst_ref=dst_ref.at[4 * p[0] + 2 * p[1] + p[2]],
                                         send_sem=send.at[base + k], recv_sem=recv.at[base + k], device_id=p,
                                         device_id_type=MESH).wait_recv()
        for cp in sends:
            cp.wait_send()
        own.wait()

    def body(c_ref, w_ref, b_ref, a_ref, parts_ref, c_all, part, send, recv, loc):
        gather(c_ref, c_all, send, recv, loc.at[0], 0)
        cv = jnp.max(c_all[...], axis=1)
        ca = cv * _sigmoid(cv)
        a_ref[...] = ca
        part[...] = jnp.dot(ca, w_ref[...], precision=lax.Precision.HIGHEST, preferred_element_type=F32) + b_ref[...]
        gather(part, parts_ref, send, recv, loc.at[1], 7)

    vmem = pl.BlockSpec(memory_space=pltpu.VMEM)
    return pl.pallas_call(
        body, name=name, in_specs=[vmem] * 3, out_specs=[vmem, vmem],
        out_shape=[_sds((8, D_MODEL)), _sds((8, 8, n))],
        scratch_shapes=[pltpu.VMEM((8, 8, D_MODEL), F32), pltpu.VMEM((8, n), F32), pltpu.SemaphoreType.DMA((14,)),
                        pltpu.SemaphoreType.DMA((14,)), pltpu.SemaphoreType.DMA((2,))],
        compiler_params=pltpu.CompilerParams(vmem_limit_bytes=VMEM_LIMIT))(c8, w_ada, b_ada)


def _sum_received(gs, shard, lands, name):
    n = len(gs)

    def body(s_ref, *refs):
        for g_ref, l_ref, o_ref in zip(refs[:n], refs[n:2 * n], refs[2 * n:]):
            o_ref[...] = ((g_ref[0] + l_ref[0].astype(F32)) + l_ref[1].astype(F32)) + l_ref[2].astype(F32)

    quarter = lambda g: (g.shape[1] // 4, g.shape[2])
    return pl.pallas_call(
        body, name=name, out_shape=[_sds(g.shape[1:]) for g in gs],
        grid_spec=pltpu.PrefetchScalarGridSpec(
            num_scalar_prefetch=1, grid=(4,),
            in_specs=[pl.BlockSpec((1,) + quarter(g), lambda i, s: (s[0], i, 0)) for g in gs]
            + [pl.BlockSpec((3,) + quarter(g), lambda i, s: (0, i, 0)) for g in gs],
            out_specs=[pl.BlockSpec(quarter(g), lambda i, s: (i, 0)) for g in gs]),
        compiler_params=pltpu.CompilerParams(dimension_semantics=("arbitrary",), vmem_limit_bytes=VMEM_LIMIT),
    )(shard.reshape(1).astype(jnp.int32), *gs, *lands)


def _adamw_outer(w, ct, dm, m, v, name):
    k, n = w.shape
    tr = k // 4

    def body(w_ref, c_ref, d_ref, m_ref, v_ref, g_out, d_out, m_out, v_out):
        cv = c_ref[...]
        dv = d_ref[...]
        g = cv[:, 0:1] * dv[0:1, :]
        for i in range(1, 8):
            g = g + cv[:, i:i + 1] * dv[i:i + 1, :]
        g_out[...] = g
        d_out[...], m_out[...], v_out[...] = _adamw_math(w_ref[...], g, m_ref[...], v_ref[...])

    row = _rows(tr, n)
    return _call(body, name=name, grid=(4,),
                 in_specs=[row, _rows(tr, 8), pl.BlockSpec((8, n), lambda i: (0, 0)), row, row],
                 out_specs=[row] * 4, out_shape=[_sds((k, n))] * 4)(w, ct, dm, m, v)


def _adamw_math(w, g, m, v):
    m_new = ADAM_B1 * m + (1.0 - ADAM_B1) * g
    v_new = ADAM_B2 * v + (1.0 - ADAM_B2) * (g * g)
    m_hat = m_new / (1.0 - ADAM_B1 ** ADAM_STEP)
    v_hat = v_new / (1.0 - ADAM_B2 ** ADAM_STEP)
    return -ADAM_LR * (m_hat / (jnp.sqrt(v_hat) + ADAM_EPS) + ADAM_WD * w), m_new, v_new


def _small_update(stats, smalls, name):
    offsets = [ST_DMOD, ST_DG1, ST_DLB, ST_DOG, ST_DAG, ST_DG2, ST_DFG]
    lb_index = 2

    def body(*refs):
        s_ref, ins, l_ref, outs = refs[0], refs[1:22], refs[22], refs[23:]
        tot = s_ref[0:1, :]
        for i in range(1, 8):
            tot = tot + s_ref[i:i + 1, :]
        l_ref[...] = jnp.zeros((1, 128), F32) + (0.5 / D_MODEL) * jnp.sum(tot[:, ST_LOSS:ST_LOSS + D_MODEL])
        for p, off in enumerate(offsets):
            w_ref, m_ref, v_ref = ins[3 * p:3 * p + 3]
            g_out, d_out, m_out, v_out = outs[4 * p:4 * p + 4]
            g = tot[:, off:off + w_ref.shape[1]]
            if p == lb_index:
                lg = w_ref[...]
                lb = _sigmoid(lg[0:1] - lg[1:2])
                g = g * lb * (1.0 - lb)
            for r in range(w_ref.shape[0]):
                rows = slice(r, r + 1)
                gr = g if r == 0 else -g
                delta, m_new, v_new = _adamw_math(w_ref[rows, :], gr, m_ref[rows, :], v_ref[rows, :])
                g_out[rows, :] = gr
                d_out[rows, :] = delta
                m_out[rows, :] = m_new
                v_out[rows, :] = v_new

    full = lambda a: pl.BlockSpec(a.shape, lambda i: (0, 0))
    flat = [a for t in smalls for a in t]
    return _call(body, name=name, grid=(1,),
                 in_specs=[full(stats)] + [full(a) for a in flat],
                 out_specs=[pl.BlockSpec((1, 128), lambda i: (0, 0))] + [full(t[0]) for t in smalls for _ in range(4)],
                 out_shape=[_sds((1, 128))] + [_sds(t[0].shape) for t in smalls for _ in range(4)])(stats, *flat)


def _adamw(params, name):
    n = len(params)

    def body(*refs):
        for p in range(n):
            w_ref, ga_ref, gb_ref, m_ref, v_ref = refs[5 * p:5 * p + 5]
            g_out, d_out, m_out, v_out = refs[5 * n + 4 * p:5 * n + 4 * p + 4]
            g = ga_ref[...] + gb_ref[...]
            g_out[...] = g
            d_out[...], m_out[...], v_out[...] = _adamw_math(w_ref[...], g, m_ref[...], v_ref[...])

    row = lambda w: _rows(w.shape[0] // 4, w.shape[1])
    out = _call(body, name=name, grid=(4,), in_specs=[row(p[0]) for p in params for _ in range(5)],
                out_specs=[row(p[0]) for p in params for _ in range(4)],
                out_shape=[_sds(p[0].shape) for p in params for _ in range(4)])(*[a for p in params for a in p])
    return [tuple(out[4 * p:4 * p + 4]) for p in range(n)]


def kernel(x, c, w_ada, b_ada, norm1_g, w_in, hg_lb_logits, hg_onorm_g, att_onorm_g, w_out, norm2_g, w_gate_up, w_down, final_g, loss_target, m_w_ada, m_b_ada, m_norm1_g, m_w_in, m_hg_lb_logits, m_hg_onorm_g, m_att_onorm_g, m_w_out, m_norm2_g, m_w_gate_up, m_w_down, m_final_g, v_w_ada, v_b_ada, v_norm1_g, v_w_in, v_hg_lb_logits, v_hg_onorm_g, v_att_onorm_g, v_w_out, v_norm2_g, v_w_gate_up, v_w_down, v_final_g):
    ix, iy, ic = _place()
    shard = 2 * ix + iy
    sample = 4 * ix + 2 * iy + ic
    n_ada = w_ada.shape[2]

    shards = [w_in[0], w_out[0], w_gate_up[0], w_down[0]]
    names = ["w_in", "w_out", "w_gu", "w_down"]
    shapes = [(N_SHARD,) + w.shape for w in shards]
    placed = [(_cast_place(shards[:1], shard, "place_w_in")[0],)]

    b_part = lax.dynamic_slice(b_ada, (0, shard * n_ada), (1, n_ada))
    c_act, parts = _mod_rows(jnp.broadcast_to(c, (8, D_MODEL)), w_ada[0], b_part, "mod_rows")
    parts = parts[::2]
    mod = lax.dynamic_index_in_dim(parts, sample, axis=1, keepdims=False).reshape(1, 6 * D_MODEL)
    (first,), mod = _exchange_start(placed[:1], mod, "half", "gather_start_w_in")
    gathering = {}

    def get_w(name, after):
        if name == "w_in":
            placed_rest = [(p,) for p in _cast_place(shards[1:], shard, "place_rest", (after,))]
            halves = _exchange_wait(first, placed_rest[0][0], "half", "gather_wait_w_in")
            (passing,), token = _exchange_start([tuple(halves)], mod, "forward", "forward_start_w_in")
            rest, token = _exchange_start(placed_rest, token, "gather", "gather_start_rest")
            (full,) = _exchange_wait(passing, token, "forward", "forward_wait_w_in")
            gathering.update(zip(names[1:], rest))
            return full
        (full,) = _exchange_wait(gathering[name], after, "gather", "gather_wait_" + name)
        return full if name == "w_gu" else full.reshape(1, -1, D_MODEL)

    scattering = {}

    def put_g(name, g, g_bf16, then):
        shape = shapes[names.index(name)]
        land = lax.empty((3,) + shape[1:], BF16)
        (started,), then = _exchange_start([(g_bf16.reshape(shape), land)], then, "scatter", "scatter_start_" + name)
        scattering[name] = (g.reshape(shape), started)
        return then

    def summed(group, after, tag):
        lands = [_exchange_wait(scattering[nm][1], after, "scatter", "scatter_wait_" + nm)[1] for nm in group]
        return _sum_received([scattering[nm][0] for nm in group], shard, lands, "sum_" + tag)

    early = ["w_down", "w_gu", "w_out"]
    swapping = []

    def late(a):
        started, a = _sibling_start(summed(early, a, "early"), a, "swap_start")
        swapping.append(started)
        return a

    def project(h1):
        own = _mm_own_shard(h1, shards[0], shard, N_SHARD, "mm_in_own")
        w_full = get_w("w_in", own)
        return _mm_other_shards(h1, w_full, own, shard, "mm_in_rest"), w_full

    dx, stats = _local_step(x[0], loss_target[0], mod, norm1_g, hg_lb_logits, hg_onorm_g, att_onorm_g,
                            norm2_g, final_g, get_w, put_g, late, project)

    gathering_stats, stats = _all_gather_start(stats, "stats_start")
    moments = [(m_w_in, v_w_in), (m_w_out, v_w_out), (m_w_gate_up, v_w_gate_up), (m_w_down, v_w_down)]

    def update(group, sums, other, tag):
        params = [(shards[names.index(nm)], s, o, moments[names.index(nm)][0][0], moments[names.index(nm)][1][0])
                  for nm, s, o in zip(group, sums, other)]
        return dict(zip(group, _adamw(params, "adamw_" + tag)))

    sums, other = _sibling_wait(swapping[0], stats, "swap_wait")
    done = update(early, sums, other, "early")
    swapping_in, stats = _sibling_start(summed(["w_in"], done["w_out"][1], "w_in"), stats, "swap_start_w_in")

    stats_all = _all_gather_wait(gathering_stats, stats, "stats_wait").reshape(8, ST_WIDTH)
    dmod = lax.dynamic_slice(stats_all, (0, ST_DMOD + shard * n_ada), (8, n_ada))

    as_row = lambda a: a.reshape(1, -1) if a.ndim == 1 else a
    smalls = [tuple(as_row(a) for a in t) for t in [
        (b_ada, m_b_ada, v_b_ada), (norm1_g, m_norm1_g, v_norm1_g),
        (hg_lb_logits, m_hg_lb_logits, v_hg_lb_logits), (hg_onorm_g, m_hg_onorm_g, v_hg_onorm_g),
        (att_onorm_g, m_att_onorm_g, v_att_onorm_g), (norm2_g, m_norm2_g, v_norm2_g),
        (final_g, m_final_g, v_final_g)]]
    loss, *small_out = _small_update(stats_all, smalls, "small_update")
    shapes_out = [b_ada.shape, norm1_g.shape, hg_lb_logits.shape, hg_onorm_g.shape, att_onorm_g.shape,
                  norm2_g.shape, final_g.shape]
    sg, sd, sm, sv = [[small_out[4 * p + i].reshape(shapes_out[p]) for p in range(7)] for i in range(4)]

    ada = _adamw_outer(w_ada[0], c_act.T, dmod, m_w_ada[0], v_w_ada[0], "adamw_w_ada")
    sum_in, other_in = _sibling_wait(swapping_in, ada[1], "swap_wait_w_in")
    done.update(update(["w_in"], sum_in, other_in, "w_in"))
    big = [ada] + [done[nm] for nm in names]
    bg, bd, bm, bv = [[t[i][None] for t in big] for i in range(4)]

    def order(b, s):
        return [b[0], s[0], s[1], b[1], s[2], s[3], s[4], b[2], s[5], b[3], b[4], s[6]]

    return (loss[0, 0], dx[None], *order(bg, sg), *order(bd, sd), *order(bm, sm), *order(bv, sv))
```
